```python
import jax, jax.numpy as jnp
from jax import lax
import numpy as np

D_MODEL = 1024
BATCH = 8
SEQ = 4096
DEPTH = 1

D_RNN = D_MODEL
N_RNN_BLOCKS = 4
RNN_BLOCK = D_RNN // N_RNN_BLOCKS
CONV_WIDTH = 4
C_RG = 8.0
MIN_RAD, MAX_RAD = 0.9, 0.999
D_POOL = D_MODEL
POOL_WINDOWS = (2, 4, 8, 16)
N_POOL_GROUPS = len(POOL_WINDOWS)
POOL_GROUP = D_POOL // N_POOL_GROUPS
N_BRANCHES = 2
D_FF = 4 * D_MODEL
N_MOD = 6
D_IN = 2 * D_RNN + D_POOL + N_BRANCHES * D_MODEL
EPS = 1e-6

kernel_name = "hybrid_rglru_pool_gated_block"


def rmsnorm(x, g):
    xf = x.astype(jnp.float32)
    y = xf * lax.rsqrt(jnp.mean(xf * xf, axis=-1, keepdims=True) + EPS)
    return (y * g.astype(jnp.float32)).astype(x.dtype)


def block_diag(x, w):
    b, s, _ = x.shape
    g, dg, _ = w.shape
    y = jnp.einsum('bsgi,gij->bsgj', x.reshape(b, s, g, dg), w)
    return y.reshape(b, s, g * dg)


def causal_conv(x, w, bias):
    k = w.shape[0]
    s = x.shape[1]
    xp = jnp.pad(x, ((0, 0), (k - 1, 0), (0, 0)))
    y = bias
    for j in range(k):
        y = y + xp[:, j:j + s] * w[j]
    return y


def rg_lru(x, w_a, b_a, w_x, b_x, a_param):
    xf = x.astype(jnp.float32)
    r = jax.nn.sigmoid(block_diag(xf, w_a.astype(jnp.float32)) + b_a.astype(jnp.float32))
    i = jax.nn.sigmoid(block_diag(xf, w_x.astype(jnp.float32)) + b_x.astype(jnp.float32))
    log_a = -C_RG * r * jax.nn.softplus(a_param.astype(jnp.float32))
    a = jnp.exp(log_a)
    mult = jnp.sqrt(-jnp.expm1(2.0 * log_a))
    first = (jnp.arange(x.shape[1]) == 0)[None, :, None]
    mult = jnp.where(first, 1.0, mult)
    bval = xf * i * mult

    def combine(l, rr):
        a1, b1 = l
        a2, b2 = rr
        return a1 * a2, a2 * b1 + b2

    _, h = lax.associative_scan(combine, (a, bval), axis=1)
    return h.astype(x.dtype)


def multiscale_pool(u, w_pool, b_pool, pool_scale):
    uf = u.astype(jnp.float32)
    s = u.shape[1]
    pos = jnp.arange(s, dtype=jnp.float32)[None, :, None]
    outs = []
    for gi, win in enumerate(POOL_WINDOWS):
        seg = uf[..., gi * POOL_GROUP:(gi + 1) * POOL_GROUP]
        cs = jnp.cumsum(seg, axis=1)
        cs_prev = jnp.pad(cs, ((0, 0), (win, 0), (0, 0)))[:, :s]
        cnt = jnp.minimum(pos + 1.0, float(win))
        outs.append((cs - cs_prev) / cnt - seg)
    p = jnp.concatenate(outs, axis=-1)
    p = block_diag(p, w_pool.astype(jnp.float32)) + b_pool.astype(jnp.float32)
    return (p * pool_scale.astype(jnp.float32)).astype(u.dtype)


def _fwd_setup_inputs(seed: int = 0) -> dict:
    key = jax.random.key(seed)
    ks = jax.random.split(key, 24)
    f32 = jnp.float32
    L = DEPTH

    def nrm(k, shape, fan_in):
        return jax.random.normal(k, shape, f32) * (fan_in ** -0.5)

    u = jax.random.uniform(ks[10], (L, D_RNN), f32)
    a_real = 0.5 * jnp.log(u * (MAX_RAD ** 2 - MIN_RAD ** 2) + MIN_RAD ** 2)
    a_param = jnp.log(jnp.expm1(-a_real))
    return {
        "x": jax.random.normal(ks[0], (BATCH, SEQ, D_MODEL), f32),
        "c": jax.random.normal(ks[1], (BATCH, D_MODEL), f32),
        "norm_mix_g": 1.0 + 0.05 * jax.random.normal(ks[2], (L, D_MODEL), f32),
        "norm_mlp_g": 1.0 + 0.05 * jax.random.normal(ks[3], (L, D_MODEL), f32),
        "w_ada": nrm(ks[4], (L, D_MODEL, N_MOD * D_MODEL), D_MODEL),
        "b_ada": 0.02 * jax.random.normal(ks[5], (L, N_MOD * D_MODEL), f32),
        "w_in": nrm(ks[6], (L, D_MODEL, D_IN), D_MODEL),
        "conv_w": nrm(ks[7], (L, CONV_WIDTH, D_RNN), CONV_WIDTH),
        "conv_b": 0.02 * jax.random.normal(ks[8], (L, D_RNN), f32),
        "w_rg_a": nrm(ks[9], (L, N_RNN_BLOCKS, RNN_BLOCK, RNN_BLOCK), RNN_BLOCK),
        "b_rg_a": 0.02 * jax.random.normal(ks[11], (L, D_RNN), f32),
        "w_rg_x": nrm(ks[12], (L, N_RNN_BLOCKS, RNN_BLOCK, RNN_BLOCK), RNN_BLOCK),
        "b_rg_x": 0.02 * jax.random.normal(ks[13], (L, D_RNN), f32),
        "a_param": a_param,
        "w_branch_a": nrm(ks[14], (L, D_RNN, D_MODEL), D_RNN),
        "w_pool": nrm(ks[15], (L, N_POOL_GROUPS, POOL_GROUP, POOL_GROUP), POOL_GROUP),
        "b_pool": 0.02 * jax.random.normal(ks[16], (L, D_POOL), f32),
        "pool_scale": 1.0 + 0.1 * jax.random.normal(ks[17], (L, D_POOL), f32),
        "w_branch_b": nrm(ks[18], (L, D_POOL, D_MODEL), D_POOL),
        "w_out": nrm(ks[19], (L, D_MODEL, D_MODEL), D_MODEL),
        "w_up": nrm(ks[20], (L, D_MODEL, D_FF), D_MODEL),
        "w_down": nrm(ks[21], (L, D_FF, D_MODEL), D_FF),
        "final_g": 1.0 + 0.05 * jax.random.normal(ks[22], (D_MODEL,), f32),
    }


def _fwd_reference(x, c, norm_mix_g, norm_mlp_g, w_ada, b_ada, w_in, conv_w, conv_b,
              w_rg_a, b_rg_a, w_rg_x, b_rg_x, a_param, w_branch_a, w_pool, b_pool,
              pool_scale, w_branch_b, w_out, w_up, w_down, final_g):
    c_act = jax.nn.silu(c)
    for l in range(DEPTH):
        mod = c_act @ w_ada[l] + b_ada[l]
        sh1, sc1, gt1, sh2, sc2, gt2 = [m[:, None, :] for m in jnp.split(mod, N_MOD, axis=-1)]

        h = rmsnorm(x, norm_mix_g[l]) * (1.0 + sc1) + sh1
        proj = h @ w_in[l]
        x_rnn, y_rnn, u_pool, g_a, g_b = jnp.split(
            proj, np.cumsum([D_RNN, D_RNN, D_POOL, D_MODEL]).tolist(), axis=-1)

        xr = causal_conv(x_rnn, conv_w[l], conv_b[l])
        hr = rg_lru(xr, w_rg_a[l], b_rg_a[l], w_rg_x[l], b_rg_x[l], a_param[l])
        branch_a = (jax.nn.gelu(y_rnn) * hr) @ w_branch_a[l]

        pooled = multiscale_pool(u_pool, w_pool[l], b_pool[l], pool_scale[l])
        branch_b = pooled @ w_branch_b[l]

        merged = jax.nn.sigmoid(g_a) * branch_a + jax.nn.sigmoid(g_b) * branch_b
        x = x + gt1 * (merged @ w_out[l])

        h = rmsnorm(x, norm_mlp_g[l]) * (1.0 + sc2) + sh2
        ff = jnp.square(jax.nn.relu(h @ w_up[l]))
        x = x + gt2 * (ff @ w_down[l])
    return rmsnorm(x, final_g)


import jax as _jax
import jax.numpy as _jnp

TWIN_FORMAT = 'train_step'
FWD_PARAMS = ['x', 'c', 'norm_mix_g', 'norm_mlp_g', 'w_ada', 'b_ada', 'w_in', 'conv_w', 'conv_b', 'w_rg_a', 'b_rg_a', 'w_rg_x', 'b_rg_x', 'a_param', 'w_branch_a', 'w_pool', 'b_pool', 'pool_scale', 'w_branch_b', 'w_out', 'w_up', 'w_down', 'final_g']
TWIN_WEIGHTS = ['norm_mix_g', 'norm_mlp_g', 'w_ada', 'b_ada', 'w_in', 'conv_w', 'conv_b', 'w_rg_a', 'b_rg_a', 'w_rg_x', 'b_rg_x', 'a_param', 'w_branch_a', 'w_pool', 'b_pool', 'pool_scale', 'w_branch_b', 'w_out', 'w_up', 'w_down', 'final_g']
TWIN_DIFF_INPUT = 'x'
TWIN_INPUTS = ['x', 'c', 'norm_mix_g', 'norm_mlp_g', 'w_ada', 'b_ada', 'w_in', 'conv_w', 'conv_b', 'w_rg_a', 'b_rg_a', 'w_rg_x', 'b_rg_x', 'a_param', 'w_branch_a', 'w_pool', 'b_pool', 'pool_scale', 'w_branch_b', 'w_out', 'w_up', 'w_down', 'final_g', 'loss_target', 'm_norm_mix_g', 'm_norm_mlp_g', 'm_w_ada', 'm_b_ada', 'm_w_in', 'm_conv_w', 'm_conv_b', 'm_w_rg_a', 'm_b_rg_a', 'm_w_rg_x', 'm_b_rg_x', 'm_a_param', 'm_w_branch_a', 'm_w_pool', 'm_b_pool', 'm_pool_scale', 'm_w_branch_b', 'm_w_out', 'm_w_up', 'm_w_down', 'm_final_g', 'v_norm_mix_g', 'v_norm_mlp_g', 'v_w_ada', 'v_b_ada', 'v_w_in', 'v_conv_w', 'v_conv_b', 'v_w_rg_a', 'v_b_rg_a', 'v_w_rg_x', 'v_b_rg_x', 'v_a_param', 'v_w_branch_a', 'v_w_pool', 'v_b_pool', 'v_pool_scale', 'v_w_branch_b', 'v_w_out', 'v_w_up', 'v_w_down', 'v_final_g']
TWIN_OUTPUTS = ['loss', 'grad_x', 'grad_norm_mix_g', 'grad_norm_mlp_g', 'grad_w_ada', 'grad_b_ada', 'grad_w_in', 'grad_conv_w', 'grad_conv_b', 'grad_w_rg_a', 'grad_b_rg_a', 'grad_w_rg_x', 'grad_b_rg_x', 'grad_a_param', 'grad_w_branch_a', 'grad_w_pool', 'grad_b_pool', 'grad_pool_scale', 'grad_w_branch_b', 'grad_w_out', 'grad_w_up', 'grad_w_down', 'grad_final_g', 'delta_norm_mix_g', 'delta_norm_mlp_g', 'delta_w_ada', 'delta_b_ada', 'delta_w_in', 'delta_conv_w', 'delta_conv_b', 'delta_w_rg_a', 'delta_b_rg_a', 'delta_w_rg_x', 'delta_b_rg_x', 'delta_a_param', 'delta_w_branch_a', 'delta_w_pool', 'delta_b_pool', 'delta_pool_scale', 'delta_w_branch_b', 'delta_w_out', 'delta_w_up', 'delta_w_down', 'delta_final_g', 'new_m_norm_mix_g', 'new_m_norm_mlp_g', 'new_m_w_ada', 'new_m_b_ada', 'new_m_w_in', 'new_m_conv_w', 'new_m_conv_b', 'new_m_w_rg_a', 'new_m_b_rg_a', 'new_m_w_rg_x', 'new_m_b_rg_x', 'new_m_a_param', 'new_m_w_branch_a', 'new_m_w_pool', 'new_m_b_pool', 'new_m_pool_scale', 'new_m_w_branch_b', 'new_m_w_out', 'new_m_w_up', 'new_m_w_down', 'new_m_final_g', 'new_v_norm_mix_g', 'new_v_norm_mlp_g', 'new_v_w_ada', 'new_v_b_ada', 'new_v_w_in', 'new_v_conv_w', 'new_v_conv_b', 'new_v_w_rg_a', 'new_v_b_rg_a', 'new_v_w_rg_x', 'new_v_b_rg_x', 'new_v_a_param', 'new_v_w_branch_a', 'new_v_w_pool', 'new_v_b_pool', 'new_v_pool_scale', 'new_v_w_branch_b', 'new_v_w_out', 'new_v_w_up', 'new_v_w_down', 'new_v_final_g']
TWIN_LEAF_KINDS = {'loss': 'loss', 'grad_x': 'grad_x', 'grad_norm_mix_g': 'grad_w', 'grad_norm_mlp_g': 'grad_w', 'grad_w_ada': 'grad_w', 'grad_b_ada': 'grad_w', 'grad_w_in': 'grad_w', 'grad_conv_w': 'grad_w', 'grad_conv_b': 'grad_w', 'grad_w_rg_a': 'grad_w', 'grad_b_rg_a': 'grad_w', 'grad_w_rg_x': 'grad_w', 'grad_b_rg_x': 'grad_w', 'grad_a_param': 'grad_w', 'grad_w_branch_a': 'grad_w', 'grad_w_pool': 'grad_w', 'grad_b_pool': 'grad_w', 'grad_pool_scale': 'grad_w', 'grad_w_branch_b': 'grad_w', 'grad_w_out': 'grad_w', 'grad_w_up': 'grad_w', 'grad_w_down': 'grad_w', 'grad_final_g': 'grad_w', 'delta_norm_mix_g': 'delta_w', 'delta_norm_mlp_g': 'delta_w', 'delta_w_ada': 'delta_w', 'delta_b_ada': 'delta_w', 'delta_w_in': 'delta_w', 'delta_conv_w': 'delta_w', 'delta_conv_b': 'delta_w', 'delta_w_rg_a': 'delta_w', 'delta_b_rg_a': 'delta_w', 'delta_w_rg_x': 'delta_w', 'delta_b_rg_x': 'delta_w', 'delta_a_param': 'delta_w', 'delta_w_branch_a': 'delta_w', 'delta_w_pool': 'delta_w', 'delta_b_pool': 'delta_w', 'delta_pool_scale': 'delta_w', 'delta_w_branch_b': 'delta_w', 'delta_w_out': 'delta_w', 'delta_w_up': 'delta_w', 'delta_w_down': 'delta_w', 'delta_final_g': 'delta_w', 'new_m_norm_mix_g': 'new_m', 'new_m_norm_mlp_g': 'new_m', 'new_m_w_ada': 'new_m', 'new_m_b_ada': 'new_m', 'new_m_w_in': 'new_m', 'new_m_conv_w': 'new_m', 'new_m_conv_b': 'new_m', 'new_m_w_rg_a': 'new_m', 'new_m_b_rg_a': 'new_m', 'new_m_w_rg_x': 'new_m', 'new_m_b_rg_x': 'new_m', 'new_m_a_param': 'new_m', 'new_m_w_branch_a': 'new_m', 'new_m_w_pool': 'new_m', 'new_m_b_pool': 'new_m', 'new_m_pool_scale': 'new_m', 'new_m_w_branch_b': 'new_m', 'new_m_w_out': 'new_m', 'new_m_w_up': 'new_m', 'new_m_w_down': 'new_m', 'new_m_final_g': 'new_m', 'new_v_norm_mix_g': 'new_v', 'new_v_norm_mlp_g': 'new_v', 'new_v_w_ada': 'new_v', 'new_v_b_ada': 'new_v', 'new_v_w_in': 'new_v', 'new_v_conv_w': 'new_v', 'new_v_conv_b': 'new_v', 'new_v_w_rg_a': 'new_v', 'new_v_b_rg_a': 'new_v', 'new_v_w_rg_x': 'new_v', 'new_v_b_rg_x': 'new_v', 'new_v_a_param': 'new_v', 'new_v_w_branch_a': 'new_v', 'new_v_w_pool': 'new_v', 'new_v_b_pool': 'new_v', 'new_v_pool_scale': 'new_v', 'new_v_w_branch_b': 'new_v', 'new_v_w_out': 'new_v', 'new_v_w_up': 'new_v', 'new_v_w_down': 'new_v', 'new_v_final_g': 'new_v'}


def _forward(args):
    return _fwd_reference(*[args[k] for k in FWD_PARAMS])


def _output_shape():
    def fwd():
        inp = _fwd_setup_inputs(0)
        return _fwd_reference(*[inp[k] for k in FWD_PARAMS])
    out = _jax.eval_shape(fwd)
    return out.shape, out.dtype

N_MICROBATCH = 1
ADAM_LR = 0.001
ADAM_B1 = 0.9
ADAM_B2 = 0.999
ADAM_EPS = 1e-08
ADAM_WD = 0.01
ADAM_STEP = 10
PER_EXAMPLE_BATCH_AXIS = {'x': 0, 'c': 0, 'loss_target': 0}
SHARED_INPUTS = []
_WEIGHT_DTYPES = {'norm_mix_g': _jnp.float32, 'norm_mlp_g': _jnp.float32, 'w_ada': _jnp.float32, 'b_ada': _jnp.float32, 'w_in': _jnp.float32, 'conv_w': _jnp.float32, 'conv_b': _jnp.float32, 'w_rg_a': _jnp.float32, 'b_rg_a': _jnp.float32, 'w_rg_x': _jnp.float32, 'b_rg_x': _jnp.float32, 'a_param': _jnp.float32, 'w_branch_a': _jnp.float32, 'w_pool': _jnp.float32, 'b_pool': _jnp.float32, 'pool_scale': _jnp.float32, 'w_branch_b': _jnp.float32, 'w_out': _jnp.float32, 'w_up': _jnp.float32, 'w_down': _jnp.float32, 'final_g': _jnp.float32}
MOMENT_SCALE = {'norm_mix_g': 9.871516e-02, 'norm_mlp_g': 1.434149e-01, 'w_ada': 2.727570e-01, 'b_ada': 5.726231e-01, 'w_in': 7.342629e-02, 'conv_w': 1.102335e-01, 'conv_b': 1.945981e-01, 'w_rg_a': 1.582886e-02, 'b_rg_a': 2.455251e-02, 'w_rg_x': 2.902141e-02, 'b_rg_x': 4.899406e-02, 'a_param': 7.061440e-02, 'w_branch_a': 1.204284e-01, 'w_pool': 5.481628e-02, 'b_pool': 1.155379e-01, 'pool_scale': 5.466973e-02, 'w_branch_b': 5.471012e-02, 'w_out': 1.302706e-01, 'w_up': 1.179802e-01, 'w_down': 3.690240e-01, 'final_g': 3.415820e+01}


def _to_microbatches(a, axis):
    t = _jnp.moveaxis(a, axis, 0)
    t = t.reshape((N_MICROBATCH, t.shape[0] // N_MICROBATCH) + t.shape[1:])
    return _jnp.moveaxis(t, 1, axis + 1)


def setup_inputs(seed: int = 0) -> dict:
    inp = _fwd_setup_inputs(seed)
    key = _jax.random.fold_in(_jax.random.key(seed), 7919)
    shape, _ = _output_shape()
    out = dict(inp)
    out["loss_target"] = _jax.random.normal(_jax.random.fold_in(key, 0), shape, _jnp.float32)
    for i, name in enumerate(TWIN_WEIGHTS):
        w = inp[name].astype(_jnp.float32)
        if MOMENT_SCALE is None:
            s = _jnp.sqrt(_jnp.mean(_jnp.square(w)) + 1e-30)
        else:
            s = MOMENT_SCALE[name]
        km, kv = _jax.random.split(_jax.random.fold_in(key, i + 1))
        out[name] = w
        out["m_" + name] = s * _jax.random.normal(km, w.shape, _jnp.float32)
        out["v_" + name] = (s * s) * _jax.random.uniform(kv, w.shape, _jnp.float32, 0.5, 1.5)
    if N_MICROBATCH > 1:
        for name, axis in PER_EXAMPLE_BATCH_AXIS.items():
            out[name] = _to_microbatches(out[name], axis)
    return {'x': out['x'], 'c': out['c'], 'norm_mix_g': out['norm_mix_g'], 'norm_mlp_g': out['norm_mlp_g'], 'w_ada': out['w_ada'], 'b_ada': out['b_ada'], 'w_in': out['w_in'], 'conv_w': out['conv_w'], 'conv_b': out['conv_b'], 'w_rg_a': out['w_rg_a'], 'b_rg_a': out['b_rg_a'], 'w_rg_x': out['w_rg_x'], 'b_rg_x': out['b_rg_x'], 'a_param': out['a_param'], 'w_branch_a': out['w_branch_a'], 'w_pool': out['w_pool'], 'b_pool': out['b_pool'], 'pool_scale': out['pool_scale'], 'w_branch_b': out['w_branch_b'], 'w_out': out['w_out'], 'w_up': out['w_up'], 'w_down': out['w_down'], 'final_g': out['final_g'], 'loss_target': out['loss_target'], 'm_norm_mix_g': out['m_norm_mix_g'], 'm_norm_mlp_g': out['m_norm_mlp_g'], 'm_w_ada': out['m_w_ada'], 'm_b_ada': out['m_b_ada'], 'm_w_in': out['m_w_in'], 'm_conv_w': out['m_conv_w'], 'm_conv_b': out['m_conv_b'], 'm_w_rg_a': out['m_w_rg_a'], 'm_b_rg_a': out['m_b_rg_a'], 'm_w_rg_x': out['m_w_rg_x'], 'm_b_rg_x': out['m_b_rg_x'], 'm_a_param': out['m_a_param'], 'm_w_branch_a': out['m_w_branch_a'], 'm_w_pool': out['m_w_pool'], 'm_b_pool': out['m_b_pool'], 'm_pool_scale': out['m_pool_scale'], 'm_w_branch_b': out['m_w_branch_b'], 'm_w_out': out['m_w_out'], 'm_w_up': out['m_w_up'], 'm_w_down': out['m_w_down'], 'm_final_g': out['m_final_g'], 'v_norm_mix_g': out['v_norm_mix_g'], 'v_norm_mlp_g': out['v_norm_mlp_g'], 'v_w_ada': out['v_w_ada'], 'v_b_ada': out['v_b_ada'], 'v_w_in': out['v_w_in'], 'v_conv_w': out['v_conv_w'], 'v_conv_b': out['v_conv_b'], 'v_w_rg_a': out['v_w_rg_a'], 'v_b_rg_a': out['v_b_rg_a'], 'v_w_rg_x': out['v_w_rg_x'], 'v_b_rg_x': out['v_b_rg_x'], 'v_a_param': out['v_a_param'], 'v_w_branch_a': out['v_w_branch_a'], 'v_w_pool': out['v_w_pool'], 'v_b_pool': out['v_b_pool'], 'v_pool_scale': out['v_pool_scale'], 'v_w_branch_b': out['v_w_branch_b'], 'v_w_out': out['v_w_out'], 'v_w_up': out['v_w_up'], 'v_w_down': out['v_w_down'], 'v_final_g': out['v_final_g']}


def _loss(weights, diff, rest, loss_target):
    with _jax.named_scope("forward"):
        args = {**rest, TWIN_DIFF_INPUT: diff, **{k: w.astype(_WEIGHT_DTYPES[k]) for k, w in weights.items()}}
        y = _forward(args)
    with _jax.named_scope("loss_head"):
        err = _jnp.square(y.astype(_jnp.float32) - loss_target)
        return 0.5 * _jnp.sum(_jnp.mean(err, axis=-1)) if err.ndim else 0.5 * err


def _adamw(w, g, m, v):
    m = ADAM_B1 * m + (1.0 - ADAM_B1) * g
    v = ADAM_B2 * v + (1.0 - ADAM_B2) * _jnp.square(g)
    m_hat = m / (1.0 - ADAM_B1 ** ADAM_STEP)
    v_hat = v / (1.0 - ADAM_B2 ** ADAM_STEP)
    delta = -ADAM_LR * (m_hat / (_jnp.sqrt(v_hat) + ADAM_EPS) + ADAM_WD * w)
    return delta, m, v


def reference(x, c, norm_mix_g, norm_mlp_g, w_ada, b_ada, w_in, conv_w, conv_b, w_rg_a, b_rg_a, w_rg_x, b_rg_x, a_param, w_branch_a, w_pool, b_pool, pool_scale, w_branch_b, w_out, w_up, w_down, final_g, loss_target, m_norm_mix_g, m_norm_mlp_g, m_w_ada, m_b_ada, m_w_in, m_conv_w, m_conv_b, m_w_rg_a, m_b_rg_a, m_w_rg_x, m_b_rg_x, m_a_param, m_w_branch_a, m_w_pool, m_b_pool, m_pool_scale, m_w_branch_b, m_w_out, m_w_up, m_w_down, m_final_g, v_norm_mix_g, v_norm_mlp_g, v_w_ada, v_b_ada, v_w_in, v_conv_w, v_conv_b, v_w_rg_a, v_b_rg_a, v_w_rg_x, v_b_rg_x, v_a_param, v_w_branch_a, v_w_pool, v_b_pool, v_pool_scale, v_w_branch_b, v_w_out, v_w_up, v_w_down, v_final_g):
    given = dict(x=x, c=c, norm_mix_g=norm_mix_g, norm_mlp_g=norm_mlp_g, w_ada=w_ada, b_ada=b_ada, w_in=w_in, conv_w=conv_w, conv_b=conv_b, w_rg_a=w_rg_a, b_rg_a=b_rg_a, w_rg_x=w_rg_x, b_rg_x=b_rg_x, a_param=a_param, w_branch_a=w_branch_a, w_pool=w_pool, b_pool=b_pool, pool_scale=pool_scale, w_branch_b=w_branch_b, w_out=w_out, w_up=w_up, w_down=w_down, final_g=final_g, loss_target=loss_target, m_norm_mix_g=m_norm_mix_g, m_norm_mlp_g=m_norm_mlp_g, m_w_ada=m_w_ada, m_b_ada=m_b_ada, m_w_in=m_w_in, m_conv_w=m_conv_w, m_conv_b=m_conv_b, m_w_rg_a=m_w_rg_a, m_b_rg_a=m_b_rg_a, m_w_rg_x=m_w_rg_x, m_b_rg_x=m_b_rg_x, m_a_param=m_a_param, m_w_branch_a=m_w_branch_a, m_w_pool=m_w_pool, m_b_pool=m_b_pool, m_pool_scale=m_pool_scale, m_w_branch_b=m_w_branch_b, m_w_out=m_w_out, m_w_up=m_w_up, m_w_down=m_w_down, m_final_g=m_final_g, v_norm_mix_g=v_norm_mix_g, v_norm_mlp_g=v_norm_mlp_g, v_w_ada=v_w_ada, v_b_ada=v_b_ada, v_w_in=v_w_in, v_conv_w=v_conv_w, v_conv_b=v_conv_b, v_w_rg_a=v_w_rg_a, v_b_rg_a=v_b_rg_a, v_w_rg_x=v_w_rg_x, v_b_rg_x=v_b_rg_x, v_a_param=v_a_param, v_w_branch_a=v_w_branch_a, v_w_pool=v_w_pool, v_b_pool=v_b_pool, v_pool_scale=v_pool_scale, v_w_branch_b=v_w_branch_b, v_w_out=v_w_out, v_w_up=v_w_up, v_w_down=v_w_down, v_final_g=v_final_g)
    weights = {n: given[n] for n in TWIN_WEIGHTS}
    shared = {n: given[n] for n in SHARED_INPUTS}
    per_example = {n: given[n] for n in ['x', 'c']}
    grad_fn = _jax.value_and_grad(_loss, argnums=(0, 1))

    def one_microbatch(ex, loss_target):
        ex = dict(ex)
        diff = ex.pop(TWIN_DIFF_INPUT)
        return grad_fn(weights, diff, {**shared, **ex}, loss_target)

    if N_MICROBATCH == 1:
        loss, (grad_w, grad_x) = one_microbatch(per_example, given["loss_target"])
    else:
        def body(carry, xs):
            loss_sum, grad_sum = carry
            l_k, (gw_k, gx_k) = one_microbatch(xs[0], xs[1])
            with _jax.named_scope("update"):
                return (loss_sum + l_k, _jax.tree.map(_jnp.add, grad_sum, gw_k)), gx_k

        init = (_jnp.zeros((), _jnp.float32), _jax.tree.map(_jnp.zeros_like, weights))
        (loss, grad_w), grad_x = _jax.lax.scan(body, init, (per_example, given["loss_target"]))
    with _jax.named_scope("update"):
        delta_w, new_m, new_v = {}, {}, {}
        for n in TWIN_WEIGHTS:
            delta_w[n], new_m[n], new_v[n] = _adamw(weights[n], grad_w[n], given["m_" + n], given["v_" + n])
    return (loss, grad_x, *[grad_w[n] for n in TWIN_WEIGHTS], *[delta_w[n] for n in TWIN_WEIGHTS],
            *[new_m[n] for n in TWIN_WEIGHTS], *[new_v[n] for n in TWIN_WEIGHTS])
```

```python
import functools

import jax
import jax.numpy as jnp
from jax import lax
from jax.experimental import pallas as pl
from jax.experimental.pallas import tpu as pltpu

F32, BF16 = jnp.float32, jnp.bfloat16
D_MODEL = 1024
D_FF = 4 * D_MODEL
N_BLOCKS = 4
BLOCK = D_MODEL // N_BLOCKS
CONV_WIDTH = 4
POOL_WINDOWS = (2, 4, 8, 16)
CONV_HALO = 8
POOL_HALO = 16
N_MOD = 6
EPS = 1e-6
C_RG = 8.0
ADAM_LR, ADAM_B1, ADAM_B2, ADAM_EPS, ADAM_WD, ADAM_STEP = 0.001, 0.9, 0.999, 1e-08, 0.01, 10
N_DEV = 8
N_CHIPS = 4
LANES = 128
VMEM_LIMIT_V7X = 56 * 2**20
MESH = pl.DeviceIdType.MESH
SDS = jax.ShapeDtypeStruct
HBM_SPEC = pl.BlockSpec(memory_space=pltpu.HBM)
VMEM_SPEC = pl.BlockSpec(memory_space=pltpu.VMEM)


def _params(*semantics):
    return pltpu.CompilerParams(dimension_semantics=semantics, vmem_limit_bytes=VMEM_LIMIT_V7X)


def _coords():
    return lax.axis_index("x"), lax.axis_index("y"), lax.axis_index("c")


def _flip(v, bit):
    return 1 - v if bit else v


def _allgather8(v, name):
    def body(v_ref, out_ref, send_sems, recv_sems, local_sem):
        x, y, c = _coords()
        me = 4 * x + 2 * y + c
        mine = pltpu.make_async_copy(v_ref, out_ref.at[me], local_sem)
        mine.start()
        sends = []
        for r in range(1, N_DEV):
            peer = (_flip(x, r & 4), _flip(y, r & 2), _flip(c, r & 1))
            cp = pltpu.make_async_remote_copy(src_ref=v_ref, dst_ref=out_ref.at[me], send_sem=send_sems.at[r - 1],
                                              recv_sem=recv_sems.at[r - 1], device_id=peer, device_id_type=MESH)
            cp.start()
            sends.append(cp)
        for r in range(1, N_DEV):
            pltpu.make_async_remote_copy(src_ref=v_ref, dst_ref=out_ref.at[me ^ r], send_sem=send_sems.at[r - 1],
                                         recv_sem=recv_sems.at[r - 1], device_id=(x, y, c), device_id_type=MESH).wait_recv()
        for cp in sends:
            cp.wait_send()
        mine.wait()

    return pl.pallas_call(
        body, name=name, out_shape=SDS((N_DEV,) + v.shape, v.dtype), in_specs=[VMEM_SPEC], out_specs=VMEM_SPEC,
        scratch_shapes=[pltpu.SemaphoreType.DMA((N_DEV - 1,)), pltpu.SemaphoreType.DMA((N_DEV - 1,)), pltpu.SemaphoreType.DMA(())],
    )(v)


def _gather_weights(shards, name):
    n = len(shards)

    def body(*refs):
        ins, outs = refs[:n], refs[n:2 * n]
        send_sems, recv_sems, local_sems = refs[2 * n:]
        x, y, c = _coords()
        k_me = 2 * x + y
        chips = [(1 - x, y), (x, 1 - y), (1 - x, 1 - y)]
        sibling = (x, y, 1 - c)

        def copy(i, sem, k, half, to):
            return pltpu.make_async_remote_copy(src_ref=outs[i].at[k, half], dst_ref=outs[i].at[k, half], send_sem=send_sems.at[sem],
                                                recv_sem=recv_sems.at[sem], device_id=to, device_id_type=MESH)

        own, first, passed = [], [], []
        for i in range(n):
            lc = pltpu.make_async_copy(ins[i], outs[i].at[k_me], local_sems.at[i])
            lc.start()
            own.append(lc)
            for j, (px, py) in enumerate(chips):
                cp = pltpu.make_async_remote_copy(src_ref=ins[i].at[c], dst_ref=outs[i].at[k_me, c], send_sem=send_sems.at[6 * i + j],
                                                  recv_sem=recv_sems.at[6 * i + j], device_id=(px, py, c), device_id_type=MESH)
                cp.start()
                first.append(cp)
        for i in range(n):
            for j, (px, py) in enumerate(chips):
                copy(i, 6 * i + j, 2 * px + py, c, (x, y, c)).wait_recv()
                fw = copy(i, 6 * i + 3 + j, 2 * px + py, c, sibling)
                fw.start()
                passed.append(fw)
        for i in range(n):
            for j, (px, py) in enumerate(chips):
                copy(i, 6 * i + 3 + j, 2 * px + py, 1 - c, (x, y, c)).wait_recv()
        for cp in first + passed:
            cp.wait_send()
        for lc in own:
            lc.wait()

    return pl.pallas_call(
        body, name=name, out_shape=[SDS((N_CHIPS,) + s.shape, s.dtype) for s in shards],
        in_specs=[HBM_SPEC] * n, out_specs=[HBM_SPEC] * n,
        scratch_shapes=[pltpu.SemaphoreType.DMA((6 * n,)), pltpu.SemaphoreType.DMA((6 * n,)), pltpu.SemaphoreType.DMA((n,))],
    )(*shards)


def _scatter_partials(grads, name):
    n = len(grads)

    def body(*refs):
        ins, outs = refs[:n], refs[n:2 * n]
        send_sems, recv_sems, local_sems = refs[2 * n:]
        x, y, c = _coords()
        own, sends = [], []
        for i in range(n):
            lc = pltpu.make_async_copy(ins[i].at[2 * x + y, c], outs[i].at[0], local_sems.at[i])
            lc.start()
            own.append(lc)
            for r in range(1, N_DEV):
                tx, ty, tc = _flip(x, r & 4), _flip(y, r & 2), _flip(c, r & 1)
                cp = pltpu.make_async_remote_copy(src_ref=ins[i].at[2 * tx + ty, tc], dst_ref=outs[i].at[r],
                                                  send_sem=send_sems.at[7 * i + r - 1], recv_sem=recv_sems.at[7 * i + r - 1],
                                                  device_id=(tx, ty, tc), device_id_type=MESH)
                cp.start()
                sends.append(cp)
        for i in range(n):
            for r in range(1, N_DEV):
                pltpu.make_async_remote_copy(src_ref=ins[i].at[0, 0], dst_ref=outs[i].at[r], send_sem=send_sems.at[7 * i + r - 1],
                                             recv_sem=recv_sems.at[7 * i + r - 1], device_id=(x, y, c), device_id_type=MESH).wait_recv()
        for cp in sends:
            cp.wait_send()
        for lc in own:
            lc.wait()

    return pl.pallas_call(
        body, name=name, out_shape=[SDS((N_DEV,) + g.shape[2:], g.dtype) for g in grads],
        in_specs=[HBM_SPEC] * n, out_specs=[HBM_SPEC] * n,
        scratch_shapes=[pltpu.SemaphoreType.DMA((7 * n,)), pltpu.SemaphoreType.DMA((7 * n,)), pltpu.SemaphoreType.DMA((n,))],
    )(*grads)


def _pair_exchange(pieces, name):
    n = len(pieces)

    def body(*refs):
        ins, outs = refs[:n], refs[n:2 * n]
        send_sems, recv_sems, local_sems = refs[2 * n:]
        x, y, c = _coords()
        own, sends = [], []
        for i in range(n):
            lc = pltpu.make_async_copy(ins[i], outs[i].at[c], local_sems.at[i])
            lc.start()
            own.append(lc)
            cp = pltpu.make_async_remote_copy(src_ref=ins[i], dst_ref=outs[i].at[c], send_sem=send_sems.at[i], recv_sem=recv_sems.at[i],
                                              device_id=(x, y, 1 - c), device_id_type=MESH)
            cp.start()
            sends.append(cp)
        for i in range(n):
            pltpu.make_async_remote_copy(src_ref=ins[i], dst_ref=outs[i].at[1 - c], send_sem=send_sems.at[i], recv_sem=recv_sems.at[i],
                                         device_id=(x, y, c), device_id_type=MESH).wait_recv()
        for cp in sends:
            cp.wait_send()
        for lc in own:
            lc.wait()

    return pl.pallas_call(
        body, name=name, out_shape=[SDS((2,) + p.shape, p.dtype) for p in pieces],
        in_specs=[HBM_SPEC] * n, out_specs=[HBM_SPEC] * n,
        scratch_shapes=[pltpu.SemaphoreType.DMA((n,)), pltpu.SemaphoreType.DMA((n,)), pltpu.SemaphoreType.DMA((n,))],
    )(*pieces)


def _matmul(a, w, *, transposed, reduce_blocks, out_dtype, name, tm):
    m_rows = a.shape[0]
    nb, r, c = w.shape
    kb, nblk = (c, r) if transposed else (r, c)
    dims = (((1,), (1,)), ((), ())) if transposed else (((1,), (0,)), ((), ()))
    assert a.shape[1] == (nb * kb if reduce_blocks else kb) and m_rows % tm == 0
    n_out = nblk if reduce_blocks else nb * nblk

    def body(a_ref, w_ref, o_ref):
        if reduce_blocks:
            acc = lax.dot_general(a_ref[:, 0:kb], w_ref[0], dims, preferred_element_type=F32)
            for k in range(1, nb):
                acc += lax.dot_general(a_ref[:, k * kb:(k + 1) * kb], w_ref[k], dims, preferred_element_type=F32)
            o_ref[...] = acc.astype(o_ref.dtype)
        else:
            a_blk = a_ref[...]
            for k in range(nb):
                o_ref[:, k * nblk:(k + 1) * nblk] = lax.dot_general(a_blk, w_ref[k], dims, preferred_element_type=F32).astype(o_ref.dtype)

    return pl.pallas_call(
        body, name=name, grid=(m_rows // tm,), out_shape=SDS((m_rows, n_out), out_dtype),
        in_specs=[pl.BlockSpec((tm, a.shape[1]), lambda m: (m, 0)), pl.BlockSpec((nb, r, c), lambda m: (0, 0, 0))],
        out_specs=pl.BlockSpec((tm, n_out), lambda m: (m, 0)), compiler_params=_params("parallel"),
    )(a, w)


def _weight_grad(a, b, *, nb, name, ts, tk):
    s_rows, k1 = a.shape
    tn = b.shape[1] // nb
    n_s = s_rows // ts
    assert s_rows % ts == 0 and k1 % tk == 0 and b.shape[1] % nb == 0

    def body(a_ref, b_ref, o_ref, acc):
        s = pl.program_id(2)
        part = lax.dot_general(a_ref[...], b_ref[...], (((0,), (0,)), ((), ())), preferred_element_type=F32)

        @pl.when(s == 0)
        def _():
            acc[...] = part

        @pl.when(s > 0)
        def _():
            acc[...] += part

        @pl.when(s == n_s - 1)
        def _():
            o_ref[0] = acc[...].astype(o_ref.dtype)

    return pl.pallas_call(
        body, name=name, grid=(nb, k1 // tk, n_s), out_shape=SDS((nb, k1, tn), BF16),
        in_specs=[pl.BlockSpec((ts, tk), lambda n, i, s: (s, i)), pl.BlockSpec((ts, tn), lambda n, i, s: (s, n))],
        out_specs=pl.BlockSpec((1, tk, tn), lambda n, i, s: (n, i, 0)), scratch_shapes=[pltpu.VMEM((tk, tn), F32)],
        compiler_params=_params("parallel", "parallel", "arbitrary"),
    )(a, b)


def _row_spec(ts, cols, col_block=0):
    return pl.BlockSpec((ts, cols), lambda t: (t, col_block))


def _vec_spec(rows, cols):
    return pl.BlockSpec((rows, cols), lambda t: (0, 0))


def _rms(x):
    r = lax.rsqrt(jnp.mean(x * x, axis=-1, keepdims=True) + EPS)
    return x * r, r


def _accumulate(acc_ref, row, value):
    acc_ref[row:row + 1, :] += jnp.sum(value, axis=0, keepdims=True)


def _gelu_parts(y):
    k0, k1 = 0.7978845608028654, 0.044715
    th = jnp.tanh(k0 * (y + k1 * (y * y * y)))
    gelu = 0.5 * y * (1.0 + th)
    dgelu = 0.5 * (1.0 + th) + 0.5 * y * (1.0 - th * th) * (k0 * (1.0 + 3.0 * k1 * (y * y)))
    return gelu, dgelu


def _neg_expm1(z):
    series = z * (1.0 + z * (0.5 + z * (1.0 / 6 + z * (1.0 / 24 + z * (1.0 / 120 + z * (1.0 / 720))))))
    return -jnp.where(z > -0.25, series, jnp.exp(z) - 1.0)


def _norm_mod(x, mod, gain, *, sc_row, sh_row, name, ts):
    s_rows, d = x.shape

    def body(x_ref, mod_ref, g_ref, h_ref):
        xh, _ = _rms(x_ref[...])
        h = (xh * g_ref[...]) * (1.0 + mod_ref[sc_row:sc_row + 1, :]) + mod_ref[sh_row:sh_row + 1, :]
        h_ref[...] = h.astype(BF16)

    return pl.pallas_call(
        body, name=name, grid=(s_rows // ts,), out_shape=SDS((s_rows, d), BF16),
        in_specs=[_row_spec(ts, d), _vec_spec(8, d), _vec_spec(1, d)], out_specs=_row_spec(ts, d),
        compiler_params=_params("parallel"),
    )(x, mod, gain)


def _seq_forward(proj, conv_w, vecs, w_a, w_x, w_p, *, ts):
    s_rows = proj.shape[0]
    d = D_MODEL

    def body(x_ref, xh_ref, y_ref, u_ref, uh_ref, cw_ref, vec_ref, wa_ref, wx_ref, wp_ref,
             xr_o, hr_o, ga_o, p_o, pooled_o, carry, a_scr, b_scr):
        t = pl.program_id(0)

        @pl.when(t == 0)
        def _():
            carry[...] = jnp.zeros_like(carry)

        x = x_ref[...]
        halo = jnp.where(t == 0, 0.0, xh_ref[...])
        xx = jnp.concatenate([halo, x], axis=0)
        xr = vec_ref[0:1, :] + x * cw_ref[CONV_WIDTH - 1:CONV_WIDTH, :]
        for j in range(CONV_WIDTH - 1):
            xr = xr + pltpu.roll(xx, CONV_WIDTH - 1 - j, 0)[CONV_HALO:] * cw_ref[j:j + 1, :]
        xr_o[...] = xr
        xrb = xr.astype(BF16)
        zr = jnp.concatenate([jnp.dot(xrb[:, g * BLOCK:(g + 1) * BLOCK], wa_ref[g], preferred_element_type=F32)
                              for g in range(N_BLOCKS)], axis=1) + vec_ref[1:2, :]
        zi = jnp.concatenate([jnp.dot(xrb[:, g * BLOCK:(g + 1) * BLOCK], wx_ref[g], preferred_element_type=F32)
                              for g in range(N_BLOCKS)], axis=1) + vec_ref[2:3, :]
        r = jax.nn.sigmoid(zr)
        gate_i = jax.nn.sigmoid(zi)
        log_a = (-C_RG * r) * jax.nn.softplus(vec_ref[3:4, :])
        rows = lax.broadcasted_iota(jnp.int32, (ts, d), 0)
        mult = jnp.where((rows == 0) & (t == 0), 1.0, jnp.sqrt(_neg_expm1(2.0 * log_a)))
        a_scr[...] = jnp.exp(log_a)
        b_scr[...] = xr * gate_i * mult

        sub = lax.broadcasted_iota(jnp.int32, (8, d), 0)

        def chunk(i, h):
            a = a_scr[pl.ds(i * 8, 8), :]
            b = b_scr[pl.ds(i * 8, 8), :]
            for s in (1, 2, 4):
                keep = sub >= s
                b = jnp.where(keep, a * pltpu.roll(b, s, 0) + b, b)
                a = jnp.where(keep, a * pltpu.roll(a, s, 0), a)
            hh = a * h + b
            hr_o[pl.ds(i * 8, 8), :] = hh
            return hh[7:8, :]

        carry[...] = lax.fori_loop(0, ts // 8, chunk, carry[...])
        gelu, _ = _gelu_parts(y_ref[...])
        ga_o[...] = (gelu * hr_o[...]).astype(BF16)

        u = u_ref[...]
        uu = jnp.concatenate([jnp.where(t == 0, 0.0, uh_ref[...]), u], axis=0)
        pos = (rows + t * ts + 1).astype(F32)
        sums, have, parts = uu, 1, []
        for g, win in enumerate(POOL_WINDOWS):
            while have < win:
                sums = sums + pltpu.roll(sums, have, 0)
                have *= 2
            mean = sums[POOL_HALO:, g * BLOCK:(g + 1) * BLOCK] / jnp.minimum(pos[:, :BLOCK], float(win))
            parts.append(mean - u[:, g * BLOCK:(g + 1) * BLOCK])
        pb = jnp.concatenate(parts, axis=1).astype(BF16)
        p_o[...] = pb
        mixed = jnp.concatenate([jnp.dot(pb[:, g * BLOCK:(g + 1) * BLOCK], wp_ref[g], preferred_element_type=F32)
                                 for g in range(N_BLOCKS)], axis=1) + vec_ref[4:5, :]
        pooled_o[...] = (mixed * vec_ref[5:6, :]).astype(BF16)

    halo_x = pl.BlockSpec((CONV_HALO, d), lambda t: (jnp.maximum(t * (ts // CONV_HALO) - 1, 0), 0))
    halo_u = pl.BlockSpec((POOL_HALO, d), lambda t: (jnp.maximum(t * (ts // POOL_HALO) - 1, 0), 2))
    wspec = pl.BlockSpec((N_BLOCKS, BLOCK, BLOCK), lambda t: (0, 0, 0))
    return pl.pallas_call(
        body, name="seq_forward", grid=(s_rows // ts,),
        out_shape=[SDS((s_rows, d), F32), SDS((s_rows, d), F32), SDS((s_rows, d), BF16), SDS((s_rows, d), BF16), SDS((s_rows, d), BF16)],
        in_specs=[_row_spec(ts, d, 0), halo_x, _row_spec(ts, d, 1), _row_spec(ts, d, 2), halo_u, _vec_spec(CONV_WIDTH, d), _vec_spec(8, d),
                  wspec, wspec, wspec],
        out_specs=[_row_spec(ts, d)] * 5,
        scratch_shapes=[pltpu.VMEM((1, d), F32), pltpu.VMEM((ts, d), F32), pltpu.VMEM((ts, d), F32)],
        compiler_params=_params("arbitrary"),
    )(proj, proj, proj, proj, proj, conv_w, vecs, w_a, w_x, w_p)


def _merge_forward(proj, br_a, br_b, *, ts):
    s_rows, d = br_a.shape

    def body(ga_ref, gb_ref, a_ref, b_ref, o_ref):
        o_ref[...] = (jax.nn.sigmoid(ga_ref[...]) * a_ref[...] + jax.nn.sigmoid(gb_ref[...]) * b_ref[...]).astype(BF16)

    return pl.pallas_call(
        body, name="merge_forward", grid=(s_rows // ts,), out_shape=SDS((s_rows, d), BF16),
        in_specs=[_row_spec(ts, d, 3), _row_spec(ts, d, 4), _row_spec(ts, d), _row_spec(ts, d)], out_specs=_row_spec(ts, d),
        compiler_params=_params("parallel"),
    )(proj, proj, br_a, br_b)


def _residual_norm(x, mo, mod, gain, *, ts):
    s_rows, d = x.shape

    def body(x_ref, mo_ref, mod_ref, g_ref, x2_ref, h_ref):
        x2 = x_ref[...] + mod_ref[2:3, :] * mo_ref[...]
        x2_ref[...] = x2
        xh, _ = _rms(x2)
        h_ref[...] = ((xh * g_ref[...]) * (1.0 + mod_ref[4:5, :]) + mod_ref[3:4, :]).astype(BF16)

    return pl.pallas_call(
        body, name="residual_norm", grid=(s_rows // ts,), out_shape=[SDS((s_rows, d), F32), SDS((s_rows, d), BF16)],
        in_specs=[_row_spec(ts, d), _row_spec(ts, d), _vec_spec(8, d), _vec_spec(1, d)], out_specs=[_row_spec(ts, d)] * 2,
        compiler_params=_params("parallel"),
    )(x, mo, mod, gain)


def _relu_squared(up, *, ts):
    s_rows, f = up.shape

    def body(u_ref, o_ref):
        o_ref[...] = jnp.square(jnp.maximum(u_ref[...], 0.0)).astype(BF16)

    return pl.pallas_call(
        body, name="relu_squared", grid=(s_rows // ts,), out_shape=SDS((s_rows, f), BF16),
        in_specs=[_row_spec(ts, f)], out_specs=_row_spec(ts, f), compiler_params=_params("parallel"),
    )(up)


def _loss_head(x2, ffo, mod, final_g, target, *, ts):
    s_rows, d = x2.shape

    def body(x2_ref, f_ref, mod_ref, g_ref, t_ref, dx_ref, df_ref, acc_ref):
        @pl.when(pl.program_id(0) == 0)
        def _():
            acc_ref[...] = jnp.zeros_like(acc_ref)

        ffo_t = f_ref[...]
        x3 = x2_ref[...] + mod_ref[5:6, :] * ffo_t
        xh, r = _rms(x3)
        err = xh * g_ref[...] - t_ref[...]
        dy = err / d
        dxh = dy * g_ref[...]
        dx3 = r * (dxh - xh * jnp.mean(dxh * xh, axis=-1, keepdims=True))
        dx_ref[...] = dx3
        df_ref[...] = (dx3 * mod_ref[5:6, :]).astype(BF16)
        _accumulate(acc_ref, 0, dy * xh)
        _accumulate(acc_ref, 1, dx3 * ffo_t)
        _accumulate(acc_ref, 2, jnp.square(err) * (0.5 / d))

    return pl.pallas_call(
        body, name="loss_head", grid=(s_rows // ts,), out_shape=[SDS((s_rows, d), F32), SDS((s_rows, d), BF16), SDS((8, d), F32)],
        in_specs=[_row_spec(ts, d), _row_spec(ts, d), _vec_spec(8, d), _vec_spec(1, d), _row_spec(ts, d)],
        out_specs=[_row_spec(ts, d), _row_spec(ts, d), _vec_spec(8, d)], compiler_params=_params("arbitrary"),
    )(x2, ffo, mod, final_g, target)


def _relu_squared_backward(dff, up, *, ts):
    s_rows, f = up.shape

    def body(d_ref, u_ref, o_ref):
        o_ref[...] = (d_ref[...] * (2.0 * jnp.maximum(u_ref[...], 0.0))).astype(BF16)

    return pl.pallas_call(
        body, name="relu_squared_backward", grid=(s_rows // ts,), out_shape=SDS((s_rows, f), BF16),
        in_specs=[_row_spec(ts, f), _row_spec(ts, f)], out_specs=_row_spec(ts, f), compiler_params=_params("parallel"),
    )(dff, up)


def _norm_mod_backward(dh, x, dres, mod, gain, mo, *, sc_row, gate_row, name, ts):
    s_rows, d = x.shape
    gated = mo is not None

    def body(*refs):
        if gated:
            dh_ref, x_ref, dres_ref, mod_ref, g_ref, mo_ref, dx_ref, dmo_ref, acc_ref = refs
        else:
            dh_ref, x_ref, dres_ref, mod_ref, g_ref, dx_ref, acc_ref = refs

        @pl.when(pl.program_id(0) == 0)
        def _():
            acc_ref[...] = jnp.zeros_like(acc_ref)

        dh_t = dh_ref[...]
        xh, r = _rms(x_ref[...])
        _accumulate(acc_ref, 0, dh_t * (xh * g_ref[...]))
        _accumulate(acc_ref, 1, dh_t)
        dn = dh_t * (1.0 + mod_ref[sc_row:sc_row + 1, :])
        _accumulate(acc_ref, 2, dn * xh)
        dxh = dn * g_ref[...]
        dx = dres_ref[...] + r * (dxh - xh * jnp.mean(dxh * xh, axis=-1, keepdims=True))
        dx_ref[...] = dx
        if gated:
            dmo_ref[...] = (dx * mod_ref[gate_row:gate_row + 1, :]).astype(BF16)
            _accumulate(acc_ref, 3, dx * mo_ref[...])

    rows = _row_spec(ts, d)
    return pl.pallas_call(
        body, name=name, grid=(s_rows // ts,),
        out_shape=[SDS((s_rows, d), F32)] + ([SDS((s_rows, d), BF16)] if gated else []) + [SDS((8, d), F32)],
        in_specs=[rows, rows, rows, _vec_spec(8, d), _vec_spec(1, d)] + ([rows] if gated else []),
        out_specs=[rows] + ([rows] if gated else []) + [_vec_spec(8, d)], compiler_params=_params("arbitrary"),
    )(*([dh, x, dres, mod, gain] + ([mo] if gated else [])))


def _merge_backward(dmerged, proj, br_a, br_b, *, ts):
    s_rows, d = br_a.shape

    def body(dm_ref, ga_ref, gb_ref, a_ref, b_ref, da_ref, db_ref, dg_ref):
        dm = dm_ref[...]
        sa, sb = jax.nn.sigmoid(ga_ref[...]), jax.nn.sigmoid(gb_ref[...])
        da_ref[...] = (dm * sa).astype(BF16)
        db_ref[...] = (dm * sb).astype(BF16)
        dg_ref[:, 0:d] = (dm * a_ref[...] * (sa * (1.0 - sa))).astype(BF16)
        dg_ref[:, d:2 * d] = (dm * b_ref[...] * (sb * (1.0 - sb))).astype(BF16)

    rows = _row_spec(ts, d)
    return pl.pallas_call(
        body, name="merge_backward", grid=(s_rows // ts,), out_shape=[SDS((s_rows, d), BF16), SDS((s_rows, d), BF16), SDS((s_rows, 2 * d), BF16)],
        in_specs=[rows, _row_spec(ts, d, 3), _row_spec(ts, d, 4), rows, rows], out_specs=[rows, rows, _row_spec(ts, 2 * d)],
        compiler_params=_params("parallel"),
    )(dmerged, proj, proj, br_a, br_b)


def _seq_backward(proj, xr, hr, p, dga, dpooled, dgab, conv_w, vecs, w_a, w_x, w_p, *, ts):
    s_rows = proj.shape[0]
    d = D_MODEL
    n_t = s_rows // ts

    def block_dots(lhs, w_ref, dims):
        return jnp.concatenate([lax.dot_general(lhs[:, g * BLOCK:(g + 1) * BLOCK], w_ref[g], dims, preferred_element_type=F32)
                                for g in range(N_BLOCKS)], axis=1)

    def add_block_grads(dw_ref, lhs, rhs):
        for g in range(N_BLOCKS):
            dw_ref[g] += lax.dot_general(lhs[:, g * BLOCK:(g + 1) * BLOCK], rhs[:, g * BLOCK:(g + 1) * BLOCK],
                                         (((0,), (0,)), ((), ())), preferred_element_type=F32)

    nn, nt = (((1,), (0,)), ((), ())), (((1,), (1,)), ((), ()))

    def body(x_ref, y_ref, xr_ref, hr_ref, hh_ref, p_ref, dga_ref, dpl_ref, dgab_ref, cw_ref, vec_ref, wa_ref, wx_ref, wp_ref,
             dproj_o, acc_o, dwa_o, dwx_o, dwp_o, g_carry, dxr_carry, q_carry, a_scr, b_scr, g_scr):
        step = pl.program_id(0)
        t = n_t - 1 - step

        @pl.when(step == 0)
        def _():
            for ref in (acc_o, dwa_o, dwx_o, dwp_o, g_carry, dxr_carry, q_carry):
                ref[...] = jnp.zeros_like(ref)

        rows = lax.broadcasted_iota(jnp.int32, (ts, d), 0)
        start = (rows == 0) & (t == 0)
        xr = xr_ref[...]
        xrb = xr.astype(BF16)
        r = jax.nn.sigmoid(block_dots(xrb, wa_ref, nn) + vec_ref[1:2, :])
        gate_i = jax.nn.sigmoid(block_dots(xrb, wx_ref, nn) + vec_ref[2:3, :])
        sp = jax.nn.softplus(vec_ref[3:4, :])
        log_a = (-C_RG * r) * sp
        a = jnp.exp(log_a)
        m_raw = jnp.sqrt(_neg_expm1(2.0 * log_a))
        mult = jnp.where(start, 1.0, m_raw)
        gelu, dgelu = _gelu_parts(y_ref[...])
        dga_t = dga_ref[...]
        hr_t = hr_ref[...]
        dy = dga_t * hr_t * dgelu

        a_scr[...] = jnp.where(rows == ts - 1, 1.0, pltpu.roll(a, ts - 1, 0))
        b_scr[...] = dga_t * gelu
        sub = lax.broadcasted_iota(jnp.int32, (8, d), 0)

        def chunk(i, g_next):
            at = pl.multiple_of((ts // 8 - 1 - i) * 8, 8)
            aa = a_scr[pl.ds(at, 8), :]
            bb = b_scr[pl.ds(at, 8), :]
            for s in (1, 2, 4):
                keep = sub < 8 - s
                bb = jnp.where(keep, bb + aa * pltpu.roll(bb, 8 - s, 0), bb)
                aa = jnp.where(keep, aa * pltpu.roll(aa, 8 - s, 0), aa)
            gg = aa * g_next + bb
            g_scr[pl.ds(at, 8), :] = gg
            return gg[0:1, :]

        g_first = lax.fori_loop(0, ts // 8, chunk, g_carry[...])
        g_carry[...] = a[0:1, :] * g_first
        g = g_scr[...]

        h_before = jnp.where(t == 0, 0.0, hh_ref[CONV_HALO - 1:CONV_HALO, :])
        h_prev = jnp.where(rows == 0, h_before, pltpu.roll(hr_t, 1, 0))
        dxr = g * gate_i * mult
        d_i = g * xr * mult
        d_mult = g * xr * gate_i
        dlog_a = (g * h_prev) * a - jnp.where(start, 0.0, d_mult * (a * a) / m_raw)
        dzr = (dlog_a * (-C_RG * sp)) * (r * (1.0 - r))
        dzi = d_i * (gate_i * (1.0 - gate_i))
        _accumulate(acc_o, 5, dzr)
        _accumulate(acc_o, 6, dzi)
        _accumulate(acc_o, 7, dlog_a * (-C_RG * r) * jax.nn.sigmoid(vec_ref[3:4, :]))
        dzrb, dzib = dzr.astype(BF16), dzi.astype(BF16)
        add_block_grads(dwa_o, xrb, dzrb)
        add_block_grads(dwx_o, xrb, dzib)
        dxr = dxr + block_dots(dzrb, wa_ref, nt) + block_dots(dzib, wx_ref, nt)

        x = x_ref[...]
        ext = jnp.concatenate([dxr, dxr_carry[...]], axis=0)
        dx = dxr * cw_ref[CONV_WIDTH - 1:CONV_WIDTH, :]
        _accumulate(acc_o, CONV_WIDTH - 1, x * dxr)
        for j in range(CONV_WIDTH - 1):
            ahead = pltpu.roll(ext, ts + CONV_HALO - (CONV_WIDTH - 1 - j), 0)[:ts]
            dx = dx + ahead * cw_ref[j:j + 1, :]
            _accumulate(acc_o, j, x * ahead)
        _accumulate(acc_o, 4, dxr)
        dxr_carry[...] = dxr[0:CONV_HALO, :]

        pb = p_ref[...]
        mixed = block_dots(pb, wp_ref, nn) + vec_ref[4:5, :]
        dpl = dpl_ref[...]
        _accumulate(acc_o, 9, dpl * mixed)
        dmixed = dpl * vec_ref[5:6, :]
        _accumulate(acc_o, 8, dmixed)
        dmb = dmixed.astype(BF16)
        add_block_grads(dwp_o, pb, dmb)
        dp = block_dots(dmb, wp_ref, nt)
        pos = (rows + t * ts + 1).astype(F32)[:, :BLOCK]
        q = jnp.concatenate([dp[:, g * BLOCK:(g + 1) * BLOCK] / jnp.minimum(pos, float(win)) for g, win in enumerate(POOL_WINDOWS)], axis=1)
        sums, have, parts = jnp.concatenate([q, q_carry[...]], axis=0), 1, []
        for g, win in enumerate(POOL_WINDOWS):
            while have < win:
                sums = sums + pltpu.roll(sums, ts + POOL_HALO - have, 0)
                have *= 2
            parts.append(sums[:ts, g * BLOCK:(g + 1) * BLOCK])
        du = jnp.concatenate(parts, axis=1) - dp
        q_carry[...] = q[0:POOL_HALO, :]

        dproj_o[:, 0:d] = dx.astype(BF16)
        dproj_o[:, d:2 * d] = dy.astype(BF16)
        dproj_o[:, 2 * d:3 * d] = du.astype(BF16)
        dproj_o[:, 3 * d:5 * d] = dgab_ref[...]

    def rev(cols, col_block=0):
        return pl.BlockSpec((ts, cols), lambda i: (n_t - 1 - i, col_block))

    halo_h = pl.BlockSpec((CONV_HALO, d), lambda i: (jnp.maximum((n_t - 1 - i) * (ts // CONV_HALO) - 1, 0), 0))
    wspec = pl.BlockSpec((N_BLOCKS, BLOCK, BLOCK), lambda i: (0, 0, 0))
    const2 = lambda rows_, cols: pl.BlockSpec((rows_, cols), lambda i: (0, 0))
    return pl.pallas_call(
        body, name="seq_backward", grid=(n_t,),
        out_shape=[SDS((s_rows, 5 * d), BF16), SDS((16, d), F32)] + [SDS((N_BLOCKS, BLOCK, BLOCK), F32)] * 3,
        in_specs=[rev(d, 0), rev(d, 1), rev(d), rev(d), halo_h, rev(d), rev(d), rev(d), rev(2 * d), const2(CONV_WIDTH, d), const2(8, d),
                  wspec, wspec, wspec],
        out_specs=[rev(5 * d), const2(16, d), wspec, wspec, wspec],
        scratch_shapes=[pltpu.VMEM((1, d), F32), pltpu.VMEM((CONV_HALO, d), F32), pltpu.VMEM((POOL_HALO, d), F32),
                        pltpu.VMEM((ts, d), F32), pltpu.VMEM((ts, d), F32), pltpu.VMEM((ts, d), F32)],
        compiler_params=_params("arbitrary"),
    )(proj, proj, xr, hr, hr, p, dga, dpooled, dgab, conv_w, vecs, w_a, w_x, w_p)


def _sum_partials(parts, name):
    _, rows, cols = parts.shape
    tr = min(rows, 256)

    def body(p_ref, o_ref):
        acc = p_ref[0].astype(F32)
        for r in range(1, N_DEV):
            acc = acc + p_ref[r].astype(F32)
        o_ref[...] = acc

    return pl.pallas_call(
        body, name=name, grid=(rows // tr,), out_shape=SDS((rows, cols), F32),
        in_specs=[pl.BlockSpec((N_DEV, tr, cols), lambda i: (0, i, 0))], out_specs=pl.BlockSpec((tr, cols), lambda i: (i, 0)),
        compiler_params=_params("parallel"),
    )(parts)


def _adamw_math(w, g, m, v):
    m = ADAM_B1 * m + (1.0 - ADAM_B1) * g
    v = ADAM_B2 * v + (1.0 - ADAM_B2) * jnp.square(g)
    m_hat = m / (1.0 - ADAM_B1 ** ADAM_STEP)
    v_hat = v / (1.0 - ADAM_B2 ** ADAM_STEP)
    delta = -ADAM_LR * (m_hat / (jnp.sqrt(v_hat) + ADAM_EPS) + ADAM_WD * w)
    return delta, m, v


def _adamw(w, g, m, v, name):
    rows, cols = w.shape
    tr = min(rows, 256)

    def body(w_ref, g_ref, m_ref, v_ref, d_ref, mo_ref, vo_ref):
        d_ref[...], mo_ref[...], vo_ref[...] = _adamw_math(w_ref[...], g_ref[...], m_ref[...], v_ref[...])

    spec = pl.BlockSpec((tr, cols), lambda i: (i, 0))
    return pl.pallas_call(
        body, name=name, grid=(rows // tr,), out_shape=[SDS((rows, cols), F32)] * 3, in_specs=[spec] * 4, out_specs=[spec] * 3,
        compiler_params=_params("parallel"),
    )(w, g, m, v)


def _ada_forward(c_all, w_ada, b_cols):
    n_cols = w_ada.shape[1]

    def body(c_ref, w_ref, b_ref, act_ref, mod_ref):
        cv = c_ref[...]
        act = cv * jax.nn.sigmoid(cv)
        act_ref[...] = act
        mod_ref[...] = jnp.dot(act.astype(BF16), w_ref[...].astype(BF16), preferred_element_type=F32) + b_ref[...]

    return pl.pallas_call(
        body, name="ada_forward", out_shape=[SDS(c_all.shape, F32), SDS((N_DEV, n_cols), F32)],
        in_specs=[VMEM_SPEC] * 3, out_specs=[VMEM_SPEC] * 2, compiler_params=pltpu.CompilerParams(vmem_limit_bytes=VMEM_LIMIT_V7X),
    )(c_all, w_ada, b_cols)


def _ada_backward(act_t, dmod_cols, w, m, v):
    rows, cols = w.shape
    tn = 512

    def body(a_ref, dm_ref, w_ref, m_ref, v_ref, g_ref, d_ref, mo_ref, vo_ref):
        act = a_ref[...].astype(BF16).astype(F32)
        dm = dm_ref[...].astype(BF16).astype(F32)
        g = act[:, 0:1] * dm[0:1, :]
        for b in range(1, N_DEV):
            g = g + act[:, b:b + 1] * dm[b:b + 1, :]
        g_ref[...] = g
        d_ref[...], mo_ref[...], vo_ref[...] = _adamw_math(w_ref[...], g, m_ref[...], v_ref[...])

    spec = pl.BlockSpec((rows, tn), lambda j: (0, j))
    return pl.pallas_call(
        body, name="ada_backward", grid=(cols // tn,), out_shape=[SDS((rows, cols), F32)] * 4,
        in_specs=[pl.BlockSpec((rows, N_DEV), lambda j: (0, 0)), pl.BlockSpec((N_DEV, tn), lambda j: (0, j)), spec, spec, spec],
        out_specs=[spec] * 4, compiler_params=_params("parallel"),
    )(act_t, dmod_cols, w, m, v)


def _sum_rows(gathered, n_loss):
    n = gathered.shape[2]

    def body(g_ref, o_ref):
        acc = g_ref[0]
        for r in range(1, N_DEV):
            acc = acc + g_ref[r]
        o_ref[...] = acc
        o_ref[:, n - n_loss:n] = jnp.broadcast_to(jnp.sum(acc[:, n - n_loss:n], axis=1, keepdims=True), (1, n_loss))

    return pl.pallas_call(body, name="sum_rows", out_shape=SDS((1, n), F32), in_specs=[VMEM_SPEC], out_specs=VMEM_SPEC)(gathered)


def _adamw_row(w, g, m, v):
    def body(w_ref, g_ref, m_ref, v_ref, d_ref, mo_ref, vo_ref):
        d_ref[...], mo_ref[...], vo_ref[...] = _adamw_math(w_ref[...], g_ref[...], m_ref[...], v_ref[...])

    return pl.pallas_call(body, name="adamw_vectors", out_shape=[SDS(w.shape, F32)] * 3, in_specs=[VMEM_SPEC] * 4, out_specs=[VMEM_SPEC] * 3)(w, g, m, v)


def _local_step(x, target, mod, gains, conv_w, vecs, wts, *, ts, tsq):
    g_mix, g_mlp, g_fin = gains
    mm = functools.partial(_matmul, tm=ts)
    h1 = _norm_mod(x, mod, g_mix, sc_row=1, sh_row=0, name="norm_mix", ts=ts)
    proj = mm(h1, wts["w_in"], transposed=False, reduce_blocks=False, out_dtype=F32, name="proj_in")
    xr, hr, ga, p, pooled = _seq_forward(proj, conv_w, vecs, wts["rg_a"], wts["rg_x"], wts["pool"], ts=tsq)
    br_a = mm(ga, wts["w_a"], transposed=False, reduce_blocks=False, out_dtype=F32, name="branch_a")
    br_b = mm(pooled, wts["w_b"], transposed=False, reduce_blocks=False, out_dtype=F32, name="branch_b")
    merged = _merge_forward(proj, br_a, br_b, ts=ts)
    mo = mm(merged, wts["w_out"], transposed=False, reduce_blocks=False, out_dtype=F32, name="mix_out")
    x2, h2 = _residual_norm(x, mo, mod, g_mlp, ts=ts)
    up = mm(h2, wts["w_up"], transposed=False, reduce_blocks=False, out_dtype=F32, name="mlp_up")
    ff = _relu_squared(up, ts=tsq)
    ffo = mm(ff, wts["w_down"], transposed=False, reduce_blocks=True, out_dtype=F32, name="mlp_down")
    dx3, dffo, sums_head = _loss_head(x2, ffo, mod, g_fin, target, ts=ts)

    dff = mm(dffo, wts["w_down"], transposed=True, reduce_blocks=False, out_dtype=F32, name="d_mlp_down")
    dup = _relu_squared_backward(dff, up, ts=tsq)
    dh2 = mm(dup, wts["w_up"], transposed=True, reduce_blocks=True, out_dtype=F32, name="d_mlp_up")
    dx2, dmo, sums_mlp = _norm_mod_backward(dh2, x2, dx3, mod, g_mlp, mo, sc_row=4, gate_row=2, name="norm_mlp_backward", ts=ts)
    dmerged = mm(dmo, wts["w_out"], transposed=True, reduce_blocks=False, out_dtype=F32, name="d_mix_out")
    dbr_a, dbr_b, dgab = _merge_backward(dmerged, proj, br_a, br_b, ts=ts)
    dga = mm(dbr_a, wts["w_a"], transposed=True, reduce_blocks=False, out_dtype=F32, name="d_branch_a")
    dpooled = mm(dbr_b, wts["w_b"], transposed=True, reduce_blocks=False, out_dtype=F32, name="d_branch_b")
    dproj, sums_seq, d_rg_a, d_rg_x, d_pool = _seq_backward(proj, xr, hr, p, dga, dpooled, dgab, conv_w, vecs,
                                                           wts["rg_a"], wts["rg_x"], wts["pool"], ts=tsq)
    dh1 = mm(dproj, wts["w_in"], transposed=True, reduce_blocks=True, out_dtype=F32, name="d_proj_in")
    grad_x, sums_mix = _norm_mod_backward(dh1, x, dx2, mod, g_mix, None, sc_row=1, gate_row=None, name="norm_mix_backward", ts=ts)

    wg = functools.partial(_weight_grad, ts=ts, tk=min(512, D_MODEL))
    grads = {
        "w_in": wg(h1, dproj, nb=N_CHIPS, name="grad_w_in"),
        "w_up": wg(h2, dup, nb=N_CHIPS, name="grad_w_up"),
        "w_down": wg(ff, dffo, nb=1, name="grad_w_down"),
        "w_a": wg(ga, dbr_a, nb=1, name="grad_w_branch_a"),
        "w_b": wg(pooled, dbr_b, nb=1, name="grad_w_branch_b"),
        "w_out": wg(merged, dmo, nb=1, name="grad_w_out"),
        "rg_a": d_rg_a, "rg_x": d_rg_x, "pool": d_pool,
    }
    return grad_x, (sums_mix, sums_mlp, sums_seq, sums_head), grads


def _blocks_to_pieces(w):
    nb, rows, n = w.shape
    q = rows // N_CHIPS
    return w.reshape(nb, N_CHIPS, q, n).transpose(1, 0, 2, 3).reshape(N_CHIPS, 2, nb // 2, q, n)


def _pieces_to_blocks(w):
    n_chips, _, half, q, n = w.shape
    return w.reshape(n_chips, 2 * half, q, n).transpose(1, 0, 2, 3).reshape(2 * half, n_chips * q, n)


def kernel(x, c, norm_mix_g, norm_mlp_g, w_ada, b_ada, w_in, conv_w, conv_b, w_rg_a, b_rg_a, w_rg_x, b_rg_x, a_param, w_branch_a, w_pool, b_pool, pool_scale, w_branch_b, w_out, w_up, w_down, final_g, loss_target, m_norm_mix_g, m_norm_mlp_g, m_w_ada, m_b_ada, m_w_in, m_conv_w, m_conv_b, m_w_rg_a, m_b_rg_a, m_w_rg_x, m_b_rg_x, m_a_param, m_w_branch_a, m_w_pool, m_b_pool, m_pool_scale, m_w_branch_b, m_w_out, m_w_up, m_w_down, m_final_g, v_norm_mix_g, v_norm_mlp_g, v_w_ada, v_b_ada, v_w_in, v_conv_w, v_conv_b, v_w_rg_a, v_b_rg_a, v_w_rg_x, v_b_rg_x, v_a_param, v_w_branch_a, v_w_pool, v_b_pool, v_pool_scale, v_w_branch_b, v_w_out, v_w_up, v_w_down, v_final_g):
    d = D_MODEL
    s_rows = x.shape[1]
    ts, tsq = min(512, s_rows), min(256, s_rows)
    xi, yi, ci = _coords()
    k_me, me = 2 * xi + yi, 4 * xi + 2 * yi + ci
    ada_cols = w_ada.shape[2]
    conv_cols = conv_w.shape[2]

    c_all = _allgather8(c, "gather_c").reshape(N_DEV, d)
    act_all, mod_cols = _ada_forward(c_all, w_ada[0], lax.dynamic_slice_in_dim(b_ada, k_me * ada_cols, ada_cols, axis=1))
    mod_all = _allgather8(mod_cols, "gather_mod")
    mod_mine = lax.dynamic_index_in_dim(mod_all, me, axis=1, keepdims=False)[0::2]
    mod = jnp.concatenate([mod_mine.reshape(N_MOD, d), jnp.zeros((8 - N_MOD, d), F32)], axis=0)
    conv_full = _allgather8(conv_w[0], "gather_conv_w")[0::2].transpose(1, 0, 2).reshape(CONV_WIDTH, d)
    vecs = jnp.concatenate([conv_b, b_rg_a, b_rg_x, a_param, b_pool, pool_scale, jnp.zeros((2, d), F32)], axis=0)

    def halves(w):
        return w.astype(BF16).reshape((2, w.shape[0] // 2) + w.shape[1:])

    names = ("w_in", "w_up", "w_down", "w_a", "w_b", "w_out", "rg_a", "rg_x", "pool")
    mine = dict(zip(names, (w_in[0], w_up[0], w_down[0], w_branch_a[0], w_branch_b[0], w_out[0], w_rg_a[0], w_rg_x[0], w_pool[0])))
    moments_m = dict(zip(names, (m_w_in[0], m_w_up[0], m_w_down[0], m_w_branch_a[0], m_w_branch_b[0], m_w_out[0], m_w_rg_a[0], m_w_rg_x[0], m_w_pool[0])))
    moments_v = dict(zip(names, (v_w_in[0], v_w_up[0], v_w_down[0], v_w_branch_a[0], v_w_branch_b[0], v_w_out[0], v_w_rg_a[0], v_w_rg_x[0], v_w_pool[0])))
    whole = dict(zip(names, _gather_weights([halves(mine[n]) for n in names], "gather_weights")))
    n_in, n_up = w_in.shape[2], w_up.shape[2]
    wts = {
        "w_in": whole["w_in"].reshape(N_CHIPS, d, n_in), "w_up": whole["w_up"].reshape(N_CHIPS, d, n_up),
        "w_down": whole["w_down"].reshape(N_CHIPS, D_FF // N_CHIPS, d),
        "w_a": whole["w_a"].reshape(1, d, d), "w_b": whole["w_b"].reshape(1, d, d), "w_out": whole["w_out"].reshape(1, d, d),
        "rg_a": _pieces_to_blocks(whole["rg_a"]), "rg_x": _pieces_to_blocks(whole["rg_x"]), "pool": _pieces_to_blocks(whole["pool"]),
    }

    grad_x, (sums_mix, sums_mlp, sums_seq, sums_head), grads = _local_step(
        x[0], loss_target[0], mod, (norm_mix_g, norm_mlp_g, final_g.reshape(1, d)), conv_full, vecs, wts, ts=ts, tsq=tsq)

    dmod = jnp.concatenate([sums_mix[1:2], sums_mix[0:1], sums_mlp[3:4], sums_mlp[1:2], sums_mlp[0:1], sums_head[1:2]], axis=1)
    row = jnp.concatenate([sums_mix[2:3], sums_mlp[2:3], sums_seq[4:5], sums_seq[5:6], sums_seq[6:7], sums_seq[7:8], sums_seq[8:9],
                           sums_seq[9:10], sums_head[0:1], sums_seq[0:CONV_WIDTH].reshape(1, CONV_WIDTH * d), dmod, sums_head[2:3]], axis=1)
    n_vec, n_conv, n_mod = 9 * d, CONV_WIDTH * d, N_MOD * d
    rows_all = _allgather8(row, "gather_sums")
    total = _sum_rows(rows_all, d)
    loss = total[0, n_vec + n_conv + n_mod]
    dmod_all = rows_all[:, 0, n_vec + n_conv:n_vec + n_conv + n_mod]
    g_ada, d_ada, m_ada, v_ada = _ada_backward(act_all.T, lax.dynamic_slice_in_dim(dmod_all, k_me * ada_cols, ada_cols, axis=1),
                                               w_ada[0], m_w_ada[0], v_w_ada[0])
    g_conv = lax.dynamic_slice_in_dim(total[:, n_vec:n_vec + n_conv].reshape(CONV_WIDTH, d), k_me * conv_cols, conv_cols, axis=1)
    g_row = jnp.concatenate([total[:, 0:n_vec], g_conv.reshape(1, CONV_WIDTH * conv_cols), total[:, n_vec + n_conv:n_vec + n_conv + n_mod]], axis=1)

    def pack(nm, nl, fg, cw, ba):
        return jnp.concatenate([nm, nl, cw[1], cw[2], cw[3], cw[4], cw[5], cw[6], fg.reshape(1, d),
                                cw[0][0].reshape(1, CONV_WIDTH * conv_cols), ba], axis=1)

    w_row = pack(norm_mix_g, norm_mlp_g, final_g, (conv_w, conv_b, b_rg_a, b_rg_x, a_param, b_pool, pool_scale), b_ada)
    m_row = pack(m_norm_mix_g, m_norm_mlp_g, m_final_g, (m_conv_w, m_conv_b, m_b_rg_a, m_b_rg_x, m_a_param, m_b_pool, m_pool_scale), m_b_ada)
    v_row = pack(v_norm_mix_g, v_norm_mlp_g, v_final_g, (v_conv_w, v_conv_b, v_b_rg_a, v_b_rg_x, v_a_param, v_b_pool, v_pool_scale), v_b_ada)
    vec_out = (g_row,) + tuple(_adamw_row(w_row, g_row, m_row, v_row))

    def unpack(r):
        out = {nme: r[:, i * d:(i + 1) * d] for i, nme in enumerate(
            ("norm_mix_g", "norm_mlp_g", "conv_b", "b_rg_a", "b_rg_x", "a_param", "b_pool", "pool_scale"))}
        out["final_g"] = r[0, 8 * d:9 * d]
        out["conv_w"] = r[:, 9 * d:9 * d + CONV_WIDTH * conv_cols].reshape(1, CONV_WIDTH, conv_cols)
        out["b_ada"] = r[:, 9 * d + CONV_WIDTH * conv_cols:]
        return out

    def pieces(n):
        g = grads[n]
        if n in ("rg_a", "rg_x", "pool"):
            return _blocks_to_pieces(g).astype(BF16)
        return g.reshape((N_CHIPS, 2, mine[n].shape[0] // 2) + mine[n].shape[1:])

    parts = _scatter_partials([pieces(n) for n in names], "scatter_partials")
    reduced = [_sum_partials(p.reshape(N_DEV, -1, p.shape[-1]), "sum_" + n) for n, p in zip(names, parts)]
    quarters = _pair_exchange(reduced, "pair_exchange")
    big_out = {}
    for n, q in zip(names, quarters):
        shape2 = (-1, mine[n].shape[-1])
        g2 = q.reshape(shape2)
        res = (g2,) + tuple(_adamw(mine[n].reshape(shape2), g2, moments_m[n].reshape(shape2), moments_v[n].reshape(shape2), "adamw_" + n))
        big_out[n] = [r.reshape((1,) + mine[n].shape) for r in res]

    key = {"w_in": "w_in", "w_rg_a": "rg_a", "w_rg_x": "rg_x", "w_branch_a": "w_a", "w_pool": "pool", "w_branch_b": "w_b", "w_out": "w_out",
           "w_up": "w_up", "w_down": "w_down"}
    order = ("norm_mix_g", "norm_mlp_g", "w_ada", "b_ada", "w_in", "conv_w", "conv_b", "w_rg_a", "b_rg_a", "w_rg_x", "b_rg_x", "a_param",
             "w_branch_a", "w_pool", "b_pool", "pool_scale", "w_branch_b", "w_out", "w_up", "w_down", "final_g")
    ada_out = [g_ada[None], d_ada[None], m_ada[None], v_ada[None]]
    outs = [loss, grad_x[None]]
    for kind in range(4):
        vec = unpack(vec_out[kind])
        for nme in order:
            outs.append(ada_out[kind] if nme == "w_ada" else big_out[key[nme]][kind] if nme in key else vec[nme])
    return tuple(outs)
```

```python
import functools

import jax
import jax.numpy as jnp
from jax import lax
from jax.experimental import pallas as pl
from jax.experimental.pallas import tpu as pltpu

F32, BF16 = jnp.float32, jnp.bfloat16
D_MODEL = 1024
D_FF = 4 * D_MODEL
N_BLOCKS = 4
BLOCK = D_MODEL // N_BLOCKS
CONV_WIDTH = 4
POOL_WINDOWS = (2, 4, 8, 16)
CONV_HALO = 8
POOL_HALO = 16
N_MOD = 6
EPS = 1e-6
C_RG = 8.0
ADAM_LR, ADAM_B1, ADAM_B2, ADAM_EPS, ADAM_WD, ADAM_STEP = 0.001, 0.9, 0.999, 1e-08, 0.01, 10
N_DEV = 8
N_CHIPS = 4
LANES = 128
VMEM_LIMIT_V7X = 56 * 2**20
MESH = pl.DeviceIdType.MESH
SDS = jax.ShapeDtypeStruct
HBM_SPEC = pl.BlockSpec(memory_space=pltpu.HBM)
VMEM_SPEC = pl.BlockSpec(memory_space=pltpu.VMEM)


def _params(*semantics):
    return pltpu.CompilerParams(dimension_semantics=semantics, vmem_limit_bytes=VMEM_LIMIT_V7X)


def _coords():
    return lax.axis_index("x"), lax.axis_index("y"), lax.axis_index("c")


def _flip(v, bit):
    return 1 - v if bit else v


def _allgather8(v, name):
    def body(v_ref, out_ref, send_sems, recv_sems, local_sem):
        x, y, c = _coords()
        me = 4 * x + 2 * y + c
        mine = pltpu.make_async_copy(v_ref, out_ref.at[me], local_sem)
        mine.start()
        sends = []
        for r in range(1, N_DEV):
            peer = (_flip(x, r & 4), _flip(y, r & 2), _flip(c, r & 1))
            cp = pltpu.make_async_remote_copy(src_ref=v_ref, dst_ref=out_ref.at[me], send_sem=send_sems.at[r - 1],
                                              recv_sem=recv_sems.at[r - 1], device_id=peer, device_id_type=MESH)
            cp.start()
            sends.append(cp)
        for r in range(1, N_DEV):
            pltpu.make_async_remote_copy(src_ref=v_ref, dst_ref=out_ref.at[me ^ r], send_sem=send_sems.at[r - 1],
                                         recv_sem=recv_sems.at[r - 1], device_id=(x, y, c), device_id_type=MESH).wait_recv()
        for cp in sends:
            cp.wait_send()
        mine.wait()

    return pl.pallas_call(
        body, name=name, out_shape=SDS((N_DEV,) + v.shape, v.dtype), in_specs=[VMEM_SPEC], out_specs=VMEM_SPEC,
        scratch_shapes=[pltpu.SemaphoreType.DMA((N_DEV - 1,)), pltpu.SemaphoreType.DMA((N_DEV - 1,)), pltpu.SemaphoreType.DMA(())],
    )(v)


def _gather_weights(shards, name):
    n = len(shards)

    def body(*refs):
        ins, outs = refs[:n], refs[n:2 * n]
        send_sems, recv_sems, local_sems = refs[2 * n:]
        x, y, c = _coords()
        k_me = 2 * x + y
        chips = [(1 - x, y), (x, 1 - y), (1 - x, 1 - y)]
        sibling = (x, y, 1 - c)

        def copy(i, sem, k, half, to):
            return pltpu.make_async_remote_copy(src_ref=outs[i].at[k, half], dst_ref=outs[i].at[k, half], send_sem=send_sems.at[sem],
                                                recv_sem=recv_sems.at[sem], device_id=to, device_id_type=MESH)

        own, first, passed = [], [], []
        for i in range(n):
            lc = pltpu.make_async_copy(ins[i], outs[i].at[k_me], local_sems.at[i])
            lc.start()
            own.append(lc)
            for j, (px, py) in enumerate(chips):
                cp = pltpu.make_async_remote_copy(src_ref=ins[i].at[c], dst_ref=outs[i].at[k_me, c], send_sem=send_sems.at[6 * i + j],
                                                  recv_sem=recv_sems.at[6 * i + j], device_id=(px, py, c), device_id_type=MESH)
                cp.start()
                first.append(cp)
        for i in range(n):
            for j, (px, py) in enumerate(chips):
                copy(i, 6 * i + j, 2 * px + py, c, (x, y, c)).wait_recv()
                fw = copy(i, 6 * i + 3 + j, 2 * px + py, c, sibling)
                fw.start()
                passed.append(fw)
        for i in range(n):
            for j, (px, py) in enumerate(chips):
                copy(i, 6 * i + 3 + j, 2 * px + py, 1 - c, (x, y, c)).wait_recv()
        for cp in first + passed:
            cp.wait_send()
        for lc in own:
            lc.wait()

    return pl.pallas_call(
        body, name=name, out_shape=[SDS((N_CHIPS,) + s.shape, s.dtype) for s in shards],
        in_specs=[HBM_SPEC] * n, out_specs=[HBM_SPEC] * n,
        scratch_shapes=[pltpu.SemaphoreType.DMA((6 * n,)), pltpu.SemaphoreType.DMA((6 * n,)), pltpu.SemaphoreType.DMA((n,))],
    )(*shards)


def _scatter_partials(grads, name):
    n = len(grads)

    def body(*refs):
        ins, outs = refs[:n], refs[n:2 * n]
        send_sems, recv_sems, local_sems = refs[2 * n:]
        x, y, c = _coords()
        own, sends = [], []
        for i in range(n):
            lc = pltpu.make_async_copy(ins[i].at[2 * x + y, c], outs[i].at[0], local_sems.at[i])
            lc.start()
            own.append(lc)
            for r in range(1, N_DEV):
                tx, ty, tc = _flip(x, r & 4), _flip(y, r & 2), _flip(c, r & 1)
                cp = pltpu.make_async_remote_copy(src_ref=ins[i].at[2 * tx + ty, tc], dst_ref=outs[i].at[r],
                                                  send_sem=send_sems.at[7 * i + r - 1], recv_sem=recv_sems.at[7 * i + r - 1],
                                                  device_id=(tx, ty, tc), device_id_type=MESH)
                cp.start()
                sends.append(cp)
        for i in range(n):
            for r in range(1, N_DEV):
                pltpu.make_async_remote_copy(src_ref=ins[i].at[0, 0], dst_ref=outs[i].at[r], send_sem=send_sems.at[7 * i + r - 1],
                                             recv_sem=recv_sems.at[7 * i + r - 1], device_id=(x, y, c), device_id_type=MESH).wait_recv()
        for cp in sends:
            cp.wait_send()
        for lc in own:
            lc.wait()

    return pl.pallas_call(
        body, name=name, out_shape=[SDS((N_DEV,) + g.shape[2:], g.dtype) for g in grads],
        in_specs=[HBM_SPEC] * n, out_specs=[HBM_SPEC] * n,
        scratch_shapes=[pltpu.SemaphoreType.DMA((7 * n,)), pltpu.SemaphoreType.DMA((7 * n,)), pltpu.SemaphoreType.DMA((n,))],
    )(*grads)


def _pair_exchange(pieces, name):
    n = len(pieces)

    def body(*refs):
        ins, outs = refs[:n], refs[n:2 * n]
        send_sems, recv_sems, local_sems = refs[2 * n:]
        x, y, c = _coords()
        own, sends = [], []
        for i in range(n):
            lc = pltpu.make_async_copy(ins[i], outs[i].at[c], local_sems.at[i])
            lc.start()
            own.append(lc)
            cp = pltpu.make_async_remote_copy(src_ref=ins[i], dst_ref=outs[i].at[c], send_sem=send_sems.at[i], recv_sem=recv_sems.at[i],
                                              device_id=(x, y, 1 - c), device_id_type=MESH)
            cp.start()
            sends.append(cp)
        for i in range(n):
            pltpu.make_async_remote_copy(src_ref=ins[i], dst_ref=outs[i].at[1 - c], send_sem=send_sems.at[i], recv_sem=recv_sems.at[i],
                                         device_id=(x, y, c), device_id_type=MESH).wait_recv()
        for cp in sends:
            cp.wait_send()
        for lc in own:
            lc.wait()

    return pl.pallas_call(
        body, name=name, out_shape=[SDS((2,) + p.shape, p.dtype) for p in pieces],
        in_specs=[HBM_SPEC] * n, out_specs=[HBM_SPEC] * n,
        scratch_shapes=[pltpu.SemaphoreType.DMA((n,)), pltpu.SemaphoreType.DMA((n,)), pltpu.SemaphoreType.DMA((n,))],
    )(*pieces)


class _Stage:
    def __init__(self):
        self.inputs, self.in_specs, self.outputs, self.out_specs, self.aliases = [], [], [], [], {}
        self.parts, self.n_copies = [], 0

    def add(self, inputs, in_spec, outputs, out_spec, n_copies, build, alias=False):
        i0, o0 = len(self.inputs), len(self.outputs)
        self.inputs += inputs
        self.in_specs += [in_spec] * len(inputs)
        self.outputs += outputs
        self.out_specs += [out_spec] * len(outputs)
        if alias:
            self.aliases.update({i0 + i: o0 + i for i in range(len(inputs))})
        self.parts.append((build, i0, len(inputs), o0, len(outputs)))
        self.n_copies += n_copies
        return list(range(o0, o0 + len(outputs)))

    def copies(self, in_refs, out_refs):
        out = []
        for build, i0, ni, o0, no in self.parts:
            out += build(in_refs[i0:i0 + ni], out_refs[o0:o0 + no])
        assert len(out) == self.n_copies
        return out

    def run(self, in_refs, out_refs, send_sems, recv_sems, start):
        x, y, c = _coords()
        for s, (src, dst, to, landing) in enumerate(self.copies(in_refs, out_refs)):
            if to is None:
                cp = pltpu.make_async_copy(src, dst, send_sems.at[s])
                cp.start() if start else cp.wait()
                continue
            cp = pltpu.make_async_remote_copy(src_ref=src, dst_ref=dst, send_sem=send_sems.at[s], recv_sem=recv_sems.at[s],
                                              device_id=to, device_id_type=MESH)
            if start:
                cp.start()
            else:
                pltpu.make_async_remote_copy(src_ref=landing, dst_ref=landing, send_sem=send_sems.at[s], recv_sem=recv_sems.at[s],
                                             device_id=(x, y, c), device_id_type=MESH).wait_recv()
                cp.wait_send()


def _hosted(stage, body, *, name, in_specs, out_specs, out_shape, grid=(), scratch_shapes=(), compiler_params=None):
    if stage is None:
        return pl.pallas_call(body, name=name, grid=grid, in_specs=in_specs, out_specs=out_specs, out_shape=out_shape,
                              scratch_shapes=list(scratch_shapes), compiler_params=compiler_params)
    single = not isinstance(out_shape, (list, tuple))
    h_out_shape = [out_shape] if single else list(out_shape)
    h_out_specs = [out_specs] if single else list(out_specs)
    n_in, n_out, n_scr = len(in_specs), len(h_out_shape), len(scratch_shapes)
    s_in, s_out = len(stage.inputs), len(stage.outputs)

    def wrapped(*refs):
        h_in, st_in = refs[:n_in], refs[n_in:n_in + s_in]
        h_o, st_o = refs[n_in + s_in:n_in + s_in + n_out], refs[n_in + s_in + n_out:n_in + s_in + n_out + s_out]
        h_scr = refs[n_in + s_in + n_out + s_out:n_in + s_in + n_out + s_out + n_scr]
        send_sems, recv_sems = refs[n_in + s_in + n_out + s_out + n_scr:]
        if not grid:
            stage.run(st_in, st_o, send_sems, recv_sems, True)
            if body is not None:
                body(*h_in, *h_o, *h_scr)
            stage.run(st_in, st_o, send_sems, recv_sems, False)
            return
        ids = [pl.program_id(a) for a in range(len(grid))]
        first = functools.reduce(jnp.logical_and, [i == 0 for i in ids])
        last = functools.reduce(jnp.logical_and, [i == g - 1 for i, g in zip(ids, grid)])
        pl.when(first)(lambda: stage.run(st_in, st_o, send_sems, recv_sems, True))
        body(*h_in, *h_o, *h_scr)
        pl.when(last)(lambda: stage.run(st_in, st_o, send_sems, recv_sems, False))

    call = pl.pallas_call(
        wrapped, name=name, grid=grid, in_specs=list(in_specs) + stage.in_specs, out_specs=h_out_specs + stage.out_specs,
        out_shape=h_out_shape + stage.outputs, input_output_aliases={n_in + i: n_out + o for i, o in stage.aliases.items()},
        scratch_shapes=list(scratch_shapes) + [pltpu.SemaphoreType.DMA((stage.n_copies,)), pltpu.SemaphoreType.DMA((stage.n_copies,))],
        compiler_params=pltpu.CompilerParams(dimension_semantics=("arbitrary",) * len(grid), vmem_limit_bytes=VMEM_LIMIT_V7X),
    )

    def run(*operands):
        outs = call(*operands, *stage.inputs)
        host = outs[:n_out]
        return (host[0] if single else host), outs[n_out:]

    return run


def _run_stage(stage, name):
    return _hosted(stage, None, name=name, in_specs=[], out_specs=[], out_shape=[])()[1]


def _chips():
    x, y, _ = _coords()
    return [(1 - x, y), (x, 1 - y), (1 - x, 1 - y)]


def _add_allgather8(stage, v):
    def build(ins, outs):
        x, y, c = _coords()
        me = 4 * x + 2 * y + c
        copies = [(ins[0], outs[0].at[me], None, None)]
        for r in range(1, N_DEV):
            peer = (_flip(x, r & 4), _flip(y, r & 2), _flip(c, r & 1))
            copies.append((ins[0], outs[0].at[me], peer, outs[0].at[me ^ r]))
        return copies

    return stage.add([v], VMEM_SPEC, [SDS((N_DEV,) + v.shape, v.dtype)], VMEM_SPEC, N_DEV, build)[0]


def _add_gather_ici(stage, shard):
    def build(ins, outs):
        x, y, c = _coords()
        k_me = 2 * x + y
        copies = [(ins[0], outs[0].at[k_me], None, None)]
        for px, py in _chips():
            copies.append((ins[0].at[c], outs[0].at[k_me, c], (px, py, c), outs[0].at[2 * px + py, c]))
        return copies

    return stage.add([shard], HBM_SPEC, [SDS((N_CHIPS,) + shard.shape, shard.dtype)], HBM_SPEC, N_CHIPS, build)[0]


def _add_gather_d2d(stage, whole):
    def build(ins, outs):
        x, y, c = _coords()
        return [(outs[0].at[2 * px + py, c], outs[0].at[2 * px + py, c], (x, y, 1 - c), outs[0].at[2 * px + py, 1 - c]) for px, py in _chips()]

    return stage.add([whole], HBM_SPEC, [SDS(whole.shape, whole.dtype)], HBM_SPEC, N_CHIPS - 1, build, alias=True)[0]


def _add_reduce_d2d(stage, grads):
    def build(ins, outs):
        x, y, c = _coords()
        return [(ins[0].at[k, 1 - c], outs[0].at[k], (x, y, 1 - c), outs[0].at[k]) for k in range(N_CHIPS)]

    return stage.add([grads], HBM_SPEC, [SDS((N_CHIPS,) + grads.shape[2:], grads.dtype)], HBM_SPEC, N_CHIPS, build)[0]


def _add_reduce_ici(stage, partial):
    def build(ins, outs):
        x, y, c = _coords()
        return [(ins[0].at[2 * px + py], outs[0].at[j], (px, py, c), outs[0].at[j]) for j, (px, py) in enumerate(_chips())]

    return stage.add([partial], HBM_SPEC, [SDS((N_CHIPS - 1,) + partial.shape[1:], partial.dtype)], HBM_SPEC, N_CHIPS - 1, build)[0]


def _add_swap_halves(stage, quarter):
    def build(ins, outs):
        x, y, c = _coords()
        return [(outs[0].at[c], outs[0].at[c], (x, y, 1 - c), outs[0].at[1 - c])]

    return stage.add([quarter], HBM_SPEC, [SDS(quarter.shape, quarter.dtype)], HBM_SPEC, 1, build, alias=True)[0]


def _sum_pair(grads, from_sibling, place, name):
    _, _, rows, cols = grads.shape
    tr = min(rows, 256)

    def body(place_ref, g_ref, s_ref, o_ref):
        o_ref[0] = (g_ref[0, 0].astype(F32) + s_ref[0].astype(F32)).astype(BF16)

    spec = pl.BlockSpec((1, tr, cols), lambda k, i, place_ref: (k, i, 0))
    return pl.pallas_call(
        body, name=name, out_shape=SDS((N_CHIPS, rows, cols), BF16),
        grid_spec=pltpu.PrefetchScalarGridSpec(
            num_scalar_prefetch=1, grid=(N_CHIPS, rows // tr),
            in_specs=[pl.BlockSpec((1, 1, tr, cols), lambda k, i, place_ref: (k, place_ref[0], i, 0)), spec], out_specs=spec),
        compiler_params=_params("parallel", "parallel"),
    )(place, grads, from_sibling)


def _sum_quarter(grads, from_sibling, from_chips, place, name):
    _, _, rows, cols = grads.shape
    tr = min(rows, 256)

    def body(place_ref, g_ref, s_ref, c_ref, o_ref):
        acc = g_ref[0, 0].astype(F32) + s_ref[0].astype(F32)
        for j in range(N_CHIPS - 1):
            acc = acc + c_ref[j].astype(F32)
        o_ref[0] = acc

    return pl.pallas_call(
        body, name=name, out_shape=SDS((2, rows, cols), F32),
        grid_spec=pltpu.PrefetchScalarGridSpec(
            num_scalar_prefetch=1, grid=(rows // tr,),
            in_specs=[pl.BlockSpec((1, 1, tr, cols), lambda i, place_ref: (place_ref[1], place_ref[0], i, 0)),
                      pl.BlockSpec((1, tr, cols), lambda i, place_ref: (place_ref[1], i, 0)),
                      pl.BlockSpec((N_CHIPS - 1, tr, cols), lambda i, place_ref: (0, i, 0))],
            out_specs=pl.BlockSpec((1, tr, cols), lambda i, place_ref: (place_ref[0], i, 0))),
        compiler_params=_params("parallel"),
    )(place, grads, from_sibling, from_chips)


def _matmul(a, w, *, transposed, reduce_blocks, out_dtype, name, tm):
    m_rows = a.shape[0]
    nb, r, c = w.shape
    kb, nblk = (c, r) if transposed else (r, c)
    dims = (((1,), (1,)), ((), ())) if transposed else (((1,), (0,)), ((), ()))
    assert a.shape[1] == (nb * kb if reduce_blocks else kb) and m_rows % tm == 0
    n_out = nblk if reduce_blocks else nb * nblk

    def body(a_ref, w_ref, o_ref):
        if reduce_blocks:
            acc = lax.dot_general(a_ref[:, 0:kb], w_ref[0], dims, preferred_element_type=F32)
            for k in range(1, nb):
                acc += lax.dot_general(a_ref[:, k * kb:(k + 1) * kb], w_ref[k], dims, preferred_element_type=F32)
            o_ref[...] = acc.astype(o_ref.dtype)
        else:
            a_blk = a_ref[...]
            for k in range(nb):
                o_ref[:, k * nblk:(k + 1) * nblk] = lax.dot_general(a_blk, w_ref[k], dims, preferred_element_type=F32).astype(o_ref.dtype)

    return pl.pallas_call(
        body, name=name, grid=(m_rows // tm,), out_shape=SDS((m_rows, n_out), out_dtype),
        in_specs=[pl.BlockSpec((tm, a.shape[1]), lambda m: (m, 0)), pl.BlockSpec((nb, r, c), lambda m: (0, 0, 0))],
        out_specs=pl.BlockSpec((tm, n_out), lambda m: (m, 0)), compiler_params=_params("parallel"),
    )(a, w)


def _weight_grad(a, b, *, nb, name, ts, tk):
    s_rows, k1 = a.shape
    tn = b.shape[1] // nb
    n_s = s_rows // ts
    assert s_rows % ts == 0 and k1 % tk == 0 and b.shape[1] % nb == 0

    def body(a_ref, b_ref, o_ref, acc):
        s = pl.program_id(2)
        part = lax.dot_general(a_ref[...], b_ref[...], (((0,), (0,)), ((), ())), preferred_element_type=F32)

        @pl.when(s == 0)
        def _():
            acc[...] = part

        @pl.when(s > 0)
        def _():
            acc[...] += part

        @pl.when(s == n_s - 1)
        def _():
            o_ref[0] = acc[...].astype(o_ref.dtype)

    return pl.pallas_call(
        body, name=name, grid=(nb, k1 // tk, n_s), out_shape=SDS((nb, k1, tn), BF16),
        in_specs=[pl.BlockSpec((ts, tk), lambda n, i, s: (s, i)), pl.BlockSpec((ts, tn), lambda n, i, s: (s, n))],
        out_specs=pl.BlockSpec((1, tk, tn), lambda n, i, s: (n, i, 0)), scratch_shapes=[pltpu.VMEM((tk, tn), F32)],
        compiler_params=_params("parallel", "parallel", "arbitrary"),
    )(a, b)


def _row_spec(ts, cols, col_block=0):
    return pl.BlockSpec((ts, cols), lambda t: (t, col_block))


def _vec_spec(rows, cols):
    return pl.BlockSpec((rows, cols), lambda t: (0, 0))


def _rms(x):
    r = lax.rsqrt(jnp.mean(x * x, axis=-1, keepdims=True) + EPS)
    return x * r, r


def _accumulate(acc_ref, row, value):
    acc_ref[row:row + 1, :] += jnp.sum(value, axis=0, keepdims=True)


def _gelu_parts(y):
    k0, k1 = 0.7978845608028654, 0.044715
    th = jnp.tanh(k0 * (y + k1 * (y * y * y)))
    gelu = 0.5 * y * (1.0 + th)
    dgelu = 0.5 * (1.0 + th) + 0.5 * y * (1.0 - th * th) * (k0 * (1.0 + 3.0 * k1 * (y * y)))
    return gelu, dgelu


def _neg_expm1(z):
    series = z * (1.0 + z * (0.5 + z * (1.0 / 6 + z * (1.0 / 24 + z * (1.0 / 120 + z * (1.0 / 720))))))
    return -jnp.where(z > -0.25, series, jnp.exp(z) - 1.0)


def _norm_mod(x, mod, gain, *, sc_row, sh_row, name, ts):
    s_rows, d = x.shape

    def body(x_ref, mod_ref, g_ref, h_ref):
        xh, _ = _rms(x_ref[...])
        h = (xh * g_ref[...]) * (1.0 + mod_ref[sc_row:sc_row + 1, :]) + mod_ref[sh_row:sh_row + 1, :]
        h_ref[...] = h.astype(BF16)

    return pl.pallas_call(
        body, name=name, grid=(s_rows // ts,), out_shape=SDS((s_rows, d), BF16),
        in_specs=[_row_spec(ts, d), _vec_spec(8, d), _vec_spec(1, d)], out_specs=_row_spec(ts, d),
        compiler_params=_params("parallel"),
    )(x, mod, gain)


def _seq_forward(proj, conv_w, vecs, w_a, w_x, w_p, *, ts):
    s_rows = proj.shape[0]
    d = D_MODEL

    def body(x_ref, xh_ref, y_ref, u_ref, uh_ref, cw_ref, vec_ref, wa_ref, wx_ref, wp_ref,
             xr_o, hr_o, ga_o, p_o, pooled_o, carry, a_scr, b_scr):
        t = pl.program_id(0)

        @pl.when(t == 0)
        def _():
            carry[...] = jnp.zeros_like(carry)

        x = x_ref[...]
        halo = jnp.where(t == 0, 0.0, xh_ref[...])
        xx = jnp.concatenate([halo, x], axis=0)
        xr = vec_ref[0:1, :] + x * cw_ref[CONV_WIDTH - 1:CONV_WIDTH, :]
        for j in range(CONV_WIDTH - 1):
            xr = xr + pltpu.roll(xx, CONV_WIDTH - 1 - j, 0)[CONV_HALO:] * cw_ref[j:j + 1, :]
        xr_o[...] = xr
        xrb = xr.astype(BF16)
        zr = jnp.concatenate([jnp.dot(xrb[:, g * BLOCK:(g + 1) * BLOCK], wa_ref[g], preferred_element_type=F32)
                              for g in range(N_BLOCKS)], axis=1) + vec_ref[1:2, :]
        zi = jnp.concatenate([jnp.dot(xrb[:, g * BLOCK:(g + 1) * BLOCK], wx_ref[g], preferred_element_type=F32)
                              for g in range(N_BLOCKS)], axis=1) + vec_ref[2:3, :]
        r = jax.nn.sigmoid(zr)
        gate_i = jax.nn.sigmoid(zi)
        log_a = (-C_RG * r) * jax.nn.softplus(vec_ref[3:4, :])
        rows = lax.broadcasted_iota(jnp.int32, (ts, d), 0)
        mult = jnp.where((rows == 0) & (t == 0), 1.0, jnp.sqrt(_neg_expm1(2.0 * log_a)))
        a_scr[...] = jnp.exp(log_a)
        b_scr[...] = xr * gate_i * mult

        sub = lax.broadcasted_iota(jnp.int32, (8, d), 0)

        def chunk(i, h):
            a = a_scr[pl.ds(i * 8, 8), :]
            b = b_scr[pl.ds(i * 8, 8), :]
            for s in (1, 2, 4):
                keep = sub >= s
                b = jnp.where(keep, a * pltpu.roll(b, s, 0) + b, b)
                a = jnp.where(keep, a * pltpu.roll(a, s, 0), a)
            hh = a * h + b
            hr_o[pl.ds(i * 8, 8), :] = hh
            return hh[7:8, :]

        carry[...] = lax.fori_loop(0, ts // 8, chunk, carry[...])
        gelu, _ = _gelu_parts(y_ref[...])
        ga_o[...] = (gelu * hr_o[...]).astype(BF16)

        u = u_ref[...]
        uu = jnp.concatenate([jnp.where(t == 0, 0.0, uh_ref[...]), u], axis=0)
        pos = (rows + t * ts + 1).astype(F32)
        sums, have, parts = uu, 1, []
        for g, win in enumerate(POOL_WINDOWS):
            while have < win:
                sums = sums + pltpu.roll(sums, have, 0)
                have *= 2
            mean = sums[POOL_HALO:, g * BLOCK:(g + 1) * BLOCK] / jnp.minimum(pos[:, :BLOCK], float(win))
            parts.append(mean - u[:, g * BLOCK:(g + 1) * BLOCK])
        pb = jnp.concatenate(parts, axis=1).astype(BF16)
        p_o[...] = pb
        mixed = jnp.concatenate([jnp.dot(pb[:, g * BLOCK:(g + 1) * BLOCK], wp_ref[g], preferred_element_type=F32)
                                 for g in range(N_BLOCKS)], axis=1) + vec_ref[4:5, :]
        pooled_o[...] = (mixed * vec_ref[5:6, :]).astype(BF16)

    halo_x = pl.BlockSpec((CONV_HALO, d), lambda t: (jnp.maximum(t * (ts // CONV_HALO) - 1, 0), 0))
    halo_u = pl.BlockSpec((POOL_HALO, d), lambda t: (jnp.maximum(t * (ts // POOL_HALO) - 1, 0), 2))
    wspec = pl.BlockSpec((N_BLOCKS, BLOCK, BLOCK), lambda t: (0, 0, 0))
    return pl.pallas_call(
        body, name="seq_forward", grid=(s_rows // ts,),
        out_shape=[SDS((s_rows, d), F32), SDS((s_rows, d), F32), SDS((s_rows, d), BF16), SDS((s_rows, d), BF16), SDS((s_rows, d), BF16)],
        in_specs=[_row_spec(ts, d, 0), halo_x, _row_spec(ts, d, 1), _row_spec(ts, d, 2), halo_u, _vec_spec(CONV_WIDTH, d), _vec_spec(8, d),
                  wspec, wspec, wspec],
        out_specs=[_row_spec(ts, d)] * 5,
        scratch_shapes=[pltpu.VMEM((1, d), F32), pltpu.VMEM((ts, d), F32), pltpu.VMEM((ts, d), F32)],
        compiler_params=_params("arbitrary"),
    )(proj, proj, proj, proj, proj, conv_w, vecs, w_a, w_x, w_p)


def _merge_forward(proj, br_a, br_b, *, ts):
    s_rows, d = br_a.shape

    def body(ga_ref, gb_ref, a_ref, b_ref, o_ref):
        o_ref[...] = (jax.nn.sigmoid(ga_ref[...]) * a_ref[...] + jax.nn.sigmoid(gb_ref[...]) * b_ref[...]).astype(BF16)

    return pl.pallas_call(
        body, name="merge_forward", grid=(s_rows // ts,), out_shape=SDS((s_rows, d), BF16),
        in_specs=[_row_spec(ts, d, 3), _row_spec(ts, d, 4), _row_spec(ts, d), _row_spec(ts, d)], out_specs=_row_spec(ts, d),
        compiler_params=_params("parallel"),
    )(proj, proj, br_a, br_b)


def _residual_norm(x, mo, mod, gain, *, ts):
    s_rows, d = x.shape

    def body(x_ref, mo_ref, mod_ref, g_ref, x2_ref, h_ref):
        x2 = x_ref[...] + mod_ref[2:3, :] * mo_ref[...]
        x2_ref[...] = x2
        xh, _ = _rms(x2)
        h_ref[...] = ((xh * g_ref[...]) * (1.0 + mod_ref[4:5, :]) + mod_ref[3:4, :]).astype(BF16)

    return pl.pallas_call(
        body, name="residual_norm", grid=(s_rows // ts,), out_shape=[SDS((s_rows, d), F32), SDS((s_rows, d), BF16)],
        in_specs=[_row_spec(ts, d), _row_spec(ts, d), _vec_spec(8, d), _vec_spec(1, d)], out_specs=[_row_spec(ts, d)] * 2,
        compiler_params=_params("parallel"),
    )(x, mo, mod, gain)


def _relu_squared(up, *, ts):
    s_rows, f = up.shape

    def body(u_ref, o_ref):
        o_ref[...] = jnp.square(jnp.maximum(u_ref[...], 0.0)).astype(BF16)

    return pl.pallas_call(
        body, name="relu_squared", grid=(s_rows // ts,), out_shape=SDS((s_rows, f), BF16),
        in_specs=[_row_spec(ts, f)], out_specs=_row_spec(ts, f), compiler_params=_params("parallel"),
    )(up)


def _loss_head(x2, ffo, mod, final_g, target, *, ts):
    s_rows, d = x2.shape

    def body(x2_ref, f_ref, mod_ref, g_ref, t_ref, dx_ref, df_ref, acc_ref):
        @pl.when(pl.program_id(0) == 0)
        def _():
            acc_ref[...] = jnp.zeros_like(acc_ref)

        ffo_t = f_ref[...]
        x3 = x2_ref[...] + mod_ref[5:6, :] * ffo_t
        xh, r = _rms(x3)
        err = xh * g_ref[...] - t_ref[...]
        dy = err / d
        dxh = dy * g_ref[...]
        dx3 = r * (dxh - xh * jnp.mean(dxh * xh, axis=-1, keepdims=True))
        dx_ref[...] = dx3
        df_ref[...] = (dx3 * mod_ref[5:6, :]).astype(BF16)
        _accumulate(acc_ref, 0, dy * xh)
        _accumulate(acc_ref, 1, dx3 * ffo_t)
        _accumulate(acc_ref, 2, jnp.square(err) * (0.5 / d))

    return pl.pallas_call(
        body, name="loss_head", grid=(s_rows // ts,), out_shape=[SDS((s_rows, d), F32), SDS((s_rows, d), BF16), SDS((8, d), F32)],
        in_specs=[_row_spec(ts, d), _row_spec(ts, d), _vec_spec(8, d), _vec_spec(1, d), _row_spec(ts, d)],
        out_specs=[_row_spec(ts, d), _row_spec(ts, d), _vec_spec(8, d)], compiler_params=_params("arbitrary"),
    )(x2, ffo, mod, final_g, target)


def _relu_squared_backward(dff, up, *, ts):
    s_rows, f = up.shape

    def body(d_ref, u_ref, o_ref):
        o_ref[...] = (d_ref[...] * (2.0 * jnp.maximum(u_ref[...], 0.0))).astype(BF16)

    return pl.pallas_call(
        body, name="relu_squared_backward", grid=(s_rows // ts,), out_shape=SDS((s_rows, f), BF16),
        in_specs=[_row_spec(ts, f), _row_spec(ts, f)], out_specs=_row_spec(ts, f), compiler_params=_params("parallel"),
    )(dff, up)


def _norm_mod_backward(dh, x, dres, mod, gain, mo, *, sc_row, gate_row, name, ts):
    s_rows, d = x.shape
    gated = mo is not None

    def body(*refs):
        if gated:
            dh_ref, x_ref, dres_ref, mod_ref, g_ref, mo_ref, dx_ref, dmo_ref, acc_ref = refs
        else:
            dh_ref, x_ref, dres_ref, mod_ref, g_ref, dx_ref, acc_ref = refs

        @pl.when(pl.program_id(0) == 0)
        def _():
            acc_ref[...] = jnp.zeros_like(acc_ref)

        dh_t = dh_ref[...]
        xh, r = _rms(x_ref[...])
        _accumulate(acc_ref, 0, dh_t * (xh * g_ref[...]))
        _accumulate(acc_ref, 1, dh_t)
        dn = dh_t * (1.0 + mod_ref[sc_row:sc_row + 1, :])
        _accumulate(acc_ref, 2, dn * xh)
        dxh = dn * g_ref[...]
        dx = dres_ref[...] + r * (dxh - xh * jnp.mean(dxh * xh, axis=-1, keepdims=True))
        dx_ref[...] = dx
        if gated:
            dmo_ref[...] = (dx * mod_ref[gate_row:gate_row + 1, :]).astype(BF16)
            _accumulate(acc_ref, 3, dx * mo_ref[...])

    rows = _row_spec(ts, d)
    return pl.pallas_call(
        body, name=name, grid=(s_rows // ts,),
        out_shape=[SDS((s_rows, d), F32)] + ([SDS((s_rows, d), BF16)] if gated else []) + [SDS((8, d), F32)],
        in_specs=[rows, rows, rows, _vec_spec(8, d), _vec_spec(1, d)] + ([rows] if gated else []),
        out_specs=[rows] + ([rows] if gated else []) + [_vec_spec(8, d)], compiler_params=_params("arbitrary"),
    )(*([dh, x, dres, mod, gain] + ([mo] if gated else [])))


def _merge_backward(dmerged, proj, br_a, br_b, *, ts):
    s_rows, d = br_a.shape

    def body(dm_ref, ga_ref, gb_ref, a_ref, b_ref, da_ref, db_ref, dg_ref):
        dm = dm_ref[...]
        sa, sb = jax.nn.sigmoid(ga_ref[...]), jax.nn.sigmoid(gb_ref[...])
        da_ref[...] = (dm * sa).astype(BF16)
        db_ref[...] = (dm * sb).astype(BF16)
        dg_ref[:, 0:d] = (dm * a_ref[...] * (sa * (1.0 - sa))).astype(BF16)
        dg_ref[:, d:2 * d] = (dm * b_ref[...] * (sb * (1.0 - sb))).astype(BF16)

    rows = _row_spec(ts, d)
    return pl.pallas_call(
        body, name="merge_backward", grid=(s_rows // ts,), out_shape=[SDS((s_rows, d), BF16), SDS((s_rows, d), BF16), SDS((s_rows, 2 * d), BF16)],
        in_specs=[rows, _row_spec(ts, d, 3), _row_spec(ts, d, 4), rows, rows], out_specs=[rows, rows, _row_spec(ts, 2 * d)],
        compiler_params=_params("parallel"),
    )(dmerged, proj, proj, br_a, br_b)


def _seq_backward(proj, xr, hr, p, dga, dpooled, dgab, conv_w, vecs, w_a, w_x, w_p, *, ts):
    s_rows = proj.shape[0]
    d = D_MODEL
    n_t = s_rows // ts

    def block_dots(lhs, w_ref, dims):
        return jnp.concatenate([lax.dot_general(lhs[:, g * BLOCK:(g + 1) * BLOCK], w_ref[g], dims, preferred_element_type=F32)
                                for g in range(N_BLOCKS)], axis=1)

    def add_block_grads(dw_ref, lhs, rhs):
        for g in range(N_BLOCKS):
            dw_ref[g] += lax.dot_general(lhs[:, g * BLOCK:(g + 1) * BLOCK], rhs[:, g * BLOCK:(g + 1) * BLOCK],
                                         (((0,), (0,)), ((), ())), preferred_element_type=F32)

    nn, nt = (((1,), (0,)), ((), ())), (((1,), (1,)), ((), ()))

    def body(x_ref, y_ref, xr_ref, hr_ref, hh_ref, p_ref, dga_ref, dpl_ref, dgab_ref, cw_ref, vec_ref, wa_ref, wx_ref, wp_ref,
             dproj_o, acc_o, dwa_o, dwx_o, dwp_o, g_carry, dxr_carry, q_carry, a_scr, b_scr, g_scr):
        step = pl.program_id(0)
        t = n_t - 1 - step

        @pl.when(step == 0)
        def _():
            for ref in (acc_o, dwa_o, dwx_o, dwp_o, g_carry, dxr_carry, q_carry):
                ref[...] = jnp.zeros_like(ref)

        rows = lax.broadcasted_iota(jnp.int32, (ts, d), 0)
        start = (rows == 0) & (t == 0)
        xr = xr_ref[...]
        xrb = xr.astype(BF16)
        r = jax.nn.sigmoid(block_dots(xrb, wa_ref, nn) + vec_ref[1:2, :])
        gate_i = jax.nn.sigmoid(block_dots(xrb, wx_ref, nn) + vec_ref[2:3, :])
        sp = jax.nn.softplus(vec_ref[3:4, :])
        log_a = (-C_RG * r) * sp
        a = jnp.exp(log_a)
        m_raw = jnp.sqrt(_neg_expm1(2.0 * log_a))
        mult = jnp.where(start, 1.0, m_raw)
        gelu, dgelu = _gelu_parts(y_ref[...])
        dga_t = dga_ref[...]
        hr_t = hr_ref[...]
        dy = dga_t * hr_t * dgelu

        a_scr[...] = jnp.where(rows == ts - 1, 1.0, pltpu.roll(a, ts - 1, 0))
        b_scr[...] = dga_t * gelu
        sub = lax.broadcasted_iota(jnp.int32, (8, d), 0)

        def chunk(i, g_next):
            at = pl.multiple_of((ts // 8 - 1 - i) * 8, 8)
            aa = a_scr[pl.ds(at, 8), :]
            bb = b_scr[pl.ds(at, 8), :]
            for s in (1, 2, 4):
                keep = sub < 8 - s
                bb = jnp.where(keep, bb + aa * pltpu.roll(bb, 8 - s, 0), bb)
                aa = jnp.where(keep, aa * pltpu.roll(aa, 8 - s, 0), aa)
            gg = aa * g_next + bb
            g_scr[pl.ds(at, 8), :] = gg
            return gg[0:1, :]

        g_first = lax.fori_loop(0, ts // 8, chunk, g_carry[...])
        g_carry[...] = a[0:1, :] * g_first
        g = g_scr[...]

        h_before = jnp.where(t == 0, 0.0, hh_ref[CONV_HALO - 1:CONV_HALO, :])
        h_prev = jnp.where(rows == 0, h_before, pltpu.roll(hr_t, 1, 0))
        dxr = g * gate_i * mult
        d_i = g * xr * mult
        d_mult = g * xr * gate_i
        dlog_a = (g * h_prev) * a - jnp.where(start, 0.0, d_mult * (a * a) / m_raw)
        dzr = (dlog_a * (-C_RG * sp)) * (r * (1.0 - r))
        dzi = d_i * (gate_i * (1.0 - gate_i))
        _accumulate(acc_o, 5, dzr)
        _accumulate(acc_o, 6, dzi)
        _accumulate(acc_o, 7, dlog_a * (-C_RG * r) * jax.nn.sigmoid(vec_ref[3:4, :]))
        dzrb, dzib = dzr.astype(BF16), dzi.astype(BF16)
        add_block_grads(dwa_o, xrb, dzrb)
        add_block_grads(dwx_o, xrb, dzib)
        dxr = dxr + block_dots(dzrb, wa_ref, nt) + block_dots(dzib, wx_ref, nt)

        x = x_ref[...]
        ext = jnp.concatenate([dxr, dxr_carry[...]], axis=0)
        dx = dxr * cw_ref[CONV_WIDTH - 1:CONV_WIDTH, :]
        _accumulate(acc_o, CONV_WIDTH - 1, x * dxr)
        for j in range(CONV_WIDTH - 1):
            ahead = pltpu.roll(ext, ts + CONV_HALO - (CONV_WIDTH - 1 - j), 0)[:ts]
            dx = dx + ahead * cw_ref[j:j + 1, :]
            _accumulate(acc_o, j, x * ahead)
        _accumulate(acc_o, 4, dxr)
        dxr_carry[...] = dxr[0:CONV_HALO, :]

        pb = p_ref[...]
        mixed = block_dots(pb, wp_ref, nn) + vec_ref[4:5, :]
        dpl = dpl_ref[...]
        _accumulate(acc_o, 9, dpl * mixed)
        dmixed = dpl * vec_ref[5:6, :]
        _accumulate(acc_o, 8, dmixed)
        dmb = dmixed.astype(BF16)
        add_block_grads(dwp_o, pb, dmb)
        dp = block_dots(dmb, wp_ref, nt)
        pos = (rows + t * ts + 1).astype(F32)[:, :BLOCK]
        q = jnp.concatenate([dp[:, g * BLOCK:(g + 1) * BLOCK] / jnp.minimum(pos, float(win)) for g, win in enumerate(POOL_WINDOWS)], axis=1)
        sums, have, parts = jnp.concatenate([q, q_carry[...]], axis=0), 1, []
        for g, win in enumerate(POOL_WINDOWS):
            while have < win:
                sums = sums + pltpu.roll(sums, ts + POOL_HALO - have, 0)
                have *= 2
            parts.append(sums[:ts, g * BLOCK:(g + 1) * BLOCK])
        du = jnp.concatenate(parts, axis=1) - dp
        q_carry[...] = q[0:POOL_HALO, :]

        dproj_o[:, 0:d] = dx.astype(BF16)
        dproj_o[:, d:2 * d] = dy.astype(BF16)
        dproj_o[:, 2 * d:3 * d] = du.astype(BF16)
        dproj_o[:, 3 * d:5 * d] = dgab_ref[...]

    def rev(cols, col_block=0):
        return pl.BlockSpec((ts, cols), lambda i: (n_t - 1 - i, col_block))

    halo_h = pl.BlockSpec((CONV_HALO, d), lambda i: (jnp.maximum((n_t - 1 - i) * (ts // CONV_HALO) - 1, 0), 0))
    wspec = pl.BlockSpec((N_BLOCKS, BLOCK, BLOCK), lambda i: (0, 0, 0))
    const2 = lambda rows_, cols: pl.BlockSpec((rows_, cols), lambda i: (0, 0))
    return pl.pallas_call(
        body, name="seq_backward", grid=(n_t,),
        out_shape=[SDS((s_rows, 5 * d), BF16), SDS((16, d), F32)] + [SDS((N_BLOCKS, BLOCK, BLOCK), F32)] * 3,
        in_specs=[rev(d, 0), rev(d, 1), rev(d), rev(d), halo_h, rev(d), rev(d), rev(d), rev(2 * d), const2(CONV_WIDTH, d), const2(8, d),
                  wspec, wspec, wspec],
        out_specs=[rev(5 * d), const2(16, d), wspec, wspec, wspec],
        scratch_shapes=[pltpu.VMEM((1, d), F32), pltpu.VMEM((CONV_HALO, d), F32), pltpu.VMEM((POOL_HALO, d), F32),
                        pltpu.VMEM((ts, d), F32), pltpu.VMEM((ts, d), F32), pltpu.VMEM((ts, d), F32)],
        compiler_params=_params("arbitrary"),
    )(proj, proj, xr, hr, hr, p, dga, dpooled, dgab, conv_w, vecs, w_a, w_x, w_p)


def _sum_partials(parts, name):
    _, rows, cols = parts.shape
    tr = min(rows, 256)

    def body(p_ref, o_ref):
        acc = p_ref[0].astype(F32)
        for r in range(1, N_DEV):
            acc = acc + p_ref[r].astype(F32)
        o_ref[...] = acc

    return pl.pallas_call(
        body, name=name, grid=(rows // tr,), out_shape=SDS((rows, cols), F32),
        in_specs=[pl.BlockSpec((N_DEV, tr, cols), lambda i: (0, i, 0))], out_specs=pl.BlockSpec((tr, cols), lambda i: (i, 0)),
        compiler_params=_params("parallel"),
    )(parts)


def _adamw_math(w, g, m, v):
    m = ADAM_B1 * m + (1.0 - ADAM_B1) * g
    v = ADAM_B2 * v + (1.0 - ADAM_B2) * jnp.square(g)
    m_hat = m / (1.0 - ADAM_B1 ** ADAM_STEP)
    v_hat = v / (1.0 - ADAM_B2 ** ADAM_STEP)
    delta = -ADAM_LR * (m_hat / (jnp.sqrt(v_hat) + ADAM_EPS) + ADAM_WD * w)
    return delta, m, v


def _adamw(w, g, m, v, name):
    rows, cols = w.shape
    tr = min(rows, 256)

    def body(w_ref, g_ref, m_ref, v_ref, d_ref, mo_ref, vo_ref):
        d_ref[...], mo_ref[...], vo_ref[...] = _adamw_math(w_ref[...], g_ref[...], m_ref[...], v_ref[...])

    spec = pl.BlockSpec((tr, cols), lambda i: (i, 0))
    return pl.pallas_call(
        body, name=name, grid=(rows // tr,), out_shape=[SDS((rows, cols), F32)] * 3, in_specs=[spec] * 4, out_specs=[spec] * 3,
        compiler_params=_params("parallel"),
    )(w, g, m, v)


def _ada_forward(c_all, w_ada, b_cols):
    n_cols = w_ada.shape[1]

    def body(c_ref, w_ref, b_ref, act_ref, mod_ref):
        cv = c_ref[...]
        act = cv * jax.nn.sigmoid(cv)
        act_ref[...] = act
        mod_ref[...] = jnp.dot(act.astype(BF16), w_ref[...].astype(BF16), preferred_element_type=F32) + b_ref[...]

    return pl.pallas_call(
        body, name="ada_forward", out_shape=[SDS(c_all.shape, F32), SDS((N_DEV, n_cols), F32)],
        in_specs=[VMEM_SPEC] * 3, out_specs=[VMEM_SPEC] * 2, compiler_params=pltpu.CompilerParams(vmem_limit_bytes=VMEM_LIMIT_V7X),
    )(c_all, w_ada, b_cols)


def _ada_backward(act_t, dmod_cols, w, m, v):
    rows, cols = w.shape
    tn = 512

    def body(a_ref, dm_ref, w_ref, m_ref, v_ref, g_ref, d_ref, mo_ref, vo_ref):
        act = a_ref[...].astype(BF16).astype(F32)
        dm = dm_ref[...].astype(BF16).astype(F32)
        g = act[:, 0:1] * dm[0:1, :]
        for b in range(1, N_DEV):
            g = g + act[:, b:b + 1] * dm[b:b + 1, :]
        g_ref[...] = g
        d_ref[...], mo_ref[...], vo_ref[...] = _adamw_math(w_ref[...], g, m_ref[...], v_ref[...])

    spec = pl.BlockSpec((rows, tn), lambda j: (0, j))
    return pl.pallas_call(
        body, name="ada_backward", grid=(cols // tn,), out_shape=[SDS((rows, cols), F32)] * 4,
        in_specs=[pl.BlockSpec((rows, N_DEV), lambda j: (0, 0)), pl.BlockSpec((N_DEV, tn), lambda j: (0, j)), spec, spec, spec],
        out_specs=[spec] * 4, compiler_params=_params("parallel"),
    )(act_t, dmod_cols, w, m, v)


def _sum_rows(gathered, n_loss):
    n = gathered.shape[2]

    def body(g_ref, o_ref):
        acc = g_ref[0]
        for r in range(1, N_DEV):
            acc = acc + g_ref[r]
        o_ref[...] = acc
        o_ref[:, n - n_loss:n] = jnp.broadcast_to(jnp.sum(acc[:, n - n_loss:n], axis=1, keepdims=True), (1, n_loss))

    return pl.pallas_call(body, name="sum_rows", out_shape=SDS((1, n), F32), in_specs=[VMEM_SPEC], out_specs=VMEM_SPEC)(gathered)


def _adamw_row(w, g, m, v):
    def body(w_ref, g_ref, m_ref, v_ref, d_ref, mo_ref, vo_ref):
        d_ref[...], mo_ref[...], vo_ref[...] = _adamw_math(w_ref[...], g_ref[...], m_ref[...], v_ref[...])

    return pl.pallas_call(body, name="adamw_vectors", out_shape=[SDS(w.shape, F32)] * 3, in_specs=[VMEM_SPEC] * 4, out_specs=[VMEM_SPEC] * 3)(w, g, m, v)


def _local_step(x, target, mod, gains, conv_w, vecs, wts, *, ts, tsq):
    g_mix, g_mlp, g_fin = gains
    mm = functools.partial(_matmul, tm=ts)
    h1 = _norm_mod(x, mod, g_mix, sc_row=1, sh_row=0, name="norm_mix", ts=ts)
    proj = mm(h1, wts["w_in"], transposed=False, reduce_blocks=False, out_dtype=F32, name="proj_in")
    xr, hr, ga, p, pooled = _seq_forward(proj, conv_w, vecs, wts["rg_a"], wts["rg_x"], wts["pool"], ts=tsq)
    br_a = mm(ga, wts["w_a"], transposed=False, reduce_blocks=False, out_dtype=F32, name="branch_a")
    br_b = mm(pooled, wts["w_b"], transposed=False, reduce_blocks=False, out_dtype=F32, name="branch_b")
    merged = _merge_forward(proj, br_a, br_b, ts=ts)
    mo = mm(merged, wts["w_out"], transposed=False, reduce_blocks=False, out_dtype=F32, name="mix_out")
    x2, h2 = _residual_norm(x, mo, mod, g_mlp, ts=ts)
    up = mm(h2, wts["w_up"], transposed=False, reduce_blocks=False, out_dtype=F32, name="mlp_up")
    ff = _relu_squared(up, ts=tsq)
    ffo = mm(ff, wts["w_down"], transposed=False, reduce_blocks=True, out_dtype=F32, name="mlp_down")
    dx3, dffo, sums_head = _loss_head(x2, ffo, mod, g_fin, target, ts=ts)

    dff = mm(dffo, wts["w_down"], transposed=True, reduce_blocks=False, out_dtype=F32, name="d_mlp_down")
    dup = _relu_squared_backward(dff, up, ts=tsq)
    dh2 = mm(dup, wts["w_up"], transposed=True, reduce_blocks=True, out_dtype=F32, name="d_mlp_up")
    dx2, dmo, sums_mlp = _norm_mod_backward(dh2, x2, dx3, mod, g_mlp, mo, sc_row=4, gate_row=2, name="norm_mlp_backward", ts=ts)
    dmerged = mm(dmo, wts["w_out"], transposed=True, reduce_blocks=False, out_dtype=F32, name="d_mix_out")
    dbr_a, dbr_b, dgab = _merge_backward(dmerged, proj, br_a, br_b, ts=ts)
    dga = mm(dbr_a, wts["w_a"], transposed=True, reduce_blocks=False, out_dtype=F32, name="d_branch_a")
    dpooled = mm(dbr_b, wts["w_b"], transposed=True, reduce_blocks=False, out_dtype=F32, name="d_branch_b")
    dproj, sums_seq, d_rg_a, d_rg_x, d_pool = _seq_backward(proj, xr, hr, p, dga, dpooled, dgab, conv_w, vecs,
                                                           wts["rg_a"], wts["rg_x"], wts["pool"], ts=tsq)
    dh1 = mm(dproj, wts["w_in"], transposed=True, reduce_blocks=True, out_dtype=F32, name="d_proj_in")
    grad_x, sums_mix = _norm_mod_backward(dh1, x, dx2, mod, g_mix, None, sc_row=1, gate_row=None, name="norm_mix_backward", ts=ts)

    wg = functools.partial(_weight_grad, ts=ts, tk=min(512, D_MODEL))
    grads = {
        "w_in": wg(h1, dproj, nb=N_CHIPS, name="grad_w_in"),
        "w_up": wg(h2, dup, nb=N_CHIPS, name="grad_w_up"),
        "w_down": wg(ff, dffo, nb=1, name="grad_w_down"),
        "w_a": wg(ga, dbr_a, nb=1, name="grad_w_branch_a"),
        "w_b": wg(pooled, dbr_b, nb=1, name="grad_w_branch_b"),
        "w_out": wg(merged, dmo, nb=1, name="grad_w_out"),
        "rg_a": d_rg_a, "rg_x": d_rg_x, "pool": d_pool,
    }
    return grad_x, (sums_mix, sums_mlp, sums_seq, sums_head), grads


def _blocks_to_pieces(w):
    nb, rows, n = w.shape
    q = rows // N_CHIPS
    return w.reshape(nb, N_CHIPS, q, n).transpose(1, 0, 2, 3).reshape(N_CHIPS, 2, nb // 2, q, n)


def _pieces_to_blocks(w):
    n_chips, _, half, q, n = w.shape
    return w.reshape(n_chips, 2 * half, q, n).transpose(1, 0, 2, 3).reshape(2 * half, n_chips * q, n)


def kernel(x, c, norm_mix_g, norm_mlp_g, w_ada, b_ada, w_in, conv_w, conv_b, w_rg_a, b_rg_a, w_rg_x, b_rg_x, a_param, w_branch_a, w_pool, b_pool, pool_scale, w_branch_b, w_out, w_up, w_down, final_g, loss_target, m_norm_mix_g, m_norm_mlp_g, m_w_ada, m_b_ada, m_w_in, m_conv_w, m_conv_b, m_w_rg_a, m_b_rg_a, m_w_rg_x, m_b_rg_x, m_a_param, m_w_branch_a, m_w_pool, m_b_pool, m_pool_scale, m_w_branch_b, m_w_out, m_w_up, m_w_down, m_final_g, v_norm_mix_g, v_norm_mlp_g, v_w_ada, v_b_ada, v_w_in, v_conv_w, v_conv_b, v_w_rg_a, v_b_rg_a, v_w_rg_x, v_b_rg_x, v_a_param, v_w_branch_a, v_w_pool, v_b_pool, v_pool_scale, v_w_branch_b, v_w_out, v_w_up, v_w_down, v_final_g):
    d = D_MODEL
    s_rows = x.shape[1]
    ts, tsq = min(512, s_rows), min(256, s_rows)
    xi, yi, ci = _coords()
    k_me, me = 2 * xi + yi, 4 * xi + 2 * yi + ci
    ada_cols = w_ada.shape[2]
    conv_cols = conv_w.shape[2]

    c_all = _allgather8(c, "gather_c").reshape(N_DEV, d)
    act_all, mod_cols = _ada_forward(c_all, w_ada[0], lax.dynamic_slice_in_dim(b_ada, k_me * ada_cols, ada_cols, axis=1))
    mod_all = _allgather8(mod_cols, "gather_mod")
    mod_mine = lax.dynamic_index_in_dim(mod_all, me, axis=1, keepdims=False)[0::2]
    mod = jnp.concatenate([mod_mine.reshape(N_MOD, d), jnp.zeros((8 - N_MOD, d), F32)], axis=0)
    conv_full = _allgather8(conv_w[0], "gather_conv_w")[0::2].transpose(1, 0, 2).reshape(CONV_WIDTH, d)
    vecs = jnp.concatenate([conv_b, b_rg_a, b_rg_x, a_param, b_pool, pool_scale, jnp.zeros((2, d), F32)], axis=0)

    def halves(w):
        return w.astype(BF16).reshape(2, -1, w.shape[-1])

    names = ("w_in", "w_up", "w_down", "w_a", "w_b", "w_out", "rg_a", "rg_x", "pool")
    mine = dict(zip(names, (w_in[0], w_up[0], w_down[0], w_branch_a[0], w_branch_b[0], w_out[0], w_rg_a[0], w_rg_x[0], w_pool[0])))
    moments_m = dict(zip(names, (m_w_in[0], m_w_up[0], m_w_down[0], m_w_branch_a[0], m_w_branch_b[0], m_w_out[0], m_w_rg_a[0], m_w_rg_x[0], m_w_pool[0])))
    moments_v = dict(zip(names, (v_w_in[0], v_w_up[0], v_w_down[0], v_w_branch_a[0], v_w_branch_b[0], v_w_out[0], v_w_rg_a[0], v_w_rg_x[0], v_w_pool[0])))
    place = jnp.stack([ci, k_me]).astype(jnp.int32)
    over_ici = _Stage()
    landed = [_add_gather_ici(over_ici, halves(mine[n])) for n in names]
    arrived = _run_stage(over_ici, "gather_over_ici")
    over_d2d = _Stage()
    landed = [_add_gather_d2d(over_d2d, arrived[i]) for i in landed]
    arrived = _run_stage(over_d2d, "gather_over_d2d")
    whole = {n: arrived[i] for n, i in zip(names, landed)}
    for n in ("rg_a", "rg_x", "pool"):
        whole[n] = whole[n].reshape(N_CHIPS, 2, N_BLOCKS // 2, BLOCK // N_CHIPS, BLOCK)
    n_in, n_up = w_in.shape[2], w_up.shape[2]
    wts = {
        "w_in": whole["w_in"].reshape(N_CHIPS, d, n_in), "w_up": whole["w_up"].reshape(N_CHIPS, d, n_up),
        "w_down": whole["w_down"].reshape(N_CHIPS, D_FF // N_CHIPS, d),
        "w_a": whole["w_a"].reshape(1, d, d), "w_b": whole["w_b"].reshape(1, d, d), "w_out": whole["w_out"].reshape(1, d, d),
        "rg_a": _pieces_to_blocks(whole["rg_a"]), "rg_x": _pieces_to_blocks(whole["rg_x"]), "pool": _pieces_to_blocks(whole["pool"]),
    }

    grad_x, (sums_mix, sums_mlp, sums_seq, sums_head), grads = _local_step(
        x[0], loss_target[0], mod, (norm_mix_g, norm_mlp_g, final_g.reshape(1, d)), conv_full, vecs, wts, ts=ts, tsq=tsq)

    dmod = jnp.concatenate([sums_mix[1:2], sums_mix[0:1], sums_mlp[3:4], sums_mlp[1:2], sums_mlp[0:1], sums_head[1:2]], axis=1)
    row = jnp.concatenate([sums_mix[2:3], sums_mlp[2:3], sums_seq[4:5], sums_seq[5:6], sums_seq[6:7], sums_seq[7:8], sums_seq[8:9],
                           sums_seq[9:10], sums_head[0:1], sums_seq[0:CONV_WIDTH].reshape(1, CONV_WIDTH * d), dmod, sums_head[2:3]], axis=1)
    n_vec, n_conv, n_mod = 9 * d, CONV_WIDTH * d, N_MOD * d
    rows_all = _allgather8(row, "gather_sums")
    total = _sum_rows(rows_all, d)
    loss = total[0, n_vec + n_conv + n_mod]
    dmod_all = rows_all[:, 0, n_vec + n_conv:n_vec + n_conv + n_mod]
    g_ada, d_ada, m_ada, v_ada = _ada_backward(act_all.T, lax.dynamic_slice_in_dim(dmod_all, k_me * ada_cols, ada_cols, axis=1),
                                               w_ada[0], m_w_ada[0], v_w_ada[0])
    g_conv = lax.dynamic_slice_in_dim(total[:, n_vec:n_vec + n_conv].reshape(CONV_WIDTH, d), k_me * conv_cols, conv_cols, axis=1)
    g_row = jnp.concatenate([total[:, 0:n_vec], g_conv.reshape(1, CONV_WIDTH * conv_cols), total[:, n_vec + n_conv:n_vec + n_conv + n_mod]], axis=1)

    def pack(nm, nl, fg, cw, ba):
        return jnp.concatenate([nm, nl, cw[1], cw[2], cw[3], cw[4], cw[5], cw[6], fg.reshape(1, d),
                                cw[0][0].reshape(1, CONV_WIDTH * conv_cols), ba], axis=1)

    w_row = pack(norm_mix_g, norm_mlp_g, final_g, (conv_w, conv_b, b_rg_a, b_rg_x, a_param, b_pool, pool_scale), b_ada)
    m_row = pack(m_norm_mix_g, m_norm_mlp_g, m_final_g, (m_conv_w, m_conv_b, m_b_rg_a, m_b_rg_x, m_a_param, m_b_pool, m_pool_scale), m_b_ada)
    v_row = pack(v_norm_mix_g, v_norm_mlp_g, v_final_g, (v_conv_w, v_conv_b, v_b_rg_a, v_b_rg_x, v_a_param, v_b_pool, v_pool_scale), v_b_ada)
    vec_out = (g_row,) + tuple(_adamw_row(w_row, g_row, m_row, v_row))

    def unpack(r):
        out = {nme: r[:, i * d:(i + 1) * d] for i, nme in enumerate(
            ("norm_mix_g", "norm_mlp_g", "conv_b", "b_rg_a", "b_rg_x", "a_param", "b_pool", "pool_scale"))}
        out["final_g"] = r[0, 8 * d:9 * d]
        out["conv_w"] = r[:, 9 * d:9 * d + CONV_WIDTH * conv_cols].reshape(1, CONV_WIDTH, conv_cols)
        out["b_ada"] = r[:, 9 * d + CONV_WIDTH * conv_cols:]
        return out

    def pieces(n):
        g = _blocks_to_pieces(grads[n]).astype(BF16) if n in ("rg_a", "rg_x", "pool") else grads[n]
        return g.reshape(N_CHIPS, 2, -1, mine[n].shape[-1])

    partials = [pieces(n) for n in names]
    to_sibling = _Stage()
    slots = [_add_reduce_d2d(to_sibling, g) for g in partials]
    from_sibling = _run_stage(to_sibling, "reduce_over_d2d")
    from_sibling = [from_sibling[i] for i in slots]
    chip_sums = [_sum_pair(g, s, place, "sum_pair_" + n) for n, g, s in zip(names, partials, from_sibling)]
    to_chips = _Stage()
    slots = [_add_reduce_ici(to_chips, p) for p in chip_sums]
    from_chips = _run_stage(to_chips, "reduce_over_ici")
    halves_done = [_sum_quarter(g, s, from_chips[i], place, "sum_quarter_" + n)
                   for n, g, s, i in zip(names, partials, from_sibling, slots)]
    swap = _Stage()
    slots = [_add_swap_halves(swap, q) for q in halves_done]
    swapped = _run_stage(swap, "swap_halves")
    quarters = [swapped[i] for i in slots]
    big_out = {}
    for n, q in zip(names, quarters):
        shape2 = (-1, mine[n].shape[-1])
        g2 = q.reshape(shape2)
        res = (g2,) + tuple(_adamw(mine[n].reshape(shape2), g2, moments_m[n].reshape(shape2), moments_v[n].reshape(shape2), "adamw_" + n))
        big_out[n] = [r.reshape((1,) + mine[n].shape) for r in res]

    key = {"w_in": "w_in", "w_rg_a": "rg_a", "w_rg_x": "rg_x", "w_branch_a": "w_a", "w_pool": "pool", "w_branch_b": "w_b", "w_out": "w_out",
           "w_up": "w_up", "w_down": "w_down"}
    order = ("norm_mix_g", "norm_mlp_g", "w_ada", "b_ada", "w_in", "conv_w", "conv_b", "w_rg_a", "b_rg_a", "w_rg_x", "b_rg_x", "a_param",
             "w_branch_a", "w_pool", "b_pool", "pool_scale", "w_branch_b", "w_out", "w_up", "w_down", "final_g")
    ada_out = [g_ada[None], d_ada[None], m_ada[None], v_ada[None]]
    outs = [loss, grad_x[None]]
    for kind in range(4):
        vec = unpack(vec_out[kind])
        for nme in order:
            outs.append(ada_out[kind] if nme == "w_ada" else big_out[key[nme]][kind] if nme in key else vec[nme])
    return tuple(outs)
```

```python
import functools

import jax
import jax.numpy as jnp
from jax import lax
from jax.experimental import pallas as pl
from jax.experimental.pallas import tpu as pltpu

F32, BF16 = jnp.float32, jnp.bfloat16
D_MODEL = 1024
D_FF = 4 * D_MODEL
N_BLOCKS = 4
BLOCK = D_MODEL // N_BLOCKS
CONV_WIDTH = 4
POOL_WINDOWS = (2, 4, 8, 16)
CONV_HALO = 8
POOL_HALO = 16
N_MOD = 6
EPS = 1e-6
C_RG = 8.0
ADAM_LR, ADAM_B1, ADAM_B2, ADAM_EPS, ADAM_WD, ADAM_STEP = 0.001, 0.9, 0.999, 1e-08, 0.01, 10
N_DEV = 8
N_CHIPS = 4
LANES = 128
VMEM_LIMIT_V7X = 56 * 2**20
MESH = pl.DeviceIdType.MESH
SDS = jax.ShapeDtypeStruct
HBM_SPEC = pl.BlockSpec(memory_space=pltpu.HBM)
VMEM_SPEC = pl.BlockSpec(memory_space=pltpu.VMEM)


def _params(*semantics):
    return pltpu.CompilerParams(dimension_semantics=semantics, vmem_limit_bytes=VMEM_LIMIT_V7X)


def _coords():
    return lax.axis_index("x"), lax.axis_index("y"), lax.axis_index("c")


def _flip(v, bit):
    return 1 - v if bit else v


def _allgather8(v, name):
    def body(v_ref, out_ref, send_sems, recv_sems, local_sem):
        x, y, c = _coords()
        me = 4 * x + 2 * y + c
        mine = pltpu.make_async_copy(v_ref, out_ref.at[me], local_sem)
        mine.start()
        sends = []
        for r in range(1, N_DEV):
            peer = (_flip(x, r & 4), _flip(y, r & 2), _flip(c, r & 1))
            cp = pltpu.make_async_remote_copy(src_ref=v_ref, dst_ref=out_ref.at[me], send_sem=send_sems.at[r - 1],
                                              recv_sem=recv_sems.at[r - 1], device_id=peer, device_id_type=MESH)
            cp.start()
            sends.append(cp)
        for r in range(1, N_DEV):
            pltpu.make_async_remote_copy(src_ref=v_ref, dst_ref=out_ref.at[me ^ r], send_sem=send_sems.at[r - 1],
                                         recv_sem=recv_sems.at[r - 1], device_id=(x, y, c), device_id_type=MESH).wait_recv()
        for cp in sends:
            cp.wait_send()
        mine.wait()

    return pl.pallas_call(
        body, name=name, out_shape=SDS((N_DEV,) + v.shape, v.dtype), in_specs=[VMEM_SPEC], out_specs=VMEM_SPEC,
        scratch_shapes=[pltpu.SemaphoreType.DMA((N_DEV - 1,)), pltpu.SemaphoreType.DMA((N_DEV - 1,)), pltpu.SemaphoreType.DMA(())],
    )(v)


def _gather_weights(shards, name):
    n = len(shards)

    def body(*refs):
        ins, outs = refs[:n], refs[n:2 * n]
        send_sems, recv_sems, local_sems = refs[2 * n:]
        x, y, c = _coords()
        k_me = 2 * x + y
        chips = [(1 - x, y), (x, 1 - y), (1 - x, 1 - y)]
        sibling = (x, y, 1 - c)

        def copy(i, sem, k, half, to):
            return pltpu.make_async_remote_copy(src_ref=outs[i].at[k, half], dst_ref=outs[i].at[k, half], send_sem=send_sems.at[sem],
                                                recv_sem=recv_sems.at[sem], device_id=to, device_id_type=MESH)

        own, first, passed = [], [], []
        for i in range(n):
            lc = pltpu.make_async_copy(ins[i], outs[i].at[k_me], local_sems.at[i])
            lc.start()
            own.append(lc)
            for j, (px, py) in enumerate(chips):
                cp = pltpu.make_async_remote_copy(src_ref=ins[i].at[c], dst_ref=outs[i].at[k_me, c], send_sem=send_sems.at[6 * i + j],
                                                  recv_sem=recv_sems.at[6 * i + j], device_id=(px, py, c), device_id_type=MESH)
                cp.start()
                first.append(cp)
        for i in range(n):
            for j, (px, py) in enumerate(chips):
                copy(i, 6 * i + j, 2 * px + py, c, (x, y, c)).wait_recv()
                fw = copy(i, 6 * i + 3 + j, 2 * px + py, c, sibling)
                fw.start()
                passed.append(fw)
        for i in range(n):
            for j, (px, py) in enumerate(chips):
                copy(i, 6 * i + 3 + j, 2 * px + py, 1 - c, (x, y, c)).wait_recv()
        for cp in first + passed:
            cp.wait_send()
        for lc in own:
            lc.wait()

    return pl.pallas_call(
        body, name=name, out_shape=[SDS((N_CHIPS,) + s.shape, s.dtype) for s in shards],
        in_specs=[HBM_SPEC] * n, out_specs=[HBM_SPEC] * n,
        scratch_shapes=[pltpu.SemaphoreType.DMA((6 * n,)), pltpu.SemaphoreType.DMA((6 * n,)), pltpu.SemaphoreType.DMA((n,))],
    )(*shards)


def _scatter_partials(grads, name):
    n = len(grads)

    def body(*refs):
        ins, outs = refs[:n], refs[n:2 * n]
        send_sems, recv_sems, local_sems = refs[2 * n:]
        x, y, c = _coords()
        own, sends = [], []
        for i in range(n):
            lc = pltpu.make_async_copy(ins[i].at[2 * x + y, c], outs[i].at[0], local_sems.at[i])
            lc.start()
            own.append(lc)
            for r in range(1, N_DEV):
                tx, ty, tc = _flip(x, r & 4), _flip(y, r & 2), _flip(c, r & 1)
                cp = pltpu.make_async_remote_copy(src_ref=ins[i].at[2 * tx + ty, tc], dst_ref=outs[i].at[r],
                                                  send_sem=send_sems.at[7 * i + r - 1], recv_sem=recv_sems.at[7 * i + r - 1],
                                                  device_id=(tx, ty, tc), device_id_type=MESH)
                cp.start()
                sends.append(cp)
        for i in range(n):
            for r in range(1, N_DEV):
                pltpu.make_async_remote_copy(src_ref=ins[i].at[0, 0], dst_ref=outs[i].at[r], send_sem=send_sems.at[7 * i + r - 1],
                                             recv_sem=recv_sems.at[7 * i + r - 1], device_id=(x, y, c), device_id_type=MESH).wait_recv()
        for cp in sends:
            cp.wait_send()
        for lc in own:
            lc.wait()

    return pl.pallas_call(
        body, name=name, out_shape=[SDS((N_DEV,) + g.shape[2:], g.dtype) for g in grads],
        in_specs=[HBM_SPEC] * n, out_specs=[HBM_SPEC] * n,
        scratch_shapes=[pltpu.SemaphoreType.DMA((7 * n,)), pltpu.SemaphoreType.DMA((7 * n,)), pltpu.SemaphoreType.DMA((n,))],
    )(*grads)


def _pair_exchange(pieces, name):
    n = len(pieces)

    def body(*refs):
        ins, outs = refs[:n], refs[n:2 * n]
        send_sems, recv_sems, local_sems = refs[2 * n:]
        x, y, c = _coords()
        own, sends = [], []
        for i in range(n):
            lc = pltpu.make_async_copy(ins[i], outs[i].at[c], local_sems.at[i])
            lc.start()
            own.append(lc)
            cp = pltpu.make_async_remote_copy(src_ref=ins[i], dst_ref=outs[i].at[c], send_sem=send_sems.at[i], recv_sem=recv_sems.at[i],
                                              device_id=(x, y, 1 - c), device_id_type=MESH)
            cp.start()
            sends.append(cp)
        for i in range(n):
            pltpu.make_async_remote_copy(src_ref=ins[i], dst_ref=outs[i].at[1 - c], send_sem=send_sems.at[i], recv_sem=recv_sems.at[i],
                                         device_id=(x, y, c), device_id_type=MESH).wait_recv()
        for cp in sends:
            cp.wait_send()
        for lc in own:
            lc.wait()

    return pl.pallas_call(
        body, name=name, out_shape=[SDS((2,) + p.shape, p.dtype) for p in pieces],
        in_specs=[HBM_SPEC] * n, out_specs=[HBM_SPEC] * n,
        scratch_shapes=[pltpu.SemaphoreType.DMA((n,)), pltpu.SemaphoreType.DMA((n,)), pltpu.SemaphoreType.DMA((n,))],
    )(*pieces)


class _Stage:
    def __init__(self):
        self.inputs, self.in_specs, self.outputs, self.out_specs, self.aliases = [], [], [], [], {}
        self.parts, self.n_copies = [], 0

    def add(self, inputs, in_spec, outputs, out_spec, n_copies, build, alias=False):
        i0, o0 = len(self.inputs), len(self.outputs)
        self.inputs += inputs
        self.in_specs += [in_spec] * len(inputs)
        self.outputs += outputs
        self.out_specs += [out_spec] * len(outputs)
        if alias:
            self.aliases.update({i0 + i: o0 + i for i in range(len(inputs))})
        self.parts.append((build, i0, len(inputs), o0, len(outputs)))
        self.n_copies += n_copies
        return list(range(o0, o0 + len(outputs)))

    def copies(self, in_refs, out_refs):
        out = []
        for build, i0, ni, o0, no in self.parts:
            out += build(in_refs[i0:i0 + ni], out_refs[o0:o0 + no])
        assert len(out) == self.n_copies
        return out

    def run(self, in_refs, out_refs, send_sems, recv_sems, start):
        x, y, c = _coords()
        for s, (src, dst, to, landing) in enumerate(self.copies(in_refs, out_refs)):
            if to is None:
                cp = pltpu.make_async_copy(src, dst, send_sems.at[s])
                cp.start() if start else cp.wait()
                continue
            cp = pltpu.make_async_remote_copy(src_ref=src, dst_ref=dst, send_sem=send_sems.at[s], recv_sem=recv_sems.at[s],
                                              device_id=to, device_id_type=MESH)
            if start:
                cp.start()
            else:
                pltpu.make_async_remote_copy(src_ref=landing, dst_ref=landing, send_sem=send_sems.at[s], recv_sem=recv_sems.at[s],
                                             device_id=(x, y, c), device_id_type=MESH).wait_recv()
                cp.wait_send()


def _hosted(stage, body, *, name, in_specs, out_specs, out_shape, grid=(), scratch_shapes=(), compiler_params=None):
    if stage is None:
        return pl.pallas_call(body, name=name, grid=grid, in_specs=in_specs, out_specs=out_specs, out_shape=out_shape,
                              scratch_shapes=list(scratch_shapes), compiler_params=compiler_params)
    single = not isinstance(out_shape, (list, tuple))
    h_out_shape = [out_shape] if single else list(out_shape)
    h_out_specs = [out_specs] if single else list(out_specs)
    n_in, n_out, n_scr = len(in_specs), len(h_out_shape), len(scratch_shapes)
    s_in, s_out = len(stage.inputs), len(stage.outputs)

    def wrapped(*refs):
        h_in, st_in = refs[:n_in], refs[n_in:n_in + s_in]
        h_o, st_o = refs[n_in + s_in:n_in + s_in + n_out], refs[n_in + s_in + n_out:n_in + s_in + n_out + s_out]
        h_scr = refs[n_in + s_in + n_out + s_out:n_in + s_in + n_out + s_out + n_scr]
        send_sems, recv_sems = refs[n_in + s_in + n_out + s_out + n_scr:]
        if not grid:
            stage.run(st_in, st_o, send_sems, recv_sems, True)
            if body is not None:
                body(*h_in, *h_o, *h_scr)
            stage.run(st_in, st_o, send_sems, recv_sems, False)
            return
        ids = [pl.program_id(a) for a in range(len(grid))]
        first = functools.reduce(jnp.logical_and, [i == 0 for i in ids])
        last = functools.reduce(jnp.logical_and, [i == g - 1 for i, g in zip(ids, grid)])
        pl.when(first)(lambda: stage.run(st_in, st_o, send_sems, recv_sems, True))
        body(*h_in, *h_o, *h_scr)
        pl.when(last)(lambda: stage.run(st_in, st_o, send_sems, recv_sems, False))

    call = pl.pallas_call(
        wrapped, name=name, grid=grid, in_specs=list(in_specs) + stage.in_specs, out_specs=h_out_specs + stage.out_specs,
        out_shape=h_out_shape + stage.outputs, input_output_aliases={n_in + i: n_out + o for i, o in stage.aliases.items()},
        scratch_shapes=list(scratch_shapes) + [pltpu.SemaphoreType.DMA((stage.n_copies,)), pltpu.SemaphoreType.DMA((stage.n_copies,))],
        compiler_params=pltpu.CompilerParams(dimension_semantics=("arbitrary",) * len(grid), vmem_limit_bytes=VMEM_LIMIT_V7X),
    )

    def run(*operands):
        outs = call(*operands, *stage.inputs)
        host = outs[:n_out]
        return (host[0] if single else host), outs[n_out:]

    return run


def _run_stage(stage, name):
    return _hosted(stage, None, name=name, in_specs=[], out_specs=[], out_shape=[])()[1]


def _chips():
    x, y, _ = _coords()
    return [(1 - x, y), (x, 1 - y), (1 - x, 1 - y)]


def _add_allgather8(stage, v):
    def build(ins, outs):
        x, y, c = _coords()
        me = 4 * x + 2 * y + c
        copies = [(ins[0], outs[0].at[me], None, None)]
        for r in range(1, N_DEV):
            peer = (_flip(x, r & 4), _flip(y, r & 2), _flip(c, r & 1))
            copies.append((ins[0], outs[0].at[me], peer, outs[0].at[me ^ r]))
        return copies

    return stage.add([v], VMEM_SPEC, [SDS((N_DEV,) + v.shape, v.dtype)], VMEM_SPEC, N_DEV, build)[0]


def _add_gather_ici(stage, shard):
    def build(ins, outs):
        x, y, c = _coords()
        k_me = 2 * x + y
        copies = [(ins[0], outs[0].at[k_me], (x, y, 1 - c), outs[0].at[k_me])]
        for px, py in _chips():
            copies.append((ins[0].at[c], outs[0].at[k_me, c], (px, py, c), outs[0].at[2 * px + py, c]))
        return copies

    return stage.add([shard], HBM_SPEC, [SDS((N_CHIPS,) + shard.shape, shard.dtype)], HBM_SPEC, N_CHIPS, build)[0]


def _add_gather_d2d(stage, whole):
    def build(ins, outs):
        x, y, c = _coords()
        return [(outs[0].at[2 * px + py, c], outs[0].at[2 * px + py, c], (x, y, 1 - c), outs[0].at[2 * px + py, 1 - c]) for px, py in _chips()]

    return stage.add([whole], HBM_SPEC, [SDS(whole.shape, whole.dtype)], HBM_SPEC, N_CHIPS - 1, build, alias=True)[0]


def _add_reduce_d2d(stage, grads):
    def build(ins, outs):
        x, y, c = _coords()
        return [(ins[0].at[k, 1 - c], outs[0].at[k], (x, y, 1 - c), outs[0].at[k]) for k in range(N_CHIPS)]

    return stage.add([grads], HBM_SPEC, [SDS((N_CHIPS,) + grads.shape[2:], grads.dtype)], HBM_SPEC, N_CHIPS, build)[0]


def _add_reduce_ici(stage, partial):
    def build(ins, outs):
        x, y, c = _coords()
        return [(ins[0].at[2 * px + py], outs[0].at[j], (px, py, c), outs[0].at[j]) for j, (px, py) in enumerate(_chips())]

    return stage.add([partial], HBM_SPEC, [SDS((N_CHIPS - 1,) + partial.shape[1:], partial.dtype)], HBM_SPEC, N_CHIPS - 1, build)[0]


def _add_swap_halves(stage, quarter):
    def build(ins, outs):
        x, y, c = _coords()
        return [(outs[0].at[c], outs[0].at[c], (x, y, 1 - c), outs[0].at[1 - c])]

    return stage.add([quarter], HBM_SPEC, [SDS(quarter.shape, quarter.dtype)], HBM_SPEC, 1, build, alias=True)[0]


def _sum_pair(grads, from_sibling, place, name):
    _, _, rows, cols = grads.shape
    tr = min(rows, 256)

    def body(place_ref, g_ref, s_ref, o_ref):
        o_ref[0] = (g_ref[0, 0].astype(F32) + s_ref[0].astype(F32)).astype(BF16)

    spec = pl.BlockSpec((1, tr, cols), lambda k, i, place_ref: (k, i, 0))
    return pl.pallas_call(
        body, name=name, out_shape=SDS((N_CHIPS, rows, cols), BF16),
        grid_spec=pltpu.PrefetchScalarGridSpec(
            num_scalar_prefetch=1, grid=(N_CHIPS, rows // tr),
            in_specs=[pl.BlockSpec((1, 1, tr, cols), lambda k, i, place_ref: (k, place_ref[0], i, 0)), spec], out_specs=spec),
        compiler_params=_params("parallel", "parallel"),
    )(place, grads, from_sibling)


def _sum_quarter(grads, from_sibling, from_chips, place, name):
    _, _, rows, cols = grads.shape
    tr = min(rows, 256)

    def body(place_ref, g_ref, s_ref, c_ref, o_ref):
        acc = g_ref[0, 0].astype(F32) + s_ref[0].astype(F32)
        for j in range(N_CHIPS - 1):
            acc = acc + c_ref[j].astype(F32)
        o_ref[0] = acc

    return pl.pallas_call(
        body, name=name, out_shape=SDS((2, rows, cols), F32),
        grid_spec=pltpu.PrefetchScalarGridSpec(
            num_scalar_prefetch=1, grid=(rows // tr,),
            in_specs=[pl.BlockSpec((1, 1, tr, cols), lambda i, place_ref: (place_ref[1], place_ref[0], i, 0)),
                      pl.BlockSpec((1, tr, cols), lambda i, place_ref: (place_ref[1], i, 0)),
                      pl.BlockSpec((N_CHIPS - 1, tr, cols), lambda i, place_ref: (0, i, 0))],
            out_specs=pl.BlockSpec((1, tr, cols), lambda i, place_ref: (place_ref[0], i, 0))),
        compiler_params=_params("parallel"),
    )(place, grads, from_sibling, from_chips)


def _matmul(a, w, *, transposed, reduce_blocks, out_dtype, name, tm, stage=None):
    m_rows = a.shape[0]
    nb, r, c = w.shape
    kb, nblk = (c, r) if transposed else (r, c)
    dims = (((1,), (1,)), ((), ())) if transposed else (((1,), (0,)), ((), ()))
    assert a.shape[1] == (nb * kb if reduce_blocks else kb) and m_rows % tm == 0
    n_out = nblk if reduce_blocks else nb * nblk

    def body(a_ref, w_ref, o_ref):
        if reduce_blocks:
            acc = lax.dot_general(a_ref[:, 0:kb], w_ref[0], dims, preferred_element_type=F32)
            for k in range(1, nb):
                acc += lax.dot_general(a_ref[:, k * kb:(k + 1) * kb], w_ref[k], dims, preferred_element_type=F32)
            o_ref[...] = acc.astype(o_ref.dtype)
        else:
            a_blk = a_ref[...]
            for k in range(nb):
                o_ref[:, k * nblk:(k + 1) * nblk] = lax.dot_general(a_blk, w_ref[k], dims, preferred_element_type=F32).astype(o_ref.dtype)

    return _hosted(
        stage, body, name=name, grid=(m_rows // tm,), out_shape=SDS((m_rows, n_out), out_dtype),
        in_specs=[pl.BlockSpec((tm, a.shape[1]), lambda m: (m, 0)), pl.BlockSpec((nb, r, c), lambda m: (0, 0, 0))],
        out_specs=pl.BlockSpec((tm, n_out), lambda m: (m, 0)), compiler_params=_params("parallel"),
    )(a, w)


def _weight_grad(a, b, *, nb, name, ts, tk, stage=None):
    s_rows, k1 = a.shape
    tn = b.shape[1] // nb
    n_s = s_rows // ts
    assert s_rows % ts == 0 and k1 % tk == 0 and b.shape[1] % nb == 0

    def body(a_ref, b_ref, o_ref, acc):
        s = pl.program_id(2)
        part = lax.dot_general(a_ref[...], b_ref[...], (((0,), (0,)), ((), ())), preferred_element_type=F32)

        @pl.when(s == 0)
        def _():
            acc[...] = part

        @pl.when(s > 0)
        def _():
            acc[...] += part

        @pl.when(s == n_s - 1)
        def _():
            o_ref[0] = acc[...].astype(o_ref.dtype)

    return _hosted(
        stage, body, name=name, grid=(nb, k1 // tk, n_s), out_shape=SDS((nb, k1, tn), BF16),
        in_specs=[pl.BlockSpec((ts, tk), lambda n, i, s: (s, i)), pl.BlockSpec((ts, tn), lambda n, i, s: (s, n))],
        out_specs=pl.BlockSpec((1, tk, tn), lambda n, i, s: (n, i, 0)), scratch_shapes=[pltpu.VMEM((tk, tn), F32)],
        compiler_params=_params("parallel", "parallel", "arbitrary"),
    )(a, b)


def _row_spec(ts, cols, col_block=0):
    return pl.BlockSpec((ts, cols), lambda t: (t, col_block))


def _vec_spec(rows, cols):
    return pl.BlockSpec((rows, cols), lambda t: (0, 0))


def _rms(x):
    r = lax.rsqrt(jnp.mean(x * x, axis=-1, keepdims=True) + EPS)
    return x * r, r


def _accumulate(acc_ref, row, value):
    acc_ref[row:row + 1, :] += jnp.sum(value, axis=0, keepdims=True)


def _gelu_parts(y):
    k0, k1 = 0.7978845608028654, 0.044715
    th = jnp.tanh(k0 * (y + k1 * (y * y * y)))
    gelu = 0.5 * y * (1.0 + th)
    dgelu = 0.5 * (1.0 + th) + 0.5 * y * (1.0 - th * th) * (k0 * (1.0 + 3.0 * k1 * (y * y)))
    return gelu, dgelu


def _neg_expm1(z):
    series = z * (1.0 + z * (0.5 + z * (1.0 / 6 + z * (1.0 / 24 + z * (1.0 / 120 + z * (1.0 / 720))))))
    return -jnp.where(z > -0.25, series, jnp.exp(z) - 1.0)


def _norm_mod(x, mod, gain, *, sc_row, sh_row, name, ts, stage=None):
    s_rows, d = x.shape

    def body(x_ref, mod_ref, g_ref, h_ref):
        xh, _ = _rms(x_ref[...])
        h = (xh * g_ref[...]) * (1.0 + mod_ref[sc_row:sc_row + 1, :]) + mod_ref[sh_row:sh_row + 1, :]
        h_ref[...] = h.astype(BF16)

    return _hosted(
        stage, body, name=name, grid=(s_rows // ts,), out_shape=SDS((s_rows, d), BF16),
        in_specs=[_row_spec(ts, d), _vec_spec(8, d), _vec_spec(1, d)], out_specs=_row_spec(ts, d),
        compiler_params=_params("parallel"),
    )(x, mod, gain)


def _seq_forward(proj, conv_w, vecs, w_a, w_x, w_p, *, ts, stage=None):
    s_rows = proj.shape[0]
    d = D_MODEL

    def body(x_ref, xh_ref, y_ref, u_ref, uh_ref, cw_ref, vec_ref, wa_ref, wx_ref, wp_ref,
             xr_o, hr_o, ga_o, p_o, pooled_o, carry, a_scr, b_scr):
        t = pl.program_id(0)

        @pl.when(t == 0)
        def _():
            carry[...] = jnp.zeros_like(carry)

        x = x_ref[...]
        halo = jnp.where(t == 0, 0.0, xh_ref[...])
        xx = jnp.concatenate([halo, x], axis=0)
        xr = vec_ref[0:1, :] + x * cw_ref[CONV_WIDTH - 1:CONV_WIDTH, :]
        for j in range(CONV_WIDTH - 1):
            xr = xr + pltpu.roll(xx, CONV_WIDTH - 1 - j, 0)[CONV_HALO:] * cw_ref[j:j + 1, :]
        xr_o[...] = xr
        xrb = xr.astype(BF16)
        zr = jnp.concatenate([jnp.dot(xrb[:, g * BLOCK:(g + 1) * BLOCK], wa_ref[g], preferred_element_type=F32)
                              for g in range(N_BLOCKS)], axis=1) + vec_ref[1:2, :]
        zi = jnp.concatenate([jnp.dot(xrb[:, g * BLOCK:(g + 1) * BLOCK], wx_ref[g], preferred_element_type=F32)
                              for g in range(N_BLOCKS)], axis=1) + vec_ref[2:3, :]
        r = jax.nn.sigmoid(zr)
        gate_i = jax.nn.sigmoid(zi)
        log_a = (-C_RG * r) * jax.nn.softplus(vec_ref[3:4, :])
        rows = lax.broadcasted_iota(jnp.int32, (ts, d), 0)
        mult = jnp.where((rows == 0) & (t == 0), 1.0, jnp.sqrt(_neg_expm1(2.0 * log_a)))
        a_scr[...] = jnp.exp(log_a)
        b_scr[...] = xr * gate_i * mult

        sub = lax.broadcasted_iota(jnp.int32, (8, d), 0)

        def chunk(i, h):
            a = a_scr[pl.ds(i * 8, 8), :]
            b = b_scr[pl.ds(i * 8, 8), :]
            for s in (1, 2, 4):
                keep = sub >= s
                b = jnp.where(keep, a * pltpu.roll(b, s, 0) + b, b)
                a = jnp.where(keep, a * pltpu.roll(a, s, 0), a)
            hh = a * h + b
            hr_o[pl.ds(i * 8, 8), :] = hh
            return hh[7:8, :]

        carry[...] = lax.fori_loop(0, ts // 8, chunk, carry[...])
        gelu, _ = _gelu_parts(y_ref[...])
        ga_o[...] = (gelu * hr_o[...]).astype(BF16)

        u = u_ref[...]
        uu = jnp.concatenate([jnp.where(t == 0, 0.0, uh_ref[...]), u], axis=0)
        pos = (rows + t * ts + 1).astype(F32)
        sums, have, parts = uu, 1, []
        for g, win in enumerate(POOL_WINDOWS):
            while have < win:
                sums = sums + pltpu.roll(sums, have, 0)
                have *= 2
            mean = sums[POOL_HALO:, g * BLOCK:(g + 1) * BLOCK] / jnp.minimum(pos[:, :BLOCK], float(win))
            parts.append(mean - u[:, g * BLOCK:(g + 1) * BLOCK])
        pb = jnp.concatenate(parts, axis=1).astype(BF16)
        p_o[...] = pb
        mixed = jnp.concatenate([jnp.dot(pb[:, g * BLOCK:(g + 1) * BLOCK], wp_ref[g], preferred_element_type=F32)
                                 for g in range(N_BLOCKS)], axis=1) + vec_ref[4:5, :]
        pooled_o[...] = (mixed * vec_ref[5:6, :]).astype(BF16)

    halo_x = pl.BlockSpec((CONV_HALO, d), lambda t: (jnp.maximum(t * (ts // CONV_HALO) - 1, 0), 0))
    halo_u = pl.BlockSpec((POOL_HALO, d), lambda t: (jnp.maximum(t * (ts // POOL_HALO) - 1, 0), 2))
    wspec = pl.BlockSpec((N_BLOCKS, BLOCK, BLOCK), lambda t: (0, 0, 0))
    return _hosted(
        stage, body, name="seq_forward", grid=(s_rows // ts,),
        out_shape=[SDS((s_rows, d), F32), SDS((s_rows, d), F32), SDS((s_rows, d), BF16), SDS((s_rows, d), BF16), SDS((s_rows, d), BF16)],
        in_specs=[_row_spec(ts, d, 0), halo_x, _row_spec(ts, d, 1), _row_spec(ts, d, 2), halo_u, _vec_spec(CONV_WIDTH, d), _vec_spec(8, d),
                  wspec, wspec, wspec],
        out_specs=[_row_spec(ts, d)] * 5,
        scratch_shapes=[pltpu.VMEM((1, d), F32), pltpu.VMEM((ts, d), F32), pltpu.VMEM((ts, d), F32)],
        compiler_params=_params("arbitrary"),
    )(proj, proj, proj, proj, proj, conv_w, vecs, w_a, w_x, w_p)


def _merge_forward(proj, br_a, br_b, *, ts):
    s_rows, d = br_a.shape

    def body(ga_ref, gb_ref, a_ref, b_ref, o_ref):
        o_ref[...] = (jax.nn.sigmoid(ga_ref[...]) * a_ref[...] + jax.nn.sigmoid(gb_ref[...]) * b_ref[...]).astype(BF16)

    return pl.pallas_call(
        body, name="merge_forward", grid=(s_rows // ts,), out_shape=SDS((s_rows, d), BF16),
        in_specs=[_row_spec(ts, d, 3), _row_spec(ts, d, 4), _row_spec(ts, d), _row_spec(ts, d)], out_specs=_row_spec(ts, d),
        compiler_params=_params("parallel"),
    )(proj, proj, br_a, br_b)


def _residual_norm(x, mo, mod, gain, *, ts):
    s_rows, d = x.shape

    def body(x_ref, mo_ref, mod_ref, g_ref, x2_ref, h_ref):
        x2 = x_ref[...] + mod_ref[2:3, :] * mo_ref[...]
        x2_ref[...] = x2
        xh, _ = _rms(x2)
        h_ref[...] = ((xh * g_ref[...]) * (1.0 + mod_ref[4:5, :]) + mod_ref[3:4, :]).astype(BF16)

    return pl.pallas_call(
        body, name="residual_norm", grid=(s_rows // ts,), out_shape=[SDS((s_rows, d), F32), SDS((s_rows, d), BF16)],
        in_specs=[_row_spec(ts, d), _row_spec(ts, d), _vec_spec(8, d), _vec_spec(1, d)], out_specs=[_row_spec(ts, d)] * 2,
        compiler_params=_params("parallel"),
    )(x, mo, mod, gain)


def _relu_squared(up, *, ts):
    s_rows, f = up.shape

    def body(u_ref, o_ref):
        o_ref[...] = jnp.square(jnp.maximum(u_ref[...], 0.0)).astype(BF16)

    return pl.pallas_call(
        body, name="relu_squared", grid=(s_rows // ts,), out_shape=SDS((s_rows, f), BF16),
        in_specs=[_row_spec(ts, f)], out_specs=_row_spec(ts, f), compiler_params=_params("parallel"),
    )(up)


def _loss_head(x2, ffo, mod, final_g, target, *, ts):
    s_rows, d = x2.shape

    def body(x2_ref, f_ref, mod_ref, g_ref, t_ref, dx_ref, df_ref, acc_ref):
        @pl.when(pl.program_id(0) == 0)
        def _():
            acc_ref[...] = jnp.zeros_like(acc_ref)

        ffo_t = f_ref[...]
        x3 = x2_ref[...] + mod_ref[5:6, :] * ffo_t
        xh, r = _rms(x3)
        err = xh * g_ref[...] - t_ref[...]
        dy = err / d
        dxh = dy * g_ref[...]
        dx3 = r * (dxh - xh * jnp.mean(dxh * xh, axis=-1, keepdims=True))
        dx_ref[...] = dx3
        df_ref[...] = (dx3 * mod_ref[5:6, :]).astype(BF16)
        _accumulate(acc_ref, 0, dy * xh)
        _accumulate(acc_ref, 1, dx3 * ffo_t)
        _accumulate(acc_ref, 2, jnp.square(err) * (0.5 / d))

    return pl.pallas_call(
        body, name="loss_head", grid=(s_rows // ts,), out_shape=[SDS((s_rows, d), F32), SDS((s_rows, d), BF16), SDS((8, d), F32)],
        in_specs=[_row_spec(ts, d), _row_spec(ts, d), _vec_spec(8, d), _vec_spec(1, d), _row_spec(ts, d)],
        out_specs=[_row_spec(ts, d), _row_spec(ts, d), _vec_spec(8, d)], compiler_params=_params("arbitrary"),
    )(x2, ffo, mod, final_g, target)


def _relu_squared_backward(dff, up, *, ts, stage=None):
    s_rows, f = up.shape

    def body(d_ref, u_ref, o_ref):
        o_ref[...] = (d_ref[...] * (2.0 * jnp.maximum(u_ref[...], 0.0))).astype(BF16)

    return _hosted(
        stage, body, name="relu_squared_backward", grid=(s_rows // ts,), out_shape=SDS((s_rows, f), BF16),
        in_specs=[_row_spec(ts, f), _row_spec(ts, f)], out_specs=_row_spec(ts, f), compiler_params=_params("parallel"),
    )(dff, up)


def _norm_mod_backward(dh, x, dres, mod, gain, mo, *, sc_row, gate_row, name, ts, stage=None):
    s_rows, d = x.shape
    gated = mo is not None

    def body(*refs):
        if gated:
            dh_ref, x_ref, dres_ref, mod_ref, g_ref, mo_ref, dx_ref, dmo_ref, acc_ref = refs
        else:
            dh_ref, x_ref, dres_ref, mod_ref, g_ref, dx_ref, acc_ref = refs

        @pl.when(pl.program_id(0) == 0)
        def _():
            acc_ref[...] = jnp.zeros_like(acc_ref)

        dh_t = dh_ref[...]
        xh, r = _rms(x_ref[...])
        _accumulate(acc_ref, 0, dh_t * (xh * g_ref[...]))
        _accumulate(acc_ref, 1, dh_t)
        dn = dh_t * (1.0 + mod_ref[sc_row:sc_row + 1, :])
        _accumulate(acc_ref, 2, dn * xh)
        dxh = dn * g_ref[...]
        dx = dres_ref[...] + r * (dxh - xh * jnp.mean(dxh * xh, axis=-1, keepdims=True))
        dx_ref[...] = dx
        if gated:
            dmo_ref[...] = (dx * mod_ref[gate_row:gate_row + 1, :]).astype(BF16)
            _accumulate(acc_ref, 3, dx * mo_ref[...])

    rows = _row_spec(ts, d)
    return _hosted(
        stage, body, name=name, grid=(s_rows // ts,),
        out_shape=[SDS((s_rows, d), F32)] + ([SDS((s_rows, d), BF16)] if gated else []) + [SDS((8, d), F32)],
        in_specs=[rows, rows, rows, _vec_spec(8, d), _vec_spec(1, d)] + ([rows] if gated else []),
        out_specs=[rows] + ([rows] if gated else []) + [_vec_spec(8, d)], compiler_params=_params("arbitrary"),
    )(*([dh, x, dres, mod, gain] + ([mo] if gated else [])))


def _merge_backward(dmerged, proj, br_a, br_b, *, ts, stage=None):
    s_rows, d = br_a.shape

    def body(dm_ref, ga_ref, gb_ref, a_ref, b_ref, da_ref, db_ref, dg_ref):
        dm = dm_ref[...]
        sa, sb = jax.nn.sigmoid(ga_ref[...]), jax.nn.sigmoid(gb_ref[...])
        da_ref[...] = (dm * sa).astype(BF16)
        db_ref[...] = (dm * sb).astype(BF16)
        dg_ref[:, 0:d] = (dm * a_ref[...] * (sa * (1.0 - sa))).astype(BF16)
        dg_ref[:, d:2 * d] = (dm * b_ref[...] * (sb * (1.0 - sb))).astype(BF16)

    rows = _row_spec(ts, d)
    return _hosted(
        stage, body, name="merge_backward", grid=(s_rows // ts,), out_shape=[SDS((s_rows, d), BF16), SDS((s_rows, d), BF16), SDS((s_rows, 2 * d), BF16)],
        in_specs=[rows, _row_spec(ts, d, 3), _row_spec(ts, d, 4), rows, rows], out_specs=[rows, rows, _row_spec(ts, 2 * d)],
        compiler_params=_params("parallel"),
    )(dmerged, proj, proj, br_a, br_b)


def _seq_backward(proj, xr, hr, p, dga, dpooled, dgab, conv_w, vecs, w_a, w_x, w_p, *, ts, stage=None):
    s_rows = proj.shape[0]
    d = D_MODEL
    n_t = s_rows // ts

    def block_dots(lhs, w_ref, dims):
        return jnp.concatenate([lax.dot_general(lhs[:, g * BLOCK:(g + 1) * BLOCK], w_ref[g], dims, preferred_element_type=F32)
                                for g in range(N_BLOCKS)], axis=1)

    def add_block_grads(dw_ref, lhs, rhs):
        for g in range(N_BLOCKS):
            dw_ref[g] += lax.dot_general(lhs[:, g * BLOCK:(g + 1) * BLOCK], rhs[:, g * BLOCK:(g + 1) * BLOCK],
                                         (((0,), (0,)), ((), ())), preferred_element_type=F32)

    nn, nt = (((1,), (0,)), ((), ())), (((1,), (1,)), ((), ()))

    def body(x_ref, y_ref, xr_ref, hr_ref, hh_ref, p_ref, dga_ref, dpl_ref, dgab_ref, cw_ref, vec_ref, wa_ref, wx_ref, wp_ref,
             dproj_o, acc_o, dwa_o, dwx_o, dwp_o, g_carry, dxr_carry, q_carry, a_scr, b_scr, g_scr):
        step = pl.program_id(0)
        t = n_t - 1 - step

        @pl.when(step == 0)
        def _():
            for ref in (acc_o, dwa_o, dwx_o, dwp_o, g_carry, dxr_carry, q_carry):
                ref[...] = jnp.zeros_like(ref)

        rows = lax.broadcasted_iota(jnp.int32, (ts, d), 0)
        start = (rows == 0) & (t == 0)
        xr = xr_ref[...]
        xrb = xr.astype(BF16)
        r = jax.nn.sigmoid(block_dots(xrb, wa_ref, nn) + vec_ref[1:2, :])
        gate_i = jax.nn.sigmoid(block_dots(xrb, wx_ref, nn) + vec_ref[2:3, :])
        sp = jax.nn.softplus(vec_ref[3:4, :])
        log_a = (-C_RG * r) * sp
        a = jnp.exp(log_a)
        m_raw = jnp.sqrt(_neg_expm1(2.0 * log_a))
        mult = jnp.where(start, 1.0, m_raw)
        gelu, dgelu = _gelu_parts(y_ref[...])
        dga_t = dga_ref[...]
        hr_t = hr_ref[...]
        dy = dga_t * hr_t * dgelu

        a_scr[...] = jnp.where(rows == ts - 1, 1.0, pltpu.roll(a, ts - 1, 0))
        b_scr[...] = dga_t * gelu
        sub = lax.broadcasted_iota(jnp.int32, (8, d), 0)

        def chunk(i, g_next):
            at = pl.multiple_of((ts // 8 - 1 - i) * 8, 8)
            aa = a_scr[pl.ds(at, 8), :]
            bb = b_scr[pl.ds(at, 8), :]
            for s in (1, 2, 4):
                keep = sub < 8 - s
                bb = jnp.where(keep, bb + aa * pltpu.roll(bb, 8 - s, 0), bb)
                aa = jnp.where(keep, aa * pltpu.roll(aa, 8 - s, 0), aa)
            gg = aa * g_next + bb
            g_scr[pl.ds(at, 8), :] = gg
            return gg[0:1, :]

        g_first = lax.fori_loop(0, ts // 8, chunk, g_carry[...])
        g_carry[...] = a[0:1, :] * g_first
        g = g_scr[...]

        h_before = jnp.where(t == 0, 0.0, hh_ref[CONV_HALO - 1:CONV_HALO, :])
        h_prev = jnp.where(rows == 0, h_before, pltpu.roll(hr_t, 1, 0))
        dxr = g * gate_i * mult
        d_i = g * xr * mult
        d_mult = g * xr * gate_i
        dlog_a = (g * h_prev) * a - jnp.where(start, 0.0, d_mult * (a * a) / m_raw)
        dzr = (dlog_a * (-C_RG * sp)) * (r * (1.0 - r))
        dzi = d_i * (gate_i * (1.0 - gate_i))
        _accumulate(acc_o, 5, dzr)
        _accumulate(acc_o, 6, dzi)
        _accumulate(acc_o, 7, dlog_a * (-C_RG * r) * jax.nn.sigmoid(vec_ref[3:4, :]))
        dzrb, dzib = dzr.astype(BF16), dzi.astype(BF16)
        add_block_grads(dwa_o, xrb, dzrb)
        add_block_grads(dwx_o, xrb, dzib)
        dxr = dxr + block_dots(dzrb, wa_ref, nt) + block_dots(dzib, wx_ref, nt)

        x = x_ref[...]
        ext = jnp.concatenate([dxr, dxr_carry[...]], axis=0)
        dx = dxr * cw_ref[CONV_WIDTH - 1:CONV_WIDTH, :]
        _accumulate(acc_o, CONV_WIDTH - 1, x * dxr)
        for j in range(CONV_WIDTH - 1):
            ahead = pltpu.roll(ext, ts + CONV_HALO - (CONV_WIDTH - 1 - j), 0)[:ts]
            dx = dx + ahead * cw_ref[j:j + 1, :]
            _accumulate(acc_o, j, x * ahead)
        _accumulate(acc_o, 4, dxr)
        dxr_carry[...] = dxr[0:CONV_HALO, :]

        pb = p_ref[...]
        mixed = block_dots(pb, wp_ref, nn) + vec_ref[4:5, :]
        dpl = dpl_ref[...]
        _accumulate(acc_o, 9, dpl * mixed)
        dmixed = dpl * vec_ref[5:6, :]
        _accumulate(acc_o, 8, dmixed)
        dmb = dmixed.astype(BF16)
        add_block_grads(dwp_o, pb, dmb)
        dp = block_dots(dmb, wp_ref, nt)
        pos = (rows + t * ts + 1).astype(F32)[:, :BLOCK]
        q = jnp.concatenate([dp[:, g * BLOCK:(g + 1) * BLOCK] / jnp.minimum(pos, float(win)) for g, win in enumerate(POOL_WINDOWS)], axis=1)
        sums, have, parts = jnp.concatenate([q, q_carry[...]], axis=0), 1, []
        for g, win in enumerate(POOL_WINDOWS):
            while have < win:
                sums = sums + pltpu.roll(sums, ts + POOL_HALO - have, 0)
                have *= 2
            parts.append(sums[:ts, g * BLOCK:(g + 1) * BLOCK])
        du = jnp.concatenate(parts, axis=1) - dp
        q_carry[...] = q[0:POOL_HALO, :]

        dproj_o[:, 0:d] = dx.astype(BF16)
        dproj_o[:, d:2 * d] = dy.astype(BF16)
        dproj_o[:, 2 * d:3 * d] = du.astype(BF16)
        dproj_o[:, 3 * d:5 * d] = dgab_ref[...]

    def rev(cols, col_block=0):
        return pl.BlockSpec((ts, cols), lambda i: (n_t - 1 - i, col_block))

    halo_h = pl.BlockSpec((CONV_HALO, d), lambda i: (jnp.maximum((n_t - 1 - i) * (ts // CONV_HALO) - 1, 0), 0))
    wspec = pl.BlockSpec((N_BLOCKS, BLOCK, BLOCK), lambda i: (0, 0, 0))
    const2 = lambda rows_, cols: pl.BlockSpec((rows_, cols), lambda i: (0, 0))
    return _hosted(
        stage, body, name="seq_backward", grid=(n_t,),
        out_shape=[SDS((s_rows, 5 * d), BF16), SDS((16, d), F32)] + [SDS((N_BLOCKS, BLOCK, BLOCK), F32)] * 3,
        in_specs=[rev(d, 0), rev(d, 1), rev(d), rev(d), halo_h, rev(d), rev(d), rev(d), rev(2 * d), const2(CONV_WIDTH, d), const2(8, d),
                  wspec, wspec, wspec],
        out_specs=[rev(5 * d), const2(16, d), wspec, wspec, wspec],
        scratch_shapes=[pltpu.VMEM((1, d), F32), pltpu.VMEM((CONV_HALO, d), F32), pltpu.VMEM((POOL_HALO, d), F32),
                        pltpu.VMEM((ts, d), F32), pltpu.VMEM((ts, d), F32), pltpu.VMEM((ts, d), F32)],
        compiler_params=_params("arbitrary"),
    )(proj, proj, xr, hr, hr, p, dga, dpooled, dgab, conv_w, vecs, w_a, w_x, w_p)


def _sum_partials(parts, name):
    _, rows, cols = parts.shape
    tr = min(rows, 256)

    def body(p_ref, o_ref):
        acc = p_ref[0].astype(F32)
        for r in range(1, N_DEV):
            acc = acc + p_ref[r].astype(F32)
        o_ref[...] = acc

    return pl.pallas_call(
        body, name=name, grid=(rows // tr,), out_shape=SDS((rows, cols), F32),
        in_specs=[pl.BlockSpec((N_DEV, tr, cols), lambda i: (0, i, 0))], out_specs=pl.BlockSpec((tr, cols), lambda i: (i, 0)),
        compiler_params=_params("parallel"),
    )(parts)


def _adamw_math(w, g, m, v):
    m = ADAM_B1 * m + (1.0 - ADAM_B1) * g
    v = ADAM_B2 * v + (1.0 - ADAM_B2) * jnp.square(g)
    m_hat = m / (1.0 - ADAM_B1 ** ADAM_STEP)
    v_hat = v / (1.0 - ADAM_B2 ** ADAM_STEP)
    delta = -ADAM_LR * (m_hat / (jnp.sqrt(v_hat) + ADAM_EPS) + ADAM_WD * w)
    return delta, m, v


def _adamw(w, g, m, v, name):
    rows, cols = w.shape
    tr = min(rows, 256)

    def body(w_ref, g_ref, m_ref, v_ref, d_ref, mo_ref, vo_ref):
        d_ref[...], mo_ref[...], vo_ref[...] = _adamw_math(w_ref[...], g_ref[...], m_ref[...], v_ref[...])

    spec = pl.BlockSpec((tr, cols), lambda i: (i, 0))
    return pl.pallas_call(
        body, name=name, grid=(rows // tr,), out_shape=[SDS((rows, cols), F32)] * 3, in_specs=[spec] * 4, out_specs=[spec] * 3,
        compiler_params=_params("parallel"),
    )(w, g, m, v)


def _ada_forward(c_all, w_ada, b_cols):
    n_cols = w_ada.shape[1]

    def body(c_ref, w_ref, b_ref, act_ref, mod_ref):
        cv = c_ref[...]
        act = cv * jax.nn.sigmoid(cv)
        act_ref[...] = act
        mod_ref[...] = jnp.dot(act.astype(BF16), w_ref[...].astype(BF16), preferred_element_type=F32) + b_ref[...]

    return pl.pallas_call(
        body, name="ada_forward", out_shape=[SDS(c_all.shape, F32), SDS((N_DEV, n_cols), F32)],
        in_specs=[VMEM_SPEC] * 3, out_specs=[VMEM_SPEC] * 2, compiler_params=pltpu.CompilerParams(vmem_limit_bytes=VMEM_LIMIT_V7X),
    )(c_all, w_ada, b_cols)


def _ada_backward(act_t, dmod_cols, w, m, v, stage=None):
    rows, cols = w.shape
    tn = 512

    def body(a_ref, dm_ref, w_ref, m_ref, v_ref, g_ref, d_ref, mo_ref, vo_ref):
        act = a_ref[...].astype(BF16).astype(F32)
        dm = dm_ref[...].astype(BF16).astype(F32)
        g = act[:, 0:1] * dm[0:1, :]
        for b in range(1, N_DEV):
            g = g + act[:, b:b + 1] * dm[b:b + 1, :]
        g_ref[...] = g
        d_ref[...], mo_ref[...], vo_ref[...] = _adamw_math(w_ref[...], g, m_ref[...], v_ref[...])

    spec = pl.BlockSpec((rows, tn), lambda j: (0, j))
    return _hosted(
        stage, body, name="ada_backward", grid=(cols // tn,), out_shape=[SDS((rows, cols), F32)] * 4,
        in_specs=[pl.BlockSpec((rows, N_DEV), lambda j: (0, 0)), pl.BlockSpec((N_DEV, tn), lambda j: (0, j)), spec, spec, spec],
        out_specs=[spec] * 4, compiler_params=_params("parallel"),
    )(act_t, dmod_cols, w, m, v)


def _sum_rows(gathered, n_loss):
    n = gathered.shape[2]

    def body(g_ref, o_ref):
        acc = g_ref[0]
        for r in range(1, N_DEV):
            acc = acc + g_ref[r]
        o_ref[...] = acc
        o_ref[:, n - n_loss:n] = jnp.broadcast_to(jnp.sum(acc[:, n - n_loss:n], axis=1, keepdims=True), (1, n_loss))

    return pl.pallas_call(body, name="sum_rows", out_shape=SDS((1, n), F32), in_specs=[VMEM_SPEC], out_specs=VMEM_SPEC)(gathered)


def _adamw_row(w, g, m, v):
    def body(w_ref, g_ref, m_ref, v_ref, d_ref, mo_ref, vo_ref):
        d_ref[...], mo_ref[...], vo_ref[...] = _adamw_math(w_ref[...], g_ref[...], m_ref[...], v_ref[...])

    return pl.pallas_call(body, name="adamw_vectors", out_shape=[SDS(w.shape, F32)] * 3, in_specs=[VMEM_SPEC] * 4, out_specs=[VMEM_SPEC] * 3)(w, g, m, v)


def _blocks_to_pieces(w):
    nb, rows, n = w.shape
    q = rows // N_CHIPS
    return w.reshape(nb, N_CHIPS, q, n).transpose(1, 0, 2, 3).reshape(N_CHIPS, 2, nb // 2, q, n)


def _pieces_to_blocks(w):
    n_chips, _, half, q, n = w.shape
    return w.reshape(n_chips, 2 * half, q, n).transpose(1, 0, 2, 3).reshape(2 * half, n_chips * q, n)


def kernel(x, c, norm_mix_g, norm_mlp_g, w_ada, b_ada, w_in, conv_w, conv_b, w_rg_a, b_rg_a, w_rg_x, b_rg_x, a_param, w_branch_a, w_pool, b_pool, pool_scale, w_branch_b, w_out, w_up, w_down, final_g, loss_target, m_norm_mix_g, m_norm_mlp_g, m_w_ada, m_b_ada, m_w_in, m_conv_w, m_conv_b, m_w_rg_a, m_b_rg_a, m_w_rg_x, m_b_rg_x, m_a_param, m_w_branch_a, m_w_pool, m_b_pool, m_pool_scale, m_w_branch_b, m_w_out, m_w_up, m_w_down, m_final_g, v_norm_mix_g, v_norm_mlp_g, v_w_ada, v_b_ada, v_w_in, v_conv_w, v_conv_b, v_w_rg_a, v_b_rg_a, v_w_rg_x, v_b_rg_x, v_a_param, v_w_branch_a, v_w_pool, v_b_pool, v_pool_scale, v_w_branch_b, v_w_out, v_w_up, v_w_down, v_final_g):
    d = D_MODEL
    s_rows = x.shape[1]
    ts, tsq = min(512, s_rows), min(256, s_rows)
    xi, yi, ci = _coords()
    k_me, me = 2 * xi + yi, 4 * xi + 2 * yi + ci
    ada_cols = w_ada.shape[2]
    conv_cols = conv_w.shape[2]

    names = ("w_in", "w_up", "w_down", "w_a", "w_b", "w_out", "rg_a", "rg_x", "pool")
    mine = dict(zip(names, (w_in[0], w_up[0], w_down[0], w_branch_a[0], w_branch_b[0], w_out[0], w_rg_a[0], w_rg_x[0], w_pool[0])))
    moments_m = dict(zip(names, (m_w_in[0], m_w_up[0], m_w_down[0], m_w_branch_a[0], m_w_branch_b[0], m_w_out[0], m_w_rg_a[0], m_w_rg_x[0], m_w_pool[0])))
    moments_v = dict(zip(names, (v_w_in[0], v_w_up[0], v_w_down[0], v_w_branch_a[0], v_w_branch_b[0], v_w_out[0], v_w_rg_a[0], v_w_rg_x[0], v_w_pool[0])))
    block_weights = ("rg_a", "rg_x", "pool")
    place = jnp.stack([ci, k_me]).astype(jnp.int32)
    g_mix, g_mlp, g_fin = norm_mix_g, norm_mlp_g, final_g.reshape(1, d)
    x0, target = x[0], loss_target[0]
    mm = functools.partial(_matmul, tm=ts, out_dtype=F32)
    wg = functools.partial(_weight_grad, ts=ts, tk=min(512, d))
    plain, back, back_sum = (dict(transposed=t, reduce_blocks=r) for t, r in ((False, False), (True, False), (True, True)))

    def halves(n):
        return mine[n].astype(BF16).reshape(2, -1, mine[n].shape[-1])

    def blocks(w):
        return _pieces_to_blocks(w.reshape(N_CHIPS, 2, N_BLOCKS // 2, BLOCK // N_CHIPS, BLOCK))

    stage = _Stage()
    i_c, i_cw, i_in = _add_allgather8(stage, c), _add_allgather8(stage, conv_w[0]), _add_gather_ici(stage, halves("w_in"))
    got = _run_stage(stage, "exchange_c_conv_w_in")
    conv_full = got[i_cw][0::2].transpose(1, 0, 2).reshape(CONV_WIDTH, d)
    act_all, mod_cols = _ada_forward(got[i_c].reshape(N_DEV, d), w_ada[0], lax.dynamic_slice_in_dim(b_ada, k_me * ada_cols, ada_cols, axis=1))
    mixers = ("rg_a", "rg_x", "pool", "w_a", "w_b", "w_out")
    stage = _Stage()
    i_mod, i_in = _add_allgather8(stage, mod_cols), _add_gather_d2d(stage, got[i_in])
    i_mix = [_add_gather_ici(stage, halves(n)) for n in mixers]
    got = _run_stage(stage, "exchange_mod_w_in_mixers")
    mod_mine = lax.dynamic_index_in_dim(got[i_mod], me, axis=1, keepdims=False)[0::2]
    mod = jnp.concatenate([mod_mine.reshape(N_MOD, d), jnp.zeros((8 - N_MOD, d), F32)], axis=0)
    w_in_all = got[i_in].reshape(N_CHIPS, d, w_in.shape[2])
    vecs = jnp.concatenate([conv_b, b_rg_a, b_rg_x, a_param, b_pool, pool_scale, jnp.zeros((2, d), F32)], axis=0)

    stage = _Stage()
    i_mix = [_add_gather_d2d(stage, got[i]) for i in i_mix]
    h1, got = _norm_mod(x0, mod, g_mix, sc_row=1, sh_row=0, name="norm_mix", ts=ts, stage=stage)
    w_rg_a_all, w_rg_x_all, w_pool_all = (blocks(got[i]) for i in i_mix[:3])
    w_a_all, w_b_all, w_out_all = (got[i].reshape(1, d, d) for i in i_mix[3:])
    stage = _Stage()
    i_up = _add_gather_ici(stage, halves("w_up"))
    proj, got = mm(h1, w_in_all, name="proj_in", stage=stage, **plain)
    stage = _Stage()
    i_up, i_down = _add_gather_d2d(stage, got[i_up]), _add_gather_ici(stage, halves("w_down"))
    (xr, hr, ga, p, pooled), got = _seq_forward(proj, conv_full, vecs, w_rg_a_all, w_rg_x_all, w_pool_all, ts=tsq, stage=stage)
    w_up_all = got[i_up].reshape(N_CHIPS, d, w_up.shape[2])
    stage = _Stage()
    i_down = _add_gather_d2d(stage, got[i_down])
    br_a, got = mm(ga, w_a_all, name="branch_a", stage=stage, **plain)
    w_down_all = got[i_down].reshape(N_CHIPS, D_FF // N_CHIPS, d)
    br_b = mm(pooled, w_b_all, name="branch_b", **plain)
    merged = _merge_forward(proj, br_a, br_b, ts=ts)
    mo = mm(merged, w_out_all, name="mix_out", **plain)
    x2, h2 = _residual_norm(x0, mo, mod, g_mlp, ts=ts)
    up = mm(h2, w_up_all, name="mlp_up", **plain)
    ff = _relu_squared(up, ts=tsq)
    ffo = mm(ff, w_down_all, name="mlp_down", transposed=False, reduce_blocks=True)
    dx3, dffo, sums_head = _loss_head(x2, ffo, mod, g_fin, target, ts=ts)

    partial_of, from_sibling, chip_sum, from_chips, half_done, quarter = {}, {}, {}, {}, {}, {}

    def publish(n, g):
        g = _blocks_to_pieces(g).astype(BF16) if n in block_weights else g
        partial_of[n] = g.reshape(N_CHIPS, 2, -1, mine[n].shape[-1])

    def exchange(to_sibling=(), to_chips=(), swap=()):
        st = _Stage()
        slots = [(n, from_sibling, _add_reduce_d2d(st, partial_of[n])) for n in to_sibling]
        slots += [(n, from_chips, _add_reduce_ici(st, chip_sum[n])) for n in to_chips]
        slots += [(n, quarter, _add_swap_halves(st, half_done[n])) for n in swap]
        return st, slots

    def collect(slots, outs):
        for n, where, i in slots:
            where[n] = outs[i]

    def sum_pairs(*ns):
        for n in ns:
            chip_sum[n] = _sum_pair(partial_of[n], from_sibling[n], place, "sum_pair_" + n)

    def sum_quarters(*ns):
        for n in ns:
            half_done[n] = _sum_quarter(partial_of[n], from_sibling[n], from_chips[n], place, "sum_quarter_" + n)

    publish("w_down", wg(ff, dffo, nb=1, name="grad_w_down"))
    stage, slots = exchange(to_sibling=["w_down"])
    dff, got = mm(dffo, w_down_all, name="d_mlp_down", stage=stage, **back)
    collect(slots, got)
    sum_pairs("w_down")
    stage, slots = exchange(to_chips=["w_down"])
    dup, got = _relu_squared_backward(dff, up, ts=tsq, stage=stage)
    collect(slots, got)
    sum_quarters("w_down")
    stage, slots = exchange(swap=["w_down"])
    g_up, got = wg(h2, dup, nb=N_CHIPS, name="grad_w_up", stage=stage)
    collect(slots, got)
    publish("w_up", g_up)
    stage, slots = exchange(to_sibling=["w_up"])
    dh2, got = mm(dup, w_up_all, name="d_mlp_up", stage=stage, **back_sum)
    collect(slots, got)
    sum_pairs("w_up")
    dx2, dmo, sums_mlp = _norm_mod_backward(dh2, x2, dx3, mod, g_mlp, mo, sc_row=4, gate_row=2, name="norm_mlp_backward", ts=ts)
    publish("w_out", wg(merged, dmo, nb=1, name="grad_w_out"))
    dmerged = mm(dmo, w_out_all, name="d_mix_out", **back)
    stage, slots = exchange(to_sibling=["w_out"])
    (dbr_a, dbr_b, dgab), got = _merge_backward(dmerged, proj, br_a, br_b, ts=ts, stage=stage)
    collect(slots, got)
    publish("w_a", wg(ga, dbr_a, nb=1, name="grad_w_branch_a"))
    publish("w_b", wg(pooled, dbr_b, nb=1, name="grad_w_branch_b"))
    stage, slots = exchange(to_sibling=["w_a", "w_b"])
    dga, got = mm(dbr_a, w_a_all, name="d_branch_a", stage=stage, **back)
    collect(slots, got)
    dpooled = mm(dbr_b, w_b_all, name="d_branch_b", **back)
    sum_pairs("w_out", "w_a", "w_b")
    stage, slots = exchange(to_chips=["w_up", "w_out", "w_a", "w_b"])
    (dproj, sums_seq, d_rg_a, d_rg_x, d_pool), got = _seq_backward(proj, xr, hr, p, dga, dpooled, dgab, conv_full, vecs,
                                                                   w_rg_a_all, w_rg_x_all, w_pool_all, ts=tsq, stage=stage)
    collect(slots, got)
    sum_quarters("w_up", "w_out", "w_a", "w_b")
    for n, g in zip(block_weights, (d_rg_a, d_rg_x, d_pool)):
        publish(n, g)
    stage, slots = exchange(to_sibling=block_weights, swap=["w_up", "w_out", "w_a", "w_b"])
    dh1, got = mm(dproj, w_in_all, name="d_proj_in", stage=stage, **back_sum)
    collect(slots, got)
    sum_pairs(*block_weights)
    stage, slots = exchange(to_chips=block_weights)
    g_in, got = wg(h1, dproj, nb=N_CHIPS, name="grad_w_in", stage=stage)
    collect(slots, got)
    publish("w_in", g_in)
    sum_quarters(*block_weights)
    stage, slots = exchange(to_sibling=["w_in"], swap=block_weights)
    (grad_x, sums_mix), got = _norm_mod_backward(dh1, x0, dx2, mod, g_mix, None, sc_row=1, gate_row=None, name="norm_mix_backward",
                                                 ts=ts, stage=stage)
    collect(slots, got)
    sum_pairs("w_in")

    dmod = jnp.concatenate([sums_mix[1:2], sums_mix[0:1], sums_mlp[3:4], sums_mlp[1:2], sums_mlp[0:1], sums_head[1:2]], axis=1)
    row = jnp.concatenate([sums_mix[2:3], sums_mlp[2:3], sums_seq[4:5], sums_seq[5:6], sums_seq[6:7], sums_seq[7:8], sums_seq[8:9],
                           sums_seq[9:10], sums_head[0:1], sums_seq[0:CONV_WIDTH].reshape(1, CONV_WIDTH * d), dmod, sums_head[2:3]], axis=1)
    n_vec, n_conv, n_mod = 9 * d, CONV_WIDTH * d, N_MOD * d
    stage, slots = exchange(to_chips=["w_in"])
    i_rows = _add_allgather8(stage, row)
    got = _run_stage(stage, "exchange_sums_w_in")
    collect(slots, got)
    rows_all = got[i_rows]
    sum_quarters("w_in")
    total = _sum_rows(rows_all, d)
    loss = total[0, n_vec + n_conv + n_mod]
    dmod_all = rows_all[:, 0, n_vec + n_conv:n_vec + n_conv + n_mod]
    stage, slots = exchange(swap=["w_in"])
    (g_ada, d_ada, m_ada, v_ada), got = _ada_backward(act_all.T, lax.dynamic_slice_in_dim(dmod_all, k_me * ada_cols, ada_cols, axis=1),
                                                      w_ada[0], m_w_ada[0], v_w_ada[0], stage=stage)
    collect(slots, got)
    g_conv = lax.dynamic_slice_in_dim(total[:, n_vec:n_vec + n_conv].reshape(CONV_WIDTH, d), k_me * conv_cols, conv_cols, axis=1)
    g_row = jnp.concatenate([total[:, 0:n_vec], g_conv.reshape(1, CONV_WIDTH * conv_cols), total[:, n_vec + n_conv:n_vec + n_conv + n_mod]], axis=1)

    def pack(nm, nl, fg, cw, ba):
        return jnp.concatenate([nm, nl, cw[1], cw[2], cw[3], cw[4], cw[5], cw[6], fg.reshape(1, d),
                                cw[0][0].reshape(1, CONV_WIDTH * conv_cols), ba], axis=1)

    w_row = pack(norm_mix_g, norm_mlp_g, final_g, (conv_w, conv_b, b_rg_a, b_rg_x, a_param, b_pool, pool_scale), b_ada)
    m_row = pack(m_norm_mix_g, m_norm_mlp_g, m_final_g, (m_conv_w, m_conv_b, m_b_rg_a, m_b_rg_x, m_a_param, m_b_pool, m_pool_scale), m_b_ada)
    v_row = pack(v_norm_mix_g, v_norm_mlp_g, v_final_g, (v_conv_w, v_conv_b, v_b_rg_a, v_b_rg_x, v_a_param, v_b_pool, v_pool_scale), v_b_ada)
    vec_out = (g_row,) + tuple(_adamw_row(w_row, g_row, m_row, v_row))

    def unpack(r):
        out = {nme: r[:, i * d:(i + 1) * d] for i, nme in enumerate(
            ("norm_mix_g", "norm_mlp_g", "conv_b", "b_rg_a", "b_rg_x", "a_param", "b_pool", "pool_scale"))}
        out["final_g"] = r[0, 8 * d:9 * d]
        out["conv_w"] = r[:, 9 * d:9 * d + CONV_WIDTH * conv_cols].reshape(1, CONV_WIDTH, conv_cols)
        out["b_ada"] = r[:, 9 * d + CONV_WIDTH * conv_cols:]
        return out

    big_out = {}
    for n in names:
        shape2 = (-1, mine[n].shape[-1])
        g2 = quarter[n].reshape(shape2)
        res = (g2,) + tuple(_adamw(mine[n].reshape(shape2), g2, moments_m[n].reshape(shape2), moments_v[n].reshape(shape2), "adamw_" + n))
        big_out[n] = [r.reshape((1,) + mine[n].shape) for r in res]

    key = {"w_in": "w_in", "w_rg_a": "rg_a", "w_rg_x": "rg_x", "w_branch_a": "w_a", "w_pool": "pool", "w_branch_b": "w_b", "w_out": "w_out",
           "w_up": "w_up", "w_down": "w_down"}
    order = ("norm_mix_g", "norm_mlp_g", "w_ada", "b_ada", "w_in", "conv_w", "conv_b", "w_rg_a", "b_rg_a", "w_rg_x", "b_rg_x", "a_param",
             "w_branch_a", "w_pool", "b_pool", "pool_scale", "w_branch_b", "w_out", "w_up", "w_down", "final_g")
    ada_out = [g_ada[None], d_ada[None], m_ada[None], v_ada[None]]
    outs = [loss, grad_x[None]]
    for kind in range(4):
        vec = unpack(vec_out[kind])
        for nme in order:
            outs.append(ada_out[kind] if nme == "w_ada" else big_out[key[nme]][kind] if nme in key else vec[nme])
    return tuple(outs)
```

```python
import functools

import jax
import jax.numpy as jnp
from jax import lax
from jax.experimental import pallas as pl
from jax.experimental.pallas import tpu as pltpu

F32, BF16 = jnp.float32, jnp.bfloat16
D_MODEL = 1024
D_FF = 4 * D_MODEL
N_BLOCKS = 4
BLOCK = D_MODEL // N_BLOCKS
CONV_WIDTH = 4
POOL_WINDOWS = (2, 4, 8, 16)
CONV_HALO = 8
POOL_HALO = 16
N_MOD = 6
EPS = 1e-6
C_RG = 8.0
ADAM_LR, ADAM_B1, ADAM_B2, ADAM_EPS, ADAM_WD, ADAM_STEP = 0.001, 0.9, 0.999, 1e-08, 0.01, 10
N_DEV = 8
N_CHIPS = 4
VMEM_LIMIT_V7X = 56 * 2**20
MESH = pl.DeviceIdType.MESH
SDS = jax.ShapeDtypeStruct
HBM_SPEC = pl.BlockSpec(memory_space=pltpu.HBM)
VMEM_SPEC = pl.BlockSpec(memory_space=pltpu.VMEM)


def _params(*semantics):
    return pltpu.CompilerParams(dimension_semantics=semantics, vmem_limit_bytes=VMEM_LIMIT_V7X)


def _coords():
    return lax.axis_index("x"), lax.axis_index("y"), lax.axis_index("c")


def _flip(v, bit):
    return 1 - v if bit else v


def _chips():
    x, y, _ = _coords()
    return [(1 - x, y), (x, 1 - y), (1 - x, 1 - y)]


class _Stage:
    def __init__(self):
        self.inputs, self.in_specs, self.outputs, self.out_specs, self.aliases = [], [], [], [], {}
        self.parts, self.n_copies = [], 0

    def add(self, inputs, in_spec, outputs, out_spec, n_copies, build, alias=False):
        i0, o0 = len(self.inputs), len(self.outputs)
        self.inputs += inputs
        self.in_specs += [in_spec] * len(inputs)
        self.outputs += outputs
        self.out_specs += [out_spec] * len(outputs)
        if alias:
            self.aliases.update({i0 + i: o0 + i for i in range(len(inputs))})
        self.parts.append((build, i0, len(inputs), o0, len(outputs)))
        self.n_copies += n_copies
        return list(range(o0, o0 + len(outputs)))

    def copies(self, in_refs, out_refs):
        out = []
        for build, i0, ni, o0, no in self.parts:
            out += build(in_refs[i0:i0 + ni], out_refs[o0:o0 + no])
        assert len(out) == self.n_copies
        return out

    def run(self, in_refs, out_refs, send_sems, recv_sems, start):
        x, y, c = _coords()
        for s, (src, dst, to, landing) in enumerate(self.copies(in_refs, out_refs)):
            if to is None:
                cp = pltpu.make_async_copy(src, dst, send_sems.at[s])
                cp.start() if start else cp.wait()
                continue
            cp = pltpu.make_async_remote_copy(src_ref=src, dst_ref=dst, send_sem=send_sems.at[s], recv_sem=recv_sems.at[s],
                                              device_id=to, device_id_type=MESH)
            if start:
                cp.start()
            else:
                pltpu.make_async_remote_copy(src_ref=landing, dst_ref=landing, send_sem=send_sems.at[s], recv_sem=recv_sems.at[s],
                                             device_id=(x, y, c), device_id_type=MESH).wait_recv()
                cp.wait_send()


def _hosted(stage, body, *, name, in_specs, out_specs, out_shape, grid=(), scratch_shapes=(), compiler_params=None):
    if stage is None:
        return pl.pallas_call(body, name=name, grid=grid, in_specs=in_specs, out_specs=out_specs, out_shape=out_shape,
                              scratch_shapes=list(scratch_shapes), compiler_params=compiler_params)
    single = not isinstance(out_shape, (list, tuple))
    h_out_shape = [out_shape] if single else list(out_shape)
    h_out_specs = [out_specs] if single else list(out_specs)
    n_in, n_out, n_scr = len(in_specs), len(h_out_shape), len(scratch_shapes)
    s_in, s_out = len(stage.inputs), len(stage.outputs)

    def wrapped(*refs):
        h_in, st_in = refs[:n_in], refs[n_in:n_in + s_in]
        h_o, st_o = refs[n_in + s_in:n_in + s_in + n_out], refs[n_in + s_in + n_out:n_in + s_in + n_out + s_out]
        h_scr = refs[n_in + s_in + n_out + s_out:n_in + s_in + n_out + s_out + n_scr]
        send_sems, recv_sems = refs[n_in + s_in + n_out + s_out + n_scr:]
        if not grid:
            stage.run(st_in, st_o, send_sems, recv_sems, True)
            if body is not None:
                body(*h_in, *h_o, *h_scr)
            stage.run(st_in, st_o, send_sems, recv_sems, False)
            return
        ids = [pl.program_id(a) for a in range(len(grid))]
        first = functools.reduce(jnp.logical_and, [i == 0 for i in ids])
        last = functools.reduce(jnp.logical_and, [i == g - 1 for i, g in zip(ids, grid)])
        pl.when(first)(lambda: stage.run(st_in, st_o, send_sems, recv_sems, True))
        body(*h_in, *h_o, *h_scr)
        pl.when(last)(lambda: stage.run(st_in, st_o, send_sems, recv_sems, False))

    call = pl.pallas_call(
        wrapped, name=name, grid=grid, in_specs=list(in_specs) + stage.in_specs, out_specs=h_out_specs + stage.out_specs,
        out_shape=h_out_shape + stage.outputs, input_output_aliases={n_in + i: n_out + o for i, o in stage.aliases.items()},
        scratch_shapes=list(scratch_shapes) + [pltpu.SemaphoreType.DMA((stage.n_copies,)), pltpu.SemaphoreType.DMA((stage.n_copies,))],
        compiler_params=pltpu.CompilerParams(dimension_semantics=("arbitrary",) * len(grid), vmem_limit_bytes=VMEM_LIMIT_V7X),
    )

    def run(*operands):
        outs = call(*operands, *stage.inputs)
        host = outs[:n_out]
        return (host[0] if single else host), outs[n_out:]

    return run


def _run_stage(stage, name):
    return _hosted(stage, None, name=name, in_specs=[], out_specs=[], out_shape=[])()[1]


def _add_allgather8(stage, v):
    def build(ins, outs):
        x, y, c = _coords()
        me = 4 * x + 2 * y + c
        copies = [(ins[0], outs[0].at[me], None, None)]
        for r in range(1, N_DEV):
            peer = (_flip(x, r & 4), _flip(y, r & 2), _flip(c, r & 1))
            copies.append((ins[0], outs[0].at[me], peer, outs[0].at[me ^ r]))
        return copies

    return stage.add([v], VMEM_SPEC, [SDS((N_DEV,) + v.shape, v.dtype)], VMEM_SPEC, N_DEV, build)[0]


def _add_gather_ici(stage, shard):
    def build(ins, outs):
        x, y, c = _coords()
        k_me = 2 * x + y
        copies = [(ins[0], outs[0].at[k_me], (x, y, 1 - c), outs[0].at[k_me])]
        for px, py in _chips():
            copies.append((ins[0].at[c], outs[0].at[k_me, c], (px, py, c), outs[0].at[2 * px + py, c]))
        return copies

    return stage.add([shard], HBM_SPEC, [SDS((N_CHIPS,) + shard.shape, shard.dtype)], HBM_SPEC, N_CHIPS, build)[0]


def _add_gather_d2d(stage, whole):
    def build(ins, outs):
        x, y, c = _coords()
        return [(outs[0].at[2 * px + py, c], outs[0].at[2 * px + py, c], (x, y, 1 - c), outs[0].at[2 * px + py, 1 - c]) for px, py in _chips()]

    return stage.add([whole], HBM_SPEC, [SDS(whole.shape, whole.dtype)], HBM_SPEC, N_CHIPS - 1, build, alias=True)[0]


def _add_reduce_d2d(stage, grads):
    def build(ins, outs):
        x, y, c = _coords()
        return [(ins[0].at[k, 1 - c], outs[0].at[k], (x, y, 1 - c), outs[0].at[k]) for k in range(N_CHIPS)]

    return stage.add([grads], HBM_SPEC, [SDS((N_CHIPS,) + grads.shape[2:], grads.dtype)], HBM_SPEC, N_CHIPS, build)[0]


def _add_reduce_ici(stage, partial):
    def build(ins, outs):
        x, y, c = _coords()
        return [(ins[0].at[2 * px + py], outs[0].at[j], (px, py, c), outs[0].at[j]) for j, (px, py) in enumerate(_chips())]

    return stage.add([partial], HBM_SPEC, [SDS((N_CHIPS - 1,) + partial.shape[1:], partial.dtype)], HBM_SPEC, N_CHIPS - 1, build)[0]


def _add_swap_halves(stage, quarter):
    def build(ins, outs):
        x, y, c = _coords()
        return [(outs[0].at[c], outs[0].at[c], (x, y, 1 - c), outs[0].at[1 - c])]

    return stage.add([quarter], HBM_SPEC, [SDS(quarter.shape, quarter.dtype)], HBM_SPEC, 1, build, alias=True)[0]


def _sum_pair(grads, from_sibling, place, name):
    _, _, rows, cols = grads.shape
    tr = min(rows, 256)

    def body(place_ref, g_ref, s_ref, o_ref):
        o_ref[0] = (g_ref[0, 0].astype(F32) + s_ref[0].astype(F32)).astype(BF16)

    spec = pl.BlockSpec((1, tr, cols), lambda k, i, place_ref: (k, i, 0))
    return pl.pallas_call(
        body, name=name, out_shape=SDS((N_CHIPS, rows, cols), BF16),
        grid_spec=pltpu.PrefetchScalarGridSpec(
            num_scalar_prefetch=1, grid=(N_CHIPS, rows // tr),
            in_specs=[pl.BlockSpec((1, 1, tr, cols), lambda k, i, place_ref: (k, place_ref[0], i, 0)), spec], out_specs=spec),
        compiler_params=_params("parallel", "parallel"),
    )(place, grads, from_sibling)


def _sum_quarter(grads, from_sibling, from_chips, place, name):
    _, _, rows, cols = grads.shape
    tr = min(rows, 256)

    def body(place_ref, g_ref, s_ref, c_ref, o_ref):
        acc = g_ref[0, 0].astype(F32) + s_ref[0].astype(F32)
        for j in range(N_CHIPS - 1):
            acc = acc + c_ref[j].astype(F32)
        o_ref[0] = acc

    return pl.pallas_call(
        body, name=name, out_shape=SDS((2, rows, cols), F32),
        grid_spec=pltpu.PrefetchScalarGridSpec(
            num_scalar_prefetch=1, grid=(rows // tr,),
            in_specs=[pl.BlockSpec((1, 1, tr, cols), lambda i, place_ref: (place_ref[1], place_ref[0], i, 0)),
                      pl.BlockSpec((1, tr, cols), lambda i, place_ref: (place_ref[1], i, 0)),
                      pl.BlockSpec((N_CHIPS - 1, tr, cols), lambda i, place_ref: (0, i, 0))],
            out_specs=pl.BlockSpec((1, tr, cols), lambda i, place_ref: (place_ref[0], i, 0))),
        compiler_params=_params("parallel"),
    )(place, grads, from_sibling, from_chips)


def _row_spec(ts, cols, col_block=0):
    return pl.BlockSpec((ts, cols), lambda t: (t, col_block))


def _vec_spec(rows, cols):
    return pl.BlockSpec((rows, cols), lambda t: (0, 0))


def _matmul(a, w, *, transposed, reduce_blocks, name, tm, outs, finish, extra=(), prepare=None, stage=None):
    m_rows = a.shape[0]
    nb, r, c = w.shape
    kb = c if transposed else r
    dims = (((1,), (1,)), ((), ())) if transposed else (((1,), (0,)), ((), ()))
    assert a.shape[1] == (nb * kb if reduce_blocks else kb) and m_rows % tm == 0
    n_extra = len(extra)

    def body(a_ref, w_ref, *rest):
        extra_refs, out_refs = rest[:n_extra], rest[n_extra:]
        if reduce_blocks:
            acc = lax.dot_general(a_ref[:, 0:kb], w_ref[0], dims, preferred_element_type=F32)
            for k in range(1, nb):
                acc += lax.dot_general(a_ref[:, k * kb:(k + 1) * kb], w_ref[k], dims, preferred_element_type=F32)
            finish(None, acc, extra_refs, out_refs)
        else:
            lhs = a_ref[...] if prepare is None else prepare(a_ref, extra_refs, out_refs)
            for k in range(nb):
                finish(k, lax.dot_general(lhs, w_ref[k], dims, preferred_element_type=F32), extra_refs, out_refs)

    return _hosted(
        stage, body, name=name, grid=(m_rows // tm,), out_shape=[s for s, _ in outs], out_specs=[s for _, s in outs],
        in_specs=[_row_spec(tm, a.shape[1]), pl.BlockSpec((nb, r, c), lambda m: (0, 0, 0))] + [s for _, s in extra],
        compiler_params=_params("arbitrary"),
    )(a, w, *[e for e, _ in extra])


def _store_blocks(n):
    def finish(k, acc, extra_refs, out_refs):
        if k is None:
            out_refs[0][...] = acc.astype(out_refs[0].dtype)
        else:
            out_refs[0][:, k * n:(k + 1) * n] = acc.astype(out_refs[0].dtype)
    return finish


def _weight_grad(a, b, *, nb, name, tk, stage=None):
    s_rows, k1 = a.shape
    tn = b.shape[1] // nb
    assert k1 % tk == 0 and b.shape[1] % nb == 0

    def body(a_ref, b_ref, o_ref):
        o_ref[0] = lax.dot_general(a_ref[...], b_ref[...], (((0,), (0,)), ((), ())), preferred_element_type=F32).astype(o_ref.dtype)

    return _hosted(
        stage, body, name=name, grid=(nb, k1 // tk), out_shape=SDS((nb, k1, tn), BF16),
        in_specs=[pl.BlockSpec((s_rows, tk), lambda n, i: (0, i)), pl.BlockSpec((s_rows, tn), lambda n, i: (0, n))],
        out_specs=pl.BlockSpec((1, tk, tn), lambda n, i: (n, i, 0)), compiler_params=_params("parallel", "parallel"),
    )(a, b)


def _rms(x):
    r = lax.rsqrt(jnp.mean(x * x, axis=-1, keepdims=True) + EPS)
    return x * r, r


def _accumulate(acc_ref, row, value):
    acc_ref[row:row + 1, :] += jnp.sum(value, axis=0, keepdims=True)


def _zero_at_first_step(acc_ref):
    @pl.when(pl.program_id(0) == 0)
    def _():
        acc_ref[...] = jnp.zeros_like(acc_ref)


def _gelu_parts(y):
    k0, k1 = 0.7978845608028654, 0.044715
    th = jnp.tanh(k0 * (y + k1 * (y * y * y)))
    gelu = 0.5 * y * (1.0 + th)
    dgelu = 0.5 * (1.0 + th) + 0.5 * y * (1.0 - th * th) * (k0 * (1.0 + 3.0 * k1 * (y * y)))
    return gelu, dgelu


def _neg_expm1(z):
    series = z * (1.0 + z * (0.5 + z * (1.0 / 6 + z * (1.0 / 24 + z * (1.0 / 120 + z * (1.0 / 720))))))
    return -jnp.where(z > -0.25, series, jnp.exp(z) - 1.0)


def _modulated_norm(x, mod_ref, g_ref, sc_row, sh_row):
    xh, _ = _rms(x)
    return ((xh * g_ref[...]) * (1.0 + mod_ref[sc_row:sc_row + 1, :]) + mod_ref[sh_row:sh_row + 1, :]).astype(BF16)


def _modulated_norm_backward(dh, x, dres, mod_ref, g_ref, acc_ref, sc_row):
    xh, r = _rms(x)
    _accumulate(acc_ref, 0, dh * (xh * g_ref[...]))
    _accumulate(acc_ref, 1, dh)
    dn = dh * (1.0 + mod_ref[sc_row:sc_row + 1, :])
    _accumulate(acc_ref, 2, dn * xh)
    dxh = dn * g_ref[...]
    return dres + r * (dxh - xh * jnp.mean(dxh * xh, axis=-1, keepdims=True))


def _seq_forward(proj, conv_w, vecs, w_a, w_x, w_p, *, ts, stage=None):
    s_rows = proj.shape[0]
    d = D_MODEL

    def body(x_ref, xh_ref, y_ref, u_ref, uh_ref, cw_ref, vec_ref, wa_ref, wx_ref, wp_ref,
             xr_o, hr_o, ga_o, p_o, pooled_o, carry, a_scr, b_scr):
        t = pl.program_id(0)

        @pl.when(t == 0)
        def _():
            carry[...] = jnp.zeros_like(carry)

        x = x_ref[...]
        halo = jnp.where(t == 0, 0.0, xh_ref[...])
        xx = jnp.concatenate([halo, x], axis=0)
        xr = vec_ref[0:1, :] + x * cw_ref[CONV_WIDTH - 1:CONV_WIDTH, :]
        for j in range(CONV_WIDTH - 1):
            xr = xr + pltpu.roll(xx, CONV_WIDTH - 1 - j, 0)[CONV_HALO:] * cw_ref[j:j + 1, :]
        xr_o[...] = xr
        xrb = xr.astype(BF16)
        zr = jnp.concatenate([jnp.dot(xrb[:, g * BLOCK:(g + 1) * BLOCK], wa_ref[g], preferred_element_type=F32)
                              for g in range(N_BLOCKS)], axis=1) + vec_ref[1:2, :]
        zi = jnp.concatenate([jnp.dot(xrb[:, g * BLOCK:(g + 1) * BLOCK], wx_ref[g], preferred_element_type=F32)
                              for g in range(N_BLOCKS)], axis=1) + vec_ref[2:3, :]
        r = jax.nn.sigmoid(zr)
        gate_i = jax.nn.sigmoid(zi)
        log_a = (-C_RG * r) * jax.nn.softplus(vec_ref[3:4, :])
        rows = lax.broadcasted_iota(jnp.int32, (ts, d), 0)
        mult = jnp.where((rows == 0) & (t == 0), 1.0, jnp.sqrt(_neg_expm1(2.0 * log_a)))
        a_scr[...] = jnp.exp(log_a)
        b_scr[...] = xr * gate_i * mult

        sub = lax.broadcasted_iota(jnp.int32, (8, d), 0)

        def chunk(i, h):
            a = a_scr[pl.ds(i * 8, 8), :]
            b = b_scr[pl.ds(i * 8, 8), :]
            for s in (1, 2, 4):
                keep = sub >= s
                b = jnp.where(keep, a * pltpu.roll(b, s, 0) + b, b)
                a = jnp.where(keep, a * pltpu.roll(a, s, 0), a)
            hh = a * h + b
            hr_o[pl.ds(i * 8, 8), :] = hh
            return hh[7:8, :]

        carry[...] = lax.fori_loop(0, ts // 8, chunk, carry[...])
        gelu, _ = _gelu_parts(y_ref[...])
        ga_o[...] = (gelu * hr_o[...]).astype(BF16)

        u = u_ref[...]
        uu = jnp.concatenate([jnp.where(t == 0, 0.0, uh_ref[...]), u], axis=0)
        pos = (rows + t * ts + 1).astype(F32)
        sums, have, parts = uu, 1, []
        for g, win in enumerate(POOL_WINDOWS):
            while have < win:
                sums = sums + pltpu.roll(sums, have, 0)
                have *= 2
            mean = sums[POOL_HALO:, g * BLOCK:(g + 1) * BLOCK] / jnp.minimum(pos[:, :BLOCK], float(win))
            parts.append(mean - u[:, g * BLOCK:(g + 1) * BLOCK])
        pb = jnp.concatenate(parts, axis=1).astype(BF16)
        p_o[...] = pb
        mixed = jnp.concatenate([jnp.dot(pb[:, g * BLOCK:(g + 1) * BLOCK], wp_ref[g], preferred_element_type=F32)
                                 for g in range(N_BLOCKS)], axis=1) + vec_ref[4:5, :]
        pooled_o[...] = (mixed * vec_ref[5:6, :]).astype(BF16)

    halo_x = pl.BlockSpec((CONV_HALO, d), lambda t: (jnp.maximum(t * (ts // CONV_HALO) - 1, 0), 0))
    halo_u = pl.BlockSpec((POOL_HALO, d), lambda t: (jnp.maximum(t * (ts // POOL_HALO) - 1, 0), 2))
    wspec = pl.BlockSpec((N_BLOCKS, BLOCK, BLOCK), lambda t: (0, 0, 0))
    return _hosted(
        stage, body, name="seq_forward", grid=(s_rows // ts,),
        out_shape=[SDS((s_rows, d), F32), SDS((s_rows, d), F32), SDS((s_rows, d), BF16), SDS((s_rows, d), BF16), SDS((s_rows, d), BF16)],
        in_specs=[_row_spec(ts, d, 0), halo_x, _row_spec(ts, d, 1), _row_spec(ts, d, 2), halo_u, _vec_spec(CONV_WIDTH, d), _vec_spec(8, d),
                  wspec, wspec, wspec],
        out_specs=[_row_spec(ts, d)] * 5,
        scratch_shapes=[pltpu.VMEM((1, d), F32), pltpu.VMEM((ts, d), F32), pltpu.VMEM((ts, d), F32)],
        compiler_params=_params("arbitrary"),
    )(proj, proj, proj, proj, proj, conv_w, vecs, w_a, w_x, w_p)


def _merge_forward(proj, br_a, br_b, *, ts):
    s_rows, d = br_a.shape

    def body(ga_ref, gb_ref, a_ref, b_ref, o_ref):
        o_ref[...] = (jax.nn.sigmoid(ga_ref[...]) * a_ref[...] + jax.nn.sigmoid(gb_ref[...]) * b_ref[...]).astype(BF16)

    return pl.pallas_call(
        body, name="merge_forward", grid=(s_rows // ts,), out_shape=SDS((s_rows, d), BF16),
        in_specs=[_row_spec(ts, d, 3), _row_spec(ts, d, 4), _row_spec(ts, d), _row_spec(ts, d)], out_specs=_row_spec(ts, d),
        compiler_params=_params("parallel"),
    )(proj, proj, br_a, br_b)


def _seq_backward(proj, xr, hr, p, dga, dpooled, dgab, conv_w, vecs, w_a, w_x, w_p, *, ts, stage=None):
    s_rows = proj.shape[0]
    d = D_MODEL
    n_t = s_rows // ts

    def block_dots(lhs, w_ref, dims):
        return jnp.concatenate([lax.dot_general(lhs[:, g * BLOCK:(g + 1) * BLOCK], w_ref[g], dims, preferred_element_type=F32)
                                for g in range(N_BLOCKS)], axis=1)

    def add_block_grads(dw_ref, lhs, rhs):
        for g in range(N_BLOCKS):
            dw_ref[g] += lax.dot_general(lhs[:, g * BLOCK:(g + 1) * BLOCK], rhs[:, g * BLOCK:(g + 1) * BLOCK],
                                         (((0,), (0,)), ((), ())), preferred_element_type=F32)

    nn, nt = (((1,), (0,)), ((), ())), (((1,), (1,)), ((), ()))

    def body(x_ref, y_ref, xr_ref, hr_ref, hh_ref, p_ref, dga_ref, dpl_ref, dgab_ref, cw_ref, vec_ref, wa_ref, wx_ref, wp_ref,
             dproj_o, acc_o, dwa_o, dwx_o, dwp_o, g_carry, dxr_carry, q_carry, a_scr, b_scr, g_scr):
        step = pl.program_id(0)
        t = n_t - 1 - step

        @pl.when(step == 0)
        def _():
            for ref in (acc_o, dwa_o, dwx_o, dwp_o, g_carry, dxr_carry, q_carry):
                ref[...] = jnp.zeros_like(ref)

        rows = lax.broadcasted_iota(jnp.int32, (ts, d), 0)
        start = (rows == 0) & (t == 0)
        xr = xr_ref[...]
        xrb = xr.astype(BF16)
        r = jax.nn.sigmoid(block_dots(xrb, wa_ref, nn) + vec_ref[1:2, :])
        gate_i = jax.nn.sigmoid(block_dots(xrb, wx_ref, nn) + vec_ref[2:3, :])
        sp = jax.nn.softplus(vec_ref[3:4, :])
        log_a = (-C_RG * r) * sp
        a = jnp.exp(log_a)
        m_raw = jnp.sqrt(_neg_expm1(2.0 * log_a))
        mult = jnp.where(start, 1.0, m_raw)
        gelu, dgelu = _gelu_parts(y_ref[...])
        dga_t = dga_ref[...]
        hr_t = hr_ref[...]
        dy = dga_t * hr_t * dgelu

        a_scr[...] = jnp.where(rows == ts - 1, 1.0, pltpu.roll(a, ts - 1, 0))
        b_scr[...] = dga_t * gelu
        sub = lax.broadcasted_iota(jnp.int32, (8, d), 0)

        def chunk(i, g_next):
            at = pl.multiple_of((ts // 8 - 1 - i) * 8, 8)
            aa = a_scr[pl.ds(at, 8), :]
            bb = b_scr[pl.ds(at, 8), :]
            for s in (1, 2, 4):
                keep = sub < 8 - s
                bb = jnp.where(keep, bb + aa * pltpu.roll(bb, 8 - s, 0), bb)
                aa = jnp.where(keep, aa * pltpu.roll(aa, 8 - s, 0), aa)
            gg = aa * g_next + bb
            g_scr[pl.ds(at, 8), :] = gg
            return gg[0:1, :]

        g_first = lax.fori_loop(0, ts // 8, chunk, g_carry[...])
        g_carry[...] = a[0:1, :] * g_first
        g = g_scr[...]

        h_before = jnp.where(t == 0, 0.0, hh_ref[CONV_HALO - 1:CONV_HALO, :])
        h_prev = jnp.where(rows == 0, h_before, pltpu.roll(hr_t, 1, 0))
        dxr = g * gate_i * mult
        d_i = g * xr * mult
        d_mult = g * xr * gate_i
        dlog_a = (g * h_prev) * a - jnp.where(start, 0.0, d_mult * (a * a) / m_raw)
        dzr = (dlog_a * (-C_RG * sp)) * (r * (1.0 - r))
        dzi = d_i * (gate_i * (1.0 - gate_i))
        _accumulate(acc_o, 5, dzr)
        _accumulate(acc_o, 6, dzi)
        _accumulate(acc_o, 7, dlog_a * (-C_RG * r) * jax.nn.sigmoid(vec_ref[3:4, :]))
        dzrb, dzib = dzr.astype(BF16), dzi.astype(BF16)
        add_block_grads(dwa_o, xrb, dzrb)
        add_block_grads(dwx_o, xrb, dzib)
        dxr = dxr + block_dots(dzrb, wa_ref, nt) + block_dots(dzib, wx_ref, nt)

        x = x_ref[...]
        ext = jnp.concatenate([dxr, dxr_carry[...]], axis=0)
        dx = dxr * cw_ref[CONV_WIDTH - 1:CONV_WIDTH, :]
        _accumulate(acc_o, CONV_WIDTH - 1, x * dxr)
        for j in range(CONV_WIDTH - 1):
            ahead = pltpu.roll(ext, ts + CONV_HALO - (CONV_WIDTH - 1 - j), 0)[:ts]
            dx = dx + ahead * cw_ref[j:j + 1, :]
            _accumulate(acc_o, j, x * ahead)
        _accumulate(acc_o, 4, dxr)
        dxr_carry[...] = dxr[0:CONV_HALO, :]

        pb = p_ref[...]
        mixed = block_dots(pb, wp_ref, nn) + vec_ref[4:5, :]
        dpl = dpl_ref[...]
        _accumulate(acc_o, 9, dpl * mixed)
        dmixed = dpl * vec_ref[5:6, :]
        _accumulate(acc_o, 8, dmixed)
        dmb = dmixed.astype(BF16)
        add_block_grads(dwp_o, pb, dmb)
        dp = block_dots(dmb, wp_ref, nt)
        pos = (rows + t * ts + 1).astype(F32)[:, :BLOCK]
        q = jnp.concatenate([dp[:, g * BLOCK:(g + 1) * BLOCK] / jnp.minimum(pos, float(win)) for g, win in enumerate(POOL_WINDOWS)], axis=1)
        sums, have, parts = jnp.concatenate([q, q_carry[...]], axis=0), 1, []
        for g, win in enumerate(POOL_WINDOWS):
            while have < win:
                sums = sums + pltpu.roll(sums, ts + POOL_HALO - have, 0)
                have *= 2
            parts.append(sums[:ts, g * BLOCK:(g + 1) * BLOCK])
        du = jnp.concatenate(parts, axis=1) - dp
        q_carry[...] = q[0:POOL_HALO, :]

        dproj_o[:, 0:d] = dx.astype(BF16)
        dproj_o[:, d:2 * d] = dy.astype(BF16)
        dproj_o[:, 2 * d:3 * d] = du.astype(BF16)
        dproj_o[:, 3 * d:5 * d] = dgab_ref[...]

    def rev(cols, col_block=0):
        return pl.BlockSpec((ts, cols), lambda i: (n_t - 1 - i, col_block))

    halo_h = pl.BlockSpec((CONV_HALO, d), lambda i: (jnp.maximum((n_t - 1 - i) * (ts // CONV_HALO) - 1, 0), 0))
    wspec = pl.BlockSpec((N_BLOCKS, BLOCK, BLOCK), lambda i: (0, 0, 0))
    return _hosted(
        stage, body, name="seq_backward", grid=(n_t,),
        out_shape=[SDS((s_rows, 5 * d), BF16), SDS((16, d), F32)] + [SDS((N_BLOCKS, BLOCK, BLOCK), F32)] * 3,
        in_specs=[rev(d, 0), rev(d, 1), rev(d), rev(d), halo_h, rev(d), rev(d), rev(d), rev(2 * d), _vec_spec(CONV_WIDTH, d), _vec_spec(8, d),
                  wspec, wspec, wspec],
        out_specs=[rev(5 * d), _vec_spec(16, d), wspec, wspec, wspec],
        scratch_shapes=[pltpu.VMEM((1, d), F32), pltpu.VMEM((CONV_HALO, d), F32), pltpu.VMEM((POOL_HALO, d), F32),
                        pltpu.VMEM((ts, d), F32), pltpu.VMEM((ts, d), F32), pltpu.VMEM((ts, d), F32)],
        compiler_params=_params("arbitrary"),
    )(proj, proj, xr, hr, hr, p, dga, dpooled, dgab, conv_w, vecs, w_a, w_x, w_p)


def _adamw_math(w, g, m, v):
    m = ADAM_B1 * m + (1.0 - ADAM_B1) * g
    v = ADAM_B2 * v + (1.0 - ADAM_B2) * jnp.square(g)
    m_hat = m / (1.0 - ADAM_B1 ** ADAM_STEP)
    v_hat = v / (1.0 - ADAM_B2 ** ADAM_STEP)
    delta = -ADAM_LR * (m_hat / (jnp.sqrt(v_hat) + ADAM_EPS) + ADAM_WD * w)
    return delta, m, v


def _adamw(w, g, m, v, name):
    rows, cols = w.shape
    tr = min(rows, 256)

    def body(w_ref, g_ref, m_ref, v_ref, d_ref, mo_ref, vo_ref):
        d_ref[...], mo_ref[...], vo_ref[...] = _adamw_math(w_ref[...], g_ref[...], m_ref[...], v_ref[...])

    spec = pl.BlockSpec((tr, cols), lambda i: (i, 0))
    return pl.pallas_call(
        body, name=name, grid=(rows // tr,), out_shape=[SDS((rows, cols), F32)] * 3, in_specs=[spec] * 4, out_specs=[spec] * 3,
        compiler_params=_params("parallel"),
    )(w, g, m, v)


def _ada_forward(c_all, w_ada, b_cols):
    n_cols = w_ada.shape[1]

    def body(c_ref, w_ref, b_ref, act_ref, mod_ref):
        cv = c_ref[...]
        act = cv * jax.nn.sigmoid(cv)
        act_ref[...] = act
        mod_ref[...] = jnp.dot(act.astype(BF16), w_ref[...].astype(BF16), preferred_element_type=F32) + b_ref[...]

    return pl.pallas_call(
        body, name="ada_forward", out_shape=[SDS(c_all.shape, F32), SDS((N_DEV, n_cols), F32)],
        in_specs=[VMEM_SPEC] * 3, out_specs=[VMEM_SPEC] * 2, compiler_params=pltpu.CompilerParams(vmem_limit_bytes=VMEM_LIMIT_V7X),
    )(c_all, w_ada, b_cols)


def _ada_backward(act_t, dmod_cols, w, m, v):
    rows, cols = w.shape
    tn = 512

    def body(a_ref, dm_ref, w_ref, m_ref, v_ref, g_ref, d_ref, mo_ref, vo_ref):
        act = a_ref[...].astype(BF16).astype(F32)
        dm = dm_ref[...].astype(BF16).astype(F32)
        g = act[:, 0:1] * dm[0:1, :]
        for b in range(1, N_DEV):
            g = g + act[:, b:b + 1] * dm[b:b + 1, :]
        g_ref[...] = g
        d_ref[...], mo_ref[...], vo_ref[...] = _adamw_math(w_ref[...], g, m_ref[...], v_ref[...])

    spec = pl.BlockSpec((rows, tn), lambda j: (0, j))
    return pl.pallas_call(
        body, name="ada_backward", grid=(cols // tn,), out_shape=[SDS((rows, cols), F32)] * 4,
        in_specs=[pl.BlockSpec((rows, N_DEV), lambda j: (0, 0)), pl.BlockSpec((N_DEV, tn), lambda j: (0, j)), spec, spec, spec],
        out_specs=[spec] * 4, compiler_params=_params("parallel"),
    )(act_t, dmod_cols, w, m, v)


def _sum_rows(gathered, n_loss):
    n = gathered.shape[2]

    def body(g_ref, o_ref):
        acc = g_ref[0]
        for r in range(1, N_DEV):
            acc = acc + g_ref[r]
        o_ref[...] = acc
        o_ref[:, n - n_loss:n] = jnp.broadcast_to(jnp.sum(acc[:, n - n_loss:n], axis=1, keepdims=True), (1, n_loss))

    return pl.pallas_call(body, name="sum_rows", out_shape=SDS((1, n), F32), in_specs=[VMEM_SPEC], out_specs=VMEM_SPEC)(gathered)


def _adamw_row(w, g, m, v):
    def body(w_ref, g_ref, m_ref, v_ref, d_ref, mo_ref, vo_ref):
        d_ref[...], mo_ref[...], vo_ref[...] = _adamw_math(w_ref[...], g_ref[...], m_ref[...], v_ref[...])

    return pl.pallas_call(body, name="adamw_vectors", out_shape=[SDS(w.shape, F32)] * 3, in_specs=[VMEM_SPEC] * 4, out_specs=[VMEM_SPEC] * 3)(w, g, m, v)


def _blocks_to_pieces(w):
    nb, rows, n = w.shape
    q = rows // N_CHIPS
    return w.reshape(nb, N_CHIPS, q, n).transpose(1, 0, 2, 3).reshape(N_CHIPS, 2, nb // 2, q, n)


def _pieces_to_blocks(w):
    n_chips, _, half, q, n = w.shape
    return w.reshape(n_chips, 2 * half, q, n).transpose(1, 0, 2, 3).reshape(2 * half, n_chips * q, n)


def kernel(x, c, norm_mix_g, norm_mlp_g, w_ada, b_ada, w_in, conv_w, conv_b, w_rg_a, b_rg_a, w_rg_x, b_rg_x, a_param, w_branch_a, w_pool, b_pool, pool_scale, w_branch_b, w_out, w_up, w_down, final_g, loss_target, m_norm_mix_g, m_norm_mlp_g, m_w_ada, m_b_ada, m_w_in, m_conv_w, m_conv_b, m_w_rg_a, m_b_rg_a, m_w_rg_x, m_b_rg_x, m_a_param, m_w_branch_a, m_w_pool, m_b_pool, m_pool_scale, m_w_branch_b, m_w_out, m_w_up, m_w_down, m_final_g, v_norm_mix_g, v_norm_mlp_g, v_w_ada, v_b_ada, v_w_in, v_conv_w, v_conv_b, v_w_rg_a, v_b_rg_a, v_w_rg_x, v_b_rg_x, v_a_param, v_w_branch_a, v_w_pool, v_b_pool, v_pool_scale, v_w_branch_b, v_w_out, v_w_up, v_w_down, v_final_g):
    d = D_MODEL
    s_rows = x.shape[1]
    ts, tsq = min(512, s_rows), min(256, s_rows)
    xi, yi, ci = _coords()
    k_me, me = 2 * xi + yi, 4 * xi + 2 * yi + ci
    ada_cols = w_ada.shape[2]
    conv_cols = conv_w.shape[2]
    n_in, n_up = w_in.shape[2], w_up.shape[2]

    names = ("w_in", "w_up", "w_down", "w_a", "w_b", "w_out", "rg_a", "rg_x", "pool")
    mine = dict(zip(names, (w_in[0], w_up[0], w_down[0], w_branch_a[0], w_branch_b[0], w_out[0], w_rg_a[0], w_rg_x[0], w_pool[0])))
    moments_m = dict(zip(names, (m_w_in[0], m_w_up[0], m_w_down[0], m_w_branch_a[0], m_w_branch_b[0], m_w_out[0], m_w_rg_a[0], m_w_rg_x[0], m_w_pool[0])))
    moments_v = dict(zip(names, (v_w_in[0], v_w_up[0], v_w_down[0], v_w_branch_a[0], v_w_branch_b[0], v_w_out[0], v_w_rg_a[0], v_w_rg_x[0], v_w_pool[0])))
    block_weights, squares = ("rg_a", "rg_x", "pool"), ("w_a", "w_b", "w_out")
    place = jnp.stack([ci, k_me]).astype(jnp.int32)
    g_mix, g_mlp, g_fin = norm_mix_g, norm_mlp_g, final_g.reshape(1, d)
    x0, target = x[0], loss_target[0]
    wg = functools.partial(_weight_grad, tk=min(512, d))
    plain, back, back_sum = (dict(transposed=t, reduce_blocks=r) for t, r in ((False, False), (True, False), (True, True)))
    rows_d, vec8, vec1 = _row_spec(ts, d), _vec_spec(8, d), _vec_spec(1, d)

    def out_rows(cols, dtype, t=ts):
        return SDS((s_rows, cols), dtype), _row_spec(t, cols)

    sums_out = (SDS((8, d), F32), vec8)

    def halves(n):
        return mine[n].astype(BF16).reshape(2, -1, mine[n].shape[-1])

    def blocks(w):
        return _pieces_to_blocks(w.reshape(N_CHIPS, 2, N_BLOCKS // 2, BLOCK // N_CHIPS, BLOCK))

    stage = _Stage()
    i_c, i_cw, i_in = _add_allgather8(stage, c), _add_allgather8(stage, conv_w[0]), _add_gather_ici(stage, halves("w_in"))
    got = _run_stage(stage, "exchange_c_conv_w_in")
    conv_full = got[i_cw][0::2].transpose(1, 0, 2).reshape(CONV_WIDTH, d)
    act_all, mod_cols = _ada_forward(got[i_c].reshape(N_DEV, d), w_ada[0], lax.dynamic_slice_in_dim(b_ada, k_me * ada_cols, ada_cols, axis=1))
    stage = _Stage()
    i_mod, i_in = _add_allgather8(stage, mod_cols), _add_gather_d2d(stage, got[i_in])
    i_blk = [_add_gather_ici(stage, halves(n)) for n in block_weights]
    got = _run_stage(stage, "exchange_mod_w_in_gates")
    mod_mine = lax.dynamic_index_in_dim(got[i_mod], me, axis=1, keepdims=False)[0::2]
    mod = jnp.concatenate([mod_mine.reshape(N_MOD, d), jnp.zeros((8 - N_MOD, d), F32)], axis=0)
    w_in_all = got[i_in].reshape(N_CHIPS, d, n_in)
    vecs = jnp.concatenate([conv_b, b_rg_a, b_rg_x, a_param, b_pool, pool_scale, jnp.zeros((2, d), F32)], axis=0)

    def norm_first(a_ref, extra_refs, out_refs):
        h = _modulated_norm(a_ref[...], extra_refs[0], extra_refs[1], 1, 0)
        out_refs[1][...] = h
        return h

    stage = _Stage()
    i_blk = [_add_gather_d2d(stage, got[i]) for i in i_blk]
    i_sq = [_add_gather_ici(stage, halves(n)) for n in squares]
    (proj, h1), got = _matmul(x0, w_in_all, name="proj_in", tm=ts, extra=[(mod, vec8), (g_mix, vec1)], prepare=norm_first,
                              outs=[out_rows(N_CHIPS * n_in, F32), out_rows(d, BF16)], finish=_store_blocks(n_in), stage=stage, **plain)
    w_rg_a_all, w_rg_x_all, w_pool_all = (blocks(got[i]) for i in i_blk)
    stage = _Stage()
    i_sq = [_add_gather_d2d(stage, got[i]) for i in i_sq]
    i_up, i_down = _add_gather_ici(stage, halves("w_up")), _add_gather_ici(stage, halves("w_down"))
    (xr, hr, ga, p, pooled), got = _seq_forward(proj, conv_full, vecs, w_rg_a_all, w_rg_x_all, w_pool_all, ts=tsq, stage=stage)
    w_a_all, w_b_all, w_out_all = (got[i].reshape(1, d, d) for i in i_sq)
    stage = _Stage()
    i_up, i_down = _add_gather_d2d(stage, got[i_up]), _add_gather_d2d(stage, got[i_down])
    (br_a,), got = _matmul(ga, w_a_all, name="branch_a", tm=ts, outs=[out_rows(d, F32)], finish=_store_blocks(d), stage=stage, **plain)
    w_up_all, w_down_all = got[i_up].reshape(N_CHIPS, d, n_up), got[i_down].reshape(N_CHIPS, D_FF // N_CHIPS, d)
    br_b, = _matmul(pooled, w_b_all, name="branch_b", tm=ts, outs=[out_rows(d, F32)], finish=_store_blocks(d), **plain)
    merged = _merge_forward(proj, br_a, br_b, ts=ts)

    def residual_norm(k, acc, extra_refs, out_refs):
        x_ref, mod_ref, g_ref = extra_refs
        out_refs[0][...] = acc
        x2_t = x_ref[...] + mod_ref[2:3, :] * acc
        out_refs[1][...] = x2_t
        out_refs[2][...] = _modulated_norm(x2_t, mod_ref, g_ref, 4, 3)

    mo, x2, h2 = _matmul(merged, w_out_all, name="mix_out", tm=ts, extra=[(x0, rows_d), (mod, vec8), (g_mlp, vec1)],
                         outs=[out_rows(d, F32), out_rows(d, F32), out_rows(d, BF16)], finish=residual_norm,
                         transposed=False, reduce_blocks=True)

    def relu_squared(k, acc, extra_refs, out_refs):
        out_refs[0][:, k * n_up:(k + 1) * n_up] = acc
        out_refs[1][:, k * n_up:(k + 1) * n_up] = jnp.square(jnp.maximum(acc, 0.0)).astype(BF16)

    up, ff = _matmul(h2, w_up_all, name="mlp_up", tm=ts, outs=[out_rows(D_FF, F32), out_rows(D_FF, BF16)], finish=relu_squared, **plain)

    def loss_head(k, acc, extra_refs, out_refs):
        x2_ref, mod_ref, g_ref, t_ref = extra_refs
        dx_ref, df_ref, acc_ref = out_refs
        _zero_at_first_step(acc_ref)
        xh, r = _rms(x2_ref[...] + mod_ref[5:6, :] * acc)
        err = xh * g_ref[...] - t_ref[...]
        dy = err / d
        dxh = dy * g_ref[...]
        dx3_t = r * (dxh - xh * jnp.mean(dxh * xh, axis=-1, keepdims=True))
        dx_ref[...] = dx3_t
        df_ref[...] = (dx3_t * mod_ref[5:6, :]).astype(BF16)
        _accumulate(acc_ref, 0, dy * xh)
        _accumulate(acc_ref, 1, dx3_t * acc)
        _accumulate(acc_ref, 2, jnp.square(err) * (0.5 / d))

    dx3, dffo, sums_head = _matmul(ff, w_down_all, name="mlp_down", tm=ts, extra=[(x2, rows_d), (mod, vec8), (g_fin, vec1), (target, rows_d)],
                                   outs=[out_rows(d, F32), out_rows(d, BF16), sums_out], finish=loss_head, transposed=False, reduce_blocks=True)

    partial_of, from_sibling, chip_sum, from_chips, half_done, quarter = {}, {}, {}, {}, {}, {}

    def publish(n, g):
        g = _blocks_to_pieces(g).astype(BF16) if n in block_weights else g
        partial_of[n] = g.reshape(N_CHIPS, 2, -1, mine[n].shape[-1])

    def exchange(to_sibling=(), to_chips=(), swap=()):
        st = _Stage()
        slots = [(n, from_sibling, _add_reduce_d2d(st, partial_of[n])) for n in to_sibling]
        slots += [(n, from_chips, _add_reduce_ici(st, chip_sum[n])) for n in to_chips]
        slots += [(n, quarter, _add_swap_halves(st, half_done[n])) for n in swap]
        return st, slots

    def collect(slots, outs):
        for n, where, i in slots:
            where[n] = outs[i]

    def sum_pairs(*ns):
        for n in ns:
            chip_sum[n] = _sum_pair(partial_of[n], from_sibling[n], place, "sum_pair_" + n)

    def sum_quarters(*ns):
        for n in ns:
            half_done[n] = _sum_quarter(partial_of[n], from_sibling[n], from_chips[n], place, "sum_quarter_" + n)

    publish("w_down", wg(ff, dffo, nb=1, name="grad_w_down"))

    def relu_squared_backward(k, acc, extra_refs, out_refs):
        cols = slice(k * (D_FF // N_CHIPS), (k + 1) * (D_FF // N_CHIPS))
        out_refs[0][:, cols] = (acc * (2.0 * jnp.maximum(extra_refs[0][:, cols], 0.0))).astype(BF16)

    stage, slots = exchange(to_sibling=["w_down"])
    (dup,), got = _matmul(dffo, w_down_all, name="d_mlp_down", tm=tsq, extra=[(up, _row_spec(tsq, D_FF))], outs=[out_rows(D_FF, BF16, tsq)],
                          finish=relu_squared_backward, stage=stage, **back)
    collect(slots, got)
    sum_pairs("w_down")
    stage, slots = exchange(to_chips=["w_down"])
    g_up, got = wg(h2, dup, nb=N_CHIPS, name="grad_w_up", stage=stage)
    collect(slots, got)
    publish("w_up", g_up)
    sum_quarters("w_down")

    def norm_mlp_backward(k, acc, extra_refs, out_refs):
        x2_ref, dres_ref, mod_ref, g_ref, mo_ref = extra_refs
        dx_ref, dmo_ref, acc_ref = out_refs
        _zero_at_first_step(acc_ref)
        dx = _modulated_norm_backward(acc, x2_ref[...], dres_ref[...], mod_ref, g_ref, acc_ref, 4)
        dx_ref[...] = dx
        dmo_ref[...] = (dx * mod_ref[2:3, :]).astype(BF16)
        _accumulate(acc_ref, 3, dx * mo_ref[...])

    stage, slots = exchange(to_sibling=["w_up"], swap=["w_down"])
    (dx2, dmo, sums_mlp), got = _matmul(dup, w_up_all, name="d_mlp_up", tm=ts, stage=stage, finish=norm_mlp_backward,
                                        extra=[(x2, rows_d), (dx3, rows_d), (mod, vec8), (g_mlp, vec1), (mo, rows_d)],
                                        outs=[out_rows(d, F32), out_rows(d, BF16), sums_out], **back_sum)
    collect(slots, got)
    sum_pairs("w_up")
    publish("w_out", wg(merged, dmo, nb=1, name="grad_w_out"))

    def merge_backward(k, acc, extra_refs, out_refs):
        ga_ref, gb_ref, a_ref, b_ref = extra_refs
        sa, sb = jax.nn.sigmoid(ga_ref[...]), jax.nn.sigmoid(gb_ref[...])
        out_refs[0][...] = (acc * sa).astype(BF16)
        out_refs[1][...] = (acc * sb).astype(BF16)
        out_refs[2][:, 0:d] = (acc * a_ref[...] * (sa * (1.0 - sa))).astype(BF16)
        out_refs[2][:, d:2 * d] = (acc * b_ref[...] * (sb * (1.0 - sb))).astype(BF16)

    stage, slots = exchange(to_sibling=["w_out"])
    (dbr_a, dbr_b, dgab), got = _matmul(dmo, w_out_all, name="d_mix_out", tm=ts, stage=stage, finish=merge_backward,
                                        extra=[(proj, _row_spec(ts, d, 3)), (proj, _row_spec(ts, d, 4)), (br_a, rows_d), (br_b, rows_d)],
                                        outs=[out_rows(d, BF16), out_rows(d, BF16), out_rows(2 * d, BF16)], **back_sum)
    collect(slots, got)
    publish("w_a", wg(ga, dbr_a, nb=1, name="grad_w_branch_a"))
    publish("w_b", wg(pooled, dbr_b, nb=1, name="grad_w_branch_b"))
    stage, slots = exchange(to_sibling=["w_a", "w_b"])
    (dga,), got = _matmul(dbr_a, w_a_all, name="d_branch_a", tm=ts, outs=[out_rows(d, F32)], finish=_store_blocks(d), stage=stage, **back)
    collect(slots, got)
    dpooled, = _matmul(dbr_b, w_b_all, name="d_branch_b", tm=ts, outs=[out_rows(d, F32)], finish=_store_blocks(d), **back)
    sum_pairs(*squares)
    stage, slots = exchange(to_chips=("w_up",) + squares)
    (dproj, sums_seq, d_rg_a, d_rg_x, d_pool), got = _seq_backward(proj, xr, hr, p, dga, dpooled, dgab, conv_full, vecs,
                                                                   w_rg_a_all, w_rg_x_all, w_pool_all, ts=tsq, stage=stage)
    collect(slots, got)
    sum_quarters("w_up", *squares)
    for n, g in zip(block_weights, (d_rg_a, d_rg_x, d_pool)):
        publish(n, g)
    stage, slots = exchange(to_sibling=block_weights, swap=("w_up",) + squares)
    g_in, got = wg(h1, dproj, nb=N_CHIPS, name="grad_w_in", stage=stage)
    collect(slots, got)
    publish("w_in", g_in)
    sum_pairs(*block_weights)
    stage, slots = exchange(to_sibling=["w_in"], to_chips=block_weights)
    collect(slots, _run_stage(stage, "exchange_w_in_gates"))
    sum_pairs("w_in")
    sum_quarters(*block_weights)

    def norm_mix_backward(k, acc, extra_refs, out_refs):
        x_ref, dres_ref, mod_ref, g_ref = extra_refs
        _zero_at_first_step(out_refs[1])
        out_refs[0][...] = _modulated_norm_backward(acc, x_ref[...], dres_ref[...], mod_ref, g_ref, out_refs[1], 1)

    stage, slots = exchange(to_chips=["w_in"], swap=block_weights)
    (grad_x, sums_mix), got = _matmul(dproj, w_in_all, name="d_proj_in", tm=ts, stage=stage, finish=norm_mix_backward,
                                      extra=[(x0, rows_d), (dx2, rows_d), (mod, vec8), (g_mix, vec1)],
                                      outs=[out_rows(d, F32), sums_out], **back_sum)
    collect(slots, got)
    sum_quarters("w_in")

    dmod = jnp.concatenate([sums_mix[1:2], sums_mix[0:1], sums_mlp[3:4], sums_mlp[1:2], sums_mlp[0:1], sums_head[1:2]], axis=1)
    row = jnp.concatenate([sums_mix[2:3], sums_mlp[2:3], sums_seq[4:5], sums_seq[5:6], sums_seq[6:7], sums_seq[7:8], sums_seq[8:9],
                           sums_seq[9:10], sums_head[0:1], sums_seq[0:CONV_WIDTH].reshape(1, CONV_WIDTH * d), dmod, sums_head[2:3]], axis=1)
    n_vec, n_conv, n_mod = 9 * d, CONV_WIDTH * d, N_MOD * d
    stage, slots = exchange(swap=["w_in"])
    i_rows = _add_allgather8(stage, row)
    got = _run_stage(stage, "exchange_sums_w_in")
    collect(slots, got)
    rows_all = got[i_rows]
    total = _sum_rows(rows_all, d)
    loss = total[0, n_vec + n_conv + n_mod]
    dmod_all = rows_all[:, 0, n_vec + n_conv:n_vec + n_conv + n_mod]
    g_ada, d_ada, m_ada, v_ada = _ada_backward(act_all.T, lax.dynamic_slice_in_dim(dmod_all, k_me * ada_cols, ada_cols, axis=1),
                                               w_ada[0], m_w_ada[0], v_w_ada[0])
    g_conv = lax.dynamic_slice_in_dim(total[:, n_vec:n_vec + n_conv].reshape(CONV_WIDTH, d), k_me * conv_cols, conv_cols, axis=1)
    g_row = jnp.concatenate([total[:, 0:n_vec], g_conv.reshape(1, CONV_WIDTH * conv_cols), total[:, n_vec + n_conv:n_vec + n_conv + n_mod]], axis=1)

    def pack(nm, nl, fg, cw, ba):
        return jnp.concatenate([nm, nl, cw[1], cw[2], cw[3], cw[4], cw[5], cw[6], fg.reshape(1, d),
                                cw[0][0].reshape(1, CONV_WIDTH * conv_cols), ba], axis=1)

    w_row = pack(norm_mix_g, norm_mlp_g, final_g, (conv_w, conv_b, b_rg_a, b_rg_x, a_param, b_pool, pool_scale), b_ada)
    m_row = pack(m_norm_mix_g, m_norm_mlp_g, m_final_g, (m_conv_w, m_conv_b, m_b_rg_a, m_b_rg_x, m_a_param, m_b_pool, m_pool_scale), m_b_ada)
    v_row = pack(v_norm_mix_g, v_norm_mlp_g, v_final_g, (v_conv_w, v_conv_b, v_b_rg_a, v_b_rg_x, v_a_param, v_b_pool, v_pool_scale), v_b_ada)
    vec_out = (g_row,) + tuple(_adamw_row(w_row, g_row, m_row, v_row))

    def unpack(r):
        out = {nme: r[:, i * d:(i + 1) * d] for i, nme in enumerate(
            ("norm_mix_g", "norm_mlp_g", "conv_b", "b_rg_a", "b_rg_x", "a_param", "b_pool", "pool_scale"))}
        out["final_g"] = r[0, 8 * d:9 * d]
        out["conv_w"] = r[:, 9 * d:9 * d + CONV_WIDTH * conv_cols].reshape(1, CONV_WIDTH, conv_cols)
        out["b_ada"] = r[:, 9 * d + CONV_WIDTH * conv_cols:]
        return out

    big_out = {}
    for n in names:
        shape2 = (-1, mine[n].shape[-1])
        g2 = quarter[n].reshape(shape2)
        res = (g2,) + tuple(_adamw(mine[n].reshape(shape2), g2, moments_m[n].reshape(shape2), moments_v[n].reshape(shape2), "adamw_" + n))
        big_out[n] = [r.reshape((1,) + mine[n].shape) for r in res]

    key = {"w_in": "w_in", "w_rg_a": "rg_a", "w_rg_x": "rg_x", "w_branch_a": "w_a", "w_pool": "pool", "w_branch_b": "w_b", "w_out": "w_out",
           "w_up": "w_up", "w_down": "w_down"}
    order = ("norm_mix_g", "norm_mlp_g", "w_ada", "b_ada", "w_in", "conv_w", "conv_b", "w_rg_a", "b_rg_a", "w_rg_x", "b_rg_x", "a_param",
             "w_branch_a", "w_pool", "b_pool", "pool_scale", "w_branch_b", "w_out", "w_up", "w_down", "final_g")
    ada_out = [g_ada[None], d_ada[None], m_ada[None], v_ada[None]]
    outs = [loss, grad_x[None]]
    for kind in range(4):
        vec = unpack(vec_out[kind])
        for nme in order:
            outs.append(ada_out[kind] if nme == "w_ada" else big_out[key[nme]][kind] if nme in key else vec[nme])
    return tuple(outs)
```

```python
import functools

import jax
import jax.numpy as jnp
from jax import lax
from jax.experimental import pallas as pl
from jax.experimental.pallas import tpu as pltpu

F32, BF16 = jnp.float32, jnp.bfloat16
D_MODEL = 1024
D_FF = 4 * D_MODEL
N_BLOCKS = 4
BLOCK = D_MODEL // N_BLOCKS
CONV_WIDTH = 4
POOL_WINDOWS = (2, 4, 8, 16)
CONV_HALO = 8
POOL_HALO = 16
N_MOD = 6
EPS = 1e-6
C_RG = 8.0
ADAM_LR, ADAM_B1, ADAM_B2, ADAM_EPS, ADAM_WD, ADAM_STEP = 0.001, 0.9, 0.999, 1e-08, 0.01, 10
N_DEV = 8
N_CHIPS = 4
VMEM_LIMIT_V7X = 56 * 2**20
MESH = pl.DeviceIdType.MESH
SDS = jax.ShapeDtypeStruct
HBM_SPEC = pl.BlockSpec(memory_space=pltpu.HBM)
VMEM_SPEC = pl.BlockSpec(memory_space=pltpu.VMEM)


def _params(*semantics):
    return pltpu.CompilerParams(dimension_semantics=semantics, vmem_limit_bytes=VMEM_LIMIT_V7X)


def _coords():
    return lax.axis_index("x"), lax.axis_index("y"), lax.axis_index("c")


def _flip(v, bit):
    return 1 - v if bit else v


def _chips():
    x, y, _ = _coords()
    return [(1 - x, y), (x, 1 - y), (1 - x, 1 - y)]


class _Stage:
    def __init__(self):
        self.inputs, self.in_specs, self.outputs, self.out_specs, self.aliases = [], [], [], [], {}
        self.parts, self.n_copies = [], 0

    def add(self, inputs, in_spec, outputs, out_spec, n_copies, build, alias=False):
        i0, o0 = len(self.inputs), len(self.outputs)
        self.inputs += inputs
        self.in_specs += [in_spec] * len(inputs)
        self.outputs += outputs
        self.out_specs += [out_spec] * len(outputs)
        if alias:
            self.aliases.update({i0 + i: o0 + i for i in range(len(inputs))})
        self.parts.append((build, i0, len(inputs), o0, len(outputs)))
        self.n_copies += n_copies
        return list(range(o0, o0 + len(outputs)))

    def copies(self, in_refs, out_refs):
        out = []
        for build, i0, ni, o0, no in self.parts:
            out += build(in_refs[i0:i0 + ni], out_refs[o0:o0 + no])
        assert len(out) == self.n_copies
        return out

    def run(self, in_refs, out_refs, send_sems, recv_sems, start):
        x, y, c = _coords()
        for s, (src, dst, to, landing) in enumerate(self.copies(in_refs, out_refs)):
            if to is None:
                cp = pltpu.make_async_copy(src, dst, send_sems.at[s])
                cp.start() if start else cp.wait()
                continue
            cp = pltpu.make_async_remote_copy(src_ref=src, dst_ref=dst, send_sem=send_sems.at[s], recv_sem=recv_sems.at[s],
                                              device_id=to, device_id_type=MESH)
            if start:
                cp.start()
            else:
                pltpu.make_async_remote_copy(src_ref=landing, dst_ref=landing, send_sem=send_sems.at[s], recv_sem=recv_sems.at[s],
                                             device_id=(x, y, c), device_id_type=MESH).wait_recv()
                cp.wait_send()


def _hosted(stage, body, *, name, in_specs, out_specs, out_shape, grid=(), scratch_shapes=(), compiler_params=None):
    if stage is None:
        return pl.pallas_call(body, name=name, grid=grid, in_specs=in_specs, out_specs=out_specs, out_shape=out_shape,
                              scratch_shapes=list(scratch_shapes), compiler_params=compiler_params)
    single = not isinstance(out_shape, (list, tuple))
    h_out_shape = [out_shape] if single else list(out_shape)
    h_out_specs = [out_specs] if single else list(out_specs)
    n_in, n_out, n_scr = len(in_specs), len(h_out_shape), len(scratch_shapes)
    s_in, s_out = len(stage.inputs), len(stage.outputs)

    def wrapped(*refs):
        h_in, st_in = refs[:n_in], refs[n_in:n_in + s_in]
        h_o, st_o = refs[n_in + s_in:n_in + s_in + n_out], refs[n_in + s_in + n_out:n_in + s_in + n_out + s_out]
        h_scr = refs[n_in + s_in + n_out + s_out:n_in + s_in + n_out + s_out + n_scr]
        send_sems, recv_sems = refs[n_in + s_in + n_out + s_out + n_scr:]
        if not grid:
            stage.run(st_in, st_o, send_sems, recv_sems, True)
            if body is not None:
                body(*h_in, *h_o, *h_scr)
            stage.run(st_in, st_o, send_sems, recv_sems, False)
            return
        ids = [pl.program_id(a) for a in range(len(grid))]
        first = functools.reduce(jnp.logical_and, [i == 0 for i in ids])
        last = functools.reduce(jnp.logical_and, [i == g - 1 for i, g in zip(ids, grid)])
        pl.when(first)(lambda: stage.run(st_in, st_o, send_sems, recv_sems, True))
        body(*h_in, *h_o, *h_scr)
        pl.when(last)(lambda: stage.run(st_in, st_o, send_sems, recv_sems, False))

    call = pl.pallas_call(
        wrapped, name=name, grid=grid, in_specs=list(in_specs) + stage.in_specs, out_specs=h_out_specs + stage.out_specs,
        out_shape=h_out_shape + stage.outputs, input_output_aliases={n_in + i: n_out + o for i, o in stage.aliases.items()},
        scratch_shapes=list(scratch_shapes) + [pltpu.SemaphoreType.DMA((stage.n_copies,)), pltpu.SemaphoreType.DMA((stage.n_copies,))],
        compiler_params=pltpu.CompilerParams(dimension_semantics=("arbitrary",) * len(grid), vmem_limit_bytes=VMEM_LIMIT_V7X),
    )

    def run(*operands):
        outs = call(*operands, *stage.inputs)
        host = outs[:n_out]
        return (host[0] if single else host), outs[n_out:]

    return run


def _run_stage(stage, name):
    return _hosted(stage, None, name=name, in_specs=[], out_specs=[], out_shape=[])()[1]


def _add_allgather8(stage, v):
    def build(ins, outs):
        x, y, c = _coords()
        me = 4 * x + 2 * y + c
        copies = [(ins[0], outs[0].at[me], None, None)]
        for r in range(1, N_DEV):
            peer = (_flip(x, r & 4), _flip(y, r & 2), _flip(c, r & 1))
            copies.append((ins[0], outs[0].at[me], peer, outs[0].at[me ^ r]))
        return copies

    return stage.add([v], VMEM_SPEC, [SDS((N_DEV,) + v.shape, v.dtype)], VMEM_SPEC, N_DEV, build)[0]


def _add_gather_ici(stage, shard):
    def build(ins, outs):
        x, y, c = _coords()
        k_me = 2 * x + y
        copies = [(ins[0], outs[0].at[k_me], (x, y, 1 - c), outs[0].at[k_me])]
        for px, py in _chips():
            copies.append((ins[0].at[c], outs[0].at[k_me, c], (px, py, c), outs[0].at[2 * px + py, c]))
        return copies

    return stage.add([shard], HBM_SPEC, [SDS((N_CHIPS,) + shard.shape, shard.dtype)], HBM_SPEC, N_CHIPS, build)[0]


def _add_gather_d2d(stage, whole):
    def build(ins, outs):
        x, y, c = _coords()
        return [(outs[0].at[2 * px + py, c], outs[0].at[2 * px + py, c], (x, y, 1 - c), outs[0].at[2 * px + py, 1 - c]) for px, py in _chips()]

    return stage.add([whole], HBM_SPEC, [SDS(whole.shape, whole.dtype)], HBM_SPEC, N_CHIPS - 1, build, alias=True)[0]


def _add_reduce_d2d(stage, grads):
    def build(ins, outs):
        x, y, c = _coords()
        return [(ins[0].at[k, 1 - c], outs[0].at[k], (x, y, 1 - c), outs[0].at[k]) for k in range(N_CHIPS)]

    return stage.add([grads], HBM_SPEC, [SDS((N_CHIPS,) + grads.shape[2:], grads.dtype)], HBM_SPEC, N_CHIPS, build)[0]


def _add_reduce_ici(stage, partial):
    def build(ins, outs):
        x, y, c = _coords()
        return [(ins[0].at[2 * px + py], outs[0].at[j], (px, py, c), outs[0].at[j]) for j, (px, py) in enumerate(_chips())]

    return stage.add([partial], HBM_SPEC, [SDS((N_CHIPS - 1,) + partial.shape[1:], partial.dtype)], HBM_SPEC, N_CHIPS - 1, build)[0]


def _add_swap_halves(stage, quarter):
    def build(ins, outs):
        x, y, c = _coords()
        return [(outs[0].at[c], outs[0].at[c], (x, y, 1 - c), outs[0].at[1 - c])]

    return stage.add([quarter], HBM_SPEC, [SDS(quarter.shape, quarter.dtype)], HBM_SPEC, 1, build, alias=True)[0]


def _sum_pair(grads, from_sibling, place, name):
    _, _, rows, cols = grads.shape
    tr = min(rows, 256)

    def body(place_ref, g_ref, s_ref, o_ref):
        o_ref[0] = (g_ref[0, 0].astype(F32) + s_ref[0].astype(F32)).astype(BF16)

    spec = pl.BlockSpec((1, tr, cols), lambda k, i, place_ref: (k, i, 0))
    return pl.pallas_call(
        body, name=name, out_shape=SDS((N_CHIPS, rows, cols), BF16),
        grid_spec=pltpu.PrefetchScalarGridSpec(
            num_scalar_prefetch=1, grid=(N_CHIPS, rows // tr),
            in_specs=[pl.BlockSpec((1, 1, tr, cols), lambda k, i, place_ref: (k, place_ref[0], i, 0)), spec], out_specs=spec),
        compiler_params=_params("parallel", "parallel"),
    )(place, grads, from_sibling)


def _sum_quarter(grads, from_sibling, from_chips, place, name):
    _, _, rows, cols = grads.shape
    tr = min(rows, 256)

    def body(place_ref, g_ref, s_ref, c_ref, o_ref):
        acc = g_ref[0, 0].astype(F32) + s_ref[0].astype(F32)
        for j in range(N_CHIPS - 1):
            acc = acc + c_ref[j].astype(F32)
        o_ref[0] = acc

    return pl.pallas_call(
        body, name=name, out_shape=SDS((2, rows, cols), F32),
        grid_spec=pltpu.PrefetchScalarGridSpec(
            num_scalar_prefetch=1, grid=(rows // tr,),
            in_specs=[pl.BlockSpec((1, 1, tr, cols), lambda i, place_ref: (place_ref[1], place_ref[0], i, 0)),
                      pl.BlockSpec((1, tr, cols), lambda i, place_ref: (place_ref[1], i, 0)),
                      pl.BlockSpec((N_CHIPS - 1, tr, cols), lambda i, place_ref: (0, i, 0))],
            out_specs=pl.BlockSpec((1, tr, cols), lambda i, place_ref: (place_ref[0], i, 0))),
        compiler_params=_params("parallel"),
    )(place, grads, from_sibling, from_chips)


def _row_spec(ts, cols, col_block=0):
    return pl.BlockSpec((ts, cols), lambda *g: (g[0], col_block))


def _tile_spec(ts, cols):
    return pl.BlockSpec((ts, cols), lambda *g: (g[0], g[1]))


def _vec_spec(rows, cols):
    return pl.BlockSpec((rows, cols), lambda *g: (0, 0))


def _matmul(a, w, *, transposed, reduce_blocks, name, tm, outs, finish, extra=(), prepare=None, stage=None):
    m_rows = a.shape[0]
    nb, r, c = w.shape
    kb = c if transposed else r
    dims = (((1,), (1,)), ((), ())) if transposed else (((1,), (0,)), ((), ()))
    assert a.shape[1] == (nb * kb if reduce_blocks else kb) and m_rows % tm == 0
    n_extra, n_outs = len(extra), len(outs)

    def body(a_ref, w_ref, *rest):
        extra_refs, out_refs, scratch = rest[:n_extra], rest[n_extra:n_extra + n_outs], rest[n_extra + n_outs:]
        if reduce_blocks:
            acc = lax.dot_general(a_ref[:, 0:kb], w_ref[0], dims, preferred_element_type=F32)
            for k in range(1, nb):
                acc += lax.dot_general(a_ref[:, k * kb:(k + 1) * kb], w_ref[k], dims, preferred_element_type=F32)
        else:
            if prepare is None:
                lhs = a_ref[...]
            else:
                @pl.when(pl.program_id(1) == 0)
                def _():
                    scratch[0][...] = prepare(a_ref, extra_refs, out_refs)
                lhs = scratch[0][...]
            acc = lax.dot_general(lhs, w_ref[0], dims, preferred_element_type=F32)
        finish(acc, extra_refs, out_refs)

    if reduce_blocks:
        grid, w_spec = (m_rows // tm,), pl.BlockSpec((nb, r, c), lambda *g: (0, 0, 0), pipeline_mode=pl.Buffered(1))
    else:
        grid, w_spec = (m_rows // tm, nb), pl.BlockSpec((1, r, c), lambda *g: (g[1], 0, 0))
    return _hosted(
        stage, body, name=name, grid=grid, out_shape=[s for s, _ in outs], out_specs=[s for _, s in outs],
        in_specs=[_row_spec(tm, a.shape[1]), w_spec] + [s for _, s in extra],
        scratch_shapes=[] if prepare is None else [pltpu.VMEM((tm, kb), BF16)],
        compiler_params=_params(*(("arbitrary",) * len(grid))),
    )(a, w, *[e for e, _ in extra])


def _store(acc, extra_refs, out_refs):
    out_refs[0][...] = acc.astype(out_refs[0].dtype)


def _weight_grad(a, b, *, nb, name, tk, stage=None):
    s_rows, k1 = a.shape
    tn = b.shape[1] // nb
    assert k1 % tk == 0 and b.shape[1] % nb == 0

    def body(a_ref, b_ref, o_ref):
        o_ref[0] = lax.dot_general(a_ref[...], b_ref[...], (((0,), (0,)), ((), ())), preferred_element_type=F32).astype(o_ref.dtype)

    return _hosted(
        stage, body, name=name, grid=(nb, k1 // tk), out_shape=SDS((nb, k1, tn), BF16),
        in_specs=[pl.BlockSpec((s_rows, tk), lambda n, i: (0, i)), pl.BlockSpec((s_rows, tn), lambda n, i: (0, n))],
        out_specs=pl.BlockSpec((1, tk, tn), lambda n, i: (n, i, 0)), compiler_params=_params("parallel", "parallel"),
    )(a, b)


def _rms(x):
    r = lax.rsqrt(jnp.mean(x * x, axis=-1, keepdims=True) + EPS)
    return x * r, r


def _accumulate(acc_ref, row, value):
    acc_ref[row:row + 1, :] += jnp.sum(value, axis=0, keepdims=True)


def _zero_at_first_step(acc_ref):
    @pl.when(pl.program_id(0) == 0)
    def _():
        acc_ref[...] = jnp.zeros_like(acc_ref)


def _gelu_parts(y):
    k0, k1 = 0.7978845608028654, 0.044715
    th = jnp.tanh(k0 * (y + k1 * (y * y * y)))
    gelu = 0.5 * y * (1.0 + th)
    dgelu = 0.5 * (1.0 + th) + 0.5 * y * (1.0 - th * th) * (k0 * (1.0 + 3.0 * k1 * (y * y)))
    return gelu, dgelu


def _one_minus_square(a, log_a):
    return -jnp.tanh(log_a) * (1.0 + a * a)


def _modulated_norm(x, mod_ref, g_ref, sc_row, sh_row):
    xh, _ = _rms(x)
    return ((xh * g_ref[...]) * (1.0 + mod_ref[sc_row:sc_row + 1, :]) + mod_ref[sh_row:sh_row + 1, :]).astype(BF16)


def _modulated_norm_backward(dh, x, dres, mod_ref, g_ref, acc_ref, sc_row):
    xh, r = _rms(x)
    _accumulate(acc_ref, 0, dh * (xh * g_ref[...]))
    _accumulate(acc_ref, 1, dh)
    dn = dh * (1.0 + mod_ref[sc_row:sc_row + 1, :])
    _accumulate(acc_ref, 2, dn * xh)
    dxh = dn * g_ref[...]
    return dres + r * (dxh - xh * jnp.mean(dxh * xh, axis=-1, keepdims=True))


def _seq_forward(proj, conv_w, vecs, w_a, w_x, w_p, *, ts, stage=None):
    s_rows = proj.shape[0]
    d = D_MODEL

    def body(x_ref, xh_ref, y_ref, u_ref, uh_ref, cw_ref, vec_ref, wa_ref, wx_ref, wp_ref,
             xr_o, hr_o, ga_o, p_o, pooled_o, carry, a_scr, b_scr):
        t = pl.program_id(0)

        @pl.when(t == 0)
        def _():
            carry[...] = jnp.zeros_like(carry)

        x = x_ref[...]
        halo = jnp.where(t == 0, 0.0, xh_ref[...])
        xx = jnp.concatenate([halo, x], axis=0)
        xr = vec_ref[0:1, :] + x * cw_ref[CONV_WIDTH - 1:CONV_WIDTH, :]
        for j in range(CONV_WIDTH - 1):
            xr = xr + pltpu.roll(xx, CONV_WIDTH - 1 - j, 0)[CONV_HALO:] * cw_ref[j:j + 1, :]
        xr_o[...] = xr
        xrb = xr.astype(BF16)
        zr = jnp.concatenate([jnp.dot(xrb[:, g * BLOCK:(g + 1) * BLOCK], wa_ref[g], preferred_element_type=F32)
                              for g in range(N_BLOCKS)], axis=1) + vec_ref[1:2, :]
        zi = jnp.concatenate([jnp.dot(xrb[:, g * BLOCK:(g + 1) * BLOCK], wx_ref[g], preferred_element_type=F32)
                              for g in range(N_BLOCKS)], axis=1) + vec_ref[2:3, :]
        r = jax.nn.sigmoid(zr)
        gate_i = jax.nn.sigmoid(zi)
        log_a = (-C_RG * r) * jax.nn.softplus(vec_ref[3:4, :])
        rows = lax.broadcasted_iota(jnp.int32, (ts, d), 0)
        a = jnp.exp(log_a)
        mult = jnp.where((rows == 0) & (t == 0), 1.0, jnp.sqrt(_one_minus_square(a, log_a)))
        a_scr[...] = a
        b_scr[...] = xr * gate_i * mult

        sub = lax.broadcasted_iota(jnp.int32, (8, d), 0)

        def chunk(i, h):
            a = a_scr[pl.ds(i * 8, 8), :]
            b = b_scr[pl.ds(i * 8, 8), :]
            for s in (1, 2, 4):
                keep = sub >= s
                b = jnp.where(keep, a * pltpu.roll(b, s, 0) + b, b)
                a = jnp.where(keep, a * pltpu.roll(a, s, 0), a)
            hh = a * h + b
            hr_o[pl.ds(i * 8, 8), :] = hh
            return hh[7:8, :]

        carry[...] = lax.fori_loop(0, ts // 8, chunk, carry[...])
        gelu, _ = _gelu_parts(y_ref[...])
        ga_o[...] = (gelu * hr_o[...]).astype(BF16)

        u = u_ref[...]
        uu = jnp.concatenate([jnp.where(t == 0, 0.0, uh_ref[...]), u], axis=0)
        pos = (rows[:, :BLOCK] + t * ts + 1).astype(F32)
        sums, have, parts = uu, 1, []
        for g, win in enumerate(POOL_WINDOWS):
            while have < win:
                sums = sums + pltpu.roll(sums, have, 0)
                have *= 2
            mean = sums[POOL_HALO:, g * BLOCK:(g + 1) * BLOCK] * (1.0 / jnp.minimum(pos, float(win)))
            parts.append(mean - u[:, g * BLOCK:(g + 1) * BLOCK])
        pb = jnp.concatenate(parts, axis=1).astype(BF16)
        p_o[...] = pb
        mixed = jnp.concatenate([jnp.dot(pb[:, g * BLOCK:(g + 1) * BLOCK], wp_ref[g], preferred_element_type=F32)
                                 for g in range(N_BLOCKS)], axis=1) + vec_ref[4:5, :]
        pooled_o[...] = (mixed * vec_ref[5:6, :]).astype(BF16)

    halo_x = pl.BlockSpec((CONV_HALO, d), lambda t: (jnp.maximum(t * (ts // CONV_HALO) - 1, 0), 0))
    halo_u = pl.BlockSpec((POOL_HALO, d), lambda t: (jnp.maximum(t * (ts // POOL_HALO) - 1, 0), 2))
    wspec = pl.BlockSpec((N_BLOCKS, BLOCK, BLOCK), lambda t: (0, 0, 0))
    return _hosted(
        stage, body, name="seq_forward", grid=(s_rows // ts,),
        out_shape=[SDS((s_rows, d), F32), SDS((s_rows, d), F32), SDS((s_rows, d), BF16), SDS((s_rows, d), BF16), SDS((s_rows, d), BF16)],
        in_specs=[_row_spec(ts, d, 0), halo_x, _row_spec(ts, d, 1), _row_spec(ts, d, 2), halo_u, _vec_spec(CONV_WIDTH, d), _vec_spec(8, d),
                  wspec, wspec, wspec],
        out_specs=[_row_spec(ts, d)] * 5,
        scratch_shapes=[pltpu.VMEM((1, d), F32), pltpu.VMEM((ts, d), F32), pltpu.VMEM((ts, d), F32)],
        compiler_params=_params("arbitrary"),
    )(proj, proj, proj, proj, proj, conv_w, vecs, w_a, w_x, w_p)


def _branches(ga, pooled, proj, w_a, w_b, *, ts, stage=None):
    s_rows, d = ga.shape

    def body(a_ref, p_ref, ga_ref, gb_ref, wa_ref, wb_ref, bra_o, brb_o, merged_o):
        bra = jnp.dot(a_ref[...], wa_ref[0], preferred_element_type=F32)
        brb = jnp.dot(p_ref[...], wb_ref[0], preferred_element_type=F32)
        bra_o[...] = bra
        brb_o[...] = brb
        merged_o[...] = (jax.nn.sigmoid(ga_ref[...]) * bra + jax.nn.sigmoid(gb_ref[...]) * brb).astype(BF16)

    rows = _row_spec(ts, d)
    wspec = pl.BlockSpec((1, d, d), lambda *g: (0, 0, 0), pipeline_mode=pl.Buffered(1))
    return _hosted(
        stage, body, name="branches", grid=(s_rows // ts,), out_shape=[SDS((s_rows, d), F32), SDS((s_rows, d), F32), SDS((s_rows, d), BF16)],
        in_specs=[rows, rows, _row_spec(ts, d, 3), _row_spec(ts, d, 4), wspec, wspec], out_specs=[rows, rows, rows],
        compiler_params=_params("arbitrary"),
    )(ga, pooled, proj, proj, w_a, w_b)


def _seq_backward(proj, xr, hr, p, dga, dpooled, dgab, conv_w, vecs, w_a, w_x, w_p, *, ts, stage=None):
    s_rows = proj.shape[0]
    d = D_MODEL
    n_t = s_rows // ts

    def block_dots(lhs, w_ref, dims):
        return jnp.concatenate([lax.dot_general(lhs[:, g * BLOCK:(g + 1) * BLOCK], w_ref[g], dims, preferred_element_type=F32)
                                for g in range(N_BLOCKS)], axis=1)

    def add_block_grads(dw_ref, lhs, rhs):
        for g in range(N_BLOCKS):
            dw_ref[g] += lax.dot_general(lhs[:, g * BLOCK:(g + 1) * BLOCK], rhs[:, g * BLOCK:(g + 1) * BLOCK],
                                         (((0,), (0,)), ((), ())), preferred_element_type=F32)

    nn, nt = (((1,), (0,)), ((), ())), (((1,), (1,)), ((), ()))

    def body(x_ref, y_ref, xr_ref, hr_ref, hh_ref, p_ref, dga_ref, dpl_ref, dgab_ref, cw_ref, vec_ref, wa_ref, wx_ref, wp_ref,
             dproj_o, acc_o, dwa_o, dwx_o, dwp_o, g_carry, dxr_carry, q_carry, a_scr, b_scr, g_scr):
        step = pl.program_id(0)
        t = n_t - 1 - step

        @pl.when(step == 0)
        def _():
            for ref in (acc_o, dwa_o, dwx_o, dwp_o, g_carry, dxr_carry, q_carry):
                ref[...] = jnp.zeros_like(ref)

        rows = lax.broadcasted_iota(jnp.int32, (ts, d), 0)
        start = (rows == 0) & (t == 0)
        xr = xr_ref[...]
        xrb = xr.astype(BF16)
        r = jax.nn.sigmoid(block_dots(xrb, wa_ref, nn) + vec_ref[1:2, :])
        gate_i = jax.nn.sigmoid(block_dots(xrb, wx_ref, nn) + vec_ref[2:3, :])
        sp = jax.nn.softplus(vec_ref[3:4, :])
        log_a = (-C_RG * r) * sp
        a = jnp.exp(log_a)
        m_square = _one_minus_square(a, log_a)
        mult = jnp.where(start, 1.0, jnp.sqrt(m_square))
        gelu, dgelu = _gelu_parts(y_ref[...])
        dga_t = dga_ref[...]
        hr_t = hr_ref[...]
        dy = dga_t * hr_t * dgelu

        a_scr[...] = jnp.where(rows == ts - 1, 1.0, pltpu.roll(a, ts - 1, 0))
        b_scr[...] = dga_t * gelu
        sub = lax.broadcasted_iota(jnp.int32, (8, d), 0)

        def chunk(i, g_next):
            at = pl.multiple_of((ts // 8 - 1 - i) * 8, 8)
            aa = a_scr[pl.ds(at, 8), :]
            bb = b_scr[pl.ds(at, 8), :]
            for s in (1, 2, 4):
                keep = sub < 8 - s
                bb = jnp.where(keep, bb + aa * pltpu.roll(bb, 8 - s, 0), bb)
                aa = jnp.where(keep, aa * pltpu.roll(aa, 8 - s, 0), aa)
            gg = aa * g_next + bb
            g_scr[pl.ds(at, 8), :] = gg
            return gg[0:1, :]

        g_first = lax.fori_loop(0, ts // 8, chunk, g_carry[...])
        g_carry[...] = a[0:1, :] * g_first
        g = g_scr[...]

        h_before = jnp.where(t == 0, 0.0, hh_ref[CONV_HALO - 1:CONV_HALO, :])
        h_prev = jnp.where(rows == 0, h_before, pltpu.roll(hr_t, 1, 0))
        dxr = g * gate_i * mult
        d_i = g * xr * mult
        d_mult = g * xr * gate_i
        dlog_a = (g * h_prev) * a - jnp.where(start, 0.0, d_mult * (a * a) * lax.rsqrt(m_square))
        dzr = (dlog_a * (-C_RG * sp)) * (r * (1.0 - r))
        dzi = d_i * (gate_i * (1.0 - gate_i))
        _accumulate(acc_o, 5, dzr)
        _accumulate(acc_o, 6, dzi)
        _accumulate(acc_o, 7, dlog_a * (-C_RG * r) * jax.nn.sigmoid(vec_ref[3:4, :]))
        dzrb, dzib = dzr.astype(BF16), dzi.astype(BF16)
        add_block_grads(dwa_o, xrb, dzrb)
        add_block_grads(dwx_o, xrb, dzib)
        dxr = dxr + block_dots(dzrb, wa_ref, nt) + block_dots(dzib, wx_ref, nt)

        x = x_ref[...]
        ext = jnp.concatenate([dxr, dxr_carry[...]], axis=0)
        dx = dxr * cw_ref[CONV_WIDTH - 1:CONV_WIDTH, :]
        _accumulate(acc_o, CONV_WIDTH - 1, x * dxr)
        for j in range(CONV_WIDTH - 1):
            ahead = pltpu.roll(ext, ts + CONV_HALO - (CONV_WIDTH - 1 - j), 0)[:ts]
            dx = dx + ahead * cw_ref[j:j + 1, :]
            _accumulate(acc_o, j, x * ahead)
        _accumulate(acc_o, 4, dxr)
        dxr_carry[...] = dxr[0:CONV_HALO, :]

        pb = p_ref[...]
        mixed = block_dots(pb, wp_ref, nn) + vec_ref[4:5, :]
        dpl = dpl_ref[...]
        _accumulate(acc_o, 9, dpl * mixed)
        dmixed = dpl * vec_ref[5:6, :]
        _accumulate(acc_o, 8, dmixed)
        dmb = dmixed.astype(BF16)
        add_block_grads(dwp_o, pb, dmb)
        dp = block_dots(dmb, wp_ref, nt)
        pos = (rows[:, :BLOCK] + t * ts + 1).astype(F32)
        q = jnp.concatenate([dp[:, g * BLOCK:(g + 1) * BLOCK] * (1.0 / jnp.minimum(pos, float(win))) for g, win in enumerate(POOL_WINDOWS)], axis=1)
        sums, have, parts = jnp.concatenate([q, q_carry[...]], axis=0), 1, []
        for g, win in enumerate(POOL_WINDOWS):
            while have < win:
                sums = sums + pltpu.roll(sums, ts + POOL_HALO - have, 0)
                have *= 2
            parts.append(sums[:ts, g * BLOCK:(g + 1) * BLOCK])
        du = jnp.concatenate(parts, axis=1) - dp
        q_carry[...] = q[0:POOL_HALO, :]

        dproj_o[:, 0:d] = dx.astype(BF16)
        dproj_o[:, d:2 * d] = dy.astype(BF16)
        dproj_o[:, 2 * d:3 * d] = du.astype(BF16)
        dproj_o[:, 3 * d:5 * d] = dgab_ref[...]

    def rev(cols, col_block=0):
        return pl.BlockSpec((ts, cols), lambda i: (n_t - 1 - i, col_block))

    halo_h = pl.BlockSpec((CONV_HALO, d), lambda i: (jnp.maximum((n_t - 1 - i) * (ts // CONV_HALO) - 1, 0), 0))
    wspec = pl.BlockSpec((N_BLOCKS, BLOCK, BLOCK), lambda i: (0, 0, 0))
    return _hosted(
        stage, body, name="seq_backward", grid=(n_t,),
        out_shape=[SDS((s_rows, 5 * d), BF16), SDS((16, d), F32)] + [SDS((N_BLOCKS, BLOCK, BLOCK), F32)] * 3,
        in_specs=[rev(d, 0), rev(d, 1), rev(d), rev(d), halo_h, rev(d), rev(d), rev(d), rev(2 * d), _vec_spec(CONV_WIDTH, d), _vec_spec(8, d),
                  wspec, wspec, wspec],
        out_specs=[rev(5 * d), _vec_spec(16, d), wspec, wspec, wspec],
        scratch_shapes=[pltpu.VMEM((1, d), F32), pltpu.VMEM((CONV_HALO, d), F32), pltpu.VMEM((POOL_HALO, d), F32),
                        pltpu.VMEM((ts, d), F32), pltpu.VMEM((ts, d), F32), pltpu.VMEM((ts, d), F32)],
        compiler_params=_params("arbitrary"),
    )(proj, proj, xr, hr, hr, p, dga, dpooled, dgab, conv_w, vecs, w_a, w_x, w_p)


def _adamw_math(w, g, m, v):
    m = ADAM_B1 * m + (1.0 - ADAM_B1) * g
    v = ADAM_B2 * v + (1.0 - ADAM_B2) * jnp.square(g)
    m_hat = m / (1.0 - ADAM_B1 ** ADAM_STEP)
    v_hat = v / (1.0 - ADAM_B2 ** ADAM_STEP)
    delta = -ADAM_LR * (m_hat / (jnp.sqrt(v_hat) + ADAM_EPS) + ADAM_WD * w)
    return delta, m, v


def _adamw(w, g, m, v, name):
    rows, cols = w.shape
    tr = min(rows, 256)

    def body(w_ref, g_ref, m_ref, v_ref, d_ref, mo_ref, vo_ref):
        d_ref[...], mo_ref[...], vo_ref[...] = _adamw_math(w_ref[...], g_ref[...], m_ref[...], v_ref[...])

    spec = pl.BlockSpec((tr, cols), lambda i: (i, 0))
    return pl.pallas_call(
        body, name=name, grid=(rows // tr,), out_shape=[SDS((rows, cols), F32)] * 3, in_specs=[spec] * 4, out_specs=[spec] * 3,
        compiler_params=_params("parallel"),
    )(w, g, m, v)


def _ada_forward(c_all, w_ada, b_cols):
    n_cols = w_ada.shape[1]

    def body(c_ref, w_ref, b_ref, act_ref, mod_ref):
        cv = c_ref[...]
        act = cv * jax.nn.sigmoid(cv)
        act_ref[...] = act
        mod_ref[...] = jnp.dot(act.astype(BF16), w_ref[...].astype(BF16), preferred_element_type=F32) + b_ref[...]

    return pl.pallas_call(
        body, name="ada_forward", out_shape=[SDS(c_all.shape, F32), SDS((N_DEV, n_cols), F32)],
        in_specs=[VMEM_SPEC] * 3, out_specs=[VMEM_SPEC] * 2, compiler_params=pltpu.CompilerParams(vmem_limit_bytes=VMEM_LIMIT_V7X),
    )(c_all, w_ada, b_cols)


def _ada_backward(act_t, dmod_cols, w, m, v):
    rows, cols = w.shape
    tn = 512

    def body(a_ref, dm_ref, w_ref, m_ref, v_ref, g_ref, d_ref, mo_ref, vo_ref):
        act = a_ref[...].astype(BF16).astype(F32)
        dm = dm_ref[...].astype(BF16).astype(F32)
        g = act[:, 0:1] * dm[0:1, :]
        for b in range(1, N_DEV):
            g = g + act[:, b:b + 1] * dm[b:b + 1, :]
        g_ref[...] = g
        d_ref[...], mo_ref[...], vo_ref[...] = _adamw_math(w_ref[...], g, m_ref[...], v_ref[...])

    spec = pl.BlockSpec((rows, tn), lambda j: (0, j))
    return pl.pallas_call(
        body, name="ada_backward", grid=(cols // tn,), out_shape=[SDS((rows, cols), F32)] * 4,
        in_specs=[pl.BlockSpec((rows, N_DEV), lambda j: (0, 0)), pl.BlockSpec((N_DEV, tn), lambda j: (0, j)), spec, spec, spec],
        out_specs=[spec] * 4, compiler_params=_params("parallel"),
    )(act_t, dmod_cols, w, m, v)


def _sum_rows(gathered, n_loss):
    n = gathered.shape[2]

    def body(g_ref, o_ref):
        acc = g_ref[0]
        for r in range(1, N_DEV):
            acc = acc + g_ref[r]
        o_ref[...] = acc
        o_ref[:, n - n_loss:n] = jnp.broadcast_to(jnp.sum(acc[:, n - n_loss:n], axis=1, keepdims=True), (1, n_loss))

    return pl.pallas_call(body, name="sum_rows", out_shape=SDS((1, n), F32), in_specs=[VMEM_SPEC], out_specs=VMEM_SPEC)(gathered)


def _adamw_vectors(total, offsets, separate, params):
    n_p = len(params)
    apart = [g for g in separate if g is not None]

    def body(*refs):
        total_ref, apart_refs = refs[0], list(refs[1:1 + len(apart)])
        ins, outs = refs[1 + len(apart):1 + len(apart) + 3 * n_p], refs[1 + len(apart) + 3 * n_p:]
        for i in range(n_p):
            w_ref, m_ref, v_ref = ins[3 * i:3 * i + 3]
            g = apart_refs.pop(0)[...] if offsets[i] is None else total_ref[:, offsets[i]:offsets[i] + w_ref.shape[1]]
            outs[4 * i][...] = g
            outs[4 * i + 1][...], outs[4 * i + 2][...], outs[4 * i + 3][...] = _adamw_math(w_ref[...], g, m_ref[...], v_ref[...])

    flat = [a for p in params for a in p]
    res = pl.pallas_call(
        body, name="adamw_vectors", out_shape=[SDS(p[0].shape, F32) for p in params for _ in range(4)],
        in_specs=[VMEM_SPEC] * (1 + len(apart) + len(flat)), out_specs=[VMEM_SPEC] * (4 * n_p),
    )(total, *apart, *flat)
    return [tuple(res[4 * i:4 * i + 4]) for i in range(n_p)]


def _blocks_to_pieces(w):
    nb, rows, n = w.shape
    q = rows // N_CHIPS
    return w.reshape(nb, N_CHIPS, q, n).transpose(1, 0, 2, 3).reshape(N_CHIPS, 2, nb // 2, q, n)


def _pieces_to_blocks(w):
    n_chips, _, half, q, n = w.shape
    return w.reshape(n_chips, 2 * half, q, n).transpose(1, 0, 2, 3).reshape(2 * half, n_chips * q, n)


def kernel(x, c, norm_mix_g, norm_mlp_g, w_ada, b_ada, w_in, conv_w, conv_b, w_rg_a, b_rg_a, w_rg_x, b_rg_x, a_param, w_branch_a, w_pool, b_pool, pool_scale, w_branch_b, w_out, w_up, w_down, final_g, loss_target, m_norm_mix_g, m_norm_mlp_g, m_w_ada, m_b_ada, m_w_in, m_conv_w, m_conv_b, m_w_rg_a, m_b_rg_a, m_w_rg_x, m_b_rg_x, m_a_param, m_w_branch_a, m_w_pool, m_b_pool, m_pool_scale, m_w_branch_b, m_w_out, m_w_up, m_w_down, m_final_g, v_norm_mix_g, v_norm_mlp_g, v_w_ada, v_b_ada, v_w_in, v_conv_w, v_conv_b, v_w_rg_a, v_b_rg_a, v_w_rg_x, v_b_rg_x, v_a_param, v_w_branch_a, v_w_pool, v_b_pool, v_pool_scale, v_w_branch_b, v_w_out, v_w_up, v_w_down, v_final_g):
    d = D_MODEL
    s_rows = x.shape[1]
    ts, tsq = min(512, s_rows), min(256, s_rows)
    xi, yi, ci = _coords()
    k_me, me = 2 * xi + yi, 4 * xi + 2 * yi + ci
    ada_cols = w_ada.shape[2]
    conv_cols = conv_w.shape[2]
    n_in, n_up = w_in.shape[2], w_up.shape[2]

    names = ("w_in", "w_up", "w_down", "w_a", "w_b", "w_out", "rg_a", "rg_x", "pool")
    mine = dict(zip(names, (w_in[0], w_up[0], w_down[0], w_branch_a[0], w_branch_b[0], w_out[0], w_rg_a[0], w_rg_x[0], w_pool[0])))
    moments_m = dict(zip(names, (m_w_in[0], m_w_up[0], m_w_down[0], m_w_branch_a[0], m_w_branch_b[0], m_w_out[0], m_w_rg_a[0], m_w_rg_x[0], m_w_pool[0])))
    moments_v = dict(zip(names, (v_w_in[0], v_w_up[0], v_w_down[0], v_w_branch_a[0], v_w_branch_b[0], v_w_out[0], v_w_rg_a[0], v_w_rg_x[0], v_w_pool[0])))
    block_weights, squares = ("rg_a", "rg_x", "pool"), ("w_a", "w_b", "w_out")
    place = jnp.stack([ci, k_me]).astype(jnp.int32)
    g_mix, g_mlp, g_fin = norm_mix_g, norm_mlp_g, final_g.reshape(1, d)
    x0, target = x[0], loss_target[0]
    wg = functools.partial(_weight_grad, tk=min(512, d))
    plain, back, back_sum = (dict(transposed=t, reduce_blocks=r) for t, r in ((False, False), (True, False), (True, True)))
    rows_d, vec8, vec1 = _row_spec(ts, d), _vec_spec(8, d), _vec_spec(1, d)
    tw = min(1024, s_rows)

    def out_rows(cols, dtype, t=ts):
        return SDS((s_rows, cols), dtype), _row_spec(t, cols)

    def out_tiles(n_blocks, cols, dtype):
        return SDS((s_rows, n_blocks * cols), dtype), _tile_spec(tw, cols)

    sums_out = (SDS((8, d), F32), vec8)

    def halves(n):
        return mine[n].astype(BF16).reshape(2, -1, mine[n].shape[-1])

    def blocks(w):
        return _pieces_to_blocks(w.reshape(N_CHIPS, 2, N_BLOCKS // 2, BLOCK // N_CHIPS, BLOCK))

    stage = _Stage()
    i_c, i_cw, i_in = _add_allgather8(stage, c), _add_allgather8(stage, conv_w[0]), _add_gather_ici(stage, halves("w_in"))
    got = _run_stage(stage, "exchange_c_conv_w_in")
    conv_full = got[i_cw][0::2].transpose(1, 0, 2).reshape(CONV_WIDTH, d)
    act_all, mod_cols = _ada_forward(got[i_c].reshape(N_DEV, d), w_ada[0], lax.dynamic_slice_in_dim(b_ada, k_me * ada_cols, ada_cols, axis=1))
    stage = _Stage()
    i_mod, i_in = _add_allgather8(stage, mod_cols), _add_gather_d2d(stage, got[i_in])
    i_blk = [_add_gather_ici(stage, halves(n)) for n in block_weights]
    got = _run_stage(stage, "exchange_mod_w_in_gates")
    mod_mine = lax.dynamic_index_in_dim(got[i_mod], me, axis=1, keepdims=False)[0::2]
    mod = jnp.concatenate([mod_mine.reshape(N_MOD, d), jnp.zeros((8 - N_MOD, d), F32)], axis=0)
    w_in_all = got[i_in].reshape(N_CHIPS, d, n_in)
    vecs = jnp.concatenate([conv_b, b_rg_a, b_rg_x, a_param, b_pool, pool_scale, jnp.zeros((2, d), F32)], axis=0)

    def norm_first(a_ref, extra_refs, out_refs):
        h = _modulated_norm(a_ref[...], extra_refs[0], extra_refs[1], 1, 0)
        out_refs[1][...] = h
        return h

    stage = _Stage()
    i_blk = [_add_gather_d2d(stage, got[i]) for i in i_blk]
    i_sq = [_add_gather_ici(stage, halves(n)) for n in squares]
    (proj, h1), got = _matmul(x0, w_in_all, name="proj_in", tm=tw, extra=[(mod, vec8), (g_mix, vec1)], prepare=norm_first,
                              outs=[out_tiles(N_CHIPS, n_in, F32), out_rows(d, BF16, tw)], finish=_store, stage=stage, **plain)
    w_rg_a_all, w_rg_x_all, w_pool_all = (blocks(got[i]) for i in i_blk)
    stage = _Stage()
    i_sq = [_add_gather_d2d(stage, got[i]) for i in i_sq]
    i_up, i_down = _add_gather_ici(stage, halves("w_up")), _add_gather_ici(stage, halves("w_down"))
    (xr, hr, ga, p, pooled), got = _seq_forward(proj, conv_full, vecs, w_rg_a_all, w_rg_x_all, w_pool_all, ts=tsq, stage=stage)
    w_a_all, w_b_all, w_out_all = (got[i].reshape(1, d, d) for i in i_sq)
    stage = _Stage()
    i_up, i_down = _add_gather_d2d(stage, got[i_up]), _add_gather_d2d(stage, got[i_down])
    (br_a, br_b, merged), got = _branches(ga, pooled, proj, w_a_all, w_b_all, ts=ts, stage=stage)
    w_up_all, w_down_all = got[i_up].reshape(N_CHIPS, d, n_up), got[i_down].reshape(N_CHIPS, D_FF // N_CHIPS, d)

    def residual_norm(acc, extra_refs, out_refs):
        x_ref, mod_ref, g_ref = extra_refs
        out_refs[0][...] = acc
        x2_t = x_ref[...] + mod_ref[2:3, :] * acc
        out_refs[1][...] = x2_t
        out_refs[2][...] = _modulated_norm(x2_t, mod_ref, g_ref, 4, 3)

    mo, x2, h2 = _matmul(merged, w_out_all, name="mix_out", tm=tw, extra=[(x0, _row_spec(tw, d)), (mod, vec8), (g_mlp, vec1)],
                         outs=[out_rows(d, F32, tw), out_rows(d, F32, tw), out_rows(d, BF16, tw)], finish=residual_norm,
                         transposed=False, reduce_blocks=True)

    def relu_squared(acc, extra_refs, out_refs):
        out_refs[0][...] = acc
        out_refs[1][...] = jnp.square(jnp.maximum(acc, 0.0)).astype(BF16)

    up, ff = _matmul(h2, w_up_all, name="mlp_up", tm=tw, outs=[out_tiles(N_CHIPS, n_up, F32), out_tiles(N_CHIPS, n_up, BF16)],
                     finish=relu_squared, **plain)

    def loss_head(acc, extra_refs, out_refs):
        x2_ref, mod_ref, g_ref, t_ref = extra_refs
        dx_ref, df_ref, acc_ref = out_refs
        _zero_at_first_step(acc_ref)
        xh, r = _rms(x2_ref[...] + mod_ref[5:6, :] * acc)
        err = xh * g_ref[...] - t_ref[...]
        dy = err / d
        dxh = dy * g_ref[...]
        dx3_t = r * (dxh - xh * jnp.mean(dxh * xh, axis=-1, keepdims=True))
        dx_ref[...] = dx3_t
        df_ref[...] = (dx3_t * mod_ref[5:6, :]).astype(BF16)
        _accumulate(acc_ref, 0, dy * xh)
        _accumulate(acc_ref, 1, dx3_t * acc)
        _accumulate(acc_ref, 2, jnp.square(err) * (0.5 / d))

    dx3, dffo, sums_head = _matmul(ff, w_down_all, name="mlp_down", tm=ts, extra=[(x2, rows_d), (mod, vec8), (g_fin, vec1), (target, rows_d)],
                                   outs=[out_rows(d, F32), out_rows(d, BF16), sums_out], finish=loss_head, transposed=False, reduce_blocks=True)

    partial_of, from_sibling, chip_sum, from_chips, half_done, quarter = {}, {}, {}, {}, {}, {}

    def publish(n, g):
        g = _blocks_to_pieces(g).astype(BF16) if n in block_weights else g
        partial_of[n] = g.reshape(N_CHIPS, 2, -1, mine[n].shape[-1])

    def exchange(to_sibling=(), to_chips=(), swap=()):
        st = _Stage()
        slots = [(n, from_sibling, _add_reduce_d2d(st, partial_of[n])) for n in to_sibling]
        slots += [(n, from_chips, _add_reduce_ici(st, chip_sum[n])) for n in to_chips]
        slots += [(n, quarter, _add_swap_halves(st, half_done[n])) for n in swap]
        return st, slots

    def collect(slots, outs):
        for n, where, i in slots:
            where[n] = outs[i]

    def sum_pairs(*ns):
        for n in ns:
            chip_sum[n] = _sum_pair(partial_of[n], from_sibling[n], place, "sum_pair_" + n)

    def sum_quarters(*ns):
        for n in ns:
            half_done[n] = _sum_quarter(partial_of[n], from_sibling[n], from_chips[n], place, "sum_quarter_" + n)

    publish("w_down", wg(ff, dffo, nb=1, name="grad_w_down"))

    def relu_squared_backward(acc, extra_refs, out_refs):
        out_refs[0][...] = (acc * (2.0 * jnp.maximum(extra_refs[0][...], 0.0))).astype(BF16)

    stage, slots = exchange(to_sibling=["w_down"])
    (dup,), got = _matmul(dffo, w_down_all, name="d_mlp_down", tm=tw, extra=[(up, _tile_spec(tw, n_up))], outs=[out_tiles(N_CHIPS, n_up, BF16)],
                          finish=relu_squared_backward, stage=stage, **back)
    collect(slots, got)
    sum_pairs("w_down")
    stage, slots = exchange(to_chips=["w_down"])
    g_up, got = wg(h2, dup, nb=N_CHIPS, name="grad_w_up", stage=stage)
    collect(slots, got)
    publish("w_up", g_up)
    sum_quarters("w_down")

    def norm_mlp_backward(acc, extra_refs, out_refs):
        x2_ref, dres_ref, mod_ref, g_ref, mo_ref = extra_refs
        dx_ref, dmo_ref, acc_ref = out_refs
        _zero_at_first_step(acc_ref)
        dx = _modulated_norm_backward(acc, x2_ref[...], dres_ref[...], mod_ref, g_ref, acc_ref, 4)
        dx_ref[...] = dx
        dmo_ref[...] = (dx * mod_ref[2:3, :]).astype(BF16)
        _accumulate(acc_ref, 3, dx * mo_ref[...])

    stage, slots = exchange(to_sibling=["w_up"], swap=["w_down"])
    (dx2, dmo, sums_mlp), got = _matmul(dup, w_up_all, name="d_mlp_up", tm=ts, stage=stage, finish=norm_mlp_backward,
                                        extra=[(x2, rows_d), (dx3, rows_d), (mod, vec8), (g_mlp, vec1), (mo, rows_d)],
                                        outs=[out_rows(d, F32), out_rows(d, BF16), sums_out], **back_sum)
    collect(slots, got)
    sum_pairs("w_up")
    publish("w_out", wg(merged, dmo, nb=1, name="grad_w_out"))

    def merge_backward(acc, extra_refs, out_refs):
        ga_ref, gb_ref, a_ref, b_ref = extra_refs
        sa, sb = jax.nn.sigmoid(ga_ref[...]), jax.nn.sigmoid(gb_ref[...])
        out_refs[0][...] = (acc * sa).astype(BF16)
        out_refs[1][...] = (acc * sb).astype(BF16)
        out_refs[2][:, 0:d] = (acc * a_ref[...] * (sa * (1.0 - sa))).astype(BF16)
        out_refs[2][:, d:2 * d] = (acc * b_ref[...] * (sb * (1.0 - sb))).astype(BF16)

    stage, slots = exchange(to_sibling=["w_out"])
    (dbr_a, dbr_b, dgab), got = _matmul(dmo, w_out_all, name="d_mix_out", tm=ts, stage=stage, finish=merge_backward,
                                        extra=[(proj, _row_spec(ts, d, 3)), (proj, _row_spec(ts, d, 4)), (br_a, rows_d), (br_b, rows_d)],
                                        outs=[out_rows(d, BF16), out_rows(d, BF16), out_rows(2 * d, BF16)], **back_sum)
    collect(slots, got)
    publish("w_a", wg(ga, dbr_a, nb=1, name="grad_w_branch_a"))
    publish("w_b", wg(pooled, dbr_b, nb=1, name="grad_w_branch_b"))
    stage, slots = exchange(to_sibling=["w_a", "w_b"])
    (dga,), got = _matmul(dbr_a, w_a_all, name="d_branch_a", tm=tw, outs=[out_rows(d, F32, tw)], finish=_store, stage=stage, **back_sum)
    collect(slots, got)
    dpooled, = _matmul(dbr_b, w_b_all, name="d_branch_b", tm=tw, outs=[out_rows(d, F32, tw)], finish=_store, **back_sum)
    sum_pairs(*squares)
    stage, slots = exchange(to_chips=("w_up",) + squares)
    (dproj, sums_seq, d_rg_a, d_rg_x, d_pool), got = _seq_backward(proj, xr, hr, p, dga, dpooled, dgab, conv_full, vecs,
                                                                   w_rg_a_all, w_rg_x_all, w_pool_all, ts=tsq, stage=stage)
    collect(slots, got)
    sum_quarters("w_up", *squares)
    for n, g in zip(block_weights, (d_rg_a, d_rg_x, d_pool)):
        publish(n, g)
    stage, slots = exchange(to_sibling=block_weights, swap=("w_up",) + squares)
    g_in, got = wg(h1, dproj, nb=N_CHIPS, name="grad_w_in", stage=stage)
    collect(slots, got)
    publish("w_in", g_in)
    sum_pairs(*block_weights)
    stage, slots = exchange(to_sibling=["w_in"], to_chips=block_weights)
    collect(slots, _run_stage(stage, "exchange_w_in_gates"))
    sum_pairs("w_in")
    sum_quarters(*block_weights)

    def norm_mix_backward(acc, extra_refs, out_refs):
        x_ref, dres_ref, mod_ref, g_ref = extra_refs
        _zero_at_first_step(out_refs[1])
        out_refs[0][...] = _modulated_norm_backward(acc, x_ref[...], dres_ref[...], mod_ref, g_ref, out_refs[1], 1)

    stage, slots = exchange(to_chips=["w_in"], swap=block_weights)
    (grad_x, sums_mix), got = _matmul(dproj, w_in_all, name="d_proj_in", tm=ts, stage=stage, finish=norm_mix_backward,
                                      extra=[(x0, rows_d), (dx2, rows_d), (mod, vec8), (g_mix, vec1)],
                                      outs=[out_rows(d, F32), sums_out], **back_sum)
    collect(slots, got)
    sum_quarters("w_in")

    dmod = jnp.concatenate([sums_mix[1:2], sums_mix[0:1], sums_mlp[3:4], sums_mlp[1:2], sums_mlp[0:1], sums_head[1:2]], axis=1)
    row = jnp.concatenate([sums_mix[2:3], sums_mlp[2:3], sums_seq[4:5], sums_seq[5:6], sums_seq[6:7], sums_seq[7:8], sums_seq[8:9],
                           sums_seq[9:10], sums_head[0:1], sums_seq[0:CONV_WIDTH].reshape(1, CONV_WIDTH * d), dmod, sums_head[2:3]], axis=1)
    n_vec, n_conv, n_mod = 9 * d, CONV_WIDTH * d, N_MOD * d
    stage, slots = exchange(swap=["w_in"])
    i_rows = _add_allgather8(stage, row)
    got = _run_stage(stage, "exchange_sums_w_in")
    collect(slots, got)
    rows_all = got[i_rows]
    total = _sum_rows(rows_all, d)
    loss = total[0, n_vec + n_conv + n_mod]
    dmod_all = rows_all[:, 0, n_vec + n_conv:n_vec + n_conv + n_mod]
    g_ada, d_ada, m_ada, v_ada = _ada_backward(act_all.T, lax.dynamic_slice_in_dim(dmod_all, k_me * ada_cols, ada_cols, axis=1),
                                               w_ada[0], m_w_ada[0], v_w_ada[0])
    g_conv = lax.dynamic_slice_in_dim(total[:, n_vec:n_vec + n_conv].reshape(CONV_WIDTH, d), k_me * conv_cols, conv_cols, axis=1)
    vec_names = ("norm_mix_g", "norm_mlp_g", "conv_b", "b_rg_a", "b_rg_x", "a_param", "b_pool", "pool_scale", "final_g", "conv_w", "b_ada")
    vec_params = [(norm_mix_g, m_norm_mix_g, v_norm_mix_g), (norm_mlp_g, m_norm_mlp_g, v_norm_mlp_g), (conv_b, m_conv_b, v_conv_b),
                  (b_rg_a, m_b_rg_a, v_b_rg_a), (b_rg_x, m_b_rg_x, v_b_rg_x), (a_param, m_a_param, v_a_param), (b_pool, m_b_pool, v_b_pool),
                  (pool_scale, m_pool_scale, v_pool_scale), tuple(a.reshape(1, d) for a in (final_g, m_final_g, v_final_g)),
                  (conv_w[0], m_conv_w[0], v_conv_w[0]), (b_ada, m_b_ada, v_b_ada)]
    offsets = [i * d for i in range(9)] + [None, n_vec + n_conv]
    vec_res = _adamw_vectors(total, offsets, [None] * 9 + [g_conv, None], vec_params)
    vec_out = {n: [r.reshape(shape) for r in res] for n, res, shape in zip(
        vec_names, vec_res, [(1, d)] * 8 + [(d,), (1, CONV_WIDTH, conv_cols), (1, N_MOD * d)])}

    big_out = {}
    for n in names:
        shape2 = (-1, mine[n].shape[-1])
        g2 = quarter[n].reshape(shape2)
        res = (g2,) + tuple(_adamw(mine[n].reshape(shape2), g2, moments_m[n].reshape(shape2), moments_v[n].reshape(shape2), "adamw_" + n))
        big_out[n] = [r.reshape((1,) + mine[n].shape) for r in res]

    key = {"w_in": "w_in", "w_rg_a": "rg_a", "w_rg_x": "rg_x", "w_branch_a": "w_a", "w_pool": "pool", "w_branch_b": "w_b", "w_out": "w_out",
           "w_up": "w_up", "w_down": "w_down"}
    order = ("norm_mix_g", "norm_mlp_g", "w_ada", "b_ada", "w_in", "conv_w", "conv_b", "w_rg_a", "b_rg_a", "w_rg_x", "b_rg_x", "a_param",
             "w_branch_a", "w_pool", "b_pool", "pool_scale", "w_branch_b", "w_out", "w_up", "w_down", "final_g")
    ada_out = [g_ada[None], d_ada[None], m_ada[None], v_ada[None]]
    outs = [loss, grad_x[None]]
    for kind in range(4):
        for nme in order:
            outs.append(ada_out[kind] if nme == "w_ada" else big_out[key[nme]][kind] if nme in key else vec_out[nme][kind])
    return tuple(outs)
```

```python
import functools

import jax
import jax.numpy as jnp
from jax import lax
from jax.experimental import pallas as pl
from jax.experimental.pallas import tpu as pltpu

F32, BF16 = jnp.float32, jnp.bfloat16
D_MODEL = 1024
D_FF = 4 * D_MODEL
N_BLOCKS = 4
BLOCK = D_MODEL // N_BLOCKS
CONV_WIDTH = 4
POOL_WINDOWS = (2, 4, 8, 16)
CONV_HALO = 8
POOL_HALO = 16
N_MOD = 6
EPS = 1e-6
C_RG = 8.0
ADAM_LR, ADAM_B1, ADAM_B2, ADAM_EPS, ADAM_WD, ADAM_STEP = 0.001, 0.9, 0.999, 1e-08, 0.01, 10
N_DEV = 8
N_CHIPS = 4
VMEM_LIMIT_V7X = 56 * 2**20
MESH = pl.DeviceIdType.MESH
SDS = jax.ShapeDtypeStruct
HBM_SPEC = pl.BlockSpec(memory_space=pltpu.HBM)
VMEM_SPEC = pl.BlockSpec(memory_space=pltpu.VMEM)


def _params(*semantics):
    return pltpu.CompilerParams(dimension_semantics=semantics, vmem_limit_bytes=VMEM_LIMIT_V7X)


def _coords():
    return lax.axis_index("x"), lax.axis_index("y"), lax.axis_index("c")


def _flip(v, bit):
    return 1 - v if bit else v


def _chips():
    x, y, _ = _coords()
    return [(1 - x, y), (x, 1 - y), (1 - x, 1 - y)]


class _Stage:
    def __init__(self):
        self.inputs, self.in_specs, self.outputs, self.out_specs, self.aliases = [], [], [], [], {}
        self.parts, self.n_copies = [], 0

    def add(self, inputs, in_spec, outputs, out_spec, n_copies, build, alias=False):
        i0, o0 = len(self.inputs), len(self.outputs)
        self.inputs += inputs
        self.in_specs += [in_spec] * len(inputs)
        self.outputs += outputs
        self.out_specs += [out_spec] * len(outputs)
        if alias:
            self.aliases.update({i0 + i: o0 + i for i in range(len(inputs))})
        self.parts.append((build, i0, len(inputs), o0, len(outputs)))
        self.n_copies += n_copies
        return list(range(o0, o0 + len(outputs)))

    def copies(self, in_refs, out_refs):
        out = []
        for build, i0, ni, o0, no in self.parts:
            out += build(in_refs[i0:i0 + ni], out_refs[o0:o0 + no])
        assert len(out) == self.n_copies
        return out

    def run(self, in_refs, out_refs, send_sems, recv_sems, start):
        x, y, c = _coords()
        for s, (src, dst, to, landing) in enumerate(self.copies(in_refs, out_refs)):
            if to is None:
                cp = pltpu.make_async_copy(src, dst, send_sems.at[s])
                cp.start() if start else cp.wait()
                continue
            cp = pltpu.make_async_remote_copy(src_ref=src, dst_ref=dst, send_sem=send_sems.at[s], recv_sem=recv_sems.at[s],
                                              device_id=to, device_id_type=MESH)
            if start:
                cp.start()
            else:
                pltpu.make_async_remote_copy(src_ref=landing, dst_ref=landing, send_sem=send_sems.at[s], recv_sem=recv_sems.at[s],
                                             device_id=(x, y, c), device_id_type=MESH).wait_recv()
                cp.wait_send()


def _hosted(stage, body, *, name, in_specs, out_specs, out_shape, grid=(), scratch_shapes=(), compiler_params=None):
    if stage is None:
        return pl.pallas_call(body, name=name, grid=grid, in_specs=in_specs, out_specs=out_specs, out_shape=out_shape,
                              scratch_shapes=list(scratch_shapes), compiler_params=compiler_params)
    single = not isinstance(out_shape, (list, tuple))
    h_out_shape = [out_shape] if single else list(out_shape)
    h_out_specs = [out_specs] if single else list(out_specs)
    n_in, n_out, n_scr = len(in_specs), len(h_out_shape), len(scratch_shapes)
    s_in, s_out = len(stage.inputs), len(stage.outputs)

    def wrapped(*refs):
        h_in, st_in = refs[:n_in], refs[n_in:n_in + s_in]
        h_o, st_o = refs[n_in + s_in:n_in + s_in + n_out], refs[n_in + s_in + n_out:n_in + s_in + n_out + s_out]
        h_scr = refs[n_in + s_in + n_out + s_out:n_in + s_in + n_out + s_out + n_scr]
        send_sems, recv_sems = refs[n_in + s_in + n_out + s_out + n_scr:]
        if not grid:
            stage.run(st_in, st_o, send_sems, recv_sems, True)
            if body is not None:
                body(*h_in, *h_o, *h_scr)
            stage.run(st_in, st_o, send_sems, recv_sems, False)
            return
        ids = [pl.program_id(a) for a in range(len(grid))]
        first = functools.reduce(jnp.logical_and, [i == 0 for i in ids])
        last = functools.reduce(jnp.logical_and, [i == g - 1 for i, g in zip(ids, grid)])
        pl.when(first)(lambda: stage.run(st_in, st_o, send_sems, recv_sems, True))
        body(*h_in, *h_o, *h_scr)
        pl.when(last)(lambda: stage.run(st_in, st_o, send_sems, recv_sems, False))

    call = pl.pallas_call(
        wrapped, name=name, grid=grid, in_specs=list(in_specs) + stage.in_specs, out_specs=h_out_specs + stage.out_specs,
        out_shape=h_out_shape + stage.outputs, input_output_aliases={n_in + i: n_out + o for i, o in stage.aliases.items()},
        scratch_shapes=list(scratch_shapes) + [pltpu.SemaphoreType.DMA((stage.n_copies,)), pltpu.SemaphoreType.DMA((stage.n_copies,))],
        compiler_params=pltpu.CompilerParams(dimension_semantics=("arbitrary",) * len(grid), vmem_limit_bytes=VMEM_LIMIT_V7X),
    )

    def run(*operands):
        outs = call(*operands, *stage.inputs)
        host = outs[:n_out]
        return (host[0] if single else host), outs[n_out:]

    return run


def _run_stage(stage, name):
    return _hosted(stage, None, name=name, in_specs=[], out_specs=[], out_shape=[])()[1]


def _add_allgather8(stage, v):
    def build(ins, outs):
        x, y, c = _coords()
        me = 4 * x + 2 * y + c
        copies = [(ins[0], outs[0].at[me], None, None)]
        for r in range(1, N_DEV):
            peer = (_flip(x, r & 4), _flip(y, r & 2), _flip(c, r & 1))
            copies.append((ins[0], outs[0].at[me], peer, outs[0].at[me ^ r]))
        return copies

    return stage.add([v], VMEM_SPEC, [SDS((N_DEV,) + v.shape, v.dtype)], VMEM_SPEC, N_DEV, build)[0]


def _add_gather_ici(stage, shard):
    def build(ins, outs):
        x, y, c = _coords()
        k_me = 2 * x + y
        copies = [(ins[0], outs[0].at[k_me], (x, y, 1 - c), outs[0].at[k_me])]
        for px, py in _chips():
            copies.append((ins[0].at[c], outs[0].at[k_me, c], (px, py, c), outs[0].at[2 * px + py, c]))
        return copies

    return stage.add([shard], HBM_SPEC, [SDS((N_CHIPS,) + shard.shape, shard.dtype)], HBM_SPEC, N_CHIPS, build)[0]


def _add_gather_d2d(stage, whole):
    def build(ins, outs):
        x, y, c = _coords()
        return [(outs[0].at[2 * px + py, c], outs[0].at[2 * px + py, c], (x, y, 1 - c), outs[0].at[2 * px + py, 1 - c]) for px, py in _chips()]

    return stage.add([whole], HBM_SPEC, [SDS(whole.shape, whole.dtype)], HBM_SPEC, N_CHIPS - 1, build, alias=True)[0]


def _add_reduce_d2d(stage, grads):
    def build(ins, outs):
        x, y, c = _coords()
        return [(ins[0].at[k, 1 - c], outs[0].at[k], (x, y, 1 - c), outs[0].at[k]) for k in range(N_CHIPS)]

    return stage.add([grads], HBM_SPEC, [SDS((N_CHIPS,) + grads.shape[2:], grads.dtype)], HBM_SPEC, N_CHIPS, build)[0]


def _add_reduce_ici(stage, partial):
    def build(ins, outs):
        x, y, c = _coords()
        return [(ins[0].at[2 * px + py], outs[0].at[j], (px, py, c), outs[0].at[j]) for j, (px, py) in enumerate(_chips())]

    return stage.add([partial], HBM_SPEC, [SDS((N_CHIPS - 1,) + partial.shape[1:], partial.dtype)], HBM_SPEC, N_CHIPS - 1, build)[0]


def _add_swap_halves(stage, quarter):
    def build(ins, outs):
        x, y, c = _coords()
        return [(outs[0].at[c], outs[0].at[c], (x, y, 1 - c), outs[0].at[1 - c])]

    return stage.add([quarter], HBM_SPEC, [SDS(quarter.shape, quarter.dtype)], HBM_SPEC, 1, build, alias=True)[0]


def _sum_pair(grads, from_sibling, place, name):
    _, _, rows, cols = grads.shape
    tr = min(rows, 256)

    def body(place_ref, g_ref, s_ref, o_ref):
        o_ref[0] = (g_ref[0, 0].astype(F32) + s_ref[0].astype(F32)).astype(BF16)

    spec = pl.BlockSpec((1, tr, cols), lambda k, i, place_ref: (k, i, 0))
    return pl.pallas_call(
        body, name=name, out_shape=SDS((N_CHIPS, rows, cols), BF16),
        grid_spec=pltpu.PrefetchScalarGridSpec(
            num_scalar_prefetch=1, grid=(N_CHIPS, rows // tr),
            in_specs=[pl.BlockSpec((1, 1, tr, cols), lambda k, i, place_ref: (k, place_ref[0], i, 0)), spec], out_specs=spec),
        compiler_params=_params("parallel", "parallel"),
    )(place, grads, from_sibling)


def _sum_quarter(grads, from_sibling, from_chips, place, name):
    _, _, rows, cols = grads.shape
    tr = min(rows, 256)

    def body(place_ref, g_ref, s_ref, c_ref, o_ref):
        acc = g_ref[0, 0].astype(F32) + s_ref[0].astype(F32)
        for j in range(N_CHIPS - 1):
            acc = acc + c_ref[j].astype(F32)
        o_ref[0] = acc

    return pl.pallas_call(
        body, name=name, out_shape=SDS((2, rows, cols), F32),
        grid_spec=pltpu.PrefetchScalarGridSpec(
            num_scalar_prefetch=1, grid=(rows // tr,),
            in_specs=[pl.BlockSpec((1, 1, tr, cols), lambda i, place_ref: (place_ref[1], place_ref[0], i, 0)),
                      pl.BlockSpec((1, tr, cols), lambda i, place_ref: (place_ref[1], i, 0)),
                      pl.BlockSpec((N_CHIPS - 1, tr, cols), lambda i, place_ref: (0, i, 0))],
            out_specs=pl.BlockSpec((1, tr, cols), lambda i, place_ref: (place_ref[0], i, 0))),
        compiler_params=_params("parallel"),
    )(place, grads, from_sibling, from_chips)


def _row_spec(ts, cols, col_block=0):
    return pl.BlockSpec((ts, cols), lambda *g: (g[0], col_block))


def _vec_spec(rows, cols):
    return pl.BlockSpec((rows, cols), lambda *g: (0, 0))


def _matmul(a, w, *, transposed, reduce_blocks, name, tm, outs, finish, extra=(), prepare=None, zero_first=(), parts=1, stage=None):
    m_rows = a.shape[0]
    nb, r, c = w.shape
    kb = c if transposed else r
    dims = (((1,), (1,)), ((), ())) if transposed else (((1,), (0,)), ((), ()))
    assert a.shape[1] == (nb * kb if reduce_blocks else kb) and m_rows % tm == 0 and tm % parts == 0
    n_extra, sub = len(extra), tm // parts

    def body(a_ref, w_ref, *rest):
        extra_refs, out_refs = rest[:n_extra], rest[n_extra:]
        if zero_first:
            @pl.when(pl.program_id(0) == 0)
            def _():
                for i in zero_first:
                    out_refs[i][...] = jnp.zeros_like(out_refs[i])
        for j in range(parts):
            rows = pl.ds(j * sub, sub)
            ex = [ref.at[rows] if ref.shape[0] == tm else ref for ref in extra_refs]
            ou = [ref.at[rows] if ref.shape[0] == tm else ref for ref in out_refs]
            if reduce_blocks:
                acc = lax.dot_general(a_ref[rows, 0:kb], w_ref[0], dims, preferred_element_type=F32)
                for k in range(1, nb):
                    acc += lax.dot_general(a_ref[rows, k * kb:(k + 1) * kb], w_ref[k], dims, preferred_element_type=F32)
                finish(None, acc, ex, ou)
            else:
                lhs = a_ref[rows, :] if prepare is None else prepare(a_ref.at[rows], ex, ou)
                for k in range(nb):
                    finish(k, lax.dot_general(lhs, w_ref[k], dims, preferred_element_type=F32), ex, ou)

    return _hosted(
        stage, body, name=name, grid=(m_rows // tm,), out_shape=[s for s, _ in outs], out_specs=[s for _, s in outs],
        in_specs=[_row_spec(tm, a.shape[1]), pl.BlockSpec((nb, r, c), lambda *g: (0, 0, 0), pipeline_mode=pl.Buffered(1))] + [s for _, s in extra],
        compiler_params=_params("arbitrary"),
    )(a, w, *[e for e, _ in extra])


def _store_blocks(n):
    def finish(k, acc, extra_refs, out_refs):
        if k is None:
            out_refs[0][...] = acc.astype(out_refs[0].dtype)
        else:
            out_refs[0][:, k * n:(k + 1) * n] = acc.astype(out_refs[0].dtype)
    return finish


def _weight_grad(a, b, *, nb, name, tk, stage=None):
    s_rows, k1 = a.shape
    tn = b.shape[1] // nb
    assert k1 % tk == 0 and b.shape[1] % nb == 0

    def body(a_ref, b_ref, o_ref):
        o_ref[0] = lax.dot_general(a_ref[...], b_ref[...], (((0,), (0,)), ((), ())), preferred_element_type=F32).astype(o_ref.dtype)

    return _hosted(
        stage, body, name=name, grid=(nb, k1 // tk), out_shape=SDS((nb, k1, tn), BF16),
        in_specs=[pl.BlockSpec((s_rows, tk), lambda n, i: (0, i)), pl.BlockSpec((s_rows, tn), lambda n, i: (0, n))],
        out_specs=pl.BlockSpec((1, tk, tn), lambda n, i: (n, i, 0)), compiler_params=_params("parallel", "parallel"),
    )(a, b)


def _rms(x):
    r = lax.rsqrt(jnp.mean(x * x, axis=-1, keepdims=True) + EPS)
    return x * r, r


def _accumulate(acc_ref, row, value):
    acc_ref[row:row + 1, :] += jnp.sum(value, axis=0, keepdims=True)


def _gelu_parts(y):
    k0, k1 = 0.7978845608028654, 0.044715
    th = jnp.tanh(k0 * (y + k1 * (y * y * y)))
    gelu = 0.5 * y * (1.0 + th)
    dgelu = 0.5 * (1.0 + th) + 0.5 * y * (1.0 - th * th) * (k0 * (1.0 + 3.0 * k1 * (y * y)))
    return gelu, dgelu


def _one_minus_square(a, log_a):
    return -jnp.tanh(log_a) * (1.0 + a * a)


def _modulated_norm(x, mod_ref, g_ref, sc_row, sh_row):
    xh, _ = _rms(x)
    return ((xh * g_ref[...]) * (1.0 + mod_ref[sc_row:sc_row + 1, :]) + mod_ref[sh_row:sh_row + 1, :]).astype(BF16)


def _modulated_norm_backward(dh, x, dres, mod_ref, g_ref, acc_ref, sc_row):
    xh, r = _rms(x)
    _accumulate(acc_ref, 0, dh * (xh * g_ref[...]))
    _accumulate(acc_ref, 1, dh)
    dn = dh * (1.0 + mod_ref[sc_row:sc_row + 1, :])
    _accumulate(acc_ref, 2, dn * xh)
    dxh = dn * g_ref[...]
    return dres + r * (dxh - xh * jnp.mean(dxh * xh, axis=-1, keepdims=True))


def _seq_forward(proj, conv_w, vecs, w_a, w_x, w_p, *, ts, stage=None):
    s_rows = proj.shape[0]
    d = D_MODEL

    def body(x_ref, xh_ref, y_ref, u_ref, uh_ref, cw_ref, vec_ref, wa_ref, wx_ref, wp_ref,
             xr_o, hr_o, ga_o, p_o, pooled_o, carry, a_scr, b_scr):
        t = pl.program_id(0)

        @pl.when(t == 0)
        def _():
            carry[...] = jnp.zeros_like(carry)

        x = x_ref[...]
        halo = jnp.where(t == 0, 0.0, xh_ref[...])
        xx = jnp.concatenate([halo, x], axis=0)
        xr = vec_ref[0:1, :] + x * cw_ref[CONV_WIDTH - 1:CONV_WIDTH, :]
        for j in range(CONV_WIDTH - 1):
            xr = xr + pltpu.roll(xx, CONV_WIDTH - 1 - j, 0)[CONV_HALO:] * cw_ref[j:j + 1, :]
        xr_o[...] = xr
        xrb = xr.astype(BF16)
        zr = jnp.concatenate([jnp.dot(xrb[:, g * BLOCK:(g + 1) * BLOCK], wa_ref[g], preferred_element_type=F32)
                              for g in range(N_BLOCKS)], axis=1) + vec_ref[1:2, :]
        zi = jnp.concatenate([jnp.dot(xrb[:, g * BLOCK:(g + 1) * BLOCK], wx_ref[g], preferred_element_type=F32)
                              for g in range(N_BLOCKS)], axis=1) + vec_ref[2:3, :]
        r = jax.nn.sigmoid(zr)
        gate_i = jax.nn.sigmoid(zi)
        log_a = (-C_RG * r) * jax.nn.softplus(vec_ref[3:4, :])
        rows = lax.broadcasted_iota(jnp.int32, (ts, d), 0)
        a = jnp.exp(log_a)
        mult = jnp.where((rows == 0) & (t == 0), 1.0, jnp.sqrt(_one_minus_square(a, log_a)))
        a_scr[...] = a
        b_scr[...] = xr * gate_i * mult

        sub = lax.broadcasted_iota(jnp.int32, (8, d), 0)

        def chunk(i, h):
            a = a_scr[pl.ds(i * 8, 8), :]
            b = b_scr[pl.ds(i * 8, 8), :]
            for s in (1, 2, 4):
                keep = sub >= s
                b = jnp.where(keep, a * pltpu.roll(b, s, 0) + b, b)
                a = jnp.where(keep, a * pltpu.roll(a, s, 0), a)
            hh = a * h + b
            hr_o[pl.ds(i * 8, 8), :] = hh
            return hh[7:8, :]

        carry[...] = lax.fori_loop(0, ts // 8, chunk, carry[...])
        gelu, _ = _gelu_parts(y_ref[...])
        ga_o[...] = (gelu * hr_o[...]).astype(BF16)

        u = u_ref[...]
        uu = jnp.concatenate([jnp.where(t == 0, 0.0, uh_ref[...]), u], axis=0)
        pos = (rows[:, :BLOCK] + t * ts + 1).astype(F32)
        sums, have, parts = uu, 1, []
        for g, win in enumerate(POOL_WINDOWS):
            while have < win:
                sums = sums + pltpu.roll(sums, have, 0)
                have *= 2
            mean = sums[POOL_HALO:, g * BLOCK:(g + 1) * BLOCK] * (1.0 / jnp.minimum(pos, float(win)))
            parts.append(mean - u[:, g * BLOCK:(g + 1) * BLOCK])
        pb = jnp.concatenate(parts, axis=1).astype(BF16)
        p_o[...] = pb
        mixed = jnp.concatenate([jnp.dot(pb[:, g * BLOCK:(g + 1) * BLOCK], wp_ref[g], preferred_element_type=F32)
                                 for g in range(N_BLOCKS)], axis=1) + vec_ref[4:5, :]
        pooled_o[...] = (mixed * vec_ref[5:6, :]).astype(BF16)

    halo_x = pl.BlockSpec((CONV_HALO, d), lambda t: (jnp.maximum(t * (ts // CONV_HALO) - 1, 0), 0))
    halo_u = pl.BlockSpec((POOL_HALO, d), lambda t: (jnp.maximum(t * (ts // POOL_HALO) - 1, 0), 2))
    wspec = pl.BlockSpec((N_BLOCKS, BLOCK, BLOCK), lambda t: (0, 0, 0))
    return _hosted(
        stage, body, name="seq_forward", grid=(s_rows // ts,),
        out_shape=[SDS((s_rows, d), F32), SDS((s_rows, d), F32), SDS((s_rows, d), BF16), SDS((s_rows, d), BF16), SDS((s_rows, d), BF16)],
        in_specs=[_row_spec(ts, d, 0), halo_x, _row_spec(ts, d, 1), _row_spec(ts, d, 2), halo_u, _vec_spec(CONV_WIDTH, d), _vec_spec(8, d),
                  wspec, wspec, wspec],
        out_specs=[_row_spec(ts, d)] * 5,
        scratch_shapes=[pltpu.VMEM((1, d), F32), pltpu.VMEM((ts, d), F32), pltpu.VMEM((ts, d), F32)],
        compiler_params=_params("arbitrary"),
    )(proj, proj, proj, proj, proj, conv_w, vecs, w_a, w_x, w_p)


def _branches(ga, pooled, proj, w_a, w_b, *, ts, stage=None):
    s_rows, d = ga.shape

    def body(a_ref, p_ref, ga_ref, gb_ref, wa_ref, wb_ref, bra_o, brb_o, merged_o):
        bra = jnp.dot(a_ref[...], wa_ref[0], preferred_element_type=F32)
        brb = jnp.dot(p_ref[...], wb_ref[0], preferred_element_type=F32)
        bra_o[...] = bra
        brb_o[...] = brb
        merged_o[...] = (jax.nn.sigmoid(ga_ref[...]) * bra + jax.nn.sigmoid(gb_ref[...]) * brb).astype(BF16)

    rows = _row_spec(ts, d)
    wspec = pl.BlockSpec((1, d, d), lambda *g: (0, 0, 0), pipeline_mode=pl.Buffered(1))
    return _hosted(
        stage, body, name="branches", grid=(s_rows // ts,), out_shape=[SDS((s_rows, d), F32), SDS((s_rows, d), F32), SDS((s_rows, d), BF16)],
        in_specs=[rows, rows, _row_spec(ts, d, 3), _row_spec(ts, d, 4), wspec, wspec], out_specs=[rows, rows, rows],
        compiler_params=_params("arbitrary"),
    )(ga, pooled, proj, proj, w_a, w_b)


def _seq_backward(proj, xr, hr, p, dga, dpooled, dgab, conv_w, vecs, w_a, w_x, w_p, *, ts, stage=None):
    s_rows = proj.shape[0]
    d = D_MODEL
    n_t = s_rows // ts

    def block_dots(lhs, w_ref, dims):
        return jnp.concatenate([lax.dot_general(lhs[:, g * BLOCK:(g + 1) * BLOCK], w_ref[g], dims, preferred_element_type=F32)
                                for g in range(N_BLOCKS)], axis=1)

    def add_block_grads(dw_ref, lhs, rhs):
        for g in range(N_BLOCKS):
            dw_ref[g] += lax.dot_general(lhs[:, g * BLOCK:(g + 1) * BLOCK], rhs[:, g * BLOCK:(g + 1) * BLOCK],
                                         (((0,), (0,)), ((), ())), preferred_element_type=F32)

    nn, nt = (((1,), (0,)), ((), ())), (((1,), (1,)), ((), ()))

    def body(x_ref, y_ref, xr_ref, hr_ref, hh_ref, p_ref, dga_ref, dpl_ref, dgab_ref, cw_ref, vec_ref, wa_ref, wx_ref, wp_ref,
             dproj_o, acc_o, dwa_o, dwx_o, dwp_o, g_carry, dxr_carry, q_carry, a_scr, b_scr, g_scr):
        step = pl.program_id(0)
        t = n_t - 1 - step

        @pl.when(step == 0)
        def _():
            for ref in (acc_o, dwa_o, dwx_o, dwp_o, g_carry, dxr_carry, q_carry):
                ref[...] = jnp.zeros_like(ref)

        rows = lax.broadcasted_iota(jnp.int32, (ts, d), 0)
        start = (rows == 0) & (t == 0)
        xr = xr_ref[...]
        xrb = xr.astype(BF16)
        r = jax.nn.sigmoid(block_dots(xrb, wa_ref, nn) + vec_ref[1:2, :])
        gate_i = jax.nn.sigmoid(block_dots(xrb, wx_ref, nn) + vec_ref[2:3, :])
        sp = jax.nn.softplus(vec_ref[3:4, :])
        log_a = (-C_RG * r) * sp
        a = jnp.exp(log_a)
        m_square = _one_minus_square(a, log_a)
        mult = jnp.where(start, 1.0, jnp.sqrt(m_square))
        gelu, dgelu = _gelu_parts(y_ref[...])
        dga_t = dga_ref[...]
        hr_t = hr_ref[...]
        dy = dga_t * hr_t * dgelu

        a_scr[...] = jnp.where(rows == ts - 1, 1.0, pltpu.roll(a, ts - 1, 0))
        b_scr[...] = dga_t * gelu
        sub = lax.broadcasted_iota(jnp.int32, (8, d), 0)

        def chunk(i, g_next):
            at = pl.multiple_of((ts // 8 - 1 - i) * 8, 8)
            aa = a_scr[pl.ds(at, 8), :]
            bb = b_scr[pl.ds(at, 8), :]
            for s in (1, 2, 4):
                keep = sub < 8 - s
                bb = jnp.where(keep, bb + aa * pltpu.roll(bb, 8 - s, 0), bb)
                aa = jnp.where(keep, aa * pltpu.roll(aa, 8 - s, 0), aa)
            gg = aa * g_next + bb
            g_scr[pl.ds(at, 8), :] = gg
            return gg[0:1, :]

        g_first = lax.fori_loop(0, ts // 8, chunk, g_carry[...])
        g_carry[...] = a[0:1, :] * g_first
        g = g_scr[...]

        h_before = jnp.where(t == 0, 0.0, hh_ref[CONV_HALO - 1:CONV_HALO, :])
        h_prev = jnp.where(rows == 0, h_before, pltpu.roll(hr_t, 1, 0))
        dxr = g * gate_i * mult
        d_i = g * xr * mult
        d_mult = g * xr * gate_i
        dlog_a = (g * h_prev) * a - jnp.where(start, 0.0, d_mult * (a * a) * lax.rsqrt(m_square))
        dzr = (dlog_a * (-C_RG * sp)) * (r * (1.0 - r))
        dzi = d_i * (gate_i * (1.0 - gate_i))
        _accumulate(acc_o, 5, dzr)
        _accumulate(acc_o, 6, dzi)
        _accumulate(acc_o, 7, dlog_a * (-C_RG * r) * jax.nn.sigmoid(vec_ref[3:4, :]))
        dzrb, dzib = dzr.astype(BF16), dzi.astype(BF16)
        add_block_grads(dwa_o, xrb, dzrb)
        add_block_grads(dwx_o, xrb, dzib)
        dxr = dxr + block_dots(dzrb, wa_ref, nt) + block_dots(dzib, wx_ref, nt)

        x = x_ref[...]
        ext = jnp.concatenate([dxr, dxr_carry[...]], axis=0)
        dx = dxr * cw_ref[CONV_WIDTH - 1:CONV_WIDTH, :]
        _accumulate(acc_o, CONV_WIDTH - 1, x * dxr)
        for j in range(CONV_WIDTH - 1):
            ahead = pltpu.roll(ext, ts + CONV_HALO - (CONV_WIDTH - 1 - j), 0)[:ts]
            dx = dx + ahead * cw_ref[j:j + 1, :]
            _accumulate(acc_o, j, x * ahead)
        _accumulate(acc_o, 4, dxr)
        dxr_carry[...] = dxr[0:CONV_HALO, :]

        pb = p_ref[...]
        mixed = block_dots(pb, wp_ref, nn) + vec_ref[4:5, :]
        dpl = dpl_ref[...]
        _accumulate(acc_o, 9, dpl * mixed)
        dmixed = dpl * vec_ref[5:6, :]
        _accumulate(acc_o, 8, dmixed)
        dmb = dmixed.astype(BF16)
        add_block_grads(dwp_o, pb, dmb)
        dp = block_dots(dmb, wp_ref, nt)
        pos = (rows[:, :BLOCK] + t * ts + 1).astype(F32)
        q = jnp.concatenate([dp[:, g * BLOCK:(g + 1) * BLOCK] * (1.0 / jnp.minimum(pos, float(win))) for g, win in enumerate(POOL_WINDOWS)], axis=1)
        sums, have, parts = jnp.concatenate([q, q_carry[...]], axis=0), 1, []
        for g, win in enumerate(POOL_WINDOWS):
            while have < win:
                sums = sums + pltpu.roll(sums, ts + POOL_HALO - have, 0)
                have *= 2
            parts.append(sums[:ts, g * BLOCK:(g + 1) * BLOCK])
        du = jnp.concatenate(parts, axis=1) - dp
        q_carry[...] = q[0:POOL_HALO, :]

        dproj_o[:, 0:d] = dx.astype(BF16)
        dproj_o[:, d:2 * d] = dy.astype(BF16)
        dproj_o[:, 2 * d:3 * d] = du.astype(BF16)
        dproj_o[:, 3 * d:5 * d] = dgab_ref[...]

    def rev(cols, col_block=0):
        return pl.BlockSpec((ts, cols), lambda i: (n_t - 1 - i, col_block))

    halo_h = pl.BlockSpec((CONV_HALO, d), lambda i: (jnp.maximum((n_t - 1 - i) * (ts // CONV_HALO) - 1, 0), 0))
    wspec = pl.BlockSpec((N_BLOCKS, BLOCK, BLOCK), lambda i: (0, 0, 0))
    return _hosted(
        stage, body, name="seq_backward", grid=(n_t,),
        out_shape=[SDS((s_rows, 5 * d), BF16), SDS((16, d), F32)] + [SDS((N_BLOCKS, BLOCK, BLOCK), F32)] * 3,
        in_specs=[rev(d, 0), rev(d, 1), rev(d), rev(d), halo_h, rev(d), rev(d), rev(d), rev(2 * d), _vec_spec(CONV_WIDTH, d), _vec_spec(8, d),
                  wspec, wspec, wspec],
        out_specs=[rev(5 * d), _vec_spec(16, d), wspec, wspec, wspec],
        scratch_shapes=[pltpu.VMEM((1, d), F32), pltpu.VMEM((CONV_HALO, d), F32), pltpu.VMEM((POOL_HALO, d), F32),
                        pltpu.VMEM((ts, d), F32), pltpu.VMEM((ts, d), F32), pltpu.VMEM((ts, d), F32)],
        compiler_params=_params("arbitrary"),
    )(proj, proj, xr, hr, hr, p, dga, dpooled, dgab, conv_w, vecs, w_a, w_x, w_p)


def _adamw_math(w, g, m, v):
    m = ADAM_B1 * m + (1.0 - ADAM_B1) * g
    v = ADAM_B2 * v + (1.0 - ADAM_B2) * jnp.square(g)
    m_hat = m / (1.0 - ADAM_B1 ** ADAM_STEP)
    v_hat = v / (1.0 - ADAM_B2 ** ADAM_STEP)
    delta = -ADAM_LR * (m_hat / (jnp.sqrt(v_hat) + ADAM_EPS) + ADAM_WD * w)
    return delta, m, v


def _adamw(w, g, m, v, name):
    rows, cols = w.shape
    tr = min(rows, 256)

    def body(w_ref, g_ref, m_ref, v_ref, d_ref, mo_ref, vo_ref):
        d_ref[...], mo_ref[...], vo_ref[...] = _adamw_math(w_ref[...], g_ref[...], m_ref[...], v_ref[...])

    spec = pl.BlockSpec((tr, cols), lambda i: (i, 0))
    return pl.pallas_call(
        body, name=name, grid=(rows // tr,), out_shape=[SDS((rows, cols), F32)] * 3, in_specs=[spec] * 4, out_specs=[spec] * 3,
        compiler_params=_params("parallel"),
    )(w, g, m, v)


def _ada_forward(c_all, w_ada, b_cols):
    n_cols = w_ada.shape[1]

    def body(c_ref, w_ref, b_ref, act_ref, mod_ref):
        cv = c_ref[...]
        act = cv * jax.nn.sigmoid(cv)
        act_ref[...] = act
        mod_ref[...] = jnp.dot(act.astype(BF16), w_ref[...].astype(BF16), preferred_element_type=F32) + b_ref[...]

    return pl.pallas_call(
        body, name="ada_forward", out_shape=[SDS(c_all.shape, F32), SDS((N_DEV, n_cols), F32)],
        in_specs=[VMEM_SPEC] * 3, out_specs=[VMEM_SPEC] * 2, compiler_params=pltpu.CompilerParams(vmem_limit_bytes=VMEM_LIMIT_V7X),
    )(c_all, w_ada, b_cols)


def _ada_backward(act_t, dmod_cols, w, m, v):
    rows, cols = w.shape
    tn = 512

    def body(a_ref, dm_ref, w_ref, m_ref, v_ref, g_ref, d_ref, mo_ref, vo_ref):
        act = a_ref[...].astype(BF16).astype(F32)
        dm = dm_ref[...].astype(BF16).astype(F32)
        g = act[:, 0:1] * dm[0:1, :]
        for b in range(1, N_DEV):
            g = g + act[:, b:b + 1] * dm[b:b + 1, :]
        g_ref[...] = g
        d_ref[...], mo_ref[...], vo_ref[...] = _adamw_math(w_ref[...], g, m_ref[...], v_ref[...])

    spec = pl.BlockSpec((rows, tn), lambda j: (0, j))
    return pl.pallas_call(
        body, name="ada_backward", grid=(cols // tn,), out_shape=[SDS((rows, cols), F32)] * 4,
        in_specs=[pl.BlockSpec((rows, N_DEV), lambda j: (0, 0)), pl.BlockSpec((N_DEV, tn), lambda j: (0, j)), spec, spec, spec],
        out_specs=[spec] * 4, compiler_params=_params("parallel"),
    )(act_t, dmod_cols, w, m, v)


def _sum_rows(gathered, n_loss):
    n = gathered.shape[2]

    def body(g_ref, o_ref):
        acc = g_ref[0]
        for r in range(1, N_DEV):
            acc = acc + g_ref[r]
        o_ref[...] = acc
        o_ref[:, n - n_loss:n] = jnp.broadcast_to(jnp.sum(acc[:, n - n_loss:n], axis=1, keepdims=True), (1, n_loss))

    return pl.pallas_call(body, name="sum_rows", out_shape=SDS((1, n), F32), in_specs=[VMEM_SPEC], out_specs=VMEM_SPEC)(gathered)


def _adamw_vectors(total, offsets, separate, params):
    n_p = len(params)
    apart = [g for g in separate if g is not None]

    def body(*refs):
        total_ref, apart_refs = refs[0], list(refs[1:1 + len(apart)])
        ins, outs = refs[1 + len(apart):1 + len(apart) + 3 * n_p], refs[1 + len(apart) + 3 * n_p:]
        for i in range(n_p):
            w_ref, m_ref, v_ref = ins[3 * i:3 * i + 3]
            g = apart_refs.pop(0)[...] if offsets[i] is None else total_ref[:, offsets[i]:offsets[i] + w_ref.shape[1]]
            outs[4 * i][...] = g
            outs[4 * i + 1][...], outs[4 * i + 2][...], outs[4 * i + 3][...] = _adamw_math(w_ref[...], g, m_ref[...], v_ref[...])

    flat = [a for p in params for a in p]
    res = pl.pallas_call(
        body, name="adamw_vectors", out_shape=[SDS(p[0].shape, F32) for p in params for _ in range(4)],
        in_specs=[VMEM_SPEC] * (1 + len(apart) + len(flat)), out_specs=[VMEM_SPEC] * (4 * n_p),
    )(total, *apart, *flat)
    return [tuple(res[4 * i:4 * i + 4]) for i in range(n_p)]


def _blocks_to_pieces(w):
    nb, rows, n = w.shape
    q = rows // N_CHIPS
    return w.reshape(nb, N_CHIPS, q, n).transpose(1, 0, 2, 3).reshape(N_CHIPS, 2, nb // 2, q, n)


def _pieces_to_blocks(w):
    n_chips, _, half, q, n = w.shape
    return w.reshape(n_chips, 2 * half, q, n).transpose(1, 0, 2, 3).reshape(2 * half, n_chips * q, n)


def kernel(x, c, norm_mix_g, norm_mlp_g, w_ada, b_ada, w_in, conv_w, conv_b, w_rg_a, b_rg_a, w_rg_x, b_rg_x, a_param, w_branch_a, w_pool, b_pool, pool_scale, w_branch_b, w_out, w_up, w_down, final_g, loss_target, m_norm_mix_g, m_norm_mlp_g, m_w_ada, m_b_ada, m_w_in, m_conv_w, m_conv_b, m_w_rg_a, m_b_rg_a, m_w_rg_x, m_b_rg_x, m_a_param, m_w_branch_a, m_w_pool, m_b_pool, m_pool_scale, m_w_branch_b, m_w_out, m_w_up, m_w_down, m_final_g, v_norm_mix_g, v_norm_mlp_g, v_w_ada, v_b_ada, v_w_in, v_conv_w, v_conv_b, v_w_rg_a, v_b_rg_a, v_w_rg_x, v_b_rg_x, v_a_param, v_w_branch_a, v_w_pool, v_b_pool, v_pool_scale, v_w_branch_b, v_w_out, v_w_up, v_w_down, v_final_g):
    d = D_MODEL
    s_rows = x.shape[1]
    ts, tsq = min(512, s_rows), min(256, s_rows)
    xi, yi, ci = _coords()
    k_me, me = 2 * xi + yi, 4 * xi + 2 * yi + ci
    ada_cols = w_ada.shape[2]
    conv_cols = conv_w.shape[2]
    n_in, n_up = w_in.shape[2], w_up.shape[2]

    names = ("w_in", "w_up", "w_down", "w_a", "w_b", "w_out", "rg_a", "rg_x", "pool")
    mine = dict(zip(names, (w_in[0], w_up[0], w_down[0], w_branch_a[0], w_branch_b[0], w_out[0], w_rg_a[0], w_rg_x[0], w_pool[0])))
    moments_m = dict(zip(names, (m_w_in[0], m_w_up[0], m_w_down[0], m_w_branch_a[0], m_w_branch_b[0], m_w_out[0], m_w_rg_a[0], m_w_rg_x[0], m_w_pool[0])))
    moments_v = dict(zip(names, (v_w_in[0], v_w_up[0], v_w_down[0], v_w_branch_a[0], v_w_branch_b[0], v_w_out[0], v_w_rg_a[0], v_w_rg_x[0], v_w_pool[0])))
    block_weights, squares = ("rg_a", "rg_x", "pool"), ("w_a", "w_b", "w_out")
    place = jnp.stack([ci, k_me]).astype(jnp.int32)
    g_mix, g_mlp, g_fin = norm_mix_g, norm_mlp_g, final_g.reshape(1, d)
    x0, target = x[0], loss_target[0]
    wg = functools.partial(_weight_grad, tk=min(512, d))
    plain, back, back_sum = (dict(transposed=t, reduce_blocks=r) for t, r in ((False, False), (True, False), (True, True)))
    rows_d, vec8, vec1 = _row_spec(ts, d), _vec_spec(8, d), _vec_spec(1, d)
    tw = min(1024, s_rows)
    halves_of_tile = 2

    def out_rows(cols, dtype, t=ts):
        return SDS((s_rows, cols), dtype), _row_spec(t, cols)

    sums_out = (SDS((8, d), F32), vec8)

    def halves(n):
        return mine[n].astype(BF16).reshape(2, -1, mine[n].shape[-1])

    def blocks(w):
        return _pieces_to_blocks(w.reshape(N_CHIPS, 2, N_BLOCKS // 2, BLOCK // N_CHIPS, BLOCK))

    stage = _Stage()
    i_c, i_cw, i_in = _add_allgather8(stage, c), _add_allgather8(stage, conv_w[0]), _add_gather_ici(stage, halves("w_in"))
    got = _run_stage(stage, "exchange_c_conv_w_in")
    conv_full = got[i_cw][0::2].transpose(1, 0, 2).reshape(CONV_WIDTH, d)
    act_all, mod_cols = _ada_forward(got[i_c].reshape(N_DEV, d), w_ada[0], lax.dynamic_slice_in_dim(b_ada, k_me * ada_cols, ada_cols, axis=1))
    stage = _Stage()
    i_mod, i_in = _add_allgather8(stage, mod_cols), _add_gather_d2d(stage, got[i_in])
    i_blk = [_add_gather_ici(stage, halves(n)) for n in block_weights]
    got = _run_stage(stage, "exchange_mod_w_in_gates")
    mod_mine = lax.dynamic_index_in_dim(got[i_mod], me, axis=1, keepdims=False)[0::2]
    mod = jnp.concatenate([mod_mine.reshape(N_MOD, d), jnp.zeros((8 - N_MOD, d), F32)], axis=0)
    w_in_all = got[i_in].reshape(N_CHIPS, d, n_in)
    vecs = jnp.concatenate([conv_b, b_rg_a, b_rg_x, a_param, b_pool, pool_scale, jnp.zeros((2, d), F32)], axis=0)

    def norm_first(a_ref, extra_refs, out_refs):
        h = _modulated_norm(a_ref[...], extra_refs[0], extra_refs[1], 1, 0)
        out_refs[1][...] = h
        return h

    stage = _Stage()
    i_blk = [_add_gather_d2d(stage, got[i]) for i in i_blk]
    i_sq = [_add_gather_ici(stage, halves(n)) for n in squares]
    (proj, h1), got = _matmul(x0, w_in_all, name="proj_in", tm=ts, extra=[(mod, vec8), (g_mix, vec1)], prepare=norm_first,
                              outs=[out_rows(N_CHIPS * n_in, F32), out_rows(d, BF16)], finish=_store_blocks(n_in), parts=halves_of_tile,
                              stage=stage, **plain)
    w_rg_a_all, w_rg_x_all, w_pool_all = (blocks(got[i]) for i in i_blk)
    stage = _Stage()
    i_sq = [_add_gather_d2d(stage, got[i]) for i in i_sq]
    i_up = _add_gather_ici(stage, halves("w_up"))
    (xr, hr, ga, p, pooled), got = _seq_forward(proj, conv_full, vecs, w_rg_a_all, w_rg_x_all, w_pool_all, ts=tsq, stage=stage)
    w_a_all, w_b_all, w_out_all = (got[i].reshape(1, d, d) for i in i_sq)
    stage = _Stage()
    i_up, i_down = _add_gather_d2d(stage, got[i_up]), _add_gather_ici(stage, halves("w_down"))
    (br_a, br_b, merged), got = _branches(ga, pooled, proj, w_a_all, w_b_all, ts=ts, stage=stage)
    w_up_all = got[i_up].reshape(N_CHIPS, d, n_up)
    stage = _Stage()
    i_down = _add_gather_d2d(stage, got[i_down])

    def residual_norm(k, acc, extra_refs, out_refs):
        x_ref, mod_ref, g_ref = extra_refs
        out_refs[0][...] = acc
        x2_t = x_ref[...] + mod_ref[2:3, :] * acc
        out_refs[1][...] = x2_t
        out_refs[2][...] = _modulated_norm(x2_t, mod_ref, g_ref, 4, 3)

    (mo, x2, h2), got = _matmul(merged, w_out_all, name="mix_out", tm=ts, extra=[(x0, rows_d), (mod, vec8), (g_mlp, vec1)],
                                outs=[out_rows(d, F32), out_rows(d, F32), out_rows(d, BF16)], finish=residual_norm, parts=halves_of_tile,
                                stage=stage, transposed=False, reduce_blocks=True)
    w_down_all = got[i_down].reshape(N_CHIPS, D_FF // N_CHIPS, d)

    def relu_squared(k, acc, extra_refs, out_refs):
        out_refs[0][:, k * n_up:(k + 1) * n_up] = acc
        out_refs[1][:, k * n_up:(k + 1) * n_up] = jnp.square(jnp.maximum(acc, 0.0)).astype(BF16)

    up, ff = _matmul(h2, w_up_all, name="mlp_up", tm=ts, outs=[out_rows(D_FF, F32), out_rows(D_FF, BF16)], finish=relu_squared,
                     parts=halves_of_tile, **plain)

    def loss_head(k, acc, extra_refs, out_refs):
        x2_ref, mod_ref, g_ref, t_ref = extra_refs
        dx_ref, df_ref, acc_ref = out_refs
        xh, r = _rms(x2_ref[...] + mod_ref[5:6, :] * acc)
        err = xh * g_ref[...] - t_ref[...]
        dy = err / d
        dxh = dy * g_ref[...]
        dx3_t = r * (dxh - xh * jnp.mean(dxh * xh, axis=-1, keepdims=True))
        dx_ref[...] = dx3_t
        df_ref[...] = (dx3_t * mod_ref[5:6, :]).astype(BF16)
        _accumulate(acc_ref, 0, dy * xh)
        _accumulate(acc_ref, 1, dx3_t * acc)
        _accumulate(acc_ref, 2, jnp.square(err) * (0.5 / d))

    dx3, dffo, sums_head = _matmul(ff, w_down_all, name="mlp_down", tm=ts, extra=[(x2, rows_d), (mod, vec8), (g_fin, vec1), (target, rows_d)],
                                   outs=[out_rows(d, F32), out_rows(d, BF16), sums_out], finish=loss_head, zero_first=(2,), parts=halves_of_tile,
                                   transposed=False, reduce_blocks=True)

    partial_of, from_sibling, chip_sum, from_chips, half_done, quarter = {}, {}, {}, {}, {}, {}

    def publish(n, g):
        g = _blocks_to_pieces(g).astype(BF16) if n in block_weights else g
        partial_of[n] = g.reshape(N_CHIPS, 2, -1, mine[n].shape[-1])

    def exchange(to_sibling=(), to_chips=(), swap=()):
        st = _Stage()
        slots = [(n, from_sibling, _add_reduce_d2d(st, partial_of[n])) for n in to_sibling]
        slots += [(n, from_chips, _add_reduce_ici(st, chip_sum[n])) for n in to_chips]
        slots += [(n, quarter, _add_swap_halves(st, half_done[n])) for n in swap]
        return st, slots

    def collect(slots, outs):
        for n, where, i in slots:
            where[n] = outs[i]

    def sum_pairs(*ns):
        for n in ns:
            chip_sum[n] = _sum_pair(partial_of[n], from_sibling[n], place, "sum_pair_" + n)

    def sum_quarters(*ns):
        for n in ns:
            half_done[n] = _sum_quarter(partial_of[n], from_sibling[n], from_chips[n], place, "sum_quarter_" + n)

    publish("w_down", wg(ff, dffo, nb=1, name="grad_w_down"))

    def relu_squared_backward(k, acc, extra_refs, out_refs):
        cols = slice(k * n_up, (k + 1) * n_up)
        out_refs[0][:, cols] = (acc * (2.0 * jnp.maximum(extra_refs[0][:, cols], 0.0))).astype(BF16)

    stage, slots = exchange(to_sibling=["w_down"])
    (dup,), got = _matmul(dffo, w_down_all, name="d_mlp_down", tm=ts, extra=[(up, _row_spec(ts, D_FF))], outs=[out_rows(D_FF, BF16)],
                          finish=relu_squared_backward, parts=halves_of_tile, stage=stage, **back)
    collect(slots, got)
    sum_pairs("w_down")
    stage, slots = exchange(to_chips=["w_down"])
    g_up, got = wg(h2, dup, nb=N_CHIPS, name="grad_w_up", stage=stage)
    collect(slots, got)
    publish("w_up", g_up)
    sum_quarters("w_down")

    def norm_mlp_backward(k, acc, extra_refs, out_refs):
        x2_ref, dres_ref, mod_ref, g_ref, mo_ref = extra_refs
        dx_ref, dmo_ref, acc_ref = out_refs
        dx =_modulated_norm_backward(acc, x2_ref[...], dres_ref[...], mod_ref, g_ref, acc_ref, 4)
        dx_ref[...] = dx
        dmo_ref[...] = (dx * mod_ref[2:3, :]).astype(BF16)
        _accumulate(acc_ref, 3, dx * mo_ref[...])

    stage, slots = exchange(to_sibling=["w_up"], swap=["w_down"])
    (dx2, dmo, sums_mlp), got = _matmul(dup, w_up_all, name="d_mlp_up", tm=ts, stage=stage, finish=norm_mlp_backward,
                                        extra=[(x2, rows_d), (dx3, rows_d), (mod, vec8), (g_mlp, vec1), (mo, rows_d)],
                                        outs=[out_rows(d, F32), out_rows(d, BF16), sums_out], zero_first=(2,), parts=halves_of_tile, **back_sum)
    collect(slots, got)
    sum_pairs("w_up")
    publish("w_out", wg(merged, dmo, nb=1, name="grad_w_out"))

    def merge_backward(k, acc, extra_refs, out_refs):
        ga_ref, gb_ref, a_ref, b_ref = extra_refs
        sa, sb = jax.nn.sigmoid(ga_ref[...]), jax.nn.sigmoid(gb_ref[...])
        out_refs[0][...] = (acc * sa).astype(BF16)
        out_refs[1][...] = (acc * sb).astype(BF16)
        out_refs[2][:, 0:d] = (acc * a_ref[...] * (sa * (1.0 - sa))).astype(BF16)
        out_refs[2][:, d:2 * d] = (acc * b_ref[...] * (sb * (1.0 - sb))).astype(BF16)

    stage, slots = exchange(to_sibling=["w_out"])
    (dbr_a, dbr_b, dgab), got = _matmul(dmo, w_out_all, name="d_mix_out", tm=ts, stage=stage, finish=merge_backward,
                                        extra=[(proj, _row_spec(ts, d, 3)), (proj, _row_spec(ts, d, 4)), (br_a, rows_d), (br_b, rows_d)],
                                        outs=[out_rows(d, BF16), out_rows(d, BF16), out_rows(2 * d, BF16)], parts=halves_of_tile, **back_sum)
    collect(slots, got)
    publish("w_a", wg(ga, dbr_a, nb=1, name="grad_w_branch_a"))
    publish("w_b", wg(pooled, dbr_b, nb=1, name="grad_w_branch_b"))
    stage, slots = exchange(to_sibling=["w_a", "w_b"])
    (dga,), got = _matmul(dbr_a, w_a_all, name="d_branch_a", tm=tw, outs=[out_rows(d, F32, tw)], finish=_store_blocks(d), stage=stage, **back_sum)
    collect(slots, got)
    dpooled, = _matmul(dbr_b, w_b_all, name="d_branch_b", tm=tw, outs=[out_rows(d, F32, tw)], finish=_store_blocks(d), **back_sum)
    sum_pairs(*squares)
    stage, slots = exchange(to_chips=("w_up",) + squares)
    (dproj, sums_seq, d_rg_a, d_rg_x, d_pool), got = _seq_backward(proj, xr, hr, p, dga, dpooled, dgab, conv_full, vecs,
                                                                   w_rg_a_all, w_rg_x_all, w_pool_all, ts=tsq, stage=stage)
    collect(slots, got)
    sum_quarters("w_up", *squares)
    for n, g in zip(block_weights, (d_rg_a, d_rg_x, d_pool)):
        publish(n, g)
    stage, slots = exchange(to_sibling=block_weights, swap=("w_up",) + squares)
    g_in, got = wg(h1, dproj, nb=N_CHIPS, name="grad_w_in", stage=stage)
    collect(slots, got)
    publish("w_in", g_in)
    sum_pairs(*block_weights)
    stage, slots = exchange(to_sibling=["w_in"], to_chips=block_weights)
    collect(slots, _run_stage(stage, "exchange_w_in_gates"))
    sum_pairs("w_in")
    sum_quarters(*block_weights)

    def norm_mix_backward(k, acc, extra_refs, out_refs):
        x_ref, dres_ref, mod_ref, g_ref = extra_refs
        out_refs[0][...] = _modulated_norm_backward(acc, x_ref[...], dres_ref[...], mod_ref, g_ref, out_refs[1], 1)

    stage, slots = exchange(to_chips=["w_in"], swap=block_weights)
    (grad_x, sums_mix), got = _matmul(dproj, w_in_all, name="d_proj_in", tm=ts, stage=stage, finish=norm_mix_backward,
                                      extra=[(x0, rows_d), (dx2, rows_d), (mod, vec8), (g_mix, vec1)],
                                      outs=[out_rows(d, F32), sums_out], zero_first=(1,), parts=halves_of_tile, **back_sum)
    collect(slots, got)
    sum_quarters("w_in")

    dmod = jnp.concatenate([sums_mix[1:2], sums_mix[0:1], sums_mlp[3:4], sums_mlp[1:2], sums_mlp[0:1], sums_head[1:2]], axis=1)
    row = jnp.concatenate([sums_mix[2:3], sums_mlp[2:3], sums_seq[4:5], sums_seq[5:6], sums_seq[6:7], sums_seq[7:8], sums_seq[8:9],
                           sums_seq[9:10], sums_head[0:1], sums_seq[0:CONV_WIDTH].reshape(1, CONV_WIDTH * d), dmod, sums_head[2:3]], axis=1)
    n_vec, n_conv, n_mod = 9 * d, CONV_WIDTH * d, N_MOD * d
    stage, slots = exchange(swap=["w_in"])
    i_rows = _add_allgather8(stage, row)
    got = _run_stage(stage, "exchange_sums_w_in")
    collect(slots, got)
    rows_all = got[i_rows]
    total = _sum_rows(rows_all, d)
    loss = total[0, n_vec + n_conv + n_mod]
    dmod_all = rows_all[:, 0, n_vec + n_conv:n_vec + n_conv + n_mod]
    g_ada, d_ada, m_ada, v_ada = _ada_backward(act_all.T, lax.dynamic_slice_in_dim(dmod_all, k_me * ada_cols, ada_cols, axis=1),
                                               w_ada[0], m_w_ada[0], v_w_ada[0])
    g_conv = lax.dynamic_slice_in_dim(total[:, n_vec:n_vec + n_conv].reshape(CONV_WIDTH, d), k_me * conv_cols, conv_cols, axis=1)
    vec_names = ("norm_mix_g", "norm_mlp_g", "conv_b", "b_rg_a", "b_rg_x", "a_param", "b_pool", "pool_scale", "final_g", "conv_w", "b_ada")
    vec_params = [(norm_mix_g, m_norm_mix_g, v_norm_mix_g), (norm_mlp_g, m_norm_mlp_g, v_norm_mlp_g), (conv_b, m_conv_b, v_conv_b),
                  (b_rg_a, m_b_rg_a, v_b_rg_a), (b_rg_x, m_b_rg_x, v_b_rg_x), (a_param, m_a_param, v_a_param), (b_pool, m_b_pool, v_b_pool),
                  (pool_scale, m_pool_scale, v_pool_scale), tuple(a.reshape(1, d) for a in (final_g, m_final_g, v_final_g)),
                  (conv_w[0], m_conv_w[0], v_conv_w[0]), (b_ada, m_b_ada, v_b_ada)]
    offsets = [i * d for i in range(9)] + [None, n_vec + n_conv]
    vec_res = _adamw_vectors(total, offsets, [None] * 9 + [g_conv, None], vec_params)
    vec_out = {n: [r.reshape(shape) for r in res] for n, res, shape in zip(
        vec_names, vec_res, [(1, d)] * 8 + [(d,), (1, CONV_WIDTH, conv_cols), (1, N_MOD * d)])}

    big_out = {}
    for n in names:
        shape2 = (-1, mine[n].shape[-1])
        g2 = quarter[n].reshape(shape2)
        res = (g2,) + tuple(_adamw(mine[n].reshape(shape2), g2, moments_m[n].reshape(shape2), moments_v[n].reshape(shape2), "adamw_" + n))
        big_out[n] = [r.reshape((1,) + mine[n].shape) for r in res]

    key = {"w_in": "w_in", "w_rg_a": "rg_a", "w_rg_x": "rg_x", "w_branch_a": "w_a", "w_pool": "pool", "w_branch_b": "w_b", "w_out": "w_out",
           "w_up": "w_up", "w_down": "w_down"}
    order = ("norm_mix_g", "norm_mlp_g", "w_ada", "b_ada", "w_in", "conv_w", "conv_b", "w_rg_a", "b_rg_a", "w_rg_x", "b_rg_x", "a_param",
             "w_branch_a", "w_pool", "b_pool", "pool_scale", "w_branch_b", "w_out", "w_up", "w_down", "final_g")
    ada_out = [g_ada[None], d_ada[None], m_ada[None], v_ada[None]]
    outs = [loss, grad_x[None]]
    for kind in range(4):
        for nme in order:
            outs.append(ada_out[kind] if nme == "w_ada" else big_out[key[nme]][kind] if nme in key else vec_out[nme][kind])
    return tuple(outs)
```

```python
import functools

import jax
import jax.numpy as jnp
from jax import lax
from jax.experimental import pallas as pl
from jax.experimental.pallas import tpu as pltpu

F32, BF16 = jnp.float32, jnp.bfloat16
D_MODEL = 1024
D_FF = 4 * D_MODEL
N_BLOCKS = 4
BLOCK = D_MODEL // N_BLOCKS
CONV_WIDTH = 4
POOL_WINDOWS = (2, 4, 8, 16)
CONV_HALO = 8
POOL_HALO = 16
N_MOD = 6
EPS = 1e-6
C_RG = 8.0
ADAM_LR, ADAM_B1, ADAM_B2, ADAM_EPS, ADAM_WD, ADAM_STEP = 0.001, 0.9, 0.999, 1e-08, 0.01, 10
N_DEV = 8
N_CHIPS = 4
VMEM_LIMIT_V7X = 56 * 2**20
MESH = pl.DeviceIdType.MESH
SDS = jax.ShapeDtypeStruct
HBM_SPEC = pl.BlockSpec(memory_space=pltpu.HBM)
VMEM_SPEC = pl.BlockSpec(memory_space=pltpu.VMEM)


def _params(*semantics):
    return pltpu.CompilerParams(dimension_semantics=semantics, vmem_limit_bytes=VMEM_LIMIT_V7X)


def _coords():
    return lax.axis_index("x"), lax.axis_index("y"), lax.axis_index("c")


def _flip(v, bit):
    return 1 - v if bit else v


def _chips():
    x, y, _ = _coords()
    return [(1 - x, y), (x, 1 - y), (1 - x, 1 - y)]


class _Stage:
    def __init__(self):
        self.inputs, self.in_specs, self.outputs, self.out_specs, self.aliases = [], [], [], [], {}
        self.parts, self.n_copies = [], 0

    def add(self, inputs, in_spec, outputs, out_spec, n_copies, build, alias=False):
        i0, o0 = len(self.inputs), len(self.outputs)
        self.inputs += inputs
        self.in_specs += [in_spec] * len(inputs)
        self.outputs += outputs
        self.out_specs += [out_spec] * len(outputs)
        if alias:
            self.aliases.update({i0 + i: o0 + i for i in range(len(inputs))})
        self.parts.append((build, i0, len(inputs), o0, len(outputs)))
        self.n_copies += n_copies
        return list(range(o0, o0 + len(outputs)))

    def copies(self, in_refs, out_refs):
        out = []
        for build, i0, ni, o0, no in self.parts:
            out += build(in_refs[i0:i0 + ni], out_refs[o0:o0 + no])
        assert len(out) == self.n_copies
        return out

    def run(self, in_refs, out_refs, send_sems, recv_sems, start):
        x, y, c = _coords()
        for s, (src, dst, to, landing) in enumerate(self.copies(in_refs, out_refs)):
            if to is None:
                cp = pltpu.make_async_copy(src, dst, send_sems.at[s])
                cp.start() if start else cp.wait()
                continue
            cp = pltpu.make_async_remote_copy(src_ref=src, dst_ref=dst, send_sem=send_sems.at[s], recv_sem=recv_sems.at[s],
                                              device_id=to, device_id_type=MESH)
            if start:
                cp.start()
            else:
                pltpu.make_async_remote_copy(src_ref=landing, dst_ref=landing, send_sem=send_sems.at[s], recv_sem=recv_sems.at[s],
                                             device_id=(x, y, c), device_id_type=MESH).wait_recv()
                cp.wait_send()


def _hosted(stage, body, *, name, in_specs, out_specs, out_shape, grid=(), scratch_shapes=(), compiler_params=None):
    if stage is None:
        return pl.pallas_call(body, name=name, grid=grid, in_specs=in_specs, out_specs=out_specs, out_shape=out_shape,
                              scratch_shapes=list(scratch_shapes), compiler_params=compiler_params)
    single = not isinstance(out_shape, (list, tuple))
    h_out_shape = [out_shape] if single else list(out_shape)
    h_out_specs = [out_specs] if single else list(out_specs)
    n_in, n_out, n_scr = len(in_specs), len(h_out_shape), len(scratch_shapes)
    s_in, s_out = len(stage.inputs), len(stage.outputs)

    def wrapped(*refs):
        h_in, st_in = refs[:n_in], refs[n_in:n_in + s_in]
        h_o, st_o = refs[n_in + s_in:n_in + s_in + n_out], refs[n_in + s_in + n_out:n_in + s_in + n_out + s_out]
        h_scr = refs[n_in + s_in + n_out + s_out:n_in + s_in + n_out + s_out + n_scr]
        send_sems, recv_sems = refs[n_in + s_in + n_out + s_out + n_scr:]
        if not grid:
            stage.run(st_in, st_o, send_sems, recv_sems, True)
            if body is not None:
                body(*h_in, *h_o, *h_scr)
            stage.run(st_in, st_o, send_sems, recv_sems, False)
            return
        ids = [pl.program_id(a) for a in range(len(grid))]
        first = functools.reduce(jnp.logical_and, [i == 0 for i in ids])
        last = functools.reduce(jnp.logical_and, [i == g - 1 for i, g in zip(ids, grid)])
        pl.when(first)(lambda: stage.run(st_in, st_o, send_sems, recv_sems, True))
        body(*h_in, *h_o, *h_scr)
        pl.when(last)(lambda: stage.run(st_in, st_o, send_sems, recv_sems, False))

    call = pl.pallas_call(
        wrapped, name=name, grid=grid, in_specs=list(in_specs) + stage.in_specs, out_specs=h_out_specs + stage.out_specs,
        out_shape=h_out_shape + stage.outputs, input_output_aliases={n_in + i: n_out + o for i, o in stage.aliases.items()},
        scratch_shapes=list(scratch_shapes) + [pltpu.SemaphoreType.DMA((stage.n_copies,)), pltpu.SemaphoreType.DMA((stage.n_copies,))],
        compiler_params=pltpu.CompilerParams(dimension_semantics=("arbitrary",) * len(grid), vmem_limit_bytes=VMEM_LIMIT_V7X),
    )

    def run(*operands):
        outs = call(*operands, *stage.inputs)
        host = outs[:n_out]
        return (host[0] if single else host), outs[n_out:]

    return run


def _run_stage(stage, name):
    return _hosted(stage, None, name=name, in_specs=[], out_specs=[], out_shape=[])()[1]


def _add_allgather8(stage, v):
    def build(ins, outs):
        x, y, c = _coords()
        me = 4 * x + 2 * y + c
        copies = [(ins[0], outs[0].at[me], None, None)]
        for r in range(1, N_DEV):
            peer = (_flip(x, r & 4), _flip(y, r & 2), _flip(c, r & 1))
            copies.append((ins[0], outs[0].at[me], peer, outs[0].at[me ^ r]))
        return copies

    return stage.add([v], VMEM_SPEC, [SDS((N_DEV,) + v.shape, v.dtype)], VMEM_SPEC, N_DEV, build)[0]


def _add_gather_ici(stage, shard, rows=None, whole=None):
    first, count = rows or (0, shard.shape[1])

    def build(ins, outs):
        x, y, c = _coords()
        k_me = 2 * x + y
        part = pl.ds(first, count)
        copies = [] if whole is not None else [(ins[0], outs[0].at[k_me], (x, y, 1 - c), outs[0].at[k_me])]
        for px, py in _chips():
            copies.append((ins[0].at[c, part], outs[0].at[k_me, c, part], (px, py, c), outs[0].at[2 * px + py, c, part]))
        return copies

    if whole is None:
        return stage.add([shard], HBM_SPEC, [SDS((N_CHIPS,) + shard.shape, shard.dtype)], HBM_SPEC, N_CHIPS, build)[0]

    def build_into(ins, outs):
        return build(ins[1:], outs)

    i0 = len(stage.inputs)
    out = stage.add([whole, shard], HBM_SPEC, [SDS(whole.shape, whole.dtype)], HBM_SPEC, N_CHIPS - 1, build_into)[0]
    stage.aliases[i0] = out
    return out


def _add_gather_d2d(stage, whole):
    def build(ins, outs):
        x, y, c = _coords()
        return [(outs[0].at[2 * px + py, c], outs[0].at[2 * px + py, c], (x, y, 1 - c), outs[0].at[2 * px + py, 1 - c]) for px, py in _chips()]

    return stage.add([whole], HBM_SPEC, [SDS(whole.shape, whole.dtype)], HBM_SPEC, N_CHIPS - 1, build, alias=True)[0]


def _add_reduce_d2d(stage, grads):
    def build(ins, outs):
        x, y, c = _coords()
        return [(ins[0].at[k, 1 - c], outs[0].at[k], (x, y, 1 - c), outs[0].at[k]) for k in range(N_CHIPS)]

    return stage.add([grads], HBM_SPEC, [SDS((N_CHIPS,) + grads.shape[2:], grads.dtype)], HBM_SPEC, N_CHIPS, build)[0]


def _add_reduce_ici(stage, partial):
    def build(ins, outs):
        x, y, c = _coords()
        return [(ins[0].at[2 * px + py], outs[0].at[j], (px, py, c), outs[0].at[j]) for j, (px, py) in enumerate(_chips())]

    return stage.add([partial], HBM_SPEC, [SDS((N_CHIPS - 1,) + partial.shape[1:], partial.dtype)], HBM_SPEC, N_CHIPS - 1, build)[0]


def _add_swap_halves(stage, quarter):
    def build(ins, outs):
        x, y, c = _coords()
        return [(outs[0].at[c], outs[0].at[c], (x, y, 1 - c), outs[0].at[1 - c])]

    return stage.add([quarter], HBM_SPEC, [SDS(quarter.shape, quarter.dtype)], HBM_SPEC, 1, build, alias=True)[0]


def _sum_pair(grads, from_sibling, place, name):
    _, _, rows, cols = grads.shape
    tr = min(rows, 256)

    def body(place_ref, g_ref, s_ref, o_ref):
        o_ref[0] = (g_ref[0, 0].astype(F32) + s_ref[0].astype(F32)).astype(BF16)

    spec = pl.BlockSpec((1, tr, cols), lambda k, i, place_ref: (k, i, 0))
    return pl.pallas_call(
        body, name=name, out_shape=SDS((N_CHIPS, rows, cols), BF16),
        grid_spec=pltpu.PrefetchScalarGridSpec(
            num_scalar_prefetch=1, grid=(N_CHIPS, rows // tr),
            in_specs=[pl.BlockSpec((1, 1, tr, cols), lambda k, i, place_ref: (k, place_ref[0], i, 0)), spec], out_specs=spec),
        compiler_params=_params("parallel", "parallel"),
    )(place, grads, from_sibling)


def _sum_quarter(grads, from_sibling, from_chips, place, name):
    _, _, rows, cols = grads.shape
    tr = min(rows, 256)

    def body(place_ref, g_ref, s_ref, c_ref, o_ref):
        acc = g_ref[0, 0].astype(F32) + s_ref[0].astype(F32)
        for j in range(N_CHIPS - 1):
            acc = acc + c_ref[j].astype(F32)
        o_ref[0] = acc

    return pl.pallas_call(
        body, name=name, out_shape=SDS((2, rows, cols), F32),
        grid_spec=pltpu.PrefetchScalarGridSpec(
            num_scalar_prefetch=1, grid=(rows // tr,),
            in_specs=[pl.BlockSpec((1, 1, tr, cols), lambda i, place_ref: (place_ref[1], place_ref[0], i, 0)),
                      pl.BlockSpec((1, tr, cols), lambda i, place_ref: (place_ref[1], i, 0)),
                      pl.BlockSpec((N_CHIPS - 1, tr, cols), lambda i, place_ref: (0, i, 0))],
            out_specs=pl.BlockSpec((1, tr, cols), lambda i, place_ref: (place_ref[0], i, 0))),
        compiler_params=_params("parallel"),
    )(place, grads, from_sibling, from_chips)


def _row_spec(ts, cols, col_block=0):
    return pl.BlockSpec((ts, cols), lambda *g: (g[0], col_block))


def _vec_spec(rows, cols):
    return pl.BlockSpec((rows, cols), lambda *g: (0, 0))


def _matmul(a, w, *, transposed, reduce_blocks, name, tm, outs, finish, extra=(), prepare=None, zero_first=(), parts=1, stage=None):
    m_rows = a.shape[0]
    nb, r, c = w.shape
    kb = c if transposed else r
    dims = (((1,), (1,)), ((), ())) if transposed else (((1,), (0,)), ((), ()))
    assert a.shape[1] == (nb * kb if reduce_blocks else kb) and m_rows % tm == 0 and tm % parts == 0
    n_extra, sub = len(extra), tm // parts

    def body(a_ref, w_ref, *rest):
        extra_refs, out_refs = rest[:n_extra], rest[n_extra:]
        if zero_first:
            @pl.when(pl.program_id(0) == 0)
            def _():
                for i in zero_first:
                    out_refs[i][...] = jnp.zeros_like(out_refs[i])
        def views(j):
            rows = pl.ds(j * sub, sub)
            return rows, [ref.at[rows] if ref.shape[0] == tm else ref for ref in extra_refs], [ref.at[rows] if ref.shape[0] == tm else ref for ref in out_refs]

        def product(j):
            rows, ex, ou = views(j)
            if reduce_blocks:
                acc = lax.dot_general(a_ref[rows, 0:kb], w_ref[0], dims, preferred_element_type=F32)
                for k in range(1, nb):
                    acc += lax.dot_general(a_ref[rows, k * kb:(k + 1) * kb], w_ref[k], dims, preferred_element_type=F32)
                return [(None, acc)]
            lhs = a_ref[rows, :] if prepare is None else prepare(a_ref.at[rows], ex, ou)
            return [(k, lax.dot_general(lhs, w_ref[k], dims, preferred_element_type=F32)) for k in range(nb)]

        pending = product(0)
        for j in range(parts):
            ahead = product(j + 1) if j + 1 < parts else None
            _, ex, ou = views(j)
            for k, acc in pending:
                finish(k, acc, ex, ou)
            pending = ahead

    return _hosted(
        stage, body, name=name, grid=(m_rows // tm,), out_shape=[s for s, _ in outs], out_specs=[s for _, s in outs],
        in_specs=[_row_spec(tm, a.shape[1]), pl.BlockSpec((nb, r, c), lambda *g: (0, 0, 0), pipeline_mode=pl.Buffered(1))] + [s for _, s in extra],
        compiler_params=_params("arbitrary"),
    )(a, w, *[e for e, _ in extra])


def _store_blocks(n):
    def finish(k, acc, extra_refs, out_refs):
        if k is None:
            out_refs[0][...] = acc.astype(out_refs[0].dtype)
        else:
            out_refs[0][:, k * n:(k + 1) * n] = acc.astype(out_refs[0].dtype)
    return finish


def _weight_grad(a, b, *, nb, name, tk, stage=None):
    s_rows, k1 = a.shape
    tn = b.shape[1] // nb
    assert k1 % tk == 0 and b.shape[1] % nb == 0

    def body(a_ref, b_ref, o_ref):
        o_ref[0] = lax.dot_general(a_ref[...], b_ref[...], (((0,), (0,)), ((), ())), preferred_element_type=F32).astype(o_ref.dtype)

    return _hosted(
        stage, body, name=name, grid=(nb, k1 // tk), out_shape=SDS((nb, k1, tn), BF16),
        in_specs=[pl.BlockSpec((s_rows, tk), lambda n, i: (0, i)), pl.BlockSpec((s_rows, tn), lambda n, i: (0, n))],
        out_specs=pl.BlockSpec((1, tk, tn), lambda n, i: (n, i, 0)), compiler_params=_params("parallel", "parallel"),
    )(a, b)


def _rms(x):
    r = lax.rsqrt(jnp.mean(x * x, axis=-1, keepdims=True) + EPS)
    return x * r, r


def _accumulate(acc_ref, row, value):
    acc_ref[row:row + 1, :] += jnp.sum(value, axis=0, keepdims=True)


def _gelu_parts(y):
    k0, k1 = 0.7978845608028654, 0.044715
    th = jnp.tanh(k0 * (y + k1 * (y * y * y)))
    gelu = 0.5 * y * (1.0 + th)
    dgelu = 0.5 * (1.0 + th) + 0.5 * y * (1.0 - th * th) * (k0 * (1.0 + 3.0 * k1 * (y * y)))
    return gelu, dgelu


def _one_minus_square(a, log_a):
    return -jnp.tanh(log_a) * (1.0 + a * a)


def _modulated_norm(x, mod_ref, g_ref, sc_row, sh_row):
    xh, _ = _rms(x)
    return ((xh * g_ref[...]) * (1.0 + mod_ref[sc_row:sc_row + 1, :]) + mod_ref[sh_row:sh_row + 1, :]).astype(BF16)


def _modulated_norm_backward(dh, x, dres, mod_ref, g_ref, acc_ref, sc_row):
    xh, r = _rms(x)
    _accumulate(acc_ref, 0, dh * (xh * g_ref[...]))
    _accumulate(acc_ref, 1, dh)
    dn = dh * (1.0 + mod_ref[sc_row:sc_row + 1, :])
    _accumulate(acc_ref, 2, dn * xh)
    dxh = dn * g_ref[...]
    return dres + r * (dxh - xh * jnp.mean(dxh * xh, axis=-1, keepdims=True))


def _seq_forward(proj, conv_w, vecs, w_a, w_x, w_p, *, ts, stage=None):
    s_rows = proj.shape[0]
    d = D_MODEL

    def body(x_ref, xh_ref, y_ref, u_ref, uh_ref, cw_ref, vec_ref, wa_ref, wx_ref, wp_ref,
             xr_o, hr_o, ga_o, p_o, pooled_o, carry, a_scr, b_scr):
        t = pl.program_id(0)

        @pl.when(t == 0)
        def _():
            carry[...] = jnp.zeros_like(carry)

        x = x_ref[...]
        halo = jnp.where(t == 0, 0.0, xh_ref[...])
        xx = jnp.concatenate([halo, x], axis=0)
        xr = vec_ref[0:1, :] + x * cw_ref[CONV_WIDTH - 1:CONV_WIDTH, :]
        for j in range(CONV_WIDTH - 1):
            xr = xr + pltpu.roll(xx, CONV_WIDTH - 1 - j, 0)[CONV_HALO:] * cw_ref[j:j + 1, :]
        xr_o[...] = xr
        xrb = xr.astype(BF16)
        zr = jnp.concatenate([jnp.dot(xrb[:, g * BLOCK:(g + 1) * BLOCK], wa_ref[g], preferred_element_type=F32)
                              for g in range(N_BLOCKS)], axis=1) + vec_ref[1:2, :]
        zi = jnp.concatenate([jnp.dot(xrb[:, g * BLOCK:(g + 1) * BLOCK], wx_ref[g], preferred_element_type=F32)
                              for g in range(N_BLOCKS)], axis=1) + vec_ref[2:3, :]
        r = jax.nn.sigmoid(zr)
        gate_i = jax.nn.sigmoid(zi)
        log_a = (-C_RG * r) * jax.nn.softplus(vec_ref[3:4, :])
        rows = lax.broadcasted_iota(jnp.int32, (ts, d), 0)
        a = jnp.exp(log_a)
        mult = jnp.where((rows == 0) & (t == 0), 1.0, jnp.sqrt(_one_minus_square(a, log_a)))
        a_scr[...] = a
        b_scr[...] = xr * gate_i * mult

        sub = lax.broadcasted_iota(jnp.int32, (8, d), 0)

        def chunk(i, h):
            a = a_scr[pl.ds(i * 8, 8), :]
            b = b_scr[pl.ds(i * 8, 8), :]
            for s in (1, 2, 4):
                keep = sub >= s
                b = jnp.where(keep, a * pltpu.roll(b, s, 0) + b, b)
                a = jnp.where(keep, a * pltpu.roll(a, s, 0), a)
            hh = a * h + b
            hr_o[pl.ds(i * 8, 8), :] = hh
            return hh[7:8, :]

        carry[...] = lax.fori_loop(0, ts // 8, chunk, carry[...])
        gelu, _ = _gelu_parts(y_ref[...])
        ga_o[...] = (gelu * hr_o[...]).astype(BF16)

        u = u_ref[...]
        uu = jnp.concatenate([jnp.where(t == 0, 0.0, uh_ref[...]), u], axis=0)
        pos = (rows[:, :BLOCK] + t * ts + 1).astype(F32)
        sums, have, parts = uu, 1, []
        for g, win in enumerate(POOL_WINDOWS):
            while have < win:
                sums = sums + pltpu.roll(sums, have, 0)
                have *= 2
            mean = sums[POOL_HALO:, g * BLOCK:(g + 1) * BLOCK] * (1.0 / jnp.minimum(pos, float(win)))
            parts.append(mean - u[:, g * BLOCK:(g + 1) * BLOCK])
        pb = jnp.concatenate(parts, axis=1).astype(BF16)
        p_o[...] = pb
        mixed = jnp.concatenate([jnp.dot(pb[:, g * BLOCK:(g + 1) * BLOCK], wp_ref[g], preferred_element_type=F32)
                                 for g in range(N_BLOCKS)], axis=1) + vec_ref[4:5, :]
        pooled_o[...] = (mixed * vec_ref[5:6, :]).astype(BF16)

    halo_x = pl.BlockSpec((CONV_HALO, d), lambda t: (jnp.maximum(t * (ts // CONV_HALO) - 1, 0), 0))
    halo_u = pl.BlockSpec((POOL_HALO, d), lambda t: (jnp.maximum(t * (ts // POOL_HALO) - 1, 0), 2))
    wspec = pl.BlockSpec((N_BLOCKS, BLOCK, BLOCK), lambda t: (0, 0, 0))
    return _hosted(
        stage, body, name="seq_forward", grid=(s_rows // ts,),
        out_shape=[SDS((s_rows, d), F32), SDS((s_rows, d), F32), SDS((s_rows, d), BF16), SDS((s_rows, d), BF16), SDS((s_rows, d), BF16)],
        in_specs=[_row_spec(ts, d, 0), halo_x, _row_spec(ts, d, 1), _row_spec(ts, d, 2), halo_u, _vec_spec(CONV_WIDTH, d), _vec_spec(8, d),
                  wspec, wspec, wspec],
        out_specs=[_row_spec(ts, d)] * 5,
        scratch_shapes=[pltpu.VMEM((1, d), F32), pltpu.VMEM((ts, d), F32), pltpu.VMEM((ts, d), F32)],
        compiler_params=_params("arbitrary"),
    )(proj, proj, proj, proj, proj, conv_w, vecs, w_a, w_x, w_p)


def _branches(ga, pooled, proj, w_a, w_b, *, ts, stage=None):
    s_rows, d = ga.shape

    def body(a_ref, p_ref, ga_ref, gb_ref, wa_ref, wb_ref, bra_o, brb_o, merged_o):
        bra = jnp.dot(a_ref[...], wa_ref[0], preferred_element_type=F32)
        brb = jnp.dot(p_ref[...], wb_ref[0], preferred_element_type=F32)
        bra_o[...] = bra
        brb_o[...] = brb
        merged_o[...] = (jax.nn.sigmoid(ga_ref[...]) * bra + jax.nn.sigmoid(gb_ref[...]) * brb).astype(BF16)

    rows = _row_spec(ts, d)
    wspec = pl.BlockSpec((1, d, d), lambda *g: (0, 0, 0), pipeline_mode=pl.Buffered(1))
    return _hosted(
        stage, body, name="branches", grid=(s_rows // ts,), out_shape=[SDS((s_rows, d), F32), SDS((s_rows, d), F32), SDS((s_rows, d), BF16)],
        in_specs=[rows, rows, _row_spec(ts, d, 3), _row_spec(ts, d, 4), wspec, wspec], out_specs=[rows, rows, rows],
        compiler_params=_params("arbitrary"),
    )(ga, pooled, proj, proj, w_a, w_b)


def _seq_backward(proj, xr, hr, p, dga, dpooled, dgab, conv_w, vecs, w_a, w_x, w_p, *, ts, stage=None):
    s_rows = proj.shape[0]
    d = D_MODEL
    n_t = s_rows // ts

    def block_dots(lhs, w_ref, dims):
        return jnp.concatenate([lax.dot_general(lhs[:, g * BLOCK:(g + 1) * BLOCK], w_ref[g], dims, preferred_element_type=F32)
                                for g in range(N_BLOCKS)], axis=1)

    def add_block_grads(dw_ref, lhs, rhs):
        for g in range(N_BLOCKS):
            dw_ref[g] += lax.dot_general(lhs[:, g * BLOCK:(g + 1) * BLOCK], rhs[:, g * BLOCK:(g + 1) * BLOCK],
                                         (((0,), (0,)), ((), ())), preferred_element_type=F32)

    nn, nt = (((1,), (0,)), ((), ())), (((1,), (1,)), ((), ()))

    def body(x_ref, y_ref, xr_ref, hr_ref, hh_ref, p_ref, dga_ref, dpl_ref, dgab_ref, cw_ref, vec_ref, wa_ref, wx_ref, wp_ref,
             dproj_o, acc_o, dwa_o, dwx_o, dwp_o, g_carry, dxr_carry, q_carry, a_scr, b_scr, g_scr):
        step = pl.program_id(0)
        t = n_t - 1 - step

        @pl.when(step == 0)
        def _():
            for ref in (acc_o, dwa_o, dwx_o, dwp_o, g_carry, dxr_carry, q_carry):
                ref[...] = jnp.zeros_like(ref)

        rows = lax.broadcasted_iota(jnp.int32, (ts, d), 0)
        start = (rows == 0) & (t == 0)
        xr = xr_ref[...]
        xrb = xr.astype(BF16)
        r = jax.nn.sigmoid(block_dots(xrb, wa_ref, nn) + vec_ref[1:2, :])
        gate_i = jax.nn.sigmoid(block_dots(xrb, wx_ref, nn) + vec_ref[2:3, :])
        sp = jax.nn.softplus(vec_ref[3:4, :])
        log_a = (-C_RG * r) * sp
        a = jnp.exp(log_a)
        m_square = _one_minus_square(a, log_a)
        mult = jnp.where(start, 1.0, jnp.sqrt(m_square))
        gelu, dgelu = _gelu_parts(y_ref[...])
        dga_t = dga_ref[...]
        hr_t = hr_ref[...]
        dy = dga_t * hr_t * dgelu

        a_scr[...] = jnp.where(rows == ts - 1, 1.0, pltpu.roll(a, ts - 1, 0))
        b_scr[...] = dga_t * gelu
        sub = lax.broadcasted_iota(jnp.int32, (8, d), 0)

        def chunk(i, g_next):
            at = pl.multiple_of((ts // 8 - 1 - i) * 8, 8)
            aa = a_scr[pl.ds(at, 8), :]
            bb = b_scr[pl.ds(at, 8), :]
            for s in (1, 2, 4):
                keep = sub < 8 - s
                bb = jnp.where(keep, bb + aa * pltpu.roll(bb, 8 - s, 0), bb)
                aa = jnp.where(keep, aa * pltpu.roll(aa, 8 - s, 0), aa)
            gg = aa * g_next + bb
            g_scr[pl.ds(at, 8), :] = gg
            return gg[0:1, :]

        g_first = lax.fori_loop(0, ts // 8, chunk, g_carry[...])
        g_carry[...] = a[0:1, :] * g_first
        g = g_scr[...]

        h_before = jnp.where(t == 0, 0.0, hh_ref[CONV_HALO - 1:CONV_HALO, :])
        h_prev = jnp.where(rows == 0, h_before, pltpu.roll(hr_t, 1, 0))
        dxr = g * gate_i * mult
        d_i = g * xr * mult
        d_mult = g * xr * gate_i
        dlog_a = (g * h_prev) * a - jnp.where(start, 0.0, d_mult * (a * a) * lax.rsqrt(m_square))
        dzr = (dlog_a * (-C_RG * sp)) * (r * (1.0 - r))
        dzi = d_i * (gate_i * (1.0 - gate_i))
        _accumulate(acc_o, 5, dzr)
        _accumulate(acc_o, 6, dzi)
        _accumulate(acc_o, 7, dlog_a * (-C_RG * r) * jax.nn.sigmoid(vec_ref[3:4, :]))
        dzrb, dzib = dzr.astype(BF16), dzi.astype(BF16)
        add_block_grads(dwa_o, xrb, dzrb)
        add_block_grads(dwx_o, xrb, dzib)
        dxr = dxr + block_dots(dzrb, wa_ref, nt) + block_dots(dzib, wx_ref, nt)

        x = x_ref[...]
        ext = jnp.concatenate([dxr, dxr_carry[...]], axis=0)
        dx = dxr * cw_ref[CONV_WIDTH - 1:CONV_WIDTH, :]
        _accumulate(acc_o, CONV_WIDTH - 1, x * dxr)
        for j in range(CONV_WIDTH - 1):
            ahead = pltpu.roll(ext, ts + CONV_HALO - (CONV_WIDTH - 1 - j), 0)[:ts]
            dx = dx + ahead * cw_ref[j:j + 1, :]
            _accumulate(acc_o, j, x * ahead)
        _accumulate(acc_o, 4, dxr)
        dxr_carry[...] = dxr[0:CONV_HALO, :]

        pb = p_ref[...]
        mixed = block_dots(pb, wp_ref, nn) + vec_ref[4:5, :]
        dpl = dpl_ref[...]
        _accumulate(acc_o, 9, dpl * mixed)
        dmixed = dpl * vec_ref[5:6, :]
        _accumulate(acc_o, 8, dmixed)
        dmb = dmixed.astype(BF16)
        add_block_grads(dwp_o, pb, dmb)
        dp = block_dots(dmb, wp_ref, nt)
        pos = (rows[:, :BLOCK] + t * ts + 1).astype(F32)
        q = jnp.concatenate([dp[:, g * BLOCK:(g + 1) * BLOCK] * (1.0 / jnp.minimum(pos, float(win))) for g, win in enumerate(POOL_WINDOWS)], axis=1)
        sums, have, parts = jnp.concatenate([q, q_carry[...]], axis=0), 1, []
        for g, win in enumerate(POOL_WINDOWS):
            while have < win:
                sums = sums + pltpu.roll(sums, ts + POOL_HALO - have, 0)
                have *= 2
            parts.append(sums[:ts, g * BLOCK:(g + 1) * BLOCK])
        du = jnp.concatenate(parts, axis=1) - dp
        q_carry[...] = q[0:POOL_HALO, :]

        dproj_o[:, 0:d] = dx.astype(BF16)
        dproj_o[:, d:2 * d] = dy.astype(BF16)
        dproj_o[:, 2 * d:3 * d] = du.astype(BF16)
        dproj_o[:, 3 * d:5 * d] = dgab_ref[...]

    def rev(cols, col_block=0):
        return pl.BlockSpec((ts, cols), lambda i: (n_t - 1 - i, col_block))

    halo_h = pl.BlockSpec((CONV_HALO, d), lambda i: (jnp.maximum((n_t - 1 - i) * (ts // CONV_HALO) - 1, 0), 0))
    wspec = pl.BlockSpec((N_BLOCKS, BLOCK, BLOCK), lambda i: (0, 0, 0))
    return _hosted(
        stage, body, name="seq_backward", grid=(n_t,),
        out_shape=[SDS((s_rows, 5 * d), BF16), SDS((16, d), F32)] + [SDS((N_BLOCKS, BLOCK, BLOCK), F32)] * 3,
        in_specs=[rev(d, 0), rev(d, 1), rev(d), rev(d), halo_h, rev(d), rev(d), rev(d), rev(2 * d), _vec_spec(CONV_WIDTH, d), _vec_spec(8, d),
                  wspec, wspec, wspec],
        out_specs=[rev(5 * d), _vec_spec(16, d), wspec, wspec, wspec],
        scratch_shapes=[pltpu.VMEM((1, d), F32), pltpu.VMEM((CONV_HALO, d), F32), pltpu.VMEM((POOL_HALO, d), F32),
                        pltpu.VMEM((ts, d), F32), pltpu.VMEM((ts, d), F32), pltpu.VMEM((ts, d), F32)],
        compiler_params=_params("arbitrary"),
    )(proj, proj, xr, hr, hr, p, dga, dpooled, dgab, conv_w, vecs, w_a, w_x, w_p)


def _adamw_math(w, g, m, v):
    m = ADAM_B1 * m + (1.0 - ADAM_B1) * g
    v = ADAM_B2 * v + (1.0 - ADAM_B2) * jnp.square(g)
    m_hat = m / (1.0 - ADAM_B1 ** ADAM_STEP)
    v_hat = v / (1.0 - ADAM_B2 ** ADAM_STEP)
    delta = -ADAM_LR * (m_hat / (jnp.sqrt(v_hat) + ADAM_EPS) + ADAM_WD * w)
    return delta, m, v


def _adamw(w, g, m, v, name):
    rows, cols = w.shape
    tr = min(rows, 256)

    def body(w_ref, g_ref, m_ref, v_ref, d_ref, mo_ref, vo_ref):
        d_ref[...], mo_ref[...], vo_ref[...] = _adamw_math(w_ref[...], g_ref[...], m_ref[...], v_ref[...])

    spec = pl.BlockSpec((tr, cols), lambda i: (i, 0))
    return pl.pallas_call(
        body, name=name, grid=(rows // tr,), out_shape=[SDS((rows, cols), F32)] * 3, in_specs=[spec] * 4, out_specs=[spec] * 3,
        compiler_params=_params("parallel"),
    )(w, g, m, v)


def _ada_forward(c_all, w_ada, b_cols):
    n_cols = w_ada.shape[1]

    def body(c_ref, w_ref, b_ref, act_ref, mod_ref):
        cv = c_ref[...]
        act = cv * jax.nn.sigmoid(cv)
        act_ref[...] = act
        mod_ref[...] = jnp.dot(act.astype(BF16), w_ref[...].astype(BF16), preferred_element_type=F32) + b_ref[...]

    return pl.pallas_call(
        body, name="ada_forward", out_shape=[SDS(c_all.shape, F32), SDS((N_DEV, n_cols), F32)],
        in_specs=[VMEM_SPEC] * 3, out_specs=[VMEM_SPEC] * 2, compiler_params=pltpu.CompilerParams(vmem_limit_bytes=VMEM_LIMIT_V7X),
    )(c_all, w_ada, b_cols)


def _ada_backward(act_t, dmod_cols, w, m, v):
    rows, cols = w.shape
    tn = 512

    def body(a_ref, dm_ref, w_ref, m_ref, v_ref, g_ref, d_ref, mo_ref, vo_ref):
        act = a_ref[...].astype(BF16).astype(F32)
        dm = dm_ref[...].astype(BF16).astype(F32)
        g = act[:, 0:1] * dm[0:1, :]
        for b in range(1, N_DEV):
            g = g + act[:, b:b + 1] * dm[b:b + 1, :]
        g_ref[...] = g
        d_ref[...], mo_ref[...], vo_ref[...] = _adamw_math(w_ref[...], g, m_ref[...], v_ref[...])

    spec = pl.BlockSpec((rows, tn), lambda j: (0, j))
    return pl.pallas_call(
        body, name="ada_backward", grid=(cols // tn,), out_shape=[SDS((rows, cols), F32)] * 4,
        in_specs=[pl.BlockSpec((rows, N_DEV), lambda j: (0, 0)), pl.BlockSpec((N_DEV, tn), lambda j: (0, j)), spec, spec, spec],
        out_specs=[spec] * 4, compiler_params=_params("parallel"),
    )(act_t, dmod_cols, w, m, v)


def _sum_rows(gathered, n_loss):
    n = gathered.shape[2]

    def body(g_ref, o_ref):
        acc = g_ref[0]
        for r in range(1, N_DEV):
            acc = acc + g_ref[r]
        o_ref[...] = acc
        o_ref[:, n - n_loss:n] = jnp.broadcast_to(jnp.sum(acc[:, n - n_loss:n], axis=1, keepdims=True), (1, n_loss))

    return pl.pallas_call(body, name="sum_rows", out_shape=SDS((1, n), F32), in_specs=[VMEM_SPEC], out_specs=VMEM_SPEC)(gathered)


def _adamw_vectors(total, offsets, separate, params):
    n_p = len(params)
    apart = [g for g in separate if g is not None]

    def body(*refs):
        total_ref, apart_refs = refs[0], list(refs[1:1 + len(apart)])
        ins, outs = refs[1 + len(apart):1 + len(apart) + 3 * n_p], refs[1 + len(apart) + 3 * n_p:]
        for i in range(n_p):
            w_ref, m_ref, v_ref = ins[3 * i:3 * i + 3]
            g = apart_refs.pop(0)[...] if offsets[i] is None else total_ref[:, offsets[i]:offsets[i] + w_ref.shape[1]]
            outs[4 * i][...] = g
            outs[4 * i + 1][...], outs[4 * i + 2][...], outs[4 * i + 3][...] = _adamw_math(w_ref[...], g, m_ref[...], v_ref[...])

    flat = [a for p in params for a in p]
    res = pl.pallas_call(
        body, name="adamw_vectors", out_shape=[SDS(p[0].shape, F32) for p in params for _ in range(4)],
        in_specs=[VMEM_SPEC] * (1 + len(apart) + len(flat)), out_specs=[VMEM_SPEC] * (4 * n_p),
    )(total, *apart, *flat)
    return [tuple(res[4 * i:4 * i + 4]) for i in range(n_p)]


def _blocks_to_pieces(w):
    nb, rows, n = w.shape
    q = rows // N_CHIPS
    return w.reshape(nb, N_CHIPS, q, n).transpose(1, 0, 2, 3).reshape(N_CHIPS, 2, nb // 2, q, n)


def _pieces_to_blocks(w):
    n_chips, _, half, q, n = w.shape
    return w.reshape(n_chips, 2 * half, q, n).transpose(1, 0, 2, 3).reshape(2 * half, n_chips * q, n)


def kernel(x, c, norm_mix_g, norm_mlp_g, w_ada, b_ada, w_in, conv_w, conv_b, w_rg_a, b_rg_a, w_rg_x, b_rg_x, a_param, w_branch_a, w_pool, b_pool, pool_scale, w_branch_b, w_out, w_up, w_down, final_g, loss_target, m_norm_mix_g, m_norm_mlp_g, m_w_ada, m_b_ada, m_w_in, m_conv_w, m_conv_b, m_w_rg_a, m_b_rg_a, m_w_rg_x, m_b_rg_x, m_a_param, m_w_branch_a, m_w_pool, m_b_pool, m_pool_scale, m_w_branch_b, m_w_out, m_w_up, m_w_down, m_final_g, v_norm_mix_g, v_norm_mlp_g, v_w_ada, v_b_ada, v_w_in, v_conv_w, v_conv_b, v_w_rg_a, v_b_rg_a, v_w_rg_x, v_b_rg_x, v_a_param, v_w_branch_a, v_w_pool, v_b_pool, v_pool_scale, v_w_branch_b, v_w_out, v_w_up, v_w_down, v_final_g):
    d = D_MODEL
    s_rows = x.shape[1]
    ts, tsq = min(512, s_rows), min(256, s_rows)
    xi, yi, ci = _coords()
    k_me, me = 2 * xi + yi, 4 * xi + 2 * yi + ci
    ada_cols = w_ada.shape[2]
    conv_cols = conv_w.shape[2]
    n_in, n_up = w_in.shape[2], w_up.shape[2]

    names = ("w_in", "w_up", "w_down", "w_a", "w_b", "w_out", "rg_a", "rg_x", "pool")
    mine = dict(zip(names, (w_in[0], w_up[0], w_down[0], w_branch_a[0], w_branch_b[0], w_out[0], w_rg_a[0], w_rg_x[0], w_pool[0])))
    moments_m = dict(zip(names, (m_w_in[0], m_w_up[0], m_w_down[0], m_w_branch_a[0], m_w_branch_b[0], m_w_out[0], m_w_rg_a[0], m_w_rg_x[0], m_w_pool[0])))
    moments_v = dict(zip(names, (v_w_in[0], v_w_up[0], v_w_down[0], v_w_branch_a[0], v_w_branch_b[0], v_w_out[0], v_w_rg_a[0], v_w_rg_x[0], v_w_pool[0])))
    block_weights, squares = ("rg_a", "rg_x", "pool"), ("w_a", "w_b", "w_out")
    place = jnp.stack([ci, k_me]).astype(jnp.int32)
    g_mix, g_mlp, g_fin = norm_mix_g, norm_mlp_g, final_g.reshape(1, d)
    x0, target = x[0], loss_target[0]
    wg = functools.partial(_weight_grad, tk=min(512, d))
    plain, back, back_sum = (dict(transposed=t, reduce_blocks=r) for t, r in ((False, False), (True, False), (True, True)))
    rows_d, vec8, vec1 = _row_spec(ts, d), _vec_spec(8, d), _vec_spec(1, d)
    tw = min(1024, s_rows)
    halves_of_tile = 2

    def out_rows(cols, dtype, t=ts):
        return SDS((s_rows, cols), dtype), _row_spec(t, cols)

    sums_out = (SDS((8, d), F32), vec8)

    def halves(n):
        return mine[n].astype(BF16).reshape(2, -1, mine[n].shape[-1])

    def blocks(w):
        return _pieces_to_blocks(w.reshape(N_CHIPS, 2, N_BLOCKS // 2, BLOCK // N_CHIPS, BLOCK))

    stage = _Stage()
    i_c, i_cw, i_in = _add_allgather8(stage, c), _add_allgather8(stage, conv_w[0]), _add_gather_ici(stage, halves("w_in"))
    got = _run_stage(stage, "exchange_c_conv_w_in")
    conv_full = got[i_cw][0::2].transpose(1, 0, 2).reshape(CONV_WIDTH, d)
    act_all, mod_cols = _ada_forward(got[i_c].reshape(N_DEV, d), w_ada[0], lax.dynamic_slice_in_dim(b_ada, k_me * ada_cols, ada_cols, axis=1))
    stage = _Stage()
    i_mod, i_in = _add_allgather8(stage, mod_cols), _add_gather_d2d(stage, got[i_in])
    i_blk = [_add_gather_ici(stage, halves(n)) for n in block_weights]
    got = _run_stage(stage, "exchange_mod_w_in_gates")
    mod_mine = lax.dynamic_index_in_dim(got[i_mod], me, axis=1, keepdims=False)[0::2]
    mod = jnp.concatenate([mod_mine.reshape(N_MOD, d), jnp.zeros((8 - N_MOD, d), F32)], axis=0)
    w_in_all = got[i_in].reshape(N_CHIPS, d, n_in)
    vecs = jnp.concatenate([conv_b, b_rg_a, b_rg_x, a_param, b_pool, pool_scale, jnp.zeros((2, d), F32)], axis=0)

    def norm_first(a_ref, extra_refs, out_refs):
        h = _modulated_norm(a_ref[...], extra_refs[0], extra_refs[1], 1, 0)
        out_refs[1][...] = h
        return h

    stage = _Stage()
    i_blk = [_add_gather_d2d(stage, got[i]) for i in i_blk]
    i_sq = [_add_gather_ici(stage, halves(n)) for n in squares]
    down_rows = mine["w_down"].shape[0] // 4
    i_down = _add_gather_ici(stage, halves("w_down"), rows=(0, down_rows))
    (proj, h1), got = _matmul(x0, w_in_all, name="proj_in", tm=ts, extra=[(mod, vec8), (g_mix, vec1)], prepare=norm_first,
                              outs=[out_rows(N_CHIPS * n_in, F32), out_rows(d, BF16)], finish=_store_blocks(n_in), parts=halves_of_tile,
                              stage=stage, **plain)
    w_rg_a_all, w_rg_x_all, w_pool_all = (blocks(got[i]) for i in i_blk)
    stage = _Stage()
    i_sq = [_add_gather_d2d(stage, got[i]) for i in i_sq]
    i_up = _add_gather_ici(stage, halves("w_up"))
    i_down = _add_gather_ici(stage, halves("w_down"), rows=(down_rows, down_rows), whole=got[i_down])
    (xr, hr, ga, p, pooled), got = _seq_forward(proj, conv_full, vecs, w_rg_a_all, w_rg_x_all, w_pool_all, ts=tsq, stage=stage)
    w_a_all, w_b_all, w_out_all = (got[i].reshape(1, d, d) for i in i_sq)
    stage = _Stage()
    i_up, i_down = _add_gather_d2d(stage, got[i_up]), _add_gather_d2d(stage, got[i_down])
    (br_a, br_b, merged), got = _branches(ga, pooled, proj, w_a_all, w_b_all, ts=ts, stage=stage)
    w_up_all, w_down_all = got[i_up].reshape(N_CHIPS, d, n_up), got[i_down].reshape(N_CHIPS, D_FF // N_CHIPS, d)

    def residual_norm(k, acc, extra_refs, out_refs):
        x_ref, mod_ref, g_ref = extra_refs
        out_refs[0][...] = acc
        x2_t = x_ref[...] + mod_ref[2:3, :] * acc
        out_refs[1][...] = x2_t
        out_refs[2][...] = _modulated_norm(x2_t, mod_ref, g_ref, 4, 3)

    mo, x2, h2 = _matmul(merged, w_out_all, name="mix_out", tm=ts, extra=[(x0, rows_d), (mod, vec8), (g_mlp, vec1)],
                         outs=[out_rows(d, F32), out_rows(d, F32), out_rows(d, BF16)], finish=residual_norm, parts=halves_of_tile,
                         transposed=False, reduce_blocks=True)

    def relu_squared(k, acc, extra_refs, out_refs):
        out_refs[0][:, k * n_up:(k + 1) * n_up] = acc
        out_refs[1][:, k * n_up:(k + 1) * n_up] = jnp.square(jnp.maximum(acc, 0.0)).astype(BF16)

    up, ff = _matmul(h2, w_up_all, name="mlp_up", tm=ts, outs=[out_rows(D_FF, F32), out_rows(D_FF, BF16)], finish=relu_squared,
                     parts=halves_of_tile, **plain)

    def loss_head(k, acc, extra_refs, out_refs):
        x2_ref, mod_ref, g_ref, t_ref = extra_refs
        dx_ref, df_ref, acc_ref = out_refs
        xh, r = _rms(x2_ref[...] + mod_ref[5:6, :] * acc)
        err = xh * g_ref[...] - t_ref[...]
        dy = err / d
        dxh = dy * g_ref[...]
        dx3_t = r * (dxh - xh * jnp.mean(dxh * xh, axis=-1, keepdims=True))
        dx_ref[...] = dx3_t
        df_ref[...] = (dx3_t * mod_ref[5:6, :]).astype(BF16)
        _accumulate(acc_ref, 0, dy * xh)
        _accumulate(acc_ref, 1, dx3_t * acc)
        _accumulate(acc_ref, 2, jnp.square(err) * (0.5 / d))

    dx3, dffo, sums_head = _matmul(ff, w_down_all, name="mlp_down", tm=ts, extra=[(x2, rows_d), (mod, vec8), (g_fin, vec1), (target, rows_d)],
                                   outs=[out_rows(d, F32), out_rows(d, BF16), sums_out], finish=loss_head, zero_first=(2,), parts=halves_of_tile,
                                   transposed=False, reduce_blocks=True)

    partial_of, from_sibling, chip_sum, from_chips, half_done, quarter = {}, {}, {}, {}, {}, {}

    def publish(n, g):
        g = _blocks_to_pieces(g).astype(BF16) if n in block_weights else g
        partial_of[n] = g.reshape(N_CHIPS, 2, -1, mine[n].shape[-1])

    def exchange(to_sibling=(), to_chips=(), swap=()):
        st = _Stage()
        slots = [(n, from_sibling, _add_reduce_d2d(st, partial_of[n])) for n in to_sibling]
        slots += [(n, from_chips, _add_reduce_ici(st, chip_sum[n])) for n in to_chips]
        slots += [(n, quarter, _add_swap_halves(st, half_done[n])) for n in swap]
        return st, slots

    def collect(slots, outs):
        for n, where, i in slots:
            where[n] = outs[i]

    def sum_pairs(*ns):
        for n in ns:
            chip_sum[n] = _sum_pair(partial_of[n], from_sibling[n], place, "sum_pair_" + n)

    def sum_quarters(*ns):
        for n in ns:
            half_done[n] = _sum_quarter(partial_of[n], from_sibling[n], from_chips[n], place, "sum_quarter_" + n)

    publish("w_down", wg(ff, dffo, nb=1, name="grad_w_down"))

    def relu_squared_backward(k, acc, extra_refs, out_refs):
        cols = slice(k * n_up, (k + 1) * n_up)
        out_refs[0][:, cols] = (acc * (2.0 * jnp.maximum(extra_refs[0][:, cols], 0.0))).astype(BF16)

    stage, slots = exchange(to_sibling=["w_down"])
    (dup,), got = _matmul(dffo, w_down_all, name="d_mlp_down", tm=ts, extra=[(up, _row_spec(ts, D_FF))], outs=[out_rows(D_FF, BF16)],
                          finish=relu_squared_backward, parts=halves_of_tile, stage=stage, **back)
    collect(slots, got)
    sum_pairs("w_down")

    def norm_mlp_backward(k, acc, extra_refs, out_refs):
        x2_ref, dres_ref, mod_ref, g_ref, mo_ref = extra_refs
        dx_ref, dmo_ref, acc_ref = out_refs
        dx =_modulated_norm_backward(acc, x2_ref[...], dres_ref[...], mod_ref, g_ref, acc_ref, 4)
        dx_ref[...] = dx
        dmo_ref[...] = (dx * mod_ref[2:3, :]).astype(BF16)
        _accumulate(acc_ref, 3, dx * mo_ref[...])

    stage, slots = exchange(to_chips=["w_down"])
    (dx2, dmo, sums_mlp), got = _matmul(dup, w_up_all, name="d_mlp_up", tm=ts, stage=stage, finish=norm_mlp_backward,
                                        extra=[(x2, rows_d), (dx3, rows_d), (mod, vec8), (g_mlp, vec1), (mo, rows_d)],
                                        outs=[out_rows(d, F32), out_rows(d, BF16), sums_out], zero_first=(2,), parts=halves_of_tile, **back_sum)
    collect(slots, got)
    sum_quarters("w_down")
    stage, slots = exchange(swap=["w_down"])
    g_up, got = wg(h2, dup, nb=N_CHIPS, name="grad_w_up", stage=stage)
    collect(slots, got)
    publish("w_up", g_up)
    publish("w_out", wg(merged, dmo, nb=1, name="grad_w_out"))

    def merge_backward(k, acc, extra_refs, out_refs):
        ga_ref, gb_ref, a_ref, b_ref = extra_refs
        sa, sb = jax.nn.sigmoid(ga_ref[...]), jax.nn.sigmoid(gb_ref[...])
        out_refs[0][...] = (acc * sa).astype(BF16)
        out_refs[1][...] = (acc * sb).astype(BF16)
        out_refs[2][:, 0:d] = (acc * a_ref[...] * (sa * (1.0 - sa))).astype(BF16)
        out_refs[2][:, d:2 * d] = (acc * b_ref[...] * (sb * (1.0 - sb))).astype(BF16)

    stage, slots = exchange(to_sibling=["w_up", "w_out"])
    (dbr_a, dbr_b, dgab), got = _matmul(dmo, w_out_all, name="d_mix_out", tm=ts, stage=stage, finish=merge_backward,
                                        extra=[(proj, _row_spec(ts, d, 3)), (proj, _row_spec(ts, d, 4)), (br_a, rows_d), (br_b, rows_d)],
                                        outs=[out_rows(d, BF16), out_rows(d, BF16), out_rows(2 * d, BF16)], parts=halves_of_tile, **back_sum)
    collect(slots, got)
    publish("w_a", wg(ga, dbr_a, nb=1, name="grad_w_branch_a"))
    publish("w_b", wg(pooled, dbr_b, nb=1, name="grad_w_branch_b"))
    stage, slots = exchange(to_sibling=["w_a", "w_b"])
    (dga,), got = _matmul(dbr_a, w_a_all, name="d_branch_a", tm=tw, outs=[out_rows(d, F32, tw)], finish=_store_blocks(d), stage=stage, **back_sum)
    collect(slots, got)
    dpooled, = _matmul(dbr_b, w_b_all, name="d_branch_b", tm=tw, outs=[out_rows(d, F32, tw)], finish=_store_blocks(d), **back_sum)
    sum_pairs("w_up", *squares)
    stage, slots = exchange(to_chips=("w_up",) + squares)
    (dproj, sums_seq, d_rg_a, d_rg_x, d_pool), got = _seq_backward(proj, xr, hr, p, dga, dpooled, dgab, conv_full, vecs,
                                                                   w_rg_a_all, w_rg_x_all, w_pool_all, ts=tsq, stage=stage)
    collect(slots, got)
    sum_quarters("w_up", *squares)
    for n, g in zip(block_weights, (d_rg_a, d_rg_x, d_pool)):
        publish(n, g)
    stage, slots = exchange(to_sibling=block_weights, swap=("w_up",) + squares)
    g_in, got = wg(h1, dproj, nb=N_CHIPS, name="grad_w_in", stage=stage)
    collect(slots, got)
    publish("w_in", g_in)
    sum_pairs(*block_weights)
    stage, slots = exchange(to_sibling=["w_in"], to_chips=block_weights)
    collect(slots, _run_stage(stage, "exchange_w_in_gates"))
    sum_pairs("w_in")
    sum_quarters(*block_weights)

    def norm_mix_backward(k, acc, extra_refs, out_refs):
        x_ref, dres_ref, mod_ref, g_ref = extra_refs
        out_refs[0][...] = _modulated_norm_backward(acc, x_ref[...], dres_ref[...], mod_ref, g_ref, out_refs[1], 1)

    stage, slots = exchange(to_chips=["w_in"], swap=block_weights)
    (grad_x, sums_mix), got = _matmul(dproj, w_in_all, name="d_proj_in", tm=ts, stage=stage, finish=norm_mix_backward,
                                      extra=[(x0, rows_d), (dx2, rows_d), (mod, vec8), (g_mix, vec1)],
                                      outs=[out_rows(d, F32), sums_out], zero_first=(1,), parts=halves_of_tile, **back_sum)
    collect(slots, got)
    sum_quarters("w_in")

    dmod = jnp.concatenate([sums_mix[1:2], sums_mix[0:1], sums_mlp[3:4], sums_mlp[1:2], sums_mlp[0:1], sums_head[1:2]], axis=1)
    row = jnp.concatenate([sums_mix[2:3], sums_mlp[2:3], sums_seq[4:5], sums_seq[5:6], sums_seq[6:7], sums_seq[7:8], sums_seq[8:9],
                           sums_seq[9:10], sums_head[0:1], sums_seq[0:CONV_WIDTH].reshape(1, CONV_WIDTH * d), dmod, sums_head[2:3]], axis=1)
    n_vec, n_conv, n_mod = 9 * d, CONV_WIDTH * d, N_MOD * d
    stage, slots = exchange(swap=["w_in"])
    i_rows = _add_allgather8(stage, row)
    got = _run_stage(stage, "exchange_sums_w_in")
    collect(slots, got)
    rows_all = got[i_rows]
    total = _sum_rows(rows_all, d)
    loss = total[0, n_vec + n_conv + n_mod]
    dmod_all = rows_all[:, 0, n_vec + n_conv:n_vec + n_conv + n_mod]
    g_ada, d_ada, m_ada, v_ada = _ada_backward(act_all.T, lax.dynamic_slice_in_dim(dmod_all, k_me * ada_cols, ada_cols, axis=1),
                                               w_ada[0], m_w_ada[0], v_w_ada[0])
    g_conv = lax.dynamic_slice_in_dim(total[:, n_vec:n_vec + n_conv].reshape(CONV_WIDTH, d), k_me * conv_cols, conv_cols, axis=1)
    vec_names = ("norm_mix_g", "norm_mlp_g", "conv_b", "b_rg_a", "b_rg_x", "a_param", "b_pool", "pool_scale", "final_g", "conv_w", "b_ada")
    vec_params = [(norm_mix_g, m_norm_mix_g, v_norm_mix_g), (norm_mlp_g, m_norm_mlp_g, v_norm_mlp_g), (conv_b, m_conv_b, v_conv_b),
                  (b_rg_a, m_b_rg_a, v_b_rg_a), (b_rg_x, m_b_rg_x, v_b_rg_x), (a_param, m_a_param, v_a_param), (b_pool, m_b_pool, v_b_pool),
                  (pool_scale, m_pool_scale, v_pool_scale), tuple(a.reshape(1, d) for a in (final_g, m_final_g, v_final_g)),
                  (conv_w[0], m_conv_w[0], v_conv_w[0]), (b_ada, m_b_ada, v_b_ada)]
    offsets = [i * d for i in range(9)] + [None, n_vec + n_conv]
    vec_res = _adamw_vectors(total, offsets, [None] * 9 + [g_conv, None], vec_params)
    vec_out = {n: [r.reshape(shape) for r in res] for n, res, shape in zip(
        vec_names, vec_res, [(1, d)] * 8 + [(d,), (1, CONV_WIDTH, conv_cols), (1, N_MOD * d)])}

    big_out = {}
    for n in names:
        shape2 = (-1, mine[n].shape[-1])
        g2 = quarter[n].reshape(shape2)
        res = (g2,) + tuple(_adamw(mine[n].reshape(shape2), g2, moments_m[n].reshape(shape2), moments_v[n].reshape(shape2), "adamw_" + n))
        big_out[n] = [r.reshape((1,) + mine[n].shape) for r in res]

    key = {"w_in": "w_in", "w_rg_a": "rg_a", "w_rg_x": "rg_x", "w_branch_a": "w_a", "w_pool": "pool", "w_branch_b": "w_b", "w_out": "w_out",
           "w_up": "w_up", "w_down": "w_down"}
    order = ("norm_mix_g", "norm_mlp_g", "w_ada", "b_ada", "w_in", "conv_w", "conv_b", "w_rg_a", "b_rg_a", "w_rg_x", "b_rg_x", "a_param",
             "w_branch_a", "w_pool", "b_pool", "pool_scale", "w_branch_b", "w_out", "w_up", "w_down", "final_g")
    ada_out = [g_ada[None], d_ada[None], m_ada[None], v_ada[None]]
    outs = [loss, grad_x[None]]
    for kind in range(4):
        for nme in order:
            outs.append(ada_out[kind] if nme == "w_ada" else big_out[key[nme]][kind] if nme in key else vec_out[nme][kind])
    return tuple(outs)
```

```python
import functools

import jax
import jax.numpy as jnp
from jax import lax
from jax.experimental import pallas as pl
from jax.experimental.pallas import tpu as pltpu

F32, BF16 = jnp.float32, jnp.bfloat16
D_MODEL = 1024
D_FF = 4 * D_MODEL
N_BLOCKS = 4
BLOCK = D_MODEL // N_BLOCKS
CONV_WIDTH = 4
POOL_WINDOWS = (2, 4, 8, 16)
CONV_HALO = 8
POOL_HALO = 16
N_MOD = 6
EPS = 1e-6
C_RG = 8.0
ADAM_LR, ADAM_B1, ADAM_B2, ADAM_EPS, ADAM_WD, ADAM_STEP = 0.001, 0.9, 0.999, 1e-08, 0.01, 10
N_DEV = 8
N_CHIPS = 4
VMEM_LIMIT_V7X = 56 * 2**20
MESH = pl.DeviceIdType.MESH
SDS = jax.ShapeDtypeStruct
HBM_SPEC = pl.BlockSpec(memory_space=pltpu.HBM)
VMEM_SPEC = pl.BlockSpec(memory_space=pltpu.VMEM)


def _params(*semantics):
    return pltpu.CompilerParams(dimension_semantics=semantics, vmem_limit_bytes=VMEM_LIMIT_V7X)


def _coords():
    return lax.axis_index("x"), lax.axis_index("y"), lax.axis_index("c")


def _flip(v, bit):
    return 1 - v if bit else v


def _chips():
    x, y, _ = _coords()
    return [(1 - x, y), (x, 1 - y), (1 - x, 1 - y)]


class _Stage:
    def __init__(self):
        self.inputs, self.in_specs, self.outputs, self.out_specs, self.aliases = [], [], [], [], {}
        self.parts, self.n_copies = [], 0

    def add(self, inputs, in_spec, outputs, out_spec, n_copies, build, alias=False):
        i0, o0 = len(self.inputs), len(self.outputs)
        self.inputs += inputs
        self.in_specs += [in_spec] * len(inputs)
        self.outputs += outputs
        self.out_specs += [out_spec] * len(outputs)
        if alias:
            self.aliases.update({i0 + i: o0 + i for i in range(len(inputs))})
        self.parts.append((build, i0, len(inputs), o0, len(outputs)))
        self.n_copies += n_copies
        return list(range(o0, o0 + len(outputs)))

    def copies(self, in_refs, out_refs):
        out = []
        for build, i0, ni, o0, no in self.parts:
            out += build(in_refs[i0:i0 + ni], out_refs[o0:o0 + no])
        assert len(out) == self.n_copies
        return out

    def run(self, in_refs, out_refs, send_sems, recv_sems, start):
        x, y, c = _coords()
        for s, (src, dst, to, landing) in enumerate(self.copies(in_refs, out_refs)):
            if to is None:
                cp = pltpu.make_async_copy(src, dst, send_sems.at[s])
                cp.start() if start else cp.wait()
                continue
            cp = pltpu.make_async_remote_copy(src_ref=src, dst_ref=dst, send_sem=send_sems.at[s], recv_sem=recv_sems.at[s],
                                              device_id=to, device_id_type=MESH)
            if start:
                cp.start()
            else:
                pltpu.make_async_remote_copy(src_ref=landing, dst_ref=landing, send_sem=send_sems.at[s], recv_sem=recv_sems.at[s],
                                             device_id=(x, y, c), device_id_type=MESH).wait_recv()
                cp.wait_send()


def _hosted(stage, body, *, name, in_specs, out_specs, out_shape, grid=(), scratch_shapes=(), compiler_params=None):
    if stage is None:
        return pl.pallas_call(body, name=name, grid=grid, in_specs=in_specs, out_specs=out_specs, out_shape=out_shape,
                              scratch_shapes=list(scratch_shapes), compiler_params=compiler_params)
    single = not isinstance(out_shape, (list, tuple))
    h_out_shape = [out_shape] if single else list(out_shape)
    h_out_specs = [out_specs] if single else list(out_specs)
    n_in, n_out, n_scr = len(in_specs), len(h_out_shape), len(scratch_shapes)
    s_in, s_out = len(stage.inputs), len(stage.outputs)

    def wrapped(*refs):
        h_in, st_in = refs[:n_in], refs[n_in:n_in + s_in]
        h_o, st_o = refs[n_in + s_in:n_in + s_in + n_out], refs[n_in + s_in + n_out:n_in + s_in + n_out + s_out]
        h_scr = refs[n_in + s_in + n_out + s_out:n_in + s_in + n_out + s_out + n_scr]
        send_sems, recv_sems = refs[n_in + s_in + n_out + s_out + n_scr:]
        if not grid:
            stage.run(st_in, st_o, send_sems, recv_sems, True)
            if body is not None:
                body(*h_in, *h_o, *h_scr)
            stage.run(st_in, st_o, send_sems, recv_sems, False)
            return
        ids = [pl.program_id(a) for a in range(len(grid))]
        first = functools.reduce(jnp.logical_and, [i == 0 for i in ids])
        last = functools.reduce(jnp.logical_and, [i == g - 1 for i, g in zip(ids, grid)])
        pl.when(first)(lambda: stage.run(st_in, st_o, send_sems, recv_sems, True))
        body(*h_in, *h_o, *h_scr)
        pl.when(last)(lambda: stage.run(st_in, st_o, send_sems, recv_sems, False))

    call = pl.pallas_call(
        wrapped, name=name, grid=grid, in_specs=list(in_specs) + stage.in_specs, out_specs=h_out_specs + stage.out_specs,
        out_shape=h_out_shape + stage.outputs, input_output_aliases={n_in + i: n_out + o for i, o in stage.aliases.items()},
        scratch_shapes=list(scratch_shapes) + [pltpu.SemaphoreType.DMA((stage.n_copies,)), pltpu.SemaphoreType.DMA((stage.n_copies,))],
        compiler_params=pltpu.CompilerParams(dimension_semantics=("arbitrary",) * len(grid), vmem_limit_bytes=VMEM_LIMIT_V7X),
    )

    def run(*operands):
        outs = call(*operands, *stage.inputs)
        host = outs[:n_out]
        return (host[0] if single else host), outs[n_out:]

    return run


def _run_stage(stage, name):
    return _hosted(stage, None, name=name, in_specs=[], out_specs=[], out_shape=[])()[1]


def _add_allgather8(stage, v):
    def build(ins, outs):
        x, y, c = _coords()
        me = 4 * x + 2 * y + c
        copies = [(ins[0], outs[0].at[me], None, None)]
        for r in range(1, N_DEV):
            peer = (_flip(x, r & 4), _flip(y, r & 2), _flip(c, r & 1))
            copies.append((ins[0], outs[0].at[me], peer, outs[0].at[me ^ r]))
        return copies

    return stage.add([v], VMEM_SPEC, [SDS((N_DEV,) + v.shape, v.dtype)], VMEM_SPEC, N_DEV, build)[0]


def _add_gather_ici(stage, shard, rows=None, whole=None):
    first, count = rows or (0, shard.shape[1])

    def build(ins, outs):
        x, y, c = _coords()
        k_me = 2 * x + y
        part = pl.ds(first, count)
        copies = [] if whole is not None else [(ins[0], outs[0].at[k_me], (x, y, 1 - c), outs[0].at[k_me])]
        for px, py in _chips():
            copies.append((ins[0].at[c, part], outs[0].at[k_me, c, part], (px, py, c), outs[0].at[2 * px + py, c, part]))
        return copies

    if whole is None:
        return stage.add([shard], HBM_SPEC, [SDS((N_CHIPS,) + shard.shape, shard.dtype)], HBM_SPEC, N_CHIPS, build)[0]

    def build_into(ins, outs):
        return build(ins[1:], outs)

    i0 = len(stage.inputs)
    out = stage.add([whole, shard], HBM_SPEC, [SDS(whole.shape, whole.dtype)], HBM_SPEC, N_CHIPS - 1, build_into)[0]
    stage.aliases[i0] = out
    return out


def _add_gather_d2d(stage, whole):
    def build(ins, outs):
        x, y, c = _coords()
        return [(outs[0].at[2 * px + py, c], outs[0].at[2 * px + py, c], (x, y, 1 - c), outs[0].at[2 * px + py, 1 - c]) for px, py in _chips()]

    return stage.add([whole], HBM_SPEC, [SDS(whole.shape, whole.dtype)], HBM_SPEC, N_CHIPS - 1, build, alias=True)[0]


def _add_reduce_d2d(stage, grads):
    def build(ins, outs):
        x, y, c = _coords()
        return [(ins[0].at[k, 1 - c], outs[0].at[k], (x, y, 1 - c), outs[0].at[k]) for k in range(N_CHIPS)]

    return stage.add([grads], HBM_SPEC, [SDS((N_CHIPS,) + grads.shape[2:], grads.dtype)], HBM_SPEC, N_CHIPS, build)[0]


def _add_reduce_ici(stage, partial):
    def build(ins, outs):
        x, y, c = _coords()
        return [(ins[0].at[2 * px + py], outs[0].at[j], (px, py, c), outs[0].at[j]) for j, (px, py) in enumerate(_chips())]

    return stage.add([partial], HBM_SPEC, [SDS((N_CHIPS - 1,) + partial.shape[1:], partial.dtype)], HBM_SPEC, N_CHIPS - 1, build)[0]


def _add_swap_halves(stage, quarter):
    def build(ins, outs):
        x, y, c = _coords()
        return [(outs[0].at[c], outs[0].at[c], (x, y, 1 - c), outs[0].at[1 - c])]

    return stage.add([quarter], HBM_SPEC, [SDS(quarter.shape, quarter.dtype)], HBM_SPEC, 1, build, alias=True)[0]


def _sum_pair(grads, from_sibling, place, name):
    _, _, rows, cols = grads.shape
    tr = min(rows, 256)

    def body(place_ref, g_ref, s_ref, o_ref):
        o_ref[0] = (g_ref[0, 0].astype(F32) + s_ref[0].astype(F32)).astype(BF16)

    spec = pl.BlockSpec((1, tr, cols), lambda k, i, place_ref: (k, i, 0))
    return pl.pallas_call(
        body, name=name, out_shape=SDS((N_CHIPS, rows, cols), BF16),
        grid_spec=pltpu.PrefetchScalarGridSpec(
            num_scalar_prefetch=1, grid=(N_CHIPS, rows // tr),
            in_specs=[pl.BlockSpec((1, 1, tr, cols), lambda k, i, place_ref: (k, place_ref[0], i, 0)), spec], out_specs=spec),
        compiler_params=_params("parallel", "parallel"),
    )(place, grads, from_sibling)


def _sum_quarter(grads, from_sibling, from_chips, place, name):
    _, _, rows, cols = grads.shape
    tr = min(rows, 256)

    def body(place_ref, g_ref, s_ref, c_ref, o_ref):
        acc = g_ref[0, 0].astype(F32) + s_ref[0].astype(F32)
        for j in range(N_CHIPS - 1):
            acc = acc + c_ref[j].astype(F32)
        o_ref[0] = acc

    return pl.pallas_call(
        body, name=name, out_shape=SDS((2, rows, cols), F32),
        grid_spec=pltpu.PrefetchScalarGridSpec(
            num_scalar_prefetch=1, grid=(rows // tr,),
            in_specs=[pl.BlockSpec((1, 1, tr, cols), lambda i, place_ref: (place_ref[1], place_ref[0], i, 0)),
                      pl.BlockSpec((1, tr, cols), lambda i, place_ref: (place_ref[1], i, 0)),
                      pl.BlockSpec((N_CHIPS - 1, tr, cols), lambda i, place_ref: (0, i, 0))],
            out_specs=pl.BlockSpec((1, tr, cols), lambda i, place_ref: (place_ref[0], i, 0))),
        compiler_params=_params("parallel"),
    )(place, grads, from_sibling, from_chips)


def _row_spec(ts, cols, col_block=0):
    return pl.BlockSpec((ts, cols), lambda *g: (g[0], col_block))


def _vec_spec(rows, cols):
    return pl.BlockSpec((rows, cols), lambda *g: (0, 0))


def _matmul(a, w, *, transposed, reduce_blocks, name, tm, outs, finish, extra=(), prepare=None, zero_first=(), parts=1, stage=None):
    m_rows = a.shape[0]
    nb, r, c = w.shape
    kb = c if transposed else r
    dims = (((1,), (1,)), ((), ())) if transposed else (((1,), (0,)), ((), ()))
    assert a.shape[1] == (nb * kb if reduce_blocks else kb) and m_rows % tm == 0 and tm % parts == 0
    n_extra, sub = len(extra), tm // parts

    def body(a_ref, w_ref, *rest):
        extra_refs, out_refs = rest[:n_extra], rest[n_extra:]
        if zero_first:
            @pl.when(pl.program_id(0) == 0)
            def _():
                for i in zero_first:
                    out_refs[i][...] = jnp.zeros_like(out_refs[i])
        def views(j):
            rows = pl.ds(j * sub, sub)
            return rows, [ref.at[rows] if ref.shape[0] == tm else ref for ref in extra_refs], [ref.at[rows] if ref.shape[0] == tm else ref for ref in out_refs]

        def product(j):
            rows, ex, ou = views(j)
            if reduce_blocks:
                acc = lax.dot_general(a_ref[rows, 0:kb], w_ref[0], dims, preferred_element_type=F32)
                for k in range(1, nb):
                    acc += lax.dot_general(a_ref[rows, k * kb:(k + 1) * kb], w_ref[k], dims, preferred_element_type=F32)
                return [(None, acc)]
            lhs = a_ref[rows, :] if prepare is None else prepare(a_ref.at[rows], ex, ou)
            return [(k, lax.dot_general(lhs, w_ref[k], dims, preferred_element_type=F32)) for k in range(nb)]

        pending = product(0)
        for j in range(parts):
            ahead = product(j + 1) if j + 1 < parts else None
            _, ex, ou = views(j)
            for k, acc in pending:
                finish(k, acc, ex, ou)
            pending = ahead

    return _hosted(
        stage, body, name=name, grid=(m_rows // tm,), out_shape=[s for s, _ in outs], out_specs=[s for _, s in outs],
        in_specs=[_row_spec(tm, a.shape[1]), pl.BlockSpec((nb, r, c), lambda *g: (0, 0, 0), pipeline_mode=pl.Buffered(1))] + [s for _, s in extra],
        compiler_params=_params("arbitrary"),
    )(a, w, *[e for e, _ in extra])


def _store_blocks(n):
    def finish(k, acc, extra_refs, out_refs):
        if k is None:
            out_refs[0][...] = acc.astype(out_refs[0].dtype)
        else:
            out_refs[0][:, k * n:(k + 1) * n] = acc.astype(out_refs[0].dtype)
    return finish


def _weight_grad(a, b, *, nb, name, tk, stage=None):
    s_rows, k1 = a.shape
    tn = b.shape[1] // nb
    assert k1 % tk == 0 and b.shape[1] % nb == 0

    def body(a_ref, b_ref, o_ref):
        o_ref[0] = lax.dot_general(a_ref[...], b_ref[...], (((0,), (0,)), ((), ())), preferred_element_type=F32).astype(o_ref.dtype)

    return _hosted(
        stage, body, name=name, grid=(nb, k1 // tk), out_shape=SDS((nb, k1, tn), BF16),
        in_specs=[pl.BlockSpec((s_rows, tk), lambda n, i: (0, i)), pl.BlockSpec((s_rows, tn), lambda n, i: (0, n))],
        out_specs=pl.BlockSpec((1, tk, tn), lambda n, i: (n, i, 0)), compiler_params=_params("parallel", "parallel"),
    )(a, b)


def _rms(x):
    r = lax.rsqrt(jnp.mean(x * x, axis=-1, keepdims=True) + EPS)
    return x * r, r


def _accumulate(acc_ref, row, value):
    acc_ref[row:row + 1, :] += jnp.sum(value, axis=0, keepdims=True)


def _gelu_parts(y):
    k0, k1 = 0.7978845608028654, 0.044715
    th = jnp.tanh(k0 * (y + k1 * (y * y * y)))
    gelu = 0.5 * y * (1.0 + th)
    dgelu = 0.5 * (1.0 + th) + 0.5 * y * (1.0 - th * th) * (k0 * (1.0 + 3.0 * k1 * (y * y)))
    return gelu, dgelu


def _one_minus_square(a, log_a):
    return -jnp.tanh(log_a) * (1.0 + a * a)


def _modulated_norm(x, mod_ref, g_ref, sc_row, sh_row):
    xh, _ = _rms(x)
    return ((xh * g_ref[...]) * (1.0 + mod_ref[sc_row:sc_row + 1, :]) + mod_ref[sh_row:sh_row + 1, :]).astype(BF16)


def _modulated_norm_backward(dh, x, dres, mod_ref, g_ref, acc_ref, sc_row):
    xh, r = _rms(x)
    _accumulate(acc_ref, 0, dh * (xh * g_ref[...]))
    _accumulate(acc_ref, 1, dh)
    dn = dh * (1.0 + mod_ref[sc_row:sc_row + 1, :])
    _accumulate(acc_ref, 2, dn * xh)
    dxh = dn * g_ref[...]
    return dres + r * (dxh - xh * jnp.mean(dxh * xh, axis=-1, keepdims=True))


def _seq_forward(proj, conv_w, vecs, w_a, w_x, w_p, *, ts, stage=None):
    s_rows = proj.shape[0]
    d = D_MODEL

    def body(x_ref, xh_ref, y_ref, u_ref, uh_ref, cw_ref, vec_ref, wa_ref, wx_ref, wp_ref,
             xr_o, hr_o, ga_o, p_o, pooled_o, carry, a_scr, b_scr):
        t = pl.program_id(0)

        @pl.when(t == 0)
        def _():
            carry[...] = jnp.zeros_like(carry)

        x = x_ref[...]
        halo = jnp.where(t == 0, 0.0, xh_ref[...])
        xx = jnp.concatenate([halo, x], axis=0)
        xr = vec_ref[0:1, :] + x * cw_ref[CONV_WIDTH - 1:CONV_WIDTH, :]
        for j in range(CONV_WIDTH - 1):
            xr = xr + pltpu.roll(xx, CONV_WIDTH - 1 - j, 0)[CONV_HALO:] * cw_ref[j:j + 1, :]
        xr_o[...] = xr
        xrb = xr.astype(BF16)
        zr = jnp.concatenate([jnp.dot(xrb[:, g * BLOCK:(g + 1) * BLOCK], wa_ref[g], preferred_element_type=F32)
                              for g in range(N_BLOCKS)], axis=1) + vec_ref[1:2, :]
        zi = jnp.concatenate([jnp.dot(xrb[:, g * BLOCK:(g + 1) * BLOCK], wx_ref[g], preferred_element_type=F32)
                              for g in range(N_BLOCKS)], axis=1) + vec_ref[2:3, :]
        r = jax.nn.sigmoid(zr)
        gate_i = jax.nn.sigmoid(zi)
        log_a = (-C_RG * r) * jax.nn.softplus(vec_ref[3:4, :])
        rows = lax.broadcasted_iota(jnp.int32, (ts, d), 0)
        a = jnp.exp(log_a)
        mult = jnp.where((rows == 0) & (t == 0), 1.0, jnp.sqrt(_one_minus_square(a, log_a)))
        a_scr[...] = a
        b_scr[...] = xr * gate_i * mult

        sub = lax.broadcasted_iota(jnp.int32, (8, d), 0)

        def chunk(i, h):
            a = a_scr[pl.ds(i * 8, 8), :]
            b = b_scr[pl.ds(i * 8, 8), :]
            for s in (1, 2, 4):
                keep = sub >= s
                b = jnp.where(keep, a * pltpu.roll(b, s, 0) + b, b)
                a = jnp.where(keep, a * pltpu.roll(a, s, 0), a)
            hh = a * h + b
            hr_o[pl.ds(i * 8, 8), :] = hh
            return hh[7:8, :]

        carry[...] = lax.fori_loop(0, ts // 8, chunk, carry[...])
        gelu, _ = _gelu_parts(y_ref[...])
        ga_o[...] = (gelu * hr_o[...]).astype(BF16)

        u = u_ref[...]
        uu = jnp.concatenate([jnp.where(t == 0, 0.0, uh_ref[...]), u], axis=0)
        pos = (rows[:, :BLOCK] + t * ts + 1).astype(F32)
        sums, have, parts = uu, 1, []
        for g, win in enumerate(POOL_WINDOWS):
            while have < win:
                sums = sums + pltpu.roll(sums, have, 0)
                have *= 2
            mean = sums[POOL_HALO:, g * BLOCK:(g + 1) * BLOCK] * (1.0 / jnp.minimum(pos, float(win)))
            parts.append(mean - u[:, g * BLOCK:(g + 1) * BLOCK])
        pb = jnp.concatenate(parts, axis=1).astype(BF16)
        p_o[...] = pb
        mixed = jnp.concatenate([jnp.dot(pb[:, g * BLOCK:(g + 1) * BLOCK], wp_ref[g], preferred_element_type=F32)
                                 for g in range(N_BLOCKS)], axis=1) + vec_ref[4:5, :]
        pooled_o[...] = (mixed * vec_ref[5:6, :]).astype(BF16)

    halo_x = pl.BlockSpec((CONV_HALO, d), lambda t: (jnp.maximum(t * (ts // CONV_HALO) - 1, 0), 0))
    halo_u = pl.BlockSpec((POOL_HALO, d), lambda t: (jnp.maximum(t * (ts // POOL_HALO) - 1, 0), 2))
    wspec = pl.BlockSpec((N_BLOCKS, BLOCK, BLOCK), lambda t: (0, 0, 0))
    return _hosted(
        stage, body, name="seq_forward", grid=(s_rows // ts,),
        out_shape=[SDS((s_rows, d), F32), SDS((s_rows, d), F32), SDS((s_rows, d), BF16), SDS((s_rows, d), BF16), SDS((s_rows, d), BF16)],
        in_specs=[_row_spec(ts, d, 0), halo_x, _row_spec(ts, d, 1), _row_spec(ts, d, 2), halo_u, _vec_spec(CONV_WIDTH, d), _vec_spec(8, d),
                  wspec, wspec, wspec],
        out_specs=[_row_spec(ts, d)] * 5,
        scratch_shapes=[pltpu.VMEM((1, d), F32), pltpu.VMEM((ts, d), F32), pltpu.VMEM((ts, d), F32)],
        compiler_params=_params("arbitrary"),
    )(proj, proj, proj, proj, proj, conv_w, vecs, w_a, w_x, w_p)


def _branches(ga, pooled, proj, w_a, w_b, *, ts, stage=None):
    s_rows, d = ga.shape

    def body(a_ref, p_ref, ga_ref, gb_ref, wa_ref, wb_ref, merged_o, sa_o, sb_o, ta_o, tb_o):
        bra = jnp.dot(a_ref[...], wa_ref[0], preferred_element_type=F32)
        brb = jnp.dot(p_ref[...], wb_ref[0], preferred_element_type=F32)
        sa, sb = jax.nn.sigmoid(ga_ref[...]), jax.nn.sigmoid(gb_ref[...])
        merged_o[...] = (sa * bra + sb * brb).astype(BF16)
        sa_o[...] = sa.astype(BF16)
        sb_o[...] = sb.astype(BF16)
        ta_o[...] = (bra * (sa * (1.0 - sa))).astype(BF16)
        tb_o[...] = (brb * (sb * (1.0 - sb))).astype(BF16)

    rows = _row_spec(ts, d)
    wspec = pl.BlockSpec((1, d, d), lambda *g: (0, 0, 0), pipeline_mode=pl.Buffered(1))
    return _hosted(
        stage, body, name="branches", grid=(s_rows // ts,), out_shape=[SDS((s_rows, d), BF16)] * 5,
        in_specs=[rows, rows, _row_spec(ts, d, 3), _row_spec(ts, d, 4), wspec, wspec], out_specs=[rows] * 5,
        compiler_params=_params("arbitrary"),
    )(ga, pooled, proj, proj, w_a, w_b)


def _seq_backward(proj, xr, hr, p, dga, dpooled, dgab, conv_w, vecs, w_a, w_x, w_p, *, ts, stage=None):
    s_rows = proj.shape[0]
    d = D_MODEL
    n_t = s_rows // ts

    def block_dots(lhs, w_ref, dims):
        return jnp.concatenate([lax.dot_general(lhs[:, g * BLOCK:(g + 1) * BLOCK], w_ref[g], dims, preferred_element_type=F32)
                                for g in range(N_BLOCKS)], axis=1)

    def add_block_grads(dw_ref, lhs, rhs):
        for g in range(N_BLOCKS):
            dw_ref[g] += lax.dot_general(lhs[:, g * BLOCK:(g + 1) * BLOCK], rhs[:, g * BLOCK:(g + 1) * BLOCK],
                                         (((0,), (0,)), ((), ())), preferred_element_type=F32)

    nn, nt = (((1,), (0,)), ((), ())), (((1,), (1,)), ((), ()))

    def body(x_ref, y_ref, xr_ref, hr_ref, hh_ref, p_ref, dga_ref, dpl_ref, dgab_ref, cw_ref, vec_ref, wa_ref, wx_ref, wp_ref,
             dproj_o, acc_o, dwa_o, dwx_o, dwp_o, g_carry, dxr_carry, q_carry, a_scr, b_scr, g_scr):
        step = pl.program_id(0)
        t = n_t - 1 - step

        @pl.when(step == 0)
        def _():
            for ref in (acc_o, dwa_o, dwx_o, dwp_o, g_carry, dxr_carry, q_carry):
                ref[...] = jnp.zeros_like(ref)

        rows = lax.broadcasted_iota(jnp.int32, (ts, d), 0)
        start = (rows == 0) & (t == 0)
        xr = xr_ref[...]
        xrb = xr.astype(BF16)
        r = jax.nn.sigmoid(block_dots(xrb, wa_ref, nn) + vec_ref[1:2, :])
        gate_i = jax.nn.sigmoid(block_dots(xrb, wx_ref, nn) + vec_ref[2:3, :])
        sp = jax.nn.softplus(vec_ref[3:4, :])
        log_a = (-C_RG * r) * sp
        a = jnp.exp(log_a)
        m_square = _one_minus_square(a, log_a)
        mult = jnp.where(start, 1.0, jnp.sqrt(m_square))
        gelu, dgelu = _gelu_parts(y_ref[...])
        dga_t = dga_ref[...]
        hr_t = hr_ref[...]
        dy = dga_t * hr_t * dgelu

        a_scr[...] = jnp.where(rows == ts - 1, 1.0, pltpu.roll(a, ts - 1, 0))
        b_scr[...] = dga_t * gelu
        sub = lax.broadcasted_iota(jnp.int32, (8, d), 0)

        def chunk(i, g_next):
            at = pl.multiple_of((ts // 8 - 1 - i) * 8, 8)
            aa = a_scr[pl.ds(at, 8), :]
            bb = b_scr[pl.ds(at, 8), :]
            for s in (1, 2, 4):
                keep = sub < 8 - s
                bb = jnp.where(keep, bb + aa * pltpu.roll(bb, 8 - s, 0), bb)
                aa = jnp.where(keep, aa * pltpu.roll(aa, 8 - s, 0), aa)
            gg = aa * g_next + bb
            g_scr[pl.ds(at, 8), :] = gg
            return gg[0:1, :]

        g_first = lax.fori_loop(0, ts // 8, chunk, g_carry[...])
        g_carry[...] = a[0:1, :] * g_first
        g = g_scr[...]

        h_before = jnp.where(t == 0, 0.0, hh_ref[CONV_HALO - 1:CONV_HALO, :])
        h_prev = jnp.where(rows == 0, h_before, pltpu.roll(hr_t, 1, 0))
        dxr = g * gate_i * mult
        d_i = g * xr * mult
        d_mult = g * xr * gate_i
        dlog_a = (g * h_prev) * a - jnp.where(start, 0.0, d_mult * (a * a) * lax.rsqrt(m_square))
        dzr = (dlog_a * (-C_RG * sp)) * (r * (1.0 - r))
        dzi = d_i * (gate_i * (1.0 - gate_i))
        _accumulate(acc_o, 5, dzr)
        _accumulate(acc_o, 6, dzi)
        _accumulate(acc_o, 7, dlog_a * (-C_RG * r) * jax.nn.sigmoid(vec_ref[3:4, :]))
        dzrb, dzib = dzr.astype(BF16), dzi.astype(BF16)
        add_block_grads(dwa_o, xrb, dzrb)
        add_block_grads(dwx_o, xrb, dzib)
        dxr = dxr + block_dots(dzrb, wa_ref, nt) + block_dots(dzib, wx_ref, nt)

        x = x_ref[...]
        ext = jnp.concatenate([dxr, dxr_carry[...]], axis=0)
        dx = dxr * cw_ref[CONV_WIDTH - 1:CONV_WIDTH, :]
        _accumulate(acc_o, CONV_WIDTH - 1, x * dxr)
        for j in range(CONV_WIDTH - 1):
            ahead = pltpu.roll(ext, ts + CONV_HALO - (CONV_WIDTH - 1 - j), 0)[:ts]
            dx = dx + ahead * cw_ref[j:j + 1, :]
            _accumulate(acc_o, j, x * ahead)
        _accumulate(acc_o, 4, dxr)
        dxr_carry[...] = dxr[0:CONV_HALO, :]

        pb = p_ref[...]
        mixed = block_dots(pb, wp_ref, nn) + vec_ref[4:5, :]
        dpl = dpl_ref[...]
        _accumulate(acc_o, 9, dpl * mixed)
        dmixed = dpl * vec_ref[5:6, :]
        _accumulate(acc_o, 8, dmixed)
        dmb = dmixed.astype(BF16)
        add_block_grads(dwp_o, pb, dmb)
        dp = block_dots(dmb, wp_ref, nt)
        pos = (rows[:, :BLOCK] + t * ts + 1).astype(F32)
        q = jnp.concatenate([dp[:, g * BLOCK:(g + 1) * BLOCK] * (1.0 / jnp.minimum(pos, float(win))) for g, win in enumerate(POOL_WINDOWS)], axis=1)
        sums, have, parts = jnp.concatenate([q, q_carry[...]], axis=0), 1, []
        for g, win in enumerate(POOL_WINDOWS):
            while have < win:
                sums = sums + pltpu.roll(sums, ts + POOL_HALO - have, 0)
                have *= 2
            parts.append(sums[:ts, g * BLOCK:(g + 1) * BLOCK])
        du = jnp.concatenate(parts, axis=1) - dp
        q_carry[...] = q[0:POOL_HALO, :]

        dproj_o[:, 0:d] = dx.astype(BF16)
        dproj_o[:, d:2 * d] = dy.astype(BF16)
        dproj_o[:, 2 * d:3 * d] = du.astype(BF16)
        dproj_o[:, 3 * d:5 * d] = dgab_ref[...]

    def rev(cols, col_block=0):
        return pl.BlockSpec((ts, cols), lambda i: (n_t - 1 - i, col_block))

    halo_h = pl.BlockSpec((CONV_HALO, d), lambda i: (jnp.maximum((n_t - 1 - i) * (ts // CONV_HALO) - 1, 0), 0))
    wspec = pl.BlockSpec((N_BLOCKS, BLOCK, BLOCK), lambda i: (0, 0, 0))
    return _hosted(
        stage, body, name="seq_backward", grid=(n_t,),
        out_shape=[SDS((s_rows, 5 * d), BF16), SDS((16, d), F32)] + [SDS((N_BLOCKS, BLOCK, BLOCK), F32)] * 3,
        in_specs=[rev(d, 0), rev(d, 1), rev(d), rev(d), halo_h, rev(d), rev(d), rev(d), rev(2 * d), _vec_spec(CONV_WIDTH, d), _vec_spec(8, d),
                  wspec, wspec, wspec],
        out_specs=[rev(5 * d), _vec_spec(16, d), wspec, wspec, wspec],
        scratch_shapes=[pltpu.VMEM((1, d), F32), pltpu.VMEM((CONV_HALO, d), F32), pltpu.VMEM((POOL_HALO, d), F32),
                        pltpu.VMEM((ts, d), F32), pltpu.VMEM((ts, d), F32), pltpu.VMEM((ts, d), F32)],
        compiler_params=_params("arbitrary"),
    )(proj, proj, xr, hr, hr, p, dga, dpooled, dgab, conv_w, vecs, w_a, w_x, w_p)


def _adamw_math(w, g, m, v):
    m = ADAM_B1 * m + (1.0 - ADAM_B1) * g
    v = ADAM_B2 * v + (1.0 - ADAM_B2) * jnp.square(g)
    m_hat = m / (1.0 - ADAM_B1 ** ADAM_STEP)
    v_hat = v / (1.0 - ADAM_B2 ** ADAM_STEP)
    delta = -ADAM_LR * (m_hat / (jnp.sqrt(v_hat) + ADAM_EPS) + ADAM_WD * w)
    return delta, m, v


def _adamw(w, g, m, v, name):
    rows, cols = w.shape
    tr = min(rows, 256)

    def body(w_ref, g_ref, m_ref, v_ref, d_ref, mo_ref, vo_ref):
        d_ref[...], mo_ref[...], vo_ref[...] = _adamw_math(w_ref[...], g_ref[...], m_ref[...], v_ref[...])

    spec = pl.BlockSpec((tr, cols), lambda i: (i, 0))
    return pl.pallas_call(
        body, name=name, grid=(rows // tr,), out_shape=[SDS((rows, cols), F32)] * 3, in_specs=[spec] * 4, out_specs=[spec] * 3,
        compiler_params=_params("parallel"),
    )(w, g, m, v)


def _ada_forward(c_all, w_ada, b_cols):
    n_cols = w_ada.shape[1]

    def body(c_ref, w_ref, b_ref, act_ref, mod_ref):
        cv = c_ref[...]
        act = cv * jax.nn.sigmoid(cv)
        act_ref[...] = act
        mod_ref[...] = jnp.dot(act.astype(BF16), w_ref[...].astype(BF16), preferred_element_type=F32) + b_ref[...]

    return pl.pallas_call(
        body, name="ada_forward", out_shape=[SDS(c_all.shape, F32), SDS((N_DEV, n_cols), F32)],
        in_specs=[VMEM_SPEC] * 3, out_specs=[VMEM_SPEC] * 2, compiler_params=pltpu.CompilerParams(vmem_limit_bytes=VMEM_LIMIT_V7X),
    )(c_all, w_ada, b_cols)


def _ada_backward(act_t, dmod_cols, w, m, v):
    rows, cols = w.shape
    tn = 512

    def body(a_ref, dm_ref, w_ref, m_ref, v_ref, g_ref, d_ref, mo_ref, vo_ref):
        act = a_ref[...].astype(BF16).astype(F32)
        dm = dm_ref[...].astype(BF16).astype(F32)
        g = act[:, 0:1] * dm[0:1, :]
        for b in range(1, N_DEV):
            g = g + act[:, b:b + 1] * dm[b:b + 1, :]
        g_ref[...] = g
        d_ref[...], mo_ref[...], vo_ref[...] = _adamw_math(w_ref[...], g, m_ref[...], v_ref[...])

    spec = pl.BlockSpec((rows, tn), lambda j: (0, j))
    return pl.pallas_call(
        body, name="ada_backward", grid=(cols // tn,), out_shape=[SDS((rows, cols), F32)] * 4,
        in_specs=[pl.BlockSpec((rows, N_DEV), lambda j: (0, 0)), pl.BlockSpec((N_DEV, tn), lambda j: (0, j)), spec, spec, spec],
        out_specs=[spec] * 4, compiler_params=_params("parallel"),
    )(act_t, dmod_cols, w, m, v)


def _sum_rows(gathered, n_loss):
    n = gathered.shape[2]

    def body(g_ref, o_ref):
        acc = g_ref[0]
        for r in range(1, N_DEV):
            acc = acc + g_ref[r]
        o_ref[...] = acc
        o_ref[:, n - n_loss:n] = jnp.broadcast_to(jnp.sum(acc[:, n - n_loss:n], axis=1, keepdims=True), (1, n_loss))

    return pl.pallas_call(body, name="sum_rows", out_shape=SDS((1, n), F32), in_specs=[VMEM_SPEC], out_specs=VMEM_SPEC)(gathered)


def _adamw_vectors(total, offsets, separate, params):
    n_p = len(params)
    apart = [g for g in separate if g is not None]

    def body(*refs):
        total_ref, apart_refs = refs[0], list(refs[1:1 + len(apart)])
        ins, outs = refs[1 + len(apart):1 + len(apart) + 3 * n_p], refs[1 + len(apart) + 3 * n_p:]
        for i in range(n_p):
            w_ref, m_ref, v_ref = ins[3 * i:3 * i + 3]
            g = apart_refs.pop(0)[...] if offsets[i] is None else total_ref[:, offsets[i]:offsets[i] + w_ref.shape[1]]
            outs[4 * i][...] = g
            outs[4 * i + 1][...], outs[4 * i + 2][...], outs[4 * i + 3][...] = _adamw_math(w_ref[...], g, m_ref[...], v_ref[...])

    flat = [a for p in params for a in p]
    res = pl.pallas_call(
        body, name="adamw_vectors", out_shape=[SDS(p[0].shape, F32) for p in params for _ in range(4)],
        in_specs=[VMEM_SPEC] * (1 + len(apart) + len(flat)), out_specs=[VMEM_SPEC] * (4 * n_p),
    )(total, *apart, *flat)
    return [tuple(res[4 * i:4 * i + 4]) for i in range(n_p)]


def _blocks_to_pieces(w):
    nb, rows, n = w.shape
    q = rows // N_CHIPS
    return w.reshape(nb, N_CHIPS, q, n).transpose(1, 0, 2, 3).reshape(N_CHIPS, 2, nb // 2, q, n)


def _pieces_to_blocks(w):
    n_chips, _, half, q, n = w.shape
    return w.reshape(n_chips, 2 * half, q, n).transpose(1, 0, 2, 3).reshape(2 * half, n_chips * q, n)


def kernel(x, c, norm_mix_g, norm_mlp_g, w_ada, b_ada, w_in, conv_w, conv_b, w_rg_a, b_rg_a, w_rg_x, b_rg_x, a_param, w_branch_a, w_pool, b_pool, pool_scale, w_branch_b, w_out, w_up, w_down, final_g, loss_target, m_norm_mix_g, m_norm_mlp_g, m_w_ada, m_b_ada, m_w_in, m_conv_w, m_conv_b, m_w_rg_a, m_b_rg_a, m_w_rg_x, m_b_rg_x, m_a_param, m_w_branch_a, m_w_pool, m_b_pool, m_pool_scale, m_w_branch_b, m_w_out, m_w_up, m_w_down, m_final_g, v_norm_mix_g, v_norm_mlp_g, v_w_ada, v_b_ada, v_w_in, v_conv_w, v_conv_b, v_w_rg_a, v_b_rg_a, v_w_rg_x, v_b_rg_x, v_a_param, v_w_branch_a, v_w_pool, v_b_pool, v_pool_scale, v_w_branch_b, v_w_out, v_w_up, v_w_down, v_final_g):
    d = D_MODEL
    s_rows = x.shape[1]
    ts, tsq = min(512, s_rows), min(256, s_rows)
    xi, yi, ci = _coords()
    k_me, me = 2 * xi + yi, 4 * xi + 2 * yi + ci
    ada_cols = w_ada.shape[2]
    conv_cols = conv_w.shape[2]
    n_in, n_up = w_in.shape[2], w_up.shape[2]

    names = ("w_in", "w_up", "w_down", "w_a", "w_b", "w_out", "rg_a", "rg_x", "pool")
    mine = dict(zip(names, (w_in[0], w_up[0], w_down[0], w_branch_a[0], w_branch_b[0], w_out[0], w_rg_a[0], w_rg_x[0], w_pool[0])))
    moments_m = dict(zip(names, (m_w_in[0], m_w_up[0], m_w_down[0], m_w_branch_a[0], m_w_branch_b[0], m_w_out[0], m_w_rg_a[0], m_w_rg_x[0], m_w_pool[0])))
    moments_v = dict(zip(names, (v_w_in[0], v_w_up[0], v_w_down[0], v_w_branch_a[0], v_w_branch_b[0], v_w_out[0], v_w_rg_a[0], v_w_rg_x[0], v_w_pool[0])))
    block_weights, squares = ("rg_a", "rg_x", "pool"), ("w_a", "w_b", "w_out")
    place = jnp.stack([ci, k_me]).astype(jnp.int32)
    g_mix, g_mlp, g_fin = norm_mix_g, norm_mlp_g, final_g.reshape(1, d)
    x0, target = x[0], loss_target[0]
    wg = functools.partial(_weight_grad, tk=min(512, d))
    plain, back, back_sum = (dict(transposed=t, reduce_blocks=r) for t, r in ((False, False), (True, False), (True, True)))
    rows_d, vec8, vec1 = _row_spec(ts, d), _vec_spec(8, d), _vec_spec(1, d)
    tw = min(1024, s_rows)
    halves_of_tile = 2

    def out_rows(cols, dtype, t=ts):
        return SDS((s_rows, cols), dtype), _row_spec(t, cols)

    sums_out = (SDS((8, d), F32), vec8)

    def halves(n):
        return mine[n].astype(BF16).reshape(2, -1, mine[n].shape[-1])

    def blocks(w):
        return _pieces_to_blocks(w.reshape(N_CHIPS, 2, N_BLOCKS // 2, BLOCK // N_CHIPS, BLOCK))

    stage = _Stage()
    i_c, i_cw, i_in = _add_allgather8(stage, c), _add_allgather8(stage, conv_w[0]), _add_gather_ici(stage, halves("w_in"))
    got = _run_stage(stage, "exchange_c_conv_w_in")
    conv_full = got[i_cw][0::2].transpose(1, 0, 2).reshape(CONV_WIDTH, d)
    act_all, mod_cols = _ada_forward(got[i_c].reshape(N_DEV, d), w_ada[0], lax.dynamic_slice_in_dim(b_ada, k_me * ada_cols, ada_cols, axis=1))
    stage = _Stage()
    i_mod, i_in = _add_allgather8(stage, mod_cols), _add_gather_d2d(stage, got[i_in])
    i_blk = [_add_gather_ici(stage, halves(n)) for n in block_weights]
    got = _run_stage(stage, "exchange_mod_w_in_gates")
    mod_mine = lax.dynamic_index_in_dim(got[i_mod], me, axis=1, keepdims=False)[0::2]
    mod = jnp.concatenate([mod_mine.reshape(N_MOD, d), jnp.zeros((8 - N_MOD, d), F32)], axis=0)
    w_in_all = got[i_in].reshape(N_CHIPS, d, n_in)
    vecs = jnp.concatenate([conv_b, b_rg_a, b_rg_x, a_param, b_pool, pool_scale, jnp.zeros((2, d), F32)], axis=0)

    def norm_first(a_ref, extra_refs, out_refs):
        h = _modulated_norm(a_ref[...], extra_refs[0], extra_refs[1], 1, 0)
        out_refs[1][...] = h
        return h

    stage = _Stage()
    i_blk = [_add_gather_d2d(stage, got[i]) for i in i_blk]
    i_sq = [_add_gather_ici(stage, halves(n)) for n in squares]
    down_rows = mine["w_down"].shape[0] // 8
    i_down = _add_gather_ici(stage, halves("w_down"), rows=(0, down_rows))
    (proj, h1), got = _matmul(x0, w_in_all, name="proj_in", tm=ts, extra=[(mod, vec8), (g_mix, vec1)], prepare=norm_first,
                              outs=[out_rows(N_CHIPS * n_in, F32), out_rows(d, BF16)], finish=_store_blocks(n_in), parts=halves_of_tile,
                              stage=stage, **plain)
    w_rg_a_all, w_rg_x_all, w_pool_all = (blocks(got[i]) for i in i_blk)
    stage = _Stage()
    i_sq = [_add_gather_d2d(stage, got[i]) for i in i_sq]
    i_up = _add_gather_ici(stage, halves("w_up"))
    i_down = _add_gather_ici(stage, halves("w_down"), rows=(down_rows, down_rows), whole=got[i_down])
    (xr, hr, ga, p, pooled), got = _seq_forward(proj, conv_full, vecs, w_rg_a_all, w_rg_x_all, w_pool_all, ts=tsq, stage=stage)
    w_a_all, w_b_all, w_out_all = (got[i].reshape(1, d, d) for i in i_sq)
    stage = _Stage()
    i_up = _add_gather_d2d(stage, got[i_up])
    i_down = _add_gather_ici(stage, halves("w_down"), rows=(2 * down_rows, 2 * down_rows), whole=got[i_down])
    (merged, s_a, s_b, t_a, t_b), got = _branches(ga, pooled, proj, w_a_all, w_b_all, ts=ts, stage=stage)
    w_up_all = got[i_up].reshape(N_CHIPS, d, n_up)
    stage = _Stage()
    i_down = _add_gather_d2d(stage, got[i_down])

    def residual_norm(k, acc, extra_refs, out_refs):
        x_ref, mod_ref, g_ref = extra_refs
        out_refs[0][...] = acc
        x2_t = x_ref[...] + mod_ref[2:3, :] * acc
        out_refs[1][...] = x2_t
        out_refs[2][...] = _modulated_norm(x2_t, mod_ref, g_ref, 4, 3)

    (mo, x2, h2), got = _matmul(merged, w_out_all, name="mix_out", tm=ts, extra=[(x0, rows_d), (mod, vec8), (g_mlp, vec1)],
                                outs=[out_rows(d, F32), out_rows(d, F32), out_rows(d, BF16)], finish=residual_norm, parts=halves_of_tile,
                                stage=stage, transposed=False, reduce_blocks=True)
    w_down_all = got[i_down].reshape(N_CHIPS, D_FF // N_CHIPS, d)

    def relu_squared(k, acc, extra_refs, out_refs):
        out_refs[0][:, k * n_up:(k + 1) * n_up] = jnp.square(jnp.maximum(acc, 0.0)).astype(BF16)

    ff, = _matmul(h2, w_up_all, name="mlp_up", tm=ts, outs=[out_rows(D_FF, BF16)], finish=relu_squared, parts=halves_of_tile, **plain)

    def loss_head(k, acc, extra_refs, out_refs):
        x2_ref, mod_ref, g_ref, t_ref = extra_refs
        dx_ref, df_ref, acc_ref = out_refs
        xh, r = _rms(x2_ref[...] + mod_ref[5:6, :] * acc)
        err = xh * g_ref[...] - t_ref[...]
        dy = err / d
        dxh = dy * g_ref[...]
        dx3_t = r * (dxh - xh * jnp.mean(dxh * xh, axis=-1, keepdims=True))
        dx_ref[...] = dx3_t
        df_ref[...] = (dx3_t * mod_ref[5:6, :]).astype(BF16)
        _accumulate(acc_ref, 0, dy * xh)
        _accumulate(acc_ref, 1, dx3_t * acc)
        _accumulate(acc_ref, 2, jnp.square(err) * (0.5 / d))

    dx3, dffo, sums_head = _matmul(ff, w_down_all, name="mlp_down", tm=ts, extra=[(x2, rows_d), (mod, vec8), (g_fin, vec1), (target, rows_d)],
                                   outs=[out_rows(d, F32), out_rows(d, BF16), sums_out], finish=loss_head, zero_first=(2,), parts=halves_of_tile,
                                   transposed=False, reduce_blocks=True)

    partial_of, from_sibling, chip_sum, from_chips, half_done, quarter = {}, {}, {}, {}, {}, {}

    def publish(n, g):
        g = _blocks_to_pieces(g).astype(BF16) if n in block_weights else g
        partial_of[n] = g.reshape(N_CHIPS, 2, -1, mine[n].shape[-1])

    def exchange(to_sibling=(), to_chips=(), swap=()):
        st = _Stage()
        slots = [(n, from_sibling, _add_reduce_d2d(st, partial_of[n])) for n in to_sibling]
        slots += [(n, from_chips, _add_reduce_ici(st, chip_sum[n])) for n in to_chips]
        slots += [(n, quarter, _add_swap_halves(st, half_done[n])) for n in swap]
        return st, slots

    def collect(slots, outs):
        for n, where, i in slots:
            where[n] = outs[i]

    def sum_pairs(*ns):
        for n in ns:
            chip_sum[n] = _sum_pair(partial_of[n], from_sibling[n], place, "sum_pair_" + n)

    def sum_quarters(*ns):
        for n in ns:
            half_done[n] = _sum_quarter(partial_of[n], from_sibling[n], from_chips[n], place, "sum_quarter_" + n)

    publish("w_down", wg(ff, dffo, nb=1, name="grad_w_down"))

    def relu_squared_backward(k, acc, extra_refs, out_refs):
        cols = slice(k * n_up, (k + 1) * n_up)
        out_refs[0][:, cols] = (acc * (2.0 * jnp.sqrt(extra_refs[0][:, cols].astype(F32)))).astype(BF16)

    stage, slots = exchange(to_sibling=["w_down"])
    (dup,), got = _matmul(dffo, w_down_all, name="d_mlp_down", tm=ts, extra=[(ff, _row_spec(ts, D_FF))], outs=[out_rows(D_FF, BF16)],
                          finish=relu_squared_backward, parts=halves_of_tile, stage=stage, **back)
    collect(slots, got)
    sum_pairs("w_down")

    def norm_mlp_backward(k, acc, extra_refs, out_refs):
        x2_ref, dres_ref, mod_ref, g_ref, mo_ref = extra_refs
        dx_ref, dmo_ref, acc_ref = out_refs
        dx =_modulated_norm_backward(acc, x2_ref[...], dres_ref[...], mod_ref, g_ref, acc_ref, 4)
        dx_ref[...] = dx
        dmo_ref[...] = (dx * mod_ref[2:3, :]).astype(BF16)
        _accumulate(acc_ref, 3, dx * mo_ref[...])

    stage, slots = exchange(to_chips=["w_down"])
    (dx2, dmo, sums_mlp), got = _matmul(dup, w_up_all, name="d_mlp_up", tm=ts, stage=stage, finish=norm_mlp_backward,
                                        extra=[(x2, rows_d), (dx3, rows_d), (mod, vec8), (g_mlp, vec1), (mo, rows_d)],
                                        outs=[out_rows(d, F32), out_rows(d, BF16), sums_out], zero_first=(2,), parts=halves_of_tile, **back_sum)
    collect(slots, got)
    sum_quarters("w_down")
    stage, slots = exchange(swap=["w_down"])
    g_up, got = wg(h2, dup, nb=N_CHIPS, name="grad_w_up", stage=stage)
    collect(slots, got)
    publish("w_up", g_up)
    publish("w_out", wg(merged, dmo, nb=1, name="grad_w_out"))

    def merge_backward(k, acc, extra_refs, out_refs):
        sa_ref, sb_ref, ta_ref, tb_ref = extra_refs
        out_refs[0][...] = (acc * sa_ref[...].astype(F32)).astype(BF16)
        out_refs[1][...] = (acc * sb_ref[...].astype(F32)).astype(BF16)
        out_refs[2][:, 0:d] = (acc * ta_ref[...].astype(F32)).astype(BF16)
        out_refs[2][:, d:2 * d] = (acc * tb_ref[...].astype(F32)).astype(BF16)

    stage, slots = exchange(to_sibling=["w_up", "w_out"])
    (dbr_a, dbr_b, dgab), got = _matmul(dmo, w_out_all, name="d_mix_out", tm=ts, stage=stage, finish=merge_backward,
                                        extra=[(s_a, rows_d), (s_b, rows_d), (t_a, rows_d), (t_b, rows_d)],
                                        outs=[out_rows(d, BF16), out_rows(d, BF16), out_rows(2 * d, BF16)], parts=halves_of_tile, **back_sum)
    collect(slots, got)
    publish("w_a", wg(ga, dbr_a, nb=1, name="grad_w_branch_a"))
    publish("w_b", wg(pooled, dbr_b, nb=1, name="grad_w_branch_b"))
    stage, slots = exchange(to_sibling=["w_a", "w_b"])
    (dga,), got = _matmul(dbr_a, w_a_all, name="d_branch_a", tm=tw, outs=[out_rows(d, F32, tw)], finish=_store_blocks(d), stage=stage, **back_sum)
    collect(slots, got)
    dpooled, = _matmul(dbr_b, w_b_all, name="d_branch_b", tm=tw, outs=[out_rows(d, F32, tw)], finish=_store_blocks(d), **back_sum)
    sum_pairs("w_up", *squares)
    stage, slots = exchange(to_chips=("w_up",) + squares)
    (dproj, sums_seq, d_rg_a, d_rg_x, d_pool), got = _seq_backward(proj, xr, hr, p, dga, dpooled, dgab, conv_full, vecs,
                                                                   w_rg_a_all, w_rg_x_all, w_pool_all, ts=tsq, stage=stage)
    collect(slots, got)
    sum_quarters("w_up", *squares)
    for n, g in zip(block_weights, (d_rg_a, d_rg_x, d_pool)):
        publish(n, g)
    stage, slots = exchange(to_sibling=block_weights, swap=("w_up",) + squares)
    g_in, got = wg(h1, dproj, nb=N_CHIPS, name="grad_w_in", stage=stage)
    collect(slots, got)
    publish("w_in", g_in)
    sum_pairs(*block_weights)
    stage, slots = exchange(to_sibling=["w_in"], to_chips=block_weights)
    collect(slots, _run_stage(stage, "exchange_w_in_gates"))
    sum_pairs("w_in")
    sum_quarters(*block_weights)

    def norm_mix_backward(k, acc, extra_refs, out_refs):
        x_ref, dres_ref, mod_ref, g_ref = extra_refs
        out_refs[0][...] = _modulated_norm_backward(acc, x_ref[...], dres_ref[...], mod_ref, g_ref, out_refs[1], 1)

    stage, slots = exchange(to_chips=["w_in"], swap=block_weights)
    (grad_x, sums_mix), got = _matmul(dproj, w_in_all, name="d_proj_in", tm=ts, stage=stage, finish=norm_mix_backward,
                                      extra=[(x0, rows_d), (dx2, rows_d), (mod, vec8), (g_mix, vec1)],
                                      outs=[out_rows(d, F32), sums_out], zero_first=(1,), parts=halves_of_tile, **back_sum)
    collect(slots, got)
    sum_quarters("w_in")

    dmod = jnp.concatenate([sums_mix[1:2], sums_mix[0:1], sums_mlp[3:4], sums_mlp[1:2], sums_mlp[0:1], sums_head[1:2]], axis=1)
    row = jnp.concatenate([sums_mix[2:3], sums_mlp[2:3], sums_seq[4:5], sums_seq[5:6], sums_seq[6:7], sums_seq[7:8], sums_seq[8:9],
                           sums_seq[9:10], sums_head[0:1], sums_seq[0:CONV_WIDTH].reshape(1, CONV_WIDTH * d), dmod, sums_head[2:3]], axis=1)
    n_vec, n_conv, n_mod = 9 * d, CONV_WIDTH * d, N_MOD * d
    stage, slots = exchange(swap=["w_in"])
    i_rows = _add_allgather8(stage, row)
    got = _run_stage(stage, "exchange_sums_w_in")
    collect(slots, got)
    rows_all = got[i_rows]
    total = _sum_rows(rows_all, d)
    loss = total[0, n_vec + n_conv + n_mod]
    dmod_all = rows_all[:, 0, n_vec + n_conv:n_vec + n_conv + n_mod]
    g_ada, d_ada, m_ada, v_ada = _ada_backward(act_all.T, lax.dynamic_slice_in_dim(dmod_all, k_me * ada_cols, ada_cols, axis=1),
                                               w_ada[0], m_w_ada[0], v_w_ada[0])
    g_conv = lax.dynamic_slice_in_dim(total[:, n_vec:n_vec + n_conv].reshape(CONV_WIDTH, d), k_me * conv_cols, conv_cols, axis=1)
    vec_names = ("norm_mix_g", "norm_mlp_g", "conv_b", "b_rg_a", "b_rg_x", "a_param", "b_pool", "pool_scale", "final_g", "conv_w", "b_ada")
    vec_params = [(norm_mix_g, m_norm_mix_g, v_norm_mix_g), (norm_mlp_g, m_norm_mlp_g, v_norm_mlp_g), (conv_b, m_conv_b, v_conv_b),
                  (b_rg_a, m_b_rg_a, v_b_rg_a), (b_rg_x, m_b_rg_x, v_b_rg_x), (a_param, m_a_param, v_a_param), (b_pool, m_b_pool, v_b_pool),
                  (pool_scale, m_pool_scale, v_pool_scale), tuple(a.reshape(1, d) for a in (final_g, m_final_g, v_final_g)),
                  (conv_w[0], m_conv_w[0], v_conv_w[0]), (b_ada, m_b_ada, v_b_ada)]
    offsets = [i * d for i in range(9)] + [None, n_vec + n_conv]
    vec_res = _adamw_vectors(total, offsets, [None] * 9 + [g_conv, None], vec_params)
    vec_out = {n: [r.reshape(shape) for r in res] for n, res, shape in zip(
        vec_names, vec_res, [(1, d)] * 8 + [(d,), (1, CONV_WIDTH, conv_cols), (1, N_MOD * d)])}

    big_out = {}
    for n in names:
        shape2 = (-1, mine[n].shape[-1])
        g2 = quarter[n].reshape(shape2)
        res = (g2,) + tuple(_adamw(mine[n].reshape(shape2), g2, moments_m[n].reshape(shape2), moments_v[n].reshape(shape2), "adamw_" + n))
        big_out[n] = [r.reshape((1,) + mine[n].shape) for r in res]

    key = {"w_in": "w_in", "w_rg_a": "rg_a", "w_rg_x": "rg_x", "w_branch_a": "w_a", "w_pool": "pool", "w_branch_b": "w_b", "w_out": "w_out",
           "w_up": "w_up", "w_down": "w_down"}
    order = ("norm_mix_g", "norm_mlp_g", "w_ada", "b_ada", "w_in", "conv_w", "conv_b", "w_rg_a", "b_rg_a", "w_rg_x", "b_rg_x", "a_param",
             "w_branch_a", "w_pool", "b_pool", "pool_scale", "w_branch_b", "w_out", "w_up", "w_down", "final_g")
    ada_out = [g_ada[None], d_ada[None], m_ada[None], v_ada[None]]
    outs = [loss, grad_x[None]]
    for kind in range(4):
        for nme in order:
            outs.append(ada_out[kind] if nme == "w_ada" else big_out[key[nme]][kind] if nme in key else vec_out[nme][kind])
    return tuple(outs)
```

```python
import functools

import jax
import jax.numpy as jnp
from jax import lax
from jax.experimental import pallas as pl
from jax.experimental.pallas import tpu as pltpu

F32, BF16 = jnp.float32, jnp.bfloat16
D_MODEL = 1024
D_FF = 4 * D_MODEL
N_BLOCKS = 4
BLOCK = D_MODEL // N_BLOCKS
CONV_WIDTH = 4
POOL_WINDOWS = (2, 4, 8, 16)
CONV_HALO = 8
POOL_HALO = 16
N_MOD = 6
EPS = 1e-6
C_RG = 8.0
ADAM_LR, ADAM_B1, ADAM_B2, ADAM_EPS, ADAM_WD, ADAM_STEP = 0.001, 0.9, 0.999, 1e-08, 0.01, 10
N_DEV = 8
N_CHIPS = 4
VMEM_LIMIT_V7X = 56 * 2**20
MESH = pl.DeviceIdType.MESH
SDS = jax.ShapeDtypeStruct
HBM_SPEC = pl.BlockSpec(memory_space=pltpu.HBM)
VMEM_SPEC = pl.BlockSpec(memory_space=pltpu.VMEM)


def _params(*semantics):
    return pltpu.CompilerParams(dimension_semantics=semantics, vmem_limit_bytes=VMEM_LIMIT_V7X)


def _coords():
    return lax.axis_index("x"), lax.axis_index("y"), lax.axis_index("c")


def _flip(v, bit):
    return 1 - v if bit else v


def _chips():
    x, y, _ = _coords()
    return [(1 - x, y), (x, 1 - y), (1 - x, 1 - y)]


class _Stage:
    def __init__(self):
        self.inputs, self.in_specs, self.outputs, self.out_specs, self.aliases = [], [], [], [], {}
        self.parts, self.n_copies = [], 0

    def add(self, inputs, in_spec, outputs, out_spec, n_copies, build, alias=False):
        i0, o0 = len(self.inputs), len(self.outputs)
        self.inputs += inputs
        self.in_specs += [in_spec] * len(inputs)
        self.outputs += outputs
        self.out_specs += [out_spec] * len(outputs)
        if alias:
            self.aliases.update({i0 + i: o0 + i for i in range(len(inputs))})
        self.parts.append((build, i0, len(inputs), o0, len(outputs)))
        self.n_copies += n_copies
        return list(range(o0, o0 + len(outputs)))

    def copies(self, in_refs, out_refs):
        out = []
        for build, i0, ni, o0, no in self.parts:
            out += build(in_refs[i0:i0 + ni], out_refs[o0:o0 + no])
        assert len(out) == self.n_copies
        return out

    def run(self, in_refs, out_refs, send_sems, recv_sems, start):
        x, y, c = _coords()
        for s, (src, dst, to, landing) in enumerate(self.copies(in_refs, out_refs)):
            if to is None:
                cp = pltpu.make_async_copy(src, dst, send_sems.at[s])
                cp.start() if start else cp.wait()
                continue
            cp = pltpu.make_async_remote_copy(src_ref=src, dst_ref=dst, send_sem=send_sems.at[s], recv_sem=recv_sems.at[s],
                                              device_id=to, device_id_type=MESH)
            if start:
                cp.start()
            else:
                pltpu.make_async_remote_copy(src_ref=landing, dst_ref=landing, send_sem=send_sems.at[s], recv_sem=recv_sems.at[s],
                                             device_id=(x, y, c), device_id_type=MESH).wait_recv()
                cp.wait_send()


def _hosted(stage, body, *, name, in_specs, out_specs, out_shape, grid=(), scratch_shapes=(), compiler_params=None):
    if stage is None:
        return pl.pallas_call(body, name=name, grid=grid, in_specs=in_specs, out_specs=out_specs, out_shape=out_shape,
                              scratch_shapes=list(scratch_shapes), compiler_params=compiler_params)
    single = not isinstance(out_shape, (list, tuple))
    h_out_shape = [out_shape] if single else list(out_shape)
    h_out_specs = [out_specs] if single else list(out_specs)
    n_in, n_out, n_scr = len(in_specs), len(h_out_shape), len(scratch_shapes)
    s_in, s_out = len(stage.inputs), len(stage.outputs)

    def wrapped(*refs):
        h_in, st_in = refs[:n_in], refs[n_in:n_in + s_in]
        h_o, st_o = refs[n_in + s_in:n_in + s_in + n_out], refs[n_in + s_in + n_out:n_in + s_in + n_out + s_out]
        h_scr = refs[n_in + s_in + n_out + s_out:n_in + s_in + n_out + s_out + n_scr]
        send_sems, recv_sems = refs[n_in + s_in + n_out + s_out + n_scr:]
        if not grid:
            stage.run(st_in, st_o, send_sems, recv_sems, True)
            if body is not None:
                body(*h_in, *h_o, *h_scr)
            stage.run(st_in, st_o, send_sems, recv_sems, False)
            return
        ids = [pl.program_id(a) for a in range(len(grid))]
        first = functools.reduce(jnp.logical_and, [i == 0 for i in ids])
        last = functools.reduce(jnp.logical_and, [i == g - 1 for i, g in zip(ids, grid)])
        pl.when(first)(lambda: stage.run(st_in, st_o, send_sems, recv_sems, True))
        body(*h_in, *h_o, *h_scr)
        pl.when(last)(lambda: stage.run(st_in, st_o, send_sems, recv_sems, False))

    call = pl.pallas_call(
        wrapped, name=name, grid=grid, in_specs=list(in_specs) + stage.in_specs, out_specs=h_out_specs + stage.out_specs,
        out_shape=h_out_shape + stage.outputs, input_output_aliases={n_in + i: n_out + o for i, o in stage.aliases.items()},
        scratch_shapes=list(scratch_shapes) + [pltpu.SemaphoreType.DMA((stage.n_copies,)), pltpu.SemaphoreType.DMA((stage.n_copies,))],
        compiler_params=pltpu.CompilerParams(dimension_semantics=("arbitrary",) * len(grid), vmem_limit_bytes=VMEM_LIMIT_V7X),
    )

    def run(*operands):
        outs = call(*operands, *stage.inputs)
        host = outs[:n_out]
        return (host[0] if single else host), outs[n_out:]

    return run


def _run_stage(stage, name):
    return _hosted(stage, None, name=name, in_specs=[], out_specs=[], out_shape=[])()[1]


def _add_allgather8(stage, v):
    def build(ins, outs):
        x, y, c = _coords()
        me = 4 * x + 2 * y + c
        copies = [(ins[0], outs[0].at[me], None, None)]
        for r in range(1, N_DEV):
            peer = (_flip(x, r & 4), _flip(y, r & 2), _flip(c, r & 1))
            copies.append((ins[0], outs[0].at[me], peer, outs[0].at[me ^ r]))
        return copies

    return stage.add([v], VMEM_SPEC, [SDS((N_DEV,) + v.shape, v.dtype)], VMEM_SPEC, N_DEV, build)[0]


def _add_gather_ici(stage, shard, rows=None, whole=None):
    first, count = rows or (0, shard.shape[1])

    def build(ins, outs):
        x, y, c = _coords()
        k_me = 2 * x + y
        part = pl.ds(first, count)
        copies = [] if whole is not None else [(ins[0], outs[0].at[k_me], (x, y, 1 - c), outs[0].at[k_me])]
        for px, py in _chips():
            copies.append((ins[0].at[c, part], outs[0].at[k_me, c, part], (px, py, c), outs[0].at[2 * px + py, c, part]))
        return copies

    if whole is None:
        return stage.add([shard], HBM_SPEC, [SDS((N_CHIPS,) + shard.shape, shard.dtype)], HBM_SPEC, N_CHIPS, build)[0]

    def build_into(ins, outs):
        return build(ins[1:], outs)

    i0 = len(stage.inputs)
    out = stage.add([whole, shard], HBM_SPEC, [SDS(whole.shape, whole.dtype)], HBM_SPEC, N_CHIPS - 1, build_into)[0]
    stage.aliases[i0] = out
    return out


def _add_gather_d2d(stage, whole):
    def build(ins, outs):
        x, y, c = _coords()
        return [(outs[0].at[2 * px + py, c], outs[0].at[2 * px + py, c], (x, y, 1 - c), outs[0].at[2 * px + py, 1 - c]) for px, py in _chips()]

    return stage.add([whole], HBM_SPEC, [SDS(whole.shape, whole.dtype)], HBM_SPEC, N_CHIPS - 1, build, alias=True)[0]


def _add_reduce_d2d(stage, grads):
    def build(ins, outs):
        x, y, c = _coords()
        return [(ins[0].at[k, 1 - c], outs[0].at[k], (x, y, 1 - c), outs[0].at[k]) for k in range(N_CHIPS)]

    return stage.add([grads], HBM_SPEC, [SDS((N_CHIPS,) + grads.shape[2:], grads.dtype)], HBM_SPEC, N_CHIPS, build)[0]


def _add_reduce_ici(stage, partial):
    def build(ins, outs):
        x, y, c = _coords()
        return [(ins[0].at[2 * px + py], outs[0].at[j], (px, py, c), outs[0].at[j]) for j, (px, py) in enumerate(_chips())]

    return stage.add([partial], HBM_SPEC, [SDS((N_CHIPS - 1,) + partial.shape[1:], partial.dtype)], HBM_SPEC, N_CHIPS - 1, build)[0]


def _add_swap_halves(stage, quarter):
    def build(ins, outs):
        x, y, c = _coords()
        return [(outs[0].at[c], outs[0].at[c], (x, y, 1 - c), outs[0].at[1 - c])]

    return stage.add([quarter], HBM_SPEC, [SDS(quarter.shape, quarter.dtype)], HBM_SPEC, 1, build, alias=True)[0]


def _sum_pair(grads, from_sibling, place, name):
    _, _, rows, cols = grads.shape
    tr = min(rows, 256)

    def body(place_ref, g_ref, s_ref, o_ref):
        o_ref[0] = (g_ref[0, 0].astype(F32) + s_ref[0].astype(F32)).astype(BF16)

    spec = pl.BlockSpec((1, tr, cols), lambda k, i, place_ref: (k, i, 0))
    return pl.pallas_call(
        body, name=name, out_shape=SDS((N_CHIPS, rows, cols), BF16),
        grid_spec=pltpu.PrefetchScalarGridSpec(
            num_scalar_prefetch=1, grid=(N_CHIPS, rows // tr),
            in_specs=[pl.BlockSpec((1, 1, tr, cols), lambda k, i, place_ref: (k, place_ref[0], i, 0)), spec], out_specs=spec),
        compiler_params=_params("parallel", "parallel"),
    )(place, grads, from_sibling)


def _sum_quarter(grads, from_sibling, from_chips, place, name):
    _, _, rows, cols = grads.shape
    tr = min(rows, 256)

    def body(place_ref, g_ref, s_ref, c_ref, o_ref):
        acc = g_ref[0, 0].astype(F32) + s_ref[0].astype(F32)
        for j in range(N_CHIPS - 1):
            acc = acc + c_ref[j].astype(F32)
        o_ref[0] = acc

    return pl.pallas_call(
        body, name=name, out_shape=SDS((2, rows, cols), F32),
        grid_spec=pltpu.PrefetchScalarGridSpec(
            num_scalar_prefetch=1, grid=(rows // tr,),
            in_specs=[pl.BlockSpec((1, 1, tr, cols), lambda i, place_ref: (place_ref[1], place_ref[0], i, 0)),
                      pl.BlockSpec((1, tr, cols), lambda i, place_ref: (place_ref[1], i, 0)),
                      pl.BlockSpec((N_CHIPS - 1, tr, cols), lambda i, place_ref: (0, i, 0))],
            out_specs=pl.BlockSpec((1, tr, cols), lambda i, place_ref: (place_ref[0], i, 0))),
        compiler_params=_params("parallel"),
    )(place, grads, from_sibling, from_chips)


def _row_spec(ts, cols, col_block=0):
    return pl.BlockSpec((ts, cols), lambda *g: (g[0], col_block))


def _vec_spec(rows, cols):
    return pl.BlockSpec((rows, cols), lambda *g: (0, 0))


def _matmul(a, w, *, transposed, reduce_blocks, name, tm, outs, finish, extra=(), prepare=None, zero_first=(), parts=1, stage=None):
    m_rows = a.shape[0]
    nb, r, c = w.shape
    kb = c if transposed else r
    dims = (((1,), (1,)), ((), ())) if transposed else (((1,), (0,)), ((), ()))
    assert a.shape[1] == (nb * kb if reduce_blocks else kb) and m_rows % tm == 0 and tm % parts == 0
    n_extra, sub = len(extra), tm // parts

    def body(a_ref, w_ref, *rest):
        extra_refs, out_refs = rest[:n_extra], rest[n_extra:]
        if zero_first:
            @pl.when(pl.program_id(0) == 0)
            def _():
                for i in zero_first:
                    out_refs[i][...] = jnp.zeros_like(out_refs[i])
        def views(j):
            rows = pl.ds(j * sub, sub)
            return rows, [ref.at[rows] if ref.shape[0] == tm else ref for ref in extra_refs], [ref.at[rows] if ref.shape[0] == tm else ref for ref in out_refs]

        def product(j):
            rows, ex, ou = views(j)
            if reduce_blocks:
                acc = lax.dot_general(a_ref[rows, 0:kb], w_ref[0], dims, preferred_element_type=F32)
                for k in range(1, nb):
                    acc += lax.dot_general(a_ref[rows, k * kb:(k + 1) * kb], w_ref[k], dims, preferred_element_type=F32)
                return [(None, acc)]
            lhs = a_ref[rows, :] if prepare is None else prepare(a_ref.at[rows], ex, ou)
            return [(k, lax.dot_general(lhs, w_ref[k], dims, preferred_element_type=F32)) for k in range(nb)]

        pending = product(0)
        for j in range(parts):
            ahead = product(j + 1) if j + 1 < parts else None
            _, ex, ou = views(j)
            for k, acc in pending:
                finish(k, acc, ex, ou)
            pending = ahead

    return _hosted(
        stage, body, name=name, grid=(m_rows // tm,), out_shape=[s for s, _ in outs], out_specs=[s for _, s in outs],
        in_specs=[_row_spec(tm, a.shape[1]), pl.BlockSpec((nb, r, c), lambda *g: (0, 0, 0), pipeline_mode=pl.Buffered(1))] + [s for _, s in extra],
        compiler_params=_params("arbitrary"),
    )(a, w, *[e for e, _ in extra])


def _store_blocks(n):
    def finish(k, acc, extra_refs, out_refs):
        if k is None:
            out_refs[0][...] = acc.astype(out_refs[0].dtype)
        else:
            out_refs[0][:, k * n:(k + 1) * n] = acc.astype(out_refs[0].dtype)
    return finish


def _weight_grad(a, b, *, nb, name, tk, stage=None):
    s_rows, k1 = a.shape
    tn = b.shape[1] // nb
    assert k1 % tk == 0 and b.shape[1] % nb == 0

    def body(a_ref, b_ref, o_ref):
        o_ref[0] = lax.dot_general(a_ref[...], b_ref[...], (((0,), (0,)), ((), ())), preferred_element_type=F32).astype(o_ref.dtype)

    return _hosted(
        stage, body, name=name, grid=(nb, k1 // tk), out_shape=SDS((nb, k1, tn), BF16),
        in_specs=[pl.BlockSpec((s_rows, tk), lambda n, i: (0, i)), pl.BlockSpec((s_rows, tn), lambda n, i: (0, n))],
        out_specs=pl.BlockSpec((1, tk, tn), lambda n, i: (n, i, 0)), compiler_params=_params("parallel", "parallel"),
    )(a, b)


def _rms(x):
    r = lax.rsqrt(jnp.mean(x * x, axis=-1, keepdims=True) + EPS)
    return x * r, r


def _accumulate(acc_ref, row, value):
    acc_ref[row:row + 1, :] += jnp.sum(value, axis=0, keepdims=True)


def _gelu_parts(y):
    k0, k1 = 0.7978845608028654, 0.044715
    th = jnp.tanh(k0 * (y + k1 * (y * y * y)))
    gelu = 0.5 * y * (1.0 + th)
    dgelu = 0.5 * (1.0 + th) + 0.5 * y * (1.0 - th * th) * (k0 * (1.0 + 3.0 * k1 * (y * y)))
    return gelu, dgelu


def _one_minus_square(a, log_a):
    return -jnp.tanh(log_a) * (1.0 + a * a)


def _modulated_norm(x, mod_ref, g_ref, sc_row, sh_row):
    xh, _ = _rms(x)
    return ((xh * g_ref[...]) * (1.0 + mod_ref[sc_row:sc_row + 1, :]) + mod_ref[sh_row:sh_row + 1, :]).astype(BF16)


def _modulated_norm_backward(dh, x, dres, mod_ref, g_ref, acc_ref, sc_row):
    xh, r = _rms(x)
    _accumulate(acc_ref, 0, dh * (xh * g_ref[...]))
    _accumulate(acc_ref, 1, dh)
    dn = dh * (1.0 + mod_ref[sc_row:sc_row + 1, :])
    _accumulate(acc_ref, 2, dn * xh)
    dxh = dn * g_ref[...]
    return dres + r * (dxh - xh * jnp.mean(dxh * xh, axis=-1, keepdims=True))


def _seq_forward(proj, conv_w, vecs, w_a, w_x, w_p, *, ts, stage=None):
    s_rows = proj.shape[0]
    d = D_MODEL

    def body(x_ref, xh_ref, y_ref, u_ref, uh_ref, cw_ref, vec_ref, wa_ref, wx_ref, wp_ref,
             xr_o, hr_o, ga_o, p_o, pooled_o, carry, a_scr, b_scr):
        t = pl.program_id(0)

        @pl.when(t == 0)
        def _():
            carry[...] = jnp.zeros_like(carry)

        x = x_ref[...]
        halo = jnp.where(t == 0, 0.0, xh_ref[...])
        xx = jnp.concatenate([halo, x], axis=0)
        xr = vec_ref[0:1, :] + x * cw_ref[CONV_WIDTH - 1:CONV_WIDTH, :]
        for j in range(CONV_WIDTH - 1):
            xr = xr + pltpu.roll(xx, CONV_WIDTH - 1 - j, 0)[CONV_HALO:] * cw_ref[j:j + 1, :]
        xr_o[...] = xr
        xrb = xr.astype(BF16)
        zr = jnp.concatenate([jnp.dot(xrb[:, g * BLOCK:(g + 1) * BLOCK], wa_ref[g], preferred_element_type=F32)
                              for g in range(N_BLOCKS)], axis=1) + vec_ref[1:2, :]
        zi = jnp.concatenate([jnp.dot(xrb[:, g * BLOCK:(g + 1) * BLOCK], wx_ref[g], preferred_element_type=F32)
                              for g in range(N_BLOCKS)], axis=1) + vec_ref[2:3, :]
        r = jax.nn.sigmoid(zr)
        gate_i = jax.nn.sigmoid(zi)
        log_a = (-C_RG * r) * jax.nn.softplus(vec_ref[3:4, :])
        rows = lax.broadcasted_iota(jnp.int32, (ts, d), 0)
        a = jnp.exp(log_a)
        mult = jnp.where((rows == 0) & (t == 0), 1.0, jnp.sqrt(_one_minus_square(a, log_a)))
        a_scr[...] = a
        b_scr[...] = xr * gate_i * mult

        sub = lax.broadcasted_iota(jnp.int32, (8, d), 0)

        def chunk(i, h):
            a = a_scr[pl.ds(i * 8, 8), :]
            b = b_scr[pl.ds(i * 8, 8), :]
            for s in (1, 2, 4):
                keep = sub >= s
                b = jnp.where(keep, a * pltpu.roll(b, s, 0) + b, b)
                a = jnp.where(keep, a * pltpu.roll(a, s, 0), a)
            hh = a * h + b
            hr_o[pl.ds(i * 8, 8), :] = hh
            return hh[7:8, :]

        carry[...] = lax.fori_loop(0, ts // 8, chunk, carry[...])
        gelu, _ = _gelu_parts(y_ref[...])
        ga_o[...] = (gelu * hr_o[...]).astype(BF16)

        u = u_ref[...]
        uu = jnp.concatenate([jnp.where(t == 0, 0.0, uh_ref[...]), u], axis=0)
        pos = (rows[:, :BLOCK] + t * ts + 1).astype(F32)
        sums, have, parts = uu, 1, []
        for g, win in enumerate(POOL_WINDOWS):
            while have < win:
                sums = sums + pltpu.roll(sums, have, 0)
                have *= 2
            mean = sums[POOL_HALO:, g * BLOCK:(g + 1) * BLOCK] * (1.0 / jnp.minimum(pos, float(win)))
            parts.append(mean - u[:, g * BLOCK:(g + 1) * BLOCK])
        pb = jnp.concatenate(parts, axis=1).astype(BF16)
        p_o[...] = pb
        mixed = jnp.concatenate([jnp.dot(pb[:, g * BLOCK:(g + 1) * BLOCK], wp_ref[g], preferred_element_type=F32)
                                 for g in range(N_BLOCKS)], axis=1) + vec_ref[4:5, :]
        pooled_o[...] = (mixed * vec_ref[5:6, :]).astype(BF16)

    halo_x = pl.BlockSpec((CONV_HALO, d), lambda t: (jnp.maximum(t * (ts // CONV_HALO) - 1, 0), 0))
    halo_u = pl.BlockSpec((POOL_HALO, d), lambda t: (jnp.maximum(t * (ts // POOL_HALO) - 1, 0), 2))
    wspec = pl.BlockSpec((N_BLOCKS, BLOCK, BLOCK), lambda t: (0, 0, 0))
    return _hosted(
        stage, body, name="seq_forward", grid=(s_rows // ts,),
        out_shape=[SDS((s_rows, d), F32), SDS((s_rows, d), F32), SDS((s_rows, d), BF16), SDS((s_rows, d), BF16), SDS((s_rows, d), BF16)],
        in_specs=[_row_spec(ts, d, 0), halo_x, _row_spec(ts, d, 1), _row_spec(ts, d, 2), halo_u, _vec_spec(CONV_WIDTH, d), _vec_spec(8, d),
                  wspec, wspec, wspec],
        out_specs=[_row_spec(ts, d)] * 5,
        scratch_shapes=[pltpu.VMEM((1, d), F32), pltpu.VMEM((ts, d), F32), pltpu.VMEM((ts, d), F32)],
        compiler_params=_params("arbitrary"),
    )(proj, proj, proj, proj, proj, conv_w, vecs, w_a, w_x, w_p)


def _branches(ga, pooled, proj, w_a, w_b, *, ts, stage=None):
    s_rows, d = ga.shape

    def body(a_ref, p_ref, ga_ref, gb_ref, wa_ref, wb_ref, merged_o, sa_o, sb_o, ta_o, tb_o):
        bra = jnp.dot(a_ref[...], wa_ref[0], preferred_element_type=F32)
        brb = jnp.dot(p_ref[...], wb_ref[0], preferred_element_type=F32)
        sa, sb = jax.nn.sigmoid(ga_ref[...]), jax.nn.sigmoid(gb_ref[...])
        merged_o[...] = (sa * bra + sb * brb).astype(BF16)
        sa_o[...] = sa.astype(BF16)
        sb_o[...] = sb.astype(BF16)
        ta_o[...] = (bra * (sa * (1.0 - sa))).astype(BF16)
        tb_o[...] = (brb * (sb * (1.0 - sb))).astype(BF16)

    rows = _row_spec(ts, d)
    wspec = pl.BlockSpec((1, d, d), lambda *g: (0, 0, 0), pipeline_mode=pl.Buffered(1))
    return _hosted(
        stage, body, name="branches", grid=(s_rows // ts,), out_shape=[SDS((s_rows, d), BF16)] * 5,
        in_specs=[rows, rows, _row_spec(ts, d, 3), _row_spec(ts, d, 4), wspec, wspec], out_specs=[rows] * 5,
        compiler_params=_params("arbitrary"),
    )(ga, pooled, proj, proj, w_a, w_b)


def _seq_backward(proj, xr, hr, p, dga, dpooled, dgab, conv_w, vecs, w_a, w_x, w_p, *, ts, stage=None):
    s_rows = proj.shape[0]
    d = D_MODEL
    n_t = s_rows // ts

    def block_dots(lhs, w_ref, dims):
        return jnp.concatenate([lax.dot_general(lhs[:, g * BLOCK:(g + 1) * BLOCK], w_ref[g], dims, preferred_element_type=F32)
                                for g in range(N_BLOCKS)], axis=1)

    def add_block_grads(dw_ref, lhs, rhs):
        for g in range(N_BLOCKS):
            dw_ref[g] += lax.dot_general(lhs[:, g * BLOCK:(g + 1) * BLOCK], rhs[:, g * BLOCK:(g + 1) * BLOCK],
                                         (((0,), (0,)), ((), ())), preferred_element_type=F32)

    nn, nt = (((1,), (0,)), ((), ())), (((1,), (1,)), ((), ()))

    def body(x_ref, y_ref, xr_ref, hr_ref, hh_ref, p_ref, dga_ref, dpl_ref, dgab_ref, cw_ref, vec_ref, wa_ref, wx_ref, wp_ref,
             dproj_o, acc_o, dwa_o, dwx_o, dwp_o, g_carry, dxr_carry, q_carry, a_scr, b_scr, g_scr):
        step = pl.program_id(0)
        t = n_t - 1 - step

        @pl.when(step == 0)
        def _():
            for ref in (acc_o, dwa_o, dwx_o, dwp_o, g_carry, dxr_carry, q_carry):
                ref[...] = jnp.zeros_like(ref)

        rows = lax.broadcasted_iota(jnp.int32, (ts, d), 0)
        start = (rows == 0) & (t == 0)
        xr = xr_ref[...]
        xrb = xr.astype(BF16)
        r = jax.nn.sigmoid(block_dots(xrb, wa_ref, nn) + vec_ref[1:2, :])
        gate_i = jax.nn.sigmoid(block_dots(xrb, wx_ref, nn) + vec_ref[2:3, :])
        sp = jax.nn.softplus(vec_ref[3:4, :])
        log_a = (-C_RG * r) * sp
        a = jnp.exp(log_a)
        m_square = _one_minus_square(a, log_a)
        mult = jnp.where(start, 1.0, jnp.sqrt(m_square))
        gelu, dgelu = _gelu_parts(y_ref[...])
        dga_t = dga_ref[...]
        hr_t = hr_ref[...]
        dy = dga_t * hr_t * dgelu

        a_scr[...] = jnp.where(rows == ts - 1, 1.0, pltpu.roll(a, ts - 1, 0))
        b_scr[...] = dga_t * gelu
        sub = lax.broadcasted_iota(jnp.int32, (8, d), 0)

        def chunk(i, g_next):
            at = pl.multiple_of((ts // 8 - 1 - i) * 8, 8)
            aa = a_scr[pl.ds(at, 8), :]
            bb = b_scr[pl.ds(at, 8), :]
            for s in (1, 2, 4):
                keep = sub < 8 - s
                bb = jnp.where(keep, bb + aa * pltpu.roll(bb, 8 - s, 0), bb)
                aa = jnp.where(keep, aa * pltpu.roll(aa, 8 - s, 0), aa)
            gg = aa * g_next + bb
            g_scr[pl.ds(at, 8), :] = gg
            return gg[0:1, :]

        g_first = lax.fori_loop(0, ts // 8, chunk, g_carry[...])
        g_carry[...] = a[0:1, :] * g_first
        g = g_scr[...]

        h_before = jnp.where(t == 0, 0.0, hh_ref[CONV_HALO - 1:CONV_HALO, :])
        h_prev = jnp.where(rows == 0, h_before, pltpu.roll(hr_t, 1, 0))
        dxr = g * gate_i * mult
        d_i = g * xr * mult
        d_mult = g * xr * gate_i
        dlog_a = (g * h_prev) * a - jnp.where(start, 0.0, d_mult * (a * a) * lax.rsqrt(m_square))
        dzr = (dlog_a * (-C_RG * sp)) * (r * (1.0 - r))
        dzi = d_i * (gate_i * (1.0 - gate_i))
        _accumulate(acc_o, 5, dzr)
        _accumulate(acc_o, 6, dzi)
        _accumulate(acc_o, 7, dlog_a * (-C_RG * r) * jax.nn.sigmoid(vec_ref[3:4, :]))
        dzrb, dzib = dzr.astype(BF16), dzi.astype(BF16)
        add_block_grads(dwa_o, xrb, dzrb)
        add_block_grads(dwx_o, xrb, dzib)
        dxr = dxr + block_dots(dzrb, wa_ref, nt) + block_dots(dzib, wx_ref, nt)

        x = x_ref[...]
        ext = jnp.concatenate([dxr, dxr_carry[...]], axis=0)
        dx = dxr * cw_ref[CONV_WIDTH - 1:CONV_WIDTH, :]
        _accumulate(acc_o, CONV_WIDTH - 1, x * dxr)
        for j in range(CONV_WIDTH - 1):
            ahead = pltpu.roll(ext, ts + CONV_HALO - (CONV_WIDTH - 1 - j), 0)[:ts]
            dx = dx + ahead * cw_ref[j:j + 1, :]
            _accumulate(acc_o, j, x * ahead)
        _accumulate(acc_o, 4, dxr)
        dxr_carry[...] = dxr[0:CONV_HALO, :]

        pb = p_ref[...]
        mixed = block_dots(pb, wp_ref, nn) + vec_ref[4:5, :]
        dpl = dpl_ref[...]
        _accumulate(acc_o, 9, dpl * mixed)
        dmixed = dpl * vec_ref[5:6, :]
        _accumulate(acc_o, 8, dmixed)
        dmb = dmixed.astype(BF16)
        add_block_grads(dwp_o, pb, dmb)
        dp = block_dots(dmb, wp_ref, nt)
        pos = (rows[:, :BLOCK] + t * ts + 1).astype(F32)
        q = jnp.concatenate([dp[:, g * BLOCK:(g + 1) * BLOCK] * (1.0 / jnp.minimum(pos, float(win))) for g, win in enumerate(POOL_WINDOWS)], axis=1)
        sums, have, parts = jnp.concatenate([q, q_carry[...]], axis=0), 1, []
        for g, win in enumerate(POOL_WINDOWS):
            while have < win:
                sums = sums + pltpu.roll(sums, ts + POOL_HALO - have, 0)
                have *= 2
            parts.append(sums[:ts, g * BLOCK:(g + 1) * BLOCK])
        du = jnp.concatenate(parts, axis=1) - dp
        q_carry[...] = q[0:POOL_HALO, :]

        dproj_o[:, 0:d] = dx.astype(BF16)
        dproj_o[:, d:2 * d] = dy.astype(BF16)
        dproj_o[:, 2 * d:3 * d] = du.astype(BF16)
        dproj_o[:, 3 * d:5 * d] = dgab_ref[...]

    def rev(cols, col_block=0):
        return pl.BlockSpec((ts, cols), lambda i: (n_t - 1 - i, col_block))

    halo_h = pl.BlockSpec((CONV_HALO, d), lambda i: (jnp.maximum((n_t - 1 - i) * (ts // CONV_HALO) - 1, 0), 0))
    wspec = pl.BlockSpec((N_BLOCKS, BLOCK, BLOCK), lambda i: (0, 0, 0))
    return _hosted(
        stage, body, name="seq_backward", grid=(n_t,),
        out_shape=[SDS((s_rows, 5 * d), BF16), SDS((16, d), F32)] + [SDS((N_BLOCKS, BLOCK, BLOCK), F32)] * 3,
        in_specs=[rev(d, 0), rev(d, 1), rev(d), rev(d), halo_h, rev(d), rev(d), rev(d), rev(2 * d), _vec_spec(CONV_WIDTH, d), _vec_spec(8, d),
                  wspec, wspec, wspec],
        out_specs=[rev(5 * d), _vec_spec(16, d), wspec, wspec, wspec],
        scratch_shapes=[pltpu.VMEM((1, d), F32), pltpu.VMEM((CONV_HALO, d), F32), pltpu.VMEM((POOL_HALO, d), F32),
                        pltpu.VMEM((ts, d), F32), pltpu.VMEM((ts, d), F32), pltpu.VMEM((ts, d), F32)],
        compiler_params=_params("arbitrary"),
    )(proj, proj, xr, hr, hr, p, dga, dpooled, dgab, conv_w, vecs, w_a, w_x, w_p)


def _adamw_math(w, g, m, v):
    m = ADAM_B1 * m + (1.0 - ADAM_B1) * g
    v = ADAM_B2 * v + (1.0 - ADAM_B2) * jnp.square(g)
    m_hat = m / (1.0 - ADAM_B1 ** ADAM_STEP)
    v_hat = v / (1.0 - ADAM_B2 ** ADAM_STEP)
    delta = -ADAM_LR * (m_hat / (jnp.sqrt(v_hat) + ADAM_EPS) + ADAM_WD * w)
    return delta, m, v


def _adamw(w, g, m, v, name):
    rows, cols = w.shape
    tr = min(rows, 256)

    def body(w_ref, g_ref, m_ref, v_ref, d_ref, mo_ref, vo_ref):
        d_ref[...], mo_ref[...], vo_ref[...] = _adamw_math(w_ref[...], g_ref[...], m_ref[...], v_ref[...])

    spec = pl.BlockSpec((tr, cols), lambda i: (i, 0))
    return pl.pallas_call(
        body, name=name, grid=(rows // tr,), out_shape=[SDS((rows, cols), F32)] * 3, in_specs=[spec] * 4, out_specs=[spec] * 3,
        compiler_params=_params("parallel"),
    )(w, g, m, v)


def _prologue(c, conv_shard, w_ada, b_cols, w_in_halves, gate_halves):
    n_cols = w_ada.shape[1]
    n_g = len(gate_halves)
    half_rows = w_in_halves.shape[1] // 2
    n_sems = 2 * N_DEV + N_CHIPS * n_g + 3 + N_DEV + 2 + 3

    def body(c_ref, conv_ref, w_ref, b_ref, win_ref, *rest):
        gate_refs = rest[:n_g]
        act_o, mod_o, conv_o, win_o = rest[n_g:n_g + 4]
        gate_o = rest[n_g + 4:2 * n_g + 4]
        c_all, mod_cols, send_sems, recv_sems = rest[2 * n_g + 4:]
        x, y, c = _coords()
        me, k_me = 4 * x + 2 * y + c, 2 * x + y
        sibling, x_nbr, y_nbr = (x, y, 1 - c), (1 - x, y, c), (x, 1 - y, c)
        k_x, k_y, k_d = 2 * (1 - x) + y, 2 * x + (1 - y), 2 * (1 - x) + (1 - y)
        counter = iter(range(n_sems))

        def copy(src, dst, to, landing):
            s = next(counter)
            out = pltpu.make_async_remote_copy(src_ref=src, dst_ref=dst, send_sem=send_sems.at[s], recv_sem=recv_sems.at[s],
                                               device_id=to, device_id_type=MESH)
            arrival = pltpu.make_async_remote_copy(src_ref=landing, dst_ref=landing, send_sem=send_sems.at[s], recv_sem=recv_sems.at[s],
                                                   device_id=(x, y, c), device_id_type=MESH)
            return out, arrival

        def allgather(src, out):
            s = next(counter)
            own = pltpu.make_async_copy(src, out.at[me], send_sems.at[s])
            peers = [copy(src, out.at[me], (_flip(x, r & 4), _flip(y, r & 2), _flip(c, r & 1)), out.at[me ^ r]) for r in range(1, N_DEV)]
            return own, peers

        def start(group):
            own, peers = group
            own.start()
            for out, _ in peers:
                out.start()

        def finish(group):
            own, peers = group
            for out, arrival in peers:
                arrival.wait_recv()
                out.wait_send()
            own.wait()

        rows_a, rows_b = pl.ds(0, half_rows), pl.ds(half_rows, half_rows)
        c_group, conv_group = allgather(c_ref, c_all), allgather(conv_ref, conv_o)
        gates = []
        for g_ref, g_out in zip(gate_refs, gate_o):
            gates.append(copy(g_ref, g_out.at[k_me], sibling, g_out.at[k_me]))
            gates += [copy(g_ref.at[c], g_out.at[k_me, c], (px, py, c), g_out.at[2 * px + py, c]) for px, py in _chips()]
        own_in = copy(win_ref, win_o.at[k_me], sibling, win_o.at[k_me])
        to_x = copy(win_ref.at[c], win_o.at[k_me, c], x_nbr, win_o.at[k_x, c])
        to_y = copy(win_ref.at[c], win_o.at[k_me, c], y_nbr, win_o.at[k_y, c])
        mod_group = allgather(mod_cols, mod_o)
        relay_y = copy(win_o.at[k_x, c, rows_a], win_o.at[k_x, c, rows_a], y_nbr, win_o.at[k_d, c, rows_a])
        relay_x = copy(win_o.at[k_y, c, rows_b], win_o.at[k_y, c, rows_b], x_nbr, win_o.at[k_d, c, rows_b])
        to_sibling = [copy(win_o.at[k, c], win_o.at[k, c], sibling, win_o.at[k, 1 - c]) for k in (k_x, k_y, k_d)]

        start(c_group)
        start(conv_group)
        for out, _ in gates + [own_in, to_x, to_y]:
            out.start()
        finish(c_group)
        cv = jnp.concatenate([c_all[r] for r in range(N_DEV)], axis=0)
        act = cv * jax.nn.sigmoid(cv)
        act_o[...] = act
        mod_cols[...] = jnp.dot(act.astype(BF16), w_ref[...].astype(BF16), preferred_element_type=F32) + b_ref[...]
        start(mod_group)
        to_x[1].wait_recv()
        relay_y[0].start()
        to_y[1].wait_recv()
        relay_x[0].start()
        finish(mod_group)
        finish(conv_group)
        for out, arrival in gates + [own_in]:
            arrival.wait_recv()
            out.wait_send()
        for out, arrival in (relay_y, relay_x):
            arrival.wait_recv()
            out.wait_send()
        to_x[0].wait_send()
        to_y[0].wait_send()
        for out, _ in to_sibling:
            out.start()
        for out, arrival in to_sibling:
            arrival.wait_recv()
            out.wait_send()

    gate_shapes = [SDS((N_CHIPS,) + g.shape, g.dtype) for g in gate_halves]
    return pl.pallas_call(
        body, name="prologue",
        out_shape=[SDS((N_DEV, c.shape[1]), F32), SDS((N_DEV, N_DEV, n_cols), F32), SDS((N_DEV,) + conv_shard.shape, F32),
                   SDS((N_CHIPS,) + w_in_halves.shape, w_in_halves.dtype)] + gate_shapes,
        in_specs=[VMEM_SPEC] * 4 + [HBM_SPEC] * (1 + n_g), out_specs=[VMEM_SPEC] * 3 + [HBM_SPEC] * (1 + n_g),
        scratch_shapes=[pltpu.VMEM((N_DEV,) + c.shape, F32), pltpu.VMEM((N_DEV, n_cols), F32),
                        pltpu.SemaphoreType.DMA((n_sems,)), pltpu.SemaphoreType.DMA((n_sems,))],
        compiler_params=pltpu.CompilerParams(vmem_limit_bytes=VMEM_LIMIT_V7X),
    )(c, conv_shard, w_ada, b_cols, w_in_halves, *gate_halves)


def _ada_backward(act_t, dmod_cols, w, m, v):
    rows, cols = w.shape
    tn = 512

    def body(a_ref, dm_ref, w_ref, m_ref, v_ref, g_ref, d_ref, mo_ref, vo_ref):
        act = a_ref[...].astype(BF16).astype(F32)
        dm = dm_ref[...].astype(BF16).astype(F32)
        g = act[:, 0:1] * dm[0:1, :]
        for b in range(1, N_DEV):
            g = g + act[:, b:b + 1] * dm[b:b + 1, :]
        g_ref[...] = g
        d_ref[...], mo_ref[...], vo_ref[...] = _adamw_math(w_ref[...], g, m_ref[...], v_ref[...])

    spec = pl.BlockSpec((rows, tn), lambda j: (0, j))
    return pl.pallas_call(
        body, name="ada_backward", grid=(cols // tn,), out_shape=[SDS((rows, cols), F32)] * 4,
        in_specs=[pl.BlockSpec((rows, N_DEV), lambda j: (0, 0)), pl.BlockSpec((N_DEV, tn), lambda j: (0, j)), spec, spec, spec],
        out_specs=[spec] * 4, compiler_params=_params("parallel"),
    )(act_t, dmod_cols, w, m, v)


def _sum_rows(gathered, n_loss):
    n = gathered.shape[2]

    def body(g_ref, o_ref):
        acc = g_ref[0]
        for r in range(1, N_DEV):
            acc = acc + g_ref[r]
        o_ref[...] = acc
        o_ref[:, n - n_loss:n] = jnp.broadcast_to(jnp.sum(acc[:, n - n_loss:n], axis=1, keepdims=True), (1, n_loss))

    return pl.pallas_call(body, name="sum_rows", out_shape=SDS((1, n), F32), in_specs=[VMEM_SPEC], out_specs=VMEM_SPEC)(gathered)


def _adamw_vectors(total, offsets, separate, params):
    n_p = len(params)
    apart = [g for g in separate if g is not None]

    def body(*refs):
        total_ref, apart_refs = refs[0], list(refs[1:1 + len(apart)])
        ins, outs = refs[1 + len(apart):1 + len(apart) + 3 * n_p], refs[1 + len(apart) + 3 * n_p:]
        for i in range(n_p):
            w_ref, m_ref, v_ref = ins[3 * i:3 * i + 3]
            g = apart_refs.pop(0)[...] if offsets[i] is None else total_ref[:, offsets[i]:offsets[i] + w_ref.shape[1]]
            outs[4 * i][...] = g
            outs[4 * i + 1][...], outs[4 * i + 2][...], outs[4 * i + 3][...] = _adamw_math(w_ref[...], g, m_ref[...], v_ref[...])

    flat = [a for p in params for a in p]
    res = pl.pallas_call(
        body, name="adamw_vectors", out_shape=[SDS(p[0].shape, F32) for p in params for _ in range(4)],
        in_specs=[VMEM_SPEC] * (1 + len(apart) + len(flat)), out_specs=[VMEM_SPEC] * (4 * n_p),
    )(total, *apart, *flat)
    return [tuple(res[4 * i:4 * i + 4]) for i in range(n_p)]


def _blocks_to_pieces(w):
    nb, rows, n = w.shape
    q = rows // N_CHIPS
    return w.reshape(nb, N_CHIPS, q, n).transpose(1, 0, 2, 3).reshape(N_CHIPS, 2, nb // 2, q, n)


def _pieces_to_blocks(w):
    n_chips, _, half, q, n = w.shape
    return w.reshape(n_chips, 2 * half, q, n).transpose(1, 0, 2, 3).reshape(2 * half, n_chips * q, n)


def kernel(x, c, norm_mix_g, norm_mlp_g, w_ada, b_ada, w_in, conv_w, conv_b, w_rg_a, b_rg_a, w_rg_x, b_rg_x, a_param, w_branch_a, w_pool, b_pool, pool_scale, w_branch_b, w_out, w_up, w_down, final_g, loss_target, m_norm_mix_g, m_norm_mlp_g, m_w_ada, m_b_ada, m_w_in, m_conv_w, m_conv_b, m_w_rg_a, m_b_rg_a, m_w_rg_x, m_b_rg_x, m_a_param, m_w_branch_a, m_w_pool, m_b_pool, m_pool_scale, m_w_branch_b, m_w_out, m_w_up, m_w_down, m_final_g, v_norm_mix_g, v_norm_mlp_g, v_w_ada, v_b_ada, v_w_in, v_conv_w, v_conv_b, v_w_rg_a, v_b_rg_a, v_w_rg_x, v_b_rg_x, v_a_param, v_w_branch_a, v_w_pool, v_b_pool, v_pool_scale, v_w_branch_b, v_w_out, v_w_up, v_w_down, v_final_g):
    d = D_MODEL
    s_rows = x.shape[1]
    ts, tsq = min(512, s_rows), min(256, s_rows)
    xi, yi, ci = _coords()
    k_me, me = 2 * xi + yi, 4 * xi + 2 * yi + ci
    ada_cols = w_ada.shape[2]
    conv_cols = conv_w.shape[2]
    n_in, n_up = w_in.shape[2], w_up.shape[2]

    names = ("w_in", "w_up", "w_down", "w_a", "w_b", "w_out", "rg_a", "rg_x", "pool")
    mine = dict(zip(names, (w_in[0], w_up[0], w_down[0], w_branch_a[0], w_branch_b[0], w_out[0], w_rg_a[0], w_rg_x[0], w_pool[0])))
    moments_m = dict(zip(names, (m_w_in[0], m_w_up[0], m_w_down[0], m_w_branch_a[0], m_w_branch_b[0], m_w_out[0], m_w_rg_a[0], m_w_rg_x[0], m_w_pool[0])))
    moments_v = dict(zip(names, (v_w_in[0], v_w_up[0], v_w_down[0], v_w_branch_a[0], v_w_branch_b[0], v_w_out[0], v_w_rg_a[0], v_w_rg_x[0], v_w_pool[0])))
    block_weights, squares = ("rg_a", "rg_x", "pool"), ("w_a", "w_b", "w_out")
    place = jnp.stack([ci, k_me]).astype(jnp.int32)
    g_mix, g_mlp, g_fin = norm_mix_g, norm_mlp_g, final_g.reshape(1, d)
    x0, target = x[0], loss_target[0]
    wg = functools.partial(_weight_grad, tk=min(512, d))
    plain, back, back_sum = (dict(transposed=t, reduce_blocks=r) for t, r in ((False, False), (True, False), (True, True)))
    rows_d, vec8, vec1 = _row_spec(ts, d), _vec_spec(8, d), _vec_spec(1, d)
    tw = min(1024, s_rows)
    halves_of_tile = 2

    def out_rows(cols, dtype, t=ts):
        return SDS((s_rows, cols), dtype), _row_spec(t, cols)

    sums_out = (SDS((8, d), F32), vec8)

    def halves(n):
        return mine[n].astype(BF16).reshape(2, -1, mine[n].shape[-1])

    def blocks(w):
        return _pieces_to_blocks(w.reshape(N_CHIPS, 2, N_BLOCKS // 2, BLOCK // N_CHIPS, BLOCK))

    act_all, mod_all, conv_all, w_in_whole, *gate_wholes = _prologue(
        c, conv_w[0], w_ada[0], lax.dynamic_slice_in_dim(b_ada, k_me * ada_cols, ada_cols, axis=1), halves("w_in"), [halves(n) for n in block_weights])
    conv_full = conv_all[0::2].transpose(1, 0, 2).reshape(CONV_WIDTH, d)
    mod_mine = lax.dynamic_index_in_dim(mod_all, me, axis=1, keepdims=False)[0::2]
    mod = jnp.concatenate([mod_mine.reshape(N_MOD, d), jnp.zeros((8 - N_MOD, d), F32)], axis=0)
    w_in_all = w_in_whole.reshape(N_CHIPS, d, n_in)
    vecs = jnp.concatenate([conv_b, b_rg_a, b_rg_x, a_param, b_pool, pool_scale, jnp.zeros((2, d), F32)], axis=0)

    def norm_first(a_ref, extra_refs, out_refs):
        h = _modulated_norm(a_ref[...], extra_refs[0], extra_refs[1], 1, 0)
        out_refs[1][...] = h
        return h

    stage = _Stage()
    i_blk = [_add_gather_d2d(stage, g) for g in gate_wholes]
    i_sq = [_add_gather_ici(stage, halves(n)) for n in squares]
    down_rows = mine["w_down"].shape[0] // 8
    i_down = _add_gather_ici(stage, halves("w_down"), rows=(0, down_rows))
    (proj, h1), got = _matmul(x0, w_in_all, name="proj_in", tm=ts, extra=[(mod, vec8), (g_mix, vec1)], prepare=norm_first,
                              outs=[out_rows(N_CHIPS * n_in, F32), out_rows(d, BF16)], finish=_store_blocks(n_in), parts=halves_of_tile,
                              stage=stage, **plain)
    w_rg_a_all, w_rg_x_all, w_pool_all = (blocks(got[i]) for i in i_blk)
    stage = _Stage()
    i_sq = [_add_gather_d2d(stage, got[i]) for i in i_sq]
    i_up = _add_gather_ici(stage, halves("w_up"))
    i_down = _add_gather_ici(stage, halves("w_down"), rows=(down_rows, down_rows), whole=got[i_down])
    (xr, hr, ga, p, pooled), got = _seq_forward(proj, conv_full, vecs, w_rg_a_all, w_rg_x_all, w_pool_all, ts=tsq, stage=stage)
    w_a_all, w_b_all, w_out_all = (got[i].reshape(1, d, d) for i in i_sq)
    stage = _Stage()
    i_up = _add_gather_d2d(stage, got[i_up])
    i_down = _add_gather_ici(stage, halves("w_down"), rows=(2 * down_rows, 2 * down_rows), whole=got[i_down])
    (merged, s_a, s_b, t_a, t_b), got = _branches(ga, pooled, proj, w_a_all, w_b_all, ts=ts, stage=stage)
    w_up_all = got[i_up].reshape(N_CHIPS, d, n_up)
    stage = _Stage()
    i_down = _add_gather_d2d(stage, got[i_down])

    def residual_norm(k, acc, extra_refs, out_refs):
        x_ref, mod_ref, g_ref = extra_refs
        out_refs[0][...] = acc
        x2_t = x_ref[...] + mod_ref[2:3, :] * acc
        out_refs[1][...] = x2_t
        out_refs[2][...] = _modulated_norm(x2_t, mod_ref, g_ref, 4, 3)

    (mo, x2, h2), got = _matmul(merged, w_out_all, name="mix_out", tm=ts, extra=[(x0, rows_d), (mod, vec8), (g_mlp, vec1)],
                                outs=[out_rows(d, F32), out_rows(d, F32), out_rows(d, BF16)], finish=residual_norm, parts=halves_of_tile,
                                stage=stage, transposed=False, reduce_blocks=True)
    w_down_all = got[i_down].reshape(N_CHIPS, D_FF // N_CHIPS, d)

    def relu_squared(k, acc, extra_refs, out_refs):
        out_refs[0][:, k * n_up:(k + 1) * n_up] = jnp.square(jnp.maximum(acc, 0.0)).astype(BF16)

    ff, = _matmul(h2, w_up_all, name="mlp_up", tm=ts, outs=[out_rows(D_FF, BF16)], finish=relu_squared, parts=halves_of_tile, **plain)

    def loss_head(k, acc, extra_refs, out_refs):
        x2_ref, mod_ref, g_ref, t_ref = extra_refs
        dx_ref, df_ref, acc_ref = out_refs
        xh, r = _rms(x2_ref[...] + mod_ref[5:6, :] * acc)
        err = xh * g_ref[...] - t_ref[...]
        dy = err / d
        dxh = dy * g_ref[...]
        dx3_t = r * (dxh - xh * jnp.mean(dxh * xh, axis=-1, keepdims=True))
        dx_ref[...] = dx3_t
        df_ref[...] = (dx3_t * mod_ref[5:6, :]).astype(BF16)
        _accumulate(acc_ref, 0, dy * xh)
        _accumulate(acc_ref, 1, dx3_t * acc)
        _accumulate(acc_ref, 2, jnp.square(err) * (0.5 / d))

    dx3, dffo, sums_head = _matmul(ff, w_down_all, name="mlp_down", tm=ts, extra=[(x2, rows_d), (mod, vec8), (g_fin, vec1), (target, rows_d)],
                                   outs=[out_rows(d, F32), out_rows(d, BF16), sums_out], finish=loss_head, zero_first=(2,), parts=halves_of_tile,
                                   transposed=False, reduce_blocks=True)

    partial_of, from_sibling, chip_sum, from_chips, half_done, quarter = {}, {}, {}, {}, {}, {}

    def publish(n, g):
        g = _blocks_to_pieces(g).astype(BF16) if n in block_weights else g
        partial_of[n] = g.reshape(N_CHIPS, 2, -1, mine[n].shape[-1])

    def exchange(to_sibling=(), to_chips=(), swap=()):
        st = _Stage()
        slots = [(n, from_sibling, _add_reduce_d2d(st, partial_of[n])) for n in to_sibling]
        slots += [(n, from_chips, _add_reduce_ici(st, chip_sum[n])) for n in to_chips]
        slots += [(n, quarter, _add_swap_halves(st, half_done[n])) for n in swap]
        return st, slots

    def collect(slots, outs):
        for n, where, i in slots:
            where[n] = outs[i]

    def sum_pairs(*ns):
        for n in ns:
            chip_sum[n] = _sum_pair(partial_of[n], from_sibling[n], place, "sum_pair_" + n)

    def sum_quarters(*ns):
        for n in ns:
            half_done[n] = _sum_quarter(partial_of[n], from_sibling[n], from_chips[n], place, "sum_quarter_" + n)

    publish("w_down", wg(ff, dffo, nb=1, name="grad_w_down"))

    def relu_squared_backward(k, acc, extra_refs, out_refs):
        cols = slice(k * n_up, (k + 1) * n_up)
        out_refs[0][:, cols] = (acc * (2.0 * jnp.sqrt(extra_refs[0][:, cols].astype(F32)))).astype(BF16)

    stage, slots = exchange(to_sibling=["w_down"])
    (dup,), got = _matmul(dffo, w_down_all, name="d_mlp_down", tm=ts, extra=[(ff, _row_spec(ts, D_FF))], outs=[out_rows(D_FF, BF16)],
                          finish=relu_squared_backward, parts=halves_of_tile, stage=stage, **back)
    collect(slots, got)
    sum_pairs("w_down")

    def norm_mlp_backward(k, acc, extra_refs, out_refs):
        x2_ref, dres_ref, mod_ref, g_ref, mo_ref = extra_refs
        dx_ref, dmo_ref, acc_ref = out_refs
        dx =_modulated_norm_backward(acc, x2_ref[...], dres_ref[...], mod_ref, g_ref, acc_ref, 4)
        dx_ref[...] = dx
        dmo_ref[...] = (dx * mod_ref[2:3, :]).astype(BF16)
        _accumulate(acc_ref, 3, dx * mo_ref[...])

    stage, slots = exchange(to_chips=["w_down"])
    (dx2, dmo, sums_mlp), got = _matmul(dup, w_up_all, name="d_mlp_up", tm=ts, stage=stage, finish=norm_mlp_backward,
                                        extra=[(x2, rows_d), (dx3, rows_d), (mod, vec8), (g_mlp, vec1), (mo, rows_d)],
                                        outs=[out_rows(d, F32), out_rows(d, BF16), sums_out], zero_first=(2,), parts=halves_of_tile, **back_sum)
    collect(slots, got)
    sum_quarters("w_down")
    stage, slots = exchange(swap=["w_down"])
    g_up, got = wg(h2, dup, nb=N_CHIPS, name="grad_w_up", stage=stage)
    collect(slots, got)
    publish("w_up", g_up)
    publish("w_out", wg(merged, dmo, nb=1, name="grad_w_out"))

    def merge_backward(k, acc, extra_refs, out_refs):
        sa_ref, sb_ref, ta_ref, tb_ref = extra_refs
        out_refs[0][...] = (acc * sa_ref[...].astype(F32)).astype(BF16)
        out_refs[1][...] = (acc * sb_ref[...].astype(F32)).astype(BF16)
        out_refs[2][:, 0:d] = (acc * ta_ref[...].astype(F32)).astype(BF16)
        out_refs[2][:, d:2 * d] = (acc * tb_ref[...].astype(F32)).astype(BF16)

    stage, slots = exchange(to_sibling=["w_up", "w_out"])
    (dbr_a, dbr_b, dgab), got = _matmul(dmo, w_out_all, name="d_mix_out", tm=ts, stage=stage, finish=merge_backward,
                                        extra=[(s_a, rows_d), (s_b, rows_d), (t_a, rows_d), (t_b, rows_d)],
                                        outs=[out_rows(d, BF16), out_rows(d, BF16), out_rows(2 * d, BF16)], parts=halves_of_tile, **back_sum)
    collect(slots, got)
    publish("w_a", wg(ga, dbr_a, nb=1, name="grad_w_branch_a"))
    publish("w_b", wg(pooled, dbr_b, nb=1, name="grad_w_branch_b"))
    stage, slots = exchange(to_sibling=["w_a", "w_b"])
    (dga,), got = _matmul(dbr_a, w_a_all, name="d_branch_a", tm=tw, outs=[out_rows(d, F32, tw)], finish=_store_blocks(d), stage=stage, **back_sum)
    collect(slots, got)
    dpooled, = _matmul(dbr_b, w_b_all, name="d_branch_b", tm=tw, outs=[out_rows(d, F32, tw)], finish=_store_blocks(d), **back_sum)
    sum_pairs("w_up", *squares)
    stage, slots = exchange(to_chips=("w_up",) + squares)
    (dproj, sums_seq, d_rg_a, d_rg_x, d_pool), got = _seq_backward(proj, xr, hr, p, dga, dpooled, dgab, conv_full, vecs,
                                                                   w_rg_a_all, w_rg_x_all, w_pool_all, ts=tsq, stage=stage)
    collect(slots, got)
    sum_quarters("w_up", *squares)
    for n, g in zip(block_weights, (d_rg_a, d_rg_x, d_pool)):
        publish(n, g)
    stage, slots = exchange(to_sibling=block_weights, swap=("w_up",) + squares)
    g_in, got = wg(h1, dproj, nb=N_CHIPS, name="grad_w_in", stage=stage)
    collect(slots, got)
    publish("w_in", g_in)
    sum_pairs(*block_weights)
    stage, slots = exchange(to_sibling=["w_in"], to_chips=block_weights)
    collect(slots, _run_stage(stage, "exchange_w_in_gates"))
    sum_pairs("w_in")
    sum_quarters(*block_weights)

    def norm_mix_backward(k, acc, extra_refs, out_refs):
        x_ref, dres_ref, mod_ref, g_ref = extra_refs
        out_refs[0][...] = _modulated_norm_backward(acc, x_ref[...], dres_ref[...], mod_ref, g_ref, out_refs[1], 1)

    stage, slots = exchange(to_chips=["w_in"], swap=block_weights)
    (grad_x, sums_mix), got = _matmul(dproj, w_in_all, name="d_proj_in", tm=ts, stage=stage, finish=norm_mix_backward,
                                      extra=[(x0, rows_d), (dx2, rows_d), (mod, vec8), (g_mix, vec1)],
                                      outs=[out_rows(d, F32), sums_out], zero_first=(1,), parts=halves_of_tile, **back_sum)
    collect(slots, got)
    sum_quarters("w_in")

    dmod = jnp.concatenate([sums_mix[1:2], sums_mix[0:1], sums_mlp[3:4], sums_mlp[1:2], sums_mlp[0:1], sums_head[1:2]], axis=1)
    row = jnp.concatenate([sums_mix[2:3], sums_mlp[2:3], sums_seq[4:5], sums_seq[5:6], sums_seq[6:7], sums_seq[7:8], sums_seq[8:9],
                           sums_seq[9:10], sums_head[0:1], sums_seq[0:CONV_WIDTH].reshape(1, CONV_WIDTH * d), dmod, sums_head[2:3]], axis=1)
    n_vec, n_conv, n_mod = 9 * d, CONV_WIDTH * d, N_MOD * d
    stage, slots = exchange(swap=["w_in"])
    i_rows = _add_allgather8(stage, row)
    got = _run_stage(stage, "exchange_sums_w_in")
    collect(slots, got)
    rows_all = got[i_rows]
    total = _sum_rows(rows_all, d)
    loss = total[0, n_vec + n_conv + n_mod]
    dmod_all = rows_all[:, 0, n_vec + n_conv:n_vec + n_conv + n_mod]
    g_ada, d_ada, m_ada, v_ada = _ada_backward(act_all.T, lax.dynamic_slice_in_dim(dmod_all, k_me * ada_cols, ada_cols, axis=1),
                                               w_ada[0], m_w_ada[0], v_w_ada[0])
    g_conv = lax.dynamic_slice_in_dim(total[:, n_vec:n_vec + n_conv].reshape(CONV_WIDTH, d), k_me * conv_cols, conv_cols, axis=1)
    vec_names = ("norm_mix_g", "norm_mlp_g", "conv_b", "b_rg_a", "b_rg_x", "a_param", "b_pool", "pool_scale", "final_g", "conv_w", "b_ada")
    vec_params = [(norm_mix_g, m_norm_mix_g, v_norm_mix_g), (norm_mlp_g, m_norm_mlp_g, v_norm_mlp_g), (conv_b, m_conv_b, v_conv_b),
                  (b_rg_a, m_b_rg_a, v_b_rg_a), (b_rg_x, m_b_rg_x, v_b_rg_x), (a_param, m_a_param, v_a_param), (b_pool, m_b_pool, v_b_pool),
                  (pool_scale, m_pool_scale, v_pool_scale), tuple(a.reshape(1, d) for a in (final_g, m_final_g, v_final_g)),
                  (conv_w[0], m_conv_w[0], v_conv_w[0]), (b_ada, m_b_ada, v_b_ada)]
    offsets = [i * d for i in range(9)] + [None, n_vec + n_conv]
    vec_res = _adamw_vectors(total, offsets, [None] * 9 + [g_conv, None], vec_params)
    vec_out = {n: [r.reshape(shape) for r in res] for n, res, shape in zip(
        vec_names, vec_res, [(1, d)] * 8 + [(d,), (1, CONV_WIDTH, conv_cols), (1, N_MOD * d)])}

    big_out = {}
    for n in names:
        shape2 = (-1, mine[n].shape[-1])
        g2 = quarter[n].reshape(shape2)
        res = (g2,) + tuple(_adamw(mine[n].reshape(shape2), g2, moments_m[n].reshape(shape2), moments_v[n].reshape(shape2), "adamw_" + n))
        big_out[n] = [r.reshape((1,) + mine[n].shape) for r in res]

    key = {"w_in": "w_in", "w_rg_a": "rg_a", "w_rg_x": "rg_x", "w_branch_a": "w_a", "w_pool": "pool", "w_branch_b": "w_b", "w_out": "w_out",
           "w_up": "w_up", "w_down": "w_down"}
    order = ("norm_mix_g", "norm_mlp_g", "w_ada", "b_ada", "w_in", "conv_w", "conv_b", "w_rg_a", "b_rg_a", "w_rg_x", "b_rg_x", "a_param",
             "w_branch_a", "w_pool", "b_pool", "pool_scale", "w_branch_b", "w_out", "w_up", "w_down", "final_g")
    ada_out = [g_ada[None], d_ada[None], m_ada[None], v_ada[None]]
    outs = [loss, grad_x[None]]
    for kind in range(4):
        for nme in order:
            outs.append(ada_out[kind] if nme == "w_ada" else big_out[key[nme]][kind] if nme in key else vec_out[nme][kind])
    return tuple(outs)
```

```python
import functools

import jax
import jax.numpy as jnp
from jax import lax
from jax.experimental import pallas as pl
from jax.experimental.pallas import tpu as pltpu

F32, BF16 = jnp.float32, jnp.bfloat16
D_MODEL = 1024
D_FF = 4 * D_MODEL
N_BLOCKS = 4
BLOCK = D_MODEL // N_BLOCKS
CONV_WIDTH = 4
POOL_WINDOWS = (2, 4, 8, 16)
CONV_HALO = 8
POOL_HALO = 16
N_MOD = 6
EPS = 1e-6
C_RG = 8.0
ADAM_LR, ADAM_B1, ADAM_B2, ADAM_EPS, ADAM_WD, ADAM_STEP = 0.001, 0.9, 0.999, 1e-08, 0.01, 10
N_DEV = 8
N_CHIPS = 4
VMEM_LIMIT_V7X = 56 * 2**20
MESH = pl.DeviceIdType.MESH
SDS = jax.ShapeDtypeStruct
HBM_SPEC = pl.BlockSpec(memory_space=pltpu.HBM)
VMEM_SPEC = pl.BlockSpec(memory_space=pltpu.VMEM)


def _params(*semantics):
    return pltpu.CompilerParams(dimension_semantics=semantics, vmem_limit_bytes=VMEM_LIMIT_V7X)


def _coords():
    return lax.axis_index("x"), lax.axis_index("y"), lax.axis_index("c")


def _flip(v, bit):
    return 1 - v if bit else v


def _chips():
    x, y, _ = _coords()
    return [(1 - x, y), (x, 1 - y), (1 - x, 1 - y)]


class _Stage:
    def __init__(self):
        self.inputs, self.in_specs, self.outputs, self.out_specs, self.aliases = [], [], [], [], {}
        self.parts, self.n_copies = [], 0

    def add(self, inputs, in_spec, outputs, out_spec, n_copies, build, alias=False):
        i0, o0 = len(self.inputs), len(self.outputs)
        self.inputs += inputs
        self.in_specs += [in_spec] * len(inputs)
        self.outputs += outputs
        self.out_specs += [out_spec] * len(outputs)
        if alias:
            self.aliases.update({i0 + i: o0 + i for i in range(len(inputs))})
        self.parts.append((build, i0, len(inputs), o0, len(outputs)))
        self.n_copies += n_copies
        return list(range(o0, o0 + len(outputs)))

    def copies(self, in_refs, out_refs):
        out = []
        for build, i0, ni, o0, no in self.parts:
            out += build(in_refs[i0:i0 + ni], out_refs[o0:o0 + no])
        assert len(out) == self.n_copies
        return out

    def run(self, in_refs, out_refs, send_sems, recv_sems, start):
        x, y, c = _coords()
        for s, (src, dst, to, landing) in enumerate(self.copies(in_refs, out_refs)):
            if to is None:
                cp = pltpu.make_async_copy(src, dst, send_sems.at[s])
                cp.start() if start else cp.wait()
                continue
            cp = pltpu.make_async_remote_copy(src_ref=src, dst_ref=dst, send_sem=send_sems.at[s], recv_sem=recv_sems.at[s],
                                              device_id=to, device_id_type=MESH)
            if start:
                cp.start()
            else:
                pltpu.make_async_remote_copy(src_ref=landing, dst_ref=landing, send_sem=send_sems.at[s], recv_sem=recv_sems.at[s],
                                             device_id=(x, y, c), device_id_type=MESH).wait_recv()
                cp.wait_send()


def _hosted(stage, body, *, name, in_specs, out_specs, out_shape, grid=(), scratch_shapes=(), compiler_params=None):
    if stage is None:
        return pl.pallas_call(body, name=name, grid=grid, in_specs=in_specs, out_specs=out_specs, out_shape=out_shape,
                              scratch_shapes=list(scratch_shapes), compiler_params=compiler_params)
    single = not isinstance(out_shape, (list, tuple))
    h_out_shape = [out_shape] if single else list(out_shape)
    h_out_specs = [out_specs] if single else list(out_specs)
    n_in, n_out, n_scr = len(in_specs), len(h_out_shape), len(scratch_shapes)
    s_in, s_out = len(stage.inputs), len(stage.outputs)

    def wrapped(*refs):
        h_in, st_in = refs[:n_in], refs[n_in:n_in + s_in]
        h_o, st_o = refs[n_in + s_in:n_in + s_in + n_out], refs[n_in + s_in + n_out:n_in + s_in + n_out + s_out]
        h_scr = refs[n_in + s_in + n_out + s_out:n_in + s_in + n_out + s_out + n_scr]
        send_sems, recv_sems = refs[n_in + s_in + n_out + s_out + n_scr:]
        if not grid:
            stage.run(st_in, st_o, send_sems, recv_sems, True)
            if body is not None:
                body(*h_in, *h_o, *h_scr)
            stage.run(st_in, st_o, send_sems, recv_sems, False)
            return
        ids = [pl.program_id(a) for a in range(len(grid))]
        first = functools.reduce(jnp.logical_and, [i == 0 for i in ids])
        last = functools.reduce(jnp.logical_and, [i == g - 1 for i, g in zip(ids, grid)])
        pl.when(first)(lambda: stage.run(st_in, st_o, send_sems, recv_sems, True))
        body(*h_in, *h_o, *h_scr)
        pl.when(last)(lambda: stage.run(st_in, st_o, send_sems, recv_sems, False))

    call = pl.pallas_call(
        wrapped, name=name, grid=grid, in_specs=list(in_specs) + stage.in_specs, out_specs=h_out_specs + stage.out_specs,
        out_shape=h_out_shape + stage.outputs, input_output_aliases={n_in + i: n_out + o for i, o in stage.aliases.items()},
        scratch_shapes=list(scratch_shapes) + [pltpu.SemaphoreType.DMA((stage.n_copies,)), pltpu.SemaphoreType.DMA((stage.n_copies,))],
        compiler_params=pltpu.CompilerParams(dimension_semantics=("arbitrary",) * len(grid), vmem_limit_bytes=VMEM_LIMIT_V7X),
    )

    def run(*operands):
        outs = call(*operands, *stage.inputs)
        host = outs[:n_out]
        return (host[0] if single else host), outs[n_out:]

    return run


def _run_stage(stage, name):
    return _hosted(stage, None, name=name, in_specs=[], out_specs=[], out_shape=[])()[1]


def _add_allgather8(stage, v):
    def build(ins, outs):
        x, y, c = _coords()
        me = 4 * x + 2 * y + c
        copies = [(ins[0], outs[0].at[me], None, None)]
        for r in range(1, N_DEV):
            peer = (_flip(x, r & 4), _flip(y, r & 2), _flip(c, r & 1))
            copies.append((ins[0], outs[0].at[me], peer, outs[0].at[me ^ r]))
        return copies

    return stage.add([v], VMEM_SPEC, [SDS((N_DEV,) + v.shape, v.dtype)], VMEM_SPEC, N_DEV, build)[0]


def _add_gather_ici(stage, shard, rows=None, whole=None):
    first, count = rows or (0, shard.shape[1])

    def build(ins, outs):
        x, y, c = _coords()
        k_me = 2 * x + y
        part = pl.ds(first, count)
        copies = [] if whole is not None else [(ins[0], outs[0].at[k_me], (x, y, 1 - c), outs[0].at[k_me])]
        for px, py in _chips():
            copies.append((ins[0].at[c, part], outs[0].at[k_me, c, part], (px, py, c), outs[0].at[2 * px + py, c, part]))
        return copies

    if whole is None:
        return stage.add([shard], HBM_SPEC, [SDS((N_CHIPS,) + shard.shape, shard.dtype)], HBM_SPEC, N_CHIPS, build)[0]

    def build_into(ins, outs):
        return build(ins[1:], outs)

    i0 = len(stage.inputs)
    out = stage.add([whole, shard], HBM_SPEC, [SDS(whole.shape, whole.dtype)], HBM_SPEC, N_CHIPS - 1, build_into)[0]
    stage.aliases[i0] = out
    return out


def _add_gather_d2d(stage, whole):
    def build(ins, outs):
        x, y, c = _coords()
        return [(outs[0].at[2 * px + py, c], outs[0].at[2 * px + py, c], (x, y, 1 - c), outs[0].at[2 * px + py, 1 - c]) for px, py in _chips()]

    return stage.add([whole], HBM_SPEC, [SDS(whole.shape, whole.dtype)], HBM_SPEC, N_CHIPS - 1, build, alias=True)[0]


def _add_reduce_d2d(stage, grads):
    def build(ins, outs):
        x, y, c = _coords()
        return [(ins[0].at[k, 1 - c], outs[0].at[k], (x, y, 1 - c), outs[0].at[k]) for k in range(N_CHIPS)]

    return stage.add([grads], HBM_SPEC, [SDS((N_CHIPS,) + grads.shape[2:], grads.dtype)], HBM_SPEC, N_CHIPS, build)[0]


def _add_reduce_ici(stage, partial):
    def build(ins, outs):
        x, y, c = _coords()
        return [(ins[0].at[2 * px + py], outs[0].at[j], (px, py, c), outs[0].at[j]) for j, (px, py) in enumerate(_chips())]

    return stage.add([partial], HBM_SPEC, [SDS((N_CHIPS - 1,) + partial.shape[1:], partial.dtype)], HBM_SPEC, N_CHIPS - 1, build)[0]


def _add_swap_halves(stage, quarter):
    def build(ins, outs):
        x, y, c = _coords()
        return [(outs[0].at[c], outs[0].at[c], (x, y, 1 - c), outs[0].at[1 - c])]

    return stage.add([quarter], HBM_SPEC, [SDS(quarter.shape, quarter.dtype)], HBM_SPEC, 1, build, alias=True)[0]


def _sum_pair(grads, from_sibling, place, name):
    _, _, rows, cols = grads.shape
    tr = min(rows, 256)

    def body(place_ref, g_ref, s_ref, o_ref):
        o_ref[0] = (g_ref[0, 0].astype(F32) + s_ref[0].astype(F32)).astype(BF16)

    spec = pl.BlockSpec((1, tr, cols), lambda k, i, place_ref: (k, i, 0))
    return pl.pallas_call(
        body, name=name, out_shape=SDS((N_CHIPS, rows, cols), BF16),
        grid_spec=pltpu.PrefetchScalarGridSpec(
            num_scalar_prefetch=1, grid=(N_CHIPS, rows // tr),
            in_specs=[pl.BlockSpec((1, 1, tr, cols), lambda k, i, place_ref: (k, place_ref[0], i, 0)), spec], out_specs=spec),
        compiler_params=_params("parallel", "parallel"),
    )(place, grads, from_sibling)


def _sum_quarter(grads, from_sibling, from_chips, place, name):
    _, _, rows, cols = grads.shape
    tr = min(rows, 256)

    def body(place_ref, g_ref, s_ref, c_ref, o_ref):
        acc = g_ref[0, 0].astype(F32) + s_ref[0].astype(F32)
        for j in range(N_CHIPS - 1):
            acc = acc + c_ref[j].astype(F32)
        o_ref[0] = acc

    return pl.pallas_call(
        body, name=name, out_shape=SDS((2, rows, cols), F32),
        grid_spec=pltpu.PrefetchScalarGridSpec(
            num_scalar_prefetch=1, grid=(rows // tr,),
            in_specs=[pl.BlockSpec((1, 1, tr, cols), lambda i, place_ref: (place_ref[1], place_ref[0], i, 0)),
                      pl.BlockSpec((1, tr, cols), lambda i, place_ref: (place_ref[1], i, 0)),
                      pl.BlockSpec((N_CHIPS - 1, tr, cols), lambda i, place_ref: (0, i, 0))],
            out_specs=pl.BlockSpec((1, tr, cols), lambda i, place_ref: (place_ref[0], i, 0))),
        compiler_params=_params("parallel"),
    )(place, grads, from_sibling, from_chips)


def _row_spec(ts, cols, col_block=0):
    return pl.BlockSpec((ts, cols), lambda *g: (g[0], col_block))


def _vec_spec(rows, cols):
    return pl.BlockSpec((rows, cols), lambda *g: (0, 0))


def _matmul(a, w, *, transposed, reduce_blocks, name, tm, outs, finish, extra=(), prepare=None, zero_first=(), parts=1, stage=None):
    m_rows = a.shape[0]
    nb, r, c = w.shape
    kb = c if transposed else r
    dims = (((1,), (1,)), ((), ())) if transposed else (((1,), (0,)), ((), ()))
    assert a.shape[1] == (nb * kb if reduce_blocks else kb) and m_rows % tm == 0 and tm % parts == 0
    n_extra, sub = len(extra), tm // parts

    def body(a_ref, w_ref, *rest):
        extra_refs, out_refs = rest[:n_extra], rest[n_extra:]
        if zero_first:
            @pl.when(pl.program_id(0) == 0)
            def _():
                for i in zero_first:
                    out_refs[i][...] = jnp.zeros_like(out_refs[i])
        def views(j):
            rows = pl.ds(j * sub, sub)
            return rows, [ref.at[rows] if ref.shape[0] == tm else ref for ref in extra_refs], [ref.at[rows] if ref.shape[0] == tm else ref for ref in out_refs]

        def product(j):
            rows, ex, ou = views(j)
            if reduce_blocks:
                acc = lax.dot_general(a_ref[rows, 0:kb], w_ref[0], dims, preferred_element_type=F32)
                for k in range(1, nb):
                    acc += lax.dot_general(a_ref[rows, k * kb:(k + 1) * kb], w_ref[k], dims, preferred_element_type=F32)
                return [(None, acc)]
            lhs = a_ref[rows, :] if prepare is None else prepare(a_ref.at[rows], ex, ou)
            return [(k, lax.dot_general(lhs, w_ref[k], dims, preferred_element_type=F32)) for k in range(nb)]

        pending = product(0)
        for j in range(parts):
            ahead = product(j + 1) if j + 1 < parts else None
            _, ex, ou = views(j)
            for k, acc in pending:
                finish(k, acc, ex, ou)
            pending = ahead

    return _hosted(
        stage, body, name=name, grid=(m_rows // tm,), out_shape=[s for s, _ in outs], out_specs=[s for _, s in outs],
        in_specs=[_row_spec(tm, a.shape[1]), pl.BlockSpec((nb, r, c), lambda *g: (0, 0, 0), pipeline_mode=pl.Buffered(1))] + [s for _, s in extra],
        compiler_params=_params("arbitrary"),
    )(a, w, *[e for e, _ in extra])


def _store_blocks(n):
    def finish(k, acc, extra_refs, out_refs):
        if k is None:
            out_refs[0][...] = acc.astype(out_refs[0].dtype)
        else:
            out_refs[0][:, k * n:(k + 1) * n] = acc.astype(out_refs[0].dtype)
    return finish


def _weight_grad(a, b, *, nb, name, tk, stage=None):
    s_rows, k1 = a.shape
    tn = b.shape[1] // nb
    assert k1 % tk == 0 and b.shape[1] % nb == 0

    def body(a_ref, b_ref, o_ref):
        o_ref[0] = lax.dot_general(a_ref[...], b_ref[...], (((0,), (0,)), ((), ())), preferred_element_type=F32).astype(o_ref.dtype)

    return _hosted(
        stage, body, name=name, grid=(nb, k1 // tk), out_shape=SDS((nb, k1, tn), BF16),
        in_specs=[pl.BlockSpec((s_rows, tk), lambda n, i: (0, i)), pl.BlockSpec((s_rows, tn), lambda n, i: (0, n))],
        out_specs=pl.BlockSpec((1, tk, tn), lambda n, i: (n, i, 0)), compiler_params=_params("parallel", "parallel"),
    )(a, b)


def _rms(x):
    r = lax.rsqrt(jnp.mean(x * x, axis=-1, keepdims=True) + EPS)
    return x * r, r


def _accumulate(acc_ref, row, value):
    acc_ref[row:row + 1, :] += jnp.sum(value, axis=0, keepdims=True)


def _gelu_parts(y):
    k0, k1 = 0.7978845608028654, 0.044715
    th = jnp.tanh(k0 * (y + k1 * (y * y * y)))
    gelu = 0.5 * y * (1.0 + th)
    dgelu = 0.5 * (1.0 + th) + 0.5 * y * (1.0 - th * th) * (k0 * (1.0 + 3.0 * k1 * (y * y)))
    return gelu, dgelu


def _one_minus_square(a, log_a):
    return -jnp.tanh(log_a) * (1.0 + a * a)


def _modulated_norm(x, mod_ref, g_ref, sc_row, sh_row):
    xh, _ = _rms(x)
    return ((xh * g_ref[...]) * (1.0 + mod_ref[sc_row:sc_row + 1, :]) + mod_ref[sh_row:sh_row + 1, :]).astype(BF16)


def _modulated_norm_backward(dh, x, dres, mod_ref, g_ref, acc_ref, sc_row):
    xh, r = _rms(x)
    _accumulate(acc_ref, 0, dh * (xh * g_ref[...]))
    _accumulate(acc_ref, 1, dh)
    dn = dh * (1.0 + mod_ref[sc_row:sc_row + 1, :])
    _accumulate(acc_ref, 2, dn * xh)
    dxh = dn * g_ref[...]
    return dres + r * (dxh - xh * jnp.mean(dxh * xh, axis=-1, keepdims=True))


def _seq_forward(proj, conv_w, vecs, w_a, w_x, w_p, *, ts, stage=None):
    s_rows = proj.shape[0]
    d = D_MODEL

    def body(x_ref, xh_ref, y_ref, u_ref, uh_ref, cw_ref, vec_ref, wa_ref, wx_ref, wp_ref,
             xr_o, hr_o, ga_o, p_o, pooled_o, carry, a_scr, b_scr):
        t = pl.program_id(0)

        @pl.when(t == 0)
        def _():
            carry[...] = jnp.zeros_like(carry)

        x = x_ref[...]
        halo = jnp.where(t == 0, 0.0, xh_ref[...])
        xx = jnp.concatenate([halo, x], axis=0)
        xr = vec_ref[0:1, :] + x * cw_ref[CONV_WIDTH - 1:CONV_WIDTH, :]
        for j in range(CONV_WIDTH - 1):
            xr = xr + pltpu.roll(xx, CONV_WIDTH - 1 - j, 0)[CONV_HALO:] * cw_ref[j:j + 1, :]
        xr_o[...] = xr
        xrb = xr.astype(BF16)
        zr = jnp.concatenate([jnp.dot(xrb[:, g * BLOCK:(g + 1) * BLOCK], wa_ref[g], preferred_element_type=F32)
                              for g in range(N_BLOCKS)], axis=1) + vec_ref[1:2, :]
        zi = jnp.concatenate([jnp.dot(xrb[:, g * BLOCK:(g + 1) * BLOCK], wx_ref[g], preferred_element_type=F32)
                              for g in range(N_BLOCKS)], axis=1) + vec_ref[2:3, :]
        r = jax.nn.sigmoid(zr)
        gate_i = jax.nn.sigmoid(zi)
        log_a = (-C_RG * r) * jax.nn.softplus(vec_ref[3:4, :])
        rows = lax.broadcasted_iota(jnp.int32, (ts, d), 0)
        a = jnp.exp(log_a)
        mult = jnp.where((rows == 0) & (t == 0), 1.0, jnp.sqrt(_one_minus_square(a, log_a)))
        a_scr[...] = a
        b_scr[...] = xr * gate_i * mult

        sub = lax.broadcasted_iota(jnp.int32, (8, d), 0)

        def chunk(i, h):
            a = a_scr[pl.ds(i * 8, 8), :]
            b = b_scr[pl.ds(i * 8, 8), :]
            for s in (1, 2, 4):
                keep = sub >= s
                b = jnp.where(keep, a * pltpu.roll(b, s, 0) + b, b)
                a = jnp.where(keep, a * pltpu.roll(a, s, 0), a)
            hh = a * h + b
            hr_o[pl.ds(i * 8, 8), :] = hh
            return hh[7:8, :]

        carry[...] = lax.fori_loop(0, ts // 8, chunk, carry[...])
        gelu, _ = _gelu_parts(y_ref[...])
        ga_o[...] = (gelu * hr_o[...]).astype(BF16)

        u = u_ref[...]
        uu = jnp.concatenate([jnp.where(t == 0, 0.0, uh_ref[...]), u], axis=0)
        pos = (rows[:, :BLOCK] + t * ts + 1).astype(F32)
        sums, have, parts = uu, 1, []
        for g, win in enumerate(POOL_WINDOWS):
            while have < win:
                sums = sums + pltpu.roll(sums, have, 0)
                have *= 2
            mean = sums[POOL_HALO:, g * BLOCK:(g + 1) * BLOCK] * (1.0 / jnp.minimum(pos, float(win)))
            parts.append(mean - u[:, g * BLOCK:(g + 1) * BLOCK])
        pb = jnp.concatenate(parts, axis=1).astype(BF16)
        p_o[...] = pb
        mixed = jnp.concatenate([jnp.dot(pb[:, g * BLOCK:(g + 1) * BLOCK], wp_ref[g], preferred_element_type=F32)
                                 for g in range(N_BLOCKS)], axis=1) + vec_ref[4:5, :]
        pooled_o[...] = (mixed * vec_ref[5:6, :]).astype(BF16)

    halo_x = pl.BlockSpec((CONV_HALO, d), lambda t: (jnp.maximum(t * (ts // CONV_HALO) - 1, 0), 0))
    halo_u = pl.BlockSpec((POOL_HALO, d), lambda t: (jnp.maximum(t * (ts // POOL_HALO) - 1, 0), 2))
    wspec = pl.BlockSpec((N_BLOCKS, BLOCK, BLOCK), lambda t: (0, 0, 0))
    return _hosted(
        stage, body, name="seq_forward", grid=(s_rows // ts,),
        out_shape=[SDS((s_rows, d), F32), SDS((s_rows, d), F32), SDS((s_rows, d), BF16), SDS((s_rows, d), BF16), SDS((s_rows, d), BF16)],
        in_specs=[_row_spec(ts, d, 0), halo_x, _row_spec(ts, d, 1), _row_spec(ts, d, 2), halo_u, _vec_spec(CONV_WIDTH, d), _vec_spec(8, d),
                  wspec, wspec, wspec],
        out_specs=[_row_spec(ts, d)] * 5,
        scratch_shapes=[pltpu.VMEM((1, d), F32), pltpu.VMEM((ts, d), F32), pltpu.VMEM((ts, d), F32)],
        compiler_params=_params("arbitrary"),
    )(proj, proj, proj, proj, proj, conv_w, vecs, w_a, w_x, w_p)


def _branches(ga, pooled, proj, w_a, w_b, *, ts, stage=None):
    s_rows, d = ga.shape

    def body(a_ref, p_ref, ga_ref, gb_ref, wa_ref, wb_ref, merged_o, sa_o, sb_o, ta_o, tb_o):
        bra = jnp.dot(a_ref[...], wa_ref[0], preferred_element_type=F32)
        brb = jnp.dot(p_ref[...], wb_ref[0], preferred_element_type=F32)
        sa, sb = jax.nn.sigmoid(ga_ref[...]), jax.nn.sigmoid(gb_ref[...])
        merged_o[...] = (sa * bra + sb * brb).astype(BF16)
        sa_o[...] = sa.astype(BF16)
        sb_o[...] = sb.astype(BF16)
        ta_o[...] = (bra * (sa * (1.0 - sa))).astype(BF16)
        tb_o[...] = (brb * (sb * (1.0 - sb))).astype(BF16)

    rows = _row_spec(ts, d)
    wspec = pl.BlockSpec((1, d, d), lambda *g: (0, 0, 0), pipeline_mode=pl.Buffered(1))
    return _hosted(
        stage, body, name="branches", grid=(s_rows // ts,), out_shape=[SDS((s_rows, d), BF16)] * 5,
        in_specs=[rows, rows, _row_spec(ts, d, 3), _row_spec(ts, d, 4), wspec, wspec], out_specs=[rows] * 5,
        compiler_params=_params("arbitrary"),
    )(ga, pooled, proj, proj, w_a, w_b)


def _seq_backward(proj, xr, hr, p, dga, dpooled, dgab, conv_w, vecs, w_a, w_x, w_p, *, ts, stage=None):
    s_rows = proj.shape[0]
    d = D_MODEL
    n_t = s_rows // ts

    def block_dots(lhs, w_ref, dims):
        return jnp.concatenate([lax.dot_general(lhs[:, g * BLOCK:(g + 1) * BLOCK], w_ref[g], dims, preferred_element_type=F32)
                                for g in range(N_BLOCKS)], axis=1)

    def add_block_grads(dw_ref, lhs, rhs):
        for g in range(N_BLOCKS):
            dw_ref[g] += lax.dot_general(lhs[:, g * BLOCK:(g + 1) * BLOCK], rhs[:, g * BLOCK:(g + 1) * BLOCK],
                                         (((0,), (0,)), ((), ())), preferred_element_type=F32)

    nn, nt = (((1,), (0,)), ((), ())), (((1,), (1,)), ((), ()))

    def body(x_ref, y_ref, xr_ref, hr_ref, hh_ref, p_ref, dga_ref, dpl_ref, dgab_ref, cw_ref, vec_ref, wa_ref, wx_ref, wp_ref,
             dproj_o, acc_o, dwa_o, dwx_o, dwp_o, g_carry, dxr_carry, q_carry, a_scr, b_scr, g_scr):
        step = pl.program_id(0)
        t = n_t - 1 - step

        @pl.when(step == 0)
        def _():
            for ref in (acc_o, dwa_o, dwx_o, dwp_o, g_carry, dxr_carry, q_carry):
                ref[...] = jnp.zeros_like(ref)

        rows = lax.broadcasted_iota(jnp.int32, (ts, d), 0)
        start = (rows == 0) & (t == 0)
        xr = xr_ref[...]
        xrb = xr.astype(BF16)
        r = jax.nn.sigmoid(block_dots(xrb, wa_ref, nn) + vec_ref[1:2, :])
        gate_i = jax.nn.sigmoid(block_dots(xrb, wx_ref, nn) + vec_ref[2:3, :])
        sp = jax.nn.softplus(vec_ref[3:4, :])
        log_a = (-C_RG * r) * sp
        a = jnp.exp(log_a)
        m_square = _one_minus_square(a, log_a)
        mult = jnp.where(start, 1.0, jnp.sqrt(m_square))
        gelu, dgelu = _gelu_parts(y_ref[...])
        dga_t = dga_ref[...]
        hr_t = hr_ref[...]
        dy = dga_t * hr_t * dgelu

        a_scr[...] = jnp.where(rows == ts - 1, 1.0, pltpu.roll(a, ts - 1, 0))
        b_scr[...] = dga_t * gelu
        sub = lax.broadcasted_iota(jnp.int32, (8, d), 0)

        def chunk(i, g_next):
            at = pl.multiple_of((ts // 8 - 1 - i) * 8, 8)
            aa = a_scr[pl.ds(at, 8), :]
            bb = b_scr[pl.ds(at, 8), :]
            for s in (1, 2, 4):
                keep = sub < 8 - s
                bb = jnp.where(keep, bb + aa * pltpu.roll(bb, 8 - s, 0), bb)
                aa = jnp.where(keep, aa * pltpu.roll(aa, 8 - s, 0), aa)
            gg = aa * g_next + bb
            g_scr[pl.ds(at, 8), :] = gg
            return gg[0:1, :]

        g_first = lax.fori_loop(0, ts // 8, chunk, g_carry[...])
        g_carry[...] = a[0:1, :] * g_first
        g = g_scr[...]

        h_before = jnp.where(t == 0, 0.0, hh_ref[CONV_HALO - 1:CONV_HALO, :])
        h_prev = jnp.where(rows == 0, h_before, pltpu.roll(hr_t, 1, 0))
        dxr = g * gate_i * mult
        d_i = g * xr * mult
        d_mult = g * xr * gate_i
        dlog_a = (g * h_prev) * a - jnp.where(start, 0.0, d_mult * (a * a) * lax.rsqrt(m_square))
        dzr = (dlog_a * (-C_RG * sp)) * (r * (1.0 - r))
        dzi = d_i * (gate_i * (1.0 - gate_i))
        _accumulate(acc_o, 5, dzr)
        _accumulate(acc_o, 6, dzi)
        _accumulate(acc_o, 7, dlog_a * (-C_RG * r) * jax.nn.sigmoid(vec_ref[3:4, :]))
        dzrb, dzib = dzr.astype(BF16), dzi.astype(BF16)
        add_block_grads(dwa_o, xrb, dzrb)
        add_block_grads(dwx_o, xrb, dzib)
        dxr = dxr + block_dots(dzrb, wa_ref, nt) + block_dots(dzib, wx_ref, nt)

        x = x_ref[...]
        ext = jnp.concatenate([dxr, dxr_carry[...]], axis=0)
        dx = dxr * cw_ref[CONV_WIDTH - 1:CONV_WIDTH, :]
        _accumulate(acc_o, CONV_WIDTH - 1, x * dxr)
        for j in range(CONV_WIDTH - 1):
            ahead = pltpu.roll(ext, ts + CONV_HALO - (CONV_WIDTH - 1 - j), 0)[:ts]
            dx = dx + ahead * cw_ref[j:j + 1, :]
            _accumulate(acc_o, j, x * ahead)
        _accumulate(acc_o, 4, dxr)
        dxr_carry[...] = dxr[0:CONV_HALO, :]

        pb = p_ref[...]
        mixed = block_dots(pb, wp_ref, nn) + vec_ref[4:5, :]
        dpl = dpl_ref[...]
        _accumulate(acc_o, 9, dpl * mixed)
        dmixed = dpl * vec_ref[5:6, :]
        _accumulate(acc_o, 8, dmixed)
        dmb = dmixed.astype(BF16)
        add_block_grads(dwp_o, pb, dmb)
        dp = block_dots(dmb, wp_ref, nt)
        pos = (rows[:, :BLOCK] + t * ts + 1).astype(F32)
        q = jnp.concatenate([dp[:, g * BLOCK:(g + 1) * BLOCK] * (1.0 / jnp.minimum(pos, float(win))) for g, win in enumerate(POOL_WINDOWS)], axis=1)
        sums, have, parts = jnp.concatenate([q, q_carry[...]], axis=0), 1, []
        for g, win in enumerate(POOL_WINDOWS):
            while have < win:
                sums = sums + pltpu.roll(sums, ts + POOL_HALO - have, 0)
                have *= 2
            parts.append(sums[:ts, g * BLOCK:(g + 1) * BLOCK])
        du = jnp.concatenate(parts, axis=1) - dp
        q_carry[...] = q[0:POOL_HALO, :]

        dproj_o[:, 0:d] = dx.astype(BF16)
        dproj_o[:, d:2 * d] = dy.astype(BF16)
        dproj_o[:, 2 * d:3 * d] = du.astype(BF16)
        dproj_o[:, 3 * d:5 * d] = dgab_ref[...]

    def rev(cols, col_block=0):
        return pl.BlockSpec((ts, cols), lambda i: (n_t - 1 - i, col_block))

    halo_h = pl.BlockSpec((CONV_HALO, d), lambda i: (jnp.maximum((n_t - 1 - i) * (ts // CONV_HALO) - 1, 0), 0))
    wspec = pl.BlockSpec((N_BLOCKS, BLOCK, BLOCK), lambda i: (0, 0, 0))
    return _hosted(
        stage, body, name="seq_backward", grid=(n_t,),
        out_shape=[SDS((s_rows, 5 * d), BF16), SDS((16, d), F32)] + [SDS((N_BLOCKS, BLOCK, BLOCK), F32)] * 3,
        in_specs=[rev(d, 0), rev(d, 1), rev(d), rev(d), halo_h, rev(d), rev(d), rev(d), rev(2 * d), _vec_spec(CONV_WIDTH, d), _vec_spec(8, d),
                  wspec, wspec, wspec],
        out_specs=[rev(5 * d), _vec_spec(16, d), wspec, wspec, wspec],
        scratch_shapes=[pltpu.VMEM((1, d), F32), pltpu.VMEM((CONV_HALO, d), F32), pltpu.VMEM((POOL_HALO, d), F32),
                        pltpu.VMEM((ts, d), F32), pltpu.VMEM((ts, d), F32), pltpu.VMEM((ts, d), F32)],
        compiler_params=_params("arbitrary"),
    )(proj, proj, xr, hr, hr, p, dga, dpooled, dgab, conv_w, vecs, w_a, w_x, w_p)


def _adamw_math(w, g, m, v):
    m = ADAM_B1 * m + (1.0 - ADAM_B1) * g
    v = ADAM_B2 * v + (1.0 - ADAM_B2) * jnp.square(g)
    m_hat = m / (1.0 - ADAM_B1 ** ADAM_STEP)
    v_hat = v / (1.0 - ADAM_B2 ** ADAM_STEP)
    delta = -ADAM_LR * (m_hat / (jnp.sqrt(v_hat) + ADAM_EPS) + ADAM_WD * w)
    return delta, m, v


def _adamw(w, g, m, v, name):
    rows, cols = w.shape
    tr = min(rows, 256)

    def body(w_ref, g_ref, m_ref, v_ref, d_ref, mo_ref, vo_ref):
        d_ref[...], mo_ref[...], vo_ref[...] = _adamw_math(w_ref[...], g_ref[...], m_ref[...], v_ref[...])

    spec = pl.BlockSpec((tr, cols), lambda i: (i, 0))
    return pl.pallas_call(
        body, name=name, grid=(rows // tr,), out_shape=[SDS((rows, cols), F32)] * 3, in_specs=[spec] * 4, out_specs=[spec] * 3,
        compiler_params=_params("parallel"),
    )(w, g, m, v)


def _prologue(c, conv_shard, w_ada, b_cols, w_in_halves, gate_halves):
    n_cols = w_ada.shape[1]
    n_g = len(gate_halves)
    half_rows = w_in_halves.shape[1] // 2
    n_sems = 3 * N_DEV + N_CHIPS * n_g + 5 + 2 + 3

    def body(c_ref, conv_ref, w_ref, b_ref, win_ref, *rest):
        gate_refs = rest[:n_g]
        act_o, mod_o, conv_o, win_o = rest[n_g:n_g + 4]
        gate_o = rest[n_g + 4:2 * n_g + 4]
        c_all, mod_cols, send_sems, recv_sems = rest[2 * n_g + 4:]
        x, y, c = _coords()
        me, k_me = 4 * x + 2 * y + c, 2 * x + y
        sibling, x_nbr, y_nbr = (x, y, 1 - c), (1 - x, y, c), (x, 1 - y, c)
        k_x, k_y, k_d = 2 * (1 - x) + y, 2 * x + (1 - y), 2 * (1 - x) + (1 - y)
        counter = iter(range(n_sems))

        def copy(src, dst, to, landing):
            s = next(counter)
            out = pltpu.make_async_remote_copy(src_ref=src, dst_ref=dst, send_sem=send_sems.at[s], recv_sem=recv_sems.at[s],
                                               device_id=to, device_id_type=MESH)
            arrival = pltpu.make_async_remote_copy(src_ref=landing, dst_ref=landing, send_sem=send_sems.at[s], recv_sem=recv_sems.at[s],
                                                   device_id=(x, y, c), device_id_type=MESH)
            return out, arrival

        def allgather(src, out):
            s = next(counter)
            own = pltpu.make_async_copy(src, out.at[me], send_sems.at[s])
            peers = [copy(src, out.at[me], (_flip(x, r & 4), _flip(y, r & 2), _flip(c, r & 1)), out.at[me ^ r]) for r in range(1, N_DEV)]
            return own, peers

        def start(group):
            own, peers = group
            own.start()
            for out, _ in peers:
                out.start()

        def finish(group):
            own, peers = group
            for out, arrival in peers:
                arrival.wait_recv()
                out.wait_send()
            own.wait()

        rows_a, rows_b = pl.ds(0, half_rows), pl.ds(half_rows, half_rows)
        c_group, conv_group = allgather(c_ref, c_all), allgather(conv_ref, conv_o)
        gates = []
        for g_ref, g_out in zip(gate_refs, gate_o):
            gates.append(copy(g_ref, g_out.at[k_me], sibling, g_out.at[k_me]))
            gates += [copy(g_ref.at[c], g_out.at[k_me, c], (px, py, c), g_out.at[2 * px + py, c]) for px, py in _chips()]
        own_in = copy(win_ref, win_o.at[k_me], sibling, win_o.at[k_me])
        to_x = [copy(win_ref.at[c, rows], win_o.at[k_me, c, rows], x_nbr, win_o.at[k_x, c, rows]) for rows in (rows_a, rows_b)]
        to_y = [copy(win_ref.at[c, rows], win_o.at[k_me, c, rows], y_nbr, win_o.at[k_y, c, rows]) for rows in (rows_b, rows_a)]
        mod_group = allgather(mod_cols, mod_o)
        relay_y = copy(win_o.at[k_x, c, rows_a], win_o.at[k_x, c, rows_a], y_nbr, win_o.at[k_d, c, rows_a])
        relay_x = copy(win_o.at[k_y, c, rows_b], win_o.at[k_y, c, rows_b], x_nbr, win_o.at[k_d, c, rows_b])
        sib_x, sib_y, sib_d = (copy(win_o.at[k, c], win_o.at[k, c], sibling, win_o.at[k, 1 - c]) for k in (k_x, k_y, k_d))

        start(c_group)
        start(conv_group)
        for out, _ in gates + [own_in] + to_x + to_y:
            out.start()
        finish(c_group)
        cv = jnp.concatenate([c_all[r] for r in range(N_DEV)], axis=0)
        act = cv * jax.nn.sigmoid(cv)
        act_o[...] = act
        mod_cols[...] = jnp.dot(act.astype(BF16), w_ref[...].astype(BF16), preferred_element_type=F32) + b_ref[...]
        start(mod_group)
        to_x[0][1].wait_recv()
        relay_y[0].start()
        to_y[0][1].wait_recv()
        relay_x[0].start()
        finish(mod_group)
        finish(conv_group)
        to_x[1][1].wait_recv()
        sib_x[0].start()
        to_y[1][1].wait_recv()
        sib_y[0].start()
        for out, arrival in gates + [own_in]:
            arrival.wait_recv()
            out.wait_send()
        for out, arrival in (relay_y, relay_x):
            arrival.wait_recv()
            out.wait_send()
        sib_d[0].start()
        for out, _ in to_x + to_y:
            out.wait_send()
        for out, arrival in (sib_x, sib_y, sib_d):
            arrival.wait_recv()
            out.wait_send()

    gate_shapes = [SDS((N_CHIPS,) + g.shape, g.dtype) for g in gate_halves]
    return pl.pallas_call(
        body, name="prologue",
        out_shape=[SDS((N_DEV, c.shape[1]), F32), SDS((N_DEV, N_DEV, n_cols), F32), SDS((N_DEV,) + conv_shard.shape, F32),
                   SDS((N_CHIPS,) + w_in_halves.shape, w_in_halves.dtype)] + gate_shapes,
        in_specs=[VMEM_SPEC] * 4 + [HBM_SPEC] * (1 + n_g), out_specs=[VMEM_SPEC] * 3 + [HBM_SPEC] * (1 + n_g),
        scratch_shapes=[pltpu.VMEM((N_DEV,) + c.shape, F32), pltpu.VMEM((N_DEV, n_cols), F32),
                        pltpu.SemaphoreType.DMA((n_sems,)), pltpu.SemaphoreType.DMA((n_sems,))],
        compiler_params=pltpu.CompilerParams(vmem_limit_bytes=VMEM_LIMIT_V7X),
    )(c, conv_shard, w_ada, b_cols, w_in_halves, *gate_halves)


def _ada_backward(act_t, dmod_cols, w, m, v):
    rows, cols = w.shape
    tn = 512

    def body(a_ref, dm_ref, w_ref, m_ref, v_ref, g_ref, d_ref, mo_ref, vo_ref):
        g = jnp.dot(a_ref[...].astype(BF16), dm_ref[...].astype(BF16), preferred_element_type=F32)
        g_ref[...] = g
        d_ref[...], mo_ref[...], vo_ref[...] = _adamw_math(w_ref[...], g, m_ref[...], v_ref[...])

    spec = pl.BlockSpec((rows, tn), lambda j: (0, j))
    return pl.pallas_call(
        body, name="ada_backward", grid=(cols // tn,), out_shape=[SDS((rows, cols), F32)] * 4,
        in_specs=[pl.BlockSpec((rows, N_DEV), lambda j: (0, 0)), pl.BlockSpec((N_DEV, tn), lambda j: (0, j)), spec, spec, spec],
        out_specs=[spec] * 4, compiler_params=_params("parallel"),
    )(act_t, dmod_cols, w, m, v)


def _sum_rows(gathered, n_loss):
    n = gathered.shape[2]

    def body(g_ref, o_ref):
        acc = g_ref[0]
        for r in range(1, N_DEV):
            acc = acc + g_ref[r]
        o_ref[...] = acc
        o_ref[:, n - n_loss:n] = jnp.broadcast_to(jnp.sum(acc[:, n - n_loss:n], axis=1, keepdims=True), (1, n_loss))

    return pl.pallas_call(body, name="sum_rows", out_shape=SDS((1, n), F32), in_specs=[VMEM_SPEC], out_specs=VMEM_SPEC)(gathered)


def _adamw_vectors(total, offsets, separate, params):
    n_p = len(params)
    apart = [g for g in separate if g is not None]

    def body(*refs):
        total_ref, apart_refs = refs[0], list(refs[1:1 + len(apart)])
        ins, outs = refs[1 + len(apart):1 + len(apart) + 3 * n_p], refs[1 + len(apart) + 3 * n_p:]
        for i in range(n_p):
            w_ref, m_ref, v_ref = ins[3 * i:3 * i + 3]
            g = apart_refs.pop(0)[...] if offsets[i] is None else total_ref[:, offsets[i]:offsets[i] + w_ref.shape[1]]
            outs[4 * i][...] = g
            outs[4 * i + 1][...], outs[4 * i + 2][...], outs[4 * i + 3][...] = _adamw_math(w_ref[...], g, m_ref[...], v_ref[...])

    flat = [a for p in params for a in p]
    res = pl.pallas_call(
        body, name="adamw_vectors", out_shape=[SDS(p[0].shape, F32) for p in params for _ in range(4)],
        in_specs=[VMEM_SPEC] * (1 + len(apart) + len(flat)), out_specs=[VMEM_SPEC] * (4 * n_p),
    )(total, *apart, *flat)
    return [tuple(res[4 * i:4 * i + 4]) for i in range(n_p)]


def _blocks_to_pieces(w):
    nb, rows, n = w.shape
    q = rows // N_CHIPS
    return w.reshape(nb, N_CHIPS, q, n).transpose(1, 0, 2, 3).reshape(N_CHIPS, 2, nb // 2, q, n)


def _pieces_to_blocks(w):
    n_chips, _, half, q, n = w.shape
    return w.reshape(n_chips, 2 * half, q, n).transpose(1, 0, 2, 3).reshape(2 * half, n_chips * q, n)


def kernel(x, c, norm_mix_g, norm_mlp_g, w_ada, b_ada, w_in, conv_w, conv_b, w_rg_a, b_rg_a, w_rg_x, b_rg_x, a_param, w_branch_a, w_pool, b_pool, pool_scale, w_branch_b, w_out, w_up, w_down, final_g, loss_target, m_norm_mix_g, m_norm_mlp_g, m_w_ada, m_b_ada, m_w_in, m_conv_w, m_conv_b, m_w_rg_a, m_b_rg_a, m_w_rg_x, m_b_rg_x, m_a_param, m_w_branch_a, m_w_pool, m_b_pool, m_pool_scale, m_w_branch_b, m_w_out, m_w_up, m_w_down, m_final_g, v_norm_mix_g, v_norm_mlp_g, v_w_ada, v_b_ada, v_w_in, v_conv_w, v_conv_b, v_w_rg_a, v_b_rg_a, v_w_rg_x, v_b_rg_x, v_a_param, v_w_branch_a, v_w_pool, v_b_pool, v_pool_scale, v_w_branch_b, v_w_out, v_w_up, v_w_down, v_final_g):
    d = D_MODEL
    s_rows = x.shape[1]
    ts, tsq = min(512, s_rows), min(256, s_rows)
    xi, yi, ci = _coords()
    k_me, me = 2 * xi + yi, 4 * xi + 2 * yi + ci
    ada_cols = w_ada.shape[2]
    conv_cols = conv_w.shape[2]
    n_in, n_up = w_in.shape[2], w_up.shape[2]

    names = ("w_in", "w_up", "w_down", "w_a", "w_b", "w_out", "rg_a", "rg_x", "pool")
    mine = dict(zip(names, (w_in[0], w_up[0], w_down[0], w_branch_a[0], w_branch_b[0], w_out[0], w_rg_a[0], w_rg_x[0], w_pool[0])))
    moments_m = dict(zip(names, (m_w_in[0], m_w_up[0], m_w_down[0], m_w_branch_a[0], m_w_branch_b[0], m_w_out[0], m_w_rg_a[0], m_w_rg_x[0], m_w_pool[0])))
    moments_v = dict(zip(names, (v_w_in[0], v_w_up[0], v_w_down[0], v_w_branch_a[0], v_w_branch_b[0], v_w_out[0], v_w_rg_a[0], v_w_rg_x[0], v_w_pool[0])))
    block_weights, squares = ("rg_a", "rg_x", "pool"), ("w_a", "w_b", "w_out")
    place = jnp.stack([ci, k_me]).astype(jnp.int32)
    g_mix, g_mlp, g_fin = norm_mix_g, norm_mlp_g, final_g.reshape(1, d)
    x0, target = x[0], loss_target[0]
    wg = functools.partial(_weight_grad, tk=min(512, d))
    plain, back, back_sum = (dict(transposed=t, reduce_blocks=r) for t, r in ((False, False), (True, False), (True, True)))
    rows_d, vec8, vec1 = _row_spec(ts, d), _vec_spec(8, d), _vec_spec(1, d)
    tw = min(1024, s_rows)
    halves_of_tile = 2

    def out_rows(cols, dtype, t=ts):
        return SDS((s_rows, cols), dtype), _row_spec(t, cols)

    sums_out = (SDS((8, d), F32), vec8)

    def halves(n):
        return mine[n].astype(BF16).reshape(2, -1, mine[n].shape[-1])

    def blocks(w):
        return _pieces_to_blocks(w.reshape(N_CHIPS, 2, N_BLOCKS // 2, BLOCK // N_CHIPS, BLOCK))

    act_all, mod_all, conv_all, w_in_whole, *gate_wholes = _prologue(
        c, conv_w[0], w_ada[0], lax.dynamic_slice_in_dim(b_ada, k_me * ada_cols, ada_cols, axis=1), halves("w_in"), [halves(n) for n in block_weights])
    conv_full = conv_all[0::2].transpose(1, 0, 2).reshape(CONV_WIDTH, d)
    mod_mine = lax.dynamic_index_in_dim(mod_all, me, axis=1, keepdims=False)[0::2]
    mod = jnp.concatenate([mod_mine.reshape(N_MOD, d), jnp.zeros((8 - N_MOD, d), F32)], axis=0)
    w_in_all = w_in_whole.reshape(N_CHIPS, d, n_in)
    vecs = jnp.concatenate([conv_b, b_rg_a, b_rg_x, a_param, b_pool, pool_scale, jnp.zeros((2, d), F32)], axis=0)

    def norm_first(a_ref, extra_refs, out_refs):
        h = _modulated_norm(a_ref[...], extra_refs[0], extra_refs[1], 1, 0)
        out_refs[1][...] = h
        return h

    stage = _Stage()
    i_blk = [_add_gather_d2d(stage, g) for g in gate_wholes]
    i_sq = [_add_gather_ici(stage, halves(n)) for n in squares]
    down_rows = mine["w_down"].shape[0] // 8
    i_down = _add_gather_ici(stage, halves("w_down"), rows=(0, down_rows))
    (proj, h1), got = _matmul(x0, w_in_all, name="proj_in", tm=ts, extra=[(mod, vec8), (g_mix, vec1)], prepare=norm_first,
                              outs=[out_rows(N_CHIPS * n_in, F32), out_rows(d, BF16)], finish=_store_blocks(n_in), parts=halves_of_tile,
                              stage=stage, **plain)
    w_rg_a_all, w_rg_x_all, w_pool_all = (blocks(got[i]) for i in i_blk)
    stage = _Stage()
    i_sq = [_add_gather_d2d(stage, got[i]) for i in i_sq]
    i_up = _add_gather_ici(stage, halves("w_up"))
    i_down = _add_gather_ici(stage, halves("w_down"), rows=(down_rows, down_rows), whole=got[i_down])
    (xr, hr, ga, p, pooled), got = _seq_forward(proj, conv_full, vecs, w_rg_a_all, w_rg_x_all, w_pool_all, ts=tsq, stage=stage)
    w_a_all, w_b_all, w_out_all = (got[i].reshape(1, d, d) for i in i_sq)
    stage = _Stage()
    i_up = _add_gather_d2d(stage, got[i_up])
    i_down = _add_gather_ici(stage, halves("w_down"), rows=(2 * down_rows, 2 * down_rows), whole=got[i_down])
    (merged, s_a, s_b, t_a, t_b), got = _branches(ga, pooled, proj, w_a_all, w_b_all, ts=ts, stage=stage)
    w_up_all = got[i_up].reshape(N_CHIPS, d, n_up)
    stage = _Stage()
    i_down = _add_gather_d2d(stage, got[i_down])

    def residual_norm(k, acc, extra_refs, out_refs):
        x_ref, mod_ref, g_ref = extra_refs
        out_refs[0][...] = acc
        x2_t = x_ref[...] + mod_ref[2:3, :] * acc
        out_refs[1][...] = x2_t
        out_refs[2][...] = _modulated_norm(x2_t, mod_ref, g_ref, 4, 3)

    (mo, x2, h2), got = _matmul(merged, w_out_all, name="mix_out", tm=ts, extra=[(x0, rows_d), (mod, vec8), (g_mlp, vec1)],
                                outs=[out_rows(d, F32), out_rows(d, F32), out_rows(d, BF16)], finish=residual_norm, parts=halves_of_tile,
                                stage=stage, transposed=False, reduce_blocks=True)
    w_down_all = got[i_down].reshape(N_CHIPS, D_FF // N_CHIPS, d)

    def relu_squared(k, acc, extra_refs, out_refs):
        out_refs[0][:, k * n_up:(k + 1) * n_up] = jnp.square(jnp.maximum(acc, 0.0)).astype(BF16)

    ff, = _matmul(h2, w_up_all, name="mlp_up", tm=ts, outs=[out_rows(D_FF, BF16)], finish=relu_squared, parts=halves_of_tile, **plain)

    def loss_head(k, acc, extra_refs, out_refs):
        x2_ref, mod_ref, g_ref, t_ref = extra_refs
        dx_ref, df_ref, acc_ref = out_refs
        xh, r = _rms(x2_ref[...] + mod_ref[5:6, :] * acc)
        err = xh * g_ref[...] - t_ref[...]
        dy = err / d
        dxh = dy * g_ref[...]
        dx3_t = r * (dxh - xh * jnp.mean(dxh * xh, axis=-1, keepdims=True))
        dx_ref[...] = dx3_t
        df_ref[...] = (dx3_t * mod_ref[5:6, :]).astype(BF16)
        _accumulate(acc_ref, 0, dy * xh)
        _accumulate(acc_ref, 1, dx3_t * acc)
        _accumulate(acc_ref, 2, jnp.square(err) * (0.5 / d))

    dx3, dffo, sums_head = _matmul(ff, w_down_all, name="mlp_down", tm=ts, extra=[(x2, rows_d), (mod, vec8), (g_fin, vec1), (target, rows_d)],
                                   outs=[out_rows(d, F32), out_rows(d, BF16), sums_out], finish=loss_head, zero_first=(2,), parts=halves_of_tile,
                                   transposed=False, reduce_blocks=True)

    partial_of, from_sibling, chip_sum, from_chips, half_done, quarter = {}, {}, {}, {}, {}, {}

    def publish(n, g):
        g = _blocks_to_pieces(g).astype(BF16) if n in block_weights else g
        partial_of[n] = g.reshape(N_CHIPS, 2, -1, mine[n].shape[-1])

    def exchange(to_sibling=(), to_chips=(), swap=()):
        st = _Stage()
        slots = [(n, from_sibling, _add_reduce_d2d(st, partial_of[n])) for n in to_sibling]
        slots += [(n, from_chips, _add_reduce_ici(st, chip_sum[n])) for n in to_chips]
        slots += [(n, quarter, _add_swap_halves(st, half_done[n])) for n in swap]
        return st, slots

    def collect(slots, outs):
        for n, where, i in slots:
            where[n] = outs[i]

    def sum_pairs(*ns):
        for n in ns:
            chip_sum[n] = _sum_pair(partial_of[n], from_sibling[n], place, "sum_pair_" + n)

    def sum_quarters(*ns):
        for n in ns:
            half_done[n] = _sum_quarter(partial_of[n], from_sibling[n], from_chips[n], place, "sum_quarter_" + n)

    publish("w_down", wg(ff, dffo, nb=1, name="grad_w_down"))

    def relu_squared_backward(k, acc, extra_refs, out_refs):
        cols = slice(k * n_up, (k + 1) * n_up)
        out_refs[0][:, cols] = (acc * (2.0 * jnp.sqrt(extra_refs[0][:, cols].astype(F32)))).astype(BF16)

    stage, slots = exchange(to_sibling=["w_down"])
    (dup,), got = _matmul(dffo, w_down_all, name="d_mlp_down", tm=ts, extra=[(ff, _row_spec(ts, D_FF))], outs=[out_rows(D_FF, BF16)],
                          finish=relu_squared_backward, parts=halves_of_tile, stage=stage, **back)
    collect(slots, got)
    sum_pairs("w_down")

    def norm_mlp_backward(k, acc, extra_refs, out_refs):
        x2_ref, dres_ref, mod_ref, g_ref, mo_ref = extra_refs
        dx_ref, dmo_ref, acc_ref = out_refs
        dx =_modulated_norm_backward(acc, x2_ref[...], dres_ref[...], mod_ref, g_ref, acc_ref, 4)
        dx_ref[...] = dx
        dmo_ref[...] = (dx * mod_ref[2:3, :]).astype(BF16)
        _accumulate(acc_ref, 3, dx * mo_ref[...])

    stage, slots = exchange(to_chips=["w_down"])
    (dx2, dmo, sums_mlp), got = _matmul(dup, w_up_all, name="d_mlp_up", tm=ts, stage=stage, finish=norm_mlp_backward,
                                        extra=[(x2, rows_d), (dx3, rows_d), (mod, vec8), (g_mlp, vec1), (mo, rows_d)],
                                        outs=[out_rows(d, F32), out_rows(d, BF16), sums_out], zero_first=(2,), parts=halves_of_tile, **back_sum)
    collect(slots, got)
    sum_quarters("w_down")
    stage, slots = exchange(swap=["w_down"])
    g_up, got = wg(h2, dup, nb=N_CHIPS, name="grad_w_up", stage=stage)
    collect(slots, got)
    publish("w_up", g_up)
    publish("w_out", wg(merged, dmo, nb=1, name="grad_w_out"))

    def merge_backward(k, acc, extra_refs, out_refs):
        sa_ref, sb_ref, ta_ref, tb_ref = extra_refs
        out_refs[0][...] = (acc * sa_ref[...].astype(F32)).astype(BF16)
        out_refs[1][...] = (acc * sb_ref[...].astype(F32)).astype(BF16)
        out_refs[2][:, 0:d] = (acc * ta_ref[...].astype(F32)).astype(BF16)
        out_refs[2][:, d:2 * d] = (acc * tb_ref[...].astype(F32)).astype(BF16)

    stage, slots = exchange(to_sibling=["w_up", "w_out"])
    (dbr_a, dbr_b, dgab), got = _matmul(dmo, w_out_all, name="d_mix_out", tm=ts, stage=stage, finish=merge_backward,
                                        extra=[(s_a, rows_d), (s_b, rows_d), (t_a, rows_d), (t_b, rows_d)],
                                        outs=[out_rows(d, BF16), out_rows(d, BF16), out_rows(2 * d, BF16)], parts=halves_of_tile, **back_sum)
    collect(slots, got)
    publish("w_a", wg(ga, dbr_a, nb=1, name="grad_w_branch_a"))
    publish("w_b", wg(pooled, dbr_b, nb=1, name="grad_w_branch_b"))
    stage, slots = exchange(to_sibling=["w_a", "w_b"])
    (dga,), got = _matmul(dbr_a, w_a_all, name="d_branch_a", tm=tw, outs=[out_rows(d, F32, tw)], finish=_store_blocks(d), stage=stage, **back_sum)
    collect(slots, got)
    dpooled, = _matmul(dbr_b, w_b_all, name="d_branch_b", tm=tw, outs=[out_rows(d, F32, tw)], finish=_store_blocks(d), **back_sum)
    sum_pairs("w_up", *squares)
    stage, slots = exchange(to_chips=("w_up",) + squares)
    (dproj, sums_seq, d_rg_a, d_rg_x, d_pool), got = _seq_backward(proj, xr, hr, p, dga, dpooled, dgab, conv_full, vecs,
                                                                   w_rg_a_all, w_rg_x_all, w_pool_all, ts=tsq, stage=stage)
    collect(slots, got)
    sum_quarters("w_up", *squares)
    for n, g in zip(block_weights, (d_rg_a, d_rg_x, d_pool)):
        publish(n, g)
    stage, slots = exchange(to_sibling=block_weights, swap=("w_up",) + squares)
    g_in, got = wg(h1, dproj, nb=N_CHIPS, name="grad_w_in", stage=stage)
    collect(slots, got)
    publish("w_in", g_in)
    sum_pairs(*block_weights)
    stage, slots = exchange(to_sibling=["w_in"], to_chips=block_weights)
    collect(slots, _run_stage(stage, "exchange_w_in_gates"))
    sum_pairs("w_in")
    sum_quarters(*block_weights)

    def norm_mix_backward(k, acc, extra_refs, out_refs):
        x_ref, dres_ref, mod_ref, g_ref = extra_refs
        out_refs[0][...] = _modulated_norm_backward(acc, x_ref[...], dres_ref[...], mod_ref, g_ref, out_refs[1], 1)

    stage, slots = exchange(to_chips=["w_in"], swap=block_weights)
    (grad_x, sums_mix), got = _matmul(dproj, w_in_all, name="d_proj_in", tm=ts, stage=stage, finish=norm_mix_backward,
                                      extra=[(x0, rows_d), (dx2, rows_d), (mod, vec8), (g_mix, vec1)],
                                      outs=[out_rows(d, F32), sums_out], zero_first=(1,), parts=halves_of_tile, **back_sum)
    collect(slots, got)
    sum_quarters("w_in")

    dmod = jnp.concatenate([sums_mix[1:2], sums_mix[0:1], sums_mlp[3:4], sums_mlp[1:2], sums_mlp[0:1], sums_head[1:2]], axis=1)
    row = jnp.concatenate([sums_mix[2:3], sums_mlp[2:3], sums_seq[4:5], sums_seq[5:6], sums_seq[6:7], sums_seq[7:8], sums_seq[8:9],
                           sums_seq[9:10], sums_head[0:1], sums_seq[0:CONV_WIDTH].reshape(1, CONV_WIDTH * d), dmod, sums_head[2:3]], axis=1)
    n_vec, n_conv, n_mod = 9 * d, CONV_WIDTH * d, N_MOD * d
    stage, slots = exchange(swap=["w_in"])
    i_rows = _add_allgather8(stage, row)
    got = _run_stage(stage, "exchange_sums_w_in")
    collect(slots, got)
    rows_all = got[i_rows]
    total = _sum_rows(rows_all, d)
    loss = total[0, n_vec + n_conv + n_mod]
    dmod_all = rows_all[:, 0, n_vec + n_conv:n_vec + n_conv + n_mod]
    g_ada, d_ada, m_ada, v_ada = _ada_backward(act_all.T, lax.dynamic_slice_in_dim(dmod_all, k_me * ada_cols, ada_cols, axis=1),
                                               w_ada[0], m_w_ada[0], v_w_ada[0])
    g_conv = lax.dynamic_slice_in_dim(total[:, n_vec:n_vec + n_conv].reshape(CONV_WIDTH, d), k_me * conv_cols, conv_cols, axis=1)
    vec_names = ("norm_mix_g", "norm_mlp_g", "conv_b", "b_rg_a", "b_rg_x", "a_param", "b_pool", "pool_scale", "final_g", "conv_w", "b_ada")
    vec_params = [(norm_mix_g, m_norm_mix_g, v_norm_mix_g), (norm_mlp_g, m_norm_mlp_g, v_norm_mlp_g), (conv_b, m_conv_b, v_conv_b),
                  (b_rg_a, m_b_rg_a, v_b_rg_a), (b_rg_x, m_b_rg_x, v_b_rg_x), (a_param, m_a_param, v_a_param), (b_pool, m_b_pool, v_b_pool),
                  (pool_scale, m_pool_scale, v_pool_scale), tuple(a.reshape(1, d) for a in (final_g, m_final_g, v_final_g)),
                  (conv_w[0], m_conv_w[0], v_conv_w[0]), (b_ada, m_b_ada, v_b_ada)]
    offsets = [i * d for i in range(9)] + [None, n_vec + n_conv]
    vec_res = _adamw_vectors(total, offsets, [None] * 9 + [g_conv, None], vec_params)
    vec_out = {n: [r.reshape(shape) for r in res] for n, res, shape in zip(
        vec_names, vec_res, [(1, d)] * 8 + [(d,), (1, CONV_WIDTH, conv_cols), (1, N_MOD * d)])}

    big_out = {}
    for n in names:
        shape2 = (-1, mine[n].shape[-1])
        g2 = quarter[n].reshape(shape2)
        res = (g2,) + tuple(_adamw(mine[n].reshape(shape2), g2, moments_m[n].reshape(shape2), moments_v[n].reshape(shape2), "adamw_" + n))
        big_out[n] = [r.reshape((1,) + mine[n].shape) for r in res]

    key = {"w_in": "w_in", "w_rg_a": "rg_a", "w_rg_x": "rg_x", "w_branch_a": "w_a", "w_pool": "pool", "w_branch_b": "w_b", "w_out": "w_out",
           "w_up": "w_up", "w_down": "w_down"}
    order = ("norm_mix_g", "norm_mlp_g", "w_ada", "b_ada", "w_in", "conv_w", "conv_b", "w_rg_a", "b_rg_a", "w_rg_x", "b_rg_x", "a_param",
             "w_branch_a", "w_pool", "b_pool", "pool_scale", "w_branch_b", "w_out", "w_up", "w_down", "final_g")
    ada_out = [g_ada[None], d_ada[None], m_ada[None], v_ada[None]]
    outs = [loss, grad_x[None]]
    for kind in range(4):
        for nme in order:
            outs.append(ada_out[kind] if nme == "w_ada" else big_out[key[nme]][kind] if nme in key else vec_out[nme][kind])
    return tuple(outs)
```

```python
import functools

import jax
import jax.numpy as jnp
from jax import lax
from jax.experimental import pallas as pl
from jax.experimental.pallas import tpu as pltpu

F32, BF16 = jnp.float32, jnp.bfloat16
D_MODEL = 1024
D_FF = 4 * D_MODEL
N_BLOCKS = 4
BLOCK = D_MODEL // N_BLOCKS
CONV_WIDTH = 4
POOL_WINDOWS = (2, 4, 8, 16)
CONV_HALO = 8
POOL_HALO = 16
N_MOD = 6
EPS = 1e-6
C_RG = 8.0
ADAM_LR, ADAM_B1, ADAM_B2, ADAM_EPS, ADAM_WD, ADAM_STEP = 0.001, 0.9, 0.999, 1e-08, 0.01, 10
N_DEV = 8
N_CHIPS = 4
VMEM_LIMIT_V7X = 56 * 2**20
MESH = pl.DeviceIdType.MESH
SDS = jax.ShapeDtypeStruct
HBM_SPEC = pl.BlockSpec(memory_space=pltpu.HBM)
VMEM_SPEC = pl.BlockSpec(memory_space=pltpu.VMEM)


def _params(*semantics):
    return pltpu.CompilerParams(dimension_semantics=semantics, vmem_limit_bytes=VMEM_LIMIT_V7X)


def _coords():
    return lax.axis_index("x"), lax.axis_index("y"), lax.axis_index("c")


def _flip(v, bit):
    return 1 - v if bit else v


def _chips():
    x, y, _ = _coords()
    return [(1 - x, y), (x, 1 - y), (1 - x, 1 - y)]


class _Stage:
    def __init__(self):
        self.inputs, self.in_specs, self.outputs, self.out_specs, self.aliases = [], [], [], [], {}
        self.parts, self.n_copies = [], 0

    def add(self, inputs, in_spec, outputs, out_spec, n_copies, build, alias=False):
        i0, o0 = len(self.inputs), len(self.outputs)
        self.inputs += inputs
        self.in_specs += [in_spec] * len(inputs)
        self.outputs += outputs
        self.out_specs += [out_spec] * len(outputs)
        if alias:
            self.aliases.update({i0 + i: o0 + i for i in range(len(inputs))})
        self.parts.append((build, i0, len(inputs), o0, len(outputs)))
        self.n_copies += n_copies
        return list(range(o0, o0 + len(outputs)))

    def copies(self, in_refs, out_refs):
        out = []
        for build, i0, ni, o0, no in self.parts:
            out += build(in_refs[i0:i0 + ni], out_refs[o0:o0 + no])
        assert len(out) == self.n_copies
        return out

    def run(self, in_refs, out_refs, send_sems, recv_sems, start):
        x, y, c = _coords()
        for s, (src, dst, to, landing) in enumerate(self.copies(in_refs, out_refs)):
            if to is None:
                cp = pltpu.make_async_copy(src, dst, send_sems.at[s])
                cp.start() if start else cp.wait()
                continue
            cp = pltpu.make_async_remote_copy(src_ref=src, dst_ref=dst, send_sem=send_sems.at[s], recv_sem=recv_sems.at[s],
                                              device_id=to, device_id_type=MESH)
            if start:
                cp.start()
            else:
                pltpu.make_async_remote_copy(src_ref=landing, dst_ref=landing, send_sem=send_sems.at[s], recv_sem=recv_sems.at[s],
                                             device_id=(x, y, c), device_id_type=MESH).wait_recv()
                cp.wait_send()


def _hosted(stage, body, *, name, in_specs, out_specs, out_shape, grid=(), scratch_shapes=(), compiler_params=None):
    if stage is None:
        return pl.pallas_call(body, name=name, grid=grid, in_specs=in_specs, out_specs=out_specs, out_shape=out_shape,
                              scratch_shapes=list(scratch_shapes), compiler_params=compiler_params)
    single = not isinstance(out_shape, (list, tuple))
    h_out_shape = [out_shape] if single else list(out_shape)
    h_out_specs = [out_specs] if single else list(out_specs)
    n_in, n_out, n_scr = len(in_specs), len(h_out_shape), len(scratch_shapes)
    s_in, s_out = len(stage.inputs), len(stage.outputs)

    def wrapped(*refs):
        h_in, st_in = refs[:n_in], refs[n_in:n_in + s_in]
        h_o, st_o = refs[n_in + s_in:n_in + s_in + n_out], refs[n_in + s_in + n_out:n_in + s_in + n_out + s_out]
        h_scr = refs[n_in + s_in + n_out + s_out:n_in + s_in + n_out + s_out + n_scr]
        send_sems, recv_sems = refs[n_in + s_in + n_out + s_out + n_scr:]
        if not grid:
            stage.run(st_in, st_o, send_sems, recv_sems, True)
            if body is not None:
                body(*h_in, *h_o, *h_scr)
            stage.run(st_in, st_o, send_sems, recv_sems, False)
            return
        ids = [pl.program_id(a) for a in range(len(grid))]
        first = functools.reduce(jnp.logical_and, [i == 0 for i in ids])
        last = functools.reduce(jnp.logical_and, [i == g - 1 for i, g in zip(ids, grid)])
        pl.when(first)(lambda: stage.run(st_in, st_o, send_sems, recv_sems, True))
        body(*h_in, *h_o, *h_scr)
        pl.when(last)(lambda: stage.run(st_in, st_o, send_sems, recv_sems, False))

    call = pl.pallas_call(
        wrapped, name=name, grid=grid, in_specs=list(in_specs) + stage.in_specs, out_specs=h_out_specs + stage.out_specs,
        out_shape=h_out_shape + stage.outputs, input_output_aliases={n_in + i: n_out + o for i, o in stage.aliases.items()},
        scratch_shapes=list(scratch_shapes) + [pltpu.SemaphoreType.DMA((stage.n_copies,)), pltpu.SemaphoreType.DMA((stage.n_copies,))],
        compiler_params=pltpu.CompilerParams(dimension_semantics=("arbitrary",) * len(grid), vmem_limit_bytes=VMEM_LIMIT_V7X),
    )

    def run(*operands):
        outs = call(*operands, *stage.inputs)
        host = outs[:n_out]
        return (host[0] if single else host), outs[n_out:]

    return run


def _run_stage(stage, name):
    return _hosted(stage, None, name=name, in_specs=[], out_specs=[], out_shape=[])()[1]


def _add_allgather8(stage, v):
    def build(ins, outs):
        x, y, c = _coords()
        me = 4 * x + 2 * y + c
        copies = [(ins[0], outs[0].at[me], None, None)]
        for r in range(1, N_DEV):
            peer = (_flip(x, r & 4), _flip(y, r & 2), _flip(c, r & 1))
            copies.append((ins[0], outs[0].at[me], peer, outs[0].at[me ^ r]))
        return copies

    return stage.add([v], VMEM_SPEC, [SDS((N_DEV,) + v.shape, v.dtype)], VMEM_SPEC, N_DEV, build)[0]


def _add_gather_ici(stage, shard, rows=None, whole=None):
    first, count = rows or (0, shard.shape[1])

    def build(ins, outs):
        x, y, c = _coords()
        k_me = 2 * x + y
        part = pl.ds(first, count)
        copies = [] if whole is not None else [(ins[0], outs[0].at[k_me], (x, y, 1 - c), outs[0].at[k_me])]
        for px, py in _chips():
            copies.append((ins[0].at[c, part], outs[0].at[k_me, c, part], (px, py, c), outs[0].at[2 * px + py, c, part]))
        return copies

    if whole is None:
        return stage.add([shard], HBM_SPEC, [SDS((N_CHIPS,) + shard.shape, shard.dtype)], HBM_SPEC, N_CHIPS, build)[0]

    def build_into(ins, outs):
        return build(ins[1:], outs)

    i0 = len(stage.inputs)
    out = stage.add([whole, shard], HBM_SPEC, [SDS(whole.shape, whole.dtype)], HBM_SPEC, N_CHIPS - 1, build_into)[0]
    stage.aliases[i0] = out
    return out


def _add_gather_d2d(stage, whole):
    def build(ins, outs):
        x, y, c = _coords()
        return [(outs[0].at[2 * px + py, c], outs[0].at[2 * px + py, c], (x, y, 1 - c), outs[0].at[2 * px + py, 1 - c]) for px, py in _chips()]

    return stage.add([whole], HBM_SPEC, [SDS(whole.shape, whole.dtype)], HBM_SPEC, N_CHIPS - 1, build, alias=True)[0]


def _add_reduce_d2d(stage, grads):
    def build(ins, outs):
        x, y, c = _coords()
        return [(ins[0].at[k, 1 - c], outs[0].at[k], (x, y, 1 - c), outs[0].at[k]) for k in range(N_CHIPS)]

    return stage.add([grads], HBM_SPEC, [SDS((N_CHIPS,) + grads.shape[2:], grads.dtype)], HBM_SPEC, N_CHIPS, build)[0]


def _add_reduce_ici(stage, partial):
    def build(ins, outs):
        x, y, c = _coords()
        return [(ins[0].at[2 * px + py], outs[0].at[j], (px, py, c), outs[0].at[j]) for j, (px, py) in enumerate(_chips())]

    return stage.add([partial], HBM_SPEC, [SDS((N_CHIPS - 1,) + partial.shape[1:], partial.dtype)], HBM_SPEC, N_CHIPS - 1, build)[0]


def _add_swap_halves(stage, quarter):
    def build(ins, outs):
        x, y, c = _coords()
        return [(outs[0].at[c], outs[0].at[c], (x, y, 1 - c), outs[0].at[1 - c])]

    return stage.add([quarter], HBM_SPEC, [SDS(quarter.shape, quarter.dtype)], HBM_SPEC, 1, build, alias=True)[0]


def _sum_pair(grads, from_sibling, place, name):
    _, _, rows, cols = grads.shape
    tr = min(rows, 512)

    def body(place_ref, g_ref, s_ref, o_ref):
        o_ref[0] = (g_ref[0, 0].astype(F32) + s_ref[0].astype(F32)).astype(BF16)

    spec = pl.BlockSpec((1, tr, cols), lambda k, i, place_ref: (k, i, 0))
    return pl.pallas_call(
        body, name=name, out_shape=SDS((N_CHIPS, rows, cols), BF16),
        grid_spec=pltpu.PrefetchScalarGridSpec(
            num_scalar_prefetch=1, grid=(N_CHIPS, rows // tr),
            in_specs=[pl.BlockSpec((1, 1, tr, cols), lambda k, i, place_ref: (k, place_ref[0], i, 0)), spec], out_specs=spec),
        compiler_params=_params("parallel", "parallel"),
    )(place, grads, from_sibling)


def _sum_quarter(grads, from_sibling, from_chips, place, name):
    _, _, rows, cols = grads.shape
    tr = min(rows, 512)

    def body(place_ref, g_ref, s_ref, c_ref, o_ref):
        acc = g_ref[0, 0].astype(F32) + s_ref[0].astype(F32)
        for j in range(N_CHIPS - 1):
            acc = acc + c_ref[j].astype(F32)
        o_ref[0] = acc

    return pl.pallas_call(
        body, name=name, out_shape=SDS((2, rows, cols), F32),
        grid_spec=pltpu.PrefetchScalarGridSpec(
            num_scalar_prefetch=1, grid=(rows // tr,),
            in_specs=[pl.BlockSpec((1, 1, tr, cols), lambda i, place_ref: (place_ref[1], place_ref[0], i, 0)),
                      pl.BlockSpec((1, tr, cols), lambda i, place_ref: (place_ref[1], i, 0)),
                      pl.BlockSpec((N_CHIPS - 1, tr, cols), lambda i, place_ref: (0, i, 0))],
            out_specs=pl.BlockSpec((1, tr, cols), lambda i, place_ref: (place_ref[0], i, 0))),
        compiler_params=_params("parallel"),
    )(place, grads, from_sibling, from_chips)


def _row_spec(ts, cols, col_block=0):
    return pl.BlockSpec((ts, cols), lambda *g: (g[0], col_block))


def _vec_spec(rows, cols):
    return pl.BlockSpec((rows, cols), lambda *g: (0, 0))


def _matmul(a, w, *, transposed, reduce_blocks, name, tm, outs, finish, extra=(), prepare=None, zero_first=(), parts=1, stage=None):
    m_rows = a.shape[0]
    nb, r, c = w.shape
    kb = c if transposed else r
    dims = (((1,), (1,)), ((), ())) if transposed else (((1,), (0,)), ((), ()))
    assert a.shape[1] == (nb * kb if reduce_blocks else kb) and m_rows % tm == 0 and tm % parts == 0
    n_extra, sub = len(extra), tm // parts

    def body(a_ref, w_ref, *rest):
        extra_refs, out_refs = rest[:n_extra], rest[n_extra:]
        if zero_first:
            @pl.when(pl.program_id(0) == 0)
            def _():
                for i in zero_first:
                    out_refs[i][...] = jnp.zeros_like(out_refs[i])
        def views(j):
            rows = pl.ds(j * sub, sub)
            return rows, [ref.at[rows] if ref.shape[0] == tm else ref for ref in extra_refs], [ref.at[rows] if ref.shape[0] == tm else ref for ref in out_refs]

        def product(j):
            rows, ex, ou = views(j)
            if reduce_blocks:
                acc = lax.dot_general(a_ref[rows, 0:kb], w_ref[0], dims, preferred_element_type=F32)
                for k in range(1, nb):
                    acc += lax.dot_general(a_ref[rows, k * kb:(k + 1) * kb], w_ref[k], dims, preferred_element_type=F32)
                return [(None, acc)]
            lhs = a_ref[rows, :] if prepare is None else prepare(a_ref.at[rows], ex, ou)
            return [(k, lax.dot_general(lhs, w_ref[k], dims, preferred_element_type=F32)) for k in range(nb)]

        pending = product(0)
        for j in range(parts):
            ahead = product(j + 1) if j + 1 < parts else None
            _, ex, ou = views(j)
            for k, acc in pending:
                finish(k, acc, ex, ou)
            pending = ahead

    return _hosted(
        stage, body, name=name, grid=(m_rows // tm,), out_shape=[s for s, _ in outs], out_specs=[s for _, s in outs],
        in_specs=[_row_spec(tm, a.shape[1]), pl.BlockSpec((nb, r, c), lambda *g: (0, 0, 0), pipeline_mode=pl.Buffered(1))] + [s for _, s in extra],
        compiler_params=_params("arbitrary"),
    )(a, w, *[e for e, _ in extra])


def _store_blocks(n):
    def finish(k, acc, extra_refs, out_refs):
        if k is None:
            out_refs[0][...] = acc.astype(out_refs[0].dtype)
        else:
            out_refs[0][:, k * n:(k + 1) * n] = acc.astype(out_refs[0].dtype)
    return finish


def _weight_grad(a, b, *, nb, name, tk, stage=None):
    s_rows, k1 = a.shape
    tn = b.shape[1] // nb
    assert k1 % tk == 0 and b.shape[1] % nb == 0

    def body(a_ref, b_ref, o_ref):
        o_ref[0] = lax.dot_general(a_ref[...], b_ref[...], (((0,), (0,)), ((), ())), preferred_element_type=F32).astype(o_ref.dtype)

    return _hosted(
        stage, body, name=name, grid=(nb, k1 // tk), out_shape=SDS((nb, k1, tn), BF16),
        in_specs=[pl.BlockSpec((s_rows, tk), lambda n, i: (0, i)), pl.BlockSpec((s_rows, tn), lambda n, i: (0, n))],
        out_specs=pl.BlockSpec((1, tk, tn), lambda n, i: (n, i, 0)), compiler_params=_params("parallel", "parallel"),
    )(a, b)


def _rms(x):
    r = lax.rsqrt(jnp.mean(x * x, axis=-1, keepdims=True) + EPS)
    return x * r, r


def _accumulate(acc_ref, row, value):
    acc_ref[row:row + 1, :] += jnp.sum(value, axis=0, keepdims=True)


def _gelu_parts(y):
    k0, k1 = 0.7978845608028654, 0.044715
    th = jnp.tanh(k0 * (y + k1 * (y * y * y)))
    gelu = 0.5 * y * (1.0 + th)
    dgelu = 0.5 * (1.0 + th) + 0.5 * y * (1.0 - th * th) * (k0 * (1.0 + 3.0 * k1 * (y * y)))
    return gelu, dgelu


def _one_minus_square(a, log_a):
    return -jnp.tanh(log_a) * (1.0 + a * a)


def _modulated_norm(x, mod_ref, g_ref, sc_row, sh_row):
    xh, _ = _rms(x)
    return ((xh * g_ref[...]) * (1.0 + mod_ref[sc_row:sc_row + 1, :]) + mod_ref[sh_row:sh_row + 1, :]).astype(BF16)


def _modulated_norm_backward(dh, x, dres, mod_ref, g_ref, acc_ref, sc_row):
    xh, r = _rms(x)
    _accumulate(acc_ref, 0, dh * (xh * g_ref[...]))
    _accumulate(acc_ref, 1, dh)
    dn = dh * (1.0 + mod_ref[sc_row:sc_row + 1, :])
    _accumulate(acc_ref, 2, dn * xh)
    dxh = dn * g_ref[...]
    return dres + r * (dxh - xh * jnp.mean(dxh * xh, axis=-1, keepdims=True))


def _seq_forward(proj, conv_w, vecs, w_a, w_x, w_p, *, ts, stage=None):
    s_rows = proj.shape[0]
    d = D_MODEL

    def body(x_ref, xh_ref, y_ref, u_ref, uh_ref, cw_ref, vec_ref, wa_ref, wx_ref, wp_ref,
             xr_o, hr_o, ga_o, p_o, pooled_o, carry, a_scr, b_scr):
        t = pl.program_id(0)

        @pl.when(t == 0)
        def _():
            carry[...] = jnp.zeros_like(carry)

        x = x_ref[...]
        halo = jnp.where(t == 0, 0.0, xh_ref[...])
        xx = jnp.concatenate([halo, x], axis=0)
        xr = vec_ref[0:1, :] + x * cw_ref[CONV_WIDTH - 1:CONV_WIDTH, :]
        for j in range(CONV_WIDTH - 1):
            xr = xr + pltpu.roll(xx, CONV_WIDTH - 1 - j, 0)[CONV_HALO:] * cw_ref[j:j + 1, :]
        xr_o[...] = xr
        xrb = xr.astype(BF16)
        zr = jnp.concatenate([jnp.dot(xrb[:, g * BLOCK:(g + 1) * BLOCK], wa_ref[g], preferred_element_type=F32)
                              for g in range(N_BLOCKS)], axis=1) + vec_ref[1:2, :]
        zi = jnp.concatenate([jnp.dot(xrb[:, g * BLOCK:(g + 1) * BLOCK], wx_ref[g], preferred_element_type=F32)
                              for g in range(N_BLOCKS)], axis=1) + vec_ref[2:3, :]
        r = jax.nn.sigmoid(zr)
        gate_i = jax.nn.sigmoid(zi)
        log_a = (-C_RG * r) * jax.nn.softplus(vec_ref[3:4, :])
        rows = lax.broadcasted_iota(jnp.int32, (ts, d), 0)
        a = jnp.exp(log_a)
        mult = jnp.where((rows == 0) & (t == 0), 1.0, jnp.sqrt(_one_minus_square(a, log_a)))
        a_scr[...] = a
        b_scr[...] = xr * gate_i * mult

        sub = lax.broadcasted_iota(jnp.int32, (8, d), 0)

        def chunk(i, h):
            a = a_scr[pl.ds(i * 8, 8), :]
            b = b_scr[pl.ds(i * 8, 8), :]
            for s in (1, 2, 4):
                keep = sub >= s
                b = jnp.where(keep, a * pltpu.roll(b, s, 0) + b, b)
                a = jnp.where(keep, a * pltpu.roll(a, s, 0), a)
            hh = a * h + b
            hr_o[pl.ds(i * 8, 8), :] = hh
            return hh[7:8, :]

        carry[...] = lax.fori_loop(0, ts // 8, chunk, carry[...])
        gelu, _ = _gelu_parts(y_ref[...])
        ga_o[...] = (gelu * hr_o[...]).astype(BF16)

        u = u_ref[...]
        uu = jnp.concatenate([jnp.where(t == 0, 0.0, uh_ref[...]), u], axis=0)
        pos = (rows[:, :BLOCK] + t * ts + 1).astype(F32)
        sums, have, parts = uu, 1, []
        for g, win in enumerate(POOL_WINDOWS):
            while have < win:
                sums = sums + pltpu.roll(sums, have, 0)
                have *= 2
            mean = sums[POOL_HALO:, g * BLOCK:(g + 1) * BLOCK] * (1.0 / jnp.minimum(pos, float(win)))
            parts.append(mean - u[:, g * BLOCK:(g + 1) * BLOCK])
        pb = jnp.concatenate(parts, axis=1).astype(BF16)
        p_o[...] = pb
        mixed = jnp.concatenate([jnp.dot(pb[:, g * BLOCK:(g + 1) * BLOCK], wp_ref[g], preferred_element_type=F32)
                                 for g in range(N_BLOCKS)], axis=1) + vec_ref[4:5, :]
        pooled_o[...] = (mixed * vec_ref[5:6, :]).astype(BF16)

    halo_x = pl.BlockSpec((CONV_HALO, d), lambda t: (jnp.maximum(t * (ts // CONV_HALO) - 1, 0), 0))
    halo_u = pl.BlockSpec((POOL_HALO, d), lambda t: (jnp.maximum(t * (ts // POOL_HALO) - 1, 0), 2))
    wspec = pl.BlockSpec((N_BLOCKS, BLOCK, BLOCK), lambda t: (0, 0, 0))
    return _hosted(
        stage, body, name="seq_forward", grid=(s_rows // ts,),
        out_shape=[SDS((s_rows, d), F32), SDS((s_rows, d), F32), SDS((s_rows, d), BF16), SDS((s_rows, d), BF16), SDS((s_rows, d), BF16)],
        in_specs=[_row_spec(ts, d, 0), halo_x, _row_spec(ts, d, 1), _row_spec(ts, d, 2), halo_u, _vec_spec(CONV_WIDTH, d), _vec_spec(8, d),
                  wspec, wspec, wspec],
        out_specs=[_row_spec(ts, d)] * 5,
        scratch_shapes=[pltpu.VMEM((1, d), F32), pltpu.VMEM((ts, d), F32), pltpu.VMEM((ts, d), F32)],
        compiler_params=_params("arbitrary"),
    )(proj, proj, proj, proj, proj, conv_w, vecs, w_a, w_x, w_p)


def _branches(ga, pooled, proj, w_a, w_b, *, ts, stage=None):
    s_rows, d = ga.shape

    def body(a_ref, p_ref, ga_ref, gb_ref, wa_ref, wb_ref, merged_o, sa_o, sb_o, ta_o, tb_o):
        bra = jnp.dot(a_ref[...], wa_ref[0], preferred_element_type=F32)
        brb = jnp.dot(p_ref[...], wb_ref[0], preferred_element_type=F32)
        sa, sb = jax.nn.sigmoid(ga_ref[...]), jax.nn.sigmoid(gb_ref[...])
        merged_o[...] = (sa * bra + sb * brb).astype(BF16)
        sa_o[...] = sa.astype(BF16)
        sb_o[...] = sb.astype(BF16)
        ta_o[...] = (bra * (sa * (1.0 - sa))).astype(BF16)
        tb_o[...] = (brb * (sb * (1.0 - sb))).astype(BF16)

    rows = _row_spec(ts, d)
    wspec = pl.BlockSpec((1, d, d), lambda *g: (0, 0, 0), pipeline_mode=pl.Buffered(1))
    return _hosted(
        stage, body, name="branches", grid=(s_rows // ts,), out_shape=[SDS((s_rows, d), BF16)] * 5,
        in_specs=[rows, rows, _row_spec(ts, d, 3), _row_spec(ts, d, 4), wspec, wspec], out_specs=[rows] * 5,
        compiler_params=_params("arbitrary"),
    )(ga, pooled, proj, proj, w_a, w_b)


def _seq_backward(proj, xr, hr, p, dga, dpooled, dgab, conv_w, vecs, w_a, w_x, w_p, *, ts, stage=None):
    s_rows = proj.shape[0]
    d = D_MODEL
    n_t = s_rows // ts

    def block_dots(lhs, w_ref, dims):
        return jnp.concatenate([lax.dot_general(lhs[:, g * BLOCK:(g + 1) * BLOCK], w_ref[g], dims, preferred_element_type=F32)
                                for g in range(N_BLOCKS)], axis=1)

    def add_block_grads(dw_ref, lhs, rhs):
        for g in range(N_BLOCKS):
            dw_ref[g] += lax.dot_general(lhs[:, g * BLOCK:(g + 1) * BLOCK], rhs[:, g * BLOCK:(g + 1) * BLOCK],
                                         (((0,), (0,)), ((), ())), preferred_element_type=F32)

    nn, nt = (((1,), (0,)), ((), ())), (((1,), (1,)), ((), ()))

    def body(x_ref, y_ref, xr_ref, hr_ref, hh_ref, p_ref, dga_ref, dpl_ref, dgab_ref, cw_ref, vec_ref, wa_ref, wx_ref, wp_ref,
             dproj_o, acc_o, dwa_o, dwx_o, dwp_o, g_carry, dxr_carry, q_carry, a_scr, b_scr, g_scr):
        step = pl.program_id(0)
        t = n_t - 1 - step

        @pl.when(step == 0)
        def _():
            for ref in (acc_o, dwa_o, dwx_o, dwp_o, g_carry, dxr_carry, q_carry):
                ref[...] = jnp.zeros_like(ref)

        rows = lax.broadcasted_iota(jnp.int32, (ts, d), 0)
        start = (rows == 0) & (t == 0)
        xr = xr_ref[...]
        xrb = xr.astype(BF16)
        r = jax.nn.sigmoid(block_dots(xrb, wa_ref, nn) + vec_ref[1:2, :])
        gate_i = jax.nn.sigmoid(block_dots(xrb, wx_ref, nn) + vec_ref[2:3, :])
        sp = jax.nn.softplus(vec_ref[3:4, :])
        log_a = (-C_RG * r) * sp
        a = jnp.exp(log_a)
        m_square = _one_minus_square(a, log_a)
        mult = jnp.where(start, 1.0, jnp.sqrt(m_square))
        gelu, dgelu = _gelu_parts(y_ref[...])
        dga_t = dga_ref[...]
        hr_t = hr_ref[...]
        dy = dga_t * hr_t * dgelu

        a_scr[...] = jnp.where(rows == ts - 1, 1.0, pltpu.roll(a, ts - 1, 0))
        b_scr[...] = dga_t * gelu
        sub = lax.broadcasted_iota(jnp.int32, (8, d), 0)

        def chunk(i, g_next):
            at = pl.multiple_of((ts // 8 - 1 - i) * 8, 8)
            aa = a_scr[pl.ds(at, 8), :]
            bb = b_scr[pl.ds(at, 8), :]
            for s in (1, 2, 4):
                keep = sub < 8 - s
                bb = jnp.where(keep, bb + aa * pltpu.roll(bb, 8 - s, 0), bb)
                aa = jnp.where(keep, aa * pltpu.roll(aa, 8 - s, 0), aa)
            gg = aa * g_next + bb
            g_scr[pl.ds(at, 8), :] = gg
            return gg[0:1, :]

        g_first = lax.fori_loop(0, ts // 8, chunk, g_carry[...])
        g_carry[...] = a[0:1, :] * g_first
        g = g_scr[...]

        h_before = jnp.where(t == 0, 0.0, hh_ref[CONV_HALO - 1:CONV_HALO, :])
        h_prev = jnp.where(rows == 0, h_before, pltpu.roll(hr_t, 1, 0))
        dxr = g * gate_i * mult
        d_i = g * xr * mult
        d_mult = g * xr * gate_i
        dlog_a = (g * h_prev) * a - jnp.where(start, 0.0, d_mult * (a * a) * lax.rsqrt(m_square))
        dzr = (dlog_a * (-C_RG * sp)) * (r * (1.0 - r))
        dzi = d_i * (gate_i * (1.0 - gate_i))
        _accumulate(acc_o, 5, dzr)
        _accumulate(acc_o, 6, dzi)
        _accumulate(acc_o, 7, dlog_a * (-C_RG * r) * jax.nn.sigmoid(vec_ref[3:4, :]))
        dzrb, dzib = dzr.astype(BF16), dzi.astype(BF16)
        add_block_grads(dwa_o, xrb, dzrb)
        add_block_grads(dwx_o, xrb, dzib)
        dxr = dxr + block_dots(dzrb, wa_ref, nt) + block_dots(dzib, wx_ref, nt)

        x = x_ref[...]
        ext = jnp.concatenate([dxr, dxr_carry[...]], axis=0)
        dx = dxr * cw_ref[CONV_WIDTH - 1:CONV_WIDTH, :]
        _accumulate(acc_o, CONV_WIDTH - 1, x * dxr)
        for j in range(CONV_WIDTH - 1):
            ahead = pltpu.roll(ext, ts + CONV_HALO - (CONV_WIDTH - 1 - j), 0)[:ts]
            dx = dx + ahead * cw_ref[j:j + 1, :]
            _accumulate(acc_o, j, x * ahead)
        _accumulate(acc_o, 4, dxr)
        dxr_carry[...] = dxr[0:CONV_HALO, :]

        pb = p_ref[...]
        mixed = block_dots(pb, wp_ref, nn) + vec_ref[4:5, :]
        dpl = dpl_ref[...]
        _accumulate(acc_o, 9, dpl * mixed)
        dmixed = dpl * vec_ref[5:6, :]
        _accumulate(acc_o, 8, dmixed)
        dmb = dmixed.astype(BF16)
        add_block_grads(dwp_o, pb, dmb)
        dp = block_dots(dmb, wp_ref, nt)
        pos = (rows[:, :BLOCK] + t * ts + 1).astype(F32)
        q = jnp.concatenate([dp[:, g * BLOCK:(g + 1) * BLOCK] * (1.0 / jnp.minimum(pos, float(win))) for g, win in enumerate(POOL_WINDOWS)], axis=1)
        sums, have, parts = jnp.concatenate([q, q_carry[...]], axis=0), 1, []
        for g, win in enumerate(POOL_WINDOWS):
            while have < win:
                sums = sums + pltpu.roll(sums, ts + POOL_HALO - have, 0)
                have *= 2
            parts.append(sums[:ts, g * BLOCK:(g + 1) * BLOCK])
        du = jnp.concatenate(parts, axis=1) - dp
        q_carry[...] = q[0:POOL_HALO, :]

        dproj_o[:, 0:d] = dx.astype(BF16)
        dproj_o[:, d:2 * d] = dy.astype(BF16)
        dproj_o[:, 2 * d:3 * d] = du.astype(BF16)
        dproj_o[:, 3 * d:5 * d] = dgab_ref[...]

    def rev(cols, col_block=0):
        return pl.BlockSpec((ts, cols), lambda i: (n_t - 1 - i, col_block))

    halo_h = pl.BlockSpec((CONV_HALO, d), lambda i: (jnp.maximum((n_t - 1 - i) * (ts // CONV_HALO) - 1, 0), 0))
    wspec = pl.BlockSpec((N_BLOCKS, BLOCK, BLOCK), lambda i: (0, 0, 0))
    return _hosted(
        stage, body, name="seq_backward", grid=(n_t,),
        out_shape=[SDS((s_rows, 5 * d), BF16), SDS((16, d), F32)] + [SDS((N_BLOCKS, BLOCK, BLOCK), F32)] * 3,
        in_specs=[rev(d, 0), rev(d, 1), rev(d), rev(d), halo_h, rev(d), rev(d), rev(d), rev(2 * d), _vec_spec(CONV_WIDTH, d), _vec_spec(8, d),
                  wspec, wspec, wspec],
        out_specs=[rev(5 * d), _vec_spec(16, d), wspec, wspec, wspec],
        scratch_shapes=[pltpu.VMEM((1, d), F32), pltpu.VMEM((CONV_HALO, d), F32), pltpu.VMEM((POOL_HALO, d), F32),
                        pltpu.VMEM((ts, d), F32), pltpu.VMEM((ts, d), F32), pltpu.VMEM((ts, d), F32)],
        compiler_params=_params("arbitrary"),
    )(proj, proj, xr, hr, hr, p, dga, dpooled, dgab, conv_w, vecs, w_a, w_x, w_p)


def _adamw_math(w, g, m, v):
    m = ADAM_B1 * m + (1.0 - ADAM_B1) * g
    v = ADAM_B2 * v + (1.0 - ADAM_B2) * jnp.square(g)
    m_hat = m / (1.0 - ADAM_B1 ** ADAM_STEP)
    v_hat = v / (1.0 - ADAM_B2 ** ADAM_STEP)
    delta = -ADAM_LR * (m_hat / (jnp.sqrt(v_hat) + ADAM_EPS) + ADAM_WD * w)
    return delta, m, v


def _adamw(w, g, m, v, name):
    rows, cols = w.shape
    tr = min(rows, 512)

    def body(w_ref, g_ref, m_ref, v_ref, d_ref, mo_ref, vo_ref):
        d_ref[...], mo_ref[...], vo_ref[...] = _adamw_math(w_ref[...], g_ref[...], m_ref[...], v_ref[...])

    spec = pl.BlockSpec((tr, cols), lambda i: (i, 0))
    return pl.pallas_call(
        body, name=name, grid=(rows // tr,), out_shape=[SDS((rows, cols), F32)] * 3, in_specs=[spec] * 4, out_specs=[spec] * 3,
        compiler_params=_params("parallel"),
    )(w, g, m, v)


def _prologue(c, conv_shard, w_ada, b_cols, w_in_halves, gate_halves):
    n_cols = w_ada.shape[1]
    n_g = len(gate_halves)
    half_rows = w_in_halves.shape[1] // 2
    n_sems = 3 * N_DEV + N_CHIPS * n_g + 5 + 2 + 3

    def body(c_ref, conv_ref, w_ref, b_ref, win_ref, *rest):
        gate_refs = rest[:n_g]
        act_o, mod_o, conv_o, win_o = rest[n_g:n_g + 4]
        gate_o = rest[n_g + 4:2 * n_g + 4]
        c_all, mod_cols, send_sems, recv_sems = rest[2 * n_g + 4:]
        x, y, c = _coords()
        me, k_me = 4 * x + 2 * y + c, 2 * x + y
        sibling, x_nbr, y_nbr = (x, y, 1 - c), (1 - x, y, c), (x, 1 - y, c)
        k_x, k_y, k_d = 2 * (1 - x) + y, 2 * x + (1 - y), 2 * (1 - x) + (1 - y)
        counter = iter(range(n_sems))

        def copy(src, dst, to, landing):
            s = next(counter)
            out = pltpu.make_async_remote_copy(src_ref=src, dst_ref=dst, send_sem=send_sems.at[s], recv_sem=recv_sems.at[s],
                                               device_id=to, device_id_type=MESH)
            arrival = pltpu.make_async_remote_copy(src_ref=landing, dst_ref=landing, send_sem=send_sems.at[s], recv_sem=recv_sems.at[s],
                                                   device_id=(x, y, c), device_id_type=MESH)
            return out, arrival

        def allgather(src, out):
            s = next(counter)
            own = pltpu.make_async_copy(src, out.at[me], send_sems.at[s])
            peers = [copy(src, out.at[me], (_flip(x, r & 4), _flip(y, r & 2), _flip(c, r & 1)), out.at[me ^ r]) for r in range(1, N_DEV)]
            return own, peers

        def start(group):
            own, peers = group
            own.start()
            for out, _ in peers:
                out.start()

        def finish(group):
            own, peers = group
            for out, arrival in peers:
                arrival.wait_recv()
                out.wait_send()
            own.wait()

        rows_a, rows_b = pl.ds(0, half_rows), pl.ds(half_rows, half_rows)
        c_group, conv_group = allgather(c_ref, c_all), allgather(conv_ref, conv_o)
        gates = []
        for g_ref, g_out in zip(gate_refs, gate_o):
            gates.append(copy(g_ref, g_out.at[k_me], sibling, g_out.at[k_me]))
            gates += [copy(g_ref.at[c], g_out.at[k_me, c], (px, py, c), g_out.at[2 * px + py, c]) for px, py in _chips()]
        own_in = copy(win_ref, win_o.at[k_me], sibling, win_o.at[k_me])
        to_x = [copy(win_ref.at[c, rows], win_o.at[k_me, c, rows], x_nbr, win_o.at[k_x, c, rows]) for rows in (rows_a, rows_b)]
        to_y = [copy(win_ref.at[c, rows], win_o.at[k_me, c, rows], y_nbr, win_o.at[k_y, c, rows]) for rows in (rows_b, rows_a)]
        mod_group = allgather(mod_cols, mod_o)
        relay_y = copy(win_o.at[k_x, c, rows_a], win_o.at[k_x, c, rows_a], y_nbr, win_o.at[k_d, c, rows_a])
        relay_x = copy(win_o.at[k_y, c, rows_b], win_o.at[k_y, c, rows_b], x_nbr, win_o.at[k_d, c, rows_b])
        sib_x, sib_y, sib_d = (copy(win_o.at[k, c], win_o.at[k, c], sibling, win_o.at[k, 1 - c]) for k in (k_x, k_y, k_d))

        start(c_group)
        start(conv_group)
        for out, _ in gates + [own_in] + to_x + to_y:
            out.start()
        finish(c_group)
        cv = jnp.concatenate([c_all[r] for r in range(N_DEV)], axis=0)
        act = cv * jax.nn.sigmoid(cv)
        act_o[...] = act
        mod_cols[...] = jnp.dot(act.astype(BF16), w_ref[...].astype(BF16), preferred_element_type=F32) + b_ref[...]
        start(mod_group)
        to_x[0][1].wait_recv()
        relay_y[0].start()
        to_y[0][1].wait_recv()
        relay_x[0].start()
        finish(mod_group)
        finish(conv_group)
        to_x[1][1].wait_recv()
        sib_x[0].start()
        to_y[1][1].wait_recv()
        sib_y[0].start()
        for out, arrival in gates + [own_in]:
            arrival.wait_recv()
            out.wait_send()
        for out, arrival in (relay_y, relay_x):
            arrival.wait_recv()
            out.wait_send()
        sib_d[0].start()
        for out, _ in to_x + to_y:
            out.wait_send()
        for out, arrival in (sib_x, sib_y, sib_d):
            arrival.wait_recv()
            out.wait_send()

    gate_shapes = [SDS((N_CHIPS,) + g.shape, g.dtype) for g in gate_halves]
    return pl.pallas_call(
        body, name="prologue",
        out_shape=[SDS((N_DEV, c.shape[1]), F32), SDS((N_DEV, N_DEV, n_cols), F32), SDS((N_DEV,) + conv_shard.shape, F32),
                   SDS((N_CHIPS,) + w_in_halves.shape, w_in_halves.dtype)] + gate_shapes,
        in_specs=[VMEM_SPEC] * 4 + [HBM_SPEC] * (1 + n_g), out_specs=[VMEM_SPEC] * 3 + [HBM_SPEC] * (1 + n_g),
        scratch_shapes=[pltpu.VMEM((N_DEV,) + c.shape, F32), pltpu.VMEM((N_DEV, n_cols), F32),
                        pltpu.SemaphoreType.DMA((n_sems,)), pltpu.SemaphoreType.DMA((n_sems,))],
        compiler_params=pltpu.CompilerParams(vmem_limit_bytes=VMEM_LIMIT_V7X),
    )(c, conv_shard, w_ada, b_cols, w_in_halves, *gate_halves)


def _ada_backward(act_t, dmod_cols, w, m, v):
    rows, cols = w.shape
    tn = 512

    def body(a_ref, dm_ref, w_ref, m_ref, v_ref, g_ref, d_ref, mo_ref, vo_ref):
        g = jnp.dot(a_ref[...].astype(BF16), dm_ref[...].astype(BF16), preferred_element_type=F32)
        g_ref[...] = g
        d_ref[...], mo_ref[...], vo_ref[...] = _adamw_math(w_ref[...], g, m_ref[...], v_ref[...])

    spec = pl.BlockSpec((rows, tn), lambda j: (0, j))
    return pl.pallas_call(
        body, name="ada_backward", grid=(cols // tn,), out_shape=[SDS((rows, cols), F32)] * 4,
        in_specs=[pl.BlockSpec((rows, N_DEV), lambda j: (0, 0)), pl.BlockSpec((N_DEV, tn), lambda j: (0, j)), spec, spec, spec],
        out_specs=[spec] * 4, compiler_params=_params("parallel"),
    )(act_t, dmod_cols, w, m, v)


def _sum_rows(gathered, n_loss):
    n = gathered.shape[2]

    def body(g_ref, o_ref):
        acc = g_ref[0]
        for r in range(1, N_DEV):
            acc = acc + g_ref[r]
        o_ref[...] = acc
        o_ref[:, n - n_loss:n] = jnp.broadcast_to(jnp.sum(acc[:, n - n_loss:n], axis=1, keepdims=True), (1, n_loss))

    return pl.pallas_call(body, name="sum_rows", out_shape=SDS((1, n), F32), in_specs=[VMEM_SPEC], out_specs=VMEM_SPEC)(gathered)


def _adamw_vectors(total, offsets, separate, params):
    n_p = len(params)
    apart = [g for g in separate if g is not None]

    def body(*refs):
        total_ref, apart_refs = refs[0], list(refs[1:1 + len(apart)])
        ins, outs = refs[1 + len(apart):1 + len(apart) + 3 * n_p], refs[1 + len(apart) + 3 * n_p:]
        for i in range(n_p):
            w_ref, m_ref, v_ref = ins[3 * i:3 * i + 3]
            g = apart_refs.pop(0)[...] if offsets[i] is None else total_ref[:, offsets[i]:offsets[i] + w_ref.shape[1]]
            outs[4 * i][...] = g
            outs[4 * i + 1][...], outs[4 * i + 2][...], outs[4 * i + 3][...] = _adamw_math(w_ref[...], g, m_ref[...], v_ref[...])

    flat = [a for p in params for a in p]
    res = pl.pallas_call(
        body, name="adamw_vectors", out_shape=[SDS(p[0].shape, F32) for p in params for _ in range(4)],
        in_specs=[VMEM_SPEC] * (1 + len(apart) + len(flat)), out_specs=[VMEM_SPEC] * (4 * n_p),
    )(total, *apart, *flat)
    return [tuple(res[4 * i:4 * i + 4]) for i in range(n_p)]


def _blocks_to_pieces(w):
    nb, rows, n = w.shape
    q = rows // N_CHIPS
    return w.reshape(nb, N_CHIPS, q, n).transpose(1, 0, 2, 3).reshape(N_CHIPS, 2, nb // 2, q, n)


def _pieces_to_blocks(w):
    n_chips, _, half, q, n = w.shape
    return w.reshape(n_chips, 2 * half, q, n).transpose(1, 0, 2, 3).reshape(2 * half, n_chips * q, n)


def kernel(x, c, norm_mix_g, norm_mlp_g, w_ada, b_ada, w_in, conv_w, conv_b, w_rg_a, b_rg_a, w_rg_x, b_rg_x, a_param, w_branch_a, w_pool, b_pool, pool_scale, w_branch_b, w_out, w_up, w_down, final_g, loss_target, m_norm_mix_g, m_norm_mlp_g, m_w_ada, m_b_ada, m_w_in, m_conv_w, m_conv_b, m_w_rg_a, m_b_rg_a, m_w_rg_x, m_b_rg_x, m_a_param, m_w_branch_a, m_w_pool, m_b_pool, m_pool_scale, m_w_branch_b, m_w_out, m_w_up, m_w_down, m_final_g, v_norm_mix_g, v_norm_mlp_g, v_w_ada, v_b_ada, v_w_in, v_conv_w, v_conv_b, v_w_rg_a, v_b_rg_a, v_w_rg_x, v_b_rg_x, v_a_param, v_w_branch_a, v_w_pool, v_b_pool, v_pool_scale, v_w_branch_b, v_w_out, v_w_up, v_w_down, v_final_g):
    d = D_MODEL
    s_rows = x.shape[1]
    ts, tsq = min(512, s_rows), min(256, s_rows)
    xi, yi, ci = _coords()
    k_me, me = 2 * xi + yi, 4 * xi + 2 * yi + ci
    ada_cols = w_ada.shape[2]
    conv_cols = conv_w.shape[2]
    n_in, n_up = w_in.shape[2], w_up.shape[2]

    names = ("w_in", "w_up", "w_down", "w_a", "w_b", "w_out", "rg_a", "rg_x", "pool")
    mine = dict(zip(names, (w_in[0], w_up[0], w_down[0], w_branch_a[0], w_branch_b[0], w_out[0], w_rg_a[0], w_rg_x[0], w_pool[0])))
    moments_m = dict(zip(names, (m_w_in[0], m_w_up[0], m_w_down[0], m_w_branch_a[0], m_w_branch_b[0], m_w_out[0], m_w_rg_a[0], m_w_rg_x[0], m_w_pool[0])))
    moments_v = dict(zip(names, (v_w_in[0], v_w_up[0], v_w_down[0], v_w_branch_a[0], v_w_branch_b[0], v_w_out[0], v_w_rg_a[0], v_w_rg_x[0], v_w_pool[0])))
    block_weights, squares = ("rg_a", "rg_x", "pool"), ("w_a", "w_b", "w_out")
    place = jnp.stack([ci, k_me]).astype(jnp.int32)
    g_mix, g_mlp, g_fin = norm_mix_g, norm_mlp_g, final_g.reshape(1, d)
    x0, target = x[0], loss_target[0]
    wg = functools.partial(_weight_grad, tk=min(1024, d))
    plain, back, back_sum = (dict(transposed=t, reduce_blocks=r) for t, r in ((False, False), (True, False), (True, True)))
    rows_d, vec8, vec1 = _row_spec(ts, d), _vec_spec(8, d), _vec_spec(1, d)
    tw = min(1024, s_rows)
    halves_of_tile = 2

    def out_rows(cols, dtype, t=ts):
        return SDS((s_rows, cols), dtype), _row_spec(t, cols)

    sums_out = (SDS((8, d), F32), vec8)

    def halves(n):
        return mine[n].astype(BF16).reshape(2, -1, mine[n].shape[-1])

    def blocks(w):
        return _pieces_to_blocks(w.reshape(N_CHIPS, 2, N_BLOCKS // 2, BLOCK // N_CHIPS, BLOCK))

    act_all, mod_all, conv_all, w_in_whole, *gate_wholes = _prologue(
        c, conv_w[0], w_ada[0], lax.dynamic_slice_in_dim(b_ada, k_me * ada_cols, ada_cols, axis=1), halves("w_in"), [halves(n) for n in block_weights])
    conv_full = conv_all[0::2].transpose(1, 0, 2).reshape(CONV_WIDTH, d)
    mod_mine = lax.dynamic_index_in_dim(mod_all, me, axis=1, keepdims=False)[0::2]
    mod = jnp.concatenate([mod_mine.reshape(N_MOD, d), jnp.zeros((8 - N_MOD, d), F32)], axis=0)
    w_in_all = w_in_whole.reshape(N_CHIPS, d, n_in)
    vecs = jnp.concatenate([conv_b, b_rg_a, b_rg_x, a_param, b_pool, pool_scale, jnp.zeros((2, d), F32)], axis=0)

    def norm_first(a_ref, extra_refs, out_refs):
        h = _modulated_norm(a_ref[...], extra_refs[0], extra_refs[1], 1, 0)
        out_refs[1][...] = h
        return h

    stage = _Stage()
    i_blk = [_add_gather_d2d(stage, g) for g in gate_wholes]
    i_sq = [_add_gather_ici(stage, halves(n)) for n in squares]
    down_rows = mine["w_down"].shape[0] // 8
    i_down = _add_gather_ici(stage, halves("w_down"), rows=(0, down_rows))
    (proj, h1), got = _matmul(x0, w_in_all, name="proj_in", tm=ts, extra=[(mod, vec8), (g_mix, vec1)], prepare=norm_first,
                              outs=[out_rows(N_CHIPS * n_in, F32), out_rows(d, BF16)], finish=_store_blocks(n_in), parts=halves_of_tile,
                              stage=stage, **plain)
    w_rg_a_all, w_rg_x_all, w_pool_all = (blocks(got[i]) for i in i_blk)
    stage = _Stage()
    i_sq = [_add_gather_d2d(stage, got[i]) for i in i_sq]
    i_up = _add_gather_ici(stage, halves("w_up"))
    i_down = _add_gather_ici(stage, halves("w_down"), rows=(down_rows, down_rows), whole=got[i_down])
    (xr, hr, ga, p, pooled), got = _seq_forward(proj, conv_full, vecs, w_rg_a_all, w_rg_x_all, w_pool_all, ts=tsq, stage=stage)
    w_a_all, w_b_all, w_out_all = (got[i].reshape(1, d, d) for i in i_sq)
    stage = _Stage()
    i_up = _add_gather_d2d(stage, got[i_up])
    i_down = _add_gather_ici(stage, halves("w_down"), rows=(2 * down_rows, 2 * down_rows), whole=got[i_down])
    (merged, s_a, s_b, t_a, t_b), got = _branches(ga, pooled, proj, w_a_all, w_b_all, ts=ts, stage=stage)
    w_up_all = got[i_up].reshape(N_CHIPS, d, n_up)
    stage = _Stage()
    i_down = _add_gather_d2d(stage, got[i_down])

    def residual_norm(k, acc, extra_refs, out_refs):
        x_ref, mod_ref, g_ref = extra_refs
        out_refs[0][...] = acc.astype(BF16)
        x2_t = x_ref[...] + mod_ref[2:3, :] * acc
        out_refs[1][...] = x2_t
        out_refs[2][...] = _modulated_norm(x2_t, mod_ref, g_ref, 4, 3)

    (mo, x2, h2), got = _matmul(merged, w_out_all, name="mix_out", tm=ts, extra=[(x0, rows_d), (mod, vec8), (g_mlp, vec1)],
                                outs=[out_rows(d, BF16), out_rows(d, F32), out_rows(d, BF16)], finish=residual_norm, parts=halves_of_tile,
                                stage=stage, transposed=False, reduce_blocks=True)
    w_down_all = got[i_down].reshape(N_CHIPS, D_FF // N_CHIPS, d)

    def relu_squared(k, acc, extra_refs, out_refs):
        out_refs[0][:, k * n_up:(k + 1) * n_up] = jnp.square(jnp.maximum(acc, 0.0)).astype(BF16)

    ff, = _matmul(h2, w_up_all, name="mlp_up", tm=ts, outs=[out_rows(D_FF, BF16)], finish=relu_squared, parts=halves_of_tile, **plain)

    def loss_head(k, acc, extra_refs, out_refs):
        x2_ref, mod_ref, g_ref, t_ref = extra_refs
        dx_ref, df_ref, acc_ref = out_refs
        xh, r = _rms(x2_ref[...] + mod_ref[5:6, :] * acc)
        err = xh * g_ref[...] - t_ref[...]
        dy = err / d
        dxh = dy * g_ref[...]
        dx3_t = r * (dxh - xh * jnp.mean(dxh * xh, axis=-1, keepdims=True))
        dx_ref[...] = dx3_t
        df_ref[...] = (dx3_t * mod_ref[5:6, :]).astype(BF16)
        _accumulate(acc_ref, 0, dy * xh)
        _accumulate(acc_ref, 1, dx3_t * acc)
        _accumulate(acc_ref, 2, jnp.square(err) * (0.5 / d))

    dx3, dffo, sums_head = _matmul(ff, w_down_all, name="mlp_down", tm=ts, extra=[(x2, rows_d), (mod, vec8), (g_fin, vec1), (target, rows_d)],
                                   outs=[out_rows(d, F32), out_rows(d, BF16), sums_out], finish=loss_head, zero_first=(2,), parts=halves_of_tile,
                                   transposed=False, reduce_blocks=True)

    partial_of, from_sibling, chip_sum, from_chips, half_done, quarter = {}, {}, {}, {}, {}, {}

    def publish(n, g):
        g = _blocks_to_pieces(g).astype(BF16) if n in block_weights else g
        partial_of[n] = g.reshape(N_CHIPS, 2, -1, mine[n].shape[-1])

    def exchange(to_sibling=(), to_chips=(), swap=()):
        st = _Stage()
        slots = [(n, from_sibling, _add_reduce_d2d(st, partial_of[n])) for n in to_sibling]
        slots += [(n, from_chips, _add_reduce_ici(st, chip_sum[n])) for n in to_chips]
        slots += [(n, quarter, _add_swap_halves(st, half_done[n])) for n in swap]
        return st, slots

    def collect(slots, outs):
        for n, where, i in slots:
            where[n] = outs[i]

    def sum_pairs(*ns):
        for n in ns:
            chip_sum[n] = _sum_pair(partial_of[n], from_sibling[n], place, "sum_pair_" + n)

    def sum_quarters(*ns):
        for n in ns:
            half_done[n] = _sum_quarter(partial_of[n], from_sibling[n], from_chips[n], place, "sum_quarter_" + n)

    publish("w_down", wg(ff, dffo, nb=1, name="grad_w_down"))

    def relu_squared_backward(k, acc, extra_refs, out_refs):
        cols = slice(k * n_up, (k + 1) * n_up)
        out_refs[0][:, cols] = (acc * (2.0 * jnp.sqrt(extra_refs[0][:, cols].astype(F32)))).astype(BF16)

    stage, slots = exchange(to_sibling=["w_down"])
    (dup,), got = _matmul(dffo, w_down_all, name="d_mlp_down", tm=ts, extra=[(ff, _row_spec(ts, D_FF))], outs=[out_rows(D_FF, BF16)],
                          finish=relu_squared_backward, parts=halves_of_tile, stage=stage, **back)
    collect(slots, got)
    sum_pairs("w_down")

    def norm_mlp_backward(k, acc, extra_refs, out_refs):
        x2_ref, dres_ref, mod_ref, g_ref, mo_ref = extra_refs
        dx_ref, dmo_ref, acc_ref = out_refs
        dx =_modulated_norm_backward(acc, x2_ref[...], dres_ref[...], mod_ref, g_ref, acc_ref, 4)
        dx_ref[...] = dx
        dmo_ref[...] = (dx * mod_ref[2:3, :]).astype(BF16)
        _accumulate(acc_ref, 3, dx * mo_ref[...].astype(F32))

    stage, slots = exchange(to_chips=["w_down"])
    (dx2, dmo, sums_mlp), got = _matmul(dup, w_up_all, name="d_mlp_up", tm=ts, stage=stage, finish=norm_mlp_backward,
                                        extra=[(x2, rows_d), (dx3, rows_d), (mod, vec8), (g_mlp, vec1), (mo, rows_d)],
                                        outs=[out_rows(d, F32), out_rows(d, BF16), sums_out], zero_first=(2,), parts=halves_of_tile, **back_sum)
    collect(slots, got)
    sum_quarters("w_down")
    stage, slots = exchange(swap=["w_down"])
    g_up, got = wg(h2, dup, nb=N_CHIPS, name="grad_w_up", stage=stage)
    collect(slots, got)
    publish("w_up", g_up)
    publish("w_out", wg(merged, dmo, nb=1, name="grad_w_out"))

    def merge_backward(k, acc, extra_refs, out_refs):
        sa_ref, sb_ref, ta_ref, tb_ref = extra_refs
        out_refs[0][...] = (acc * sa_ref[...].astype(F32)).astype(BF16)
        out_refs[1][...] = (acc * sb_ref[...].astype(F32)).astype(BF16)
        out_refs[2][:, 0:d] = (acc * ta_ref[...].astype(F32)).astype(BF16)
        out_refs[2][:, d:2 * d] = (acc * tb_ref[...].astype(F32)).astype(BF16)

    stage, slots = exchange(to_sibling=["w_up", "w_out"])
    (dbr_a, dbr_b, dgab), got = _matmul(dmo, w_out_all, name="d_mix_out", tm=ts, stage=stage, finish=merge_backward,
                                        extra=[(s_a, rows_d), (s_b, rows_d), (t_a, rows_d), (t_b, rows_d)],
                                        outs=[out_rows(d, BF16), out_rows(d, BF16), out_rows(2 * d, BF16)], parts=halves_of_tile, **back_sum)
    collect(slots, got)
    publish("w_a", wg(ga, dbr_a, nb=1, name="grad_w_branch_a"))
    publish("w_b", wg(pooled, dbr_b, nb=1, name="grad_w_branch_b"))
    stage, slots = exchange(to_sibling=["w_a", "w_b"])
    (dga,), got = _matmul(dbr_a, w_a_all, name="d_branch_a", tm=tw, outs=[out_rows(d, F32, tw)], finish=_store_blocks(d), stage=stage, **back_sum)
    collect(slots, got)
    dpooled, = _matmul(dbr_b, w_b_all, name="d_branch_b", tm=tw, outs=[out_rows(d, F32, tw)], finish=_store_blocks(d), **back_sum)
    sum_pairs("w_up", *squares)
    stage, slots = exchange(to_chips=("w_up",) + squares)
    (dproj, sums_seq, d_rg_a, d_rg_x, d_pool), got = _seq_backward(proj, xr, hr, p, dga, dpooled, dgab, conv_full, vecs,
                                                                   w_rg_a_all, w_rg_x_all, w_pool_all, ts=tsq, stage=stage)
    collect(slots, got)
    sum_quarters("w_up", *squares)
    for n, g in zip(block_weights, (d_rg_a, d_rg_x, d_pool)):
        publish(n, g)
    stage, slots = exchange(to_sibling=block_weights, swap=("w_up",) + squares)
    g_in, got = wg(h1, dproj, nb=N_CHIPS, name="grad_w_in", stage=stage)
    collect(slots, got)
    publish("w_in", g_in)
    sum_pairs(*block_weights)
    stage, slots = exchange(to_sibling=["w_in"], to_chips=block_weights)
    collect(slots, _run_stage(stage, "exchange_w_in_gates"))
    sum_pairs("w_in")
    sum_quarters(*block_weights)

    def norm_mix_backward(k, acc, extra_refs, out_refs):
        x_ref, dres_ref, mod_ref, g_ref = extra_refs
        out_refs[0][...] = _modulated_norm_backward(acc, x_ref[...], dres_ref[...], mod_ref, g_ref, out_refs[1], 1)

    stage, slots = exchange(to_chips=["w_in"], swap=block_weights)
    (grad_x, sums_mix), got = _matmul(dproj, w_in_all, name="d_proj_in", tm=ts, stage=stage, finish=norm_mix_backward,
                                      extra=[(x0, rows_d), (dx2, rows_d), (mod, vec8), (g_mix, vec1)],
                                      outs=[out_rows(d, F32), sums_out], zero_first=(1,), parts=halves_of_tile, **back_sum)
    collect(slots, got)
    sum_quarters("w_in")

    dmod = jnp.concatenate([sums_mix[1:2], sums_mix[0:1], sums_mlp[3:4], sums_mlp[1:2], sums_mlp[0:1], sums_head[1:2]], axis=1)
    row = jnp.concatenate([sums_mix[2:3], sums_mlp[2:3], sums_seq[4:5], sums_seq[5:6], sums_seq[6:7], sums_seq[7:8], sums_seq[8:9],
                           sums_seq[9:10], sums_head[0:1], sums_seq[0:CONV_WIDTH].reshape(1, CONV_WIDTH * d), dmod, sums_head[2:3]], axis=1)
    n_vec, n_conv, n_mod = 9 * d, CONV_WIDTH * d, N_MOD * d
    stage, slots = exchange(swap=["w_in"])
    i_rows = _add_allgather8(stage, row)
    got = _run_stage(stage, "exchange_sums_w_in")
    collect(slots, got)
    rows_all = got[i_rows]
    total = _sum_rows(rows_all, d)
    loss = total[0, n_vec + n_conv + n_mod]
    dmod_all = rows_all[:, 0, n_vec + n_conv:n_vec + n_conv + n_mod]
    g_ada, d_ada, m_ada, v_ada = _ada_backward(act_all.T, lax.dynamic_slice_in_dim(dmod_all, k_me * ada_cols, ada_cols, axis=1),
                                               w_ada[0], m_w_ada[0], v_w_ada[0])
    g_conv = lax.dynamic_slice_in_dim(total[:, n_vec:n_vec + n_conv].reshape(CONV_WIDTH, d), k_me * conv_cols, conv_cols, axis=1)
    vec_names = ("norm_mix_g", "norm_mlp_g", "conv_b", "b_rg_a", "b_rg_x", "a_param", "b_pool", "pool_scale", "final_g", "conv_w", "b_ada")
    vec_params = [(norm_mix_g, m_norm_mix_g, v_norm_mix_g), (norm_mlp_g, m_norm_mlp_g, v_norm_mlp_g), (conv_b, m_conv_b, v_conv_b),
                  (b_rg_a, m_b_rg_a, v_b_rg_a), (b_rg_x, m_b_rg_x, v_b_rg_x), (a_param, m_a_param, v_a_param), (b_pool, m_b_pool, v_b_pool),
                  (pool_scale, m_pool_scale, v_pool_scale), tuple(a.reshape(1, d) for a in (final_g, m_final_g, v_final_g)),
                  (conv_w[0], m_conv_w[0], v_conv_w[0]), (b_ada, m_b_ada, v_b_ada)]
    offsets = [i * d for i in range(9)] + [None, n_vec + n_conv]
    vec_res = _adamw_vectors(total, offsets, [None] * 9 + [g_conv, None], vec_params)
    vec_out = {n: [r.reshape(shape) for r in res] for n, res, shape in zip(
        vec_names, vec_res, [(1, d)] * 8 + [(d,), (1, CONV_WIDTH, conv_cols), (1, N_MOD * d)])}

    big_out = {}
    for n in names:
        shape2 = (-1, mine[n].shape[-1])
        g2 = quarter[n].reshape(shape2)
        res = (g2,) + tuple(_adamw(mine[n].reshape(shape2), g2, moments_m[n].reshape(shape2), moments_v[n].reshape(shape2), "adamw_" + n))
        big_out[n] = [r.reshape((1,) + mine[n].shape) for r in res]

    key = {"w_in": "w_in", "w_rg_a": "rg_a", "w_rg_x": "rg_x", "w_branch_a": "w_a", "w_pool": "pool", "w_branch_b": "w_b", "w_out": "w_out",
           "w_up": "w_up", "w_down": "w_down"}
    order = ("norm_mix_g", "norm_mlp_g", "w_ada", "b_ada", "w_in", "conv_w", "conv_b", "w_rg_a", "b_rg_a", "w_rg_x", "b_rg_x", "a_param",
             "w_branch_a", "w_pool", "b_pool", "pool_scale", "w_branch_b", "w_out", "w_up", "w_down", "final_g")
    ada_out = [g_ada[None], d_ada[None], m_ada[None], v_ada[None]]
    outs = [loss, grad_x[None]]
    for kind in range(4):
        for nme in order:
            outs.append(ada_out[kind] if nme == "w_ada" else big_out[key[nme]][kind] if nme in key else vec_out[nme][kind])
    return tuple(outs)
```

```python
import functools

import jax
import jax.numpy as jnp
from jax import lax
from jax.experimental import pallas as pl
from jax.experimental.pallas import tpu as pltpu

F32, BF16 = jnp.float32, jnp.bfloat16
D_MODEL = 1024
D_FF = 4 * D_MODEL
N_BLOCKS = 4
BLOCK = D_MODEL // N_BLOCKS
CONV_WIDTH = 4
POOL_WINDOWS = (2, 4, 8, 16)
CONV_HALO = 8
POOL_HALO = 16
ROW_CHUNK = 16
N_MOD = 6
EPS = 1e-6
C_RG = 8.0
ADAM_LR, ADAM_B1, ADAM_B2, ADAM_EPS, ADAM_WD, ADAM_STEP = 0.001, 0.9, 0.999, 1e-08, 0.01, 10
N_DEV = 8
N_CHIPS = 4
VMEM_LIMIT_V7X = 56 * 2**20
MESH = pl.DeviceIdType.MESH
SDS = jax.ShapeDtypeStruct
HBM_SPEC = pl.BlockSpec(memory_space=pltpu.HBM)
VMEM_SPEC = pl.BlockSpec(memory_space=pltpu.VMEM)


def _params(*semantics):
    return pltpu.CompilerParams(dimension_semantics=semantics, vmem_limit_bytes=VMEM_LIMIT_V7X)


def _coords():
    return lax.axis_index("x"), lax.axis_index("y"), lax.axis_index("c")


def _flip(v, bit):
    return 1 - v if bit else v


def _chips():
    x, y, _ = _coords()
    return [(1 - x, y), (x, 1 - y), (1 - x, 1 - y)]


class _Stage:
    def __init__(self):
        self.inputs, self.in_specs, self.outputs, self.out_specs, self.aliases = [], [], [], [], {}
        self.parts, self.n_copies = [], 0

    def add(self, inputs, in_spec, outputs, out_spec, n_copies, build, alias=False):
        i0, o0 = len(self.inputs), len(self.outputs)
        self.inputs += inputs
        self.in_specs += [in_spec] * len(inputs)
        self.outputs += outputs
        self.out_specs += [out_spec] * len(outputs)
        if alias:
            self.aliases.update({i0 + i: o0 + i for i in range(len(inputs))})
        self.parts.append((build, i0, len(inputs), o0, len(outputs)))
        self.n_copies += n_copies
        return list(range(o0, o0 + len(outputs)))

    def copies(self, in_refs, out_refs):
        out = []
        for build, i0, ni, o0, no in self.parts:
            out += build(in_refs[i0:i0 + ni], out_refs[o0:o0 + no])
        assert len(out) == self.n_copies
        return out

    def run(self, in_refs, out_refs, send_sems, recv_sems, start):
        x, y, c = _coords()
        for s, (src, dst, to, landing) in enumerate(self.copies(in_refs, out_refs)):
            if to is None:
                cp = pltpu.make_async_copy(src, dst, send_sems.at[s])
                cp.start() if start else cp.wait()
                continue
            cp = pltpu.make_async_remote_copy(src_ref=src, dst_ref=dst, send_sem=send_sems.at[s], recv_sem=recv_sems.at[s],
                                              device_id=to, device_id_type=MESH)
            if start:
                cp.start()
            else:
                pltpu.make_async_remote_copy(src_ref=landing, dst_ref=landing, send_sem=send_sems.at[s], recv_sem=recv_sems.at[s],
                                             device_id=(x, y, c), device_id_type=MESH).wait_recv()
                cp.wait_send()


def _hosted(stage, body, *, name, in_specs, out_specs, out_shape, grid=(), scratch_shapes=(), compiler_params=None):
    if stage is None:
        return pl.pallas_call(body, name=name, grid=grid, in_specs=in_specs, out_specs=out_specs, out_shape=out_shape,
                              scratch_shapes=list(scratch_shapes), compiler_params=compiler_params)
    single = not isinstance(out_shape, (list, tuple))
    h_out_shape = [out_shape] if single else list(out_shape)
    h_out_specs = [out_specs] if single else list(out_specs)
    n_in, n_out, n_scr = len(in_specs), len(h_out_shape), len(scratch_shapes)
    s_in, s_out = len(stage.inputs), len(stage.outputs)

    def wrapped(*refs):
        h_in, st_in = refs[:n_in], refs[n_in:n_in + s_in]
        h_o, st_o = refs[n_in + s_in:n_in + s_in + n_out], refs[n_in + s_in + n_out:n_in + s_in + n_out + s_out]
        h_scr = refs[n_in + s_in + n_out + s_out:n_in + s_in + n_out + s_out + n_scr]
        send_sems, recv_sems = refs[n_in + s_in + n_out + s_out + n_scr:]
        if not grid:
            stage.run(st_in, st_o, send_sems, recv_sems, True)
            if body is not None:
                body(*h_in, *h_o, *h_scr)
            stage.run(st_in, st_o, send_sems, recv_sems, False)
            return
        ids = [pl.program_id(a) for a in range(len(grid))]
        first = functools.reduce(jnp.logical_and, [i == 0 for i in ids])
        last = functools.reduce(jnp.logical_and, [i == g - 1 for i, g in zip(ids, grid)])
        pl.when(first)(lambda: stage.run(st_in, st_o, send_sems, recv_sems, True))
        body(*h_in, *h_o, *h_scr)
        pl.when(last)(lambda: stage.run(st_in, st_o, send_sems, recv_sems, False))

    call = pl.pallas_call(
        wrapped, name=name, grid=grid, in_specs=list(in_specs) + stage.in_specs, out_specs=h_out_specs + stage.out_specs,
        out_shape=h_out_shape + stage.outputs, input_output_aliases={n_in + i: n_out + o for i, o in stage.aliases.items()},
        scratch_shapes=list(scratch_shapes) + [pltpu.SemaphoreType.DMA((stage.n_copies,)), pltpu.SemaphoreType.DMA((stage.n_copies,))],
        compiler_params=pltpu.CompilerParams(dimension_semantics=("arbitrary",) * len(grid), vmem_limit_bytes=VMEM_LIMIT_V7X),
    )

    def run(*operands):
        outs = call(*operands, *stage.inputs)
        host = outs[:n_out]
        return (host[0] if single else host), outs[n_out:]

    return run


def _run_stage(stage, name):
    return _hosted(stage, None, name=name, in_specs=[], out_specs=[], out_shape=[])()[1]


def _add_allgather8(stage, v):
    def build(ins, outs):
        x, y, c = _coords()
        me = 4 * x + 2 * y + c
        copies = [(ins[0], outs[0].at[me], None, None)]
        for r in range(1, N_DEV):
            peer = (_flip(x, r & 4), _flip(y, r & 2), _flip(c, r & 1))
            copies.append((ins[0], outs[0].at[me], peer, outs[0].at[me ^ r]))
        return copies

    return stage.add([v], VMEM_SPEC, [SDS((N_DEV,) + v.shape, v.dtype)], VMEM_SPEC, N_DEV, build)[0]


def _add_gather_ici(stage, shard, rows=None, whole=None):
    first, count = rows or (0, shard.shape[1])

    def build(ins, outs):
        x, y, c = _coords()
        k_me = 2 * x + y
        part = pl.ds(first, count)
        copies = [] if whole is not None else [(ins[0], outs[0].at[k_me], (x, y, 1 - c), outs[0].at[k_me])]
        for px, py in _chips():
            copies.append((ins[0].at[c, part], outs[0].at[k_me, c, part], (px, py, c), outs[0].at[2 * px + py, c, part]))
        return copies

    if whole is None:
        return stage.add([shard], HBM_SPEC, [SDS((N_CHIPS,) + shard.shape, shard.dtype)], HBM_SPEC, N_CHIPS, build)[0]

    def build_into(ins, outs):
        return build(ins[1:], outs)

    i0 = len(stage.inputs)
    out = stage.add([whole, shard], HBM_SPEC, [SDS(whole.shape, whole.dtype)], HBM_SPEC, N_CHIPS - 1, build_into)[0]
    stage.aliases[i0] = out
    return out


def _add_gather_d2d(stage, whole):
    def build(ins, outs):
        x, y, c = _coords()
        return [(outs[0].at[2 * px + py, c], outs[0].at[2 * px + py, c], (x, y, 1 - c), outs[0].at[2 * px + py, 1 - c]) for px, py in _chips()]

    return stage.add([whole], HBM_SPEC, [SDS(whole.shape, whole.dtype)], HBM_SPEC, N_CHIPS - 1, build, alias=True)[0]


def _add_reduce_d2d(stage, grads):
    def build(ins, outs):
        x, y, c = _coords()
        return [(ins[0].at[k, 1 - c], outs[0].at[k], (x, y, 1 - c), outs[0].at[k]) for k in range(N_CHIPS)]

    return stage.add([grads], HBM_SPEC, [SDS((N_CHIPS,) + grads.shape[2:], grads.dtype)], HBM_SPEC, N_CHIPS, build)[0]


def _add_reduce_ici(stage, partial):
    def build(ins, outs):
        x, y, c = _coords()
        return [(ins[0].at[2 * px + py], outs[0].at[j], (px, py, c), outs[0].at[j]) for j, (px, py) in enumerate(_chips())]

    return stage.add([partial], HBM_SPEC, [SDS((N_CHIPS - 1,) + partial.shape[1:], partial.dtype)], HBM_SPEC, N_CHIPS - 1, build)[0]


def _add_swap_halves(stage, quarter):
    def build(ins, outs):
        x, y, c = _coords()
        return [(outs[0].at[c], outs[0].at[c], (x, y, 1 - c), outs[0].at[1 - c])]

    return stage.add([quarter], HBM_SPEC, [SDS(quarter.shape, quarter.dtype)], HBM_SPEC, 1, build, alias=True)[0]


def _sum_pair(grads, from_sibling, place, name):
    _, _, rows, cols = grads.shape
    tr = min(rows, 512)

    def body(place_ref, g_ref, s_ref, o_ref):
        o_ref[0] = (g_ref[0, 0].astype(F32) + s_ref[0].astype(F32)).astype(BF16)

    spec = pl.BlockSpec((1, tr, cols), lambda k, i, place_ref: (k, i, 0))
    return pl.pallas_call(
        body, name=name, out_shape=SDS((N_CHIPS, rows, cols), BF16),
        grid_spec=pltpu.PrefetchScalarGridSpec(
            num_scalar_prefetch=1, grid=(N_CHIPS, rows // tr),
            in_specs=[pl.BlockSpec((1, 1, tr, cols), lambda k, i, place_ref: (k, place_ref[0], i, 0)), spec], out_specs=spec),
        compiler_params=_params("parallel", "parallel"),
    )(place, grads, from_sibling)


def _sum_quarter(grads, from_sibling, from_chips, place, name):
    _, _, rows, cols = grads.shape
    tr = min(rows, 512)

    def body(place_ref, g_ref, s_ref, c_ref, o_ref):
        acc = g_ref[0, 0].astype(F32) + s_ref[0].astype(F32)
        for j in range(N_CHIPS - 1):
            acc = acc + c_ref[j].astype(F32)
        o_ref[0] = acc

    return pl.pallas_call(
        body, name=name, out_shape=SDS((2, rows, cols), F32),
        grid_spec=pltpu.PrefetchScalarGridSpec(
            num_scalar_prefetch=1, grid=(rows // tr,),
            in_specs=[pl.BlockSpec((1, 1, tr, cols), lambda i, place_ref: (place_ref[1], place_ref[0], i, 0)),
                      pl.BlockSpec((1, tr, cols), lambda i, place_ref: (place_ref[1], i, 0)),
                      pl.BlockSpec((N_CHIPS - 1, tr, cols), lambda i, place_ref: (0, i, 0))],
            out_specs=pl.BlockSpec((1, tr, cols), lambda i, place_ref: (place_ref[0], i, 0))),
        compiler_params=_params("parallel"),
    )(place, grads, from_sibling, from_chips)


def _row_spec(ts, cols, col_block=0):
    return pl.BlockSpec((ts, cols), lambda *g: (g[0], col_block))


def _vec_spec(rows, cols):
    return pl.BlockSpec((rows, cols), lambda *g: (0, 0))


def _matmul(a, w, *, transposed, reduce_blocks, name, tm, outs, finish, extra=(), prepare=None, zero_first=(), parts=1,
            chunk_prepare=False, chunk_finish=False, stage=None):
    m_rows = a.shape[0]
    nb, r, c = w.shape
    kb, nblk = (c, r) if transposed else (r, c)
    dims = (((1,), (1,)), ((), ())) if transposed else (((1,), (0,)), ((), ()))
    assert a.shape[1] == (nb * kb if reduce_blocks else kb) and m_rows % tm == 0 and tm % parts == 0
    n_extra, n_outs, sub = len(extra), len(outs), tm // parts
    n_acc = 1 if reduce_blocks else nb
    chunk = min(ROW_CHUNK, sub)
    chunk_prepare = chunk_prepare and prepare is not None
    scratch = ([pltpu.VMEM((n_acc, sub, nblk), F32)] if chunk_finish else []) + ([pltpu.VMEM((sub, kb), BF16)] if chunk_prepare else [])

    def body(a_ref, w_ref, *rest):
        extra_refs, out_refs, scr = rest[:n_extra], rest[n_extra:n_extra + n_outs], list(rest[n_extra + n_outs:])
        acc_scr = scr.pop(0) if chunk_finish else None
        lhs_scr = scr.pop(0) if chunk_prepare else None
        if zero_first:
            @pl.when(pl.program_id(0) == 0)
            def _():
                for i in zero_first:
                    out_refs[i][...] = jnp.zeros_like(out_refs[i])

        def views(first, count):
            rows = pl.ds(first, count)
            return [ref.at[rows] if ref.shape[0] == tm else ref for ref in extra_refs], [ref.at[rows] if ref.shape[0] == tm else ref for ref in out_refs]

        def products(rows, lhs):
            if reduce_blocks:
                acc = lax.dot_general(a_ref[rows, 0:kb], w_ref[0], dims, preferred_element_type=F32)
                for k in range(1, nb):
                    acc += lax.dot_general(a_ref[rows, k * kb:(k + 1) * kb], w_ref[k], dims, preferred_element_type=F32)
                return [acc]
            return [lax.dot_general(lhs, w_ref[k], dims, preferred_element_type=F32) for k in range(nb)]

        for j in range(parts):
            rows = pl.ds(j * sub, sub)

            def chunks(fn):
                def step(i, carry):
                    at = pl.multiple_of(i * chunk, chunk)
                    fn(at, *views(j * sub + at, chunk))
                    return carry
                lax.fori_loop(0, sub // chunk, step, 0)

            ex, ou = views(j * sub, sub)
            lhs = None
            if not reduce_blocks:
                if prepare is None:
                    lhs = a_ref[rows, :]
                elif not chunk_prepare:
                    lhs = prepare(a_ref.at[rows], ex, ou)
                else:
                    def prepare_chunk(at, ex_c, ou_c):
                        lhs_scr[pl.ds(at, chunk), :] = prepare(a_ref.at[pl.ds(j * sub + at, chunk)], ex_c, ou_c)
                    chunks(prepare_chunk)
                    lhs = lhs_scr[...]
            if not chunk_finish:
                for k, acc in enumerate(products(rows, lhs)):
                    finish(None if reduce_blocks else k, acc, ex, ou)
                continue
            for k, acc in enumerate(products(rows, lhs)):
                acc_scr[k] = acc

            def finish_chunk(at, ex_c, ou_c):
                for k in range(n_acc):
                    finish(None if reduce_blocks else k, acc_scr[k, pl.ds(at, chunk), :], ex_c, ou_c)
            chunks(finish_chunk)

    return _hosted(
        stage, body, name=name, grid=(m_rows // tm,), out_shape=[s for s, _ in outs], out_specs=[s for _, s in outs],
        in_specs=[_row_spec(tm, a.shape[1]), pl.BlockSpec((nb, r, c), lambda *g: (0, 0, 0), pipeline_mode=pl.Buffered(1))] + [s for _, s in extra],
        scratch_shapes=scratch, compiler_params=_params("arbitrary"),
    )(a, w, *[e for e, _ in extra])


def _store_blocks(n):
    def finish(k, acc, extra_refs, out_refs):
        if k is None:
            out_refs[0][...] = acc.astype(out_refs[0].dtype)
        else:
            out_refs[0][:, k * n:(k + 1) * n] = acc.astype(out_refs[0].dtype)
    return finish


def _weight_grad(a, b, *, nb, name, tk, stage=None):
    s_rows, k1 = a.shape
    tn = b.shape[1] // nb
    assert k1 % tk == 0 and b.shape[1] % nb == 0

    def body(a_ref, b_ref, o_ref):
        o_ref[0] = lax.dot_general(a_ref[...], b_ref[...], (((0,), (0,)), ((), ())), preferred_element_type=F32).astype(o_ref.dtype)

    return _hosted(
        stage, body, name=name, grid=(nb, k1 // tk), out_shape=SDS((nb, k1, tn), BF16),
        in_specs=[pl.BlockSpec((s_rows, tk), lambda n, i: (0, i)), pl.BlockSpec((s_rows, tn), lambda n, i: (0, n))],
        out_specs=pl.BlockSpec((1, tk, tn), lambda n, i: (n, i, 0)), compiler_params=_params("parallel", "parallel"),
    )(a, b)


def _rms(x):
    r = lax.rsqrt(jnp.mean(x * x, axis=-1, keepdims=True) + EPS)
    return x * r, r


def _accumulate(acc_ref, row, value):
    acc_ref[row:row + 1, :] += jnp.sum(value, axis=0, keepdims=True)


def _gelu_parts(y):
    k0, k1 = 0.7978845608028654, 0.044715
    th = jnp.tanh(k0 * (y + k1 * (y * y * y)))
    gelu = 0.5 * y * (1.0 + th)
    dgelu = 0.5 * (1.0 + th) + 0.5 * y * (1.0 - th * th) * (k0 * (1.0 + 3.0 * k1 * (y * y)))
    return gelu, dgelu


def _one_minus_square(a, log_a):
    return -jnp.tanh(log_a) * (1.0 + a * a)


def _modulated_norm(x, mod_ref, g_ref, sc_row, sh_row):
    xh, _ = _rms(x)
    return ((xh * g_ref[...]) * (1.0 + mod_ref[sc_row:sc_row + 1, :]) + mod_ref[sh_row:sh_row + 1, :]).astype(BF16)


def _modulated_norm_backward(dh, x, dres, mod_ref, g_ref, acc_ref, sc_row):
    xh, r = _rms(x)
    _accumulate(acc_ref, 0, dh * (xh * g_ref[...]))
    _accumulate(acc_ref, 1, dh)
    dn = dh * (1.0 + mod_ref[sc_row:sc_row + 1, :])
    _accumulate(acc_ref, 2, dn * xh)
    dxh = dn * g_ref[...]
    return dres + r * (dxh - xh * jnp.mean(dxh * xh, axis=-1, keepdims=True))


def _seq_forward(proj, conv_w, vecs, w_a, w_x, w_p, *, ts, stage=None):
    s_rows = proj.shape[0]
    d = D_MODEL

    def body(x_ref, xh_ref, y_ref, u_ref, uh_ref, cw_ref, vec_ref, wa_ref, wx_ref, wp_ref,
             xr_o, hr_o, ga_o, p_o, pooled_o, carry, a_scr, b_scr):
        t = pl.program_id(0)

        @pl.when(t == 0)
        def _():
            carry[...] = jnp.zeros_like(carry)

        x = x_ref[...]
        halo = jnp.where(t == 0, 0.0, xh_ref[...])
        xx = jnp.concatenate([halo, x], axis=0)
        xr = vec_ref[0:1, :] + x * cw_ref[CONV_WIDTH - 1:CONV_WIDTH, :]
        for j in range(CONV_WIDTH - 1):
            xr = xr + pltpu.roll(xx, CONV_WIDTH - 1 - j, 0)[CONV_HALO:] * cw_ref[j:j + 1, :]
        xr_o[...] = xr
        xrb = xr.astype(BF16)
        zr = jnp.concatenate([jnp.dot(xrb[:, g * BLOCK:(g + 1) * BLOCK], wa_ref[g], preferred_element_type=F32)
                              for g in range(N_BLOCKS)], axis=1) + vec_ref[1:2, :]
        zi = jnp.concatenate([jnp.dot(xrb[:, g * BLOCK:(g + 1) * BLOCK], wx_ref[g], preferred_element_type=F32)
                              for g in range(N_BLOCKS)], axis=1) + vec_ref[2:3, :]
        r = jax.nn.sigmoid(zr)
        gate_i = jax.nn.sigmoid(zi)
        log_a = (-C_RG * r) * jax.nn.softplus(vec_ref[3:4, :])
        rows = lax.broadcasted_iota(jnp.int32, (ts, d), 0)
        a = jnp.exp(log_a)
        mult = jnp.where((rows == 0) & (t == 0), 1.0, jnp.sqrt(_one_minus_square(a, log_a)))
        a_scr[...] = a
        b_scr[...] = xr * gate_i * mult

        sub = lax.broadcasted_iota(jnp.int32, (8, d), 0)

        def chunk(i, h):
            a = a_scr[pl.ds(i * 8, 8), :]
            b = b_scr[pl.ds(i * 8, 8), :]
            for s in (1, 2, 4):
                keep = sub >= s
                b = jnp.where(keep, a * pltpu.roll(b, s, 0) + b, b)
                a = jnp.where(keep, a * pltpu.roll(a, s, 0), a)
            hh = a * h + b
            hr_o[pl.ds(i * 8, 8), :] = hh
            return hh[7:8, :]

        carry[...] = lax.fori_loop(0, ts // 8, chunk, carry[...])
        gelu, _ = _gelu_parts(y_ref[...])
        ga_o[...] = (gelu * hr_o[...]).astype(BF16)

        u = u_ref[...]
        uu = jnp.concatenate([jnp.where(t == 0, 0.0, uh_ref[...]), u], axis=0)
        pos = (rows[:, :BLOCK] + t * ts + 1).astype(F32)
        sums, have, parts = uu, 1, []
        for g, win in enumerate(POOL_WINDOWS):
            while have < win:
                sums = sums + pltpu.roll(sums, have, 0)
                have *= 2
            mean = sums[POOL_HALO:, g * BLOCK:(g + 1) * BLOCK] * (1.0 / jnp.minimum(pos, float(win)))
            parts.append(mean - u[:, g * BLOCK:(g + 1) * BLOCK])
        pb = jnp.concatenate(parts, axis=1).astype(BF16)
        p_o[...] = pb
        mixed = jnp.concatenate([jnp.dot(pb[:, g * BLOCK:(g + 1) * BLOCK], wp_ref[g], preferred_element_type=F32)
                                 for g in range(N_BLOCKS)], axis=1) + vec_ref[4:5, :]
        pooled_o[...] = (mixed * vec_ref[5:6, :]).astype(BF16)

    halo_x = pl.BlockSpec((CONV_HALO, d), lambda t: (jnp.maximum(t * (ts // CONV_HALO) - 1, 0), 0))
    halo_u = pl.BlockSpec((POOL_HALO, d), lambda t: (jnp.maximum(t * (ts // POOL_HALO) - 1, 0), 2))
    wspec = pl.BlockSpec((N_BLOCKS, BLOCK, BLOCK), lambda t: (0, 0, 0))
    return _hosted(
        stage, body, name="seq_forward", grid=(s_rows // ts,),
        out_shape=[SDS((s_rows, d), F32), SDS((s_rows, d), F32), SDS((s_rows, d), BF16), SDS((s_rows, d), BF16), SDS((s_rows, d), BF16)],
        in_specs=[_row_spec(ts, d, 0), halo_x, _row_spec(ts, d, 1), _row_spec(ts, d, 2), halo_u, _vec_spec(CONV_WIDTH, d), _vec_spec(8, d),
                  wspec, wspec, wspec],
        out_specs=[_row_spec(ts, d)] * 5,
        scratch_shapes=[pltpu.VMEM((1, d), F32), pltpu.VMEM((ts, d), F32), pltpu.VMEM((ts, d), F32)],
        compiler_params=_params("arbitrary"),
    )(proj, proj, proj, proj, proj, conv_w, vecs, w_a, w_x, w_p)


def _branches(ga, pooled, proj, w_a, w_b, *, ts, stage=None):
    s_rows, d = ga.shape

    def body(a_ref, p_ref, ga_ref, gb_ref, wa_ref, wb_ref, merged_o, sa_o, sb_o, ta_o, tb_o):
        bra = jnp.dot(a_ref[...], wa_ref[0], preferred_element_type=F32)
        brb = jnp.dot(p_ref[...], wb_ref[0], preferred_element_type=F32)
        sa, sb = jax.nn.sigmoid(ga_ref[...]), jax.nn.sigmoid(gb_ref[...])
        merged_o[...] = (sa * bra + sb * brb).astype(BF16)
        sa_o[...] = sa.astype(BF16)
        sb_o[...] = sb.astype(BF16)
        ta_o[...] = (bra * (sa * (1.0 - sa))).astype(BF16)
        tb_o[...] = (brb * (sb * (1.0 - sb))).astype(BF16)

    rows = _row_spec(ts, d)
    wspec = pl.BlockSpec((1, d, d), lambda *g: (0, 0, 0), pipeline_mode=pl.Buffered(1))
    return _hosted(
        stage, body, name="branches", grid=(s_rows // ts,), out_shape=[SDS((s_rows, d), BF16)] * 5,
        in_specs=[rows, rows, _row_spec(ts, d, 3), _row_spec(ts, d, 4), wspec, wspec], out_specs=[rows] * 5,
        compiler_params=_params("arbitrary"),
    )(ga, pooled, proj, proj, w_a, w_b)


def _seq_backward(proj, xr, hr, p, dga, dpooled, dgab, conv_w, vecs, w_a, w_x, w_p, *, ts, stage=None):
    s_rows = proj.shape[0]
    d = D_MODEL
    n_t = s_rows // ts

    def block_dots(lhs, w_ref, dims):
        return jnp.concatenate([lax.dot_general(lhs[:, g * BLOCK:(g + 1) * BLOCK], w_ref[g], dims, preferred_element_type=F32)
                                for g in range(N_BLOCKS)], axis=1)

    def add_block_grads(dw_ref, lhs, rhs):
        for g in range(N_BLOCKS):
            dw_ref[g] += lax.dot_general(lhs[:, g * BLOCK:(g + 1) * BLOCK], rhs[:, g * BLOCK:(g + 1) * BLOCK],
                                         (((0,), (0,)), ((), ())), preferred_element_type=F32)

    nn, nt = (((1,), (0,)), ((), ())), (((1,), (1,)), ((), ()))

    def body(x_ref, y_ref, xr_ref, hr_ref, hh_ref, p_ref, dga_ref, dpl_ref, dgab_ref, cw_ref, vec_ref, wa_ref, wx_ref, wp_ref,
             dproj_o, acc_o, dwa_o, dwx_o, dwp_o, g_carry, dxr_carry, q_carry, a_scr, b_scr, g_scr):
        step = pl.program_id(0)
        t = n_t - 1 - step

        @pl.when(step == 0)
        def _():
            for ref in (acc_o, dwa_o, dwx_o, dwp_o, g_carry, dxr_carry, q_carry):
                ref[...] = jnp.zeros_like(ref)

        rows = lax.broadcasted_iota(jnp.int32, (ts, d), 0)
        start = (rows == 0) & (t == 0)
        xr = xr_ref[...]
        xrb = xr.astype(BF16)
        r = jax.nn.sigmoid(block_dots(xrb, wa_ref, nn) + vec_ref[1:2, :])
        gate_i = jax.nn.sigmoid(block_dots(xrb, wx_ref, nn) + vec_ref[2:3, :])
        sp = jax.nn.softplus(vec_ref[3:4, :])
        log_a = (-C_RG * r) * sp
        a = jnp.exp(log_a)
        m_square = _one_minus_square(a, log_a)
        mult = jnp.where(start, 1.0, jnp.sqrt(m_square))
        gelu, dgelu = _gelu_parts(y_ref[...])
        dga_t = dga_ref[...]
        hr_t = hr_ref[...]
        dy = dga_t * hr_t * dgelu

        a_scr[...] = jnp.where(rows == ts - 1, 1.0, pltpu.roll(a, ts - 1, 0))
        b_scr[...] = dga_t * gelu
        sub = lax.broadcasted_iota(jnp.int32, (8, d), 0)

        def chunk(i, g_next):
            at = pl.multiple_of((ts // 8 - 1 - i) * 8, 8)
            aa = a_scr[pl.ds(at, 8), :]
            bb = b_scr[pl.ds(at, 8), :]
            for s in (1, 2, 4):
                keep = sub < 8 - s
                bb = jnp.where(keep, bb + aa * pltpu.roll(bb, 8 - s, 0), bb)
                aa = jnp.where(keep, aa * pltpu.roll(aa, 8 - s, 0), aa)
            gg = aa * g_next + bb
            g_scr[pl.ds(at, 8), :] = gg
            return gg[0:1, :]

        g_first = lax.fori_loop(0, ts // 8, chunk, g_carry[...])
        g_carry[...] = a[0:1, :] * g_first
        g = g_scr[...]

        h_before = jnp.where(t == 0, 0.0, hh_ref[CONV_HALO - 1:CONV_HALO, :])
        h_prev = jnp.where(rows == 0, h_before, pltpu.roll(hr_t, 1, 0))
        dxr = g * gate_i * mult
        d_i = g * xr * mult
        d_mult = g * xr * gate_i
        dlog_a = (g * h_prev) * a - jnp.where(start, 0.0, d_mult * (a * a) * lax.rsqrt(m_square))
        dzr = (dlog_a * (-C_RG * sp)) * (r * (1.0 - r))
        dzi = d_i * (gate_i * (1.0 - gate_i))
        _accumulate(acc_o, 5, dzr)
        _accumulate(acc_o, 6, dzi)
        _accumulate(acc_o, 7, dlog_a * (-C_RG * r) * jax.nn.sigmoid(vec_ref[3:4, :]))
        dzrb, dzib = dzr.astype(BF16), dzi.astype(BF16)
        add_block_grads(dwa_o, xrb, dzrb)
        add_block_grads(dwx_o, xrb, dzib)
        dxr = dxr + block_dots(dzrb, wa_ref, nt) + block_dots(dzib, wx_ref, nt)

        x = x_ref[...]
        ext = jnp.concatenate([dxr, dxr_carry[...]], axis=0)
        dx = dxr * cw_ref[CONV_WIDTH - 1:CONV_WIDTH, :]
        _accumulate(acc_o, CONV_WIDTH - 1, x * dxr)
        for j in range(CONV_WIDTH - 1):
            ahead = pltpu.roll(ext, ts + CONV_HALO - (CONV_WIDTH - 1 - j), 0)[:ts]
            dx = dx + ahead * cw_ref[j:j + 1, :]
            _accumulate(acc_o, j, x * ahead)
        _accumulate(acc_o, 4, dxr)
        dxr_carry[...] = dxr[0:CONV_HALO, :]

        pb = p_ref[...]
        mixed = block_dots(pb, wp_ref, nn) + vec_ref[4:5, :]
        dpl = dpl_ref[...]
        _accumulate(acc_o, 9, dpl * mixed)
        dmixed = dpl * vec_ref[5:6, :]
        _accumulate(acc_o, 8, dmixed)
        dmb = dmixed.astype(BF16)
        add_block_grads(dwp_o, pb, dmb)
        dp = block_dots(dmb, wp_ref, nt)
        pos = (rows[:, :BLOCK] + t * ts + 1).astype(F32)
        q = jnp.concatenate([dp[:, g * BLOCK:(g + 1) * BLOCK] * (1.0 / jnp.minimum(pos, float(win))) for g, win in enumerate(POOL_WINDOWS)], axis=1)
        sums, have, parts = jnp.concatenate([q, q_carry[...]], axis=0), 1, []
        for g, win in enumerate(POOL_WINDOWS):
            while have < win:
                sums = sums + pltpu.roll(sums, ts + POOL_HALO - have, 0)
                have *= 2
            parts.append(sums[:ts, g * BLOCK:(g + 1) * BLOCK])
        du = jnp.concatenate(parts, axis=1) - dp
        q_carry[...] = q[0:POOL_HALO, :]

        dproj_o[:, 0:d] = dx.astype(BF16)
        dproj_o[:, d:2 * d] = dy.astype(BF16)
        dproj_o[:, 2 * d:3 * d] = du.astype(BF16)
        dproj_o[:, 3 * d:5 * d] = dgab_ref[...]

    def rev(cols, col_block=0):
        return pl.BlockSpec((ts, cols), lambda i: (n_t - 1 - i, col_block))

    halo_h = pl.BlockSpec((CONV_HALO, d), lambda i: (jnp.maximum((n_t - 1 - i) * (ts // CONV_HALO) - 1, 0), 0))
    wspec = pl.BlockSpec((N_BLOCKS, BLOCK, BLOCK), lambda i: (0, 0, 0))
    return _hosted(
        stage, body, name="seq_backward", grid=(n_t,),
        out_shape=[SDS((s_rows, 5 * d), BF16), SDS((16, d), F32)] + [SDS((N_BLOCKS, BLOCK, BLOCK), F32)] * 3,
        in_specs=[rev(d, 0), rev(d, 1), rev(d), rev(d), halo_h, rev(d), rev(d), rev(d), rev(2 * d), _vec_spec(CONV_WIDTH, d), _vec_spec(8, d),
                  wspec, wspec, wspec],
        out_specs=[rev(5 * d), _vec_spec(16, d), wspec, wspec, wspec],
        scratch_shapes=[pltpu.VMEM((1, d), F32), pltpu.VMEM((CONV_HALO, d), F32), pltpu.VMEM((POOL_HALO, d), F32),
                        pltpu.VMEM((ts, d), F32), pltpu.VMEM((ts, d), F32), pltpu.VMEM((ts, d), F32)],
        compiler_params=_params("arbitrary"),
    )(proj, proj, xr, hr, hr, p, dga, dpooled, dgab, conv_w, vecs, w_a, w_x, w_p)


def _adamw_math(w, g, m, v):
    m = ADAM_B1 * m + (1.0 - ADAM_B1) * g
    v = ADAM_B2 * v + (1.0 - ADAM_B2) * jnp.square(g)
    m_hat = m / (1.0 - ADAM_B1 ** ADAM_STEP)
    v_hat = v / (1.0 - ADAM_B2 ** ADAM_STEP)
    delta = -ADAM_LR * (m_hat / (jnp.sqrt(v_hat) + ADAM_EPS) + ADAM_WD * w)
    return delta, m, v


def _adamw(w, g, m, v, name):
    rows, cols = w.shape
    tr = min(rows, 512)

    def body(w_ref, g_ref, m_ref, v_ref, d_ref, mo_ref, vo_ref):
        d_ref[...], mo_ref[...], vo_ref[...] = _adamw_math(w_ref[...], g_ref[...], m_ref[...], v_ref[...])

    spec = pl.BlockSpec((tr, cols), lambda i: (i, 0))
    return pl.pallas_call(
        body, name=name, grid=(rows // tr,), out_shape=[SDS((rows, cols), F32)] * 3, in_specs=[spec] * 4, out_specs=[spec] * 3,
        compiler_params=_params("parallel"),
    )(w, g, m, v)


def _prologue(c, conv_shard, w_ada, b_cols, w_in_halves, gate_halves):
    n_cols = w_ada.shape[1]
    n_g = len(gate_halves)
    half_rows = w_in_halves.shape[1] // 2
    n_sems = 3 * N_DEV + N_CHIPS * n_g + 5 + 2 + 3

    def body(c_ref, conv_ref, w_ref, b_ref, win_ref, *rest):
        gate_refs = rest[:n_g]
        act_o, mod_o, conv_o, win_o = rest[n_g:n_g + 4]
        gate_o = rest[n_g + 4:2 * n_g + 4]
        c_all, mod_cols, send_sems, recv_sems = rest[2 * n_g + 4:]
        x, y, c = _coords()
        me, k_me = 4 * x + 2 * y + c, 2 * x + y
        sibling, x_nbr, y_nbr = (x, y, 1 - c), (1 - x, y, c), (x, 1 - y, c)
        k_x, k_y, k_d = 2 * (1 - x) + y, 2 * x + (1 - y), 2 * (1 - x) + (1 - y)
        counter = iter(range(n_sems))

        def copy(src, dst, to, landing):
            s = next(counter)
            out = pltpu.make_async_remote_copy(src_ref=src, dst_ref=dst, send_sem=send_sems.at[s], recv_sem=recv_sems.at[s],
                                               device_id=to, device_id_type=MESH)
            arrival = pltpu.make_async_remote_copy(src_ref=landing, dst_ref=landing, send_sem=send_sems.at[s], recv_sem=recv_sems.at[s],
                                                   device_id=(x, y, c), device_id_type=MESH)
            return out, arrival

        def allgather(src, out):
            s = next(counter)
            own = pltpu.make_async_copy(src, out.at[me], send_sems.at[s])
            peers = [copy(src, out.at[me], (_flip(x, r & 4), _flip(y, r & 2), _flip(c, r & 1)), out.at[me ^ r]) for r in range(1, N_DEV)]
            return own, peers

        def start(group):
            own, peers = group
            own.start()
            for out, _ in peers:
                out.start()

        def finish(group):
            own, peers = group
            for out, arrival in peers:
                arrival.wait_recv()
                out.wait_send()
            own.wait()

        rows_a, rows_b = pl.ds(0, half_rows), pl.ds(half_rows, half_rows)
        c_group, conv_group = allgather(c_ref, c_all), allgather(conv_ref, conv_o)
        gates = []
        for g_ref, g_out in zip(gate_refs, gate_o):
            gates.append(copy(g_ref, g_out.at[k_me], sibling, g_out.at[k_me]))
            gates += [copy(g_ref.at[c], g_out.at[k_me, c], (px, py, c), g_out.at[2 * px + py, c]) for px, py in _chips()]
        own_in = copy(win_ref, win_o.at[k_me], sibling, win_o.at[k_me])
        to_x = [copy(win_ref.at[c, rows], win_o.at[k_me, c, rows], x_nbr, win_o.at[k_x, c, rows]) for rows in (rows_a, rows_b)]
        to_y = [copy(win_ref.at[c, rows], win_o.at[k_me, c, rows], y_nbr, win_o.at[k_y, c, rows]) for rows in (rows_b, rows_a)]
        mod_group = allgather(mod_cols, mod_o)
        relay_y = copy(win_o.at[k_x, c, rows_a], win_o.at[k_x, c, rows_a], y_nbr, win_o.at[k_d, c, rows_a])
        relay_x = copy(win_o.at[k_y, c, rows_b], win_o.at[k_y, c, rows_b], x_nbr, win_o.at[k_d, c, rows_b])
        sib_x, sib_y, sib_d = (copy(win_o.at[k, c], win_o.at[k, c], sibling, win_o.at[k, 1 - c]) for k in (k_x, k_y, k_d))

        start(c_group)
        start(conv_group)
        for out, _ in gates + [own_in] + to_x + to_y:
            out.start()
        finish(c_group)
        cv = jnp.concatenate([c_all[r] for r in range(N_DEV)], axis=0)
        act = cv * jax.nn.sigmoid(cv)
        act_o[...] = act
        mod_cols[...] = jnp.dot(act.astype(BF16), w_ref[...].astype(BF16), preferred_element_type=F32) + b_ref[...]
        start(mod_group)
        to_x[0][1].wait_recv()
        relay_y[0].start()
        to_y[0][1].wait_recv()
        relay_x[0].start()
        finish(mod_group)
        finish(conv_group)
        to_x[1][1].wait_recv()
        sib_x[0].start()
        to_y[1][1].wait_recv()
        sib_y[0].start()
        for out, arrival in gates + [own_in]:
            arrival.wait_recv()
            out.wait_send()
        for out, arrival in (relay_y, relay_x):
            arrival.wait_recv()
            out.wait_send()
        sib_d[0].start()
        for out, _ in to_x + to_y:
            out.wait_send()
        for out, arrival in (sib_x, sib_y, sib_d):
            arrival.wait_recv()
            out.wait_send()

    gate_shapes = [SDS((N_CHIPS,) + g.shape, g.dtype) for g in gate_halves]
    return pl.pallas_call(
        body, name="prologue",
        out_shape=[SDS((N_DEV, c.shape[1]), F32), SDS((N_DEV, N_DEV, n_cols), F32), SDS((N_DEV,) + conv_shard.shape, F32),
                   SDS((N_CHIPS,) + w_in_halves.shape, w_in_halves.dtype)] + gate_shapes,
        in_specs=[VMEM_SPEC] * 4 + [HBM_SPEC] * (1 + n_g), out_specs=[VMEM_SPEC] * 3 + [HBM_SPEC] * (1 + n_g),
        scratch_shapes=[pltpu.VMEM((N_DEV,) + c.shape, F32), pltpu.VMEM((N_DEV, n_cols), F32),
                        pltpu.SemaphoreType.DMA((n_sems,)), pltpu.SemaphoreType.DMA((n_sems,))],
        compiler_params=pltpu.CompilerParams(vmem_limit_bytes=VMEM_LIMIT_V7X),
    )(c, conv_shard, w_ada, b_cols, w_in_halves, *gate_halves)


def _ada_backward(act_t, dmod_cols, w, m, v):
    rows, cols = w.shape
    tn = 512

    def body(a_ref, dm_ref, w_ref, m_ref, v_ref, g_ref, d_ref, mo_ref, vo_ref):
        g = jnp.dot(a_ref[...].astype(BF16), dm_ref[...].astype(BF16), preferred_element_type=F32)
        g_ref[...] = g
        d_ref[...], mo_ref[...], vo_ref[...] = _adamw_math(w_ref[...], g, m_ref[...], v_ref[...])

    spec = pl.BlockSpec((rows, tn), lambda j: (0, j))
    return pl.pallas_call(
        body, name="ada_backward", grid=(cols // tn,), out_shape=[SDS((rows, cols), F32)] * 4,
        in_specs=[pl.BlockSpec((rows, N_DEV), lambda j: (0, 0)), pl.BlockSpec((N_DEV, tn), lambda j: (0, j)), spec, spec, spec],
        out_specs=[spec] * 4, compiler_params=_params("parallel"),
    )(act_t, dmod_cols, w, m, v)


def _sum_rows(gathered, n_loss):
    n = gathered.shape[2]

    def body(g_ref, o_ref):
        acc = g_ref[0]
        for r in range(1, N_DEV):
            acc = acc + g_ref[r]
        o_ref[...] = acc
        o_ref[:, n - n_loss:n] = jnp.broadcast_to(jnp.sum(acc[:, n - n_loss:n], axis=1, keepdims=True), (1, n_loss))

    return pl.pallas_call(body, name="sum_rows", out_shape=SDS((1, n), F32), in_specs=[VMEM_SPEC], out_specs=VMEM_SPEC)(gathered)


def _adamw_vectors(total, offsets, separate, params):
    n_p = len(params)
    apart = [g for g in separate if g is not None]

    def body(*refs):
        total_ref, apart_refs = refs[0], list(refs[1:1 + len(apart)])
        ins, outs = refs[1 + len(apart):1 + len(apart) + 3 * n_p], refs[1 + len(apart) + 3 * n_p:]
        for i in range(n_p):
            w_ref, m_ref, v_ref = ins[3 * i:3 * i + 3]
            g = apart_refs.pop(0)[...] if offsets[i] is None else total_ref[:, offsets[i]:offsets[i] + w_ref.shape[1]]
            outs[4 * i][...] = g
            outs[4 * i + 1][...], outs[4 * i + 2][...], outs[4 * i + 3][...] = _adamw_math(w_ref[...], g, m_ref[...], v_ref[...])

    flat = [a for p in params for a in p]
    res = pl.pallas_call(
        body, name="adamw_vectors", out_shape=[SDS(p[0].shape, F32) for p in params for _ in range(4)],
        in_specs=[VMEM_SPEC] * (1 + len(apart) + len(flat)), out_specs=[VMEM_SPEC] * (4 * n_p),
    )(total, *apart, *flat)
    return [tuple(res[4 * i:4 * i + 4]) for i in range(n_p)]


def _blocks_to_pieces(w):
    nb, rows, n = w.shape
    q = rows // N_CHIPS
    return w.reshape(nb, N_CHIPS, q, n).transpose(1, 0, 2, 3).reshape(N_CHIPS, 2, nb // 2, q, n)


def _pieces_to_blocks(w):
    n_chips, _, half, q, n = w.shape
    return w.reshape(n_chips, 2 * half, q, n).transpose(1, 0, 2, 3).reshape(2 * half, n_chips * q, n)


def kernel(x, c, norm_mix_g, norm_mlp_g, w_ada, b_ada, w_in, conv_w, conv_b, w_rg_a, b_rg_a, w_rg_x, b_rg_x, a_param, w_branch_a, w_pool, b_pool, pool_scale, w_branch_b, w_out, w_up, w_down, final_g, loss_target, m_norm_mix_g, m_norm_mlp_g, m_w_ada, m_b_ada, m_w_in, m_conv_w, m_conv_b, m_w_rg_a, m_b_rg_a, m_w_rg_x, m_b_rg_x, m_a_param, m_w_branch_a, m_w_pool, m_b_pool, m_pool_scale, m_w_branch_b, m_w_out, m_w_up, m_w_down, m_final_g, v_norm_mix_g, v_norm_mlp_g, v_w_ada, v_b_ada, v_w_in, v_conv_w, v_conv_b, v_w_rg_a, v_b_rg_a, v_w_rg_x, v_b_rg_x, v_a_param, v_w_branch_a, v_w_pool, v_b_pool, v_pool_scale, v_w_branch_b, v_w_out, v_w_up, v_w_down, v_final_g):
    d = D_MODEL
    s_rows = x.shape[1]
    ts, tsq = min(512, s_rows), min(256, s_rows)
    xi, yi, ci = _coords()
    k_me, me = 2 * xi + yi, 4 * xi + 2 * yi + ci
    ada_cols = w_ada.shape[2]
    conv_cols = conv_w.shape[2]
    n_in, n_up = w_in.shape[2], w_up.shape[2]

    names = ("w_in", "w_up", "w_down", "w_a", "w_b", "w_out", "rg_a", "rg_x", "pool")
    mine = dict(zip(names, (w_in[0], w_up[0], w_down[0], w_branch_a[0], w_branch_b[0], w_out[0], w_rg_a[0], w_rg_x[0], w_pool[0])))
    moments_m = dict(zip(names, (m_w_in[0], m_w_up[0], m_w_down[0], m_w_branch_a[0], m_w_branch_b[0], m_w_out[0], m_w_rg_a[0], m_w_rg_x[0], m_w_pool[0])))
    moments_v = dict(zip(names, (v_w_in[0], v_w_up[0], v_w_down[0], v_w_branch_a[0], v_w_branch_b[0], v_w_out[0], v_w_rg_a[0], v_w_rg_x[0], v_w_pool[0])))
    block_weights, squares = ("rg_a", "rg_x", "pool"), ("w_a", "w_b", "w_out")
    place = jnp.stack([ci, k_me]).astype(jnp.int32)
    g_mix, g_mlp, g_fin = norm_mix_g, norm_mlp_g, final_g.reshape(1, d)
    x0, target = x[0], loss_target[0]
    wg = functools.partial(_weight_grad, tk=min(1024, d))
    plain, back, back_sum = (dict(transposed=t, reduce_blocks=r) for t, r in ((False, False), (True, False), (True, True)))
    rows_d, vec8, vec1 = _row_spec(ts, d), _vec_spec(8, d), _vec_spec(1, d)
    tw = min(1024, s_rows)
    halves_of_tile = 2

    def out_rows(cols, dtype, t=ts):
        return SDS((s_rows, cols), dtype), _row_spec(t, cols)

    sums_out = (SDS((8, d), F32), vec8)

    def halves(n):
        return mine[n].astype(BF16).reshape(2, -1, mine[n].shape[-1])

    def blocks(w):
        return _pieces_to_blocks(w.reshape(N_CHIPS, 2, N_BLOCKS // 2, BLOCK // N_CHIPS, BLOCK))

    act_all, mod_all, conv_all, w_in_whole, *gate_wholes = _prologue(
        c, conv_w[0], w_ada[0], lax.dynamic_slice_in_dim(b_ada, k_me * ada_cols, ada_cols, axis=1), halves("w_in"), [halves(n) for n in block_weights])
    conv_full = conv_all[0::2].transpose(1, 0, 2).reshape(CONV_WIDTH, d)
    mod_mine = lax.dynamic_index_in_dim(mod_all, me, axis=1, keepdims=False)[0::2]
    mod = jnp.concatenate([mod_mine.reshape(N_MOD, d), jnp.zeros((8 - N_MOD, d), F32)], axis=0)
    w_in_all = w_in_whole.reshape(N_CHIPS, d, n_in)
    vecs = jnp.concatenate([conv_b, b_rg_a, b_rg_x, a_param, b_pool, pool_scale, jnp.zeros((2, d), F32)], axis=0)

    def norm_first(a_ref, extra_refs, out_refs):
        h = _modulated_norm(a_ref[...], extra_refs[0], extra_refs[1], 1, 0)
        out_refs[1][...] = h
        return h

    stage = _Stage()
    i_blk = [_add_gather_d2d(stage, g) for g in gate_wholes]
    i_sq = [_add_gather_ici(stage, halves(n)) for n in squares]
    down_rows = mine["w_down"].shape[0] // 8
    i_down = _add_gather_ici(stage, halves("w_down"), rows=(0, down_rows))
    (proj, h1), got = _matmul(x0, w_in_all, name="proj_in", tm=ts, extra=[(mod, vec8), (g_mix, vec1)], prepare=norm_first,
                              outs=[out_rows(N_CHIPS * n_in, F32), out_rows(d, BF16)], finish=_store_blocks(n_in), parts=halves_of_tile, chunk_prepare=True,
                              stage=stage, **plain)
    w_rg_a_all, w_rg_x_all, w_pool_all = (blocks(got[i]) for i in i_blk)
    stage = _Stage()
    i_sq = [_add_gather_d2d(stage, got[i]) for i in i_sq]
    i_up = _add_gather_ici(stage, halves("w_up"))
    i_down = _add_gather_ici(stage, halves("w_down"), rows=(down_rows, down_rows), whole=got[i_down])
    (xr, hr, ga, p, pooled), got = _seq_forward(proj, conv_full, vecs, w_rg_a_all, w_rg_x_all, w_pool_all, ts=tsq, stage=stage)
    w_a_all, w_b_all, w_out_all = (got[i].reshape(1, d, d) for i in i_sq)
    stage = _Stage()
    i_up = _add_gather_d2d(stage, got[i_up])
    i_down = _add_gather_ici(stage, halves("w_down"), rows=(2 * down_rows, 2 * down_rows), whole=got[i_down])
    (merged, s_a, s_b, t_a, t_b), got = _branches(ga, pooled, proj, w_a_all, w_b_all, ts=ts, stage=stage)
    w_up_all = got[i_up].reshape(N_CHIPS, d, n_up)
    stage = _Stage()
    i_down = _add_gather_d2d(stage, got[i_down])

    def residual_norm(k, acc, extra_refs, out_refs):
        x_ref, mod_ref, g_ref = extra_refs
        out_refs[0][...] = acc.astype(BF16)
        x2_t = x_ref[...] + mod_ref[2:3, :] * acc
        out_refs[1][...] = x2_t
        out_refs[2][...] = _modulated_norm(x2_t, mod_ref, g_ref, 4, 3)

    (mo, x2, h2), got = _matmul(merged, w_out_all, name="mix_out", tm=ts, extra=[(x0, rows_d), (mod, vec8), (g_mlp, vec1)],
                                outs=[out_rows(d, BF16), out_rows(d, F32), out_rows(d, BF16)], finish=residual_norm, parts=halves_of_tile, chunk_finish=True,
                                stage=stage, transposed=False, reduce_blocks=True)
    w_down_all = got[i_down].reshape(N_CHIPS, D_FF // N_CHIPS, d)

    def relu_squared(k, acc, extra_refs, out_refs):
        out_refs[0][:, k * n_up:(k + 1) * n_up] = jnp.square(jnp.maximum(acc, 0.0)).astype(BF16)

    ff, = _matmul(h2, w_up_all, name="mlp_up", tm=ts, outs=[out_rows(D_FF, BF16)], finish=relu_squared, parts=halves_of_tile, **plain)

    def loss_head(k, acc, extra_refs, out_refs):
        x2_ref, mod_ref, g_ref, t_ref = extra_refs
        dx_ref, df_ref, acc_ref = out_refs
        xh, r = _rms(x2_ref[...] + mod_ref[5:6, :] * acc)
        err = xh * g_ref[...] - t_ref[...]
        dy = err / d
        dxh = dy * g_ref[...]
        dx3_t = r * (dxh - xh * jnp.mean(dxh * xh, axis=-1, keepdims=True))
        dx_ref[...] = dx3_t
        df_ref[...] = (dx3_t * mod_ref[5:6, :]).astype(BF16)
        _accumulate(acc_ref, 0, dy * xh)
        _accumulate(acc_ref, 1, dx3_t * acc)
        _accumulate(acc_ref, 2, jnp.square(err) * (0.5 / d))

    dx3, dffo, sums_head = _matmul(ff, w_down_all, name="mlp_down", tm=ts, extra=[(x2, rows_d), (mod, vec8), (g_fin, vec1), (target, rows_d)],
                                   outs=[out_rows(d, F32), out_rows(d, BF16), sums_out], finish=loss_head, zero_first=(2,), parts=halves_of_tile, chunk_finish=True,
                                   transposed=False, reduce_blocks=True)

    partial_of, from_sibling, chip_sum, from_chips, half_done, quarter = {}, {}, {}, {}, {}, {}

    def publish(n, g):
        g = _blocks_to_pieces(g).astype(BF16) if n in block_weights else g
        partial_of[n] = g.reshape(N_CHIPS, 2, -1, mine[n].shape[-1])

    def exchange(to_sibling=(), to_chips=(), swap=()):
        st = _Stage()
        slots = [(n, from_sibling, _add_reduce_d2d(st, partial_of[n])) for n in to_sibling]
        slots += [(n, from_chips, _add_reduce_ici(st, chip_sum[n])) for n in to_chips]
        slots += [(n, quarter, _add_swap_halves(st, half_done[n])) for n in swap]
        return st, slots

    def collect(slots, outs):
        for n, where, i in slots:
            where[n] = outs[i]

    def sum_pairs(*ns):
        for n in ns:
            chip_sum[n] = _sum_pair(partial_of[n], from_sibling[n], place, "sum_pair_" + n)

    def sum_quarters(*ns):
        for n in ns:
            half_done[n] = _sum_quarter(partial_of[n], from_sibling[n], from_chips[n], place, "sum_quarter_" + n)

    publish("w_down", wg(ff, dffo, nb=1, name="grad_w_down"))

    def relu_squared_backward(k, acc, extra_refs, out_refs):
        cols = slice(k * n_up, (k + 1) * n_up)
        out_refs[0][:, cols] = (acc * (2.0 * jnp.sqrt(extra_refs[0][:, cols].astype(F32)))).astype(BF16)

    stage, slots = exchange(to_sibling=["w_down"])
    (dup,), got = _matmul(dffo, w_down_all, name="d_mlp_down", tm=ts, extra=[(ff, _row_spec(ts, D_FF))], outs=[out_rows(D_FF, BF16)],
                          finish=relu_squared_backward, parts=halves_of_tile, stage=stage, **back)
    collect(slots, got)
    sum_pairs("w_down")

    def norm_mlp_backward(k, acc, extra_refs, out_refs):
        x2_ref, dres_ref, mod_ref, g_ref, mo_ref = extra_refs
        dx_ref, dmo_ref, acc_ref = out_refs
        dx =_modulated_norm_backward(acc, x2_ref[...], dres_ref[...], mod_ref, g_ref, acc_ref, 4)
        dx_ref[...] = dx
        dmo_ref[...] = (dx * mod_ref[2:3, :]).astype(BF16)
        _accumulate(acc_ref, 3, dx * mo_ref[...].astype(F32))

    stage, slots = exchange(to_chips=["w_down"])
    (dx2, dmo, sums_mlp), got = _matmul(dup, w_up_all, name="d_mlp_up", tm=ts, stage=stage, finish=norm_mlp_backward,
                                        extra=[(x2, rows_d), (dx3, rows_d), (mod, vec8), (g_mlp, vec1), (mo, rows_d)],
                                        outs=[out_rows(d, F32), out_rows(d, BF16), sums_out], zero_first=(2,), parts=halves_of_tile, chunk_finish=True, **back_sum)
    collect(slots, got)
    sum_quarters("w_down")
    stage, slots = exchange(swap=["w_down"])
    g_up, got = wg(h2, dup, nb=N_CHIPS, name="grad_w_up", stage=stage)
    collect(slots, got)
    publish("w_up", g_up)
    publish("w_out", wg(merged, dmo, nb=1, name="grad_w_out"))

    def merge_backward(k, acc, extra_refs, out_refs):
        sa_ref, sb_ref, ta_ref, tb_ref = extra_refs
        out_refs[0][...] = (acc * sa_ref[...].astype(F32)).astype(BF16)
        out_refs[1][...] = (acc * sb_ref[...].astype(F32)).astype(BF16)
        out_refs[2][:, 0:d] = (acc * ta_ref[...].astype(F32)).astype(BF16)
        out_refs[2][:, d:2 * d] = (acc * tb_ref[...].astype(F32)).astype(BF16)

    stage, slots = exchange(to_sibling=["w_up", "w_out"])
    (dbr_a, dbr_b, dgab), got = _matmul(dmo, w_out_all, name="d_mix_out", tm=ts, stage=stage, finish=merge_backward,
                                        extra=[(s_a, rows_d), (s_b, rows_d), (t_a, rows_d), (t_b, rows_d)],
                                        outs=[out_rows(d, BF16), out_rows(d, BF16), out_rows(2 * d, BF16)], parts=halves_of_tile, chunk_finish=True, **back_sum)
    collect(slots, got)
    publish("w_a", wg(ga, dbr_a, nb=1, name="grad_w_branch_a"))
    publish("w_b", wg(pooled, dbr_b, nb=1, name="grad_w_branch_b"))
    stage, slots = exchange(to_sibling=["w_a", "w_b"])
    (dga,), got = _matmul(dbr_a, w_a_all, name="d_branch_a", tm=tw, outs=[out_rows(d, F32, tw)], finish=_store_blocks(d), stage=stage, **back_sum)
    collect(slots, got)
    dpooled, = _matmul(dbr_b, w_b_all, name="d_branch_b", tm=tw, outs=[out_rows(d, F32, tw)], finish=_store_blocks(d), **back_sum)
    sum_pairs("w_up", *squares)
    stage, slots = exchange(to_chips=("w_up",) + squares)
    (dproj, sums_seq, d_rg_a, d_rg_x, d_pool), got = _seq_backward(proj, xr, hr, p, dga, dpooled, dgab, conv_full, vecs,
                                                                   w_rg_a_all, w_rg_x_all, w_pool_all, ts=tsq, stage=stage)
    collect(slots, got)
    sum_quarters("w_up", *squares)
    for n, g in zip(block_weights, (d_rg_a, d_rg_x, d_pool)):
        publish(n, g)
    stage, slots = exchange(to_sibling=block_weights, swap=("w_up",) + squares)
    g_in, got = wg(h1, dproj, nb=N_CHIPS, name="grad_w_in", stage=stage)
    collect(slots, got)
    publish("w_in", g_in)
    sum_pairs(*block_weights)
    stage, slots = exchange(to_sibling=["w_in"], to_chips=block_weights)
    collect(slots, _run_stage(stage, "exchange_w_in_gates"))
    sum_pairs("w_in")
    sum_quarters(*block_weights)

    def norm_mix_backward(k, acc, extra_refs, out_refs):
        x_ref, dres_ref, mod_ref, g_ref = extra_refs
        out_refs[0][...] = _modulated_norm_backward(acc, x_ref[...], dres_ref[...], mod_ref, g_ref, out_refs[1], 1)

    stage, slots = exchange(to_chips=["w_in"], swap=block_weights)
    (grad_x, sums_mix), got = _matmul(dproj, w_in_all, name="d_proj_in", tm=ts, stage=stage, finish=norm_mix_backward,
                                      extra=[(x0, rows_d), (dx2, rows_d), (mod, vec8), (g_mix, vec1)],
                                      outs=[out_rows(d, F32), sums_out], zero_first=(1,), parts=halves_of_tile, chunk_finish=True, **back_sum)
    collect(slots, got)
    sum_quarters("w_in")

    dmod = jnp.concatenate([sums_mix[1:2], sums_mix[0:1], sums_mlp[3:4], sums_mlp[1:2], sums_mlp[0:1], sums_head[1:2]], axis=1)
    row = jnp.concatenate([sums_mix[2:3], sums_mlp[2:3], sums_seq[4:5], sums_seq[5:6], sums_seq[6:7], sums_seq[7:8], sums_seq[8:9],
                           sums_seq[9:10], sums_head[0:1], sums_seq[0:CONV_WIDTH].reshape(1, CONV_WIDTH * d), dmod, sums_head[2:3]], axis=1)
    n_vec, n_conv, n_mod = 9 * d, CONV_WIDTH * d, N_MOD * d
    stage, slots = exchange(swap=["w_in"])
    i_rows = _add_allgather8(stage, row)
    got = _run_stage(stage, "exchange_sums_w_in")
    collect(slots, got)
    rows_all = got[i_rows]
    total = _sum_rows(rows_all, d)
    loss = total[0, n_vec + n_conv + n_mod]
    dmod_all = rows_all[:, 0, n_vec + n_conv:n_vec + n_conv + n_mod]
    g_ada, d_ada, m_ada, v_ada = _ada_backward(act_all.T, lax.dynamic_slice_in_dim(dmod_all, k_me * ada_cols, ada_cols, axis=1),
                                               w_ada[0], m_w_ada[0], v_w_ada[0])
    g_conv = lax.dynamic_slice_in_dim(total[:, n_vec:n_vec + n_conv].reshape(CONV_WIDTH, d), k_me * conv_cols, conv_cols, axis=1)
    vec_names = ("norm_mix_g", "norm_mlp_g", "conv_b", "b_rg_a", "b_rg_x", "a_param", "b_pool", "pool_scale", "final_g", "conv_w", "b_ada")
    vec_params = [(norm_mix_g, m_norm_mix_g, v_norm_mix_g), (norm_mlp_g, m_norm_mlp_g, v_norm_mlp_g), (conv_b, m_conv_b, v_conv_b),
                  (b_rg_a, m_b_rg_a, v_b_rg_a), (b_rg_x, m_b_rg_x, v_b_rg_x), (a_param, m_a_param, v_a_param), (b_pool, m_b_pool, v_b_pool),
                  (pool_scale, m_pool_scale, v_pool_scale), tuple(a.reshape(1, d) for a in (final_g, m_final_g, v_final_g)),
                  (conv_w[0], m_conv_w[0], v_conv_w[0]), (b_ada, m_b_ada, v_b_ada)]
    offsets = [i * d for i in range(9)] + [None, n_vec + n_conv]
    vec_res = _adamw_vectors(total, offsets, [None] * 9 + [g_conv, None], vec_params)
    vec_out = {n: [r.reshape(shape) for r in res] for n, res, shape in zip(
        vec_names, vec_res, [(1, d)] * 8 + [(d,), (1, CONV_WIDTH, conv_cols), (1, N_MOD * d)])}

    big_out = {}
    for n in names:
        shape2 = (-1, mine[n].shape[-1])
        g2 = quarter[n].reshape(shape2)
        res = (g2,) + tuple(_adamw(mine[n].reshape(shape2), g2, moments_m[n].reshape(shape2), moments_v[n].reshape(shape2), "adamw_" + n))
        big_out[n] = [r.reshape((1,) + mine[n].shape) for r in res]

    key = {"w_in": "w_in", "w_rg_a": "rg_a", "w_rg_x": "rg_x", "w_branch_a": "w_a", "w_pool": "pool", "w_branch_b": "w_b", "w_out": "w_out",
           "w_up": "w_up", "w_down": "w_down"}
    order = ("norm_mix_g", "norm_mlp_g", "w_ada", "b_ada", "w_in", "conv_w", "conv_b", "w_rg_a", "b_rg_a", "w_rg_x", "b_rg_x", "a_param",
             "w_branch_a", "w_pool", "b_pool", "pool_scale", "w_branch_b", "w_out", "w_up", "w_down", "final_g")
    ada_out = [g_ada[None], d_ada[None], m_ada[None], v_ada[None]]
    outs = [loss, grad_x[None]]
    for kind in range(4):
        for nme in order:
            outs.append(ada_out[kind] if nme == "w_ada" else big_out[key[nme]][kind] if nme in key else vec_out[nme][kind])
    return tuple(outs)
```

```python
import functools

import jax
import jax.numpy as jnp
from jax import lax
from jax.experimental import pallas as pl
from jax.experimental.pallas import tpu as pltpu

F32, BF16 = jnp.float32, jnp.bfloat16
D_MODEL = 1024
D_FF = 4 * D_MODEL
N_BLOCKS = 4
BLOCK = D_MODEL // N_BLOCKS
CONV_WIDTH = 4
POOL_WINDOWS = (2, 4, 8, 16)
CONV_HALO = 8
POOL_HALO = 16
N_MOD = 6
EPS = 1e-6
C_RG = 8.0
ADAM_LR, ADAM_B1, ADAM_B2, ADAM_EPS, ADAM_WD, ADAM_STEP = 0.001, 0.9, 0.999, 1e-08, 0.01, 10
N_DEV = 8
N_CHIPS = 4
VMEM_LIMIT_V7X = 56 * 2**20
MESH = pl.DeviceIdType.MESH
SDS = jax.ShapeDtypeStruct
HBM_SPEC = pl.BlockSpec(memory_space=pltpu.HBM)
VMEM_SPEC = pl.BlockSpec(memory_space=pltpu.VMEM)


def _params(*semantics):
    return pltpu.CompilerParams(dimension_semantics=semantics, vmem_limit_bytes=VMEM_LIMIT_V7X)


def _coords():
    return lax.axis_index("x"), lax.axis_index("y"), lax.axis_index("c")


def _flip(v, bit):
    return 1 - v if bit else v


def _chips():
    x, y, _ = _coords()
    return [(1 - x, y), (x, 1 - y), (1 - x, 1 - y)]


class _Stage:
    def __init__(self):
        self.inputs, self.in_specs, self.outputs, self.out_specs, self.aliases = [], [], [], [], {}
        self.parts, self.n_copies = [], 0

    def add(self, inputs, in_spec, outputs, out_spec, n_copies, build, alias=False):
        i0, o0 = len(self.inputs), len(self.outputs)
        self.inputs += inputs
        self.in_specs += [in_spec] * len(inputs)
        self.outputs += outputs
        self.out_specs += [out_spec] * len(outputs)
        if alias:
            self.aliases.update({i0 + i: o0 + i for i in range(len(inputs))})
        self.parts.append((build, i0, len(inputs), o0, len(outputs)))
        self.n_copies += n_copies
        return list(range(o0, o0 + len(outputs)))

    def copies(self, in_refs, out_refs):
        out = []
        for build, i0, ni, o0, no in self.parts:
            out += build(in_refs[i0:i0 + ni], out_refs[o0:o0 + no])
        assert len(out) == self.n_copies
        return out

    def run(self, in_refs, out_refs, send_sems, recv_sems, start):
        x, y, c = _coords()
        for s, (src, dst, to, landing) in enumerate(self.copies(in_refs, out_refs)):
            if to is None:
                cp = pltpu.make_async_copy(src, dst, send_sems.at[s])
                cp.start() if start else cp.wait()
                continue
            cp = pltpu.make_async_remote_copy(src_ref=src, dst_ref=dst, send_sem=send_sems.at[s], recv_sem=recv_sems.at[s],
                                              device_id=to, device_id_type=MESH)
            if start:
                cp.start()
            else:
                pltpu.make_async_remote_copy(src_ref=landing, dst_ref=landing, send_sem=send_sems.at[s], recv_sem=recv_sems.at[s],
                                             device_id=(x, y, c), device_id_type=MESH).wait_recv()
                cp.wait_send()


def _hosted(stage, body, *, name, in_specs, out_specs, out_shape, grid=(), scratch_shapes=(), compiler_params=None):
    if stage is None:
        return pl.pallas_call(body, name=name, grid=grid, in_specs=in_specs, out_specs=out_specs, out_shape=out_shape,
                              scratch_shapes=list(scratch_shapes), compiler_params=compiler_params)
    single = not isinstance(out_shape, (list, tuple))
    h_out_shape = [out_shape] if single else list(out_shape)
    h_out_specs = [out_specs] if single else list(out_specs)
    n_in, n_out, n_scr = len(in_specs), len(h_out_shape), len(scratch_shapes)
    s_in, s_out = len(stage.inputs), len(stage.outputs)

    def wrapped(*refs):
        h_in, st_in = refs[:n_in], refs[n_in:n_in + s_in]
        h_o, st_o = refs[n_in + s_in:n_in + s_in + n_out], refs[n_in + s_in + n_out:n_in + s_in + n_out + s_out]
        h_scr = refs[n_in + s_in + n_out + s_out:n_in + s_in + n_out + s_out + n_scr]
        send_sems, recv_sems = refs[n_in + s_in + n_out + s_out + n_scr:]
        if not grid:
            stage.run(st_in, st_o, send_sems, recv_sems, True)
            if body is not None:
                body(*h_in, *h_o, *h_scr)
            stage.run(st_in, st_o, send_sems, recv_sems, False)
            return
        ids = [pl.program_id(a) for a in range(len(grid))]
        first = functools.reduce(jnp.logical_and, [i == 0 for i in ids])
        last = functools.reduce(jnp.logical_and, [i == g - 1 for i, g in zip(ids, grid)])
        pl.when(first)(lambda: stage.run(st_in, st_o, send_sems, recv_sems, True))
        body(*h_in, *h_o, *h_scr)
        pl.when(last)(lambda: stage.run(st_in, st_o, send_sems, recv_sems, False))

    call = pl.pallas_call(
        wrapped, name=name, grid=grid, in_specs=list(in_specs) + stage.in_specs, out_specs=h_out_specs + stage.out_specs,
        out_shape=h_out_shape + stage.outputs, input_output_aliases={n_in + i: n_out + o for i, o in stage.aliases.items()},
        scratch_shapes=list(scratch_shapes) + [pltpu.SemaphoreType.DMA((stage.n_copies,)), pltpu.SemaphoreType.DMA((stage.n_copies,))],
        compiler_params=pltpu.CompilerParams(dimension_semantics=("arbitrary",) * len(grid), vmem_limit_bytes=VMEM_LIMIT_V7X),
    )

    def run(*operands):
        outs = call(*operands, *stage.inputs)
        host = outs[:n_out]
        return (host[0] if single else host), outs[n_out:]

    return run


def _run_stage(stage, name):
    return _hosted(stage, None, name=name, in_specs=[], out_specs=[], out_shape=[])()[1]


def _add_allgather8(stage, v):
    def build(ins, outs):
        x, y, c = _coords()
        me = 4 * x + 2 * y + c
        copies = [(ins[0], outs[0].at[me], None, None)]
        for r in range(1, N_DEV):
            peer = (_flip(x, r & 4), _flip(y, r & 2), _flip(c, r & 1))
            copies.append((ins[0], outs[0].at[me], peer, outs[0].at[me ^ r]))
        return copies

    return stage.add([v], VMEM_SPEC, [SDS((N_DEV,) + v.shape, v.dtype)], VMEM_SPEC, N_DEV, build)[0]


def _add_gather_ici(stage, shard, rows=None, whole=None):
    first, count = rows or (0, shard.shape[1])

    def build(ins, outs):
        x, y, c = _coords()
        k_me = 2 * x + y
        part = pl.ds(first, count)
        copies = [] if whole is not None else [(ins[0], outs[0].at[k_me], (x, y, 1 - c), outs[0].at[k_me])]
        for px, py in _chips():
            copies.append((ins[0].at[c, part], outs[0].at[k_me, c, part], (px, py, c), outs[0].at[2 * px + py, c, part]))
        return copies

    if whole is None:
        return stage.add([shard], HBM_SPEC, [SDS((N_CHIPS,) + shard.shape, shard.dtype)], HBM_SPEC, N_CHIPS, build)[0]

    def build_into(ins, outs):
        return build(ins[1:], outs)

    i0 = len(stage.inputs)
    out = stage.add([whole, shard], HBM_SPEC, [SDS(whole.shape, whole.dtype)], HBM_SPEC, N_CHIPS - 1, build_into)[0]
    stage.aliases[i0] = out
    return out


def _add_gather_d2d(stage, whole):
    def build(ins, outs):
        x, y, c = _coords()
        return [(outs[0].at[2 * px + py, c], outs[0].at[2 * px + py, c], (x, y, 1 - c), outs[0].at[2 * px + py, 1 - c]) for px, py in _chips()]

    return stage.add([whole], HBM_SPEC, [SDS(whole.shape, whole.dtype)], HBM_SPEC, N_CHIPS - 1, build, alias=True)[0]


def _add_reduce_d2d(stage, grads):
    def build(ins, outs):
        x, y, c = _coords()
        return [(ins[0].at[k, 1 - c], outs[0].at[k], (x, y, 1 - c), outs[0].at[k]) for k in range(N_CHIPS)]

    return stage.add([grads], HBM_SPEC, [SDS((N_CHIPS,) + grads.shape[2:], grads.dtype)], HBM_SPEC, N_CHIPS, build)[0]


def _add_reduce_ici(stage, partial):
    def build(ins, outs):
        x, y, c = _coords()
        return [(ins[0].at[2 * px + py], outs[0].at[j], (px, py, c), outs[0].at[j]) for j, (px, py) in enumerate(_chips())]

    return stage.add([partial], HBM_SPEC, [SDS((N_CHIPS - 1,) + partial.shape[1:], partial.dtype)], HBM_SPEC, N_CHIPS - 1, build)[0]


def _add_swap_halves(stage, quarter):
    def build(ins, outs):
        x, y, c = _coords()
        return [(outs[0].at[c], outs[0].at[c], (x, y, 1 - c), outs[0].at[1 - c])]

    return stage.add([quarter], HBM_SPEC, [SDS(quarter.shape, quarter.dtype)], HBM_SPEC, 1, build, alias=True)[0]


def _sum_pair(grads, from_sibling, place, name):
    _, _, rows, cols = grads.shape
    tr = min(rows, 512)

    def body(place_ref, g_ref, s_ref, o_ref):
        o_ref[0] = (g_ref[0, 0].astype(F32) + s_ref[0].astype(F32)).astype(BF16)

    spec = pl.BlockSpec((1, tr, cols), lambda k, i, place_ref: (k, i, 0))
    return pl.pallas_call(
        body, name=name, out_shape=SDS((N_CHIPS, rows, cols), BF16),
        grid_spec=pltpu.PrefetchScalarGridSpec(
            num_scalar_prefetch=1, grid=(N_CHIPS, rows // tr),
            in_specs=[pl.BlockSpec((1, 1, tr, cols), lambda k, i, place_ref: (k, place_ref[0], i, 0)), spec], out_specs=spec),
        compiler_params=_params("parallel", "parallel"),
    )(place, grads, from_sibling)


def _sum_quarter(grads, from_sibling, from_chips, place, name):
    _, _, rows, cols = grads.shape
    tr = min(rows, 512)

    def body(place_ref, g_ref, s_ref, c_ref, o_ref):
        acc = g_ref[0, 0].astype(F32) + s_ref[0].astype(F32)
        for j in range(N_CHIPS - 1):
            acc = acc + c_ref[j].astype(F32)
        o_ref[0] = acc

    return pl.pallas_call(
        body, name=name, out_shape=SDS((2, rows, cols), F32),
        grid_spec=pltpu.PrefetchScalarGridSpec(
            num_scalar_prefetch=1, grid=(rows // tr,),
            in_specs=[pl.BlockSpec((1, 1, tr, cols), lambda i, place_ref: (place_ref[1], place_ref[0], i, 0)),
                      pl.BlockSpec((1, tr, cols), lambda i, place_ref: (place_ref[1], i, 0)),
                      pl.BlockSpec((N_CHIPS - 1, tr, cols), lambda i, place_ref: (0, i, 0))],
            out_specs=pl.BlockSpec((1, tr, cols), lambda i, place_ref: (place_ref[0], i, 0))),
        compiler_params=_params("parallel"),
    )(place, grads, from_sibling, from_chips)


def _row_spec(ts, cols, col_block=0):
    return pl.BlockSpec((ts, cols), lambda *g: (g[0], col_block))


def _vec_spec(rows, cols):
    return pl.BlockSpec((rows, cols), lambda *g: (0, 0))


def _matmul(a, w, *, transposed, reduce_blocks, name, tm, outs, finish, extra=(), prepare=None, zero_first=(), parts=1, stage=None):
    m_rows = a.shape[0]
    nb, r, c = w.shape
    kb = c if transposed else r
    dims = (((1,), (1,)), ((), ())) if transposed else (((1,), (0,)), ((), ()))
    assert a.shape[1] == (nb * kb if reduce_blocks else kb) and m_rows % tm == 0 and tm % parts == 0
    n_extra, sub = len(extra), tm // parts

    def body(a_ref, w_ref, *rest):
        extra_refs, out_refs = rest[:n_extra], rest[n_extra:]
        if zero_first:
            @pl.when(pl.program_id(0) == 0)
            def _():
                for i in zero_first:
                    out_refs[i][...] = jnp.zeros_like(out_refs[i])

        def views(j):
            rows = pl.ds(j * sub, sub)
            return rows, [ref.at[rows] if ref.shape[0] == tm else ref for ref in extra_refs], [ref.at[rows] if ref.shape[0] == tm else ref for ref in out_refs]

        def product(j):
            rows, ex, ou = views(j)
            if reduce_blocks:
                acc = lax.dot_general(a_ref[rows, 0:kb], w_ref[0], dims, preferred_element_type=F32)
                for k in range(1, nb):
                    acc += lax.dot_general(a_ref[rows, k * kb:(k + 1) * kb], w_ref[k], dims, preferred_element_type=F32)
                return [(None, acc)]
            lhs = a_ref[rows, :] if prepare is None else prepare(a_ref.at[rows], ex, ou)
            return [(k, lax.dot_general(lhs, w_ref[k], dims, preferred_element_type=F32)) for k in range(nb)]

        pending = product(0)
        for j in range(parts):
            ahead = product(j + 1) if j + 1 < parts else None
            _, ex, ou = views(j)
            for k, acc in pending:
                finish(k, acc, ex, ou)
            pending = ahead

    return _hosted(
        stage, body, name=name, grid=(m_rows // tm,), out_shape=[s for s, _ in outs], out_specs=[s for _, s in outs],
        in_specs=[_row_spec(tm, a.shape[1]), pl.BlockSpec((nb, r, c), lambda *g: (0, 0, 0), pipeline_mode=pl.Buffered(1))] + [s for _, s in extra],
        compiler_params=_params("arbitrary"),
    )(a, w, *[e for e, _ in extra])


def _store_blocks(n):
    def finish(k, acc, extra_refs, out_refs):
        if k is None:
            out_refs[0][...] = acc.astype(out_refs[0].dtype)
        else:
            out_refs[0][:, k * n:(k + 1) * n] = acc.astype(out_refs[0].dtype)
    return finish


def _weight_grad(a, b, *, nb, name, tk, stage=None):
    s_rows, k1 = a.shape
    tn = b.shape[1] // nb
    assert k1 % tk == 0 and b.shape[1] % nb == 0

    def body(a_ref, b_ref, o_ref):
        o_ref[0] = lax.dot_general(a_ref[...], b_ref[...], (((0,), (0,)), ((), ())), preferred_element_type=F32).astype(o_ref.dtype)

    return _hosted(
        stage, body, name=name, grid=(nb, k1 // tk), out_shape=SDS((nb, k1, tn), BF16),
        in_specs=[pl.BlockSpec((s_rows, tk), lambda n, i: (0, i)), pl.BlockSpec((s_rows, tn), lambda n, i: (0, n))],
        out_specs=pl.BlockSpec((1, tk, tn), lambda n, i: (n, i, 0)), compiler_params=_params("parallel", "parallel"),
    )(a, b)


def _rms(x):
    r = lax.rsqrt(jnp.mean(x * x, axis=-1, keepdims=True) + EPS)
    return x * r, r


def _accumulate(acc_ref, row, value):
    acc_ref[row:row + 1, :] += jnp.sum(value, axis=0, keepdims=True)


def _gelu_parts(y):
    k0, k1 = 0.7978845608028654, 0.044715
    th = jnp.tanh(k0 * (y + k1 * (y * y * y)))
    gelu = 0.5 * y * (1.0 + th)
    dgelu = 0.5 * (1.0 + th) + 0.5 * y * (1.0 - th * th) * (k0 * (1.0 + 3.0 * k1 * (y * y)))
    return gelu, dgelu


def _one_minus_square(a, log_a):
    return -jnp.tanh(log_a) * (1.0 + a * a)


def _modulated_norm(x, mod_ref, g_ref, sc_row, sh_row):
    xh, _ = _rms(x)
    return ((xh * g_ref[...]) * (1.0 + mod_ref[sc_row:sc_row + 1, :]) + mod_ref[sh_row:sh_row + 1, :]).astype(BF16)


def _modulated_norm_backward(dh, x, dres, mod_ref, g_ref, acc_ref, sc_row):
    xh, r = _rms(x)
    _accumulate(acc_ref, 0, dh * (xh * g_ref[...]))
    _accumulate(acc_ref, 1, dh)
    dn = dh * (1.0 + mod_ref[sc_row:sc_row + 1, :])
    _accumulate(acc_ref, 2, dn * xh)
    dxh = dn * g_ref[...]
    return dres + r * (dxh - xh * jnp.mean(dxh * xh, axis=-1, keepdims=True))


def _seq_forward(proj, conv_w, vecs, w_a, w_x, w_p, *, ts, stage=None):
    s_rows = proj.shape[0]
    d = D_MODEL

    def body(x_ref, xh_ref, y_ref, u_ref, uh_ref, cw_ref, vec_ref, wa_ref, wx_ref, wp_ref,
             xr_o, hr_o, ga_o, p_o, pooled_o, carry, a_scr, b_scr):
        t = pl.program_id(0)

        @pl.when(t == 0)
        def _():
            carry[...] = jnp.zeros_like(carry)

        x = x_ref[...]
        halo = jnp.where(t == 0, 0.0, xh_ref[...])
        xx = jnp.concatenate([halo, x], axis=0)
        xr = vec_ref[0:1, :] + x * cw_ref[CONV_WIDTH - 1:CONV_WIDTH, :]
        for j in range(CONV_WIDTH - 1):
            xr = xr + pltpu.roll(xx, CONV_WIDTH - 1 - j, 0)[CONV_HALO:] * cw_ref[j:j + 1, :]
        xr_o[...] = xr
        xrb = xr.astype(BF16)
        zr = jnp.concatenate([jnp.dot(xrb[:, g * BLOCK:(g + 1) * BLOCK], wa_ref[g], preferred_element_type=F32)
                              for g in range(N_BLOCKS)], axis=1) + vec_ref[1:2, :]
        zi = jnp.concatenate([jnp.dot(xrb[:, g * BLOCK:(g + 1) * BLOCK], wx_ref[g], preferred_element_type=F32)
                              for g in range(N_BLOCKS)], axis=1) + vec_ref[2:3, :]
        r = jax.nn.sigmoid(zr)
        gate_i = jax.nn.sigmoid(zi)
        log_a = (-C_RG * r) * jax.nn.softplus(vec_ref[3:4, :])
        rows = lax.broadcasted_iota(jnp.int32, (ts, d), 0)
        a = jnp.exp(log_a)
        mult = jnp.where((rows == 0) & (t == 0), 1.0, jnp.sqrt(_one_minus_square(a, log_a)))
        a_scr[...] = a
        b_scr[...] = xr * gate_i * mult

        sub = lax.broadcasted_iota(jnp.int32, (8, d), 0)

        def chunk(i, h):
            a = a_scr[pl.ds(i * 8, 8), :]
            b = b_scr[pl.ds(i * 8, 8), :]
            for s in (1, 2, 4):
                keep = sub >= s
                b = jnp.where(keep, a * pltpu.roll(b, s, 0) + b, b)
                a = jnp.where(keep, a * pltpu.roll(a, s, 0), a)
            hh = a * h + b
            hr_o[pl.ds(i * 8, 8), :] = hh
            return hh[7:8, :]

        carry[...] = lax.fori_loop(0, ts // 8, chunk, carry[...])
        gelu, _ = _gelu_parts(y_ref[...])
        ga_o[...] = (gelu * hr_o[...]).astype(BF16)

        u = u_ref[...]
        uu = jnp.concatenate([jnp.where(t == 0, 0.0, uh_ref[...]), u], axis=0)
        pos = (rows[:, :BLOCK] + t * ts + 1).astype(F32)
        sums, have, parts = uu, 1, []
        for g, win in enumerate(POOL_WINDOWS):
            while have < win:
                sums = sums + pltpu.roll(sums, have, 0)
                have *= 2
            mean = sums[POOL_HALO:, g * BLOCK:(g + 1) * BLOCK] * (1.0 / jnp.minimum(pos, float(win)))
            parts.append(mean - u[:, g * BLOCK:(g + 1) * BLOCK])
        pb = jnp.concatenate(parts, axis=1).astype(BF16)
        p_o[...] = pb
        mixed = jnp.concatenate([jnp.dot(pb[:, g * BLOCK:(g + 1) * BLOCK], wp_ref[g], preferred_element_type=F32)
                                 for g in range(N_BLOCKS)], axis=1) + vec_ref[4:5, :]
        pooled_o[...] = (mixed * vec_ref[5:6, :]).astype(BF16)

    halo_x = pl.BlockSpec((CONV_HALO, d), lambda t: (jnp.maximum(t * (ts // CONV_HALO) - 1, 0), 0))
    halo_u = pl.BlockSpec((POOL_HALO, d), lambda t: (jnp.maximum(t * (ts // POOL_HALO) - 1, 0), 2))
    wspec = pl.BlockSpec((N_BLOCKS, BLOCK, BLOCK), lambda t: (0, 0, 0))
    return _hosted(
        stage, body, name="seq_forward", grid=(s_rows // ts,),
        out_shape=[SDS((s_rows, d), F32), SDS((s_rows, d), F32), SDS((s_rows, d), BF16), SDS((s_rows, d), BF16), SDS((s_rows, d), BF16)],
        in_specs=[_row_spec(ts, d, 0), halo_x, _row_spec(ts, d, 1), _row_spec(ts, d, 2), halo_u, _vec_spec(CONV_WIDTH, d), _vec_spec(8, d),
                  wspec, wspec, wspec],
        out_specs=[_row_spec(ts, d)] * 5,
        scratch_shapes=[pltpu.VMEM((1, d), F32), pltpu.VMEM((ts, d), F32), pltpu.VMEM((ts, d), F32)],
        compiler_params=_params("arbitrary"),
    )(proj, proj, proj, proj, proj, conv_w, vecs, w_a, w_x, w_p)


def _branches(ga, pooled, proj, w_a, w_b, *, ts, stage=None):
    s_rows, d = ga.shape

    def body(a_ref, p_ref, ga_ref, gb_ref, wa_ref, wb_ref, merged_o, sa_o, sb_o, ta_o, tb_o):
        bra = jnp.dot(a_ref[...], wa_ref[0], preferred_element_type=F32)
        brb = jnp.dot(p_ref[...], wb_ref[0], preferred_element_type=F32)
        sa, sb = jax.nn.sigmoid(ga_ref[...]), jax.nn.sigmoid(gb_ref[...])
        merged_o[...] = (sa * bra + sb * brb).astype(BF16)
        sa_o[...] = sa.astype(BF16)
        sb_o[...] = sb.astype(BF16)
        ta_o[...] = (bra * (sa * (1.0 - sa))).astype(BF16)
        tb_o[...] = (brb * (sb * (1.0 - sb))).astype(BF16)

    rows = _row_spec(ts, d)
    wspec = pl.BlockSpec((1, d, d), lambda *g: (0, 0, 0), pipeline_mode=pl.Buffered(1))
    return _hosted(
        stage, body, name="branches", grid=(s_rows // ts,), out_shape=[SDS((s_rows, d), BF16)] * 5,
        in_specs=[rows, rows, _row_spec(ts, d, 3), _row_spec(ts, d, 4), wspec, wspec], out_specs=[rows] * 5,
        compiler_params=_params("arbitrary"),
    )(ga, pooled, proj, proj, w_a, w_b)


def _seq_backward_by_block(proj, xr, hr, p, dga, dpooled, dgab, conv_w, vecs, w_a, w_x, w_p, *, ts, stage=None):
    s_rows = proj.shape[0]
    d = D_MODEL
    n_t = s_rows // ts
    nn, nt, tn = (((1,), (0,)), ((), ())), (((1,), (1,)), ((), ())), (((0,), (0,)), ((), ()))

    def dot(lhs, rhs, dims):
        return lax.dot_general(lhs, rhs, dims, preferred_element_type=F32)

    def body(x_ref, y_ref, xr_ref, hr_ref, hh_ref, p_ref, dga_ref, dpl_ref, dgab_ref, cw_ref, vec_ref, wa_ref, wx_ref, wp_ref,
             dproj_o, acc_o, dwa_o, dwx_o, dwp_o, g_carry, dxr_carry, q_carry, a_scr, b_scr, g_scr, a_keep, r_keep, i_keep, m_keep):
        step = pl.program_id(0)
        t = n_t - 1 - step

        @pl.when(step == 0)
        def _():
            for ref in (acc_o, dwa_o, dwx_o, dwp_o, g_carry, dxr_carry, q_carry):
                ref[...] = jnp.zeros_like(ref)

        rows = lax.broadcasted_iota(jnp.int32, (ts, BLOCK), 0)
        start = (rows == 0) & (t == 0)

        def add_sum(row, lanes, value):
            acc_o[row:row + 1, lanes] += jnp.sum(value, axis=0, keepdims=True)

        for g in range(N_BLOCKS):
            lanes = slice(g * BLOCK, (g + 1) * BLOCK)
            xrb = xr_ref[:, lanes].astype(BF16)
            r = jax.nn.sigmoid(dot(xrb, wa_ref[g], nn) + vec_ref[1:2, lanes])
            gate_i = jax.nn.sigmoid(dot(xrb, wx_ref[g], nn) + vec_ref[2:3, lanes])
            log_a = (-C_RG * r) * jax.nn.softplus(vec_ref[3:4, lanes])
            a = jnp.exp(log_a)
            a_keep[:, lanes], r_keep[:, lanes], i_keep[:, lanes], m_keep[:, lanes] = a, r, gate_i, _one_minus_square(a, log_a)
            gelu, dgelu = _gelu_parts(y_ref[:, lanes])
            dga_t = dga_ref[:, lanes]
            dproj_o[:, d + g * BLOCK:d + (g + 1) * BLOCK] = (dga_t * hr_ref[:, lanes] * dgelu).astype(BF16)
            a_scr[:, lanes] = jnp.where(rows == ts - 1, 1.0, pltpu.roll(a, ts - 1, 0))
            b_scr[:, lanes] = dga_t * gelu

        sub = lax.broadcasted_iota(jnp.int32, (8, d), 0)

        def chunk(i, g_next):
            at = pl.multiple_of((ts // 8 - 1 - i) * 8, 8)
            aa = a_scr[pl.ds(at, 8), :]
            bb = b_scr[pl.ds(at, 8), :]
            for s in (1, 2, 4):
                keep = sub < 8 - s
                bb = jnp.where(keep, bb + aa * pltpu.roll(bb, 8 - s, 0), bb)
                aa = jnp.where(keep, aa * pltpu.roll(aa, 8 - s, 0), aa)
            gg = aa * g_next + bb
            g_scr[pl.ds(at, 8), :] = gg
            return gg[0:1, :]

        g_first = lax.fori_loop(0, ts // 8, chunk, g_carry[...])
        g_carry[...] = a_keep[0:1, :] * g_first

        pos = (rows + t * ts + 1).astype(F32)
        for g, win in enumerate(POOL_WINDOWS):
            lanes = slice(g * BLOCK, (g + 1) * BLOCK)
            gs, xr, a, r, gate_i, m_square = g_scr[:, lanes], xr_ref[:, lanes], a_keep[:, lanes], r_keep[:, lanes], i_keep[:, lanes], m_keep[:, lanes]
            sp = jax.nn.softplus(vec_ref[3:4, lanes])
            mult = jnp.where(start, 1.0, jnp.sqrt(m_square))
            h_before = jnp.where(t == 0, 0.0, hh_ref[CONV_HALO - 1:CONV_HALO, lanes])
            h_prev = jnp.where(rows == 0, h_before, pltpu.roll(hr_ref[:, lanes], 1, 0))
            dxr = gs * gate_i * mult
            d_i = gs * xr * mult
            d_mult = gs * xr * gate_i
            dlog_a = (gs * h_prev) * a - jnp.where(start, 0.0, d_mult * (a * a) * lax.rsqrt(m_square))
            dzr = (dlog_a * (-C_RG * sp)) * (r * (1.0 - r))
            dzi = d_i * (gate_i * (1.0 - gate_i))
            add_sum(5, lanes, dzr)
            add_sum(6, lanes, dzi)
            add_sum(7, lanes, dlog_a * (-C_RG * r) * jax.nn.sigmoid(vec_ref[3:4, lanes]))
            xrb, dzrb, dzib = xr.astype(BF16), dzr.astype(BF16), dzi.astype(BF16)
            dwa_o[g] += dot(xrb, dzrb, tn)
            dwx_o[g] += dot(xrb, dzib, tn)
            dxr = dxr + dot(dzrb, wa_ref[g], nt) + dot(dzib, wx_ref[g], nt)

            x = x_ref[:, lanes]
            ext = jnp.concatenate([dxr, dxr_carry[:, lanes]], axis=0)
            dx = dxr * cw_ref[CONV_WIDTH - 1:CONV_WIDTH, lanes]
            add_sum(CONV_WIDTH - 1, lanes, x * dxr)
            for j in range(CONV_WIDTH - 1):
                ahead = pltpu.roll(ext, ts + CONV_HALO - (CONV_WIDTH - 1 - j), 0)[:ts]
                dx = dx + ahead * cw_ref[j:j + 1, lanes]
                add_sum(j, lanes, x * ahead)
            add_sum(4, lanes, dxr)
            dxr_carry[:, lanes] = dxr[0:CONV_HALO, :]
            dproj_o[:, lanes] = dx.astype(BF16)

            pb = p_ref[:, lanes]
            mixed = dot(pb, wp_ref[g], nn) + vec_ref[4:5, lanes]
            dpl = dpl_ref[:, lanes]
            add_sum(9, lanes, dpl * mixed)
            dmixed = dpl * vec_ref[5:6, lanes]
            add_sum(8, lanes, dmixed)
            dmb = dmixed.astype(BF16)
            dwp_o[g] += dot(pb, dmb, tn)
            dp = dot(dmb, wp_ref[g], nt)
            q = dp * (1.0 / jnp.minimum(pos, float(win)))
            sums, have = jnp.concatenate([q, q_carry[:, lanes]], axis=0), 1
            while have < win:
                sums = sums + pltpu.roll(sums, ts + POOL_HALO - have, 0)
                have *= 2
            q_carry[:, lanes] = q[0:POOL_HALO, :]
            dproj_o[:, 2 * d + g * BLOCK:2 * d + (g + 1) * BLOCK] = (sums[:ts] - dp).astype(BF16)

        dproj_o[:, 3 * d:5 * d] = dgab_ref[...]

    def rev(cols, col_block=0):
        return pl.BlockSpec((ts, cols), lambda i: (n_t - 1 - i, col_block))

    halo_h = pl.BlockSpec((CONV_HALO, d), lambda i: (jnp.maximum((n_t - 1 - i) * (ts // CONV_HALO) - 1, 0), 0))
    wspec = pl.BlockSpec((N_BLOCKS, BLOCK, BLOCK), lambda i: (0, 0, 0))
    return _hosted(
        stage, body, name="seq_backward", grid=(n_t,),
        out_shape=[SDS((s_rows, 5 * d), BF16), SDS((16, d), F32)] + [SDS((N_BLOCKS, BLOCK, BLOCK), F32)] * 3,
        in_specs=[rev(d, 0), rev(d, 1), rev(d), rev(d), halo_h, rev(d), rev(d), rev(d), rev(2 * d), _vec_spec(CONV_WIDTH, d), _vec_spec(8, d),
                  wspec, wspec, wspec],
        out_specs=[rev(5 * d), _vec_spec(16, d), wspec, wspec, wspec],
        scratch_shapes=[pltpu.VMEM((1, d), F32), pltpu.VMEM((CONV_HALO, d), F32), pltpu.VMEM((POOL_HALO, d), F32)] + [pltpu.VMEM((ts, d), F32)] * 7,
        compiler_params=_params("arbitrary"),
    )(proj, proj, xr, hr, hr, p, dga, dpooled, dgab, conv_w, vecs, w_a, w_x, w_p)


def _seq_backward(proj, xr, hr, p, dga, dpooled, dgab, conv_w, vecs, w_a, w_x, w_p, *, ts, stage=None):
    s_rows = proj.shape[0]
    d = D_MODEL
    n_t = s_rows // ts

    def block_dots(lhs, w_ref, dims):
        return jnp.concatenate([lax.dot_general(lhs[:, g * BLOCK:(g + 1) * BLOCK], w_ref[g], dims, preferred_element_type=F32)
                                for g in range(N_BLOCKS)], axis=1)

    def add_block_grads(dw_ref, lhs, rhs):
        for g in range(N_BLOCKS):
            dw_ref[g] += lax.dot_general(lhs[:, g * BLOCK:(g + 1) * BLOCK], rhs[:, g * BLOCK:(g + 1) * BLOCK],
                                         (((0,), (0,)), ((), ())), preferred_element_type=F32)

    nn, nt = (((1,), (0,)), ((), ())), (((1,), (1,)), ((), ()))

    def body(x_ref, y_ref, xr_ref, hr_ref, hh_ref, p_ref, dga_ref, dpl_ref, dgab_ref, cw_ref, vec_ref, wa_ref, wx_ref, wp_ref,
             dproj_o, acc_o, dwa_o, dwx_o, dwp_o, g_carry, dxr_carry, q_carry, a_scr, b_scr, g_scr):
        step = pl.program_id(0)
        t = n_t - 1 - step

        @pl.when(step == 0)
        def _():
            for ref in (acc_o, dwa_o, dwx_o, dwp_o, g_carry, dxr_carry, q_carry):
                ref[...] = jnp.zeros_like(ref)

        rows = lax.broadcasted_iota(jnp.int32, (ts, d), 0)
        start = (rows == 0) & (t == 0)
        xr = xr_ref[...]
        xrb = xr.astype(BF16)
        r = jax.nn.sigmoid(block_dots(xrb, wa_ref, nn) + vec_ref[1:2, :])
        gate_i = jax.nn.sigmoid(block_dots(xrb, wx_ref, nn) + vec_ref[2:3, :])
        sp = jax.nn.softplus(vec_ref[3:4, :])
        log_a = (-C_RG * r) * sp
        a = jnp.exp(log_a)
        m_square = _one_minus_square(a, log_a)
        mult = jnp.where(start, 1.0, jnp.sqrt(m_square))
        gelu, dgelu = _gelu_parts(y_ref[...])
        dga_t = dga_ref[...]
        hr_t = hr_ref[...]
        dy = dga_t * hr_t * dgelu

        a_scr[...] = jnp.where(rows == ts - 1, 1.0, pltpu.roll(a, ts - 1, 0))
        b_scr[...] = dga_t * gelu
        sub = lax.broadcasted_iota(jnp.int32, (8, d), 0)

        def chunk(i, g_next):
            at = pl.multiple_of((ts // 8 - 1 - i) * 8, 8)
            aa = a_scr[pl.ds(at, 8), :]
            bb = b_scr[pl.ds(at, 8), :]
            for s in (1, 2, 4):
                keep = sub < 8 - s
                bb = jnp.where(keep, bb + aa * pltpu.roll(bb, 8 - s, 0), bb)
                aa = jnp.where(keep, aa * pltpu.roll(aa, 8 - s, 0), aa)
            gg = aa * g_next + bb
            g_scr[pl.ds(at, 8), :] = gg
            return gg[0:1, :]

        g_first = lax.fori_loop(0, ts // 8, chunk, g_carry[...])
        g_carry[...] = a[0:1, :] * g_first
        g = g_scr[...]

        h_before = jnp.where(t == 0, 0.0, hh_ref[CONV_HALO - 1:CONV_HALO, :])
        h_prev = jnp.where(rows == 0, h_before, pltpu.roll(hr_t, 1, 0))
        dxr = g * gate_i * mult
        d_i = g * xr * mult
        d_mult = g * xr * gate_i
        dlog_a = (g * h_prev) * a - jnp.where(start, 0.0, d_mult * (a * a) * lax.rsqrt(m_square))
        dzr = (dlog_a * (-C_RG * sp)) * (r * (1.0 - r))
        dzi = d_i * (gate_i * (1.0 - gate_i))
        _accumulate(acc_o, 5, dzr)
        _accumulate(acc_o, 6, dzi)
        _accumulate(acc_o, 7, dlog_a * (-C_RG * r) * jax.nn.sigmoid(vec_ref[3:4, :]))
        dzrb, dzib = dzr.astype(BF16), dzi.astype(BF16)
        add_block_grads(dwa_o, xrb, dzrb)
        add_block_grads(dwx_o, xrb, dzib)
        dxr = dxr + block_dots(dzrb, wa_ref, nt) + block_dots(dzib, wx_ref, nt)

        x = x_ref[...]
        ext = jnp.concatenate([dxr, dxr_carry[...]], axis=0)
        dx = dxr * cw_ref[CONV_WIDTH - 1:CONV_WIDTH, :]
        _accumulate(acc_o, CONV_WIDTH - 1, x * dxr)
        for j in range(CONV_WIDTH - 1):
            ahead = pltpu.roll(ext, ts + CONV_HALO - (CONV_WIDTH - 1 - j), 0)[:ts]
            dx = dx + ahead * cw_ref[j:j + 1, :]
            _accumulate(acc_o, j, x * ahead)
        _accumulate(acc_o, 4, dxr)
        dxr_carry[...] = dxr[0:CONV_HALO, :]

        pb = p_ref[...]
        mixed = block_dots(pb, wp_ref, nn) + vec_ref[4:5, :]
        dpl = dpl_ref[...]
        _accumulate(acc_o, 9, dpl * mixed)
        dmixed = dpl * vec_ref[5:6, :]
        _accumulate(acc_o, 8, dmixed)
        dmb = dmixed.astype(BF16)
        add_block_grads(dwp_o, pb, dmb)
        dp = block_dots(dmb, wp_ref, nt)
        pos = (rows[:, :BLOCK] + t * ts + 1).astype(F32)
        q = jnp.concatenate([dp[:, g * BLOCK:(g + 1) * BLOCK] * (1.0 / jnp.minimum(pos, float(win))) for g, win in enumerate(POOL_WINDOWS)], axis=1)
        sums, have, parts = jnp.concatenate([q, q_carry[...]], axis=0), 1, []
        for g, win in enumerate(POOL_WINDOWS):
            while have < win:
                sums = sums + pltpu.roll(sums, ts + POOL_HALO - have, 0)
                have *= 2
            parts.append(sums[:ts, g * BLOCK:(g + 1) * BLOCK])
        du = jnp.concatenate(parts, axis=1) - dp
        q_carry[...] = q[0:POOL_HALO, :]

        dproj_o[:, 0:d] = dx.astype(BF16)
        dproj_o[:, d:2 * d] = dy.astype(BF16)
        dproj_o[:, 2 * d:3 * d] = du.astype(BF16)
        dproj_o[:, 3 * d:5 * d] = dgab_ref[...]

    def rev(cols, col_block=0):
        return pl.BlockSpec((ts, cols), lambda i: (n_t - 1 - i, col_block))

    halo_h = pl.BlockSpec((CONV_HALO, d), lambda i: (jnp.maximum((n_t - 1 - i) * (ts // CONV_HALO) - 1, 0), 0))
    wspec = pl.BlockSpec((N_BLOCKS, BLOCK, BLOCK), lambda i: (0, 0, 0))
    return _hosted(
        stage, body, name="seq_backward", grid=(n_t,),
        out_shape=[SDS((s_rows, 5 * d), BF16), SDS((16, d), F32)] + [SDS((N_BLOCKS, BLOCK, BLOCK), F32)] * 3,
        in_specs=[rev(d, 0), rev(d, 1), rev(d), rev(d), halo_h, rev(d), rev(d), rev(d), rev(2 * d), _vec_spec(CONV_WIDTH, d), _vec_spec(8, d),
                  wspec, wspec, wspec],
        out_specs=[rev(5 * d), _vec_spec(16, d), wspec, wspec, wspec],
        scratch_shapes=[pltpu.VMEM((1, d), F32), pltpu.VMEM((CONV_HALO, d), F32), pltpu.VMEM((POOL_HALO, d), F32),
                        pltpu.VMEM((ts, d), F32), pltpu.VMEM((ts, d), F32), pltpu.VMEM((ts, d), F32)],
        compiler_params=_params("arbitrary"),
    )(proj, proj, xr, hr, hr, p, dga, dpooled, dgab, conv_w, vecs, w_a, w_x, w_p)


def _adamw_math(w, g, m, v):
    m = ADAM_B1 * m + (1.0 - ADAM_B1) * g
    v = ADAM_B2 * v + (1.0 - ADAM_B2) * jnp.square(g)
    m_hat = m / (1.0 - ADAM_B1 ** ADAM_STEP)
    v_hat = v / (1.0 - ADAM_B2 ** ADAM_STEP)
    delta = -ADAM_LR * (m_hat / (jnp.sqrt(v_hat) + ADAM_EPS) + ADAM_WD * w)
    return delta, m, v


def _adamw(w, g, m, v, name):
    rows, cols = w.shape
    tr = min(rows, 512)

    def body(w_ref, g_ref, m_ref, v_ref, d_ref, mo_ref, vo_ref):
        d_ref[...], mo_ref[...], vo_ref[...] = _adamw_math(w_ref[...], g_ref[...], m_ref[...], v_ref[...])

    spec = pl.BlockSpec((tr, cols), lambda i: (i, 0))
    return pl.pallas_call(
        body, name=name, grid=(rows // tr,), out_shape=[SDS((rows, cols), F32)] * 3, in_specs=[spec] * 4, out_specs=[spec] * 3,
        compiler_params=_params("parallel"),
    )(w, g, m, v)


def _prologue(c, conv_shard, w_ada, b_cols, w_in_halves, gate_halves):
    n_cols = w_ada.shape[1]
    n_g = len(gate_halves)
    half_rows = w_in_halves.shape[1] // 2
    n_sems = 3 * N_DEV + N_CHIPS * n_g + 5 + 2 + 3

    def body(c_ref, conv_ref, w_ref, b_ref, win_ref, *rest):
        gate_refs = rest[:n_g]
        act_o, mod_o, conv_o, win_o = rest[n_g:n_g + 4]
        gate_o = rest[n_g + 4:2 * n_g + 4]
        c_all, mod_cols, send_sems, recv_sems = rest[2 * n_g + 4:]
        x, y, c = _coords()
        me, k_me = 4 * x + 2 * y + c, 2 * x + y
        sibling, x_nbr, y_nbr = (x, y, 1 - c), (1 - x, y, c), (x, 1 - y, c)
        k_x, k_y, k_d = 2 * (1 - x) + y, 2 * x + (1 - y), 2 * (1 - x) + (1 - y)
        counter = iter(range(n_sems))

        def copy(src, dst, to, landing):
            s = next(counter)
            out = pltpu.make_async_remote_copy(src_ref=src, dst_ref=dst, send_sem=send_sems.at[s], recv_sem=recv_sems.at[s],
                                               device_id=to, device_id_type=MESH)
            arrival = pltpu.make_async_remote_copy(src_ref=landing, dst_ref=landing, send_sem=send_sems.at[s], recv_sem=recv_sems.at[s],
                                                   device_id=(x, y, c), device_id_type=MESH)
            return out, arrival

        def allgather(src, out):
            s = next(counter)
            own = pltpu.make_async_copy(src, out.at[me], send_sems.at[s])
            peers = [copy(src, out.at[me], (_flip(x, r & 4), _flip(y, r & 2), _flip(c, r & 1)), out.at[me ^ r]) for r in range(1, N_DEV)]
            return own, peers

        def start(group):
            own, peers = group
            own.start()
            for out, _ in peers:
                out.start()

        def finish(group):
            own, peers = group
            for out, arrival in peers:
                arrival.wait_recv()
                out.wait_send()
            own.wait()

        rows_a, rows_b = pl.ds(0, half_rows), pl.ds(half_rows, half_rows)
        c_group, conv_group = allgather(c_ref, c_all), allgather(conv_ref, conv_o)
        gates = []
        for g_ref, g_out in zip(gate_refs, gate_o):
            gates.append(copy(g_ref, g_out.at[k_me], sibling, g_out.at[k_me]))
            gates += [copy(g_ref.at[c], g_out.at[k_me, c], (px, py, c), g_out.at[2 * px + py, c]) for px, py in _chips()]
        own_in = copy(win_ref, win_o.at[k_me], sibling, win_o.at[k_me])
        to_x = [copy(win_ref.at[c, rows], win_o.at[k_me, c, rows], x_nbr, win_o.at[k_x, c, rows]) for rows in (rows_a, rows_b)]
        to_y = [copy(win_ref.at[c, rows], win_o.at[k_me, c, rows], y_nbr, win_o.at[k_y, c, rows]) for rows in (rows_b, rows_a)]
        mod_group = allgather(mod_cols, mod_o)
        relay_y = copy(win_o.at[k_x, c, rows_a], win_o.at[k_x, c, rows_a], y_nbr, win_o.at[k_d, c, rows_a])
        relay_x = copy(win_o.at[k_y, c, rows_b], win_o.at[k_y, c, rows_b], x_nbr, win_o.at[k_d, c, rows_b])
        sib_x, sib_y, sib_d = (copy(win_o.at[k, c], win_o.at[k, c], sibling, win_o.at[k, 1 - c]) for k in (k_x, k_y, k_d))

        start(c_group)
        start(conv_group)
        for out, _ in gates + [own_in] + to_x + to_y:
            out.start()
        finish(c_group)
        cv = jnp.concatenate([c_all[r] for r in range(N_DEV)], axis=0)
        act = cv * jax.nn.sigmoid(cv)
        act_o[...] = act
        mod_cols[...] = jnp.dot(act.astype(BF16), w_ref[...].astype(BF16), preferred_element_type=F32) + b_ref[...]
        start(mod_group)
        to_x[0][1].wait_recv()
        relay_y[0].start()
        to_y[0][1].wait_recv()
        relay_x[0].start()
        finish(mod_group)
        finish(conv_group)
        to_x[1][1].wait_recv()
        sib_x[0].start()
        to_y[1][1].wait_recv()
        sib_y[0].start()
        for out, arrival in gates + [own_in]:
            arrival.wait_recv()
            out.wait_send()
        for out, arrival in (relay_y, relay_x):
            arrival.wait_recv()
            out.wait_send()
        sib_d[0].start()
        for out, _ in to_x + to_y:
            out.wait_send()
        for out, arrival in (sib_x, sib_y, sib_d):
            arrival.wait_recv()
            out.wait_send()

    gate_shapes = [SDS((N_CHIPS,) + g.shape, g.dtype) for g in gate_halves]
    return pl.pallas_call(
        body, name="prologue",
        out_shape=[SDS((N_DEV, c.shape[1]), F32), SDS((N_DEV, N_DEV, n_cols), F32), SDS((N_DEV,) + conv_shard.shape, F32),
                   SDS((N_CHIPS,) + w_in_halves.shape, w_in_halves.dtype)] + gate_shapes,
        in_specs=[VMEM_SPEC] * 4 + [HBM_SPEC] * (1 + n_g), out_specs=[VMEM_SPEC] * 3 + [HBM_SPEC] * (1 + n_g),
        scratch_shapes=[pltpu.VMEM((N_DEV,) + c.shape, F32), pltpu.VMEM((N_DEV, n_cols), F32),
                        pltpu.SemaphoreType.DMA((n_sems,)), pltpu.SemaphoreType.DMA((n_sems,))],
        compiler_params=pltpu.CompilerParams(vmem_limit_bytes=VMEM_LIMIT_V7X),
    )(c, conv_shard, w_ada, b_cols, w_in_halves, *gate_halves)


def _ada_backward(act_t, dmod_cols, w, m, v):
    rows, cols = w.shape
    tn = 512

    def body(a_ref, dm_ref, w_ref, m_ref, v_ref, g_ref, d_ref, mo_ref, vo_ref):
        g = jnp.dot(a_ref[...].astype(BF16), dm_ref[...].astype(BF16), preferred_element_type=F32)
        g_ref[...] = g
        d_ref[...], mo_ref[...], vo_ref[...] = _adamw_math(w_ref[...], g, m_ref[...], v_ref[...])

    spec = pl.BlockSpec((rows, tn), lambda j: (0, j))
    return pl.pallas_call(
        body, name="ada_backward", grid=(cols // tn,), out_shape=[SDS((rows, cols), F32)] * 4,
        in_specs=[pl.BlockSpec((rows, N_DEV), lambda j: (0, 0)), pl.BlockSpec((N_DEV, tn), lambda j: (0, j)), spec, spec, spec],
        out_specs=[spec] * 4, compiler_params=_params("parallel"),
    )(act_t, dmod_cols, w, m, v)


def _sum_rows(gathered, n_loss):
    n = gathered.shape[2]

    def body(g_ref, o_ref):
        acc = g_ref[0]
        for r in range(1, N_DEV):
            acc = acc + g_ref[r]
        o_ref[...] = acc
        o_ref[:, n - n_loss:n] = jnp.broadcast_to(jnp.sum(acc[:, n - n_loss:n], axis=1, keepdims=True), (1, n_loss))

    return pl.pallas_call(body, name="sum_rows", out_shape=SDS((1, n), F32), in_specs=[VMEM_SPEC], out_specs=VMEM_SPEC)(gathered)


def _adamw_vectors(total, offsets, separate, params):
    n_p = len(params)
    apart = [g for g in separate if g is not None]

    def body(*refs):
        total_ref, apart_refs = refs[0], list(refs[1:1 + len(apart)])
        ins, outs = refs[1 + len(apart):1 + len(apart) + 3 * n_p], refs[1 + len(apart) + 3 * n_p:]
        for i in range(n_p):
            w_ref, m_ref, v_ref = ins[3 * i:3 * i + 3]
            g = apart_refs.pop(0)[...] if offsets[i] is None else total_ref[:, offsets[i]:offsets[i] + w_ref.shape[1]]
            outs[4 * i][...] = g
            outs[4 * i + 1][...], outs[4 * i + 2][...], outs[4 * i + 3][...] = _adamw_math(w_ref[...], g, m_ref[...], v_ref[...])

    flat = [a for p in params for a in p]
    res = pl.pallas_call(
        body, name="adamw_vectors", out_shape=[SDS(p[0].shape, F32) for p in params for _ in range(4)],
        in_specs=[VMEM_SPEC] * (1 + len(apart) + len(flat)), out_specs=[VMEM_SPEC] * (4 * n_p),
    )(total, *apart, *flat)
    return [tuple(res[4 * i:4 * i + 4]) for i in range(n_p)]


def _blocks_to_pieces(w):
    nb, rows, n = w.shape
    q = rows // N_CHIPS
    return w.reshape(nb, N_CHIPS, q, n).transpose(1, 0, 2, 3).reshape(N_CHIPS, 2, nb // 2, q, n)


def _pieces_to_blocks(w):
    n_chips, _, half, q, n = w.shape
    return w.reshape(n_chips, 2 * half, q, n).transpose(1, 0, 2, 3).reshape(2 * half, n_chips * q, n)


def kernel(x, c, norm_mix_g, norm_mlp_g, w_ada, b_ada, w_in, conv_w, conv_b, w_rg_a, b_rg_a, w_rg_x, b_rg_x, a_param, w_branch_a, w_pool, b_pool, pool_scale, w_branch_b, w_out, w_up, w_down, final_g, loss_target, m_norm_mix_g, m_norm_mlp_g, m_w_ada, m_b_ada, m_w_in, m_conv_w, m_conv_b, m_w_rg_a, m_b_rg_a, m_w_rg_x, m_b_rg_x, m_a_param, m_w_branch_a, m_w_pool, m_b_pool, m_pool_scale, m_w_branch_b, m_w_out, m_w_up, m_w_down, m_final_g, v_norm_mix_g, v_norm_mlp_g, v_w_ada, v_b_ada, v_w_in, v_conv_w, v_conv_b, v_w_rg_a, v_b_rg_a, v_w_rg_x, v_b_rg_x, v_a_param, v_w_branch_a, v_w_pool, v_b_pool, v_pool_scale, v_w_branch_b, v_w_out, v_w_up, v_w_down, v_final_g):
    d = D_MODEL
    s_rows = x.shape[1]
    ts, tsq = min(512, s_rows), min(256, s_rows)
    xi, yi, ci = _coords()
    k_me, me = 2 * xi + yi, 4 * xi + 2 * yi + ci
    ada_cols = w_ada.shape[2]
    conv_cols = conv_w.shape[2]
    n_in, n_up = w_in.shape[2], w_up.shape[2]

    names = ("w_in", "w_up", "w_down", "w_a", "w_b", "w_out", "rg_a", "rg_x", "pool")
    mine = dict(zip(names, (w_in[0], w_up[0], w_down[0], w_branch_a[0], w_branch_b[0], w_out[0], w_rg_a[0], w_rg_x[0], w_pool[0])))
    moments_m = dict(zip(names, (m_w_in[0], m_w_up[0], m_w_down[0], m_w_branch_a[0], m_w_branch_b[0], m_w_out[0], m_w_rg_a[0], m_w_rg_x[0], m_w_pool[0])))
    moments_v = dict(zip(names, (v_w_in[0], v_w_up[0], v_w_down[0], v_w_branch_a[0], v_w_branch_b[0], v_w_out[0], v_w_rg_a[0], v_w_rg_x[0], v_w_pool[0])))
    block_weights, squares = ("rg_a", "rg_x", "pool"), ("w_a", "w_b", "w_out")
    place = jnp.stack([ci, k_me]).astype(jnp.int32)
    g_mix, g_mlp, g_fin = norm_mix_g, norm_mlp_g, final_g.reshape(1, d)
    x0, target = x[0], loss_target[0]
    wg = functools.partial(_weight_grad, tk=min(1024, d))
    plain, back, back_sum = (dict(transposed=t, reduce_blocks=r) for t, r in ((False, False), (True, False), (True, True)))
    rows_d, vec8, vec1 = _row_spec(ts, d), _vec_spec(8, d), _vec_spec(1, d)
    tw = min(1024, s_rows)
    halves_of_tile = 2

    def out_rows(cols, dtype, t=ts):
        return SDS((s_rows, cols), dtype), _row_spec(t, cols)

    sums_out = (SDS((8, d), F32), vec8)

    def halves(n):
        return mine[n].astype(BF16).reshape(2, -1, mine[n].shape[-1])

    def blocks(w):
        return _pieces_to_blocks(w.reshape(N_CHIPS, 2, N_BLOCKS // 2, BLOCK // N_CHIPS, BLOCK))

    act_all, mod_all, conv_all, w_in_whole, *gate_wholes = _prologue(
        c, conv_w[0], w_ada[0], lax.dynamic_slice_in_dim(b_ada, k_me * ada_cols, ada_cols, axis=1), halves("w_in"), [halves(n) for n in block_weights])
    conv_full = conv_all[0::2].transpose(1, 0, 2).reshape(CONV_WIDTH, d)
    mod_mine = lax.dynamic_index_in_dim(mod_all, me, axis=1, keepdims=False)[0::2]
    mod = jnp.concatenate([mod_mine.reshape(N_MOD, d), jnp.zeros((8 - N_MOD, d), F32)], axis=0)
    w_in_all = w_in_whole.reshape(N_CHIPS, d, n_in)
    vecs = jnp.concatenate([conv_b, b_rg_a, b_rg_x, a_param, b_pool, pool_scale, jnp.zeros((2, d), F32)], axis=0)

    def norm_first(a_ref, extra_refs, out_refs):
        h = _modulated_norm(a_ref[...], extra_refs[0], extra_refs[1], 1, 0)
        out_refs[1][...] = h
        return h

    stage = _Stage()
    i_blk = [_add_gather_d2d(stage, g) for g in gate_wholes]
    i_sq = [_add_gather_ici(stage, halves(n)) for n in squares]
    down_rows = mine["w_down"].shape[0] // 8
    i_down = _add_gather_ici(stage, halves("w_down"), rows=(0, down_rows))
    (proj, h1), got = _matmul(x0, w_in_all, name="proj_in", tm=ts, extra=[(mod, vec8), (g_mix, vec1)], prepare=norm_first,
                              outs=[out_rows(N_CHIPS * n_in, F32), out_rows(d, BF16)], finish=_store_blocks(n_in), parts=halves_of_tile,                              stage=stage, **plain)
    w_rg_a_all, w_rg_x_all, w_pool_all = (blocks(got[i]) for i in i_blk)
    stage = _Stage()
    i_sq = [_add_gather_d2d(stage, got[i]) for i in i_sq]
    i_up = _add_gather_ici(stage, halves("w_up"))
    i_down = _add_gather_ici(stage, halves("w_down"), rows=(down_rows, down_rows), whole=got[i_down])
    (xr, hr, ga, p, pooled), got = _seq_forward(proj, conv_full, vecs, w_rg_a_all, w_rg_x_all, w_pool_all, ts=tsq, stage=stage)
    w_a_all, w_b_all, w_out_all = (got[i].reshape(1, d, d) for i in i_sq)
    stage = _Stage()
    i_up = _add_gather_d2d(stage, got[i_up])
    i_down = _add_gather_ici(stage, halves("w_down"), rows=(2 * down_rows, 2 * down_rows), whole=got[i_down])
    (merged, s_a, s_b, t_a, t_b), got = _branches(ga, pooled, proj, w_a_all, w_b_all, ts=ts, stage=stage)
    w_up_all = got[i_up].reshape(N_CHIPS, d, n_up)
    stage = _Stage()
    i_down = _add_gather_d2d(stage, got[i_down])

    def residual_norm(k, acc, extra_refs, out_refs):
        x_ref, mod_ref, g_ref = extra_refs
        out_refs[0][...] = acc.astype(BF16)
        x2_t = x_ref[...] + mod_ref[2:3, :] * acc
        out_refs[1][...] = x2_t
        out_refs[2][...] = _modulated_norm(x2_t, mod_ref, g_ref, 4, 3)

    (mo, x2, h2), got = _matmul(merged, w_out_all, name="mix_out", tm=ts, extra=[(x0, rows_d), (mod, vec8), (g_mlp, vec1)],
                                outs=[out_rows(d, BF16), out_rows(d, F32), out_rows(d, BF16)], finish=residual_norm, parts=halves_of_tile,                                stage=stage, transposed=False, reduce_blocks=True)
    w_down_all = got[i_down].reshape(N_CHIPS, D_FF // N_CHIPS, d)

    def relu_squared(k, acc, extra_refs, out_refs):
        out_refs[0][:, k * n_up:(k + 1) * n_up] = jnp.square(jnp.maximum(acc, 0.0)).astype(BF16)

    ff, = _matmul(h2, w_up_all, name="mlp_up", tm=ts, outs=[out_rows(D_FF, BF16)], finish=relu_squared, parts=halves_of_tile, **plain)

    def loss_head(k, acc, extra_refs, out_refs):
        x2_ref, mod_ref, g_ref, t_ref = extra_refs
        dx_ref, df_ref, acc_ref = out_refs
        xh, r = _rms(x2_ref[...] + mod_ref[5:6, :] * acc)
        err = xh * g_ref[...] - t_ref[...]
        dy = err / d
        dxh = dy * g_ref[...]
        dx3_t = r * (dxh - xh * jnp.mean(dxh * xh, axis=-1, keepdims=True))
        dx_ref[...] = dx3_t
        df_ref[...] = (dx3_t * mod_ref[5:6, :]).astype(BF16)
        _accumulate(acc_ref, 0, dy * xh)
        _accumulate(acc_ref, 1, dx3_t * acc)
        _accumulate(acc_ref, 2, jnp.square(err) * (0.5 / d))

    dx3, dffo, sums_head = _matmul(ff, w_down_all, name="mlp_down", tm=ts, extra=[(x2, rows_d), (mod, vec8), (g_fin, vec1), (target, rows_d)],
                                   outs=[out_rows(d, F32), out_rows(d, BF16), sums_out], finish=loss_head, zero_first=(2,), parts=halves_of_tile,                                   transposed=False, reduce_blocks=True)

    partial_of, from_sibling, chip_sum, from_chips, half_done, quarter = {}, {}, {}, {}, {}, {}

    def publish(n, g):
        g = _blocks_to_pieces(g).astype(BF16) if n in block_weights else g
        partial_of[n] = g.reshape(N_CHIPS, 2, -1, mine[n].shape[-1])

    def exchange(to_sibling=(), to_chips=(), swap=()):
        st = _Stage()
        slots = [(n, from_sibling, _add_reduce_d2d(st, partial_of[n])) for n in to_sibling]
        slots += [(n, from_chips, _add_reduce_ici(st, chip_sum[n])) for n in to_chips]
        slots += [(n, quarter, _add_swap_halves(st, half_done[n])) for n in swap]
        return st, slots

    def collect(slots, outs):
        for n, where, i in slots:
            where[n] = outs[i]

    def sum_pairs(*ns):
        for n in ns:
            chip_sum[n] = _sum_pair(partial_of[n], from_sibling[n], place, "sum_pair_" + n)

    def sum_quarters(*ns):
        for n in ns:
            half_done[n] = _sum_quarter(partial_of[n], from_sibling[n], from_chips[n], place, "sum_quarter_" + n)

    publish("w_down", wg(ff, dffo, nb=1, name="grad_w_down"))

    def relu_squared_backward(k, acc, extra_refs, out_refs):
        cols = slice(k * n_up, (k + 1) * n_up)
        out_refs[0][:, cols] = (acc * (2.0 * jnp.sqrt(extra_refs[0][:, cols].astype(F32)))).astype(BF16)

    stage, slots = exchange(to_sibling=["w_down"])
    (dup,), got = _matmul(dffo, w_down_all, name="d_mlp_down", tm=ts, extra=[(ff, _row_spec(ts, D_FF))], outs=[out_rows(D_FF, BF16)],
                          finish=relu_squared_backward, parts=halves_of_tile, stage=stage, **back)
    collect(slots, got)
    sum_pairs("w_down")

    def norm_mlp_backward(k, acc, extra_refs, out_refs):
        x2_ref, dres_ref, mod_ref, g_ref, mo_ref = extra_refs
        dx_ref, dmo_ref, acc_ref = out_refs
        dx =_modulated_norm_backward(acc, x2_ref[...], dres_ref[...], mod_ref, g_ref, acc_ref, 4)
        dx_ref[...] = dx
        dmo_ref[...] = (dx * mod_ref[2:3, :]).astype(BF16)
        _accumulate(acc_ref, 3, dx * mo_ref[...].astype(F32))

    stage, slots = exchange(to_chips=["w_down"])
    (dx2, dmo, sums_mlp), got = _matmul(dup, w_up_all, name="d_mlp_up", tm=ts, stage=stage, finish=norm_mlp_backward,
                                        extra=[(x2, rows_d), (dx3, rows_d), (mod, vec8), (g_mlp, vec1), (mo, rows_d)],
                                        outs=[out_rows(d, F32), out_rows(d, BF16), sums_out], zero_first=(2,), parts=halves_of_tile, **back_sum)
    collect(slots, got)
    sum_quarters("w_down")
    stage, slots = exchange(swap=["w_down"])
    g_up, got = wg(h2, dup, nb=N_CHIPS, name="grad_w_up", stage=stage)
    collect(slots, got)
    publish("w_up", g_up)
    publish("w_out", wg(merged, dmo, nb=1, name="grad_w_out"))

    def merge_backward(k, acc, extra_refs, out_refs):
        sa_ref, sb_ref, ta_ref, tb_ref = extra_refs
        out_refs[0][...] = (acc * sa_ref[...].astype(F32)).astype(BF16)
        out_refs[1][...] = (acc * sb_ref[...].astype(F32)).astype(BF16)
        out_refs[2][:, 0:d] = (acc * ta_ref[...].astype(F32)).astype(BF16)
        out_refs[2][:, d:2 * d] = (acc * tb_ref[...].astype(F32)).astype(BF16)

    stage, slots = exchange(to_sibling=["w_up", "w_out"])
    (dbr_a, dbr_b, dgab), got = _matmul(dmo, w_out_all, name="d_mix_out", tm=ts, stage=stage, finish=merge_backward,
                                        extra=[(s_a, rows_d), (s_b, rows_d), (t_a, rows_d), (t_b, rows_d)],
                                        outs=[out_rows(d, BF16), out_rows(d, BF16), out_rows(2 * d, BF16)], parts=halves_of_tile, **back_sum)
    collect(slots, got)
    publish("w_a", wg(ga, dbr_a, nb=1, name="grad_w_branch_a"))
    publish("w_b", wg(pooled, dbr_b, nb=1, name="grad_w_branch_b"))
    stage, slots = exchange(to_sibling=["w_a", "w_b"])
    (dga,), got = _matmul(dbr_a, w_a_all, name="d_branch_a", tm=tw, outs=[out_rows(d, F32, tw)], finish=_store_blocks(d), stage=stage, **back_sum)
    collect(slots, got)
    dpooled, = _matmul(dbr_b, w_b_all, name="d_branch_b", tm=tw, outs=[out_rows(d, F32, tw)], finish=_store_blocks(d), **back_sum)
    sum_pairs("w_up", *squares)
    stage, slots = exchange(to_chips=("w_up",) + squares)
    (dproj, sums_seq, d_rg_a, d_rg_x, d_pool), got = _seq_backward_by_block(proj, xr, hr, p, dga, dpooled, dgab, conv_full, vecs,
                                                                            w_rg_a_all, w_rg_x_all, w_pool_all, ts=tsq, stage=stage)
    collect(slots, got)
    sum_quarters("w_up", *squares)
    for n, g in zip(block_weights, (d_rg_a, d_rg_x, d_pool)):
        publish(n, g)
    stage, slots = exchange(to_sibling=block_weights, swap=("w_up",) + squares)
    g_in, got = wg(h1, dproj, nb=N_CHIPS, name="grad_w_in", stage=stage)
    collect(slots, got)
    publish("w_in", g_in)
    sum_pairs(*block_weights)
    stage, slots = exchange(to_sibling=["w_in"], to_chips=block_weights)
    collect(slots, _run_stage(stage, "exchange_w_in_gates"))
    sum_pairs("w_in")
    sum_quarters(*block_weights)

    def norm_mix_backward(k, acc, extra_refs, out_refs):
        x_ref, dres_ref, mod_ref, g_ref = extra_refs
        out_refs[0][...] = _modulated_norm_backward(acc, x_ref[...], dres_ref[...], mod_ref, g_ref, out_refs[1], 1)

    stage, slots = exchange(to_chips=["w_in"], swap=block_weights)
    (grad_x, sums_mix), got = _matmul(dproj, w_in_all, name="d_proj_in", tm=ts, stage=stage, finish=norm_mix_backward,
                                      extra=[(x0, rows_d), (dx2, rows_d), (mod, vec8), (g_mix, vec1)],
                                      outs=[out_rows(d, F32), sums_out], zero_first=(1,), parts=halves_of_tile, **back_sum)
    collect(slots, got)
    sum_quarters("w_in")

    dmod = jnp.concatenate([sums_mix[1:2], sums_mix[0:1], sums_mlp[3:4], sums_mlp[1:2], sums_mlp[0:1], sums_head[1:2]], axis=1)
    row = jnp.concatenate([sums_mix[2:3], sums_mlp[2:3], sums_seq[4:5], sums_seq[5:6], sums_seq[6:7], sums_seq[7:8], sums_seq[8:9],
                           sums_seq[9:10], sums_head[0:1], sums_seq[0:CONV_WIDTH].reshape(1, CONV_WIDTH * d), dmod, sums_head[2:3]], axis=1)
    n_vec, n_conv, n_mod = 9 * d, CONV_WIDTH * d, N_MOD * d
    stage, slots = exchange(swap=["w_in"])
    i_rows = _add_allgather8(stage, row)
    got = _run_stage(stage, "exchange_sums_w_in")
    collect(slots, got)
    rows_all = got[i_rows]
    total = _sum_rows(rows_all, d)
    loss = total[0, n_vec + n_conv + n_mod]
    dmod_all = rows_all[:, 0, n_vec + n_conv:n_vec + n_conv + n_mod]
    g_ada, d_ada, m_ada, v_ada = _ada_backward(act_all.T, lax.dynamic_slice_in_dim(dmod_all, k_me * ada_cols, ada_cols, axis=1),
                                               w_ada[0], m_w_ada[0], v_w_ada[0])
    g_conv = lax.dynamic_slice_in_dim(total[:, n_vec:n_vec + n_conv].reshape(CONV_WIDTH, d), k_me * conv_cols, conv_cols, axis=1)
    vec_names = ("norm_mix_g", "norm_mlp_g", "conv_b", "b_rg_a", "b_rg_x", "a_param", "b_pool", "pool_scale", "final_g", "conv_w", "b_ada")
    vec_params = [(norm_mix_g, m_norm_mix_g, v_norm_mix_g), (norm_mlp_g, m_norm_mlp_g, v_norm_mlp_g), (conv_b, m_conv_b, v_conv_b),
                  (b_rg_a, m_b_rg_a, v_b_rg_a), (b_rg_x, m_b_rg_x, v_b_rg_x), (a_param, m_a_param, v_a_param), (b_pool, m_b_pool, v_b_pool),
                  (pool_scale, m_pool_scale, v_pool_scale), tuple(a.reshape(1, d) for a in (final_g, m_final_g, v_final_g)),
                  (conv_w[0], m_conv_w[0], v_conv_w[0]), (b_ada, m_b_ada, v_b_ada)]
    offsets = [i * d for i in range(9)] + [None, n_vec + n_conv]
    vec_res = _adamw_vectors(total, offsets, [None] * 9 + [g_conv, None], vec_params)
    vec_out = {n: [r.reshape(shape) for r in res] for n, res, shape in zip(
        vec_names, vec_res, [(1, d)] * 8 + [(d,), (1, CONV_WIDTH, conv_cols), (1, N_MOD * d)])}

    big_out = {}
    for n in names:
        shape2 = (-1, mine[n].shape[-1])
        g2 = quarter[n].reshape(shape2)
        res = (g2,) + tuple(_adamw(mine[n].reshape(shape2), g2, moments_m[n].reshape(shape2), moments_v[n].reshape(shape2), "adamw_" + n))
        big_out[n] = [r.reshape((1,) + mine[n].shape) for r in res]

    key = {"w_in": "w_in", "w_rg_a": "rg_a", "w_rg_x": "rg_x", "w_branch_a": "w_a", "w_pool": "pool", "w_branch_b": "w_b", "w_out": "w_out",
           "w_up": "w_up", "w_down": "w_down"}
    order = ("norm_mix_g", "norm_mlp_g", "w_ada", "b_ada", "w_in", "conv_w", "conv_b", "w_rg_a", "b_rg_a", "w_rg_x", "b_rg_x", "a_param",
             "w_branch_a", "w_pool", "b_pool", "pool_scale", "w_branch_b", "w_out", "w_up", "w_down", "final_g")
    ada_out = [g_ada[None], d_ada[None], m_ada[None], v_ada[None]]
    outs = [loss, grad_x[None]]
    for kind in range(4):
        for nme in order:
            outs.append(ada_out[kind] if nme == "w_ada" else big_out[key[nme]][kind] if nme in key else vec_out[nme][kind])
    return tuple(outs)
```

```python
import functools

import jax
import jax.numpy as jnp
from jax import lax
from jax.experimental import pallas as pl
from jax.experimental.pallas import tpu as pltpu

F32, BF16 = jnp.float32, jnp.bfloat16
D_MODEL = 1024
D_FF = 4 * D_MODEL
N_BLOCKS = 4
BLOCK = D_MODEL // N_BLOCKS
CONV_WIDTH = 4
POOL_WINDOWS = (2, 4, 8, 16)
CONV_HALO = 8
POOL_HALO = 16
N_MOD = 6
EPS = 1e-6
C_RG = 8.0
ADAM_LR, ADAM_B1, ADAM_B2, ADAM_EPS, ADAM_WD, ADAM_STEP = 0.001, 0.9, 0.999, 1e-08, 0.01, 10
N_DEV = 8
N_CHIPS = 4
VMEM_LIMIT_V7X = 56 * 2**20
MESH = pl.DeviceIdType.MESH
SDS = jax.ShapeDtypeStruct
HBM_SPEC = pl.BlockSpec(memory_space=pltpu.HBM)
VMEM_SPEC = pl.BlockSpec(memory_space=pltpu.VMEM)


def _params(*semantics):
    return pltpu.CompilerParams(dimension_semantics=semantics, vmem_limit_bytes=VMEM_LIMIT_V7X)


def _coords():
    return lax.axis_index("x"), lax.axis_index("y"), lax.axis_index("c")


def _flip(v, bit):
    return 1 - v if bit else v


def _chips():
    x, y, _ = _coords()
    return [(1 - x, y), (x, 1 - y), (1 - x, 1 - y)]


class _Stage:
    def __init__(self):
        self.inputs, self.in_specs, self.outputs, self.out_specs, self.aliases = [], [], [], [], {}
        self.parts, self.n_copies = [], 0

    def add(self, inputs, in_spec, outputs, out_spec, n_copies, build, alias=False):
        i0, o0 = len(self.inputs), len(self.outputs)
        self.inputs += inputs
        self.in_specs += [in_spec] * len(inputs)
        self.outputs += outputs
        self.out_specs += [out_spec] * len(outputs)
        if alias:
            self.aliases.update({i0 + i: o0 + i for i in range(len(inputs))})
        self.parts.append((build, i0, len(inputs), o0, len(outputs)))
        self.n_copies += n_copies
        return list(range(o0, o0 + len(outputs)))

    def copies(self, in_refs, out_refs):
        out = []
        for build, i0, ni, o0, no in self.parts:
            out += build(in_refs[i0:i0 + ni], out_refs[o0:o0 + no])
        assert len(out) == self.n_copies
        return out

    def run(self, in_refs, out_refs, send_sems, recv_sems, start):
        x, y, c = _coords()
        for s, (src, dst, to, landing) in enumerate(self.copies(in_refs, out_refs)):
            if to is None:
                cp = pltpu.make_async_copy(src, dst, send_sems.at[s])
                cp.start() if start else cp.wait()
                continue
            cp = pltpu.make_async_remote_copy(src_ref=src, dst_ref=dst, send_sem=send_sems.at[s], recv_sem=recv_sems.at[s],
                                              device_id=to, device_id_type=MESH)
            if start:
                cp.start()
            else:
                pltpu.make_async_remote_copy(src_ref=landing, dst_ref=landing, send_sem=send_sems.at[s], recv_sem=recv_sems.at[s],
                                             device_id=(x, y, c), device_id_type=MESH).wait_recv()
                cp.wait_send()


def _hosted(stage, body, *, name, in_specs, out_specs, out_shape, grid=(), scratch_shapes=(), compiler_params=None):
    if stage is None:
        return pl.pallas_call(body, name=name, grid=grid, in_specs=in_specs, out_specs=out_specs, out_shape=out_shape,
                              scratch_shapes=list(scratch_shapes), compiler_params=compiler_params)
    single = not isinstance(out_shape, (list, tuple))
    h_out_shape = [out_shape] if single else list(out_shape)
    h_out_specs = [out_specs] if single else list(out_specs)
    n_in, n_out, n_scr = len(in_specs), len(h_out_shape), len(scratch_shapes)
    s_in, s_out = len(stage.inputs), len(stage.outputs)

    def wrapped(*refs):
        h_in, st_in = refs[:n_in], refs[n_in:n_in + s_in]
        h_o, st_o = refs[n_in + s_in:n_in + s_in + n_out], refs[n_in + s_in + n_out:n_in + s_in + n_out + s_out]
        h_scr = refs[n_in + s_in + n_out + s_out:n_in + s_in + n_out + s_out + n_scr]
        send_sems, recv_sems = refs[n_in + s_in + n_out + s_out + n_scr:]
        if not grid:
            stage.run(st_in, st_o, send_sems, recv_sems, True)
            if body is not None:
                body(*h_in, *h_o, *h_scr)
            stage.run(st_in, st_o, send_sems, recv_sems, False)
            return
        ids = [pl.program_id(a) for a in range(len(grid))]
        first = functools.reduce(jnp.logical_and, [i == 0 for i in ids])
        last = functools.reduce(jnp.logical_and, [i == g - 1 for i, g in zip(ids, grid)])
        pl.when(first)(lambda: stage.run(st_in, st_o, send_sems, recv_sems, True))
        body(*h_in, *h_o, *h_scr)
        pl.when(last)(lambda: stage.run(st_in, st_o, send_sems, recv_sems, False))

    call = pl.pallas_call(
        wrapped, name=name, grid=grid, in_specs=list(in_specs) + stage.in_specs, out_specs=h_out_specs + stage.out_specs,
        out_shape=h_out_shape + stage.outputs, input_output_aliases={n_in + i: n_out + o for i, o in stage.aliases.items()},
        scratch_shapes=list(scratch_shapes) + [pltpu.SemaphoreType.DMA((stage.n_copies,)), pltpu.SemaphoreType.DMA((stage.n_copies,))],
        compiler_params=pltpu.CompilerParams(dimension_semantics=("arbitrary",) * len(grid), vmem_limit_bytes=VMEM_LIMIT_V7X),
    )

    def run(*operands):
        outs = call(*operands, *stage.inputs)
        host = outs[:n_out]
        return (host[0] if single else host), outs[n_out:]

    return run


def _run_stage(stage, name):
    return _hosted(stage, None, name=name, in_specs=[], out_specs=[], out_shape=[])()[1]


def _add_allgather8(stage, v):
    def build(ins, outs):
        x, y, c = _coords()
        me = 4 * x + 2 * y + c
        copies = [(ins[0], outs[0].at[me], None, None)]
        for r in range(1, N_DEV):
            peer = (_flip(x, r & 4), _flip(y, r & 2), _flip(c, r & 1))
            copies.append((ins[0], outs[0].at[me], peer, outs[0].at[me ^ r]))
        return copies

    return stage.add([v], VMEM_SPEC, [SDS((N_DEV,) + v.shape, v.dtype)], VMEM_SPEC, N_DEV, build)[0]


def _add_gather_ici(stage, shard, rows=None, whole=None):
    first, count = rows or (0, shard.shape[1])

    def build(ins, outs):
        x, y, c = _coords()
        k_me = 2 * x + y
        part = pl.ds(first, count)
        copies = [] if whole is not None else [(ins[0], outs[0].at[k_me], (x, y, 1 - c), outs[0].at[k_me])]
        for px, py in _chips():
            copies.append((ins[0].at[c, part], outs[0].at[k_me, c, part], (px, py, c), outs[0].at[2 * px + py, c, part]))
        return copies

    if whole is None:
        return stage.add([shard], HBM_SPEC, [SDS((N_CHIPS,) + shard.shape, shard.dtype)], HBM_SPEC, N_CHIPS, build)[0]

    def build_into(ins, outs):
        return build(ins[1:], outs)

    i0 = len(stage.inputs)
    out = stage.add([whole, shard], HBM_SPEC, [SDS(whole.shape, whole.dtype)], HBM_SPEC, N_CHIPS - 1, build_into)[0]
    stage.aliases[i0] = out
    return out


def _add_gather_d2d(stage, whole):
    def build(ins, outs):
        x, y, c = _coords()
        return [(outs[0].at[2 * px + py, c], outs[0].at[2 * px + py, c], (x, y, 1 - c), outs[0].at[2 * px + py, 1 - c]) for px, py in _chips()]

    return stage.add([whole], HBM_SPEC, [SDS(whole.shape, whole.dtype)], HBM_SPEC, N_CHIPS - 1, build, alias=True)[0]


def _add_reduce_d2d(stage, grads):
    def build(ins, outs):
        x, y, c = _coords()
        return [(ins[0].at[k, 1 - c], outs[0].at[k], (x, y, 1 - c), outs[0].at[k]) for k in range(N_CHIPS)]

    return stage.add([grads], HBM_SPEC, [SDS((N_CHIPS,) + grads.shape[2:], grads.dtype)], HBM_SPEC, N_CHIPS, build)[0]


def _add_reduce_ici(stage, partial):
    def build(ins, outs):
        x, y, c = _coords()
        return [(ins[0].at[2 * px + py], outs[0].at[j], (px, py, c), outs[0].at[j]) for j, (px, py) in enumerate(_chips())]

    return stage.add([partial], HBM_SPEC, [SDS((N_CHIPS - 1,) + partial.shape[1:], partial.dtype)], HBM_SPEC, N_CHIPS - 1, build)[0]


def _add_swap_halves(stage, quarter):
    def build(ins, outs):
        x, y, c = _coords()
        return [(outs[0].at[c], outs[0].at[c], (x, y, 1 - c), outs[0].at[1 - c])]

    return stage.add([quarter], HBM_SPEC, [SDS(quarter.shape, quarter.dtype)], HBM_SPEC, 1, build, alias=True)[0]


def _sum_pair(grads, from_sibling, place, name):
    _, _, rows, cols = grads.shape
    tr = min(rows, 512)

    def body(place_ref, g_ref, s_ref, o_ref):
        o_ref[0] = (g_ref[0, 0].astype(F32) + s_ref[0].astype(F32)).astype(BF16)

    spec = pl.BlockSpec((1, tr, cols), lambda k, i, place_ref: (k, i, 0))
    return pl.pallas_call(
        body, name=name, out_shape=SDS((N_CHIPS, rows, cols), BF16),
        grid_spec=pltpu.PrefetchScalarGridSpec(
            num_scalar_prefetch=1, grid=(N_CHIPS, rows // tr),
            in_specs=[pl.BlockSpec((1, 1, tr, cols), lambda k, i, place_ref: (k, place_ref[0], i, 0)), spec], out_specs=spec),
        compiler_params=_params("parallel", "parallel"),
    )(place, grads, from_sibling)


def _sum_quarter(grads, from_sibling, from_chips, place, name):
    _, _, rows, cols = grads.shape
    tr = min(rows, 512)

    def body(place_ref, g_ref, s_ref, c_ref, o_ref):
        acc = g_ref[0, 0].astype(F32) + s_ref[0].astype(F32)
        for j in range(N_CHIPS - 1):
            acc = acc + c_ref[j].astype(F32)
        o_ref[0] = acc

    return pl.pallas_call(
        body, name=name, out_shape=SDS((2, rows, cols), F32),
        grid_spec=pltpu.PrefetchScalarGridSpec(
            num_scalar_prefetch=1, grid=(rows // tr,),
            in_specs=[pl.BlockSpec((1, 1, tr, cols), lambda i, place_ref: (place_ref[1], place_ref[0], i, 0)),
                      pl.BlockSpec((1, tr, cols), lambda i, place_ref: (place_ref[1], i, 0)),
                      pl.BlockSpec((N_CHIPS - 1, tr, cols), lambda i, place_ref: (0, i, 0))],
            out_specs=pl.BlockSpec((1, tr, cols), lambda i, place_ref: (place_ref[0], i, 0))),
        compiler_params=_params("parallel"),
    )(place, grads, from_sibling, from_chips)


def _row_spec(ts, cols, col_block=0):
    return pl.BlockSpec((ts, cols), lambda *g: (g[0], col_block))


def _vec_spec(rows, cols):
    return pl.BlockSpec((rows, cols), lambda *g: (0, 0))


def _matmul(a, w, *, transposed, reduce_blocks, name, tm, outs, finish, extra=(), prepare=None, zero_first=(), parts=1, stage=None):
    m_rows = a.shape[0]
    nb, r, c = w.shape
    kb = c if transposed else r
    dims = (((1,), (1,)), ((), ())) if transposed else (((1,), (0,)), ((), ()))
    assert a.shape[1] == (nb * kb if reduce_blocks else kb) and m_rows % tm == 0 and tm % parts == 0
    n_extra, sub = len(extra), tm // parts

    def body(a_ref, w_ref, *rest):
        extra_refs, out_refs = rest[:n_extra], rest[n_extra:]
        if zero_first:
            @pl.when(pl.program_id(0) == 0)
            def _():
                for i in zero_first:
                    out_refs[i][...] = jnp.zeros_like(out_refs[i])

        def views(j):
            rows = pl.ds(j * sub, sub)
            return rows, [ref.at[rows] if ref.shape[0] == tm else ref for ref in extra_refs], [ref.at[rows] if ref.shape[0] == tm else ref for ref in out_refs]

        def product(j):
            rows, ex, ou = views(j)
            if reduce_blocks:
                acc = lax.dot_general(a_ref[rows, 0:kb], w_ref[0], dims, preferred_element_type=F32)
                for k in range(1, nb):
                    acc += lax.dot_general(a_ref[rows, k * kb:(k + 1) * kb], w_ref[k], dims, preferred_element_type=F32)
                return [(None, acc)]
            lhs = a_ref[rows, :] if prepare is None else prepare(a_ref.at[rows], ex, ou)
            return [(k, lax.dot_general(lhs, w_ref[k], dims, preferred_element_type=F32)) for k in range(nb)]

        pending = product(0)
        for j in range(parts):
            ahead = product(j + 1) if j + 1 < parts else None
            _, ex, ou = views(j)
            for k, acc in pending:
                finish(k, acc, ex, ou)
            pending = ahead

    return _hosted(
        stage, body, name=name, grid=(m_rows // tm,), out_shape=[s for s, _ in outs], out_specs=[s for _, s in outs],
        in_specs=[_row_spec(tm, a.shape[1]), pl.BlockSpec((nb, r, c), lambda *g: (0, 0, 0), pipeline_mode=pl.Buffered(1))] + [s for _, s in extra],
        compiler_params=_params("arbitrary"),
    )(a, w, *[e for e, _ in extra])


def _store_blocks(n):
    def finish(k, acc, extra_refs, out_refs):
        if k is None:
            out_refs[0][...] = acc.astype(out_refs[0].dtype)
        else:
            out_refs[0][:, k * n:(k + 1) * n] = acc.astype(out_refs[0].dtype)
    return finish


def _weight_grad(a, b, *, nb, name, tk, stage=None):
    s_rows, k1 = a.shape
    tn = b.shape[1] // nb
    assert k1 % tk == 0 and b.shape[1] % nb == 0

    def body(a_ref, b_ref, o_ref):
        o_ref[0] = lax.dot_general(a_ref[...], b_ref[...], (((0,), (0,)), ((), ())), preferred_element_type=F32).astype(o_ref.dtype)

    return _hosted(
        stage, body, name=name, grid=(nb, k1 // tk), out_shape=SDS((nb, k1, tn), BF16),
        in_specs=[pl.BlockSpec((s_rows, tk), lambda n, i: (0, i)), pl.BlockSpec((s_rows, tn), lambda n, i: (0, n))],
        out_specs=pl.BlockSpec((1, tk, tn), lambda n, i: (n, i, 0)), compiler_params=_params("parallel", "parallel"),
    )(a, b)


def _rms(x):
    r = lax.rsqrt(jnp.mean(x * x, axis=-1, keepdims=True) + EPS)
    return x * r, r


def _accumulate(acc_ref, row, value):
    acc_ref[row:row + 1, :] += jnp.sum(value, axis=0, keepdims=True)


def _gelu_parts(y):
    k0, k1 = 0.7978845608028654, 0.044715
    th = jnp.tanh(k0 * (y + k1 * (y * y * y)))
    gelu = 0.5 * y * (1.0 + th)
    dgelu = 0.5 * (1.0 + th) + 0.5 * y * (1.0 - th * th) * (k0 * (1.0 + 3.0 * k1 * (y * y)))
    return gelu, dgelu


def _one_minus_square(a, log_a):
    return -jnp.tanh(log_a) * (1.0 + a * a)


def _modulated_norm(x, mod_ref, g_ref, sc_row, sh_row):
    xh, _ = _rms(x)
    return ((xh * g_ref[...]) * (1.0 + mod_ref[sc_row:sc_row + 1, :]) + mod_ref[sh_row:sh_row + 1, :]).astype(BF16)


def _modulated_norm_backward(dh, x, dres, mod_ref, g_ref, acc_ref, sc_row):
    xh, r = _rms(x)
    _accumulate(acc_ref, 0, dh * (xh * g_ref[...]))
    _accumulate(acc_ref, 1, dh)
    dn = dh * (1.0 + mod_ref[sc_row:sc_row + 1, :])
    _accumulate(acc_ref, 2, dn * xh)
    dxh = dn * g_ref[...]
    return dres + r * (dxh - xh * jnp.mean(dxh * xh, axis=-1, keepdims=True))


def _seq_forward(proj, conv_w, vecs, w_a, w_x, w_p, *, ts, stage=None):
    s_rows = proj.shape[0]
    d = D_MODEL

    def body(x_ref, xh_ref, y_ref, u_ref, uh_ref, cw_ref, vec_ref, wa_ref, wx_ref, wp_ref,
             xr_o, hr_o, ga_o, p_o, pooled_o, carry, a_scr, b_scr):
        t = pl.program_id(0)

        @pl.when(t == 0)
        def _():
            carry[...] = jnp.zeros_like(carry)

        x = x_ref[...]
        halo = jnp.where(t == 0, 0.0, xh_ref[...])
        xx = jnp.concatenate([halo, x], axis=0)
        xr = vec_ref[0:1, :] + x * cw_ref[CONV_WIDTH - 1:CONV_WIDTH, :]
        for j in range(CONV_WIDTH - 1):
            xr = xr + pltpu.roll(xx, CONV_WIDTH - 1 - j, 0)[CONV_HALO:] * cw_ref[j:j + 1, :]
        xr_o[...] = xr
        xrb = xr.astype(BF16)
        zr = jnp.concatenate([jnp.dot(xrb[:, g * BLOCK:(g + 1) * BLOCK], wa_ref[g], preferred_element_type=F32)
                              for g in range(N_BLOCKS)], axis=1) + vec_ref[1:2, :]
        zi = jnp.concatenate([jnp.dot(xrb[:, g * BLOCK:(g + 1) * BLOCK], wx_ref[g], preferred_element_type=F32)
                              for g in range(N_BLOCKS)], axis=1) + vec_ref[2:3, :]
        r = jax.nn.sigmoid(zr)
        gate_i = jax.nn.sigmoid(zi)
        log_a = (-C_RG * r) * jax.nn.softplus(vec_ref[3:4, :])
        rows = lax.broadcasted_iota(jnp.int32, (ts, d), 0)
        a = jnp.exp(log_a)
        mult = jnp.where((rows == 0) & (t == 0), 1.0, jnp.sqrt(_one_minus_square(a, log_a)))
        a_scr[...] = a
        b_scr[...] = xr * gate_i * mult

        sub = lax.broadcasted_iota(jnp.int32, (8, d), 0)

        def chunk(i, h):
            a = a_scr[pl.ds(i * 8, 8), :]
            b = b_scr[pl.ds(i * 8, 8), :]
            for s in (1, 2, 4):
                keep = sub >= s
                b = jnp.where(keep, a * pltpu.roll(b, s, 0) + b, b)
                a = jnp.where(keep, a * pltpu.roll(a, s, 0), a)
            hh = a * h + b
            hr_o[pl.ds(i * 8, 8), :] = hh
            return hh[7:8, :]

        carry[...] = lax.fori_loop(0, ts // 8, chunk, carry[...])
        gelu, _ = _gelu_parts(y_ref[...])
        ga_o[...] = (gelu * hr_o[...]).astype(BF16)

        u = u_ref[...]
        uu = jnp.concatenate([jnp.where(t == 0, 0.0, uh_ref[...]), u], axis=0)
        pos = (rows[:, :BLOCK] + t * ts + 1).astype(F32)
        sums, have, parts = uu, 1, []
        for g, win in enumerate(POOL_WINDOWS):
            while have < win:
                sums = sums + pltpu.roll(sums, have, 0)
                have *= 2
            mean = sums[POOL_HALO:, g * BLOCK:(g + 1) * BLOCK] * (1.0 / jnp.minimum(pos, float(win)))
            parts.append(mean - u[:, g * BLOCK:(g + 1) * BLOCK])
        pb = jnp.concatenate(parts, axis=1).astype(BF16)
        p_o[...] = pb
        mixed = jnp.concatenate([jnp.dot(pb[:, g * BLOCK:(g + 1) * BLOCK], wp_ref[g], preferred_element_type=F32)
                                 for g in range(N_BLOCKS)], axis=1) + vec_ref[4:5, :]
        pooled_o[...] = (mixed * vec_ref[5:6, :]).astype(BF16)

    halo_x = pl.BlockSpec((CONV_HALO, d), lambda t: (jnp.maximum(t * (ts // CONV_HALO) - 1, 0), 0))
    halo_u = pl.BlockSpec((POOL_HALO, d), lambda t: (jnp.maximum(t * (ts // POOL_HALO) - 1, 0), 2))
    wspec = pl.BlockSpec((N_BLOCKS, BLOCK, BLOCK), lambda t: (0, 0, 0))
    return _hosted(
        stage, body, name="seq_forward", grid=(s_rows // ts,),
        out_shape=[SDS((s_rows, d), F32), SDS((s_rows, d), F32), SDS((s_rows, d), BF16), SDS((s_rows, d), BF16), SDS((s_rows, d), BF16)],
        in_specs=[_row_spec(ts, d, 0), halo_x, _row_spec(ts, d, 1), _row_spec(ts, d, 2), halo_u, _vec_spec(CONV_WIDTH, d), _vec_spec(8, d),
                  wspec, wspec, wspec],
        out_specs=[_row_spec(ts, d)] * 5,
        scratch_shapes=[pltpu.VMEM((1, d), F32), pltpu.VMEM((ts, d), F32), pltpu.VMEM((ts, d), F32)],
        compiler_params=_params("arbitrary"),
    )(proj, proj, proj, proj, proj, conv_w, vecs, w_a, w_x, w_p)


def _branches(ga, pooled, proj, w_a, w_b, *, ts, stage=None):
    s_rows, d = ga.shape

    def body(a_ref, p_ref, ga_ref, gb_ref, wa_ref, wb_ref, merged_o, sa_o, sb_o, ta_o, tb_o):
        bra = jnp.dot(a_ref[...], wa_ref[0], preferred_element_type=F32)
        brb = jnp.dot(p_ref[...], wb_ref[0], preferred_element_type=F32)
        sa, sb = jax.nn.sigmoid(ga_ref[...]), jax.nn.sigmoid(gb_ref[...])
        merged_o[...] = (sa * bra + sb * brb).astype(BF16)
        sa_o[...] = sa.astype(BF16)
        sb_o[...] = sb.astype(BF16)
        ta_o[...] = (bra * (sa * (1.0 - sa))).astype(BF16)
        tb_o[...] = (brb * (sb * (1.0 - sb))).astype(BF16)

    rows = _row_spec(ts, d)
    wspec = pl.BlockSpec((1, d, d), lambda *g: (0, 0, 0), pipeline_mode=pl.Buffered(1))
    return _hosted(
        stage, body, name="branches", grid=(s_rows // ts,), out_shape=[SDS((s_rows, d), BF16)] * 5,
        in_specs=[rows, rows, _row_spec(ts, d, 3), _row_spec(ts, d, 4), wspec, wspec], out_specs=[rows] * 5,
        compiler_params=_params("arbitrary"),
    )(ga, pooled, proj, proj, w_a, w_b)


def _seq_backward_by_block(proj, xr, hr, p, dga, dpooled, dgab, conv_w, vecs, w_a, w_x, w_p, *, ts, stage=None):
    s_rows = proj.shape[0]
    d = D_MODEL
    n_t = s_rows // ts
    nn, nt, tn = (((1,), (0,)), ((), ())), (((1,), (1,)), ((), ())), (((0,), (0,)), ((), ()))

    def dot(lhs, rhs, dims):
        return lax.dot_general(lhs, rhs, dims, preferred_element_type=F32)

    def body(x_ref, y_ref, xr_ref, hr_ref, hh_ref, p_ref, dga_ref, dpl_ref, dgab_ref, cw_ref, vec_ref, wa_ref, wx_ref, wp_ref,
             dproj_o, acc_o, dwa_o, dwx_o, dwp_o, g_carry, dxr_carry, q_carry, a_scr, b_scr, g_scr, a_keep, r_keep, i_keep, m_keep):
        step = pl.program_id(0)
        t = n_t - 1 - step

        @pl.when(step == 0)
        def _():
            for ref in (acc_o, dwa_o, dwx_o, dwp_o, g_carry, dxr_carry, q_carry):
                ref[...] = jnp.zeros_like(ref)

        rows = lax.broadcasted_iota(jnp.int32, (ts, BLOCK), 0)
        start = (rows == 0) & (t == 0)

        def add_sum(row, lanes, value):
            acc_o[row:row + 1, lanes] += jnp.sum(value, axis=0, keepdims=True)

        for g in range(N_BLOCKS):
            lanes = slice(g * BLOCK, (g + 1) * BLOCK)
            xrb = xr_ref[:, lanes].astype(BF16)
            r = jax.nn.sigmoid(dot(xrb, wa_ref[g], nn) + vec_ref[1:2, lanes])
            gate_i = jax.nn.sigmoid(dot(xrb, wx_ref[g], nn) + vec_ref[2:3, lanes])
            log_a = (-C_RG * r) * jax.nn.softplus(vec_ref[3:4, lanes])
            a = jnp.exp(log_a)
            a_keep[:, lanes], r_keep[:, lanes], i_keep[:, lanes], m_keep[:, lanes] = a, r, gate_i, _one_minus_square(a, log_a)
            gelu, dgelu = _gelu_parts(y_ref[:, lanes])
            dga_t = dga_ref[:, lanes]
            dproj_o[:, d + g * BLOCK:d + (g + 1) * BLOCK] = (dga_t * hr_ref[:, lanes] * dgelu).astype(BF16)
            a_scr[:, lanes] = jnp.where(rows == ts - 1, 1.0, pltpu.roll(a, ts - 1, 0))
            b_scr[:, lanes] = dga_t * gelu

        sub = lax.broadcasted_iota(jnp.int32, (8, d), 0)

        def chunk(i, g_next):
            at = pl.multiple_of((ts // 8 - 1 - i) * 8, 8)
            aa = a_scr[pl.ds(at, 8), :]
            bb = b_scr[pl.ds(at, 8), :]
            for s in (1, 2, 4):
                keep = sub < 8 - s
                bb = jnp.where(keep, bb + aa * pltpu.roll(bb, 8 - s, 0), bb)
                aa = jnp.where(keep, aa * pltpu.roll(aa, 8 - s, 0), aa)
            gg = aa * g_next + bb
            g_scr[pl.ds(at, 8), :] = gg
            return gg[0:1, :]

        g_first = lax.fori_loop(0, ts // 8, chunk, g_carry[...])
        g_carry[...] = a_keep[0:1, :] * g_first

        pos = (rows + t * ts + 1).astype(F32)
        for g, win in enumerate(POOL_WINDOWS):
            lanes = slice(g * BLOCK, (g + 1) * BLOCK)
            gs, xr, a, r, gate_i, m_square = g_scr[:, lanes], xr_ref[:, lanes], a_keep[:, lanes], r_keep[:, lanes], i_keep[:, lanes], m_keep[:, lanes]
            sp = jax.nn.softplus(vec_ref[3:4, lanes])
            mult = jnp.where(start, 1.0, jnp.sqrt(m_square))
            h_before = jnp.where(t == 0, 0.0, hh_ref[CONV_HALO - 1:CONV_HALO, lanes])
            h_prev = jnp.where(rows == 0, h_before, pltpu.roll(hr_ref[:, lanes], 1, 0))
            dxr = gs * gate_i * mult
            d_i = gs * xr * mult
            d_mult = gs * xr * gate_i
            dlog_a = (gs * h_prev) * a - jnp.where(start, 0.0, d_mult * (a * a) * lax.rsqrt(m_square))
            dzr = (dlog_a * (-C_RG * sp)) * (r * (1.0 - r))
            dzi = d_i * (gate_i * (1.0 - gate_i))
            add_sum(5, lanes, dzr)
            add_sum(6, lanes, dzi)
            add_sum(7, lanes, dlog_a * (-C_RG * r) * jax.nn.sigmoid(vec_ref[3:4, lanes]))
            xrb, dzrb, dzib = xr.astype(BF16), dzr.astype(BF16), dzi.astype(BF16)
            dwa_o[g] += dot(xrb, dzrb, tn)
            dwx_o[g] += dot(xrb, dzib, tn)
            dxr = dxr + dot(dzrb, wa_ref[g], nt) + dot(dzib, wx_ref[g], nt)

            x = x_ref[:, lanes]
            ext = jnp.concatenate([dxr, dxr_carry[:, lanes]], axis=0)
            dx = dxr * cw_ref[CONV_WIDTH - 1:CONV_WIDTH, lanes]
            add_sum(CONV_WIDTH - 1, lanes, x * dxr)
            for j in range(CONV_WIDTH - 1):
                ahead = pltpu.roll(ext, ts + CONV_HALO - (CONV_WIDTH - 1 - j), 0)[:ts]
                dx = dx + ahead * cw_ref[j:j + 1, lanes]
                add_sum(j, lanes, x * ahead)
            add_sum(4, lanes, dxr)
            dxr_carry[:, lanes] = dxr[0:CONV_HALO, :]
            dproj_o[:, lanes] = dx.astype(BF16)

            pb = p_ref[:, lanes]
            mixed = dot(pb, wp_ref[g], nn) + vec_ref[4:5, lanes]
            dpl = dpl_ref[:, lanes]
            add_sum(9, lanes, dpl * mixed)
            dmixed = dpl * vec_ref[5:6, lanes]
            add_sum(8, lanes, dmixed)
            dmb = dmixed.astype(BF16)
            dwp_o[g] += dot(pb, dmb, tn)
            dp = dot(dmb, wp_ref[g], nt)
            q = dp * (1.0 / jnp.minimum(pos, float(win)))
            sums, have = jnp.concatenate([q, q_carry[:, lanes]], axis=0), 1
            while have < win:
                sums = sums + pltpu.roll(sums, ts + POOL_HALO - have, 0)
                have *= 2
            q_carry[:, lanes] = q[0:POOL_HALO, :]
            dproj_o[:, 2 * d + g * BLOCK:2 * d + (g + 1) * BLOCK] = (sums[:ts] - dp).astype(BF16)

        dproj_o[:, 3 * d:5 * d] = dgab_ref[...]

    def rev(cols, col_block=0):
        return pl.BlockSpec((ts, cols), lambda i: (n_t - 1 - i, col_block))

    halo_h = pl.BlockSpec((CONV_HALO, d), lambda i: (jnp.maximum((n_t - 1 - i) * (ts // CONV_HALO) - 1, 0), 0))
    wspec = pl.BlockSpec((N_BLOCKS, BLOCK, BLOCK), lambda i: (0, 0, 0))
    return _hosted(
        stage, body, name="seq_backward", grid=(n_t,),
        out_shape=[SDS((s_rows, 5 * d), BF16), SDS((16, d), F32)] + [SDS((N_BLOCKS, BLOCK, BLOCK), F32)] * 3,
        in_specs=[rev(d, 0), rev(d, 1), rev(d), rev(d), halo_h, rev(d), rev(d), rev(d), rev(2 * d), _vec_spec(CONV_WIDTH, d), _vec_spec(8, d),
                  wspec, wspec, wspec],
        out_specs=[rev(5 * d), _vec_spec(16, d), wspec, wspec, wspec],
        scratch_shapes=[pltpu.VMEM((1, d), F32), pltpu.VMEM((CONV_HALO, d), F32), pltpu.VMEM((POOL_HALO, d), F32)] + [pltpu.VMEM((ts, d), F32)] * 7,
        compiler_params=_params("arbitrary"),
    )(proj, proj, xr, hr, hr, p, dga, dpooled, dgab, conv_w, vecs, w_a, w_x, w_p)


def _adamw_math(w, g, m, v):
    m = ADAM_B1 * m + (1.0 - ADAM_B1) * g
    v = ADAM_B2 * v + (1.0 - ADAM_B2) * jnp.square(g)
    m_hat = m / (1.0 - ADAM_B1 ** ADAM_STEP)
    v_hat = v / (1.0 - ADAM_B2 ** ADAM_STEP)
    delta = -ADAM_LR * (m_hat / (jnp.sqrt(v_hat) + ADAM_EPS) + ADAM_WD * w)
    return delta, m, v


def _adamw(w, g, m, v, name):
    rows, cols = w.shape
    tr = min(rows, 512)

    def body(w_ref, g_ref, m_ref, v_ref, d_ref, mo_ref, vo_ref):
        d_ref[...], mo_ref[...], vo_ref[...] = _adamw_math(w_ref[...], g_ref[...], m_ref[...], v_ref[...])

    spec = pl.BlockSpec((tr, cols), lambda i: (i, 0))
    return pl.pallas_call(
        body, name=name, grid=(rows // tr,), out_shape=[SDS((rows, cols), F32)] * 3, in_specs=[spec] * 4, out_specs=[spec] * 3,
        compiler_params=_params("parallel"),
    )(w, g, m, v)


def _prologue(c, conv_shard, w_ada, b_cols, w_in_halves, gate_halves):
    n_cols = w_ada.shape[1]
    n_g = len(gate_halves)
    half_rows = w_in_halves.shape[1] // 2
    n_sems = 3 * N_DEV + N_CHIPS * n_g + 5 + 2 + 3

    def body(c_ref, conv_ref, w_ref, b_ref, win_ref, *rest):
        gate_refs = rest[:n_g]
        act_o, mod_o, conv_o, win_o = rest[n_g:n_g + 4]
        gate_o = rest[n_g + 4:2 * n_g + 4]
        c_all, mod_cols, send_sems, recv_sems = rest[2 * n_g + 4:]
        x, y, c = _coords()
        me, k_me = 4 * x + 2 * y + c, 2 * x + y
        sibling, x_nbr, y_nbr = (x, y, 1 - c), (1 - x, y, c), (x, 1 - y, c)
        k_x, k_y, k_d = 2 * (1 - x) + y, 2 * x + (1 - y), 2 * (1 - x) + (1 - y)
        counter = iter(range(n_sems))

        def copy(src, dst, to, landing):
            s = next(counter)
            out = pltpu.make_async_remote_copy(src_ref=src, dst_ref=dst, send_sem=send_sems.at[s], recv_sem=recv_sems.at[s],
                                               device_id=to, device_id_type=MESH)
            arrival = pltpu.make_async_remote_copy(src_ref=landing, dst_ref=landing, send_sem=send_sems.at[s], recv_sem=recv_sems.at[s],
                                                   device_id=(x, y, c), device_id_type=MESH)
            return out, arrival

        def allgather(src, out):
            s = next(counter)
            own = pltpu.make_async_copy(src, out.at[me], send_sems.at[s])
            peers = [copy(src, out.at[me], (_flip(x, r & 4), _flip(y, r & 2), _flip(c, r & 1)), out.at[me ^ r]) for r in range(1, N_DEV)]
            return own, peers

        def start(group):
            own, peers = group
            own.start()
            for out, _ in peers:
                out.start()

        def finish(group):
            own, peers = group
            for out, arrival in peers:
                arrival.wait_recv()
                out.wait_send()
            own.wait()

        rows_a, rows_b = pl.ds(0, half_rows), pl.ds(half_rows, half_rows)
        c_group, conv_group = allgather(c_ref, c_all), allgather(conv_ref, conv_o)
        gates = []
        for g_ref, g_out in zip(gate_refs, gate_o):
            gates.append(copy(g_ref, g_out.at[k_me], sibling, g_out.at[k_me]))
            gates += [copy(g_ref.at[c], g_out.at[k_me, c], (px, py, c), g_out.at[2 * px + py, c]) for px, py in _chips()]
        own_in = copy(win_ref, win_o.at[k_me], sibling, win_o.at[k_me])
        to_x = [copy(win_ref.at[c, rows], win_o.at[k_me, c, rows], x_nbr, win_o.at[k_x, c, rows]) for rows in (rows_a, rows_b)]
        to_y = [copy(win_ref.at[c, rows], win_o.at[k_me, c, rows], y_nbr, win_o.at[k_y, c, rows]) for rows in (rows_b, rows_a)]
        mod_group = allgather(mod_cols, mod_o)
        relay_y = copy(win_o.at[k_x, c, rows_a], win_o.at[k_x, c, rows_a], y_nbr, win_o.at[k_d, c, rows_a])
        relay_x = copy(win_o.at[k_y, c, rows_b], win_o.at[k_y, c, rows_b], x_nbr, win_o.at[k_d, c, rows_b])
        sib_x, sib_y, sib_d = (copy(win_o.at[k, c], win_o.at[k, c], sibling, win_o.at[k, 1 - c]) for k in (k_x, k_y, k_d))

        start(c_group)
        start(conv_group)
        for out, _ in gates + [own_in] + to_x + to_y:
            out.start()
        finish(c_group)
        cv = jnp.concatenate([c_all[r] for r in range(N_DEV)], axis=0)
        act = cv * jax.nn.sigmoid(cv)
        act_o[...] = act
        mod_cols[...] = jnp.dot(act.astype(BF16), w_ref[...].astype(BF16), preferred_element_type=F32) + b_ref[...]
        start(mod_group)
        to_x[0][1].wait_recv()
        relay_y[0].start()
        to_y[0][1].wait_recv()
        relay_x[0].start()
        finish(mod_group)
        finish(conv_group)
        to_x[1][1].wait_recv()
        sib_x[0].start()
        to_y[1][1].wait_recv()
        sib_y[0].start()
        for out, arrival in gates + [own_in]:
            arrival.wait_recv()
            out.wait_send()
        for out, arrival in (relay_y, relay_x):
            arrival.wait_recv()
            out.wait_send()
        sib_d[0].start()
        for out, _ in to_x + to_y:
            out.wait_send()
        for out, arrival in (sib_x, sib_y, sib_d):
            arrival.wait_recv()
            out.wait_send()

    gate_shapes = [SDS((N_CHIPS,) + g.shape, g.dtype) for g in gate_halves]
    return pl.pallas_call(
        body, name="prologue",
        out_shape=[SDS((N_DEV, c.shape[1]), F32), SDS((N_DEV, N_DEV, n_cols), F32), SDS((N_DEV,) + conv_shard.shape, F32),
                   SDS((N_CHIPS,) + w_in_halves.shape, w_in_halves.dtype)] + gate_shapes,
        in_specs=[VMEM_SPEC] * 4 + [HBM_SPEC] * (1 + n_g), out_specs=[VMEM_SPEC] * 3 + [HBM_SPEC] * (1 + n_g),
        scratch_shapes=[pltpu.VMEM((N_DEV,) + c.shape, F32), pltpu.VMEM((N_DEV, n_cols), F32),
                        pltpu.SemaphoreType.DMA((n_sems,)), pltpu.SemaphoreType.DMA((n_sems,))],
        compiler_params=pltpu.CompilerParams(vmem_limit_bytes=VMEM_LIMIT_V7X),
    )(c, conv_shard, w_ada, b_cols, w_in_halves, *gate_halves)


def _ada_backward(act_t, dmod_cols, w, m, v):
    rows, cols = w.shape
    tn = 512

    def body(a_ref, dm_ref, w_ref, m_ref, v_ref, g_ref, d_ref, mo_ref, vo_ref):
        g = jnp.dot(a_ref[...].astype(BF16), dm_ref[...].astype(BF16), preferred_element_type=F32)
        g_ref[...] = g
        d_ref[...], mo_ref[...], vo_ref[...] = _adamw_math(w_ref[...], g, m_ref[...], v_ref[...])

    spec = pl.BlockSpec((rows, tn), lambda j: (0, j))
    return pl.pallas_call(
        body, name="ada_backward", grid=(cols // tn,), out_shape=[SDS((rows, cols), F32)] * 4,
        in_specs=[pl.BlockSpec((rows, N_DEV), lambda j: (0, 0)), pl.BlockSpec((N_DEV, tn), lambda j: (0, j)), spec, spec, spec],
        out_specs=[spec] * 4, compiler_params=_params("parallel"),
    )(act_t, dmod_cols, w, m, v)


def _sum_rows(gathered, n_loss):
    n = gathered.shape[2]

    def body(g_ref, o_ref):
        acc = g_ref[0]
        for r in range(1, N_DEV):
            acc = acc + g_ref[r]
        o_ref[...] = acc
        o_ref[:, n - n_loss:n] = jnp.broadcast_to(jnp.sum(acc[:, n - n_loss:n], axis=1, keepdims=True), (1, n_loss))

    return pl.pallas_call(body, name="sum_rows", out_shape=SDS((1, n), F32), in_specs=[VMEM_SPEC], out_specs=VMEM_SPEC)(gathered)


def _adamw_vectors(total, offsets, separate, params):
    n_p = len(params)
    apart = [g for g in separate if g is not None]

    def body(*refs):
        total_ref, apart_refs = refs[0], list(refs[1:1 + len(apart)])
        ins, outs = refs[1 + len(apart):1 + len(apart) + 3 * n_p], refs[1 + len(apart) + 3 * n_p:]
        for i in range(n_p):
            w_ref, m_ref, v_ref = ins[3 * i:3 * i + 3]
            g = apart_refs.pop(0)[...] if offsets[i] is None else total_ref[:, offsets[i]:offsets[i] + w_ref.shape[1]]
            outs[4 * i][...] = g
            outs[4 * i + 1][...], outs[4 * i + 2][...], outs[4 * i + 3][...] = _adamw_math(w_ref[...], g, m_ref[...], v_ref[...])

    flat = [a for p in params for a in p]
    res = pl.pallas_call(
        body, name="adamw_vectors", out_shape=[SDS(p[0].shape, F32) for p in params for _ in range(4)],
        in_specs=[VMEM_SPEC] * (1 + len(apart) + len(flat)), out_specs=[VMEM_SPEC] * (4 * n_p),
    )(total, *apart, *flat)
    return [tuple(res[4 * i:4 * i + 4]) for i in range(n_p)]


def _blocks_to_pieces(w):
    nb, rows, n = w.shape
    q = rows // N_CHIPS
    return w.reshape(nb, N_CHIPS, q, n).transpose(1, 0, 2, 3).reshape(N_CHIPS, 2, nb // 2, q, n)


def _pieces_to_blocks(w):
    n_chips, _, half, q, n = w.shape
    return w.reshape(n_chips, 2 * half, q, n).transpose(1, 0, 2, 3).reshape(2 * half, n_chips * q, n)


def kernel(x, c, norm_mix_g, norm_mlp_g, w_ada, b_ada, w_in, conv_w, conv_b, w_rg_a, b_rg_a, w_rg_x, b_rg_x, a_param, w_branch_a, w_pool, b_pool, pool_scale, w_branch_b, w_out, w_up, w_down, final_g, loss_target, m_norm_mix_g, m_norm_mlp_g, m_w_ada, m_b_ada, m_w_in, m_conv_w, m_conv_b, m_w_rg_a, m_b_rg_a, m_w_rg_x, m_b_rg_x, m_a_param, m_w_branch_a, m_w_pool, m_b_pool, m_pool_scale, m_w_branch_b, m_w_out, m_w_up, m_w_down, m_final_g, v_norm_mix_g, v_norm_mlp_g, v_w_ada, v_b_ada, v_w_in, v_conv_w, v_conv_b, v_w_rg_a, v_b_rg_a, v_w_rg_x, v_b_rg_x, v_a_param, v_w_branch_a, v_w_pool, v_b_pool, v_pool_scale, v_w_branch_b, v_w_out, v_w_up, v_w_down, v_final_g):
    d = D_MODEL
    s_rows = x.shape[1]
    ts, tsq = min(512, s_rows), min(256, s_rows)
    xi, yi, ci = _coords()
    k_me, me = 2 * xi + yi, 4 * xi + 2 * yi + ci
    ada_cols = w_ada.shape[2]
    conv_cols = conv_w.shape[2]
    n_in, n_up = w_in.shape[2], w_up.shape[2]

    names = ("w_in", "w_up", "w_down", "w_a", "w_b", "w_out", "rg_a", "rg_x", "pool")
    mine = dict(zip(names, (w_in[0], w_up[0], w_down[0], w_branch_a[0], w_branch_b[0], w_out[0], w_rg_a[0], w_rg_x[0], w_pool[0])))
    moments_m = dict(zip(names, (m_w_in[0], m_w_up[0], m_w_down[0], m_w_branch_a[0], m_w_branch_b[0], m_w_out[0], m_w_rg_a[0], m_w_rg_x[0], m_w_pool[0])))
    moments_v = dict(zip(names, (v_w_in[0], v_w_up[0], v_w_down[0], v_w_branch_a[0], v_w_branch_b[0], v_w_out[0], v_w_rg_a[0], v_w_rg_x[0], v_w_pool[0])))
    block_weights, squares = ("rg_a", "rg_x", "pool"), ("w_a", "w_b", "w_out")
    place = jnp.stack([ci, k_me]).astype(jnp.int32)
    g_mix, g_mlp, g_fin = norm_mix_g, norm_mlp_g, final_g.reshape(1, d)
    x0, target = x[0], loss_target[0]
    wg = functools.partial(_weight_grad, tk=min(1024, d))
    plain, back, back_sum = (dict(transposed=t, reduce_blocks=r) for t, r in ((False, False), (True, False), (True, True)))
    rows_d, vec8, vec1 = _row_spec(ts, d), _vec_spec(8, d), _vec_spec(1, d)
    tw = min(1024, s_rows)
    halves_of_tile = 2

    def out_rows(cols, dtype, t=ts):
        return SDS((s_rows, cols), dtype), _row_spec(t, cols)

    sums_out = (SDS((8, d), F32), vec8)

    def halves(n):
        return mine[n].astype(BF16).reshape(2, -1, mine[n].shape[-1])

    def blocks(w):
        return _pieces_to_blocks(w.reshape(N_CHIPS, 2, N_BLOCKS // 2, BLOCK // N_CHIPS, BLOCK))

    act_all, mod_all, conv_all, w_in_whole, *gate_wholes = _prologue(
        c, conv_w[0], w_ada[0], lax.dynamic_slice_in_dim(b_ada, k_me * ada_cols, ada_cols, axis=1), halves("w_in"), [halves(n) for n in block_weights])
    conv_full = conv_all[0::2].transpose(1, 0, 2).reshape(CONV_WIDTH, d)
    mod_mine = lax.dynamic_index_in_dim(mod_all, me, axis=1, keepdims=False)[0::2]
    mod = jnp.concatenate([mod_mine.reshape(N_MOD, d), jnp.zeros((8 - N_MOD, d), F32)], axis=0)
    w_in_all = w_in_whole.reshape(N_CHIPS, d, n_in)
    vecs = jnp.concatenate([conv_b, b_rg_a, b_rg_x, a_param, b_pool, pool_scale, jnp.zeros((2, d), F32)], axis=0)

    def norm_first(a_ref, extra_refs, out_refs):
        h = _modulated_norm(a_ref[...], extra_refs[0], extra_refs[1], 1, 0)
        out_refs[1][...] = h
        return h

    stage = _Stage()
    i_blk = [_add_gather_d2d(stage, g) for g in gate_wholes]
    i_sq = [_add_gather_ici(stage, halves(n)) for n in squares]
    down_rows = mine["w_down"].shape[0] // 8
    i_down = _add_gather_ici(stage, halves("w_down"), rows=(0, down_rows))
    (proj, h1), got = _matmul(x0, w_in_all, name="proj_in", tm=ts, extra=[(mod, vec8), (g_mix, vec1)], prepare=norm_first,
                              outs=[out_rows(N_CHIPS * n_in, F32), out_rows(d, BF16)], finish=_store_blocks(n_in), parts=halves_of_tile,                              stage=stage, **plain)
    w_rg_a_all, w_rg_x_all, w_pool_all = (blocks(got[i]) for i in i_blk)
    stage = _Stage()
    i_sq = [_add_gather_d2d(stage, got[i]) for i in i_sq]
    i_up = _add_gather_ici(stage, halves("w_up"))
    i_down = _add_gather_ici(stage, halves("w_down"), rows=(down_rows, down_rows), whole=got[i_down])
    (xr, hr, ga, p, pooled), got = _seq_forward(proj, conv_full, vecs, w_rg_a_all, w_rg_x_all, w_pool_all, ts=tsq, stage=stage)
    w_a_all, w_b_all, w_out_all = (got[i].reshape(1, d, d) for i in i_sq)
    stage = _Stage()
    i_up = _add_gather_d2d(stage, got[i_up])
    i_down = _add_gather_ici(stage, halves("w_down"), rows=(2 * down_rows, 2 * down_rows), whole=got[i_down])
    (merged, s_a, s_b, t_a, t_b), got = _branches(ga, pooled, proj, w_a_all, w_b_all, ts=ts, stage=stage)
    w_up_all = got[i_up].reshape(N_CHIPS, d, n_up)
    stage = _Stage()
    i_down = _add_gather_d2d(stage, got[i_down])

    def residual_norm(k, acc, extra_refs, out_refs):
        x_ref, mod_ref, g_ref = extra_refs
        out_refs[0][...] = acc.astype(BF16)
        x2_t = x_ref[...] + mod_ref[2:3, :] * acc
        out_refs[1][...] = x2_t
        out_refs[2][...] = _modulated_norm(x2_t, mod_ref, g_ref, 4, 3)

    (mo, x2, h2), got = _matmul(merged, w_out_all, name="mix_out", tm=ts, extra=[(x0, rows_d), (mod, vec8), (g_mlp, vec1)],
                                outs=[out_rows(d, BF16), out_rows(d, F32), out_rows(d, BF16)], finish=residual_norm, parts=halves_of_tile,                                stage=stage, transposed=False, reduce_blocks=True)
    w_down_all = got[i_down].reshape(N_CHIPS, D_FF // N_CHIPS, d)

    def relu_squared(k, acc, extra_refs, out_refs):
        out_refs[0][:, k * n_up:(k + 1) * n_up] = jnp.square(jnp.maximum(acc, 0.0)).astype(BF16)

    ff, = _matmul(h2, w_up_all, name="mlp_up", tm=ts, outs=[out_rows(D_FF, BF16)], finish=relu_squared, parts=halves_of_tile, **plain)

    def loss_head(k, acc, extra_refs, out_refs):
        x2_ref, mod_ref, g_ref, t_ref = extra_refs
        dx_ref, df_ref, acc_ref = out_refs
        xh, r = _rms(x2_ref[...] + mod_ref[5:6, :] * acc)
        err = xh * g_ref[...] - t_ref[...]
        dy = err / d
        dxh = dy * g_ref[...]
        dx3_t = r * (dxh - xh * jnp.mean(dxh * xh, axis=-1, keepdims=True))
        dx_ref[...] = dx3_t
        df_ref[...] = (dx3_t * mod_ref[5:6, :]).astype(BF16)
        _accumulate(acc_ref, 0, dy * xh)
        _accumulate(acc_ref, 1, dx3_t * acc)
        _accumulate(acc_ref, 2, jnp.square(err) * (0.5 / d))

    dx3, dffo, sums_head = _matmul(ff, w_down_all, name="mlp_down", tm=ts, extra=[(x2, rows_d), (mod, vec8), (g_fin, vec1), (target, rows_d)],
                                   outs=[out_rows(d, F32), out_rows(d, BF16), sums_out], finish=loss_head, zero_first=(2,), parts=halves_of_tile,                                   transposed=False, reduce_blocks=True)

    partial_of, from_sibling, chip_sum, from_chips, half_done, quarter = {}, {}, {}, {}, {}, {}

    def publish(n, g):
        g = _blocks_to_pieces(g).astype(BF16) if n in block_weights else g
        partial_of[n] = g.reshape(N_CHIPS, 2, -1, mine[n].shape[-1])

    def exchange(to_sibling=(), to_chips=(), swap=()):
        st = _Stage()
        slots = [(n, from_sibling, _add_reduce_d2d(st, partial_of[n])) for n in to_sibling]
        slots += [(n, from_chips, _add_reduce_ici(st, chip_sum[n])) for n in to_chips]
        slots += [(n, quarter, _add_swap_halves(st, half_done[n])) for n in swap]
        return st, slots

    def collect(slots, outs):
        for n, where, i in slots:
            where[n] = outs[i]

    def sum_pairs(*ns):
        for n in ns:
            chip_sum[n] = _sum_pair(partial_of[n], from_sibling[n], place, "sum_pair_" + n)

    def sum_quarters(*ns):
        for n in ns:
            half_done[n] = _sum_quarter(partial_of[n], from_sibling[n], from_chips[n], place, "sum_quarter_" + n)

    publish("w_down", wg(ff, dffo, nb=1, name="grad_w_down"))

    def relu_squared_backward(k, acc, extra_refs, out_refs):
        cols = slice(k * n_up, (k + 1) * n_up)
        out_refs[0][:, cols] = (acc * (2.0 * jnp.sqrt(extra_refs[0][:, cols].astype(F32)))).astype(BF16)

    stage, slots = exchange(to_sibling=["w_down"])
    (dup,), got = _matmul(dffo, w_down_all, name="d_mlp_down", tm=ts, extra=[(ff, _row_spec(ts, D_FF))], outs=[out_rows(D_FF, BF16)],
                          finish=relu_squared_backward, parts=halves_of_tile, stage=stage, **back)
    collect(slots, got)
    sum_pairs("w_down")

    def norm_mlp_backward(k, acc, extra_refs, out_refs):
        x2_ref, dres_ref, mod_ref, g_ref, mo_ref = extra_refs
        dx_ref, dmo_ref, acc_ref = out_refs
        dx =_modulated_norm_backward(acc, x2_ref[...], dres_ref[...], mod_ref, g_ref, acc_ref, 4)
        dx_ref[...] = dx
        dmo_ref[...] = (dx * mod_ref[2:3, :]).astype(BF16)
        _accumulate(acc_ref, 3, dx * mo_ref[...].astype(F32))

    stage, slots = exchange(to_chips=["w_down"])
    (dx2, dmo, sums_mlp), got = _matmul(dup, w_up_all, name="d_mlp_up", tm=ts, stage=stage, finish=norm_mlp_backward,
                                        extra=[(x2, rows_d), (dx3, rows_d), (mod, vec8), (g_mlp, vec1), (mo, rows_d)],
                                        outs=[out_rows(d, F32), out_rows(d, BF16), sums_out], zero_first=(2,), parts=halves_of_tile, **back_sum)
    collect(slots, got)
    sum_quarters("w_down")
    publish("w_up", wg(h2, dup, nb=N_CHIPS, name="grad_w_up"))
    publish("w_out", wg(merged, dmo, nb=1, name="grad_w_out"))

    def merge_backward(k, acc, extra_refs, out_refs):
        sa_ref, sb_ref, ta_ref, tb_ref = extra_refs
        out_refs[0][...] = (acc * sa_ref[...].astype(F32)).astype(BF16)
        out_refs[1][...] = (acc * sb_ref[...].astype(F32)).astype(BF16)
        out_refs[2][:, 0:d] = (acc * ta_ref[...].astype(F32)).astype(BF16)
        out_refs[2][:, d:2 * d] = (acc * tb_ref[...].astype(F32)).astype(BF16)

    stage, slots = exchange(to_sibling=["w_up", "w_out"], swap=["w_down"])
    (dbr_a, dbr_b, dgab), got = _matmul(dmo, w_out_all, name="d_mix_out", tm=ts, stage=stage, finish=merge_backward,
                                        extra=[(s_a, rows_d), (s_b, rows_d), (t_a, rows_d), (t_b, rows_d)],
                                        outs=[out_rows(d, BF16), out_rows(d, BF16), out_rows(2 * d, BF16)], parts=halves_of_tile, **back_sum)
    collect(slots, got)
    publish("w_a", wg(ga, dbr_a, nb=1, name="grad_w_branch_a"))
    publish("w_b", wg(pooled, dbr_b, nb=1, name="grad_w_branch_b"))
    dga, = _matmul(dbr_a, w_a_all, name="d_branch_a", tm=tw, outs=[out_rows(d, F32, tw)], finish=_store_blocks(d), **back_sum)
    dpooled, = _matmul(dbr_b, w_b_all, name="d_branch_b", tm=tw, outs=[out_rows(d, F32, tw)], finish=_store_blocks(d), **back_sum)
    sum_pairs("w_up", "w_out")
    stage, slots = exchange(to_sibling=["w_a", "w_b"], to_chips=["w_up", "w_out"])
    (dproj, sums_seq, d_rg_a, d_rg_x, d_pool), got = _seq_backward_by_block(proj, xr, hr, p, dga, dpooled, dgab, conv_full, vecs,
                                                                            w_rg_a_all, w_rg_x_all, w_pool_all, ts=tsq, stage=stage)
    collect(slots, got)
    sum_quarters("w_up", "w_out")
    sum_pairs("w_a", "w_b")
    for n, g in zip(block_weights, (d_rg_a, d_rg_x, d_pool)):
        publish(n, g)
    stage, slots = exchange(to_sibling=block_weights, to_chips=["w_a", "w_b"], swap=["w_up", "w_out"])
    g_in, got = wg(h1, dproj, nb=N_CHIPS, name="grad_w_in", stage=stage)
    collect(slots, got)
    publish("w_in", g_in)
    sum_pairs(*block_weights)
    sum_quarters("w_a", "w_b")
    stage, slots = exchange(to_sibling=["w_in"], to_chips=block_weights)
    collect(slots, _run_stage(stage, "exchange_w_in_gates"))
    sum_pairs("w_in")
    sum_quarters(*block_weights)

    def norm_mix_backward(k, acc, extra_refs, out_refs):
        x_ref, dres_ref, mod_ref, g_ref = extra_refs
        out_refs[0][...] = _modulated_norm_backward(acc, x_ref[...], dres_ref[...], mod_ref, g_ref, out_refs[1], 1)

    stage, slots = exchange(to_chips=["w_in"], swap=block_weights + ("w_a", "w_b"))
    (grad_x, sums_mix), got = _matmul(dproj, w_in_all, name="d_proj_in", tm=ts, stage=stage, finish=norm_mix_backward,
                                      extra=[(x0, rows_d), (dx2, rows_d), (mod, vec8), (g_mix, vec1)],
                                      outs=[out_rows(d, F32), sums_out], zero_first=(1,), parts=halves_of_tile, **back_sum)
    collect(slots, got)
    sum_quarters("w_in")

    dmod = jnp.concatenate([sums_mix[1:2], sums_mix[0:1], sums_mlp[3:4], sums_mlp[1:2], sums_mlp[0:1], sums_head[1:2]], axis=1)
    row = jnp.concatenate([sums_mix[2:3], sums_mlp[2:3], sums_seq[4:5], sums_seq[5:6], sums_seq[6:7], sums_seq[7:8], sums_seq[8:9],
                           sums_seq[9:10], sums_head[0:1], sums_seq[0:CONV_WIDTH].reshape(1, CONV_WIDTH * d), dmod, sums_head[2:3]], axis=1)
    n_vec, n_conv, n_mod = 9 * d, CONV_WIDTH * d, N_MOD * d
    stage, slots = exchange(swap=["w_in"])
    i_rows = _add_allgather8(stage, row)
    got = _run_stage(stage, "exchange_sums_w_in")
    collect(slots, got)
    rows_all = got[i_rows]
    total = _sum_rows(rows_all, d)
    loss = total[0, n_vec + n_conv + n_mod]
    dmod_all = rows_all[:, 0, n_vec + n_conv:n_vec + n_conv + n_mod]
    g_ada, d_ada, m_ada, v_ada = _ada_backward(act_all.T, lax.dynamic_slice_in_dim(dmod_all, k_me * ada_cols, ada_cols, axis=1),
                                               w_ada[0], m_w_ada[0], v_w_ada[0])
    g_conv = lax.dynamic_slice_in_dim(total[:, n_vec:n_vec + n_conv].reshape(CONV_WIDTH, d), k_me * conv_cols, conv_cols, axis=1)
    vec_names = ("norm_mix_g", "norm_mlp_g", "conv_b", "b_rg_a", "b_rg_x", "a_param", "b_pool", "pool_scale", "final_g", "conv_w", "b_ada")
    vec_params = [(norm_mix_g, m_norm_mix_g, v_norm_mix_g), (norm_mlp_g, m_norm_mlp_g, v_norm_mlp_g), (conv_b, m_conv_b, v_conv_b),
                  (b_rg_a, m_b_rg_a, v_b_rg_a), (b_rg_x, m_b_rg_x, v_b_rg_x), (a_param, m_a_param, v_a_param), (b_pool, m_b_pool, v_b_pool),
                  (pool_scale, m_pool_scale, v_pool_scale), tuple(a.reshape(1, d) for a in (final_g, m_final_g, v_final_g)),
                  (conv_w[0], m_conv_w[0], v_conv_w[0]), (b_ada, m_b_ada, v_b_ada)]
    offsets = [i * d for i in range(9)] + [None, n_vec + n_conv]
    vec_res = _adamw_vectors(total, offsets, [None] * 9 + [g_conv, None], vec_params)
    vec_out = {n: [r.reshape(shape) for r in res] for n, res, shape in zip(
        vec_names, vec_res, [(1, d)] * 8 + [(d,), (1, CONV_WIDTH, conv_cols), (1, N_MOD * d)])}

    big_out = {}
    for n in names:
        shape2 = (-1, mine[n].shape[-1])
        g2 = quarter[n].reshape(shape2)
        res = (g2,) + tuple(_adamw(mine[n].reshape(shape2), g2, moments_m[n].reshape(shape2), moments_v[n].reshape(shape2), "adamw_" + n))
        big_out[n] = [r.reshape((1,) + mine[n].shape) for r in res]

    key = {"w_in": "w_in", "w_rg_a": "rg_a", "w_rg_x": "rg_x", "w_branch_a": "w_a", "w_pool": "pool", "w_branch_b": "w_b", "w_out": "w_out",
           "w_up": "w_up", "w_down": "w_down"}
    order = ("norm_mix_g", "norm_mlp_g", "w_ada", "b_ada", "w_in", "conv_w", "conv_b", "w_rg_a", "b_rg_a", "w_rg_x", "b_rg_x", "a_param",
             "w_branch_a", "w_pool", "b_pool", "pool_scale", "w_branch_b", "w_out", "w_up", "w_down", "final_g")
    ada_out = [g_ada[None], d_ada[None], m_ada[None], v_ada[None]]
    outs = [loss, grad_x[None]]
    for kind in range(4):
        for nme in order:
            outs.append(ada_out[kind] if nme == "w_ada" else big_out[key[nme]][kind] if nme in key else vec_out[nme][kind])
    return tuple(outs)
```

```python
import functools

import jax
import jax.numpy as jnp
from jax import lax
from jax.experimental import pallas as pl
from jax.experimental.pallas import tpu as pltpu

F32, BF16 = jnp.float32, jnp.bfloat16
D_MODEL = 1024
D_FF = 4 * D_MODEL
N_BLOCKS = 4
BLOCK = D_MODEL // N_BLOCKS
CONV_WIDTH = 4
POOL_WINDOWS = (2, 4, 8, 16)
CONV_HALO = 8
POOL_HALO = 16
N_MOD = 6
EPS = 1e-6
C_RG = 8.0
ADAM_LR, ADAM_B1, ADAM_B2, ADAM_EPS, ADAM_WD, ADAM_STEP = 0.001, 0.9, 0.999, 1e-08, 0.01, 10
N_DEV = 8
N_CHIPS = 4
VMEM_LIMIT_V7X = 56 * 2**20
MESH = pl.DeviceIdType.MESH
SDS = jax.ShapeDtypeStruct
HBM_SPEC = pl.BlockSpec(memory_space=pltpu.HBM)
VMEM_SPEC = pl.BlockSpec(memory_space=pltpu.VMEM)


def _params(*semantics):
    return pltpu.CompilerParams(dimension_semantics=semantics, vmem_limit_bytes=VMEM_LIMIT_V7X)


def _coords():
    return lax.axis_index("x"), lax.axis_index("y"), lax.axis_index("c")


def _flip(v, bit):
    return 1 - v if bit else v


def _chips():
    x, y, _ = _coords()
    return [(1 - x, y), (x, 1 - y), (1 - x, 1 - y)]


PEER_SETS = [frozenset(s) for s in (("sibling",), ("chips",), ("sibling", "chips"), ("sibling", "chips", "others"))]


class _Stage:
    def __init__(self):
        self.inputs, self.in_specs, self.outputs, self.out_specs, self.aliases = [], [], [], [], {}
        self.parts, self.n_copies, self.peers = [], 0, set()

    def add(self, inputs, in_spec, outputs, out_spec, n_copies, build, peers, alias=False):
        self.peers |= set(peers)
        i0, o0 = len(self.inputs), len(self.outputs)
        self.inputs += inputs
        self.in_specs += [in_spec] * len(inputs)
        self.outputs += outputs
        self.out_specs += [out_spec] * len(outputs)
        if alias:
            self.aliases.update({i0 + i: o0 + i for i in range(len(inputs))})
        self.parts.append((build, i0, len(inputs), o0, len(outputs)))
        self.n_copies += n_copies
        return list(range(o0, o0 + len(outputs)))

    def copies(self, in_refs, out_refs):
        out = []
        for build, i0, ni, o0, no in self.parts:
            out += build(in_refs[i0:i0 + ni], out_refs[o0:o0 + no])
        assert len(out) == self.n_copies
        return out

    def run(self, in_refs, out_refs, send_sems, recv_sems, start):
        x, y, c = _coords()
        for s, (src, dst, to, landing) in enumerate(self.copies(in_refs, out_refs)):
            if to is None:
                cp = pltpu.make_async_copy(src, dst, send_sems.at[s])
                cp.start() if start else cp.wait()
                continue
            cp = pltpu.make_async_remote_copy(src_ref=src, dst_ref=dst, send_sem=send_sems.at[s], recv_sem=recv_sems.at[s],
                                              device_id=to, device_id_type=MESH)
            if start:
                cp.start()
            else:
                pltpu.make_async_remote_copy(src_ref=landing, dst_ref=landing, send_sem=send_sems.at[s], recv_sem=recv_sems.at[s],
                                             device_id=(x, y, c), device_id_type=MESH).wait_recv()
                cp.wait_send()


def _hosted(stage, body, *, name, in_specs, out_specs, out_shape, grid=(), scratch_shapes=(), compiler_params=None):
    if stage is None:
        return pl.pallas_call(body, name=name, grid=grid, in_specs=in_specs, out_specs=out_specs, out_shape=out_shape,
                              scratch_shapes=list(scratch_shapes), compiler_params=compiler_params)
    single = not isinstance(out_shape, (list, tuple))
    h_out_shape = [out_shape] if single else list(out_shape)
    h_out_specs = [out_specs] if single else list(out_specs)
    n_in, n_out, n_scr = len(in_specs), len(h_out_shape), len(scratch_shapes)
    s_in, s_out = len(stage.inputs), len(stage.outputs)

    def wrapped(*refs):
        h_in, st_in = refs[:n_in], refs[n_in:n_in + s_in]
        h_o, st_o = refs[n_in + s_in:n_in + s_in + n_out], refs[n_in + s_in + n_out:n_in + s_in + n_out + s_out]
        h_scr = refs[n_in + s_in + n_out + s_out:n_in + s_in + n_out + s_out + n_scr]
        send_sems, recv_sems = refs[n_in + s_in + n_out + s_out + n_scr:]
        def begin():
            x, y, c = _coords()
            peers = [(x, y, 1 - c)] * ("sibling" in stage.peers)
            peers += [(px, py, c) for px, py in _chips()] * ("chips" in stage.peers)
            peers += [(px, py, 1 - c) for px, py in _chips()] * ("others" in stage.peers)
            barrier = pltpu.get_barrier_semaphore()
            for peer in peers:
                pl.semaphore_signal(barrier, inc=1, device_id=peer, device_id_type=MESH)
            pl.semaphore_wait(barrier, len(peers))
            stage.run(st_in, st_o, send_sems, recv_sems, True)

        if not grid:
            begin()
            if body is not None:
                body(*h_in, *h_o, *h_scr)
            stage.run(st_in, st_o, send_sems, recv_sems, False)
            return
        ids = [pl.program_id(a) for a in range(len(grid))]
        first = functools.reduce(jnp.logical_and, [i == 0 for i in ids])
        last = functools.reduce(jnp.logical_and, [i == g - 1 for i, g in zip(ids, grid)])
        pl.when(first)(begin)
        body(*h_in, *h_o, *h_scr)
        pl.when(last)(lambda: stage.run(st_in, st_o, send_sems, recv_sems, False))

    call = pl.pallas_call(
        wrapped, name=name, grid=grid, in_specs=list(in_specs) + stage.in_specs, out_specs=h_out_specs + stage.out_specs,
        out_shape=h_out_shape + stage.outputs, input_output_aliases={n_in + i: n_out + o for i, o in stage.aliases.items()},
        scratch_shapes=list(scratch_shapes) + [pltpu.SemaphoreType.DMA((stage.n_copies,)), pltpu.SemaphoreType.DMA((stage.n_copies,))],
        compiler_params=pltpu.CompilerParams(dimension_semantics=("arbitrary",) * len(grid), vmem_limit_bytes=VMEM_LIMIT_V7X,
                                             collective_id=PEER_SETS.index(frozenset(stage.peers))),
    )

    def run(*operands):
        outs = call(*operands, *stage.inputs)
        host = outs[:n_out]
        return (host[0] if single else host), outs[n_out:]

    return run


def _run_stage(stage, name):
    return _hosted(stage, None, name=name, in_specs=[], out_specs=[], out_shape=[])()[1]


def _add_allgather8(stage, v):
    def build(ins, outs):
        x, y, c = _coords()
        me = 4 * x + 2 * y + c
        copies = [(ins[0], outs[0].at[me], None, None)]
        for r in range(1, N_DEV):
            peer = (_flip(x, r & 4), _flip(y, r & 2), _flip(c, r & 1))
            copies.append((ins[0], outs[0].at[me], peer, outs[0].at[me ^ r]))
        return copies

    return stage.add([v], VMEM_SPEC, [SDS((N_DEV,) + v.shape, v.dtype)], VMEM_SPEC, N_DEV, build, ("sibling", "chips", "others"))[0]


def _add_gather_ici(stage, shard, rows=None, whole=None):
    first, count = rows or (0, shard.shape[1])

    def build(ins, outs):
        x, y, c = _coords()
        k_me = 2 * x + y
        part = pl.ds(first, count)
        copies = [] if whole is not None else [(ins[0], outs[0].at[k_me], (x, y, 1 - c), outs[0].at[k_me])]
        for px, py in _chips():
            copies.append((ins[0].at[c, part], outs[0].at[k_me, c, part], (px, py, c), outs[0].at[2 * px + py, c, part]))
        return copies

    if whole is None:
        return stage.add([shard], HBM_SPEC, [SDS((N_CHIPS,) + shard.shape, shard.dtype)], HBM_SPEC, N_CHIPS, build, ("sibling", "chips"))[0]

    def build_into(ins, outs):
        return build(ins[1:], outs)

    i0 = len(stage.inputs)
    out = stage.add([whole, shard], HBM_SPEC, [SDS(whole.shape, whole.dtype)], HBM_SPEC, N_CHIPS - 1, build_into, ("chips",))[0]
    stage.aliases[i0] = out
    return out


def _add_gather_d2d(stage, whole):
    def build(ins, outs):
        x, y, c = _coords()
        return [(outs[0].at[2 * px + py, c], outs[0].at[2 * px + py, c], (x, y, 1 - c), outs[0].at[2 * px + py, 1 - c]) for px, py in _chips()]

    return stage.add([whole], HBM_SPEC, [SDS(whole.shape, whole.dtype)], HBM_SPEC, N_CHIPS - 1, build, ("sibling",), alias=True)[0]


def _add_reduce_d2d(stage, grads):
    def build(ins, outs):
        x, y, c = _coords()
        return [(ins[0].at[k, 1 - c], outs[0].at[k], (x, y, 1 - c), outs[0].at[k]) for k in range(N_CHIPS)]

    return stage.add([grads], HBM_SPEC, [SDS((N_CHIPS,) + grads.shape[2:], grads.dtype)], HBM_SPEC, N_CHIPS, build, ("sibling",))[0]


def _add_reduce_ici(stage, partial):
    def build(ins, outs):
        x, y, c = _coords()
        return [(ins[0].at[2 * px + py], outs[0].at[j], (px, py, c), outs[0].at[j]) for j, (px, py) in enumerate(_chips())]

    return stage.add([partial], HBM_SPEC, [SDS((N_CHIPS - 1,) + partial.shape[1:], partial.dtype)], HBM_SPEC, N_CHIPS - 1, build, ("chips",))[0]


def _add_swap_halves(stage, quarter):
    def build(ins, outs):
        x, y, c = _coords()
        return [(outs[0].at[c], outs[0].at[c], (x, y, 1 - c), outs[0].at[1 - c])]

    return stage.add([quarter], HBM_SPEC, [SDS(quarter.shape, quarter.dtype)], HBM_SPEC, 1, build, ("sibling",), alias=True)[0]


def _sum_pair(grads, from_sibling, place, name):
    _, _, rows, cols = grads.shape
    tr = min(rows, 512)

    def body(place_ref, g_ref, s_ref, o_ref):
        o_ref[0] = (g_ref[0, 0].astype(F32) + s_ref[0].astype(F32)).astype(BF16)

    spec = pl.BlockSpec((1, tr, cols), lambda k, i, place_ref: (k, i, 0))
    return pl.pallas_call(
        body, name=name, out_shape=SDS((N_CHIPS, rows, cols), BF16),
        grid_spec=pltpu.PrefetchScalarGridSpec(
            num_scalar_prefetch=1, grid=(N_CHIPS, rows // tr),
            in_specs=[pl.BlockSpec((1, 1, tr, cols), lambda k, i, place_ref: (k, place_ref[0], i, 0)), spec], out_specs=spec),
        compiler_params=_params("parallel", "parallel"),
    )(place, grads, from_sibling)


def _sum_quarter(grads, from_sibling, from_chips, place, name):
    _, _, rows, cols = grads.shape
    tr = min(rows, 512)

    def body(place_ref, g_ref, s_ref, c_ref, o_ref):
        acc = g_ref[0, 0].astype(F32) + s_ref[0].astype(F32)
        for j in range(N_CHIPS - 1):
            acc = acc + c_ref[j].astype(F32)
        o_ref[0] = acc

    return pl.pallas_call(
        body, name=name, out_shape=SDS((2, rows, cols), F32),
        grid_spec=pltpu.PrefetchScalarGridSpec(
            num_scalar_prefetch=1, grid=(rows // tr,),
            in_specs=[pl.BlockSpec((1, 1, tr, cols), lambda i, place_ref: (place_ref[1], place_ref[0], i, 0)),
                      pl.BlockSpec((1, tr, cols), lambda i, place_ref: (place_ref[1], i, 0)),
                      pl.BlockSpec((N_CHIPS - 1, tr, cols), lambda i, place_ref: (0, i, 0))],
            out_specs=pl.BlockSpec((1, tr, cols), lambda i, place_ref: (place_ref[0], i, 0))),
        compiler_params=_params("parallel"),
    )(place, grads, from_sibling, from_chips)


def _row_spec(ts, cols, col_block=0):
    return pl.BlockSpec((ts, cols), lambda *g: (g[0], col_block))


def _vec_spec(rows, cols):
    return pl.BlockSpec((rows, cols), lambda *g: (0, 0))


def _matmul(a, w, *, transposed, reduce_blocks, name, tm, outs, finish, extra=(), prepare=None, zero_first=(), parts=1, stage=None):
    m_rows = a.shape[0]
    nb, r, c = w.shape
    kb = c if transposed else r
    dims = (((1,), (1,)), ((), ())) if transposed else (((1,), (0,)), ((), ()))
    assert a.shape[1] == (nb * kb if reduce_blocks else kb) and m_rows % tm == 0 and tm % parts == 0
    n_extra, sub = len(extra), tm // parts

    def body(a_ref, w_ref, *rest):
        extra_refs, out_refs = rest[:n_extra], rest[n_extra:]
        if zero_first:
            @pl.when(pl.program_id(0) == 0)
            def _():
                for i in zero_first:
                    out_refs[i][...] = jnp.zeros_like(out_refs[i])

        def views(j):
            rows = pl.ds(j * sub, sub)
            return rows, [ref.at[rows] if ref.shape[0] == tm else ref for ref in extra_refs], [ref.at[rows] if ref.shape[0] == tm else ref for ref in out_refs]

        def product(j):
            rows, ex, ou = views(j)
            if reduce_blocks:
                acc = lax.dot_general(a_ref[rows, 0:kb], w_ref[0], dims, preferred_element_type=F32)
                for k in range(1, nb):
                    acc += lax.dot_general(a_ref[rows, k * kb:(k + 1) * kb], w_ref[k], dims, preferred_element_type=F32)
                return [(None, acc)]
            lhs = a_ref[rows, :] if prepare is None else prepare(a_ref.at[rows], ex, ou)
            return [(k, lax.dot_general(lhs, w_ref[k], dims, preferred_element_type=F32)) for k in range(nb)]

        pending = product(0)
        for j in range(parts):
            ahead = product(j + 1) if j + 1 < parts else None
            _, ex, ou = views(j)
            for k, acc in pending:
                finish(k, acc, ex, ou)
            pending = ahead

    return _hosted(
        stage, body, name=name, grid=(m_rows // tm,), out_shape=[s for s, _ in outs], out_specs=[s for _, s in outs],
        in_specs=[_row_spec(tm, a.shape[1]), pl.BlockSpec((nb, r, c), lambda *g: (0, 0, 0), pipeline_mode=pl.Buffered(1))] + [s for _, s in extra],
        compiler_params=_params("arbitrary"),
    )(a, w, *[e for e, _ in extra])


def _store_blocks(n):
    def finish(k, acc, extra_refs, out_refs):
        if k is None:
            out_refs[0][...] = acc.astype(out_refs[0].dtype)
        else:
            out_refs[0][:, k * n:(k + 1) * n] = acc.astype(out_refs[0].dtype)
    return finish


def _weight_grad(a, b, *, nb, name, tk, stage=None):
    s_rows, k1 = a.shape
    tn = b.shape[1] // nb
    assert k1 % tk == 0 and b.shape[1] % nb == 0

    def body(a_ref, b_ref, o_ref):
        o_ref[0] = lax.dot_general(a_ref[...], b_ref[...], (((0,), (0,)), ((), ())), preferred_element_type=F32).astype(o_ref.dtype)

    return _hosted(
        stage, body, name=name, grid=(nb, k1 // tk), out_shape=SDS((nb, k1, tn), BF16),
        in_specs=[pl.BlockSpec((s_rows, tk), lambda n, i: (0, i)), pl.BlockSpec((s_rows, tn), lambda n, i: (0, n))],
        out_specs=pl.BlockSpec((1, tk, tn), lambda n, i: (n, i, 0)), compiler_params=_params("parallel", "parallel"),
    )(a, b)


def _rms(x):
    r = lax.rsqrt(jnp.mean(x * x, axis=-1, keepdims=True) + EPS)
    return x * r, r


def _accumulate(acc_ref, row, value):
    acc_ref[row:row + 1, :] += jnp.sum(value, axis=0, keepdims=True)


def _gelu_parts(y):
    k0, k1 = 0.7978845608028654, 0.044715
    th = jnp.tanh(k0 * (y + k1 * (y * y * y)))
    gelu = 0.5 * y * (1.0 + th)
    dgelu = 0.5 * (1.0 + th) + 0.5 * y * (1.0 - th * th) * (k0 * (1.0 + 3.0 * k1 * (y * y)))
    return gelu, dgelu


def _one_minus_square(a, log_a):
    return -jnp.tanh(log_a) * (1.0 + a * a)


def _modulated_norm(x, mod_ref, g_ref, sc_row, sh_row):
    xh, _ = _rms(x)
    return ((xh * g_ref[...]) * (1.0 + mod_ref[sc_row:sc_row + 1, :]) + mod_ref[sh_row:sh_row + 1, :]).astype(BF16)


def _modulated_norm_backward(dh, x, dres, mod_ref, g_ref, acc_ref, sc_row):
    xh, r = _rms(x)
    _accumulate(acc_ref, 0, dh * (xh * g_ref[...]))
    _accumulate(acc_ref, 1, dh)
    dn = dh * (1.0 + mod_ref[sc_row:sc_row + 1, :])
    _accumulate(acc_ref, 2, dn * xh)
    dxh = dn * g_ref[...]
    return dres + r * (dxh - xh * jnp.mean(dxh * xh, axis=-1, keepdims=True))


def _seq_forward(proj, conv_w, vecs, w_a, w_x, w_p, *, ts, stage=None):
    s_rows = proj.shape[0]
    d = D_MODEL

    def body(x_ref, xh_ref, y_ref, u_ref, uh_ref, cw_ref, vec_ref, wa_ref, wx_ref, wp_ref,
             xr_o, hr_o, ga_o, p_o, pooled_o, carry, a_scr, b_scr):
        t = pl.program_id(0)

        @pl.when(t == 0)
        def _():
            carry[...] = jnp.zeros_like(carry)

        x = x_ref[...]
        halo = jnp.where(t == 0, 0.0, xh_ref[...])
        xx = jnp.concatenate([halo, x], axis=0)
        xr = vec_ref[0:1, :] + x * cw_ref[CONV_WIDTH - 1:CONV_WIDTH, :]
        for j in range(CONV_WIDTH - 1):
            xr = xr + pltpu.roll(xx, CONV_WIDTH - 1 - j, 0)[CONV_HALO:] * cw_ref[j:j + 1, :]
        xr_o[...] = xr
        xrb = xr.astype(BF16)
        zr = jnp.concatenate([jnp.dot(xrb[:, g * BLOCK:(g + 1) * BLOCK], wa_ref[g], preferred_element_type=F32)
                              for g in range(N_BLOCKS)], axis=1) + vec_ref[1:2, :]
        zi = jnp.concatenate([jnp.dot(xrb[:, g * BLOCK:(g + 1) * BLOCK], wx_ref[g], preferred_element_type=F32)
                              for g in range(N_BLOCKS)], axis=1) + vec_ref[2:3, :]
        r = jax.nn.sigmoid(zr)
        gate_i = jax.nn.sigmoid(zi)
        log_a = (-C_RG * r) * jax.nn.softplus(vec_ref[3:4, :])
        rows = lax.broadcasted_iota(jnp.int32, (ts, d), 0)
        a = jnp.exp(log_a)
        mult = jnp.where((rows == 0) & (t == 0), 1.0, jnp.sqrt(_one_minus_square(a, log_a)))
        a_scr[...] = a
        b_scr[...] = xr * gate_i * mult

        sub = lax.broadcasted_iota(jnp.int32, (8, d), 0)

        def chunk(i, h):
            a = a_scr[pl.ds(i * 8, 8), :]
            b = b_scr[pl.ds(i * 8, 8), :]
            for s in (1, 2, 4):
                keep = sub >= s
                b = jnp.where(keep, a * pltpu.roll(b, s, 0) + b, b)
                a = jnp.where(keep, a * pltpu.roll(a, s, 0), a)
            hh = a * h + b
            hr_o[pl.ds(i * 8, 8), :] = hh
            return hh[7:8, :]

        carry[...] = lax.fori_loop(0, ts // 8, chunk, carry[...])
        gelu, _ = _gelu_parts(y_ref[...])
        ga_o[...] = (gelu * hr_o[...]).astype(BF16)

        u = u_ref[...]
        uu = jnp.concatenate([jnp.where(t == 0, 0.0, uh_ref[...]), u], axis=0)
        pos = (rows[:, :BLOCK] + t * ts + 1).astype(F32)
        sums, have, parts = uu, 1, []
        for g, win in enumerate(POOL_WINDOWS):
            while have < win:
                sums = sums + pltpu.roll(sums, have, 0)
                have *= 2
            mean = sums[POOL_HALO:, g * BLOCK:(g + 1) * BLOCK] * (1.0 / jnp.minimum(pos, float(win)))
            parts.append(mean - u[:, g * BLOCK:(g + 1) * BLOCK])
        pb = jnp.concatenate(parts, axis=1).astype(BF16)
        p_o[...] = pb
        mixed = jnp.concatenate([jnp.dot(pb[:, g * BLOCK:(g + 1) * BLOCK], wp_ref[g], preferred_element_type=F32)
                                 for g in range(N_BLOCKS)], axis=1) + vec_ref[4:5, :]
        pooled_o[...] = (mixed * vec_ref[5:6, :]).astype(BF16)

    halo_x = pl.BlockSpec((CONV_HALO, d), lambda t: (jnp.maximum(t * (ts // CONV_HALO) - 1, 0), 0))
    halo_u = pl.BlockSpec((POOL_HALO, d), lambda t: (jnp.maximum(t * (ts // POOL_HALO) - 1, 0), 2))
    wspec = pl.BlockSpec((N_BLOCKS, BLOCK, BLOCK), lambda t: (0, 0, 0))
    return _hosted(
        stage, body, name="seq_forward", grid=(s_rows // ts,),
        out_shape=[SDS((s_rows, d), F32), SDS((s_rows, d), F32), SDS((s_rows, d), BF16), SDS((s_rows, d), BF16), SDS((s_rows, d), BF16)],
        in_specs=[_row_spec(ts, d, 0), halo_x, _row_spec(ts, d, 1), _row_spec(ts, d, 2), halo_u, _vec_spec(CONV_WIDTH, d), _vec_spec(8, d),
                  wspec, wspec, wspec],
        out_specs=[_row_spec(ts, d)] * 5,
        scratch_shapes=[pltpu.VMEM((1, d), F32), pltpu.VMEM((ts, d), F32), pltpu.VMEM((ts, d), F32)],
        compiler_params=_params("arbitrary"),
    )(proj, proj, proj, proj, proj, conv_w, vecs, w_a, w_x, w_p)


def _branches(ga, pooled, proj, w_a, w_b, *, ts, stage=None):
    s_rows, d = ga.shape

    def body(a_ref, p_ref, ga_ref, gb_ref, wa_ref, wb_ref, merged_o, sa_o, sb_o, ta_o, tb_o):
        bra = jnp.dot(a_ref[...], wa_ref[0], preferred_element_type=F32)
        brb = jnp.dot(p_ref[...], wb_ref[0], preferred_element_type=F32)
        sa, sb = jax.nn.sigmoid(ga_ref[...]), jax.nn.sigmoid(gb_ref[...])
        merged_o[...] = (sa * bra + sb * brb).astype(BF16)
        sa_o[...] = sa.astype(BF16)
        sb_o[...] = sb.astype(BF16)
        ta_o[...] = (bra * (sa * (1.0 - sa))).astype(BF16)
        tb_o[...] = (brb * (sb * (1.0 - sb))).astype(BF16)

    rows = _row_spec(ts, d)
    wspec = pl.BlockSpec((1, d, d), lambda *g: (0, 0, 0), pipeline_mode=pl.Buffered(1))
    return _hosted(
        stage, body, name="branches", grid=(s_rows // ts,), out_shape=[SDS((s_rows, d), BF16)] * 5,
        in_specs=[rows, rows, _row_spec(ts, d, 3), _row_spec(ts, d, 4), wspec, wspec], out_specs=[rows] * 5,
        compiler_params=_params("arbitrary"),
    )(ga, pooled, proj, proj, w_a, w_b)


def _seq_backward_by_block(proj, xr, hr, p, dga, dpooled, dgab, conv_w, vecs, w_a, w_x, w_p, *, ts, stage=None):
    s_rows = proj.shape[0]
    d = D_MODEL
    n_t = s_rows // ts
    nn, nt, tn = (((1,), (0,)), ((), ())), (((1,), (1,)), ((), ())), (((0,), (0,)), ((), ()))

    def dot(lhs, rhs, dims):
        return lax.dot_general(lhs, rhs, dims, preferred_element_type=F32)

    def body(x_ref, y_ref, xr_ref, hr_ref, hh_ref, p_ref, dga_ref, dpl_ref, dgab_ref, cw_ref, vec_ref, wa_ref, wx_ref, wp_ref,
             dproj_o, acc_o, dwa_o, dwx_o, dwp_o, g_carry, dxr_carry, q_carry, a_scr, b_scr, g_scr, a_keep, r_keep, i_keep, m_keep):
        step = pl.program_id(0)
        t = n_t - 1 - step

        @pl.when(step == 0)
        def _():
            for ref in (acc_o, dwa_o, dwx_o, dwp_o, g_carry, dxr_carry, q_carry):
                ref[...] = jnp.zeros_like(ref)

        rows = lax.broadcasted_iota(jnp.int32, (ts, BLOCK), 0)
        start = (rows == 0) & (t == 0)

        def add_sum(row, lanes, value):
            acc_o[row:row + 1, lanes] += jnp.sum(value, axis=0, keepdims=True)

        for g in range(N_BLOCKS):
            lanes = slice(g * BLOCK, (g + 1) * BLOCK)
            xrb = xr_ref[:, lanes].astype(BF16)
            r = jax.nn.sigmoid(dot(xrb, wa_ref[g], nn) + vec_ref[1:2, lanes])
            gate_i = jax.nn.sigmoid(dot(xrb, wx_ref[g], nn) + vec_ref[2:3, lanes])
            log_a = (-C_RG * r) * jax.nn.softplus(vec_ref[3:4, lanes])
            a = jnp.exp(log_a)
            a_keep[:, lanes], r_keep[:, lanes], i_keep[:, lanes], m_keep[:, lanes] = a, r, gate_i, _one_minus_square(a, log_a)
            gelu, dgelu = _gelu_parts(y_ref[:, lanes])
            dga_t = dga_ref[:, lanes]
            dproj_o[:, d + g * BLOCK:d + (g + 1) * BLOCK] = (dga_t * hr_ref[:, lanes] * dgelu).astype(BF16)
            a_scr[:, lanes] = jnp.where(rows == ts - 1, 1.0, pltpu.roll(a, ts - 1, 0))
            b_scr[:, lanes] = dga_t * gelu

        sub = lax.broadcasted_iota(jnp.int32, (8, d), 0)

        def chunk(i, g_next):
            at = pl.multiple_of((ts // 8 - 1 - i) * 8, 8)
            aa = a_scr[pl.ds(at, 8), :]
            bb = b_scr[pl.ds(at, 8), :]
            for s in (1, 2, 4):
                keep = sub < 8 - s
                bb = jnp.where(keep, bb + aa * pltpu.roll(bb, 8 - s, 0), bb)
                aa = jnp.where(keep, aa * pltpu.roll(aa, 8 - s, 0), aa)
            gg = aa * g_next + bb
            g_scr[pl.ds(at, 8), :] = gg
            return gg[0:1, :]

        g_first = lax.fori_loop(0, ts // 8, chunk, g_carry[...])
        g_carry[...] = a_keep[0:1, :] * g_first

        pos = (rows + t * ts + 1).astype(F32)
        for g, win in enumerate(POOL_WINDOWS):
            lanes = slice(g * BLOCK, (g + 1) * BLOCK)
            gs, xr, a, r, gate_i, m_square = g_scr[:, lanes], xr_ref[:, lanes], a_keep[:, lanes], r_keep[:, lanes], i_keep[:, lanes], m_keep[:, lanes]
            sp = jax.nn.softplus(vec_ref[3:4, lanes])
            mult = jnp.where(start, 1.0, jnp.sqrt(m_square))
            h_before = jnp.where(t == 0, 0.0, hh_ref[CONV_HALO - 1:CONV_HALO, lanes])
            h_prev = jnp.where(rows == 0, h_before, pltpu.roll(hr_ref[:, lanes], 1, 0))
            dxr = gs * gate_i * mult
            d_i = gs * xr * mult
            d_mult = gs * xr * gate_i
            dlog_a = (gs * h_prev) * a - jnp.where(start, 0.0, d_mult * (a * a) * lax.rsqrt(m_square))
            dzr = (dlog_a * (-C_RG * sp)) * (r * (1.0 - r))
            dzi = d_i * (gate_i * (1.0 - gate_i))
            add_sum(5, lanes, dzr)
            add_sum(6, lanes, dzi)
            add_sum(7, lanes, dlog_a * (-C_RG * r) * jax.nn.sigmoid(vec_ref[3:4, lanes]))
            xrb, dzrb, dzib = xr.astype(BF16), dzr.astype(BF16), dzi.astype(BF16)
            dwa_o[g] += dot(xrb, dzrb, tn)
            dwx_o[g] += dot(xrb, dzib, tn)
            dxr = dxr + dot(dzrb, wa_ref[g], nt) + dot(dzib, wx_ref[g], nt)

            x = x_ref[:, lanes]
            ext = jnp.concatenate([dxr, dxr_carry[:, lanes]], axis=0)
            dx = dxr * cw_ref[CONV_WIDTH - 1:CONV_WIDTH, lanes]
            add_sum(CONV_WIDTH - 1, lanes, x * dxr)
            for j in range(CONV_WIDTH - 1):
                ahead = pltpu.roll(ext, ts + CONV_HALO - (CONV_WIDTH - 1 - j), 0)[:ts]
                dx = dx + ahead * cw_ref[j:j + 1, lanes]
                add_sum(j, lanes, x * ahead)
            add_sum(4, lanes, dxr)
            dxr_carry[:, lanes] = dxr[0:CONV_HALO, :]
            dproj_o[:, lanes] = dx.astype(BF16)

            pb = p_ref[:, lanes]
            mixed = dot(pb, wp_ref[g], nn) + vec_ref[4:5, lanes]
            dpl = dpl_ref[:, lanes]
            add_sum(9, lanes, dpl * mixed)
            dmixed = dpl * vec_ref[5:6, lanes]
            add_sum(8, lanes, dmixed)
            dmb = dmixed.astype(BF16)
            dwp_o[g] += dot(pb, dmb, tn)
            dp = dot(dmb, wp_ref[g], nt)
            q = dp * (1.0 / jnp.minimum(pos, float(win)))
            sums, have = jnp.concatenate([q, q_carry[:, lanes]], axis=0), 1
            while have < win:
                sums = sums + pltpu.roll(sums, ts + POOL_HALO - have, 0)
                have *= 2
            q_carry[:, lanes] = q[0:POOL_HALO, :]
            dproj_o[:, 2 * d + g * BLOCK:2 * d + (g + 1) * BLOCK] = (sums[:ts] - dp).astype(BF16)

        dproj_o[:, 3 * d:5 * d] = dgab_ref[...]

    def rev(cols, col_block=0):
        return pl.BlockSpec((ts, cols), lambda i: (n_t - 1 - i, col_block))

    halo_h = pl.BlockSpec((CONV_HALO, d), lambda i: (jnp.maximum((n_t - 1 - i) * (ts // CONV_HALO) - 1, 0), 0))
    wspec = pl.BlockSpec((N_BLOCKS, BLOCK, BLOCK), lambda i: (0, 0, 0))
    return _hosted(
        stage, body, name="seq_backward", grid=(n_t,),
        out_shape=[SDS((s_rows, 5 * d), BF16), SDS((16, d), F32)] + [SDS((N_BLOCKS, BLOCK, BLOCK), F32)] * 3,
        in_specs=[rev(d, 0), rev(d, 1), rev(d), rev(d), halo_h, rev(d), rev(d), rev(d), rev(2 * d), _vec_spec(CONV_WIDTH, d), _vec_spec(8, d),
                  wspec, wspec, wspec],
        out_specs=[rev(5 * d), _vec_spec(16, d), wspec, wspec, wspec],
        scratch_shapes=[pltpu.VMEM((1, d), F32), pltpu.VMEM((CONV_HALO, d), F32), pltpu.VMEM((POOL_HALO, d), F32)] + [pltpu.VMEM((ts, d), F32)] * 7,
        compiler_params=_params("arbitrary"),
    )(proj, proj, xr, hr, hr, p, dga, dpooled, dgab, conv_w, vecs, w_a, w_x, w_p)


def _adamw_math(w, g, m, v):
    m = ADAM_B1 * m + (1.0 - ADAM_B1) * g
    v = ADAM_B2 * v + (1.0 - ADAM_B2) * jnp.square(g)
    m_hat = m / (1.0 - ADAM_B1 ** ADAM_STEP)
    v_hat = v / (1.0 - ADAM_B2 ** ADAM_STEP)
    delta = -ADAM_LR * (m_hat / (jnp.sqrt(v_hat) + ADAM_EPS) + ADAM_WD * w)
    return delta, m, v


def _adamw(w, g, m, v, name):
    rows, cols = w.shape
    tr = min(rows, 512)

    def body(w_ref, g_ref, m_ref, v_ref, d_ref, mo_ref, vo_ref):
        d_ref[...], mo_ref[...], vo_ref[...] = _adamw_math(w_ref[...], g_ref[...], m_ref[...], v_ref[...])

    spec = pl.BlockSpec((tr, cols), lambda i: (i, 0))
    return pl.pallas_call(
        body, name=name, grid=(rows // tr,), out_shape=[SDS((rows, cols), F32)] * 3, in_specs=[spec] * 4, out_specs=[spec] * 3,
        compiler_params=_params("parallel"),
    )(w, g, m, v)


def _prologue(c, conv_shard, w_ada, b_cols, w_in_halves, gate_halves):
    n_cols = w_ada.shape[1]
    n_g = len(gate_halves)
    half_rows = w_in_halves.shape[1] // 2
    n_sems = 3 * N_DEV + N_CHIPS * n_g + 5 + 2 + 3

    def body(c_ref, conv_ref, w_ref, b_ref, win_ref, *rest):
        gate_refs = rest[:n_g]
        act_o, mod_o, conv_o, win_o = rest[n_g:n_g + 4]
        gate_o = rest[n_g + 4:2 * n_g + 4]
        c_all, mod_cols, send_sems, recv_sems = rest[2 * n_g + 4:]
        x, y, c = _coords()
        me, k_me = 4 * x + 2 * y + c, 2 * x + y
        sibling, x_nbr, y_nbr = (x, y, 1 - c), (1 - x, y, c), (x, 1 - y, c)
        k_x, k_y, k_d = 2 * (1 - x) + y, 2 * x + (1 - y), 2 * (1 - x) + (1 - y)
        counter = iter(range(n_sems))

        def copy(src, dst, to, landing):
            s = next(counter)
            out = pltpu.make_async_remote_copy(src_ref=src, dst_ref=dst, send_sem=send_sems.at[s], recv_sem=recv_sems.at[s],
                                               device_id=to, device_id_type=MESH)
            arrival = pltpu.make_async_remote_copy(src_ref=landing, dst_ref=landing, send_sem=send_sems.at[s], recv_sem=recv_sems.at[s],
                                                   device_id=(x, y, c), device_id_type=MESH)
            return out, arrival

        def allgather(src, out):
            s = next(counter)
            own = pltpu.make_async_copy(src, out.at[me], send_sems.at[s])
            peers = [copy(src, out.at[me], (_flip(x, r & 4), _flip(y, r & 2), _flip(c, r & 1)), out.at[me ^ r]) for r in range(1, N_DEV)]
            return own, peers

        def start(group):
            own, peers = group
            own.start()
            for out, _ in peers:
                out.start()

        def finish(group):
            own, peers = group
            for out, arrival in peers:
                arrival.wait_recv()
                out.wait_send()
            own.wait()

        rows_a, rows_b = pl.ds(0, half_rows), pl.ds(half_rows, half_rows)
        c_group, conv_group = allgather(c_ref, c_all), allgather(conv_ref, conv_o)
        gates = []
        for g_ref, g_out in zip(gate_refs, gate_o):
            gates.append(copy(g_ref, g_out.at[k_me], sibling, g_out.at[k_me]))
            gates += [copy(g_ref.at[c], g_out.at[k_me, c], (px, py, c), g_out.at[2 * px + py, c]) for px, py in _chips()]
        own_in = copy(win_ref, win_o.at[k_me], sibling, win_o.at[k_me])
        to_x = [copy(win_ref.at[c, rows], win_o.at[k_me, c, rows], x_nbr, win_o.at[k_x, c, rows]) for rows in (rows_a, rows_b)]
        to_y = [copy(win_ref.at[c, rows], win_o.at[k_me, c, rows], y_nbr, win_o.at[k_y, c, rows]) for rows in (rows_b, rows_a)]
        mod_group = allgather(mod_cols, mod_o)
        relay_y = copy(win_o.at[k_x, c, rows_a], win_o.at[k_x, c, rows_a], y_nbr, win_o.at[k_d, c, rows_a])
        relay_x = copy(win_o.at[k_y, c, rows_b], win_o.at[k_y, c, rows_b], x_nbr, win_o.at[k_d, c, rows_b])
        sib_x, sib_y, sib_d = (copy(win_o.at[k, c], win_o.at[k, c], sibling, win_o.at[k, 1 - c]) for k in (k_x, k_y, k_d))

        start(c_group)
        start(conv_group)
        for out, _ in gates + [own_in] + to_x + to_y:
            out.start()
        finish(c_group)
        cv = jnp.concatenate([c_all[r] for r in range(N_DEV)], axis=0)
        act = cv * jax.nn.sigmoid(cv)
        act_o[...] = act
        mod_cols[...] = jnp.dot(act.astype(BF16), w_ref[...].astype(BF16), preferred_element_type=F32) + b_ref[...]
        start(mod_group)
        to_x[0][1].wait_recv()
        relay_y[0].start()
        to_y[0][1].wait_recv()
        relay_x[0].start()
        finish(mod_group)
        finish(conv_group)
        to_x[1][1].wait_recv()
        sib_x[0].start()
        to_y[1][1].wait_recv()
        sib_y[0].start()
        for out, arrival in gates + [own_in]:
            arrival.wait_recv()
            out.wait_send()
        for out, arrival in (relay_y, relay_x):
            arrival.wait_recv()
            out.wait_send()
        sib_d[0].start()
        for out, _ in to_x + to_y:
            out.wait_send()
        for out, arrival in (sib_x, sib_y, sib_d):
            arrival.wait_recv()
            out.wait_send()

    gate_shapes = [SDS((N_CHIPS,) + g.shape, g.dtype) for g in gate_halves]
    return pl.pallas_call(
        body, name="prologue",
        out_shape=[SDS((N_DEV, c.shape[1]), F32), SDS((N_DEV, N_DEV, n_cols), F32), SDS((N_DEV,) + conv_shard.shape, F32),
                   SDS((N_CHIPS,) + w_in_halves.shape, w_in_halves.dtype)] + gate_shapes,
        in_specs=[VMEM_SPEC] * 4 + [HBM_SPEC] * (1 + n_g), out_specs=[VMEM_SPEC] * 3 + [HBM_SPEC] * (1 + n_g),
        scratch_shapes=[pltpu.VMEM((N_DEV,) + c.shape, F32), pltpu.VMEM((N_DEV, n_cols), F32),
                        pltpu.SemaphoreType.DMA((n_sems,)), pltpu.SemaphoreType.DMA((n_sems,))],
        compiler_params=pltpu.CompilerParams(vmem_limit_bytes=VMEM_LIMIT_V7X),
    )(c, conv_shard, w_ada, b_cols, w_in_halves, *gate_halves)


def _ada_backward(act_t, dmod_cols, w, m, v):
    rows, cols = w.shape
    tn = 512

    def body(a_ref, dm_ref, w_ref, m_ref, v_ref, g_ref, d_ref, mo_ref, vo_ref):
        g = jnp.dot(a_ref[...].astype(BF16), dm_ref[...].astype(BF16), preferred_element_type=F32)
        g_ref[...] = g
        d_ref[...], mo_ref[...], vo_ref[...] = _adamw_math(w_ref[...], g, m_ref[...], v_ref[...])

    spec = pl.BlockSpec((rows, tn), lambda j: (0, j))
    return pl.pallas_call(
        body, name="ada_backward", grid=(cols // tn,), out_shape=[SDS((rows, cols), F32)] * 4,
        in_specs=[pl.BlockSpec((rows, N_DEV), lambda j: (0, 0)), pl.BlockSpec((N_DEV, tn), lambda j: (0, j)), spec, spec, spec],
        out_specs=[spec] * 4, compiler_params=_params("parallel"),
    )(act_t, dmod_cols, w, m, v)


def _sum_rows(gathered, n_loss):
    n = gathered.shape[2]

    def body(g_ref, o_ref):
        acc = g_ref[0]
        for r in range(1, N_DEV):
            acc = acc + g_ref[r]
        o_ref[...] = acc
        o_ref[:, n - n_loss:n] = jnp.broadcast_to(jnp.sum(acc[:, n - n_loss:n], axis=1, keepdims=True), (1, n_loss))

    return pl.pallas_call(body, name="sum_rows", out_shape=SDS((1, n), F32), in_specs=[VMEM_SPEC], out_specs=VMEM_SPEC)(gathered)


def _adamw_vectors(total, offsets, separate, params):
    n_p = len(params)
    apart = [g for g in separate if g is not None]

    def body(*refs):
        total_ref, apart_refs = refs[0], list(refs[1:1 + len(apart)])
        ins, outs = refs[1 + len(apart):1 + len(apart) + 3 * n_p], refs[1 + len(apart) + 3 * n_p:]
        for i in range(n_p):
            w_ref, m_ref, v_ref = ins[3 * i:3 * i + 3]
            g = apart_refs.pop(0)[...] if offsets[i] is None else total_ref[:, offsets[i]:offsets[i] + w_ref.shape[1]]
            outs[4 * i][...] = g
            outs[4 * i + 1][...], outs[4 * i + 2][...], outs[4 * i + 3][...] = _adamw_math(w_ref[...], g, m_ref[...], v_ref[...])

    flat = [a for p in params for a in p]
    res = pl.pallas_call(
        body, name="adamw_vectors", out_shape=[SDS(p[0].shape, F32) for p in params for _ in range(4)],
        in_specs=[VMEM_SPEC] * (1 + len(apart) + len(flat)), out_specs=[VMEM_SPEC] * (4 * n_p),
    )(total, *apart, *flat)
    return [tuple(res[4 * i:4 * i + 4]) for i in range(n_p)]


def _blocks_to_pieces(w):
    nb, rows, n = w.shape
    q = rows // N_CHIPS
    return w.reshape(nb, N_CHIPS, q, n).transpose(1, 0, 2, 3).reshape(N_CHIPS, 2, nb // 2, q, n)


def _pieces_to_blocks(w):
    n_chips, _, half, q, n = w.shape
    return w.reshape(n_chips, 2 * half, q, n).transpose(1, 0, 2, 3).reshape(2 * half, n_chips * q, n)


def kernel(x, c, norm_mix_g, norm_mlp_g, w_ada, b_ada, w_in, conv_w, conv_b, w_rg_a, b_rg_a, w_rg_x, b_rg_x, a_param, w_branch_a, w_pool, b_pool, pool_scale, w_branch_b, w_out, w_up, w_down, final_g, loss_target, m_norm_mix_g, m_norm_mlp_g, m_w_ada, m_b_ada, m_w_in, m_conv_w, m_conv_b, m_w_rg_a, m_b_rg_a, m_w_rg_x, m_b_rg_x, m_a_param, m_w_branch_a, m_w_pool, m_b_pool, m_pool_scale, m_w_branch_b, m_w_out, m_w_up, m_w_down, m_final_g, v_norm_mix_g, v_norm_mlp_g, v_w_ada, v_b_ada, v_w_in, v_conv_w, v_conv_b, v_w_rg_a, v_b_rg_a, v_w_rg_x, v_b_rg_x, v_a_param, v_w_branch_a, v_w_pool, v_b_pool, v_pool_scale, v_w_branch_b, v_w_out, v_w_up, v_w_down, v_final_g):
    d = D_MODEL
    s_rows = x.shape[1]
    ts, tsq = min(512, s_rows), min(256, s_rows)
    xi, yi, ci = _coords()
    k_me, me = 2 * xi + yi, 4 * xi + 2 * yi + ci
    ada_cols = w_ada.shape[2]
    conv_cols = conv_w.shape[2]
    n_in, n_up = w_in.shape[2], w_up.shape[2]

    names = ("w_in", "w_up", "w_down", "w_a", "w_b", "w_out", "rg_a", "rg_x", "pool")
    mine = dict(zip(names, (w_in[0], w_up[0], w_down[0], w_branch_a[0], w_branch_b[0], w_out[0], w_rg_a[0], w_rg_x[0], w_pool[0])))
    moments_m = dict(zip(names, (m_w_in[0], m_w_up[0], m_w_down[0], m_w_branch_a[0], m_w_branch_b[0], m_w_out[0], m_w_rg_a[0], m_w_rg_x[0], m_w_pool[0])))
    moments_v = dict(zip(names, (v_w_in[0], v_w_up[0], v_w_down[0], v_w_branch_a[0], v_w_branch_b[0], v_w_out[0], v_w_rg_a[0], v_w_rg_x[0], v_w_pool[0])))
    block_weights, squares = ("rg_a", "rg_x", "pool"), ("w_a", "w_b", "w_out")
    place = jnp.stack([ci, k_me]).astype(jnp.int32)
    g_mix, g_mlp, g_fin = norm_mix_g, norm_mlp_g, final_g.reshape(1, d)
    x0, target = x[0], loss_target[0]
    wg = functools.partial(_weight_grad, tk=min(1024, d))
    plain, back, back_sum = (dict(transposed=t, reduce_blocks=r) for t, r in ((False, False), (True, False), (True, True)))
    rows_d, vec8, vec1 = _row_spec(ts, d), _vec_spec(8, d), _vec_spec(1, d)
    tw = min(1024, s_rows)
    halves_of_tile = 2

    def out_rows(cols, dtype, t=ts):
        return SDS((s_rows, cols), dtype), _row_spec(t, cols)

    sums_out = (SDS((8, d), F32), vec8)

    def halves(n):
        return mine[n].astype(BF16).reshape(2, -1, mine[n].shape[-1])

    def blocks(w):
        return _pieces_to_blocks(w.reshape(N_CHIPS, 2, N_BLOCKS // 2, BLOCK // N_CHIPS, BLOCK))

    act_all, mod_all, conv_all, w_in_whole, *gate_wholes = _prologue(
        c, conv_w[0], w_ada[0], lax.dynamic_slice_in_dim(b_ada, k_me * ada_cols, ada_cols, axis=1), halves("w_in"), [halves(n) for n in block_weights])
    conv_full = conv_all[0::2].transpose(1, 0, 2).reshape(CONV_WIDTH, d)
    mod_mine = lax.dynamic_index_in_dim(mod_all, me, axis=1, keepdims=False)[0::2]
    mod = jnp.concatenate([mod_mine.reshape(N_MOD, d), jnp.zeros((8 - N_MOD, d), F32)], axis=0)
    w_in_all = w_in_whole.reshape(N_CHIPS, d, n_in)
    vecs = jnp.concatenate([conv_b, b_rg_a, b_rg_x, a_param, b_pool, pool_scale, jnp.zeros((2, d), F32)], axis=0)

    def norm_first(a_ref, extra_refs, out_refs):
        h = _modulated_norm(a_ref[...], extra_refs[0], extra_refs[1], 1, 0)
        out_refs[1][...] = h
        return h

    stage = _Stage()
    i_blk = [_add_gather_d2d(stage, g) for g in gate_wholes]
    i_sq = [_add_gather_ici(stage, halves(n)) for n in squares]
    down_rows = mine["w_down"].shape[0] // 8
    i_down = _add_gather_ici(stage, halves("w_down"), rows=(0, down_rows))
    (proj, h1), got = _matmul(x0, w_in_all, name="proj_in", tm=ts, extra=[(mod, vec8), (g_mix, vec1)], prepare=norm_first,
                              outs=[out_rows(N_CHIPS * n_in, F32), out_rows(d, BF16)], finish=_store_blocks(n_in), parts=halves_of_tile,                              stage=stage, **plain)
    w_rg_a_all, w_rg_x_all, w_pool_all = (blocks(got[i]) for i in i_blk)
    stage = _Stage()
    i_sq = [_add_gather_d2d(stage, got[i]) for i in i_sq]
    i_up = _add_gather_ici(stage, halves("w_up"))
    i_down = _add_gather_ici(stage, halves("w_down"), rows=(down_rows, down_rows), whole=got[i_down])
    (xr, hr, ga, p, pooled), got = _seq_forward(proj, conv_full, vecs, w_rg_a_all, w_rg_x_all, w_pool_all, ts=tsq, stage=stage)
    w_a_all, w_b_all, w_out_all = (got[i].reshape(1, d, d) for i in i_sq)
    stage = _Stage()
    i_up = _add_gather_d2d(stage, got[i_up])
    i_down = _add_gather_ici(stage, halves("w_down"), rows=(2 * down_rows, 2 * down_rows), whole=got[i_down])
    (merged, s_a, s_b, t_a, t_b), got = _branches(ga, pooled, proj, w_a_all, w_b_all, ts=ts, stage=stage)
    w_up_all = got[i_up].reshape(N_CHIPS, d, n_up)
    stage = _Stage()
    i_down = _add_gather_d2d(stage, got[i_down])

    def residual_norm(k, acc, extra_refs, out_refs):
        x_ref, mod_ref, g_ref = extra_refs
        out_refs[0][...] = acc.astype(BF16)
        x2_t = x_ref[...] + mod_ref[2:3, :] * acc
        out_refs[1][...] = x2_t
        out_refs[2][...] = _modulated_norm(x2_t, mod_ref, g_ref, 4, 3)

    (mo, x2, h2), got = _matmul(merged, w_out_all, name="mix_out", tm=ts, extra=[(x0, rows_d), (mod, vec8), (g_mlp, vec1)],
                                outs=[out_rows(d, BF16), out_rows(d, F32), out_rows(d, BF16)], finish=residual_norm, parts=halves_of_tile,                                stage=stage, transposed=False, reduce_blocks=True)
    w_down_all = got[i_down].reshape(N_CHIPS, D_FF // N_CHIPS, d)

    def relu_squared(k, acc, extra_refs, out_refs):
        out_refs[0][:, k * n_up:(k + 1) * n_up] = jnp.square(jnp.maximum(acc, 0.0)).astype(BF16)

    ff, = _matmul(h2, w_up_all, name="mlp_up", tm=ts, outs=[out_rows(D_FF, BF16)], finish=relu_squared, parts=halves_of_tile, **plain)

    def loss_head(k, acc, extra_refs, out_refs):
        x2_ref, mod_ref, g_ref, t_ref = extra_refs
        dx_ref, df_ref, acc_ref = out_refs
        xh, r = _rms(x2_ref[...] + mod_ref[5:6, :] * acc)
        err = xh * g_ref[...] - t_ref[...]
        dy = err / d
        dxh = dy * g_ref[...]
        dx3_t = r * (dxh - xh * jnp.mean(dxh * xh, axis=-1, keepdims=True))
        dx_ref[...] = dx3_t
        df_ref[...] = (dx3_t * mod_ref[5:6, :]).astype(BF16)
        _accumulate(acc_ref, 0, dy * xh)
        _accumulate(acc_ref, 1, dx3_t * acc)
        _accumulate(acc_ref, 2, jnp.square(err) * (0.5 / d))

    dx3, dffo, sums_head = _matmul(ff, w_down_all, name="mlp_down", tm=ts, extra=[(x2, rows_d), (mod, vec8), (g_fin, vec1), (target, rows_d)],
                                   outs=[out_rows(d, F32), out_rows(d, BF16), sums_out], finish=loss_head, zero_first=(2,), parts=halves_of_tile,                                   transposed=False, reduce_blocks=True)

    partial_of, from_sibling, chip_sum, from_chips, half_done, quarter = {}, {}, {}, {}, {}, {}

    def publish(n, g):
        g = _blocks_to_pieces(g).astype(BF16) if n in block_weights else g
        partial_of[n] = g.reshape(N_CHIPS, 2, -1, mine[n].shape[-1])

    def exchange(to_sibling=(), to_chips=(), swap=()):
        st = _Stage()
        slots = [(n, from_sibling, _add_reduce_d2d(st, partial_of[n])) for n in to_sibling]
        slots += [(n, from_chips, _add_reduce_ici(st, chip_sum[n])) for n in to_chips]
        slots += [(n, quarter, _add_swap_halves(st, half_done[n])) for n in swap]
        return st, slots

    def collect(slots, outs):
        for n, where, i in slots:
            where[n] = outs[i]

    def sum_pairs(*ns):
        for n in ns:
            chip_sum[n] = _sum_pair(partial_of[n], from_sibling[n], place, "sum_pair_" + n)

    def sum_quarters(*ns):
        for n in ns:
            half_done[n] = _sum_quarter(partial_of[n], from_sibling[n], from_chips[n], place, "sum_quarter_" + n)

    publish("w_down", wg(ff, dffo, nb=1, name="grad_w_down"))

    def relu_squared_backward(k, acc, extra_refs, out_refs):
        cols = slice(k * n_up, (k + 1) * n_up)
        out_refs[0][:, cols] = (acc * (2.0 * jnp.sqrt(extra_refs[0][:, cols].astype(F32)))).astype(BF16)

    stage, slots = exchange(to_sibling=["w_down"])
    (dup,), got = _matmul(dffo, w_down_all, name="d_mlp_down", tm=ts, extra=[(ff, _row_spec(ts, D_FF))], outs=[out_rows(D_FF, BF16)],
                          finish=relu_squared_backward, parts=halves_of_tile, stage=stage, **back)
    collect(slots, got)
    sum_pairs("w_down")

    def norm_mlp_backward(k, acc, extra_refs, out_refs):
        x2_ref, dres_ref, mod_ref, g_ref, mo_ref = extra_refs
        dx_ref, dmo_ref, acc_ref = out_refs
        dx =_modulated_norm_backward(acc, x2_ref[...], dres_ref[...], mod_ref, g_ref, acc_ref, 4)
        dx_ref[...] = dx
        dmo_ref[...] = (dx * mod_ref[2:3, :]).astype(BF16)
        _accumulate(acc_ref, 3, dx * mo_ref[...].astype(F32))

    stage, slots = exchange(to_chips=["w_down"])
    (dx2, dmo, sums_mlp), got = _matmul(dup, w_up_all, name="d_mlp_up", tm=ts, stage=stage, finish=norm_mlp_backward,
                                        extra=[(x2, rows_d), (dx3, rows_d), (mod, vec8), (g_mlp, vec1), (mo, rows_d)],
                                        outs=[out_rows(d, F32), out_rows(d, BF16), sums_out], zero_first=(2,), parts=halves_of_tile, **back_sum)
    collect(slots, got)
    sum_quarters("w_down")
    publish("w_up", wg(h2, dup, nb=N_CHIPS, name="grad_w_up"))
    publish("w_out", wg(merged, dmo, nb=1, name="grad_w_out"))

    def merge_backward(k, acc, extra_refs, out_refs):
        sa_ref, sb_ref, ta_ref, tb_ref = extra_refs
        out_refs[0][...] = (acc * sa_ref[...].astype(F32)).astype(BF16)
        out_refs[1][...] = (acc * sb_ref[...].astype(F32)).astype(BF16)
        out_refs[2][:, 0:d] = (acc * ta_ref[...].astype(F32)).astype(BF16)
        out_refs[2][:, d:2 * d] = (acc * tb_ref[...].astype(F32)).astype(BF16)

    stage, slots = exchange(to_sibling=["w_up", "w_out"], swap=["w_down"])
    (dbr_a, dbr_b, dgab), got = _matmul(dmo, w_out_all, name="d_mix_out", tm=ts, stage=stage, finish=merge_backward,
                                        extra=[(s_a, rows_d), (s_b, rows_d), (t_a, rows_d), (t_b, rows_d)],
                                        outs=[out_rows(d, BF16), out_rows(d, BF16), out_rows(2 * d, BF16)], parts=halves_of_tile, **back_sum)
    collect(slots, got)
    publish("w_a", wg(ga, dbr_a, nb=1, name="grad_w_branch_a"))
    publish("w_b", wg(pooled, dbr_b, nb=1, name="grad_w_branch_b"))
    dga, = _matmul(dbr_a, w_a_all, name="d_branch_a", tm=tw, outs=[out_rows(d, F32, tw)], finish=_store_blocks(d), **back_sum)
    dpooled, = _matmul(dbr_b, w_b_all, name="d_branch_b", tm=tw, outs=[out_rows(d, F32, tw)], finish=_store_blocks(d), **back_sum)
    sum_pairs("w_up", "w_out")
    stage, slots = exchange(to_sibling=["w_a", "w_b"], to_chips=["w_up", "w_out"])
    (dproj, sums_seq, d_rg_a, d_rg_x, d_pool), got = _seq_backward_by_block(proj, xr, hr, p, dga, dpooled, dgab, conv_full, vecs,
                                                                            w_rg_a_all, w_rg_x_all, w_pool_all, ts=tsq, stage=stage)
    collect(slots, got)
    sum_quarters("w_up", "w_out")
    sum_pairs("w_a", "w_b")
    for n, g in zip(block_weights, (d_rg_a, d_rg_x, d_pool)):
        publish(n, g)
    stage, slots = exchange(to_sibling=block_weights, to_chips=["w_a", "w_b"], swap=["w_up", "w_out"])
    g_in, got = wg(h1, dproj, nb=N_CHIPS, name="grad_w_in", stage=stage)
    collect(slots, got)
    publish("w_in", g_in)
    sum_pairs(*block_weights)
    sum_quarters("w_a", "w_b")
    stage, slots = exchange(to_sibling=["w_in"], to_chips=block_weights)
    collect(slots, _run_stage(stage, "exchange_w_in_gates"))
    sum_pairs("w_in")
    sum_quarters(*block_weights)

    def norm_mix_backward(k, acc, extra_refs, out_refs):
        x_ref, dres_ref, mod_ref, g_ref = extra_refs
        out_refs[0][...] = _modulated_norm_backward(acc, x_ref[...], dres_ref[...], mod_ref, g_ref, out_refs[1], 1)

    stage, slots = exchange(to_chips=["w_in"], swap=block_weights + ("w_a", "w_b"))
    (grad_x, sums_mix), got = _matmul(dproj, w_in_all, name="d_proj_in", tm=ts, stage=stage, finish=norm_mix_backward,
                                      extra=[(x0, rows_d), (dx2, rows_d), (mod, vec8), (g_mix, vec1)],
                                      outs=[out_rows(d, F32), sums_out], zero_first=(1,), parts=halves_of_tile, **back_sum)
    collect(slots, got)
    sum_quarters("w_in")

    dmod = jnp.concatenate([sums_mix[1:2], sums_mix[0:1], sums_mlp[3:4], sums_mlp[1:2], sums_mlp[0:1], sums_head[1:2]], axis=1)
    row = jnp.concatenate([sums_mix[2:3], sums_mlp[2:3], sums_seq[4:5], sums_seq[5:6], sums_seq[6:7], sums_seq[7:8], sums_seq[8:9],
                           sums_seq[9:10], sums_head[0:1], sums_seq[0:CONV_WIDTH].reshape(1, CONV_WIDTH * d), dmod, sums_head[2:3]], axis=1)
    n_vec, n_conv, n_mod = 9 * d, CONV_WIDTH * d, N_MOD * d
    stage, slots = exchange(swap=["w_in"])
    i_rows = _add_allgather8(stage, row)
    got = _run_stage(stage, "exchange_sums_w_in")
    collect(slots, got)
    rows_all = got[i_rows]
    total = _sum_rows(rows_all, d)
    loss = total[0, n_vec + n_conv + n_mod]
    dmod_all = rows_all[:, 0, n_vec + n_conv:n_vec + n_conv + n_mod]
    g_ada, d_ada, m_ada, v_ada = _ada_backward(act_all.T, lax.dynamic_slice_in_dim(dmod_all, k_me * ada_cols, ada_cols, axis=1),
                                               w_ada[0], m_w_ada[0], v_w_ada[0])
    g_conv = lax.dynamic_slice_in_dim(total[:, n_vec:n_vec + n_conv].reshape(CONV_WIDTH, d), k_me * conv_cols, conv_cols, axis=1)
    vec_names = ("norm_mix_g", "norm_mlp_g", "conv_b", "b_rg_a", "b_rg_x", "a_param", "b_pool", "pool_scale", "final_g", "conv_w", "b_ada")
    vec_params = [(norm_mix_g, m_norm_mix_g, v_norm_mix_g), (norm_mlp_g, m_norm_mlp_g, v_norm_mlp_g), (conv_b, m_conv_b, v_conv_b),
                  (b_rg_a, m_b_rg_a, v_b_rg_a), (b_rg_x, m_b_rg_x, v_b_rg_x), (a_param, m_a_param, v_a_param), (b_pool, m_b_pool, v_b_pool),
                  (pool_scale, m_pool_scale, v_pool_scale), tuple(a.reshape(1, d) for a in (final_g, m_final_g, v_final_g)),
                  (conv_w[0], m_conv_w[0], v_conv_w[0]), (b_ada, m_b_ada, v_b_ada)]
    offsets = [i * d for i in range(9)] + [None, n_vec + n_conv]
    vec_res = _adamw_vectors(total, offsets, [None] * 9 + [g_conv, None], vec_params)
    vec_out = {n: [r.reshape(shape) for r in res] for n, res, shape in zip(
        vec_names, vec_res, [(1, d)] * 8 + [(d,), (1, CONV_WIDTH, conv_cols), (1, N_MOD * d)])}

    big_out = {}
    for n in names:
        shape2 = (-1, mine[n].shape[-1])
        g2 = quarter[n].reshape(shape2)
        res = (g2,) + tuple(_adamw(mine[n].reshape(shape2), g2, moments_m[n].reshape(shape2), moments_v[n].reshape(shape2), "adamw_" + n))
        big_out[n] = [r.reshape((1,) + mine[n].shape) for r in res]

    key = {"w_in": "w_in", "w_rg_a": "rg_a", "w_rg_x": "rg_x", "w_branch_a": "w_a", "w_pool": "pool", "w_branch_b": "w_b", "w_out": "w_out",
           "w_up": "w_up", "w_down": "w_down"}
    order = ("norm_mix_g", "norm_mlp_g", "w_ada", "b_ada", "w_in", "conv_w", "conv_b", "w_rg_a", "b_rg_a", "w_rg_x", "b_rg_x", "a_param",
             "w_branch_a", "w_pool", "b_pool", "pool_scale", "w_branch_b", "w_out", "w_up", "w_down", "final_g")
    ada_out = [g_ada[None], d_ada[None], m_ada[None], v_ada[None]]
    outs = [loss, grad_x[None]]
    for kind in range(4):
        for nme in order:
            outs.append(ada_out[kind] if nme == "w_ada" else big_out[key[nme]][kind] if nme in key else vec_out[nme][kind])
    return tuple(outs)
```

```python
import functools

import jax
import jax.numpy as jnp
from jax import lax
from jax.experimental import pallas as pl
from jax.experimental.pallas import tpu as pltpu

F32, BF16 = jnp.float32, jnp.bfloat16
D_MODEL = 1024
D_FF = 4 * D_MODEL
N_BLOCKS = 4
BLOCK = D_MODEL // N_BLOCKS
CONV_WIDTH = 4
POOL_WINDOWS = (2, 4, 8, 16)
CONV_HALO = 8
POOL_HALO = 16
N_MOD = 6
EPS = 1e-6
C_RG = 8.0
ADAM_LR, ADAM_B1, ADAM_B2, ADAM_EPS, ADAM_WD, ADAM_STEP = 0.001, 0.9, 0.999, 1e-08, 0.01, 10
N_DEV = 8
N_CHIPS = 4
VMEM_LIMIT_V7X = 56 * 2**20
MESH = pl.DeviceIdType.MESH
SDS = jax.ShapeDtypeStruct
HBM_SPEC = pl.BlockSpec(memory_space=pltpu.HBM)
VMEM_SPEC = pl.BlockSpec(memory_space=pltpu.VMEM)


def _params(*semantics):
    return pltpu.CompilerParams(dimension_semantics=semantics, vmem_limit_bytes=VMEM_LIMIT_V7X)


def _coords():
    return lax.axis_index("x"), lax.axis_index("y"), lax.axis_index("c")


def _flip(v, bit):
    return 1 - v if bit else v


def _chips():
    x, y, _ = _coords()
    return [(1 - x, y), (x, 1 - y), (1 - x, 1 - y)]


PEER_SETS = [frozenset(s) for s in (("sibling",), ("chips",), ("sibling", "chips"), ("sibling", "chips", "others"))]


class _Stage:
    def __init__(self):
        self.inputs, self.in_specs, self.outputs, self.out_specs, self.aliases = [], [], [], [], {}
        self.parts, self.n_copies, self.peers = [], 0, set()

    def add(self, inputs, in_spec, outputs, out_spec, n_copies, build, peers, alias=False):
        self.peers |= set(peers)
        i0, o0 = len(self.inputs), len(self.outputs)
        self.inputs += inputs
        self.in_specs += [in_spec] * len(inputs)
        self.outputs += outputs
        self.out_specs += [out_spec] * len(outputs)
        if alias:
            self.aliases.update({i0 + i: o0 + i for i in range(len(inputs))})
        self.parts.append((build, i0, len(inputs), o0, len(outputs)))
        self.n_copies += n_copies
        return list(range(o0, o0 + len(outputs)))

    def copies(self, in_refs, out_refs):
        out = []
        for build, i0, ni, o0, no in self.parts:
            out += build(in_refs[i0:i0 + ni], out_refs[o0:o0 + no])
        assert len(out) == self.n_copies
        return out

    def run(self, in_refs, out_refs, send_sems, recv_sems, start):
        x, y, c = _coords()
        for s, (src, dst, to, landing) in enumerate(self.copies(in_refs, out_refs)):
            if to is None:
                cp = pltpu.make_async_copy(src, dst, send_sems.at[s])
                cp.start() if start else cp.wait()
                continue
            cp = pltpu.make_async_remote_copy(src_ref=src, dst_ref=dst, send_sem=send_sems.at[s], recv_sem=recv_sems.at[s],
                                              device_id=to, device_id_type=MESH)
            if start:
                cp.start()
            else:
                pltpu.make_async_remote_copy(src_ref=landing, dst_ref=landing, send_sem=send_sems.at[s], recv_sem=recv_sems.at[s],
                                             device_id=(x, y, c), device_id_type=MESH).wait_recv()
                cp.wait_send()


def _hosted(stage, body, *, name, in_specs, out_specs, out_shape, grid=(), scratch_shapes=(), compiler_params=None):
    if stage is None:
        return pl.pallas_call(body, name=name, grid=grid, in_specs=in_specs, out_specs=out_specs, out_shape=out_shape,
                              scratch_shapes=list(scratch_shapes), compiler_params=compiler_params)
    single = not isinstance(out_shape, (list, tuple))
    h_out_shape = [out_shape] if single else list(out_shape)
    h_out_specs = [out_specs] if single else list(out_specs)
    n_in, n_out, n_scr = len(in_specs), len(h_out_shape), len(scratch_shapes)
    s_in, s_out = len(stage.inputs), len(stage.outputs)

    def wrapped(*refs):
        h_in, st_in = refs[:n_in], refs[n_in:n_in + s_in]
        h_o, st_o = refs[n_in + s_in:n_in + s_in + n_out], refs[n_in + s_in + n_out:n_in + s_in + n_out + s_out]
        h_scr = refs[n_in + s_in + n_out + s_out:n_in + s_in + n_out + s_out + n_scr]
        send_sems, recv_sems = refs[n_in + s_in + n_out + s_out + n_scr:]
        def begin():
            x, y, c = _coords()
            peers = [(x, y, 1 - c)] * ("sibling" in stage.peers)
            peers += [(px, py, c) for px, py in _chips()] * ("chips" in stage.peers)
            peers += [(px, py, 1 - c) for px, py in _chips()] * ("others" in stage.peers)
            barrier = pltpu.get_barrier_semaphore()
            for peer in peers:
                pl.semaphore_signal(barrier, inc=1, device_id=peer, device_id_type=MESH)
            pl.semaphore_wait(barrier, len(peers))
            stage.run(st_in, st_o, send_sems, recv_sems, True)

        if not grid:
            begin()
            if body is not None:
                body(*h_in, *h_o, *h_scr)
            stage.run(st_in, st_o, send_sems, recv_sems, False)
            return
        ids = [pl.program_id(a) for a in range(len(grid))]
        first = functools.reduce(jnp.logical_and, [i == 0 for i in ids])
        last = functools.reduce(jnp.logical_and, [i == g - 1 for i, g in zip(ids, grid)])
        pl.when(first)(begin)
        body(*h_in, *h_o, *h_scr)
        pl.when(last)(lambda: stage.run(st_in, st_o, send_sems, recv_sems, False))

    call = pl.pallas_call(
        wrapped, name=name, grid=grid, in_specs=list(in_specs) + stage.in_specs, out_specs=h_out_specs + stage.out_specs,
        out_shape=h_out_shape + stage.outputs, input_output_aliases={n_in + i: n_out + o for i, o in stage.aliases.items()},
        scratch_shapes=list(scratch_shapes) + [pltpu.SemaphoreType.DMA((stage.n_copies,)), pltpu.SemaphoreType.DMA((stage.n_copies,))],
        compiler_params=pltpu.CompilerParams(dimension_semantics=("arbitrary",) * len(grid), vmem_limit_bytes=VMEM_LIMIT_V7X,
                                             collective_id=PEER_SETS.index(frozenset(stage.peers))),
    )

    def run(*operands):
        outs = call(*operands, *stage.inputs)
        host = outs[:n_out]
        return (host[0] if single else host), outs[n_out:]

    return run


def _run_stage(stage, name):
    return _hosted(stage, None, name=name, in_specs=[], out_specs=[], out_shape=[])()[1]


def _add_allgather8(stage, v):
    def build(ins, outs):
        x, y, c = _coords()
        me = 4 * x + 2 * y + c
        copies = [(ins[0], outs[0].at[me], None, None)]
        for r in range(1, N_DEV):
            peer = (_flip(x, r & 4), _flip(y, r & 2), _flip(c, r & 1))
            copies.append((ins[0], outs[0].at[me], peer, outs[0].at[me ^ r]))
        return copies

    return stage.add([v], VMEM_SPEC, [SDS((N_DEV,) + v.shape, v.dtype)], VMEM_SPEC, N_DEV, build, ("sibling", "chips", "others"))[0]


def _add_gather_ici(stage, shard, rows=None, whole=None):
    first, count = rows or (0, shard.shape[1])

    def build(ins, outs):
        x, y, c = _coords()
        k_me = 2 * x + y
        part = pl.ds(first, count)
        copies = [] if whole is not None else [(ins[0], outs[0].at[k_me], (x, y, 1 - c), outs[0].at[k_me])]
        for px, py in _chips():
            copies.append((ins[0].at[c, part], outs[0].at[k_me, c, part], (px, py, c), outs[0].at[2 * px + py, c, part]))
        return copies

    if whole is None:
        return stage.add([shard], HBM_SPEC, [SDS((N_CHIPS,) + shard.shape, shard.dtype)], HBM_SPEC, N_CHIPS, build, ("sibling", "chips"))[0]

    def build_into(ins, outs):
        return build(ins[1:], outs)

    i0 = len(stage.inputs)
    out = stage.add([whole, shard], HBM_SPEC, [SDS(whole.shape, whole.dtype)], HBM_SPEC, N_CHIPS - 1, build_into, ("chips",))[0]
    stage.aliases[i0] = out
    return out


def _add_gather_d2d(stage, whole):
    def build(ins, outs):
        x, y, c = _coords()
        return [(outs[0].at[2 * px + py, c], outs[0].at[2 * px + py, c], (x, y, 1 - c), outs[0].at[2 * px + py, 1 - c]) for px, py in _chips()]

    return stage.add([whole], HBM_SPEC, [SDS(whole.shape, whole.dtype)], HBM_SPEC, N_CHIPS - 1, build, ("sibling",), alias=True)[0]


def _add_reduce_d2d(stage, grads):
    def build(ins, outs):
        x, y, c = _coords()
        return [(ins[0].at[k, 1 - c], outs[0].at[k], (x, y, 1 - c), outs[0].at[k]) for k in range(N_CHIPS)]

    return stage.add([grads], HBM_SPEC, [SDS((N_CHIPS,) + grads.shape[2:], grads.dtype)], HBM_SPEC, N_CHIPS, build, ("sibling",))[0]


def _add_reduce_ici(stage, partial):
    def build(ins, outs):
        x, y, c = _coords()
        return [(ins[0].at[2 * px + py], outs[0].at[j], (px, py, c), outs[0].at[j]) for j, (px, py) in enumerate(_chips())]

    return stage.add([partial], HBM_SPEC, [SDS((N_CHIPS - 1,) + partial.shape[1:], partial.dtype)], HBM_SPEC, N_CHIPS - 1, build, ("chips",))[0]


def _add_swap_halves(stage, quarter):
    def build(ins, outs):
        x, y, c = _coords()
        return [(outs[0].at[c], outs[0].at[c], (x, y, 1 - c), outs[0].at[1 - c])]

    return stage.add([quarter], HBM_SPEC, [SDS(quarter.shape, quarter.dtype)], HBM_SPEC, 1, build, ("sibling",), alias=True)[0]


def _sum_pair(grads, from_sibling, place, name):
    _, _, rows, cols = grads.shape
    tr = min(rows, 512)

    def body(place_ref, g_ref, s_ref, o_ref):
        o_ref[0] = (g_ref[0, 0].astype(F32) + s_ref[0].astype(F32)).astype(BF16)

    spec = pl.BlockSpec((1, tr, cols), lambda k, i, place_ref: (k, i, 0))
    return pl.pallas_call(
        body, name=name, out_shape=SDS((N_CHIPS, rows, cols), BF16),
        grid_spec=pltpu.PrefetchScalarGridSpec(
            num_scalar_prefetch=1, grid=(N_CHIPS, rows // tr),
            in_specs=[pl.BlockSpec((1, 1, tr, cols), lambda k, i, place_ref: (k, place_ref[0], i, 0)), spec], out_specs=spec),
        compiler_params=_params("parallel", "parallel"),
    )(place, grads, from_sibling)


def _sum_quarter(grads, from_sibling, from_chips, place, name):
    _, _, rows, cols = grads.shape
    tr = min(rows, 512)

    def body(place_ref, g_ref, s_ref, c_ref, o_ref):
        acc = g_ref[0, 0].astype(F32) + s_ref[0].astype(F32)
        for j in range(N_CHIPS - 1):
            acc = acc + c_ref[j].astype(F32)
        o_ref[0] = acc

    return pl.pallas_call(
        body, name=name, out_shape=SDS((2, rows, cols), F32),
        grid_spec=pltpu.PrefetchScalarGridSpec(
            num_scalar_prefetch=1, grid=(rows // tr,),
            in_specs=[pl.BlockSpec((1, 1, tr, cols), lambda i, place_ref: (place_ref[1], place_ref[0], i, 0)),
                      pl.BlockSpec((1, tr, cols), lambda i, place_ref: (place_ref[1], i, 0)),
                      pl.BlockSpec((N_CHIPS - 1, tr, cols), lambda i, place_ref: (0, i, 0))],
            out_specs=pl.BlockSpec((1, tr, cols), lambda i, place_ref: (place_ref[0], i, 0))),
        compiler_params=_params("parallel"),
    )(place, grads, from_sibling, from_chips)


def _row_spec(ts, cols, col_block=0):
    return pl.BlockSpec((ts, cols), lambda *g: (g[0], col_block))


def _vec_spec(rows, cols):
    return pl.BlockSpec((rows, cols), lambda *g: (0, 0))


def _matmul(a, w, *, transposed, reduce_blocks, name, tm, outs, finish, extra=(), prepare=None, zero_first=(), parts=1, stage=None):
    m_rows = a.shape[0]
    nb, r, c = w.shape
    kb = c if transposed else r
    dims = (((1,), (1,)), ((), ())) if transposed else (((1,), (0,)), ((), ()))
    assert a.shape[1] == (nb * kb if reduce_blocks else kb) and m_rows % tm == 0 and tm % parts == 0
    n_extra, sub = len(extra), tm // parts

    def body(a_ref, w_ref, *rest):
        extra_refs, out_refs = rest[:n_extra], rest[n_extra:]
        if zero_first:
            @pl.when(pl.program_id(0) == 0)
            def _():
                for i in zero_first:
                    out_refs[i][...] = jnp.zeros_like(out_refs[i])

        def views(j):
            rows = pl.ds(j * sub, sub)
            return rows, [ref.at[rows] if ref.shape[0] == tm else ref for ref in extra_refs], [ref.at[rows] if ref.shape[0] == tm else ref for ref in out_refs]

        def product(j):
            rows, ex, ou = views(j)
            if reduce_blocks:
                acc = lax.dot_general(a_ref[rows, 0:kb], w_ref[0], dims, preferred_element_type=F32)
                for k in range(1, nb):
                    acc += lax.dot_general(a_ref[rows, k * kb:(k + 1) * kb], w_ref[k], dims, preferred_element_type=F32)
                return [(None, acc)]
            lhs = a_ref[rows, :] if prepare is None else prepare(a_ref.at[rows], ex, ou)
            return [(k, lax.dot_general(lhs, w_ref[k], dims, preferred_element_type=F32)) for k in range(nb)]

        pending = product(0)
        for j in range(parts):
            ahead = product(j + 1) if j + 1 < parts else None
            _, ex, ou = views(j)
            for k, acc in pending:
                finish(k, acc, ex, ou)
            pending = ahead

    return _hosted(
        stage, body, name=name, grid=(m_rows // tm,), out_shape=[s for s, _ in outs], out_specs=[s for _, s in outs],
        in_specs=[_row_spec(tm, a.shape[1]), pl.BlockSpec((nb, r, c), lambda *g: (0, 0, 0), pipeline_mode=pl.Buffered(1))] + [s for _, s in extra],
        compiler_params=_params("arbitrary"),
    )(a, w, *[e for e, _ in extra])


def _store_blocks(n):
    def finish(k, acc, extra_refs, out_refs):
        if k is None:
            out_refs[0][...] = acc.astype(out_refs[0].dtype)
        else:
            out_refs[0][:, k * n:(k + 1) * n] = acc.astype(out_refs[0].dtype)
    return finish


def _weight_grad(a, b, *, nb, name, tk, stage=None):
    s_rows, k1 = a.shape
    tn = b.shape[1] // nb
    assert k1 % tk == 0 and b.shape[1] % nb == 0

    def body(a_ref, b_ref, o_ref):
        o_ref[0] = lax.dot_general(a_ref[...], b_ref[...], (((0,), (0,)), ((), ())), preferred_element_type=F32).astype(o_ref.dtype)

    return _hosted(
        stage, body, name=name, grid=(nb, k1 // tk), out_shape=SDS((nb, k1, tn), BF16),
        in_specs=[pl.BlockSpec((s_rows, tk), lambda n, i: (0, i)), pl.BlockSpec((s_rows, tn), lambda n, i: (0, n))],
        out_specs=pl.BlockSpec((1, tk, tn), lambda n, i: (n, i, 0)), compiler_params=_params("parallel", "parallel"),
    )(a, b)


def _rms(x):
    r = lax.rsqrt(jnp.mean(x * x, axis=-1, keepdims=True) + EPS)
    return x * r, r


def _accumulate(acc_ref, row, value):
    acc_ref[row:row + 1, :] += jnp.sum(value, axis=0, keepdims=True)


def _gelu_parts(y):
    k0, k1 = 0.7978845608028654, 0.044715
    th = jnp.tanh(k0 * (y + k1 * (y * y * y)))
    gelu = 0.5 * y * (1.0 + th)
    dgelu = 0.5 * (1.0 + th) + 0.5 * y * (1.0 - th * th) * (k0 * (1.0 + 3.0 * k1 * (y * y)))
    return gelu, dgelu


def _one_minus_square(a, log_a):
    return -jnp.tanh(log_a) * (1.0 + a * a)


def _modulated_norm(x, mod_ref, g_ref, sc_row, sh_row):
    xh, _ = _rms(x)
    return ((xh * g_ref[...]) * (1.0 + mod_ref[sc_row:sc_row + 1, :]) + mod_ref[sh_row:sh_row + 1, :]).astype(BF16)


def _modulated_norm_backward(dh, x, dres, mod_ref, g_ref, acc_ref, sc_row):
    xh, r = _rms(x)
    _accumulate(acc_ref, 0, dh * (xh * g_ref[...]))
    _accumulate(acc_ref, 1, dh)
    dn = dh * (1.0 + mod_ref[sc_row:sc_row + 1, :])
    _accumulate(acc_ref, 2, dn * xh)
    dxh = dn * g_ref[...]
    return dres + r * (dxh - xh * jnp.mean(dxh * xh, axis=-1, keepdims=True))


def _seq_forward(proj, conv_w, vecs, w_a, w_x, w_p, *, ts, stage=None):
    s_rows = proj.shape[0]
    d = D_MODEL

    def body(x_ref, xh_ref, y_ref, u_ref, uh_ref, cw_ref, vec_ref, wa_ref, wx_ref, wp_ref,
             xr_o, hr_o, ga_o, p_o, pooled_o, carry, a_scr, b_scr):
        t = pl.program_id(0)

        @pl.when(t == 0)
        def _():
            carry[...] = jnp.zeros_like(carry)

        x = x_ref[...]
        halo = jnp.where(t == 0, 0.0, xh_ref[...])
        xx = jnp.concatenate([halo, x], axis=0)
        xr = vec_ref[0:1, :] + x * cw_ref[CONV_WIDTH - 1:CONV_WIDTH, :]
        for j in range(CONV_WIDTH - 1):
            xr = xr + pltpu.roll(xx, CONV_WIDTH - 1 - j, 0)[CONV_HALO:] * cw_ref[j:j + 1, :]
        xr_o[...] = xr
        xrb = xr.astype(BF16)
        zr = jnp.concatenate([jnp.dot(xrb[:, g * BLOCK:(g + 1) * BLOCK], wa_ref[g], preferred_element_type=F32)
                              for g in range(N_BLOCKS)], axis=1) + vec_ref[1:2, :]
        zi = jnp.concatenate([jnp.dot(xrb[:, g * BLOCK:(g + 1) * BLOCK], wx_ref[g], preferred_element_type=F32)
                              for g in range(N_BLOCKS)], axis=1) + vec_ref[2:3, :]
        r = jax.nn.sigmoid(zr)
        gate_i = jax.nn.sigmoid(zi)
        log_a = (-C_RG * r) * jax.nn.softplus(vec_ref[3:4, :])
        rows = lax.broadcasted_iota(jnp.int32, (ts, d), 0)
        a = jnp.exp(log_a)
        mult = jnp.where((rows == 0) & (t == 0), 1.0, jnp.sqrt(_one_minus_square(a, log_a)))
        a_scr[...] = a
        b_scr[...] = xr * gate_i * mult

        sub = lax.broadcasted_iota(jnp.int32, (8, d), 0)

        def chunk(i, h):
            a = a_scr[pl.ds(i * 8, 8), :]
            b = b_scr[pl.ds(i * 8, 8), :]
            for s in (1, 2, 4):
                keep = sub >= s
                b = jnp.where(keep, a * pltpu.roll(b, s, 0) + b, b)
                a = jnp.where(keep, a * pltpu.roll(a, s, 0), a)
            hh = a * h + b
            hr_o[pl.ds(i * 8, 8), :] = hh
            return hh[7:8, :]

        carry[...] = lax.fori_loop(0, ts // 8, chunk, carry[...])
        gelu, _ = _gelu_parts(y_ref[...])
        ga_o[...] = (gelu * hr_o[...]).astype(BF16)

        u = u_ref[...]
        uu = jnp.concatenate([jnp.where(t == 0, 0.0, uh_ref[...]), u], axis=0)
        pos = (rows[:, :BLOCK] + t * ts + 1).astype(F32)
        sums, have, parts = uu, 1, []
        for g, win in enumerate(POOL_WINDOWS):
            while have < win:
                sums = sums + pltpu.roll(sums, have, 0)
                have *= 2
            mean = sums[POOL_HALO:, g * BLOCK:(g + 1) * BLOCK] * (1.0 / jnp.minimum(pos, float(win)))
            parts.append(mean - u[:, g * BLOCK:(g + 1) * BLOCK])
        pb = jnp.concatenate(parts, axis=1).astype(BF16)
        p_o[...] = pb
        mixed = jnp.concatenate([jnp.dot(pb[:, g * BLOCK:(g + 1) * BLOCK], wp_ref[g], preferred_element_type=F32)
                                 for g in range(N_BLOCKS)], axis=1) + vec_ref[4:5, :]
        pooled_o[...] = (mixed * vec_ref[5:6, :]).astype(BF16)

    halo_x = pl.BlockSpec((CONV_HALO, d), lambda t: (jnp.maximum(t * (ts // CONV_HALO) - 1, 0), 0))
    halo_u = pl.BlockSpec((POOL_HALO, d), lambda t: (jnp.maximum(t * (ts // POOL_HALO) - 1, 0), 2))
    wspec = pl.BlockSpec((N_BLOCKS, BLOCK, BLOCK), lambda t: (0, 0, 0))
    return _hosted(
        stage, body, name="seq_forward", grid=(s_rows // ts,),
        out_shape=[SDS((s_rows, d), F32), SDS((s_rows, d), F32), SDS((s_rows, d), BF16), SDS((s_rows, d), BF16), SDS((s_rows, d), BF16)],
        in_specs=[_row_spec(ts, d, 0), halo_x, _row_spec(ts, d, 1), _row_spec(ts, d, 2), halo_u, _vec_spec(CONV_WIDTH, d), _vec_spec(8, d),
                  wspec, wspec, wspec],
        out_specs=[_row_spec(ts, d)] * 5,
        scratch_shapes=[pltpu.VMEM((1, d), F32), pltpu.VMEM((ts, d), F32), pltpu.VMEM((ts, d), F32)],
        compiler_params=_params("arbitrary"),
    )(proj, proj, proj, proj, proj, conv_w, vecs, w_a, w_x, w_p)


def _branches(ga, pooled, proj, w_a, w_b, *, ts, stage=None):
    s_rows, d = ga.shape

    def body(a_ref, p_ref, ga_ref, gb_ref, wa_ref, wb_ref, merged_o, sa_o, sb_o, ta_o, tb_o):
        bra = jnp.dot(a_ref[...], wa_ref[0], preferred_element_type=F32)
        brb = jnp.dot(p_ref[...], wb_ref[0], preferred_element_type=F32)
        sa, sb = jax.nn.sigmoid(ga_ref[...]), jax.nn.sigmoid(gb_ref[...])
        merged_o[...] = (sa * bra + sb * brb).astype(BF16)
        sa_o[...] = sa.astype(BF16)
        sb_o[...] = sb.astype(BF16)
        ta_o[...] = (bra * (sa * (1.0 - sa))).astype(BF16)
        tb_o[...] = (brb * (sb * (1.0 - sb))).astype(BF16)

    rows = _row_spec(ts, d)
    wspec = pl.BlockSpec((1, d, d), lambda *g: (0, 0, 0), pipeline_mode=pl.Buffered(1))
    return _hosted(
        stage, body, name="branches", grid=(s_rows // ts,), out_shape=[SDS((s_rows, d), BF16)] * 5,
        in_specs=[rows, rows, _row_spec(ts, d, 3), _row_spec(ts, d, 4), wspec, wspec], out_specs=[rows] * 5,
        compiler_params=_params("arbitrary"),
    )(ga, pooled, proj, proj, w_a, w_b)


def _seq_backward_by_block(proj, xr, hr, p, dga, dpooled, dgab, conv_w, vecs, w_a, w_x, w_p, *, ts, stage=None):
    s_rows = proj.shape[0]
    d = D_MODEL
    n_t = s_rows // ts
    nn, nt, tn = (((1,), (0,)), ((), ())), (((1,), (1,)), ((), ())), (((0,), (0,)), ((), ()))

    def dot(lhs, rhs, dims):
        return lax.dot_general(lhs, rhs, dims, preferred_element_type=F32)

    def body(x_ref, y_ref, xr_ref, hr_ref, hh_ref, p_ref, dga_ref, dpl_ref, dgab_ref, cw_ref, vec_ref, wa_ref, wx_ref, wp_ref,
             dproj_o, acc_o, dwa_o, dwx_o, dwp_o, g_carry, dxr_carry, q_carry, a_scr, b_scr, g_scr, a_keep, r_keep, i_keep, m_keep):
        step = pl.program_id(0)
        t = n_t - 1 - step

        @pl.when(step == 0)
        def _():
            for ref in (acc_o, dwa_o, dwx_o, dwp_o, g_carry, dxr_carry, q_carry):
                ref[...] = jnp.zeros_like(ref)

        rows = lax.broadcasted_iota(jnp.int32, (ts, BLOCK), 0)
        start = (rows == 0) & (t == 0)

        def add_sum(row, lanes, value):
            acc_o[row:row + 1, lanes] += jnp.sum(value, axis=0, keepdims=True)

        for g in range(N_BLOCKS):
            lanes = slice(g * BLOCK, (g + 1) * BLOCK)
            xrb = xr_ref[:, lanes].astype(BF16)
            r = jax.nn.sigmoid(dot(xrb, wa_ref[g], nn) + vec_ref[1:2, lanes])
            gate_i = jax.nn.sigmoid(dot(xrb, wx_ref[g], nn) + vec_ref[2:3, lanes])
            log_a = (-C_RG * r) * jax.nn.softplus(vec_ref[3:4, lanes])
            a = jnp.exp(log_a)
            a_keep[:, lanes], r_keep[:, lanes], i_keep[:, lanes], m_keep[:, lanes] = a, r, gate_i, _one_minus_square(a, log_a)
            gelu, dgelu = _gelu_parts(y_ref[:, lanes])
            dga_t = dga_ref[:, lanes]
            dproj_o[:, d + g * BLOCK:d + (g + 1) * BLOCK] = (dga_t * hr_ref[:, lanes] * dgelu).astype(BF16)
            a_scr[:, lanes] = jnp.where(rows == ts - 1, 1.0, pltpu.roll(a, ts - 1, 0))
            b_scr[:, lanes] = dga_t * gelu

        sub = lax.broadcasted_iota(jnp.int32, (8, d), 0)

        def chunk(i, g_next):
            at = pl.multiple_of((ts // 8 - 1 - i) * 8, 8)
            aa = a_scr[pl.ds(at, 8), :]
            bb = b_scr[pl.ds(at, 8), :]
            for s in (1, 2, 4):
                keep = sub < 8 - s
                bb = jnp.where(keep, bb + aa * pltpu.roll(bb, 8 - s, 0), bb)
                aa = jnp.where(keep, aa * pltpu.roll(aa, 8 - s, 0), aa)
            gg = aa * g_next + bb
            g_scr[pl.ds(at, 8), :] = gg
            return gg[0:1, :]

        g_first = lax.fori_loop(0, ts // 8, chunk, g_carry[...])
        g_carry[...] = a_keep[0:1, :] * g_first

        pos = (rows + t * ts + 1).astype(F32)
        for g, win in enumerate(POOL_WINDOWS):
            lanes = slice(g * BLOCK, (g + 1) * BLOCK)
            gs, xr, a, r, gate_i, m_square = g_scr[:, lanes], xr_ref[:, lanes], a_keep[:, lanes], r_keep[:, lanes], i_keep[:, lanes], m_keep[:, lanes]
            sp = jax.nn.softplus(vec_ref[3:4, lanes])
            mult = jnp.where(start, 1.0, jnp.sqrt(m_square))
            h_before = jnp.where(t == 0, 0.0, hh_ref[CONV_HALO - 1:CONV_HALO, lanes])
            h_prev = jnp.where(rows == 0, h_before, pltpu.roll(hr_ref[:, lanes], 1, 0))
            gm = gs * mult
            dxr = gm * gate_i
            d_i = gm * xr
            d_mult = (gs * xr) * gate_i
            dlog_a = (gs * h_prev) * a - jnp.where(start, 0.0, d_mult * (a * a) * lax.rsqrt(m_square))
            dzr = (dlog_a * (-C_RG * sp)) * (r * (1.0 - r))
            dzi = d_i * (gate_i * (1.0 - gate_i))
            add_sum(5, lanes, dzr)
            add_sum(6, lanes, dzi)
            acc_o[7:8, lanes] += jnp.sum(dlog_a * r, axis=0, keepdims=True) * (-C_RG * jax.nn.sigmoid(vec_ref[3:4, lanes]))
            xrb, dzrb, dzib = xr.astype(BF16), dzr.astype(BF16), dzi.astype(BF16)
            dwa_o[g] += dot(xrb, dzrb, tn)
            dwx_o[g] += dot(xrb, dzib, tn)
            dxr = dxr + dot(dzrb, wa_ref[g], nt) + dot(dzib, wx_ref[g], nt)

            x = x_ref[:, lanes]
            ext = jnp.concatenate([dxr, dxr_carry[:, lanes]], axis=0)
            dx = dxr * cw_ref[CONV_WIDTH - 1:CONV_WIDTH, lanes]
            add_sum(CONV_WIDTH - 1, lanes, x * dxr)
            for j in range(CONV_WIDTH - 1):
                ahead = pltpu.roll(ext, ts + CONV_HALO - (CONV_WIDTH - 1 - j), 0)[:ts]
                dx = dx + ahead * cw_ref[j:j + 1, lanes]
                add_sum(j, lanes, x * ahead)
            add_sum(4, lanes, dxr)
            dxr_carry[:, lanes] = dxr[0:CONV_HALO, :]
            dproj_o[:, lanes] = dx.astype(BF16)

            pb = p_ref[:, lanes]
            mixed = dot(pb, wp_ref[g], nn) + vec_ref[4:5, lanes]
            dpl = dpl_ref[:, lanes]
            add_sum(9, lanes, dpl * mixed)
            dmixed = dpl * vec_ref[5:6, lanes]
            add_sum(8, lanes, dmixed)
            dmb = dmixed.astype(BF16)
            dwp_o[g] += dot(pb, dmb, tn)
            dp = dot(dmb, wp_ref[g], nt)
            q = dp * (1.0 / jnp.minimum(pos, float(win)))
            sums, have = jnp.concatenate([q, q_carry[:, lanes]], axis=0), 1
            while have < win:
                sums = sums + pltpu.roll(sums, ts + POOL_HALO - have, 0)
                have *= 2
            q_carry[:, lanes] = q[0:POOL_HALO, :]
            dproj_o[:, 2 * d + g * BLOCK:2 * d + (g + 1) * BLOCK] = (sums[:ts] - dp).astype(BF16)

        dproj_o[:, 3 * d:5 * d] = dgab_ref[...]

    def rev(cols, col_block=0):
        return pl.BlockSpec((ts, cols), lambda i: (n_t - 1 - i, col_block))

    halo_h = pl.BlockSpec((CONV_HALO, d), lambda i: (jnp.maximum((n_t - 1 - i) * (ts // CONV_HALO) - 1, 0), 0))
    wspec = pl.BlockSpec((N_BLOCKS, BLOCK, BLOCK), lambda i: (0, 0, 0))
    return _hosted(
        stage, body, name="seq_backward", grid=(n_t,),
        out_shape=[SDS((s_rows, 5 * d), BF16), SDS((16, d), F32)] + [SDS((N_BLOCKS, BLOCK, BLOCK), F32)] * 3,
        in_specs=[rev(d, 0), rev(d, 1), rev(d), rev(d), halo_h, rev(d), rev(d), rev(d), rev(2 * d), _vec_spec(CONV_WIDTH, d), _vec_spec(8, d),
                  wspec, wspec, wspec],
        out_specs=[rev(5 * d), _vec_spec(16, d), wspec, wspec, wspec],
        scratch_shapes=[pltpu.VMEM((1, d), F32), pltpu.VMEM((CONV_HALO, d), F32), pltpu.VMEM((POOL_HALO, d), F32)] + [pltpu.VMEM((ts, d), F32)] * 7,
        compiler_params=_params("arbitrary"),
    )(proj, proj, xr, hr, hr, p, dga, dpooled, dgab, conv_w, vecs, w_a, w_x, w_p)


def _adamw_math(w, g, m, v):
    m = ADAM_B1 * m + (1.0 - ADAM_B1) * g
    v = ADAM_B2 * v + (1.0 - ADAM_B2) * jnp.square(g)
    m_hat = m / (1.0 - ADAM_B1 ** ADAM_STEP)
    v_hat = v / (1.0 - ADAM_B2 ** ADAM_STEP)
    delta = -ADAM_LR * (m_hat / (jnp.sqrt(v_hat) + ADAM_EPS) + ADAM_WD * w)
    return delta, m, v


def _adamw(w, g, m, v, name):
    rows, cols = w.shape
    tr = min(rows, 512)

    def body(w_ref, g_ref, m_ref, v_ref, g_out, d_ref, mo_ref, vo_ref):
        g = g_ref[...]
        g_out[...] = g
        d_ref[...], mo_ref[...], vo_ref[...] = _adamw_math(w_ref[...], g, m_ref[...], v_ref[...])

    spec = pl.BlockSpec((tr, cols), lambda i: (i, 0))
    return pl.pallas_call(
        body, name=name, grid=(rows // tr,), out_shape=[SDS((rows, cols), F32)] * 4, in_specs=[spec] * 4, out_specs=[spec] * 4,
        compiler_params=_params("parallel"),
    )(w, g, m, v)


def _prologue(c, conv_shard, w_ada, b_cols, w_in_halves, gate_halves):
    n_cols = w_ada.shape[1]
    n_g = len(gate_halves)
    half_rows = w_in_halves.shape[1] // 2
    n_sems = 3 * N_DEV + N_CHIPS * n_g + 5 + 2 + 3

    def body(c_ref, conv_ref, w_ref, b_ref, win_ref, *rest):
        gate_refs = rest[:n_g]
        act_o, mod_o, conv_o, win_o = rest[n_g:n_g + 4]
        gate_o = rest[n_g + 4:2 * n_g + 4]
        c_all, mod_cols, send_sems, recv_sems = rest[2 * n_g + 4:]
        x, y, c = _coords()
        me, k_me = 4 * x + 2 * y + c, 2 * x + y
        sibling, x_nbr, y_nbr = (x, y, 1 - c), (1 - x, y, c), (x, 1 - y, c)
        k_x, k_y, k_d = 2 * (1 - x) + y, 2 * x + (1 - y), 2 * (1 - x) + (1 - y)
        counter = iter(range(n_sems))

        def copy(src, dst, to, landing):
            s = next(counter)
            out = pltpu.make_async_remote_copy(src_ref=src, dst_ref=dst, send_sem=send_sems.at[s], recv_sem=recv_sems.at[s],
                                               device_id=to, device_id_type=MESH)
            arrival = pltpu.make_async_remote_copy(src_ref=landing, dst_ref=landing, send_sem=send_sems.at[s], recv_sem=recv_sems.at[s],
                                                   device_id=(x, y, c), device_id_type=MESH)
            return out, arrival

        def allgather(src, out):
            s = next(counter)
            own = pltpu.make_async_copy(src, out.at[me], send_sems.at[s])
            peers = [copy(src, out.at[me], (_flip(x, r & 4), _flip(y, r & 2), _flip(c, r & 1)), out.at[me ^ r]) for r in range(1, N_DEV)]
            return own, peers

        def start(group):
            own, peers = group
            own.start()
            for out, _ in peers:
                out.start()

        def finish(group):
            own, peers = group
            for out, arrival in peers:
                arrival.wait_recv()
                out.wait_send()
            own.wait()

        rows_a, rows_b = pl.ds(0, half_rows), pl.ds(half_rows, half_rows)
        c_group, conv_group = allgather(c_ref, c_all), allgather(conv_ref, conv_o)
        gates = []
        for g_ref, g_out in zip(gate_refs, gate_o):
            gates.append(copy(g_ref, g_out.at[k_me], sibling, g_out.at[k_me]))
            gates += [copy(g_ref.at[c], g_out.at[k_me, c], (px, py, c), g_out.at[2 * px + py, c]) for px, py in _chips()]
        own_in = copy(win_ref, win_o.at[k_me], sibling, win_o.at[k_me])
        to_x = [copy(win_ref.at[c, rows], win_o.at[k_me, c, rows], x_nbr, win_o.at[k_x, c, rows]) for rows in (rows_a, rows_b)]
        to_y = [copy(win_ref.at[c, rows], win_o.at[k_me, c, rows], y_nbr, win_o.at[k_y, c, rows]) for rows in (rows_b, rows_a)]
        mod_group = allgather(mod_cols, mod_o)
        relay_y = copy(win_o.at[k_x, c, rows_a], win_o.at[k_x, c, rows_a], y_nbr, win_o.at[k_d, c, rows_a])
        relay_x = copy(win_o.at[k_y, c, rows_b], win_o.at[k_y, c, rows_b], x_nbr, win_o.at[k_d, c, rows_b])
        sib_x, sib_y, sib_d = (copy(win_o.at[k, c], win_o.at[k, c], sibling, win_o.at[k, 1 - c]) for k in (k_x, k_y, k_d))

        start(c_group)
        start(conv_group)
        for out, _ in gates + [own_in] + to_x + to_y:
            out.start()
        finish(c_group)
        cv = jnp.concatenate([c_all[r] for r in range(N_DEV)], axis=0)
        act = cv * jax.nn.sigmoid(cv)
        act_o[...] = act
        mod_cols[...] = jnp.dot(act.astype(BF16), w_ref[...].astype(BF16), preferred_element_type=F32) + b_ref[...]
        start(mod_group)
        to_x[0][1].wait_recv()
        relay_y[0].start()
        to_y[0][1].wait_recv()
        relay_x[0].start()
        finish(mod_group)
        finish(conv_group)
        to_x[1][1].wait_recv()
        sib_x[0].start()
        to_y[1][1].wait_recv()
        sib_y[0].start()
        for out, arrival in gates + [own_in]:
            arrival.wait_recv()
            out.wait_send()
        for out, arrival in (relay_y, relay_x):
            arrival.wait_recv()
            out.wait_send()
        sib_d[0].start()
        for out, _ in to_x + to_y:
            out.wait_send()
        for out, arrival in (sib_x, sib_y, sib_d):
            arrival.wait_recv()
            out.wait_send()

    gate_shapes = [SDS((N_CHIPS,) + g.shape, g.dtype) for g in gate_halves]
    return pl.pallas_call(
        body, name="prologue",
        out_shape=[SDS((N_DEV, c.shape[1]), F32), SDS((N_DEV, N_DEV, n_cols), F32), SDS((N_DEV,) + conv_shard.shape, F32),
                   SDS((N_CHIPS,) + w_in_halves.shape, w_in_halves.dtype)] + gate_shapes,
        in_specs=[VMEM_SPEC] * 4 + [HBM_SPEC] * (1 + n_g), out_specs=[VMEM_SPEC] * 3 + [HBM_SPEC] * (1 + n_g),
        scratch_shapes=[pltpu.VMEM((N_DEV,) + c.shape, F32), pltpu.VMEM((N_DEV, n_cols), F32),
                        pltpu.SemaphoreType.DMA((n_sems,)), pltpu.SemaphoreType.DMA((n_sems,))],
        compiler_params=pltpu.CompilerParams(vmem_limit_bytes=VMEM_LIMIT_V7X),
    )(c, conv_shard, w_ada, b_cols, w_in_halves, *gate_halves)


def _ada_backward(act_t, dmod_cols, w, m, v):
    rows, cols = w.shape
    tn = 512

    def body(a_ref, dm_ref, w_ref, m_ref, v_ref, g_ref, d_ref, mo_ref, vo_ref):
        g = jnp.dot(a_ref[...].astype(BF16), dm_ref[...].astype(BF16), preferred_element_type=F32)
        g_ref[...] = g
        d_ref[...], mo_ref[...], vo_ref[...] = _adamw_math(w_ref[...], g, m_ref[...], v_ref[...])

    spec = pl.BlockSpec((rows, tn), lambda j: (0, j))
    return pl.pallas_call(
        body, name="ada_backward", grid=(cols // tn,), out_shape=[SDS((rows, cols), F32)] * 4,
        in_specs=[pl.BlockSpec((rows, N_DEV), lambda j: (0, 0)), pl.BlockSpec((N_DEV, tn), lambda j: (0, j)), spec, spec, spec],
        out_specs=[spec] * 4, compiler_params=_params("parallel"),
    )(act_t, dmod_cols, w, m, v)


def _sum_rows(gathered, n_loss):
    n = gathered.shape[2]

    def body(g_ref, o_ref):
        acc = g_ref[0]
        for r in range(1, N_DEV):
            acc = acc + g_ref[r]
        o_ref[...] = acc
        o_ref[:, n - n_loss:n] = jnp.broadcast_to(jnp.sum(acc[:, n - n_loss:n], axis=1, keepdims=True), (1, n_loss))

    return pl.pallas_call(body, name="sum_rows", out_shape=SDS((1, n), F32), in_specs=[VMEM_SPEC], out_specs=VMEM_SPEC)(gathered)


def _adamw_vectors(total, offsets, separate, params):
    n_p = len(params)
    apart = [g for g in separate if g is not None]

    def body(*refs):
        total_ref, apart_refs = refs[0], list(refs[1:1 + len(apart)])
        ins, outs = refs[1 + len(apart):1 + len(apart) + 3 * n_p], refs[1 + len(apart) + 3 * n_p:]
        for i in range(n_p):
            w_ref, m_ref, v_ref = ins[3 * i:3 * i + 3]
            g = apart_refs.pop(0)[...] if offsets[i] is None else total_ref[:, offsets[i]:offsets[i] + w_ref.shape[1]]
            outs[4 * i][...] = g
            outs[4 * i + 1][...], outs[4 * i + 2][...], outs[4 * i + 3][...] = _adamw_math(w_ref[...], g, m_ref[...], v_ref[...])

    flat = [a for p in params for a in p]
    res = pl.pallas_call(
        body, name="adamw_vectors", out_shape=[SDS(p[0].shape, F32) for p in params for _ in range(4)],
        in_specs=[VMEM_SPEC] * (1 + len(apart) + len(flat)), out_specs=[VMEM_SPEC] * (4 * n_p),
    )(total, *apart, *flat)
    return [tuple(res[4 * i:4 * i + 4]) for i in range(n_p)]


def _blocks_to_pieces(w):
    nb, rows, n = w.shape
    q = rows // N_CHIPS
    return w.reshape(nb, N_CHIPS, q, n).transpose(1, 0, 2, 3).reshape(N_CHIPS, 2, nb // 2, q, n)


def _pieces_to_blocks(w):
    n_chips, _, half, q, n = w.shape
    return w.reshape(n_chips, 2 * half, q, n).transpose(1, 0, 2, 3).reshape(2 * half, n_chips * q, n)


def kernel(x, c, norm_mix_g, norm_mlp_g, w_ada, b_ada, w_in, conv_w, conv_b, w_rg_a, b_rg_a, w_rg_x, b_rg_x, a_param, w_branch_a, w_pool, b_pool, pool_scale, w_branch_b, w_out, w_up, w_down, final_g, loss_target, m_norm_mix_g, m_norm_mlp_g, m_w_ada, m_b_ada, m_w_in, m_conv_w, m_conv_b, m_w_rg_a, m_b_rg_a, m_w_rg_x, m_b_rg_x, m_a_param, m_w_branch_a, m_w_pool, m_b_pool, m_pool_scale, m_w_branch_b, m_w_out, m_w_up, m_w_down, m_final_g, v_norm_mix_g, v_norm_mlp_g, v_w_ada, v_b_ada, v_w_in, v_conv_w, v_conv_b, v_w_rg_a, v_b_rg_a, v_w_rg_x, v_b_rg_x, v_a_param, v_w_branch_a, v_w_pool, v_b_pool, v_pool_scale, v_w_branch_b, v_w_out, v_w_up, v_w_down, v_final_g):
    d = D_MODEL
    s_rows = x.shape[1]
    ts, tsq = min(512, s_rows), min(256, s_rows)
    xi, yi, ci = _coords()
    k_me, me = 2 * xi + yi, 4 * xi + 2 * yi + ci
    ada_cols = w_ada.shape[2]
    conv_cols = conv_w.shape[2]
    n_in, n_up = w_in.shape[2], w_up.shape[2]

    names = ("w_in", "w_up", "w_down", "w_a", "w_b", "w_out", "rg_a", "rg_x", "pool")
    mine = dict(zip(names, (w_in[0], w_up[0], w_down[0], w_branch_a[0], w_branch_b[0], w_out[0], w_rg_a[0], w_rg_x[0], w_pool[0])))
    moments_m = dict(zip(names, (m_w_in[0], m_w_up[0], m_w_down[0], m_w_branch_a[0], m_w_branch_b[0], m_w_out[0], m_w_rg_a[0], m_w_rg_x[0], m_w_pool[0])))
    moments_v = dict(zip(names, (v_w_in[0], v_w_up[0], v_w_down[0], v_w_branch_a[0], v_w_branch_b[0], v_w_out[0], v_w_rg_a[0], v_w_rg_x[0], v_w_pool[0])))
    block_weights, squares = ("rg_a", "rg_x", "pool"), ("w_a", "w_b", "w_out")
    place = jnp.stack([ci, k_me]).astype(jnp.int32)
    g_mix, g_mlp, g_fin = norm_mix_g, norm_mlp_g, final_g.reshape(1, d)
    x0, target = x[0], loss_target[0]
    wg = functools.partial(_weight_grad, tk=min(1024, d))
    plain, back, back_sum = (dict(transposed=t, reduce_blocks=r) for t, r in ((False, False), (True, False), (True, True)))
    rows_d, vec8, vec1 = _row_spec(ts, d), _vec_spec(8, d), _vec_spec(1, d)
    tw = min(1024, s_rows)
    halves_of_tile = 2

    def out_rows(cols, dtype, t=ts):
        return SDS((s_rows, cols), dtype), _row_spec(t, cols)

    sums_out = (SDS((8, d), F32), vec8)

    def halves(n):
        return mine[n].astype(BF16).reshape(2, -1, mine[n].shape[-1])

    def blocks(w):
        return _pieces_to_blocks(w.reshape(N_CHIPS, 2, N_BLOCKS // 2, BLOCK // N_CHIPS, BLOCK))

    act_all, mod_all, conv_all, w_in_whole, *gate_wholes = _prologue(
        c, conv_w[0], w_ada[0], lax.dynamic_slice_in_dim(b_ada, k_me * ada_cols, ada_cols, axis=1), halves("w_in"), [halves(n) for n in block_weights])
    conv_full = conv_all[0::2].transpose(1, 0, 2).reshape(CONV_WIDTH, d)
    mod_mine = lax.dynamic_index_in_dim(mod_all, me, axis=1, keepdims=False)[0::2]
    mod = jnp.concatenate([mod_mine.reshape(N_MOD, d), jnp.zeros((8 - N_MOD, d), F32)], axis=0)
    w_in_all = w_in_whole.reshape(N_CHIPS, d, n_in)
    vecs = jnp.concatenate([conv_b, b_rg_a, b_rg_x, a_param, b_pool, pool_scale, jnp.zeros((2, d), F32)], axis=0)

    def norm_first(a_ref, extra_refs, out_refs):
        h = _modulated_norm(a_ref[...], extra_refs[0], extra_refs[1], 1, 0)
        out_refs[1][...] = h
        return h

    stage = _Stage()
    i_blk = [_add_gather_d2d(stage, g) for g in gate_wholes]
    i_sq = [_add_gather_ici(stage, halves(n)) for n in squares]
    down_rows = mine["w_down"].shape[0] // 8
    i_down = _add_gather_ici(stage, halves("w_down"), rows=(0, down_rows))
    (proj, h1), got = _matmul(x0, w_in_all, name="proj_in", tm=ts, extra=[(mod, vec8), (g_mix, vec1)], prepare=norm_first,
                              outs=[out_rows(N_CHIPS * n_in, F32), out_rows(d, BF16)], finish=_store_blocks(n_in), parts=halves_of_tile,                              stage=stage, **plain)
    w_rg_a_all, w_rg_x_all, w_pool_all = (blocks(got[i]) for i in i_blk)
    stage = _Stage()
    i_sq = [_add_gather_d2d(stage, got[i]) for i in i_sq]
    i_up = _add_gather_ici(stage, halves("w_up"))
    i_down = _add_gather_ici(stage, halves("w_down"), rows=(down_rows, down_rows), whole=got[i_down])
    (xr, hr, ga, p, pooled), got = _seq_forward(proj, conv_full, vecs, w_rg_a_all, w_rg_x_all, w_pool_all, ts=tsq, stage=stage)
    w_a_all, w_b_all, w_out_all = (got[i].reshape(1, d, d) for i in i_sq)
    stage = _Stage()
    i_up = _add_gather_d2d(stage, got[i_up])
    i_down = _add_gather_ici(stage, halves("w_down"), rows=(2 * down_rows, 2 * down_rows), whole=got[i_down])
    (merged, s_a, s_b, t_a, t_b), got = _branches(ga, pooled, proj, w_a_all, w_b_all, ts=ts, stage=stage)
    w_up_all = got[i_up].reshape(N_CHIPS, d, n_up)
    stage = _Stage()
    i_down = _add_gather_d2d(stage, got[i_down])

    def residual_norm(k, acc, extra_refs, out_refs):
        x_ref, mod_ref, g_ref = extra_refs
        out_refs[0][...] = acc.astype(BF16)
        x2_t = x_ref[...] + mod_ref[2:3, :] * acc
        out_refs[1][...] = x2_t
        out_refs[2][...] = _modulated_norm(x2_t, mod_ref, g_ref, 4, 3)

    (mo, x2, h2), got = _matmul(merged, w_out_all, name="mix_out", tm=ts, extra=[(x0, rows_d), (mod, vec8), (g_mlp, vec1)],
                                outs=[out_rows(d, BF16), out_rows(d, F32), out_rows(d, BF16)], finish=residual_norm, parts=halves_of_tile,                                stage=stage, transposed=False, reduce_blocks=True)
    w_down_all = got[i_down].reshape(N_CHIPS, D_FF // N_CHIPS, d)

    def relu_squared(k, acc, extra_refs, out_refs):
        out_refs[0][:, k * n_up:(k + 1) * n_up] = jnp.square(jnp.maximum(acc, 0.0)).astype(BF16)

    ff, = _matmul(h2, w_up_all, name="mlp_up", tm=ts, outs=[out_rows(D_FF, BF16)], finish=relu_squared, parts=halves_of_tile, **plain)

    def loss_head(k, acc, extra_refs, out_refs):
        x2_ref, mod_ref, g_ref, t_ref = extra_refs
        dx_ref, df_ref, acc_ref = out_refs
        xh, r = _rms(x2_ref[...] + mod_ref[5:6, :] * acc)
        err = xh * g_ref[...] - t_ref[...]
        dy = err / d
        dxh = dy * g_ref[...]
        dx3_t = r * (dxh - xh * jnp.mean(dxh * xh, axis=-1, keepdims=True))
        dx_ref[...] = dx3_t
        df_ref[...] = (dx3_t * mod_ref[5:6, :]).astype(BF16)
        _accumulate(acc_ref, 0, dy * xh)
        _accumulate(acc_ref, 1, dx3_t * acc)
        _accumulate(acc_ref, 2, jnp.square(err) * (0.5 / d))

    dx3, dffo, sums_head = _matmul(ff, w_down_all, name="mlp_down", tm=ts, extra=[(x2, rows_d), (mod, vec8), (g_fin, vec1), (target, rows_d)],
                                   outs=[out_rows(d, F32), out_rows(d, BF16), sums_out], finish=loss_head, zero_first=(2,), parts=halves_of_tile,                                   transposed=False, reduce_blocks=True)

    partial_of, from_sibling, chip_sum, from_chips, half_done, quarter = {}, {}, {}, {}, {}, {}

    def publish(n, g):
        g = _blocks_to_pieces(g).astype(BF16) if n in block_weights else g
        partial_of[n] = g.reshape(N_CHIPS, 2, -1, mine[n].shape[-1])

    def exchange(to_sibling=(), to_chips=(), swap=()):
        st = _Stage()
        slots = [(n, from_sibling, _add_reduce_d2d(st, partial_of[n])) for n in to_sibling]
        slots += [(n, from_chips, _add_reduce_ici(st, chip_sum[n])) for n in to_chips]
        slots += [(n, quarter, _add_swap_halves(st, half_done[n])) for n in swap]
        return st, slots

    def collect(slots, outs):
        for n, where, i in slots:
            where[n] = outs[i]

    def sum_pairs(*ns):
        for n in ns:
            chip_sum[n] = _sum_pair(partial_of[n], from_sibling[n], place, "sum_pair_" + n)

    def sum_quarters(*ns):
        for n in ns:
            half_done[n] = _sum_quarter(partial_of[n], from_sibling[n], from_chips[n], place, "sum_quarter_" + n)

    publish("w_down", wg(ff, dffo, nb=1, name="grad_w_down"))

    def relu_squared_backward(k, acc, extra_refs, out_refs):
        cols = slice(k * n_up, (k + 1) * n_up)
        out_refs[0][:, cols] = (acc * (2.0 * jnp.sqrt(extra_refs[0][:, cols].astype(F32)))).astype(BF16)

    stage, slots = exchange(to_sibling=["w_down"])
    (dup,), got = _matmul(dffo, w_down_all, name="d_mlp_down", tm=ts, extra=[(ff, _row_spec(ts, D_FF))], outs=[out_rows(D_FF, BF16)],
                          finish=relu_squared_backward, parts=halves_of_tile, stage=stage, **back)
    collect(slots, got)
    sum_pairs("w_down")

    def norm_mlp_backward(k, acc, extra_refs, out_refs):
        x2_ref, dres_ref, mod_ref, g_ref, mo_ref = extra_refs
        dx_ref, dmo_ref, acc_ref = out_refs
        dx =_modulated_norm_backward(acc, x2_ref[...], dres_ref[...], mod_ref, g_ref, acc_ref, 4)
        dx_ref[...] = dx
        dmo_ref[...] = (dx * mod_ref[2:3, :]).astype(BF16)
        _accumulate(acc_ref, 3, dx * mo_ref[...].astype(F32))

    stage, slots = exchange(to_chips=["w_down"])
    (dx2, dmo, sums_mlp), got = _matmul(dup, w_up_all, name="d_mlp_up", tm=ts, stage=stage, finish=norm_mlp_backward,
                                        extra=[(x2, rows_d), (dx3, rows_d), (mod, vec8), (g_mlp, vec1), (mo, rows_d)],
                                        outs=[out_rows(d, F32), out_rows(d, BF16), sums_out], zero_first=(2,), parts=halves_of_tile, **back_sum)
    collect(slots, got)
    sum_quarters("w_down")
    publish("w_up", wg(h2, dup, nb=N_CHIPS, name="grad_w_up"))
    publish("w_out", wg(merged, dmo, nb=1, name="grad_w_out"))

    def merge_backward(k, acc, extra_refs, out_refs):
        sa_ref, sb_ref, ta_ref, tb_ref = extra_refs
        out_refs[0][...] = (acc * sa_ref[...].astype(F32)).astype(BF16)
        out_refs[1][...] = (acc * sb_ref[...].astype(F32)).astype(BF16)
        out_refs[2][:, 0:d] = (acc * ta_ref[...].astype(F32)).astype(BF16)
        out_refs[2][:, d:2 * d] = (acc * tb_ref[...].astype(F32)).astype(BF16)

    stage, slots = exchange(to_sibling=["w_up", "w_out"], swap=["w_down"])
    (dbr_a, dbr_b, dgab), got = _matmul(dmo, w_out_all, name="d_mix_out", tm=ts, stage=stage, finish=merge_backward,
                                        extra=[(s_a, rows_d), (s_b, rows_d), (t_a, rows_d), (t_b, rows_d)],
                                        outs=[out_rows(d, BF16), out_rows(d, BF16), out_rows(2 * d, BF16)], parts=halves_of_tile, **back_sum)
    collect(slots, got)
    publish("w_a", wg(ga, dbr_a, nb=1, name="grad_w_branch_a"))
    publish("w_b", wg(pooled, dbr_b, nb=1, name="grad_w_branch_b"))
    dga, = _matmul(dbr_a, w_a_all, name="d_branch_a", tm=tw, outs=[out_rows(d, F32, tw)], finish=_store_blocks(d), **back_sum)
    dpooled, = _matmul(dbr_b, w_b_all, name="d_branch_b", tm=tw, outs=[out_rows(d, F32, tw)], finish=_store_blocks(d), **back_sum)
    sum_pairs("w_up", "w_out")
    stage, slots = exchange(to_sibling=["w_a", "w_b"], to_chips=["w_up", "w_out"])
    (dproj, sums_seq, d_rg_a, d_rg_x, d_pool), got = _seq_backward_by_block(proj, xr, hr, p, dga, dpooled, dgab, conv_full, vecs,
                                                                            w_rg_a_all, w_rg_x_all, w_pool_all, ts=tsq, stage=stage)
    collect(slots, got)
    sum_quarters("w_up", "w_out")
    sum_pairs("w_a", "w_b")
    for n, g in zip(block_weights, (d_rg_a, d_rg_x, d_pool)):
        publish(n, g)
    stage, slots = exchange(to_sibling=block_weights, to_chips=["w_a", "w_b"], swap=["w_up", "w_out"])
    g_in, got = wg(h1, dproj, nb=N_CHIPS, name="grad_w_in", stage=stage)
    collect(slots, got)
    publish("w_in", g_in)
    sum_pairs(*block_weights)
    sum_quarters("w_a", "w_b")
    stage, slots = exchange(to_sibling=["w_in"])
    collect(slots, _run_stage(stage, "exchange_w_in"))
    sum_pairs("w_in")

    def norm_mix_backward(k, acc, extra_refs, out_refs):
        x_ref, dres_ref, mod_ref, g_ref = extra_refs
        out_refs[0][...] = _modulated_norm_backward(acc, x_ref[...], dres_ref[...], mod_ref, g_ref, out_refs[1], 1)

    stage, slots = exchange(to_chips=("w_in",) + block_weights, swap=("w_a", "w_b"))
    (grad_x, sums_mix), got = _matmul(dproj, w_in_all, name="d_proj_in", tm=ts, stage=stage, finish=norm_mix_backward,
                                      extra=[(x0, rows_d), (dx2, rows_d), (mod, vec8), (g_mix, vec1)],
                                      outs=[out_rows(d, F32), sums_out], zero_first=(1,), parts=halves_of_tile, **back_sum)
    collect(slots, got)
    sum_quarters("w_in", *block_weights)

    dmod = jnp.concatenate([sums_mix[1:2], sums_mix[0:1], sums_mlp[3:4], sums_mlp[1:2], sums_mlp[0:1], sums_head[1:2]], axis=1)
    row = jnp.concatenate([sums_mix[2:3], sums_mlp[2:3], sums_seq[4:5], sums_seq[5:6], sums_seq[6:7], sums_seq[7:8], sums_seq[8:9],
                           sums_seq[9:10], sums_head[0:1], sums_seq[0:CONV_WIDTH].reshape(1, CONV_WIDTH * d), dmod, sums_head[2:3]], axis=1)
    n_vec, n_conv, n_mod = 9 * d, CONV_WIDTH * d, N_MOD * d
    stage, slots = exchange(swap=("w_in",) + block_weights)
    i_rows = _add_allgather8(stage, row)
    got = _run_stage(stage, "exchange_sums_w_in")
    collect(slots, got)
    rows_all = got[i_rows]
    total = _sum_rows(rows_all, d)
    loss = total[0, n_vec + n_conv + n_mod]
    dmod_all = rows_all[:, 0, n_vec + n_conv:n_vec + n_conv + n_mod]
    g_ada, d_ada, m_ada, v_ada = _ada_backward(act_all.T, lax.dynamic_slice_in_dim(dmod_all, k_me * ada_cols, ada_cols, axis=1),
                                               w_ada[0], m_w_ada[0], v_w_ada[0])
    g_conv = lax.dynamic_slice_in_dim(total[:, n_vec:n_vec + n_conv].reshape(CONV_WIDTH, d), k_me * conv_cols, conv_cols, axis=1)
    vec_names = ("norm_mix_g", "norm_mlp_g", "conv_b", "b_rg_a", "b_rg_x", "a_param", "b_pool", "pool_scale", "final_g", "conv_w", "b_ada")
    vec_params = [(norm_mix_g, m_norm_mix_g, v_norm_mix_g), (norm_mlp_g, m_norm_mlp_g, v_norm_mlp_g), (conv_b, m_conv_b, v_conv_b),
                  (b_rg_a, m_b_rg_a, v_b_rg_a), (b_rg_x, m_b_rg_x, v_b_rg_x), (a_param, m_a_param, v_a_param), (b_pool, m_b_pool, v_b_pool),
                  (pool_scale, m_pool_scale, v_pool_scale), tuple(a.reshape(1, d) for a in (final_g, m_final_g, v_final_g)),
                  (conv_w[0], m_conv_w[0], v_conv_w[0]), (b_ada, m_b_ada, v_b_ada)]
    offsets = [i * d for i in range(9)] + [None, n_vec + n_conv]
    vec_res = _adamw_vectors(total, offsets, [None] * 9 + [g_conv, None], vec_params)
    vec_out = {n: [r.reshape(shape) for r in res] for n, res, shape in zip(
        vec_names, vec_res, [(1, d)] * 8 + [(d,), (1, CONV_WIDTH, conv_cols), (1, N_MOD * d)])}

    big_out = {}
    for n in names:
        shape2 = (-1, mine[n].shape[-1])
        g2 = quarter[n].reshape(shape2)
        res = _adamw(mine[n].reshape(shape2), g2, moments_m[n].reshape(shape2), moments_v[n].reshape(shape2), "adamw_" + n)
        big_out[n] = [r.reshape((1,) + mine[n].shape) for r in res]

    key = {"w_in": "w_in", "w_rg_a": "rg_a", "w_rg_x": "rg_x", "w_branch_a": "w_a", "w_pool": "pool", "w_branch_b": "w_b", "w_out": "w_out",
           "w_up": "w_up", "w_down": "w_down"}
    order = ("norm_mix_g", "norm_mlp_g", "w_ada", "b_ada", "w_in", "conv_w", "conv_b", "w_rg_a", "b_rg_a", "w_rg_x", "b_rg_x", "a_param",
             "w_branch_a", "w_pool", "b_pool", "pool_scale", "w_branch_b", "w_out", "w_up", "w_down", "final_g")
    ada_out = [g_ada[None], d_ada[None], m_ada[None], v_ada[None]]
    outs = [loss, grad_x[None]]
    for kind in range(4):
        for nme in order:
            outs.append(ada_out[kind] if nme == "w_ada" else big_out[key[nme]][kind] if nme in key else vec_out[nme][kind])
    return tuple(outs)
```

```python
import functools

import jax
import jax.numpy as jnp
from jax import lax
from jax.experimental import pallas as pl
from jax.experimental.pallas import tpu as pltpu

F32, BF16 = jnp.float32, jnp.bfloat16
D_MODEL = 1024
D_FF = 4 * D_MODEL
N_BLOCKS = 4
BLOCK = D_MODEL // N_BLOCKS
CONV_WIDTH = 4
POOL_WINDOWS = (2, 4, 8, 16)
CONV_HALO = 8
POOL_HALO = 16
N_MOD = 6
EPS = 1e-6
C_RG = 8.0
ADAM_LR, ADAM_B1, ADAM_B2, ADAM_EPS, ADAM_WD, ADAM_STEP = 0.001, 0.9, 0.999, 1e-08, 0.01, 10
N_DEV = 8
N_CHIPS = 4
VMEM_LIMIT_V7X = 56 * 2**20
MESH = pl.DeviceIdType.MESH
SDS = jax.ShapeDtypeStruct
HBM_SPEC = pl.BlockSpec(memory_space=pltpu.HBM)
VMEM_SPEC = pl.BlockSpec(memory_space=pltpu.VMEM)


def _params(*semantics):
    return pltpu.CompilerParams(dimension_semantics=semantics, vmem_limit_bytes=VMEM_LIMIT_V7X)


def _coords():
    return lax.axis_index("x"), lax.axis_index("y"), lax.axis_index("c")


def _flip(v, bit):
    return 1 - v if bit else v


def _chips():
    x, y, _ = _coords()
    return [(1 - x, y), (x, 1 - y), (1 - x, 1 - y)]


PEER_SETS = [frozenset(s) for s in (("sibling",), ("chips",), ("sibling", "chips"), ("sibling", "chips", "others"))]


class _Stage:
    def __init__(self):
        self.inputs, self.in_specs, self.outputs, self.out_specs, self.aliases = [], [], [], [], {}
        self.parts, self.n_copies, self.peers = [], 0, set()

    def add(self, inputs, in_spec, outputs, out_spec, n_copies, build, peers, alias=False):
        self.peers |= set(peers)
        i0, o0 = len(self.inputs), len(self.outputs)
        self.inputs += inputs
        self.in_specs += [in_spec] * len(inputs)
        self.outputs += outputs
        self.out_specs += [out_spec] * len(outputs)
        if alias:
            self.aliases.update({i0 + i: o0 + i for i in range(len(inputs))})
        self.parts.append((build, i0, len(inputs), o0, len(outputs)))
        self.n_copies += n_copies
        return list(range(o0, o0 + len(outputs)))

    def copies(self, in_refs, out_refs):
        out = []
        for build, i0, ni, o0, no in self.parts:
            out += build(in_refs[i0:i0 + ni], out_refs[o0:o0 + no])
        assert len(out) == self.n_copies
        return out

    def run(self, in_refs, out_refs, send_sems, recv_sems, start):
        x, y, c = _coords()
        for s, (src, dst, to, landing) in enumerate(self.copies(in_refs, out_refs)):
            if to is None:
                cp = pltpu.make_async_copy(src, dst, send_sems.at[s])
                cp.start() if start else cp.wait()
                continue
            cp = pltpu.make_async_remote_copy(src_ref=src, dst_ref=dst, send_sem=send_sems.at[s], recv_sem=recv_sems.at[s],
                                              device_id=to, device_id_type=MESH)
            if start:
                cp.start()
            else:
                pltpu.make_async_remote_copy(src_ref=landing, dst_ref=landing, send_sem=send_sems.at[s], recv_sem=recv_sems.at[s],
                                             device_id=(x, y, c), device_id_type=MESH).wait_recv()
                cp.wait_send()


def _hosted(stage, body, *, name, in_specs, out_specs, out_shape, grid=(), scratch_shapes=(), compiler_params=None):
    if stage is None:
        return pl.pallas_call(body, name=name, grid=grid, in_specs=in_specs, out_specs=out_specs, out_shape=out_shape,
                              scratch_shapes=list(scratch_shapes), compiler_params=compiler_params)
    single = not isinstance(out_shape, (list, tuple))
    h_out_shape = [out_shape] if single else list(out_shape)
    h_out_specs = [out_specs] if single else list(out_specs)
    n_in, n_out, n_scr = len(in_specs), len(h_out_shape), len(scratch_shapes)
    s_in, s_out = len(stage.inputs), len(stage.outputs)

    def wrapped(*refs):
        h_in, st_in = refs[:n_in], refs[n_in:n_in + s_in]
        h_o, st_o = refs[n_in + s_in:n_in + s_in + n_out], refs[n_in + s_in + n_out:n_in + s_in + n_out + s_out]
        h_scr = refs[n_in + s_in + n_out + s_out:n_in + s_in + n_out + s_out + n_scr]
        send_sems, recv_sems = refs[n_in + s_in + n_out + s_out + n_scr:]
        def begin():
            x, y, c = _coords()
            peers = [(x, y, 1 - c)] * ("sibling" in stage.peers)
            peers += [(px, py, c) for px, py in _chips()] * ("chips" in stage.peers)
            peers += [(px, py, 1 - c) for px, py in _chips()] * ("others" in stage.peers)
            barrier = pltpu.get_barrier_semaphore()
            for peer in peers:
                pl.semaphore_signal(barrier, inc=1, device_id=peer, device_id_type=MESH)
            pl.semaphore_wait(barrier, len(peers))
            stage.run(st_in, st_o, send_sems, recv_sems, True)

        if not grid:
            begin()
            if body is not None:
                body(*h_in, *h_o, *h_scr)
            stage.run(st_in, st_o, send_sems, recv_sems, False)
            return
        ids = [pl.program_id(a) for a in range(len(grid))]
        first = functools.reduce(jnp.logical_and, [i == 0 for i in ids])
        last = functools.reduce(jnp.logical_and, [i == g - 1 for i, g in zip(ids, grid)])
        pl.when(first)(begin)
        body(*h_in, *h_o, *h_scr)
        pl.when(last)(lambda: stage.run(st_in, st_o, send_sems, recv_sems, False))

    call = pl.pallas_call(
        wrapped, name=name, grid=grid, in_specs=list(in_specs) + stage.in_specs, out_specs=h_out_specs + stage.out_specs,
        out_shape=h_out_shape + stage.outputs, input_output_aliases={n_in + i: n_out + o for i, o in stage.aliases.items()},
        scratch_shapes=list(scratch_shapes) + [pltpu.SemaphoreType.DMA((stage.n_copies,)), pltpu.SemaphoreType.DMA((stage.n_copies,))],
        compiler_params=pltpu.CompilerParams(dimension_semantics=("arbitrary",) * len(grid), vmem_limit_bytes=VMEM_LIMIT_V7X,
                                             collective_id=PEER_SETS.index(frozenset(stage.peers))),
    )

    def run(*operands):
        outs = call(*operands, *stage.inputs)
        host = outs[:n_out]
        return (host[0] if single else host), outs[n_out:]

    return run


def _run_stage(stage, name):
    return _hosted(stage, None, name=name, in_specs=[], out_specs=[], out_shape=[])()[1]


def _add_allgather8(stage, v):
    def build(ins, outs):
        x, y, c = _coords()
        me = 4 * x + 2 * y + c
        copies = [(ins[0], outs[0].at[me], None, None)]
        for r in range(1, N_DEV):
            peer = (_flip(x, r & 4), _flip(y, r & 2), _flip(c, r & 1))
            copies.append((ins[0], outs[0].at[me], peer, outs[0].at[me ^ r]))
        return copies

    return stage.add([v], VMEM_SPEC, [SDS((N_DEV,) + v.shape, v.dtype)], VMEM_SPEC, N_DEV, build, ("sibling", "chips", "others"))[0]


def _add_gather_ici(stage, shard, rows=None, whole=None):
    first, count = rows or (0, shard.shape[1])

    def build(ins, outs):
        x, y, c = _coords()
        k_me = 2 * x + y
        part = pl.ds(first, count)
        copies = [] if whole is not None else [(ins[0], outs[0].at[k_me], (x, y, 1 - c), outs[0].at[k_me])]
        for px, py in _chips():
            copies.append((ins[0].at[c, part], outs[0].at[k_me, c, part], (px, py, c), outs[0].at[2 * px + py, c, part]))
        return copies

    if whole is None:
        return stage.add([shard], HBM_SPEC, [SDS((N_CHIPS,) + shard.shape, shard.dtype)], HBM_SPEC, N_CHIPS, build, ("sibling", "chips"))[0]

    def build_into(ins, outs):
        return build(ins[1:], outs)

    i0 = len(stage.inputs)
    out = stage.add([whole, shard], HBM_SPEC, [SDS(whole.shape, whole.dtype)], HBM_SPEC, N_CHIPS - 1, build_into, ("chips",))[0]
    stage.aliases[i0] = out
    return out


def _add_gather_d2d(stage, whole):
    def build(ins, outs):
        x, y, c = _coords()
        return [(outs[0].at[2 * px + py, c], outs[0].at[2 * px + py, c], (x, y, 1 - c), outs[0].at[2 * px + py, 1 - c]) for px, py in _chips()]

    return stage.add([whole], HBM_SPEC, [SDS(whole.shape, whole.dtype)], HBM_SPEC, N_CHIPS - 1, build, ("sibling",), alias=True)[0]


def _add_reduce_d2d(stage, grads):
    def build(ins, outs):
        x, y, c = _coords()
        return [(ins[0].at[k, 1 - c], outs[0].at[k], (x, y, 1 - c), outs[0].at[k]) for k in range(N_CHIPS)]

    return stage.add([grads], HBM_SPEC, [SDS((N_CHIPS,) + grads.shape[2:], grads.dtype)], HBM_SPEC, N_CHIPS, build, ("sibling",))[0]


def _add_reduce_ici(stage, partial):
    def build(ins, outs):
        x, y, c = _coords()
        return [(ins[0].at[2 * px + py], outs[0].at[j], (px, py, c), outs[0].at[j]) for j, (px, py) in enumerate(_chips())]

    return stage.add([partial], HBM_SPEC, [SDS((N_CHIPS - 1,) + partial.shape[1:], partial.dtype)], HBM_SPEC, N_CHIPS - 1, build, ("chips",))[0]


def _add_swap_halves(stage, quarter):
    def build(ins, outs):
        x, y, c = _coords()
        return [(outs[0].at[c], outs[0].at[c], (x, y, 1 - c), outs[0].at[1 - c])]

    return stage.add([quarter], HBM_SPEC, [SDS(quarter.shape, quarter.dtype)], HBM_SPEC, 1, build, ("sibling",), alias=True)[0]


def _sum_pair(grads, from_sibling, place, name):
    _, _, rows, cols = grads.shape
    tr = min(rows, 512)

    def body(place_ref, g_ref, s_ref, o_ref):
        o_ref[0] = (g_ref[0, 0].astype(F32) + s_ref[0].astype(F32)).astype(BF16)

    spec = pl.BlockSpec((1, tr, cols), lambda k, i, place_ref: (k, i, 0))
    return pl.pallas_call(
        body, name=name, out_shape=SDS((N_CHIPS, rows, cols), BF16),
        grid_spec=pltpu.PrefetchScalarGridSpec(
            num_scalar_prefetch=1, grid=(N_CHIPS, rows // tr),
            in_specs=[pl.BlockSpec((1, 1, tr, cols), lambda k, i, place_ref: (k, place_ref[0], i, 0)), spec], out_specs=spec),
        compiler_params=_params("parallel", "parallel"),
    )(place, grads, from_sibling)


def _sum_quarter(grads, from_sibling, from_chips, place, name):
    _, _, rows, cols = grads.shape
    tr = min(rows, 512)

    def body(place_ref, g_ref, s_ref, c_ref, o_ref):
        acc = g_ref[0, 0].astype(F32) + s_ref[0].astype(F32)
        for j in range(N_CHIPS - 1):
            acc = acc + c_ref[j].astype(F32)
        o_ref[0] = acc

    return pl.pallas_call(
        body, name=name, out_shape=SDS((2, rows, cols), F32),
        grid_spec=pltpu.PrefetchScalarGridSpec(
            num_scalar_prefetch=1, grid=(rows // tr,),
            in_specs=[pl.BlockSpec((1, 1, tr, cols), lambda i, place_ref: (place_ref[1], place_ref[0], i, 0)),
                      pl.BlockSpec((1, tr, cols), lambda i, place_ref: (place_ref[1], i, 0)),
                      pl.BlockSpec((N_CHIPS - 1, tr, cols), lambda i, place_ref: (0, i, 0))],
            out_specs=pl.BlockSpec((1, tr, cols), lambda i, place_ref: (place_ref[0], i, 0))),
        compiler_params=_params("parallel"),
    )(place, grads, from_sibling, from_chips)


def _row_spec(ts, cols, col_block=0):
    return pl.BlockSpec((ts, cols), lambda *g: (g[0], col_block))


def _vec_spec(rows, cols):
    return pl.BlockSpec((rows, cols), lambda *g: (0, 0))


def _matmul(a, w, *, transposed, reduce_blocks, name, tm, outs, finish, extra=(), prepare=None, zero_first=(), parts=1, stage=None):
    m_rows = a.shape[0]
    nb, r, c = w.shape
    kb = c if transposed else r
    dims = (((1,), (1,)), ((), ())) if transposed else (((1,), (0,)), ((), ()))
    assert a.shape[1] == (nb * kb if reduce_blocks else kb) and m_rows % tm == 0 and tm % parts == 0
    n_extra, sub = len(extra), tm // parts

    def body(a_ref, w_ref, *rest):
        extra_refs, out_refs = rest[:n_extra], rest[n_extra:]
        if zero_first:
            @pl.when(pl.program_id(0) == 0)
            def _():
                for i in zero_first:
                    out_refs[i][...] = jnp.zeros_like(out_refs[i])

        def views(j):
            rows = pl.ds(j * sub, sub)
            return rows, [ref.at[rows] if ref.shape[0] == tm else ref for ref in extra_refs], [ref.at[rows] if ref.shape[0] == tm else ref for ref in out_refs]

        def product(j):
            rows, ex, ou = views(j)
            if reduce_blocks:
                acc = lax.dot_general(a_ref[rows, 0:kb], w_ref[0], dims, preferred_element_type=F32)
                for k in range(1, nb):
                    acc += lax.dot_general(a_ref[rows, k * kb:(k + 1) * kb], w_ref[k], dims, preferred_element_type=F32)
                return [(None, acc)]
            lhs = a_ref[rows, :] if prepare is None else prepare(a_ref.at[rows], ex, ou)
            return [(k, lax.dot_general(lhs, w_ref[k], dims, preferred_element_type=F32)) for k in range(nb)]

        pending = product(0)
        for j in range(parts):
            ahead = product(j + 1) if j + 1 < parts else None
            _, ex, ou = views(j)
            for k, acc in pending:
                finish(k, acc, ex, ou)
            pending = ahead

    return _hosted(
        stage, body, name=name, grid=(m_rows // tm,), out_shape=[s for s, _ in outs], out_specs=[s for _, s in outs],
        in_specs=[_row_spec(tm, a.shape[1]), pl.BlockSpec((nb, r, c), lambda *g: (0, 0, 0), pipeline_mode=pl.Buffered(1))] + [s for _, s in extra],
        compiler_params=_params("arbitrary"),
    )(a, w, *[e for e, _ in extra])


def _store_blocks(n):
    def finish(k, acc, extra_refs, out_refs):
        if k is None:
            out_refs[0][...] = acc.astype(out_refs[0].dtype)
        else:
            out_refs[0][:, k * n:(k + 1) * n] = acc.astype(out_refs[0].dtype)
    return finish


def _weight_grad(a, b, *, nb, name, tk, stage=None):
    s_rows, k1 = a.shape
    tn = b.shape[1] // nb
    assert k1 % tk == 0 and b.shape[1] % nb == 0

    def body(a_ref, b_ref, o_ref):
        o_ref[0] = lax.dot_general(a_ref[...], b_ref[...], (((0,), (0,)), ((), ())), preferred_element_type=F32).astype(o_ref.dtype)

    return _hosted(
        stage, body, name=name, grid=(nb, k1 // tk), out_shape=SDS((nb, k1, tn), BF16),
        in_specs=[pl.BlockSpec((s_rows, tk), lambda n, i: (0, i)), pl.BlockSpec((s_rows, tn), lambda n, i: (0, n))],
        out_specs=pl.BlockSpec((1, tk, tn), lambda n, i: (n, i, 0)), compiler_params=_params("parallel", "parallel"),
    )(a, b)


def _rms(x):
    r = lax.rsqrt(jnp.mean(x * x, axis=-1, keepdims=True) + EPS)
    return x * r, r


def _accumulate(acc_ref, row, value):
    acc_ref[row:row + 1, :] += jnp.sum(value, axis=0, keepdims=True)


def _gelu_parts(y):
    k0, k1 = 0.7978845608028654, 0.044715
    th = jnp.tanh(k0 * (y + k1 * (y * y * y)))
    gelu = 0.5 * y * (1.0 + th)
    dgelu = 0.5 * (1.0 + th) + 0.5 * y * (1.0 - th * th) * (k0 * (1.0 + 3.0 * k1 * (y * y)))
    return gelu, dgelu


def _one_minus_square(a, log_a):
    return -jnp.tanh(log_a) * (1.0 + a * a)


def _modulated_norm(x, mod_ref, g_ref, sc_row, sh_row):
    xh, _ = _rms(x)
    return ((xh * g_ref[...]) * (1.0 + mod_ref[sc_row:sc_row + 1, :]) + mod_ref[sh_row:sh_row + 1, :]).astype(BF16)


def _modulated_norm_backward(dh, x, dres, mod_ref, g_ref, acc_ref, sc_row):
    xh, r = _rms(x)
    _accumulate(acc_ref, 0, dh * (xh * g_ref[...]))
    _accumulate(acc_ref, 1, dh)
    dn = dh * (1.0 + mod_ref[sc_row:sc_row + 1, :])
    _accumulate(acc_ref, 2, dn * xh)
    dxh = dn * g_ref[...]
    return dres + r * (dxh - xh * jnp.mean(dxh * xh, axis=-1, keepdims=True))


def _seq_forward(proj, conv_w, vecs, w_a, w_x, w_p, *, ts, stage=None):
    s_rows = proj.shape[0]
    d = D_MODEL

    def body(x_ref, xh_ref, y_ref, u_ref, uh_ref, cw_ref, vec_ref, wa_ref, wx_ref, wp_ref,
             xr_o, hr_o, ga_o, p_o, pooled_o, carry, a_scr, b_scr):
        t = pl.program_id(0)

        @pl.when(t == 0)
        def _():
            carry[...] = jnp.zeros_like(carry)

        rows = lax.broadcasted_iota(jnp.int32, (ts, BLOCK), 0)
        start = (rows == 0) & (t == 0)
        for g in range(N_BLOCKS):
            lanes = slice(g * BLOCK, (g + 1) * BLOCK)
            x = x_ref[:, lanes]
            xx = jnp.concatenate([jnp.where(t == 0, 0.0, xh_ref[:, lanes]), x], axis=0)
            xr = vec_ref[0:1, lanes] + x * cw_ref[CONV_WIDTH - 1:CONV_WIDTH, lanes]
            for j in range(CONV_WIDTH - 1):
                xr = xr + pltpu.roll(xx, CONV_WIDTH - 1 - j, 0)[CONV_HALO:] * cw_ref[j:j + 1, lanes]
            xr_o[:, lanes] = xr
            xrb = xr.astype(BF16)
            r = jax.nn.sigmoid(jnp.dot(xrb, wa_ref[g], preferred_element_type=F32) + vec_ref[1:2, lanes])
            gate_i = jax.nn.sigmoid(jnp.dot(xrb, wx_ref[g], preferred_element_type=F32) + vec_ref[2:3, lanes])
            log_a = (-C_RG * r) * jax.nn.softplus(vec_ref[3:4, lanes])
            a = jnp.exp(log_a)
            a_scr[:, lanes] = a
            b_scr[:, lanes] = xr * gate_i * jnp.where(start, 1.0, jnp.sqrt(_one_minus_square(a, log_a)))

        sub = lax.broadcasted_iota(jnp.int32, (8, d), 0)

        def chunk(i, h):
            a = a_scr[pl.ds(i * 8, 8), :]
            b = b_scr[pl.ds(i * 8, 8), :]
            for s in (1, 2, 4):
                keep = sub >= s
                b = jnp.where(keep, a * pltpu.roll(b, s, 0) + b, b)
                a = jnp.where(keep, a * pltpu.roll(a, s, 0), a)
            hh = a * h + b
            hr_o[pl.ds(i * 8, 8), :] = hh
            return hh[7:8, :]

        carry[...] = lax.fori_loop(0, ts // 8, chunk, carry[...])
        pos = (rows + t * ts + 1).astype(F32)
        for g, win in enumerate(POOL_WINDOWS):
            lanes = slice(g * BLOCK, (g + 1) * BLOCK)
            gelu, _ = _gelu_parts(y_ref[:, lanes])
            ga_o[:, lanes] = (gelu * hr_o[:, lanes]).astype(BF16)
            u = u_ref[:, lanes]
            sums, have = jnp.concatenate([jnp.where(t == 0, 0.0, uh_ref[:, lanes]), u], axis=0), 1
            while have < win:
                sums = sums + pltpu.roll(sums, have, 0)
                have *= 2
            pb = (sums[POOL_HALO:] * (1.0 / jnp.minimum(pos, float(win))) - u).astype(BF16)
            p_o[:, lanes] = pb
            mixed = jnp.dot(pb, wp_ref[g], preferred_element_type=F32) + vec_ref[4:5, lanes]
            pooled_o[:, lanes] = (mixed * vec_ref[5:6, lanes]).astype(BF16)

    halo_x = pl.BlockSpec((CONV_HALO, d), lambda t: (jnp.maximum(t * (ts // CONV_HALO) - 1, 0), 0))
    halo_u = pl.BlockSpec((POOL_HALO, d), lambda t: (jnp.maximum(t * (ts // POOL_HALO) - 1, 0), 2))
    wspec = pl.BlockSpec((N_BLOCKS, BLOCK, BLOCK), lambda t: (0, 0, 0))
    return _hosted(
        stage, body, name="seq_forward", grid=(s_rows // ts,),
        out_shape=[SDS((s_rows, d), F32), SDS((s_rows, d), F32), SDS((s_rows, d), BF16), SDS((s_rows, d), BF16), SDS((s_rows, d), BF16)],
        in_specs=[_row_spec(ts, d, 0), halo_x, _row_spec(ts, d, 1), _row_spec(ts, d, 2), halo_u, _vec_spec(CONV_WIDTH, d), _vec_spec(8, d),
                  wspec, wspec, wspec],
        out_specs=[_row_spec(ts, d)] * 5,
        scratch_shapes=[pltpu.VMEM((1, d), F32), pltpu.VMEM((ts, d), F32), pltpu.VMEM((ts, d), F32)],
        compiler_params=_params("arbitrary"),
    )(proj, proj, proj, proj, proj, conv_w, vecs, w_a, w_x, w_p)


def _branches(ga, pooled, proj, w_a, w_b, *, ts, stage=None):
    s_rows, d = ga.shape

    def body(a_ref, p_ref, ga_ref, gb_ref, wa_ref, wb_ref, merged_o, sa_o, sb_o, ta_o, tb_o):
        bra = jnp.dot(a_ref[...], wa_ref[0], preferred_element_type=F32)
        brb = jnp.dot(p_ref[...], wb_ref[0], preferred_element_type=F32)
        sa, sb = jax.nn.sigmoid(ga_ref[...]), jax.nn.sigmoid(gb_ref[...])
        merged_o[...] = (sa * bra + sb * brb).astype(BF16)
        sa_o[...] = sa.astype(BF16)
        sb_o[...] = sb.astype(BF16)
        ta_o[...] = (bra * (sa * (1.0 - sa))).astype(BF16)
        tb_o[...] = (brb * (sb * (1.0 - sb))).astype(BF16)

    rows = _row_spec(ts, d)
    wspec = pl.BlockSpec((1, d, d), lambda *g: (0, 0, 0), pipeline_mode=pl.Buffered(1))
    return _hosted(
        stage, body, name="branches", grid=(s_rows // ts,), out_shape=[SDS((s_rows, d), BF16)] * 5,
        in_specs=[rows, rows, _row_spec(ts, d, 3), _row_spec(ts, d, 4), wspec, wspec], out_specs=[rows] * 5,
        compiler_params=_params("arbitrary"),
    )(ga, pooled, proj, proj, w_a, w_b)


def _seq_backward_by_block(proj, xr, hr, p, dga, dpooled, dgab, conv_w, vecs, w_a, w_x, w_p, *, ts, stage=None):
    s_rows = proj.shape[0]
    d = D_MODEL
    n_t = s_rows // ts
    nn, nt, tn = (((1,), (0,)), ((), ())), (((1,), (1,)), ((), ())), (((0,), (0,)), ((), ()))

    def dot(lhs, rhs, dims):
        return lax.dot_general(lhs, rhs, dims, preferred_element_type=F32)

    def body(x_ref, y_ref, xr_ref, hr_ref, hh_ref, p_ref, dga_ref, dpl_ref, dgab_ref, cw_ref, vec_ref, wa_ref, wx_ref, wp_ref,
             dproj_o, acc_o, dwa_o, dwx_o, dwp_o, g_carry, dxr_carry, q_carry, a_scr, b_scr, g_scr, a_keep, r_keep, i_keep, m_keep):
        step = pl.program_id(0)
        t = n_t - 1 - step

        @pl.when(step == 0)
        def _():
            for ref in (acc_o, dwa_o, dwx_o, dwp_o, g_carry, dxr_carry, q_carry):
                ref[...] = jnp.zeros_like(ref)

        rows = lax.broadcasted_iota(jnp.int32, (ts, BLOCK), 0)
        start = (rows == 0) & (t == 0)

        def add_sum(row, lanes, value):
            acc_o[row:row + 1, lanes] += jnp.sum(value, axis=0, keepdims=True)

        for g in range(N_BLOCKS):
            lanes = slice(g * BLOCK, (g + 1) * BLOCK)
            xrb = xr_ref[:, lanes].astype(BF16)
            r = jax.nn.sigmoid(dot(xrb, wa_ref[g], nn) + vec_ref[1:2, lanes])
            gate_i = jax.nn.sigmoid(dot(xrb, wx_ref[g], nn) + vec_ref[2:3, lanes])
            log_a = (-C_RG * r) * jax.nn.softplus(vec_ref[3:4, lanes])
            a = jnp.exp(log_a)
            a_keep[:, lanes], r_keep[:, lanes], i_keep[:, lanes], m_keep[:, lanes] = a, r, gate_i, _one_minus_square(a, log_a)
            gelu, dgelu = _gelu_parts(y_ref[:, lanes])
            dga_t = dga_ref[:, lanes]
            dproj_o[:, d + g * BLOCK:d + (g + 1) * BLOCK] = (dga_t * hr_ref[:, lanes] * dgelu).astype(BF16)
            a_scr[:, lanes] = jnp.where(rows == ts - 1, 1.0, pltpu.roll(a, ts - 1, 0))
            b_scr[:, lanes] = dga_t * gelu

        sub = lax.broadcasted_iota(jnp.int32, (8, d), 0)

        def chunk(i, g_next):
            at = pl.multiple_of((ts // 8 - 1 - i) * 8, 8)
            aa = a_scr[pl.ds(at, 8), :]
            bb = b_scr[pl.ds(at, 8), :]
            for s in (1, 2, 4):
                keep = sub < 8 - s
                bb = jnp.where(keep, bb + aa * pltpu.roll(bb, 8 - s, 0), bb)
                aa = jnp.where(keep, aa * pltpu.roll(aa, 8 - s, 0), aa)
            gg = aa * g_next + bb
            g_scr[pl.ds(at, 8), :] = gg
            return gg[0:1, :]

        g_first = lax.fori_loop(0, ts // 8, chunk, g_carry[...])
        g_carry[...] = a_keep[0:1, :] * g_first

        pos = (rows + t * ts + 1).astype(F32)
        for g, win in enumerate(POOL_WINDOWS):
            lanes = slice(g * BLOCK, (g + 1) * BLOCK)
            gs, xr, a, r, gate_i, m_square = g_scr[:, lanes], xr_ref[:, lanes], a_keep[:, lanes], r_keep[:, lanes], i_keep[:, lanes], m_keep[:, lanes]
            sp = jax.nn.softplus(vec_ref[3:4, lanes])
            mult = jnp.where(start, 1.0, jnp.sqrt(m_square))
            h_before = jnp.where(t == 0, 0.0, hh_ref[CONV_HALO - 1:CONV_HALO, lanes])
            h_prev = jnp.where(rows == 0, h_before, pltpu.roll(hr_ref[:, lanes], 1, 0))
            gm = gs * mult
            dxr = gm * gate_i
            d_i = gm * xr
            d_mult = (gs * xr) * gate_i
            dlog_a = (gs * h_prev) * a - jnp.where(start, 0.0, d_mult * (a * a) * lax.rsqrt(m_square))
            dzr = (dlog_a * (-C_RG * sp)) * (r * (1.0 - r))
            dzi = d_i * (gate_i * (1.0 - gate_i))
            add_sum(5, lanes, dzr)
            add_sum(6, lanes, dzi)
            acc_o[7:8, lanes] += jnp.sum(dlog_a * r, axis=0, keepdims=True) * (-C_RG * jax.nn.sigmoid(vec_ref[3:4, lanes]))
            xrb, dzrb, dzib = xr.astype(BF16), dzr.astype(BF16), dzi.astype(BF16)
            dwa_o[g] += dot(xrb, dzrb, tn)
            dwx_o[g] += dot(xrb, dzib, tn)
            dxr = dxr + dot(dzrb, wa_ref[g], nt) + dot(dzib, wx_ref[g], nt)

            x = x_ref[:, lanes]
            ext = jnp.concatenate([dxr, dxr_carry[:, lanes]], axis=0)
            dx = dxr * cw_ref[CONV_WIDTH - 1:CONV_WIDTH, lanes]
            add_sum(CONV_WIDTH - 1, lanes, x * dxr)
            for j in range(CONV_WIDTH - 1):
                ahead = pltpu.roll(ext, ts + CONV_HALO - (CONV_WIDTH - 1 - j), 0)[:ts]
                dx = dx + ahead * cw_ref[j:j + 1, lanes]
                add_sum(j, lanes, x * ahead)
            add_sum(4, lanes, dxr)
            dxr_carry[:, lanes] = dxr[0:CONV_HALO, :]
            dproj_o[:, lanes] = dx.astype(BF16)

            pb = p_ref[:, lanes]
            mixed = dot(pb, wp_ref[g], nn) + vec_ref[4:5, lanes]
            dpl = dpl_ref[:, lanes]
            add_sum(9, lanes, dpl * mixed)
            dmixed = dpl * vec_ref[5:6, lanes]
            add_sum(8, lanes, dmixed)
            dmb = dmixed.astype(BF16)
            dwp_o[g] += dot(pb, dmb, tn)
            dp = dot(dmb, wp_ref[g], nt)
            q = dp * (1.0 / jnp.minimum(pos, float(win)))
            sums, have = jnp.concatenate([q, q_carry[:, lanes]], axis=0), 1
            while have < win:
                sums = sums + pltpu.roll(sums, ts + POOL_HALO - have, 0)
                have *= 2
            q_carry[:, lanes] = q[0:POOL_HALO, :]
            dproj_o[:, 2 * d + g * BLOCK:2 * d + (g + 1) * BLOCK] = (sums[:ts] - dp).astype(BF16)

        dproj_o[:, 3 * d:5 * d] = dgab_ref[...]

    def rev(cols, col_block=0):
        return pl.BlockSpec((ts, cols), lambda i: (n_t - 1 - i, col_block))

    halo_h = pl.BlockSpec((CONV_HALO, d), lambda i: (jnp.maximum((n_t - 1 - i) * (ts // CONV_HALO) - 1, 0), 0))
    wspec = pl.BlockSpec((N_BLOCKS, BLOCK, BLOCK), lambda i: (0, 0, 0))
    return _hosted(
        stage, body, name="seq_backward", grid=(n_t,),
        out_shape=[SDS((s_rows, 5 * d), BF16), SDS((16, d), F32)] + [SDS((N_BLOCKS, BLOCK, BLOCK), F32)] * 3,
        in_specs=[rev(d, 0), rev(d, 1), rev(d), rev(d), halo_h, rev(d), rev(d), rev(d), rev(2 * d), _vec_spec(CONV_WIDTH, d), _vec_spec(8, d),
                  wspec, wspec, wspec],
        out_specs=[rev(5 * d), _vec_spec(16, d), wspec, wspec, wspec],
        scratch_shapes=[pltpu.VMEM((1, d), F32), pltpu.VMEM((CONV_HALO, d), F32), pltpu.VMEM((POOL_HALO, d), F32)] + [pltpu.VMEM((ts, d), F32)] * 7,
        compiler_params=_params("arbitrary"),
    )(proj, proj, xr, hr, hr, p, dga, dpooled, dgab, conv_w, vecs, w_a, w_x, w_p)


def _adamw_math(w, g, m, v):
    m = ADAM_B1 * m + (1.0 - ADAM_B1) * g
    v = ADAM_B2 * v + (1.0 - ADAM_B2) * jnp.square(g)
    m_hat = m / (1.0 - ADAM_B1 ** ADAM_STEP)
    v_hat = v / (1.0 - ADAM_B2 ** ADAM_STEP)
    delta = -ADAM_LR * (m_hat / (jnp.sqrt(v_hat) + ADAM_EPS) + ADAM_WD * w)
    return delta, m, v


def _adamw(w, g, m, v, name):
    rows, cols = w.shape
    tr = min(rows, 512)

    def body(w_ref, g_ref, m_ref, v_ref, g_out, d_ref, mo_ref, vo_ref):
        g = g_ref[...]
        g_out[...] = g
        d_ref[...], mo_ref[...], vo_ref[...] = _adamw_math(w_ref[...], g, m_ref[...], v_ref[...])

    spec = pl.BlockSpec((tr, cols), lambda i: (i, 0))
    return pl.pallas_call(
        body, name=name, grid=(rows // tr,), out_shape=[SDS((rows, cols), F32)] * 4, in_specs=[spec] * 4, out_specs=[spec] * 4,
        compiler_params=_params("parallel"),
    )(w, g, m, v)


def _prologue(c, conv_shard, w_ada, b_cols, w_in_halves, gate_halves):
    n_cols = w_ada.shape[1]
    n_g = len(gate_halves)
    half_rows = w_in_halves.shape[1] // 2
    n_sems = 3 * N_DEV + N_CHIPS * n_g + 5 + 2 + 3

    def body(c_ref, conv_ref, w_ref, b_ref, win_ref, *rest):
        gate_refs = rest[:n_g]
        act_o, mod_o, conv_o, win_o = rest[n_g:n_g + 4]
        gate_o = rest[n_g + 4:2 * n_g + 4]
        c_all, mod_cols, send_sems, recv_sems = rest[2 * n_g + 4:]
        x, y, c = _coords()
        me, k_me = 4 * x + 2 * y + c, 2 * x + y
        sibling, x_nbr, y_nbr = (x, y, 1 - c), (1 - x, y, c), (x, 1 - y, c)
        k_x, k_y, k_d = 2 * (1 - x) + y, 2 * x + (1 - y), 2 * (1 - x) + (1 - y)
        counter = iter(range(n_sems))

        def copy(src, dst, to, landing):
            s = next(counter)
            out = pltpu.make_async_remote_copy(src_ref=src, dst_ref=dst, send_sem=send_sems.at[s], recv_sem=recv_sems.at[s],
                                               device_id=to, device_id_type=MESH)
            arrival = pltpu.make_async_remote_copy(src_ref=landing, dst_ref=landing, send_sem=send_sems.at[s], recv_sem=recv_sems.at[s],
                                                   device_id=(x, y, c), device_id_type=MESH)
            return out, arrival

        def allgather(src, out):
            s = next(counter)
            own = pltpu.make_async_copy(src, out.at[me], send_sems.at[s])
            peers = [copy(src, out.at[me], (_flip(x, r & 4), _flip(y, r & 2), _flip(c, r & 1)), out.at[me ^ r]) for r in range(1, N_DEV)]
            return own, peers

        def start(group):
            own, peers = group
            own.start()
            for out, _ in peers:
                out.start()

        def finish(group):
            own, peers = group
            for out, arrival in peers:
                arrival.wait_recv()
                out.wait_send()
            own.wait()

        rows_a, rows_b = pl.ds(0, half_rows), pl.ds(half_rows, half_rows)
        c_group, conv_group = allgather(c_ref, c_all), allgather(conv_ref, conv_o)
        gates = []
        for g_ref, g_out in zip(gate_refs, gate_o):
            gates.append(copy(g_ref, g_out.at[k_me], sibling, g_out.at[k_me]))
            gates += [copy(g_ref.at[c], g_out.at[k_me, c], (px, py, c), g_out.at[2 * px + py, c]) for px, py in _chips()]
        own_in = copy(win_ref, win_o.at[k_me], sibling, win_o.at[k_me])
        to_x = [copy(win_ref.at[c, rows], win_o.at[k_me, c, rows], x_nbr, win_o.at[k_x, c, rows]) for rows in (rows_a, rows_b)]
        to_y = [copy(win_ref.at[c, rows], win_o.at[k_me, c, rows], y_nbr, win_o.at[k_y, c, rows]) for rows in (rows_b, rows_a)]
        mod_group = allgather(mod_cols, mod_o)
        relay_y = copy(win_o.at[k_x, c, rows_a], win_o.at[k_x, c, rows_a], y_nbr, win_o.at[k_d, c, rows_a])
        relay_x = copy(win_o.at[k_y, c, rows_b], win_o.at[k_y, c, rows_b], x_nbr, win_o.at[k_d, c, rows_b])
        sib_x, sib_y, sib_d = (copy(win_o.at[k, c], win_o.at[k, c], sibling, win_o.at[k, 1 - c]) for k in (k_x, k_y, k_d))

        start(c_group)
        start(conv_group)
        for out, _ in gates + [own_in] + to_x + to_y:
            out.start()
        finish(c_group)
        cv = jnp.concatenate([c_all[r] for r in range(N_DEV)], axis=0)
        act = cv * jax.nn.sigmoid(cv)
        act_o[...] = act
        mod_cols[...] = jnp.dot(act.astype(BF16), w_ref[...].astype(BF16), preferred_element_type=F32) + b_ref[...]
        start(mod_group)
        to_x[0][1].wait_recv()
        relay_y[0].start()
        to_y[0][1].wait_recv()
        relay_x[0].start()
        finish(mod_group)
        finish(conv_group)
        to_x[1][1].wait_recv()
        sib_x[0].start()
        to_y[1][1].wait_recv()
        sib_y[0].start()
        for out, arrival in gates + [own_in]:
            arrival.wait_recv()
            out.wait_send()
        for out, arrival in (relay_y, relay_x):
            arrival.wait_recv()
            out.wait_send()
        sib_d[0].start()
        for out, _ in to_x + to_y:
            out.wait_send()
        for out, arrival in (sib_x, sib_y, sib_d):
            arrival.wait_recv()
            out.wait_send()

    gate_shapes = [SDS((N_CHIPS,) + g.shape, g.dtype) for g in gate_halves]
    return pl.pallas_call(
        body, name="prologue",
        out_shape=[SDS((N_DEV, c.shape[1]), F32), SDS((N_DEV, N_DEV, n_cols), F32), SDS((N_DEV,) + conv_shard.shape, F32),
                   SDS((N_CHIPS,) + w_in_halves.shape, w_in_halves.dtype)] + gate_shapes,
        in_specs=[VMEM_SPEC] * 4 + [HBM_SPEC] * (1 + n_g), out_specs=[VMEM_SPEC] * 3 + [HBM_SPEC] * (1 + n_g),
        scratch_shapes=[pltpu.VMEM((N_DEV,) + c.shape, F32), pltpu.VMEM((N_DEV, n_cols), F32),
                        pltpu.SemaphoreType.DMA((n_sems,)), pltpu.SemaphoreType.DMA((n_sems,))],
        compiler_params=pltpu.CompilerParams(vmem_limit_bytes=VMEM_LIMIT_V7X),
    )(c, conv_shard, w_ada, b_cols, w_in_halves, *gate_halves)


def _ada_backward(act_t, dmod_cols, w, m, v):
    rows, cols = w.shape
    tn = 512

    def body(a_ref, dm_ref, w_ref, m_ref, v_ref, g_ref, d_ref, mo_ref, vo_ref):
        g = jnp.dot(a_ref[...].astype(BF16), dm_ref[...].astype(BF16), preferred_element_type=F32)
        g_ref[...] = g
        d_ref[...], mo_ref[...], vo_ref[...] = _adamw_math(w_ref[...], g, m_ref[...], v_ref[...])

    spec = pl.BlockSpec((rows, tn), lambda j: (0, j))
    return pl.pallas_call(
        body, name="ada_backward", grid=(cols // tn,), out_shape=[SDS((rows, cols), F32)] * 4,
        in_specs=[pl.BlockSpec((rows, N_DEV), lambda j: (0, 0)), pl.BlockSpec((N_DEV, tn), lambda j: (0, j)), spec, spec, spec],
        out_specs=[spec] * 4, compiler_params=_params("parallel"),
    )(act_t, dmod_cols, w, m, v)


def _sum_rows(gathered, n_loss):
    n = gathered.shape[2]

    def body(g_ref, o_ref):
        acc = g_ref[0]
        for r in range(1, N_DEV):
            acc = acc + g_ref[r]
        o_ref[...] = acc
        o_ref[:, n - n_loss:n] = jnp.broadcast_to(jnp.sum(acc[:, n - n_loss:n], axis=1, keepdims=True), (1, n_loss))

    return pl.pallas_call(body, name="sum_rows", out_shape=SDS((1, n), F32), in_specs=[VMEM_SPEC], out_specs=VMEM_SPEC)(gathered)


def _adamw_vectors(total, offsets, separate, params):
    n_p = len(params)
    apart = [g for g in separate if g is not None]

    def body(*refs):
        total_ref, apart_refs = refs[0], list(refs[1:1 + len(apart)])
        ins, outs = refs[1 + len(apart):1 + len(apart) + 3 * n_p], refs[1 + len(apart) + 3 * n_p:]
        for i in range(n_p):
            w_ref, m_ref, v_ref = ins[3 * i:3 * i + 3]
            g = apart_refs.pop(0)[...] if offsets[i] is None else total_ref[:, offsets[i]:offsets[i] + w_ref.shape[1]]
            outs[4 * i][...] = g
            outs[4 * i + 1][...], outs[4 * i + 2][...], outs[4 * i + 3][...] = _adamw_math(w_ref[...], g, m_ref[...], v_ref[...])

    flat = [a for p in params for a in p]
    res = pl.pallas_call(
        body, name="adamw_vectors", out_shape=[SDS(p[0].shape, F32) for p in params for _ in range(4)],
        in_specs=[VMEM_SPEC] * (1 + len(apart) + len(flat)), out_specs=[VMEM_SPEC] * (4 * n_p),
    )(total, *apart, *flat)
    return [tuple(res[4 * i:4 * i + 4]) for i in range(n_p)]


def _blocks_to_pieces(w):
    nb, rows, n = w.shape
    q = rows // N_CHIPS
    return w.reshape(nb, N_CHIPS, q, n).transpose(1, 0, 2, 3).reshape(N_CHIPS, 2, nb // 2, q, n)


def _pieces_to_blocks(w):
    n_chips, _, half, q, n = w.shape
    return w.reshape(n_chips, 2 * half, q, n).transpose(1, 0, 2, 3).reshape(2 * half, n_chips * q, n)


def kernel(x, c, norm_mix_g, norm_mlp_g, w_ada, b_ada, w_in, conv_w, conv_b, w_rg_a, b_rg_a, w_rg_x, b_rg_x, a_param, w_branch_a, w_pool, b_pool, pool_scale, w_branch_b, w_out, w_up, w_down, final_g, loss_target, m_norm_mix_g, m_norm_mlp_g, m_w_ada, m_b_ada, m_w_in, m_conv_w, m_conv_b, m_w_rg_a, m_b_rg_a, m_w_rg_x, m_b_rg_x, m_a_param, m_w_branch_a, m_w_pool, m_b_pool, m_pool_scale, m_w_branch_b, m_w_out, m_w_up, m_w_down, m_final_g, v_norm_mix_g, v_norm_mlp_g, v_w_ada, v_b_ada, v_w_in, v_conv_w, v_conv_b, v_w_rg_a, v_b_rg_a, v_w_rg_x, v_b_rg_x, v_a_param, v_w_branch_a, v_w_pool, v_b_pool, v_pool_scale, v_w_branch_b, v_w_out, v_w_up, v_w_down, v_final_g):
    d = D_MODEL
    s_rows = x.shape[1]
    ts, tsq = min(512, s_rows), min(256, s_rows)
    xi, yi, ci = _coords()
    k_me, me = 2 * xi + yi, 4 * xi + 2 * yi + ci
    ada_cols = w_ada.shape[2]
    conv_cols = conv_w.shape[2]
    n_in, n_up = w_in.shape[2], w_up.shape[2]

    names = ("w_in", "w_up", "w_down", "w_a", "w_b", "w_out", "rg_a", "rg_x", "pool")
    mine = dict(zip(names, (w_in[0], w_up[0], w_down[0], w_branch_a[0], w_branch_b[0], w_out[0], w_rg_a[0], w_rg_x[0], w_pool[0])))
    moments_m = dict(zip(names, (m_w_in[0], m_w_up[0], m_w_down[0], m_w_branch_a[0], m_w_branch_b[0], m_w_out[0], m_w_rg_a[0], m_w_rg_x[0], m_w_pool[0])))
    moments_v = dict(zip(names, (v_w_in[0], v_w_up[0], v_w_down[0], v_w_branch_a[0], v_w_branch_b[0], v_w_out[0], v_w_rg_a[0], v_w_rg_x[0], v_w_pool[0])))
    block_weights, squares = ("rg_a", "rg_x", "pool"), ("w_a", "w_b", "w_out")
    place = jnp.stack([ci, k_me]).astype(jnp.int32)
    g_mix, g_mlp, g_fin = norm_mix_g, norm_mlp_g, final_g.reshape(1, d)
    x0, target = x[0], loss_target[0]
    wg = functools.partial(_weight_grad, tk=min(1024, d))
    plain, back, back_sum = (dict(transposed=t, reduce_blocks=r) for t, r in ((False, False), (True, False), (True, True)))
    rows_d, vec8, vec1 = _row_spec(ts, d), _vec_spec(8, d), _vec_spec(1, d)
    tw = min(1024, s_rows)
    halves_of_tile = 2

    def out_rows(cols, dtype, t=ts):
        return SDS((s_rows, cols), dtype), _row_spec(t, cols)

    sums_out = (SDS((8, d), F32), vec8)

    def halves(n):
        return mine[n].astype(BF16).reshape(2, -1, mine[n].shape[-1])

    def blocks(w):
        return _pieces_to_blocks(w.reshape(N_CHIPS, 2, N_BLOCKS // 2, BLOCK // N_CHIPS, BLOCK))

    act_all, mod_all, conv_all, w_in_whole, *gate_wholes = _prologue(
        c, conv_w[0], w_ada[0], lax.dynamic_slice_in_dim(b_ada, k_me * ada_cols, ada_cols, axis=1), halves("w_in"), [halves(n) for n in block_weights])
    conv_full = conv_all[0::2].transpose(1, 0, 2).reshape(CONV_WIDTH, d)
    mod_mine = lax.dynamic_index_in_dim(mod_all, me, axis=1, keepdims=False)[0::2]
    mod = jnp.concatenate([mod_mine.reshape(N_MOD, d), jnp.zeros((8 - N_MOD, d), F32)], axis=0)
    w_in_all = w_in_whole.reshape(N_CHIPS, d, n_in)
    vecs = jnp.concatenate([conv_b, b_rg_a, b_rg_x, a_param, b_pool, pool_scale, jnp.zeros((2, d), F32)], axis=0)

    def norm_first(a_ref, extra_refs, out_refs):
        h = _modulated_norm(a_ref[...], extra_refs[0], extra_refs[1], 1, 0)
        out_refs[1][...] = h
        return h

    stage = _Stage()
    i_blk = [_add_gather_d2d(stage, g) for g in gate_wholes]
    i_sq = [_add_gather_ici(stage, halves(n)) for n in squares]
    down_rows = mine["w_down"].shape[0] // 8
    i_down = _add_gather_ici(stage, halves("w_down"), rows=(0, down_rows))
    (proj, h1), got = _matmul(x0, w_in_all, name="proj_in", tm=ts, extra=[(mod, vec8), (g_mix, vec1)], prepare=norm_first,
                              outs=[out_rows(N_CHIPS * n_in, F32), out_rows(d, BF16)], finish=_store_blocks(n_in), parts=halves_of_tile,                              stage=stage, **plain)
    w_rg_a_all, w_rg_x_all, w_pool_all = (blocks(got[i]) for i in i_blk)
    stage = _Stage()
    i_sq = [_add_gather_d2d(stage, got[i]) for i in i_sq]
    i_up = _add_gather_ici(stage, halves("w_up"))
    i_down = _add_gather_ici(stage, halves("w_down"), rows=(down_rows, down_rows), whole=got[i_down])
    (xr, hr, ga, p, pooled), got = _seq_forward(proj, conv_full, vecs, w_rg_a_all, w_rg_x_all, w_pool_all, ts=tsq, stage=stage)
    w_a_all, w_b_all, w_out_all = (got[i].reshape(1, d, d) for i in i_sq)
    stage = _Stage()
    i_up = _add_gather_d2d(stage, got[i_up])
    i_down = _add_gather_ici(stage, halves("w_down"), rows=(2 * down_rows, 2 * down_rows), whole=got[i_down])
    (merged, s_a, s_b, t_a, t_b), got = _branches(ga, pooled, proj, w_a_all, w_b_all, ts=ts, stage=stage)
    w_up_all = got[i_up].reshape(N_CHIPS, d, n_up)
    stage = _Stage()
    i_down = _add_gather_d2d(stage, got[i_down])

    def residual_norm(k, acc, extra_refs, out_refs):
        x_ref, mod_ref, g_ref = extra_refs
        out_refs[0][...] = acc.astype(BF16)
        x2_t = x_ref[...] + mod_ref[2:3, :] * acc
        out_refs[1][...] = x2_t
        out_refs[2][...] = _modulated_norm(x2_t, mod_ref, g_ref, 4, 3)

    (mo, x2, h2), got = _matmul(merged, w_out_all, name="mix_out", tm=ts, extra=[(x0, rows_d), (mod, vec8), (g_mlp, vec1)],
                                outs=[out_rows(d, BF16), out_rows(d, F32), out_rows(d, BF16)], finish=residual_norm, parts=halves_of_tile,                                stage=stage, transposed=False, reduce_blocks=True)
    w_down_all = got[i_down].reshape(N_CHIPS, D_FF // N_CHIPS, d)

    def relu_squared(k, acc, extra_refs, out_refs):
        out_refs[0][:, k * n_up:(k + 1) * n_up] = jnp.square(jnp.maximum(acc, 0.0)).astype(BF16)

    ff, = _matmul(h2, w_up_all, name="mlp_up", tm=ts, outs=[out_rows(D_FF, BF16)], finish=relu_squared, parts=halves_of_tile, **plain)

    def loss_head(k, acc, extra_refs, out_refs):
        x2_ref, mod_ref, g_ref, t_ref = extra_refs
        dx_ref, df_ref, acc_ref = out_refs
        xh, r = _rms(x2_ref[...] + mod_ref[5:6, :] * acc)
        err = xh * g_ref[...] - t_ref[...]
        dy = err / d
        dxh = dy * g_ref[...]
        dx3_t = r * (dxh - xh * jnp.mean(dxh * xh, axis=-1, keepdims=True))
        dx_ref[...] = dx3_t
        df_ref[...] = (dx3_t * mod_ref[5:6, :]).astype(BF16)
        _accumulate(acc_ref, 0, dy * xh)
        _accumulate(acc_ref, 1, dx3_t * acc)
        _accumulate(acc_ref, 2, jnp.square(err) * (0.5 / d))

    dx3, dffo, sums_head = _matmul(ff, w_down_all, name="mlp_down", tm=ts, extra=[(x2, rows_d), (mod, vec8), (g_fin, vec1), (target, rows_d)],
                                   outs=[out_rows(d, F32), out_rows(d, BF16), sums_out], finish=loss_head, zero_first=(2,), parts=halves_of_tile,                                   transposed=False, reduce_blocks=True)

    partial_of, from_sibling, chip_sum, from_chips, half_done, quarter = {}, {}, {}, {}, {}, {}

    def publish(n, g):
        g = _blocks_to_pieces(g).astype(BF16) if n in block_weights else g
        partial_of[n] = g.reshape(N_CHIPS, 2, -1, mine[n].shape[-1])

    def exchange(to_sibling=(), to_chips=(), swap=()):
        st = _Stage()
        slots = [(n, from_sibling, _add_reduce_d2d(st, partial_of[n])) for n in to_sibling]
        slots += [(n, from_chips, _add_reduce_ici(st, chip_sum[n])) for n in to_chips]
        slots += [(n, quarter, _add_swap_halves(st, half_done[n])) for n in swap]
        return st, slots

    def collect(slots, outs):
        for n, where, i in slots:
            where[n] = outs[i]

    def sum_pairs(*ns):
        for n in ns:
            chip_sum[n] = _sum_pair(partial_of[n], from_sibling[n], place, "sum_pair_" + n)

    def sum_quarters(*ns):
        for n in ns:
            half_done[n] = _sum_quarter(partial_of[n], from_sibling[n], from_chips[n], place, "sum_quarter_" + n)

    publish("w_down", wg(ff, dffo, nb=1, name="grad_w_down"))

    def relu_squared_backward(k, acc, extra_refs, out_refs):
        cols = slice(k * n_up, (k + 1) * n_up)
        out_refs[0][:, cols] = (acc * (2.0 * jnp.sqrt(extra_refs[0][:, cols].astype(F32)))).astype(BF16)

    stage, slots = exchange(to_sibling=["w_down"])
    (dup,), got = _matmul(dffo, w_down_all, name="d_mlp_down", tm=ts, extra=[(ff, _row_spec(ts, D_FF))], outs=[out_rows(D_FF, BF16)],
                          finish=relu_squared_backward, parts=halves_of_tile, stage=stage, **back)
    collect(slots, got)
    sum_pairs("w_down")

    def norm_mlp_backward(k, acc, extra_refs, out_refs):
        x2_ref, dres_ref, mod_ref, g_ref, mo_ref = extra_refs
        dx_ref, dmo_ref, acc_ref = out_refs
        dx =_modulated_norm_backward(acc, x2_ref[...], dres_ref[...], mod_ref, g_ref, acc_ref, 4)
        dx_ref[...] = dx
        dmo_ref[...] = (dx * mod_ref[2:3, :]).astype(BF16)
        _accumulate(acc_ref, 3, dx * mo_ref[...].astype(F32))

    stage, slots = exchange(to_chips=["w_down"])
    (dx2, dmo, sums_mlp), got = _matmul(dup, w_up_all, name="d_mlp_up", tm=ts, stage=stage, finish=norm_mlp_backward,
                                        extra=[(x2, rows_d), (dx3, rows_d), (mod, vec8), (g_mlp, vec1), (mo, rows_d)],
                                        outs=[out_rows(d, F32), out_rows(d, BF16), sums_out], zero_first=(2,), parts=halves_of_tile, **back_sum)
    collect(slots, got)
    sum_quarters("w_down")
    publish("w_up", wg(h2, dup, nb=N_CHIPS, name="grad_w_up"))
    publish("w_out", wg(merged, dmo, nb=1, name="grad_w_out"))

    def merge_backward(k, acc, extra_refs, out_refs):
        sa_ref, sb_ref, ta_ref, tb_ref = extra_refs
        out_refs[0][...] = (acc * sa_ref[...].astype(F32)).astype(BF16)
        out_refs[1][...] = (acc * sb_ref[...].astype(F32)).astype(BF16)
        out_refs[2][:, 0:d] = (acc * ta_ref[...].astype(F32)).astype(BF16)
        out_refs[2][:, d:2 * d] = (acc * tb_ref[...].astype(F32)).astype(BF16)

    stage, slots = exchange(to_sibling=["w_up", "w_out"], swap=["w_down"])
    (dbr_a, dbr_b, dgab), got = _matmul(dmo, w_out_all, name="d_mix_out", tm=ts, stage=stage, finish=merge_backward,
                                        extra=[(s_a, rows_d), (s_b, rows_d), (t_a, rows_d), (t_b, rows_d)],
                                        outs=[out_rows(d, BF16), out_rows(d, BF16), out_rows(2 * d, BF16)], parts=halves_of_tile, **back_sum)
    collect(slots, got)
    publish("w_a", wg(ga, dbr_a, nb=1, name="grad_w_branch_a"))
    publish("w_b", wg(pooled, dbr_b, nb=1, name="grad_w_branch_b"))
    dga, = _matmul(dbr_a, w_a_all, name="d_branch_a", tm=tw, outs=[out_rows(d, F32, tw)], finish=_store_blocks(d), **back_sum)
    dpooled, = _matmul(dbr_b, w_b_all, name="d_branch_b", tm=tw, outs=[out_rows(d, F32, tw)], finish=_store_blocks(d), **back_sum)
    sum_pairs("w_up", "w_out")
    stage, slots = exchange(to_sibling=["w_a", "w_b"], to_chips=["w_up", "w_out"])
    (dproj, sums_seq, d_rg_a, d_rg_x, d_pool), got = _seq_backward_by_block(proj, xr, hr, p, dga, dpooled, dgab, conv_full, vecs,
                                                                            w_rg_a_all, w_rg_x_all, w_pool_all, ts=tsq, stage=stage)
    collect(slots, got)
    sum_quarters("w_up", "w_out")
    sum_pairs("w_a", "w_b")
    for n, g in zip(block_weights, (d_rg_a, d_rg_x, d_pool)):
        publish(n, g)
    stage, slots = exchange(to_sibling=block_weights, to_chips=["w_a", "w_b"], swap=["w_up", "w_out"])
    g_in, got = wg(h1, dproj, nb=N_CHIPS, name="grad_w_in", stage=stage)
    collect(slots, got)
    publish("w_in", g_in)
    sum_pairs(*block_weights)
    sum_quarters("w_a", "w_b")
    stage, slots = exchange(to_sibling=["w_in"], to_chips=block_weights)
    collect(slots, _run_stage(stage, "exchange_w_in_gates"))
    sum_pairs("w_in")
    sum_quarters(*block_weights)

    def norm_mix_backward(k, acc, extra_refs, out_refs):
        x_ref, dres_ref, mod_ref, g_ref = extra_refs
        out_refs[0][...] = _modulated_norm_backward(acc, x_ref[...], dres_ref[...], mod_ref, g_ref, out_refs[1], 1)

    stage, slots = exchange(to_chips=["w_in"], swap=block_weights + ("w_a", "w_b"))
    (grad_x, sums_mix), got = _matmul(dproj, w_in_all, name="d_proj_in", tm=ts, stage=stage, finish=norm_mix_backward,
                                      extra=[(x0, rows_d), (dx2, rows_d), (mod, vec8), (g_mix, vec1)],
                                      outs=[out_rows(d, F32), sums_out], zero_first=(1,), parts=halves_of_tile, **back_sum)
    collect(slots, got)
    sum_quarters("w_in")

    dmod = jnp.concatenate([sums_mix[1:2], sums_mix[0:1], sums_mlp[3:4], sums_mlp[1:2], sums_mlp[0:1], sums_head[1:2]], axis=1)
    row = jnp.concatenate([sums_mix[2:3], sums_mlp[2:3], sums_seq[4:5], sums_seq[5:6], sums_seq[6:7], sums_seq[7:8], sums_seq[8:9],
                           sums_seq[9:10], sums_head[0:1], sums_seq[0:CONV_WIDTH].reshape(1, CONV_WIDTH * d), dmod, sums_head[2:3]], axis=1)
    n_vec, n_conv, n_mod = 9 * d, CONV_WIDTH * d, N_MOD * d
    stage, slots = exchange(swap=["w_in"])
    i_rows = _add_allgather8(stage, row)
    got = _run_stage(stage, "exchange_sums_w_in")
    collect(slots, got)
    rows_all = got[i_rows]
    total = _sum_rows(rows_all, d)
    loss = total[0, n_vec + n_conv + n_mod]
    dmod_all = rows_all[:, 0, n_vec + n_conv:n_vec + n_conv + n_mod]
    g_ada, d_ada, m_ada, v_ada = _ada_backward(act_all.T, lax.dynamic_slice_in_dim(dmod_all, k_me * ada_cols, ada_cols, axis=1),
                                               w_ada[0], m_w_ada[0], v_w_ada[0])
    g_conv = lax.dynamic_slice_in_dim(total[:, n_vec:n_vec + n_conv].reshape(CONV_WIDTH, d), k_me * conv_cols, conv_cols, axis=1)
    vec_names = ("norm_mix_g", "norm_mlp_g", "conv_b", "b_rg_a", "b_rg_x", "a_param", "b_pool", "pool_scale", "final_g", "conv_w", "b_ada")
    vec_params = [(norm_mix_g, m_norm_mix_g, v_norm_mix_g), (norm_mlp_g, m_norm_mlp_g, v_norm_mlp_g), (conv_b, m_conv_b, v_conv_b),
                  (b_rg_a, m_b_rg_a, v_b_rg_a), (b_rg_x, m_b_rg_x, v_b_rg_x), (a_param, m_a_param, v_a_param), (b_pool, m_b_pool, v_b_pool),
                  (pool_scale, m_pool_scale, v_pool_scale), tuple(a.reshape(1, d) for a in (final_g, m_final_g, v_final_g)),
                  (conv_w[0], m_conv_w[0], v_conv_w[0]), (b_ada, m_b_ada, v_b_ada)]
    offsets = [i * d for i in range(9)] + [None, n_vec + n_conv]
    vec_res = _adamw_vectors(total, offsets, [None] * 9 + [g_conv, None], vec_params)
    vec_out = {n: [r.reshape(shape) for r in res] for n, res, shape in zip(
        vec_names, vec_res, [(1, d)] * 8 + [(d,), (1, CONV_WIDTH, conv_cols), (1, N_MOD * d)])}

    big_out = {}
    for n in names:
        shape2 = (-1, mine[n].shape[-1])
        g2 = quarter[n].reshape(shape2)
        res = _adamw(mine[n].reshape(shape2), g2, moments_m[n].reshape(shape2), moments_v[n].reshape(shape2), "adamw_" + n)
        big_out[n] = [r.reshape((1,) + mine[n].shape) for r in res]

    key = {"w_in": "w_in", "w_rg_a": "rg_a", "w_rg_x": "rg_x", "w_branch_a": "w_a", "w_pool": "pool", "w_branch_b": "w_b", "w_out": "w_out",
           "w_up": "w_up", "w_down": "w_down"}
    order = ("norm_mix_g", "norm_mlp_g", "w_ada", "b_ada", "w_in", "conv_w", "conv_b", "w_rg_a", "b_rg_a", "w_rg_x", "b_rg_x", "a_param",
             "w_branch_a", "w_pool", "b_pool", "pool_scale", "w_branch_b", "w_out", "w_up", "w_down", "final_g")
    ada_out = [g_ada[None], d_ada[None], m_ada[None], v_ada[None]]
    outs = [loss, grad_x[None]]
    for kind in range(4):
        for nme in order:
            outs.append(ada_out[kind] if nme == "w_ada" else big_out[key[nme]][kind] if nme in key else vec_out[nme][kind])
    return tuple(outs)
```

```python
import functools

import jax
import jax.numpy as jnp
from jax import lax
from jax.experimental import pallas as pl
from jax.experimental.pallas import tpu as pltpu

F32, BF16 = jnp.float32, jnp.bfloat16
D_MODEL = 1024
D_FF = 4 * D_MODEL
N_BLOCKS = 4
BLOCK = D_MODEL // N_BLOCKS
CONV_WIDTH = 4
POOL_WINDOWS = (2, 4, 8, 16)
CONV_HALO = 8
POOL_HALO = 16
N_MOD = 6
EPS = 1e-6
C_RG = 8.0
ADAM_LR, ADAM_B1, ADAM_B2, ADAM_EPS, ADAM_WD, ADAM_STEP = 0.001, 0.9, 0.999, 1e-08, 0.01, 10
N_DEV = 8
N_CHIPS = 4
VMEM_LIMIT_V7X = 56 * 2**20
MESH = pl.DeviceIdType.MESH
SDS = jax.ShapeDtypeStruct
HBM_SPEC = pl.BlockSpec(memory_space=pltpu.HBM)
VMEM_SPEC = pl.BlockSpec(memory_space=pltpu.VMEM)


def _params(*semantics):
    return pltpu.CompilerParams(dimension_semantics=semantics, vmem_limit_bytes=VMEM_LIMIT_V7X)


def _coords():
    return lax.axis_index("x"), lax.axis_index("y"), lax.axis_index("c")


def _flip(v, bit):
    return 1 - v if bit else v


def _chips():
    x, y, _ = _coords()
    return [(1 - x, y), (x, 1 - y), (1 - x, 1 - y)]


PEER_SETS = [frozenset(s) for s in (("sibling",), ("chips",), ("sibling", "chips"), ("sibling", "chips", "others"))]


class _Stage:
    def __init__(self):
        self.inputs, self.in_specs, self.outputs, self.out_specs, self.aliases = [], [], [], [], {}
        self.parts, self.n_copies, self.peers = [], 0, set()

    def add(self, inputs, in_spec, outputs, out_spec, n_copies, build, peers, alias=False):
        self.peers |= set(peers)
        i0, o0 = len(self.inputs), len(self.outputs)
        self.inputs += inputs
        self.in_specs += [in_spec] * len(inputs)
        self.outputs += outputs
        self.out_specs += [out_spec] * len(outputs)
        if alias:
            self.aliases.update({i0 + i: o0 + i for i in range(len(inputs))})
        self.parts.append((build, i0, len(inputs), o0, len(outputs)))
        self.n_copies += n_copies
        return list(range(o0, o0 + len(outputs)))

    def copies(self, in_refs, out_refs):
        out = []
        for build, i0, ni, o0, no in self.parts:
            out += build(in_refs[i0:i0 + ni], out_refs[o0:o0 + no])
        assert len(out) == self.n_copies
        return out

    def run(self, in_refs, out_refs, send_sems, recv_sems, start):
        x, y, c = _coords()
        for s, (src, dst, to, landing) in enumerate(self.copies(in_refs, out_refs)):
            if to is None:
                cp = pltpu.make_async_copy(src, dst, send_sems.at[s])
                cp.start() if start else cp.wait()
                continue
            cp = pltpu.make_async_remote_copy(src_ref=src, dst_ref=dst, send_sem=send_sems.at[s], recv_sem=recv_sems.at[s],
                                              device_id=to, device_id_type=MESH)
            if start:
                cp.start()
            else:
                pltpu.make_async_remote_copy(src_ref=landing, dst_ref=landing, send_sem=send_sems.at[s], recv_sem=recv_sems.at[s],
                                             device_id=(x, y, c), device_id_type=MESH).wait_recv()
                cp.wait_send()


def _hosted(stage, body, *, name, in_specs, out_specs, out_shape, grid=(), scratch_shapes=(), compiler_params=None):
    if stage is None:
        return pl.pallas_call(body, name=name, grid=grid, in_specs=in_specs, out_specs=out_specs, out_shape=out_shape,
                              scratch_shapes=list(scratch_shapes), compiler_params=compiler_params)
    single = not isinstance(out_shape, (list, tuple))
    h_out_shape = [out_shape] if single else list(out_shape)
    h_out_specs = [out_specs] if single else list(out_specs)
    n_in, n_out, n_scr = len(in_specs), len(h_out_shape), len(scratch_shapes)
    s_in, s_out = len(stage.inputs), len(stage.outputs)

    def wrapped(*refs):
        h_in, st_in = refs[:n_in], refs[n_in:n_in + s_in]
        h_o, st_o = refs[n_in + s_in:n_in + s_in + n_out], refs[n_in + s_in + n_out:n_in + s_in + n_out + s_out]
        h_scr = refs[n_in + s_in + n_out + s_out:n_in + s_in + n_out + s_out + n_scr]
        send_sems, recv_sems = refs[n_in + s_in + n_out + s_out + n_scr:]
        def begin():
            x, y, c = _coords()
            peers = [(x, y, 1 - c)] * ("sibling" in stage.peers)
            peers += [(px, py, c) for px, py in _chips()] * ("chips" in stage.peers)
            peers += [(px, py, 1 - c) for px, py in _chips()] * ("others" in stage.peers)
            barrier = pltpu.get_barrier_semaphore()
            for peer in peers:
                pl.semaphore_signal(barrier, inc=1, device_id=peer, device_id_type=MESH)
            pl.semaphore_wait(barrier, len(peers))
            stage.run(st_in, st_o, send_sems, recv_sems, True)

        if not grid:
            begin()
            if body is not None:
                body(*h_in, *h_o, *h_scr)
            stage.run(st_in, st_o, send_sems, recv_sems, False)
            return
        ids = [pl.program_id(a) for a in range(len(grid))]
        first = functools.reduce(jnp.logical_and, [i == 0 for i in ids])
        last = functools.reduce(jnp.logical_and, [i == g - 1 for i, g in zip(ids, grid)])
        pl.when(first)(begin)
        body(*h_in, *h_o, *h_scr)
        pl.when(last)(lambda: stage.run(st_in, st_o, send_sems, recv_sems, False))

    call = pl.pallas_call(
        wrapped, name=name, grid=grid, in_specs=list(in_specs) + stage.in_specs, out_specs=h_out_specs + stage.out_specs,
        out_shape=h_out_shape + stage.outputs, input_output_aliases={n_in + i: n_out + o for i, o in stage.aliases.items()},
        scratch_shapes=list(scratch_shapes) + [pltpu.SemaphoreType.DMA((stage.n_copies,)), pltpu.SemaphoreType.DMA((stage.n_copies,))],
        compiler_params=pltpu.CompilerParams(dimension_semantics=("arbitrary",) * len(grid), vmem_limit_bytes=VMEM_LIMIT_V7X,
                                             collective_id=PEER_SETS.index(frozenset(stage.peers))),
    )

    def run(*operands):
        outs = call(*operands, *stage.inputs)
        host = outs[:n_out]
        return (host[0] if single else host), outs[n_out:]

    return run


def _run_stage(stage, name):
    return _hosted(stage, None, name=name, in_specs=[], out_specs=[], out_shape=[])()[1]


def _add_allgather8(stage, v):
    def build(ins, outs):
        x, y, c = _coords()
        me = 4 * x + 2 * y + c
        copies = [(ins[0], outs[0].at[me], None, None)]
        for r in range(1, N_DEV):
            peer = (_flip(x, r & 4), _flip(y, r & 2), _flip(c, r & 1))
            copies.append((ins[0], outs[0].at[me], peer, outs[0].at[me ^ r]))
        return copies

    return stage.add([v], VMEM_SPEC, [SDS((N_DEV,) + v.shape, v.dtype)], VMEM_SPEC, N_DEV, build, ("sibling", "chips", "others"))[0]


def _add_gather_ici(stage, shard, rows=None, whole=None):
    first, count = rows or (0, shard.shape[1])

    def build(ins, outs):
        x, y, c = _coords()
        k_me = 2 * x + y
        part = pl.ds(first, count)
        copies = [] if whole is not None else [(ins[0], outs[0].at[k_me], (x, y, 1 - c), outs[0].at[k_me])]
        for px, py in _chips():
            copies.append((ins[0].at[c, part], outs[0].at[k_me, c, part], (px, py, c), outs[0].at[2 * px + py, c, part]))
        return copies

    if whole is None:
        return stage.add([shard], HBM_SPEC, [SDS((N_CHIPS,) + shard.shape, shard.dtype)], HBM_SPEC, N_CHIPS, build, ("sibling", "chips"))[0]

    def build_into(ins, outs):
        return build(ins[1:], outs)

    i0 = len(stage.inputs)
    out = stage.add([whole, shard], HBM_SPEC, [SDS(whole.shape, whole.dtype)], HBM_SPEC, N_CHIPS - 1, build_into, ("chips",))[0]
    stage.aliases[i0] = out
    return out


def _add_gather_d2d(stage, whole):
    def build(ins, outs):
        x, y, c = _coords()
        return [(outs[0].at[2 * px + py, c], outs[0].at[2 * px + py, c], (x, y, 1 - c), outs[0].at[2 * px + py, 1 - c]) for px, py in _chips()]

    return stage.add([whole], HBM_SPEC, [SDS(whole.shape, whole.dtype)], HBM_SPEC, N_CHIPS - 1, build, ("sibling",), alias=True)[0]


def _add_reduce_d2d(stage, grads):
    def build(ins, outs):
        x, y, c = _coords()
        return [(ins[0].at[k, 1 - c], outs[0].at[k], (x, y, 1 - c), outs[0].at[k]) for k in range(N_CHIPS)]

    return stage.add([grads], HBM_SPEC, [SDS((N_CHIPS,) + grads.shape[2:], grads.dtype)], HBM_SPEC, N_CHIPS, build, ("sibling",))[0]


def _add_reduce_ici(stage, partial):
    def build(ins, outs):
        x, y, c = _coords()
        return [(ins[0].at[2 * px + py], outs[0].at[j], (px, py, c), outs[0].at[j]) for j, (px, py) in enumerate(_chips())]

    return stage.add([partial], HBM_SPEC, [SDS((N_CHIPS - 1,) + partial.shape[1:], partial.dtype)], HBM_SPEC, N_CHIPS - 1, build, ("chips",))[0]


def _add_swap_halves(stage, quarter):
    def build(ins, outs):
        x, y, c = _coords()
        return [(outs[0].at[c], outs[0].at[c], (x, y, 1 - c), outs[0].at[1 - c])]

    return stage.add([quarter], HBM_SPEC, [SDS(quarter.shape, quarter.dtype)], HBM_SPEC, 1, build, ("sibling",), alias=True)[0]


def _sum_pair(grads, from_sibling, place, name):
    _, _, rows, cols = grads.shape
    tr = min(rows, 512)

    def body(place_ref, g_ref, s_ref, o_ref):
        o_ref[0] = (g_ref[0, 0].astype(F32) + s_ref[0].astype(F32)).astype(BF16)

    spec = pl.BlockSpec((1, tr, cols), lambda k, i, place_ref: (k, i, 0))
    return pl.pallas_call(
        body, name=name, out_shape=SDS((N_CHIPS, rows, cols), BF16),
        grid_spec=pltpu.PrefetchScalarGridSpec(
            num_scalar_prefetch=1, grid=(N_CHIPS, rows // tr),
            in_specs=[pl.BlockSpec((1, 1, tr, cols), lambda k, i, place_ref: (k, place_ref[0], i, 0)), spec], out_specs=spec),
        compiler_params=_params("parallel", "parallel"),
    )(place, grads, from_sibling)


def _sum_quarter(grads, from_sibling, from_chips, place, name):
    _, _, rows, cols = grads.shape
    tr = min(rows, 512)

    def body(place_ref, g_ref, s_ref, c_ref, o_ref):
        acc = g_ref[0, 0].astype(F32) + s_ref[0].astype(F32)
        for j in range(N_CHIPS - 1):
            acc = acc + c_ref[j].astype(F32)
        o_ref[0] = acc

    return pl.pallas_call(
        body, name=name, out_shape=SDS((2, rows, cols), F32),
        grid_spec=pltpu.PrefetchScalarGridSpec(
            num_scalar_prefetch=1, grid=(rows // tr,),
            in_specs=[pl.BlockSpec((1, 1, tr, cols), lambda i, place_ref: (place_ref[1], place_ref[0], i, 0)),
                      pl.BlockSpec((1, tr, cols), lambda i, place_ref: (place_ref[1], i, 0)),
                      pl.BlockSpec((N_CHIPS - 1, tr, cols), lambda i, place_ref: (0, i, 0))],
            out_specs=pl.BlockSpec((1, tr, cols), lambda i, place_ref: (place_ref[0], i, 0))),
        compiler_params=_params("parallel"),
    )(place, grads, from_sibling, from_chips)


def _row_spec(ts, cols, col_block=0):
    return pl.BlockSpec((ts, cols), lambda *g: (g[0], col_block))


def _vec_spec(rows, cols):
    return pl.BlockSpec((rows, cols), lambda *g: (0, 0))


def _matmul(a, w, *, transposed, reduce_blocks, name, tm, outs, finish, extra=(), prepare=None, zero_first=(), parts=1, stage=None):
    m_rows = a.shape[0]
    nb, r, c = w.shape
    kb = c if transposed else r
    dims = (((1,), (1,)), ((), ())) if transposed else (((1,), (0,)), ((), ()))
    assert a.shape[1] == (nb * kb if reduce_blocks else kb) and m_rows % tm == 0 and tm % parts == 0
    n_extra, sub = len(extra), tm // parts

    def body(a_ref, w_ref, *rest):
        extra_refs, out_refs = rest[:n_extra], rest[n_extra:]
        if zero_first:
            @pl.when(pl.program_id(0) == 0)
            def _():
                for i in zero_first:
                    out_refs[i][...] = jnp.zeros_like(out_refs[i])

        def views(j):
            rows = pl.ds(j * sub, sub)
            return rows, [ref.at[rows] if ref.shape[0] == tm else ref for ref in extra_refs], [ref.at[rows] if ref.shape[0] == tm else ref for ref in out_refs]

        def product(j):
            rows, ex, ou = views(j)
            if reduce_blocks:
                acc = lax.dot_general(a_ref[rows, 0:kb], w_ref[0], dims, preferred_element_type=F32)
                for k in range(1, nb):
                    acc += lax.dot_general(a_ref[rows, k * kb:(k + 1) * kb], w_ref[k], dims, preferred_element_type=F32)
                return [(None, acc)]
            lhs = a_ref[rows, :] if prepare is None else prepare(a_ref.at[rows], ex, ou)
            return [(k, lax.dot_general(lhs, w_ref[k], dims, preferred_element_type=F32)) for k in range(nb)]

        pending = product(0)
        for j in range(parts):
            ahead = product(j + 1) if j + 1 < parts else None
            _, ex, ou = views(j)
            for k, acc in pending:
                finish(k, acc, ex, ou)
            pending = ahead

    return _hosted(
        stage, body, name=name, grid=(m_rows // tm,), out_shape=[s for s, _ in outs], out_specs=[s for _, s in outs],
        in_specs=[_row_spec(tm, a.shape[1]), pl.BlockSpec((nb, r, c), lambda *g: (0, 0, 0), pipeline_mode=pl.Buffered(1))] + [s for _, s in extra],
        compiler_params=_params("arbitrary"),
    )(a, w, *[e for e, _ in extra])


def _store_blocks(n):
    def finish(k, acc, extra_refs, out_refs):
        if k is None:
            out_refs[0][...] = acc.astype(out_refs[0].dtype)
        else:
            out_refs[0][:, k * n:(k + 1) * n] = acc.astype(out_refs[0].dtype)
    return finish


def _weight_grad(a, b, *, nb, name, tk, stage=None):
    s_rows, k1 = a.shape
    tn = b.shape[1] // nb
    assert k1 % tk == 0 and b.shape[1] % nb == 0

    def body(a_ref, b_ref, o_ref):
        o_ref[0] = lax.dot_general(a_ref[...], b_ref[...], (((0,), (0,)), ((), ())), preferred_element_type=F32).astype(o_ref.dtype)

    return _hosted(
        stage, body, name=name, grid=(nb, k1 // tk), out_shape=SDS((nb, k1, tn), BF16),
        in_specs=[pl.BlockSpec((s_rows, tk), lambda n, i: (0, i)), pl.BlockSpec((s_rows, tn), lambda n, i: (0, n))],
        out_specs=pl.BlockSpec((1, tk, tn), lambda n, i: (n, i, 0)), compiler_params=_params("parallel", "parallel"),
    )(a, b)


def _rms(x):
    r = lax.rsqrt(jnp.mean(x * x, axis=-1, keepdims=True) + EPS)
    return x * r, r


def _accumulate(acc_ref, row, value):
    acc_ref[row:row + 1, :] += jnp.sum(value, axis=0, keepdims=True)


def _gelu_parts(y):
    k0, k1 = 0.7978845608028654, 0.044715
    th = jnp.tanh(k0 * (y + k1 * (y * y * y)))
    gelu = 0.5 * y * (1.0 + th)
    dgelu = 0.5 * (1.0 + th) + 0.5 * y * (1.0 - th * th) * (k0 * (1.0 + 3.0 * k1 * (y * y)))
    return gelu, dgelu


def _one_minus_square(a, log_a):
    return -jnp.tanh(log_a) * (1.0 + a * a)


def _modulated_norm(x, mod_ref, g_ref, sc_row, sh_row):
    xh, _ = _rms(x)
    return ((xh * g_ref[...]) * (1.0 + mod_ref[sc_row:sc_row + 1, :]) + mod_ref[sh_row:sh_row + 1, :]).astype(BF16)


def _modulated_norm_backward(dh, x, dres, mod_ref, g_ref, acc_ref, sc_row):
    xh, r = _rms(x)
    _accumulate(acc_ref, 0, dh * (xh * g_ref[...]))
    _accumulate(acc_ref, 1, dh)
    dn = dh * (1.0 + mod_ref[sc_row:sc_row + 1, :])
    _accumulate(acc_ref, 2, dn * xh)
    dxh = dn * g_ref[...]
    return dres + r * (dxh - xh * jnp.mean(dxh * xh, axis=-1, keepdims=True))


def _seq_forward(proj, conv_w, vecs, w_a, w_x, w_p, *, ts, stage=None):
    s_rows = proj.shape[0]
    d = D_MODEL

    def body(x_ref, xh_ref, y_ref, u_ref, uh_ref, cw_ref, vec_ref, wa_ref, wx_ref, wp_ref,
             xr_o, hr_o, ga_o, p_o, pooled_o, carry, a_scr, b_scr):
        t = pl.program_id(0)

        @pl.when(t == 0)
        def _():
            carry[...] = jnp.zeros_like(carry)

        rows = lax.broadcasted_iota(jnp.int32, (ts, BLOCK), 0)
        start = (rows == 0) & (t == 0)
        for g in range(N_BLOCKS):
            lanes = slice(g * BLOCK, (g + 1) * BLOCK)
            x = x_ref[:, lanes]
            xx = jnp.concatenate([jnp.where(t == 0, 0.0, xh_ref[:, lanes]), x], axis=0)
            xr = vec_ref[0:1, lanes] + x * cw_ref[CONV_WIDTH - 1:CONV_WIDTH, lanes]
            for j in range(CONV_WIDTH - 1):
                xr = xr + pltpu.roll(xx, CONV_WIDTH - 1 - j, 0)[CONV_HALO:] * cw_ref[j:j + 1, lanes]
            xr_o[:, lanes] = xr
            xrb = xr.astype(BF16)
            r = jax.nn.sigmoid(jnp.dot(xrb, wa_ref[g], preferred_element_type=F32) + vec_ref[1:2, lanes])
            gate_i = jax.nn.sigmoid(jnp.dot(xrb, wx_ref[g], preferred_element_type=F32) + vec_ref[2:3, lanes])
            log_a = (-C_RG * r) * jax.nn.softplus(vec_ref[3:4, lanes])
            a = jnp.exp(log_a)
            a_scr[:, lanes] = a
            b_scr[:, lanes] = xr * gate_i * jnp.where(start, 1.0, jnp.sqrt(_one_minus_square(a, log_a)))

        sub = lax.broadcasted_iota(jnp.int32, (8, d), 0)

        def chunk(i, h):
            a = a_scr[pl.ds(i * 8, 8), :]
            b = b_scr[pl.ds(i * 8, 8), :]
            for s in (1, 2, 4):
                keep = sub >= s
                b = jnp.where(keep, a * pltpu.roll(b, s, 0) + b, b)
                a = jnp.where(keep, a * pltpu.roll(a, s, 0), a)
            hh = a * h + b
            hr_o[pl.ds(i * 8, 8), :] = hh
            return hh[7:8, :]

        carry[...] = lax.fori_loop(0, ts // 8, chunk, carry[...])
        pos = (rows + t * ts + 1).astype(F32)
        for g, win in enumerate(POOL_WINDOWS):
            lanes = slice(g * BLOCK, (g + 1) * BLOCK)
            gelu, _ = _gelu_parts(y_ref[:, lanes])
            ga_o[:, lanes] = (gelu * hr_o[:, lanes]).astype(BF16)
            u = u_ref[:, lanes]
            sums, have = jnp.concatenate([jnp.where(t == 0, 0.0, uh_ref[:, lanes]), u], axis=0), 1
            while have < win:
                sums = sums + pltpu.roll(sums, have, 0)
                have *= 2
            pb = (sums[POOL_HALO:] * (1.0 / jnp.minimum(pos, float(win))) - u).astype(BF16)
            p_o[:, lanes] = pb
            mixed = jnp.dot(pb, wp_ref[g], preferred_element_type=F32) + vec_ref[4:5, lanes]
            pooled_o[:, lanes] = (mixed * vec_ref[5:6, lanes]).astype(BF16)

    halo_x = pl.BlockSpec((CONV_HALO, d), lambda t: (jnp.maximum(t * (ts // CONV_HALO) - 1, 0), 0))
    halo_u = pl.BlockSpec((POOL_HALO, d), lambda t: (jnp.maximum(t * (ts // POOL_HALO) - 1, 0), 2))
    wspec = pl.BlockSpec((N_BLOCKS, BLOCK, BLOCK), lambda t: (0, 0, 0))
    return _hosted(
        stage, body, name="seq_forward", grid=(s_rows // ts,),
        out_shape=[SDS((s_rows, d), F32), SDS((s_rows, d), F32), SDS((s_rows, d), BF16), SDS((s_rows, d), BF16), SDS((s_rows, d), BF16)],
        in_specs=[_row_spec(ts, d, 0), halo_x, _row_spec(ts, d, 1), _row_spec(ts, d, 2), halo_u, _vec_spec(CONV_WIDTH, d), _vec_spec(8, d),
                  wspec, wspec, wspec],
        out_specs=[_row_spec(ts, d)] * 5,
        scratch_shapes=[pltpu.VMEM((1, d), F32), pltpu.VMEM((ts, d), F32), pltpu.VMEM((ts, d), F32)],
        compiler_params=_params("arbitrary"),
    )(proj, proj, proj, proj, proj, conv_w, vecs, w_a, w_x, w_p)


def _branches(ga, pooled, proj, w_a, w_b, *, ts, stage=None):
    s_rows, d = ga.shape

    def body(a_ref, p_ref, ga_ref, gb_ref, wa_ref, wb_ref, merged_o, sa_o, sb_o, ta_o, tb_o):
        bra = jnp.dot(a_ref[...], wa_ref[0], preferred_element_type=F32)
        brb = jnp.dot(p_ref[...], wb_ref[0], preferred_element_type=F32)
        sa, sb = jax.nn.sigmoid(ga_ref[...]), jax.nn.sigmoid(gb_ref[...])
        merged_o[...] = (sa * bra + sb * brb).astype(BF16)
        sa_o[...] = sa.astype(BF16)
        sb_o[...] = sb.astype(BF16)
        ta_o[...] = (bra * (sa * (1.0 - sa))).astype(BF16)
        tb_o[...] = (brb * (sb * (1.0 - sb))).astype(BF16)

    rows = _row_spec(ts, d)
    wspec = pl.BlockSpec((1, d, d), lambda *g: (0, 0, 0), pipeline_mode=pl.Buffered(1))
    return _hosted(
        stage, body, name="branches", grid=(s_rows // ts,), out_shape=[SDS((s_rows, d), BF16)] * 5,
        in_specs=[rows, rows, _row_spec(ts, d, 3), _row_spec(ts, d, 4), wspec, wspec], out_specs=[rows] * 5,
        compiler_params=_params("arbitrary"),
    )(ga, pooled, proj, proj, w_a, w_b)


def _seq_backward_by_block(proj, xr, hr, p, dga, dpooled, dgab, conv_w, vecs, w_a, w_x, w_p, *, ts, stage=None):
    s_rows = proj.shape[0]
    d = D_MODEL
    n_t = s_rows // ts
    nn, nt, tn = (((1,), (0,)), ((), ())), (((1,), (1,)), ((), ())), (((0,), (0,)), ((), ()))

    def dot(lhs, rhs, dims):
        return lax.dot_general(lhs, rhs, dims, preferred_element_type=F32)

    def body(x_ref, y_ref, xr_ref, hr_ref, hh_ref, p_ref, dga_ref, dpl_ref, dgab_ref, cw_ref, vec_ref, wa_ref, wx_ref, wp_ref,
             dproj_o, acc_o, dwa_o, dwx_o, dwp_o, g_carry, dxr_carry, q_carry, a_scr, b_scr, g_scr, a_keep, r_keep, i_keep, m_keep):
        step = pl.program_id(0)
        t = n_t - 1 - step

        @pl.when(step == 0)
        def _():
            for ref in (acc_o, dwa_o, dwx_o, dwp_o, g_carry, dxr_carry, q_carry):
                ref[...] = jnp.zeros_like(ref)

        rows = lax.broadcasted_iota(jnp.int32, (ts, BLOCK), 0)
        start = (rows == 0) & (t == 0)

        def add_sum(row, lanes, value):
            acc_o[row:row + 1, lanes] += jnp.sum(value, axis=0, keepdims=True)

        for g in range(N_BLOCKS):
            lanes = slice(g * BLOCK, (g + 1) * BLOCK)
            xrb = xr_ref[:, lanes].astype(BF16)
            r = jax.nn.sigmoid(dot(xrb, wa_ref[g], nn) + vec_ref[1:2, lanes])
            gate_i = jax.nn.sigmoid(dot(xrb, wx_ref[g], nn) + vec_ref[2:3, lanes])
            log_a = (-C_RG * r) * jax.nn.softplus(vec_ref[3:4, lanes])
            a = jnp.exp(log_a)
            a_keep[:, lanes], r_keep[:, lanes], i_keep[:, lanes], m_keep[:, lanes] = a, r, gate_i, _one_minus_square(a, log_a)
            gelu, dgelu = _gelu_parts(y_ref[:, lanes])
            dga_t = dga_ref[:, lanes]
            dproj_o[:, d + g * BLOCK:d + (g + 1) * BLOCK] = (dga_t * hr_ref[:, lanes] * dgelu).astype(BF16)
            a_scr[:, lanes] = jnp.where(rows == ts - 1, 1.0, pltpu.roll(a, ts - 1, 0))
            b_scr[:, lanes] = dga_t * gelu

        sub = lax.broadcasted_iota(jnp.int32, (8, d), 0)

        def chunk(i, g_next):
            at = pl.multiple_of((ts // 8 - 1 - i) * 8, 8)
            aa = a_scr[pl.ds(at, 8), :]
            bb = b_scr[pl.ds(at, 8), :]
            for s in (1, 2, 4):
                keep = sub < 8 - s
                bb = jnp.where(keep, bb + aa * pltpu.roll(bb, 8 - s, 0), bb)
                aa = jnp.where(keep, aa * pltpu.roll(aa, 8 - s, 0), aa)
            gg = aa * g_next + bb
            g_scr[pl.ds(at, 8), :] = gg
            return gg[0:1, :]

        g_first = lax.fori_loop(0, ts // 8, chunk, g_carry[...])
        g_carry[...] = a_keep[0:1, :] * g_first

        pos = (rows + t * ts + 1).astype(F32)
        for g, win in enumerate(POOL_WINDOWS):
            lanes = slice(g * BLOCK, (g + 1) * BLOCK)
            gs, xr, a, r, gate_i, m_square = g_scr[:, lanes], xr_ref[:, lanes], a_keep[:, lanes], r_keep[:, lanes], i_keep[:, lanes], m_keep[:, lanes]
            sp = jax.nn.softplus(vec_ref[3:4, lanes])
            mult = jnp.where(start, 1.0, jnp.sqrt(m_square))
            h_before = jnp.where(t == 0, 0.0, hh_ref[CONV_HALO - 1:CONV_HALO, lanes])
            h_prev = jnp.where(rows == 0, h_before, pltpu.roll(hr_ref[:, lanes], 1, 0))
            gm = gs * mult
            dxr = gm * gate_i
            d_i = gm * xr
            d_mult = (gs * xr) * gate_i
            dlog_a = (gs * h_prev) * a - jnp.where(start, 0.0, d_mult * (a * a) * lax.rsqrt(m_square))
            dzr = (dlog_a * (-C_RG * sp)) * (r * (1.0 - r))
            dzi = d_i * (gate_i * (1.0 - gate_i))
            add_sum(5, lanes, dzr)
            add_sum(6, lanes, dzi)
            acc_o[7:8, lanes] += jnp.sum(dlog_a * r, axis=0, keepdims=True) * (-C_RG * jax.nn.sigmoid(vec_ref[3:4, lanes]))
            xrb, dzrb, dzib = xr.astype(BF16), dzr.astype(BF16), dzi.astype(BF16)
            dwa_o[g] += dot(xrb, dzrb, tn)
            dwx_o[g] += dot(xrb, dzib, tn)
            dxr = dxr + dot(dzrb, wa_ref[g], nt) + dot(dzib, wx_ref[g], nt)

            x = x_ref[:, lanes]
            ext = jnp.concatenate([dxr, dxr_carry[:, lanes]], axis=0)
            dx = dxr * cw_ref[CONV_WIDTH - 1:CONV_WIDTH, lanes]
            add_sum(CONV_WIDTH - 1, lanes, x * dxr)
            for j in range(CONV_WIDTH - 1):
                ahead = pltpu.roll(ext, ts + CONV_HALO - (CONV_WIDTH - 1 - j), 0)[:ts]
                dx = dx + ahead * cw_ref[j:j + 1, lanes]
                add_sum(j, lanes, x * ahead)
            add_sum(4, lanes, dxr)
            dxr_carry[:, lanes] = dxr[0:CONV_HALO, :]
            dproj_o[:, lanes] = dx.astype(BF16)

            pb = p_ref[:, lanes]
            mixed = dot(pb, wp_ref[g], nn) + vec_ref[4:5, lanes]
            dpl = dpl_ref[:, lanes]
            add_sum(9, lanes, dpl * mixed)
            dmixed = dpl * vec_ref[5:6, lanes]
            add_sum(8, lanes, dmixed)
            dmb = dmixed.astype(BF16)
            dwp_o[g] += dot(pb, dmb, tn)
            dp = dot(dmb, wp_ref[g], nt)
            q = dp * (1.0 / jnp.minimum(pos, float(win)))
            sums, have = jnp.concatenate([q, q_carry[:, lanes]], axis=0), 1
            while have < win:
                sums = sums + pltpu.roll(sums, ts + POOL_HALO - have, 0)
                have *= 2
            q_carry[:, lanes] = q[0:POOL_HALO, :]
            dproj_o[:, 2 * d + g * BLOCK:2 * d + (g + 1) * BLOCK] = (sums[:ts] - dp).astype(BF16)

        dproj_o[:, 3 * d:5 * d] = dgab_ref[...]

    def rev(cols, col_block=0):
        return pl.BlockSpec((ts, cols), lambda i: (n_t - 1 - i, col_block))

    halo_h = pl.BlockSpec((CONV_HALO, d), lambda i: (jnp.maximum((n_t - 1 - i) * (ts // CONV_HALO) - 1, 0), 0))
    wspec = pl.BlockSpec((N_BLOCKS, BLOCK, BLOCK), lambda i: (0, 0, 0))
    return _hosted(
        stage, body, name="seq_backward", grid=(n_t,),
        out_shape=[SDS((s_rows, 5 * d), BF16), SDS((16, d), F32)] + [SDS((N_BLOCKS, BLOCK, BLOCK), F32)] * 3,
        in_specs=[rev(d, 0), rev(d, 1), rev(d), rev(d), halo_h, rev(d), rev(d), rev(d), rev(2 * d), _vec_spec(CONV_WIDTH, d), _vec_spec(8, d),
                  wspec, wspec, wspec],
        out_specs=[rev(5 * d), _vec_spec(16, d), wspec, wspec, wspec],
        scratch_shapes=[pltpu.VMEM((1, d), F32), pltpu.VMEM((CONV_HALO, d), F32), pltpu.VMEM((POOL_HALO, d), F32)] + [pltpu.VMEM((ts, d), F32)] * 7,
        compiler_params=_params("arbitrary"),
    )(proj, proj, xr, hr, hr, p, dga, dpooled, dgab, conv_w, vecs, w_a, w_x, w_p)


def _adamw_math(w, g, m, v):
    m = ADAM_B1 * m + (1.0 - ADAM_B1) * g
    v = ADAM_B2 * v + (1.0 - ADAM_B2) * jnp.square(g)
    m_hat = m / (1.0 - ADAM_B1 ** ADAM_STEP)
    v_hat = v / (1.0 - ADAM_B2 ** ADAM_STEP)
    delta = -ADAM_LR * (m_hat / (jnp.sqrt(v_hat) + ADAM_EPS) + ADAM_WD * w)
    return delta, m, v


def _adamw(w, g, m, v, name):
    rows, cols = w.shape
    tr = min(rows, 512)

    def body(w_ref, g_ref, m_ref, v_ref, g_out, d_ref, mo_ref, vo_ref):
        g = g_ref[...]
        g_out[...] = g
        d_ref[...], mo_ref[...], vo_ref[...] = _adamw_math(w_ref[...], g, m_ref[...], v_ref[...])

    spec = pl.BlockSpec((tr, cols), lambda i: (i, 0))
    return pl.pallas_call(
        body, name=name, grid=(rows // tr,), out_shape=[SDS((rows, cols), F32)] * 4, in_specs=[spec] * 4, out_specs=[spec] * 4,
        compiler_params=_params("parallel"),
    )(w, g, m, v)


def _prologue(c, conv_shard, w_ada, b_cols, w_in_halves, gate_halves):
    n_cols = w_ada.shape[1]
    n_g = len(gate_halves)
    half_rows = w_in_halves.shape[1] // 2
    n_sems = 3 * N_DEV + N_CHIPS * n_g + 5 + 2 + 3

    def body(c_ref, conv_ref, w_ref, b_ref, win_ref, *rest):
        gate_refs = rest[:n_g]
        act_o, mod_o, conv_o, win_o = rest[n_g:n_g + 4]
        gate_o = rest[n_g + 4:2 * n_g + 4]
        c_all, mod_cols, send_sems, recv_sems = rest[2 * n_g + 4:]
        x, y, c = _coords()
        me, k_me = 4 * x + 2 * y + c, 2 * x + y
        sibling, x_nbr, y_nbr = (x, y, 1 - c), (1 - x, y, c), (x, 1 - y, c)
        k_x, k_y, k_d = 2 * (1 - x) + y, 2 * x + (1 - y), 2 * (1 - x) + (1 - y)
        counter = iter(range(n_sems))

        def copy(src, dst, to, landing):
            s = next(counter)
            out = pltpu.make_async_remote_copy(src_ref=src, dst_ref=dst, send_sem=send_sems.at[s], recv_sem=recv_sems.at[s],
                                               device_id=to, device_id_type=MESH)
            arrival = pltpu.make_async_remote_copy(src_ref=landing, dst_ref=landing, send_sem=send_sems.at[s], recv_sem=recv_sems.at[s],
                                                   device_id=(x, y, c), device_id_type=MESH)
            return out, arrival

        def allgather(src, out):
            s = next(counter)
            own = pltpu.make_async_copy(src, out.at[me], send_sems.at[s])
            peers = [copy(src, out.at[me], (_flip(x, r & 4), _flip(y, r & 2), _flip(c, r & 1)), out.at[me ^ r]) for r in range(1, N_DEV)]
            return own, peers

        def start(group):
            own, peers = group
            own.start()
            for out, _ in peers:
                out.start()

        def finish(group):
            own, peers = group
            for out, arrival in peers:
                arrival.wait_recv()
                out.wait_send()
            own.wait()

        rows_a, rows_b = pl.ds(0, half_rows), pl.ds(half_rows, half_rows)
        c_group, conv_group = allgather(c_ref, c_all), allgather(conv_ref, conv_o)
        gates = []
        for g_ref, g_out in zip(gate_refs, gate_o):
            gates.append(copy(g_ref, g_out.at[k_me], sibling, g_out.at[k_me]))
            gates += [copy(g_ref.at[c], g_out.at[k_me, c], (px, py, c), g_out.at[2 * px + py, c]) for px, py in _chips()]
        own_in = copy(win_ref, win_o.at[k_me], sibling, win_o.at[k_me])
        to_x = [copy(win_ref.at[c, rows], win_o.at[k_me, c, rows], x_nbr, win_o.at[k_x, c, rows]) for rows in (rows_a, rows_b)]
        to_y = [copy(win_ref.at[c, rows], win_o.at[k_me, c, rows], y_nbr, win_o.at[k_y, c, rows]) for rows in (rows_b, rows_a)]
        mod_group = allgather(mod_cols, mod_o)
        relay_y = copy(win_o.at[k_x, c, rows_a], win_o.at[k_x, c, rows_a], y_nbr, win_o.at[k_d, c, rows_a])
        relay_x = copy(win_o.at[k_y, c, rows_b], win_o.at[k_y, c, rows_b], x_nbr, win_o.at[k_d, c, rows_b])
        sib_x, sib_y, sib_d = (copy(win_o.at[k, c], win_o.at[k, c], sibling, win_o.at[k, 1 - c]) for k in (k_x, k_y, k_d))

        start(c_group)
        start(conv_group)
        for out, _ in gates + [own_in] + to_x + to_y:
            out.start()
        finish(c_group)
        cv = jnp.concatenate([c_all[r] for r in range(N_DEV)], axis=0)
        act = cv * jax.nn.sigmoid(cv)
        act_o[...] = act
        mod_cols[...] = jnp.dot(act.astype(BF16), w_ref[...].astype(BF16), preferred_element_type=F32) + b_ref[...]
        start(mod_group)
        to_x[0][1].wait_recv()
        relay_y[0].start()
        to_y[0][1].wait_recv()
        relay_x[0].start()
        finish(mod_group)
        finish(conv_group)
        to_x[1][1].wait_recv()
        sib_x[0].start()
        to_y[1][1].wait_recv()
        sib_y[0].start()
        for out, arrival in gates + [own_in]:
            arrival.wait_recv()
            out.wait_send()
        for out, arrival in (relay_y, relay_x):
            arrival.wait_recv()
            out.wait_send()
        sib_d[0].start()
        for out, _ in to_x + to_y:
            out.wait_send()
        for out, arrival in (sib_x, sib_y, sib_d):
            arrival.wait_recv()
            out.wait_send()

    gate_shapes = [SDS((N_CHIPS,) + g.shape, g.dtype) for g in gate_halves]
    return pl.pallas_call(
        body, name="prologue",
        out_shape=[SDS((N_DEV, c.shape[1]), F32), SDS((N_DEV, N_DEV, n_cols), F32), SDS((N_DEV,) + conv_shard.shape, F32),
                   SDS((N_CHIPS,) + w_in_halves.shape, w_in_halves.dtype)] + gate_shapes,
        in_specs=[VMEM_SPEC] * 4 + [HBM_SPEC] * (1 + n_g), out_specs=[VMEM_SPEC] * 3 + [HBM_SPEC] * (1 + n_g),
        scratch_shapes=[pltpu.VMEM((N_DEV,) + c.shape, F32), pltpu.VMEM((N_DEV, n_cols), F32),
                        pltpu.SemaphoreType.DMA((n_sems,)), pltpu.SemaphoreType.DMA((n_sems,))],
        compiler_params=pltpu.CompilerParams(vmem_limit_bytes=VMEM_LIMIT_V7X),
    )(c, conv_shard, w_ada, b_cols, w_in_halves, *gate_halves)


def _ada_backward(act_t, dmod_cols, w, m, v):
    rows, cols = w.shape
    tn = 512

    def body(a_ref, dm_ref, w_ref, m_ref, v_ref, g_ref, d_ref, mo_ref, vo_ref):
        g = jnp.dot(a_ref[...].astype(BF16), dm_ref[...].astype(BF16), preferred_element_type=F32)
        g_ref[...] = g
        d_ref[...], mo_ref[...], vo_ref[...] = _adamw_math(w_ref[...], g, m_ref[...], v_ref[...])

    spec = pl.BlockSpec((rows, tn), lambda j: (0, j))
    return pl.pallas_call(
        body, name="ada_backward", grid=(cols // tn,), out_shape=[SDS((rows, cols), F32)] * 4,
        in_specs=[pl.BlockSpec((rows, N_DEV), lambda j: (0, 0)), pl.BlockSpec((N_DEV, tn), lambda j: (0, j)), spec, spec, spec],
        out_specs=[spec] * 4, compiler_params=_params("parallel"),
    )(act_t, dmod_cols, w, m, v)


def _sum_rows(gathered, n_loss):
    n = gathered.shape[2]

    def body(g_ref, o_ref):
        acc = g_ref[0]
        for r in range(1, N_DEV):
            acc = acc + g_ref[r]
        o_ref[...] = acc
        o_ref[:, n - n_loss:n] = jnp.broadcast_to(jnp.sum(acc[:, n - n_loss:n], axis=1, keepdims=True), (1, n_loss))

    return pl.pallas_call(body, name="sum_rows", out_shape=SDS((1, n), F32), in_specs=[VMEM_SPEC], out_specs=VMEM_SPEC)(gathered)


def _adamw_vectors(total, offsets, separate, params):
    n_p = len(params)
    apart = [g for g in separate if g is not None]

    def body(*refs):
        total_ref, apart_refs = refs[0], list(refs[1:1 + len(apart)])
        ins, outs = refs[1 + len(apart):1 + len(apart) + 3 * n_p], refs[1 + len(apart) + 3 * n_p:]
        for i in range(n_p):
            w_ref, m_ref, v_ref = ins[3 * i:3 * i + 3]
            g = apart_refs.pop(0)[...] if offsets[i] is None else total_ref[:, offsets[i]:offsets[i] + w_ref.shape[1]]
            outs[4 * i][...] = g
            outs[4 * i + 1][...], outs[4 * i + 2][...], outs[4 * i + 3][...] = _adamw_math(w_ref[...], g, m_ref[...], v_ref[...])

    flat = [a for p in params for a in p]
    res = pl.pallas_call(
        body, name="adamw_vectors", out_shape=[SDS(p[0].shape, F32) for p in params for _ in range(4)],
        in_specs=[VMEM_SPEC] * (1 + len(apart) + len(flat)), out_specs=[VMEM_SPEC] * (4 * n_p),
    )(total, *apart, *flat)
    return [tuple(res[4 * i:4 * i + 4]) for i in range(n_p)]


def _blocks_to_pieces(w):
    nb, rows, n = w.shape
    q = rows // N_CHIPS
    return w.reshape(nb, N_CHIPS, q, n).transpose(1, 0, 2, 3).reshape(N_CHIPS, 2, nb // 2, q, n)


def _pieces_to_blocks(w):
    n_chips, _, half, q, n = w.shape
    return w.reshape(n_chips, 2 * half, q, n).transpose(1, 0, 2, 3).reshape(2 * half, n_chips * q, n)


def kernel(x, c, norm_mix_g, norm_mlp_g, w_ada, b_ada, w_in, conv_w, conv_b, w_rg_a, b_rg_a, w_rg_x, b_rg_x, a_param, w_branch_a, w_pool, b_pool, pool_scale, w_branch_b, w_out, w_up, w_down, final_g, loss_target, m_norm_mix_g, m_norm_mlp_g, m_w_ada, m_b_ada, m_w_in, m_conv_w, m_conv_b, m_w_rg_a, m_b_rg_a, m_w_rg_x, m_b_rg_x, m_a_param, m_w_branch_a, m_w_pool, m_b_pool, m_pool_scale, m_w_branch_b, m_w_out, m_w_up, m_w_down, m_final_g, v_norm_mix_g, v_norm_mlp_g, v_w_ada, v_b_ada, v_w_in, v_conv_w, v_conv_b, v_w_rg_a, v_b_rg_a, v_w_rg_x, v_b_rg_x, v_a_param, v_w_branch_a, v_w_pool, v_b_pool, v_pool_scale, v_w_branch_b, v_w_out, v_w_up, v_w_down, v_final_g):
    d = D_MODEL
    s_rows = x.shape[1]
    ts, tsq = min(512, s_rows), min(256, s_rows)
    xi, yi, ci = _coords()
    k_me, me = 2 * xi + yi, 4 * xi + 2 * yi + ci
    ada_cols = w_ada.shape[2]
    conv_cols = conv_w.shape[2]
    n_in, n_up = w_in.shape[2], w_up.shape[2]

    names = ("w_in", "w_up", "w_down", "w_a", "w_b", "w_out", "rg_a", "rg_x", "pool")
    mine = dict(zip(names, (w_in[0], w_up[0], w_down[0], w_branch_a[0], w_branch_b[0], w_out[0], w_rg_a[0], w_rg_x[0], w_pool[0])))
    moments_m = dict(zip(names, (m_w_in[0], m_w_up[0], m_w_down[0], m_w_branch_a[0], m_w_branch_b[0], m_w_out[0], m_w_rg_a[0], m_w_rg_x[0], m_w_pool[0])))
    moments_v = dict(zip(names, (v_w_in[0], v_w_up[0], v_w_down[0], v_w_branch_a[0], v_w_branch_b[0], v_w_out[0], v_w_rg_a[0], v_w_rg_x[0], v_w_pool[0])))
    block_weights, squares = ("rg_a", "rg_x", "pool"), ("w_a", "w_b", "w_out")
    place = jnp.stack([ci, k_me]).astype(jnp.int32)
    g_mix, g_mlp, g_fin = norm_mix_g, norm_mlp_g, final_g.reshape(1, d)
    x0, target = x[0], loss_target[0]
    wg = functools.partial(_weight_grad, tk=min(1024, d))
    plain, back, back_sum = (dict(transposed=t, reduce_blocks=r) for t, r in ((False, False), (True, False), (True, True)))
    rows_d, vec8, vec1 = _row_spec(ts, d), _vec_spec(8, d), _vec_spec(1, d)
    tw = min(1024, s_rows)
    halves_of_tile = 2

    def out_rows(cols, dtype, t=ts):
        return SDS((s_rows, cols), dtype), _row_spec(t, cols)

    sums_out = (SDS((8, d), F32), vec8)

    def halves(n):
        return mine[n].astype(BF16).reshape(2, -1, mine[n].shape[-1])

    def blocks(w):
        return _pieces_to_blocks(w.reshape(N_CHIPS, 2, N_BLOCKS // 2, BLOCK // N_CHIPS, BLOCK))

    act_all, mod_all, conv_all, w_in_whole, *gate_wholes = _prologue(
        c, conv_w[0], w_ada[0], lax.dynamic_slice_in_dim(b_ada, k_me * ada_cols, ada_cols, axis=1), halves("w_in"), [halves(n) for n in block_weights])
    conv_full = conv_all[0::2].transpose(1, 0, 2).reshape(CONV_WIDTH, d)
    mod_mine = lax.dynamic_index_in_dim(mod_all, me, axis=1, keepdims=False)[0::2]
    mod = jnp.concatenate([mod_mine.reshape(N_MOD, d), jnp.zeros((8 - N_MOD, d), F32)], axis=0)
    w_in_all = w_in_whole.reshape(N_CHIPS, d, n_in)
    vecs = jnp.concatenate([conv_b, b_rg_a, b_rg_x, a_param, b_pool, pool_scale, jnp.zeros((2, d), F32)], axis=0)

    def norm_first(a_ref, extra_refs, out_refs):
        h = _modulated_norm(a_ref[...], extra_refs[0], extra_refs[1], 1, 0)
        out_refs[1][...] = h
        return h

    stage = _Stage()
    i_blk = [_add_gather_d2d(stage, g) for g in gate_wholes]
    i_sq = [_add_gather_ici(stage, halves(n)) for n in squares]
    down_rows = mine["w_down"].shape[0] // 8
    i_down = _add_gather_ici(stage, halves("w_down"), rows=(0, down_rows))
    (proj, h1), got = _matmul(x0, w_in_all, name="proj_in", tm=ts, extra=[(mod, vec8), (g_mix, vec1)], prepare=norm_first,
                              outs=[out_rows(N_CHIPS * n_in, F32), out_rows(d, BF16)], finish=_store_blocks(n_in), parts=halves_of_tile,                              stage=stage, **plain)
    w_rg_a_all, w_rg_x_all, w_pool_all = (blocks(got[i]) for i in i_blk)
    stage = _Stage()
    i_sq = [_add_gather_d2d(stage, got[i]) for i in i_sq]
    i_up = _add_gather_ici(stage, halves("w_up"))
    w_down_so_far = got[i_down]
    (xr, hr, ga, p, pooled), got = _seq_forward(proj, conv_full, vecs, w_rg_a_all, w_rg_x_all, w_pool_all, ts=tsq, stage=stage)
    w_a_all, w_b_all, w_out_all = (got[i].reshape(1, d, d) for i in i_sq)
    stage = _Stage()
    i_up = _add_gather_d2d(stage, got[i_up])
    i_down = _add_gather_ici(stage, halves("w_down"), rows=(down_rows, 2 * down_rows), whole=w_down_so_far)
    (merged, s_a, s_b, t_a, t_b), got = _branches(ga, pooled, proj, w_a_all, w_b_all, ts=ts, stage=stage)
    w_up_all = got[i_up].reshape(N_CHIPS, d, n_up)
    stage = _Stage()
    i_down = _add_gather_ici(stage, halves("w_down"), rows=(3 * down_rows, down_rows), whole=got[i_down])

    def residual_norm(k, acc, extra_refs, out_refs):
        x_ref, mod_ref, g_ref = extra_refs
        out_refs[0][...] = acc.astype(BF16)
        x2_t = x_ref[...] + mod_ref[2:3, :] * acc
        out_refs[1][...] = x2_t
        out_refs[2][...] = _modulated_norm(x2_t, mod_ref, g_ref, 4, 3)

    (mo, x2, h2), got = _matmul(merged, w_out_all, name="mix_out", tm=ts, extra=[(x0, rows_d), (mod, vec8), (g_mlp, vec1)],
                                outs=[out_rows(d, BF16), out_rows(d, F32), out_rows(d, BF16)], finish=residual_norm, parts=halves_of_tile,                                stage=stage, transposed=False, reduce_blocks=True)
    stage = _Stage()
    i_down = _add_gather_d2d(stage, got[i_down])

    def relu_squared(k, acc, extra_refs, out_refs):
        out_refs[0][:, k * n_up:(k + 1) * n_up] = jnp.square(jnp.maximum(acc, 0.0)).astype(BF16)

    (ff,), got = _matmul(h2, w_up_all, name="mlp_up", tm=ts, outs=[out_rows(D_FF, BF16)], finish=relu_squared, parts=halves_of_tile,
                         stage=stage, **plain)
    w_down_all = got[i_down].reshape(N_CHIPS, D_FF // N_CHIPS, d)

    def loss_head(k, acc, extra_refs, out_refs):
        x2_ref, mod_ref, g_ref, t_ref = extra_refs
        dx_ref, df_ref, acc_ref = out_refs
        xh, r = _rms(x2_ref[...] + mod_ref[5:6, :] * acc)
        err = xh * g_ref[...] - t_ref[...]
        dy = err / d
        dxh = dy * g_ref[...]
        dx3_t = r * (dxh - xh * jnp.mean(dxh * xh, axis=-1, keepdims=True))
        dx_ref[...] = dx3_t
        df_ref[...] = (dx3_t * mod_ref[5:6, :]).astype(BF16)
        _accumulate(acc_ref, 0, dy * xh)
        _accumulate(acc_ref, 1, dx3_t * acc)
        _accumulate(acc_ref, 2, jnp.square(err) * (0.5 / d))

    dx3, dffo, sums_head = _matmul(ff, w_down_all, name="mlp_down", tm=ts, extra=[(x2, rows_d), (mod, vec8), (g_fin, vec1), (target, rows_d)],
                                   outs=[out_rows(d, F32), out_rows(d, BF16), sums_out], finish=loss_head, zero_first=(2,), parts=halves_of_tile,                                   transposed=False, reduce_blocks=True)

    partial_of, from_sibling, chip_sum, from_chips, half_done, quarter = {}, {}, {}, {}, {}, {}

    def publish(n, g):
        g = _blocks_to_pieces(g).astype(BF16) if n in block_weights else g
        partial_of[n] = g.reshape(N_CHIPS, 2, -1, mine[n].shape[-1])

    def exchange(to_sibling=(), to_chips=(), swap=()):
        st = _Stage()
        slots = [(n, from_sibling, _add_reduce_d2d(st, partial_of[n])) for n in to_sibling]
        slots += [(n, from_chips, _add_reduce_ici(st, chip_sum[n])) for n in to_chips]
        slots += [(n, quarter, _add_swap_halves(st, half_done[n])) for n in swap]
        return st, slots

    def collect(slots, outs):
        for n, where, i in slots:
            where[n] = outs[i]

    def sum_pairs(*ns):
        for n in ns:
            chip_sum[n] = _sum_pair(partial_of[n], from_sibling[n], place, "sum_pair_" + n)

    def sum_quarters(*ns):
        for n in ns:
            half_done[n] = _sum_quarter(partial_of[n], from_sibling[n], from_chips[n], place, "sum_quarter_" + n)

    publish("w_down", wg(ff, dffo, nb=1, name="grad_w_down"))

    def relu_squared_backward(k, acc, extra_refs, out_refs):
        cols = slice(k * n_up, (k + 1) * n_up)
        out_refs[0][:, cols] = (acc * (2.0 * jnp.sqrt(extra_refs[0][:, cols].astype(F32)))).astype(BF16)

    stage, slots = exchange(to_sibling=["w_down"])
    (dup,), got = _matmul(dffo, w_down_all, name="d_mlp_down", tm=ts, extra=[(ff, _row_spec(ts, D_FF))], outs=[out_rows(D_FF, BF16)],
                          finish=relu_squared_backward, parts=halves_of_tile, stage=stage, **back)
    collect(slots, got)
    sum_pairs("w_down")

    def norm_mlp_backward(k, acc, extra_refs, out_refs):
        x2_ref, dres_ref, mod_ref, g_ref, mo_ref = extra_refs
        dx_ref, dmo_ref, acc_ref = out_refs
        dx =_modulated_norm_backward(acc, x2_ref[...], dres_ref[...], mod_ref, g_ref, acc_ref, 4)
        dx_ref[...] = dx
        dmo_ref[...] = (dx * mod_ref[2:3, :]).astype(BF16)
        _accumulate(acc_ref, 3, dx * mo_ref[...].astype(F32))

    stage, slots = exchange(to_chips=["w_down"])
    (dx2, dmo, sums_mlp), got = _matmul(dup, w_up_all, name="d_mlp_up", tm=ts, stage=stage, finish=norm_mlp_backward,
                                        extra=[(x2, rows_d), (dx3, rows_d), (mod, vec8), (g_mlp, vec1), (mo, rows_d)],
                                        outs=[out_rows(d, F32), out_rows(d, BF16), sums_out], zero_first=(2,), parts=halves_of_tile, **back_sum)
    collect(slots, got)
    sum_quarters("w_down")
    publish("w_up", wg(h2, dup, nb=N_CHIPS, name="grad_w_up"))
    publish("w_out", wg(merged, dmo, nb=1, name="grad_w_out"))

    def merge_backward(k, acc, extra_refs, out_refs):
        sa_ref, sb_ref, ta_ref, tb_ref = extra_refs
        out_refs[0][...] = (acc * sa_ref[...].astype(F32)).astype(BF16)
        out_refs[1][...] = (acc * sb_ref[...].astype(F32)).astype(BF16)
        out_refs[2][:, 0:d] = (acc * ta_ref[...].astype(F32)).astype(BF16)
        out_refs[2][:, d:2 * d] = (acc * tb_ref[...].astype(F32)).astype(BF16)

    stage, slots = exchange(to_sibling=["w_up", "w_out"], swap=["w_down"])
    (dbr_a, dbr_b, dgab), got = _matmul(dmo, w_out_all, name="d_mix_out", tm=ts, stage=stage, finish=merge_backward,
                                        extra=[(s_a, rows_d), (s_b, rows_d), (t_a, rows_d), (t_b, rows_d)],
                                        outs=[out_rows(d, BF16), out_rows(d, BF16), out_rows(2 * d, BF16)], parts=halves_of_tile, **back_sum)
    collect(slots, got)
    publish("w_a", wg(ga, dbr_a, nb=1, name="grad_w_branch_a"))
    publish("w_b", wg(pooled, dbr_b, nb=1, name="grad_w_branch_b"))
    dga, = _matmul(dbr_a, w_a_all, name="d_branch_a", tm=tw, outs=[out_rows(d, F32, tw)], finish=_store_blocks(d), **back_sum)
    dpooled, = _matmul(dbr_b, w_b_all, name="d_branch_b", tm=tw, outs=[out_rows(d, F32, tw)], finish=_store_blocks(d), **back_sum)
    sum_pairs("w_up", "w_out")
    stage, slots = exchange(to_sibling=["w_a", "w_b"], to_chips=["w_up", "w_out"])
    (dproj, sums_seq, d_rg_a, d_rg_x, d_pool), got = _seq_backward_by_block(proj, xr, hr, p, dga, dpooled, dgab, conv_full, vecs,
                                                                            w_rg_a_all, w_rg_x_all, w_pool_all, ts=tsq, stage=stage)
    collect(slots, got)
    sum_quarters("w_up", "w_out")
    sum_pairs("w_a", "w_b")
    for n, g in zip(block_weights, (d_rg_a, d_rg_x, d_pool)):
        publish(n, g)
    stage, slots = exchange(to_sibling=block_weights, to_chips=["w_a", "w_b"], swap=["w_up", "w_out"])
    g_in, got = wg(h1, dproj, nb=N_CHIPS, name="grad_w_in", stage=stage)
    collect(slots, got)
    publish("w_in", g_in)
    sum_pairs(*block_weights)
    sum_quarters("w_a", "w_b")
    stage, slots = exchange(to_sibling=["w_in"], to_chips=block_weights)
    collect(slots, _run_stage(stage, "exchange_w_in_gates"))
    sum_pairs("w_in")
    sum_quarters(*block_weights)

    def norm_mix_backward(k, acc, extra_refs, out_refs):
        x_ref, dres_ref, mod_ref, g_ref = extra_refs
        out_refs[0][...] = _modulated_norm_backward(acc, x_ref[...], dres_ref[...], mod_ref, g_ref, out_refs[1], 1)

    stage, slots = exchange(to_chips=["w_in"], swap=block_weights + ("w_a", "w_b"))
    (grad_x, sums_mix), got = _matmul(dproj, w_in_all, name="d_proj_in", tm=ts, stage=stage, finish=norm_mix_backward,
                                      extra=[(x0, rows_d), (dx2, rows_d), (mod, vec8), (g_mix, vec1)],
                                      outs=[out_rows(d, F32), sums_out], zero_first=(1,), parts=halves_of_tile, **back_sum)
    collect(slots, got)
    sum_quarters("w_in")

    dmod = jnp.concatenate([sums_mix[1:2], sums_mix[0:1], sums_mlp[3:4], sums_mlp[1:2], sums_mlp[0:1], sums_head[1:2]], axis=1)
    row = jnp.concatenate([sums_mix[2:3], sums_mlp[2:3], sums_seq[4:5], sums_seq[5:6], sums_seq[6:7], sums_seq[7:8], sums_seq[8:9],
                           sums_seq[9:10], sums_head[0:1], sums_seq[0:CONV_WIDTH].reshape(1, CONV_WIDTH * d), dmod, sums_head[2:3]], axis=1)
    n_vec, n_conv, n_mod = 9 * d, CONV_WIDTH * d, N_MOD * d
    stage, slots = exchange(swap=["w_in"])
    i_rows = _add_allgather8(stage, row)
    got = _run_stage(stage, "exchange_sums_w_in")
    collect(slots, got)
    rows_all = got[i_rows]
    total = _sum_rows(rows_all, d)
    loss = total[0, n_vec + n_conv + n_mod]
    dmod_all = rows_all[:, 0, n_vec + n_conv:n_vec + n_conv + n_mod]
    g_ada, d_ada, m_ada, v_ada = _ada_backward(act_all.T, lax.dynamic_slice_in_dim(dmod_all, k_me * ada_cols, ada_cols, axis=1),
                                               w_ada[0], m_w_ada[0], v_w_ada[0])
    g_conv = lax.dynamic_slice_in_dim(total[:, n_vec:n_vec + n_conv].reshape(CONV_WIDTH, d), k_me * conv_cols, conv_cols, axis=1)
    vec_names = ("norm_mix_g", "norm_mlp_g", "conv_b", "b_rg_a", "b_rg_x", "a_param", "b_pool", "pool_scale", "final_g", "conv_w", "b_ada")
    vec_params = [(norm_mix_g, m_norm_mix_g, v_norm_mix_g), (norm_mlp_g, m_norm_mlp_g, v_norm_mlp_g), (conv_b, m_conv_b, v_conv_b),
                  (b_rg_a, m_b_rg_a, v_b_rg_a), (b_rg_x, m_b_rg_x, v_b_rg_x), (a_param, m_a_param, v_a_param), (b_pool, m_b_pool, v_b_pool),
                  (pool_scale, m_pool_scale, v_pool_scale), tuple(a.reshape(1, d) for a in (final_g, m_final_g, v_final_g)),
                  (conv_w[0], m_conv_w[0], v_conv_w[0]), (b_ada, m_b_ada, v_b_ada)]
    offsets = [i * d for i in range(9)] + [None, n_vec + n_conv]
    vec_res = _adamw_vectors(total, offsets, [None] * 9 + [g_conv, None], vec_params)
    vec_out = {n: [r.reshape(shape) for r in res] for n, res, shape in zip(
        vec_names, vec_res, [(1, d)] * 8 + [(d,), (1, CONV_WIDTH, conv_cols), (1, N_MOD * d)])}

    big_out = {}
    for n in names:
        shape2 = (-1, mine[n].shape[-1])
        g2 = quarter[n].reshape(shape2)
        res = _adamw(mine[n].reshape(shape2), g2, moments_m[n].reshape(shape2), moments_v[n].reshape(shape2), "adamw_" + n)
        big_out[n] = [r.reshape((1,) + mine[n].shape) for r in res]

    key = {"w_in": "w_in", "w_rg_a": "rg_a", "w_rg_x": "rg_x", "w_branch_a": "w_a", "w_pool": "pool", "w_branch_b": "w_b", "w_out": "w_out",
           "w_up": "w_up", "w_down": "w_down"}
    order = ("norm_mix_g", "norm_mlp_g", "w_ada", "b_ada", "w_in", "conv_w", "conv_b", "w_rg_a", "b_rg_a", "w_rg_x", "b_rg_x", "a_param",
             "w_branch_a", "w_pool", "b_pool", "pool_scale", "w_branch_b", "w_out", "w_up", "w_down", "final_g")
    ada_out = [g_ada[None], d_ada[None], m_ada[None], v_ada[None]]
    outs = [loss, grad_x[None]]
    for kind in range(4):
        for nme in order:
            outs.append(ada_out[kind] if nme == "w_ada" else big_out[key[nme]][kind] if nme in key else vec_out[nme][kind])
    return tuple(outs)
```

```python
import functools

import jax
import jax.numpy as jnp
from jax import lax
from jax.experimental import pallas as pl
from jax.experimental.pallas import tpu as pltpu

F32, BF16 = jnp.float32, jnp.bfloat16
D_MODEL = 1024
D_FF = 4 * D_MODEL
N_BLOCKS = 4
BLOCK = D_MODEL // N_BLOCKS
CONV_WIDTH = 4
POOL_WINDOWS = (2, 4, 8, 16)
CONV_HALO = 8
POOL_HALO = 16
N_MOD = 6
EPS = 1e-6
C_RG = 8.0
ADAM_LR, ADAM_B1, ADAM_B2, ADAM_EPS, ADAM_WD, ADAM_STEP = 0.001, 0.9, 0.999, 1e-08, 0.01, 10
N_DEV = 8
N_CHIPS = 4
VMEM_LIMIT_V7X = 56 * 2**20
MESH = pl.DeviceIdType.MESH
SDS = jax.ShapeDtypeStruct
HBM_SPEC = pl.BlockSpec(memory_space=pltpu.HBM)
VMEM_SPEC = pl.BlockSpec(memory_space=pltpu.VMEM)


def _params(*semantics):
    return pltpu.CompilerParams(dimension_semantics=semantics, vmem_limit_bytes=VMEM_LIMIT_V7X)


def _coords():
    return lax.axis_index("x"), lax.axis_index("y"), lax.axis_index("c")


def _flip(v, bit):
    return 1 - v if bit else v


def _chips():
    x, y, _ = _coords()
    return [(1 - x, y), (x, 1 - y), (1 - x, 1 - y)]


PEER_SETS = [frozenset(s) for s in (("sibling",), ("chips",), ("sibling", "chips"), ("sibling", "chips", "others"))]


class _Stage:
    def __init__(self):
        self.inputs, self.in_specs, self.outputs, self.out_specs, self.aliases = [], [], [], [], {}
        self.parts, self.n_copies, self.peers = [], 0, set()

    def add(self, inputs, in_spec, outputs, out_spec, n_copies, build, peers, alias=False):
        self.peers |= set(peers)
        i0, o0 = len(self.inputs), len(self.outputs)
        self.inputs += inputs
        self.in_specs += [in_spec] * len(inputs)
        self.outputs += outputs
        self.out_specs += [out_spec] * len(outputs)
        if alias:
            self.aliases.update({i0 + i: o0 + i for i in range(len(inputs))})
        self.parts.append((build, i0, len(inputs), o0, len(outputs)))
        self.n_copies += n_copies
        return list(range(o0, o0 + len(outputs)))

    def copies(self, in_refs, out_refs):
        out = []
        for build, i0, ni, o0, no in self.parts:
            out += build(in_refs[i0:i0 + ni], out_refs[o0:o0 + no])
        assert len(out) == self.n_copies
        return out

    def run(self, in_refs, out_refs, send_sems, recv_sems, start):
        x, y, c = _coords()
        for s, (src, dst, to, landing) in enumerate(self.copies(in_refs, out_refs)):
            if to is None:
                cp = pltpu.make_async_copy(src, dst, send_sems.at[s])
                cp.start() if start else cp.wait()
                continue
            cp = pltpu.make_async_remote_copy(src_ref=src, dst_ref=dst, send_sem=send_sems.at[s], recv_sem=recv_sems.at[s],
                                              device_id=to, device_id_type=MESH)
            if start:
                cp.start()
            else:
                pltpu.make_async_remote_copy(src_ref=landing, dst_ref=landing, send_sem=send_sems.at[s], recv_sem=recv_sems.at[s],
                                             device_id=(x, y, c), device_id_type=MESH).wait_recv()
                cp.wait_send()


def _hosted(stage, body, *, name, in_specs, out_specs, out_shape, grid=(), scratch_shapes=(), compiler_params=None):
    if stage is None:
        return pl.pallas_call(body, name=name, grid=grid, in_specs=in_specs, out_specs=out_specs, out_shape=out_shape,
                              scratch_shapes=list(scratch_shapes), compiler_params=compiler_params)
    single = not isinstance(out_shape, (list, tuple))
    h_out_shape = [out_shape] if single else list(out_shape)
    h_out_specs = [out_specs] if single else list(out_specs)
    n_in, n_out, n_scr = len(in_specs), len(h_out_shape), len(scratch_shapes)
    s_in, s_out = len(stage.inputs), len(stage.outputs)

    def wrapped(*refs):
        h_in, st_in = refs[:n_in], refs[n_in:n_in + s_in]
        h_o, st_o = refs[n_in + s_in:n_in + s_in + n_out], refs[n_in + s_in + n_out:n_in + s_in + n_out + s_out]
        h_scr = refs[n_in + s_in + n_out + s_out:n_in + s_in + n_out + s_out + n_scr]
        send_sems, recv_sems = refs[n_in + s_in + n_out + s_out + n_scr:]
        def begin():
            x, y, c = _coords()
            peers = [(x, y, 1 - c)] * ("sibling" in stage.peers)
            peers += [(px, py, c) for px, py in _chips()] * ("chips" in stage.peers)
            peers += [(px, py, 1 - c) for px, py in _chips()] * ("others" in stage.peers)
            barrier = pltpu.get_barrier_semaphore()
            for peer in peers:
                pl.semaphore_signal(barrier, inc=1, device_id=peer, device_id_type=MESH)
            pl.semaphore_wait(barrier, len(peers))
            stage.run(st_in, st_o, send_sems, recv_sems, True)

        if not grid:
            begin()
            if body is not None:
                body(*h_in, *h_o, *h_scr)
            stage.run(st_in, st_o, send_sems, recv_sems, False)
            return
        ids = [pl.program_id(a) for a in range(len(grid))]
        first = functools.reduce(jnp.logical_and, [i == 0 for i in ids])
        last = functools.reduce(jnp.logical_and, [i == g - 1 for i, g in zip(ids, grid)])
        pl.when(first)(begin)
        body(*h_in, *h_o, *h_scr)
        pl.when(last)(lambda: stage.run(st_in, st_o, send_sems, recv_sems, False))

    call = pl.pallas_call(
        wrapped, name=name, grid=grid, in_specs=list(in_specs) + stage.in_specs, out_specs=h_out_specs + stage.out_specs,
        out_shape=h_out_shape + stage.outputs, input_output_aliases={n_in + i: n_out + o for i, o in stage.aliases.items()},
        scratch_shapes=list(scratch_shapes) + [pltpu.SemaphoreType.DMA((stage.n_copies,)), pltpu.SemaphoreType.DMA((stage.n_copies,))],
        compiler_params=pltpu.CompilerParams(dimension_semantics=("arbitrary",) * len(grid), vmem_limit_bytes=VMEM_LIMIT_V7X,
                                             collective_id=PEER_SETS.index(frozenset(stage.peers))),
    )

    def run(*operands):
        outs = call(*operands, *stage.inputs)
        host = outs[:n_out]
        return (host[0] if single else host), outs[n_out:]

    return run


def _run_stage(stage, name):
    return _hosted(stage, None, name=name, in_specs=[], out_specs=[], out_shape=[])()[1]


def _add_allgather8(stage, v):
    def build(ins, outs):
        x, y, c = _coords()
        me = 4 * x + 2 * y + c
        copies = [(ins[0], outs[0].at[me], None, None)]
        for r in range(1, N_DEV):
            peer = (_flip(x, r & 4), _flip(y, r & 2), _flip(c, r & 1))
            copies.append((ins[0], outs[0].at[me], peer, outs[0].at[me ^ r]))
        return copies

    return stage.add([v], VMEM_SPEC, [SDS((N_DEV,) + v.shape, v.dtype)], VMEM_SPEC, N_DEV, build, ("sibling", "chips", "others"))[0]


def _add_gather_ici(stage, shard, rows=None, whole=None):
    first, count = rows or (0, shard.shape[1])

    def build(ins, outs):
        x, y, c = _coords()
        k_me = 2 * x + y
        part = pl.ds(first, count)
        copies = [] if whole is not None else [(ins[0], outs[0].at[k_me], (x, y, 1 - c), outs[0].at[k_me])]
        for px, py in _chips():
            copies.append((ins[0].at[c, part], outs[0].at[k_me, c, part], (px, py, c), outs[0].at[2 * px + py, c, part]))
        return copies

    if whole is None:
        return stage.add([shard], HBM_SPEC, [SDS((N_CHIPS,) + shard.shape, shard.dtype)], HBM_SPEC, N_CHIPS, build, ("sibling", "chips"))[0]

    def build_into(ins, outs):
        return build(ins[1:], outs)

    i0 = len(stage.inputs)
    out = stage.add([whole, shard], HBM_SPEC, [SDS(whole.shape, whole.dtype)], HBM_SPEC, N_CHIPS - 1, build_into, ("chips",))[0]
    stage.aliases[i0] = out
    return out


def _add_gather_d2d(stage, whole):
    def build(ins, outs):
        x, y, c = _coords()
        return [(outs[0].at[2 * px + py, c], outs[0].at[2 * px + py, c], (x, y, 1 - c), outs[0].at[2 * px + py, 1 - c]) for px, py in _chips()]

    return stage.add([whole], HBM_SPEC, [SDS(whole.shape, whole.dtype)], HBM_SPEC, N_CHIPS - 1, build, ("sibling",), alias=True)[0]


def _add_reduce_d2d(stage, grads):
    def build(ins, outs):
        x, y, c = _coords()
        return [(ins[0].at[k, 1 - c], outs[0].at[k], (x, y, 1 - c), outs[0].at[k]) for k in range(N_CHIPS)]

    return stage.add([grads], HBM_SPEC, [SDS((N_CHIPS,) + grads.shape[2:], grads.dtype)], HBM_SPEC, N_CHIPS, build, ("sibling",))[0]


def _add_reduce_ici(stage, partial):
    def build(ins, outs):
        x, y, c = _coords()
        return [(ins[0].at[2 * px + py], outs[0].at[j], (px, py, c), outs[0].at[j]) for j, (px, py) in enumerate(_chips())]

    return stage.add([partial], HBM_SPEC, [SDS((N_CHIPS - 1,) + partial.shape[1:], partial.dtype)], HBM_SPEC, N_CHIPS - 1, build, ("chips",))[0]


def _add_swap_halves(stage, quarter):
    def build(ins, outs):
        x, y, c = _coords()
        return [(outs[0].at[c], outs[0].at[c], (x, y, 1 - c), outs[0].at[1 - c])]

    return stage.add([quarter], HBM_SPEC, [SDS(quarter.shape, quarter.dtype)], HBM_SPEC, 1, build, ("sibling",), alias=True)[0]


def _sum_pair(grads, from_sibling, place, name):
    _, _, rows, cols = grads.shape
    tr = min(rows, 512)

    def body(place_ref, g_ref, s_ref, o_ref):
        o_ref[0] = (g_ref[0, 0].astype(F32) + s_ref[0].astype(F32)).astype(BF16)

    spec = pl.BlockSpec((1, tr, cols), lambda k, i, place_ref: (k, i, 0))
    return pl.pallas_call(
        body, name=name, out_shape=SDS((N_CHIPS, rows, cols), BF16),
        grid_spec=pltpu.PrefetchScalarGridSpec(
            num_scalar_prefetch=1, grid=(N_CHIPS, rows // tr),
            in_specs=[pl.BlockSpec((1, 1, tr, cols), lambda k, i, place_ref: (k, place_ref[0], i, 0)), spec], out_specs=spec),
        compiler_params=_params("parallel", "parallel"),
    )(place, grads, from_sibling)


def _sum_quarter(grads, from_sibling, from_chips, place, name):
    _, _, rows, cols = grads.shape
    tr = min(rows, 512)

    def body(place_ref, g_ref, s_ref, c_ref, o_ref):
        acc = g_ref[0, 0].astype(F32) + s_ref[0].astype(F32)
        for j in range(N_CHIPS - 1):
            acc = acc + c_ref[j].astype(F32)
        o_ref[0] = acc

    return pl.pallas_call(
        body, name=name, out_shape=SDS((2, rows, cols), F32),
        grid_spec=pltpu.PrefetchScalarGridSpec(
            num_scalar_prefetch=1, grid=(rows // tr,),
            in_specs=[pl.BlockSpec((1, 1, tr, cols), lambda i, place_ref: (place_ref[1], place_ref[0], i, 0)),
                      pl.BlockSpec((1, tr, cols), lambda i, place_ref: (place_ref[1], i, 0)),
                      pl.BlockSpec((N_CHIPS - 1, tr, cols), lambda i, place_ref: (0, i, 0))],
            out_specs=pl.BlockSpec((1, tr, cols), lambda i, place_ref: (place_ref[0], i, 0))),
        compiler_params=_params("parallel"),
    )(place, grads, from_sibling, from_chips)


def _row_spec(ts, cols, col_block=0):
    return pl.BlockSpec((ts, cols), lambda *g: (g[0], col_block))


def _vec_spec(rows, cols):
    return pl.BlockSpec((rows, cols), lambda *g: (0, 0))


def _matmul(a, w, *, transposed, reduce_blocks, name, tm, outs, finish, extra=(), prepare=None, zero_first=(), parts=1, stage=None):
    m_rows = a.shape[0]
    nb, r, c = w.shape
    kb = c if transposed else r
    dims = (((1,), (1,)), ((), ())) if transposed else (((1,), (0,)), ((), ()))
    assert a.shape[1] == (nb * kb if reduce_blocks else kb) and m_rows % tm == 0 and tm % parts == 0
    n_extra, sub = len(extra), tm // parts

    def body(a_ref, w_ref, *rest):
        extra_refs, out_refs = rest[:n_extra], rest[n_extra:]
        if zero_first:
            @pl.when(pl.program_id(0) == 0)
            def _():
                for i in zero_first:
                    out_refs[i][...] = jnp.zeros_like(out_refs[i])

        def views(j):
            rows = pl.ds(j * sub, sub)
            return rows, [ref.at[rows] if ref.shape[0] == tm else ref for ref in extra_refs], [ref.at[rows] if ref.shape[0] == tm else ref for ref in out_refs]

        def product(j):
            rows, ex, ou = views(j)
            if reduce_blocks:
                acc = lax.dot_general(a_ref[rows, 0:kb], w_ref[0], dims, preferred_element_type=F32)
                for k in range(1, nb):
                    acc += lax.dot_general(a_ref[rows, k * kb:(k + 1) * kb], w_ref[k], dims, preferred_element_type=F32)
                return [(None, acc)]
            lhs = a_ref[rows, :] if prepare is None else prepare(a_ref.at[rows], ex, ou)
            return [(k, lax.dot_general(lhs, w_ref[k], dims, preferred_element_type=F32)) for k in range(nb)]

        pending = product(0)
        for j in range(parts):
            ahead = product(j + 1) if j + 1 < parts else None
            _, ex, ou = views(j)
            for k, acc in pending:
                finish(k, acc, ex, ou)
            pending = ahead

    return _hosted(
        stage, body, name=name, grid=(m_rows // tm,), out_shape=[s for s, _ in outs], out_specs=[s for _, s in outs],
        in_specs=[_row_spec(tm, a.shape[1]), pl.BlockSpec((nb, r, c), lambda *g: (0, 0, 0), pipeline_mode=pl.Buffered(1))] + [s for _, s in extra],
        compiler_params=_params("arbitrary"),
    )(a, w, *[e for e, _ in extra])


def _store_blocks(n):
    def finish(k, acc, extra_refs, out_refs):
        if k is None:
            out_refs[0][...] = acc.astype(out_refs[0].dtype)
        else:
            out_refs[0][:, k * n:(k + 1) * n] = acc.astype(out_refs[0].dtype)
    return finish


def _weight_grad(a, b, *, nb, name, tk, stage=None):
    s_rows, k1 = a.shape
    tn = b.shape[1] // nb
    assert k1 % tk == 0 and b.shape[1] % nb == 0

    def body(a_ref, b_ref, o_ref):
        o_ref[0] = lax.dot_general(a_ref[...], b_ref[...], (((0,), (0,)), ((), ())), preferred_element_type=F32).astype(o_ref.dtype)

    return _hosted(
        stage, body, name=name, grid=(nb, k1 // tk), out_shape=SDS((nb, k1, tn), BF16),
        in_specs=[pl.BlockSpec((s_rows, tk), lambda n, i: (0, i)), pl.BlockSpec((s_rows, tn), lambda n, i: (0, n))],
        out_specs=pl.BlockSpec((1, tk, tn), lambda n, i: (n, i, 0)), compiler_params=_params("parallel", "parallel"),
    )(a, b)


def _rms(x):
    r = lax.rsqrt(jnp.mean(x * x, axis=-1, keepdims=True) + EPS)
    return x * r, r


def _accumulate(acc_ref, row, value):
    acc_ref[row:row + 1, :] += jnp.sum(value, axis=0, keepdims=True)


def _gelu_parts(y):
    k0, k1 = 0.7978845608028654, 0.044715
    th = jnp.tanh(k0 * (y + k1 * (y * y * y)))
    gelu = 0.5 * y * (1.0 + th)
    dgelu = 0.5 * (1.0 + th) + 0.5 * y * (1.0 - th * th) * (k0 * (1.0 + 3.0 * k1 * (y * y)))
    return gelu, dgelu


def _sigmoid(z):
    return 0.5 * jnp.tanh(0.5 * z) + 0.5


def _one_minus_square(a, log_a):
    return -jnp.tanh(log_a) * (1.0 + a * a)


def _modulated_norm(x, mod_ref, g_ref, sc_row, sh_row):
    xh, _ = _rms(x)
    return ((xh * g_ref[...]) * (1.0 + mod_ref[sc_row:sc_row + 1, :]) + mod_ref[sh_row:sh_row + 1, :]).astype(BF16)


def _modulated_norm_backward(dh, x, dres, mod_ref, g_ref, acc_ref, sc_row):
    xh, r = _rms(x)
    _accumulate(acc_ref, 0, dh * (xh * g_ref[...]))
    _accumulate(acc_ref, 1, dh)
    dn = dh * (1.0 + mod_ref[sc_row:sc_row + 1, :])
    _accumulate(acc_ref, 2, dn * xh)
    dxh = dn * g_ref[...]
    return dres + r * (dxh - xh * jnp.mean(dxh * xh, axis=-1, keepdims=True))


def _seq_forward(proj, conv_w, vecs, w_a, w_x, w_p, *, ts, stage=None):
    s_rows = proj.shape[0]
    d = D_MODEL

    def body(x_ref, xh_ref, y_ref, u_ref, uh_ref, cw_ref, vec_ref, wa_ref, wx_ref, wp_ref,
             xr_o, hr_o, ga_o, p_o, pooled_o, carry, a_scr, b_scr):
        t = pl.program_id(0)

        @pl.when(t == 0)
        def _():
            carry[...] = jnp.zeros_like(carry)

        rows = lax.broadcasted_iota(jnp.int32, (ts, BLOCK), 0)
        start = (rows == 0) & (t == 0)
        for g in range(N_BLOCKS):
            lanes = slice(g * BLOCK, (g + 1) * BLOCK)
            x = x_ref[:, lanes]
            xx = jnp.concatenate([jnp.where(t == 0, 0.0, xh_ref[:, lanes]), x], axis=0)
            xr = vec_ref[0:1, lanes] + x * cw_ref[CONV_WIDTH - 1:CONV_WIDTH, lanes]
            for j in range(CONV_WIDTH - 1):
                xr = xr + pltpu.roll(xx, CONV_WIDTH - 1 - j, 0)[CONV_HALO:] * cw_ref[j:j + 1, lanes]
            xr_o[:, lanes] = xr
            xrb = xr.astype(BF16)
            r = _sigmoid(jnp.dot(xrb, wa_ref[g], preferred_element_type=F32) + vec_ref[1:2, lanes])
            gate_i = _sigmoid(jnp.dot(xrb, wx_ref[g], preferred_element_type=F32) + vec_ref[2:3, lanes])
            log_a = (-C_RG * r) * jax.nn.softplus(vec_ref[3:4, lanes])
            a = jnp.exp(log_a)
            a_scr[:, lanes] = a
            b_scr[:, lanes] = xr * gate_i * jnp.where(start, 1.0, jnp.sqrt(_one_minus_square(a, log_a)))

        sub = lax.broadcasted_iota(jnp.int32, (8, d), 0)

        def chunk(i, h):
            a = a_scr[pl.ds(i * 8, 8), :]
            b = b_scr[pl.ds(i * 8, 8), :]
            for s in (1, 2, 4):
                keep = sub >= s
                b = jnp.where(keep, a * pltpu.roll(b, s, 0) + b, b)
                a = jnp.where(keep, a * pltpu.roll(a, s, 0), a)
            hh = a * h + b
            hr_o[pl.ds(i * 8, 8), :] = hh
            return hh[7:8, :]

        carry[...] = lax.fori_loop(0, ts // 8, chunk, carry[...])
        pos = (rows + t * ts + 1).astype(F32)
        for g, win in enumerate(POOL_WINDOWS):
            lanes = slice(g * BLOCK, (g + 1) * BLOCK)
            gelu, _ = _gelu_parts(y_ref[:, lanes])
            ga_o[:, lanes] = (gelu * hr_o[:, lanes]).astype(BF16)
            u = u_ref[:, lanes]
            sums, have = jnp.concatenate([jnp.where(t == 0, 0.0, uh_ref[:, lanes]), u], axis=0), 1
            while have < win:
                sums = sums + pltpu.roll(sums, have, 0)
                have *= 2
            pb = (sums[POOL_HALO:] * (1.0 / jnp.minimum(pos, float(win))) - u).astype(BF16)
            p_o[:, lanes] = pb
            mixed = jnp.dot(pb, wp_ref[g], preferred_element_type=F32) + vec_ref[4:5, lanes]
            pooled_o[:, lanes] = (mixed * vec_ref[5:6, lanes]).astype(BF16)

    halo_x = pl.BlockSpec((CONV_HALO, d), lambda t: (jnp.maximum(t * (ts // CONV_HALO) - 1, 0), 0))
    halo_u = pl.BlockSpec((POOL_HALO, d), lambda t: (jnp.maximum(t * (ts // POOL_HALO) - 1, 0), 2))
    wspec = pl.BlockSpec((N_BLOCKS, BLOCK, BLOCK), lambda t: (0, 0, 0))
    return _hosted(
        stage, body, name="seq_forward", grid=(s_rows // ts,),
        out_shape=[SDS((s_rows, d), F32), SDS((s_rows, d), F32), SDS((s_rows, d), BF16), SDS((s_rows, d), BF16), SDS((s_rows, d), BF16)],
        in_specs=[_row_spec(ts, d, 0), halo_x, _row_spec(ts, d, 1), _row_spec(ts, d, 2), halo_u, _vec_spec(CONV_WIDTH, d), _vec_spec(8, d),
                  wspec, wspec, wspec],
        out_specs=[_row_spec(ts, d)] * 5,
        scratch_shapes=[pltpu.VMEM((1, d), F32), pltpu.VMEM((ts, d), F32), pltpu.VMEM((ts, d), F32)],
        compiler_params=_params("arbitrary"),
    )(proj, proj, proj, proj, proj, conv_w, vecs, w_a, w_x, w_p)


def _branches(ga, pooled, proj, w_a, w_b, *, ts, stage=None):
    s_rows, d = ga.shape

    def body(a_ref, p_ref, ga_ref, gb_ref, wa_ref, wb_ref, merged_o, sa_o, sb_o, ta_o, tb_o):
        bra = jnp.dot(a_ref[...], wa_ref[0], preferred_element_type=F32)
        brb = jnp.dot(p_ref[...], wb_ref[0], preferred_element_type=F32)
        sa, sb = _sigmoid(ga_ref[...]), _sigmoid(gb_ref[...])
        merged_o[...] = (sa * bra + sb * brb).astype(BF16)
        sa_o[...] = sa.astype(BF16)
        sb_o[...] = sb.astype(BF16)
        ta_o[...] = (bra * (sa * (1.0 - sa))).astype(BF16)
        tb_o[...] = (brb * (sb * (1.0 - sb))).astype(BF16)

    rows = _row_spec(ts, d)
    wspec = pl.BlockSpec((1, d, d), lambda *g: (0, 0, 0), pipeline_mode=pl.Buffered(1))
    return _hosted(
        stage, body, name="branches", grid=(s_rows // ts,), out_shape=[SDS((s_rows, d), BF16)] * 5,
        in_specs=[rows, rows, _row_spec(ts, d, 3), _row_spec(ts, d, 4), wspec, wspec], out_specs=[rows] * 5,
        compiler_params=_params("arbitrary"),
    )(ga, pooled, proj, proj, w_a, w_b)


def _seq_backward_by_block(proj, xr, hr, p, dga, dpooled, dgab, conv_w, vecs, w_a, w_x, w_p, *, ts, stage=None):
    s_rows = proj.shape[0]
    d = D_MODEL
    n_t = s_rows // ts
    nn, nt, tn = (((1,), (0,)), ((), ())), (((1,), (1,)), ((), ())), (((0,), (0,)), ((), ()))

    def dot(lhs, rhs, dims):
        return lax.dot_general(lhs, rhs, dims, preferred_element_type=F32)

    def body(x_ref, y_ref, xr_ref, hr_ref, hh_ref, p_ref, dga_ref, dpl_ref, dgab_ref, cw_ref, vec_ref, wa_ref, wx_ref, wp_ref,
             dproj_o, acc_o, dwa_o, dwx_o, dwp_o, g_carry, dxr_carry, q_carry, a_scr, b_scr, g_scr, a_keep, r_keep, i_keep, m_keep):
        step = pl.program_id(0)
        t = n_t - 1 - step

        @pl.when(step == 0)
        def _():
            for ref in (acc_o, dwa_o, dwx_o, dwp_o, g_carry, dxr_carry, q_carry):
                ref[...] = jnp.zeros_like(ref)

        rows = lax.broadcasted_iota(jnp.int32, (ts, BLOCK), 0)
        start = (rows == 0) & (t == 0)

        def add_sum(row, lanes, value):
            acc_o[row:row + 1, lanes] += jnp.sum(value, axis=0, keepdims=True)

        for g in range(N_BLOCKS):
            lanes = slice(g * BLOCK, (g + 1) * BLOCK)
            xrb = xr_ref[:, lanes].astype(BF16)
            r = _sigmoid(dot(xrb, wa_ref[g], nn) + vec_ref[1:2, lanes])
            gate_i = _sigmoid(dot(xrb, wx_ref[g], nn) + vec_ref[2:3, lanes])
            log_a = (-C_RG * r) * jax.nn.softplus(vec_ref[3:4, lanes])
            a = jnp.exp(log_a)
            a_keep[:, lanes], r_keep[:, lanes], i_keep[:, lanes], m_keep[:, lanes] = a, r, gate_i, _one_minus_square(a, log_a)
            gelu, dgelu = _gelu_parts(y_ref[:, lanes])
            dga_t = dga_ref[:, lanes]
            dproj_o[:, d + g * BLOCK:d + (g + 1) * BLOCK] = (dga_t * hr_ref[:, lanes] * dgelu).astype(BF16)
            a_scr[:, lanes] = jnp.where(rows == ts - 1, 1.0, pltpu.roll(a, ts - 1, 0))
            b_scr[:, lanes] = dga_t * gelu

        sub = lax.broadcasted_iota(jnp.int32, (8, d), 0)

        def chunk(i, g_next):
            at = pl.multiple_of((ts // 8 - 1 - i) * 8, 8)
            aa = a_scr[pl.ds(at, 8), :]
            bb = b_scr[pl.ds(at, 8), :]
            for s in (1, 2, 4):
                keep = sub < 8 - s
                bb = jnp.where(keep, bb + aa * pltpu.roll(bb, 8 - s, 0), bb)
                aa = jnp.where(keep, aa * pltpu.roll(aa, 8 - s, 0), aa)
            gg = aa * g_next + bb
            g_scr[pl.ds(at, 8), :] = gg
            return gg[0:1, :]

        g_first = lax.fori_loop(0, ts // 8, chunk, g_carry[...])
        g_carry[...] = a_keep[0:1, :] * g_first

        pos = (rows + t * ts + 1).astype(F32)
        for g, win in enumerate(POOL_WINDOWS):
            lanes = slice(g * BLOCK, (g + 1) * BLOCK)
            gs, xr, a, r, gate_i, m_square = g_scr[:, lanes], xr_ref[:, lanes], a_keep[:, lanes], r_keep[:, lanes], i_keep[:, lanes], m_keep[:, lanes]
            sp = jax.nn.softplus(vec_ref[3:4, lanes])
            mult = jnp.where(start, 1.0, jnp.sqrt(m_square))
            h_before = jnp.where(t == 0, 0.0, hh_ref[CONV_HALO - 1:CONV_HALO, lanes])
            h_prev = jnp.where(rows == 0, h_before, pltpu.roll(hr_ref[:, lanes], 1, 0))
            gm = gs * mult
            dxr = gm * gate_i
            d_i = gm * xr
            d_mult = (gs * xr) * gate_i
            dlog_a = (gs * h_prev) * a - jnp.where(start, 0.0, d_mult * (a * a) * lax.rsqrt(m_square))
            dzr = (dlog_a * (-C_RG * sp)) * (r * (1.0 - r))
            dzi = d_i * (gate_i * (1.0 - gate_i))
            add_sum(5, lanes, dzr)
            add_sum(6, lanes, dzi)
            acc_o[7:8, lanes] += jnp.sum(dlog_a * r, axis=0, keepdims=True) * (-C_RG * jax.nn.sigmoid(vec_ref[3:4, lanes]))
            xrb, dzrb, dzib = xr.astype(BF16), dzr.astype(BF16), dzi.astype(BF16)
            dwa_o[g] += dot(xrb, dzrb, tn)
            dwx_o[g] += dot(xrb, dzib, tn)
            dxr = dxr + dot(dzrb, wa_ref[g], nt) + dot(dzib, wx_ref[g], nt)

            x = x_ref[:, lanes]
            ext = jnp.concatenate([dxr, dxr_carry[:, lanes]], axis=0)
            dx = dxr * cw_ref[CONV_WIDTH - 1:CONV_WIDTH, lanes]
            add_sum(CONV_WIDTH - 1, lanes, x * dxr)
            for j in range(CONV_WIDTH - 1):
                ahead = pltpu.roll(ext, ts + CONV_HALO - (CONV_WIDTH - 1 - j), 0)[:ts]
                dx = dx + ahead * cw_ref[j:j + 1, lanes]
                add_sum(j, lanes, x * ahead)
            add_sum(4, lanes, dxr)
            dxr_carry[:, lanes] = dxr[0:CONV_HALO, :]
            dproj_o[:, lanes] = dx.astype(BF16)

            pb = p_ref[:, lanes]
            mixed = dot(pb, wp_ref[g], nn) + vec_ref[4:5, lanes]
            dpl = dpl_ref[:, lanes]
            add_sum(9, lanes, dpl * mixed)
            dmixed = dpl * vec_ref[5:6, lanes]
            add_sum(8, lanes, dmixed)
            dmb = dmixed.astype(BF16)
            dwp_o[g] += dot(pb, dmb, tn)
            dp = dot(dmb, wp_ref[g], nt)
            q = dp * (1.0 / jnp.minimum(pos, float(win)))
            sums, have = jnp.concatenate([q, q_carry[:, lanes]], axis=0), 1
            while have < win:
                sums = sums + pltpu.roll(sums, ts + POOL_HALO - have, 0)
                have *= 2
            q_carry[:, lanes] = q[0:POOL_HALO, :]
            dproj_o[:, 2 * d + g * BLOCK:2 * d + (g + 1) * BLOCK] = (sums[:ts] - dp).astype(BF16)

        dproj_o[:, 3 * d:5 * d] = dgab_ref[...]

    def rev(cols, col_block=0):
        return pl.BlockSpec((ts, cols), lambda i: (n_t - 1 - i, col_block))

    halo_h = pl.BlockSpec((CONV_HALO, d), lambda i: (jnp.maximum((n_t - 1 - i) * (ts // CONV_HALO) - 1, 0), 0))
    wspec = pl.BlockSpec((N_BLOCKS, BLOCK, BLOCK), lambda i: (0, 0, 0))
    return _hosted(
        stage, body, name="seq_backward", grid=(n_t,),
        out_shape=[SDS((s_rows, 5 * d), BF16), SDS((16, d), F32)] + [SDS((N_BLOCKS, BLOCK, BLOCK), F32)] * 3,
        in_specs=[rev(d, 0), rev(d, 1), rev(d), rev(d), halo_h, rev(d), rev(d), rev(d), rev(2 * d), _vec_spec(CONV_WIDTH, d), _vec_spec(8, d),
                  wspec, wspec, wspec],
        out_specs=[rev(5 * d), _vec_spec(16, d), wspec, wspec, wspec],
        scratch_shapes=[pltpu.VMEM((1, d), F32), pltpu.VMEM((CONV_HALO, d), F32), pltpu.VMEM((POOL_HALO, d), F32)] + [pltpu.VMEM((ts, d), F32)] * 7,
        compiler_params=_params("arbitrary"),
    )(proj, proj, xr, hr, hr, p, dga, dpooled, dgab, conv_w, vecs, w_a, w_x, w_p)


def _adamw_math(w, g, m, v):
    m = ADAM_B1 * m + (1.0 - ADAM_B1) * g
    v = ADAM_B2 * v + (1.0 - ADAM_B2) * jnp.square(g)
    m_hat = m / (1.0 - ADAM_B1 ** ADAM_STEP)
    v_hat = v / (1.0 - ADAM_B2 ** ADAM_STEP)
    delta = -ADAM_LR * (m_hat / (jnp.sqrt(v_hat) + ADAM_EPS) + ADAM_WD * w)
    return delta, m, v


def _adamw(w, g, m, v, name):
    rows, cols = w.shape
    tr = min(rows, 512)

    def body(w_ref, g_ref, m_ref, v_ref, g_out, d_ref, mo_ref, vo_ref):
        g = g_ref[...]
        g_out[...] = g
        d_ref[...], mo_ref[...], vo_ref[...] = _adamw_math(w_ref[...], g, m_ref[...], v_ref[...])

    spec = pl.BlockSpec((tr, cols), lambda i: (i, 0))
    return pl.pallas_call(
        body, name=name, grid=(rows // tr,), out_shape=[SDS((rows, cols), F32)] * 4, in_specs=[spec] * 4, out_specs=[spec] * 4,
        compiler_params=_params("parallel"),
    )(w, g, m, v)


def _prologue(c, conv_shard, w_ada, b_cols, w_in_halves, gate_halves):
    n_cols = w_ada.shape[1]
    n_g = len(gate_halves)
    half_rows = w_in_halves.shape[1] // 2
    n_sems = 3 * N_DEV + N_CHIPS * n_g + 5 + 2 + 3

    def body(c_ref, conv_ref, w_ref, b_ref, win_ref, *rest):
        gate_refs = rest[:n_g]
        act_o, mod_o, conv_o, win_o = rest[n_g:n_g + 4]
        gate_o = rest[n_g + 4:2 * n_g + 4]
        c_all, mod_cols, send_sems, recv_sems = rest[2 * n_g + 4:]
        x, y, c = _coords()
        me, k_me = 4 * x + 2 * y + c, 2 * x + y
        sibling, x_nbr, y_nbr = (x, y, 1 - c), (1 - x, y, c), (x, 1 - y, c)
        k_x, k_y, k_d = 2 * (1 - x) + y, 2 * x + (1 - y), 2 * (1 - x) + (1 - y)
        counter = iter(range(n_sems))

        def copy(src, dst, to, landing):
            s = next(counter)
            out = pltpu.make_async_remote_copy(src_ref=src, dst_ref=dst, send_sem=send_sems.at[s], recv_sem=recv_sems.at[s],
                                               device_id=to, device_id_type=MESH)
            arrival = pltpu.make_async_remote_copy(src_ref=landing, dst_ref=landing, send_sem=send_sems.at[s], recv_sem=recv_sems.at[s],
                                                   device_id=(x, y, c), device_id_type=MESH)
            return out, arrival

        def allgather(src, out):
            s = next(counter)
            own = pltpu.make_async_copy(src, out.at[me], send_sems.at[s])
            peers = [copy(src, out.at[me], (_flip(x, r & 4), _flip(y, r & 2), _flip(c, r & 1)), out.at[me ^ r]) for r in range(1, N_DEV)]
            return own, peers

        def start(group):
            own, peers = group
            own.start()
            for out, _ in peers:
                out.start()

        def finish(group):
            own, peers = group
            for out, arrival in peers:
                arrival.wait_recv()
                out.wait_send()
            own.wait()

        rows_a, rows_b = pl.ds(0, half_rows), pl.ds(half_rows, half_rows)
        c_group, conv_group = allgather(c_ref, c_all), allgather(conv_ref, conv_o)
        gates = []
        for g_ref, g_out in zip(gate_refs, gate_o):
            gates.append(copy(g_ref, g_out.at[k_me], sibling, g_out.at[k_me]))
            gates += [copy(g_ref.at[c], g_out.at[k_me, c], (px, py, c), g_out.at[2 * px + py, c]) for px, py in _chips()]
        own_in = copy(win_ref, win_o.at[k_me], sibling, win_o.at[k_me])
        to_x = [copy(win_ref.at[c, rows], win_o.at[k_me, c, rows], x_nbr, win_o.at[k_x, c, rows]) for rows in (rows_a, rows_b)]
        to_y = [copy(win_ref.at[c, rows], win_o.at[k_me, c, rows], y_nbr, win_o.at[k_y, c, rows]) for rows in (rows_b, rows_a)]
        mod_group = allgather(mod_cols, mod_o)
        relay_y = copy(win_o.at[k_x, c, rows_a], win_o.at[k_x, c, rows_a], y_nbr, win_o.at[k_d, c, rows_a])
        relay_x = copy(win_o.at[k_y, c, rows_b], win_o.at[k_y, c, rows_b], x_nbr, win_o.at[k_d, c, rows_b])
        sib_x, sib_y, sib_d = (copy(win_o.at[k, c], win_o.at[k, c], sibling, win_o.at[k, 1 - c]) for k in (k_x, k_y, k_d))

        start(c_group)
        start(conv_group)
        for out, _ in gates + [own_in] + to_x + to_y:
            out.start()
        finish(c_group)
        cv = jnp.concatenate([c_all[r] for r in range(N_DEV)], axis=0)
        act = cv * jax.nn.sigmoid(cv)
        act_o[...] = act
        mod_cols[...] = jnp.dot(act.astype(BF16), w_ref[...].astype(BF16), preferred_element_type=F32) + b_ref[...]
        start(mod_group)
        to_x[0][1].wait_recv()
        relay_y[0].start()
        to_y[0][1].wait_recv()
        relay_x[0].start()
        finish(mod_group)
        finish(conv_group)
        to_x[1][1].wait_recv()
        sib_x[0].start()
        to_y[1][1].wait_recv()
        sib_y[0].start()
        for out, arrival in gates + [own_in]:
            arrival.wait_recv()
            out.wait_send()
        for out, arrival in (relay_y, relay_x):
            arrival.wait_recv()
            out.wait_send()
        sib_d[0].start()
        for out, _ in to_x + to_y:
            out.wait_send()
        for out, arrival in (sib_x, sib_y, sib_d):
            arrival.wait_recv()
            out.wait_send()

    gate_shapes = [SDS((N_CHIPS,) + g.shape, g.dtype) for g in gate_halves]
    return pl.pallas_call(
        body, name="prologue",
        out_shape=[SDS((N_DEV, c.shape[1]), F32), SDS((N_DEV, N_DEV, n_cols), F32), SDS((N_DEV,) + conv_shard.shape, F32),
                   SDS((N_CHIPS,) + w_in_halves.shape, w_in_halves.dtype)] + gate_shapes,
        in_specs=[VMEM_SPEC] * 4 + [HBM_SPEC] * (1 + n_g), out_specs=[VMEM_SPEC] * 3 + [HBM_SPEC] * (1 + n_g),
        scratch_shapes=[pltpu.VMEM((N_DEV,) + c.shape, F32), pltpu.VMEM((N_DEV, n_cols), F32),
                        pltpu.SemaphoreType.DMA((n_sems,)), pltpu.SemaphoreType.DMA((n_sems,))],
        compiler_params=pltpu.CompilerParams(vmem_limit_bytes=VMEM_LIMIT_V7X),
    )(c, conv_shard, w_ada, b_cols, w_in_halves, *gate_halves)


def _ada_backward(act_t, dmod_cols, w, m, v):
    rows, cols = w.shape
    tn = 512

    def body(a_ref, dm_ref, w_ref, m_ref, v_ref, g_ref, d_ref, mo_ref, vo_ref):
        g = jnp.dot(a_ref[...].astype(BF16), dm_ref[...].astype(BF16), preferred_element_type=F32)
        g_ref[...] = g
        d_ref[...], mo_ref[...], vo_ref[...] = _adamw_math(w_ref[...], g, m_ref[...], v_ref[...])

    spec = pl.BlockSpec((rows, tn), lambda j: (0, j))
    return pl.pallas_call(
        body, name="ada_backward", grid=(cols // tn,), out_shape=[SDS((rows, cols), F32)] * 4,
        in_specs=[pl.BlockSpec((rows, N_DEV), lambda j: (0, 0)), pl.BlockSpec((N_DEV, tn), lambda j: (0, j)), spec, spec, spec],
        out_specs=[spec] * 4, compiler_params=_params("parallel"),
    )(act_t, dmod_cols, w, m, v)


def _sum_rows(gathered, n_loss):
    n = gathered.shape[2]

    def body(g_ref, o_ref):
        acc = g_ref[0]
        for r in range(1, N_DEV):
            acc = acc + g_ref[r]
        o_ref[...] = acc
        o_ref[:, n - n_loss:n] = jnp.broadcast_to(jnp.sum(acc[:, n - n_loss:n], axis=1, keepdims=True), (1, n_loss))

    return pl.pallas_call(body, name="sum_rows", out_shape=SDS((1, n), F32), in_specs=[VMEM_SPEC], out_specs=VMEM_SPEC)(gathered)


def _adamw_vectors(total, offsets, separate, params):
    n_p = len(params)
    apart = [g for g in separate if g is not None]

    def body(*refs):
        total_ref, apart_refs = refs[0], list(refs[1:1 + len(apart)])
        ins, outs = refs[1 + len(apart):1 + len(apart) + 3 * n_p], refs[1 + len(apart) + 3 * n_p:]
        for i in range(n_p):
            w_ref, m_ref, v_ref = ins[3 * i:3 * i + 3]
            g = apart_refs.pop(0)[...] if offsets[i] is None else total_ref[:, offsets[i]:offsets[i] + w_ref.shape[1]]
            outs[4 * i][...] = g
            outs[4 * i + 1][...], outs[4 * i + 2][...], outs[4 * i + 3][...] = _adamw_math(w_ref[...], g, m_ref[...], v_ref[...])

    flat = [a for p in params for a in p]
    res = pl.pallas_call(
        body, name="adamw_vectors", out_shape=[SDS(p[0].shape, F32) for p in params for _ in range(4)],
        in_specs=[VMEM_SPEC] * (1 + len(apart) + len(flat)), out_specs=[VMEM_SPEC] * (4 * n_p),
    )(total, *apart, *flat)
    return [tuple(res[4 * i:4 * i + 4]) for i in range(n_p)]


def _blocks_to_pieces(w):
    nb, rows, n = w.shape
    q = rows // N_CHIPS
    return w.reshape(nb, N_CHIPS, q, n).transpose(1, 0, 2, 3).reshape(N_CHIPS, 2, nb // 2, q, n)


def _pieces_to_blocks(w):
    n_chips, _, half, q, n = w.shape
    return w.reshape(n_chips, 2 * half, q, n).transpose(1, 0, 2, 3).reshape(2 * half, n_chips * q, n)


def kernel(x, c, norm_mix_g, norm_mlp_g, w_ada, b_ada, w_in, conv_w, conv_b, w_rg_a, b_rg_a, w_rg_x, b_rg_x, a_param, w_branch_a, w_pool, b_pool, pool_scale, w_branch_b, w_out, w_up, w_down, final_g, loss_target, m_norm_mix_g, m_norm_mlp_g, m_w_ada, m_b_ada, m_w_in, m_conv_w, m_conv_b, m_w_rg_a, m_b_rg_a, m_w_rg_x, m_b_rg_x, m_a_param, m_w_branch_a, m_w_pool, m_b_pool, m_pool_scale, m_w_branch_b, m_w_out, m_w_up, m_w_down, m_final_g, v_norm_mix_g, v_norm_mlp_g, v_w_ada, v_b_ada, v_w_in, v_conv_w, v_conv_b, v_w_rg_a, v_b_rg_a, v_w_rg_x, v_b_rg_x, v_a_param, v_w_branch_a, v_w_pool, v_b_pool, v_pool_scale, v_w_branch_b, v_w_out, v_w_up, v_w_down, v_final_g):
    d = D_MODEL
    s_rows = x.shape[1]
    ts, tsq = min(512, s_rows), min(256, s_rows)
    xi, yi, ci = _coords()
    k_me, me = 2 * xi + yi, 4 * xi + 2 * yi + ci
    ada_cols = w_ada.shape[2]
    conv_cols = conv_w.shape[2]
    n_in, n_up = w_in.shape[2], w_up.shape[2]

    names = ("w_in", "w_up", "w_down", "w_a", "w_b", "w_out", "rg_a", "rg_x", "pool")
    mine = dict(zip(names, (w_in[0], w_up[0], w_down[0], w_branch_a[0], w_branch_b[0], w_out[0], w_rg_a[0], w_rg_x[0], w_pool[0])))
    moments_m = dict(zip(names, (m_w_in[0], m_w_up[0], m_w_down[0], m_w_branch_a[0], m_w_branch_b[0], m_w_out[0], m_w_rg_a[0], m_w_rg_x[0], m_w_pool[0])))
    moments_v = dict(zip(names, (v_w_in[0], v_w_up[0], v_w_down[0], v_w_branch_a[0], v_w_branch_b[0], v_w_out[0], v_w_rg_a[0], v_w_rg_x[0], v_w_pool[0])))
    block_weights, squares = ("rg_a", "rg_x", "pool"), ("w_a", "w_b", "w_out")
    place = jnp.stack([ci, k_me]).astype(jnp.int32)
    g_mix, g_mlp, g_fin = norm_mix_g, norm_mlp_g, final_g.reshape(1, d)
    x0, target = x[0], loss_target[0]
    wg = functools.partial(_weight_grad, tk=min(1024, d))
    plain, back, back_sum = (dict(transposed=t, reduce_blocks=r) for t, r in ((False, False), (True, False), (True, True)))
    rows_d, vec8, vec1 = _row_spec(ts, d), _vec_spec(8, d), _vec_spec(1, d)
    tw = min(1024, s_rows)
    halves_of_tile = 2

    def out_rows(cols, dtype, t=ts):
        return SDS((s_rows, cols), dtype), _row_spec(t, cols)

    sums_out = (SDS((8, d), F32), vec8)

    def halves(n):
        return mine[n].astype(BF16).reshape(2, -1, mine[n].shape[-1])

    def blocks(w):
        return _pieces_to_blocks(w.reshape(N_CHIPS, 2, N_BLOCKS // 2, BLOCK // N_CHIPS, BLOCK))

    act_all, mod_all, conv_all, w_in_whole, *gate_wholes = _prologue(
        c, conv_w[0], w_ada[0], lax.dynamic_slice_in_dim(b_ada, k_me * ada_cols, ada_cols, axis=1), halves("w_in"), [halves(n) for n in block_weights])
    conv_full = conv_all[0::2].transpose(1, 0, 2).reshape(CONV_WIDTH, d)
    mod_mine = lax.dynamic_index_in_dim(mod_all, me, axis=1, keepdims=False)[0::2]
    mod = jnp.concatenate([mod_mine.reshape(N_MOD, d), jnp.zeros((8 - N_MOD, d), F32)], axis=0)
    w_in_all = w_in_whole.reshape(N_CHIPS, d, n_in)
    vecs = jnp.concatenate([conv_b, b_rg_a, b_rg_x, a_param, b_pool, pool_scale, jnp.zeros((2, d), F32)], axis=0)

    def norm_first(a_ref, extra_refs, out_refs):
        h = _modulated_norm(a_ref[...], extra_refs[0], extra_refs[1], 1, 0)
        out_refs[1][...] = h
        return h

    stage = _Stage()
    i_blk = [_add_gather_d2d(stage, g) for g in gate_wholes]
    i_sq = [_add_gather_ici(stage, halves(n)) for n in squares]
    down_rows = mine["w_down"].shape[0] // 8
    i_down = _add_gather_ici(stage, halves("w_down"), rows=(0, down_rows))
    (proj, h1), got = _matmul(x0, w_in_all, name="proj_in", tm=ts, extra=[(mod, vec8), (g_mix, vec1)], prepare=norm_first,
                              outs=[out_rows(N_CHIPS * n_in, F32), out_rows(d, BF16)], finish=_store_blocks(n_in), parts=halves_of_tile,                              stage=stage, **plain)
    w_rg_a_all, w_rg_x_all, w_pool_all = (blocks(got[i]) for i in i_blk)
    stage = _Stage()
    i_sq = [_add_gather_d2d(stage, got[i]) for i in i_sq]
    i_up = _add_gather_ici(stage, halves("w_up"))
    w_down_so_far = got[i_down]
    (xr, hr, ga, p, pooled), got = _seq_forward(proj, conv_full, vecs, w_rg_a_all, w_rg_x_all, w_pool_all, ts=tsq, stage=stage)
    w_a_all, w_b_all, w_out_all = (got[i].reshape(1, d, d) for i in i_sq)
    stage = _Stage()
    i_up = _add_gather_d2d(stage, got[i_up])
    i_down = _add_gather_ici(stage, halves("w_down"), rows=(down_rows, 2 * down_rows), whole=w_down_so_far)
    (merged, s_a, s_b, t_a, t_b), got = _branches(ga, pooled, proj, w_a_all, w_b_all, ts=ts, stage=stage)
    w_up_all = got[i_up].reshape(N_CHIPS, d, n_up)
    stage = _Stage()
    i_down = _add_gather_ici(stage, halves("w_down"), rows=(3 * down_rows, down_rows), whole=got[i_down])

    def residual_norm(k, acc, extra_refs, out_refs):
        x_ref, mod_ref, g_ref = extra_refs
        out_refs[0][...] = acc.astype(BF16)
        x2_t = x_ref[...] + mod_ref[2:3, :] * acc
        out_refs[1][...] = x2_t
        out_refs[2][...] = _modulated_norm(x2_t, mod_ref, g_ref, 4, 3)

    (mo, x2, h2), got = _matmul(merged, w_out_all, name="mix_out", tm=ts, extra=[(x0, rows_d), (mod, vec8), (g_mlp, vec1)],
                                outs=[out_rows(d, BF16), out_rows(d, F32), out_rows(d, BF16)], finish=residual_norm, parts=halves_of_tile,                                stage=stage, transposed=False, reduce_blocks=True)
    stage = _Stage()
    i_down = _add_gather_d2d(stage, got[i_down])

    def relu_squared(k, acc, extra_refs, out_refs):
        out_refs[0][:, k * n_up:(k + 1) * n_up] = jnp.square(jnp.maximum(acc, 0.0)).astype(BF16)

    (ff,), got = _matmul(h2, w_up_all, name="mlp_up", tm=ts, outs=[out_rows(D_FF, BF16)], finish=relu_squared, parts=halves_of_tile,
                         stage=stage, **plain)
    w_down_all = got[i_down].reshape(N_CHIPS, D_FF // N_CHIPS, d)

    def loss_head(k, acc, extra_refs, out_refs):
        x2_ref, mod_ref, g_ref, t_ref = extra_refs
        dx_ref, df_ref, acc_ref = out_refs
        xh, r = _rms(x2_ref[...] + mod_ref[5:6, :] * acc)
        err = xh * g_ref[...] - t_ref[...]
        dy = err / d
        dxh = dy * g_ref[...]
        dx3_t = r * (dxh - xh * jnp.mean(dxh * xh, axis=-1, keepdims=True))
        dx_ref[...] = dx3_t
        df_ref[...] = (dx3_t * mod_ref[5:6, :]).astype(BF16)
        _accumulate(acc_ref, 0, dy * xh)
        _accumulate(acc_ref, 1, dx3_t * acc)
        _accumulate(acc_ref, 2, jnp.square(err) * (0.5 / d))

    dx3, dffo, sums_head = _matmul(ff, w_down_all, name="mlp_down", tm=ts, extra=[(x2, rows_d), (mod, vec8), (g_fin, vec1), (target, rows_d)],
                                   outs=[out_rows(d, F32), out_rows(d, BF16), sums_out], finish=loss_head, zero_first=(2,), parts=halves_of_tile,                                   transposed=False, reduce_blocks=True)

    partial_of, from_sibling, chip_sum, from_chips, half_done, quarter = {}, {}, {}, {}, {}, {}

    def publish(n, g):
        g = _blocks_to_pieces(g).astype(BF16) if n in block_weights else g
        partial_of[n] = g.reshape(N_CHIPS, 2, -1, mine[n].shape[-1])

    def exchange(to_sibling=(), to_chips=(), swap=()):
        st = _Stage()
        slots = [(n, from_sibling, _add_reduce_d2d(st, partial_of[n])) for n in to_sibling]
        slots += [(n, from_chips, _add_reduce_ici(st, chip_sum[n])) for n in to_chips]
        slots += [(n, quarter, _add_swap_halves(st, half_done[n])) for n in swap]
        return st, slots

    def collect(slots, outs):
        for n, where, i in slots:
            where[n] = outs[i]

    def sum_pairs(*ns):
        for n in ns:
            chip_sum[n] = _sum_pair(partial_of[n], from_sibling[n], place, "sum_pair_" + n)

    def sum_quarters(*ns):
        for n in ns:
            half_done[n] = _sum_quarter(partial_of[n], from_sibling[n], from_chips[n], place, "sum_quarter_" + n)

    publish("w_down", wg(ff, dffo, nb=1, name="grad_w_down"))

    def relu_squared_backward(k, acc, extra_refs, out_refs):
        cols = slice(k * n_up, (k + 1) * n_up)
        out_refs[0][:, cols] = (acc * (2.0 * jnp.sqrt(extra_refs[0][:, cols].astype(F32)))).astype(BF16)

    stage, slots = exchange(to_sibling=["w_down"])
    (dup,), got = _matmul(dffo, w_down_all, name="d_mlp_down", tm=ts, extra=[(ff, _row_spec(ts, D_FF))], outs=[out_rows(D_FF, BF16)],
                          finish=relu_squared_backward, parts=halves_of_tile, stage=stage, **back)
    collect(slots, got)
    sum_pairs("w_down")

    def norm_mlp_backward(k, acc, extra_refs, out_refs):
        x2_ref, dres_ref, mod_ref, g_ref, mo_ref = extra_refs
        dx_ref, dmo_ref, acc_ref = out_refs
        dx =_modulated_norm_backward(acc, x2_ref[...], dres_ref[...], mod_ref, g_ref, acc_ref, 4)
        dx_ref[...] = dx
        dmo_ref[...] = (dx * mod_ref[2:3, :]).astype(BF16)
        _accumulate(acc_ref, 3, dx * mo_ref[...].astype(F32))

    stage, slots = exchange(to_chips=["w_down"])
    (dx2, dmo, sums_mlp), got = _matmul(dup, w_up_all, name="d_mlp_up", tm=ts, stage=stage, finish=norm_mlp_backward,
                                        extra=[(x2, rows_d), (dx3, rows_d), (mod, vec8), (g_mlp, vec1), (mo, rows_d)],
                                        outs=[out_rows(d, F32), out_rows(d, BF16), sums_out], zero_first=(2,), parts=halves_of_tile, **back_sum)
    collect(slots, got)
    sum_quarters("w_down")
    publish("w_up", wg(h2, dup, nb=N_CHIPS, name="grad_w_up"))
    publish("w_out", wg(merged, dmo, nb=1, name="grad_w_out"))

    def merge_backward(k, acc, extra_refs, out_refs):
        sa_ref, sb_ref, ta_ref, tb_ref = extra_refs
        out_refs[0][...] = (acc * sa_ref[...].astype(F32)).astype(BF16)
        out_refs[1][...] = (acc * sb_ref[...].astype(F32)).astype(BF16)
        out_refs[2][:, 0:d] = (acc * ta_ref[...].astype(F32)).astype(BF16)
        out_refs[2][:, d:2 * d] = (acc * tb_ref[...].astype(F32)).astype(BF16)

    stage, slots = exchange(to_sibling=["w_up", "w_out"], swap=["w_down"])
    (dbr_a, dbr_b, dgab), got = _matmul(dmo, w_out_all, name="d_mix_out", tm=ts, stage=stage, finish=merge_backward,
                                        extra=[(s_a, rows_d), (s_b, rows_d), (t_a, rows_d), (t_b, rows_d)],
                                        outs=[out_rows(d, BF16), out_rows(d, BF16), out_rows(2 * d, BF16)], parts=halves_of_tile, **back_sum)
    collect(slots, got)
    publish("w_a", wg(ga, dbr_a, nb=1, name="grad_w_branch_a"))
    publish("w_b", wg(pooled, dbr_b, nb=1, name="grad_w_branch_b"))
    dga, = _matmul(dbr_a, w_a_all, name="d_branch_a", tm=tw, outs=[out_rows(d, F32, tw)], finish=_store_blocks(d), **back_sum)
    dpooled, = _matmul(dbr_b, w_b_all, name="d_branch_b", tm=tw, outs=[out_rows(d, F32, tw)], finish=_store_blocks(d), **back_sum)
    sum_pairs("w_up", "w_out")
    stage, slots = exchange(to_sibling=["w_a", "w_b"], to_chips=["w_up", "w_out"])
    (dproj, sums_seq, d_rg_a, d_rg_x, d_pool), got = _seq_backward_by_block(proj, xr, hr, p, dga, dpooled, dgab, conv_full, vecs,
                                                                            w_rg_a_all, w_rg_x_all, w_pool_all, ts=tsq, stage=stage)
    collect(slots, got)
    sum_quarters("w_up", "w_out")
    sum_pairs("w_a", "w_b")
    for n, g in zip(block_weights, (d_rg_a, d_rg_x, d_pool)):
        publish(n, g)
    stage, slots = exchange(to_sibling=block_weights, to_chips=["w_a", "w_b"], swap=["w_up", "w_out"])
    g_in, got = wg(h1, dproj, nb=N_CHIPS, name="grad_w_in", stage=stage)
    collect(slots, got)
    publish("w_in", g_in)
    sum_pairs(*block_weights)
    sum_quarters("w_a", "w_b")
    stage, slots = exchange(to_sibling=["w_in"], to_chips=block_weights)
    collect(slots, _run_stage(stage, "exchange_w_in_gates"))
    sum_pairs("w_in")
    sum_quarters(*block_weights)

    def norm_mix_backward(k, acc, extra_refs, out_refs):
        x_ref, dres_ref, mod_ref, g_ref = extra_refs
        out_refs[0][...] = _modulated_norm_backward(acc, x_ref[...], dres_ref[...], mod_ref, g_ref, out_refs[1], 1)

    stage, slots = exchange(to_chips=["w_in"], swap=block_weights + ("w_a", "w_b"))
    (grad_x, sums_mix), got = _matmul(dproj, w_in_all, name="d_proj_in", tm=ts, stage=stage, finish=norm_mix_backward,
                                      extra=[(x0, rows_d), (dx2, rows_d), (mod, vec8), (g_mix, vec1)],
                                      outs=[out_rows(d, F32), sums_out], zero_first=(1,), parts=halves_of_tile, **back_sum)
    collect(slots, got)
    sum_quarters("w_in")

    dmod = jnp.concatenate([sums_mix[1:2], sums_mix[0:1], sums_mlp[3:4], sums_mlp[1:2], sums_mlp[0:1], sums_head[1:2]], axis=1)
    row = jnp.concatenate([sums_mix[2:3], sums_mlp[2:3], sums_seq[4:5], sums_seq[5:6], sums_seq[6:7], sums_seq[7:8], sums_seq[8:9],
                           sums_seq[9:10], sums_head[0:1], sums_seq[0:CONV_WIDTH].reshape(1, CONV_WIDTH * d), dmod, sums_head[2:3]], axis=1)
    n_vec, n_conv, n_mod = 9 * d, CONV_WIDTH * d, N_MOD * d
    stage, slots = exchange(swap=["w_in"])
    i_rows = _add_allgather8(stage, row)
    got = _run_stage(stage, "exchange_sums_w_in")
    collect(slots, got)
    rows_all = got[i_rows]
    total = _sum_rows(rows_all, d)
    loss = total[0, n_vec + n_conv + n_mod]
    dmod_all = rows_all[:, 0, n_vec + n_conv:n_vec + n_conv + n_mod]
    g_ada, d_ada, m_ada, v_ada = _ada_backward(act_all.T, lax.dynamic_slice_in_dim(dmod_all, k_me * ada_cols, ada_cols, axis=1),
                                               w_ada[0], m_w_ada[0], v_w_ada[0])
    g_conv = lax.dynamic_slice_in_dim(total[:, n_vec:n_vec + n_conv].reshape(CONV_WIDTH, d), k_me * conv_cols, conv_cols, axis=1)
    vec_names = ("norm_mix_g", "norm_mlp_g", "conv_b", "b_rg_a", "b_rg_x", "a_param", "b_pool", "pool_scale", "final_g", "conv_w", "b_ada")
    vec_params = [(norm_mix_g, m_norm_mix_g, v_norm_mix_g), (norm_mlp_g, m_norm_mlp_g, v_norm_mlp_g), (conv_b, m_conv_b, v_conv_b),
                  (b_rg_a, m_b_rg_a, v_b_rg_a), (b_rg_x, m_b_rg_x, v_b_rg_x), (a_param, m_a_param, v_a_param), (b_pool, m_b_pool, v_b_pool),
                  (pool_scale, m_pool_scale, v_pool_scale), tuple(a.reshape(1, d) for a in (final_g, m_final_g, v_final_g)),
                  (conv_w[0], m_conv_w[0], v_conv_w[0]), (b_ada, m_b_ada, v_b_ada)]
    offsets = [i * d for i in range(9)] + [None, n_vec + n_conv]
    vec_res = _adamw_vectors(total, offsets, [None] * 9 + [g_conv, None], vec_params)
    vec_out = {n: [r.reshape(shape) for r in res] for n, res, shape in zip(
        vec_names, vec_res, [(1, d)] * 8 + [(d,), (1, CONV_WIDTH, conv_cols), (1, N_MOD * d)])}

    big_out = {}
    for n in names:
        shape2 = (-1, mine[n].shape[-1])
        g2 = quarter[n].reshape(shape2)
        res = _adamw(mine[n].reshape(shape2), g2, moments_m[n].reshape(shape2), moments_v[n].reshape(shape2), "adamw_" + n)
        big_out[n] = [r.reshape((1,) + mine[n].shape) for r in res]

    key = {"w_in": "w_in", "w_rg_a": "rg_a", "w_rg_x": "rg_x", "w_branch_a": "w_a", "w_pool": "pool", "w_branch_b": "w_b", "w_out": "w_out",
           "w_up": "w_up", "w_down": "w_down"}
    order = ("norm_mix_g", "norm_mlp_g", "w_ada", "b_ada", "w_in", "conv_w", "conv_b", "w_rg_a", "b_rg_a", "w_rg_x", "b_rg_x", "a_param",
             "w_branch_a", "w_pool", "b_pool", "pool_scale", "w_branch_b", "w_out", "w_up", "w_down", "final_g")
    ada_out = [g_ada[None], d_ada[None], m_ada[None], v_ada[None]]
    outs = [loss, grad_x[None]]
    for kind in range(4):
        for nme in order:
            outs.append(ada_out[kind] if nme == "w_ada" else big_out[key[nme]][kind] if nme in key else vec_out[nme][kind])
    return tuple(outs)
```

```python
import functools

import jax
import jax.numpy as jnp
from jax import lax
from jax.experimental import pallas as pl
from jax.experimental.pallas import tpu as pltpu

F32, BF16 = jnp.float32, jnp.bfloat16
D_MODEL = 1024
D_FF = 4 * D_MODEL
N_BLOCKS = 4
BLOCK = D_MODEL // N_BLOCKS
CONV_WIDTH = 4
POOL_WINDOWS = (2, 4, 8, 16)
CONV_HALO = 8
POOL_HALO = 16
N_MOD = 6
EPS = 1e-6
C_RG = 8.0
ADAM_LR, ADAM_B1, ADAM_B2, ADAM_EPS, ADAM_WD, ADAM_STEP = 0.001, 0.9, 0.999, 1e-08, 0.01, 10
N_DEV = 8
N_CHIPS = 4
VMEM_LIMIT_V7X = 56 * 2**20
MESH = pl.DeviceIdType.MESH
SDS = jax.ShapeDtypeStruct
HBM_SPEC = pl.BlockSpec(memory_space=pltpu.HBM)
VMEM_SPEC = pl.BlockSpec(memory_space=pltpu.VMEM)


def _params(*semantics):
    return pltpu.CompilerParams(dimension_semantics=semantics, vmem_limit_bytes=VMEM_LIMIT_V7X)


def _coords():
    return lax.axis_index("x"), lax.axis_index("y"), lax.axis_index("c")


def _flip(v, bit):
    return 1 - v if bit else v


def _chips():
    x, y, _ = _coords()
    return [(1 - x, y), (x, 1 - y), (1 - x, 1 - y)]


PEER_SETS = [frozenset(s) for s in (("sibling",), ("chips",), ("sibling", "chips"), ("sibling", "chips", "others"))]


class _Stage:
    def __init__(self):
        self.inputs, self.in_specs, self.outputs, self.out_specs, self.aliases = [], [], [], [], {}
        self.parts, self.n_copies, self.peers = [], 0, set()

    def add(self, inputs, in_spec, outputs, out_spec, n_copies, build, peers, alias=False):
        self.peers |= set(peers)
        i0, o0 = len(self.inputs), len(self.outputs)
        self.inputs += inputs
        self.in_specs += [in_spec] * len(inputs)
        self.outputs += outputs
        self.out_specs += [out_spec] * len(outputs)
        if alias:
            self.aliases.update({i0 + i: o0 + i for i in range(len(inputs))})
        self.parts.append((build, i0, len(inputs), o0, len(outputs)))
        self.n_copies += n_copies
        return list(range(o0, o0 + len(outputs)))

    def copies(self, in_refs, out_refs):
        out = []
        for build, i0, ni, o0, no in self.parts:
            out += build(in_refs[i0:i0 + ni], out_refs[o0:o0 + no])
        assert len(out) == self.n_copies
        return out

    def run(self, in_refs, out_refs, send_sems, recv_sems, start):
        x, y, c = _coords()
        for s, (src, dst, to, landing) in enumerate(self.copies(in_refs, out_refs)):
            if to is None:
                cp = pltpu.make_async_copy(src, dst, send_sems.at[s])
                cp.start() if start else cp.wait()
                continue
            cp = pltpu.make_async_remote_copy(src_ref=src, dst_ref=dst, send_sem=send_sems.at[s], recv_sem=recv_sems.at[s],
                                              device_id=to, device_id_type=MESH)
            if start:
                cp.start()
            else:
                pltpu.make_async_remote_copy(src_ref=landing, dst_ref=landing, send_sem=send_sems.at[s], recv_sem=recv_sems.at[s],
                                             device_id=(x, y, c), device_id_type=MESH).wait_recv()
                cp.wait_send()


def _hosted(stage, body, *, name, in_specs, out_specs, out_shape, grid=(), scratch_shapes=(), compiler_params=None):
    if stage is None:
        return pl.pallas_call(body, name=name, grid=grid, in_specs=in_specs, out_specs=out_specs, out_shape=out_shape,
                              scratch_shapes=list(scratch_shapes), compiler_params=compiler_params)
    single = not isinstance(out_shape, (list, tuple))
    h_out_shape = [out_shape] if single else list(out_shape)
    h_out_specs = [out_specs] if single else list(out_specs)
    n_in, n_out, n_scr = len(in_specs), len(h_out_shape), len(scratch_shapes)
    s_in, s_out = len(stage.inputs), len(stage.outputs)

    def wrapped(*refs):
        h_in, st_in = refs[:n_in], refs[n_in:n_in + s_in]
        h_o, st_o = refs[n_in + s_in:n_in + s_in + n_out], refs[n_in + s_in + n_out:n_in + s_in + n_out + s_out]
        h_scr = refs[n_in + s_in + n_out + s_out:n_in + s_in + n_out + s_out + n_scr]
        send_sems, recv_sems = refs[n_in + s_in + n_out + s_out + n_scr:]
        def begin():
            x, y, c = _coords()
            peers = [(x, y, 1 - c)] * ("sibling" in stage.peers)
            peers += [(px, py, c) for px, py in _chips()] * ("chips" in stage.peers)
            peers += [(px, py, 1 - c) for px, py in _chips()] * ("others" in stage.peers)
            barrier = pltpu.get_barrier_semaphore()
            for peer in peers:
                pl.semaphore_signal(barrier, inc=1, device_id=peer, device_id_type=MESH)
            pl.semaphore_wait(barrier, len(peers))
            stage.run(st_in, st_o, send_sems, recv_sems, True)

        if not grid:
            begin()
            if body is not None:
                body(*h_in, *h_o, *h_scr)
            stage.run(st_in, st_o, send_sems, recv_sems, False)
            return
        ids = [pl.program_id(a) for a in range(len(grid))]
        first = functools.reduce(jnp.logical_and, [i == 0 for i in ids])
        last = functools.reduce(jnp.logical_and, [i == g - 1 for i, g in zip(ids, grid)])
        pl.when(first)(begin)
        body(*h_in, *h_o, *h_scr)
        pl.when(last)(lambda: stage.run(st_in, st_o, send_sems, recv_sems, False))

    call = pl.pallas_call(
        wrapped, name=name, grid=grid, in_specs=list(in_specs) + stage.in_specs, out_specs=h_out_specs + stage.out_specs,
        out_shape=h_out_shape + stage.outputs, input_output_aliases={n_in + i: n_out + o for i, o in stage.aliases.items()},
        scratch_shapes=list(scratch_shapes) + [pltpu.SemaphoreType.DMA((stage.n_copies,)), pltpu.SemaphoreType.DMA((stage.n_copies,))],
        compiler_params=pltpu.CompilerParams(dimension_semantics=("arbitrary",) * len(grid), vmem_limit_bytes=VMEM_LIMIT_V7X,
                                             collective_id=PEER_SETS.index(frozenset(stage.peers))),
    )

    def run(*operands):
        outs = call(*operands, *stage.inputs)
        host = outs[:n_out]
        return (host[0] if single else host), outs[n_out:]

    return run


def _run_stage(stage, name):
    return _hosted(stage, None, name=name, in_specs=[], out_specs=[], out_shape=[])()[1]


def _add_allgather8(stage, v):
    def build(ins, outs):
        x, y, c = _coords()
        me = 4 * x + 2 * y + c
        copies = [(ins[0], outs[0].at[me], None, None)]
        for r in range(1, N_DEV):
            peer = (_flip(x, r & 4), _flip(y, r & 2), _flip(c, r & 1))
            copies.append((ins[0], outs[0].at[me], peer, outs[0].at[me ^ r]))
        return copies

    return stage.add([v], VMEM_SPEC, [SDS((N_DEV,) + v.shape, v.dtype)], VMEM_SPEC, N_DEV, build, ("sibling", "chips", "others"))[0]


def _add_gather_ici(stage, shard, rows=None, whole=None):
    first, count = rows or (0, shard.shape[1])

    def build(ins, outs):
        x, y, c = _coords()
        k_me = 2 * x + y
        part = pl.ds(first, count)
        copies = [] if whole is not None else [(ins[0], outs[0].at[k_me], (x, y, 1 - c), outs[0].at[k_me])]
        for px, py in _chips():
            copies.append((ins[0].at[c, part], outs[0].at[k_me, c, part], (px, py, c), outs[0].at[2 * px + py, c, part]))
        return copies

    if whole is None:
        return stage.add([shard], HBM_SPEC, [SDS((N_CHIPS,) + shard.shape, shard.dtype)], HBM_SPEC, N_CHIPS, build, ("sibling", "chips"))[0]

    def build_into(ins, outs):
        return build(ins[1:], outs)

    i0 = len(stage.inputs)
    out = stage.add([whole, shard], HBM_SPEC, [SDS(whole.shape, whole.dtype)], HBM_SPEC, N_CHIPS - 1, build_into, ("chips",))[0]
    stage.aliases[i0] = out
    return out


def _add_gather_d2d(stage, whole):
    def build(ins, outs):
        x, y, c = _coords()
        return [(outs[0].at[2 * px + py, c], outs[0].at[2 * px + py, c], (x, y, 1 - c), outs[0].at[2 * px + py, 1 - c]) for px, py in _chips()]

    return stage.add([whole], HBM_SPEC, [SDS(whole.shape, whole.dtype)], HBM_SPEC, N_CHIPS - 1, build, ("sibling",), alias=True)[0]


def _add_reduce_d2d(stage, grads):
    def build(ins, outs):
        x, y, c = _coords()
        return [(ins[0].at[k, 1 - c], outs[0].at[k], (x, y, 1 - c), outs[0].at[k]) for k in range(N_CHIPS)]

    return stage.add([grads], HBM_SPEC, [SDS((N_CHIPS,) + grads.shape[2:], grads.dtype)], HBM_SPEC, N_CHIPS, build, ("sibling",))[0]


def _add_reduce_ici(stage, partial):
    def build(ins, outs):
        x, y, c = _coords()
        return [(ins[0].at[2 * px + py], outs[0].at[j], (px, py, c), outs[0].at[j]) for j, (px, py) in enumerate(_chips())]

    return stage.add([partial], HBM_SPEC, [SDS((N_CHIPS - 1,) + partial.shape[1:], partial.dtype)], HBM_SPEC, N_CHIPS - 1, build, ("chips",))[0]


def _add_swap_halves(stage, quarter):
    def build(ins, outs):
        x, y, c = _coords()
        return [(outs[0].at[c], outs[0].at[c], (x, y, 1 - c), outs[0].at[1 - c])]

    return stage.add([quarter], HBM_SPEC, [SDS(quarter.shape, quarter.dtype)], HBM_SPEC, 1, build, ("sibling",), alias=True)[0]


def _sum_pair(grads, from_sibling, place, name):
    _, _, rows, cols = grads.shape
    tr = min(rows, 512)

    def body(place_ref, g_ref, s_ref, o_ref):
        o_ref[0] = (g_ref[0, 0].astype(F32) + s_ref[0].astype(F32)).astype(BF16)

    spec = pl.BlockSpec((1, tr, cols), lambda k, i, place_ref: (k, i, 0))
    return pl.pallas_call(
        body, name=name, out_shape=SDS((N_CHIPS, rows, cols), BF16),
        grid_spec=pltpu.PrefetchScalarGridSpec(
            num_scalar_prefetch=1, grid=(N_CHIPS, rows // tr),
            in_specs=[pl.BlockSpec((1, 1, tr, cols), lambda k, i, place_ref: (k, place_ref[0], i, 0)), spec], out_specs=spec),
        compiler_params=_params("parallel", "parallel"),
    )(place, grads, from_sibling)


def _sum_quarter(grads, from_sibling, from_chips, place, name):
    _, _, rows, cols = grads.shape
    tr = min(rows, 512)

    def body(place_ref, g_ref, s_ref, c_ref, o_ref):
        acc = g_ref[0, 0].astype(F32) + s_ref[0].astype(F32)
        for j in range(N_CHIPS - 1):
            acc = acc + c_ref[j].astype(F32)
        o_ref[0] = acc

    return pl.pallas_call(
        body, name=name, out_shape=SDS((2, rows, cols), F32),
        grid_spec=pltpu.PrefetchScalarGridSpec(
            num_scalar_prefetch=1, grid=(rows // tr,),
            in_specs=[pl.BlockSpec((1, 1, tr, cols), lambda i, place_ref: (place_ref[1], place_ref[0], i, 0)),
                      pl.BlockSpec((1, tr, cols), lambda i, place_ref: (place_ref[1], i, 0)),
                      pl.BlockSpec((N_CHIPS - 1, tr, cols), lambda i, place_ref: (0, i, 0))],
            out_specs=pl.BlockSpec((1, tr, cols), lambda i, place_ref: (place_ref[0], i, 0))),
        compiler_params=_params("parallel"),
    )(place, grads, from_sibling, from_chips)


def _row_spec(ts, cols, col_block=0):
    return pl.BlockSpec((ts, cols), lambda *g: (g[0], col_block))


def _vec_spec(rows, cols):
    return pl.BlockSpec((rows, cols), lambda *g: (0, 0))


def _matmul(a, w, *, transposed, reduce_blocks, name, tm, outs, finish, extra=(), prepare=None, zero_first=(), parts=1, stage=None):
    m_rows = a.shape[0]
    nb, r, c = w.shape
    kb = c if transposed else r
    dims = (((1,), (1,)), ((), ())) if transposed else (((1,), (0,)), ((), ()))
    assert a.shape[1] == (nb * kb if reduce_blocks else kb) and m_rows % tm == 0 and tm % parts == 0
    n_extra, sub = len(extra), tm // parts

    def body(a_ref, w_ref, *rest):
        extra_refs, out_refs = rest[:n_extra], rest[n_extra:]
        if zero_first:
            @pl.when(pl.program_id(0) == 0)
            def _():
                for i in zero_first:
                    out_refs[i][...] = jnp.zeros_like(out_refs[i])

        def views(j):
            rows = pl.ds(j * sub, sub)
            return rows, [ref.at[rows] if ref.shape[0] == tm else ref for ref in extra_refs], [ref.at[rows] if ref.shape[0] == tm else ref for ref in out_refs]

        def product(j):
            rows, ex, ou = views(j)
            if reduce_blocks:
                acc = lax.dot_general(a_ref[rows, 0:kb], w_ref[0], dims, preferred_element_type=F32)
                for k in range(1, nb):
                    acc += lax.dot_general(a_ref[rows, k * kb:(k + 1) * kb], w_ref[k], dims, preferred_element_type=F32)
                return [(None, acc)]
            lhs = a_ref[rows, :] if prepare is None else prepare(a_ref.at[rows], ex, ou)
            return [(k, lax.dot_general(lhs, w_ref[k], dims, preferred_element_type=F32)) for k in range(nb)]

        pending = product(0)
        for j in range(parts):
            ahead = product(j + 1) if j + 1 < parts else None
            _, ex, ou = views(j)
            for k, acc in pending:
                finish(k, acc, ex, ou)
            pending = ahead

    return _hosted(
        stage, body, name=name, grid=(m_rows // tm,), out_shape=[s for s, _ in outs], out_specs=[s for _, s in outs],
        in_specs=[_row_spec(tm, a.shape[1]), pl.BlockSpec((nb, r, c), lambda *g: (0, 0, 0), pipeline_mode=pl.Buffered(1))] + [s for _, s in extra],
        compiler_params=_params("arbitrary"),
    )(a, w, *[e for e, _ in extra])


def _store_blocks(n):
    def finish(k, acc, extra_refs, out_refs):
        if k is None:
            out_refs[0][...] = acc.astype(out_refs[0].dtype)
        else:
            out_refs[0][:, k * n:(k + 1) * n] = acc.astype(out_refs[0].dtype)
    return finish


def _weight_grad(a, b, *, nb, name, tk, stage=None):
    s_rows, k1 = a.shape
    tn = b.shape[1] // nb
    assert k1 % tk == 0 and b.shape[1] % nb == 0

    def body(a_ref, b_ref, o_ref):
        o_ref[0] = lax.dot_general(a_ref[...], b_ref[...], (((0,), (0,)), ((), ())), preferred_element_type=F32).astype(o_ref.dtype)

    return _hosted(
        stage, body, name=name, grid=(nb, k1 // tk), out_shape=SDS((nb, k1, tn), BF16),
        in_specs=[pl.BlockSpec((s_rows, tk), lambda n, i: (0, i)), pl.BlockSpec((s_rows, tn), lambda n, i: (0, n))],
        out_specs=pl.BlockSpec((1, tk, tn), lambda n, i: (n, i, 0)), compiler_params=_params("parallel", "parallel"),
    )(a, b)


def _rms(x):
    r = lax.rsqrt(jnp.mean(x * x, axis=-1, keepdims=True) + EPS)
    return x * r, r


def _accumulate(acc_ref, row, value):
    acc_ref[row:row + 1, :] += jnp.sum(value, axis=0, keepdims=True)


def _gelu_parts(y):
    k0, k1 = 0.7978845608028654, 0.044715
    th = jnp.tanh(k0 * (y + k1 * (y * y * y)))
    gelu = 0.5 * y * (1.0 + th)
    dgelu = 0.5 * (1.0 + th) + 0.5 * y * (1.0 - th * th) * (k0 * (1.0 + 3.0 * k1 * (y * y)))
    return gelu, dgelu


def _sigmoid(z):
    return 0.5 * jnp.tanh(0.5 * z) + 0.5


def _one_minus_square(a, log_a):
    return -jnp.tanh(log_a) * (1.0 + a * a)


def _modulated_norm(x, mod_ref, g_ref, sc_row, sh_row):
    xh, _ = _rms(x)
    return ((xh * g_ref[...]) * (1.0 + mod_ref[sc_row:sc_row + 1, :]) + mod_ref[sh_row:sh_row + 1, :]).astype(BF16)


def _modulated_norm_backward(dh, x, dres, mod_ref, g_ref, acc_ref, sc_row):
    xh, r = _rms(x)
    _accumulate(acc_ref, 0, dh * (xh * g_ref[...]))
    _accumulate(acc_ref, 1, dh)
    dn = dh * (1.0 + mod_ref[sc_row:sc_row + 1, :])
    _accumulate(acc_ref, 2, dn * xh)
    dxh = dn * g_ref[...]
    return dres + r * (dxh - xh * jnp.mean(dxh * xh, axis=-1, keepdims=True))


def _seq_forward(proj, conv_w, vecs, w_a, w_x, w_p, *, ts, stage=None):
    s_rows = proj.shape[0]
    d = D_MODEL

    def body(x_ref, xh_ref, y_ref, u_ref, uh_ref, cw_ref, vec_ref, wa_ref, wx_ref, wp_ref,
             xr_o, hr_o, ga_o, p_o, pooled_o, carry, a_scr, b_scr):
        t = pl.program_id(0)

        @pl.when(t == 0)
        def _():
            carry[...] = jnp.zeros_like(carry)

        rows = lax.broadcasted_iota(jnp.int32, (ts, BLOCK), 0)
        start = (rows == 0) & (t == 0)
        for g in range(N_BLOCKS):
            lanes = slice(g * BLOCK, (g + 1) * BLOCK)
            x = x_ref[:, lanes]
            xx = jnp.concatenate([jnp.where(t == 0, 0.0, xh_ref[:, lanes]), x], axis=0)
            xr = vec_ref[0:1, lanes] + x * cw_ref[CONV_WIDTH - 1:CONV_WIDTH, lanes]
            for j in range(CONV_WIDTH - 1):
                xr = xr + pltpu.roll(xx, CONV_WIDTH - 1 - j, 0)[CONV_HALO:] * cw_ref[j:j + 1, lanes]
            xr_o[:, lanes] = xr
            xrb = xr.astype(BF16)
            r = _sigmoid(jnp.dot(xrb, wa_ref[g], preferred_element_type=F32) + vec_ref[1:2, lanes])
            gate_i = _sigmoid(jnp.dot(xrb, wx_ref[g], preferred_element_type=F32) + vec_ref[2:3, lanes])
            log_a = (-C_RG * r) * jax.nn.softplus(vec_ref[3:4, lanes])
            a = jnp.exp(log_a)
            a_scr[:, lanes] = a
            b_scr[:, lanes] = xr * gate_i * jnp.where(start, 1.0, jnp.sqrt(_one_minus_square(a, log_a)))

        sub = lax.broadcasted_iota(jnp.int32, (8, d), 0)

        def chunk(i, h):
            a = a_scr[pl.ds(i * 8, 8), :]
            b = b_scr[pl.ds(i * 8, 8), :]
            for s in (1, 2, 4):
                keep = sub >= s
                b = jnp.where(keep, a * pltpu.roll(b, s, 0) + b, b)
                a = jnp.where(keep, a * pltpu.roll(a, s, 0), a)
            hh = a * h + b
            hr_o[pl.ds(i * 8, 8), :] = hh
            return hh[7:8, :]

        carry[...] = lax.fori_loop(0, ts // 8, chunk, carry[...])
        pos = (rows + t * ts + 1).astype(F32)
        for g, win in enumerate(POOL_WINDOWS):
            lanes = slice(g * BLOCK, (g + 1) * BLOCK)
            gelu, _ = _gelu_parts(y_ref[:, lanes])
            ga_o[:, lanes] = (gelu * hr_o[:, lanes]).astype(BF16)
            u = u_ref[:, lanes]
            sums, have = jnp.concatenate([jnp.where(t == 0, 0.0, uh_ref[:, lanes]), u], axis=0), 1
            while have < win:
                sums = sums + pltpu.roll(sums, have, 0)
                have *= 2
            pb = (sums[POOL_HALO:] * (1.0 / jnp.minimum(pos, float(win))) - u).astype(BF16)
            p_o[:, lanes] = pb
            mixed = jnp.dot(pb, wp_ref[g], preferred_element_type=F32) + vec_ref[4:5, lanes]
            pooled_o[:, lanes] = (mixed * vec_ref[5:6, lanes]).astype(BF16)

    halo_x = pl.BlockSpec((CONV_HALO, d), lambda t: (jnp.maximum(t * (ts // CONV_HALO) - 1, 0), 0))
    halo_u = pl.BlockSpec((POOL_HALO, d), lambda t: (jnp.maximum(t * (ts // POOL_HALO) - 1, 0), 2))
    wspec = pl.BlockSpec((N_BLOCKS, BLOCK, BLOCK), lambda t: (0, 0, 0))
    return _hosted(
        stage, body, name="seq_forward", grid=(s_rows // ts,),
        out_shape=[SDS((s_rows, d), F32), SDS((s_rows, d), F32), SDS((s_rows, d), BF16), SDS((s_rows, d), BF16), SDS((s_rows, d), BF16)],
        in_specs=[_row_spec(ts, d, 0), halo_x, _row_spec(ts, d, 1), _row_spec(ts, d, 2), halo_u, _vec_spec(CONV_WIDTH, d), _vec_spec(8, d),
                  wspec, wspec, wspec],
        out_specs=[_row_spec(ts, d)] * 5,
        scratch_shapes=[pltpu.VMEM((1, d), F32), pltpu.VMEM((ts, d), F32), pltpu.VMEM((ts, d), F32)],
        compiler_params=_params("arbitrary"),
    )(proj, proj, proj, proj, proj, conv_w, vecs, w_a, w_x, w_p)


def _branches(ga, pooled, proj, w_a, w_b, *, ts, stage=None):
    s_rows, d = ga.shape

    def body(a_ref, p_ref, ga_ref, gb_ref, wa_ref, wb_ref, merged_o, sa_o, sb_o, ta_o, tb_o):
        bra = jnp.dot(a_ref[...], wa_ref[0], preferred_element_type=F32)
        brb = jnp.dot(p_ref[...], wb_ref[0], preferred_element_type=F32)
        sa, sb = _sigmoid(ga_ref[...]), _sigmoid(gb_ref[...])
        merged_o[...] = (sa * bra + sb * brb).astype(BF16)
        sa_o[...] = sa.astype(BF16)
        sb_o[...] = sb.astype(BF16)
        ta_o[...] = (bra * (sa * (1.0 - sa))).astype(BF16)
        tb_o[...] = (brb * (sb * (1.0 - sb))).astype(BF16)

    rows = _row_spec(ts, d)
    wspec = pl.BlockSpec((1, d, d), lambda *g: (0, 0, 0), pipeline_mode=pl.Buffered(1))
    return _hosted(
        stage, body, name="branches", grid=(s_rows // ts,), out_shape=[SDS((s_rows, d), BF16)] * 5,
        in_specs=[rows, rows, _row_spec(ts, d, 3), _row_spec(ts, d, 4), wspec, wspec], out_specs=[rows] * 5,
        compiler_params=_params("arbitrary"),
    )(ga, pooled, proj, proj, w_a, w_b)


def _seq_backward_by_block(proj, xr, hr, p, dga, dpooled, dgab, conv_w, vecs, w_a, w_x, w_p, *, ts, stage=None):
    s_rows = proj.shape[0]
    d = D_MODEL
    n_t = s_rows // ts
    nn, nt, tn = (((1,), (0,)), ((), ())), (((1,), (1,)), ((), ())), (((0,), (0,)), ((), ()))

    def dot(lhs, rhs, dims):
        return lax.dot_general(lhs, rhs, dims, preferred_element_type=F32)

    def body(x_ref, y_ref, xr_ref, hr_ref, hh_ref, p_ref, dga_ref, dpl_ref, dgab_ref, cw_ref, vec_ref, wa_ref, wx_ref, wp_ref,
             dproj_o, acc_o, dwa_o, dwx_o, dwp_o, g_carry, dxr_carry, q_carry, a_scr, b_scr, g_scr, a_keep, r_keep, i_keep, m_keep):
        step = pl.program_id(0)
        t = n_t - 1 - step

        @pl.when(step == 0)
        def _():
            for ref in (acc_o, dwa_o, dwx_o, dwp_o, g_carry, dxr_carry, q_carry):
                ref[...] = jnp.zeros_like(ref)

        rows = lax.broadcasted_iota(jnp.int32, (ts, BLOCK), 0)
        start = (rows == 0) & (t == 0)

        def add_sum(row, lanes, value):
            acc_o[row:row + 1, lanes] += jnp.sum(value, axis=0, keepdims=True)

        for g in range(N_BLOCKS):
            lanes = slice(g * BLOCK, (g + 1) * BLOCK)
            xrb = xr_ref[:, lanes].astype(BF16)
            r = _sigmoid(dot(xrb, wa_ref[g], nn) + vec_ref[1:2, lanes])
            gate_i = _sigmoid(dot(xrb, wx_ref[g], nn) + vec_ref[2:3, lanes])
            log_a = (-C_RG * r) * jax.nn.softplus(vec_ref[3:4, lanes])
            a = jnp.exp(log_a)
            a_keep[:, lanes], r_keep[:, lanes], i_keep[:, lanes], m_keep[:, lanes] = a, r, gate_i, _one_minus_square(a, log_a)
            gelu, dgelu = _gelu_parts(y_ref[:, lanes])
            dga_t = dga_ref[:, lanes].astype(F32)
            dproj_o[:, d + g * BLOCK:d + (g + 1) * BLOCK] = (dga_t * hr_ref[:, lanes] * dgelu).astype(BF16)
            a_scr[:, lanes] = jnp.where(rows == ts - 1, 1.0, pltpu.roll(a, ts - 1, 0))
            b_scr[:, lanes] = dga_t * gelu

        sub = lax.broadcasted_iota(jnp.int32, (8, d), 0)

        def chunk(i, g_next):
            at = pl.multiple_of((ts // 8 - 1 - i) * 8, 8)
            aa = a_scr[pl.ds(at, 8), :]
            bb = b_scr[pl.ds(at, 8), :]
            for s in (1, 2, 4):
                keep = sub < 8 - s
                bb = jnp.where(keep, bb + aa * pltpu.roll(bb, 8 - s, 0), bb)
                aa = jnp.where(keep, aa * pltpu.roll(aa, 8 - s, 0), aa)
            gg = aa * g_next + bb
            g_scr[pl.ds(at, 8), :] = gg
            return gg[0:1, :]

        g_first = lax.fori_loop(0, ts // 8, chunk, g_carry[...])
        g_carry[...] = a_keep[0:1, :] * g_first

        pos = (rows + t * ts + 1).astype(F32)
        for g, win in enumerate(POOL_WINDOWS):
            lanes = slice(g * BLOCK, (g + 1) * BLOCK)
            gs, xr, a, r, gate_i, m_square = g_scr[:, lanes], xr_ref[:, lanes], a_keep[:, lanes], r_keep[:, lanes], i_keep[:, lanes], m_keep[:, lanes]
            sp = jax.nn.softplus(vec_ref[3:4, lanes])
            mult = jnp.where(start, 1.0, jnp.sqrt(m_square))
            h_before = jnp.where(t == 0, 0.0, hh_ref[CONV_HALO - 1:CONV_HALO, lanes])
            h_prev = jnp.where(rows == 0, h_before, pltpu.roll(hr_ref[:, lanes], 1, 0))
            gm = gs * mult
            dxr = gm * gate_i
            d_i = gm * xr
            d_mult = (gs * xr) * gate_i
            dlog_a = (gs * h_prev) * a - jnp.where(start, 0.0, d_mult * (a * a) * lax.rsqrt(m_square))
            dzr = (dlog_a * (-C_RG * sp)) * (r * (1.0 - r))
            dzi = d_i * (gate_i * (1.0 - gate_i))
            add_sum(5, lanes, dzr)
            add_sum(6, lanes, dzi)
            acc_o[7:8, lanes] += jnp.sum(dlog_a * r, axis=0, keepdims=True) * (-C_RG * jax.nn.sigmoid(vec_ref[3:4, lanes]))
            xrb, dzrb, dzib = xr.astype(BF16), dzr.astype(BF16), dzi.astype(BF16)
            dwa_o[g] += dot(xrb, dzrb, tn)
            dwx_o[g] += dot(xrb, dzib, tn)
            dxr = dxr + dot(dzrb, wa_ref[g], nt) + dot(dzib, wx_ref[g], nt)

            x = x_ref[:, lanes]
            ext = jnp.concatenate([dxr, dxr_carry[:, lanes]], axis=0)
            dx = dxr * cw_ref[CONV_WIDTH - 1:CONV_WIDTH, lanes]
            add_sum(CONV_WIDTH - 1, lanes, x * dxr)
            for j in range(CONV_WIDTH - 1):
                ahead = pltpu.roll(ext, ts + CONV_HALO - (CONV_WIDTH - 1 - j), 0)[:ts]
                dx = dx + ahead * cw_ref[j:j + 1, lanes]
                add_sum(j, lanes, x * ahead)
            add_sum(4, lanes, dxr)
            dxr_carry[:, lanes] = dxr[0:CONV_HALO, :]
            dproj_o[:, lanes] = dx.astype(BF16)

            pb = p_ref[:, lanes]
            mixed = dot(pb, wp_ref[g], nn) + vec_ref[4:5, lanes]
            dpl = dpl_ref[:, lanes].astype(F32)
            add_sum(9, lanes, dpl * mixed)
            dmixed = dpl * vec_ref[5:6, lanes]
            add_sum(8, lanes, dmixed)
            dmb = dmixed.astype(BF16)
            dwp_o[g] += dot(pb, dmb, tn)
            dp = dot(dmb, wp_ref[g], nt)
            q = dp * (1.0 / jnp.minimum(pos, float(win)))
            sums, have = jnp.concatenate([q, q_carry[:, lanes]], axis=0), 1
            while have < win:
                sums = sums + pltpu.roll(sums, ts + POOL_HALO - have, 0)
                have *= 2
            q_carry[:, lanes] = q[0:POOL_HALO, :]
            dproj_o[:, 2 * d + g * BLOCK:2 * d + (g + 1) * BLOCK] = (sums[:ts] - dp).astype(BF16)

        dproj_o[:, 3 * d:5 * d] = dgab_ref[...]

    def rev(cols, col_block=0):
        return pl.BlockSpec((ts, cols), lambda i: (n_t - 1 - i, col_block))

    halo_h = pl.BlockSpec((CONV_HALO, d), lambda i: (jnp.maximum((n_t - 1 - i) * (ts // CONV_HALO) - 1, 0), 0))
    wspec = pl.BlockSpec((N_BLOCKS, BLOCK, BLOCK), lambda i: (0, 0, 0))
    return _hosted(
        stage, body, name="seq_backward", grid=(n_t,),
        out_shape=[SDS((s_rows, 5 * d), BF16), SDS((16, d), F32)] + [SDS((N_BLOCKS, BLOCK, BLOCK), F32)] * 3,
        in_specs=[rev(d, 0), rev(d, 1), rev(d), rev(d), halo_h, rev(d), rev(d), rev(d), rev(2 * d), _vec_spec(CONV_WIDTH, d), _vec_spec(8, d),
                  wspec, wspec, wspec],
        out_specs=[rev(5 * d), _vec_spec(16, d), wspec, wspec, wspec],
        scratch_shapes=[pltpu.VMEM((1, d), F32), pltpu.VMEM((CONV_HALO, d), F32), pltpu.VMEM((POOL_HALO, d), F32)] + [pltpu.VMEM((ts, d), F32)] * 7,
        compiler_params=_params("arbitrary"),
    )(proj, proj, xr, hr, hr, p, dga, dpooled, dgab, conv_w, vecs, w_a, w_x, w_p)


def _adamw_math(w, g, m, v):
    m = ADAM_B1 * m + (1.0 - ADAM_B1) * g
    v = ADAM_B2 * v + (1.0 - ADAM_B2) * jnp.square(g)
    m_hat = m / (1.0 - ADAM_B1 ** ADAM_STEP)
    v_hat = v / (1.0 - ADAM_B2 ** ADAM_STEP)
    delta = -ADAM_LR * (m_hat / (jnp.sqrt(v_hat) + ADAM_EPS) + ADAM_WD * w)
    return delta, m, v


def _adamw(w, g, m, v, name):
    rows, cols = w.shape
    tr = min(rows, 512)

    def body(w_ref, g_ref, m_ref, v_ref, g_out, d_ref, mo_ref, vo_ref):
        g = g_ref[...]
        g_out[...] = g
        d_ref[...], mo_ref[...], vo_ref[...] = _adamw_math(w_ref[...], g, m_ref[...], v_ref[...])

    spec = pl.BlockSpec((tr, cols), lambda i: (i, 0))
    return pl.pallas_call(
        body, name=name, grid=(rows // tr,), out_shape=[SDS((rows, cols), F32)] * 4, in_specs=[spec] * 4, out_specs=[spec] * 4,
        compiler_params=_params("parallel"),
    )(w, g, m, v)


def _prologue(c, conv_shard, w_ada, b_cols, w_in_halves, gate_halves):
    n_cols = w_ada.shape[1]
    n_g = len(gate_halves)
    half_rows = w_in_halves.shape[1] // 2
    n_sems = 3 * N_DEV + N_CHIPS * n_g + 5 + 2 + 3

    def body(c_ref, conv_ref, w_ref, b_ref, win_ref, *rest):
        gate_refs = rest[:n_g]
        act_o, mod_o, conv_o, win_o = rest[n_g:n_g + 4]
        gate_o = rest[n_g + 4:2 * n_g + 4]
        c_all, mod_cols, send_sems, recv_sems = rest[2 * n_g + 4:]
        x, y, c = _coords()
        me, k_me = 4 * x + 2 * y + c, 2 * x + y
        sibling, x_nbr, y_nbr = (x, y, 1 - c), (1 - x, y, c), (x, 1 - y, c)
        k_x, k_y, k_d = 2 * (1 - x) + y, 2 * x + (1 - y), 2 * (1 - x) + (1 - y)
        counter = iter(range(n_sems))

        def copy(src, dst, to, landing):
            s = next(counter)
            out = pltpu.make_async_remote_copy(src_ref=src, dst_ref=dst, send_sem=send_sems.at[s], recv_sem=recv_sems.at[s],
                                               device_id=to, device_id_type=MESH)
            arrival = pltpu.make_async_remote_copy(src_ref=landing, dst_ref=landing, send_sem=send_sems.at[s], recv_sem=recv_sems.at[s],
                                                   device_id=(x, y, c), device_id_type=MESH)
            return out, arrival

        def allgather(src, out):
            s = next(counter)
            own = pltpu.make_async_copy(src, out.at[me], send_sems.at[s])
            peers = [copy(src, out.at[me], (_flip(x, r & 4), _flip(y, r & 2), _flip(c, r & 1)), out.at[me ^ r]) for r in range(1, N_DEV)]
            return own, peers

        def start(group):
            own, peers = group
            own.start()
            for out, _ in peers:
                out.start()

        def finish(group):
            own, peers = group
            for out, arrival in peers:
                arrival.wait_recv()
                out.wait_send()
            own.wait()

        rows_a, rows_b = pl.ds(0, half_rows), pl.ds(half_rows, half_rows)
        c_group, conv_group = allgather(c_ref, c_all), allgather(conv_ref, conv_o)
        gates = []
        for g_ref, g_out in zip(gate_refs, gate_o):
            gates.append(copy(g_ref, g_out.at[k_me], sibling, g_out.at[k_me]))
            gates += [copy(g_ref.at[c], g_out.at[k_me, c], (px, py, c), g_out.at[2 * px + py, c]) for px, py in _chips()]
        own_in = copy(win_ref, win_o.at[k_me], sibling, win_o.at[k_me])
        to_x = [copy(win_ref.at[c, rows], win_o.at[k_me, c, rows], x_nbr, win_o.at[k_x, c, rows]) for rows in (rows_a, rows_b)]
        to_y = [copy(win_ref.at[c, rows], win_o.at[k_me, c, rows], y_nbr, win_o.at[k_y, c, rows]) for rows in (rows_b, rows_a)]
        mod_group = allgather(mod_cols, mod_o)
        relay_y = copy(win_o.at[k_x, c, rows_a], win_o.at[k_x, c, rows_a], y_nbr, win_o.at[k_d, c, rows_a])
        relay_x = copy(win_o.at[k_y, c, rows_b], win_o.at[k_y, c, rows_b], x_nbr, win_o.at[k_d, c, rows_b])
        sib_x, sib_y, sib_d = (copy(win_o.at[k, c], win_o.at[k, c], sibling, win_o.at[k, 1 - c]) for k in (k_x, k_y, k_d))

        start(c_group)
        start(conv_group)
        for out, _ in gates + [own_in] + to_x + to_y:
            out.start()
        finish(c_group)
        cv = jnp.concatenate([c_all[r] for r in range(N_DEV)], axis=0)
        act = cv * jax.nn.sigmoid(cv)
        act_o[...] = act
        mod_cols[...] = jnp.dot(act.astype(BF16), w_ref[...].astype(BF16), preferred_element_type=F32) + b_ref[...]
        start(mod_group)
        to_x[0][1].wait_recv()
        relay_y[0].start()
        to_y[0][1].wait_recv()
        relay_x[0].start()
        finish(mod_group)
        finish(conv_group)
        to_x[1][1].wait_recv()
        sib_x[0].start()
        to_y[1][1].wait_recv()
        sib_y[0].start()
        for out, arrival in gates + [own_in]:
            arrival.wait_recv()
            out.wait_send()
        for out, arrival in (relay_y, relay_x):
            arrival.wait_recv()
            out.wait_send()
        sib_d[0].start()
        for out, _ in to_x + to_y:
            out.wait_send()
        for out, arrival in (sib_x, sib_y, sib_d):
            arrival.wait_recv()
            out.wait_send()

    gate_shapes = [SDS((N_CHIPS,) + g.shape, g.dtype) for g in gate_halves]
    return pl.pallas_call(
        body, name="prologue",
        out_shape=[SDS((N_DEV, c.shape[1]), F32), SDS((N_DEV, N_DEV, n_cols), F32), SDS((N_DEV,) + conv_shard.shape, F32),
                   SDS((N_CHIPS,) + w_in_halves.shape, w_in_halves.dtype)] + gate_shapes,
        in_specs=[VMEM_SPEC] * 4 + [HBM_SPEC] * (1 + n_g), out_specs=[VMEM_SPEC] * 3 + [HBM_SPEC] * (1 + n_g),
        scratch_shapes=[pltpu.VMEM((N_DEV,) + c.shape, F32), pltpu.VMEM((N_DEV, n_cols), F32),
                        pltpu.SemaphoreType.DMA((n_sems,)), pltpu.SemaphoreType.DMA((n_sems,))],
        compiler_params=pltpu.CompilerParams(vmem_limit_bytes=VMEM_LIMIT_V7X),
    )(c, conv_shard, w_ada, b_cols, w_in_halves, *gate_halves)


def _ada_backward(act_t, dmod_cols, w, m, v):
    rows, cols = w.shape
    tn = 512

    def body(a_ref, dm_ref, w_ref, m_ref, v_ref, g_ref, d_ref, mo_ref, vo_ref):
        g = jnp.dot(a_ref[...].astype(BF16), dm_ref[...].astype(BF16), preferred_element_type=F32)
        g_ref[...] = g
        d_ref[...], mo_ref[...], vo_ref[...] = _adamw_math(w_ref[...], g, m_ref[...], v_ref[...])

    spec = pl.BlockSpec((rows, tn), lambda j: (0, j))
    return pl.pallas_call(
        body, name="ada_backward", grid=(cols // tn,), out_shape=[SDS((rows, cols), F32)] * 4,
        in_specs=[pl.BlockSpec((rows, N_DEV), lambda j: (0, 0)), pl.BlockSpec((N_DEV, tn), lambda j: (0, j)), spec, spec, spec],
        out_specs=[spec] * 4, compiler_params=_params("parallel"),
    )(act_t, dmod_cols, w, m, v)


def _sum_rows(gathered, n_loss):
    n = gathered.shape[2]

    def body(g_ref, o_ref):
        acc = g_ref[0]
        for r in range(1, N_DEV):
            acc = acc + g_ref[r]
        o_ref[...] = acc
        o_ref[:, n - n_loss:n] = jnp.broadcast_to(jnp.sum(acc[:, n - n_loss:n], axis=1, keepdims=True), (1, n_loss))

    return pl.pallas_call(body, name="sum_rows", out_shape=SDS((1, n), F32), in_specs=[VMEM_SPEC], out_specs=VMEM_SPEC)(gathered)


def _adamw_vectors(total, offsets, separate, params):
    n_p = len(params)
    apart = [g for g in separate if g is not None]

    def body(*refs):
        total_ref, apart_refs = refs[0], list(refs[1:1 + len(apart)])
        ins, outs = refs[1 + len(apart):1 + len(apart) + 3 * n_p], refs[1 + len(apart) + 3 * n_p:]
        for i in range(n_p):
            w_ref, m_ref, v_ref = ins[3 * i:3 * i + 3]
            g = apart_refs.pop(0)[...] if offsets[i] is None else total_ref[:, offsets[i]:offsets[i] + w_ref.shape[1]]
            outs[4 * i][...] = g
            outs[4 * i + 1][...], outs[4 * i + 2][...], outs[4 * i + 3][...] = _adamw_math(w_ref[...], g, m_ref[...], v_ref[...])

    flat = [a for p in params for a in p]
    res = pl.pallas_call(
        body, name="adamw_vectors", out_shape=[SDS(p[0].shape, F32) for p in params for _ in range(4)],
        in_specs=[VMEM_SPEC] * (1 + len(apart) + len(flat)), out_specs=[VMEM_SPEC] * (4 * n_p),
    )(total, *apart, *flat)
    return [tuple(res[4 * i:4 * i + 4]) for i in range(n_p)]


def _blocks_to_pieces(w):
    nb, rows, n = w.shape
    q = rows // N_CHIPS
    return w.reshape(nb, N_CHIPS, q, n).transpose(1, 0, 2, 3).reshape(N_CHIPS, 2, nb // 2, q, n)


def _pieces_to_blocks(w):
    n_chips, _, half, q, n = w.shape
    return w.reshape(n_chips, 2 * half, q, n).transpose(1, 0, 2, 3).reshape(2 * half, n_chips * q, n)


def kernel(x, c, norm_mix_g, norm_mlp_g, w_ada, b_ada, w_in, conv_w, conv_b, w_rg_a, b_rg_a, w_rg_x, b_rg_x, a_param, w_branch_a, w_pool, b_pool, pool_scale, w_branch_b, w_out, w_up, w_down, final_g, loss_target, m_norm_mix_g, m_norm_mlp_g, m_w_ada, m_b_ada, m_w_in, m_conv_w, m_conv_b, m_w_rg_a, m_b_rg_a, m_w_rg_x, m_b_rg_x, m_a_param, m_w_branch_a, m_w_pool, m_b_pool, m_pool_scale, m_w_branch_b, m_w_out, m_w_up, m_w_down, m_final_g, v_norm_mix_g, v_norm_mlp_g, v_w_ada, v_b_ada, v_w_in, v_conv_w, v_conv_b, v_w_rg_a, v_b_rg_a, v_w_rg_x, v_b_rg_x, v_a_param, v_w_branch_a, v_w_pool, v_b_pool, v_pool_scale, v_w_branch_b, v_w_out, v_w_up, v_w_down, v_final_g):
    d = D_MODEL
    s_rows = x.shape[1]
    ts, tsq = min(512, s_rows), min(256, s_rows)
    xi, yi, ci = _coords()
    k_me, me = 2 * xi + yi, 4 * xi + 2 * yi + ci
    ada_cols = w_ada.shape[2]
    conv_cols = conv_w.shape[2]
    n_in, n_up = w_in.shape[2], w_up.shape[2]

    names = ("w_in", "w_up", "w_down", "w_a", "w_b", "w_out", "rg_a", "rg_x", "pool")
    mine = dict(zip(names, (w_in[0], w_up[0], w_down[0], w_branch_a[0], w_branch_b[0], w_out[0], w_rg_a[0], w_rg_x[0], w_pool[0])))
    moments_m = dict(zip(names, (m_w_in[0], m_w_up[0], m_w_down[0], m_w_branch_a[0], m_w_branch_b[0], m_w_out[0], m_w_rg_a[0], m_w_rg_x[0], m_w_pool[0])))
    moments_v = dict(zip(names, (v_w_in[0], v_w_up[0], v_w_down[0], v_w_branch_a[0], v_w_branch_b[0], v_w_out[0], v_w_rg_a[0], v_w_rg_x[0], v_w_pool[0])))
    block_weights, squares = ("rg_a", "rg_x", "pool"), ("w_a", "w_b", "w_out")
    place = jnp.stack([ci, k_me]).astype(jnp.int32)
    g_mix, g_mlp, g_fin = norm_mix_g, norm_mlp_g, final_g.reshape(1, d)
    x0, target = x[0], loss_target[0]
    wg = functools.partial(_weight_grad, tk=min(1024, d))
    plain, back, back_sum = (dict(transposed=t, reduce_blocks=r) for t, r in ((False, False), (True, False), (True, True)))
    rows_d, vec8, vec1 = _row_spec(ts, d), _vec_spec(8, d), _vec_spec(1, d)
    tw = min(1024, s_rows)
    halves_of_tile = 2

    def out_rows(cols, dtype, t=ts):
        return SDS((s_rows, cols), dtype), _row_spec(t, cols)

    sums_out = (SDS((8, d), F32), vec8)

    def halves(n):
        return mine[n].astype(BF16).reshape(2, -1, mine[n].shape[-1])

    def blocks(w):
        return _pieces_to_blocks(w.reshape(N_CHIPS, 2, N_BLOCKS // 2, BLOCK // N_CHIPS, BLOCK))

    act_all, mod_all, conv_all, w_in_whole, *gate_wholes = _prologue(
        c, conv_w[0], w_ada[0], lax.dynamic_slice_in_dim(b_ada, k_me * ada_cols, ada_cols, axis=1), halves("w_in"), [halves(n) for n in block_weights])
    conv_full = conv_all[0::2].transpose(1, 0, 2).reshape(CONV_WIDTH, d)
    mod_mine = lax.dynamic_index_in_dim(mod_all, me, axis=1, keepdims=False)[0::2]
    mod = jnp.concatenate([mod_mine.reshape(N_MOD, d), jnp.zeros((8 - N_MOD, d), F32)], axis=0)
    w_in_all = w_in_whole.reshape(N_CHIPS, d, n_in)
    vecs = jnp.concatenate([conv_b, b_rg_a, b_rg_x, a_param, b_pool, pool_scale, jnp.zeros((2, d), F32)], axis=0)

    def norm_first(a_ref, extra_refs, out_refs):
        h = _modulated_norm(a_ref[...], extra_refs[0], extra_refs[1], 1, 0)
        out_refs[1][...] = h
        return h

    stage = _Stage()
    i_blk = [_add_gather_d2d(stage, g) for g in gate_wholes]
    i_sq = [_add_gather_ici(stage, halves(n)) for n in squares]
    down_rows = mine["w_down"].shape[0] // 8
    i_down = _add_gather_ici(stage, halves("w_down"), rows=(0, down_rows))
    (proj, h1), got = _matmul(x0, w_in_all, name="proj_in", tm=ts, extra=[(mod, vec8), (g_mix, vec1)], prepare=norm_first,
                              outs=[out_rows(N_CHIPS * n_in, F32), out_rows(d, BF16)], finish=_store_blocks(n_in), parts=halves_of_tile,                              stage=stage, **plain)
    w_rg_a_all, w_rg_x_all, w_pool_all = (blocks(got[i]) for i in i_blk)
    stage = _Stage()
    i_sq = [_add_gather_d2d(stage, got[i]) for i in i_sq]
    i_up = _add_gather_ici(stage, halves("w_up"))
    w_down_so_far = got[i_down]
    (xr, hr, ga, p, pooled), got = _seq_forward(proj, conv_full, vecs, w_rg_a_all, w_rg_x_all, w_pool_all, ts=tsq, stage=stage)
    w_a_all, w_b_all, w_out_all = (got[i].reshape(1, d, d) for i in i_sq)
    stage = _Stage()
    i_up = _add_gather_d2d(stage, got[i_up])
    i_down = _add_gather_ici(stage, halves("w_down"), rows=(down_rows, 2 * down_rows), whole=w_down_so_far)
    (merged, s_a, s_b, t_a, t_b), got = _branches(ga, pooled, proj, w_a_all, w_b_all, ts=ts, stage=stage)
    w_up_all = got[i_up].reshape(N_CHIPS, d, n_up)
    stage = _Stage()
    i_down = _add_gather_ici(stage, halves("w_down"), rows=(3 * down_rows, down_rows), whole=got[i_down])

    def residual_norm(k, acc, extra_refs, out_refs):
        x_ref, mod_ref, g_ref = extra_refs
        out_refs[0][...] = acc.astype(BF16)
        x2_t = x_ref[...] + mod_ref[2:3, :] * acc
        out_refs[1][...] = x2_t
        out_refs[2][...] = _modulated_norm(x2_t, mod_ref, g_ref, 4, 3)

    (mo, x2, h2), got = _matmul(merged, w_out_all, name="mix_out", tm=ts, extra=[(x0, rows_d), (mod, vec8), (g_mlp, vec1)],
                                outs=[out_rows(d, BF16), out_rows(d, F32), out_rows(d, BF16)], finish=residual_norm, parts=halves_of_tile,                                stage=stage, transposed=False, reduce_blocks=True)
    stage = _Stage()
    i_down = _add_gather_d2d(stage, got[i_down])

    def relu_squared(k, acc, extra_refs, out_refs):
        out_refs[0][:, k * n_up:(k + 1) * n_up] = jnp.square(jnp.maximum(acc, 0.0)).astype(BF16)

    (ff,), got = _matmul(h2, w_up_all, name="mlp_up", tm=ts, outs=[out_rows(D_FF, BF16)], finish=relu_squared, parts=halves_of_tile,
                         stage=stage, **plain)
    w_down_all = got[i_down].reshape(N_CHIPS, D_FF // N_CHIPS, d)

    def loss_head(k, acc, extra_refs, out_refs):
        x2_ref, mod_ref, g_ref, t_ref = extra_refs
        dx_ref, df_ref, acc_ref = out_refs
        xh, r = _rms(x2_ref[...] + mod_ref[5:6, :] * acc)
        err = xh * g_ref[...] - t_ref[...]
        dy = err / d
        dxh = dy * g_ref[...]
        dx3_t = r * (dxh - xh * jnp.mean(dxh * xh, axis=-1, keepdims=True))
        dx_ref[...] = dx3_t
        df_ref[...] = (dx3_t * mod_ref[5:6, :]).astype(BF16)
        _accumulate(acc_ref, 0, dy * xh)
        _accumulate(acc_ref, 1, dx3_t * acc)
        _accumulate(acc_ref, 2, jnp.square(err) * (0.5 / d))

    dx3, dffo, sums_head = _matmul(ff, w_down_all, name="mlp_down", tm=ts, extra=[(x2, rows_d), (mod, vec8), (g_fin, vec1), (target, rows_d)],
                                   outs=[out_rows(d, F32), out_rows(d, BF16), sums_out], finish=loss_head, zero_first=(2,), parts=halves_of_tile,                                   transposed=False, reduce_blocks=True)

    partial_of, from_sibling, chip_sum, from_chips, half_done, quarter = {}, {}, {}, {}, {}, {}

    def publish(n, g):
        g = _blocks_to_pieces(g).astype(BF16) if n in block_weights else g
        partial_of[n] = g.reshape(N_CHIPS, 2, -1, mine[n].shape[-1])

    def exchange(to_sibling=(), to_chips=(), swap=()):
        st = _Stage()
        slots = [(n, from_sibling, _add_reduce_d2d(st, partial_of[n])) for n in to_sibling]
        slots += [(n, from_chips, _add_reduce_ici(st, chip_sum[n])) for n in to_chips]
        slots += [(n, quarter, _add_swap_halves(st, half_done[n])) for n in swap]
        return st, slots

    def collect(slots, outs):
        for n, where, i in slots:
            where[n] = outs[i]

    def sum_pairs(*ns):
        for n in ns:
            chip_sum[n] = _sum_pair(partial_of[n], from_sibling[n], place, "sum_pair_" + n)

    def sum_quarters(*ns):
        for n in ns:
            half_done[n] = _sum_quarter(partial_of[n], from_sibling[n], from_chips[n], place, "sum_quarter_" + n)

    publish("w_down", wg(ff, dffo, nb=1, name="grad_w_down"))

    def relu_squared_backward(k, acc, extra_refs, out_refs):
        cols = slice(k * n_up, (k + 1) * n_up)
        out_refs[0][:, cols] = (acc * (2.0 * jnp.sqrt(extra_refs[0][:, cols].astype(F32)))).astype(BF16)

    stage, slots = exchange(to_sibling=["w_down"])
    (dup,), got = _matmul(dffo, w_down_all, name="d_mlp_down", tm=ts, extra=[(ff, _row_spec(ts, D_FF))], outs=[out_rows(D_FF, BF16)],
                          finish=relu_squared_backward, parts=halves_of_tile, stage=stage, **back)
    collect(slots, got)
    sum_pairs("w_down")

    def norm_mlp_backward(k, acc, extra_refs, out_refs):
        x2_ref, dres_ref, mod_ref, g_ref, mo_ref = extra_refs
        dx_ref, dmo_ref, acc_ref = out_refs
        dx =_modulated_norm_backward(acc, x2_ref[...], dres_ref[...], mod_ref, g_ref, acc_ref, 4)
        dx_ref[...] = dx
        dmo_ref[...] = (dx * mod_ref[2:3, :]).astype(BF16)
        _accumulate(acc_ref, 3, dx * mo_ref[...].astype(F32))

    stage, slots = exchange(to_chips=["w_down"])
    (dx2, dmo, sums_mlp), got = _matmul(dup, w_up_all, name="d_mlp_up", tm=ts, stage=stage, finish=norm_mlp_backward,
                                        extra=[(x2, rows_d), (dx3, rows_d), (mod, vec8), (g_mlp, vec1), (mo, rows_d)],
                                        outs=[out_rows(d, F32), out_rows(d, BF16), sums_out], zero_first=(2,), parts=halves_of_tile, **back_sum)
    collect(slots, got)
    sum_quarters("w_down")
    publish("w_up", wg(h2, dup, nb=N_CHIPS, name="grad_w_up"))
    publish("w_out", wg(merged, dmo, nb=1, name="grad_w_out"))

    def merge_backward(k, acc, extra_refs, out_refs):
        sa_ref, sb_ref, ta_ref, tb_ref = extra_refs
        out_refs[0][...] = (acc * sa_ref[...].astype(F32)).astype(BF16)
        out_refs[1][...] = (acc * sb_ref[...].astype(F32)).astype(BF16)
        out_refs[2][:, 0:d] = (acc * ta_ref[...].astype(F32)).astype(BF16)
        out_refs[2][:, d:2 * d] = (acc * tb_ref[...].astype(F32)).astype(BF16)

    stage, slots = exchange(to_sibling=["w_up", "w_out"], swap=["w_down"])
    (dbr_a, dbr_b, dgab), got = _matmul(dmo, w_out_all, name="d_mix_out", tm=ts, stage=stage, finish=merge_backward,
                                        extra=[(s_a, rows_d), (s_b, rows_d), (t_a, rows_d), (t_b, rows_d)],
                                        outs=[out_rows(d, BF16), out_rows(d, BF16), out_rows(2 * d, BF16)], parts=halves_of_tile, **back_sum)
    collect(slots, got)
    publish("w_a", wg(ga, dbr_a, nb=1, name="grad_w_branch_a"))
    publish("w_b", wg(pooled, dbr_b, nb=1, name="grad_w_branch_b"))
    dga, = _matmul(dbr_a, w_a_all, name="d_branch_a", tm=tw, outs=[out_rows(d, BF16, tw)], finish=_store_blocks(d), **back_sum)
    dpooled, = _matmul(dbr_b, w_b_all, name="d_branch_b", tm=tw, outs=[out_rows(d, BF16, tw)], finish=_store_blocks(d), **back_sum)
    sum_pairs("w_up", "w_out")
    stage, slots = exchange(to_sibling=["w_a", "w_b"], to_chips=["w_up", "w_out"])
    (dproj, sums_seq, d_rg_a, d_rg_x, d_pool), got = _seq_backward_by_block(proj, xr, hr, p, dga, dpooled, dgab, conv_full, vecs,
                                                                            w_rg_a_all, w_rg_x_all, w_pool_all, ts=tsq, stage=stage)
    collect(slots, got)
    sum_quarters("w_up", "w_out")
    sum_pairs("w_a", "w_b")
    for n, g in zip(block_weights, (d_rg_a, d_rg_x, d_pool)):
        publish(n, g)
    stage, slots = exchange(to_sibling=block_weights, to_chips=["w_a", "w_b"], swap=["w_up", "w_out"])
    g_in, got = wg(h1, dproj, nb=N_CHIPS, name="grad_w_in", stage=stage)
    collect(slots, got)
    publish("w_in", g_in)
    sum_pairs(*block_weights)
    sum_quarters("w_a", "w_b")
    stage, slots = exchange(to_sibling=["w_in"], to_chips=block_weights)
    collect(slots, _run_stage(stage, "exchange_w_in_gates"))
    sum_pairs("w_in")
    sum_quarters(*block_weights)

    def norm_mix_backward(k, acc, extra_refs, out_refs):
        x_ref, dres_ref, mod_ref, g_ref = extra_refs
        out_refs[0][...] = _modulated_norm_backward(acc, x_ref[...], dres_ref[...], mod_ref, g_ref, out_refs[1], 1)

    stage, slots = exchange(to_chips=["w_in"], swap=block_weights + ("w_a", "w_b"))
    (grad_x, sums_mix), got = _matmul(dproj, w_in_all, name="d_proj_in", tm=ts, stage=stage, finish=norm_mix_backward,
                                      extra=[(x0, rows_d), (dx2, rows_d), (mod, vec8), (g_mix, vec1)],
                                      outs=[out_rows(d, F32), sums_out], zero_first=(1,), parts=halves_of_tile, **back_sum)
    collect(slots, got)
    sum_quarters("w_in")

    dmod = jnp.concatenate([sums_mix[1:2], sums_mix[0:1], sums_mlp[3:4], sums_mlp[1:2], sums_mlp[0:1], sums_head[1:2]], axis=1)
    row = jnp.concatenate([sums_mix[2:3], sums_mlp[2:3], sums_seq[4:5], sums_seq[5:6], sums_seq[6:7], sums_seq[7:8], sums_seq[8:9],
                           sums_seq[9:10], sums_head[0:1], sums_seq[0:CONV_WIDTH].reshape(1, CONV_WIDTH * d), dmod, sums_head[2:3]], axis=1)
    n_vec, n_conv, n_mod = 9 * d, CONV_WIDTH * d, N_MOD * d
    stage, slots = exchange(swap=["w_in"])
    i_rows = _add_allgather8(stage, row)
    got = _run_stage(stage, "exchange_sums_w_in")
    collect(slots, got)
    rows_all = got[i_rows]
    total = _sum_rows(rows_all, d)
    loss = total[0, n_vec + n_conv + n_mod]
    dmod_all = rows_all[:, 0, n_vec + n_conv:n_vec + n_conv + n_mod]
    g_ada, d_ada, m_ada, v_ada = _ada_backward(act_all.T, lax.dynamic_slice_in_dim(dmod_all, k_me * ada_cols, ada_cols, axis=1),
                                               w_ada[0], m_w_ada[0], v_w_ada[0])
    g_conv = lax.dynamic_slice_in_dim(total[:, n_vec:n_vec + n_conv].reshape(CONV_WIDTH, d), k_me * conv_cols, conv_cols, axis=1)
    vec_names = ("norm_mix_g", "norm_mlp_g", "conv_b", "b_rg_a", "b_rg_x", "a_param", "b_pool", "pool_scale", "final_g", "conv_w", "b_ada")
    vec_params = [(norm_mix_g, m_norm_mix_g, v_norm_mix_g), (norm_mlp_g, m_norm_mlp_g, v_norm_mlp_g), (conv_b, m_conv_b, v_conv_b),
                  (b_rg_a, m_b_rg_a, v_b_rg_a), (b_rg_x, m_b_rg_x, v_b_rg_x), (a_param, m_a_param, v_a_param), (b_pool, m_b_pool, v_b_pool),
                  (pool_scale, m_pool_scale, v_pool_scale), tuple(a.reshape(1, d) for a in (final_g, m_final_g, v_final_g)),
                  (conv_w[0], m_conv_w[0], v_conv_w[0]), (b_ada, m_b_ada, v_b_ada)]
    offsets = [i * d for i in range(9)] + [None, n_vec + n_conv]
    vec_res = _adamw_vectors(total, offsets, [None] * 9 + [g_conv, None], vec_params)
    vec_out = {n: [r.reshape(shape) for r in res] for n, res, shape in zip(
        vec_names, vec_res, [(1, d)] * 8 + [(d,), (1, CONV_WIDTH, conv_cols), (1, N_MOD * d)])}

    big_out = {}
    for n in names:
        shape2 = (-1, mine[n].shape[-1])
        g2 = quarter[n].reshape(shape2)
        res = _adamw(mine[n].reshape(shape2), g2, moments_m[n].reshape(shape2), moments_v[n].reshape(shape2), "adamw_" + n)
        big_out[n] = [r.reshape((1,) + mine[n].shape) for r in res]

    key = {"w_in": "w_in", "w_rg_a": "rg_a", "w_rg_x": "rg_x", "w_branch_a": "w_a", "w_pool": "pool", "w_branch_b": "w_b", "w_out": "w_out",
           "w_up": "w_up", "w_down": "w_down"}
    order = ("norm_mix_g", "norm_mlp_g", "w_ada", "b_ada", "w_in", "conv_w", "conv_b", "w_rg_a", "b_rg_a", "w_rg_x", "b_rg_x", "a_param",
             "w_branch_a", "w_pool", "b_pool", "pool_scale", "w_branch_b", "w_out", "w_up", "w_down", "final_g")
    ada_out = [g_ada[None], d_ada[None], m_ada[None], v_ada[None]]
    outs = [loss, grad_x[None]]
    for kind in range(4):
        for nme in order:
            outs.append(ada_out[kind] if nme == "w_ada" else big_out[key[nme]][kind] if nme in key else vec_out[nme][kind])
    return tuple(outs)
```

```python
import functools

import jax
import jax.numpy as jnp
from jax import lax
from jax.experimental import pallas as pl
from jax.experimental.pallas import tpu as pltpu

F32, BF16 = jnp.float32, jnp.bfloat16
D_MODEL = 1024
D_FF = 4 * D_MODEL
N_BLOCKS = 4
BLOCK = D_MODEL // N_BLOCKS
CONV_WIDTH = 4
POOL_WINDOWS = (2, 4, 8, 16)
CONV_HALO = 8
POOL_HALO = 16
N_MOD = 6
EPS = 1e-6
C_RG = 8.0
ADAM_LR, ADAM_B1, ADAM_B2, ADAM_EPS, ADAM_WD, ADAM_STEP = 0.001, 0.9, 0.999, 1e-08, 0.01, 10
N_DEV = 8
N_CHIPS = 4
VMEM_LIMIT_V7X = 56 * 2**20
MESH = pl.DeviceIdType.MESH
SDS = jax.ShapeDtypeStruct
HBM_SPEC = pl.BlockSpec(memory_space=pltpu.HBM)
VMEM_SPEC = pl.BlockSpec(memory_space=pltpu.VMEM)


def _params(*semantics):
    return pltpu.CompilerParams(dimension_semantics=semantics, vmem_limit_bytes=VMEM_LIMIT_V7X)


def _coords():
    return lax.axis_index("x"), lax.axis_index("y"), lax.axis_index("c")


def _flip(v, bit):
    return 1 - v if bit else v


def _chips():
    x, y, _ = _coords()
    return [(1 - x, y), (x, 1 - y), (1 - x, 1 - y)]


PEER_SETS = [frozenset(s) for s in (("sibling",), ("chips",), ("sibling", "chips"), ("sibling", "chips", "others"))]


class _Stage:
    def __init__(self):
        self.inputs, self.in_specs, self.outputs, self.out_specs, self.aliases = [], [], [], [], {}
        self.parts, self.n_copies, self.peers = [], 0, set()

    def add(self, inputs, in_spec, outputs, out_spec, n_copies, build, peers, alias=False):
        self.peers |= set(peers)
        i0, o0 = len(self.inputs), len(self.outputs)
        self.inputs += inputs
        self.in_specs += [in_spec] * len(inputs)
        self.outputs += outputs
        self.out_specs += [out_spec] * len(outputs)
        if alias:
            self.aliases.update({i0 + i: o0 + i for i in range(len(inputs))})
        self.parts.append((build, i0, len(inputs), o0, len(outputs)))
        self.n_copies += n_copies
        return list(range(o0, o0 + len(outputs)))

    def copies(self, in_refs, out_refs):
        out = []
        for build, i0, ni, o0, no in self.parts:
            out += build(in_refs[i0:i0 + ni], out_refs[o0:o0 + no])
        assert len(out) == self.n_copies
        return out

    def run(self, in_refs, out_refs, send_sems, recv_sems, start):
        x, y, c = _coords()
        for s, (src, dst, to, landing) in enumerate(self.copies(in_refs, out_refs)):
            if to is None:
                cp = pltpu.make_async_copy(src, dst, send_sems.at[s])
                cp.start() if start else cp.wait()
                continue
            cp = pltpu.make_async_remote_copy(src_ref=src, dst_ref=dst, send_sem=send_sems.at[s], recv_sem=recv_sems.at[s],
                                              device_id=to, device_id_type=MESH)
            if start:
                cp.start()
            else:
                pltpu.make_async_remote_copy(src_ref=landing, dst_ref=landing, send_sem=send_sems.at[s], recv_sem=recv_sems.at[s],
                                             device_id=(x, y, c), device_id_type=MESH).wait_recv()
                cp.wait_send()


def _hosted(stage, body, *, name, in_specs, out_specs, out_shape, grid=(), scratch_shapes=(), compiler_params=None):
    if stage is None:
        return pl.pallas_call(body, name=name, grid=grid, in_specs=in_specs, out_specs=out_specs, out_shape=out_shape,
                              scratch_shapes=list(scratch_shapes), compiler_params=compiler_params)
    single = not isinstance(out_shape, (list, tuple))
    h_out_shape = [out_shape] if single else list(out_shape)
    h_out_specs = [out_specs] if single else list(out_specs)
    n_in, n_out, n_scr = len(in_specs), len(h_out_shape), len(scratch_shapes)
    s_in, s_out = len(stage.inputs), len(stage.outputs)

    def wrapped(*refs):
        h_in, st_in = refs[:n_in], refs[n_in:n_in + s_in]
        h_o, st_o = refs[n_in + s_in:n_in + s_in + n_out], refs[n_in + s_in + n_out:n_in + s_in + n_out + s_out]
        h_scr = refs[n_in + s_in + n_out + s_out:n_in + s_in + n_out + s_out + n_scr]
        send_sems, recv_sems = refs[n_in + s_in + n_out + s_out + n_scr:]
        def begin():
            x, y, c = _coords()
            peers = [(x, y, 1 - c)] * ("sibling" in stage.peers)
            peers += [(px, py, c) for px, py in _chips()] * ("chips" in stage.peers)
            peers += [(px, py, 1 - c) for px, py in _chips()] * ("others" in stage.peers)
            barrier = pltpu.get_barrier_semaphore()
            for peer in peers:
                pl.semaphore_signal(barrier, inc=1, device_id=peer, device_id_type=MESH)
            pl.semaphore_wait(barrier, len(peers))
            stage.run(st_in, st_o, send_sems, recv_sems, True)

        if not grid:
            begin()
            if body is not None:
                body(*h_in, *h_o, *h_scr)
            stage.run(st_in, st_o, send_sems, recv_sems, False)
            return
        ids = [pl.program_id(a) for a in range(len(grid))]
        first = functools.reduce(jnp.logical_and, [i == 0 for i in ids])
        last = functools.reduce(jnp.logical_and, [i == g - 1 for i, g in zip(ids, grid)])
        pl.when(first)(begin)
        body(*h_in, *h_o, *h_scr)
        pl.when(last)(lambda: stage.run(st_in, st_o, send_sems, recv_sems, False))

    call = pl.pallas_call(
        wrapped, name=name, grid=grid, in_specs=list(in_specs) + stage.in_specs, out_specs=h_out_specs + stage.out_specs,
        out_shape=h_out_shape + stage.outputs, input_output_aliases={n_in + i: n_out + o for i, o in stage.aliases.items()},
        scratch_shapes=list(scratch_shapes) + [pltpu.SemaphoreType.DMA((stage.n_copies,)), pltpu.SemaphoreType.DMA((stage.n_copies,))],
        compiler_params=pltpu.CompilerParams(dimension_semantics=("arbitrary",) * len(grid), vmem_limit_bytes=VMEM_LIMIT_V7X,
                                             collective_id=PEER_SETS.index(frozenset(stage.peers))),
    )

    def run(*operands):
        outs = call(*operands, *stage.inputs)
        host = outs[:n_out]
        return (host[0] if single else host), outs[n_out:]

    return run


def _run_stage(stage, name):
    return _hosted(stage, None, name=name, in_specs=[], out_specs=[], out_shape=[])()[1]


def _add_allgather8(stage, v):
    def build(ins, outs):
        x, y, c = _coords()
        me = 4 * x + 2 * y + c
        copies = [(ins[0], outs[0].at[me], None, None)]
        for r in range(1, N_DEV):
            peer = (_flip(x, r & 4), _flip(y, r & 2), _flip(c, r & 1))
            copies.append((ins[0], outs[0].at[me], peer, outs[0].at[me ^ r]))
        return copies

    return stage.add([v], VMEM_SPEC, [SDS((N_DEV,) + v.shape, v.dtype)], VMEM_SPEC, N_DEV, build, ("sibling", "chips", "others"))[0]


def _add_gather_ici(stage, shard, rows=None, whole=None):
    first, count = rows or (0, shard.shape[1])

    def build(ins, outs):
        x, y, c = _coords()
        k_me = 2 * x + y
        part = pl.ds(first, count)
        copies = [] if whole is not None else [(ins[0], outs[0].at[k_me], (x, y, 1 - c), outs[0].at[k_me])]
        for px, py in _chips():
            copies.append((ins[0].at[c, part], outs[0].at[k_me, c, part], (px, py, c), outs[0].at[2 * px + py, c, part]))
        return copies

    if whole is None:
        return stage.add([shard], HBM_SPEC, [SDS((N_CHIPS,) + shard.shape, shard.dtype)], HBM_SPEC, N_CHIPS, build, ("sibling", "chips"))[0]

    def build_into(ins, outs):
        return build(ins[1:], outs)

    i0 = len(stage.inputs)
    out = stage.add([whole, shard], HBM_SPEC, [SDS(whole.shape, whole.dtype)], HBM_SPEC, N_CHIPS - 1, build_into, ("chips",))[0]
    stage.aliases[i0] = out
    return out


def _add_gather_d2d(stage, whole):
    def build(ins, outs):
        x, y, c = _coords()
        return [(outs[0].at[2 * px + py, c], outs[0].at[2 * px + py, c], (x, y, 1 - c), outs[0].at[2 * px + py, 1 - c]) for px, py in _chips()]

    return stage.add([whole], HBM_SPEC, [SDS(whole.shape, whole.dtype)], HBM_SPEC, N_CHIPS - 1, build, ("sibling",), alias=True)[0]


def _add_reduce_d2d(stage, grads):
    def build(ins, outs):
        x, y, c = _coords()
        return [(ins[0].at[k, 1 - c], outs[0].at[k], (x, y, 1 - c), outs[0].at[k]) for k in range(N_CHIPS)]

    return stage.add([grads], HBM_SPEC, [SDS((N_CHIPS,) + grads.shape[2:], grads.dtype)], HBM_SPEC, N_CHIPS, build, ("sibling",))[0]


def _add_reduce_ici(stage, partial):
    def build(ins, outs):
        x, y, c = _coords()
        return [(ins[0].at[2 * px + py], outs[0].at[j], (px, py, c), outs[0].at[j]) for j, (px, py) in enumerate(_chips())]

    return stage.add([partial], HBM_SPEC, [SDS((N_CHIPS - 1,) + partial.shape[1:], partial.dtype)], HBM_SPEC, N_CHIPS - 1, build, ("chips",))[0]


def _add_swap_halves(stage, quarter):
    def build(ins, outs):
        x, y, c = _coords()
        return [(outs[0].at[c], outs[0].at[c], (x, y, 1 - c), outs[0].at[1 - c])]

    return stage.add([quarter], HBM_SPEC, [SDS(quarter.shape, quarter.dtype)], HBM_SPEC, 1, build, ("sibling",), alias=True)[0]


def _sum_pair(grads, from_sibling, place, name):
    _, _, rows, cols = grads.shape
    tr = min(rows, 512)

    def body(place_ref, g_ref, s_ref, o_ref):
        o_ref[0] = (g_ref[0, 0].astype(F32) + s_ref[0].astype(F32)).astype(BF16)

    spec = pl.BlockSpec((1, tr, cols), lambda k, i, place_ref: (k, i, 0))
    return pl.pallas_call(
        body, name=name, out_shape=SDS((N_CHIPS, rows, cols), BF16),
        grid_spec=pltpu.PrefetchScalarGridSpec(
            num_scalar_prefetch=1, grid=(N_CHIPS, rows // tr),
            in_specs=[pl.BlockSpec((1, 1, tr, cols), lambda k, i, place_ref: (k, place_ref[0], i, 0)), spec], out_specs=spec),
        compiler_params=_params("parallel", "parallel"),
    )(place, grads, from_sibling)


def _sum_quarter(grads, from_sibling, from_chips, place, name):
    _, _, rows, cols = grads.shape
    tr = min(rows, 512)

    def body(place_ref, g_ref, s_ref, c_ref, o_ref):
        acc = g_ref[0, 0].astype(F32) + s_ref[0].astype(F32)
        for j in range(N_CHIPS - 1):
            acc = acc + c_ref[j].astype(F32)
        o_ref[0] = acc

    return pl.pallas_call(
        body, name=name, out_shape=SDS((2, rows, cols), F32),
        grid_spec=pltpu.PrefetchScalarGridSpec(
            num_scalar_prefetch=1, grid=(rows // tr,),
            in_specs=[pl.BlockSpec((1, 1, tr, cols), lambda i, place_ref: (place_ref[1], place_ref[0], i, 0)),
                      pl.BlockSpec((1, tr, cols), lambda i, place_ref: (place_ref[1], i, 0)),
                      pl.BlockSpec((N_CHIPS - 1, tr, cols), lambda i, place_ref: (0, i, 0))],
            out_specs=pl.BlockSpec((1, tr, cols), lambda i, place_ref: (place_ref[0], i, 0))),
        compiler_params=_params("parallel"),
    )(place, grads, from_sibling, from_chips)


def _row_spec(ts, cols, col_block=0):
    return pl.BlockSpec((ts, cols), lambda *g: (g[0], col_block))


def _vec_spec(rows, cols):
    return pl.BlockSpec((rows, cols), lambda *g: (0, 0))


def _matmul(a, w, *, transposed, reduce_blocks, name, tm, outs, finish, extra=(), prepare=None, zero_first=(), parts=1, stage=None):
    m_rows = a.shape[0]
    nb, r, c = w.shape
    kb = c if transposed else r
    dims = (((1,), (1,)), ((), ())) if transposed else (((1,), (0,)), ((), ()))
    assert a.shape[1] == (nb * kb if reduce_blocks else kb) and m_rows % tm == 0 and tm % parts == 0
    n_extra, sub = len(extra), tm // parts

    def body(a_ref, w_ref, *rest):
        extra_refs, out_refs = rest[:n_extra], rest[n_extra:]
        if zero_first:
            @pl.when(pl.program_id(0) == 0)
            def _():
                for i in zero_first:
                    out_refs[i][...] = jnp.zeros_like(out_refs[i])

        def views(j):
            rows = pl.ds(j * sub, sub)
            return rows, [ref.at[rows] if ref.shape[0] == tm else ref for ref in extra_refs], [ref.at[rows] if ref.shape[0] == tm else ref for ref in out_refs]

        def product(j):
            rows, ex, ou = views(j)
            if reduce_blocks:
                acc = lax.dot_general(a_ref[rows, 0:kb], w_ref[0], dims, preferred_element_type=F32)
                for k in range(1, nb):
                    acc += lax.dot_general(a_ref[rows, k * kb:(k + 1) * kb], w_ref[k], dims, preferred_element_type=F32)
                return [(None, acc)]
            lhs = a_ref[rows, :] if prepare is None else prepare(a_ref.at[rows], ex, ou)
            return [(k, lax.dot_general(lhs, w_ref[k], dims, preferred_element_type=F32)) for k in range(nb)]

        pending = product(0)
        for j in range(parts):
            ahead = product(j + 1) if j + 1 < parts else None
            _, ex, ou = views(j)
            for k, acc in pending:
                finish(k, acc, ex, ou)
            pending = ahead

    return _hosted(
        stage, body, name=name, grid=(m_rows // tm,), out_shape=[s for s, _ in outs], out_specs=[s for _, s in outs],
        in_specs=[_row_spec(tm, a.shape[1]), pl.BlockSpec((nb, r, c), lambda *g: (0, 0, 0), pipeline_mode=pl.Buffered(1))] + [s for _, s in extra],
        compiler_params=_params("arbitrary"),
    )(a, w, *[e for e, _ in extra])


def _store_blocks(n):
    def finish(k, acc, extra_refs, out_refs):
        if k is None:
            out_refs[0][...] = acc.astype(out_refs[0].dtype)
        else:
            out_refs[0][:, k * n:(k + 1) * n] = acc.astype(out_refs[0].dtype)
    return finish


def _weight_grad(a, b, *, nb, name, tk, stage=None):
    s_rows, k1 = a.shape
    tn = b.shape[1] // nb
    assert k1 % tk == 0 and b.shape[1] % nb == 0

    def body(a_ref, b_ref, o_ref):
        o_ref[0] = lax.dot_general(a_ref[...], b_ref[...], (((0,), (0,)), ((), ())), preferred_element_type=F32).astype(o_ref.dtype)

    return _hosted(
        stage, body, name=name, grid=(nb, k1 // tk), out_shape=SDS((nb, k1, tn), BF16),
        in_specs=[pl.BlockSpec((s_rows, tk), lambda n, i: (0, i)), pl.BlockSpec((s_rows, tn), lambda n, i: (0, n))],
        out_specs=pl.BlockSpec((1, tk, tn), lambda n, i: (n, i, 0)), compiler_params=_params("parallel", "parallel"),
    )(a, b)


def _rms(x):
    r = lax.rsqrt(jnp.mean(x * x, axis=-1, keepdims=True) + EPS)
    return x * r, r


def _accumulate(acc_ref, row, value):
    acc_ref[row:row + 1, :] += jnp.sum(value, axis=0, keepdims=True)


def _gelu_parts(y):
    k0, k1 = 0.7978845608028654, 0.044715
    th = jnp.tanh(k0 * (y + k1 * (y * y * y)))
    gelu = 0.5 * y * (1.0 + th)
    dgelu = 0.5 * (1.0 + th) + 0.5 * y * (1.0 - th * th) * (k0 * (1.0 + 3.0 * k1 * (y * y)))
    return gelu, dgelu


def _sigmoid(z):
    return 0.5 * jnp.tanh(0.5 * z) + 0.5


def _one_minus_square(a, log_a):
    return -jnp.tanh(log_a) * (1.0 + a * a)


def _modulated_norm(x, mod_ref, g_ref, sc_row, sh_row):
    xh, _ = _rms(x)
    return ((xh * g_ref[...]) * (1.0 + mod_ref[sc_row:sc_row + 1, :]) + mod_ref[sh_row:sh_row + 1, :]).astype(BF16)


def _modulated_norm_backward(dh, x, dres, mod_ref, g_ref, acc_ref, sc_row):
    xh, r = _rms(x)
    _accumulate(acc_ref, 0, dh * (xh * g_ref[...]))
    _accumulate(acc_ref, 1, dh)
    dn = dh * (1.0 + mod_ref[sc_row:sc_row + 1, :])
    _accumulate(acc_ref, 2, dn * xh)
    dxh = dn * g_ref[...]
    return dres + r * (dxh - xh * jnp.mean(dxh * xh, axis=-1, keepdims=True))


def _seq_forward(proj, conv_w, vecs, w_a, w_x, w_p, *, ts, stage=None):
    s_rows = proj.shape[0]
    d = D_MODEL

    def body(x_ref, xh_ref, y_ref, u_ref, uh_ref, cw_ref, vec_ref, wa_ref, wx_ref, wp_ref,
             xr_o, hr_o, ga_o, p_o, pooled_o, carry, a_scr, b_scr):
        t = pl.program_id(0)

        @pl.when(t == 0)
        def _():
            carry[...] = jnp.zeros_like(carry)

        rows = lax.broadcasted_iota(jnp.int32, (ts, BLOCK), 0)
        start = (rows == 0) & (t == 0)
        for g in range(N_BLOCKS):
            lanes = slice(g * BLOCK, (g + 1) * BLOCK)
            x = x_ref[:, lanes]
            xx = jnp.concatenate([jnp.where(t == 0, 0.0, xh_ref[:, lanes]), x], axis=0)
            xr = vec_ref[0:1, lanes] + x * cw_ref[CONV_WIDTH - 1:CONV_WIDTH, lanes]
            for j in range(CONV_WIDTH - 1):
                xr = xr + pltpu.roll(xx, CONV_WIDTH - 1 - j, 0)[CONV_HALO:] * cw_ref[j:j + 1, lanes]
            xr_o[:, lanes] = xr
            xrb = xr.astype(BF16)
            r = _sigmoid(jnp.dot(xrb, wa_ref[g], preferred_element_type=F32) + vec_ref[1:2, lanes])
            gate_i = _sigmoid(jnp.dot(xrb, wx_ref[g], preferred_element_type=F32) + vec_ref[2:3, lanes])
            log_a = (-C_RG * r) * jax.nn.softplus(vec_ref[3:4, lanes])
            a = jnp.exp(log_a)
            a_scr[:, lanes] = a
            b_scr[:, lanes] = xr * gate_i * jnp.where(start, 1.0, jnp.sqrt(_one_minus_square(a, log_a)))

        sub = lax.broadcasted_iota(jnp.int32, (8, d), 0)

        def chunk(i, h):
            a = a_scr[pl.ds(i * 8, 8), :]
            b = b_scr[pl.ds(i * 8, 8), :]
            for s in (1, 2, 4):
                keep = sub >= s
                b = jnp.where(keep, a * pltpu.roll(b, s, 0) + b, b)
                a = jnp.where(keep, a * pltpu.roll(a, s, 0), a)
            hh = a * h + b
            hr_o[pl.ds(i * 8, 8), :] = hh
            return hh[7:8, :]

        carry[...] = lax.fori_loop(0, ts // 8, chunk, carry[...])
        pos = (rows + t * ts + 1).astype(F32)
        for g, win in enumerate(POOL_WINDOWS):
            lanes = slice(g * BLOCK, (g + 1) * BLOCK)
            gelu, _ = _gelu_parts(y_ref[:, lanes])
            ga_o[:, lanes] = (gelu * hr_o[:, lanes]).astype(BF16)
            u = u_ref[:, lanes]
            sums, have = jnp.concatenate([jnp.where(t == 0, 0.0, uh_ref[:, lanes]), u], axis=0), 1
            while have < win:
                sums = sums + pltpu.roll(sums, have, 0)
                have *= 2
            pb = (sums[POOL_HALO:] * (1.0 / jnp.minimum(pos, float(win))) - u).astype(BF16)
            p_o[:, lanes] = pb
            mixed = jnp.dot(pb, wp_ref[g], preferred_element_type=F32) + vec_ref[4:5, lanes]
            pooled_o[:, lanes] = (mixed * vec_ref[5:6, lanes]).astype(BF16)

    halo_x = pl.BlockSpec((CONV_HALO, d), lambda t: (jnp.maximum(t * (ts // CONV_HALO) - 1, 0), 0))
    halo_u = pl.BlockSpec((POOL_HALO, d), lambda t: (jnp.maximum(t * (ts // POOL_HALO) - 1, 0), 2))
    wspec = pl.BlockSpec((N_BLOCKS, BLOCK, BLOCK), lambda t: (0, 0, 0))
    return _hosted(
        stage, body, name="seq_forward", grid=(s_rows // ts,),
        out_shape=[SDS((s_rows, d), F32), SDS((s_rows, d), F32), SDS((s_rows, d), BF16), SDS((s_rows, d), BF16), SDS((s_rows, d), BF16)],
        in_specs=[_row_spec(ts, d, 0), halo_x, _row_spec(ts, d, 1), _row_spec(ts, d, 2), halo_u, _vec_spec(CONV_WIDTH, d), _vec_spec(8, d),
                  wspec, wspec, wspec],
        out_specs=[_row_spec(ts, d)] * 5,
        scratch_shapes=[pltpu.VMEM((1, d), F32), pltpu.VMEM((ts, d), F32), pltpu.VMEM((ts, d), F32)],
        compiler_params=_params("arbitrary"),
    )(proj, proj, proj, proj, proj, conv_w, vecs, w_a, w_x, w_p)


def _branches(ga, pooled, proj, w_a, w_b, *, ts, stage=None):
    s_rows, d = ga.shape

    def body(a_ref, p_ref, ga_ref, gb_ref, wa_ref, wb_ref, merged_o, sa_o, sb_o, ta_o, tb_o):
        bra = jnp.dot(a_ref[...], wa_ref[0], preferred_element_type=F32)
        brb = jnp.dot(p_ref[...], wb_ref[0], preferred_element_type=F32)
        sa, sb = _sigmoid(ga_ref[...]), _sigmoid(gb_ref[...])
        merged_o[...] = (sa * bra + sb * brb).astype(BF16)
        sa_o[...] = sa.astype(BF16)
        sb_o[...] = sb.astype(BF16)
        ta_o[...] = (bra * (sa * (1.0 - sa))).astype(BF16)
        tb_o[...] = (brb * (sb * (1.0 - sb))).astype(BF16)

    rows = _row_spec(ts, d)
    wspec = pl.BlockSpec((1, d, d), lambda *g: (0, 0, 0), pipeline_mode=pl.Buffered(1))
    return _hosted(
        stage, body, name="branches", grid=(s_rows // ts,), out_shape=[SDS((s_rows, d), BF16)] * 5,
        in_specs=[rows, rows, _row_spec(ts, d, 3), _row_spec(ts, d, 4), wspec, wspec], out_specs=[rows] * 5,
        compiler_params=_params("arbitrary"),
    )(ga, pooled, proj, proj, w_a, w_b)


def _seq_backward_by_block(proj, xr, hr, p, dga, dpooled, dgab, conv_w, vecs, w_a, w_x, w_p, *, ts, stage=None):
    s_rows = proj.shape[0]
    d = D_MODEL
    n_t = s_rows // ts
    nn, nt, tn = (((1,), (0,)), ((), ())), (((1,), (1,)), ((), ())), (((0,), (0,)), ((), ()))

    def dot(lhs, rhs, dims):
        return lax.dot_general(lhs, rhs, dims, preferred_element_type=F32)

    def body(x_ref, y_ref, xr_ref, hr_ref, hh_ref, p_ref, dga_ref, dpl_ref, dgab_ref, cw_ref, vec_ref, wa_ref, wx_ref, wp_ref,
             dproj_o, acc_o, dwa_o, dwx_o, dwp_o, g_carry, dxr_carry, q_carry, a_scr, b_scr, g_scr, a_keep, r_keep, i_keep, m_keep):
        step = pl.program_id(0)
        t = n_t - 1 - step

        @pl.when(step == 0)
        def _():
            for ref in (acc_o, dwa_o, dwx_o, dwp_o, g_carry, dxr_carry, q_carry):
                ref[...] = jnp.zeros_like(ref)

        rows = lax.broadcasted_iota(jnp.int32, (ts, BLOCK), 0)
        start = (rows == 0) & (t == 0)

        def add_sum(row, lanes, value):
            acc_o[row:row + 1, lanes] += jnp.sum(value, axis=0, keepdims=True)

        for g in range(N_BLOCKS):
            lanes = slice(g * BLOCK, (g + 1) * BLOCK)
            xrb = xr_ref[:, lanes].astype(BF16)
            r = _sigmoid(dot(xrb, wa_ref[g], nn) + vec_ref[1:2, lanes])
            gate_i = _sigmoid(dot(xrb, wx_ref[g], nn) + vec_ref[2:3, lanes])
            log_a = (-C_RG * r) * jax.nn.softplus(vec_ref[3:4, lanes])
            a = jnp.exp(log_a)
            a_keep[:, lanes], r_keep[:, lanes], i_keep[:, lanes], m_keep[:, lanes] = a, r, gate_i, _one_minus_square(a, log_a)
            gelu, dgelu = _gelu_parts(y_ref[:, lanes])
            dga_t = dga_ref[:, lanes].astype(F32)
            dproj_o[:, d + g * BLOCK:d + (g + 1) * BLOCK] = (dga_t * hr_ref[:, lanes] * dgelu).astype(BF16)
            a_scr[:, lanes] = jnp.where(rows == ts - 1, 1.0, pltpu.roll(a, ts - 1, 0))
            b_scr[:, lanes] = dga_t * gelu

        sub = lax.broadcasted_iota(jnp.int32, (8, d), 0)

        def chunk(i, g_next):
            at = pl.multiple_of((ts // 8 - 1 - i) * 8, 8)
            aa = a_scr[pl.ds(at, 8), :]
            bb = b_scr[pl.ds(at, 8), :]
            for s in (1, 2, 4):
                keep = sub < 8 - s
                bb = jnp.where(keep, bb + aa * pltpu.roll(bb, 8 - s, 0), bb)
                aa = jnp.where(keep, aa * pltpu.roll(aa, 8 - s, 0), aa)
            gg = aa * g_next + bb
            g_scr[pl.ds(at, 8), :] = gg
            return gg[0:1, :]

        g_first = lax.fori_loop(0, ts // 8, chunk, g_carry[...])
        g_carry[...] = a_keep[0:1, :] * g_first

        pos = (rows + t * ts + 1).astype(F32)
        for g, win in enumerate(POOL_WINDOWS):
            lanes = slice(g * BLOCK, (g + 1) * BLOCK)
            gs, xr, a, r, gate_i, m_square = g_scr[:, lanes], xr_ref[:, lanes], a_keep[:, lanes], r_keep[:, lanes], i_keep[:, lanes], m_keep[:, lanes]
            sp = jax.nn.softplus(vec_ref[3:4, lanes])
            mult = jnp.where(start, 1.0, jnp.sqrt(m_square))
            h_before = jnp.where(t == 0, 0.0, hh_ref[CONV_HALO - 1:CONV_HALO, lanes])
            h_prev = jnp.where(rows == 0, h_before, pltpu.roll(hr_ref[:, lanes], 1, 0))
            gm = gs * mult
            dxr = gm * gate_i
            d_i = gm * xr
            d_mult = (gs * xr) * gate_i
            dlog_a = (gs * h_prev) * a - jnp.where(start, 0.0, d_mult * (a * a) * lax.rsqrt(m_square))
            dzr = (dlog_a * (-C_RG * sp)) * (r * (1.0 - r))
            dzi = d_i * (gate_i * (1.0 - gate_i))
            add_sum(5, lanes, dzr)
            add_sum(6, lanes, dzi)
            acc_o[7:8, lanes] += jnp.sum(dlog_a * r, axis=0, keepdims=True) * (-C_RG * jax.nn.sigmoid(vec_ref[3:4, lanes]))
            xrb, dzrb, dzib = xr.astype(BF16), dzr.astype(BF16), dzi.astype(BF16)
            dwa_o[g] += dot(xrb, dzrb, tn)
            dwx_o[g] += dot(xrb, dzib, tn)
            dxr = dxr + dot(dzrb, wa_ref[g], nt) + dot(dzib, wx_ref[g], nt)

            x = x_ref[:, lanes]
            ext = jnp.concatenate([dxr, dxr_carry[:, lanes]], axis=0)
            dx = dxr * cw_ref[CONV_WIDTH - 1:CONV_WIDTH, lanes]
            add_sum(CONV_WIDTH - 1, lanes, x * dxr)
            for j in range(CONV_WIDTH - 1):
                ahead = pltpu.roll(ext, ts + CONV_HALO - (CONV_WIDTH - 1 - j), 0)[:ts]
                dx = dx + ahead * cw_ref[j:j + 1, lanes]
                add_sum(j, lanes, x * ahead)
            add_sum(4, lanes, dxr)
            dxr_carry[:, lanes] = dxr[0:CONV_HALO, :]
            dproj_o[:, lanes] = dx.astype(BF16)

            pb = p_ref[:, lanes]
            mixed = dot(pb, wp_ref[g], nn) + vec_ref[4:5, lanes]
            dpl = dpl_ref[:, lanes].astype(F32)
            add_sum(9, lanes, dpl * mixed)
            dmixed = dpl * vec_ref[5:6, lanes]
            add_sum(8, lanes, dmixed)
            dmb = dmixed.astype(BF16)
            dwp_o[g] += dot(pb, dmb, tn)
            dp = dot(dmb, wp_ref[g], nt)
            q = dp * (1.0 / jnp.minimum(pos, float(win)))
            sums, have = jnp.concatenate([q, q_carry[:, lanes]], axis=0), 1
            while have < win:
                sums = sums + pltpu.roll(sums, ts + POOL_HALO - have, 0)
                have *= 2
            q_carry[:, lanes] = q[0:POOL_HALO, :]
            dproj_o[:, 2 * d + g * BLOCK:2 * d + (g + 1) * BLOCK] = (sums[:ts] - dp).astype(BF16)

        dproj_o[:, 3 * d:5 * d] = dgab_ref[...]

    def rev(cols, col_block=0):
        return pl.BlockSpec((ts, cols), lambda i: (n_t - 1 - i, col_block))

    halo_h = pl.BlockSpec((CONV_HALO, d), lambda i: (jnp.maximum((n_t - 1 - i) * (ts // CONV_HALO) - 1, 0), 0))
    wspec = pl.BlockSpec((N_BLOCKS, BLOCK, BLOCK), lambda i: (0, 0, 0))
    return _hosted(
        stage, body, name="seq_backward", grid=(n_t,),
        out_shape=[SDS((s_rows, 5 * d), BF16), SDS((16, d), F32)] + [SDS((N_BLOCKS, BLOCK, BLOCK), F32)] * 3,
        in_specs=[rev(d, 0), rev(d, 1), rev(d), rev(d), halo_h, rev(d), rev(d), rev(d), rev(2 * d), _vec_spec(CONV_WIDTH, d), _vec_spec(8, d),
                  wspec, wspec, wspec],
        out_specs=[rev(5 * d), _vec_spec(16, d), wspec, wspec, wspec],
        scratch_shapes=[pltpu.VMEM((1, d), F32), pltpu.VMEM((CONV_HALO, d), F32), pltpu.VMEM((POOL_HALO, d), F32)] + [pltpu.VMEM((ts, d), F32)] * 7,
        compiler_params=_params("arbitrary"),
    )(proj, proj, xr, hr, hr, p, dga, dpooled, dgab, conv_w, vecs, w_a, w_x, w_p)


def _adamw_math(w, g, m, v):
    m = ADAM_B1 * m + (1.0 - ADAM_B1) * g
    v = ADAM_B2 * v + (1.0 - ADAM_B2) * jnp.square(g)
    m_hat = m / (1.0 - ADAM_B1 ** ADAM_STEP)
    v_hat = v / (1.0 - ADAM_B2 ** ADAM_STEP)
    delta = -ADAM_LR * (m_hat / (jnp.sqrt(v_hat) + ADAM_EPS) + ADAM_WD * w)
    return delta, m, v


def _adamw(w, g, m, v, name):
    rows, cols = w.shape
    tr = min(rows, 512)

    def body(w_ref, g_ref, m_ref, v_ref, g_out, d_ref, mo_ref, vo_ref):
        g = g_ref[...]
        g_out[...] = g
        d_ref[...], mo_ref[...], vo_ref[...] = _adamw_math(w_ref[...], g, m_ref[...], v_ref[...])

    spec = pl.BlockSpec((tr, cols), lambda i: (i, 0))
    return pl.pallas_call(
        body, name=name, grid=(rows // tr,), out_shape=[SDS((rows, cols), F32)] * 4, in_specs=[spec] * 4, out_specs=[spec] * 4,
        compiler_params=_params("parallel"),
    )(w, g, m, v)


def _prologue(c, conv_shard, w_ada, b_cols, w_in_halves, gate_halves):
    n_cols = w_ada.shape[1]
    n_g = len(gate_halves)
    half_rows = w_in_halves.shape[1] // 2
    n_sems = 3 * N_DEV + N_CHIPS * n_g + 5 + 2 + 3

    def body(c_ref, conv_ref, w_ref, b_ref, win_ref, *rest):
        gate_refs = rest[:n_g]
        act_o, mod_o, conv_o, win_o = rest[n_g:n_g + 4]
        gate_o = rest[n_g + 4:2 * n_g + 4]
        c_all, mod_cols, send_sems, recv_sems = rest[2 * n_g + 4:]
        x, y, c = _coords()
        me, k_me = 4 * x + 2 * y + c, 2 * x + y
        sibling, x_nbr, y_nbr = (x, y, 1 - c), (1 - x, y, c), (x, 1 - y, c)
        k_x, k_y, k_d = 2 * (1 - x) + y, 2 * x + (1 - y), 2 * (1 - x) + (1 - y)
        counter = iter(range(n_sems))

        def copy(src, dst, to, landing):
            s = next(counter)
            out = pltpu.make_async_remote_copy(src_ref=src, dst_ref=dst, send_sem=send_sems.at[s], recv_sem=recv_sems.at[s],
                                               device_id=to, device_id_type=MESH)
            arrival = pltpu.make_async_remote_copy(src_ref=landing, dst_ref=landing, send_sem=send_sems.at[s], recv_sem=recv_sems.at[s],
                                                   device_id=(x, y, c), device_id_type=MESH)
            return out, arrival

        def allgather(src, out):
            s = next(counter)
            own = pltpu.make_async_copy(src, out.at[me], send_sems.at[s])
            peers = [copy(src, out.at[me], (_flip(x, r & 4), _flip(y, r & 2), _flip(c, r & 1)), out.at[me ^ r]) for r in range(1, N_DEV)]
            return own, peers

        def start(group):
            own, peers = group
            own.start()
            for out, _ in peers:
                out.start()

        def finish(group):
            own, peers = group
            for out, arrival in peers:
                arrival.wait_recv()
                out.wait_send()
            own.wait()

        rows_a, rows_b = pl.ds(0, half_rows), pl.ds(half_rows, half_rows)
        c_group, conv_group = allgather(c_ref, c_all), allgather(conv_ref, conv_o)
        gates = []
        for g_ref, g_out in zip(gate_refs, gate_o):
            gates.append(copy(g_ref, g_out.at[k_me], sibling, g_out.at[k_me]))
            gates += [copy(g_ref.at[c], g_out.at[k_me, c], (px, py, c), g_out.at[2 * px + py, c]) for px, py in _chips()]
        own_in = copy(win_ref, win_o.at[k_me], sibling, win_o.at[k_me])
        to_x = [copy(win_ref.at[c, rows], win_o.at[k_me, c, rows], x_nbr, win_o.at[k_x, c, rows]) for rows in (rows_a, rows_b)]
        to_y = [copy(win_ref.at[c, rows], win_o.at[k_me, c, rows], y_nbr, win_o.at[k_y, c, rows]) for rows in (rows_b, rows_a)]
        mod_group = allgather(mod_cols, mod_o)
        relay_y = copy(win_o.at[k_x, c, rows_a], win_o.at[k_x, c, rows_a], y_nbr, win_o.at[k_d, c, rows_a])
        relay_x = copy(win_o.at[k_y, c, rows_b], win_o.at[k_y, c, rows_b], x_nbr, win_o.at[k_d, c, rows_b])
        sib_x, sib_y, sib_d = (copy(win_o.at[k, c], win_o.at[k, c], sibling, win_o.at[k, 1 - c]) for k in (k_x, k_y, k_d))

        start(c_group)
        for out, _ in to_x + to_y + [own_in] + gates:
            out.start()
        start(conv_group)
        finish(c_group)
        cv = jnp.concatenate([c_all[r] for r in range(N_DEV)], axis=0)
        act = cv * jax.nn.sigmoid(cv)
        act_o[...] = act
        mod_cols[...] = jnp.dot(act.astype(BF16), w_ref[...].astype(BF16), preferred_element_type=F32) + b_ref[...]
        start(mod_group)
        to_x[0][1].wait_recv()
        relay_y[0].start()
        to_y[0][1].wait_recv()
        relay_x[0].start()
        finish(mod_group)
        finish(conv_group)
        to_x[1][1].wait_recv()
        sib_x[0].start()
        to_y[1][1].wait_recv()
        sib_y[0].start()
        for out, arrival in gates + [own_in]:
            arrival.wait_recv()
            out.wait_send()
        for out, arrival in (relay_y, relay_x):
            arrival.wait_recv()
            out.wait_send()
        sib_d[0].start()
        for out, _ in to_x + to_y:
            out.wait_send()
        for out, arrival in (sib_x, sib_y, sib_d):
            arrival.wait_recv()
            out.wait_send()

    gate_shapes = [SDS((N_CHIPS,) + g.shape, g.dtype) for g in gate_halves]
    return pl.pallas_call(
        body, name="prologue",
        out_shape=[SDS((N_DEV, c.shape[1]), F32), SDS((N_DEV, N_DEV, n_cols), F32), SDS((N_DEV,) + conv_shard.shape, F32),
                   SDS((N_CHIPS,) + w_in_halves.shape, w_in_halves.dtype)] + gate_shapes,
        in_specs=[VMEM_SPEC] * 4 + [HBM_SPEC] * (1 + n_g), out_specs=[VMEM_SPEC] * 3 + [HBM_SPEC] * (1 + n_g),
        scratch_shapes=[pltpu.VMEM((N_DEV,) + c.shape, F32), pltpu.VMEM((N_DEV, n_cols), F32),
                        pltpu.SemaphoreType.DMA((n_sems,)), pltpu.SemaphoreType.DMA((n_sems,))],
        compiler_params=pltpu.CompilerParams(vmem_limit_bytes=VMEM_LIMIT_V7X),
    )(c, conv_shard, w_ada, b_cols, w_in_halves, *gate_halves)


def _ada_backward(act_t, dmod_cols, w, m, v):
    rows, cols = w.shape
    tn = 512

    def body(a_ref, dm_ref, w_ref, m_ref, v_ref, g_ref, d_ref, mo_ref, vo_ref):
        g = jnp.dot(a_ref[...].astype(BF16), dm_ref[...].astype(BF16), preferred_element_type=F32)
        g_ref[...] = g
        d_ref[...], mo_ref[...], vo_ref[...] = _adamw_math(w_ref[...], g, m_ref[...], v_ref[...])

    spec = pl.BlockSpec((rows, tn), lambda j: (0, j))
    return pl.pallas_call(
        body, name="ada_backward", grid=(cols // tn,), out_shape=[SDS((rows, cols), F32)] * 4,
        in_specs=[pl.BlockSpec((rows, N_DEV), lambda j: (0, 0)), pl.BlockSpec((N_DEV, tn), lambda j: (0, j)), spec, spec, spec],
        out_specs=[spec] * 4, compiler_params=_params("parallel"),
    )(act_t, dmod_cols, w, m, v)


def _sum_rows(gathered, n_loss):
    n = gathered.shape[2]

    def body(g_ref, o_ref):
        acc = g_ref[0]
        for r in range(1, N_DEV):
            acc = acc + g_ref[r]
        o_ref[...] = acc
        o_ref[:, n - n_loss:n] = jnp.broadcast_to(jnp.sum(acc[:, n - n_loss:n], axis=1, keepdims=True), (1, n_loss))

    return pl.pallas_call(body, name="sum_rows", out_shape=SDS((1, n), F32), in_specs=[VMEM_SPEC], out_specs=VMEM_SPEC)(gathered)


def _adamw_vectors(total, offsets, separate, params):
    n_p = len(params)
    apart = [g for g in separate if g is not None]

    def body(*refs):
        total_ref, apart_refs = refs[0], list(refs[1:1 + len(apart)])
        ins, outs = refs[1 + len(apart):1 + len(apart) + 3 * n_p], refs[1 + len(apart) + 3 * n_p:]
        for i in range(n_p):
            w_ref, m_ref, v_ref = ins[3 * i:3 * i + 3]
            g = apart_refs.pop(0)[...] if offsets[i] is None else total_ref[:, offsets[i]:offsets[i] + w_ref.shape[1]]
            outs[4 * i][...] = g
            outs[4 * i + 1][...], outs[4 * i + 2][...], outs[4 * i + 3][...] = _adamw_math(w_ref[...], g, m_ref[...], v_ref[...])

    flat = [a for p in params for a in p]
    res = pl.pallas_call(
        body, name="adamw_vectors", out_shape=[SDS(p[0].shape, F32) for p in params for _ in range(4)],
        in_specs=[VMEM_SPEC] * (1 + len(apart) + len(flat)), out_specs=[VMEM_SPEC] * (4 * n_p),
    )(total, *apart, *flat)
    return [tuple(res[4 * i:4 * i + 4]) for i in range(n_p)]


def _blocks_to_pieces(w):
    nb, rows, n = w.shape
    q = rows // N_CHIPS
    return w.reshape(nb, N_CHIPS, q, n).transpose(1, 0, 2, 3).reshape(N_CHIPS, 2, nb // 2, q, n)


def _pieces_to_blocks(w):
    n_chips, _, half, q, n = w.shape
    return w.reshape(n_chips, 2 * half, q, n).transpose(1, 0, 2, 3).reshape(2 * half, n_chips * q, n)


def kernel(x, c, norm_mix_g, norm_mlp_g, w_ada, b_ada, w_in, conv_w, conv_b, w_rg_a, b_rg_a, w_rg_x, b_rg_x, a_param, w_branch_a, w_pool, b_pool, pool_scale, w_branch_b, w_out, w_up, w_down, final_g, loss_target, m_norm_mix_g, m_norm_mlp_g, m_w_ada, m_b_ada, m_w_in, m_conv_w, m_conv_b, m_w_rg_a, m_b_rg_a, m_w_rg_x, m_b_rg_x, m_a_param, m_w_branch_a, m_w_pool, m_b_pool, m_pool_scale, m_w_branch_b, m_w_out, m_w_up, m_w_down, m_final_g, v_norm_mix_g, v_norm_mlp_g, v_w_ada, v_b_ada, v_w_in, v_conv_w, v_conv_b, v_w_rg_a, v_b_rg_a, v_w_rg_x, v_b_rg_x, v_a_param, v_w_branch_a, v_w_pool, v_b_pool, v_pool_scale, v_w_branch_b, v_w_out, v_w_up, v_w_down, v_final_g):
    d = D_MODEL
    s_rows = x.shape[1]
    ts, tsq = min(512, s_rows), min(256, s_rows)
    xi, yi, ci = _coords()
    k_me, me = 2 * xi + yi, 4 * xi + 2 * yi + ci
    ada_cols = w_ada.shape[2]
    conv_cols = conv_w.shape[2]
    n_in, n_up = w_in.shape[2], w_up.shape[2]

    names = ("w_in", "w_up", "w_down", "w_a", "w_b", "w_out", "rg_a", "rg_x", "pool")
    mine = dict(zip(names, (w_in[0], w_up[0], w_down[0], w_branch_a[0], w_branch_b[0], w_out[0], w_rg_a[0], w_rg_x[0], w_pool[0])))
    moments_m = dict(zip(names, (m_w_in[0], m_w_up[0], m_w_down[0], m_w_branch_a[0], m_w_branch_b[0], m_w_out[0], m_w_rg_a[0], m_w_rg_x[0], m_w_pool[0])))
    moments_v = dict(zip(names, (v_w_in[0], v_w_up[0], v_w_down[0], v_w_branch_a[0], v_w_branch_b[0], v_w_out[0], v_w_rg_a[0], v_w_rg_x[0], v_w_pool[0])))
    block_weights, squares = ("rg_a", "rg_x", "pool"), ("w_a", "w_b", "w_out")
    place = jnp.stack([ci, k_me]).astype(jnp.int32)
    g_mix, g_mlp, g_fin = norm_mix_g, norm_mlp_g, final_g.reshape(1, d)
    x0, target = x[0], loss_target[0]
    wg = functools.partial(_weight_grad, tk=min(1024, d))
    plain, back, back_sum = (dict(transposed=t, reduce_blocks=r) for t, r in ((False, False), (True, False), (True, True)))
    rows_d, vec8, vec1 = _row_spec(ts, d), _vec_spec(8, d), _vec_spec(1, d)
    tw = min(1024, s_rows)
    halves_of_tile = 2

    def out_rows(cols, dtype, t=ts):
        return SDS((s_rows, cols), dtype), _row_spec(t, cols)

    sums_out = (SDS((8, d), F32), vec8)

    def halves(n):
        return mine[n].astype(BF16).reshape(2, -1, mine[n].shape[-1])

    def blocks(w):
        return _pieces_to_blocks(w.reshape(N_CHIPS, 2, N_BLOCKS // 2, BLOCK // N_CHIPS, BLOCK))

    act_all, mod_all, conv_all, w_in_whole, *gate_wholes = _prologue(
        c, conv_w[0], w_ada[0], lax.dynamic_slice_in_dim(b_ada, k_me * ada_cols, ada_cols, axis=1), halves("w_in"), [halves(n) for n in block_weights])
    conv_full = conv_all[0::2].transpose(1, 0, 2).reshape(CONV_WIDTH, d)
    mod_mine = lax.dynamic_index_in_dim(mod_all, me, axis=1, keepdims=False)[0::2]
    mod = jnp.concatenate([mod_mine.reshape(N_MOD, d), jnp.zeros((8 - N_MOD, d), F32)], axis=0)
    w_in_all = w_in_whole.reshape(N_CHIPS, d, n_in)
    vecs = jnp.concatenate([conv_b, b_rg_a, b_rg_x, a_param, b_pool, pool_scale, jnp.zeros((2, d), F32)], axis=0)

    def norm_first(a_ref, extra_refs, out_refs):
        h = _modulated_norm(a_ref[...], extra_refs[0], extra_refs[1], 1, 0)
        out_refs[1][...] = h
        return h

    stage = _Stage()
    i_blk = [_add_gather_d2d(stage, g) for g in gate_wholes]
    i_sq = [_add_gather_ici(stage, halves(n)) for n in squares]
    down_rows = mine["w_down"].shape[0] // 8
    i_down = _add_gather_ici(stage, halves("w_down"), rows=(0, down_rows))
    (proj, h1), got = _matmul(x0, w_in_all, name="proj_in", tm=ts, extra=[(mod, vec8), (g_mix, vec1)], prepare=norm_first,
                              outs=[out_rows(N_CHIPS * n_in, F32), out_rows(d, BF16)], finish=_store_blocks(n_in), parts=halves_of_tile,                              stage=stage, **plain)
    w_rg_a_all, w_rg_x_all, w_pool_all = (blocks(got[i]) for i in i_blk)
    stage = _Stage()
    i_sq = [_add_gather_d2d(stage, got[i]) for i in i_sq]
    i_up = _add_gather_ici(stage, halves("w_up"))
    w_down_so_far = got[i_down]
    (xr, hr, ga, p, pooled), got = _seq_forward(proj, conv_full, vecs, w_rg_a_all, w_rg_x_all, w_pool_all, ts=tsq, stage=stage)
    w_a_all, w_b_all, w_out_all = (got[i].reshape(1, d, d) for i in i_sq)
    stage = _Stage()
    i_up = _add_gather_d2d(stage, got[i_up])
    i_down = _add_gather_ici(stage, halves("w_down"), rows=(down_rows, 2 * down_rows), whole=w_down_so_far)
    (merged, s_a, s_b, t_a, t_b), got = _branches(ga, pooled, proj, w_a_all, w_b_all, ts=ts, stage=stage)
    w_up_all = got[i_up].reshape(N_CHIPS, d, n_up)
    stage = _Stage()
    i_down = _add_gather_ici(stage, halves("w_down"), rows=(3 * down_rows, down_rows), whole=got[i_down])

    def residual_norm(k, acc, extra_refs, out_refs):
        x_ref, mod_ref, g_ref = extra_refs
        out_refs[0][...] = acc.astype(BF16)
        x2_t = x_ref[...] + mod_ref[2:3, :] * acc
        out_refs[1][...] = x2_t
        out_refs[2][...] = _modulated_norm(x2_t, mod_ref, g_ref, 4, 3)

    (mo, x2, h2), got = _matmul(merged, w_out_all, name="mix_out", tm=ts, extra=[(x0, rows_d), (mod, vec8), (g_mlp, vec1)],
                                outs=[out_rows(d, BF16), out_rows(d, F32), out_rows(d, BF16)], finish=residual_norm, parts=halves_of_tile,                                stage=stage, transposed=False, reduce_blocks=True)
    stage = _Stage()
    i_down = _add_gather_d2d(stage, got[i_down])

    def relu_squared(k, acc, extra_refs, out_refs):
        out_refs[0][:, k * n_up:(k + 1) * n_up] = jnp.square(jnp.maximum(acc, 0.0)).astype(BF16)

    (ff,), got = _matmul(h2, w_up_all, name="mlp_up", tm=ts, outs=[out_rows(D_FF, BF16)], finish=relu_squared, parts=halves_of_tile,
                         stage=stage, **plain)
    w_down_all = got[i_down].reshape(N_CHIPS, D_FF // N_CHIPS, d)

    def loss_head(k, acc, extra_refs, out_refs):
        x2_ref, mod_ref, g_ref, t_ref = extra_refs
        dx_ref, df_ref, acc_ref = out_refs
        xh, r = _rms(x2_ref[...] + mod_ref[5:6, :] * acc)
        err = xh * g_ref[...] - t_ref[...]
        dy = err / d
        dxh = dy * g_ref[...]
        dx3_t = r * (dxh - xh * jnp.mean(dxh * xh, axis=-1, keepdims=True))
        dx_ref[...] = dx3_t
        df_ref[...] = (dx3_t * mod_ref[5:6, :]).astype(BF16)
        _accumulate(acc_ref, 0, dy * xh)
        _accumulate(acc_ref, 1, dx3_t * acc)
        _accumulate(acc_ref, 2, jnp.square(err) * (0.5 / d))

    dx3, dffo, sums_head = _matmul(ff, w_down_all, name="mlp_down", tm=ts, extra=[(x2, rows_d), (mod, vec8), (g_fin, vec1), (target, rows_d)],
                                   outs=[out_rows(d, F32), out_rows(d, BF16), sums_out], finish=loss_head, zero_first=(2,), parts=halves_of_tile,                                   transposed=False, reduce_blocks=True)

    partial_of, from_sibling, chip_sum, from_chips, half_done, quarter = {}, {}, {}, {}, {}, {}

    def publish(n, g):
        g = _blocks_to_pieces(g).astype(BF16) if n in block_weights else g
        partial_of[n] = g.reshape(N_CHIPS, 2, -1, mine[n].shape[-1])

    def exchange(to_sibling=(), to_chips=(), swap=()):
        st = _Stage()
        slots = [(n, from_sibling, _add_reduce_d2d(st, partial_of[n])) for n in to_sibling]
        slots += [(n, from_chips, _add_reduce_ici(st, chip_sum[n])) for n in to_chips]
        slots += [(n, quarter, _add_swap_halves(st, half_done[n])) for n in swap]
        return st, slots

    def collect(slots, outs):
        for n, where, i in slots:
            where[n] = outs[i]

    def sum_pairs(*ns):
        for n in ns:
            chip_sum[n] = _sum_pair(partial_of[n], from_sibling[n], place, "sum_pair_" + n)

    def sum_quarters(*ns):
        for n in ns:
            half_done[n] = _sum_quarter(partial_of[n], from_sibling[n], from_chips[n], place, "sum_quarter_" + n)

    publish("w_down", wg(ff, dffo, nb=1, name="grad_w_down"))

    def relu_squared_backward(k, acc, extra_refs, out_refs):
        cols = slice(k * n_up, (k + 1) * n_up)
        out_refs[0][:, cols] = (acc * (2.0 * jnp.sqrt(extra_refs[0][:, cols].astype(F32)))).astype(BF16)

    stage, slots = exchange(to_sibling=["w_down"])
    (dup,), got = _matmul(dffo, w_down_all, name="d_mlp_down", tm=ts, extra=[(ff, _row_spec(ts, D_FF))], outs=[out_rows(D_FF, BF16)],
                          finish=relu_squared_backward, parts=halves_of_tile, stage=stage, **back)
    collect(slots, got)
    sum_pairs("w_down")

    def norm_mlp_backward(k, acc, extra_refs, out_refs):
        x2_ref, dres_ref, mod_ref, g_ref, mo_ref = extra_refs
        dx_ref, dmo_ref, acc_ref = out_refs
        dx =_modulated_norm_backward(acc, x2_ref[...], dres_ref[...], mod_ref, g_ref, acc_ref, 4)
        dx_ref[...] = dx
        dmo_ref[...] = (dx * mod_ref[2:3, :]).astype(BF16)
        _accumulate(acc_ref, 3, dx * mo_ref[...].astype(F32))

    stage, slots = exchange(to_chips=["w_down"])
    (dx2, dmo, sums_mlp), got = _matmul(dup, w_up_all, name="d_mlp_up", tm=ts, stage=stage, finish=norm_mlp_backward,
                                        extra=[(x2, rows_d), (dx3, rows_d), (mod, vec8), (g_mlp, vec1), (mo, rows_d)],
                                        outs=[out_rows(d, F32), out_rows(d, BF16), sums_out], zero_first=(2,), parts=halves_of_tile, **back_sum)
    collect(slots, got)
    sum_quarters("w_down")
    publish("w_up", wg(h2, dup, nb=N_CHIPS, name="grad_w_up"))
    publish("w_out", wg(merged, dmo, nb=1, name="grad_w_out"))

    def merge_backward(k, acc, extra_refs, out_refs):
        sa_ref, sb_ref, ta_ref, tb_ref = extra_refs
        out_refs[0][...] = (acc * sa_ref[...].astype(F32)).astype(BF16)
        out_refs[1][...] = (acc * sb_ref[...].astype(F32)).astype(BF16)
        out_refs[2][:, 0:d] = (acc * ta_ref[...].astype(F32)).astype(BF16)
        out_refs[2][:, d:2 * d] = (acc * tb_ref[...].astype(F32)).astype(BF16)

    stage, slots = exchange(to_sibling=["w_up", "w_out"], swap=["w_down"])
    (dbr_a, dbr_b, dgab), got = _matmul(dmo, w_out_all, name="d_mix_out", tm=ts, stage=stage, finish=merge_backward,
                                        extra=[(s_a, rows_d), (s_b, rows_d), (t_a, rows_d), (t_b, rows_d)],
                                        outs=[out_rows(d, BF16), out_rows(d, BF16), out_rows(2 * d, BF16)], parts=halves_of_tile, **back_sum)
    collect(slots, got)
    publish("w_a", wg(ga, dbr_a, nb=1, name="grad_w_branch_a"))
    publish("w_b", wg(pooled, dbr_b, nb=1, name="grad_w_branch_b"))
    dga, = _matmul(dbr_a, w_a_all, name="d_branch_a", tm=tw, outs=[out_rows(d, BF16, tw)], finish=_store_blocks(d), **back_sum)
    dpooled, = _matmul(dbr_b, w_b_all, name="d_branch_b", tm=tw, outs=[out_rows(d, BF16, tw)], finish=_store_blocks(d), **back_sum)
    sum_pairs("w_up", "w_out")
    stage, slots = exchange(to_sibling=["w_a", "w_b"], to_chips=["w_up", "w_out"])
    (dproj, sums_seq, d_rg_a, d_rg_x, d_pool), got = _seq_backward_by_block(proj, xr, hr, p, dga, dpooled, dgab, conv_full, vecs,
                                                                            w_rg_a_all, w_rg_x_all, w_pool_all, ts=tsq, stage=stage)
    collect(slots, got)
    sum_quarters("w_up", "w_out")
    sum_pairs("w_a", "w_b")
    for n, g in zip(block_weights, (d_rg_a, d_rg_x, d_pool)):
        publish(n, g)
    stage, slots = exchange(to_sibling=block_weights, to_chips=["w_a", "w_b"], swap=["w_up", "w_out"])
    g_in, got = wg(h1, dproj, nb=N_CHIPS, name="grad_w_in", stage=stage)
    collect(slots, got)
    publish("w_in", g_in)
    sum_pairs(*block_weights)
    sum_quarters("w_a", "w_b")
    stage, slots = exchange(to_sibling=["w_in"], to_chips=block_weights)
    collect(slots, _run_stage(stage, "exchange_w_in_gates"))
    sum_pairs("w_in")
    sum_quarters(*block_weights)

    def norm_mix_backward(k, acc, extra_refs, out_refs):
        x_ref, dres_ref, mod_ref, g_ref = extra_refs
        out_refs[0][...] = _modulated_norm_backward(acc, x_ref[...], dres_ref[...], mod_ref, g_ref, out_refs[1], 1)

    stage, slots = exchange(to_chips=["w_in"], swap=block_weights + ("w_a", "w_b"))
    (grad_x, sums_mix), got = _matmul(dproj, w_in_all, name="d_proj_in", tm=ts, stage=stage, finish=norm_mix_backward,
                                      extra=[(x0, rows_d), (dx2, rows_d), (mod, vec8), (g_mix, vec1)],
                                      outs=[out_rows(d, F32), sums_out], zero_first=(1,), parts=halves_of_tile, **back_sum)
    collect(slots, got)
    sum_quarters("w_in")

    dmod = jnp.concatenate([sums_mix[1:2], sums_mix[0:1], sums_mlp[3:4], sums_mlp[1:2], sums_mlp[0:1], sums_head[1:2]], axis=1)
    row = jnp.concatenate([sums_mix[2:3], sums_mlp[2:3], sums_seq[4:5], sums_seq[5:6], sums_seq[6:7], sums_seq[7:8], sums_seq[8:9],
                           sums_seq[9:10], sums_head[0:1], sums_seq[0:CONV_WIDTH].reshape(1, CONV_WIDTH * d), dmod, sums_head[2:3]], axis=1)
    n_vec, n_conv, n_mod = 9 * d, CONV_WIDTH * d, N_MOD * d
    stage, slots = exchange(swap=["w_in"])
    i_rows = _add_allgather8(stage, row)
    got = _run_stage(stage, "exchange_sums_w_in")
    collect(slots, got)
    rows_all = got[i_rows]
    total = _sum_rows(rows_all, d)
    loss = total[0, n_vec + n_conv + n_mod]
    dmod_all = rows_all[:, 0, n_vec + n_conv:n_vec + n_conv + n_mod]
    g_ada, d_ada, m_ada, v_ada = _ada_backward(act_all.T, lax.dynamic_slice_in_dim(dmod_all, k_me * ada_cols, ada_cols, axis=1),
                                               w_ada[0], m_w_ada[0], v_w_ada[0])
    g_conv = lax.dynamic_slice_in_dim(total[:, n_vec:n_vec + n_conv].reshape(CONV_WIDTH, d), k_me * conv_cols, conv_cols, axis=1)
    vec_names = ("norm_mix_g", "norm_mlp_g", "conv_b", "b_rg_a", "b_rg_x", "a_param", "b_pool", "pool_scale", "final_g", "conv_w", "b_ada")
    vec_params = [(norm_mix_g, m_norm_mix_g, v_norm_mix_g), (norm_mlp_g, m_norm_mlp_g, v_norm_mlp_g), (conv_b, m_conv_b, v_conv_b),
                  (b_rg_a, m_b_rg_a, v_b_rg_a), (b_rg_x, m_b_rg_x, v_b_rg_x), (a_param, m_a_param, v_a_param), (b_pool, m_b_pool, v_b_pool),
                  (pool_scale, m_pool_scale, v_pool_scale), tuple(a.reshape(1, d) for a in (final_g, m_final_g, v_final_g)),
                  (conv_w[0], m_conv_w[0], v_conv_w[0]), (b_ada, m_b_ada, v_b_ada)]
    offsets = [i * d for i in range(9)] + [None, n_vec + n_conv]
    vec_res = _adamw_vectors(total, offsets, [None] * 9 + [g_conv, None], vec_params)
    vec_out = {n: [r.reshape(shape) for r in res] for n, res, shape in zip(
        vec_names, vec_res, [(1, d)] * 8 + [(d,), (1, CONV_WIDTH, conv_cols), (1, N_MOD * d)])}

    big_out = {}
    for n in names:
        shape2 = (-1, mine[n].shape[-1])
        g2 = quarter[n].reshape(shape2)
        res = _adamw(mine[n].reshape(shape2), g2, moments_m[n].reshape(shape2), moments_v[n].reshape(shape2), "adamw_" + n)
        big_out[n] = [r.reshape((1,) + mine[n].shape) for r in res]

    key = {"w_in": "w_in", "w_rg_a": "rg_a", "w_rg_x": "rg_x", "w_branch_a": "w_a", "w_pool": "pool", "w_branch_b": "w_b", "w_out": "w_out",
           "w_up": "w_up", "w_down": "w_down"}
    order = ("norm_mix_g", "norm_mlp_g", "w_ada", "b_ada", "w_in", "conv_w", "conv_b", "w_rg_a", "b_rg_a", "w_rg_x", "b_rg_x", "a_param",
             "w_branch_a", "w_pool", "b_pool", "pool_scale", "w_branch_b", "w_out", "w_up", "w_down", "final_g")
    ada_out = [g_ada[None], d_ada[None], m_ada[None], v_ada[None]]
    outs = [loss, grad_x[None]]
    for kind in range(4):
        for nme in order:
            outs.append(ada_out[kind] if nme == "w_ada" else big_out[key[nme]][kind] if nme in key else vec_out[nme][kind])
    return tuple(outs)
```

```python
import functools

import jax
import jax.numpy as jnp
from jax import lax
from jax.experimental import pallas as pl
from jax.experimental.pallas import tpu as pltpu

F32, BF16 = jnp.float32, jnp.bfloat16
D_MODEL = 1024
D_FF = 4 * D_MODEL
N_BLOCKS = 4
BLOCK = D_MODEL // N_BLOCKS
CONV_WIDTH = 4
POOL_WINDOWS = (2, 4, 8, 16)
CONV_HALO = 8
POOL_HALO = 16
N_MOD = 6
EPS = 1e-6
C_RG = 8.0
ADAM_LR, ADAM_B1, ADAM_B2, ADAM_EPS, ADAM_WD, ADAM_STEP = 0.001, 0.9, 0.999, 1e-08, 0.01, 10
N_DEV = 8
N_CHIPS = 4
VMEM_LIMIT_V7X = 56 * 2**20
MESH = pl.DeviceIdType.MESH
SDS = jax.ShapeDtypeStruct
HBM_SPEC = pl.BlockSpec(memory_space=pltpu.HBM)
VMEM_SPEC = pl.BlockSpec(memory_space=pltpu.VMEM)


def _params(*semantics):
    return pltpu.CompilerParams(dimension_semantics=semantics, vmem_limit_bytes=VMEM_LIMIT_V7X)


def _coords():
    return lax.axis_index("x"), lax.axis_index("y"), lax.axis_index("c")


def _flip(v, bit):
    return 1 - v if bit else v


def _chips():
    x, y, _ = _coords()
    return [(1 - x, y), (x, 1 - y), (1 - x, 1 - y)]


PEER_SETS = [frozenset(s) for s in (("sibling",), ("chips",), ("sibling", "chips"), ("sibling", "chips", "others"))]


class _Stage:
    def __init__(self):
        self.inputs, self.in_specs, self.outputs, self.out_specs, self.aliases = [], [], [], [], {}
        self.parts, self.n_copies, self.peers = [], 0, set()

    def add(self, inputs, in_spec, outputs, out_spec, n_copies, build, peers, alias=False):
        self.peers |= set(peers)
        i0, o0 = len(self.inputs), len(self.outputs)
        self.inputs += inputs
        self.in_specs += [in_spec] * len(inputs)
        self.outputs += outputs
        self.out_specs += [out_spec] * len(outputs)
        if alias:
            self.aliases.update({i0 + i: o0 + i for i in range(len(inputs))})
        self.parts.append((build, i0, len(inputs), o0, len(outputs)))
        self.n_copies += n_copies
        return list(range(o0, o0 + len(outputs)))

    def copies(self, in_refs, out_refs):
        out = []
        for build, i0, ni, o0, no in self.parts:
            out += build(in_refs[i0:i0 + ni], out_refs[o0:o0 + no])
        assert len(out) == self.n_copies
        return out

    def run(self, in_refs, out_refs, send_sems, recv_sems, start):
        x, y, c = _coords()
        for s, (src, dst, to, landing) in enumerate(self.copies(in_refs, out_refs)):
            if to is None:
                cp = pltpu.make_async_copy(src, dst, send_sems.at[s])
                cp.start() if start else cp.wait()
                continue
            cp = pltpu.make_async_remote_copy(src_ref=src, dst_ref=dst, send_sem=send_sems.at[s], recv_sem=recv_sems.at[s],
                                              device_id=to, device_id_type=MESH)
            if start:
                cp.start()
            else:
                pltpu.make_async_remote_copy(src_ref=landing, dst_ref=landing, send_sem=send_sems.at[s], recv_sem=recv_sems.at[s],
                                             device_id=(x, y, c), device_id_type=MESH).wait_recv()
                cp.wait_send()


def _hosted(stage, body, *, name, in_specs, out_specs, out_shape, grid=(), scratch_shapes=(), compiler_params=None):
    if stage is None:
        return pl.pallas_call(body, name=name, grid=grid, in_specs=in_specs, out_specs=out_specs, out_shape=out_shape,
                              scratch_shapes=list(scratch_shapes), compiler_params=compiler_params)
    single = not isinstance(out_shape, (list, tuple))
    h_out_shape = [out_shape] if single else list(out_shape)
    h_out_specs = [out_specs] if single else list(out_specs)
    n_in, n_out, n_scr = len(in_specs), len(h_out_shape), len(scratch_shapes)
    s_in, s_out = len(stage.inputs), len(stage.outputs)

    def wrapped(*refs):
        h_in, st_in = refs[:n_in], refs[n_in:n_in + s_in]
        h_o, st_o = refs[n_in + s_in:n_in + s_in + n_out], refs[n_in + s_in + n_out:n_in + s_in + n_out + s_out]
        h_scr = refs[n_in + s_in + n_out + s_out:n_in + s_in + n_out + s_out + n_scr]
        send_sems, recv_sems = refs[n_in + s_in + n_out + s_out + n_scr:]
        def begin():
            x, y, c = _coords()
            peers = [(x, y, 1 - c)] * ("sibling" in stage.peers)
            peers += [(px, py, c) for px, py in _chips()] * ("chips" in stage.peers)
            peers += [(px, py, 1 - c) for px, py in _chips()] * ("others" in stage.peers)
            barrier = pltpu.get_barrier_semaphore()
            for peer in peers:
                pl.semaphore_signal(barrier, inc=1, device_id=peer, device_id_type=MESH)
            pl.semaphore_wait(barrier, len(peers))
            stage.run(st_in, st_o, send_sems, recv_sems, True)

        if not grid:
            begin()
            if body is not None:
                body(*h_in, *h_o, *h_scr)
            stage.run(st_in, st_o, send_sems, recv_sems, False)
            return
        ids = [pl.program_id(a) for a in range(len(grid))]
        first = functools.reduce(jnp.logical_and, [i == 0 for i in ids])
        last = functools.reduce(jnp.logical_and, [i == g - 1 for i, g in zip(ids, grid)])
        pl.when(first)(begin)
        body(*h_in, *h_o, *h_scr)
        pl.when(last)(lambda: stage.run(st_in, st_o, send_sems, recv_sems, False))

    call = pl.pallas_call(
        wrapped, name=name, grid=grid, in_specs=list(in_specs) + stage.in_specs, out_specs=h_out_specs + stage.out_specs,
        out_shape=h_out_shape + stage.outputs, input_output_aliases={n_in + i: n_out + o for i, o in stage.aliases.items()},
        scratch_shapes=list(scratch_shapes) + [pltpu.SemaphoreType.DMA((stage.n_copies,)), pltpu.SemaphoreType.DMA((stage.n_copies,))],
        compiler_params=pltpu.CompilerParams(dimension_semantics=("arbitrary",) * len(grid), vmem_limit_bytes=VMEM_LIMIT_V7X,
                                             collective_id=PEER_SETS.index(frozenset(stage.peers))),
    )

    def run(*operands):
        outs = call(*operands, *stage.inputs)
        host = outs[:n_out]
        return (host[0] if single else host), outs[n_out:]

    return run


def _run_stage(stage, name):
    return _hosted(stage, None, name=name, in_specs=[], out_specs=[], out_shape=[])()[1]


def _add_allgather8(stage, v):
    def build(ins, outs):
        x, y, c = _coords()
        me = 4 * x + 2 * y + c
        copies = [(ins[0], outs[0].at[me], None, None)]
        for r in range(1, N_DEV):
            peer = (_flip(x, r & 4), _flip(y, r & 2), _flip(c, r & 1))
            copies.append((ins[0], outs[0].at[me], peer, outs[0].at[me ^ r]))
        return copies

    return stage.add([v], VMEM_SPEC, [SDS((N_DEV,) + v.shape, v.dtype)], VMEM_SPEC, N_DEV, build, ("sibling", "chips", "others"))[0]


def _add_gather_ici(stage, shard, rows=None, whole=None):
    first, count = rows or (0, shard.shape[1])

    def build(ins, outs):
        x, y, c = _coords()
        k_me = 2 * x + y
        part = pl.ds(first, count)
        copies = [] if whole is not None else [(ins[0], outs[0].at[k_me], (x, y, 1 - c), outs[0].at[k_me])]
        for px, py in _chips():
            copies.append((ins[0].at[c, part], outs[0].at[k_me, c, part], (px, py, c), outs[0].at[2 * px + py, c, part]))
        return copies

    if whole is None:
        return stage.add([shard], HBM_SPEC, [SDS((N_CHIPS,) + shard.shape, shard.dtype)], HBM_SPEC, N_CHIPS, build, ("sibling", "chips"))[0]

    def build_into(ins, outs):
        return build(ins[1:], outs)

    i0 = len(stage.inputs)
    out = stage.add([whole, shard], HBM_SPEC, [SDS(whole.shape, whole.dtype)], HBM_SPEC, N_CHIPS - 1, build_into, ("chips",))[0]
    stage.aliases[i0] = out
    return out


def _add_gather_d2d(stage, whole):
    def build(ins, outs):
        x, y, c = _coords()
        return [(outs[0].at[2 * px + py, c], outs[0].at[2 * px + py, c], (x, y, 1 - c), outs[0].at[2 * px + py, 1 - c]) for px, py in _chips()]

    return stage.add([whole], HBM_SPEC, [SDS(whole.shape, whole.dtype)], HBM_SPEC, N_CHIPS - 1, build, ("sibling",), alias=True)[0]


def _add_reduce_d2d(stage, grads):
    def build(ins, outs):
        x, y, c = _coords()
        return [(ins[0].at[k, 1 - c], outs[0].at[k], (x, y, 1 - c), outs[0].at[k]) for k in range(N_CHIPS)]

    return stage.add([grads], HBM_SPEC, [SDS((N_CHIPS,) + grads.shape[2:], grads.dtype)], HBM_SPEC, N_CHIPS, build, ("sibling",))[0]


def _add_reduce_ici(stage, partial):
    def build(ins, outs):
        x, y, c = _coords()
        return [(ins[0].at[2 * px + py], outs[0].at[j], (px, py, c), outs[0].at[j]) for j, (px, py) in enumerate(_chips())]

    return stage.add([partial], HBM_SPEC, [SDS((N_CHIPS - 1,) + partial.shape[1:], partial.dtype)], HBM_SPEC, N_CHIPS - 1, build, ("chips",))[0]


def _add_swap_halves(stage, quarter):
    def build(ins, outs):
        x, y, c = _coords()
        return [(outs[0].at[c], outs[0].at[c], (x, y, 1 - c), outs[0].at[1 - c])]

    return stage.add([quarter], HBM_SPEC, [SDS(quarter.shape, quarter.dtype)], HBM_SPEC, 1, build, ("sibling",), alias=True)[0]


def _sum_pair(grads, from_sibling, place, name):
    _, _, rows, cols = grads.shape
    tr = min(rows, 512)

    def body(place_ref, g_ref, s_ref, o_ref):
        o_ref[0] = (g_ref[0, 0].astype(F32) + s_ref[0].astype(F32)).astype(BF16)

    spec = pl.BlockSpec((1, tr, cols), lambda k, i, place_ref: (k, i, 0))
    return pl.pallas_call(
        body, name=name, out_shape=SDS((N_CHIPS, rows, cols), BF16),
        grid_spec=pltpu.PrefetchScalarGridSpec(
            num_scalar_prefetch=1, grid=(N_CHIPS, rows // tr),
            in_specs=[pl.BlockSpec((1, 1, tr, cols), lambda k, i, place_ref: (k, place_ref[0], i, 0)), spec], out_specs=spec),
        compiler_params=_params("parallel", "parallel"),
    )(place, grads, from_sibling)


def _sum_quarter(grads, from_sibling, from_chips, place, name):
    _, _, rows, cols = grads.shape
    tr = min(rows, 512)

    def body(place_ref, g_ref, s_ref, c_ref, o_ref):
        acc = g_ref[0, 0].astype(F32) + s_ref[0].astype(F32)
        for j in range(N_CHIPS - 1):
            acc = acc + c_ref[j].astype(F32)
        o_ref[0] = acc

    return pl.pallas_call(
        body, name=name, out_shape=SDS((2, rows, cols), F32),
        grid_spec=pltpu.PrefetchScalarGridSpec(
            num_scalar_prefetch=1, grid=(rows // tr,),
            in_specs=[pl.BlockSpec((1, 1, tr, cols), lambda i, place_ref: (place_ref[1], place_ref[0], i, 0)),
                      pl.BlockSpec((1, tr, cols), lambda i, place_ref: (place_ref[1], i, 0)),
                      pl.BlockSpec((N_CHIPS - 1, tr, cols), lambda i, place_ref: (0, i, 0))],
            out_specs=pl.BlockSpec((1, tr, cols), lambda i, place_ref: (place_ref[0], i, 0))),
        compiler_params=_params("parallel"),
    )(place, grads, from_sibling, from_chips)


def _row_spec(ts, cols, col_block=0):
    return pl.BlockSpec((ts, cols), lambda *g: (g[0], col_block))


def _vec_spec(rows, cols):
    return pl.BlockSpec((rows, cols), lambda *g: (0, 0))


def _matmul(a, w, *, transposed, reduce_blocks, name, tm, outs, finish, extra=(), prepare=None, zero_first=(), parts=1, stage=None):
    m_rows = a.shape[0]
    nb, r, c = w.shape
    kb = c if transposed else r
    dims = (((1,), (1,)), ((), ())) if transposed else (((1,), (0,)), ((), ()))
    assert a.shape[1] == (nb * kb if reduce_blocks else kb) and m_rows % tm == 0 and tm % parts == 0
    n_extra, sub = len(extra), tm // parts

    def body(a_ref, w_ref, *rest):
        extra_refs, out_refs = rest[:n_extra], rest[n_extra:]
        if zero_first:
            @pl.when(pl.program_id(0) == 0)
            def _():
                for i in zero_first:
                    out_refs[i][...] = jnp.zeros_like(out_refs[i])

        def views(j):
            rows = pl.ds(j * sub, sub)
            return rows, [ref.at[rows] if ref.shape[0] == tm else ref for ref in extra_refs], [ref.at[rows] if ref.shape[0] == tm else ref for ref in out_refs]

        def product(j):
            rows, ex, ou = views(j)
            if reduce_blocks:
                acc = lax.dot_general(a_ref[rows, 0:kb], w_ref[0], dims, preferred_element_type=F32)
                for k in range(1, nb):
                    acc += lax.dot_general(a_ref[rows, k * kb:(k + 1) * kb], w_ref[k], dims, preferred_element_type=F32)
                return [(None, acc)]
            lhs = a_ref[rows, :] if prepare is None else prepare(a_ref.at[rows], ex, ou)
            return [(k, lax.dot_general(lhs, w_ref[k], dims, preferred_element_type=F32)) for k in range(nb)]

        pending = product(0)
        for j in range(parts):
            ahead = product(j + 1) if j + 1 < parts else None
            _, ex, ou = views(j)
            for k, acc in pending:
                finish(k, acc, ex, ou)
            pending = ahead

    return _hosted(
        stage, body, name=name, grid=(m_rows // tm,), out_shape=[s for s, _ in outs], out_specs=[s for _, s in outs],
        in_specs=[_row_spec(tm, a.shape[1]), pl.BlockSpec((nb, r, c), lambda *g: (0, 0, 0), pipeline_mode=pl.Buffered(1))] + [s for _, s in extra],
        compiler_params=_params("arbitrary"),
    )(a, w, *[e for e, _ in extra])


def _store_blocks(n):
    def finish(k, acc, extra_refs, out_refs):
        if k is None:
            out_refs[0][...] = acc.astype(out_refs[0].dtype)
        else:
            out_refs[0][:, k * n:(k + 1) * n] = acc.astype(out_refs[0].dtype)
    return finish


def _weight_grad(a, b, *, nb, name, tk, stage=None):
    s_rows, k1 = a.shape
    tn = b.shape[1] // nb
    assert k1 % tk == 0 and b.shape[1] % nb == 0

    def body(a_ref, b_ref, o_ref):
        o_ref[0] = lax.dot_general(a_ref[...], b_ref[...], (((0,), (0,)), ((), ())), preferred_element_type=F32).astype(o_ref.dtype)

    return _hosted(
        stage, body, name=name, grid=(nb, k1 // tk), out_shape=SDS((nb, k1, tn), BF16),
        in_specs=[pl.BlockSpec((s_rows, tk), lambda n, i: (0, i)), pl.BlockSpec((s_rows, tn), lambda n, i: (0, n))],
        out_specs=pl.BlockSpec((1, tk, tn), lambda n, i: (n, i, 0)), compiler_params=_params("parallel", "parallel"),
    )(a, b)


def _rms(x):
    r = lax.rsqrt(jnp.mean(x * x, axis=-1, keepdims=True) + EPS)
    return x * r, r


def _accumulate(acc_ref, row, value):
    acc_ref[row:row + 1, :] += jnp.sum(value, axis=0, keepdims=True)


def _gelu_parts(y):
    k0, k1 = 0.7978845608028654, 0.044715
    th = jnp.tanh(k0 * (y + k1 * (y * y * y)))
    gelu = 0.5 * y * (1.0 + th)
    dgelu = 0.5 * (1.0 + th) + 0.5 * y * (1.0 - th * th) * (k0 * (1.0 + 3.0 * k1 * (y * y)))
    return gelu, dgelu


def _sigmoid(z):
    return 0.5 * jnp.tanh(0.5 * z) + 0.5


def _one_minus_square(a, log_a):
    return -jnp.tanh(log_a) * (1.0 + a * a)


def _modulated_norm(x, mod_ref, g_ref, sc_row, sh_row):
    xh, _ = _rms(x)
    return ((xh * g_ref[...]) * (1.0 + mod_ref[sc_row:sc_row + 1, :]) + mod_ref[sh_row:sh_row + 1, :]).astype(BF16)


def _modulated_norm_backward(dh, x, dres, mod_ref, g_ref, acc_ref, sc_row):
    xh, r = _rms(x)
    _accumulate(acc_ref, 0, dh * (xh * g_ref[...]))
    _accumulate(acc_ref, 1, dh)
    dn = dh * (1.0 + mod_ref[sc_row:sc_row + 1, :])
    _accumulate(acc_ref, 2, dn * xh)
    dxh = dn * g_ref[...]
    return dres + r * (dxh - xh * jnp.mean(dxh * xh, axis=-1, keepdims=True))


def _seq_forward(proj, conv_w, vecs, w_a, w_x, w_p, *, ts, stage=None):
    s_rows = proj.shape[0]
    d = D_MODEL

    def body(x_ref, xh_ref, y_ref, u_ref, uh_ref, cw_ref, vec_ref, wa_ref, wx_ref, wp_ref,
             xr_o, hr_o, ga_o, p_o, pooled_o, carry, a_scr, b_scr):
        t = pl.program_id(0)

        @pl.when(t == 0)
        def _():
            carry[...] = jnp.zeros_like(carry)

        rows = lax.broadcasted_iota(jnp.int32, (ts, BLOCK), 0)
        start = (rows == 0) & (t == 0)
        for g in range(N_BLOCKS):
            lanes = slice(g * BLOCK, (g + 1) * BLOCK)
            x = x_ref[:, lanes]
            xx = jnp.concatenate([jnp.where(t == 0, 0.0, xh_ref[:, lanes]), x], axis=0)
            xr = vec_ref[0:1, lanes] + x * cw_ref[CONV_WIDTH - 1:CONV_WIDTH, lanes]
            for j in range(CONV_WIDTH - 1):
                xr = xr + pltpu.roll(xx, CONV_WIDTH - 1 - j, 0)[CONV_HALO:] * cw_ref[j:j + 1, lanes]
            xr_o[:, lanes] = xr
            xrb = xr.astype(BF16)
            r = _sigmoid(jnp.dot(xrb, wa_ref[g], preferred_element_type=F32) + vec_ref[1:2, lanes])
            gate_i = _sigmoid(jnp.dot(xrb, wx_ref[g], preferred_element_type=F32) + vec_ref[2:3, lanes])
            log_a = (-C_RG * r) * jax.nn.softplus(vec_ref[3:4, lanes])
            a = jnp.exp(log_a)
            a_scr[:, lanes] = a
            b_scr[:, lanes] = xr * gate_i * jnp.where(start, 1.0, jnp.sqrt(_one_minus_square(a, log_a)))

        sub = lax.broadcasted_iota(jnp.int32, (8, d), 0)

        def chunk(i, h):
            a = a_scr[pl.ds(i * 8, 8), :]
            b = b_scr[pl.ds(i * 8, 8), :]
            for s in (1, 2, 4):
                keep = sub >= s
                b = jnp.where(keep, a * pltpu.roll(b, s, 0) + b, b)
                a = jnp.where(keep, a * pltpu.roll(a, s, 0), a)
            hh = a * h + b
            hr_o[pl.ds(i * 8, 8), :] = hh
            return hh[7:8, :]

        carry[...] = lax.fori_loop(0, ts // 8, chunk, carry[...])
        pos = (rows + t * ts + 1).astype(F32)
        for g, win in enumerate(POOL_WINDOWS):
            lanes = slice(g * BLOCK, (g + 1) * BLOCK)
            gelu, _ = _gelu_parts(y_ref[:, lanes])
            ga_o[:, lanes] = (gelu * hr_o[:, lanes]).astype(BF16)
            u = u_ref[:, lanes]
            sums, have = jnp.concatenate([jnp.where(t == 0, 0.0, uh_ref[:, lanes]), u], axis=0), 1
            while have < win:
                sums = sums + pltpu.roll(sums, have, 0)
                have *= 2
            pb = (sums[POOL_HALO:] * (1.0 / jnp.minimum(pos, float(win))) - u).astype(BF16)
            p_o[:, lanes] = pb
            mixed = jnp.dot(pb, wp_ref[g], preferred_element_type=F32) + vec_ref[4:5, lanes]
            pooled_o[:, lanes] = (mixed * vec_ref[5:6, lanes]).astype(BF16)

    halo_x = pl.BlockSpec((CONV_HALO, d), lambda t: (jnp.maximum(t * (ts // CONV_HALO) - 1, 0), 0))
    halo_u = pl.BlockSpec((POOL_HALO, d), lambda t: (jnp.maximum(t * (ts // POOL_HALO) - 1, 0), 2))
    wspec = pl.BlockSpec((N_BLOCKS, BLOCK, BLOCK), lambda t: (0, 0, 0))
    return _hosted(
        stage, body, name="seq_forward", grid=(s_rows // ts,),
        out_shape=[SDS((s_rows, d), F32), SDS((s_rows, d), F32), SDS((s_rows, d), BF16), SDS((s_rows, d), BF16), SDS((s_rows, d), BF16)],
        in_specs=[_row_spec(ts, d, 0), halo_x, _row_spec(ts, d, 1), _row_spec(ts, d, 2), halo_u, _vec_spec(CONV_WIDTH, d), _vec_spec(8, d),
                  wspec, wspec, wspec],
        out_specs=[_row_spec(ts, d)] * 5,
        scratch_shapes=[pltpu.VMEM((1, d), F32), pltpu.VMEM((ts, d), F32), pltpu.VMEM((ts, d), F32)],
        compiler_params=_params("arbitrary"),
    )(proj, proj, proj, proj, proj, conv_w, vecs, w_a, w_x, w_p)


def _branches(ga, pooled, proj, w_a, w_b, *, ts, stage=None):
    s_rows, d = ga.shape

    def body(a_ref, p_ref, ga_ref, gb_ref, wa_ref, wb_ref, merged_o, sa_o, sb_o, ta_o, tb_o):
        bra = jnp.dot(a_ref[...], wa_ref[0], preferred_element_type=F32)
        brb = jnp.dot(p_ref[...], wb_ref[0], preferred_element_type=F32)
        sa, sb = _sigmoid(ga_ref[...]), _sigmoid(gb_ref[...])
        merged_o[...] = (sa * bra + sb * brb).astype(BF16)
        sa_o[...] = sa.astype(BF16)
        sb_o[...] = sb.astype(BF16)
        ta_o[...] = (bra * (sa * (1.0 - sa))).astype(BF16)
        tb_o[...] = (brb * (sb * (1.0 - sb))).astype(BF16)

    rows = _row_spec(ts, d)
    wspec = pl.BlockSpec((1, d, d), lambda *g: (0, 0, 0), pipeline_mode=pl.Buffered(1))
    return _hosted(
        stage, body, name="branches", grid=(s_rows // ts,), out_shape=[SDS((s_rows, d), BF16)] * 5,
        in_specs=[rows, rows, _row_spec(ts, d, 3), _row_spec(ts, d, 4), wspec, wspec], out_specs=[rows] * 5,
        compiler_params=_params("arbitrary"),
    )(ga, pooled, proj, proj, w_a, w_b)


def _seq_backward_by_block(proj, xr, hr, p, dga, dpooled, dgab, conv_w, vecs, w_a, w_x, w_p, *, ts, stage=None):
    s_rows = proj.shape[0]
    d = D_MODEL
    n_t = s_rows // ts
    nn, nt, tn = (((1,), (0,)), ((), ())), (((1,), (1,)), ((), ())), (((0,), (0,)), ((), ()))

    def dot(lhs, rhs, dims):
        return lax.dot_general(lhs, rhs, dims, preferred_element_type=F32)

    def body(x_ref, y_ref, xr_ref, hr_ref, hh_ref, p_ref, dga_ref, dpl_ref, dgab_ref, cw_ref, vec_ref, wa_ref, wx_ref, wp_ref,
             dproj_o, acc_o, dwa_o, dwx_o, dwp_o, g_carry, dxr_carry, q_carry, a_scr, b_scr, g_scr, a_keep, r_keep, i_keep, m_keep):
        step = pl.program_id(0)
        t = n_t - 1 - step

        @pl.when(step == 0)
        def _():
            for ref in (acc_o, dwa_o, dwx_o, dwp_o, g_carry, dxr_carry, q_carry):
                ref[...] = jnp.zeros_like(ref)

        rows = lax.broadcasted_iota(jnp.int32, (ts, BLOCK), 0)
        start = (rows == 0) & (t == 0)

        def add_sum(row, lanes, value):
            acc_o[row:row + 1, lanes] += jnp.sum(value, axis=0, keepdims=True)

        for g in range(N_BLOCKS):
            lanes = slice(g * BLOCK, (g + 1) * BLOCK)
            xrb = xr_ref[:, lanes].astype(BF16)
            r = _sigmoid(dot(xrb, wa_ref[g], nn) + vec_ref[1:2, lanes])
            gate_i = _sigmoid(dot(xrb, wx_ref[g], nn) + vec_ref[2:3, lanes])
            log_a = (-C_RG * r) * jax.nn.softplus(vec_ref[3:4, lanes])
            a = jnp.exp(log_a)
            a_keep[:, lanes], r_keep[:, lanes], i_keep[:, lanes], m_keep[:, lanes] = a, r, gate_i, _one_minus_square(a, log_a)
            gelu, dgelu = _gelu_parts(y_ref[:, lanes])
            dga_t = dga_ref[:, lanes].astype(F32)
            dproj_o[:, d + g * BLOCK:d + (g + 1) * BLOCK] = (dga_t * hr_ref[:, lanes] * dgelu).astype(BF16)
            a_scr[:, lanes] = jnp.where(rows == ts - 1, 1.0, pltpu.roll(a, ts - 1, 0))
            b_scr[:, lanes] = dga_t * gelu

        sub = lax.broadcasted_iota(jnp.int32, (8, d), 0)

        def chunk(i, g_next):
            at = pl.multiple_of((ts // 8 - 1 - i) * 8, 8)
            aa = a_scr[pl.ds(at, 8), :]
            bb = b_scr[pl.ds(at, 8), :]
            for s in (1, 2, 4):
                keep = sub < 8 - s
                bb = jnp.where(keep, bb + aa * pltpu.roll(bb, 8 - s, 0), bb)
                aa = jnp.where(keep, aa * pltpu.roll(aa, 8 - s, 0), aa)
            gg = aa * g_next + bb
            g_scr[pl.ds(at, 8), :] = gg
            return gg[0:1, :]

        g_first = lax.fori_loop(0, ts // 8, chunk, g_carry[...])
        g_carry[...] = a_keep[0:1, :] * g_first

        pos = (rows + t * ts + 1).astype(F32)
        for g, win in enumerate(POOL_WINDOWS):
            lanes = slice(g * BLOCK, (g + 1) * BLOCK)
            gs, xr, a, r, gate_i, m_square = g_scr[:, lanes], xr_ref[:, lanes], a_keep[:, lanes], r_keep[:, lanes], i_keep[:, lanes], m_keep[:, lanes]
            sp = jax.nn.softplus(vec_ref[3:4, lanes])
            mult = jnp.where(start, 1.0, jnp.sqrt(m_square))
            h_before = jnp.where(t == 0, 0.0, hh_ref[CONV_HALO - 1:CONV_HALO, lanes])
            h_prev = jnp.where(rows == 0, h_before, pltpu.roll(hr_ref[:, lanes], 1, 0))
            gm = gs * mult
            dxr = gm * gate_i
            d_i = gm * xr
            d_mult = (gs * xr) * gate_i
            dlog_a = (gs * h_prev) * a - jnp.where(start, 0.0, d_mult * (a * a) * lax.rsqrt(m_square))
            dzr = (dlog_a * (-C_RG * sp)) * (r * (1.0 - r))
            dzi = d_i * (gate_i * (1.0 - gate_i))
            add_sum(5, lanes, dzr)
            add_sum(6, lanes, dzi)
            acc_o[7:8, lanes] += jnp.sum(dlog_a * r, axis=0, keepdims=True) * (-C_RG * jax.nn.sigmoid(vec_ref[3:4, lanes]))
            xrb, dzrb, dzib = xr.astype(BF16), dzr.astype(BF16), dzi.astype(BF16)
            dwa_o[g] += dot(xrb, dzrb, tn)
            dwx_o[g] += dot(xrb, dzib, tn)
            dxr = dxr + dot(dzrb, wa_ref[g], nt) + dot(dzib, wx_ref[g], nt)

            x = x_ref[:, lanes]
            ext = jnp.concatenate([dxr, dxr_carry[:, lanes]], axis=0)
            dx = dxr * cw_ref[CONV_WIDTH - 1:CONV_WIDTH, lanes]
            add_sum(CONV_WIDTH - 1, lanes, x * dxr)
            for j in range(CONV_WIDTH - 1):
                ahead = pltpu.roll(ext, ts + CONV_HALO - (CONV_WIDTH - 1 - j), 0)[:ts]
                dx = dx + ahead * cw_ref[j:j + 1, lanes]
                add_sum(j, lanes, x * ahead)
            add_sum(4, lanes, dxr)
            dxr_carry[:, lanes] = dxr[0:CONV_HALO, :]
            dproj_o[:, lanes] = dx.astype(BF16)

            pb = p_ref[:, lanes]
            mixed = dot(pb, wp_ref[g], nn) + vec_ref[4:5, lanes]
            dpl = dpl_ref[:, lanes].astype(F32)
            add_sum(9, lanes, dpl * mixed)
            dmixed = dpl * vec_ref[5:6, lanes]
            add_sum(8, lanes, dmixed)
            dmb = dmixed.astype(BF16)
            dwp_o[g] += dot(pb, dmb, tn)
            dp = dot(dmb, wp_ref[g], nt)
            q = dp * (1.0 / jnp.minimum(pos, float(win)))
            sums, have = jnp.concatenate([q, q_carry[:, lanes]], axis=0), 1
            while have < win:
                sums = sums + pltpu.roll(sums, ts + POOL_HALO - have, 0)
                have *= 2
            q_carry[:, lanes] = q[0:POOL_HALO, :]
            dproj_o[:, 2 * d + g * BLOCK:2 * d + (g + 1) * BLOCK] = (sums[:ts] - dp).astype(BF16)

        dproj_o[:, 3 * d:5 * d] = dgab_ref[...]

    def rev(cols, col_block=0):
        return pl.BlockSpec((ts, cols), lambda i: (n_t - 1 - i, col_block))

    halo_h = pl.BlockSpec((CONV_HALO, d), lambda i: (jnp.maximum((n_t - 1 - i) * (ts // CONV_HALO) - 1, 0), 0))
    wspec = pl.BlockSpec((N_BLOCKS, BLOCK, BLOCK), lambda i: (0, 0, 0))
    return _hosted(
        stage, body, name="seq_backward", grid=(n_t,),
        out_shape=[SDS((s_rows, 5 * d), BF16), SDS((16, d), F32)] + [SDS((N_BLOCKS, BLOCK, BLOCK), F32)] * 3,
        in_specs=[rev(d, 0), rev(d, 1), rev(d), rev(d), halo_h, rev(d), rev(d), rev(d), rev(2 * d), _vec_spec(CONV_WIDTH, d), _vec_spec(8, d),
                  wspec, wspec, wspec],
        out_specs=[rev(5 * d), _vec_spec(16, d), wspec, wspec, wspec],
        scratch_shapes=[pltpu.VMEM((1, d), F32), pltpu.VMEM((CONV_HALO, d), F32), pltpu.VMEM((POOL_HALO, d), F32)] + [pltpu.VMEM((ts, d), F32)] * 7,
        compiler_params=_params("arbitrary"),
    )(proj, proj, xr, hr, hr, p, dga, dpooled, dgab, conv_w, vecs, w_a, w_x, w_p)


def _adamw_math(w, g, m, v):
    m = ADAM_B1 * m + (1.0 - ADAM_B1) * g
    v = ADAM_B2 * v + (1.0 - ADAM_B2) * jnp.square(g)
    m_hat = m / (1.0 - ADAM_B1 ** ADAM_STEP)
    v_hat = v / (1.0 - ADAM_B2 ** ADAM_STEP)
    delta = -ADAM_LR * (m_hat / (jnp.sqrt(v_hat) + ADAM_EPS) + ADAM_WD * w)
    return delta, m, v


def _adamw(w, g, m, v, name):
    rows, cols = w.shape
    tr = min(rows, 512)

    def body(w_ref, g_ref, m_ref, v_ref, g_out, d_ref, mo_ref, vo_ref):
        g = g_ref[...]
        g_out[...] = g
        d_ref[...], mo_ref[...], vo_ref[...] = _adamw_math(w_ref[...], g, m_ref[...], v_ref[...])

    spec = pl.BlockSpec((tr, cols), lambda i: (i, 0))
    return pl.pallas_call(
        body, name=name, grid=(rows // tr,), out_shape=[SDS((rows, cols), F32)] * 4, in_specs=[spec] * 4, out_specs=[spec] * 4,
        compiler_params=_params("parallel"),
    )(w, g, m, v)


def _prologue(c, conv_shard, w_ada, b_cols, w_in_halves, gate_halves):
    n_cols = w_ada.shape[1]
    n_g = len(gate_halves)
    half_rows = w_in_halves.shape[1] // 2
    n_sems = 3 * N_DEV + N_CHIPS * n_g + 5 + 2 + 3

    def body(c_ref, conv_ref, w_ref, b_ref, win_ref, *rest):
        gate_refs = rest[:n_g]
        act_o, mod_o, conv_o, win_o = rest[n_g:n_g + 4]
        gate_o = rest[n_g + 4:2 * n_g + 4]
        c_all, mod_cols, w_vmem, send_sems, recv_sems = rest[2 * n_g + 4:]
        fetch_w = pltpu.make_async_copy(w_ref, w_vmem, send_sems.at[n_sems])
        x, y, c = _coords()
        me, k_me = 4 * x + 2 * y + c, 2 * x + y
        sibling, x_nbr, y_nbr = (x, y, 1 - c), (1 - x, y, c), (x, 1 - y, c)
        k_x, k_y, k_d = 2 * (1 - x) + y, 2 * x + (1 - y), 2 * (1 - x) + (1 - y)
        counter = iter(range(n_sems))

        def copy(src, dst, to, landing):
            s = next(counter)
            out = pltpu.make_async_remote_copy(src_ref=src, dst_ref=dst, send_sem=send_sems.at[s], recv_sem=recv_sems.at[s],
                                               device_id=to, device_id_type=MESH)
            arrival = pltpu.make_async_remote_copy(src_ref=landing, dst_ref=landing, send_sem=send_sems.at[s], recv_sem=recv_sems.at[s],
                                                   device_id=(x, y, c), device_id_type=MESH)
            return out, arrival

        def allgather(src, out):
            s = next(counter)
            own = pltpu.make_async_copy(src, out.at[me], send_sems.at[s])
            peers = [copy(src, out.at[me], (_flip(x, r & 4), _flip(y, r & 2), _flip(c, r & 1)), out.at[me ^ r]) for r in range(1, N_DEV)]
            return own, peers

        def start(group):
            own, peers = group
            own.start()
            for out, _ in peers:
                out.start()

        def finish(group):
            own, peers = group
            for out, arrival in peers:
                arrival.wait_recv()
                out.wait_send()
            own.wait()

        rows_a, rows_b = pl.ds(0, half_rows), pl.ds(half_rows, half_rows)
        c_group, conv_group = allgather(c_ref, c_all), allgather(conv_ref, conv_o)
        gates = []
        for g_ref, g_out in zip(gate_refs, gate_o):
            gates.append(copy(g_ref, g_out.at[k_me], sibling, g_out.at[k_me]))
            gates += [copy(g_ref.at[c], g_out.at[k_me, c], (px, py, c), g_out.at[2 * px + py, c]) for px, py in _chips()]
        own_in = copy(win_ref, win_o.at[k_me], sibling, win_o.at[k_me])
        to_x = [copy(win_ref.at[c, rows], win_o.at[k_me, c, rows], x_nbr, win_o.at[k_x, c, rows]) for rows in (rows_a, rows_b)]
        to_y = [copy(win_ref.at[c, rows], win_o.at[k_me, c, rows], y_nbr, win_o.at[k_y, c, rows]) for rows in (rows_b, rows_a)]
        mod_group = allgather(mod_cols, mod_o)
        relay_y = copy(win_o.at[k_x, c, rows_a], win_o.at[k_x, c, rows_a], y_nbr, win_o.at[k_d, c, rows_a])
        relay_x = copy(win_o.at[k_y, c, rows_b], win_o.at[k_y, c, rows_b], x_nbr, win_o.at[k_d, c, rows_b])
        sib_x, sib_y, sib_d = (copy(win_o.at[k, c], win_o.at[k, c], sibling, win_o.at[k, 1 - c]) for k in (k_x, k_y, k_d))

        start(c_group)
        for out, _ in to_x + to_y + [own_in] + gates:
            out.start()
        start(conv_group)
        fetch_w.start()
        finish(c_group)
        cv = jnp.concatenate([c_all[r] for r in range(N_DEV)], axis=0)
        act = cv * jax.nn.sigmoid(cv)
        act_o[...] = act
        fetch_w.wait()
        mod_cols[...] = jnp.dot(act.astype(BF16), w_vmem[...].astype(BF16), preferred_element_type=F32) + b_ref[...]
        start(mod_group)
        to_x[0][1].wait_recv()
        relay_y[0].start()
        to_y[0][1].wait_recv()
        relay_x[0].start()
        finish(mod_group)
        finish(conv_group)
        to_x[1][1].wait_recv()
        sib_x[0].start()
        to_y[1][1].wait_recv()
        sib_y[0].start()
        for out, arrival in gates + [own_in]:
            arrival.wait_recv()
            out.wait_send()
        for out, arrival in (relay_y, relay_x):
            arrival.wait_recv()
            out.wait_send()
        sib_d[0].start()
        for out, _ in to_x + to_y:
            out.wait_send()
        for out, arrival in (sib_x, sib_y, sib_d):
            arrival.wait_recv()
            out.wait_send()

    gate_shapes = [SDS((N_CHIPS,) + g.shape, g.dtype) for g in gate_halves]
    return pl.pallas_call(
        body, name="prologue",
        out_shape=[SDS((N_DEV, c.shape[1]), F32), SDS((N_DEV, N_DEV, n_cols), F32), SDS((N_DEV,) + conv_shard.shape, F32),
                   SDS((N_CHIPS,) + w_in_halves.shape, w_in_halves.dtype)] + gate_shapes,
        in_specs=[VMEM_SPEC, VMEM_SPEC, HBM_SPEC, VMEM_SPEC] + [HBM_SPEC] * (1 + n_g), out_specs=[VMEM_SPEC] * 3 + [HBM_SPEC] * (1 + n_g),
        scratch_shapes=[pltpu.VMEM((N_DEV,) + c.shape, F32), pltpu.VMEM((N_DEV, n_cols), F32), pltpu.VMEM(w_ada.shape, F32),
                        pltpu.SemaphoreType.DMA((n_sems + 1,)), pltpu.SemaphoreType.DMA((n_sems,))],
        compiler_params=pltpu.CompilerParams(vmem_limit_bytes=VMEM_LIMIT_V7X),
    )(c, conv_shard, w_ada, b_cols, w_in_halves, *gate_halves)


def _ada_backward(act_t, dmod_cols, w, m, v):
    rows, cols = w.shape
    tn = 512

    def body(a_ref, dm_ref, w_ref, m_ref, v_ref, g_ref, d_ref, mo_ref, vo_ref):
        g = jnp.dot(a_ref[...].astype(BF16), dm_ref[...].astype(BF16), preferred_element_type=F32)
        g_ref[...] = g
        d_ref[...], mo_ref[...], vo_ref[...] = _adamw_math(w_ref[...], g, m_ref[...], v_ref[...])

    spec = pl.BlockSpec((rows, tn), lambda j: (0, j))
    return pl.pallas_call(
        body, name="ada_backward", grid=(cols // tn,), out_shape=[SDS((rows, cols), F32)] * 4,
        in_specs=[pl.BlockSpec((rows, N_DEV), lambda j: (0, 0)), pl.BlockSpec((N_DEV, tn), lambda j: (0, j)), spec, spec, spec],
        out_specs=[spec] * 4, compiler_params=_params("parallel"),
    )(act_t, dmod_cols, w, m, v)


def _sum_rows(gathered, n_loss):
    n = gathered.shape[2]

    def body(g_ref, o_ref):
        acc = g_ref[0]
        for r in range(1, N_DEV):
            acc = acc + g_ref[r]
        o_ref[...] = acc
        o_ref[:, n - n_loss:n] = jnp.broadcast_to(jnp.sum(acc[:, n - n_loss:n], axis=1, keepdims=True), (1, n_loss))

    return pl.pallas_call(body, name="sum_rows", out_shape=SDS((1, n), F32), in_specs=[VMEM_SPEC], out_specs=VMEM_SPEC)(gathered)


def _adamw_vectors(total, offsets, separate, params):
    n_p = len(params)
    apart = [g for g in separate if g is not None]

    def body(*refs):
        total_ref, apart_refs = refs[0], list(refs[1:1 + len(apart)])
        ins, outs = refs[1 + len(apart):1 + len(apart) + 3 * n_p], refs[1 + len(apart) + 3 * n_p:]
        for i in range(n_p):
            w_ref, m_ref, v_ref = ins[3 * i:3 * i + 3]
            g = apart_refs.pop(0)[...] if offsets[i] is None else total_ref[:, offsets[i]:offsets[i] + w_ref.shape[1]]
            outs[4 * i][...] = g
            outs[4 * i + 1][...], outs[4 * i + 2][...], outs[4 * i + 3][...] = _adamw_math(w_ref[...], g, m_ref[...], v_ref[...])

    flat = [a for p in params for a in p]
    res = pl.pallas_call(
        body, name="adamw_vectors", out_shape=[SDS(p[0].shape, F32) for p in params for _ in range(4)],
        in_specs=[VMEM_SPEC] * (1 + len(apart) + len(flat)), out_specs=[VMEM_SPEC] * (4 * n_p),
    )(total, *apart, *flat)
    return [tuple(res[4 * i:4 * i + 4]) for i in range(n_p)]


def _blocks_to_pieces(w):
    nb, rows, n = w.shape
    q = rows // N_CHIPS
    return w.reshape(nb, N_CHIPS, q, n).transpose(1, 0, 2, 3).reshape(N_CHIPS, 2, nb // 2, q, n)


def _pieces_to_blocks(w):
    n_chips, _, half, q, n = w.shape
    return w.reshape(n_chips, 2 * half, q, n).transpose(1, 0, 2, 3).reshape(2 * half, n_chips * q, n)


def kernel(x, c, norm_mix_g, norm_mlp_g, w_ada, b_ada, w_in, conv_w, conv_b, w_rg_a, b_rg_a, w_rg_x, b_rg_x, a_param, w_branch_a, w_pool, b_pool, pool_scale, w_branch_b, w_out, w_up, w_down, final_g, loss_target, m_norm_mix_g, m_norm_mlp_g, m_w_ada, m_b_ada, m_w_in, m_conv_w, m_conv_b, m_w_rg_a, m_b_rg_a, m_w_rg_x, m_b_rg_x, m_a_param, m_w_branch_a, m_w_pool, m_b_pool, m_pool_scale, m_w_branch_b, m_w_out, m_w_up, m_w_down, m_final_g, v_norm_mix_g, v_norm_mlp_g, v_w_ada, v_b_ada, v_w_in, v_conv_w, v_conv_b, v_w_rg_a, v_b_rg_a, v_w_rg_x, v_b_rg_x, v_a_param, v_w_branch_a, v_w_pool, v_b_pool, v_pool_scale, v_w_branch_b, v_w_out, v_w_up, v_w_down, v_final_g):
    d = D_MODEL
    s_rows = x.shape[1]
    ts, tsq = min(512, s_rows), min(256, s_rows)
    xi, yi, ci = _coords()
    k_me, me = 2 * xi + yi, 4 * xi + 2 * yi + ci
    ada_cols = w_ada.shape[2]
    conv_cols = conv_w.shape[2]
    n_in, n_up = w_in.shape[2], w_up.shape[2]

    names = ("w_in", "w_up", "w_down", "w_a", "w_b", "w_out", "rg_a", "rg_x", "pool")
    mine = dict(zip(names, (w_in[0], w_up[0], w_down[0], w_branch_a[0], w_branch_b[0], w_out[0], w_rg_a[0], w_rg_x[0], w_pool[0])))
    moments_m = dict(zip(names, (m_w_in[0], m_w_up[0], m_w_down[0], m_w_branch_a[0], m_w_branch_b[0], m_w_out[0], m_w_rg_a[0], m_w_rg_x[0], m_w_pool[0])))
    moments_v = dict(zip(names, (v_w_in[0], v_w_up[0], v_w_down[0], v_w_branch_a[0], v_w_branch_b[0], v_w_out[0], v_w_rg_a[0], v_w_rg_x[0], v_w_pool[0])))
    block_weights, squares = ("rg_a", "rg_x", "pool"), ("w_a", "w_b", "w_out")
    place = jnp.stack([ci, k_me]).astype(jnp.int32)
    g_mix, g_mlp, g_fin = norm_mix_g, norm_mlp_g, final_g.reshape(1, d)
    x0, target = x[0], loss_target[0]
    wg = functools.partial(_weight_grad, tk=min(1024, d))
    plain, back, back_sum = (dict(transposed=t, reduce_blocks=r) for t, r in ((False, False), (True, False), (True, True)))
    rows_d, vec8, vec1 = _row_spec(ts, d), _vec_spec(8, d), _vec_spec(1, d)
    tw = min(1024, s_rows)
    halves_of_tile = 2

    def out_rows(cols, dtype, t=ts):
        return SDS((s_rows, cols), dtype), _row_spec(t, cols)

    sums_out = (SDS((8, d), F32), vec8)

    def halves(n):
        return mine[n].astype(BF16).reshape(2, -1, mine[n].shape[-1])

    def blocks(w):
        return _pieces_to_blocks(w.reshape(N_CHIPS, 2, N_BLOCKS // 2, BLOCK // N_CHIPS, BLOCK))

    act_all, mod_all, conv_all, w_in_whole, *gate_wholes = _prologue(
        c, conv_w[0], w_ada[0], lax.dynamic_slice_in_dim(b_ada, k_me * ada_cols, ada_cols, axis=1), halves("w_in"), [halves(n) for n in block_weights])
    conv_full = conv_all[0::2].transpose(1, 0, 2).reshape(CONV_WIDTH, d)
    mod_mine = lax.dynamic_index_in_dim(mod_all, me, axis=1, keepdims=False)[0::2]
    mod = jnp.concatenate([mod_mine.reshape(N_MOD, d), jnp.zeros((8 - N_MOD, d), F32)], axis=0)
    w_in_all = w_in_whole.reshape(N_CHIPS, d, n_in)
    vecs = jnp.concatenate([conv_b, b_rg_a, b_rg_x, a_param, b_pool, pool_scale, jnp.zeros((2, d), F32)], axis=0)

    def norm_first(a_ref, extra_refs, out_refs):
        h = _modulated_norm(a_ref[...], extra_refs[0], extra_refs[1], 1, 0)
        out_refs[1][...] = h
        return h

    stage = _Stage()
    i_blk = [_add_gather_d2d(stage, g) for g in gate_wholes]
    i_sq = [_add_gather_ici(stage, halves(n)) for n in squares]
    down_rows = mine["w_down"].shape[0] // 8
    i_down = _add_gather_ici(stage, halves("w_down"), rows=(0, down_rows))
    (proj, h1), got = _matmul(x0, w_in_all, name="proj_in", tm=ts, extra=[(mod, vec8), (g_mix, vec1)], prepare=norm_first,
                              outs=[out_rows(N_CHIPS * n_in, F32), out_rows(d, BF16)], finish=_store_blocks(n_in), parts=halves_of_tile,                              stage=stage, **plain)
    w_rg_a_all, w_rg_x_all, w_pool_all = (blocks(got[i]) for i in i_blk)
    stage = _Stage()
    i_sq = [_add_gather_d2d(stage, got[i]) for i in i_sq]
    i_up = _add_gather_ici(stage, halves("w_up"))
    w_down_so_far = got[i_down]
    (xr, hr, ga, p, pooled), got = _seq_forward(proj, conv_full, vecs, w_rg_a_all, w_rg_x_all, w_pool_all, ts=tsq, stage=stage)
    w_a_all, w_b_all, w_out_all = (got[i].reshape(1, d, d) for i in i_sq)
    stage = _Stage()
    i_up = _add_gather_d2d(stage, got[i_up])
    i_down = _add_gather_ici(stage, halves("w_down"), rows=(down_rows, 2 * down_rows), whole=w_down_so_far)
    (merged, s_a, s_b, t_a, t_b), got = _branches(ga, pooled, proj, w_a_all, w_b_all, ts=ts, stage=stage)
    w_up_all = got[i_up].reshape(N_CHIPS, d, n_up)
    stage = _Stage()
    i_down = _add_gather_ici(stage, halves("w_down"), rows=(3 * down_rows, down_rows), whole=got[i_down])

    def residual_norm(k, acc, extra_refs, out_refs):
        x_ref, mod_ref, g_ref = extra_refs
        out_refs[0][...] = acc.astype(BF16)
        x2_t = x_ref[...] + mod_ref[2:3, :] * acc
        out_refs[1][...] = x2_t
        out_refs[2][...] = _modulated_norm(x2_t, mod_ref, g_ref, 4, 3)

    (mo, x2, h2), got = _matmul(merged, w_out_all, name="mix_out", tm=ts, extra=[(x0, rows_d), (mod, vec8), (g_mlp, vec1)],
                                outs=[out_rows(d, BF16), out_rows(d, F32), out_rows(d, BF16)], finish=residual_norm, parts=halves_of_tile,                                stage=stage, transposed=False, reduce_blocks=True)
    stage = _Stage()
    i_down = _add_gather_d2d(stage, got[i_down])

    def relu_squared(k, acc, extra_refs, out_refs):
        out_refs[0][:, k * n_up:(k + 1) * n_up] = jnp.square(jnp.maximum(acc, 0.0)).astype(BF16)

    (ff,), got = _matmul(h2, w_up_all, name="mlp_up", tm=ts, outs=[out_rows(D_FF, BF16)], finish=relu_squared, parts=halves_of_tile,
                         stage=stage, **plain)
    w_down_all = got[i_down].reshape(N_CHIPS, D_FF // N_CHIPS, d)

    def loss_head(k, acc, extra_refs, out_refs):
        x2_ref, mod_ref, g_ref, t_ref = extra_refs
        dx_ref, df_ref, acc_ref = out_refs
        xh, r = _rms(x2_ref[...] + mod_ref[5:6, :] * acc)
        err = xh * g_ref[...] - t_ref[...]
        dy = err / d
        dxh = dy * g_ref[...]
        dx3_t = r * (dxh - xh * jnp.mean(dxh * xh, axis=-1, keepdims=True))
        dx_ref[...] = dx3_t
        df_ref[...] = (dx3_t * mod_ref[5:6, :]).astype(BF16)
        _accumulate(acc_ref, 0, dy * xh)
        _accumulate(acc_ref, 1, dx3_t * acc)
        _accumulate(acc_ref, 2, jnp.square(err) * (0.5 / d))

    dx3, dffo, sums_head = _matmul(ff, w_down_all, name="mlp_down", tm=ts, extra=[(x2, rows_d), (mod, vec8), (g_fin, vec1), (target, rows_d)],
                                   outs=[out_rows(d, F32), out_rows(d, BF16), sums_out], finish=loss_head, zero_first=(2,), parts=halves_of_tile,                                   transposed=False, reduce_blocks=True)

    partial_of, from_sibling, chip_sum, from_chips, half_done, quarter = {}, {}, {}, {}, {}, {}

    def publish(n, g):
        g = _blocks_to_pieces(g).astype(BF16) if n in block_weights else g
        partial_of[n] = g.reshape(N_CHIPS, 2, -1, mine[n].shape[-1])

    def exchange(to_sibling=(), to_chips=(), swap=()):
        st = _Stage()
        slots = [(n, from_sibling, _add_reduce_d2d(st, partial_of[n])) for n in to_sibling]
        slots += [(n, from_chips, _add_reduce_ici(st, chip_sum[n])) for n in to_chips]
        slots += [(n, quarter, _add_swap_halves(st, half_done[n])) for n in swap]
        return st, slots

    def collect(slots, outs):
        for n, where, i in slots:
            where[n] = outs[i]

    def sum_pairs(*ns):
        for n in ns:
            chip_sum[n] = _sum_pair(partial_of[n], from_sibling[n], place, "sum_pair_" + n)

    def sum_quarters(*ns):
        for n in ns:
            half_done[n] = _sum_quarter(partial_of[n], from_sibling[n], from_chips[n], place, "sum_quarter_" + n)

    publish("w_down", wg(ff, dffo, nb=1, name="grad_w_down"))

    def relu_squared_backward(k, acc, extra_refs, out_refs):
        cols = slice(k * n_up, (k + 1) * n_up)
        out_refs[0][:, cols] = (acc * (2.0 * jnp.sqrt(extra_refs[0][:, cols].astype(F32)))).astype(BF16)

    stage, slots = exchange(to_sibling=["w_down"])
    (dup,), got = _matmul(dffo, w_down_all, name="d_mlp_down", tm=ts, extra=[(ff, _row_spec(ts, D_FF))], outs=[out_rows(D_FF, BF16)],
                          finish=relu_squared_backward, parts=halves_of_tile, stage=stage, **back)
    collect(slots, got)
    sum_pairs("w_down")

    def norm_mlp_backward(k, acc, extra_refs, out_refs):
        x2_ref, dres_ref, mod_ref, g_ref, mo_ref = extra_refs
        dx_ref, dmo_ref, acc_ref = out_refs
        dx =_modulated_norm_backward(acc, x2_ref[...], dres_ref[...], mod_ref, g_ref, acc_ref, 4)
        dx_ref[...] = dx
        dmo_ref[...] = (dx * mod_ref[2:3, :]).astype(BF16)
        _accumulate(acc_ref, 3, dx * mo_ref[...].astype(F32))

    stage, slots = exchange(to_chips=["w_down"])
    (dx2, dmo, sums_mlp), got = _matmul(dup, w_up_all, name="d_mlp_up", tm=ts, stage=stage, finish=norm_mlp_backward,
                                        extra=[(x2, rows_d), (dx3, rows_d), (mod, vec8), (g_mlp, vec1), (mo, rows_d)],
                                        outs=[out_rows(d, F32), out_rows(d, BF16), sums_out], zero_first=(2,), parts=halves_of_tile, **back_sum)
    collect(slots, got)
    sum_quarters("w_down")
    publish("w_up", wg(h2, dup, nb=N_CHIPS, name="grad_w_up"))
    publish("w_out", wg(merged, dmo, nb=1, name="grad_w_out"))

    def merge_backward(k, acc, extra_refs, out_refs):
        sa_ref, sb_ref, ta_ref, tb_ref = extra_refs
        out_refs[0][...] = (acc * sa_ref[...].astype(F32)).astype(BF16)
        out_refs[1][...] = (acc * sb_ref[...].astype(F32)).astype(BF16)
        out_refs[2][:, 0:d] = (acc * ta_ref[...].astype(F32)).astype(BF16)
        out_refs[2][:, d:2 * d] = (acc * tb_ref[...].astype(F32)).astype(BF16)

    stage, slots = exchange(to_sibling=["w_up", "w_out"], swap=["w_down"])
    (dbr_a, dbr_b, dgab), got = _matmul(dmo, w_out_all, name="d_mix_out", tm=ts, stage=stage, finish=merge_backward,
                                        extra=[(s_a, rows_d), (s_b, rows_d), (t_a, rows_d), (t_b, rows_d)],
                                        outs=[out_rows(d, BF16), out_rows(d, BF16), out_rows(2 * d, BF16)], parts=halves_of_tile, **back_sum)
    collect(slots, got)
    publish("w_a", wg(ga, dbr_a, nb=1, name="grad_w_branch_a"))
    publish("w_b", wg(pooled, dbr_b, nb=1, name="grad_w_branch_b"))
    dga, = _matmul(dbr_a, w_a_all, name="d_branch_a", tm=tw, outs=[out_rows(d, BF16, tw)], finish=_store_blocks(d), **back_sum)
    dpooled, = _matmul(dbr_b, w_b_all, name="d_branch_b", tm=tw, outs=[out_rows(d, BF16, tw)], finish=_store_blocks(d), **back_sum)
    sum_pairs("w_up", "w_out")
    stage, slots = exchange(to_sibling=["w_a", "w_b"], to_chips=["w_up", "w_out"])
    (dproj, sums_seq, d_rg_a, d_rg_x, d_pool), got = _seq_backward_by_block(proj, xr, hr, p, dga, dpooled, dgab, conv_full, vecs,
                                                                            w_rg_a_all, w_rg_x_all, w_pool_all, ts=tsq, stage=stage)
    collect(slots, got)
    sum_quarters("w_up", "w_out")
    sum_pairs("w_a", "w_b")
    for n, g in zip(block_weights, (d_rg_a, d_rg_x, d_pool)):
        publish(n, g)
    stage, slots = exchange(to_sibling=block_weights, to_chips=["w_a", "w_b"], swap=["w_up", "w_out"])
    g_in, got = wg(h1, dproj, nb=N_CHIPS, name="grad_w_in", stage=stage)
    collect(slots, got)
    publish("w_in", g_in)
    sum_pairs(*block_weights)
    sum_quarters("w_a", "w_b")
    stage, slots = exchange(to_sibling=["w_in"], to_chips=block_weights)
    collect(slots, _run_stage(stage, "exchange_w_in_gates"))
    sum_pairs("w_in")
    sum_quarters(*block_weights)

    def norm_mix_backward(k, acc, extra_refs, out_refs):
        x_ref, dres_ref, mod_ref, g_ref = extra_refs
        out_refs[0][...] = _modulated_norm_backward(acc, x_ref[...], dres_ref[...], mod_ref, g_ref, out_refs[1], 1)

    stage, slots = exchange(to_chips=["w_in"], swap=block_weights + ("w_a", "w_b"))
    (grad_x, sums_mix), got = _matmul(dproj, w_in_all, name="d_proj_in", tm=ts, stage=stage, finish=norm_mix_backward,
                                      extra=[(x0, rows_d), (dx2, rows_d), (mod, vec8), (g_mix, vec1)],
                                      outs=[out_rows(d, F32), sums_out], zero_first=(1,), parts=halves_of_tile, **back_sum)
    collect(slots, got)
    sum_quarters("w_in")

    dmod = jnp.concatenate([sums_mix[1:2], sums_mix[0:1], sums_mlp[3:4], sums_mlp[1:2], sums_mlp[0:1], sums_head[1:2]], axis=1)
    row = jnp.concatenate([sums_mix[2:3], sums_mlp[2:3], sums_seq[4:5], sums_seq[5:6], sums_seq[6:7], sums_seq[7:8], sums_seq[8:9],
                           sums_seq[9:10], sums_head[0:1], sums_seq[0:CONV_WIDTH].reshape(1, CONV_WIDTH * d), dmod, sums_head[2:3]], axis=1)
    n_vec, n_conv, n_mod = 9 * d, CONV_WIDTH * d, N_MOD * d
    stage, slots = exchange(swap=["w_in"])
    i_rows = _add_allgather8(stage, row)
    got = _run_stage(stage, "exchange_sums_w_in")
    collect(slots, got)
    rows_all = got[i_rows]
    total = _sum_rows(rows_all, d)
    loss = total[0, n_vec + n_conv + n_mod]
    dmod_all = rows_all[:, 0, n_vec + n_conv:n_vec + n_conv + n_mod]
    g_ada, d_ada, m_ada, v_ada = _ada_backward(act_all.T, lax.dynamic_slice_in_dim(dmod_all, k_me * ada_cols, ada_cols, axis=1),
                                               w_ada[0], m_w_ada[0], v_w_ada[0])
    g_conv = lax.dynamic_slice_in_dim(total[:, n_vec:n_vec + n_conv].reshape(CONV_WIDTH, d), k_me * conv_cols, conv_cols, axis=1)
    vec_names = ("norm_mix_g", "norm_mlp_g", "conv_b", "b_rg_a", "b_rg_x", "a_param", "b_pool", "pool_scale", "final_g", "conv_w", "b_ada")
    vec_params = [(norm_mix_g, m_norm_mix_g, v_norm_mix_g), (norm_mlp_g, m_norm_mlp_g, v_norm_mlp_g), (conv_b, m_conv_b, v_conv_b),
                  (b_rg_a, m_b_rg_a, v_b_rg_a), (b_rg_x, m_b_rg_x, v_b_rg_x), (a_param, m_a_param, v_a_param), (b_pool, m_b_pool, v_b_pool),
                  (pool_scale, m_pool_scale, v_pool_scale), tuple(a.reshape(1, d) for a in (final_g, m_final_g, v_final_g)),
                  (conv_w[0], m_conv_w[0], v_conv_w[0]), (b_ada, m_b_ada, v_b_ada)]
    offsets = [i * d for i in range(9)] + [None, n_vec + n_conv]
    vec_res = _adamw_vectors(total, offsets, [None] * 9 + [g_conv, None], vec_params)
    vec_out = {n: [r.reshape(shape) for r in res] for n, res, shape in zip(
        vec_names, vec_res, [(1, d)] * 8 + [(d,), (1, CONV_WIDTH, conv_cols), (1, N_MOD * d)])}

    big_out = {}
    for n in names:
        shape2 = (-1, mine[n].shape[-1])
        g2 = quarter[n].reshape(shape2)
        res = _adamw(mine[n].reshape(shape2), g2, moments_m[n].reshape(shape2), moments_v[n].reshape(shape2), "adamw_" + n)
        big_out[n] = [r.reshape((1,) + mine[n].shape) for r in res]

    key = {"w_in": "w_in", "w_rg_a": "rg_a", "w_rg_x": "rg_x", "w_branch_a": "w_a", "w_pool": "pool", "w_branch_b": "w_b", "w_out": "w_out",
           "w_up": "w_up", "w_down": "w_down"}
    order = ("norm_mix_g", "norm_mlp_g", "w_ada", "b_ada", "w_in", "conv_w", "conv_b", "w_rg_a", "b_rg_a", "w_rg_x", "b_rg_x", "a_param",
             "w_branch_a", "w_pool", "b_pool", "pool_scale", "w_branch_b", "w_out", "w_up", "w_down", "final_g")
    ada_out = [g_ada[None], d_ada[None], m_ada[None], v_ada[None]]
    outs = [loss, grad_x[None]]
    for kind in range(4):
        for nme in order:
            outs.append(ada_out[kind] if nme == "w_ada" else big_out[key[nme]][kind] if nme in key else vec_out[nme][kind])
    return tuple(outs)
```

```python
import functools

import jax
import jax.numpy as jnp
from jax import lax
from jax.experimental import pallas as pl
from jax.experimental.pallas import tpu as pltpu

F32, BF16 = jnp.float32, jnp.bfloat16
D_MODEL = 1024
D_FF = 4 * D_MODEL
N_BLOCKS = 4
BLOCK = D_MODEL // N_BLOCKS
CONV_WIDTH = 4
POOL_WINDOWS = (2, 4, 8, 16)
CONV_HALO = 8
POOL_HALO = 16
N_MOD = 6
EPS = 1e-6
C_RG = 8.0
ADAM_LR, ADAM_B1, ADAM_B2, ADAM_EPS, ADAM_WD, ADAM_STEP = 0.001, 0.9, 0.999, 1e-08, 0.01, 10
N_DEV = 8
N_CHIPS = 4
VMEM_LIMIT_V7X = 56 * 2**20
MESH = pl.DeviceIdType.MESH
SDS = jax.ShapeDtypeStruct
HBM_SPEC = pl.BlockSpec(memory_space=pltpu.HBM)
VMEM_SPEC = pl.BlockSpec(memory_space=pltpu.VMEM)


def _params(*semantics):
    return pltpu.CompilerParams(dimension_semantics=semantics, vmem_limit_bytes=VMEM_LIMIT_V7X)


def _coords():
    return lax.axis_index("x"), lax.axis_index("y"), lax.axis_index("c")


def _flip(v, bit):
    return 1 - v if bit else v


def _chips():
    x, y, _ = _coords()
    return [(1 - x, y), (x, 1 - y), (1 - x, 1 - y)]


PEER_SETS = [frozenset(s) for s in (("sibling",), ("chips",), ("sibling", "chips"), ("sibling", "chips", "others"))]


class _Stage:
    def __init__(self):
        self.inputs, self.in_specs, self.outputs, self.out_specs, self.aliases = [], [], [], [], {}
        self.parts, self.n_copies, self.peers = [], 0, set()

    def add(self, inputs, in_spec, outputs, out_spec, n_copies, build, peers, alias=False):
        self.peers |= set(peers)
        i0, o0 = len(self.inputs), len(self.outputs)
        self.inputs += inputs
        self.in_specs += [in_spec] * len(inputs)
        self.outputs += outputs
        self.out_specs += [out_spec] * len(outputs)
        if alias:
            self.aliases.update({i0 + i: o0 + i for i in range(len(inputs))})
        self.parts.append((build, i0, len(inputs), o0, len(outputs)))
        self.n_copies += n_copies
        return list(range(o0, o0 + len(outputs)))

    def copies(self, in_refs, out_refs):
        out = []
        for build, i0, ni, o0, no in self.parts:
            out += build(in_refs[i0:i0 + ni], out_refs[o0:o0 + no])
        assert len(out) == self.n_copies
        return out

    def run(self, in_refs, out_refs, send_sems, recv_sems, start):
        x, y, c = _coords()
        for s, (src, dst, to, landing) in enumerate(self.copies(in_refs, out_refs)):
            if to is None:
                cp = pltpu.make_async_copy(src, dst, send_sems.at[s])
                cp.start() if start else cp.wait()
                continue
            cp = pltpu.make_async_remote_copy(src_ref=src, dst_ref=dst, send_sem=send_sems.at[s], recv_sem=recv_sems.at[s],
                                              device_id=to, device_id_type=MESH)
            if start:
                cp.start()
            else:
                pltpu.make_async_remote_copy(src_ref=landing, dst_ref=landing, send_sem=send_sems.at[s], recv_sem=recv_sems.at[s],
                                             device_id=(x, y, c), device_id_type=MESH).wait_recv()
                cp.wait_send()


def _hosted(stage, body, *, name, in_specs, out_specs, out_shape, grid=(), scratch_shapes=(), compiler_params=None):
    if stage is None:
        return pl.pallas_call(body, name=name, grid=grid, in_specs=in_specs, out_specs=out_specs, out_shape=out_shape,
                              scratch_shapes=list(scratch_shapes), compiler_params=compiler_params)
    single = not isinstance(out_shape, (list, tuple))
    h_out_shape = [out_shape] if single else list(out_shape)
    h_out_specs = [out_specs] if single else list(out_specs)
    n_in, n_out, n_scr = len(in_specs), len(h_out_shape), len(scratch_shapes)
    s_in, s_out = len(stage.inputs), len(stage.outputs)

    def wrapped(*refs):
        h_in, st_in = refs[:n_in], refs[n_in:n_in + s_in]
        h_o, st_o = refs[n_in + s_in:n_in + s_in + n_out], refs[n_in + s_in + n_out:n_in + s_in + n_out + s_out]
        h_scr = refs[n_in + s_in + n_out + s_out:n_in + s_in + n_out + s_out + n_scr]
        send_sems, recv_sems = refs[n_in + s_in + n_out + s_out + n_scr:]
        def begin():
            x, y, c = _coords()
            peers = [(x, y, 1 - c)] * ("sibling" in stage.peers)
            peers += [(px, py, c) for px, py in _chips()] * ("chips" in stage.peers)
            peers += [(px, py, 1 - c) for px, py in _chips()] * ("others" in stage.peers)
            barrier = pltpu.get_barrier_semaphore()
            for peer in peers:
                pl.semaphore_signal(barrier, inc=1, device_id=peer, device_id_type=MESH)
            pl.semaphore_wait(barrier, len(peers))
            stage.run(st_in, st_o, send_sems, recv_sems, True)

        if not grid:
            begin()
            if body is not None:
                body(*h_in, *h_o, *h_scr)
            stage.run(st_in, st_o, send_sems, recv_sems, False)
            return
        ids = [pl.program_id(a) for a in range(len(grid))]
        first = functools.reduce(jnp.logical_and, [i == 0 for i in ids])
        last = functools.reduce(jnp.logical_and, [i == g - 1 for i, g in zip(ids, grid)])
        pl.when(first)(begin)
        body(*h_in, *h_o, *h_scr)
        pl.when(last)(lambda: stage.run(st_in, st_o, send_sems, recv_sems, False))

    call = pl.pallas_call(
        wrapped, name=name, grid=grid, in_specs=list(in_specs) + stage.in_specs, out_specs=h_out_specs + stage.out_specs,
        out_shape=h_out_shape + stage.outputs, input_output_aliases={n_in + i: n_out + o for i, o in stage.aliases.items()},
        scratch_shapes=list(scratch_shapes) + [pltpu.SemaphoreType.DMA((stage.n_copies,)), pltpu.SemaphoreType.DMA((stage.n_copies,))],
        compiler_params=pltpu.CompilerParams(dimension_semantics=("arbitrary",) * len(grid), vmem_limit_bytes=VMEM_LIMIT_V7X,
                                             collective_id=PEER_SETS.index(frozenset(stage.peers))),
    )

    def run(*operands):
        outs = call(*operands, *stage.inputs)
        host = outs[:n_out]
        return (host[0] if single else host), outs[n_out:]

    return run


def _run_stage(stage, name):
    return _hosted(stage, None, name=name, in_specs=[], out_specs=[], out_shape=[])()[1]


def _add_allgather8(stage, v):
    def build(ins, outs):
        x, y, c = _coords()
        me = 4 * x + 2 * y + c
        copies = [(ins[0], outs[0].at[me], None, None)]
        for r in range(1, N_DEV):
            peer = (_flip(x, r & 4), _flip(y, r & 2), _flip(c, r & 1))
            copies.append((ins[0], outs[0].at[me], peer, outs[0].at[me ^ r]))
        return copies

    return stage.add([v], VMEM_SPEC, [SDS((N_DEV,) + v.shape, v.dtype)], VMEM_SPEC, N_DEV, build, ("sibling", "chips", "others"))[0]


def _add_gather_ici(stage, shard, rows=None, whole=None):
    first, count = rows or (0, shard.shape[1])

    def build(ins, outs):
        x, y, c = _coords()
        k_me = 2 * x + y
        part = pl.ds(first, count)
        copies = [] if whole is not None else [(ins[0], outs[0].at[k_me], (x, y, 1 - c), outs[0].at[k_me])]
        for px, py in _chips():
            copies.append((ins[0].at[c, part], outs[0].at[k_me, c, part], (px, py, c), outs[0].at[2 * px + py, c, part]))
        return copies

    if whole is None:
        return stage.add([shard], HBM_SPEC, [SDS((N_CHIPS,) + shard.shape, shard.dtype)], HBM_SPEC, N_CHIPS, build, ("sibling", "chips"))[0]

    def build_into(ins, outs):
        return build(ins[1:], outs)

    i0 = len(stage.inputs)
    out = stage.add([whole, shard], HBM_SPEC, [SDS(whole.shape, whole.dtype)], HBM_SPEC, N_CHIPS - 1, build_into, ("chips",))[0]
    stage.aliases[i0] = out
    return out


def _add_gather_d2d(stage, whole):
    def build(ins, outs):
        x, y, c = _coords()
        return [(outs[0].at[2 * px + py, c], outs[0].at[2 * px + py, c], (x, y, 1 - c), outs[0].at[2 * px + py, 1 - c]) for px, py in _chips()]

    return stage.add([whole], HBM_SPEC, [SDS(whole.shape, whole.dtype)], HBM_SPEC, N_CHIPS - 1, build, ("sibling",), alias=True)[0]


def _add_reduce_d2d(stage, grads):
    def build(ins, outs):
        x, y, c = _coords()
        return [(ins[0].at[k, 1 - c], outs[0].at[k], (x, y, 1 - c), outs[0].at[k]) for k in range(N_CHIPS)]

    return stage.add([grads], HBM_SPEC, [SDS((N_CHIPS,) + grads.shape[2:], grads.dtype)], HBM_SPEC, N_CHIPS, build, ("sibling",))[0]


def _add_reduce_ici(stage, partial):
    def build(ins, outs):
        x, y, c = _coords()
        return [(ins[0].at[2 * px + py], outs[0].at[j], (px, py, c), outs[0].at[j]) for j, (px, py) in enumerate(_chips())]

    return stage.add([partial], HBM_SPEC, [SDS((N_CHIPS - 1,) + partial.shape[1:], partial.dtype)], HBM_SPEC, N_CHIPS - 1, build, ("chips",))[0]


def _add_swap_halves(stage, quarter):
    def build(ins, outs):
        x, y, c = _coords()
        return [(outs[0].at[c], outs[0].at[c], (x, y, 1 - c), outs[0].at[1 - c])]

    return stage.add([quarter], HBM_SPEC, [SDS(quarter.shape, quarter.dtype)], HBM_SPEC, 1, build, ("sibling",), alias=True)[0]


def _sum_pair(grads, from_sibling, place, name):
    _, _, rows, cols = grads.shape
    tr = min(rows, 512)

    def body(place_ref, g_ref, s_ref, o_ref):
        o_ref[0] = (g_ref[0, 0].astype(F32) + s_ref[0].astype(F32)).astype(BF16)

    spec = pl.BlockSpec((1, tr, cols), lambda k, i, place_ref: (k, i, 0))
    return pl.pallas_call(
        body, name=name, out_shape=SDS((N_CHIPS, rows, cols), BF16),
        grid_spec=pltpu.PrefetchScalarGridSpec(
            num_scalar_prefetch=1, grid=(N_CHIPS, rows // tr),
            in_specs=[pl.BlockSpec((1, 1, tr, cols), lambda k, i, place_ref: (k, place_ref[0], i, 0)), spec], out_specs=spec),
        compiler_params=_params("parallel", "parallel"),
    )(place, grads, from_sibling)


def _sum_quarter(grads, from_sibling, from_chips, place, name):
    _, _, rows, cols = grads.shape
    tr = min(rows, 256)

    def body(place_ref, g_ref, s_ref, c_ref, o_ref):
        acc = g_ref[0, 0].astype(F32) + s_ref[0].astype(F32)
        for j in range(N_CHIPS - 1):
            acc = acc + c_ref[j].astype(F32)
        o_ref[0] = acc

    return pl.pallas_call(
        body, name=name, out_shape=SDS((2, rows, cols), F32),
        grid_spec=pltpu.PrefetchScalarGridSpec(
            num_scalar_prefetch=1, grid=(rows // tr,),
            in_specs=[pl.BlockSpec((1, 1, tr, cols), lambda i, place_ref: (place_ref[1], place_ref[0], i, 0)),
                      pl.BlockSpec((1, tr, cols), lambda i, place_ref: (place_ref[1], i, 0)),
                      pl.BlockSpec((N_CHIPS - 1, tr, cols), lambda i, place_ref: (0, i, 0))],
            out_specs=pl.BlockSpec((1, tr, cols), lambda i, place_ref: (place_ref[0], i, 0))),
        compiler_params=_params("parallel"),
    )(place, grads, from_sibling, from_chips)


def _row_spec(ts, cols, col_block=0):
    return pl.BlockSpec((ts, cols), lambda *g: (g[0], col_block))


def _vec_spec(rows, cols):
    return pl.BlockSpec((rows, cols), lambda *g: (0, 0))


def _matmul(a, w, *, transposed, reduce_blocks, name, tm, outs, finish, extra=(), prepare=None, zero_first=(), parts=1, stage=None):
    m_rows = a.shape[0]
    nb, r, c = w.shape
    kb = c if transposed else r
    dims = (((1,), (1,)), ((), ())) if transposed else (((1,), (0,)), ((), ()))
    assert a.shape[1] == (nb * kb if reduce_blocks else kb) and m_rows % tm == 0 and tm % parts == 0
    n_extra, sub = len(extra), tm // parts

    def body(a_ref, w_ref, *rest):
        extra_refs, out_refs = rest[:n_extra], rest[n_extra:]
        if zero_first:
            @pl.when(pl.program_id(0) == 0)
            def _():
                for i in zero_first:
                    out_refs[i][...] = jnp.zeros_like(out_refs[i])

        def views(j):
            rows = pl.ds(j * sub, sub)
            return rows, [ref.at[rows] if ref.shape[0] == tm else ref for ref in extra_refs], [ref.at[rows] if ref.shape[0] == tm else ref for ref in out_refs]

        def product(j):
            rows, ex, ou = views(j)
            if reduce_blocks:
                acc = lax.dot_general(a_ref[rows, 0:kb], w_ref[0], dims, preferred_element_type=F32)
                for k in range(1, nb):
                    acc += lax.dot_general(a_ref[rows, k * kb:(k + 1) * kb], w_ref[k], dims, preferred_element_type=F32)
                return [(None, acc)]
            lhs = a_ref[rows, :] if prepare is None else prepare(a_ref.at[rows], ex, ou)
            return [(k, lax.dot_general(lhs, w_ref[k], dims, preferred_element_type=F32)) for k in range(nb)]

        pending = product(0)
        for j in range(parts):
            ahead = product(j + 1) if j + 1 < parts else None
            _, ex, ou = views(j)
            for k, acc in pending:
                finish(k, acc, ex, ou)
            pending = ahead

    return _hosted(
        stage, body, name=name, grid=(m_rows // tm,), out_shape=[s for s, _ in outs], out_specs=[s for _, s in outs],
        in_specs=[_row_spec(tm, a.shape[1]), pl.BlockSpec((nb, r, c), lambda *g: (0, 0, 0), pipeline_mode=pl.Buffered(1))] + [s for _, s in extra],
        compiler_params=_params("arbitrary"),
    )(a, w, *[e for e, _ in extra])


def _store_blocks(n):
    def finish(k, acc, extra_refs, out_refs):
        if k is None:
            out_refs[0][...] = acc.astype(out_refs[0].dtype)
        else:
            out_refs[0][:, k * n:(k + 1) * n] = acc.astype(out_refs[0].dtype)
    return finish


def _weight_grad(a, b, *, nb, name, tk, stage=None):
    s_rows, k1 = a.shape
    tn = b.shape[1] // nb
    assert k1 % tk == 0 and b.shape[1] % nb == 0

    def body(a_ref, b_ref, o_ref):
        o_ref[0] = lax.dot_general(a_ref[...], b_ref[...], (((0,), (0,)), ((), ())), preferred_element_type=F32).astype(o_ref.dtype)

    return _hosted(
        stage, body, name=name, grid=(nb, k1 // tk), out_shape=SDS((nb, k1, tn), BF16),
        in_specs=[pl.BlockSpec((s_rows, tk), lambda n, i: (0, i)), pl.BlockSpec((s_rows, tn), lambda n, i: (0, n))],
        out_specs=pl.BlockSpec((1, tk, tn), lambda n, i: (n, i, 0)), compiler_params=_params("parallel", "parallel"),
    )(a, b)


def _rms(x):
    r = lax.rsqrt(jnp.mean(x * x, axis=-1, keepdims=True) + EPS)
    return x * r, r


def _accumulate(acc_ref, row, value):
    acc_ref[row:row + 1, :] += jnp.sum(value, axis=0, keepdims=True)


def _gelu_parts(y):
    k0, k1 = 0.7978845608028654, 0.044715
    th = jnp.tanh(k0 * (y + k1 * (y * y * y)))
    gelu = 0.5 * y * (1.0 + th)
    dgelu = 0.5 * (1.0 + th) + 0.5 * y * (1.0 - th * th) * (k0 * (1.0 + 3.0 * k1 * (y * y)))
    return gelu, dgelu


def _sigmoid(z):
    return 0.5 * jnp.tanh(0.5 * z) + 0.5


def _one_minus_square(a, log_a):
    return -jnp.tanh(log_a) * (1.0 + a * a)


def _modulated_norm(x, mod_ref, g_ref, sc_row, sh_row):
    xh, _ = _rms(x)
    return ((xh * g_ref[...]) * (1.0 + mod_ref[sc_row:sc_row + 1, :]) + mod_ref[sh_row:sh_row + 1, :]).astype(BF16)


def _modulated_norm_backward(dh, x, dres, mod_ref, g_ref, acc_ref, sc_row):
    xh, r = _rms(x)
    _accumulate(acc_ref, 0, dh * (xh * g_ref[...]))
    _accumulate(acc_ref, 1, dh)
    dn = dh * (1.0 + mod_ref[sc_row:sc_row + 1, :])
    _accumulate(acc_ref, 2, dn * xh)
    dxh = dn * g_ref[...]
    return dres + r * (dxh - xh * jnp.mean(dxh * xh, axis=-1, keepdims=True))


def _seq_forward(proj, conv_w, vecs, w_a, w_x, w_p, *, ts, stage=None):
    s_rows = proj.shape[0]
    d = D_MODEL

    def body(x_ref, xh_ref, y_ref, u_ref, uh_ref, cw_ref, vec_ref, wa_ref, wx_ref, wp_ref,
             xr_o, hr_o, ga_o, p_o, pooled_o, carry, a_scr, b_scr):
        t = pl.program_id(0)

        @pl.when(t == 0)
        def _():
            carry[...] = jnp.zeros_like(carry)

        rows = lax.broadcasted_iota(jnp.int32, (ts, BLOCK), 0)
        start = (rows == 0) & (t == 0)
        for g in range(N_BLOCKS):
            lanes = slice(g * BLOCK, (g + 1) * BLOCK)
            x = x_ref[:, lanes]
            xx = jnp.concatenate([jnp.where(t == 0, 0.0, xh_ref[:, lanes]), x], axis=0)
            xr = vec_ref[0:1, lanes] + x * cw_ref[CONV_WIDTH - 1:CONV_WIDTH, lanes]
            for j in range(CONV_WIDTH - 1):
                xr = xr + pltpu.roll(xx, CONV_WIDTH - 1 - j, 0)[CONV_HALO:] * cw_ref[j:j + 1, lanes]
            xr_o[:, lanes] = xr
            xrb = xr.astype(BF16)
            r = _sigmoid(jnp.dot(xrb, wa_ref[g], preferred_element_type=F32) + vec_ref[1:2, lanes])
            gate_i = _sigmoid(jnp.dot(xrb, wx_ref[g], preferred_element_type=F32) + vec_ref[2:3, lanes])
            log_a = (-C_RG * r) * jax.nn.softplus(vec_ref[3:4, lanes])
            a = jnp.exp(log_a)
            a_scr[:, lanes] = a
            b_scr[:, lanes] = xr * gate_i * jnp.where(start, 1.0, jnp.sqrt(_one_minus_square(a, log_a)))

        sub = lax.broadcasted_iota(jnp.int32, (8, d), 0)

        def chunk(i, h):
            a = a_scr[pl.ds(i * 8, 8), :]
            b = b_scr[pl.ds(i * 8, 8), :]
            for s in (1, 2, 4):
                keep = sub >= s
                b = jnp.where(keep, a * pltpu.roll(b, s, 0) + b, b)
                a = jnp.where(keep, a * pltpu.roll(a, s, 0), a)
            hh = a * h + b
            hr_o[pl.ds(i * 8, 8), :] = hh
            return hh[7:8, :]

        carry[...] = lax.fori_loop(0, ts // 8, chunk, carry[...])
        pos = (rows + t * ts + 1).astype(F32)
        for g, win in enumerate(POOL_WINDOWS):
            lanes = slice(g * BLOCK, (g + 1) * BLOCK)
            gelu, _ = _gelu_parts(y_ref[:, lanes])
            ga_o[:, lanes] = (gelu * hr_o[:, lanes]).astype(BF16)
            u = u_ref[:, lanes]
            sums, have = jnp.concatenate([jnp.where(t == 0, 0.0, uh_ref[:, lanes]), u], axis=0), 1
            while have < win:
                sums = sums + pltpu.roll(sums, have, 0)
                have *= 2
            pb = (sums[POOL_HALO:] * (1.0 / jnp.minimum(pos, float(win))) - u).astype(BF16)
            p_o[:, lanes] = pb
            mixed = jnp.dot(pb, wp_ref[g], preferred_element_type=F32) + vec_ref[4:5, lanes]
            pooled_o[:, lanes] = (mixed * vec_ref[5:6, lanes]).astype(BF16)

    halo_x = pl.BlockSpec((CONV_HALO, d), lambda t: (jnp.maximum(t * (ts // CONV_HALO) - 1, 0), 0))
    halo_u = pl.BlockSpec((POOL_HALO, d), lambda t: (jnp.maximum(t * (ts // POOL_HALO) - 1, 0), 2))
    wspec = pl.BlockSpec((N_BLOCKS, BLOCK, BLOCK), lambda t: (0, 0, 0))
    return _hosted(
        stage, body, name="seq_forward", grid=(s_rows // ts,),
        out_shape=[SDS((s_rows, d), F32), SDS((s_rows, d), F32), SDS((s_rows, d), BF16), SDS((s_rows, d), BF16), SDS((s_rows, d), BF16)],
        in_specs=[_row_spec(ts, d, 0), halo_x, _row_spec(ts, d, 1), _row_spec(ts, d, 2), halo_u, _vec_spec(CONV_WIDTH, d), _vec_spec(8, d),
                  wspec, wspec, wspec],
        out_specs=[_row_spec(ts, d)] * 5,
        scratch_shapes=[pltpu.VMEM((1, d), F32), pltpu.VMEM((ts, d), F32), pltpu.VMEM((ts, d), F32)],
        compiler_params=_params("arbitrary"),
    )(proj, proj, proj, proj, proj, conv_w, vecs, w_a, w_x, w_p)


def _branches(ga, pooled, proj, w_a, w_b, *, ts, stage=None):
    s_rows, d = ga.shape

    def body(a_ref, p_ref, ga_ref, gb_ref, wa_ref, wb_ref, merged_o, sa_o, sb_o, ta_o, tb_o):
        bra = jnp.dot(a_ref[...], wa_ref[0], preferred_element_type=F32)
        brb = jnp.dot(p_ref[...], wb_ref[0], preferred_element_type=F32)
        sa, sb = _sigmoid(ga_ref[...]), _sigmoid(gb_ref[...])
        merged_o[...] = (sa * bra + sb * brb).astype(BF16)
        sa_o[...] = sa.astype(BF16)
        sb_o[...] = sb.astype(BF16)
        ta_o[...] = (bra * (sa * (1.0 - sa))).astype(BF16)
        tb_o[...] = (brb * (sb * (1.0 - sb))).astype(BF16)

    rows = _row_spec(ts, d)
    wspec = pl.BlockSpec((1, d, d), lambda *g: (0, 0, 0), pipeline_mode=pl.Buffered(1))
    return _hosted(
        stage, body, name="branches", grid=(s_rows // ts,), out_shape=[SDS((s_rows, d), BF16)] * 5,
        in_specs=[rows, rows, _row_spec(ts, d, 3), _row_spec(ts, d, 4), wspec, wspec], out_specs=[rows] * 5,
        compiler_params=_params("arbitrary"),
    )(ga, pooled, proj, proj, w_a, w_b)


def _seq_backward_by_block(proj, xr, hr, p, dga, dpooled, dgab, conv_w, vecs, w_a, w_x, w_p, *, ts, stage=None):
    s_rows = proj.shape[0]
    d = D_MODEL
    n_t = s_rows // ts
    nn, nt, tn = (((1,), (0,)), ((), ())), (((1,), (1,)), ((), ())), (((0,), (0,)), ((), ()))

    def dot(lhs, rhs, dims):
        return lax.dot_general(lhs, rhs, dims, preferred_element_type=F32)

    def body(x_ref, y_ref, xr_ref, hr_ref, hh_ref, p_ref, dga_ref, dpl_ref, dgab_ref, cw_ref, vec_ref, wa_ref, wx_ref, wp_ref,
             dproj_o, acc_o, dwa_o, dwx_o, dwp_o, g_carry, dxr_carry, q_carry, a_scr, b_scr, g_scr, a_keep, r_keep, i_keep, m_keep):
        step = pl.program_id(0)
        t = n_t - 1 - step

        @pl.when(step == 0)
        def _():
            for ref in (acc_o, dwa_o, dwx_o, dwp_o, g_carry, dxr_carry, q_carry):
                ref[...] = jnp.zeros_like(ref)

        rows = lax.broadcasted_iota(jnp.int32, (ts, BLOCK), 0)
        start = (rows == 0) & (t == 0)

        def add_sum(row, lanes, value):
            acc_o[row:row + 1, lanes] += jnp.sum(value, axis=0, keepdims=True)

        for g in range(N_BLOCKS):
            lanes = slice(g * BLOCK, (g + 1) * BLOCK)
            xrb = xr_ref[:, lanes].astype(BF16)
            r = _sigmoid(dot(xrb, wa_ref[g], nn) + vec_ref[1:2, lanes])
            gate_i = _sigmoid(dot(xrb, wx_ref[g], nn) + vec_ref[2:3, lanes])
            log_a = (-C_RG * r) * jax.nn.softplus(vec_ref[3:4, lanes])
            a = jnp.exp(log_a)
            a_keep[:, lanes], r_keep[:, lanes], i_keep[:, lanes], m_keep[:, lanes] = a, r, gate_i, _one_minus_square(a, log_a)
            gelu, dgelu = _gelu_parts(y_ref[:, lanes])
            dga_t = dga_ref[:, lanes].astype(F32)
            dproj_o[:, d + g * BLOCK:d + (g + 1) * BLOCK] = (dga_t * hr_ref[:, lanes] * dgelu).astype(BF16)
            a_scr[:, lanes] = jnp.where(rows == ts - 1, 1.0, pltpu.roll(a, ts - 1, 0))
            b_scr[:, lanes] = dga_t * gelu

        sub = lax.broadcasted_iota(jnp.int32, (8, d), 0)

        def chunk(i, g_next):
            at = pl.multiple_of((ts // 8 - 1 - i) * 8, 8)
            aa = a_scr[pl.ds(at, 8), :]
            bb = b_scr[pl.ds(at, 8), :]
            for s in (1, 2, 4):
                keep = sub < 8 - s
                bb = jnp.where(keep, bb + aa * pltpu.roll(bb, 8 - s, 0), bb)
                aa = jnp.where(keep, aa * pltpu.roll(aa, 8 - s, 0), aa)
            gg = aa * g_next + bb
            g_scr[pl.ds(at, 8), :] = gg
            return gg[0:1, :]

        g_first = lax.fori_loop(0, ts // 8, chunk, g_carry[...])
        g_carry[...] = a_keep[0:1, :] * g_first

        pos = (rows + t * ts + 1).astype(F32)
        for g, win in enumerate(POOL_WINDOWS):
            lanes = slice(g * BLOCK, (g + 1) * BLOCK)
            gs, xr, a, r, gate_i, m_square = g_scr[:, lanes], xr_ref[:, lanes], a_keep[:, lanes], r_keep[:, lanes], i_keep[:, lanes], m_keep[:, lanes]
            sp = jax.nn.softplus(vec_ref[3:4, lanes])
            mult = jnp.where(start, 1.0, jnp.sqrt(m_square))
            h_before = jnp.where(t == 0, 0.0, hh_ref[CONV_HALO - 1:CONV_HALO, lanes])
            h_prev = jnp.where(rows == 0, h_before, pltpu.roll(hr_ref[:, lanes], 1, 0))
            gm = gs * mult
            dxr = gm * gate_i
            d_i = gm * xr
            d_mult = (gs * xr) * gate_i
            dlog_a = (gs * h_prev) * a - jnp.where(start, 0.0, d_mult * (a * a) * lax.rsqrt(m_square))
            dzr = (dlog_a * (-C_RG * sp)) * (r * (1.0 - r))
            dzi = d_i * (gate_i * (1.0 - gate_i))
            add_sum(5, lanes, dzr)
            add_sum(6, lanes, dzi)
            acc_o[7:8, lanes] += jnp.sum(dlog_a * r, axis=0, keepdims=True) * (-C_RG * jax.nn.sigmoid(vec_ref[3:4, lanes]))
            xrb, dzrb, dzib = xr.astype(BF16), dzr.astype(BF16), dzi.astype(BF16)
            dwa_o[g] += dot(xrb, dzrb, tn)
            dwx_o[g] += dot(xrb, dzib, tn)
            dxr = dxr + dot(dzrb, wa_ref[g], nt) + dot(dzib, wx_ref[g], nt)

            x = x_ref[:, lanes]
            ext = jnp.concatenate([dxr, dxr_carry[:, lanes]], axis=0)
            dx = dxr * cw_ref[CONV_WIDTH - 1:CONV_WIDTH, lanes]
            add_sum(CONV_WIDTH - 1, lanes, x * dxr)
            for j in range(CONV_WIDTH - 1):
                ahead = pltpu.roll(ext, ts + CONV_HALO - (CONV_WIDTH - 1 - j), 0)[:ts]
                dx = dx + ahead * cw_ref[j:j + 1, lanes]
                add_sum(j, lanes, x * ahead)
            add_sum(4, lanes, dxr)
            dxr_carry[:, lanes] = dxr[0:CONV_HALO, :]
            dproj_o[:, lanes] = dx.astype(BF16)

            pb = p_ref[:, lanes]
            mixed = dot(pb, wp_ref[g], nn) + vec_ref[4:5, lanes]
            dpl = dpl_ref[:, lanes].astype(F32)
            add_sum(9, lanes, dpl * mixed)
            dmixed = dpl * vec_ref[5:6, lanes]
            add_sum(8, lanes, dmixed)
            dmb = dmixed.astype(BF16)
            dwp_o[g] += dot(pb, dmb, tn)
            dp = dot(dmb, wp_ref[g], nt)
            q = dp * (1.0 / jnp.minimum(pos, float(win)))
            sums, have = jnp.concatenate([q, q_carry[:, lanes]], axis=0), 1
            while have < win:
                sums = sums + pltpu.roll(sums, ts + POOL_HALO - have, 0)
                have *= 2
            q_carry[:, lanes] = q[0:POOL_HALO, :]
            dproj_o[:, 2 * d + g * BLOCK:2 * d + (g + 1) * BLOCK] = (sums[:ts] - dp).astype(BF16)

        dproj_o[:, 3 * d:5 * d] = dgab_ref[...]

    def rev(cols, col_block=0):
        return pl.BlockSpec((ts, cols), lambda i: (n_t - 1 - i, col_block))

    halo_h = pl.BlockSpec((CONV_HALO, d), lambda i: (jnp.maximum((n_t - 1 - i) * (ts // CONV_HALO) - 1, 0), 0))
    wspec = pl.BlockSpec((N_BLOCKS, BLOCK, BLOCK), lambda i: (0, 0, 0))
    return _hosted(
        stage, body, name="seq_backward", grid=(n_t,),
        out_shape=[SDS((s_rows, 5 * d), BF16), SDS((16, d), F32)] + [SDS((N_BLOCKS, BLOCK, BLOCK), F32)] * 3,
        in_specs=[rev(d, 0), rev(d, 1), rev(d), rev(d), halo_h, rev(d), rev(d), rev(d), rev(2 * d), _vec_spec(CONV_WIDTH, d), _vec_spec(8, d),
                  wspec, wspec, wspec],
        out_specs=[rev(5 * d), _vec_spec(16, d), wspec, wspec, wspec],
        scratch_shapes=[pltpu.VMEM((1, d), F32), pltpu.VMEM((CONV_HALO, d), F32), pltpu.VMEM((POOL_HALO, d), F32)] + [pltpu.VMEM((ts, d), F32)] * 7,
        compiler_params=_params("arbitrary"),
    )(proj, proj, xr, hr, hr, p, dga, dpooled, dgab, conv_w, vecs, w_a, w_x, w_p)


def _adamw_math(w, g, m, v):
    m = ADAM_B1 * m + (1.0 - ADAM_B1) * g
    v = ADAM_B2 * v + (1.0 - ADAM_B2) * jnp.square(g)
    m_hat = m / (1.0 - ADAM_B1 ** ADAM_STEP)
    v_hat = v / (1.0 - ADAM_B2 ** ADAM_STEP)
    delta = -ADAM_LR * (m_hat / (jnp.sqrt(v_hat) + ADAM_EPS) + ADAM_WD * w)
    return delta, m, v


def _adamw(w, g, m, v, name):
    rows, cols = w.shape
    tr = min(rows, 512)

    def body(w_ref, g_ref, m_ref, v_ref, g_out, d_ref, mo_ref, vo_ref):
        g = g_ref[...]
        g_out[...] = g
        d_ref[...], mo_ref[...], vo_ref[...] = _adamw_math(w_ref[...], g, m_ref[...], v_ref[...])

    spec = pl.BlockSpec((tr, cols), lambda i: (i, 0))
    return pl.pallas_call(
        body, name=name, grid=(rows // tr,), out_shape=[SDS((rows, cols), F32)] * 4, in_specs=[spec] * 4, out_specs=[spec] * 4,
        compiler_params=_params("parallel"),
    )(w, g, m, v)


def _prologue(c, conv_shard, w_ada, b_cols, w_in_halves, gate_halves):
    n_cols = w_ada.shape[1]
    n_g = len(gate_halves)
    half_rows = w_in_halves.shape[1] // 2
    n_sems = 3 * N_DEV + N_CHIPS * n_g + 5 + 2 + 3

    def body(c_ref, conv_ref, w_ref, b_ref, win_ref, *rest):
        gate_refs = rest[:n_g]
        act_o, mod_o, conv_o, win_o = rest[n_g:n_g + 4]
        gate_o = rest[n_g + 4:2 * n_g + 4]
        c_all, mod_cols, w_vmem, send_sems, recv_sems = rest[2 * n_g + 4:]
        fetch_w = pltpu.make_async_copy(w_ref, w_vmem, send_sems.at[n_sems])
        x, y, c = _coords()
        me, k_me = 4 * x + 2 * y + c, 2 * x + y
        sibling, x_nbr, y_nbr = (x, y, 1 - c), (1 - x, y, c), (x, 1 - y, c)
        k_x, k_y, k_d = 2 * (1 - x) + y, 2 * x + (1 - y), 2 * (1 - x) + (1 - y)
        counter = iter(range(n_sems))

        def copy(src, dst, to, landing):
            s = next(counter)
            out = pltpu.make_async_remote_copy(src_ref=src, dst_ref=dst, send_sem=send_sems.at[s], recv_sem=recv_sems.at[s],
                                               device_id=to, device_id_type=MESH)
            arrival = pltpu.make_async_remote_copy(src_ref=landing, dst_ref=landing, send_sem=send_sems.at[s], recv_sem=recv_sems.at[s],
                                                   device_id=(x, y, c), device_id_type=MESH)
            return out, arrival

        def allgather(src, out):
            s = next(counter)
            own = pltpu.make_async_copy(src, out.at[me], send_sems.at[s])
            peers = [copy(src, out.at[me], (_flip(x, r & 4), _flip(y, r & 2), _flip(c, r & 1)), out.at[me ^ r]) for r in range(1, N_DEV)]
            return own, peers

        def start(group):
            own, peers = group
            own.start()
            for out, _ in peers:
                out.start()

        def finish(group):
            own, peers = group
            for out, arrival in peers:
                arrival.wait_recv()
                out.wait_send()
            own.wait()

        rows_a, rows_b = pl.ds(0, half_rows), pl.ds(half_rows, half_rows)
        c_group, conv_group = allgather(c_ref, c_all), allgather(conv_ref, conv_o)
        gates = []
        for g_ref, g_out in zip(gate_refs, gate_o):
            gates.append(copy(g_ref, g_out.at[k_me], sibling, g_out.at[k_me]))
            gates += [copy(g_ref.at[c], g_out.at[k_me, c], (px, py, c), g_out.at[2 * px + py, c]) for px, py in _chips()]
        own_in = copy(win_ref, win_o.at[k_me], sibling, win_o.at[k_me])
        to_x = [copy(win_ref.at[c, rows], win_o.at[k_me, c, rows], x_nbr, win_o.at[k_x, c, rows]) for rows in (rows_a, rows_b)]
        to_y = [copy(win_ref.at[c, rows], win_o.at[k_me, c, rows], y_nbr, win_o.at[k_y, c, rows]) for rows in (rows_b, rows_a)]
        mod_group = allgather(mod_cols, mod_o)
        relay_y = copy(win_o.at[k_x, c, rows_a], win_o.at[k_x, c, rows_a], y_nbr, win_o.at[k_d, c, rows_a])
        relay_x = copy(win_o.at[k_y, c, rows_b], win_o.at[k_y, c, rows_b], x_nbr, win_o.at[k_d, c, rows_b])
        sib_x, sib_y, sib_d = (copy(win_o.at[k, c], win_o.at[k, c], sibling, win_o.at[k, 1 - c]) for k in (k_x, k_y, k_d))

        start(c_group)
        for out, _ in to_x + to_y + [own_in] + gates:
            out.start()
        start(conv_group)
        fetch_w.start()
        finish(c_group)
        cv = jnp.concatenate([c_all[r] for r in range(N_DEV)], axis=0)
        act = cv * jax.nn.sigmoid(cv)
        act_o[...] = act
        fetch_w.wait()
        mod_cols[...] = jnp.dot(act.astype(BF16), w_vmem[...].astype(BF16), preferred_element_type=F32) + b_ref[...]
        start(mod_group)
        to_x[0][1].wait_recv()
        relay_y[0].start()
        to_y[0][1].wait_recv()
        relay_x[0].start()
        finish(mod_group)
        finish(conv_group)
        to_x[1][1].wait_recv()
        sib_x[0].start()
        to_y[1][1].wait_recv()
        sib_y[0].start()
        for out, arrival in gates + [own_in]:
            arrival.wait_recv()
            out.wait_send()
        for out, arrival in (relay_y, relay_x):
            arrival.wait_recv()
            out.wait_send()
        sib_d[0].start()
        for out, _ in to_x + to_y:
            out.wait_send()
        for out, arrival in (sib_x, sib_y, sib_d):
            arrival.wait_recv()
            out.wait_send()

    gate_shapes = [SDS((N_CHIPS,) + g.shape, g.dtype) for g in gate_halves]
    return pl.pallas_call(
        body, name="prologue",
        out_shape=[SDS((N_DEV, c.shape[1]), F32), SDS((N_DEV, N_DEV, n_cols), F32), SDS((N_DEV,) + conv_shard.shape, F32),
                   SDS((N_CHIPS,) + w_in_halves.shape, w_in_halves.dtype)] + gate_shapes,
        in_specs=[VMEM_SPEC, VMEM_SPEC, HBM_SPEC, VMEM_SPEC] + [HBM_SPEC] * (1 + n_g), out_specs=[VMEM_SPEC] * 3 + [HBM_SPEC] * (1 + n_g),
        scratch_shapes=[pltpu.VMEM((N_DEV,) + c.shape, F32), pltpu.VMEM((N_DEV, n_cols), F32), pltpu.VMEM(w_ada.shape, F32),
                        pltpu.SemaphoreType.DMA((n_sems + 1,)), pltpu.SemaphoreType.DMA((n_sems,))],
        compiler_params=pltpu.CompilerParams(vmem_limit_bytes=VMEM_LIMIT_V7X),
    )(c, conv_shard, w_ada, b_cols, w_in_halves, *gate_halves)


def _ada_backward(act_t, dmod_cols, w, m, v):
    rows, cols = w.shape
    tn = 512

    def body(a_ref, dm_ref, w_ref, m_ref, v_ref, g_ref, d_ref, mo_ref, vo_ref):
        g = jnp.dot(a_ref[...].astype(BF16), dm_ref[...].astype(BF16), preferred_element_type=F32)
        g_ref[...] = g
        d_ref[...], mo_ref[...], vo_ref[...] = _adamw_math(w_ref[...], g, m_ref[...], v_ref[...])

    spec = pl.BlockSpec((rows, tn), lambda j: (0, j))
    return pl.pallas_call(
        body, name="ada_backward", grid=(cols // tn,), out_shape=[SDS((rows, cols), F32)] * 4,
        in_specs=[pl.BlockSpec((rows, N_DEV), lambda j: (0, 0)), pl.BlockSpec((N_DEV, tn), lambda j: (0, j)), spec, spec, spec],
        out_specs=[spec] * 4, compiler_params=_params("parallel"),
    )(act_t, dmod_cols, w, m, v)


def _sum_rows(gathered, n_loss):
    n = gathered.shape[2]

    def body(g_ref, o_ref):
        acc = g_ref[0]
        for r in range(1, N_DEV):
            acc = acc + g_ref[r]
        o_ref[...] = acc
        o_ref[:, n - n_loss:n] = jnp.broadcast_to(jnp.sum(acc[:, n - n_loss:n], axis=1, keepdims=True), (1, n_loss))

    return pl.pallas_call(body, name="sum_rows", out_shape=SDS((1, n), F32), in_specs=[VMEM_SPEC], out_specs=VMEM_SPEC)(gathered)


def _adamw_vectors(total, offsets, separate, params):
    n_p = len(params)
    apart = [g for g in separate if g is not None]

    def body(*refs):
        total_ref, apart_refs = refs[0], list(refs[1:1 + len(apart)])
        ins, outs = refs[1 + len(apart):1 + len(apart) + 3 * n_p], refs[1 + len(apart) + 3 * n_p:]
        for i in range(n_p):
            w_ref, m_ref, v_ref = ins[3 * i:3 * i + 3]
            g = apart_refs.pop(0)[...] if offsets[i] is None else total_ref[:, offsets[i]:offsets[i] + w_ref.shape[1]]
            outs[4 * i][...] = g
            outs[4 * i + 1][...], outs[4 * i + 2][...], outs[4 * i + 3][...] = _adamw_math(w_ref[...], g, m_ref[...], v_ref[...])

    flat = [a for p in params for a in p]
    res = pl.pallas_call(
        body, name="adamw_vectors", out_shape=[SDS(p[0].shape, F32) for p in params for _ in range(4)],
        in_specs=[VMEM_SPEC] * (1 + len(apart) + len(flat)), out_specs=[VMEM_SPEC] * (4 * n_p),
    )(total, *apart, *flat)
    return [tuple(res[4 * i:4 * i + 4]) for i in range(n_p)]


def _blocks_to_pieces(w):
    nb, rows, n = w.shape
    q = rows // N_CHIPS
    return w.reshape(nb, N_CHIPS, q, n).transpose(1, 0, 2, 3).reshape(N_CHIPS, 2, nb // 2, q, n)


def _pieces_to_blocks(w):
    n_chips, _, half, q, n = w.shape
    return w.reshape(n_chips, 2 * half, q, n).transpose(1, 0, 2, 3).reshape(2 * half, n_chips * q, n)


def kernel(x, c, norm_mix_g, norm_mlp_g, w_ada, b_ada, w_in, conv_w, conv_b, w_rg_a, b_rg_a, w_rg_x, b_rg_x, a_param, w_branch_a, w_pool, b_pool, pool_scale, w_branch_b, w_out, w_up, w_down, final_g, loss_target, m_norm_mix_g, m_norm_mlp_g, m_w_ada, m_b_ada, m_w_in, m_conv_w, m_conv_b, m_w_rg_a, m_b_rg_a, m_w_rg_x, m_b_rg_x, m_a_param, m_w_branch_a, m_w_pool, m_b_pool, m_pool_scale, m_w_branch_b, m_w_out, m_w_up, m_w_down, m_final_g, v_norm_mix_g, v_norm_mlp_g, v_w_ada, v_b_ada, v_w_in, v_conv_w, v_conv_b, v_w_rg_a, v_b_rg_a, v_w_rg_x, v_b_rg_x, v_a_param, v_w_branch_a, v_w_pool, v_b_pool, v_pool_scale, v_w_branch_b, v_w_out, v_w_up, v_w_down, v_final_g):
    d = D_MODEL
    s_rows = x.shape[1]
    ts, tsq = min(512, s_rows), min(256, s_rows)
    xi, yi, ci = _coords()
    k_me, me = 2 * xi + yi, 4 * xi + 2 * yi + ci
    ada_cols = w_ada.shape[2]
    conv_cols = conv_w.shape[2]
    n_in, n_up = w_in.shape[2], w_up.shape[2]

    names = ("w_in", "w_up", "w_down", "w_a", "w_b", "w_out", "rg_a", "rg_x", "pool")
    mine = dict(zip(names, (w_in[0], w_up[0], w_down[0], w_branch_a[0], w_branch_b[0], w_out[0], w_rg_a[0], w_rg_x[0], w_pool[0])))
    moments_m = dict(zip(names, (m_w_in[0], m_w_up[0], m_w_down[0], m_w_branch_a[0], m_w_branch_b[0], m_w_out[0], m_w_rg_a[0], m_w_rg_x[0], m_w_pool[0])))
    moments_v = dict(zip(names, (v_w_in[0], v_w_up[0], v_w_down[0], v_w_branch_a[0], v_w_branch_b[0], v_w_out[0], v_w_rg_a[0], v_w_rg_x[0], v_w_pool[0])))
    block_weights, squares = ("rg_a", "rg_x", "pool"), ("w_a", "w_b", "w_out")
    place = jnp.stack([ci, k_me]).astype(jnp.int32)
    g_mix, g_mlp, g_fin = norm_mix_g, norm_mlp_g, final_g.reshape(1, d)
    x0, target = x[0], loss_target[0]
    wg = functools.partial(_weight_grad, tk=min(512, d))
    plain, back, back_sum = (dict(transposed=t, reduce_blocks=r) for t, r in ((False, False), (True, False), (True, True)))
    rows_d, vec8, vec1 = _row_spec(ts, d), _vec_spec(8, d), _vec_spec(1, d)
    tw = min(1024, s_rows)
    halves_of_tile = 2

    def out_rows(cols, dtype, t=ts):
        return SDS((s_rows, cols), dtype), _row_spec(t, cols)

    sums_out = (SDS((8, d), F32), vec8)

    def halves(n):
        return mine[n].astype(BF16).reshape(2, -1, mine[n].shape[-1])

    def blocks(w):
        return _pieces_to_blocks(w.reshape(N_CHIPS, 2, N_BLOCKS // 2, BLOCK // N_CHIPS, BLOCK))

    act_all, mod_all, conv_all, w_in_whole, *gate_wholes = _prologue(
        c, conv_w[0], w_ada[0], lax.dynamic_slice_in_dim(b_ada, k_me * ada_cols, ada_cols, axis=1), halves("w_in"), [halves(n) for n in block_weights])
    conv_full = conv_all[0::2].transpose(1, 0, 2).reshape(CONV_WIDTH, d)
    mod_mine = lax.dynamic_index_in_dim(mod_all, me, axis=1, keepdims=False)[0::2]
    mod = jnp.concatenate([mod_mine.reshape(N_MOD, d), jnp.zeros((8 - N_MOD, d), F32)], axis=0)
    w_in_all = w_in_whole.reshape(N_CHIPS, d, n_in)
    vecs = jnp.concatenate([conv_b, b_rg_a, b_rg_x, a_param, b_pool, pool_scale, jnp.zeros((2, d), F32)], axis=0)

    def norm_first(a_ref, extra_refs, out_refs):
        h = _modulated_norm(a_ref[...], extra_refs[0], extra_refs[1], 1, 0)
        out_refs[1][...] = h
        return h

    stage = _Stage()
    i_blk = [_add_gather_d2d(stage, g) for g in gate_wholes]
    i_sq = [_add_gather_ici(stage, halves(n)) for n in squares]
    down_rows = mine["w_down"].shape[0] // 8
    i_down = _add_gather_ici(stage, halves("w_down"), rows=(0, down_rows))
    (proj, h1), got = _matmul(x0, w_in_all, name="proj_in", tm=ts, extra=[(mod, vec8), (g_mix, vec1)], prepare=norm_first,
                              outs=[out_rows(N_CHIPS * n_in, F32), out_rows(d, BF16)], finish=_store_blocks(n_in), parts=halves_of_tile,                              stage=stage, **plain)
    w_rg_a_all, w_rg_x_all, w_pool_all = (blocks(got[i]) for i in i_blk)
    stage = _Stage()
    i_sq = [_add_gather_d2d(stage, got[i]) for i in i_sq]
    i_up = _add_gather_ici(stage, halves("w_up"))
    w_down_so_far = got[i_down]
    (xr, hr, ga, p, pooled), got = _seq_forward(proj, conv_full, vecs, w_rg_a_all, w_rg_x_all, w_pool_all, ts=tsq, stage=stage)
    w_a_all, w_b_all, w_out_all = (got[i].reshape(1, d, d) for i in i_sq)
    stage = _Stage()
    i_up = _add_gather_d2d(stage, got[i_up])
    i_down = _add_gather_ici(stage, halves("w_down"), rows=(down_rows, 2 * down_rows), whole=w_down_so_far)
    (merged, s_a, s_b, t_a, t_b), got = _branches(ga, pooled, proj, w_a_all, w_b_all, ts=ts, stage=stage)
    w_up_all = got[i_up].reshape(N_CHIPS, d, n_up)
    stage = _Stage()
    i_down = _add_gather_ici(stage, halves("w_down"), rows=(3 * down_rows, down_rows), whole=got[i_down])

    def residual_norm(k, acc, extra_refs, out_refs):
        x_ref, mod_ref, g_ref = extra_refs
        out_refs[0][...] = acc.astype(BF16)
        x2_t = x_ref[...] + mod_ref[2:3, :] * acc
        out_refs[1][...] = x2_t
        out_refs[2][...] = _modulated_norm(x2_t, mod_ref, g_ref, 4, 3)

    (mo, x2, h2), got = _matmul(merged, w_out_all, name="mix_out", tm=ts, extra=[(x0, rows_d), (mod, vec8), (g_mlp, vec1)],
                                outs=[out_rows(d, BF16), out_rows(d, F32), out_rows(d, BF16)], finish=residual_norm, parts=halves_of_tile,                                stage=stage, transposed=False, reduce_blocks=True)
    stage = _Stage()
    i_down = _add_gather_d2d(stage, got[i_down])

    def relu_squared(k, acc, extra_refs, out_refs):
        out_refs[0][:, k * n_up:(k + 1) * n_up] = jnp.square(jnp.maximum(acc, 0.0)).astype(BF16)

    (ff,), got = _matmul(h2, w_up_all, name="mlp_up", tm=ts, outs=[out_rows(D_FF, BF16)], finish=relu_squared, parts=halves_of_tile,
                         stage=stage, **plain)
    w_down_all = got[i_down].reshape(N_CHIPS, D_FF // N_CHIPS, d)

    def loss_head(k, acc, extra_refs, out_refs):
        x2_ref, mod_ref, g_ref, t_ref = extra_refs
        dx_ref, df_ref, acc_ref = out_refs
        xh, r = _rms(x2_ref[...] + mod_ref[5:6, :] * acc)
        err = xh * g_ref[...] - t_ref[...]
        dy = err / d
        dxh = dy * g_ref[...]
        dx3_t = r * (dxh - xh * jnp.mean(dxh * xh, axis=-1, keepdims=True))
        dx_ref[...] = dx3_t
        df_ref[...] = (dx3_t * mod_ref[5:6, :]).astype(BF16)
        _accumulate(acc_ref, 0, dy * xh)
        _accumulate(acc_ref, 1, dx3_t * acc)
        _accumulate(acc_ref, 2, jnp.square(err) * (0.5 / d))

    dx3, dffo, sums_head = _matmul(ff, w_down_all, name="mlp_down", tm=ts, extra=[(x2, rows_d), (mod, vec8), (g_fin, vec1), (target, rows_d)],
                                   outs=[out_rows(d, F32), out_rows(d, BF16), sums_out], finish=loss_head, zero_first=(2,), parts=halves_of_tile,                                   transposed=False, reduce_blocks=True)

    partial_of, from_sibling, chip_sum, from_chips, half_done, quarter = {}, {}, {}, {}, {}, {}

    def publish(n, g):
        g = _blocks_to_pieces(g).astype(BF16) if n in block_weights else g
        partial_of[n] = g.reshape(N_CHIPS, 2, -1, mine[n].shape[-1])

    def exchange(to_sibling=(), to_chips=(), swap=()):
        st = _Stage()
        slots = [(n, from_sibling, _add_reduce_d2d(st, partial_of[n])) for n in to_sibling]
        slots += [(n, from_chips, _add_reduce_ici(st, chip_sum[n])) for n in to_chips]
        slots += [(n, quarter, _add_swap_halves(st, half_done[n])) for n in swap]
        return st, slots

    def collect(slots, outs):
        for n, where, i in slots:
            where[n] = outs[i]

    def sum_pairs(*ns):
        for n in ns:
            chip_sum[n] = _sum_pair(partial_of[n], from_sibling[n], place, "sum_pair_" + n)

    def sum_quarters(*ns):
        for n in ns:
            half_done[n] = _sum_quarter(partial_of[n], from_sibling[n], from_chips[n], place, "sum_quarter_" + n)

    publish("w_down", wg(ff, dffo, nb=1, name="grad_w_down"))

    def relu_squared_backward(k, acc, extra_refs, out_refs):
        cols = slice(k * n_up, (k + 1) * n_up)
        out_refs[0][:, cols] = (acc * (2.0 * jnp.sqrt(extra_refs[0][:, cols].astype(F32)))).astype(BF16)

    stage, slots = exchange(to_sibling=["w_down"])
    (dup,), got = _matmul(dffo, w_down_all, name="d_mlp_down", tm=ts, extra=[(ff, _row_spec(ts, D_FF))], outs=[out_rows(D_FF, BF16)],
                          finish=relu_squared_backward, parts=halves_of_tile, stage=stage, **back)
    collect(slots, got)
    sum_pairs("w_down")

    def norm_mlp_backward(k, acc, extra_refs, out_refs):
        x2_ref, dres_ref, mod_ref, g_ref, mo_ref = extra_refs
        dx_ref, dmo_ref, acc_ref = out_refs
        dx =_modulated_norm_backward(acc, x2_ref[...], dres_ref[...], mod_ref, g_ref, acc_ref, 4)
        dx_ref[...] = dx
        dmo_ref[...] = (dx * mod_ref[2:3, :]).astype(BF16)
        _accumulate(acc_ref, 3, dx * mo_ref[...].astype(F32))

    stage, slots = exchange(to_chips=["w_down"])
    (dx2, dmo, sums_mlp), got = _matmul(dup, w_up_all, name="d_mlp_up", tm=ts, stage=stage, finish=norm_mlp_backward,
                                        extra=[(x2, rows_d), (dx3, rows_d), (mod, vec8), (g_mlp, vec1), (mo, rows_d)],
                                        outs=[out_rows(d, F32), out_rows(d, BF16), sums_out], zero_first=(2,), parts=halves_of_tile, **back_sum)
    collect(slots, got)
    sum_quarters("w_down")
    publish("w_up", wg(h2, dup, nb=N_CHIPS, name="grad_w_up"))
    publish("w_out", wg(merged, dmo, nb=1, name="grad_w_out"))

    def merge_backward(k, acc, extra_refs, out_refs):
        sa_ref, sb_ref, ta_ref, tb_ref = extra_refs
        out_refs[0][...] = (acc * sa_ref[...].astype(F32)).astype(BF16)
        out_refs[1][...] = (acc * sb_ref[...].astype(F32)).astype(BF16)
        out_refs[2][:, 0:d] = (acc * ta_ref[...].astype(F32)).astype(BF16)
        out_refs[2][:, d:2 * d] = (acc * tb_ref[...].astype(F32)).astype(BF16)

    stage, slots = exchange(to_sibling=["w_up", "w_out"], swap=["w_down"])
    (dbr_a, dbr_b, dgab), got = _matmul(dmo, w_out_all, name="d_mix_out", tm=ts, stage=stage, finish=merge_backward,
                                        extra=[(s_a, rows_d), (s_b, rows_d), (t_a, rows_d), (t_b, rows_d)],
                                        outs=[out_rows(d, BF16), out_rows(d, BF16), out_rows(2 * d, BF16)], parts=halves_of_tile, **back_sum)
    collect(slots, got)
    publish("w_a", wg(ga, dbr_a, nb=1, name="grad_w_branch_a"))
    publish("w_b", wg(pooled, dbr_b, nb=1, name="grad_w_branch_b"))
    dga, = _matmul(dbr_a, w_a_all, name="d_branch_a", tm=tw, outs=[out_rows(d, BF16, tw)], finish=_store_blocks(d), **back_sum)
    dpooled, = _matmul(dbr_b, w_b_all, name="d_branch_b", tm=tw, outs=[out_rows(d, BF16, tw)], finish=_store_blocks(d), **back_sum)
    sum_pairs("w_up", "w_out")
    stage, slots = exchange(to_sibling=["w_a", "w_b"], to_chips=["w_up", "w_out"])
    (dproj, sums_seq, d_rg_a, d_rg_x, d_pool), got = _seq_backward_by_block(proj, xr, hr, p, dga, dpooled, dgab, conv_full, vecs,
                                                                            w_rg_a_all, w_rg_x_all, w_pool_all, ts=tsq, stage=stage)
    collect(slots, got)
    sum_quarters("w_up", "w_out")
    sum_pairs("w_a", "w_b")
    for n, g in zip(block_weights, (d_rg_a, d_rg_x, d_pool)):
        publish(n, g)
    stage, slots = exchange(to_sibling=block_weights, to_chips=["w_a", "w_b"], swap=["w_up", "w_out"])
    g_in, got = wg(h1, dproj, nb=N_CHIPS, name="grad_w_in", stage=stage)
    collect(slots, got)
    publish("w_in", g_in)
    sum_pairs(*block_weights)
    sum_quarters("w_a", "w_b")
    stage, slots = exchange(to_sibling=["w_in"], to_chips=block_weights)
    collect(slots, _run_stage(stage, "exchange_w_in_gates"))
    sum_pairs("w_in")
    sum_quarters(*block_weights)

    def norm_mix_backward(k, acc, extra_refs, out_refs):
        x_ref, dres_ref, mod_ref, g_ref = extra_refs
        out_refs[0][...] = _modulated_norm_backward(acc, x_ref[...], dres_ref[...], mod_ref, g_ref, out_refs[1], 1)

    stage, slots = exchange(to_chips=["w_in"], swap=block_weights + ("w_a", "w_b"))
    (grad_x, sums_mix), got = _matmul(dproj, w_in_all, name="d_proj_in", tm=ts, stage=stage, finish=norm_mix_backward,
                                      extra=[(x0, rows_d), (dx2, rows_d), (mod, vec8), (g_mix, vec1)],
                                      outs=[out_rows(d, F32), sums_out], zero_first=(1,), parts=halves_of_tile, **back_sum)
    collect(slots, got)
    sum_quarters("w_in")

    dmod = jnp.concatenate([sums_mix[1:2], sums_mix[0:1], sums_mlp[3:4], sums_mlp[1:2], sums_mlp[0:1], sums_head[1:2]], axis=1)
    row = jnp.concatenate([sums_mix[2:3], sums_mlp[2:3], sums_seq[4:5], sums_seq[5:6], sums_seq[6:7], sums_seq[7:8], sums_seq[8:9],
                           sums_seq[9:10], sums_head[0:1], sums_seq[0:CONV_WIDTH].reshape(1, CONV_WIDTH * d), dmod, sums_head[2:3]], axis=1)
    n_vec, n_conv, n_mod = 9 * d, CONV_WIDTH * d, N_MOD * d
    stage, slots = exchange(swap=["w_in"])
    i_rows = _add_allgather8(stage, row)
    got = _run_stage(stage, "exchange_sums_w_in")
    collect(slots, got)
    rows_all = got[i_rows]
    total = _sum_rows(rows_all, d)
    loss = total[0, n_vec + n_conv + n_mod]
    dmod_all = rows_all[:, 0, n_vec + n_conv:n_vec + n_conv + n_mod]
    g_ada, d_ada, m_ada, v_ada = _ada_backward(act_all.T, lax.dynamic_slice_in_dim(dmod_all, k_me * ada_cols, ada_cols, axis=1),
                                               w_ada[0], m_w_ada[0], v_w_ada[0])
    g_conv = lax.dynamic_slice_in_dim(total[:, n_vec:n_vec + n_conv].reshape(CONV_WIDTH, d), k_me * conv_cols, conv_cols, axis=1)
    vec_names = ("norm_mix_g", "norm_mlp_g", "conv_b", "b_rg_a", "b_rg_x", "a_param", "b_pool", "pool_scale", "final_g", "conv_w", "b_ada")
    vec_params = [(norm_mix_g, m_norm_mix_g, v_norm_mix_g), (norm_mlp_g, m_norm_mlp_g, v_norm_mlp_g), (conv_b, m_conv_b, v_conv_b),
                  (b_rg_a, m_b_rg_a, v_b_rg_a), (b_rg_x, m_b_rg_x, v_b_rg_x), (a_param, m_a_param, v_a_param), (b_pool, m_b_pool, v_b_pool),
                  (pool_scale, m_pool_scale, v_pool_scale), tuple(a.reshape(1, d) for a in (final_g, m_final_g, v_final_g)),
                  (conv_w[0], m_conv_w[0], v_conv_w[0]), (b_ada, m_b_ada, v_b_ada)]
    offsets = [i * d for i in range(9)] + [None, n_vec + n_conv]
    vec_res = _adamw_vectors(total, offsets, [None] * 9 + [g_conv, None], vec_params)
    vec_out = {n: [r.reshape(shape) for r in res] for n, res, shape in zip(
        vec_names, vec_res, [(1, d)] * 8 + [(d,), (1, CONV_WIDTH, conv_cols), (1, N_MOD * d)])}

    big_out = {}
    for n in names:
        shape2 = (-1, mine[n].shape[-1])
        g2 = quarter[n].reshape(shape2)
        res = _adamw(mine[n].reshape(shape2), g2, moments_m[n].reshape(shape2), moments_v[n].reshape(shape2), "adamw_" + n)
        big_out[n] = [r.reshape((1,) + mine[n].shape) for r in res]

    key = {"w_in": "w_in", "w_rg_a": "rg_a", "w_rg_x": "rg_x", "w_branch_a": "w_a", "w_pool": "pool", "w_branch_b": "w_b", "w_out": "w_out",
           "w_up": "w_up", "w_down": "w_down"}
    order = ("norm_mix_g", "norm_mlp_g", "w_ada", "b_ada", "w_in", "conv_w", "conv_b", "w_rg_a", "b_rg_a", "w_rg_x", "b_rg_x", "a_param",
             "w_branch_a", "w_pool", "b_pool", "pool_scale", "w_branch_b", "w_out", "w_up", "w_down", "final_g")
    ada_out = [g_ada[None], d_ada[None], m_ada[None], v_ada[None]]
    outs = [loss, grad_x[None]]
    for kind in range(4):
        for nme in order:
            outs.append(ada_out[kind] if nme == "w_ada" else big_out[key[nme]][kind] if nme in key else vec_out[nme][kind])
    return tuple(outs)
```

```python
import functools

import jax
import jax.numpy as jnp
from jax import lax
from jax.experimental import pallas as pl
from jax.experimental.pallas import tpu as pltpu
from jax.experimental.pallas import tpu_sc as plsc

F32, BF16 = jnp.float32, jnp.bfloat16
D_MODEL = 1024
D_FF = 4 * D_MODEL
N_BLOCKS = 4
BLOCK = D_MODEL // N_BLOCKS
CONV_WIDTH = 4
POOL_WINDOWS = (2, 4, 8, 16)
CONV_HALO = 8
POOL_HALO = 16
N_MOD = 6
EPS = 1e-6
C_RG = 8.0
ADAM_LR, ADAM_B1, ADAM_B2, ADAM_EPS, ADAM_WD, ADAM_STEP = 0.001, 0.9, 0.999, 1e-08, 0.01, 10
N_DEV = 8
N_CHIPS = 4
SC_TILES, SC_LANES, SC_SLAB = 32, 16, 8
VMEM_LIMIT_V7X = 56 * 2**20
MESH = pl.DeviceIdType.MESH
SDS = jax.ShapeDtypeStruct
HBM_SPEC = pl.BlockSpec(memory_space=pltpu.HBM)
VMEM_SPEC = pl.BlockSpec(memory_space=pltpu.VMEM)


def _params(*semantics):
    return pltpu.CompilerParams(dimension_semantics=semantics, vmem_limit_bytes=VMEM_LIMIT_V7X)


def _coords():
    return lax.axis_index("x"), lax.axis_index("y"), lax.axis_index("c")


def _flip(v, bit):
    return 1 - v if bit else v


def _chips():
    x, y, _ = _coords()
    return [(1 - x, y), (x, 1 - y), (1 - x, 1 - y)]


PEER_SETS = [frozenset(s) for s in (("sibling",), ("chips",), ("sibling", "chips"), ("sibling", "chips", "others"))]


class _Stage:
    def __init__(self):
        self.inputs, self.in_specs, self.outputs, self.out_specs, self.aliases = [], [], [], [], {}
        self.parts, self.n_copies, self.peers = [], 0, set()

    def add(self, inputs, in_spec, outputs, out_spec, n_copies, build, peers, alias=False):
        self.peers |= set(peers)
        i0, o0 = len(self.inputs), len(self.outputs)
        self.inputs += inputs
        self.in_specs += [in_spec] * len(inputs)
        self.outputs += outputs
        self.out_specs += [out_spec] * len(outputs)
        if alias:
            self.aliases.update({i0 + i: o0 + i for i in range(len(inputs))})
        self.parts.append((build, i0, len(inputs), o0, len(outputs)))
        self.n_copies += n_copies
        return list(range(o0, o0 + len(outputs)))

    def copies(self, in_refs, out_refs):
        out = []
        for build, i0, ni, o0, no in self.parts:
            out += build(in_refs[i0:i0 + ni], out_refs[o0:o0 + no])
        assert len(out) == self.n_copies
        return out

    def run(self, in_refs, out_refs, send_sems, recv_sems, start):
        x, y, c = _coords()
        for s, (src, dst, to, landing) in enumerate(self.copies(in_refs, out_refs)):
            if to is None:
                cp = pltpu.make_async_copy(src, dst, send_sems.at[s])
                cp.start() if start else cp.wait()
                continue
            cp = pltpu.make_async_remote_copy(src_ref=src, dst_ref=dst, send_sem=send_sems.at[s], recv_sem=recv_sems.at[s],
                                              device_id=to, device_id_type=MESH)
            if start:
                cp.start()
            else:
                pltpu.make_async_remote_copy(src_ref=landing, dst_ref=landing, send_sem=send_sems.at[s], recv_sem=recv_sems.at[s],
                                             device_id=(x, y, c), device_id_type=MESH).wait_recv()
                cp.wait_send()


def _hosted(stage, body, *, name, in_specs, out_specs, out_shape, grid=(), scratch_shapes=(), compiler_params=None):
    if stage is None:
        return pl.pallas_call(body, name=name, grid=grid, in_specs=in_specs, out_specs=out_specs, out_shape=out_shape,
                              scratch_shapes=list(scratch_shapes), compiler_params=compiler_params)
    single = not isinstance(out_shape, (list, tuple))
    h_out_shape = [out_shape] if single else list(out_shape)
    h_out_specs = [out_specs] if single else list(out_specs)
    n_in, n_out, n_scr = len(in_specs), len(h_out_shape), len(scratch_shapes)
    s_in, s_out = len(stage.inputs), len(stage.outputs)

    def wrapped(*refs):
        h_in, st_in = refs[:n_in], refs[n_in:n_in + s_in]
        h_o, st_o = refs[n_in + s_in:n_in + s_in + n_out], refs[n_in + s_in + n_out:n_in + s_in + n_out + s_out]
        h_scr = refs[n_in + s_in + n_out + s_out:n_in + s_in + n_out + s_out + n_scr]
        send_sems, recv_sems = refs[n_in + s_in + n_out + s_out + n_scr:]
        def begin():
            x, y, c = _coords()
            peers = [(x, y, 1 - c)] * ("sibling" in stage.peers)
            peers += [(px, py, c) for px, py in _chips()] * ("chips" in stage.peers)
            peers += [(px, py, 1 - c) for px, py in _chips()] * ("others" in stage.peers)
            barrier = pltpu.get_barrier_semaphore()
            for peer in peers:
                pl.semaphore_signal(barrier, inc=1, device_id=peer, device_id_type=MESH)
            pl.semaphore_wait(barrier, len(peers))
            stage.run(st_in, st_o, send_sems, recv_sems, True)

        if not grid:
            begin()
            if body is not None:
                body(*h_in, *h_o, *h_scr)
            stage.run(st_in, st_o, send_sems, recv_sems, False)
            return
        ids = [pl.program_id(a) for a in range(len(grid))]
        first = functools.reduce(jnp.logical_and, [i == 0 for i in ids])
        last = functools.reduce(jnp.logical_and, [i == g - 1 for i, g in zip(ids, grid)])
        pl.when(first)(begin)
        body(*h_in, *h_o, *h_scr)
        pl.when(last)(lambda: stage.run(st_in, st_o, send_sems, recv_sems, False))

    call = pl.pallas_call(
        wrapped, name=name, grid=grid, in_specs=list(in_specs) + stage.in_specs, out_specs=h_out_specs + stage.out_specs,
        out_shape=h_out_shape + stage.outputs, input_output_aliases={n_in + i: n_out + o for i, o in stage.aliases.items()},
        scratch_shapes=list(scratch_shapes) + [pltpu.SemaphoreType.DMA((stage.n_copies,)), pltpu.SemaphoreType.DMA((stage.n_copies,))],
        compiler_params=pltpu.CompilerParams(dimension_semantics=("arbitrary",) * len(grid), vmem_limit_bytes=VMEM_LIMIT_V7X,
                                             collective_id=PEER_SETS.index(frozenset(stage.peers))),
    )

    def run(*operands):
        outs = call(*operands, *stage.inputs)
        host = outs[:n_out]
        return (host[0] if single else host), outs[n_out:]

    return run


def _run_stage(stage, name):
    return _hosted(stage, None, name=name, in_specs=[], out_specs=[], out_shape=[])()[1]


def _add_allgather8(stage, v):
    def build(ins, outs):
        x, y, c = _coords()
        me = 4 * x + 2 * y + c
        copies = [(ins[0], outs[0].at[me], None, None)]
        for r in range(1, N_DEV):
            peer = (_flip(x, r & 4), _flip(y, r & 2), _flip(c, r & 1))
            copies.append((ins[0], outs[0].at[me], peer, outs[0].at[me ^ r]))
        return copies

    return stage.add([v], VMEM_SPEC, [SDS((N_DEV,) + v.shape, v.dtype)], VMEM_SPEC, N_DEV, build, ("sibling", "chips", "others"))[0]


def _add_gather_ici(stage, shard, rows=None, whole=None):
    first, count = rows or (0, shard.shape[1])

    def build(ins, outs):
        x, y, c = _coords()
        k_me = 2 * x + y
        part = pl.ds(first, count)
        copies = [] if whole is not None else [(ins[0], outs[0].at[k_me], (x, y, 1 - c), outs[0].at[k_me])]
        for px, py in _chips():
            copies.append((ins[0].at[c, part], outs[0].at[k_me, c, part], (px, py, c), outs[0].at[2 * px + py, c, part]))
        return copies

    if whole is None:
        return stage.add([shard], HBM_SPEC, [SDS((N_CHIPS,) + shard.shape, shard.dtype)], HBM_SPEC, N_CHIPS, build, ("sibling", "chips"))[0]

    def build_into(ins, outs):
        return build(ins[1:], outs)

    i0 = len(stage.inputs)
    out = stage.add([whole, shard], HBM_SPEC, [SDS(whole.shape, whole.dtype)], HBM_SPEC, N_CHIPS - 1, build_into, ("chips",))[0]
    stage.aliases[i0] = out
    return out


def _add_gather_d2d(stage, whole):
    def build(ins, outs):
        x, y, c = _coords()
        return [(outs[0].at[2 * px + py, c], outs[0].at[2 * px + py, c], (x, y, 1 - c), outs[0].at[2 * px + py, 1 - c]) for px, py in _chips()]

    return stage.add([whole], HBM_SPEC, [SDS(whole.shape, whole.dtype)], HBM_SPEC, N_CHIPS - 1, build, ("sibling",), alias=True)[0]


def _add_reduce_d2d(stage, grads):
    def build(ins, outs):
        x, y, c = _coords()
        return [(ins[0].at[k, 1 - c], outs[0].at[k], (x, y, 1 - c), outs[0].at[k]) for k in range(N_CHIPS)]

    return stage.add([grads], HBM_SPEC, [SDS((N_CHIPS,) + grads.shape[2:], grads.dtype)], HBM_SPEC, N_CHIPS, build, ("sibling",))[0]


def _add_reduce_ici(stage, partial):
    def build(ins, outs):
        x, y, c = _coords()
        return [(ins[0].at[2 * px + py], outs[0].at[j], (px, py, c), outs[0].at[j]) for j, (px, py) in enumerate(_chips())]

    return stage.add([partial], HBM_SPEC, [SDS((N_CHIPS - 1,) + partial.shape[1:], partial.dtype)], HBM_SPEC, N_CHIPS - 1, build, ("chips",))[0]


def _add_swap_halves(stage, quarter):
    def build(ins, outs):
        x, y, c = _coords()
        return [(outs[0].at[c], outs[0].at[c], (x, y, 1 - c), outs[0].at[1 - c])]

    return stage.add([quarter], HBM_SPEC, [SDS(quarter.shape, quarter.dtype)], HBM_SPEC, 1, build, ("sibling",), alias=True)[0]


def _sum_pair(grads, from_sibling, place, name):
    _, _, rows, cols = grads.shape
    tr = min(rows, 512)

    def body(place_ref, g_ref, s_ref, o_ref):
        o_ref[0] = (g_ref[0, 0].astype(F32) + s_ref[0].astype(F32)).astype(BF16)

    spec = pl.BlockSpec((1, tr, cols), lambda k, i, place_ref: (k, i, 0))
    return pl.pallas_call(
        body, name=name, out_shape=SDS((N_CHIPS, rows, cols), BF16),
        grid_spec=pltpu.PrefetchScalarGridSpec(
            num_scalar_prefetch=1, grid=(N_CHIPS, rows // tr),
            in_specs=[pl.BlockSpec((1, 1, tr, cols), lambda k, i, place_ref: (k, place_ref[0], i, 0)), spec], out_specs=spec),
        compiler_params=_params("parallel", "parallel"),
    )(place, grads, from_sibling)


def _sum_quarter(grads, from_sibling, from_chips, place, name):
    _, _, rows, cols = grads.shape
    tr = min(rows, 256)

    def body(place_ref, g_ref, s_ref, c_ref, o_ref):
        acc = g_ref[0, 0].astype(F32) + s_ref[0].astype(F32)
        for j in range(N_CHIPS - 1):
            acc = acc + c_ref[j].astype(F32)
        o_ref[0] = acc

    return pl.pallas_call(
        body, name=name, out_shape=SDS((2, rows, cols), F32),
        grid_spec=pltpu.PrefetchScalarGridSpec(
            num_scalar_prefetch=1, grid=(rows // tr,),
            in_specs=[pl.BlockSpec((1, 1, tr, cols), lambda i, place_ref: (place_ref[1], place_ref[0], i, 0)),
                      pl.BlockSpec((1, tr, cols), lambda i, place_ref: (place_ref[1], i, 0)),
                      pl.BlockSpec((N_CHIPS - 1, tr, cols), lambda i, place_ref: (0, i, 0))],
            out_specs=pl.BlockSpec((1, tr, cols), lambda i, place_ref: (place_ref[0], i, 0))),
        compiler_params=_params("parallel"),
    )(place, grads, from_sibling, from_chips)


def _row_spec(ts, cols, col_block=0):
    return pl.BlockSpec((ts, cols), lambda *g: (g[0], col_block))


def _vec_spec(rows, cols):
    return pl.BlockSpec((rows, cols), lambda *g: (0, 0))


def _matmul(a, w, *, transposed, reduce_blocks, name, tm, outs, finish, extra=(), prepare=None, zero_first=(), parts=1, stage=None):
    m_rows = a.shape[0]
    nb, r, c = w.shape
    kb = c if transposed else r
    dims = (((1,), (1,)), ((), ())) if transposed else (((1,), (0,)), ((), ()))
    assert a.shape[1] == (nb * kb if reduce_blocks else kb) and m_rows % tm == 0 and tm % parts == 0
    n_extra, sub = len(extra), tm // parts

    def body(a_ref, w_ref, *rest):
        extra_refs, out_refs = rest[:n_extra], rest[n_extra:]
        if zero_first:
            @pl.when(pl.program_id(0) == 0)
            def _():
                for i in zero_first:
                    out_refs[i][...] = jnp.zeros_like(out_refs[i])

        def views(j):
            rows = pl.ds(j * sub, sub)
            return rows, [ref.at[rows] if ref.shape[0] == tm else ref for ref in extra_refs], [ref.at[rows] if ref.shape[0] == tm else ref for ref in out_refs]

        def product(j):
            rows, ex, ou = views(j)
            if reduce_blocks:
                acc = lax.dot_general(a_ref[rows, 0:kb], w_ref[0], dims, preferred_element_type=F32)
                for k in range(1, nb):
                    acc += lax.dot_general(a_ref[rows, k * kb:(k + 1) * kb], w_ref[k], dims, preferred_element_type=F32)
                return [(None, acc)]
            lhs = a_ref[rows, :] if prepare is None else prepare(a_ref.at[rows], ex, ou)
            return [(k, lax.dot_general(lhs, w_ref[k], dims, preferred_element_type=F32)) for k in range(nb)]

        pending = product(0)
        for j in range(parts):
            ahead = product(j + 1) if j + 1 < parts else None
            _, ex, ou = views(j)
            for k, acc in pending:
                finish(k, acc, ex, ou)
            pending = ahead

    return _hosted(
        stage, body, name=name, grid=(m_rows // tm,), out_shape=[s for s, _ in outs], out_specs=[s for _, s in outs],
        in_specs=[_row_spec(tm, a.shape[1]), pl.BlockSpec((nb, r, c), lambda *g: (0, 0, 0), pipeline_mode=pl.Buffered(1))] + [s for _, s in extra],
        compiler_params=_params("arbitrary"),
    )(a, w, *[e for e, _ in extra])


def _store_blocks(n):
    def finish(k, acc, extra_refs, out_refs):
        if k is None:
            out_refs[0][...] = acc.astype(out_refs[0].dtype)
        else:
            out_refs[0][:, k * n:(k + 1) * n] = acc.astype(out_refs[0].dtype)
    return finish


def _weight_grad(a, b, *, nb, name, tk, stage=None):
    s_rows, k1 = a.shape
    tn = b.shape[1] // nb
    assert k1 % tk == 0 and b.shape[1] % nb == 0

    def body(a_ref, b_ref, o_ref):
        o_ref[0] = lax.dot_general(a_ref[...], b_ref[...], (((0,), (0,)), ((), ())), preferred_element_type=F32).astype(o_ref.dtype)

    return _hosted(
        stage, body, name=name, grid=(nb, k1 // tk), out_shape=SDS((nb, k1, tn), BF16),
        in_specs=[pl.BlockSpec((s_rows, tk), lambda n, i: (0, i)), pl.BlockSpec((s_rows, tn), lambda n, i: (0, n))],
        out_specs=pl.BlockSpec((1, tk, tn), lambda n, i: (n, i, 0)), compiler_params=_params("parallel", "parallel"),
    )(a, b)


def _rms(x):
    r = lax.rsqrt(jnp.mean(x * x, axis=-1, keepdims=True) + EPS)
    return x * r, r


def _accumulate(acc_ref, row, value):
    acc_ref[row:row + 1, :] += jnp.sum(value, axis=0, keepdims=True)


def _gelu_parts(y):
    k0, k1 = 0.7978845608028654, 0.044715
    th = jnp.tanh(k0 * (y + k1 * (y * y * y)))
    gelu = 0.5 * y * (1.0 + th)
    dgelu = 0.5 * (1.0 + th) + 0.5 * y * (1.0 - th * th) * (k0 * (1.0 + 3.0 * k1 * (y * y)))
    return gelu, dgelu


def _sigmoid(z):
    return 0.5 * jnp.tanh(0.5 * z) + 0.5


def _one_minus_square(a, log_a):
    return -jnp.tanh(log_a) * (1.0 + a * a)


def _modulated_norm(x, mod_ref, g_ref, sc_row, sh_row):
    xh, _ = _rms(x)
    return ((xh * g_ref[...]) * (1.0 + mod_ref[sc_row:sc_row + 1, :]) + mod_ref[sh_row:sh_row + 1, :]).astype(BF16)


def _modulated_norm_backward(dh, x, dres, mod_ref, g_ref, acc_ref, sc_row):
    xh, r = _rms(x)
    _accumulate(acc_ref, 0, dh * (xh * g_ref[...]))
    _accumulate(acc_ref, 1, dh)
    dn = dh * (1.0 + mod_ref[sc_row:sc_row + 1, :])
    _accumulate(acc_ref, 2, dn * xh)
    dxh = dn * g_ref[...]
    return dres + r * (dxh - xh * jnp.mean(dxh * xh, axis=-1, keepdims=True))


def _seq_forward(proj, conv_w, vecs, w_a, w_x, w_p, *, ts, stage=None):
    s_rows = proj.shape[0]
    d = D_MODEL

    def body(x_ref, xh_ref, y_ref, u_ref, uh_ref, cw_ref, vec_ref, wa_ref, wx_ref, wp_ref,
             xr_o, hr_o, ga_o, p_o, pooled_o, carry, a_scr, b_scr):
        t = pl.program_id(0)

        @pl.when(t == 0)
        def _():
            carry[...] = jnp.zeros_like(carry)

        rows = lax.broadcasted_iota(jnp.int32, (ts, BLOCK), 0)
        start = (rows == 0) & (t == 0)
        for g in range(N_BLOCKS):
            lanes = slice(g * BLOCK, (g + 1) * BLOCK)
            x = x_ref[:, lanes]
            xx = jnp.concatenate([jnp.where(t == 0, 0.0, xh_ref[:, lanes]), x], axis=0)
            xr = vec_ref[0:1, lanes] + x * cw_ref[CONV_WIDTH - 1:CONV_WIDTH, lanes]
            for j in range(CONV_WIDTH - 1):
                xr = xr + pltpu.roll(xx, CONV_WIDTH - 1 - j, 0)[CONV_HALO:] * cw_ref[j:j + 1, lanes]
            xr_o[:, lanes] = xr
            xrb = xr.astype(BF16)
            r = _sigmoid(jnp.dot(xrb, wa_ref[g], preferred_element_type=F32) + vec_ref[1:2, lanes])
            gate_i = _sigmoid(jnp.dot(xrb, wx_ref[g], preferred_element_type=F32) + vec_ref[2:3, lanes])
            log_a = (-C_RG * r) * jax.nn.softplus(vec_ref[3:4, lanes])
            a = jnp.exp(log_a)
            a_scr[:, lanes] = a
            b_scr[:, lanes] = xr * gate_i * jnp.where(start, 1.0, jnp.sqrt(_one_minus_square(a, log_a)))

        sub = lax.broadcasted_iota(jnp.int32, (8, d), 0)

        def chunk(i, h):
            a = a_scr[pl.ds(i * 8, 8), :]
            b = b_scr[pl.ds(i * 8, 8), :]
            for s in (1, 2, 4):
                keep = sub >= s
                b = jnp.where(keep, a * pltpu.roll(b, s, 0) + b, b)
                a = jnp.where(keep, a * pltpu.roll(a, s, 0), a)
            hh = a * h + b
            hr_o[pl.ds(i * 8, 8), :] = hh
            return hh[7:8, :]

        carry[...] = lax.fori_loop(0, ts // 8, chunk, carry[...])
        pos = (rows + t * ts + 1).astype(F32)
        for g, win in enumerate(POOL_WINDOWS):
            lanes = slice(g * BLOCK, (g + 1) * BLOCK)
            gelu, _ = _gelu_parts(y_ref[:, lanes])
            ga_o[:, lanes] = (gelu * hr_o[:, lanes]).astype(BF16)
            u = u_ref[:, lanes]
            sums, have = jnp.concatenate([jnp.where(t == 0, 0.0, uh_ref[:, lanes]), u], axis=0), 1
            while have < win:
                sums = sums + pltpu.roll(sums, have, 0)
                have *= 2
            pb = (sums[POOL_HALO:] * (1.0 / jnp.minimum(pos, float(win))) - u).astype(BF16)
            p_o[:, lanes] = pb
            mixed = jnp.dot(pb, wp_ref[g], preferred_element_type=F32) + vec_ref[4:5, lanes]
            pooled_o[:, lanes] = (mixed * vec_ref[5:6, lanes]).astype(BF16)

    halo_x = pl.BlockSpec((CONV_HALO, d), lambda t: (jnp.maximum(t * (ts // CONV_HALO) - 1, 0), 0))
    halo_u = pl.BlockSpec((POOL_HALO, d), lambda t: (jnp.maximum(t * (ts // POOL_HALO) - 1, 0), 2))
    wspec = pl.BlockSpec((N_BLOCKS, BLOCK, BLOCK), lambda t: (0, 0, 0))
    return _hosted(
        stage, body, name="seq_forward", grid=(s_rows // ts,),
        out_shape=[SDS((s_rows, d), F32), SDS((s_rows, d), F32), SDS((s_rows, d), BF16), SDS((s_rows, d), BF16), SDS((s_rows, d), BF16)],
        in_specs=[_row_spec(ts, d, 0), halo_x, _row_spec(ts, d, 1), _row_spec(ts, d, 2), halo_u, _vec_spec(CONV_WIDTH, d), _vec_spec(8, d),
                  wspec, wspec, wspec],
        out_specs=[_row_spec(ts, d)] * 5,
        scratch_shapes=[pltpu.VMEM((1, d), F32), pltpu.VMEM((ts, d), F32), pltpu.VMEM((ts, d), F32)],
        compiler_params=_params("arbitrary"),
    )(proj, proj, proj, proj, proj, conv_w, vecs, w_a, w_x, w_p)


def _branches(ga, pooled, proj, w_a, w_b, *, ts, stage=None):
    s_rows, d = ga.shape

    def body(a_ref, p_ref, ga_ref, gb_ref, wa_ref, wb_ref, merged_o, sa_o, sb_o, ta_o, tb_o):
        bra = jnp.dot(a_ref[...], wa_ref[0], preferred_element_type=F32)
        brb = jnp.dot(p_ref[...], wb_ref[0], preferred_element_type=F32)
        sa, sb = _sigmoid(ga_ref[...]), _sigmoid(gb_ref[...])
        merged_o[...] = (sa * bra + sb * brb).astype(BF16)
        sa_o[...] = sa.astype(BF16)
        sb_o[...] = sb.astype(BF16)
        ta_o[...] = (bra * (sa * (1.0 - sa))).astype(BF16)
        tb_o[...] = (brb * (sb * (1.0 - sb))).astype(BF16)

    rows = _row_spec(ts, d)
    wspec = pl.BlockSpec((1, d, d), lambda *g: (0, 0, 0), pipeline_mode=pl.Buffered(1))
    return _hosted(
        stage, body, name="branches", grid=(s_rows // ts,), out_shape=[SDS((s_rows, d), BF16)] * 5,
        in_specs=[rows, rows, _row_spec(ts, d, 3), _row_spec(ts, d, 4), wspec, wspec], out_specs=[rows] * 5,
        compiler_params=_params("arbitrary"),
    )(ga, pooled, proj, proj, w_a, w_b)


def _seq_backward_by_block(proj, xr, hr, p, dga, dpooled, dgab, conv_w, vecs, w_a, w_x, w_p, *, ts, stage=None):
    s_rows = proj.shape[0]
    d = D_MODEL
    n_t = s_rows // ts
    nn, nt, tn = (((1,), (0,)), ((), ())), (((1,), (1,)), ((), ())), (((0,), (0,)), ((), ()))

    def dot(lhs, rhs, dims):
        return lax.dot_general(lhs, rhs, dims, preferred_element_type=F32)

    def body(x_ref, y_ref, xr_ref, hr_ref, hh_ref, p_ref, dga_ref, dpl_ref, dgab_ref, cw_ref, vec_ref, wa_ref, wx_ref, wp_ref,
             dproj_o, acc_o, dwa_o, dwx_o, dwp_o, g_carry, dxr_carry, q_carry, a_scr, b_scr, g_scr, a_keep, r_keep, i_keep, m_keep):
        step = pl.program_id(0)
        t = n_t - 1 - step

        @pl.when(step == 0)
        def _():
            for ref in (acc_o, dwa_o, dwx_o, dwp_o, g_carry, dxr_carry, q_carry):
                ref[...] = jnp.zeros_like(ref)

        rows = lax.broadcasted_iota(jnp.int32, (ts, BLOCK), 0)
        start = (rows == 0) & (t == 0)

        def add_sum(row, lanes, value):
            acc_o[row:row + 1, lanes] += jnp.sum(value, axis=0, keepdims=True)

        for g in range(N_BLOCKS):
            lanes = slice(g * BLOCK, (g + 1) * BLOCK)
            xrb = xr_ref[:, lanes].astype(BF16)
            r = _sigmoid(dot(xrb, wa_ref[g], nn) + vec_ref[1:2, lanes])
            gate_i = _sigmoid(dot(xrb, wx_ref[g], nn) + vec_ref[2:3, lanes])
            log_a = (-C_RG * r) * jax.nn.softplus(vec_ref[3:4, lanes])
            a = jnp.exp(log_a)
            a_keep[:, lanes], r_keep[:, lanes], i_keep[:, lanes], m_keep[:, lanes] = a, r, gate_i, _one_minus_square(a, log_a)
            gelu, dgelu = _gelu_parts(y_ref[:, lanes])
            dga_t = dga_ref[:, lanes].astype(F32)
            dproj_o[:, d + g * BLOCK:d + (g + 1) * BLOCK] = (dga_t * hr_ref[:, lanes] * dgelu).astype(BF16)
            a_scr[:, lanes] = jnp.where(rows == ts - 1, 1.0, pltpu.roll(a, ts - 1, 0))
            b_scr[:, lanes] = dga_t * gelu

        sub = lax.broadcasted_iota(jnp.int32, (8, d), 0)

        def chunk(i, g_next):
            at = pl.multiple_of((ts // 8 - 1 - i) * 8, 8)
            aa = a_scr[pl.ds(at, 8), :]
            bb = b_scr[pl.ds(at, 8), :]
            for s in (1, 2, 4):
                keep = sub < 8 - s
                bb = jnp.where(keep, bb + aa * pltpu.roll(bb, 8 - s, 0), bb)
                aa = jnp.where(keep, aa * pltpu.roll(aa, 8 - s, 0), aa)
            gg = aa * g_next + bb
            g_scr[pl.ds(at, 8), :] = gg
            return gg[0:1, :]

        g_first = lax.fori_loop(0, ts // 8, chunk, g_carry[...])
        g_carry[...] = a_keep[0:1, :] * g_first

        pos = (rows + t * ts + 1).astype(F32)
        for g, win in enumerate(POOL_WINDOWS):
            lanes = slice(g * BLOCK, (g + 1) * BLOCK)
            gs, xr, a, r, gate_i, m_square = g_scr[:, lanes], xr_ref[:, lanes], a_keep[:, lanes], r_keep[:, lanes], i_keep[:, lanes], m_keep[:, lanes]
            sp = jax.nn.softplus(vec_ref[3:4, lanes])
            mult = jnp.where(start, 1.0, jnp.sqrt(m_square))
            h_before = jnp.where(t == 0, 0.0, hh_ref[CONV_HALO - 1:CONV_HALO, lanes])
            h_prev = jnp.where(rows == 0, h_before, pltpu.roll(hr_ref[:, lanes], 1, 0))
            gm = gs * mult
            dxr = gm * gate_i
            d_i = gm * xr
            d_mult = (gs * xr) * gate_i
            dlog_a = (gs * h_prev) * a - jnp.where(start, 0.0, d_mult * (a * a) * lax.rsqrt(m_square))
            dzr = (dlog_a * (-C_RG * sp)) * (r * (1.0 - r))
            dzi = d_i * (gate_i * (1.0 - gate_i))
            add_sum(5, lanes, dzr)
            add_sum(6, lanes, dzi)
            acc_o[7:8, lanes] += jnp.sum(dlog_a * r, axis=0, keepdims=True) * (-C_RG * jax.nn.sigmoid(vec_ref[3:4, lanes]))
            xrb, dzrb, dzib = xr.astype(BF16), dzr.astype(BF16), dzi.astype(BF16)
            dwa_o[g] += dot(xrb, dzrb, tn)
            dwx_o[g] += dot(xrb, dzib, tn)
            dxr = dxr + dot(dzrb, wa_ref[g], nt) + dot(dzib, wx_ref[g], nt)

            x = x_ref[:, lanes]
            ext = jnp.concatenate([dxr, dxr_carry[:, lanes]], axis=0)
            dx = dxr * cw_ref[CONV_WIDTH - 1:CONV_WIDTH, lanes]
            add_sum(CONV_WIDTH - 1, lanes, x * dxr)
            for j in range(CONV_WIDTH - 1):
                ahead = pltpu.roll(ext, ts + CONV_HALO - (CONV_WIDTH - 1 - j), 0)[:ts]
                dx = dx + ahead * cw_ref[j:j + 1, lanes]
                add_sum(j, lanes, x * ahead)
            add_sum(4, lanes, dxr)
            dxr_carry[:, lanes] = dxr[0:CONV_HALO, :]
            dproj_o[:, lanes] = dx.astype(BF16)

            pb = p_ref[:, lanes]
            mixed = dot(pb, wp_ref[g], nn) + vec_ref[4:5, lanes]
            dpl = dpl_ref[:, lanes].astype(F32)
            add_sum(9, lanes, dpl * mixed)
            dmixed = dpl * vec_ref[5:6, lanes]
            add_sum(8, lanes, dmixed)
            dmb = dmixed.astype(BF16)
            dwp_o[g] += dot(pb, dmb, tn)
            dp = dot(dmb, wp_ref[g], nt)
            q = dp * (1.0 / jnp.minimum(pos, float(win)))
            sums, have = jnp.concatenate([q, q_carry[:, lanes]], axis=0), 1
            while have < win:
                sums = sums + pltpu.roll(sums, ts + POOL_HALO - have, 0)
                have *= 2
            q_carry[:, lanes] = q[0:POOL_HALO, :]
            dproj_o[:, 2 * d + g * BLOCK:2 * d + (g + 1) * BLOCK] = (sums[:ts] - dp).astype(BF16)

        dproj_o[:, 3 * d:5 * d] = dgab_ref[...]

    def rev(cols, col_block=0):
        return pl.BlockSpec((ts, cols), lambda i: (n_t - 1 - i, col_block))

    halo_h = pl.BlockSpec((CONV_HALO, d), lambda i: (jnp.maximum((n_t - 1 - i) * (ts // CONV_HALO) - 1, 0), 0))
    wspec = pl.BlockSpec((N_BLOCKS, BLOCK, BLOCK), lambda i: (0, 0, 0))
    return _hosted(
        stage, body, name="seq_backward", grid=(n_t,),
        out_shape=[SDS((s_rows, 5 * d), BF16), SDS((16, d), F32)] + [SDS((N_BLOCKS, BLOCK, BLOCK), F32)] * 3,
        in_specs=[rev(d, 0), rev(d, 1), rev(d), rev(d), halo_h, rev(d), rev(d), rev(d), rev(2 * d), _vec_spec(CONV_WIDTH, d), _vec_spec(8, d),
                  wspec, wspec, wspec],
        out_specs=[rev(5 * d), _vec_spec(16, d), wspec, wspec, wspec],
        scratch_shapes=[pltpu.VMEM((1, d), F32), pltpu.VMEM((CONV_HALO, d), F32), pltpu.VMEM((POOL_HALO, d), F32)] + [pltpu.VMEM((ts, d), F32)] * 7,
        compiler_params=_params("arbitrary"),
    )(proj, proj, xr, hr, hr, p, dga, dpooled, dgab, conv_w, vecs, w_a, w_x, w_p)


def _adamw_math(w, g, m, v):
    m = ADAM_B1 * m + (1.0 - ADAM_B1) * g
    v = ADAM_B2 * v + (1.0 - ADAM_B2) * (g * g)
    m_hat = m / (1.0 - ADAM_B1 ** ADAM_STEP)
    v_hat = v / (1.0 - ADAM_B2 ** ADAM_STEP)
    delta = -ADAM_LR * (m_hat / (jnp.sqrt(v_hat) + ADAM_EPS) + ADAM_WD * w)
    return delta, m, v


def _adamw(w, g, m, v, name):
    rows, cols = w.shape
    tr = min(rows, 512)

    def body(w_ref, g_ref, m_ref, v_ref, g_out, d_ref, mo_ref, vo_ref):
        g = g_ref[...]
        g_out[...] = g
        d_ref[...], mo_ref[...], vo_ref[...] = _adamw_math(w_ref[...], g, m_ref[...], v_ref[...])

    spec = pl.BlockSpec((tr, cols), lambda i: (i, 0))
    return pl.pallas_call(
        body, name=name, grid=(rows // tr,), out_shape=[SDS((rows, cols), F32)] * 4, in_specs=[spec] * 4, out_specs=[spec] * 4,
        compiler_params=_params("parallel"),
    )(w, g, m, v)


def _adamw_on_sparsecores(w, g, m, v, name):
    rows, cols = w.shape
    per_tile = rows // SC_TILES
    assert rows % (SC_TILES * SC_SLAB) == 0 and cols % SC_LANES == 0

    def body(w_hbm, g_hbm, m_hbm, v_hbm, g_out, d_out, m_out, v_out, wb, gb, mb, vb, db):
        tile = lax.axis_index("subcore") * 2 + lax.axis_index("core")

        @pl.loop(0, per_tile, step=SC_SLAB)
        def _(r0):
            at = pl.ds(tile * per_tile + r0, SC_SLAB)
            for src, buf in ((w_hbm, wb), (g_hbm, gb), (m_hbm, mb), (v_hbm, vb)):
                pltpu.sync_copy(src.at[at], buf)
            for r in range(SC_SLAB):
                @pl.loop(0, cols, step=SC_LANES)
                def _(col):
                    lanes = pl.ds(col, SC_LANES)
                    db[r, lanes], mb[r, lanes], vb[r, lanes] = _adamw_math(wb[r, lanes], gb[r, lanes], mb[r, lanes], vb[r, lanes])
            for buf, dst in ((gb, g_out), (db, d_out), (mb, m_out), (vb, v_out)):
                pltpu.sync_copy(buf, dst.at[at])

    return pl.kernel(
        body, name=name, out_type=[SDS(w.shape, F32)] * 4,
        mesh=plsc.VectorSubcoreMesh(core_axis_name="core", subcore_axis_name="subcore"),
        scratch_types=[pltpu.VMEM((SC_SLAB, cols), F32)] * 5,
    )(w, g, m, v)


def _prologue(c, conv_shard, w_ada, b_cols, w_in_halves, gate_halves):
    n_cols = w_ada.shape[1]
    n_g = len(gate_halves)
    half_rows = w_in_halves.shape[1] // 2
    n_sems = 3 * N_DEV + N_CHIPS * n_g + 5 + 2 + 3

    def body(c_ref, conv_ref, w_ref, b_ref, win_ref, *rest):
        gate_refs = rest[:n_g]
        act_o, mod_o, conv_o, win_o = rest[n_g:n_g + 4]
        gate_o = rest[n_g + 4:2 * n_g + 4]
        c_all, mod_cols, w_vmem, send_sems, recv_sems = rest[2 * n_g + 4:]
        fetch_w = pltpu.make_async_copy(w_ref, w_vmem, send_sems.at[n_sems])
        x, y, c = _coords()
        me, k_me = 4 * x + 2 * y + c, 2 * x + y
        sibling, x_nbr, y_nbr = (x, y, 1 - c), (1 - x, y, c), (x, 1 - y, c)
        k_x, k_y, k_d = 2 * (1 - x) + y, 2 * x + (1 - y), 2 * (1 - x) + (1 - y)
        counter = iter(range(n_sems))

        def copy(src, dst, to, landing):
            s = next(counter)
            out = pltpu.make_async_remote_copy(src_ref=src, dst_ref=dst, send_sem=send_sems.at[s], recv_sem=recv_sems.at[s],
                                               device_id=to, device_id_type=MESH)
            arrival = pltpu.make_async_remote_copy(src_ref=landing, dst_ref=landing, send_sem=send_sems.at[s], recv_sem=recv_sems.at[s],
                                                   device_id=(x, y, c), device_id_type=MESH)
            return out, arrival

        def allgather(src, out):
            s = next(counter)
            own = pltpu.make_async_copy(src, out.at[me], send_sems.at[s])
            peers = [copy(src, out.at[me], (_flip(x, r & 4), _flip(y, r & 2), _flip(c, r & 1)), out.at[me ^ r]) for r in range(1, N_DEV)]
            return own, peers

        def start(group):
            own, peers = group
            own.start()
            for out, _ in peers:
                out.start()

        def finish(group):
            own, peers = group
            for out, arrival in peers:
                arrival.wait_recv()
                out.wait_send()
            own.wait()

        rows_a, rows_b = pl.ds(0, half_rows), pl.ds(half_rows, half_rows)
        c_group, conv_group = allgather(c_ref, c_all), allgather(conv_ref, conv_o)
        gates = []
        for g_ref, g_out in zip(gate_refs, gate_o):
            gates.append(copy(g_ref, g_out.at[k_me], sibling, g_out.at[k_me]))
            gates += [copy(g_ref.at[c], g_out.at[k_me, c], (px, py, c), g_out.at[2 * px + py, c]) for px, py in _chips()]
        own_in = copy(win_ref, win_o.at[k_me], sibling, win_o.at[k_me])
        to_x = [copy(win_ref.at[c, rows], win_o.at[k_me, c, rows], x_nbr, win_o.at[k_x, c, rows]) for rows in (rows_a, rows_b)]
        to_y = [copy(win_ref.at[c, rows], win_o.at[k_me, c, rows], y_nbr, win_o.at[k_y, c, rows]) for rows in (rows_b, rows_a)]
        mod_group = allgather(mod_cols, mod_o)
        relay_y = copy(win_o.at[k_x, c, rows_a], win_o.at[k_x, c, rows_a], y_nbr, win_o.at[k_d, c, rows_a])
        relay_x = copy(win_o.at[k_y, c, rows_b], win_o.at[k_y, c, rows_b], x_nbr, win_o.at[k_d, c, rows_b])
        sib_x, sib_y, sib_d = (copy(win_o.at[k, c], win_o.at[k, c], sibling, win_o.at[k, 1 - c]) for k in (k_x, k_y, k_d))

        start(c_group)
        for out, _ in to_x + to_y + [own_in] + gates:
            out.start()
        start(conv_group)
        fetch_w.start()
        finish(c_group)
        cv = jnp.concatenate([c_all[r] for r in range(N_DEV)], axis=0)
        act = cv * jax.nn.sigmoid(cv)
        act_o[...] = act
        fetch_w.wait()
        mod_cols[...] = jnp.dot(act.astype(BF16), w_vmem[...].astype(BF16), preferred_element_type=F32) + b_ref[...]
        start(mod_group)
        to_x[0][1].wait_recv()
        relay_y[0].start()
        to_y[0][1].wait_recv()
        relay_x[0].start()
        finish(mod_group)
        finish(conv_group)
        to_x[1][1].wait_recv()
        sib_x[0].start()
        to_y[1][1].wait_recv()
        sib_y[0].start()
        for out, arrival in gates + [own_in]:
            arrival.wait_recv()
            out.wait_send()
        for out, arrival in (relay_y, relay_x):
            arrival.wait_recv()
            out.wait_send()
        sib_d[0].start()
        for out, _ in to_x + to_y:
            out.wait_send()
        for out, arrival in (sib_x, sib_y, sib_d):
            arrival.wait_recv()
            out.wait_send()

    gate_shapes = [SDS((N_CHIPS,) + g.shape, g.dtype) for g in gate_halves]
    return pl.pallas_call(
        body, name="prologue",
        out_shape=[SDS((N_DEV, c.shape[1]), F32), SDS((N_DEV, N_DEV, n_cols), F32), SDS((N_DEV,) + conv_shard.shape, F32),
                   SDS((N_CHIPS,) + w_in_halves.shape, w_in_halves.dtype)] + gate_shapes,
        in_specs=[VMEM_SPEC, VMEM_SPEC, HBM_SPEC, VMEM_SPEC] + [HBM_SPEC] * (1 + n_g), out_specs=[VMEM_SPEC] * 3 + [HBM_SPEC] * (1 + n_g),
        scratch_shapes=[pltpu.VMEM((N_DEV,) + c.shape, F32), pltpu.VMEM((N_DEV, n_cols), F32), pltpu.VMEM(w_ada.shape, F32),
                        pltpu.SemaphoreType.DMA((n_sems + 1,)), pltpu.SemaphoreType.DMA((n_sems,))],
        compiler_params=pltpu.CompilerParams(vmem_limit_bytes=VMEM_LIMIT_V7X),
    )(c, conv_shard, w_ada, b_cols, w_in_halves, *gate_halves)


def _ada_backward(act_t, dmod_cols, w, m, v):
    rows, cols = w.shape
    tn = 512

    def body(a_ref, dm_ref, w_ref, m_ref, v_ref, g_ref, d_ref, mo_ref, vo_ref):
        g = jnp.dot(a_ref[...].astype(BF16), dm_ref[...].astype(BF16), preferred_element_type=F32)
        g_ref[...] = g
        d_ref[...], mo_ref[...], vo_ref[...] = _adamw_math(w_ref[...], g, m_ref[...], v_ref[...])

    spec = pl.BlockSpec((rows, tn), lambda j: (0, j))
    return pl.pallas_call(
        body, name="ada_backward", grid=(cols // tn,), out_shape=[SDS((rows, cols), F32)] * 4,
        in_specs=[pl.BlockSpec((rows, N_DEV), lambda j: (0, 0)), pl.BlockSpec((N_DEV, tn), lambda j: (0, j)), spec, spec, spec],
        out_specs=[spec] * 4, compiler_params=_params("parallel"),
    )(act_t, dmod_cols, w, m, v)


def _sum_rows(gathered, n_loss):
    n = gathered.shape[2]

    def body(g_ref, o_ref):
        acc = g_ref[0]
        for r in range(1, N_DEV):
            acc = acc + g_ref[r]
        o_ref[...] = acc
        o_ref[:, n - n_loss:n] = jnp.broadcast_to(jnp.sum(acc[:, n - n_loss:n], axis=1, keepdims=True), (1, n_loss))

    return pl.pallas_call(body, name="sum_rows", out_shape=SDS((1, n), F32), in_specs=[VMEM_SPEC], out_specs=VMEM_SPEC)(gathered)


def _adamw_vectors(total, offsets, separate, params):
    n_p = len(params)
    apart = [g for g in separate if g is not None]

    def body(*refs):
        total_ref, apart_refs = refs[0], list(refs[1:1 + len(apart)])
        ins, outs = refs[1 + len(apart):1 + len(apart) + 3 * n_p], refs[1 + len(apart) + 3 * n_p:]
        for i in range(n_p):
            w_ref, m_ref, v_ref = ins[3 * i:3 * i + 3]
            g = apart_refs.pop(0)[...] if offsets[i] is None else total_ref[:, offsets[i]:offsets[i] + w_ref.shape[1]]
            outs[4 * i][...] = g
            outs[4 * i + 1][...], outs[4 * i + 2][...], outs[4 * i + 3][...] = _adamw_math(w_ref[...], g, m_ref[...], v_ref[...])

    flat = [a for p in params for a in p]
    res = pl.pallas_call(
        body, name="adamw_vectors", out_shape=[SDS(p[0].shape, F32) for p in params for _ in range(4)],
        in_specs=[VMEM_SPEC] * (1 + len(apart) + len(flat)), out_specs=[VMEM_SPEC] * (4 * n_p),
    )(total, *apart, *flat)
    return [tuple(res[4 * i:4 * i + 4]) for i in range(n_p)]


def _blocks_to_pieces(w):
    nb, rows, n = w.shape
    q = rows // N_CHIPS
    return w.reshape(nb, N_CHIPS, q, n).transpose(1, 0, 2, 3).reshape(N_CHIPS, 2, nb // 2, q, n)


def _pieces_to_blocks(w):
    n_chips, _, half, q, n = w.shape
    return w.reshape(n_chips, 2 * half, q, n).transpose(1, 0, 2, 3).reshape(2 * half, n_chips * q, n)


def kernel(x, c, norm_mix_g, norm_mlp_g, w_ada, b_ada, w_in, conv_w, conv_b, w_rg_a, b_rg_a, w_rg_x, b_rg_x, a_param, w_branch_a, w_pool, b_pool, pool_scale, w_branch_b, w_out, w_up, w_down, final_g, loss_target, m_norm_mix_g, m_norm_mlp_g, m_w_ada, m_b_ada, m_w_in, m_conv_w, m_conv_b, m_w_rg_a, m_b_rg_a, m_w_rg_x, m_b_rg_x, m_a_param, m_w_branch_a, m_w_pool, m_b_pool, m_pool_scale, m_w_branch_b, m_w_out, m_w_up, m_w_down, m_final_g, v_norm_mix_g, v_norm_mlp_g, v_w_ada, v_b_ada, v_w_in, v_conv_w, v_conv_b, v_w_rg_a, v_b_rg_a, v_w_rg_x, v_b_rg_x, v_a_param, v_w_branch_a, v_w_pool, v_b_pool, v_pool_scale, v_w_branch_b, v_w_out, v_w_up, v_w_down, v_final_g):
    d = D_MODEL
    s_rows = x.shape[1]
    ts, tsq = min(512, s_rows), min(256, s_rows)
    xi, yi, ci = _coords()
    k_me, me = 2 * xi + yi, 4 * xi + 2 * yi + ci
    ada_cols = w_ada.shape[2]
    conv_cols = conv_w.shape[2]
    n_in, n_up = w_in.shape[2], w_up.shape[2]

    names = ("w_in", "w_up", "w_down", "w_a", "w_b", "w_out", "rg_a", "rg_x", "pool")
    mine = dict(zip(names, (w_in[0], w_up[0], w_down[0], w_branch_a[0], w_branch_b[0], w_out[0], w_rg_a[0], w_rg_x[0], w_pool[0])))
    moments_m = dict(zip(names, (m_w_in[0], m_w_up[0], m_w_down[0], m_w_branch_a[0], m_w_branch_b[0], m_w_out[0], m_w_rg_a[0], m_w_rg_x[0], m_w_pool[0])))
    moments_v = dict(zip(names, (v_w_in[0], v_w_up[0], v_w_down[0], v_w_branch_a[0], v_w_branch_b[0], v_w_out[0], v_w_rg_a[0], v_w_rg_x[0], v_w_pool[0])))
    block_weights, squares = ("rg_a", "rg_x", "pool"), ("w_a", "w_b", "w_out")
    place = jnp.stack([ci, k_me]).astype(jnp.int32)
    g_mix, g_mlp, g_fin = norm_mix_g, norm_mlp_g, final_g.reshape(1, d)
    x0, target = x[0], loss_target[0]
    wg = functools.partial(_weight_grad, tk=min(512, d))
    plain, back, back_sum = (dict(transposed=t, reduce_blocks=r) for t, r in ((False, False), (True, False), (True, True)))
    rows_d, vec8, vec1 = _row_spec(ts, d), _vec_spec(8, d), _vec_spec(1, d)
    tw = min(1024, s_rows)
    halves_of_tile = 2

    def out_rows(cols, dtype, t=ts):
        return SDS((s_rows, cols), dtype), _row_spec(t, cols)

    sums_out = (SDS((8, d), F32), vec8)

    def halves(n):
        return mine[n].astype(BF16).reshape(2, -1, mine[n].shape[-1])

    def blocks(w):
        return _pieces_to_blocks(w.reshape(N_CHIPS, 2, N_BLOCKS // 2, BLOCK // N_CHIPS, BLOCK))

    act_all, mod_all, conv_all, w_in_whole, *gate_wholes = _prologue(
        c, conv_w[0], w_ada[0], lax.dynamic_slice_in_dim(b_ada, k_me * ada_cols, ada_cols, axis=1), halves("w_in"), [halves(n) for n in block_weights])
    conv_full = conv_all[0::2].transpose(1, 0, 2).reshape(CONV_WIDTH, d)
    mod_mine = lax.dynamic_index_in_dim(mod_all, me, axis=1, keepdims=False)[0::2]
    mod = jnp.concatenate([mod_mine.reshape(N_MOD, d), jnp.zeros((8 - N_MOD, d), F32)], axis=0)
    w_in_all = w_in_whole.reshape(N_CHIPS, d, n_in)
    vecs = jnp.concatenate([conv_b, b_rg_a, b_rg_x, a_param, b_pool, pool_scale, jnp.zeros((2, d), F32)], axis=0)

    def norm_first(a_ref, extra_refs, out_refs):
        h = _modulated_norm(a_ref[...], extra_refs[0], extra_refs[1], 1, 0)
        out_refs[1][...] = h
        return h

    stage = _Stage()
    i_blk = [_add_gather_d2d(stage, g) for g in gate_wholes]
    i_sq = [_add_gather_ici(stage, halves(n)) for n in squares]
    down_rows = mine["w_down"].shape[0] // 8
    i_down = _add_gather_ici(stage, halves("w_down"), rows=(0, down_rows))
    (proj, h1), got = _matmul(x0, w_in_all, name="proj_in", tm=ts, extra=[(mod, vec8), (g_mix, vec1)], prepare=norm_first,
                              outs=[out_rows(N_CHIPS * n_in, F32), out_rows(d, BF16)], finish=_store_blocks(n_in), parts=halves_of_tile,                              stage=stage, **plain)
    w_rg_a_all, w_rg_x_all, w_pool_all = (blocks(got[i]) for i in i_blk)
    stage = _Stage()
    i_sq = [_add_gather_d2d(stage, got[i]) for i in i_sq]
    i_up = _add_gather_ici(stage, halves("w_up"))
    w_down_so_far = got[i_down]
    (xr, hr, ga, p, pooled), got = _seq_forward(proj, conv_full, vecs, w_rg_a_all, w_rg_x_all, w_pool_all, ts=tsq, stage=stage)
    w_a_all, w_b_all, w_out_all = (got[i].reshape(1, d, d) for i in i_sq)
    stage = _Stage()
    i_up = _add_gather_d2d(stage, got[i_up])
    i_down = _add_gather_ici(stage, halves("w_down"), rows=(down_rows, 2 * down_rows), whole=w_down_so_far)
    (merged, s_a, s_b, t_a, t_b), got = _branches(ga, pooled, proj, w_a_all, w_b_all, ts=ts, stage=stage)
    w_up_all = got[i_up].reshape(N_CHIPS, d, n_up)
    stage = _Stage()
    i_down = _add_gather_ici(stage, halves("w_down"), rows=(3 * down_rows, down_rows), whole=got[i_down])

    def residual_norm(k, acc, extra_refs, out_refs):
        x_ref, mod_ref, g_ref = extra_refs
        out_refs[0][...] = acc.astype(BF16)
        x2_t = x_ref[...] + mod_ref[2:3, :] * acc
        out_refs[1][...] = x2_t
        out_refs[2][...] = _modulated_norm(x2_t, mod_ref, g_ref, 4, 3)

    (mo, x2, h2), got = _matmul(merged, w_out_all, name="mix_out", tm=ts, extra=[(x0, rows_d), (mod, vec8), (g_mlp, vec1)],
                                outs=[out_rows(d, BF16), out_rows(d, F32), out_rows(d, BF16)], finish=residual_norm, parts=halves_of_tile,                                stage=stage, transposed=False, reduce_blocks=True)
    stage = _Stage()
    i_down = _add_gather_d2d(stage, got[i_down])

    def relu_squared(k, acc, extra_refs, out_refs):
        out_refs[0][:, k * n_up:(k + 1) * n_up] = jnp.square(jnp.maximum(acc, 0.0)).astype(BF16)

    (ff,), got = _matmul(h2, w_up_all, name="mlp_up", tm=ts, outs=[out_rows(D_FF, BF16)], finish=relu_squared, parts=halves_of_tile,
                         stage=stage, **plain)
    w_down_all = got[i_down].reshape(N_CHIPS, D_FF // N_CHIPS, d)

    def loss_head(k, acc, extra_refs, out_refs):
        x2_ref, mod_ref, g_ref, t_ref = extra_refs
        dx_ref, df_ref, acc_ref = out_refs
        xh, r = _rms(x2_ref[...] + mod_ref[5:6, :] * acc)
        err = xh * g_ref[...] - t_ref[...]
        dy = err / d
        dxh = dy * g_ref[...]
        dx3_t = r * (dxh - xh * jnp.mean(dxh * xh, axis=-1, keepdims=True))
        dx_ref[...] = dx3_t
        df_ref[...] = (dx3_t * mod_ref[5:6, :]).astype(BF16)
        _accumulate(acc_ref, 0, dy * xh)
        _accumulate(acc_ref, 1, dx3_t * acc)
        _accumulate(acc_ref, 2, jnp.square(err) * (0.5 / d))

    dx3, dffo, sums_head = _matmul(ff, w_down_all, name="mlp_down", tm=ts, extra=[(x2, rows_d), (mod, vec8), (g_fin, vec1), (target, rows_d)],
                                   outs=[out_rows(d, F32), out_rows(d, BF16), sums_out], finish=loss_head, zero_first=(2,), parts=halves_of_tile,                                   transposed=False, reduce_blocks=True)

    partial_of, from_sibling, chip_sum, from_chips, half_done, quarter = {}, {}, {}, {}, {}, {}

    def publish(n, g):
        g = _blocks_to_pieces(g).astype(BF16) if n in block_weights else g
        partial_of[n] = g.reshape(N_CHIPS, 2, -1, mine[n].shape[-1])

    def exchange(to_sibling=(), to_chips=(), swap=()):
        st = _Stage()
        slots = [(n, from_sibling, _add_reduce_d2d(st, partial_of[n])) for n in to_sibling]
        slots += [(n, from_chips, _add_reduce_ici(st, chip_sum[n])) for n in to_chips]
        slots += [(n, quarter, _add_swap_halves(st, half_done[n])) for n in swap]
        return st, slots

    def collect(slots, outs):
        for n, where, i in slots:
            where[n] = outs[i]

    def sum_pairs(*ns):
        for n in ns:
            chip_sum[n] = _sum_pair(partial_of[n], from_sibling[n], place, "sum_pair_" + n)

    def sum_quarters(*ns):
        for n in ns:
            half_done[n] = _sum_quarter(partial_of[n], from_sibling[n], from_chips[n], place, "sum_quarter_" + n)

    publish("w_down", wg(ff, dffo, nb=1, name="grad_w_down"))

    def relu_squared_backward(k, acc, extra_refs, out_refs):
        cols = slice(k * n_up, (k + 1) * n_up)
        out_refs[0][:, cols] = (acc * (2.0 * jnp.sqrt(extra_refs[0][:, cols].astype(F32)))).astype(BF16)

    stage, slots = exchange(to_sibling=["w_down"])
    (dup,), got = _matmul(dffo, w_down_all, name="d_mlp_down", tm=ts, extra=[(ff, _row_spec(ts, D_FF))], outs=[out_rows(D_FF, BF16)],
                          finish=relu_squared_backward, parts=halves_of_tile, stage=stage, **back)
    collect(slots, got)
    sum_pairs("w_down")

    def norm_mlp_backward(k, acc, extra_refs, out_refs):
        x2_ref, dres_ref, mod_ref, g_ref, mo_ref = extra_refs
        dx_ref, dmo_ref, acc_ref = out_refs
        dx =_modulated_norm_backward(acc, x2_ref[...], dres_ref[...], mod_ref, g_ref, acc_ref, 4)
        dx_ref[...] = dx
        dmo_ref[...] = (dx * mod_ref[2:3, :]).astype(BF16)
        _accumulate(acc_ref, 3, dx * mo_ref[...].astype(F32))

    stage, slots = exchange(to_chips=["w_down"])
    (dx2, dmo, sums_mlp), got = _matmul(dup, w_up_all, name="d_mlp_up", tm=ts, stage=stage, finish=norm_mlp_backward,
                                        extra=[(x2, rows_d), (dx3, rows_d), (mod, vec8), (g_mlp, vec1), (mo, rows_d)],
                                        outs=[out_rows(d, F32), out_rows(d, BF16), sums_out], zero_first=(2,), parts=halves_of_tile, **back_sum)
    collect(slots, got)
    sum_quarters("w_down")
    publish("w_up", wg(h2, dup, nb=N_CHIPS, name="grad_w_up"))
    publish("w_out", wg(merged, dmo, nb=1, name="grad_w_out"))

    def merge_backward(k, acc, extra_refs, out_refs):
        sa_ref, sb_ref, ta_ref, tb_ref = extra_refs
        out_refs[0][...] = (acc * sa_ref[...].astype(F32)).astype(BF16)
        out_refs[1][...] = (acc * sb_ref[...].astype(F32)).astype(BF16)
        out_refs[2][:, 0:d] = (acc * ta_ref[...].astype(F32)).astype(BF16)
        out_refs[2][:, d:2 * d] = (acc * tb_ref[...].astype(F32)).astype(BF16)

    stage, slots = exchange(to_sibling=["w_up", "w_out"], swap=["w_down"])
    (dbr_a, dbr_b, dgab), got = _matmul(dmo, w_out_all, name="d_mix_out", tm=ts, stage=stage, finish=merge_backward,
                                        extra=[(s_a, rows_d), (s_b, rows_d), (t_a, rows_d), (t_b, rows_d)],
                                        outs=[out_rows(d, BF16), out_rows(d, BF16), out_rows(2 * d, BF16)], parts=halves_of_tile, **back_sum)
    collect(slots, got)
    publish("w_a", wg(ga, dbr_a, nb=1, name="grad_w_branch_a"))
    publish("w_b", wg(pooled, dbr_b, nb=1, name="grad_w_branch_b"))
    dga, = _matmul(dbr_a, w_a_all, name="d_branch_a", tm=tw, outs=[out_rows(d, BF16, tw)], finish=_store_blocks(d), **back_sum)
    dpooled, = _matmul(dbr_b, w_b_all, name="d_branch_b", tm=tw, outs=[out_rows(d, BF16, tw)], finish=_store_blocks(d), **back_sum)
    sum_pairs("w_up", "w_out")
    stage, slots = exchange(to_sibling=["w_a", "w_b"], to_chips=["w_up", "w_out"])
    (dproj, sums_seq, d_rg_a, d_rg_x, d_pool), got = _seq_backward_by_block(proj, xr, hr, p, dga, dpooled, dgab, conv_full, vecs,
                                                                            w_rg_a_all, w_rg_x_all, w_pool_all, ts=tsq, stage=stage)
    collect(slots, got)
    sum_quarters("w_up", "w_out")
    sum_pairs("w_a", "w_b")
    for n, g in zip(block_weights, (d_rg_a, d_rg_x, d_pool)):
        publish(n, g)
    stage, slots = exchange(to_sibling=block_weights, to_chips=["w_a", "w_b"], swap=["w_up", "w_out"])
    g_in, got = wg(h1, dproj, nb=N_CHIPS, name="grad_w_in", stage=stage)
    collect(slots, got)
    publish("w_in", g_in)
    sum_pairs(*block_weights)
    sum_quarters("w_a", "w_b")
    stage, slots = exchange(to_sibling=["w_in"], to_chips=block_weights)
    collect(slots, _run_stage(stage, "exchange_w_in_gates"))
    sum_pairs("w_in")
    sum_quarters(*block_weights)

    def norm_mix_backward(k, acc, extra_refs, out_refs):
        x_ref, dres_ref, mod_ref, g_ref = extra_refs
        out_refs[0][...] = _modulated_norm_backward(acc, x_ref[...], dres_ref[...], mod_ref, g_ref, out_refs[1], 1)

    stage, slots = exchange(to_chips=["w_in"], swap=block_weights + ("w_a", "w_b"))
    (grad_x, sums_mix), got = _matmul(dproj, w_in_all, name="d_proj_in", tm=ts, stage=stage, finish=norm_mix_backward,
                                      extra=[(x0, rows_d), (dx2, rows_d), (mod, vec8), (g_mix, vec1)],
                                      outs=[out_rows(d, F32), sums_out], zero_first=(1,), parts=halves_of_tile, **back_sum)
    collect(slots, got)
    sum_quarters("w_in")

    dmod = jnp.concatenate([sums_mix[1:2], sums_mix[0:1], sums_mlp[3:4], sums_mlp[1:2], sums_mlp[0:1], sums_head[1:2]], axis=1)
    row = jnp.concatenate([sums_mix[2:3], sums_mlp[2:3], sums_seq[4:5], sums_seq[5:6], sums_seq[6:7], sums_seq[7:8], sums_seq[8:9],
                           sums_seq[9:10], sums_head[0:1], sums_seq[0:CONV_WIDTH].reshape(1, CONV_WIDTH * d), dmod, sums_head[2:3]], axis=1)
    n_vec, n_conv, n_mod = 9 * d, CONV_WIDTH * d, N_MOD * d
    stage, slots = exchange(swap=["w_in"])
    i_rows = _add_allgather8(stage, row)
    got = _run_stage(stage, "exchange_sums_w_in")
    collect(slots, got)
    rows_all = got[i_rows]
    total = _sum_rows(rows_all, d)
    loss = total[0, n_vec + n_conv + n_mod]
    dmod_all = rows_all[:, 0, n_vec + n_conv:n_vec + n_conv + n_mod]
    g_ada, d_ada, m_ada, v_ada = _ada_backward(act_all.T, lax.dynamic_slice_in_dim(dmod_all, k_me * ada_cols, ada_cols, axis=1),
                                               w_ada[0], m_w_ada[0], v_w_ada[0])
    g_conv = lax.dynamic_slice_in_dim(total[:, n_vec:n_vec + n_conv].reshape(CONV_WIDTH, d), k_me * conv_cols, conv_cols, axis=1)
    vec_names = ("norm_mix_g", "norm_mlp_g", "conv_b", "b_rg_a", "b_rg_x", "a_param", "b_pool", "pool_scale", "final_g", "conv_w", "b_ada")
    vec_params = [(norm_mix_g, m_norm_mix_g, v_norm_mix_g), (norm_mlp_g, m_norm_mlp_g, v_norm_mlp_g), (conv_b, m_conv_b, v_conv_b),
                  (b_rg_a, m_b_rg_a, v_b_rg_a), (b_rg_x, m_b_rg_x, v_b_rg_x), (a_param, m_a_param, v_a_param), (b_pool, m_b_pool, v_b_pool),
                  (pool_scale, m_pool_scale, v_pool_scale), tuple(a.reshape(1, d) for a in (final_g, m_final_g, v_final_g)),
                  (conv_w[0], m_conv_w[0], v_conv_w[0]), (b_ada, m_b_ada, v_b_ada)]
    offsets = [i * d for i in range(9)] + [None, n_vec + n_conv]
    vec_res = _adamw_vectors(total, offsets, [None] * 9 + [g_conv, None], vec_params)
    vec_out = {n: [r.reshape(shape) for r in res] for n, res, shape in zip(
        vec_names, vec_res, [(1, d)] * 8 + [(d,), (1, CONV_WIDTH, conv_cols), (1, N_MOD * d)])}

    big_out = {}
    for n in names:
        shape2 = (-1, mine[n].shape[-1])
        g2 = quarter[n].reshape(shape2)
        early = n in ("w_down", "w_up") + squares
        res = (_adamw_on_sparsecores if early else _adamw)(
            mine[n].reshape(shape2), g2, moments_m[n].reshape(shape2), moments_v[n].reshape(shape2), "adamw_" + n)
        big_out[n] = [r.reshape((1,) + mine[n].shape) for r in res]

    key = {"w_in": "w_in", "w_rg_a": "rg_a", "w_rg_x": "rg_x", "w_branch_a": "w_a", "w_pool": "pool", "w_branch_b": "w_b", "w_out": "w_out",
           "w_up": "w_up", "w_down": "w_down"}
    order = ("norm_mix_g", "norm_mlp_g", "w_ada", "b_ada", "w_in", "conv_w", "conv_b", "w_rg_a", "b_rg_a", "w_rg_x", "b_rg_x", "a_param",
             "w_branch_a", "w_pool", "b_pool", "pool_scale", "w_branch_b", "w_out", "w_up", "w_down", "final_g")
    ada_out = [g_ada[None], d_ada[None], m_ada[None], v_ada[None]]
    outs = [loss, grad_x[None]]
    for kind in range(4):
        for nme in order:
            outs.append(ada_out[kind] if nme == "w_ada" else big_out[key[nme]][kind] if nme in key else vec_out[nme][kind])
    return tuple(outs)
```

```python
import functools

import jax
import jax.numpy as jnp
from jax import lax
from jax.experimental import pallas as pl
from jax.experimental.pallas import tpu as pltpu
from jax.experimental.pallas import tpu_sc as plsc

F32, BF16 = jnp.float32, jnp.bfloat16
D_MODEL = 1024
D_FF = 4 * D_MODEL
N_BLOCKS = 4
BLOCK = D_MODEL // N_BLOCKS
CONV_WIDTH = 4
POOL_WINDOWS = (2, 4, 8, 16)
CONV_HALO = 8
POOL_HALO = 16
N_MOD = 6
EPS = 1e-6
C_RG = 8.0
ADAM_LR, ADAM_B1, ADAM_B2, ADAM_EPS, ADAM_WD, ADAM_STEP = 0.001, 0.9, 0.999, 1e-08, 0.01, 10
N_DEV = 8
N_CHIPS = 4
SC_TILES, SC_LANES, SC_SLAB = 32, 16, 8
VMEM_LIMIT_V7X = 56 * 2**20
MESH = pl.DeviceIdType.MESH
SDS = jax.ShapeDtypeStruct
HBM_SPEC = pl.BlockSpec(memory_space=pltpu.HBM)
VMEM_SPEC = pl.BlockSpec(memory_space=pltpu.VMEM)


def _params(*semantics):
    return pltpu.CompilerParams(dimension_semantics=semantics, vmem_limit_bytes=VMEM_LIMIT_V7X)


def _coords():
    return lax.axis_index("x"), lax.axis_index("y"), lax.axis_index("c")


def _flip(v, bit):
    return 1 - v if bit else v


def _chips():
    x, y, _ = _coords()
    return [(1 - x, y), (x, 1 - y), (1 - x, 1 - y)]


PEER_SETS = [frozenset(s) for s in (("sibling",), ("chips",), ("sibling", "chips"), ("sibling", "chips", "others"))]


class _Stage:
    def __init__(self):
        self.inputs, self.in_specs, self.outputs, self.out_specs, self.aliases = [], [], [], [], {}
        self.parts, self.n_copies, self.peers = [], 0, set()

    def add(self, inputs, in_spec, outputs, out_spec, n_copies, build, peers, alias=False):
        self.peers |= set(peers)
        i0, o0 = len(self.inputs), len(self.outputs)
        self.inputs += inputs
        self.in_specs += [in_spec] * len(inputs)
        self.outputs += outputs
        self.out_specs += [out_spec] * len(outputs)
        if alias:
            self.aliases.update({i0 + i: o0 + i for i in range(len(inputs))})
        self.parts.append((build, i0, len(inputs), o0, len(outputs)))
        self.n_copies += n_copies
        return list(range(o0, o0 + len(outputs)))

    def copies(self, in_refs, out_refs):
        out = []
        for build, i0, ni, o0, no in self.parts:
            out += build(in_refs[i0:i0 + ni], out_refs[o0:o0 + no])
        assert len(out) == self.n_copies
        return out

    def run(self, in_refs, out_refs, send_sems, recv_sems, start):
        x, y, c = _coords()
        for s, (src, dst, to, landing) in enumerate(self.copies(in_refs, out_refs)):
            if to is None:
                cp = pltpu.make_async_copy(src, dst, send_sems.at[s])
                cp.start() if start else cp.wait()
                continue
            cp = pltpu.make_async_remote_copy(src_ref=src, dst_ref=dst, send_sem=send_sems.at[s], recv_sem=recv_sems.at[s],
                                              device_id=to, device_id_type=MESH)
            if start:
                cp.start()
            else:
                pltpu.make_async_remote_copy(src_ref=landing, dst_ref=landing, send_sem=send_sems.at[s], recv_sem=recv_sems.at[s],
                                             device_id=(x, y, c), device_id_type=MESH).wait_recv()
                cp.wait_send()


def _hosted(stage, body, *, name, in_specs, out_specs, out_shape, grid=(), scratch_shapes=(), compiler_params=None):
    if stage is None:
        return pl.pallas_call(body, name=name, grid=grid, in_specs=in_specs, out_specs=out_specs, out_shape=out_shape,
                              scratch_shapes=list(scratch_shapes), compiler_params=compiler_params)
    single = not isinstance(out_shape, (list, tuple))
    h_out_shape = [out_shape] if single else list(out_shape)
    h_out_specs = [out_specs] if single else list(out_specs)
    n_in, n_out, n_scr = len(in_specs), len(h_out_shape), len(scratch_shapes)
    s_in, s_out = len(stage.inputs), len(stage.outputs)

    def wrapped(*refs):
        h_in, st_in = refs[:n_in], refs[n_in:n_in + s_in]
        h_o, st_o = refs[n_in + s_in:n_in + s_in + n_out], refs[n_in + s_in + n_out:n_in + s_in + n_out + s_out]
        h_scr = refs[n_in + s_in + n_out + s_out:n_in + s_in + n_out + s_out + n_scr]
        send_sems, recv_sems = refs[n_in + s_in + n_out + s_out + n_scr:]
        def begin():
            x, y, c = _coords()
            peers = [(x, y, 1 - c)] * ("sibling" in stage.peers)
            peers += [(px, py, c) for px, py in _chips()] * ("chips" in stage.peers)
            peers += [(px, py, 1 - c) for px, py in _chips()] * ("others" in stage.peers)
            barrier = pltpu.get_barrier_semaphore()
            for peer in peers:
                pl.semaphore_signal(barrier, inc=1, device_id=peer, device_id_type=MESH)
            pl.semaphore_wait(barrier, len(peers))
            stage.run(st_in, st_o, send_sems, recv_sems, True)

        if not grid:
            begin()
            if body is not None:
                body(*h_in, *h_o, *h_scr)
            stage.run(st_in, st_o, send_sems, recv_sems, False)
            return
        ids = [pl.program_id(a) for a in range(len(grid))]
        first = functools.reduce(jnp.logical_and, [i == 0 for i in ids])
        last = functools.reduce(jnp.logical_and, [i == g - 1 for i, g in zip(ids, grid)])
        pl.when(first)(begin)
        body(*h_in, *h_o, *h_scr)
        pl.when(last)(lambda: stage.run(st_in, st_o, send_sems, recv_sems, False))

    call = pl.pallas_call(
        wrapped, name=name, grid=grid, in_specs=list(in_specs) + stage.in_specs, out_specs=h_out_specs + stage.out_specs,
        out_shape=h_out_shape + stage.outputs, input_output_aliases={n_in + i: n_out + o for i, o in stage.aliases.items()},
        scratch_shapes=list(scratch_shapes) + [pltpu.SemaphoreType.DMA((stage.n_copies,)), pltpu.SemaphoreType.DMA((stage.n_copies,))],
        compiler_params=pltpu.CompilerParams(dimension_semantics=("arbitrary",) * len(grid), vmem_limit_bytes=VMEM_LIMIT_V7X,
                                             collective_id=PEER_SETS.index(frozenset(stage.peers))),
    )

    def run(*operands):
        outs = call(*operands, *stage.inputs)
        host = outs[:n_out]
        return (host[0] if single else host), outs[n_out:]

    return run


def _run_stage(stage, name):
    return _hosted(stage, None, name=name, in_specs=[], out_specs=[], out_shape=[])()[1]


def _add_allgather8(stage, v):
    def build(ins, outs):
        x, y, c = _coords()
        me = 4 * x + 2 * y + c
        copies = [(ins[0], outs[0].at[me], None, None)]
        for r in range(1, N_DEV):
            peer = (_flip(x, r & 4), _flip(y, r & 2), _flip(c, r & 1))
            copies.append((ins[0], outs[0].at[me], peer, outs[0].at[me ^ r]))
        return copies

    return stage.add([v], VMEM_SPEC, [SDS((N_DEV,) + v.shape, v.dtype)], VMEM_SPEC, N_DEV, build, ("sibling", "chips", "others"))[0]


def _add_gather_ici(stage, shard, rows=None, whole=None):
    first, count = rows or (0, shard.shape[1])

    def build(ins, outs):
        x, y, c = _coords()
        k_me = 2 * x + y
        part = pl.ds(first, count)
        copies = [] if whole is not None else [(ins[0], outs[0].at[k_me], (x, y, 1 - c), outs[0].at[k_me])]
        for px, py in _chips():
            copies.append((ins[0].at[c, part], outs[0].at[k_me, c, part], (px, py, c), outs[0].at[2 * px + py, c, part]))
        return copies

    if whole is None:
        return stage.add([shard], HBM_SPEC, [SDS((N_CHIPS,) + shard.shape, shard.dtype)], HBM_SPEC, N_CHIPS, build, ("sibling", "chips"))[0]

    def build_into(ins, outs):
        return build(ins[1:], outs)

    i0 = len(stage.inputs)
    out = stage.add([whole, shard], HBM_SPEC, [SDS(whole.shape, whole.dtype)], HBM_SPEC, N_CHIPS - 1, build_into, ("chips",))[0]
    stage.aliases[i0] = out
    return out


def _add_gather_d2d(stage, whole):
    def build(ins, outs):
        x, y, c = _coords()
        return [(outs[0].at[2 * px + py, c], outs[0].at[2 * px + py, c], (x, y, 1 - c), outs[0].at[2 * px + py, 1 - c]) for px, py in _chips()]

    return stage.add([whole], HBM_SPEC, [SDS(whole.shape, whole.dtype)], HBM_SPEC, N_CHIPS - 1, build, ("sibling",), alias=True)[0]


def _add_reduce_d2d(stage, grads):
    def build(ins, outs):
        x, y, c = _coords()
        return [(ins[0].at[k, 1 - c], outs[0].at[k], (x, y, 1 - c), outs[0].at[k]) for k in range(N_CHIPS)]

    return stage.add([grads], HBM_SPEC, [SDS((N_CHIPS,) + grads.shape[2:], grads.dtype)], HBM_SPEC, N_CHIPS, build, ("sibling",))[0]


def _add_reduce_ici(stage, partial):
    def build(ins, outs):
        x, y, c = _coords()
        return [(ins[0].at[2 * px + py], outs[0].at[j], (px, py, c), outs[0].at[j]) for j, (px, py) in enumerate(_chips())]

    return stage.add([partial], HBM_SPEC, [SDS((N_CHIPS - 1,) + partial.shape[1:], partial.dtype)], HBM_SPEC, N_CHIPS - 1, build, ("chips",))[0]


def _add_swap_halves(stage, quarter):
    def build(ins, outs):
        x, y, c = _coords()
        return [(outs[0].at[c], outs[0].at[c], (x, y, 1 - c), outs[0].at[1 - c])]

    return stage.add([quarter], HBM_SPEC, [SDS(quarter.shape, quarter.dtype)], HBM_SPEC, 1, build, ("sibling",), alias=True)[0]


def _sum_pair(grads, from_sibling, place, name):
    _, _, rows, cols = grads.shape
    tr = min(rows, 512)

    def body(place_ref, g_ref, s_ref, o_ref):
        o_ref[0] = (g_ref[0, 0].astype(F32) + s_ref[0].astype(F32)).astype(BF16)

    spec = pl.BlockSpec((1, tr, cols), lambda k, i, place_ref: (k, i, 0))
    return pl.pallas_call(
        body, name=name, out_shape=SDS((N_CHIPS, rows, cols), BF16),
        grid_spec=pltpu.PrefetchScalarGridSpec(
            num_scalar_prefetch=1, grid=(N_CHIPS, rows // tr),
            in_specs=[pl.BlockSpec((1, 1, tr, cols), lambda k, i, place_ref: (k, place_ref[0], i, 0)), spec], out_specs=spec),
        compiler_params=_params("parallel", "parallel"),
    )(place, grads, from_sibling)


def _sum_quarter(grads, from_sibling, from_chips, place, name):
    _, _, rows, cols = grads.shape
    tr = min(rows, 256)

    def body(place_ref, g_ref, s_ref, c_ref, o_ref):
        acc = g_ref[0, 0].astype(F32) + s_ref[0].astype(F32)
        for j in range(N_CHIPS - 1):
            acc = acc + c_ref[j].astype(F32)
        o_ref[0] = acc

    return pl.pallas_call(
        body, name=name, out_shape=SDS((2, rows, cols), F32),
        grid_spec=pltpu.PrefetchScalarGridSpec(
            num_scalar_prefetch=1, grid=(rows // tr,),
            in_specs=[pl.BlockSpec((1, 1, tr, cols), lambda i, place_ref: (place_ref[1], place_ref[0], i, 0)),
                      pl.BlockSpec((1, tr, cols), lambda i, place_ref: (place_ref[1], i, 0)),
                      pl.BlockSpec((N_CHIPS - 1, tr, cols), lambda i, place_ref: (0, i, 0))],
            out_specs=pl.BlockSpec((1, tr, cols), lambda i, place_ref: (place_ref[0], i, 0))),
        compiler_params=_params("parallel"),
    )(place, grads, from_sibling, from_chips)


def _row_spec(ts, cols, col_block=0):
    return pl.BlockSpec((ts, cols), lambda *g: (g[0], col_block))


def _vec_spec(rows, cols):
    return pl.BlockSpec((rows, cols), lambda *g: (0, 0))


def _matmul(a, w, *, transposed, reduce_blocks, name, tm, outs, finish, extra=(), prepare=None, zero_first=(), parts=1, stage=None):
    m_rows = a.shape[0]
    nb, r, c = w.shape
    kb = c if transposed else r
    dims = (((1,), (1,)), ((), ())) if transposed else (((1,), (0,)), ((), ()))
    assert a.shape[1] == (nb * kb if reduce_blocks else kb) and m_rows % tm == 0 and tm % parts == 0
    n_extra, sub = len(extra), tm // parts

    def body(a_ref, w_ref, *rest):
        extra_refs, out_refs = rest[:n_extra], rest[n_extra:]
        if zero_first:
            @pl.when(pl.program_id(0) == 0)
            def _():
                for i in zero_first:
                    out_refs[i][...] = jnp.zeros_like(out_refs[i])

        def views(j):
            rows = pl.ds(j * sub, sub)
            return rows, [ref.at[rows] if ref.shape[0] == tm else ref for ref in extra_refs], [ref.at[rows] if ref.shape[0] == tm else ref for ref in out_refs]

        def product(j):
            rows, ex, ou = views(j)
            if reduce_blocks:
                acc = lax.dot_general(a_ref[rows, 0:kb], w_ref[0], dims, preferred_element_type=F32)
                for k in range(1, nb):
                    acc += lax.dot_general(a_ref[rows, k * kb:(k + 1) * kb], w_ref[k], dims, preferred_element_type=F32)
                return [(None, acc)]
            lhs = a_ref[rows, :] if prepare is None else prepare(a_ref.at[rows], ex, ou)
            return [(k, lax.dot_general(lhs, w_ref[k], dims, preferred_element_type=F32)) for k in range(nb)]

        pending = product(0)
        for j in range(parts):
            ahead = product(j + 1) if j + 1 < parts else None
            _, ex, ou = views(j)
            for k, acc in pending:
                finish(k, acc, ex, ou)
            pending = ahead

    return _hosted(
        stage, body, name=name, grid=(m_rows // tm,), out_shape=[s for s, _ in outs], out_specs=[s for _, s in outs],
        in_specs=[_row_spec(tm, a.shape[1]), pl.BlockSpec((nb, r, c), lambda *g: (0, 0, 0), pipeline_mode=pl.Buffered(1))] + [s for _, s in extra],
        compiler_params=_params("arbitrary"),
    )(a, w, *[e for e, _ in extra])


def _store_blocks(n):
    def finish(k, acc, extra_refs, out_refs):
        if k is None:
            out_refs[0][...] = acc.astype(out_refs[0].dtype)
        else:
            out_refs[0][:, k * n:(k + 1) * n] = acc.astype(out_refs[0].dtype)
    return finish


def _weight_grad(a, b, *, nb, name, tk, stage=None):
    s_rows, k1 = a.shape
    tn = b.shape[1] // nb
    assert k1 % tk == 0 and b.shape[1] % nb == 0

    def body(a_ref, b_ref, o_ref):
        o_ref[0] = lax.dot_general(a_ref[...], b_ref[...], (((0,), (0,)), ((), ())), preferred_element_type=F32).astype(o_ref.dtype)

    return _hosted(
        stage, body, name=name, grid=(nb, k1 // tk), out_shape=SDS((nb, k1, tn), BF16),
        in_specs=[pl.BlockSpec((s_rows, tk), lambda n, i: (0, i)), pl.BlockSpec((s_rows, tn), lambda n, i: (0, n))],
        out_specs=pl.BlockSpec((1, tk, tn), lambda n, i: (n, i, 0)), compiler_params=_params("parallel", "parallel"),
    )(a, b)


def _rms(x):
    r = lax.rsqrt(jnp.mean(x * x, axis=-1, keepdims=True) + EPS)
    return x * r, r


def _accumulate(acc_ref, row, value):
    acc_ref[row:row + 1, :] += jnp.sum(value, axis=0, keepdims=True)


def _gelu_parts(y):
    k0, k1 = 0.7978845608028654, 0.044715
    th = jnp.tanh(k0 * (y + k1 * (y * y * y)))
    gelu = 0.5 * y * (1.0 + th)
    dgelu = 0.5 * (1.0 + th) + 0.5 * y * (1.0 - th * th) * (k0 * (1.0 + 3.0 * k1 * (y * y)))
    return gelu, dgelu


def _sigmoid(z):
    return 0.5 * jnp.tanh(0.5 * z) + 0.5


def _one_minus_square(a, log_a):
    return -jnp.tanh(log_a) * (1.0 + a * a)


def _modulated_norm(x, mod_ref, g_ref, sc_row, sh_row):
    xh, _ = _rms(x)
    return ((xh * g_ref[...]) * (1.0 + mod_ref[sc_row:sc_row + 1, :]) + mod_ref[sh_row:sh_row + 1, :]).astype(BF16)


def _modulated_norm_backward(dh, x, dres, mod_ref, g_ref, acc_ref, sc_row):
    xh, r = _rms(x)
    _accumulate(acc_ref, 0, dh * (xh * g_ref[...]))
    _accumulate(acc_ref, 1, dh)
    dn = dh * (1.0 + mod_ref[sc_row:sc_row + 1, :])
    _accumulate(acc_ref, 2, dn * xh)
    dxh = dn * g_ref[...]
    return dres + r * (dxh - xh * jnp.mean(dxh * xh, axis=-1, keepdims=True))


def _seq_forward(proj, conv_w, vecs, w_a, w_x, w_p, *, ts, stage=None):
    s_rows = proj.shape[0]
    d = D_MODEL

    def body(x_ref, xh_ref, y_ref, u_ref, uh_ref, cw_ref, vec_ref, wa_ref, wx_ref, wp_ref,
             xr_o, hr_o, ga_o, p_o, pooled_o, carry, a_scr, b_scr):
        t = pl.program_id(0)

        @pl.when(t == 0)
        def _():
            carry[...] = jnp.zeros_like(carry)

        rows = lax.broadcasted_iota(jnp.int32, (ts, BLOCK), 0)
        start = (rows == 0) & (t == 0)
        for g in range(N_BLOCKS):
            lanes = slice(g * BLOCK, (g + 1) * BLOCK)
            x = x_ref[:, lanes]
            xx = jnp.concatenate([jnp.where(t == 0, 0.0, xh_ref[:, lanes]), x], axis=0)
            xr = vec_ref[0:1, lanes] + x * cw_ref[CONV_WIDTH - 1:CONV_WIDTH, lanes]
            for j in range(CONV_WIDTH - 1):
                xr = xr + pltpu.roll(xx, CONV_WIDTH - 1 - j, 0)[CONV_HALO:] * cw_ref[j:j + 1, lanes]
            xr_o[:, lanes] = xr
            xrb = xr.astype(BF16)
            r = _sigmoid(jnp.dot(xrb, wa_ref[g], preferred_element_type=F32) + vec_ref[1:2, lanes])
            gate_i = _sigmoid(jnp.dot(xrb, wx_ref[g], preferred_element_type=F32) + vec_ref[2:3, lanes])
            log_a = (-C_RG * r) * jax.nn.softplus(vec_ref[3:4, lanes])
            a = jnp.exp(log_a)
            a_scr[:, lanes] = a
            b_scr[:, lanes] = xr * gate_i * jnp.where(start, 1.0, jnp.sqrt(_one_minus_square(a, log_a)))

        sub = lax.broadcasted_iota(jnp.int32, (8, d), 0)

        def chunk(i, h):
            a = a_scr[pl.ds(i * 8, 8), :]
            b = b_scr[pl.ds(i * 8, 8), :]
            for s in (1, 2, 4):
                keep = sub >= s
                b = jnp.where(keep, a * pltpu.roll(b, s, 0) + b, b)
                a = jnp.where(keep, a * pltpu.roll(a, s, 0), a)
            hh = a * h + b
            hr_o[pl.ds(i * 8, 8), :] = hh
            return hh[7:8, :]

        carry[...] = lax.fori_loop(0, ts // 8, chunk, carry[...])
        pos = (rows + t * ts + 1).astype(F32)
        for g, win in enumerate(POOL_WINDOWS):
            lanes = slice(g * BLOCK, (g + 1) * BLOCK)
            gelu, _ = _gelu_parts(y_ref[:, lanes])
            ga_o[:, lanes] = (gelu * hr_o[:, lanes]).astype(BF16)
            u = u_ref[:, lanes]
            sums, have = jnp.concatenate([jnp.where(t == 0, 0.0, uh_ref[:, lanes]), u], axis=0), 1
            while have < win:
                sums = sums + pltpu.roll(sums, have, 0)
                have *= 2
            pb = (sums[POOL_HALO:] * (1.0 / jnp.minimum(pos, float(win))) - u).astype(BF16)
            p_o[:, lanes] = pb
            mixed = jnp.dot(pb, wp_ref[g], preferred_element_type=F32) + vec_ref[4:5, lanes]
            pooled_o[:, lanes] = (mixed * vec_ref[5:6, lanes]).astype(BF16)

    halo_x = pl.BlockSpec((CONV_HALO, d), lambda t: (jnp.maximum(t * (ts // CONV_HALO) - 1, 0), 0))
    halo_u = pl.BlockSpec((POOL_HALO, d), lambda t: (jnp.maximum(t * (ts // POOL_HALO) - 1, 0), 2))
    wspec = pl.BlockSpec((N_BLOCKS, BLOCK, BLOCK), lambda t: (0, 0, 0))
    return _hosted(
        stage, body, name="seq_forward", grid=(s_rows // ts,),
        out_shape=[SDS((s_rows, d), F32), SDS((s_rows, d), F32), SDS((s_rows, d), BF16), SDS((s_rows, d), BF16), SDS((s_rows, d), BF16)],
        in_specs=[_row_spec(ts, d, 0), halo_x, _row_spec(ts, d, 1), _row_spec(ts, d, 2), halo_u, _vec_spec(CONV_WIDTH, d), _vec_spec(8, d),
                  wspec, wspec, wspec],
        out_specs=[_row_spec(ts, d)] * 5,
        scratch_shapes=[pltpu.VMEM((1, d), F32), pltpu.VMEM((ts, d), F32), pltpu.VMEM((ts, d), F32)],
        compiler_params=_params("arbitrary"),
    )(proj, proj, proj, proj, proj, conv_w, vecs, w_a, w_x, w_p)


def _branches(ga, pooled, proj, w_a, w_b, *, ts, stage=None):
    s_rows, d = ga.shape

    def body(a_ref, p_ref, ga_ref, gb_ref, wa_ref, wb_ref, merged_o, sa_o, sb_o, ta_o, tb_o):
        bra = jnp.dot(a_ref[...], wa_ref[0], preferred_element_type=F32)
        brb = jnp.dot(p_ref[...], wb_ref[0], preferred_element_type=F32)
        sa, sb = _sigmoid(ga_ref[...]), _sigmoid(gb_ref[...])
        merged_o[...] = (sa * bra + sb * brb).astype(BF16)
        sa_o[...] = sa.astype(BF16)
        sb_o[...] = sb.astype(BF16)
        ta_o[...] = (bra * (sa * (1.0 - sa))).astype(BF16)
        tb_o[...] = (brb * (sb * (1.0 - sb))).astype(BF16)

    rows = _row_spec(ts, d)
    wspec = pl.BlockSpec((1, d, d), lambda *g: (0, 0, 0), pipeline_mode=pl.Buffered(1))
    return _hosted(
        stage, body, name="branches", grid=(s_rows // ts,), out_shape=[SDS((s_rows, d), BF16)] * 5,
        in_specs=[rows, rows, _row_spec(ts, d, 3), _row_spec(ts, d, 4), wspec, wspec], out_specs=[rows] * 5,
        compiler_params=_params("arbitrary"),
    )(ga, pooled, proj, proj, w_a, w_b)


def _seq_backward_by_block(proj, xr, hr, p, dga, dpooled, dgab, conv_w, vecs, w_a, w_x, w_p, *, ts, stage=None):
    s_rows = proj.shape[0]
    d = D_MODEL
    n_t = s_rows // ts
    nn, nt, tn = (((1,), (0,)), ((), ())), (((1,), (1,)), ((), ())), (((0,), (0,)), ((), ()))

    def dot(lhs, rhs, dims):
        return lax.dot_general(lhs, rhs, dims, preferred_element_type=F32)

    def body(x_ref, y_ref, xr_ref, hr_ref, hh_ref, p_ref, dga_ref, dpl_ref, dgab_ref, cw_ref, vec_ref, wa_ref, wx_ref, wp_ref,
             dproj_o, acc_o, dwa_o, dwx_o, dwp_o, g_carry, dxr_carry, q_carry, a_scr, b_scr, g_scr, a_keep, r_keep, i_keep, m_keep):
        step = pl.program_id(0)
        t = n_t - 1 - step

        @pl.when(step == 0)
        def _():
            for ref in (acc_o, dwa_o, dwx_o, dwp_o, g_carry, dxr_carry, q_carry):
                ref[...] = jnp.zeros_like(ref)

        rows = lax.broadcasted_iota(jnp.int32, (ts, BLOCK), 0)
        start = (rows == 0) & (t == 0)

        def add_sum(row, lanes, value):
            acc_o[row:row + 1, lanes] += jnp.sum(value, axis=0, keepdims=True)

        for g in range(N_BLOCKS):
            lanes = slice(g * BLOCK, (g + 1) * BLOCK)
            xrb = xr_ref[:, lanes].astype(BF16)
            r = _sigmoid(dot(xrb, wa_ref[g], nn) + vec_ref[1:2, lanes])
            gate_i = _sigmoid(dot(xrb, wx_ref[g], nn) + vec_ref[2:3, lanes])
            log_a = (-C_RG * r) * jax.nn.softplus(vec_ref[3:4, lanes])
            a = jnp.exp(log_a)
            a_keep[:, lanes], r_keep[:, lanes], i_keep[:, lanes], m_keep[:, lanes] = a, r, gate_i, _one_minus_square(a, log_a)
            gelu, dgelu = _gelu_parts(y_ref[:, lanes])
            dga_t = dga_ref[:, lanes].astype(F32)
            dproj_o[:, d + g * BLOCK:d + (g + 1) * BLOCK] = (dga_t * hr_ref[:, lanes] * dgelu).astype(BF16)
            a_scr[:, lanes] = jnp.where(rows == ts - 1, 1.0, pltpu.roll(a, ts - 1, 0))
            b_scr[:, lanes] = dga_t * gelu

        sub = lax.broadcasted_iota(jnp.int32, (8, d), 0)

        def chunk(i, g_next):
            at = pl.multiple_of((ts // 8 - 1 - i) * 8, 8)
            aa = a_scr[pl.ds(at, 8), :]
            bb = b_scr[pl.ds(at, 8), :]
            for s in (1, 2, 4):
                keep = sub < 8 - s
                bb = jnp.where(keep, bb + aa * pltpu.roll(bb, 8 - s, 0), bb)
                aa = jnp.where(keep, aa * pltpu.roll(aa, 8 - s, 0), aa)
            gg = aa * g_next + bb
            g_scr[pl.ds(at, 8), :] = gg
            return gg[0:1, :]

        g_first = lax.fori_loop(0, ts // 8, chunk, g_carry[...])
        g_carry[...] = a_keep[0:1, :] * g_first

        pos = (rows + t * ts + 1).astype(F32)
        for g, win in enumerate(POOL_WINDOWS):
            lanes = slice(g * BLOCK, (g + 1) * BLOCK)
            gs, xr, a, r, gate_i, m_square = g_scr[:, lanes], xr_ref[:, lanes], a_keep[:, lanes], r_keep[:, lanes], i_keep[:, lanes], m_keep[:, lanes]
            sp = jax.nn.softplus(vec_ref[3:4, lanes])
            mult = jnp.where(start, 1.0, jnp.sqrt(m_square))
            h_before = jnp.where(t == 0, 0.0, hh_ref[CONV_HALO - 1:CONV_HALO, lanes])
            h_prev = jnp.where(rows == 0, h_before, pltpu.roll(hr_ref[:, lanes], 1, 0))
            gm = gs * mult
            dxr = gm * gate_i
            d_i = gm * xr
            d_mult = (gs * xr) * gate_i
            dlog_a = (gs * h_prev) * a - jnp.where(start, 0.0, d_mult * (a * a) * lax.rsqrt(m_square))
            dzr = (dlog_a * (-C_RG * sp)) * (r * (1.0 - r))
            dzi = d_i * (gate_i * (1.0 - gate_i))
            add_sum(5, lanes, dzr)
            add_sum(6, lanes, dzi)
            acc_o[7:8, lanes] += jnp.sum(dlog_a * r, axis=0, keepdims=True) * (-C_RG * jax.nn.sigmoid(vec_ref[3:4, lanes]))
            xrb, dzrb, dzib = xr.astype(BF16), dzr.astype(BF16), dzi.astype(BF16)
            dwa_o[g] += dot(xrb, dzrb, tn)
            dwx_o[g] += dot(xrb, dzib, tn)
            dxr = dxr + dot(dzrb, wa_ref[g], nt) + dot(dzib, wx_ref[g], nt)

            x = x_ref[:, lanes]
            ext = jnp.concatenate([dxr, dxr_carry[:, lanes]], axis=0)
            dx = dxr * cw_ref[CONV_WIDTH - 1:CONV_WIDTH, lanes]
            add_sum(CONV_WIDTH - 1, lanes, x * dxr)
            for j in range(CONV_WIDTH - 1):
                ahead = pltpu.roll(ext, ts + CONV_HALO - (CONV_WIDTH - 1 - j), 0)[:ts]
                dx = dx + ahead * cw_ref[j:j + 1, lanes]
                add_sum(j, lanes, x * ahead)
            add_sum(4, lanes, dxr)
            dxr_carry[:, lanes] = dxr[0:CONV_HALO, :]
            dproj_o[:, lanes] = dx.astype(BF16)

            pb = p_ref[:, lanes]
            mixed = dot(pb, wp_ref[g], nn) + vec_ref[4:5, lanes]
            dpl = dpl_ref[:, lanes].astype(F32)
            add_sum(9, lanes, dpl * mixed)
            dmixed = dpl * vec_ref[5:6, lanes]
            add_sum(8, lanes, dmixed)
            dmb = dmixed.astype(BF16)
            dwp_o[g] += dot(pb, dmb, tn)
            dp = dot(dmb, wp_ref[g], nt)
            q = dp * (1.0 / jnp.minimum(pos, float(win)))
            sums, have = jnp.concatenate([q, q_carry[:, lanes]], axis=0), 1
            while have < win:
                sums = sums + pltpu.roll(sums, ts + POOL_HALO - have, 0)
                have *= 2
            q_carry[:, lanes] = q[0:POOL_HALO, :]
            dproj_o[:, 2 * d + g * BLOCK:2 * d + (g + 1) * BLOCK] = (sums[:ts] - dp).astype(BF16)

        dproj_o[:, 3 * d:5 * d] = dgab_ref[...]

    def rev(cols, col_block=0):
        return pl.BlockSpec((ts, cols), lambda i: (n_t - 1 - i, col_block))

    halo_h = pl.BlockSpec((CONV_HALO, d), lambda i: (jnp.maximum((n_t - 1 - i) * (ts // CONV_HALO) - 1, 0), 0))
    wspec = pl.BlockSpec((N_BLOCKS, BLOCK, BLOCK), lambda i: (0, 0, 0))
    return _hosted(
        stage, body, name="seq_backward", grid=(n_t,),
        out_shape=[SDS((s_rows, 5 * d), BF16), SDS((16, d), F32)] + [SDS((N_BLOCKS, BLOCK, BLOCK), F32)] * 3,
        in_specs=[rev(d, 0), rev(d, 1), rev(d), rev(d), halo_h, rev(d), rev(d), rev(d), rev(2 * d), _vec_spec(CONV_WIDTH, d), _vec_spec(8, d),
                  wspec, wspec, wspec],
        out_specs=[rev(5 * d), _vec_spec(16, d), wspec, wspec, wspec],
        scratch_shapes=[pltpu.VMEM((1, d), F32), pltpu.VMEM((CONV_HALO, d), F32), pltpu.VMEM((POOL_HALO, d), F32)] + [pltpu.VMEM((ts, d), F32)] * 7,
        compiler_params=_params("arbitrary"),
    )(proj, proj, xr, hr, hr, p, dga, dpooled, dgab, conv_w, vecs, w_a, w_x, w_p)


def _adamw_math(w, g, m, v):
    m = ADAM_B1 * m + (1.0 - ADAM_B1) * g
    v = ADAM_B2 * v + (1.0 - ADAM_B2) * (g * g)
    m_hat = m / (1.0 - ADAM_B1 ** ADAM_STEP)
    v_hat = v / (1.0 - ADAM_B2 ** ADAM_STEP)
    delta = -ADAM_LR * (m_hat / (jnp.sqrt(v_hat) + ADAM_EPS) + ADAM_WD * w)
    return delta, m, v


def _adamw(w, g, m, v, name):
    rows, cols = w.shape
    tr = min(rows, 512)

    def body(w_ref, g_ref, m_ref, v_ref, g_out, d_ref, mo_ref, vo_ref):
        g = g_ref[...]
        g_out[...] = g
        d_ref[...], mo_ref[...], vo_ref[...] = _adamw_math(w_ref[...], g, m_ref[...], v_ref[...])

    spec = pl.BlockSpec((tr, cols), lambda i: (i, 0))
    return pl.pallas_call(
        body, name=name, grid=(rows // tr,), out_shape=[SDS((rows, cols), F32)] * 4, in_specs=[spec] * 4, out_specs=[spec] * 4,
        compiler_params=_params("parallel"),
    )(w, g, m, v)


def _adamw_on_sparsecores(w, g, m, v, name):
    rows, cols = w.shape
    per_tile = rows // SC_TILES
    assert rows % (SC_TILES * SC_SLAB) == 0 and cols % SC_LANES == 0

    def body(w_hbm, g_hbm, m_hbm, v_hbm, g_out, d_out, m_out, v_out, wb, gb, mb, vb, db):
        tile = lax.axis_index("subcore") * 2 + lax.axis_index("core")

        @pl.loop(0, per_tile, step=SC_SLAB)
        def _(r0):
            at = pl.ds(tile * per_tile + r0, SC_SLAB)
            for src, buf in ((w_hbm, wb), (g_hbm, gb), (m_hbm, mb), (v_hbm, vb)):
                pltpu.sync_copy(src.at[at], buf)
            for r in range(SC_SLAB):
                @pl.loop(0, cols, step=SC_LANES)
                def _(col):
                    lanes = pl.ds(col, SC_LANES)
                    db[r, lanes], mb[r, lanes], vb[r, lanes] = _adamw_math(wb[r, lanes], gb[r, lanes], mb[r, lanes], vb[r, lanes])
            for buf, dst in ((gb, g_out), (db, d_out), (mb, m_out), (vb, v_out)):
                pltpu.sync_copy(buf, dst.at[at])

    return pl.kernel(
        body, name=name, out_type=[SDS(w.shape, F32)] * 4,
        mesh=plsc.VectorSubcoreMesh(core_axis_name="core", subcore_axis_name="subcore"),
        scratch_types=[pltpu.VMEM((SC_SLAB, cols), F32)] * 5,
    )(w, g, m, v)


def _prologue(c, conv_shard, w_ada, b_cols, w_in_halves, gate_halves):
    n_cols = w_ada.shape[1]
    n_g = len(gate_halves)
    half_rows = w_in_halves.shape[1] // 2
    n_sems = 3 * N_DEV + N_CHIPS * n_g + 5 + 2 + 3

    def body(c_ref, conv_ref, w_ref, b_ref, win_ref, *rest):
        gate_refs = rest[:n_g]
        act_o, mod_o, conv_o, win_o = rest[n_g:n_g + 4]
        gate_o = rest[n_g + 4:2 * n_g + 4]
        c_all, mod_cols, w_vmem, send_sems, recv_sems = rest[2 * n_g + 4:]
        fetch_w = pltpu.make_async_copy(w_ref, w_vmem, send_sems.at[n_sems])
        x, y, c = _coords()
        me, k_me = 4 * x + 2 * y + c, 2 * x + y
        sibling, x_nbr, y_nbr = (x, y, 1 - c), (1 - x, y, c), (x, 1 - y, c)
        k_x, k_y, k_d = 2 * (1 - x) + y, 2 * x + (1 - y), 2 * (1 - x) + (1 - y)
        counter = iter(range(n_sems))

        def copy(src, dst, to, landing):
            s = next(counter)
            out = pltpu.make_async_remote_copy(src_ref=src, dst_ref=dst, send_sem=send_sems.at[s], recv_sem=recv_sems.at[s],
                                               device_id=to, device_id_type=MESH)
            arrival = pltpu.make_async_remote_copy(src_ref=landing, dst_ref=landing, send_sem=send_sems.at[s], recv_sem=recv_sems.at[s],
                                                   device_id=(x, y, c), device_id_type=MESH)
            return out, arrival

        def allgather(src, out):
            s = next(counter)
            own = pltpu.make_async_copy(src, out.at[me], send_sems.at[s])
            peers = [copy(src, out.at[me], (_flip(x, r & 4), _flip(y, r & 2), _flip(c, r & 1)), out.at[me ^ r]) for r in range(1, N_DEV)]
            return own, peers

        def start(group):
            own, peers = group
            own.start()
            for out, _ in peers:
                out.start()

        def finish(group):
            own, peers = group
            for out, arrival in peers:
                arrival.wait_recv()
                out.wait_send()
            own.wait()

        rows_a, rows_b = pl.ds(0, half_rows), pl.ds(half_rows, half_rows)
        c_group, conv_group = allgather(c_ref, c_all), allgather(conv_ref, conv_o)
        gates = []
        for g_ref, g_out in zip(gate_refs, gate_o):
            gates.append(copy(g_ref, g_out.at[k_me], sibling, g_out.at[k_me]))
            gates += [copy(g_ref.at[c], g_out.at[k_me, c], (px, py, c), g_out.at[2 * px + py, c]) for px, py in _chips()]
        own_in = copy(win_ref, win_o.at[k_me], sibling, win_o.at[k_me])
        to_x = [copy(win_ref.at[c, rows], win_o.at[k_me, c, rows], x_nbr, win_o.at[k_x, c, rows]) for rows in (rows_a, rows_b)]
        to_y = [copy(win_ref.at[c, rows], win_o.at[k_me, c, rows], y_nbr, win_o.at[k_y, c, rows]) for rows in (rows_b, rows_a)]
        mod_group = allgather(mod_cols, mod_o)
        relay_y = copy(win_o.at[k_x, c, rows_a], win_o.at[k_x, c, rows_a], y_nbr, win_o.at[k_d, c, rows_a])
        relay_x = copy(win_o.at[k_y, c, rows_b], win_o.at[k_y, c, rows_b], x_nbr, win_o.at[k_d, c, rows_b])
        sib_x, sib_y, sib_d = (copy(win_o.at[k, c], win_o.at[k, c], sibling, win_o.at[k, 1 - c]) for k in (k_x, k_y, k_d))

        start(c_group)
        for out, _ in to_x + to_y + [own_in] + gates:
            out.start()
        start(conv_group)
        fetch_w.start()
        finish(c_group)
        cv = jnp.concatenate([c_all[r] for r in range(N_DEV)], axis=0)
        act = cv * jax.nn.sigmoid(cv)
        act_o[...] = act
        fetch_w.wait()
        mod_cols[...] = jnp.dot(act.astype(BF16), w_vmem[...].astype(BF16), preferred_element_type=F32) + b_ref[...]
        start(mod_group)
        to_x[0][1].wait_recv()
        relay_y[0].start()
        to_y[0][1].wait_recv()
        relay_x[0].start()
        finish(mod_group)
        finish(conv_group)
        to_x[1][1].wait_recv()
        sib_x[0].start()
        to_y[1][1].wait_recv()
        sib_y[0].start()
        for out, arrival in gates + [own_in]:
            arrival.wait_recv()
            out.wait_send()
        for out, arrival in (relay_y, relay_x):
            arrival.wait_recv()
            out.wait_send()
        sib_d[0].start()
        for out, _ in to_x + to_y:
            out.wait_send()
        for out, arrival in (sib_x, sib_y, sib_d):
            arrival.wait_recv()
            out.wait_send()

    gate_shapes = [SDS((N_CHIPS,) + g.shape, g.dtype) for g in gate_halves]
    return pl.pallas_call(
        body, name="prologue",
        out_shape=[SDS((N_DEV, c.shape[1]), F32), SDS((N_DEV, N_DEV, n_cols), F32), SDS((N_DEV,) + conv_shard.shape, F32),
                   SDS((N_CHIPS,) + w_in_halves.shape, w_in_halves.dtype)] + gate_shapes,
        in_specs=[VMEM_SPEC, VMEM_SPEC, HBM_SPEC, VMEM_SPEC] + [HBM_SPEC] * (1 + n_g), out_specs=[VMEM_SPEC] * 3 + [HBM_SPEC] * (1 + n_g),
        scratch_shapes=[pltpu.VMEM((N_DEV,) + c.shape, F32), pltpu.VMEM((N_DEV, n_cols), F32), pltpu.VMEM(w_ada.shape, F32),
                        pltpu.SemaphoreType.DMA((n_sems + 1,)), pltpu.SemaphoreType.DMA((n_sems,))],
        compiler_params=pltpu.CompilerParams(vmem_limit_bytes=VMEM_LIMIT_V7X),
    )(c, conv_shard, w_ada, b_cols, w_in_halves, *gate_halves)


def _ada_backward(act_t, dmod_cols, w, m, v):
    rows, cols = w.shape
    tn = 512

    def body(a_ref, dm_ref, w_ref, m_ref, v_ref, g_ref, d_ref, mo_ref, vo_ref):
        g = jnp.dot(a_ref[...].astype(BF16), dm_ref[...].astype(BF16), preferred_element_type=F32)
        g_ref[...] = g
        d_ref[...], mo_ref[...], vo_ref[...] = _adamw_math(w_ref[...], g, m_ref[...], v_ref[...])

    spec = pl.BlockSpec((rows, tn), lambda j: (0, j))
    return pl.pallas_call(
        body, name="ada_backward", grid=(cols // tn,), out_shape=[SDS((rows, cols), F32)] * 4,
        in_specs=[pl.BlockSpec((rows, N_DEV), lambda j: (0, 0)), pl.BlockSpec((N_DEV, tn), lambda j: (0, j)), spec, spec, spec],
        out_specs=[spec] * 4, compiler_params=_params("parallel"),
    )(act_t, dmod_cols, w, m, v)


def _sum_rows(gathered, n_loss):
    n = gathered.shape[2]

    def body(g_ref, o_ref):
        acc = g_ref[0]
        for r in range(1, N_DEV):
            acc = acc + g_ref[r]
        o_ref[...] = acc
        o_ref[:, n - n_loss:n] = jnp.broadcast_to(jnp.sum(acc[:, n - n_loss:n], axis=1, keepdims=True), (1, n_loss))

    return pl.pallas_call(body, name="sum_rows", out_shape=SDS((1, n), F32), in_specs=[VMEM_SPEC], out_specs=VMEM_SPEC)(gathered)


def _adamw_vectors(total, offsets, separate, params):
    n_p = len(params)
    apart = [g for g in separate if g is not None]

    def body(*refs):
        total_ref, apart_refs = refs[0], list(refs[1:1 + len(apart)])
        ins, outs = refs[1 + len(apart):1 + len(apart) + 3 * n_p], refs[1 + len(apart) + 3 * n_p:]
        for i in range(n_p):
            w_ref, m_ref, v_ref = ins[3 * i:3 * i + 3]
            g = apart_refs.pop(0)[...] if offsets[i] is None else total_ref[:, offsets[i]:offsets[i] + w_ref.shape[1]]
            outs[4 * i][...] = g
            outs[4 * i + 1][...], outs[4 * i + 2][...], outs[4 * i + 3][...] = _adamw_math(w_ref[...], g, m_ref[...], v_ref[...])

    flat = [a for p in params for a in p]
    res = pl.pallas_call(
        body, name="adamw_vectors", out_shape=[SDS(p[0].shape, F32) for p in params for _ in range(4)],
        in_specs=[VMEM_SPEC] * (1 + len(apart) + len(flat)), out_specs=[VMEM_SPEC] * (4 * n_p),
    )(total, *apart, *flat)
    return [tuple(res[4 * i:4 * i + 4]) for i in range(n_p)]


def _blocks_to_pieces(w):
    nb, rows, n = w.shape
    q = rows // N_CHIPS
    return w.reshape(nb, N_CHIPS, q, n).transpose(1, 0, 2, 3).reshape(N_CHIPS, 2, nb // 2, q, n)


def _pieces_to_blocks(w):
    n_chips, _, half, q, n = w.shape
    return w.reshape(n_chips, 2 * half, q, n).transpose(1, 0, 2, 3).reshape(2 * half, n_chips * q, n)


def kernel(x, c, norm_mix_g, norm_mlp_g, w_ada, b_ada, w_in, conv_w, conv_b, w_rg_a, b_rg_a, w_rg_x, b_rg_x, a_param, w_branch_a, w_pool, b_pool, pool_scale, w_branch_b, w_out, w_up, w_down, final_g, loss_target, m_norm_mix_g, m_norm_mlp_g, m_w_ada, m_b_ada, m_w_in, m_conv_w, m_conv_b, m_w_rg_a, m_b_rg_a, m_w_rg_x, m_b_rg_x, m_a_param, m_w_branch_a, m_w_pool, m_b_pool, m_pool_scale, m_w_branch_b, m_w_out, m_w_up, m_w_down, m_final_g, v_norm_mix_g, v_norm_mlp_g, v_w_ada, v_b_ada, v_w_in, v_conv_w, v_conv_b, v_w_rg_a, v_b_rg_a, v_w_rg_x, v_b_rg_x, v_a_param, v_w_branch_a, v_w_pool, v_b_pool, v_pool_scale, v_w_branch_b, v_w_out, v_w_up, v_w_down, v_final_g):
    d = D_MODEL
    s_rows = x.shape[1]
    ts, tsq = min(512, s_rows), min(256, s_rows)
    xi, yi, ci = _coords()
    k_me, me = 2 * xi + yi, 4 * xi + 2 * yi + ci
    ada_cols = w_ada.shape[2]
    conv_cols = conv_w.shape[2]
    n_in, n_up = w_in.shape[2], w_up.shape[2]

    names = ("w_in", "w_up", "w_down", "w_a", "w_b", "w_out", "rg_a", "rg_x", "pool")
    mine = dict(zip(names, (w_in[0], w_up[0], w_down[0], w_branch_a[0], w_branch_b[0], w_out[0], w_rg_a[0], w_rg_x[0], w_pool[0])))
    moments_m = dict(zip(names, (m_w_in[0], m_w_up[0], m_w_down[0], m_w_branch_a[0], m_w_branch_b[0], m_w_out[0], m_w_rg_a[0], m_w_rg_x[0], m_w_pool[0])))
    moments_v = dict(zip(names, (v_w_in[0], v_w_up[0], v_w_down[0], v_w_branch_a[0], v_w_branch_b[0], v_w_out[0], v_w_rg_a[0], v_w_rg_x[0], v_w_pool[0])))
    block_weights, squares = ("rg_a", "rg_x", "pool"), ("w_a", "w_b", "w_out")
    place = jnp.stack([ci, k_me]).astype(jnp.int32)
    g_mix, g_mlp, g_fin = norm_mix_g, norm_mlp_g, final_g.reshape(1, d)
    x0, target = x[0], loss_target[0]
    wg = functools.partial(_weight_grad, tk=min(512, d))
    plain, back, back_sum = (dict(transposed=t, reduce_blocks=r) for t, r in ((False, False), (True, False), (True, True)))
    rows_d, vec8, vec1 = _row_spec(ts, d), _vec_spec(8, d), _vec_spec(1, d)
    tw = min(1024, s_rows)
    halves_of_tile = 2

    def out_rows(cols, dtype, t=ts):
        return SDS((s_rows, cols), dtype), _row_spec(t, cols)

    sums_out = (SDS((8, d), F32), vec8)

    def halves(n):
        return mine[n].astype(BF16).reshape(2, -1, mine[n].shape[-1])

    def blocks(w):
        return _pieces_to_blocks(w.reshape(N_CHIPS, 2, N_BLOCKS // 2, BLOCK // N_CHIPS, BLOCK))

    act_all, mod_all, conv_all, w_in_whole, *gate_wholes = _prologue(
        c, conv_w[0], w_ada[0], lax.dynamic_slice_in_dim(b_ada, k_me * ada_cols, ada_cols, axis=1), halves("w_in"), [halves(n) for n in block_weights])
    conv_full = conv_all[0::2].transpose(1, 0, 2).reshape(CONV_WIDTH, d)
    mod_mine = lax.dynamic_index_in_dim(mod_all, me, axis=1, keepdims=False)[0::2]
    mod = jnp.concatenate([mod_mine.reshape(N_MOD, d), jnp.zeros((8 - N_MOD, d), F32)], axis=0)
    w_in_all = w_in_whole.reshape(N_CHIPS, d, n_in)
    vecs = jnp.concatenate([conv_b, b_rg_a, b_rg_x, a_param, b_pool, pool_scale, jnp.zeros((2, d), F32)], axis=0)

    def norm_first(a_ref, extra_refs, out_refs):
        h = _modulated_norm(a_ref[...], extra_refs[0], extra_refs[1], 1, 0)
        out_refs[1][...] = h
        return h

    stage = _Stage()
    i_blk = [_add_gather_d2d(stage, g) for g in gate_wholes]
    i_sq = [_add_gather_ici(stage, halves(n)) for n in squares]
    down_rows = mine["w_down"].shape[0] // 8
    i_down = _add_gather_ici(stage, halves("w_down"), rows=(0, down_rows))
    (proj, h1), got = _matmul(x0, w_in_all, name="proj_in", tm=ts, extra=[(mod, vec8), (g_mix, vec1)], prepare=norm_first,
                              outs=[out_rows(N_CHIPS * n_in, F32), out_rows(d, BF16)], finish=_store_blocks(n_in), parts=halves_of_tile,                              stage=stage, **plain)
    w_rg_a_all, w_rg_x_all, w_pool_all = (blocks(got[i]) for i in i_blk)
    stage = _Stage()
    i_sq = [_add_gather_d2d(stage, got[i]) for i in i_sq]
    i_up = _add_gather_ici(stage, halves("w_up"))
    w_down_so_far = got[i_down]
    (xr, hr, ga, p, pooled), got = _seq_forward(proj, conv_full, vecs, w_rg_a_all, w_rg_x_all, w_pool_all, ts=tsq, stage=stage)
    w_a_all, w_b_all, w_out_all = (got[i].reshape(1, d, d) for i in i_sq)
    stage = _Stage()
    i_up = _add_gather_d2d(stage, got[i_up])
    i_down = _add_gather_ici(stage, halves("w_down"), rows=(down_rows, 2 * down_rows), whole=w_down_so_far)
    (merged, s_a, s_b, t_a, t_b), got = _branches(ga, pooled, proj, w_a_all, w_b_all, ts=ts, stage=stage)
    w_up_all = got[i_up].reshape(N_CHIPS, d, n_up)
    stage = _Stage()
    i_down = _add_gather_ici(stage, halves("w_down"), rows=(3 * down_rows, down_rows), whole=got[i_down])

    def residual_norm(k, acc, extra_refs, out_refs):
        x_ref, mod_ref, g_ref = extra_refs
        out_refs[0][...] = acc.astype(BF16)
        x2_t = x_ref[...] + mod_ref[2:3, :] * acc
        out_refs[1][...] = x2_t
        out_refs[2][...] = _modulated_norm(x2_t, mod_ref, g_ref, 4, 3)

    (mo, x2, h2), got = _matmul(merged, w_out_all, name="mix_out", tm=ts, extra=[(x0, rows_d), (mod, vec8), (g_mlp, vec1)],
                                outs=[out_rows(d, BF16), out_rows(d, F32), out_rows(d, BF16)], finish=residual_norm, parts=halves_of_tile,                                stage=stage, transposed=False, reduce_blocks=True)
    stage = _Stage()
    i_down = _add_gather_d2d(stage, got[i_down])

    def relu_squared(k, acc, extra_refs, out_refs):
        out_refs[0][:, k * n_up:(k + 1) * n_up] = jnp.square(jnp.maximum(acc, 0.0)).astype(BF16)

    (ff,), got = _matmul(h2, w_up_all, name="mlp_up", tm=ts, outs=[out_rows(D_FF, BF16)], finish=relu_squared, parts=halves_of_tile,
                         stage=stage, **plain)
    w_down_all = got[i_down].reshape(N_CHIPS, D_FF // N_CHIPS, d)

    def loss_head(k, acc, extra_refs, out_refs):
        x2_ref, mod_ref, g_ref, t_ref = extra_refs
        dx_ref, df_ref, acc_ref = out_refs
        xh, r = _rms(x2_ref[...] + mod_ref[5:6, :] * acc)
        err = xh * g_ref[...] - t_ref[...]
        dy = err / d
        dxh = dy * g_ref[...]
        dx3_t = r * (dxh - xh * jnp.mean(dxh * xh, axis=-1, keepdims=True))
        dx_ref[...] = dx3_t
        df_ref[...] = (dx3_t * mod_ref[5:6, :]).astype(BF16)
        _accumulate(acc_ref, 0, dy * xh)
        _accumulate(acc_ref, 1, dx3_t * acc)
        _accumulate(acc_ref, 2, jnp.square(err) * (0.5 / d))

    dx3, dffo, sums_head = _matmul(ff, w_down_all, name="mlp_down", tm=ts, extra=[(x2, rows_d), (mod, vec8), (g_fin, vec1), (target, rows_d)],
                                   outs=[out_rows(d, F32), out_rows(d, BF16), sums_out], finish=loss_head, zero_first=(2,), parts=halves_of_tile,                                   transposed=False, reduce_blocks=True)

    partial_of, from_sibling, chip_sum, from_chips, half_done, quarter = {}, {}, {}, {}, {}, {}

    def publish(n, g):
        g = _blocks_to_pieces(g).astype(BF16) if n in block_weights else g
        partial_of[n] = g.reshape(N_CHIPS, 2, -1, mine[n].shape[-1])

    def exchange(to_sibling=(), to_chips=(), swap=()):
        st = _Stage()
        slots = [(n, from_sibling, _add_reduce_d2d(st, partial_of[n])) for n in to_sibling]
        slots += [(n, from_chips, _add_reduce_ici(st, chip_sum[n])) for n in to_chips]
        slots += [(n, quarter, _add_swap_halves(st, half_done[n])) for n in swap]
        return st, slots

    big_out = {}

    def update(n):
        shape2 = (-1, mine[n].shape[-1])
        early = n in ("w_down", "w_up") + squares
        res = (_adamw_on_sparsecores if early else _adamw)(
            mine[n].reshape(shape2), quarter[n].reshape(shape2), moments_m[n].reshape(shape2), moments_v[n].reshape(shape2), "adamw_" + n)
        big_out[n] = [r.reshape((1,) + mine[n].shape) for r in res]

    def collect(slots, outs):
        for n, where, i in slots:
            where[n] = outs[i]
            if where is quarter:
                update(n)

    def sum_pairs(*ns):
        for n in ns:
            chip_sum[n] = _sum_pair(partial_of[n], from_sibling[n], place, "sum_pair_" + n)

    def sum_quarters(*ns):
        for n in ns:
            half_done[n] = _sum_quarter(partial_of[n], from_sibling[n], from_chips[n], place, "sum_quarter_" + n)

    publish("w_down", wg(ff, dffo, nb=1, name="grad_w_down"))

    def relu_squared_backward(k, acc, extra_refs, out_refs):
        cols = slice(k * n_up, (k + 1) * n_up)
        out_refs[0][:, cols] = (acc * (2.0 * jnp.sqrt(extra_refs[0][:, cols].astype(F32)))).astype(BF16)

    stage, slots = exchange(to_sibling=["w_down"])
    (dup,), got = _matmul(dffo, w_down_all, name="d_mlp_down", tm=ts, extra=[(ff, _row_spec(ts, D_FF))], outs=[out_rows(D_FF, BF16)],
                          finish=relu_squared_backward, parts=halves_of_tile, stage=stage, **back)
    collect(slots, got)
    sum_pairs("w_down")

    def norm_mlp_backward(k, acc, extra_refs, out_refs):
        x2_ref, dres_ref, mod_ref, g_ref, mo_ref = extra_refs
        dx_ref, dmo_ref, acc_ref = out_refs
        dx =_modulated_norm_backward(acc, x2_ref[...], dres_ref[...], mod_ref, g_ref, acc_ref, 4)
        dx_ref[...] = dx
        dmo_ref[...] = (dx * mod_ref[2:3, :]).astype(BF16)
        _accumulate(acc_ref, 3, dx * mo_ref[...].astype(F32))

    stage, slots = exchange(to_chips=["w_down"])
    (dx2, dmo, sums_mlp), got = _matmul(dup, w_up_all, name="d_mlp_up", tm=ts, stage=stage, finish=norm_mlp_backward,
                                        extra=[(x2, rows_d), (dx3, rows_d), (mod, vec8), (g_mlp, vec1), (mo, rows_d)],
                                        outs=[out_rows(d, F32), out_rows(d, BF16), sums_out], zero_first=(2,), parts=halves_of_tile, **back_sum)
    collect(slots, got)
    sum_quarters("w_down")
    publish("w_up", wg(h2, dup, nb=N_CHIPS, name="grad_w_up"))
    publish("w_out", wg(merged, dmo, nb=1, name="grad_w_out"))

    def merge_backward(k, acc, extra_refs, out_refs):
        sa_ref, sb_ref, ta_ref, tb_ref = extra_refs
        out_refs[0][...] = (acc * sa_ref[...].astype(F32)).astype(BF16)
        out_refs[1][...] = (acc * sb_ref[...].astype(F32)).astype(BF16)
        out_refs[2][:, 0:d] = (acc * ta_ref[...].astype(F32)).astype(BF16)
        out_refs[2][:, d:2 * d] = (acc * tb_ref[...].astype(F32)).astype(BF16)

    stage, slots = exchange(to_sibling=["w_up", "w_out"], swap=["w_down"])
    (dbr_a, dbr_b, dgab), got = _matmul(dmo, w_out_all, name="d_mix_out", tm=ts, stage=stage, finish=merge_backward,
                                        extra=[(s_a, rows_d), (s_b, rows_d), (t_a, rows_d), (t_b, rows_d)],
                                        outs=[out_rows(d, BF16), out_rows(d, BF16), out_rows(2 * d, BF16)], parts=halves_of_tile, **back_sum)
    collect(slots, got)
    publish("w_a", wg(ga, dbr_a, nb=1, name="grad_w_branch_a"))
    publish("w_b", wg(pooled, dbr_b, nb=1, name="grad_w_branch_b"))
    dga, = _matmul(dbr_a, w_a_all, name="d_branch_a", tm=tw, outs=[out_rows(d, BF16, tw)], finish=_store_blocks(d), **back_sum)
    dpooled, = _matmul(dbr_b, w_b_all, name="d_branch_b", tm=tw, outs=[out_rows(d, BF16, tw)], finish=_store_blocks(d), **back_sum)
    sum_pairs("w_up", "w_out")
    stage, slots = exchange(to_sibling=["w_a", "w_b"], to_chips=["w_up", "w_out"])
    (dproj, sums_seq, d_rg_a, d_rg_x, d_pool), got = _seq_backward_by_block(proj, xr, hr, p, dga, dpooled, dgab, conv_full, vecs,
                                                                            w_rg_a_all, w_rg_x_all, w_pool_all, ts=tsq, stage=stage)
    collect(slots, got)
    sum_quarters("w_up", "w_out")
    sum_pairs("w_a", "w_b")
    for n, g in zip(block_weights, (d_rg_a, d_rg_x, d_pool)):
        publish(n, g)
    stage, slots = exchange(to_sibling=block_weights, to_chips=["w_a", "w_b"], swap=["w_up", "w_out"])
    g_in, got = wg(h1, dproj, nb=N_CHIPS, name="grad_w_in", stage=stage)
    collect(slots, got)
    publish("w_in", g_in)
    sum_pairs(*block_weights)
    sum_quarters("w_a", "w_b")
    stage, slots = exchange(to_sibling=["w_in"], to_chips=block_weights)
    collect(slots, _run_stage(stage, "exchange_w_in_gates"))
    sum_pairs("w_in")
    sum_quarters(*block_weights)

    def norm_mix_backward(k, acc, extra_refs, out_refs):
        x_ref, dres_ref, mod_ref, g_ref = extra_refs
        out_refs[0][...] = _modulated_norm_backward(acc, x_ref[...], dres_ref[...], mod_ref, g_ref, out_refs[1], 1)

    stage, slots = exchange(to_chips=["w_in"], swap=block_weights + ("w_a", "w_b"))
    (grad_x, sums_mix), got = _matmul(dproj, w_in_all, name="d_proj_in", tm=ts, stage=stage, finish=norm_mix_backward,
                                      extra=[(x0, rows_d), (dx2, rows_d), (mod, vec8), (g_mix, vec1)],
                                      outs=[out_rows(d, F32), sums_out], zero_first=(1,), parts=halves_of_tile, **back_sum)
    collect(slots, got)
    sum_quarters("w_in")

    dmod = jnp.concatenate([sums_mix[1:2], sums_mix[0:1], sums_mlp[3:4], sums_mlp[1:2], sums_mlp[0:1], sums_head[1:2]], axis=1)
    row = jnp.concatenate([sums_mix[2:3], sums_mlp[2:3], sums_seq[4:5], sums_seq[5:6], sums_seq[6:7], sums_seq[7:8], sums_seq[8:9],
                           sums_seq[9:10], sums_head[0:1], sums_seq[0:CONV_WIDTH].reshape(1, CONV_WIDTH * d), dmod, sums_head[2:3]], axis=1)
    n_vec, n_conv, n_mod = 9 * d, CONV_WIDTH * d, N_MOD * d
    stage, slots = exchange(swap=["w_in"])
    i_rows = _add_allgather8(stage, row)
    got = _run_stage(stage, "exchange_sums_w_in")
    collect(slots, got)
    rows_all = got[i_rows]
    total = _sum_rows(rows_all, d)
    loss = total[0, n_vec + n_conv + n_mod]
    dmod_all = rows_all[:, 0, n_vec + n_conv:n_vec + n_conv + n_mod]
    g_ada, d_ada, m_ada, v_ada = _ada_backward(act_all.T, lax.dynamic_slice_in_dim(dmod_all, k_me * ada_cols, ada_cols, axis=1),
                                               w_ada[0], m_w_ada[0], v_w_ada[0])
    g_conv = lax.dynamic_slice_in_dim(total[:, n_vec:n_vec + n_conv].reshape(CONV_WIDTH, d), k_me * conv_cols, conv_cols, axis=1)
    vec_names = ("norm_mix_g", "norm_mlp_g", "conv_b", "b_rg_a", "b_rg_x", "a_param", "b_pool", "pool_scale", "final_g", "conv_w", "b_ada")
    vec_params = [(norm_mix_g, m_norm_mix_g, v_norm_mix_g), (norm_mlp_g, m_norm_mlp_g, v_norm_mlp_g), (conv_b, m_conv_b, v_conv_b),
                  (b_rg_a, m_b_rg_a, v_b_rg_a), (b_rg_x, m_b_rg_x, v_b_rg_x), (a_param, m_a_param, v_a_param), (b_pool, m_b_pool, v_b_pool),
                  (pool_scale, m_pool_scale, v_pool_scale), tuple(a.reshape(1, d) for a in (final_g, m_final_g, v_final_g)),
                  (conv_w[0], m_conv_w[0], v_conv_w[0]), (b_ada, m_b_ada, v_b_ada)]
    offsets = [i * d for i in range(9)] + [None, n_vec + n_conv]
    vec_res = _adamw_vectors(total, offsets, [None] * 9 + [g_conv, None], vec_params)
    vec_out = {n: [r.reshape(shape) for r in res] for n, res, shape in zip(
        vec_names, vec_res, [(1, d)] * 8 + [(d,), (1, CONV_WIDTH, conv_cols), (1, N_MOD * d)])}

    key = {"w_in": "w_in", "w_rg_a": "rg_a", "w_rg_x": "rg_x", "w_branch_a": "w_a", "w_pool": "pool", "w_branch_b": "w_b", "w_out": "w_out",
           "w_up": "w_up", "w_down": "w_down"}
    order = ("norm_mix_g", "norm_mlp_g", "w_ada", "b_ada", "w_in", "conv_w", "conv_b", "w_rg_a", "b_rg_a", "w_rg_x", "b_rg_x", "a_param",
             "w_branch_a", "w_pool", "b_pool", "pool_scale", "w_branch_b", "w_out", "w_up", "w_down", "final_g")
    ada_out = [g_ada[None], d_ada[None], m_ada[None], v_ada[None]]
    outs = [loss, grad_x[None]]
    for kind in range(4):
        for nme in order:
            outs.append(ada_out[kind] if nme == "w_ada" else big_out[key[nme]][kind] if nme in key else vec_out[nme][kind])
    return tuple(outs)
```

```python
import functools

import jax
import jax.numpy as jnp
from jax import lax
from jax.experimental import pallas as pl
from jax.experimental.pallas import tpu as pltpu
from jax.experimental.pallas import tpu_sc as plsc

F32, BF16 = jnp.float32, jnp.bfloat16
D_MODEL = 1024
D_FF = 4 * D_MODEL
N_BLOCKS = 4
BLOCK = D_MODEL // N_BLOCKS
CONV_WIDTH = 4
POOL_WINDOWS = (2, 4, 8, 16)
CONV_HALO = 8
POOL_HALO = 16
N_MOD = 6
EPS = 1e-6
C_RG = 8.0
ADAM_LR, ADAM_B1, ADAM_B2, ADAM_EPS, ADAM_WD, ADAM_STEP = 0.001, 0.9, 0.999, 1e-08, 0.01, 10
N_DEV = 8
N_CHIPS = 4
SC_TILES, SC_LANES, SC_SLAB = 32, 16, 8
VMEM_LIMIT_V7X = 56 * 2**20
MESH = pl.DeviceIdType.MESH
SDS = jax.ShapeDtypeStruct
HBM_SPEC = pl.BlockSpec(memory_space=pltpu.HBM)
VMEM_SPEC = pl.BlockSpec(memory_space=pltpu.VMEM)


def _params(*semantics):
    return pltpu.CompilerParams(dimension_semantics=semantics, vmem_limit_bytes=VMEM_LIMIT_V7X)


def _coords():
    return lax.axis_index("x"), lax.axis_index("y"), lax.axis_index("c")


def _flip(v, bit):
    return 1 - v if bit else v


def _chips():
    x, y, _ = _coords()
    return [(1 - x, y), (x, 1 - y), (1 - x, 1 - y)]


PEER_SETS = [frozenset(s) for s in (("sibling",), ("chips",), ("sibling", "chips"), ("sibling", "chips", "others"))]


class _Stage:
    def __init__(self):
        self.inputs, self.in_specs, self.outputs, self.out_specs, self.aliases = [], [], [], [], {}
        self.parts, self.n_copies, self.peers = [], 0, set()

    def add(self, inputs, in_spec, outputs, out_spec, n_copies, build, peers, alias=False):
        self.peers |= set(peers)
        i0, o0 = len(self.inputs), len(self.outputs)
        self.inputs += inputs
        self.in_specs += [in_spec] * len(inputs)
        self.outputs += outputs
        self.out_specs += [out_spec] * len(outputs)
        if alias:
            self.aliases.update({i0 + i: o0 + i for i in range(len(inputs))})
        self.parts.append((build, i0, len(inputs), o0, len(outputs)))
        self.n_copies += n_copies
        return list(range(o0, o0 + len(outputs)))

    def copies(self, in_refs, out_refs):
        out = []
        for build, i0, ni, o0, no in self.parts:
            out += build(in_refs[i0:i0 + ni], out_refs[o0:o0 + no])
        assert len(out) == self.n_copies
        return out

    def run(self, in_refs, out_refs, send_sems, recv_sems, start):
        x, y, c = _coords()
        for s, (src, dst, to, landing) in enumerate(self.copies(in_refs, out_refs)):
            if to is None:
                cp = pltpu.make_async_copy(src, dst, send_sems.at[s])
                cp.start() if start else cp.wait()
                continue
            cp = pltpu.make_async_remote_copy(src_ref=src, dst_ref=dst, send_sem=send_sems.at[s], recv_sem=recv_sems.at[s],
                                              device_id=to, device_id_type=MESH)
            if start:
                cp.start()
            else:
                pltpu.make_async_remote_copy(src_ref=landing, dst_ref=landing, send_sem=send_sems.at[s], recv_sem=recv_sems.at[s],
                                             device_id=(x, y, c), device_id_type=MESH).wait_recv()
                cp.wait_send()


def _hosted(stage, body, *, name, in_specs, out_specs, out_shape, grid=(), scratch_shapes=(), compiler_params=None):
    if stage is None:
        return pl.pallas_call(body, name=name, grid=grid, in_specs=in_specs, out_specs=out_specs, out_shape=out_shape,
                              scratch_shapes=list(scratch_shapes), compiler_params=compiler_params)
    single = not isinstance(out_shape, (list, tuple))
    h_out_shape = [out_shape] if single else list(out_shape)
    h_out_specs = [out_specs] if single else list(out_specs)
    n_in, n_out, n_scr = len(in_specs), len(h_out_shape), len(scratch_shapes)
    s_in, s_out = len(stage.inputs), len(stage.outputs)

    def wrapped(*refs):
        h_in, st_in = refs[:n_in], refs[n_in:n_in + s_in]
        h_o, st_o = refs[n_in + s_in:n_in + s_in + n_out], refs[n_in + s_in + n_out:n_in + s_in + n_out + s_out]
        h_scr = refs[n_in + s_in + n_out + s_out:n_in + s_in + n_out + s_out + n_scr]
        send_sems, recv_sems = refs[n_in + s_in + n_out + s_out + n_scr:]
        def begin():
            x, y, c = _coords()
            peers = [(x, y, 1 - c)] * ("sibling" in stage.peers)
            peers += [(px, py, c) for px, py in _chips()] * ("chips" in stage.peers)
            peers += [(px, py, 1 - c) for px, py in _chips()] * ("others" in stage.peers)
            barrier = pltpu.get_barrier_semaphore()
            for peer in peers:
                pl.semaphore_signal(barrier, inc=1, device_id=peer, device_id_type=MESH)
            pl.semaphore_wait(barrier, len(peers))
            stage.run(st_in, st_o, send_sems, recv_sems, True)

        if not grid:
            begin()
            if body is not None:
                body(*h_in, *h_o, *h_scr)
            stage.run(st_in, st_o, send_sems, recv_sems, False)
            return
        ids = [pl.program_id(a) for a in range(len(grid))]
        first = functools.reduce(jnp.logical_and, [i == 0 for i in ids])
        last = functools.reduce(jnp.logical_and, [i == g - 1 for i, g in zip(ids, grid)])
        pl.when(first)(begin)
        body(*h_in, *h_o, *h_scr)
        pl.when(last)(lambda: stage.run(st_in, st_o, send_sems, recv_sems, False))

    call = pl.pallas_call(
        wrapped, name=name, grid=grid, in_specs=list(in_specs) + stage.in_specs, out_specs=h_out_specs + stage.out_specs,
        out_shape=h_out_shape + stage.outputs, input_output_aliases={n_in + i: n_out + o for i, o in stage.aliases.items()},
        scratch_shapes=list(scratch_shapes) + [pltpu.SemaphoreType.DMA((stage.n_copies,)), pltpu.SemaphoreType.DMA((stage.n_copies,))],
        compiler_params=pltpu.CompilerParams(dimension_semantics=("arbitrary",) * len(grid), vmem_limit_bytes=VMEM_LIMIT_V7X,
                                             collective_id=PEER_SETS.index(frozenset(stage.peers))),
    )

    def run(*operands):
        outs = call(*operands, *stage.inputs)
        host = outs[:n_out]
        return (host[0] if single else host), outs[n_out:]

    return run


def _run_stage(stage, name):
    return _hosted(stage, None, name=name, in_specs=[], out_specs=[], out_shape=[])()[1]


def _add_allgather8(stage, v):
    def build(ins, outs):
        x, y, c = _coords()
        me = 4 * x + 2 * y + c
        copies = [(ins[0], outs[0].at[me], None, None)]
        for r in range(1, N_DEV):
            peer = (_flip(x, r & 4), _flip(y, r & 2), _flip(c, r & 1))
            copies.append((ins[0], outs[0].at[me], peer, outs[0].at[me ^ r]))
        return copies

    return stage.add([v], VMEM_SPEC, [SDS((N_DEV,) + v.shape, v.dtype)], VMEM_SPEC, N_DEV, build, ("sibling", "chips", "others"))[0]


def _add_gather_ici(stage, shard, rows=None, whole=None):
    first, count = rows or (0, shard.shape[1])

    def build(ins, outs):
        x, y, c = _coords()
        k_me = 2 * x + y
        part = pl.ds(first, count)
        copies = [] if whole is not None else [(ins[0], outs[0].at[k_me], (x, y, 1 - c), outs[0].at[k_me])]
        for px, py in _chips():
            copies.append((ins[0].at[c, part], outs[0].at[k_me, c, part], (px, py, c), outs[0].at[2 * px + py, c, part]))
        return copies

    if whole is None:
        return stage.add([shard], HBM_SPEC, [SDS((N_CHIPS,) + shard.shape, shard.dtype)], HBM_SPEC, N_CHIPS, build, ("sibling", "chips"))[0]

    def build_into(ins, outs):
        return build(ins[1:], outs)

    i0 = len(stage.inputs)
    out = stage.add([whole, shard], HBM_SPEC, [SDS(whole.shape, whole.dtype)], HBM_SPEC, N_CHIPS - 1, build_into, ("chips",))[0]
    stage.aliases[i0] = out
    return out


def _add_gather_d2d(stage, whole):
    def build(ins, outs):
        x, y, c = _coords()
        return [(outs[0].at[2 * px + py, c], outs[0].at[2 * px + py, c], (x, y, 1 - c), outs[0].at[2 * px + py, 1 - c]) for px, py in _chips()]

    return stage.add([whole], HBM_SPEC, [SDS(whole.shape, whole.dtype)], HBM_SPEC, N_CHIPS - 1, build, ("sibling",), alias=True)[0]


def _add_reduce_d2d(stage, grads):
    def build(ins, outs):
        x, y, c = _coords()
        return [(ins[0].at[k, 1 - c], outs[0].at[k], (x, y, 1 - c), outs[0].at[k]) for k in range(N_CHIPS)]

    return stage.add([grads], HBM_SPEC, [SDS((N_CHIPS,) + grads.shape[2:], grads.dtype)], HBM_SPEC, N_CHIPS, build, ("sibling",))[0]


def _add_reduce_ici(stage, partial):
    def build(ins, outs):
        x, y, c = _coords()
        return [(ins[0].at[2 * px + py], outs[0].at[j], (px, py, c), outs[0].at[j]) for j, (px, py) in enumerate(_chips())]

    return stage.add([partial], HBM_SPEC, [SDS((N_CHIPS - 1,) + partial.shape[1:], partial.dtype)], HBM_SPEC, N_CHIPS - 1, build, ("chips",))[0]


def _add_swap_halves(stage, quarter):
    def build(ins, outs):
        x, y, c = _coords()
        return [(outs[0].at[c], outs[0].at[c], (x, y, 1 - c), outs[0].at[1 - c])]

    return stage.add([quarter], HBM_SPEC, [SDS(quarter.shape, quarter.dtype)], HBM_SPEC, 1, build, ("sibling",), alias=True)[0]


def _sum_pair(grads, from_sibling, place, name):
    _, _, rows, cols = grads.shape
    tr = min(rows, 512)

    def body(place_ref, g_ref, s_ref, o_ref):
        o_ref[0] = (g_ref[0, 0].astype(F32) + s_ref[0].astype(F32)).astype(BF16)

    spec = pl.BlockSpec((1, tr, cols), lambda k, i, place_ref: (k, i, 0))
    return pl.pallas_call(
        body, name=name, out_shape=SDS((N_CHIPS, rows, cols), BF16),
        grid_spec=pltpu.PrefetchScalarGridSpec(
            num_scalar_prefetch=1, grid=(N_CHIPS, rows // tr),
            in_specs=[pl.BlockSpec((1, 1, tr, cols), lambda k, i, place_ref: (k, place_ref[0], i, 0)), spec], out_specs=spec),
        compiler_params=_params("parallel", "parallel"),
    )(place, grads, from_sibling)


def _sum_quarter(grads, from_sibling, from_chips, place, name):
    _, _, rows, cols = grads.shape
    tr = min(rows, 256)

    def body(place_ref, g_ref, s_ref, c_ref, o_ref):
        acc = g_ref[0, 0].astype(F32) + s_ref[0].astype(F32)
        for j in range(N_CHIPS - 1):
            acc = acc + c_ref[j].astype(F32)
        o_ref[0] = acc

    return pl.pallas_call(
        body, name=name, out_shape=SDS((2, rows, cols), F32),
        grid_spec=pltpu.PrefetchScalarGridSpec(
            num_scalar_prefetch=1, grid=(rows // tr,),
            in_specs=[pl.BlockSpec((1, 1, tr, cols), lambda i, place_ref: (place_ref[1], place_ref[0], i, 0)),
                      pl.BlockSpec((1, tr, cols), lambda i, place_ref: (place_ref[1], i, 0)),
                      pl.BlockSpec((N_CHIPS - 1, tr, cols), lambda i, place_ref: (0, i, 0))],
            out_specs=pl.BlockSpec((1, tr, cols), lambda i, place_ref: (place_ref[0], i, 0))),
        compiler_params=_params("parallel"),
    )(place, grads, from_sibling, from_chips)


def _row_spec(ts, cols, col_block=0):
    return pl.BlockSpec((ts, cols), lambda *g: (g[0], col_block))


def _vec_spec(rows, cols):
    return pl.BlockSpec((rows, cols), lambda *g: (0, 0))


def _matmul(a, w, *, transposed, reduce_blocks, name, tm, outs, finish, extra=(), prepare=None, zero_first=(), parts=1, stage=None):
    m_rows = a.shape[0]
    nb, r, c = w.shape
    kb = c if transposed else r
    dims = (((1,), (1,)), ((), ())) if transposed else (((1,), (0,)), ((), ()))
    assert a.shape[1] == (nb * kb if reduce_blocks else kb) and m_rows % tm == 0 and tm % parts == 0
    n_extra, sub = len(extra), tm // parts

    def body(a_ref, w_ref, *rest):
        extra_refs, out_refs = rest[:n_extra], rest[n_extra:]
        if zero_first:
            @pl.when(pl.program_id(0) == 0)
            def _():
                for i in zero_first:
                    out_refs[i][...] = jnp.zeros_like(out_refs[i])

        def views(j):
            rows = pl.ds(j * sub, sub)
            return rows, [ref.at[rows] if ref.shape[0] == tm else ref for ref in extra_refs], [ref.at[rows] if ref.shape[0] == tm else ref for ref in out_refs]

        def product(j):
            rows, ex, ou = views(j)
            if reduce_blocks:
                acc = lax.dot_general(a_ref[rows, 0:kb], w_ref[0], dims, preferred_element_type=F32)
                for k in range(1, nb):
                    acc += lax.dot_general(a_ref[rows, k * kb:(k + 1) * kb], w_ref[k], dims, preferred_element_type=F32)
                return [(None, acc)]
            lhs = a_ref[rows, :] if prepare is None else prepare(a_ref.at[rows], ex, ou)
            return [(k, lax.dot_general(lhs, w_ref[k], dims, preferred_element_type=F32)) for k in range(nb)]

        pending = product(0)
        for j in range(parts):
            ahead = product(j + 1) if j + 1 < parts else None
            _, ex, ou = views(j)
            for k, acc in pending:
                finish(k, acc, ex, ou)
            pending = ahead

    return _hosted(
        stage, body, name=name, grid=(m_rows // tm,), out_shape=[s for s, _ in outs], out_specs=[s for _, s in outs],
        in_specs=[_row_spec(tm, a.shape[1]), pl.BlockSpec((nb, r, c), lambda *g: (0, 0, 0), pipeline_mode=pl.Buffered(1))] + [s for _, s in extra],
        compiler_params=_params("arbitrary"),
    )(a, w, *[e for e, _ in extra])


def _store_blocks(n):
    def finish(k, acc, extra_refs, out_refs):
        if k is None:
            out_refs[0][...] = acc.astype(out_refs[0].dtype)
        else:
            out_refs[0][:, k * n:(k + 1) * n] = acc.astype(out_refs[0].dtype)
    return finish


def _weight_grad(a, b, *, nb, name, tk, stage=None):
    s_rows, k1 = a.shape
    tn = b.shape[1] // nb
    assert k1 % tk == 0 and b.shape[1] % nb == 0

    def body(a_ref, b_ref, o_ref):
        o_ref[0] = lax.dot_general(a_ref[...], b_ref[...], (((0,), (0,)), ((), ())), preferred_element_type=F32).astype(o_ref.dtype)

    return _hosted(
        stage, body, name=name, grid=(nb, k1 // tk), out_shape=SDS((nb, k1, tn), BF16),
        in_specs=[pl.BlockSpec((s_rows, tk), lambda n, i: (0, i)), pl.BlockSpec((s_rows, tn), lambda n, i: (0, n))],
        out_specs=pl.BlockSpec((1, tk, tn), lambda n, i: (n, i, 0)), compiler_params=_params("parallel", "parallel"),
    )(a, b)


def _rms(x):
    r = lax.rsqrt(jnp.mean(x * x, axis=-1, keepdims=True) + EPS)
    return x * r, r


def _accumulate(acc_ref, row, value):
    acc_ref[row:row + 1, :] += jnp.sum(value, axis=0, keepdims=True)


def _gelu_parts(y):
    k0, k1 = 0.7978845608028654, 0.044715
    th = jnp.tanh(k0 * (y + k1 * (y * y * y)))
    gelu = 0.5 * y * (1.0 + th)
    dgelu = 0.5 * (1.0 + th) + 0.5 * y * (1.0 - th * th) * (k0 * (1.0 + 3.0 * k1 * (y * y)))
    return gelu, dgelu


def _sigmoid(z):
    return 0.5 * jnp.tanh(0.5 * z) + 0.5


def _one_minus_square(a, log_a):
    return -jnp.tanh(log_a) * (1.0 + a * a)


def _modulated_norm(x, mod_ref, g_ref, sc_row, sh_row):
    xh, _ = _rms(x)
    return ((xh * g_ref[...]) * (1.0 + mod_ref[sc_row:sc_row + 1, :]) + mod_ref[sh_row:sh_row + 1, :]).astype(BF16)


def _modulated_norm_backward(dh, x, dres, mod_ref, g_ref, acc_ref, sc_row):
    xh, r = _rms(x)
    _accumulate(acc_ref, 0, dh * (xh * g_ref[...]))
    _accumulate(acc_ref, 1, dh)
    dn = dh * (1.0 + mod_ref[sc_row:sc_row + 1, :])
    _accumulate(acc_ref, 2, dn * xh)
    dxh = dn * g_ref[...]
    return dres + r * (dxh - xh * jnp.mean(dxh * xh, axis=-1, keepdims=True))


def _seq_forward(proj, conv_w, vecs, w_a, w_x, w_p, *, ts, stage=None):
    s_rows = proj.shape[0]
    d = D_MODEL

    def body(x_ref, xh_ref, y_ref, u_ref, uh_ref, cw_ref, vec_ref, wa_ref, wx_ref, wp_ref,
             xr_o, hr_o, ga_o, p_o, pooled_o, carry, a_scr, b_scr):
        t = pl.program_id(0)

        @pl.when(t == 0)
        def _():
            carry[...] = jnp.zeros_like(carry)

        rows = lax.broadcasted_iota(jnp.int32, (ts, BLOCK), 0)
        start = (rows == 0) & (t == 0)
        for g in range(N_BLOCKS):
            lanes = slice(g * BLOCK, (g + 1) * BLOCK)
            x = x_ref[:, lanes]
            xx = jnp.concatenate([jnp.where(t == 0, 0.0, xh_ref[:, lanes]), x], axis=0)
            xr = vec_ref[0:1, lanes] + x * cw_ref[CONV_WIDTH - 1:CONV_WIDTH, lanes]
            for j in range(CONV_WIDTH - 1):
                xr = xr + pltpu.roll(xx, CONV_WIDTH - 1 - j, 0)[CONV_HALO:] * cw_ref[j:j + 1, lanes]
            xr_o[:, lanes] = xr
            xrb = xr.astype(BF16)
            r = _sigmoid(jnp.dot(xrb, wa_ref[g], preferred_element_type=F32) + vec_ref[1:2, lanes])
            gate_i = _sigmoid(jnp.dot(xrb, wx_ref[g], preferred_element_type=F32) + vec_ref[2:3, lanes])
            log_a = (-C_RG * r) * jax.nn.softplus(vec_ref[3:4, lanes])
            a = jnp.exp(log_a)
            a_scr[:, lanes] = a
            b_scr[:, lanes] = xr * gate_i * jnp.where(start, 1.0, jnp.sqrt(_one_minus_square(a, log_a)))

        sub = lax.broadcasted_iota(jnp.int32, (8, d), 0)

        def chunk(i, h):
            a = a_scr[pl.ds(i * 8, 8), :]
            b = b_scr[pl.ds(i * 8, 8), :]
            for s in (1, 2, 4):
                keep = sub >= s
                b = jnp.where(keep, a * pltpu.roll(b, s, 0) + b, b)
                a = jnp.where(keep, a * pltpu.roll(a, s, 0), a)
            hh = a * h + b
            hr_o[pl.ds(i * 8, 8), :] = hh
            return hh[7:8, :]

        carry[...] = lax.fori_loop(0, ts // 8, chunk, carry[...])
        pos = (rows + t * ts + 1).astype(F32)
        for g, win in enumerate(POOL_WINDOWS):
            lanes = slice(g * BLOCK, (g + 1) * BLOCK)
            gelu, _ = _gelu_parts(y_ref[:, lanes])
            ga_o[:, lanes] = (gelu * hr_o[:, lanes]).astype(BF16)
            u = u_ref[:, lanes]
            sums, have = jnp.concatenate([jnp.where(t == 0, 0.0, uh_ref[:, lanes]), u], axis=0), 1
            while have < win:
                sums = sums + pltpu.roll(sums, have, 0)
                have *= 2
            pb = (sums[POOL_HALO:] * (1.0 / jnp.minimum(pos, float(win))) - u).astype(BF16)
            p_o[:, lanes] = pb
            mixed = jnp.dot(pb, wp_ref[g], preferred_element_type=F32) + vec_ref[4:5, lanes]
            pooled_o[:, lanes] = (mixed * vec_ref[5:6, lanes]).astype(BF16)

    halo_x = pl.BlockSpec((CONV_HALO, d), lambda t: (jnp.maximum(t * (ts // CONV_HALO) - 1, 0), 0))
    halo_u = pl.BlockSpec((POOL_HALO, d), lambda t: (jnp.maximum(t * (ts // POOL_HALO) - 1, 0), 2))
    wspec = pl.BlockSpec((N_BLOCKS, BLOCK, BLOCK), lambda t: (0, 0, 0))
    return _hosted(
        stage, body, name="seq_forward", grid=(s_rows // ts,),
        out_shape=[SDS((s_rows, d), F32), SDS((s_rows, d), F32), SDS((s_rows, d), BF16), SDS((s_rows, d), BF16), SDS((s_rows, d), BF16)],
        in_specs=[_row_spec(ts, d, 0), halo_x, _row_spec(ts, d, 1), _row_spec(ts, d, 2), halo_u, _vec_spec(CONV_WIDTH, d), _vec_spec(8, d),
                  wspec, wspec, wspec],
        out_specs=[_row_spec(ts, d)] * 5,
        scratch_shapes=[pltpu.VMEM((1, d), F32), pltpu.VMEM((ts, d), F32), pltpu.VMEM((ts, d), F32)],
        compiler_params=_params("arbitrary"),
    )(proj, proj, proj, proj, proj, conv_w, vecs, w_a, w_x, w_p)


def _branches(ga, pooled, proj, w_a, w_b, *, ts, stage=None):
    s_rows, d = ga.shape

    def body(a_ref, p_ref, ga_ref, gb_ref, wa_ref, wb_ref, merged_o, sa_o, sb_o, ta_o, tb_o):
        bra = jnp.dot(a_ref[...], wa_ref[0], preferred_element_type=F32)
        brb = jnp.dot(p_ref[...], wb_ref[0], preferred_element_type=F32)
        sa, sb = _sigmoid(ga_ref[...]), _sigmoid(gb_ref[...])
        merged_o[...] = (sa * bra + sb * brb).astype(BF16)
        sa_o[...] = sa.astype(BF16)
        sb_o[...] = sb.astype(BF16)
        ta_o[...] = (bra * (sa * (1.0 - sa))).astype(BF16)
        tb_o[...] = (brb * (sb * (1.0 - sb))).astype(BF16)

    rows = _row_spec(ts, d)
    wspec = pl.BlockSpec((1, d, d), lambda *g: (0, 0, 0), pipeline_mode=pl.Buffered(1))
    return _hosted(
        stage, body, name="branches", grid=(s_rows // ts,), out_shape=[SDS((s_rows, d), BF16)] * 5,
        in_specs=[rows, rows, _row_spec(ts, d, 3), _row_spec(ts, d, 4), wspec, wspec], out_specs=[rows] * 5,
        compiler_params=_params("arbitrary"),
    )(ga, pooled, proj, proj, w_a, w_b)


def _seq_backward_by_block(proj, xr, hr, p, dga, dpooled, dgab, conv_w, vecs, w_a, w_x, w_p, *, ts, stage=None):
    s_rows = proj.shape[0]
    d = D_MODEL
    n_t = s_rows // ts
    nn, nt, tn = (((1,), (0,)), ((), ())), (((1,), (1,)), ((), ())), (((0,), (0,)), ((), ()))

    def dot(lhs, rhs, dims):
        return lax.dot_general(lhs, rhs, dims, preferred_element_type=F32)

    def body(x_ref, y_ref, xr_ref, hr_ref, hh_ref, p_ref, dga_ref, dpl_ref, dgab_ref, cw_ref, vec_ref, wa_ref, wx_ref, wp_ref,
             dproj_o, acc_o, dwa_o, dwx_o, dwp_o, g_carry, dxr_carry, q_carry, a_scr, b_scr, g_scr, a_keep, r_keep, i_keep, m_keep):
        step = pl.program_id(0)
        t = n_t - 1 - step

        @pl.when(step == 0)
        def _():
            for ref in (acc_o, dwa_o, dwx_o, dwp_o, g_carry, dxr_carry, q_carry):
                ref[...] = jnp.zeros_like(ref)

        rows = lax.broadcasted_iota(jnp.int32, (ts, BLOCK), 0)
        start = (rows == 0) & (t == 0)

        def add_sum(row, lanes, value):
            acc_o[row:row + 1, lanes] += jnp.sum(value, axis=0, keepdims=True)

        for g in range(N_BLOCKS):
            lanes = slice(g * BLOCK, (g + 1) * BLOCK)
            xrb = xr_ref[:, lanes].astype(BF16)
            r = _sigmoid(dot(xrb, wa_ref[g], nn) + vec_ref[1:2, lanes])
            gate_i = _sigmoid(dot(xrb, wx_ref[g], nn) + vec_ref[2:3, lanes])
            log_a = (-C_RG * r) * jax.nn.softplus(vec_ref[3:4, lanes])
            a = jnp.exp(log_a)
            a_keep[:, lanes], r_keep[:, lanes], i_keep[:, lanes], m_keep[:, lanes] = a, r, gate_i, _one_minus_square(a, log_a)
            gelu, dgelu = _gelu_parts(y_ref[:, lanes])
            dga_t = dga_ref[:, lanes].astype(F32)
            dproj_o[:, d + g * BLOCK:d + (g + 1) * BLOCK] = (dga_t * hr_ref[:, lanes] * dgelu).astype(BF16)
            a_scr[:, lanes] = jnp.where(rows == ts - 1, 1.0, pltpu.roll(a, ts - 1, 0))
            b_scr[:, lanes] = dga_t * gelu

        sub = lax.broadcasted_iota(jnp.int32, (8, d), 0)

        def chunk(i, g_next):
            at = pl.multiple_of((ts // 8 - 1 - i) * 8, 8)
            aa = a_scr[pl.ds(at, 8), :]
            bb = b_scr[pl.ds(at, 8), :]
            for s in (1, 2, 4):
                keep = sub < 8 - s
                bb = jnp.where(keep, bb + aa * pltpu.roll(bb, 8 - s, 0), bb)
                aa = jnp.where(keep, aa * pltpu.roll(aa, 8 - s, 0), aa)
            gg = aa * g_next + bb
            g_scr[pl.ds(at, 8), :] = gg
            return gg[0:1, :]

        g_first = lax.fori_loop(0, ts // 8, chunk, g_carry[...])
        g_carry[...] = a_keep[0:1, :] * g_first

        pos = (rows + t * ts + 1).astype(F32)
        for g, win in enumerate(POOL_WINDOWS):
            lanes = slice(g * BLOCK, (g + 1) * BLOCK)
            gs, xr, a, r, gate_i, m_square = g_scr[:, lanes], xr_ref[:, lanes], a_keep[:, lanes], r_keep[:, lanes], i_keep[:, lanes], m_keep[:, lanes]
            sp = jax.nn.softplus(vec_ref[3:4, lanes])
            mult = jnp.where(start, 1.0, jnp.sqrt(m_square))
            h_before = jnp.where(t == 0, 0.0, hh_ref[CONV_HALO - 1:CONV_HALO, lanes])
            h_prev = jnp.where(rows == 0, h_before, pltpu.roll(hr_ref[:, lanes], 1, 0))
            gm = gs * mult
            dxr = gm * gate_i
            d_i = gm * xr
            d_mult = (gs * xr) * gate_i
            dlog_a = (gs * h_prev) * a - jnp.where(start, 0.0, d_mult * (a * a) * lax.rsqrt(m_square))
            dzr = (dlog_a * (-C_RG * sp)) * (r * (1.0 - r))
            dzi = d_i * (gate_i * (1.0 - gate_i))
            add_sum(5, lanes, dzr)
            add_sum(6, lanes, dzi)
            acc_o[7:8, lanes] += jnp.sum(dlog_a * r, axis=0, keepdims=True) * (-C_RG * jax.nn.sigmoid(vec_ref[3:4, lanes]))
            xrb, dzrb, dzib = xr.astype(BF16), dzr.astype(BF16), dzi.astype(BF16)
            dwa_o[g] += dot(xrb, dzrb, tn)
            dwx_o[g] += dot(xrb, dzib, tn)
            dxr = dxr + dot(dzrb, wa_ref[g], nt) + dot(dzib, wx_ref[g], nt)

            x = x_ref[:, lanes]
            ext = jnp.concatenate([dxr, dxr_carry[:, lanes]], axis=0)
            dx = dxr * cw_ref[CONV_WIDTH - 1:CONV_WIDTH, lanes]
            add_sum(CONV_WIDTH - 1, lanes, x * dxr)
            for j in range(CONV_WIDTH - 1):
                ahead = pltpu.roll(ext, ts + CONV_HALO - (CONV_WIDTH - 1 - j), 0)[:ts]
                dx = dx + ahead * cw_ref[j:j + 1, lanes]
                add_sum(j, lanes, x * ahead)
            add_sum(4, lanes, dxr)
            dxr_carry[:, lanes] = dxr[0:CONV_HALO, :]
            dproj_o[:, lanes] = dx.astype(BF16)

            pb = p_ref[:, lanes]
            mixed = dot(pb, wp_ref[g], nn) + vec_ref[4:5, lanes]
            dpl = dpl_ref[:, lanes].astype(F32)
            add_sum(9, lanes, dpl * mixed)
            dmixed = dpl * vec_ref[5:6, lanes]
            add_sum(8, lanes, dmixed)
            dmb = dmixed.astype(BF16)
            dwp_o[g] += dot(pb, dmb, tn)
            dp = dot(dmb, wp_ref[g], nt)
            q = dp * (1.0 / jnp.minimum(pos, float(win)))
            sums, have = jnp.concatenate([q, q_carry[:, lanes]], axis=0), 1
            while have < win:
                sums = sums + pltpu.roll(sums, ts + POOL_HALO - have, 0)
                have *= 2
            q_carry[:, lanes] = q[0:POOL_HALO, :]
            dproj_o[:, 2 * d + g * BLOCK:2 * d + (g + 1) * BLOCK] = (sums[:ts] - dp).astype(BF16)

        dproj_o[:, 3 * d:5 * d] = dgab_ref[...]

    def rev(cols, col_block=0):
        return pl.BlockSpec((ts, cols), lambda i: (n_t - 1 - i, col_block))

    halo_h = pl.BlockSpec((CONV_HALO, d), lambda i: (jnp.maximum((n_t - 1 - i) * (ts // CONV_HALO) - 1, 0), 0))
    wspec = pl.BlockSpec((N_BLOCKS, BLOCK, BLOCK), lambda i: (0, 0, 0))
    return _hosted(
        stage, body, name="seq_backward", grid=(n_t,),
        out_shape=[SDS((s_rows, 5 * d), BF16), SDS((16, d), F32)] + [SDS((N_BLOCKS, BLOCK, BLOCK), F32)] * 3,
        in_specs=[rev(d, 0), rev(d, 1), rev(d), rev(d), halo_h, rev(d), rev(d), rev(d), rev(2 * d), _vec_spec(CONV_WIDTH, d), _vec_spec(8, d),
                  wspec, wspec, wspec],
        out_specs=[rev(5 * d), _vec_spec(16, d), wspec, wspec, wspec],
        scratch_shapes=[pltpu.VMEM((1, d), F32), pltpu.VMEM((CONV_HALO, d), F32), pltpu.VMEM((POOL_HALO, d), F32)] + [pltpu.VMEM((ts, d), F32)] * 7,
        compiler_params=_params("arbitrary"),
    )(proj, proj, xr, hr, hr, p, dga, dpooled, dgab, conv_w, vecs, w_a, w_x, w_p)


def _adamw_math(w, g, m, v):
    m = ADAM_B1 * m + (1.0 - ADAM_B1) * g
    v = ADAM_B2 * v + (1.0 - ADAM_B2) * (g * g)
    m_hat = m / (1.0 - ADAM_B1 ** ADAM_STEP)
    v_hat = v / (1.0 - ADAM_B2 ** ADAM_STEP)
    delta = -ADAM_LR * (m_hat / (jnp.sqrt(v_hat) + ADAM_EPS) + ADAM_WD * w)
    return delta, m, v


def _adamw(w, g, m, v, name):
    rows, cols = w.shape
    tr = min(rows, 512)

    def body(w_ref, g_ref, m_ref, v_ref, g_out, d_ref, mo_ref, vo_ref):
        g = g_ref[...]
        g_out[...] = g
        d_ref[...], mo_ref[...], vo_ref[...] = _adamw_math(w_ref[...], g, m_ref[...], v_ref[...])

    spec = pl.BlockSpec((tr, cols), lambda i: (i, 0))
    return pl.pallas_call(
        body, name=name, grid=(rows // tr,), out_shape=[SDS((rows, cols), F32)] * 4, in_specs=[spec] * 4, out_specs=[spec] * 4,
        compiler_params=_params("parallel"),
    )(w, g, m, v)


def _adamw_on_sparsecores(w, g, m, v, name):
    rows, cols = w.shape
    per_tile = rows // SC_TILES
    assert rows % (SC_TILES * SC_SLAB) == 0 and cols % SC_LANES == 0

    def body(w_hbm, g_hbm, m_hbm, v_hbm, g_out, d_out, m_out, v_out, wb, gb, mb, vb, db):
        tile = lax.axis_index("subcore") * 2 + lax.axis_index("core")

        @pl.loop(0, per_tile, step=SC_SLAB)
        def _(r0):
            at = pl.ds(tile * per_tile + r0, SC_SLAB)
            for src, buf in ((w_hbm, wb), (g_hbm, gb), (m_hbm, mb), (v_hbm, vb)):
                pltpu.sync_copy(src.at[at], buf)
            for r in range(SC_SLAB):
                @pl.loop(0, cols, step=SC_LANES)
                def _(col):
                    lanes = pl.ds(col, SC_LANES)
                    db[r, lanes], mb[r, lanes], vb[r, lanes] = _adamw_math(wb[r, lanes], gb[r, lanes], mb[r, lanes], vb[r, lanes])
            for buf, dst in ((gb, g_out), (db, d_out), (mb, m_out), (vb, v_out)):
                pltpu.sync_copy(buf, dst.at[at])

    return pl.kernel(
        body, name=name, out_type=[SDS(w.shape, F32)] * 4,
        mesh=plsc.VectorSubcoreMesh(core_axis_name="core", subcore_axis_name="subcore"),
        scratch_types=[pltpu.VMEM((SC_SLAB, cols), F32)] * 5,
    )(w, g, m, v)


def _prologue(c, conv_shard, w_ada, b_cols, w_in_halves, gate_halves):
    n_cols = w_ada.shape[1]
    n_g = len(gate_halves)
    half_rows = w_in_halves.shape[1] // 2
    n_sems = 3 * N_DEV + N_CHIPS * n_g + 5 + 2 + 3

    def body(c_ref, conv_ref, w_ref, b_ref, win_ref, *rest):
        gate_refs = rest[:n_g]
        act_o, mod_o, conv_o, win_o = rest[n_g:n_g + 4]
        gate_o = rest[n_g + 4:2 * n_g + 4]
        c_all, mod_cols, w_vmem, send_sems, recv_sems = rest[2 * n_g + 4:]
        fetch_w = pltpu.make_async_copy(w_ref, w_vmem, send_sems.at[n_sems])
        x, y, c = _coords()
        me, k_me = 4 * x + 2 * y + c, 2 * x + y
        sibling, x_nbr, y_nbr = (x, y, 1 - c), (1 - x, y, c), (x, 1 - y, c)
        k_x, k_y, k_d = 2 * (1 - x) + y, 2 * x + (1 - y), 2 * (1 - x) + (1 - y)
        counter = iter(range(n_sems))

        def copy(src, dst, to, landing):
            s = next(counter)
            out = pltpu.make_async_remote_copy(src_ref=src, dst_ref=dst, send_sem=send_sems.at[s], recv_sem=recv_sems.at[s],
                                               device_id=to, device_id_type=MESH)
            arrival = pltpu.make_async_remote_copy(src_ref=landing, dst_ref=landing, send_sem=send_sems.at[s], recv_sem=recv_sems.at[s],
                                                   device_id=(x, y, c), device_id_type=MESH)
            return out, arrival

        def allgather(src, out):
            s = next(counter)
            own = pltpu.make_async_copy(src, out.at[me], send_sems.at[s])
            peers = [copy(src, out.at[me], (_flip(x, r & 4), _flip(y, r & 2), _flip(c, r & 1)), out.at[me ^ r]) for r in range(1, N_DEV)]
            return own, peers

        def start(group):
            own, peers = group
            own.start()
            for out, _ in peers:
                out.start()

        def finish(group):
            own, peers = group
            for out, arrival in peers:
                arrival.wait_recv()
                out.wait_send()
            own.wait()

        rows_a, rows_b = pl.ds(0, half_rows), pl.ds(half_rows, half_rows)
        c_group, conv_group = allgather(c_ref, c_all), allgather(conv_ref, conv_o)
        gates = []
        for g_ref, g_out in zip(gate_refs, gate_o):
            gates.append(copy(g_ref, g_out.at[k_me], sibling, g_out.at[k_me]))
            gates += [copy(g_ref.at[c], g_out.at[k_me, c], (px, py, c), g_out.at[2 * px + py, c]) for px, py in _chips()]
        own_in = copy(win_ref, win_o.at[k_me], sibling, win_o.at[k_me])
        to_x = [copy(win_ref.at[c, rows], win_o.at[k_me, c, rows], x_nbr, win_o.at[k_x, c, rows]) for rows in (rows_a, rows_b)]
        to_y = [copy(win_ref.at[c, rows], win_o.at[k_me, c, rows], y_nbr, win_o.at[k_y, c, rows]) for rows in (rows_b, rows_a)]
        mod_group = allgather(mod_cols, mod_o)
        relay_y = copy(win_o.at[k_x, c, rows_a], win_o.at[k_x, c, rows_a], y_nbr, win_o.at[k_d, c, rows_a])
        relay_x = copy(win_o.at[k_y, c, rows_b], win_o.at[k_y, c, rows_b], x_nbr, win_o.at[k_d, c, rows_b])
        sib_x, sib_y, sib_d = (copy(win_o.at[k, c], win_o.at[k, c], sibling, win_o.at[k, 1 - c]) for k in (k_x, k_y, k_d))

        start(c_group)
        for out, _ in to_x + to_y + [own_in] + gates:
            out.start()
        start(conv_group)
        fetch_w.start()
        finish(c_group)
        cv = jnp.concatenate([c_all[r] for r in range(N_DEV)], axis=0)
        act = cv * jax.nn.sigmoid(cv)
        act_o[...] = act
        fetch_w.wait()
        mod_cols[...] = jnp.dot(act.astype(BF16), w_vmem[...].astype(BF16), preferred_element_type=F32) + b_ref[...]
        start(mod_group)
        to_x[0][1].wait_recv()
        relay_y[0].start()
        to_y[0][1].wait_recv()
        relay_x[0].start()
        finish(mod_group)
        finish(conv_group)
        to_x[1][1].wait_recv()
        sib_x[0].start()
        to_y[1][1].wait_recv()
        sib_y[0].start()
        for out, arrival in gates + [own_in]:
            arrival.wait_recv()
            out.wait_send()
        for out, arrival in (relay_y, relay_x):
            arrival.wait_recv()
            out.wait_send()
        sib_d[0].start()
        for out, _ in to_x + to_y:
            out.wait_send()
        for out, arrival in (sib_x, sib_y, sib_d):
            arrival.wait_recv()
            out.wait_send()

    gate_shapes = [SDS((N_CHIPS,) + g.shape, g.dtype) for g in gate_halves]
    return pl.pallas_call(
        body, name="prologue",
        out_shape=[SDS((N_DEV, c.shape[1]), F32), SDS((N_DEV, N_DEV, n_cols), F32), SDS((N_DEV,) + conv_shard.shape, F32),
                   SDS((N_CHIPS,) + w_in_halves.shape, w_in_halves.dtype)] + gate_shapes,
        in_specs=[VMEM_SPEC, VMEM_SPEC, HBM_SPEC, VMEM_SPEC] + [HBM_SPEC] * (1 + n_g), out_specs=[VMEM_SPEC] * 3 + [HBM_SPEC] * (1 + n_g),
        scratch_shapes=[pltpu.VMEM((N_DEV,) + c.shape, F32), pltpu.VMEM((N_DEV, n_cols), F32), pltpu.VMEM(w_ada.shape, F32),
                        pltpu.SemaphoreType.DMA((n_sems + 1,)), pltpu.SemaphoreType.DMA((n_sems,))],
        compiler_params=pltpu.CompilerParams(vmem_limit_bytes=VMEM_LIMIT_V7X),
    )(c, conv_shard, w_ada, b_cols, w_in_halves, *gate_halves)


def _ada_backward(act_t, dmod_cols, w, m, v):
    rows, cols = w.shape
    tn = 512

    def body(a_ref, dm_ref, w_ref, m_ref, v_ref, g_ref, d_ref, mo_ref, vo_ref):
        g = jnp.dot(a_ref[...].astype(BF16), dm_ref[...].astype(BF16), preferred_element_type=F32)
        g_ref[...] = g
        d_ref[...], mo_ref[...], vo_ref[...] = _adamw_math(w_ref[...], g, m_ref[...], v_ref[...])

    spec = pl.BlockSpec((rows, tn), lambda j: (0, j))
    return pl.pallas_call(
        body, name="ada_backward", grid=(cols // tn,), out_shape=[SDS((rows, cols), F32)] * 4,
        in_specs=[pl.BlockSpec((rows, N_DEV), lambda j: (0, 0)), pl.BlockSpec((N_DEV, tn), lambda j: (0, j)), spec, spec, spec],
        out_specs=[spec] * 4, compiler_params=_params("parallel"),
    )(act_t, dmod_cols, w, m, v)


def _sum_rows(gathered, n_loss):
    n = gathered.shape[2]

    def body(g_ref, o_ref):
        acc = g_ref[0]
        for r in range(1, N_DEV):
            acc = acc + g_ref[r]
        o_ref[...] = acc
        o_ref[:, n - n_loss:n] = jnp.broadcast_to(jnp.sum(acc[:, n - n_loss:n], axis=1, keepdims=True), (1, n_loss))

    return pl.pallas_call(body, name="sum_rows", out_shape=SDS((1, n), F32), in_specs=[VMEM_SPEC], out_specs=VMEM_SPEC)(gathered)


def _adamw_vectors(total, offsets, separate, params):
    n_p = len(params)
    apart = [g for g in separate if g is not None]

    def body(*refs):
        total_ref, apart_refs = refs[0], list(refs[1:1 + len(apart)])
        ins, outs = refs[1 + len(apart):1 + len(apart) + 3 * n_p], refs[1 + len(apart) + 3 * n_p:]
        for i in range(n_p):
            w_ref, m_ref, v_ref = ins[3 * i:3 * i + 3]
            g = apart_refs.pop(0)[...] if offsets[i] is None else total_ref[:, offsets[i]:offsets[i] + w_ref.shape[1]]
            outs[4 * i][...] = g
            outs[4 * i + 1][...], outs[4 * i + 2][...], outs[4 * i + 3][...] = _adamw_math(w_ref[...], g, m_ref[...], v_ref[...])

    flat = [a for p in params for a in p]
    res = pl.pallas_call(
        body, name="adamw_vectors", out_shape=[SDS(p[0].shape, F32) for p in params for _ in range(4)],
        in_specs=[VMEM_SPEC] * (1 + len(apart) + len(flat)), out_specs=[VMEM_SPEC] * (4 * n_p),
    )(total, *apart, *flat)
    return [tuple(res[4 * i:4 * i + 4]) for i in range(n_p)]


def _blocks_to_pieces(w):
    nb, rows, n = w.shape
    q = rows // N_CHIPS
    return w.reshape(nb, N_CHIPS, q, n).transpose(1, 0, 2, 3).reshape(N_CHIPS, 2, nb // 2, q, n)


def _pieces_to_blocks(w):
    n_chips, _, half, q, n = w.shape
    return w.reshape(n_chips, 2 * half, q, n).transpose(1, 0, 2, 3).reshape(2 * half, n_chips * q, n)


def kernel(x, c, norm_mix_g, norm_mlp_g, w_ada, b_ada, w_in, conv_w, conv_b, w_rg_a, b_rg_a, w_rg_x, b_rg_x, a_param, w_branch_a, w_pool, b_pool, pool_scale, w_branch_b, w_out, w_up, w_down, final_g, loss_target, m_norm_mix_g, m_norm_mlp_g, m_w_ada, m_b_ada, m_w_in, m_conv_w, m_conv_b, m_w_rg_a, m_b_rg_a, m_w_rg_x, m_b_rg_x, m_a_param, m_w_branch_a, m_w_pool, m_b_pool, m_pool_scale, m_w_branch_b, m_w_out, m_w_up, m_w_down, m_final_g, v_norm_mix_g, v_norm_mlp_g, v_w_ada, v_b_ada, v_w_in, v_conv_w, v_conv_b, v_w_rg_a, v_b_rg_a, v_w_rg_x, v_b_rg_x, v_a_param, v_w_branch_a, v_w_pool, v_b_pool, v_pool_scale, v_w_branch_b, v_w_out, v_w_up, v_w_down, v_final_g):
    d = D_MODEL
    s_rows = x.shape[1]
    ts, tsq = min(512, s_rows), min(256, s_rows)
    xi, yi, ci = _coords()
    k_me, me = 2 * xi + yi, 4 * xi + 2 * yi + ci
    ada_cols = w_ada.shape[2]
    conv_cols = conv_w.shape[2]
    n_in, n_up = w_in.shape[2], w_up.shape[2]

    names = ("w_in", "w_up", "w_down", "w_a", "w_b", "w_out", "rg_a", "rg_x", "pool")
    mine = dict(zip(names, (w_in[0], w_up[0], w_down[0], w_branch_a[0], w_branch_b[0], w_out[0], w_rg_a[0], w_rg_x[0], w_pool[0])))
    moments_m = dict(zip(names, (m_w_in[0], m_w_up[0], m_w_down[0], m_w_branch_a[0], m_w_branch_b[0], m_w_out[0], m_w_rg_a[0], m_w_rg_x[0], m_w_pool[0])))
    moments_v = dict(zip(names, (v_w_in[0], v_w_up[0], v_w_down[0], v_w_branch_a[0], v_w_branch_b[0], v_w_out[0], v_w_rg_a[0], v_w_rg_x[0], v_w_pool[0])))
    block_weights, squares = ("rg_a", "rg_x", "pool"), ("w_a", "w_b", "w_out")
    place = jnp.stack([ci, k_me]).astype(jnp.int32)
    g_mix, g_mlp, g_fin = norm_mix_g, norm_mlp_g, final_g.reshape(1, d)
    x0, target = x[0], loss_target[0]
    wg = functools.partial(_weight_grad, tk=min(512, d))
    plain, back, back_sum = (dict(transposed=t, reduce_blocks=r) for t, r in ((False, False), (True, False), (True, True)))
    rows_d, vec8, vec1 = _row_spec(ts, d), _vec_spec(8, d), _vec_spec(1, d)
    tw = min(1024, s_rows)
    halves_of_tile = 2

    def out_rows(cols, dtype, t=ts):
        return SDS((s_rows, cols), dtype), _row_spec(t, cols)

    sums_out = (SDS((8, d), F32), vec8)

    def halves(n):
        return mine[n].astype(BF16).reshape(2, -1, mine[n].shape[-1])

    def blocks(w):
        return _pieces_to_blocks(w.reshape(N_CHIPS, 2, N_BLOCKS // 2, BLOCK // N_CHIPS, BLOCK))

    act_all, mod_all, conv_all, w_in_whole, *gate_wholes = _prologue(
        c, conv_w[0], w_ada[0], lax.dynamic_slice_in_dim(b_ada, k_me * ada_cols, ada_cols, axis=1), halves("w_in"), [halves(n) for n in block_weights])
    conv_full = conv_all[0::2].transpose(1, 0, 2).reshape(CONV_WIDTH, d)
    mod_mine = lax.dynamic_index_in_dim(mod_all, me, axis=1, keepdims=False)[0::2]
    mod = jnp.concatenate([mod_mine.reshape(N_MOD, d), jnp.zeros((8 - N_MOD, d), F32)], axis=0)
    w_in_all = w_in_whole.reshape(N_CHIPS, d, n_in)
    vecs = jnp.concatenate([conv_b, b_rg_a, b_rg_x, a_param, b_pool, pool_scale, jnp.zeros((2, d), F32)], axis=0)

    def norm_first(a_ref, extra_refs, out_refs):
        h = _modulated_norm(a_ref[...], extra_refs[0], extra_refs[1], 1, 0)
        out_refs[1][...] = h
        return h

    stage = _Stage()
    i_blk = [_add_gather_d2d(stage, g) for g in gate_wholes]
    i_sq = [_add_gather_ici(stage, halves(n)) for n in squares]
    down_rows = mine["w_down"].shape[0] // 8
    i_down = _add_gather_ici(stage, halves("w_down"), rows=(0, down_rows))
    (proj, h1), got = _matmul(x0, w_in_all, name="proj_in", tm=ts, extra=[(mod, vec8), (g_mix, vec1)], prepare=norm_first,
                              outs=[out_rows(N_CHIPS * n_in, F32), out_rows(d, BF16)], finish=_store_blocks(n_in), parts=halves_of_tile,                              stage=stage, **plain)
    w_rg_a_all, w_rg_x_all, w_pool_all = (blocks(got[i]) for i in i_blk)
    stage = _Stage()
    i_sq = [_add_gather_d2d(stage, got[i]) for i in i_sq]
    i_up = _add_gather_ici(stage, halves("w_up"))
    w_down_so_far = got[i_down]
    (xr, hr, ga, p, pooled), got = _seq_forward(proj, conv_full, vecs, w_rg_a_all, w_rg_x_all, w_pool_all, ts=tsq, stage=stage)
    w_a_all, w_b_all, w_out_all = (got[i].reshape(1, d, d) for i in i_sq)
    stage = _Stage()
    i_up = _add_gather_d2d(stage, got[i_up])
    i_down = _add_gather_ici(stage, halves("w_down"), rows=(down_rows, 2 * down_rows), whole=w_down_so_far)
    (merged, s_a, s_b, t_a, t_b), got = _branches(ga, pooled, proj, w_a_all, w_b_all, ts=ts, stage=stage)
    w_up_all = got[i_up].reshape(N_CHIPS, d, n_up)
    stage = _Stage()
    i_down = _add_gather_ici(stage, halves("w_down"), rows=(3 * down_rows, down_rows), whole=got[i_down])

    def residual_norm(k, acc, extra_refs, out_refs):
        x_ref, mod_ref, g_ref = extra_refs
        out_refs[0][...] = acc.astype(BF16)
        x2_t = x_ref[...] + mod_ref[2:3, :] * acc
        out_refs[1][...] = x2_t
        out_refs[2][...] = _modulated_norm(x2_t, mod_ref, g_ref, 4, 3)

    (mo, x2, h2), got = _matmul(merged, w_out_all, name="mix_out", tm=ts, extra=[(x0, rows_d), (mod, vec8), (g_mlp, vec1)],
                                outs=[out_rows(d, BF16), out_rows(d, F32), out_rows(d, BF16)], finish=residual_norm, parts=halves_of_tile,                                stage=stage, transposed=False, reduce_blocks=True)
    stage = _Stage()
    i_down = _add_gather_d2d(stage, got[i_down])

    def relu_squared(k, acc, extra_refs, out_refs):
        out_refs[0][:, k * n_up:(k + 1) * n_up] = jnp.square(jnp.maximum(acc, 0.0)).astype(BF16)

    (ff,), got = _matmul(h2, w_up_all, name="mlp_up", tm=ts, outs=[out_rows(D_FF, BF16)], finish=relu_squared, parts=halves_of_tile,
                         stage=stage, **plain)
    w_down_all = got[i_down].reshape(N_CHIPS, D_FF // N_CHIPS, d)

    def loss_head(k, acc, extra_refs, out_refs):
        x2_ref, mod_ref, g_ref, t_ref = extra_refs
        dx_ref, df_ref, acc_ref = out_refs
        xh, r = _rms(x2_ref[...] + mod_ref[5:6, :] * acc)
        err = xh * g_ref[...] - t_ref[...]
        dy = err / d
        dxh = dy * g_ref[...]
        dx3_t = r * (dxh - xh * jnp.mean(dxh * xh, axis=-1, keepdims=True))
        dx_ref[...] = dx3_t
        df_ref[...] = (dx3_t * mod_ref[5:6, :]).astype(BF16)
        _accumulate(acc_ref, 0, dy * xh)
        _accumulate(acc_ref, 1, dx3_t * acc)
        _accumulate(acc_ref, 2, jnp.square(err) * (0.5 / d))

    dx3, dffo, sums_head = _matmul(ff, w_down_all, name="mlp_down", tm=ts, extra=[(x2, rows_d), (mod, vec8), (g_fin, vec1), (target, rows_d)],
                                   outs=[out_rows(d, F32), out_rows(d, BF16), sums_out], finish=loss_head, zero_first=(2,), parts=halves_of_tile,                                   transposed=False, reduce_blocks=True)

    partial_of, from_sibling, chip_sum, from_chips, half_done, quarter = {}, {}, {}, {}, {}, {}

    def publish(n, g):
        g = _blocks_to_pieces(g).astype(BF16) if n in block_weights else g
        partial_of[n] = g.reshape(N_CHIPS, 2, -1, mine[n].shape[-1])

    def exchange(to_sibling=(), to_chips=(), swap=()):
        st = _Stage()
        slots = [(n, from_sibling, _add_reduce_d2d(st, partial_of[n])) for n in to_sibling]
        slots += [(n, from_chips, _add_reduce_ici(st, chip_sum[n])) for n in to_chips]
        slots += [(n, quarter, _add_swap_halves(st, half_done[n])) for n in swap]
        return st, slots

    big_out = {}

    def update(n):
        shape2 = (-1, mine[n].shape[-1])
        early = n in squares
        res = (_adamw_on_sparsecores if early else _adamw)(
            mine[n].reshape(shape2), quarter[n].reshape(shape2), moments_m[n].reshape(shape2), moments_v[n].reshape(shape2), "adamw_" + n)
        big_out[n] = [r.reshape((1,) + mine[n].shape) for r in res]

    def collect(slots, outs):
        for n, where, i in slots:
            where[n] = outs[i]
            if where is quarter:
                update(n)

    def sum_pairs(*ns):
        for n in ns:
            chip_sum[n] = _sum_pair(partial_of[n], from_sibling[n], place, "sum_pair_" + n)

    def sum_quarters(*ns):
        for n in ns:
            half_done[n] = _sum_quarter(partial_of[n], from_sibling[n], from_chips[n], place, "sum_quarter_" + n)

    publish("w_down", wg(ff, dffo, nb=1, name="grad_w_down"))

    def relu_squared_backward(k, acc, extra_refs, out_refs):
        cols = slice(k * n_up, (k + 1) * n_up)
        out_refs[0][:, cols] = (acc * (2.0 * jnp.sqrt(extra_refs[0][:, cols].astype(F32)))).astype(BF16)

    stage, slots = exchange(to_sibling=["w_down"])
    (dup,), got = _matmul(dffo, w_down_all, name="d_mlp_down", tm=ts, extra=[(ff, _row_spec(ts, D_FF))], outs=[out_rows(D_FF, BF16)],
                          finish=relu_squared_backward, parts=halves_of_tile, stage=stage, **back)
    collect(slots, got)
    sum_pairs("w_down")

    def norm_mlp_backward(k, acc, extra_refs, out_refs):
        x2_ref, dres_ref, mod_ref, g_ref, mo_ref = extra_refs
        dx_ref, dmo_ref, acc_ref = out_refs
        dx =_modulated_norm_backward(acc, x2_ref[...], dres_ref[...], mod_ref, g_ref, acc_ref, 4)
        dx_ref[...] = dx
        dmo_ref[...] = (dx * mod_ref[2:3, :]).astype(BF16)
        _accumulate(acc_ref, 3, dx * mo_ref[...].astype(F32))

    stage, slots = exchange(to_chips=["w_down"])
    (dx2, dmo, sums_mlp), got = _matmul(dup, w_up_all, name="d_mlp_up", tm=ts, stage=stage, finish=norm_mlp_backward,
                                        extra=[(x2, rows_d), (dx3, rows_d), (mod, vec8), (g_mlp, vec1), (mo, rows_d)],
                                        outs=[out_rows(d, F32), out_rows(d, BF16), sums_out], zero_first=(2,), parts=halves_of_tile, **back_sum)
    collect(slots, got)
    sum_quarters("w_down")
    publish("w_up", wg(h2, dup, nb=N_CHIPS, name="grad_w_up"))
    publish("w_out", wg(merged, dmo, nb=1, name="grad_w_out"))

    def merge_backward(k, acc, extra_refs, out_refs):
        sa_ref, sb_ref, ta_ref, tb_ref = extra_refs
        out_refs[0][...] = (acc * sa_ref[...].astype(F32)).astype(BF16)
        out_refs[1][...] = (acc * sb_ref[...].astype(F32)).astype(BF16)
        out_refs[2][:, 0:d] = (acc * ta_ref[...].astype(F32)).astype(BF16)
        out_refs[2][:, d:2 * d] = (acc * tb_ref[...].astype(F32)).astype(BF16)

    stage, slots = exchange(to_sibling=["w_up", "w_out"], swap=["w_down"])
    (dbr_a, dbr_b, dgab), got = _matmul(dmo, w_out_all, name="d_mix_out", tm=ts, stage=stage, finish=merge_backward,
                                        extra=[(s_a, rows_d), (s_b, rows_d), (t_a, rows_d), (t_b, rows_d)],
                                        outs=[out_rows(d, BF16), out_rows(d, BF16), out_rows(2 * d, BF16)], parts=halves_of_tile, **back_sum)
    collect(slots, got)
    publish("w_a", wg(ga, dbr_a, nb=1, name="grad_w_branch_a"))
    publish("w_b", wg(pooled, dbr_b, nb=1, name="grad_w_branch_b"))
    dga, = _matmul(dbr_a, w_a_all, name="d_branch_a", tm=tw, outs=[out_rows(d, BF16, tw)], finish=_store_blocks(d), **back_sum)
    dpooled, = _matmul(dbr_b, w_b_all, name="d_branch_b", tm=tw, outs=[out_rows(d, BF16, tw)], finish=_store_blocks(d), **back_sum)
    sum_pairs("w_up", "w_out")
    stage, slots = exchange(to_sibling=["w_a", "w_b"], to_chips=["w_up", "w_out"])
    (dproj, sums_seq, d_rg_a, d_rg_x, d_pool), got = _seq_backward_by_block(proj, xr, hr, p, dga, dpooled, dgab, conv_full, vecs,
                                                                            w_rg_a_all, w_rg_x_all, w_pool_all, ts=tsq, stage=stage)
    collect(slots, got)
    sum_quarters("w_up", "w_out")
    sum_pairs("w_a", "w_b")
    for n, g in zip(block_weights, (d_rg_a, d_rg_x, d_pool)):
        publish(n, g)
    stage, slots = exchange(to_sibling=block_weights, to_chips=["w_a", "w_b"], swap=["w_up", "w_out"])
    g_in, got = wg(h1, dproj, nb=N_CHIPS, name="grad_w_in", stage=stage)
    collect(slots, got)
    publish("w_in", g_in)
    sum_pairs(*block_weights)
    sum_quarters("w_a", "w_b")
    stage, slots = exchange(to_sibling=["w_in"], to_chips=block_weights)
    collect(slots, _run_stage(stage, "exchange_w_in_gates"))
    sum_pairs("w_in")
    sum_quarters(*block_weights)

    def norm_mix_backward(k, acc, extra_refs, out_refs):
        x_ref, dres_ref, mod_ref, g_ref = extra_refs
        out_refs[0][...] = _modulated_norm_backward(acc, x_ref[...], dres_ref[...], mod_ref, g_ref, out_refs[1], 1)

    stage, slots = exchange(to_chips=["w_in"], swap=block_weights + ("w_a", "w_b"))
    (grad_x, sums_mix), got = _matmul(dproj, w_in_all, name="d_proj_in", tm=ts, stage=stage, finish=norm_mix_backward,
                                      extra=[(x0, rows_d), (dx2, rows_d), (mod, vec8), (g_mix, vec1)],
                                      outs=[out_rows(d, F32), sums_out], zero_first=(1,), parts=halves_of_tile, **back_sum)
    collect(slots, got)
    sum_quarters("w_in")

    dmod = jnp.concatenate([sums_mix[1:2], sums_mix[0:1], sums_mlp[3:4], sums_mlp[1:2], sums_mlp[0:1], sums_head[1:2]], axis=1)
    row = jnp.concatenate([sums_mix[2:3], sums_mlp[2:3], sums_seq[4:5], sums_seq[5:6], sums_seq[6:7], sums_seq[7:8], sums_seq[8:9],
                           sums_seq[9:10], sums_head[0:1], sums_seq[0:CONV_WIDTH].reshape(1, CONV_WIDTH * d), dmod, sums_head[2:3]], axis=1)
    n_vec, n_conv, n_mod = 9 * d, CONV_WIDTH * d, N_MOD * d
    stage, slots = exchange(swap=["w_in"])
    i_rows = _add_allgather8(stage, row)
    got = _run_stage(stage, "exchange_sums_w_in")
    collect(slots, got)
    rows_all = got[i_rows]
    total = _sum_rows(rows_all, d)
    loss = total[0, n_vec + n_conv + n_mod]
    dmod_all = rows_all[:, 0, n_vec + n_conv:n_vec + n_conv + n_mod]
    g_ada, d_ada, m_ada, v_ada = _ada_backward(act_all.T, lax.dynamic_slice_in_dim(dmod_all, k_me * ada_cols, ada_cols, axis=1),
                                               w_ada[0], m_w_ada[0], v_w_ada[0])
    g_conv = lax.dynamic_slice_in_dim(total[:, n_vec:n_vec + n_conv].reshape(CONV_WIDTH, d), k_me * conv_cols, conv_cols, axis=1)
    vec_names = ("norm_mix_g", "norm_mlp_g", "conv_b", "b_rg_a", "b_rg_x", "a_param", "b_pool", "pool_scale", "final_g", "conv_w", "b_ada")
    vec_params = [(norm_mix_g, m_norm_mix_g, v_norm_mix_g), (norm_mlp_g, m_norm_mlp_g, v_norm_mlp_g), (conv_b, m_conv_b, v_conv_b),
                  (b_rg_a, m_b_rg_a, v_b_rg_a), (b_rg_x, m_b_rg_x, v_b_rg_x), (a_param, m_a_param, v_a_param), (b_pool, m_b_pool, v_b_pool),
                  (pool_scale, m_pool_scale, v_pool_scale), tuple(a.reshape(1, d) for a in (final_g, m_final_g, v_final_g)),
                  (conv_w[0], m_conv_w[0], v_conv_w[0]), (b_ada, m_b_ada, v_b_ada)]
    offsets = [i * d for i in range(9)] + [None, n_vec + n_conv]
    vec_res = _adamw_vectors(total, offsets, [None] * 9 + [g_conv, None], vec_params)
    vec_out = {n: [r.reshape(shape) for r in res] for n, res, shape in zip(
        vec_names, vec_res, [(1, d)] * 8 + [(d,), (1, CONV_WIDTH, conv_cols), (1, N_MOD * d)])}

    key = {"w_in": "w_in", "w_rg_a": "rg_a", "w_rg_x": "rg_x", "w_branch_a": "w_a", "w_pool": "pool", "w_branch_b": "w_b", "w_out": "w_out",
           "w_up": "w_up", "w_down": "w_down"}
    order = ("norm_mix_g", "norm_mlp_g", "w_ada", "b_ada", "w_in", "conv_w", "conv_b", "w_rg_a", "b_rg_a", "w_rg_x", "b_rg_x", "a_param",
             "w_branch_a", "w_pool", "b_pool", "pool_scale", "w_branch_b", "w_out", "w_up", "w_down", "final_g")
    ada_out = [g_ada[None], d_ada[None], m_ada[None], v_ada[None]]
    outs = [loss, grad_x[None]]
    for kind in range(4):
        for nme in order:
            outs.append(ada_out[kind] if nme == "w_ada" else big_out[key[nme]][kind] if nme in key else vec_out[nme][kind])
    return tuple(outs)
```
